```python
import jax
import jax.numpy as jnp
from jax import lax
import numpy as np

D_MODEL = 1024
BATCH = 16
SEQ = 2048
DEPTH = 2

GRID_W = 64
CTX_LEN = 256
HEAD_DIM = 64
ATT_HEADS = 8
ATT_KV_HEADS = 2
ATT_GROUPS = ATT_HEADS // ATT_KV_HEADS
ATT_WIDTH = ATT_HEADS * HEAD_DIM
KV_WIDTH = ATT_KV_HEADS * HEAD_DIM
WINDOW = 128
BLOCK = 128
ATT_SCALE = HEAD_DIM ** -0.5
ROPE_THETA = 10000.0
ROPE_FREQS = HEAD_DIM // 4
POOL_WINDOWS = (2, 4, 8, 16)
POOL_GROUPS = len(POOL_WINDOWS)
POOL_WIDTH = D_MODEL // 2
POOL_GROUP_W = POOL_WIDTH // POOL_GROUPS
MIX_AB_IN = ATT_WIDTH + 2 * KV_WIDTH + POOL_WIDTH
MIX_AB_OUT = ATT_WIDTH + POOL_WIDTH
LRU_WIDTH = D_MODEL
LRU_BLOCKS = 8
LRU_BLOCK_W = LRU_WIDTH // LRU_BLOCKS
LRU_C = 8.0
CONV_W = 4
CONV_LEFT = (CONV_W - 1) // 2
D_FF = 2816
N_MOD = 9
LN_EPS = 1e-5
NEG_INF = -1e30
DEEPNORM_ALPHA = (2 * DEPTH) ** 0.25
DEEPNORM_BETA = (8 * DEPTH) ** -0.25
N_EVEN = (DEPTH + 1) // 2
N_ODD = DEPTH // 2

kernel_name = 'hybrid_window_attn_pool_rglru_macaron_dit'


def layer_norm(x, g, b):
    xf = x.astype(jnp.float32)
    mu = jnp.mean(xf, axis=-1, keepdims=True)
    var = jnp.mean(jnp.square(xf - mu), axis=-1, keepdims=True)
    return ((xf - mu) * lax.rsqrt(var + LN_EPS)).astype(x.dtype) * g + b


def residual_post_norm(x, y, g, b):
    return layer_norm(DEEPNORM_ALPHA * x + y, g, b)


def modulate(x, shift, scale):
    return x * (1.0 + scale) + shift


def swiglu(x, w_gate, w_up, w_down):
    return (jax.nn.silu(x @ w_gate) * (x @ w_up)) @ w_down


def axial_rope(rows):
    row = jnp.repeat(jnp.arange(rows, dtype=jnp.float32), GRID_W)
    col = jnp.tile(jnp.arange(GRID_W, dtype=jnp.float32), rows)
    inv = ROPE_THETA ** (-jnp.arange(ROPE_FREQS, dtype=jnp.float32) / ROPE_FREQS)
    ang = jnp.concatenate([row[:, None] * inv, col[:, None] * inv], axis=-1)
    return jnp.cos(ang), jnp.sin(ang)


def apply_rope(x, cos, sin):
    half = HEAD_DIM // 2
    cs = cos[None, :, None, :].astype(x.dtype)
    sn = sin[None, :, None, :].astype(x.dtype)
    x1, x2 = x[..., :half], x[..., half:]
    return jnp.concatenate([x1 * cs - x2 * sn, x2 * cs + x1 * sn], axis=-1)


def windowed_sink_attention(q, k, v, k_ctx, v_ctx, sink):
    bsz, n = q.shape[0], q.shape[1]
    nb = n // BLOCK
    qb = q.reshape(bsz, nb, BLOCK, ATT_KV_HEADS, ATT_GROUPS, HEAD_DIM)

    def band(t):
        tp = jnp.pad(t, ((0, 0), (BLOCK, BLOCK), (0, 0), (0, 0)))
        tp = tp.reshape(bsz, nb + 2, BLOCK, ATT_KV_HEADS, HEAD_DIM)
        return jnp.concatenate([tp[:, :-2], tp[:, 1:-1], tp[:, 2:]], axis=2)

    kb, vb = band(k), band(v)
    s_win = jnp.einsum('bnqkgd,bnskd->bnkgqs', qb, kb, preferred_element_type=jnp.float32) * ATT_SCALE
    qpos = jnp.arange(n).reshape(nb, BLOCK)
    kpos = (jnp.arange(nb)[:, None] - 1) * BLOCK + jnp.arange(3 * BLOCK)[None, :]
    valid = ((jnp.abs(qpos[:, :, None] - kpos[:, None, :]) <= WINDOW)
             & (kpos[:, None, :] >= 0) & (kpos[:, None, :] < n))
    s_win = jnp.where(valid[None, :, None, None], s_win, NEG_INF)
    s_ctx = jnp.einsum('bnqkgd,bckd->bnkgqc', qb, k_ctx, preferred_element_type=jnp.float32) * ATT_SCALE
    sink_l = sink.astype(jnp.float32).reshape(ATT_KV_HEADS, ATT_GROUPS)[None, None, :, :, None, None]
    m = jnp.maximum(jnp.maximum(s_win.max(-1, keepdims=True), s_ctx.max(-1, keepdims=True)), sink_l)
    e_win = jnp.exp(s_win - m)
    e_ctx = jnp.exp(s_ctx - m)
    denom = e_win.sum(-1, keepdims=True) + e_ctx.sum(-1, keepdims=True) + jnp.exp(sink_l - m)
    o = (jnp.einsum('bnkgqs,bnskd->bnqkgd', (e_win / denom).astype(v.dtype), vb)
         + jnp.einsum('bnkgqc,bckd->bnqkgd', (e_ctx / denom).astype(v.dtype), v_ctx))
    return o.reshape(bsz, n, ATT_WIDTH)


def context_sink_attention(q_ctx, k_ctx, v_ctx, sink):
    bsz, n_c = q_ctx.shape[0], q_ctx.shape[1]
    qg = q_ctx.reshape(bsz, n_c, ATT_KV_HEADS, ATT_GROUPS, HEAD_DIM)
    s = jnp.einsum('bqkgd,bskd->bkgqs', qg, k_ctx, preferred_element_type=jnp.float32) * ATT_SCALE
    sink_c = jnp.broadcast_to(sink.astype(jnp.float32).reshape(ATT_KV_HEADS, ATT_GROUPS)[None, :, :, None, None],
                              s.shape[:-1] + (1,))
    p = jax.nn.softmax(jnp.concatenate([s, sink_c], axis=-1), axis=-1)[..., :n_c]
    o = jnp.einsum('bkgqs,bskd->bqkgd', p.astype(v_ctx.dtype), v_ctx)
    return o.reshape(bsz, n_c, ATT_WIDTH)


def multiscale_pool(u, w_pool, pool_scale):
    bsz, n = u.shape[0], u.shape[1]
    uf = u.astype(jnp.float32)
    cs = jnp.pad(jnp.cumsum(uf, axis=1), ((0, 0), (1, 0), (0, 0)))
    t = jnp.arange(n)
    diffs = []
    for g, w in enumerate(POOL_WINDOWS):
        r = w // 2
        lo = jnp.maximum(t - r, 0)
        hi = jnp.minimum(t + r, n - 1) + 1
        sl = slice(g * POOL_GROUP_W, (g + 1) * POOL_GROUP_W)
        seg = cs[:, :, sl]
        mean = (seg[:, hi] - seg[:, lo]) / (hi - lo).astype(jnp.float32)[None, :, None]
        diffs.append(mean - uf[:, :, sl])
    d = jnp.stack(diffs, axis=2).astype(u.dtype)
    y = jnp.einsum('blgc,gce->blge', d, w_pool).reshape(bsz, n, POOL_WIDTH)
    return y * pool_scale


def attention_pool_mixer(h, hc, cos, sin, w_in, sink, w_pool, pool_scale, w_out, need_ctx_out):
    bsz, n = h.shape[0], h.shape[1]
    n_c = hc.shape[1]
    splits = [ATT_WIDTH, ATT_WIDTH + KV_WIDTH, ATT_WIDTH + 2 * KV_WIDTH]
    q, k, v, u = jnp.split(h @ w_in, splits, axis=-1)
    if need_ctx_out:
        q_c, k_c, v_c, u_c = jnp.split(hc @ w_in, splits, axis=-1)
    else:
        k_c, v_c = jnp.split(hc @ w_in[:, ATT_WIDTH:ATT_WIDTH + 2 * KV_WIDTH], 2, axis=-1)
    q = apply_rope(q.reshape(bsz, n, ATT_HEADS, HEAD_DIM), cos, sin)
    k = apply_rope(k.reshape(bsz, n, ATT_KV_HEADS, HEAD_DIM), cos, sin)
    v = v.reshape(bsz, n, ATT_KV_HEADS, HEAD_DIM)
    k_c = k_c.reshape(bsz, n_c, ATT_KV_HEADS, HEAD_DIM)
    v_c = v_c.reshape(bsz, n_c, ATT_KV_HEADS, HEAD_DIM)
    att = windowed_sink_attention(q, k, v, k_c, v_c, sink)
    pool = multiscale_pool(u, w_pool, pool_scale)
    out = jnp.concatenate([att, pool], axis=-1) @ w_out
    if not need_ctx_out:
        return out, None
    att_c = context_sink_attention(q_c, k_c, v_c, sink)
    pool_c = multiscale_pool(u_c, w_pool, pool_scale)
    out_c = jnp.concatenate([att_c, pool_c], axis=-1) @ w_out
    return out, out_c


def centred_depthwise_conv(x, w, b):
    n = x.shape[1]
    xp = jnp.pad(x, ((0, 0), (CONV_LEFT, CONV_W - 1 - CONV_LEFT), (0, 0)))
    y = b
    for tap in range(CONV_W):
        y = y + xp[:, tap:tap + n] * w[tap]
    return y


def rglru_coeffs(u, wa, ba, wx, bx, lam):
    bsz, n = u.shape[0], u.shape[1]
    ub = u.reshape(bsz, n, LRU_BLOCKS, LRU_BLOCK_W)
    r = jax.nn.sigmoid(jnp.einsum('blhi,hij->blhj', ub, wa).reshape(bsz, n, LRU_WIDTH) + ba)
    gi = jax.nn.sigmoid(jnp.einsum('blhi,hij->blhj', ub, wx).reshape(bsz, n, LRU_WIDTH) + bx)
    log_a = -LRU_C * r.astype(jnp.float32) * jax.nn.softplus(-lam.astype(jnp.float32))
    a = jnp.exp(log_a)
    b = jnp.sqrt(-jnp.expm1(2.0 * log_a)) * (gi * u).astype(jnp.float32)
    return a, b


def linear_scan(a, b, h0):
    b = b.at[:, 0].add(a[:, 0] * h0)

    def combine(left, right):
        return left[0] * right[0], right[0] * left[1] + right[1]

    _, h = lax.associative_scan(combine, (a, b), axis=1)
    return h


def recurrent_mixer(h, hc, w_in, conv_w, conv_b, wa, ba, wx, bx, lam, w_out, need_ctx_out):
    gate, u = jnp.split(h @ w_in, 2, axis=-1)
    if need_ctx_out:
        gate_c, u_c = jnp.split(hc @ w_in, 2, axis=-1)
    else:
        u_c = hc @ w_in[:, LRU_WIDTH:]
    u = centred_depthwise_conv(u, conv_w, conv_b)
    u_c = centred_depthwise_conv(u_c, conv_w, conv_b)
    h0 = jnp.zeros((h.shape[0], LRU_WIDTH), jnp.float32)
    ys, ys_c = [], []
    for direction in range(2):
        a, b = rglru_coeffs(u, wa[direction], ba[direction], wx[direction], bx[direction], lam[direction])
        a_c, b_c = rglru_coeffs(u_c, wa[direction], ba[direction], wx[direction], bx[direction], lam[direction])
        if direction == 1:
            a, b, a_c, b_c = (jnp.flip(a, 1), jnp.flip(b, 1), jnp.flip(a_c, 1), jnp.flip(b_c, 1))
        s_c = linear_scan(a_c, b_c, h0)
        s = linear_scan(a, b, s_c[:, -1])
        if direction == 1:
            s, s_c = jnp.flip(s, 1), jnp.flip(s_c, 1)
        ys.append(s)
        ys_c.append(s_c)
    y = (ys[0] + ys[1]).astype(h.dtype)
    out = (jax.nn.gelu(gate) * y) @ w_out
    if not need_ctx_out:
        return out, None
    y_c = (ys_c[0] + ys_c[1]).astype(hc.dtype)
    out_c = (jax.nn.gelu(gate_c) * y_c) @ w_out
    return out, out_c


def _fwd_setup_inputs(seed: int = 0) -> dict:
    key = jax.random.key(seed)
    ks = jax.random.split(key, 26)

    def nrm(i, shape, scale):
        return jax.random.normal(ks[i], shape, jnp.float32) * scale

    lam_u = jax.random.uniform(ks[23], (N_ODD, 2, LRU_WIDTH), jnp.float32, 0.9, 0.999)
    return {
        'x': nrm(0, (BATCH, SEQ, D_MODEL), 1.0),
        'c': nrm(1, (BATCH, D_MODEL), 1.0),
        'ctx': nrm(2, (BATCH, CTX_LEN, D_MODEL), 1.0),
        'c_ctx': nrm(3, (D_MODEL,), 1.0),
        'w_mod': nrm(4, (DEPTH, D_MODEL, N_MOD * D_MODEL), 0.5 * D_MODEL ** -0.5),
        'b_mod': nrm(5, (DEPTH, N_MOD * D_MODEL), 0.02),
        'ln_g': 1.0 + nrm(6, (DEPTH, 3, D_MODEL), 0.02),
        'ln_b': nrm(7, (DEPTH, 3, D_MODEL), 0.02),
        'ffn_w_gate': nrm(8, (DEPTH, 2, D_MODEL, D_FF), D_MODEL ** -0.5),
        'ffn_w_up': nrm(9, (DEPTH, 2, D_MODEL, D_FF), D_MODEL ** -0.5),
        'ffn_w_down': nrm(10, (DEPTH, 2, D_FF, D_MODEL), DEEPNORM_BETA * D_FF ** -0.5),
        'mix_ab_w_in': nrm(11, (N_EVEN, D_MODEL, MIX_AB_IN), D_MODEL ** -0.5),
        'attn_sink': nrm(12, (N_EVEN, ATT_HEADS), 0.5),
        'pool_w': nrm(13, (N_EVEN, POOL_GROUPS, POOL_GROUP_W, POOL_GROUP_W), POOL_GROUP_W ** -0.5),
        'pool_scale': 1.0 + nrm(14, (N_EVEN, POOL_WIDTH), 0.1),
        'mix_ab_w_out': nrm(15, (N_EVEN, MIX_AB_OUT, D_MODEL), DEEPNORM_BETA * MIX_AB_OUT ** -0.5),
        'lru_w_in': nrm(16, (N_ODD, D_MODEL, 2 * LRU_WIDTH), D_MODEL ** -0.5),
        'lru_conv_w': nrm(17, (N_ODD, CONV_W, LRU_WIDTH), CONV_W ** -0.5),
        'lru_conv_b': nrm(18, (N_ODD, LRU_WIDTH), 0.02),
        'lru_wa': nrm(19, (N_ODD, 2, LRU_BLOCKS, LRU_BLOCK_W, LRU_BLOCK_W), LRU_BLOCK_W ** -0.5),
        'lru_ba': nrm(20, (N_ODD, 2, LRU_WIDTH), 0.02),
        'lru_wx': nrm(21, (N_ODD, 2, LRU_BLOCKS, LRU_BLOCK_W, LRU_BLOCK_W), LRU_BLOCK_W ** -0.5),
        'lru_bx': nrm(22, (N_ODD, 2, LRU_WIDTH), 0.02),
        'lru_lambda': jnp.log(lam_u) - jnp.log1p(-lam_u),
        'lru_w_out': nrm(24, (N_ODD, LRU_WIDTH, D_MODEL), DEEPNORM_BETA * LRU_WIDTH ** -0.5),
    }


def _fwd_reference(x, c, ctx, c_ctx, w_mod, b_mod, ln_g, ln_b, ffn_w_gate, ffn_w_up, ffn_w_down,
              mix_ab_w_in, attn_sink, pool_w, pool_scale, mix_ab_w_out,
              lru_w_in, lru_conv_w, lru_conv_b, lru_wa, lru_ba, lru_wx, lru_bx, lru_lambda, lru_w_out):
    rows = x.shape[1] // GRID_W
    cos, sin = axial_rope(rows)
    h, hc = x, ctx
    for layer in range(DEPTH):
        ctx_out = layer < DEPTH - 1
        m = jnp.split((jax.nn.silu(c) @ w_mod[layer] + b_mod[layer])[:, None, :], N_MOD, axis=-1)
        mc = jnp.split((jax.nn.silu(c_ctx) @ w_mod[layer] + b_mod[layer])[None, None, :], N_MOD, axis=-1)
        ffn1 = (ffn_w_gate[layer, 0], ffn_w_up[layer, 0], ffn_w_down[layer, 0])
        ffn2 = (ffn_w_gate[layer, 1], ffn_w_up[layer, 1], ffn_w_down[layer, 1])
        h = residual_post_norm(h, 0.5 * m[2] * swiglu(modulate(h, m[0], m[1]), *ffn1), ln_g[layer, 0], ln_b[layer, 0])
        hc = residual_post_norm(hc, 0.5 * mc[2] * swiglu(modulate(hc, mc[0], mc[1]), *ffn1), ln_g[layer, 0], ln_b[layer, 0])
        h_in = modulate(h, m[3], m[4])
        hc_in = modulate(hc, mc[3], mc[4])
        idx = layer // 2
        if layer % 2 == 0:
            y, y_c = attention_pool_mixer(h_in, hc_in, cos, sin, mix_ab_w_in[idx], attn_sink[idx],
                                          pool_w[idx], pool_scale[idx], mix_ab_w_out[idx], ctx_out)
        else:
            y, y_c = recurrent_mixer(h_in, hc_in, lru_w_in[idx], lru_conv_w[idx], lru_conv_b[idx],
                                     lru_wa[idx], lru_ba[idx], lru_wx[idx], lru_bx[idx], lru_lambda[idx],
                                     lru_w_out[idx], ctx_out)
        h = residual_post_norm(h, m[5] * y, ln_g[layer, 1], ln_b[layer, 1])
        h = residual_post_norm(h, 0.5 * m[8] * swiglu(modulate(h, m[6], m[7]), *ffn2), ln_g[layer, 2], ln_b[layer, 2])
        if ctx_out:
            hc = residual_post_norm(hc, mc[5] * y_c, ln_g[layer, 1], ln_b[layer, 1])
            hc = residual_post_norm(hc, 0.5 * mc[8] * swiglu(modulate(hc, mc[6], mc[7]), *ffn2), ln_g[layer, 2], ln_b[layer, 2])
    return h


import jax as _jax
import jax.numpy as _jnp

TWIN_FORMAT = 'train_step'
FWD_PARAMS = ['x', 'c', 'ctx', 'c_ctx', 'w_mod', 'b_mod', 'ln_g', 'ln_b', 'ffn_w_gate', 'ffn_w_up', 'ffn_w_down', 'mix_ab_w_in', 'attn_sink', 'pool_w', 'pool_scale', 'mix_ab_w_out', 'lru_w_in', 'lru_conv_w', 'lru_conv_b', 'lru_wa', 'lru_ba', 'lru_wx', 'lru_bx', 'lru_lambda', 'lru_w_out']
TWIN_WEIGHTS = ['c_ctx', 'w_mod', 'b_mod', 'ln_g', 'ln_b', 'ffn_w_gate', 'ffn_w_up', 'ffn_w_down', 'mix_ab_w_in', 'attn_sink', 'pool_w', 'pool_scale', 'mix_ab_w_out', 'lru_w_in', 'lru_conv_w', 'lru_conv_b', 'lru_wa', 'lru_ba', 'lru_wx', 'lru_bx', 'lru_lambda', 'lru_w_out']
TWIN_DIFF_INPUT = 'x'
TWIN_INPUTS = ['x', 'c', 'ctx', 'c_ctx', 'w_mod', 'b_mod', 'ln_g', 'ln_b', 'ffn_w_gate', 'ffn_w_up', 'ffn_w_down', 'mix_ab_w_in', 'attn_sink', 'pool_w', 'pool_scale', 'mix_ab_w_out', 'lru_w_in', 'lru_conv_w', 'lru_conv_b', 'lru_wa', 'lru_ba', 'lru_wx', 'lru_bx', 'lru_lambda', 'lru_w_out', 'loss_target', 'm_c_ctx', 'm_w_mod', 'm_b_mod', 'm_ln_g', 'm_ln_b', 'm_ffn_w_gate', 'm_ffn_w_up', 'm_ffn_w_down', 'm_mix_ab_w_in', 'm_attn_sink', 'm_pool_w', 'm_pool_scale', 'm_mix_ab_w_out', 'm_lru_w_in', 'm_lru_conv_w', 'm_lru_conv_b', 'm_lru_wa', 'm_lru_ba', 'm_lru_wx', 'm_lru_bx', 'm_lru_lambda', 'm_lru_w_out', 'v_c_ctx', 'v_w_mod', 'v_b_mod', 'v_ln_g', 'v_ln_b', 'v_ffn_w_gate', 'v_ffn_w_up', 'v_ffn_w_down', 'v_mix_ab_w_in', 'v_attn_sink', 'v_pool_w', 'v_pool_scale', 'v_mix_ab_w_out', 'v_lru_w_in', 'v_lru_conv_w', 'v_lru_conv_b', 'v_lru_wa', 'v_lru_ba', 'v_lru_wx', 'v_lru_bx', 'v_lru_lambda', 'v_lru_w_out']
TWIN_OUTPUTS = ['loss', 'grad_x', 'grad_c_ctx', 'grad_w_mod', 'grad_b_mod', 'grad_ln_g', 'grad_ln_b', 'grad_ffn_w_gate', 'grad_ffn_w_up', 'grad_ffn_w_down', 'grad_mix_ab_w_in', 'grad_attn_sink', 'grad_pool_w', 'grad_pool_scale', 'grad_mix_ab_w_out', 'grad_lru_w_in', 'grad_lru_conv_w', 'grad_lru_conv_b', 'grad_lru_wa', 'grad_lru_ba', 'grad_lru_wx', 'grad_lru_bx', 'grad_lru_lambda', 'grad_lru_w_out', 'delta_c_ctx', 'delta_w_mod', 'delta_b_mod', 'delta_ln_g', 'delta_ln_b', 'delta_ffn_w_gate', 'delta_ffn_w_up', 'delta_ffn_w_down', 'delta_mix_ab_w_in', 'delta_attn_sink', 'delta_pool_w', 'delta_pool_scale', 'delta_mix_ab_w_out', 'delta_lru_w_in', 'delta_lru_conv_w', 'delta_lru_conv_b', 'delta_lru_wa', 'delta_lru_ba', 'delta_lru_wx', 'delta_lru_bx', 'delta_lru_lambda', 'delta_lru_w_out', 'new_m_c_ctx', 'new_m_w_mod', 'new_m_b_mod', 'new_m_ln_g', 'new_m_ln_b', 'new_m_ffn_w_gate', 'new_m_ffn_w_up', 'new_m_ffn_w_down', 'new_m_mix_ab_w_in', 'new_m_attn_sink', 'new_m_pool_w', 'new_m_pool_scale', 'new_m_mix_ab_w_out', 'new_m_lru_w_in', 'new_m_lru_conv_w', 'new_m_lru_conv_b', 'new_m_lru_wa', 'new_m_lru_ba', 'new_m_lru_wx', 'new_m_lru_bx', 'new_m_lru_lambda', 'new_m_lru_w_out', 'new_v_c_ctx', 'new_v_w_mod', 'new_v_b_mod', 'new_v_ln_g', 'new_v_ln_b', 'new_v_ffn_w_gate', 'new_v_ffn_w_up', 'new_v_ffn_w_down', 'new_v_mix_ab_w_in', 'new_v_attn_sink', 'new_v_pool_w', 'new_v_pool_scale', 'new_v_mix_ab_w_out', 'new_v_lru_w_in', 'new_v_lru_conv_w', 'new_v_lru_conv_b', 'new_v_lru_wa', 'new_v_lru_ba', 'new_v_lru_wx', 'new_v_lru_bx', 'new_v_lru_lambda', 'new_v_lru_w_out']
TWIN_LEAF_KINDS = {'loss': 'loss', 'grad_x': 'grad_x', 'grad_c_ctx': 'grad_w', 'grad_w_mod': 'grad_w', 'grad_b_mod': 'grad_w', 'grad_ln_g': 'grad_w', 'grad_ln_b': 'grad_w', 'grad_ffn_w_gate': 'grad_w', 'grad_ffn_w_up': 'grad_w', 'grad_ffn_w_down': 'grad_w', 'grad_mix_ab_w_in': 'grad_w', 'grad_attn_sink': 'grad_w', 'grad_pool_w': 'grad_w', 'grad_pool_scale': 'grad_w', 'grad_mix_ab_w_out': 'grad_w', 'grad_lru_w_in': 'grad_w', 'grad_lru_conv_w': 'grad_w', 'grad_lru_conv_b': 'grad_w', 'grad_lru_wa': 'grad_w', 'grad_lru_ba': 'grad_w', 'grad_lru_wx': 'grad_w', 'grad_lru_bx': 'grad_w', 'grad_lru_lambda': 'grad_w', 'grad_lru_w_out': 'grad_w', 'delta_c_ctx': 'delta_w', 'delta_w_mod': 'delta_w', 'delta_b_mod': 'delta_w', 'delta_ln_g': 'delta_w', 'delta_ln_b': 'delta_w', 'delta_ffn_w_gate': 'delta_w', 'delta_ffn_w_up': 'delta_w', 'delta_ffn_w_down': 'delta_w', 'delta_mix_ab_w_in': 'delta_w', 'delta_attn_sink': 'delta_w', 'delta_pool_w': 'delta_w', 'delta_pool_scale': 'delta_w', 'delta_mix_ab_w_out': 'delta_w', 'delta_lru_w_in': 'delta_w', 'delta_lru_conv_w': 'delta_w', 'delta_lru_conv_b': 'delta_w', 'delta_lru_wa': 'delta_w', 'delta_lru_ba': 'delta_w', 'delta_lru_wx': 'delta_w', 'delta_lru_bx': 'delta_w', 'delta_lru_lambda': 'delta_w', 'delta_lru_w_out': 'delta_w', 'new_m_c_ctx': 'new_m', 'new_m_w_mod': 'new_m', 'new_m_b_mod': 'new_m', 'new_m_ln_g': 'new_m', 'new_m_ln_b': 'new_m', 'new_m_ffn_w_gate': 'new_m', 'new_m_ffn_w_up': 'new_m', 'new_m_ffn_w_down': 'new_m', 'new_m_mix_ab_w_in': 'new_m', 'new_m_attn_sink': 'new_m', 'new_m_pool_w': 'new_m', 'new_m_pool_scale': 'new_m', 'new_m_mix_ab_w_out': 'new_m', 'new_m_lru_w_in': 'new_m', 'new_m_lru_conv_w': 'new_m', 'new_m_lru_conv_b': 'new_m', 'new_m_lru_wa': 'new_m', 'new_m_lru_ba': 'new_m', 'new_m_lru_wx': 'new_m', 'new_m_lru_bx': 'new_m', 'new_m_lru_lambda': 'new_m', 'new_m_lru_w_out': 'new_m', 'new_v_c_ctx': 'new_v', 'new_v_w_mod': 'new_v', 'new_v_b_mod': 'new_v', 'new_v_ln_g': 'new_v', 'new_v_ln_b': 'new_v', 'new_v_ffn_w_gate': 'new_v', 'new_v_ffn_w_up': 'new_v', 'new_v_ffn_w_down': 'new_v', 'new_v_mix_ab_w_in': 'new_v', 'new_v_attn_sink': 'new_v', 'new_v_pool_w': 'new_v', 'new_v_pool_scale': 'new_v', 'new_v_mix_ab_w_out': 'new_v', 'new_v_lru_w_in': 'new_v', 'new_v_lru_conv_w': 'new_v', 'new_v_lru_conv_b': 'new_v', 'new_v_lru_wa': 'new_v', 'new_v_lru_ba': 'new_v', 'new_v_lru_wx': 'new_v', 'new_v_lru_bx': 'new_v', 'new_v_lru_lambda': 'new_v', 'new_v_lru_w_out': 'new_v'}


def _forward(args):
    return _fwd_reference(*[args[k] for k in FWD_PARAMS])


def _output_shape():
    out = _jax.eval_shape(lambda: _forward(_fwd_setup_inputs(0)))
    return out.shape, out.dtype

N_MICROBATCH = 1
ADAM_LR = 0.001
ADAM_B1 = 0.9
ADAM_B2 = 0.999
ADAM_EPS = 1e-08
ADAM_WD = 0.01
ADAM_STEP = 10
PER_EXAMPLE_BATCH_AXIS = {'x': 0, 'c': 0, 'ctx': 0, 'loss_target': 0}
SHARED_INPUTS = []
_WEIGHT_DTYPES = {'c_ctx': _jnp.float32, 'w_mod': _jnp.float32, 'b_mod': _jnp.float32, 'ln_g': _jnp.float32, 'ln_b': _jnp.float32, 'ffn_w_gate': _jnp.float32, 'ffn_w_up': _jnp.float32, 'ffn_w_down': _jnp.float32, 'mix_ab_w_in': _jnp.float32, 'attn_sink': _jnp.float32, 'pool_w': _jnp.float32, 'pool_scale': _jnp.float32, 'mix_ab_w_out': _jnp.float32, 'lru_w_in': _jnp.float32, 'lru_conv_w': _jnp.float32, 'lru_conv_b': _jnp.float32, 'lru_wa': _jnp.float32, 'lru_ba': _jnp.float32, 'lru_wx': _jnp.float32, 'lru_bx': _jnp.float32, 'lru_lambda': _jnp.float32, 'lru_w_out': _jnp.float32}
MOMENT_SCALE = {'c_ctx': 4.342700e-03, 'w_mod': 1.899804e-02, 'b_mod': 3.126869e-02, 'ln_g': 1.312180e+01, 'ln_b': 6.840066e-01, 'ffn_w_gate': 4.323468e-03, 'ffn_w_up': 4.210431e-03, 'ffn_w_down': 1.398710e-02, 'mix_ab_w_in': 1.263703e-02, 'attn_sink': 4.694415e-05, 'pool_w': 1.844815e-02, 'pool_scale': 1.931672e-02, 'mix_ab_w_out': 2.790438e-02, 'lru_w_in': 2.478221e-02, 'lru_conv_w': 2.674670e-02, 'lru_conv_b': 6.116770e-02, 'lru_wa': 2.480513e-03, 'lru_ba': 3.252364e-03, 'lru_wx': 4.284927e-03, 'lru_bx': 6.033316e-03, 'lru_lambda': 8.035676e-03, 'lru_w_out': 4.934090e-02}


def _to_microbatches(a, axis):
    t = _jnp.moveaxis(a, axis, 0)
    t = t.reshape((N_MICROBATCH, t.shape[0] // N_MICROBATCH) + t.shape[1:])
    return _jnp.moveaxis(t, 1, axis + 1)


def setup_inputs(seed: int = 0) -> dict:
    inp = _fwd_setup_inputs(seed)
    key = _jax.random.fold_in(_jax.random.key(seed), 7919)
    shape, _ = _output_shape()
    out = dict(inp)
    out["loss_target"] = _jax.random.normal(_jax.random.fold_in(key, 0), shape, _jnp.float32)
    for i, name in enumerate(TWIN_WEIGHTS):
        w = inp[name].astype(_jnp.float32)
        if MOMENT_SCALE is None:
            s = _jnp.sqrt(_jnp.mean(_jnp.square(w)) + 1e-30)
        else:
            s = MOMENT_SCALE[name]
        km, kv = _jax.random.split(_jax.random.fold_in(key, i + 1))
        out[name] = w
        out["m_" + name] = s * _jax.random.normal(km, w.shape, _jnp.float32)
        out["v_" + name] = (s * s) * _jax.random.uniform(kv, w.shape, _jnp.float32, 0.5, 1.5)
    if N_MICROBATCH > 1:
        for name, axis in PER_EXAMPLE_BATCH_AXIS.items():
            out[name] = _to_microbatches(out[name], axis)
    return {'x': out['x'], 'c': out['c'], 'ctx': out['ctx'], 'c_ctx': out['c_ctx'], 'w_mod': out['w_mod'], 'b_mod': out['b_mod'], 'ln_g': out['ln_g'], 'ln_b': out['ln_b'], 'ffn_w_gate': out['ffn_w_gate'], 'ffn_w_up': out['ffn_w_up'], 'ffn_w_down': out['ffn_w_down'], 'mix_ab_w_in': out['mix_ab_w_in'], 'attn_sink': out['attn_sink'], 'pool_w': out['pool_w'], 'pool_scale': out['pool_scale'], 'mix_ab_w_out': out['mix_ab_w_out'], 'lru_w_in': out['lru_w_in'], 'lru_conv_w': out['lru_conv_w'], 'lru_conv_b': out['lru_conv_b'], 'lru_wa': out['lru_wa'], 'lru_ba': out['lru_ba'], 'lru_wx': out['lru_wx'], 'lru_bx': out['lru_bx'], 'lru_lambda': out['lru_lambda'], 'lru_w_out': out['lru_w_out'], 'loss_target': out['loss_target'], 'm_c_ctx': out['m_c_ctx'], 'm_w_mod': out['m_w_mod'], 'm_b_mod': out['m_b_mod'], 'm_ln_g': out['m_ln_g'], 'm_ln_b': out['m_ln_b'], 'm_ffn_w_gate': out['m_ffn_w_gate'], 'm_ffn_w_up': out['m_ffn_w_up'], 'm_ffn_w_down': out['m_ffn_w_down'], 'm_mix_ab_w_in': out['m_mix_ab_w_in'], 'm_attn_sink': out['m_attn_sink'], 'm_pool_w': out['m_pool_w'], 'm_pool_scale': out['m_pool_scale'], 'm_mix_ab_w_out': out['m_mix_ab_w_out'], 'm_lru_w_in': out['m_lru_w_in'], 'm_lru_conv_w': out['m_lru_conv_w'], 'm_lru_conv_b': out['m_lru_conv_b'], 'm_lru_wa': out['m_lru_wa'], 'm_lru_ba': out['m_lru_ba'], 'm_lru_wx': out['m_lru_wx'], 'm_lru_bx': out['m_lru_bx'], 'm_lru_lambda': out['m_lru_lambda'], 'm_lru_w_out': out['m_lru_w_out'], 'v_c_ctx': out['v_c_ctx'], 'v_w_mod': out['v_w_mod'], 'v_b_mod': out['v_b_mod'], 'v_ln_g': out['v_ln_g'], 'v_ln_b': out['v_ln_b'], 'v_ffn_w_gate': out['v_ffn_w_gate'], 'v_ffn_w_up': out['v_ffn_w_up'], 'v_ffn_w_down': out['v_ffn_w_down'], 'v_mix_ab_w_in': out['v_mix_ab_w_in'], 'v_attn_sink': out['v_attn_sink'], 'v_pool_w': out['v_pool_w'], 'v_pool_scale': out['v_pool_scale'], 'v_mix_ab_w_out': out['v_mix_ab_w_out'], 'v_lru_w_in': out['v_lru_w_in'], 'v_lru_conv_w': out['v_lru_conv_w'], 'v_lru_conv_b': out['v_lru_conv_b'], 'v_lru_wa': out['v_lru_wa'], 'v_lru_ba': out['v_lru_ba'], 'v_lru_wx': out['v_lru_wx'], 'v_lru_bx': out['v_lru_bx'], 'v_lru_lambda': out['v_lru_lambda'], 'v_lru_w_out': out['v_lru_w_out']}


def _loss(weights, diff, rest, loss_target):
    with _jax.named_scope("forward"):
        args = {**rest, TWIN_DIFF_INPUT: diff, **{k: w.astype(_WEIGHT_DTYPES[k]) for k, w in weights.items()}}
        y = _forward(args)
    with _jax.named_scope("loss_head"):
        err = _jnp.square(y.astype(_jnp.float32) - loss_target)
        return 0.5 * _jnp.sum(_jnp.mean(err, axis=-1)) if err.ndim else 0.5 * err


def _adamw(w, g, m, v):
    m = ADAM_B1 * m + (1.0 - ADAM_B1) * g
    v = ADAM_B2 * v + (1.0 - ADAM_B2) * _jnp.square(g)
    m_hat = m / (1.0 - ADAM_B1 ** ADAM_STEP)
    v_hat = v / (1.0 - ADAM_B2 ** ADAM_STEP)
    delta = -ADAM_LR * (m_hat / (_jnp.sqrt(v_hat) + ADAM_EPS) + ADAM_WD * w)
    return delta, m, v


def reference(x, c, ctx, c_ctx, w_mod, b_mod, ln_g, ln_b, ffn_w_gate, ffn_w_up, ffn_w_down, mix_ab_w_in, attn_sink, pool_w, pool_scale, mix_ab_w_out, lru_w_in, lru_conv_w, lru_conv_b, lru_wa, lru_ba, lru_wx, lru_bx, lru_lambda, lru_w_out, loss_target, m_c_ctx, m_w_mod, m_b_mod, m_ln_g, m_ln_b, m_ffn_w_gate, m_ffn_w_up, m_ffn_w_down, m_mix_ab_w_in, m_attn_sink, m_pool_w, m_pool_scale, m_mix_ab_w_out, m_lru_w_in, m_lru_conv_w, m_lru_conv_b, m_lru_wa, m_lru_ba, m_lru_wx, m_lru_bx, m_lru_lambda, m_lru_w_out, v_c_ctx, v_w_mod, v_b_mod, v_ln_g, v_ln_b, v_ffn_w_gate, v_ffn_w_up, v_ffn_w_down, v_mix_ab_w_in, v_attn_sink, v_pool_w, v_pool_scale, v_mix_ab_w_out, v_lru_w_in, v_lru_conv_w, v_lru_conv_b, v_lru_wa, v_lru_ba, v_lru_wx, v_lru_bx, v_lru_lambda, v_lru_w_out):
    given = dict(x=x, c=c, ctx=ctx, c_ctx=c_ctx, w_mod=w_mod, b_mod=b_mod, ln_g=ln_g, ln_b=ln_b, ffn_w_gate=ffn_w_gate, ffn_w_up=ffn_w_up, ffn_w_down=ffn_w_down, mix_ab_w_in=mix_ab_w_in, attn_sink=attn_sink, pool_w=pool_w, pool_scale=pool_scale, mix_ab_w_out=mix_ab_w_out, lru_w_in=lru_w_in, lru_conv_w=lru_conv_w, lru_conv_b=lru_conv_b, lru_wa=lru_wa, lru_ba=lru_ba, lru_wx=lru_wx, lru_bx=lru_bx, lru_lambda=lru_lambda, lru_w_out=lru_w_out, loss_target=loss_target, m_c_ctx=m_c_ctx, m_w_mod=m_w_mod, m_b_mod=m_b_mod, m_ln_g=m_ln_g, m_ln_b=m_ln_b, m_ffn_w_gate=m_ffn_w_gate, m_ffn_w_up=m_ffn_w_up, m_ffn_w_down=m_ffn_w_down, m_mix_ab_w_in=m_mix_ab_w_in, m_attn_sink=m_attn_sink, m_pool_w=m_pool_w, m_pool_scale=m_pool_scale, m_mix_ab_w_out=m_mix_ab_w_out, m_lru_w_in=m_lru_w_in, m_lru_conv_w=m_lru_conv_w, m_lru_conv_b=m_lru_conv_b, m_lru_wa=m_lru_wa, m_lru_ba=m_lru_ba, m_lru_wx=m_lru_wx, m_lru_bx=m_lru_bx, m_lru_lambda=m_lru_lambda, m_lru_w_out=m_lru_w_out, v_c_ctx=v_c_ctx, v_w_mod=v_w_mod, v_b_mod=v_b_mod, v_ln_g=v_ln_g, v_ln_b=v_ln_b, v_ffn_w_gate=v_ffn_w_gate, v_ffn_w_up=v_ffn_w_up, v_ffn_w_down=v_ffn_w_down, v_mix_ab_w_in=v_mix_ab_w_in, v_attn_sink=v_attn_sink, v_pool_w=v_pool_w, v_pool_scale=v_pool_scale, v_mix_ab_w_out=v_mix_ab_w_out, v_lru_w_in=v_lru_w_in, v_lru_conv_w=v_lru_conv_w, v_lru_conv_b=v_lru_conv_b, v_lru_wa=v_lru_wa, v_lru_ba=v_lru_ba, v_lru_wx=v_lru_wx, v_lru_bx=v_lru_bx, v_lru_lambda=v_lru_lambda, v_lru_w_out=v_lru_w_out)
    weights = {n: given[n] for n in TWIN_WEIGHTS}
    shared = {n: given[n] for n in SHARED_INPUTS}
    per_example = {n: given[n] for n in ['x', 'c', 'ctx']}
    grad_fn = _jax.value_and_grad(_loss, argnums=(0, 1))

    def one_microbatch(ex, loss_target):
        ex = dict(ex)
        diff = ex.pop(TWIN_DIFF_INPUT)
        return grad_fn(weights, diff, {**shared, **ex}, loss_target)

    if N_MICROBATCH == 1:
        loss, (grad_w, grad_x) = one_microbatch(per_example, given["loss_target"])
    else:
        def body(carry, xs):
            loss_sum, grad_sum = carry
            l_k, (gw_k, gx_k) = one_microbatch(xs[0], xs[1])
            with _jax.named_scope("update"):
                return (loss_sum + l_k, _jax.tree.map(_jnp.add, grad_sum, gw_k)), gx_k

        init = (_jnp.zeros((), _jnp.float32), _jax.tree.map(_jnp.zeros_like, weights))
        (loss, grad_w), grad_x = _jax.lax.scan(body, init, (per_example, given["loss_target"]))
    with _jax.named_scope("update"):
        delta_w, new_m, new_v = {}, {}, {}
        for n in TWIN_WEIGHTS:
            delta_w[n], new_m[n], new_v[n] = _adamw(weights[n], grad_w[n], given["m_" + n], given["v_" + n])
    return (loss, grad_x, *[grad_w[n] for n in TWIN_WEIGHTS], *[delta_w[n] for n in TWIN_WEIGHTS],
            *[new_m[n] for n in TWIN_WEIGHTS], *[new_v[n] for n in TWIN_WEIGHTS])
```

```python
import functools
import math

import jax
import jax.numpy as jnp
from jax import lax
from jax.experimental import pallas as pl
from jax.experimental.pallas import tpu as pltpu

F32 = jnp.float32
BF16 = jnp.bfloat16
MESH = pl.DeviceIdType.MESH

D = 1024
N_MOD = 9
N_DEV = 8
HEAD_DIM = 64
ATT_HEADS = 8
KV_HEADS = 2
ATT_W = 512
BLK = 128
ATT_SCALE = HEAD_DIM ** -0.5
GRID_W = 64
ROPE_FREQS = HEAD_DIM // 4
ROPE_THETA = 10000.0
POOL_R = (1, 2, 4, 8)
LRU_C = 8.0
LN_EPS = 1e-5
NEG_INF = -1e30
ALPHA = 4.0 ** 0.25
LR, B1, B2, EPS, WD, STEP = 0.001, 0.9, 0.999, 1e-08, 0.01, 10
VMEM_LIMIT = 56 * 1024 * 1024
ROW_TILE = 512


def _params(sem=None):
    if sem is None:
        return pltpu.CompilerParams(vmem_limit_bytes=VMEM_LIMIT)
    return pltpu.CompilerParams(dimension_semantics=sem, vmem_limit_bytes=VMEM_LIMIT)


def _sigmoid(x):
    return 1.0 / (1.0 + jnp.exp(-x))


def _dot(a, b):
    return jnp.dot(a.astype(BF16), b.astype(BF16), preferred_element_type=F32)


def _dot_nt(a, b):
    return lax.dot_general(a.astype(BF16), b.astype(BF16), (((1,), (1,)), ((), ())), preferred_element_type=F32)


def _dot_tn(a, b):
    return lax.dot_general(a.astype(BF16), b.astype(BF16), (((0,), (0,)), ((), ())), preferred_element_type=F32)


def _pick(n, cap):
    best = None
    for m in range(128, min(n, cap) + 1, 128):
        if n % m == 0:
            best = m
    return n if best is None else best


def _chunks(width, step=256):
    out, c = [], 0
    while c < width:
        w = min(step, width - c)
        out.append((c, w))
        c += w
    return out


class _Cfg:
    def __init__(self, n_lat, n_ctx):
        self.n_lat, self.n_ctx = n_lat, n_ctx
        self.t_lat, self.t_ctx = 2 * n_lat, 2 * n_ctx
        self.T = self.t_lat + self.t_ctx
        self.tm = min(ROW_TILE, self.t_ctx)
        assert n_lat % self.tm == 0 and self.t_ctx % self.tm == 0 and n_lat >= 3 * BLK and n_ctx % BLK == 0
        self.nt = self.T // self.tm
        self.nlt = n_lat // self.tm
        self.ctx_blk = self.t_lat // n_ctx

    def seg(self, i):
        return jnp.minimum(i // self.nlt, 2)

    def first_of_seg(self, i):
        return jnp.where(i < 2 * self.nlt, i % self.nlt == 0, i == 2 * self.nlt)


def _modulate(cfg, h, mod, shift_idx, scale_idx, name):
    tm = cfg.tm

    def body(h_ref, mod_ref, o_ref):
        sh = mod_ref[shift_idx:shift_idx + 1, :]
        sc = mod_ref[scale_idx:scale_idx + 1, :]
        o_ref[...] = (h_ref[...] * (1.0 + sc) + sh).astype(BF16)

    return pl.pallas_call(
        body, grid=(cfg.nt,), name=name,
        in_specs=[pl.BlockSpec((tm, D), lambda i: (i, 0)),
                  pl.BlockSpec((None, N_MOD, D), lambda i: (cfg.seg(i), 0, 0))],
        out_specs=pl.BlockSpec((tm, D), lambda i: (i, 0)),
        out_shape=jax.ShapeDtypeStruct((cfg.T, D), BF16),
        compiler_params=_params(("parallel",)),
    )(h, mod)


def _ln_fwd(cfg, h, y, mod, gate_idx, coef, lng, lnb, mod_next, next_idx, name):
    tm = cfg.tm
    has_next = next_idx is not None

    def body(*refs):
        if has_next:
            h_ref, y_ref, mod_ref, g_ref, b_ref, modn_ref, hn_ref, xhat_ref, rstd_ref, xin_ref = refs
        else:
            h_ref, y_ref, mod_ref, g_ref, b_ref, hn_ref, xhat_ref, rstd_ref = refs
        gate = mod_ref[gate_idx:gate_idx + 1, :]
        z = ALPHA * h_ref[...] + (coef * gate) * y_ref[...]
        mu = jnp.mean(z, axis=-1, keepdims=True)
        zc = z - mu
        var = jnp.mean(zc * zc, axis=-1, keepdims=True)
        rstd = lax.rsqrt(var + LN_EPS)
        xhat = zc * rstd
        hn = xhat * g_ref[...] + b_ref[...]
        hn_ref[...] = hn
        xhat_ref[...] = xhat
        rstd_ref[...] = rstd
        if has_next:
            sh = modn_ref[next_idx[0]:next_idx[0] + 1, :]
            sc = modn_ref[next_idx[1]:next_idx[1] + 1, :]
            xin_ref[...] = (hn * (1.0 + sc) + sh).astype(BF16)

    row = pl.BlockSpec((tm, D), lambda i: (i, 0))
    modspec = pl.BlockSpec((None, N_MOD, D), lambda i: (cfg.seg(i), 0, 0))
    vec = pl.BlockSpec((1, D), lambda i: (0, 0))
    in_specs = [row, row, modspec, vec, vec]
    args = [h, y, mod, lng, lnb]
    out_specs = [row, row, pl.BlockSpec((tm, 1), lambda i: (i, 0))]
    out_shape = [jax.ShapeDtypeStruct((cfg.T, D), F32), jax.ShapeDtypeStruct((cfg.T, D), F32),
                 jax.ShapeDtypeStruct((cfg.T, 1), F32)]
    if has_next:
        in_specs.append(modspec)
        args.append(mod_next)
        out_specs.append(row)
        out_shape.append(jax.ShapeDtypeStruct((cfg.T, D), BF16))
    return pl.pallas_call(body, grid=(cfg.nt,), name=name, in_specs=in_specs, out_specs=out_specs,
                          out_shape=out_shape, compiler_params=_params(("parallel",)))(*args)


def _ln_bwd(cfg, up, xhat, rstd, y, mod, gate_idx, coef, lng, name):
    tm = cfg.tm
    fused = len(up) > 1
    scale_next = up[4] if fused else None

    def body(*refs):
        if fused:
            dres_n, dxin_n, hn_ref, modn_ref, xhat_ref, rstd_ref, y_ref, mod_ref, g_ref, dres_ref, dys_ref, st_ref = refs
        else:
            dhn_ref, xhat_ref, rstd_ref, y_ref, mod_ref, g_ref, dres_ref, dys_ref, st_ref = refs
        i = pl.program_id(0)

        @pl.when(cfg.first_of_seg(i))
        def _():
            st_ref[...] = jnp.zeros_like(st_ref)

        if fused:
            dxin = dxin_n[...]
            sc = modn_ref[scale_next:scale_next + 1, :]
            dhn = dres_n[...] + dxin * (1.0 + sc)
            st_ref[3:4, :] += jnp.sum(dxin * hn_ref[...], axis=0, keepdims=True)
            st_ref[4:5, :] += jnp.sum(dxin, axis=0, keepdims=True)
        else:
            dhn = dhn_ref[...]
        xhat = xhat_ref[...]
        gdh = dhn * g_ref[...]
        m1 = jnp.mean(gdh, axis=-1, keepdims=True)
        m2 = jnp.mean(gdh * xhat, axis=-1, keepdims=True)
        dz = rstd_ref[...] * (gdh - m1 - xhat * m2)
        gate = mod_ref[gate_idx:gate_idx + 1, :]
        dres_ref[...] = ALPHA * dz
        dys_ref[...] = ((coef * gate) * dz).astype(BF16)
        st_ref[0:1, :] += jnp.sum(dhn * xhat, axis=0, keepdims=True)
        st_ref[1:2, :] += jnp.sum(dhn, axis=0, keepdims=True)
        st_ref[2:3, :] += jnp.sum((coef * dz) * y_ref[...], axis=0, keepdims=True)

    row = pl.BlockSpec((tm, D), lambda i: (i, 0))
    modspec = pl.BlockSpec((None, N_MOD, D), lambda i: (cfg.seg(i), 0, 0))
    vec = pl.BlockSpec((1, D), lambda i: (0, 0))
    col = pl.BlockSpec((tm, 1), lambda i: (i, 0))
    if fused:
        in_specs = [row, row, row, modspec, row, col, row, modspec, vec]
        args = [up[0], up[1], up[2], up[3], xhat, rstd, y, mod, lng]
    else:
        in_specs = [row, row, col, row, modspec, vec]
        args = [up[0], xhat, rstd, y, mod, lng]
    return pl.pallas_call(
        body, grid=(cfg.nt,), name=name, in_specs=in_specs,
        out_specs=[row, row, pl.BlockSpec((None, 8, D), lambda i: (cfg.seg(i), 0, 0))],
        out_shape=[jax.ShapeDtypeStruct((cfg.T, D), F32), jax.ShapeDtypeStruct((cfg.T, D), BF16),
                   jax.ShapeDtypeStruct((3, 8, D), F32)],
        compiler_params=_params(("arbitrary",)))(*args)


def _modulate_bwd(cfg, dres, dxin, h, mod, scale_idx, name):
    tm = cfg.tm

    def body(dres_ref, dxin_ref, h_ref, mod_ref, dh_ref, st_ref):
        i = pl.program_id(0)

        @pl.when(cfg.first_of_seg(i))
        def _():
            st_ref[...] = jnp.zeros_like(st_ref)

        dxin = dxin_ref[...]
        sc = mod_ref[scale_idx:scale_idx + 1, :]
        dh_ref[...] = dres_ref[...] + dxin * (1.0 + sc)
        st_ref[3:4, :] += jnp.sum(dxin * h_ref[...], axis=0, keepdims=True)
        st_ref[4:5, :] += jnp.sum(dxin, axis=0, keepdims=True)

    row = pl.BlockSpec((tm, D), lambda i: (i, 0))
    return pl.pallas_call(
        body, grid=(cfg.nt,), name=name,
        in_specs=[row, row, row, pl.BlockSpec((None, N_MOD, D), lambda i: (cfg.seg(i), 0, 0))],
        out_specs=[row, pl.BlockSpec((None, 8, D), lambda i: (cfg.seg(i), 0, 0))],
        out_shape=[jax.ShapeDtypeStruct((cfg.T, D), F32), jax.ShapeDtypeStruct((3, 8, D), F32)],
        compiler_params=_params(("arbitrary",)))(dres, dxin, h, mod)


def _loss(cfg, h, target, name):
    tm = cfg.tm
    n_lt = 2 * cfg.nlt

    def body(h_ref, t_ref, dy_ref, l_ref):
        i = pl.program_id(0)

        @pl.when(i == 0)
        def _():
            l_ref[...] = jnp.zeros_like(l_ref)

        @pl.when(i < n_lt)
        def _():
            err = h_ref[...] - t_ref[...]
            dy_ref[...] = err * (1.0 / D)
            part = jnp.sum(jnp.sum(err * err, axis=1, keepdims=True), axis=0, keepdims=True) * (0.5 / D)
            l_ref[...] += jnp.broadcast_to(part, l_ref.shape)

        @pl.when(i >= n_lt)
        def _():
            dy_ref[...] = jnp.zeros_like(dy_ref)

    return pl.pallas_call(
        body, grid=(cfg.nt,), name=name,
        in_specs=[pl.BlockSpec((tm, D), lambda i: (i, 0)),
                  pl.BlockSpec((tm, D), lambda i: (jnp.minimum(i, n_lt - 1), 0))],
        out_specs=[pl.BlockSpec((tm, D), lambda i: (i, 0)), pl.BlockSpec((8, 128), lambda i: (0, 0))],
        out_shape=[jax.ShapeDtypeStruct((cfg.T, D), F32), jax.ShapeDtypeStruct((8, 128), F32)],
        compiler_params=_params(("arbitrary",)))(h, target)


def _matmul(a, b, mode, out_dtype, name, bm_cap=512, bn_cap=1408, bk_cap=1024):
    if mode == "nn":
        (M, K), N = a.shape, b.shape[1]
    elif mode == "nt":
        (M, K), N = a.shape, b.shape[0]
    else:
        (K, M), N = a.shape, b.shape[1]
    bm, bn, bk = _pick(M, bm_cap), _pick(N, bn_cap), _pick(K, bk_cap)
    nk = K // bk

    def body(a_ref, b_ref, o_ref, acc_ref):
        k = pl.program_id(2)
        if mode == "nn":
            part = _dot(a_ref[...], b_ref[...])
        elif mode == "nt":
            part = _dot_nt(a_ref[...], b_ref[...])
        else:
            part = _dot_tn(a_ref[...], b_ref[...])

        @pl.when(k == 0)
        def _():
            acc_ref[...] = part

        @pl.when(k > 0)
        def _():
            acc_ref[...] += part

        @pl.when(k == nk - 1)
        def _():
            o_ref[...] = acc_ref[...].astype(out_dtype)

    if mode == "nn":
        a_spec = pl.BlockSpec((bm, bk), lambda i, j, k: (i, k))
        b_spec = pl.BlockSpec((bk, bn), lambda i, j, k: (k, j))
    elif mode == "nt":
        a_spec = pl.BlockSpec((bm, bk), lambda i, j, k: (i, k))
        b_spec = pl.BlockSpec((bn, bk), lambda i, j, k: (j, k))
    else:
        a_spec = pl.BlockSpec((bk, bm), lambda i, j, k: (k, i))
        b_spec = pl.BlockSpec((bk, bn), lambda i, j, k: (k, j))
    return pl.pallas_call(
        body, grid=(M // bm, N // bn, nk), name=name, in_specs=[a_spec, b_spec],
        out_specs=pl.BlockSpec((bm, bn), lambda i, j, k: (i, j)),
        out_shape=jax.ShapeDtypeStruct((M, N), out_dtype),
        scratch_shapes=[pltpu.VMEM((bm, bn), F32)],
        compiler_params=_params(("parallel", "parallel", "arbitrary")))(a, b)


def _ffn_fwd(cfg, xin, wg, wu, wd, name):
    tm, T = cfg.tm, xin.shape[0]
    F = wg.shape[1]
    tf = F // 2
    assert tf % 128 == 0

    def body(x_ref, wg_ref, wu_ref, wd_ref, g_ref, u_ref, y_ref):
        j = pl.program_id(1)
        x = x_ref[...]
        acc = None
        for c0, cw in _chunks(tf):
            g = _dot(x, wg_ref[:, c0:c0 + cw])
            u = _dot(x, wu_ref[:, c0:c0 + cw])
            g_ref[:, c0:c0 + cw] = g
            u_ref[:, c0:c0 + cw] = u
            part = _dot(g * _sigmoid(g) * u, wd_ref[c0:c0 + cw, :])
            acc = part if acc is None else acc + part

        @pl.when(j == 0)
        def _():
            y_ref[...] = acc

        @pl.when(j > 0)
        def _():
            y_ref[...] += acc

    return pl.pallas_call(
        body, grid=(T // tm, 2), name=name,
        in_specs=[pl.BlockSpec((tm, D), lambda i, j: (i, 0)),
                  pl.BlockSpec((D, tf), lambda i, j: (0, j)),
                  pl.BlockSpec((D, tf), lambda i, j: (0, j)),
                  pl.BlockSpec((tf, D), lambda i, j: (j, 0))],
        out_specs=[pl.BlockSpec((tm, tf), lambda i, j: (i, j)),
                   pl.BlockSpec((tm, tf), lambda i, j: (i, j)),
                   pl.BlockSpec((tm, D), lambda i, j: (i, 0))],
        out_shape=[jax.ShapeDtypeStruct((T, F), F32), jax.ShapeDtypeStruct((T, F), F32),
                   jax.ShapeDtypeStruct((T, D), F32)],
        compiler_params=_params(("parallel", "arbitrary")))(xin, wg, wu, wd)


def _ffn_bwd(cfg, dys, g, u, wg, wu, wd, name):
    tm, T = cfg.tm, dys.shape[0]
    F = wg.shape[1]
    tf = F // 2

    def body(dy_ref, g_ref, u_ref, wg_ref, wu_ref, wd_ref, dg_ref, du_ref, a_ref, dx_ref):
        j = pl.program_id(1)
        dy = dy_ref[...]
        acc = None
        for c0, cw in _chunks(tf):
            gg = g_ref[:, c0:c0 + cw]
            uu = u_ref[:, c0:c0 + cw]
            da = _dot_nt(dy, wd_ref[c0:c0 + cw, :])
            s = _sigmoid(gg)
            silu = gg * s
            a_ref[:, c0:c0 + cw] = (silu * uu).astype(BF16)
            du = (da * silu).astype(BF16)
            dg = (da * uu * (s * (1.0 + gg * (1.0 - s)))).astype(BF16)
            du_ref[:, c0:c0 + cw] = du
            dg_ref[:, c0:c0 + cw] = dg
            part = _dot_nt(dg, wg_ref[:, c0:c0 + cw]) + _dot_nt(du, wu_ref[:, c0:c0 + cw])
            acc = part if acc is None else acc + part

        @pl.when(j == 0)
        def _():
            dx_ref[...] = acc

        @pl.when(j > 0)
        def _():
            dx_ref[...] += acc

    blk = pl.BlockSpec((tm, tf), lambda i, j: (i, j))
    return pl.pallas_call(
        body, grid=(T // tm, 2), name=name,
        in_specs=[pl.BlockSpec((tm, D), lambda i, j: (i, 0)), blk, blk,
                  pl.BlockSpec((D, tf), lambda i, j: (0, j)),
                  pl.BlockSpec((D, tf), lambda i, j: (0, j)),
                  pl.BlockSpec((tf, D), lambda i, j: (j, 0))],
        out_specs=[blk, blk, blk, pl.BlockSpec((tm, D), lambda i, j: (i, 0))],
        out_shape=[jax.ShapeDtypeStruct((T, F), BF16), jax.ShapeDtypeStruct((T, F), BF16),
                   jax.ShapeDtypeStruct((T, F), BF16), jax.ShapeDtypeStruct((T, D), F32)],
        compiler_params=_params(("parallel", "arbitrary")))(dys, g, u, wg, wu, wd)


def _swap_halves(x):
    w = x.shape[1]
    lane = lax.broadcasted_iota(jnp.int32, (1, w), 1)
    return jnp.where((lane & 63) < 32, pltpu.roll(x, w - 32, 1), pltpu.roll(x, 32, 1))


def _rope(x, cos, sin):
    return x * cos + _swap_halves(x) * sin


def _rope_t(dy, cos, sin):
    return dy * cos + _swap_halves(dy * sin)


def _rope_tables(n_lat):
    rows = n_lat // GRID_W
    row = jnp.repeat(jnp.arange(rows, dtype=F32), GRID_W)
    col = jnp.tile(jnp.arange(GRID_W, dtype=F32), rows)
    inv = ROPE_THETA ** (-jnp.arange(ROPE_FREQS, dtype=F32) / ROPE_FREQS)
    ang = jnp.concatenate([row[:, None] * inv, col[:, None] * inv], axis=-1)
    cs, sn = jnp.cos(ang), jnp.sin(ang)
    cos = jnp.concatenate([cs, cs, cs, cs], axis=-1)
    sin = jnp.concatenate([-sn, sn, -sn, sn], axis=-1)
    return cos, sin


def _attn_specs(cfg):
    n_lat, n_ctx, cb = cfg.n_lat, cfg.n_ctx, cfg.ctx_blk
    return [pl.BlockSpec((n_lat, ATT_W), lambda e: (e, 0)),
            pl.BlockSpec((n_lat, 128), lambda e: (e, 4)),
            pl.BlockSpec((n_lat, 128), lambda e: (e, 5)),
            pl.BlockSpec((n_ctx, ATT_W), lambda e: (cb + e, 0)),
            pl.BlockSpec((n_ctx, 128), lambda e: (cb + e, 4)),
            pl.BlockSpec((n_ctx, 128), lambda e: (cb + e, 5)),
            pl.BlockSpec((n_lat, 128), lambda e: (0, 0)),
            pl.BlockSpec((n_lat, 128), lambda e: (0, 0)),
            pl.BlockSpec((8, 128), lambda e: (0, 0))]


def _attn_prepare(kh, kl, vl, kc, vc, ka, kb, va, vb, kca, kcb, vca, vcb):
    lane = lax.broadcasted_iota(jnp.int32, (1, 128), 1)
    own = (lane < 64) if kh == 0 else (lane >= 64)

    def split(x, ra, rb):
        mine = jnp.where(own, x, 0.0)
        other = pltpu.roll(mine, 64, 1)
        a, b = (mine, other) if kh == 0 else (other, mine)
        ra[...] = a.astype(BF16)
        rb[...] = b.astype(BF16)

    split(kl, ka, kb)
    split(vl, va, vb)
    split(kc, kca, kcb)
    split(vc, vca, vcb)


def _softmax_parts(s_list, sk):
    m = sk
    for s in s_list:
        m = jnp.maximum(m, jnp.max(s, axis=1, keepdims=True))
    es = [jnp.exp(s - m) for s in s_list]
    esk = jnp.exp(sk - m)
    den = esk
    for e in es:
        den = den + jnp.sum(e, axis=1, keepdims=True)
    inv = 1.0 / den
    return [e * inv for e in es], esk * inv


def _window(cfg, n):
    r0 = pl.multiple_of(n * BLK, BLK)
    start = pl.multiple_of(jnp.clip((n - 1) * BLK, 0, cfg.n_lat - 3 * BLK), BLK)
    qpos = r0 + lax.broadcasted_iota(jnp.int32, (BLK, 1), 0)
    kpos = start + lax.broadcasted_iota(jnp.int32, (1, 3 * BLK), 1)
    valid = jnp.abs(qpos - kpos) <= BLK
    return r0, start, valid


def _attn_fwd(cfg, p, cos, sin, sink_rows, name):
    n_lat, n_ctx = cfg.n_lat, cfg.n_ctx

    def body(q_ref, k_ref, v_ref, qc_ref, kc_ref, vc_ref, cos_ref, sin_ref, sink_ref, o_ref, oc_ref,
             qr, ka, kb, va, vb, kca, kcb, vca, vcb):
        cos_t, sin_t = cos_ref[...], sin_ref[...]
        for gq in range(4):
            qr[:, gq * 128:(gq + 1) * 128] = _rope(q_ref[:, gq * 128:(gq + 1) * 128], cos_t, sin_t).astype(BF16)
        kl = _rope(k_ref[...], cos_t, sin_t)
        for kh in range(KV_HEADS):
            _attn_prepare(kh, kl, v_ref[...], kc_ref[...], vc_ref[...], ka, kb, va, vb, kca, kcb, vca, vcb)

            def lat_block(n, carry):
                r0, start, valid = _window(cfg, n)
                win = pl.ds(start, 3 * BLK)
                for pr in range(2):
                    lanes = slice((kh * 2 + pr) * 128, (kh * 2 + pr + 1) * 128)
                    qp = qr[pl.ds(r0, BLK), lanes]
                    o = None
                    for half, (kw, kcx, vw, vcx) in enumerate(((ka, kca, va, vca), (kb, kcb, vb, vcb))):
                        head = kh * 4 + pr * 2 + half
                        s_w = jnp.where(valid, _dot_nt(qp, kw[win, :]) * ATT_SCALE, NEG_INF)
                        s_c = _dot_nt(qp, kcx[...]) * ATT_SCALE
                        (p_w, p_c), _ = _softmax_parts([s_w, s_c], sink_ref[head:head + 1, 0:1])
                        part = _dot(p_w, vw[win, :]) + _dot(p_c, vcx[...])
                        o = part if o is None else o + part
                    o_ref[pl.ds(r0, BLK), lanes] = o.astype(BF16)
                return carry

            lax.fori_loop(0, n_lat // BLK, lat_block, 0)
            for n in range(n_ctx // BLK):
                rows = slice(n * BLK, (n + 1) * BLK)
                for pr in range(2):
                    lanes = slice((kh * 2 + pr) * 128, (kh * 2 + pr + 1) * 128)
                    qp = qc_ref[rows, lanes]
                    o = None
                    for half, (kcx, vcx) in enumerate(((kca, vca), (kcb, vcb))):
                        head = kh * 4 + pr * 2 + half
                        s_c = _dot_nt(qp, kcx[...]) * ATT_SCALE
                        (p_c,), _ = _softmax_parts([s_c], sink_ref[head:head + 1, 0:1])
                        part = _dot(p_c, vcx[...])
                        o = part if o is None else o + part
                    oc_ref[rows, lanes] = o.astype(BF16)

    return pl.pallas_call(
        body, grid=(2,), name=name, in_specs=_attn_specs(cfg),
        out_specs=[pl.BlockSpec((n_lat, ATT_W), lambda e: (e, 0)), pl.BlockSpec((n_ctx, ATT_W), lambda e: (e, 0))],
        out_shape=[jax.ShapeDtypeStruct((cfg.t_lat, ATT_W), BF16), jax.ShapeDtypeStruct((cfg.t_ctx, ATT_W), BF16)],
        scratch_shapes=[pltpu.VMEM((n_lat, ATT_W), BF16)] + [pltpu.VMEM((n_lat, 128), BF16)] * 4
        + [pltpu.VMEM((n_ctx, 128), BF16)] * 4,
        compiler_params=_params(("parallel",)))(p, p, p, p, p, p, cos, sin, sink_rows)


def _attn_bwd(cfg, p, dcat, cos, sin, sink_rows, name):
    n_lat, n_ctx, cb = cfg.n_lat, cfg.n_ctx, cfg.ctx_blk

    def body(q_ref, k_ref, v_ref, qc_ref, kc_ref, vc_ref, cos_ref, sin_ref, sink_ref, do_ref, doc_ref,
             dq_ref, dk_ref, dv_ref, dqc_ref, dkc_ref, dvc_ref, dsink_ref,
             qr, ka, kb, va, vb, kca, kcb, vca, vcb, dqs, dka, dva, dkca, dvca):
        cos_t, sin_t = cos_ref[...], sin_ref[...]
        lane = lax.broadcasted_iota(jnp.int32, (1, 128), 1)
        lo = lane < 64
        for gq in range(4):
            qr[:, gq * 128:(gq + 1) * 128] = _rope(q_ref[:, gq * 128:(gq + 1) * 128], cos_t, sin_t).astype(BF16)
        kl = _rope(k_ref[...], cos_t, sin_t)
        dsink_ref[...] = jnp.zeros_like(dsink_ref)
        dka[...] = jnp.zeros_like(dka)
        dva[...] = jnp.zeros_like(dva)
        dkca[...] = jnp.zeros_like(dkca)
        dvca[...] = jnp.zeros_like(dvca)

        def halves(x):
            return jnp.where(lo, x, 0).astype(BF16), jnp.where(lo, 0, x).astype(BF16)

        for kh in range(KV_HEADS):
            _attn_prepare(kh, kl, v_ref[...], kc_ref[...], vc_ref[...], ka, kb, va, vb, kca, kcb, vca, vcb)

            def one_head(head, qp, q_half, do_p, do_half, kw, kcx, vw, vcx, win, valid):
                sk = sink_ref[head:head + 1, 0:1]
                s_list = [_dot_nt(qp, kcx[...]) * ATT_SCALE]
                if win is not None:
                    s_list.insert(0, jnp.where(valid, _dot_nt(qp, kw[win, :]) * ATT_SCALE, NEG_INF))
                probs, p_sink = _softmax_parts(s_list, sk)
                vals = [vcx[...]] if win is None else [vw[win, :], vcx[...]]
                dps = [_dot_nt(do_p, vv) for vv in vals]
                dr = None
                for pp, dp in zip(probs, dps):
                    t = jnp.sum(pp * dp, axis=1, keepdims=True)
                    dr = t if dr is None else dr + t
                dss = [(pp * (dp - dr) * ATT_SCALE).astype(BF16) for pp, dp in zip(probs, dps)]
                dsink_ref[head:head + 1, :] += jnp.broadcast_to(
                    jnp.sum(-p_sink * dr, axis=0, keepdims=True), (1, 128))
                p_c, ds_c = probs[-1], dss[-1]
                dq = _dot(ds_c, kcx[...])
                dkca[kh] += _dot_tn(ds_c, q_half)
                dvca[kh] += _dot_tn(p_c, do_half)
                if win is not None:
                    dq = dq + _dot(dss[0], kw[win, :])
                    dka[kh, win, :] += _dot_tn(dss[0], q_half)
                    dva[kh, win, :] += _dot_tn(probs[0], do_half)
                return dq

            def lat_block(n, carry):
                r0, start, valid = _window(cfg, n)
                win = pl.ds(start, 3 * BLK)
                for pr in range(2):
                    lanes = slice((kh * 2 + pr) * 128, (kh * 2 + pr + 1) * 128)
                    qp = qr[pl.ds(r0, BLK), lanes]
                    do_p = do_ref[pl.ds(r0, BLK), lanes]
                    q_h, do_h = halves(qp), halves(do_p)
                    dq = None
                    for half, (kw, kcx, vw, vcx) in enumerate(((ka, kca, va, vca), (kb, kcb, vb, vcb))):
                        part = one_head(kh * 4 + pr * 2 + half, qp, q_h[half], do_p, do_h[half],
                                        kw, kcx, vw, vcx, win, valid)
                        dq = part if dq is None else dq + part
                    dqs[pl.ds(r0, BLK), lanes] = dq
                return carry

            lax.fori_loop(0, n_lat // BLK, lat_block, 0)
            for n in range(n_ctx // BLK):
                rows = slice(n * BLK, (n + 1) * BLK)
                for pr in range(2):
                    lanes = slice((kh * 2 + pr) * 128, (kh * 2 + pr + 1) * 128)
                    qp = qc_ref[rows, lanes].astype(BF16)
                    do_p = doc_ref[rows, lanes]
                    q_h, do_h = halves(qp), halves(do_p)
                    dq = None
                    for half, (kcx, vcx) in enumerate(((kca, vca), (kcb, vcb))):
                        part = one_head(kh * 4 + pr * 2 + half, qp, q_h[half], do_p, do_h[half],
                                        None, kcx, None, vcx, None, None)
                        dq = part if dq is None else dq + part
                    dqc_ref[rows, lanes] = dq.astype(BF16)

        def fold(acc):
            r0 = acc[0] + pltpu.roll(acc[0], 64, 1)
            r1 = acc[1] + pltpu.roll(acc[1], 64, 1)
            return jnp.where(lo, r0, r1)

        for gq in range(4):
            sl = slice(gq * 128, (gq + 1) * 128)
            dq_ref[:, sl] = _rope_t(dqs[:, sl], cos_t, sin_t).astype(BF16)
        dk_ref[...] = _rope_t(fold(dka), cos_t, sin_t).astype(BF16)
        dv_ref[...] = fold(dva).astype(BF16)
        dkc_ref[...] = fold(dkca).astype(BF16)
        dvc_ref[...] = fold(dvca).astype(BF16)

    lat = lambda w: pl.BlockSpec((n_lat, w), lambda e: (e, 0))
    ctx = lambda w: pl.BlockSpec((n_ctx, w), lambda e: (e, 0))
    sd = jax.ShapeDtypeStruct
    return pl.pallas_call(
        body, grid=(2,), name=name,
        in_specs=_attn_specs(cfg) + [pl.BlockSpec((n_lat, ATT_W), lambda e: (e, 0)),
                                     pl.BlockSpec((n_ctx, ATT_W), lambda e: (cb + e, 0))],
        out_specs=[lat(ATT_W), lat(128), lat(128), ctx(ATT_W), ctx(128), ctx(128),
                   pl.BlockSpec((None, 8, 128), lambda e: (e, 0, 0))],
        out_shape=[sd((cfg.t_lat, ATT_W), BF16), sd((cfg.t_lat, 128), BF16), sd((cfg.t_lat, 128), BF16),
                   sd((cfg.t_ctx, ATT_W), BF16), sd((cfg.t_ctx, 128), BF16), sd((cfg.t_ctx, 128), BF16),
                   sd((2, 8, 128), F32)],
        scratch_shapes=[pltpu.VMEM((n_lat, ATT_W), BF16)] + [pltpu.VMEM((n_lat, 128), BF16)] * 4
        + [pltpu.VMEM((n_ctx, 128), BF16)] * 4
        + [pltpu.VMEM((n_lat, ATT_W), F32), pltpu.VMEM((2, n_lat, 128), F32), pltpu.VMEM((2, n_lat, 128), F32),
           pltpu.VMEM((2, n_ctx, 128), F32), pltpu.VMEM((2, n_ctx, 128), F32)],
        compiler_params=_params(("parallel",)))(p, p, p, p, p, p, cos, sin, sink_rows, dcat, dcat)


def _shift_down(x, k, row):
    return jnp.where(row >= k, pltpu.roll(x, k, 0), 0.0)


def _shift_up(x, k, row):
    n = x.shape[0]
    return jnp.where(row < n - k, pltpu.roll(x, n - k, 0), 0.0)


def _window_sum(x, r, row):
    below, above, k = x, x, 1
    while k < r:
        below = below + _shift_down(below, k, row)
        above = above + _shift_up(above, k, row)
        k *= 2
    return below + _shift_down(x, r, row) + _shift_up(above, 1, row)


def _inv_count(r, row, n):
    cnt = jnp.minimum(row + r, n - 1) + 1 - jnp.maximum(row - r, 0)
    return 1.0 / cnt.astype(F32)


def _pool_fwd(p, w, scale, n, blk0, n_seg, name):
    def body(u0, u1, u2, u3, w_ref, sc_ref, o_ref):
        row = lax.broadcasted_iota(jnp.int32, (n, 1), 0)
        for g, u_ref in enumerate((u0, u1, u2, u3)):
            u = u_ref[...]
            d = _window_sum(u, POOL_R[g], row) * _inv_count(POOL_R[g], row, n) - u
            o_ref[:, g * 128:(g + 1) * 128] = (_dot(d, w_ref[g]) * sc_ref[:, g * 128:(g + 1) * 128]).astype(BF16)

    return pl.pallas_call(
        body, grid=(n_seg,), name=name,
        in_specs=[pl.BlockSpec((n, 128), functools.partial(lambda g, e: (blk0 + e, 6 + g), g)) for g in range(4)]
        + [pl.BlockSpec((4, 128, 128), lambda e: (0, 0, 0)), pl.BlockSpec((1, 512), lambda e: (0, 0))],
        out_specs=pl.BlockSpec((n, 512), lambda e: (e, 0)),
        out_shape=jax.ShapeDtypeStruct((n_seg * n, 512), BF16),
        compiler_params=_params(("parallel",)))(p, p, p, p, w, scale)


def _pool_bwd(p, w, scale, dcat, n, blk0, n_seg, name):
    def body(u0, u1, u2, u3, w_ref, sc_ref, dp_ref, du_ref, dw_ref, dsc_ref):
        e = pl.program_id(0)

        @pl.when(e == 0)
        def _():
            dw_ref[...] = jnp.zeros_like(dw_ref)
            dsc_ref[...] = jnp.zeros_like(dsc_ref)

        row = lax.broadcasted_iota(jnp.int32, (n, 1), 0)
        for g, u_ref in enumerate((u0, u1, u2, u3)):
            sl = slice(g * 128, (g + 1) * 128)
            u = u_ref[...]
            inv = _inv_count(POOL_R[g], row, n)
            d = _window_sum(u, POOL_R[g], row) * inv - u
            dp = dp_ref[:, sl]
            dsc_ref[:, sl] += jnp.sum(dp * _dot(d, w_ref[g]), axis=0, keepdims=True)
            dyp = dp * sc_ref[:, sl]
            dw_ref[g] += _dot_tn(d, dyp)
            dd = _dot_nt(dyp, w_ref[g])
            du_ref[:, sl] = (_window_sum(dd * inv, POOL_R[g], row) - dd).astype(BF16)

    return pl.pallas_call(
        body, grid=(n_seg,), name=name,
        in_specs=[pl.BlockSpec((n, 128), functools.partial(lambda g, e: (blk0 + e, 6 + g), g)) for g in range(4)]
        + [pl.BlockSpec((4, 128, 128), lambda e: (0, 0, 0)), pl.BlockSpec((1, 512), lambda e: (0, 0)),
           pl.BlockSpec((n, 512), lambda e: (blk0 + e, 1))],
        out_specs=[pl.BlockSpec((n, 512), lambda e: (e, 0)),
                   pl.BlockSpec((4, 128, 128), lambda e: (0, 0, 0)), pl.BlockSpec((1, 512), lambda e: (0, 0))],
        out_shape=[jax.ShapeDtypeStruct((n_seg * n, 512), BF16), jax.ShapeDtypeStruct((4, 128, 128), F32),
                   jax.ShapeDtypeStruct((1, 512), F32)],
        compiler_params=_params(("arbitrary",)))(p, p, p, p, w, scale, dcat)


def _gelu(x):
    t = jnp.tanh(math.sqrt(2.0 / math.pi) * (x + 0.044715 * x * x * x))
    return 0.5 * x * (1.0 + t), t


def _gelu_grad(x, t):
    return 0.5 * (1.0 + t) + 0.5 * x * (1.0 - t * t) * (math.sqrt(2.0 / math.pi) * (1.0 + 3 * 0.044715 * x * x))


def _neg_expm1(x):
    series = -x * (1.0 + x * (0.5 + x * (1.0 / 6.0 + x * (1.0 / 24.0 + x * (1.0 / 120.0)))))
    return jnp.where(x > -0.05, series, 1.0 - jnp.exp(x))


def _softplus_neg(lam):
    x = -lam
    e = jnp.exp(-jnp.abs(x))
    log1p = jnp.where(e < 1e-2, e * (1.0 - e * (0.5 - e * (1.0 / 3.0))), jnp.log(1.0 + e))
    return jnp.maximum(x, 0.0) + log1p, -_sigmoid(x)


def _conv(u, w_ref, b_ref, row):
    return (b_ref[...] + _shift_down(u, 1, row) * w_ref[0:1, :] + u * w_ref[1:2, :]
            + _shift_up(u, 1, row) * w_ref[2:3, :] + _shift_up(u, 2, row) * w_ref[3:4, :])


def _lru_gates(uc, d, wa_ref, ba_ref, wx_ref, bx_ref, lam_ref):
    r = _sigmoid(_dot(uc, wa_ref[d]) + ba_ref[d:d + 1, :])
    gi = _sigmoid(_dot(uc, wx_ref[d]) + bx_ref[d:d + 1, :])
    sp, dsp = _softplus_neg(lam_ref[d:d + 1, :])
    la = (-LRU_C) * r * sp
    a = jnp.exp(la)
    sq = jnp.sqrt(_neg_expm1(2.0 * la))
    return r, gi, sp, dsp, a, sq


def _tile_scan(a, b, reverse):
    n = a.shape[0]
    row8 = lax.broadcasted_iota(jnp.int32, (n, 1), 0) & 7
    for k in (1, 2, 4):
        if reverse:
            m = row8 < 8 - k
            a_sh = jnp.where(m, pltpu.roll(a, n - k, 0), 1.0)
            b_sh = jnp.where(m, pltpu.roll(b, n - k, 0), 0.0)
        else:
            m = row8 >= k
            a_sh = jnp.where(m, pltpu.roll(a, k, 0), 1.0)
            b_sh = jnp.where(m, pltpu.roll(b, k, 0), 0.0)
        b = a * b_sh + b
        a = a * a_sh
    return a, b


def _carry_scan(a_ref, b_ref, n, reverse, carry):
    nt8 = n // 8

    def step(i, c):
        t = (nt8 - 1 - i) if reverse else i
        off = pl.multiple_of(t * 8, 8)
        h = a_ref[pl.ds(off, 8), :] * c + b_ref[pl.ds(off, 8), :]
        b_ref[pl.ds(off, 8), :] = h
        return h[0:1, :] if reverse else h[7:8, :]

    return lax.fori_loop(0, nt8, step, carry)


def _chain_scan(segs, reverse):
    carry = jnp.zeros((1, 128), F32)
    for a, b, a_ref, b_ref, n in segs:
        a2, b2 = _tile_scan(a, b, reverse)
        a_ref[...] = a2
        b_ref[...] = b2
        carry = _carry_scan(a_ref, b_ref, n, reverse, carry)


def _lru_specs(cfg):
    n_lat, n_ctx, cb = cfg.n_lat, cfg.n_ctx, cfg.ctx_blk
    return [pl.BlockSpec((n_lat, 128), lambda hb, e: (e, hb)),
            pl.BlockSpec((n_lat, 128), lambda hb, e: (e, 8 + hb)),
            pl.BlockSpec((n_ctx, 128), lambda hb, e: (cb + e, hb)),
            pl.BlockSpec((n_ctx, 128), lambda hb, e: (cb + e, 8 + hb)),
            pl.BlockSpec((4, 128), lambda hb, e: (0, hb)),
            pl.BlockSpec((1, 128), lambda hb, e: (0, hb)),
            pl.BlockSpec((2, None, 128, 128), lambda hb, e: (0, hb, 0, 0)),
            pl.BlockSpec((2, 128), lambda hb, e: (0, hb)),
            pl.BlockSpec((2, None, 128, 128), lambda hb, e: (0, hb, 0, 0)),
            pl.BlockSpec((2, 128), lambda hb, e: (0, hb)),
            pl.BlockSpec((2, 128), lambda hb, e: (0, hb))]


def _lru_fwd(cfg, p, consts, name):
    n_lat, n_ctx = cfg.n_lat, cfg.n_ctx

    def body(gl_ref, ul_ref, gc_ref, uc_ref, cw_ref, cb_ref, wa_ref, ba_ref, wx_ref, bx_ref, lam_ref,
             zl_ref, zc_ref, al, bl, ac, bc):
        row_l = lax.broadcasted_iota(jnp.int32, (n_lat, 1), 0)
        row_c = lax.broadcasted_iota(jnp.int32, (n_ctx, 1), 0)
        uc_l = _conv(ul_ref[...], cw_ref, cb_ref, row_l)
        uc_c = _conv(uc_ref[...], cw_ref, cb_ref, row_c)
        y_l = y_c = None
        for d in range(2):
            _, gi_l, _, _, a_l, sq_l = _lru_gates(uc_l, d, wa_ref, ba_ref, wx_ref, bx_ref, lam_ref)
            _, gi_c, _, _, a_c, sq_c = _lru_gates(uc_c, d, wa_ref, ba_ref, wx_ref, bx_ref, lam_ref)
            _chain_scan([(a_c, sq_c * (gi_c * uc_c), ac, bc, n_ctx), (a_l, sq_l * (gi_l * uc_l), al, bl, n_lat)],
                        reverse=(d == 1))
            y_l = bl[...] if y_l is None else y_l + bl[...]
            y_c = bc[...] if y_c is None else y_c + bc[...]
        zl_ref[...] = (_gelu(gl_ref[...])[0] * y_l).astype(BF16)
        zc_ref[...] = (_gelu(gc_ref[...])[0] * y_c).astype(BF16)

    return pl.pallas_call(
        body, grid=(8, 2), name=name, in_specs=_lru_specs(cfg),
        out_specs=[pl.BlockSpec((n_lat, 128), lambda hb, e: (e, hb)), pl.BlockSpec((n_ctx, 128), lambda hb, e: (e, hb))],
        out_shape=[jax.ShapeDtypeStruct((cfg.t_lat, D), BF16), jax.ShapeDtypeStruct((cfg.t_ctx, D), BF16)],
        scratch_shapes=[pltpu.VMEM((n_lat, 128), F32)] * 2 + [pltpu.VMEM((n_ctx, 128), F32)] * 2,
        compiler_params=_params(("parallel", "arbitrary")))(p, p, p, p, *consts)


def _lru_bwd(cfg, p, dz, consts, name):
    n_lat, n_ctx, cb = cfg.n_lat, cfg.n_ctx, cfg.ctx_blk

    def body(gl_ref, ul_ref, gc_ref, uc_ref, cw_ref, cb_ref, wa_ref, ba_ref, wx_ref, bx_ref, lam_ref,
             dzl_ref, dzc_ref, dgl_ref, dul_ref, dgc_ref, duc_ref, dwa_ref, dwx_ref, vec_ref,
             al, bl, ac, bc, hl, hc):
        e = pl.program_id(1)

        @pl.when(e == 0)
        def _():
            dwa_ref[...] = jnp.zeros_like(dwa_ref)
            dwx_ref[...] = jnp.zeros_like(dwx_ref)
            vec_ref[...] = jnp.zeros_like(vec_ref)

        row_l = lax.broadcasted_iota(jnp.int32, (n_lat, 1), 0)
        row_c = lax.broadcasted_iota(jnp.int32, (n_ctx, 1), 0)
        u_l, u_c = ul_ref[...], uc_ref[...]
        uc_l = _conv(u_l, cw_ref, cb_ref, row_l)
        uc_c = _conv(u_c, cw_ref, cb_ref, row_c)
        for d in range(2):
            _, gi_l, _, _, a_l, sq_l = _lru_gates(uc_l, d, wa_ref, ba_ref, wx_ref, bx_ref, lam_ref)
            _, gi_c, _, _, a_c, sq_c = _lru_gates(uc_c, d, wa_ref, ba_ref, wx_ref, bx_ref, lam_ref)
            _chain_scan([(a_c, sq_c * (gi_c * uc_c), ac, bc, n_ctx), (a_l, sq_l * (gi_l * uc_l), al, bl, n_lat)],
                        reverse=(d == 1))
            hl[d] = bl[...]
            hc[d] = bc[...]
        gel_l, t_l = _gelu(gl_ref[...])
        gel_c, t_c = _gelu(gc_ref[...])
        dz_l, dz_c = dzl_ref[...], dzc_ref[...]
        dgl_ref[...] = (dz_l * (hl[0] + hl[1]) * _gelu_grad(gl_ref[...], t_l)).astype(BF16)
        dgc_ref[...] = (dz_c * (hc[0] + hc[1]) * _gelu_grad(gc_ref[...], t_c)).astype(BF16)
        dy_l, dy_c = dz_l * gel_l, dz_c * gel_c
        duc_l = jnp.zeros((n_lat, 128), F32)
        duc_c = jnp.zeros((n_ctx, 128), F32)
        for d in range(2):
            r_l, gi_l, sp, dsp, a_l, sq_l = _lru_gates(uc_l, d, wa_ref, ba_ref, wx_ref, bx_ref, lam_ref)
            r_c, gi_c, _, _, a_c, sq_c = _lru_gates(uc_c, d, wa_ref, ba_ref, wx_ref, bx_ref, lam_ref)
            if d == 0:
                an_l = _shift_up(a_l, 1, row_l)
                an_c = jnp.where(row_c < n_ctx - 1, pltpu.roll(a_c, n_ctx - 1, 0), a_l[0:1, :])
            else:
                an_l = _shift_down(a_l, 1, row_l)
                an_c = jnp.where(row_c >= 1, pltpu.roll(a_c, 1, 0), a_l[n_lat - 1:n_lat, :])
            _chain_scan([(an_l, dy_l, al, bl, n_lat), (an_c, dy_c, ac, bc, n_ctx)], reverse=(d == 0))
            dsp_sum = jnp.zeros((1, 128), F32)
            for (dh, h, r, gi, a, sq, uc, seg) in ((bl[...], hl[d], r_l, gi_l, a_l, sq_l, uc_l, "l"),
                                                  (bc[...], hc[d], r_c, gi_c, a_c, sq_c, uc_c, "c")):
                b0 = sq * (gi * uc)
                t1 = dh * sq
                dla = dh * (h - b0) - (dh * gi * uc) * (a * a) / sq
                dzr = (dla * ((-LRU_C) * sp)) * r * (1.0 - r)
                dzi = (t1 * uc) * gi * (1.0 - gi)
                dsp_sum = dsp_sum + jnp.sum(dla * ((-LRU_C) * r), axis=0, keepdims=True)
                dwa_ref[d] += _dot_tn(uc, dzr)
                dwx_ref[d] += _dot_tn(uc, dzi)
                vec_ref[d:d + 1, :] += jnp.sum(dzr, axis=0, keepdims=True)
                vec_ref[2 + d:3 + d, :] += jnp.sum(dzi, axis=0, keepdims=True)
                duc = t1 * gi + _dot_nt(dzr, wa_ref[d]) + _dot_nt(dzi, wx_ref[d])
                if seg == "l":
                    duc_l = duc_l + duc
                else:
                    duc_c = duc_c + duc
            vec_ref[4 + d:5 + d, :] += dsp_sum * dsp
        for duc, u, row, du_ref in ((duc_l, u_l, row_l, dul_ref), (duc_c, u_c, row_c, duc_ref)):
            du_ref[...] = (_shift_up(duc, 1, row) * cw_ref[0:1, :] + duc * cw_ref[1:2, :]
                           + _shift_down(duc, 1, row) * cw_ref[2:3, :]
                           + _shift_down(duc, 2, row) * cw_ref[3:4, :]).astype(BF16)
            vec_ref[6:7, :] += jnp.sum(duc * _shift_down(u, 1, row), axis=0, keepdims=True)
            vec_ref[7:8, :] += jnp.sum(duc * u, axis=0, keepdims=True)
            vec_ref[8:9, :] += jnp.sum(duc * _shift_up(u, 1, row), axis=0, keepdims=True)
            vec_ref[9:10, :] += jnp.sum(duc * _shift_up(u, 2, row), axis=0, keepdims=True)
            vec_ref[10:11, :] += jnp.sum(duc, axis=0, keepdims=True)

    lat = pl.BlockSpec((n_lat, 128), lambda hb, e: (e, hb))
    ctx = pl.BlockSpec((n_ctx, 128), lambda hb, e: (e, hb))
    wspec = pl.BlockSpec((2, None, 128, 128), lambda hb, e: (0, hb, 0, 0))
    sd = jax.ShapeDtypeStruct
    return pl.pallas_call(
        body, grid=(8, 2), name=name,
        in_specs=_lru_specs(cfg) + [pl.BlockSpec((n_lat, 128), lambda hb, e: (e, hb)),
                                    pl.BlockSpec((n_ctx, 128), lambda hb, e: (cb + e, hb))],
        out_specs=[lat, lat, ctx, ctx, wspec, wspec, pl.BlockSpec((None, 16, 128), lambda hb, e: (hb, 0, 0))],
        out_shape=[sd((cfg.t_lat, D), BF16), sd((cfg.t_lat, D), BF16), sd((cfg.t_ctx, D), BF16), sd((cfg.t_ctx, D), BF16),
                   sd((2, 8, 128, 128), F32), sd((2, 8, 128, 128), F32), sd((8, 16, 128), F32)],
        scratch_shapes=[pltpu.VMEM((n_lat, 128), F32)] * 2 + [pltpu.VMEM((n_ctx, 128), F32)] * 2
        + [pltpu.VMEM((2, n_lat, 128), F32), pltpu.VMEM((2, n_ctx, 128), F32)],
        compiler_params=_params(("parallel", "arbitrary")))(p, p, p, p, *consts, dz, dz)


def _position():
    x, y, c = lax.axis_index("x"), lax.axis_index("y"), lax.axis_index("c")
    return x, y, c, 4 * x + 2 * y + c


def _peer(x, y, c, k):
    px = 1 - x if k & 4 else x
    py = 1 - y if k & 2 else y
    pc = 1 - c if k & 1 else c
    return (px, py, pc), 4 * px + 2 * py + pc


def _all_gather(v, name, in_vmem):
    def body(v_ref, o_ref, send_sems, recv_sems, local_sem):
        x, y, c, me = _position()
        mine = pltpu.make_async_copy(v_ref, o_ref.at[me], local_sem)
        mine.start()
        sends = []
        for k in range(1, N_DEV):
            peer, _ = _peer(x, y, c, k)
            cp = pltpu.make_async_remote_copy(src_ref=v_ref, dst_ref=o_ref.at[me], send_sem=send_sems.at[k - 1],
                                              recv_sem=recv_sems.at[k - 1], device_id=peer, device_id_type=MESH)
            cp.start()
            sends.append(cp)
        for k in range(1, N_DEV):
            peer, peer_lin = _peer(x, y, c, k)
            pltpu.make_async_remote_copy(src_ref=v_ref, dst_ref=o_ref.at[peer_lin], send_sem=send_sems.at[k - 1],
                                         recv_sem=recv_sems.at[k - 1], device_id=peer, device_id_type=MESH).wait_recv()
        for cp in sends:
            cp.wait_send()
        mine.wait()

    space = pltpu.VMEM if in_vmem else pl.ANY
    return pl.pallas_call(
        body, name=name,
        in_specs=[pl.BlockSpec(memory_space=space)], out_specs=pl.BlockSpec(memory_space=space),
        out_shape=jax.ShapeDtypeStruct((N_DEV,) + v.shape, v.dtype),
        scratch_shapes=[pltpu.SemaphoreType.DMA((N_DEV - 1,)), pltpu.SemaphoreType.DMA((N_DEV - 1,)),
                        pltpu.SemaphoreType.DMA],
        compiler_params=pltpu.CompilerParams(vmem_limit_bytes=VMEM_LIMIT))(v)


def _exchange(g, name):
    def body(g_ref, o_ref, send_sems, recv_sems, local_sem):
        x, y, c, me = _position()
        mine = pltpu.make_async_copy(g_ref.at[me], o_ref.at[me], local_sem)
        mine.start()
        sends = []
        for k in range(1, N_DEV):
            peer, peer_lin = _peer(x, y, c, k)
            cp = pltpu.make_async_remote_copy(src_ref=g_ref.at[peer_lin], dst_ref=o_ref.at[me],
                                              send_sem=send_sems.at[k - 1], recv_sem=recv_sems.at[k - 1],
                                              device_id=peer, device_id_type=MESH)
            cp.start()
            sends.append(cp)
        for k in range(1, N_DEV):
            peer, peer_lin = _peer(x, y, c, k)
            pltpu.make_async_remote_copy(src_ref=g_ref.at[peer_lin], dst_ref=o_ref.at[peer_lin],
                                         send_sem=send_sems.at[k - 1], recv_sem=recv_sems.at[k - 1],
                                         device_id=peer, device_id_type=MESH).wait_recv()
        for cp in sends:
            cp.wait_send()
        mine.wait()

    return pl.pallas_call(
        body, name=name,
        in_specs=[pl.BlockSpec(memory_space=pl.ANY)], out_specs=pl.BlockSpec(memory_space=pl.ANY),
        out_shape=jax.ShapeDtypeStruct(g.shape, g.dtype),
        scratch_shapes=[pltpu.SemaphoreType.DMA((N_DEV - 1,)), pltpu.SemaphoreType.DMA((N_DEV - 1,)),
                        pltpu.SemaphoreType.DMA],
        compiler_params=pltpu.CompilerParams(vmem_limit_bytes=VMEM_LIMIT))(g)


def _sum_blocks(v, r0, rows, tr, name):
    k, _, cols = v.shape
    assert r0 % tr == 0 and rows % tr == 0

    def body(v_ref, o_ref):
        acc = v_ref[0].astype(F32)
        for s in range(1, k):
            acc = acc + v_ref[s].astype(F32)
        o_ref[...] = acc

    return pl.pallas_call(
        body, grid=(rows // tr,), name=name,
        in_specs=[pl.BlockSpec((k, tr, cols), lambda i: (0, r0 // tr + i, 0))],
        out_specs=pl.BlockSpec((tr, cols), lambda i: (i, 0)),
        out_shape=jax.ShapeDtypeStruct((rows, cols), F32),
        compiler_params=_params(("parallel",)))(v)


def _adam_math(w, g, m, v):
    m2 = B1 * m + (1.0 - B1) * g
    v2 = B2 * v + (1.0 - B2) * (g * g)
    m_hat = m2 / (1.0 - B1 ** STEP)
    v_hat = v2 / (1.0 - B2 ** STEP)
    return -LR * (m_hat / (jnp.sqrt(v_hat) + EPS) + WD * w), m2, v2


def _adamw(w, g, m, v, name):
    rows, cols = w.shape
    tr = rows
    for cand in (512, 256, 128, 64, 32, 16, 8):
        if rows % cand == 0 and cand * cols * 4 <= 2 * 1024 * 1024:
            tr = cand
            break

    def body(w_ref, g_ref, m_ref, v_ref, d_ref, m2_ref, v2_ref):
        d_ref[...], m2_ref[...], v2_ref[...] = _adam_math(w_ref[...], g_ref[...], m_ref[...], v_ref[...])

    blk = pl.BlockSpec((tr, cols), lambda i: (i, 0))
    return pl.pallas_call(
        body, grid=(rows // tr,), name=name, in_specs=[blk] * 4, out_specs=[blk] * 3,
        out_shape=[jax.ShapeDtypeStruct((rows, cols), F32)] * 3,
        compiler_params=_params(("parallel",)))(w, g, m, v)


def _adamw_reduce(recv, r0, w, m, v, tr, name):
    rows, cols = w.shape
    assert r0 % tr == 0 and rows % tr == 0

    def body(r_ref, w_ref, m_ref, v_ref, g_ref, d_ref, m2_ref, v2_ref):
        g = r_ref[0].astype(F32)
        for s in range(1, N_DEV):
            g = g + r_ref[s].astype(F32)
        g_ref[...] = g
        d_ref[...], m2_ref[...], v2_ref[...] = _adam_math(w_ref[...], g, m_ref[...], v_ref[...])

    blk = pl.BlockSpec((tr, cols), lambda i: (i, 0))
    return pl.pallas_call(
        body, grid=(rows // tr,), name=name,
        in_specs=[pl.BlockSpec((N_DEV, tr, cols), lambda i: (0, r0 // tr + i, 0)), blk, blk, blk],
        out_specs=[blk] * 4, out_shape=[jax.ShapeDtypeStruct((rows, cols), F32)] * 4,
        compiler_params=_params(("parallel",)))(recv, w, m, v)


def _as2d(a):
    n = a.size
    if n % 1024 == 0:
        return a.reshape(n // 1024, 1024)
    if n % 128 == 0:
        return a.reshape(n // 128, 128)
    return a.reshape(1, n)


def _cols_to_blocks(a):
    lead, c = a.shape[:-1], a.shape[-1] // N_DEV
    return jnp.moveaxis(a.reshape(lead + (N_DEV, c)), -2, 0)


def _blocks_to_cols(a):
    b = jnp.moveaxis(a, 0, -2)
    return b.reshape(b.shape[:-2] + (b.shape[-2] * b.shape[-1],))


def _rows_to_blocks(a):
    lead, r, c = a.shape[:-2], a.shape[-2] // N_DEV, a.shape[-1]
    return jnp.moveaxis(a.reshape(lead + (N_DEV, r, c)), -3, 0)


def _blocks_to_rows(a):
    b = jnp.moveaxis(a, 0, -3)
    return b.reshape(b.shape[:-3] + (b.shape[-3] * b.shape[-2], b.shape[-1]))


def _silu(x):
    return x * jax.nn.sigmoid(x)


def kernel(x, c, ctx, c_ctx, w_mod, b_mod, ln_g, ln_b, ffn_w_gate, ffn_w_up, ffn_w_down, mix_ab_w_in, attn_sink, pool_w, pool_scale, mix_ab_w_out, lru_w_in, lru_conv_w, lru_conv_b, lru_wa, lru_ba, lru_wx, lru_bx, lru_lambda, lru_w_out, loss_target, m_c_ctx, m_w_mod, m_b_mod, m_ln_g, m_ln_b, m_ffn_w_gate, m_ffn_w_up, m_ffn_w_down, m_mix_ab_w_in, m_attn_sink, m_pool_w, m_pool_scale, m_mix_ab_w_out, m_lru_w_in, m_lru_conv_w, m_lru_conv_b, m_lru_wa, m_lru_ba, m_lru_wx, m_lru_bx, m_lru_lambda, m_lru_w_out, v_c_ctx, v_w_mod, v_b_mod, v_ln_g, v_ln_b, v_ffn_w_gate, v_ffn_w_up, v_ffn_w_down, v_mix_ab_w_in, v_attn_sink, v_pool_w, v_pool_scale, v_mix_ab_w_out, v_lru_w_in, v_lru_conv_w, v_lru_conv_b, v_lru_wa, v_lru_ba, v_lru_wx, v_lru_bx, v_lru_lambda, v_lru_w_out):
    weights = dict(c_ctx=c_ctx, w_mod=w_mod, b_mod=b_mod, ln_g=ln_g, ln_b=ln_b, ffn_w_gate=ffn_w_gate,
                   ffn_w_up=ffn_w_up, ffn_w_down=ffn_w_down, mix_ab_w_in=mix_ab_w_in, attn_sink=attn_sink,
                   pool_w=pool_w, pool_scale=pool_scale, mix_ab_w_out=mix_ab_w_out, lru_w_in=lru_w_in,
                   lru_conv_w=lru_conv_w, lru_conv_b=lru_conv_b, lru_wa=lru_wa, lru_ba=lru_ba, lru_wx=lru_wx,
                   lru_bx=lru_bx, lru_lambda=lru_lambda, lru_w_out=lru_w_out)
    mom_m = dict(c_ctx=m_c_ctx, w_mod=m_w_mod, b_mod=m_b_mod, ln_g=m_ln_g, ln_b=m_ln_b, ffn_w_gate=m_ffn_w_gate,
                 ffn_w_up=m_ffn_w_up, ffn_w_down=m_ffn_w_down, mix_ab_w_in=m_mix_ab_w_in, attn_sink=m_attn_sink,
                 pool_w=m_pool_w, pool_scale=m_pool_scale, mix_ab_w_out=m_mix_ab_w_out, lru_w_in=m_lru_w_in,
                 lru_conv_w=m_lru_conv_w, lru_conv_b=m_lru_conv_b, lru_wa=m_lru_wa, lru_ba=m_lru_ba, lru_wx=m_lru_wx,
                 lru_bx=m_lru_bx, lru_lambda=m_lru_lambda, lru_w_out=m_lru_w_out)
    mom_v = dict(c_ctx=v_c_ctx, w_mod=v_w_mod, b_mod=v_b_mod, ln_g=v_ln_g, ln_b=v_ln_b, ffn_w_gate=v_ffn_w_gate,
                 ffn_w_up=v_ffn_w_up, ffn_w_down=v_ffn_w_down, mix_ab_w_in=v_mix_ab_w_in, attn_sink=v_attn_sink,
                 pool_w=v_pool_w, pool_scale=v_pool_scale, mix_ab_w_out=v_mix_ab_w_out, lru_w_in=v_lru_w_in,
                 lru_conv_w=v_lru_conv_w, lru_conv_b=v_lru_conv_b, lru_wa=v_lru_wa, lru_ba=v_lru_ba, lru_wx=v_lru_wx,
                 lru_bx=v_lru_bx, lru_lambda=v_lru_lambda, lru_w_out=v_lru_w_out)
    names = list(weights)

    n_lat, n_ctx = x.shape[1], ctx.shape[1]
    cfg = _Cfg(n_lat, n_ctx)
    T = cfg.T
    _, _, _, me = _position()
    mcols = w_mod.shape[2]
    F = ffn_w_gate.shape[3] * N_DEV

    small_names = ["ln_g", "ln_b", "lru_conv_w", "lru_conv_b", "lru_ba", "lru_bx", "lru_lambda"]
    small_parts = [c.reshape(-1, 128)] + [weights[n].reshape(-1, 128) for n in small_names]
    small_rows = [a.shape[0] for a in small_parts]
    pad = (-sum(small_rows)) % 8
    small = jnp.concatenate(small_parts + [jnp.zeros((pad, 128), F32)], axis=0)
    small_all = _all_gather(small, "gather_small", True)

    def small_full(idx):
        r0 = sum(small_rows[:idx])
        part = small_all[:, r0:r0 + small_rows[idx], :]
        return part

    c_all = small_full(0).reshape(2 * N_DEV, D)
    full_small = {}
    for i, n in enumerate(small_names):
        shp = weights[n].shape
        full_small[n] = _blocks_to_cols(small_full(i + 1).reshape((N_DEV,) + shp))[0] if n not in ("ln_g", "ln_b") \
            else _blocks_to_cols(small_full(i + 1).reshape((N_DEV,) + shp))
    ln_g_f, ln_b_f = full_small["ln_g"], full_small["ln_b"]

    s_rows = jnp.zeros((32, D), F32).at[:16].set(_silu(c_all)).at[16].set(_silu(c_ctx)).astype(BF16)
    mod_mine = jnp.stack([_matmul(s_rows, w_mod[l], "nn", F32, "mod_fwd", bn_cap=1280) for l in range(2)])
    mod_all = _all_gather(mod_mine.reshape(64, mcols), "gather_mod", True).reshape(N_DEV, 2, 32, mcols)
    mod_full = _blocks_to_cols(mod_all) + b_mod[:, None, :]
    ex0 = 2 * me
    mods = []
    for l in range(2):
        rows = jnp.stack([lax.dynamic_index_in_dim(mod_full[l], ex0, 0, False),
                          lax.dynamic_index_in_dim(mod_full[l], ex0 + 1, 0, False), mod_full[l, 16]])
        mods.append(rows.reshape(3, N_MOD, D))

    big_names = ["ffn_w_gate", "ffn_w_up", "ffn_w_down", "mix_ab_w_in", "mix_ab_w_out", "lru_w_in", "lru_w_out"]
    big_rows = [weights[n].size // 1024 for n in big_names]
    packed = jnp.concatenate([weights[n].astype(BF16).reshape(-1, 1024) for n in big_names], axis=0)
    gathered = _all_gather(packed, "gather_weights", False)

    def big_blocks(n):
        i = big_names.index(n)
        r0 = sum(big_rows[:i])
        return gathered[:, r0:r0 + big_rows[i], :].reshape((N_DEV,) + weights[n].shape)

    wg_f = _blocks_to_cols(big_blocks("ffn_w_gate"))
    wu_f = _blocks_to_cols(big_blocks("ffn_w_up"))
    wd_f = _blocks_to_rows(big_blocks("ffn_w_down"))
    w_ab_in = _blocks_to_cols(big_blocks("mix_ab_w_in"))[0]
    w_ab_out = _blocks_to_rows(big_blocks("mix_ab_w_out"))[0]
    w_lru_in = _blocks_to_cols(big_blocks("lru_w_in"))[0]
    w_lru_out = _blocks_to_rows(big_blocks("lru_w_out"))[0]

    h0 = jnp.concatenate([x.reshape(cfg.t_lat, D), ctx.reshape(cfg.t_ctx, D)], axis=0)
    cos, sin = _rope_tables(n_lat)
    sink_rows = jnp.broadcast_to(attn_sink[0][:, None], (8, 128)).astype(F32)
    lru_consts = (full_small["lru_conv_w"], full_small["lru_conv_b"][None, :], lru_wa[0], full_small["lru_ba"],
                  lru_wx[0], full_small["lru_bx"], full_small["lru_lambda"])

    saved = []
    h = h0
    xin = _modulate(cfg, h0, mods[0], 0, 1, "modulate_in")
    for l in range(2):
        st = {"h_in": h, "xin1": xin}
        g1, u1, y1 = _ffn_fwd(cfg, xin, wg_f[l, 0], wu_f[l, 0], wd_f[l, 0], "ffn_fwd")
        h1, xhat1, rstd1, xin2 = _ln_fwd(cfg, h, y1, mods[l], 2, 0.5, ln_g_f[l, 0][None], ln_b_f[l, 0][None],
                                          mods[l], (3, 4), "ln_fwd_a")
        st.update(g1=g1, u1=u1, y1=y1, h1=h1, xhat1=xhat1, rstd1=rstd1, xin2=xin2)
        if l == 0:
            p = _matmul(xin2, w_ab_in, "nn", F32, "mix_ab_in")
            att_l, att_c = _attn_fwd(cfg, p, cos, sin, sink_rows, "attn_fwd")
            pool_l = _pool_fwd(p, pool_w[0], pool_scale, n_lat, 0, 2, "pool_fwd_lat")
            pool_c = _pool_fwd(p, pool_w[0], pool_scale, n_ctx, cfg.ctx_blk, 2, "pool_fwd_ctx")
            cat = jnp.concatenate([jnp.concatenate([att_l, pool_l], axis=1),
                                   jnp.concatenate([att_c, pool_c], axis=1)], axis=0)
            y2 = _matmul(cat, w_ab_out, "nn", F32, "mix_ab_out")
        else:
            p = _matmul(xin2, w_lru_in, "nn", F32, "lru_in")
            z_l, z_c = _lru_fwd(cfg, p, lru_consts, "lru_fwd")
            cat = jnp.concatenate([z_l, z_c], axis=0)
            y2 = _matmul(cat, w_lru_out, "nn", F32, "lru_out")
        h2, xhat2, rstd2, xin3 = _ln_fwd(cfg, h1, y2, mods[l], 5, 1.0, ln_g_f[l, 1][None], ln_b_f[l, 1][None],
                                          mods[l], (6, 7), "ln_fwd_b")
        g3, u3, y3 = _ffn_fwd(cfg, xin3, wg_f[l, 1], wu_f[l, 1], wd_f[l, 1], "ffn_fwd")
        if l == 0:
            h3, xhat3, rstd3, xin = _ln_fwd(cfg, h2, y3, mods[l], 8, 0.5, ln_g_f[l, 2][None], ln_b_f[l, 2][None],
                                            mods[1], (0, 1), "ln_fwd_a")
        else:
            h3, xhat3, rstd3 = _ln_fwd(cfg, h2, y3, mods[l], 8, 0.5, ln_g_f[l, 2][None], ln_b_f[l, 2][None],
                                       None, None, "ln_fwd_last")
        st.update(p=p, cat=cat, y2=y2, h2=h2, xhat2=xhat2, rstd2=rstd2, xin3=xin3, g3=g3, u3=u3, y3=y3,
                  xhat3=xhat3, rstd3=rstd3)
        saved.append(st)
        h = h3

    dy, loss_tile = _loss(cfg, h, loss_target.reshape(cfg.t_lat, D), "loss")
    loss = lax.psum(loss_tile[0, 0], ("x", "y", "c"))

    grads = {}
    dmod = [None, None]
    g_ffn = {n: [[None, None], [None, None]] for n in ("ffn_w_gate", "ffn_w_up", "ffn_w_down")}
    dln_g = [[None] * 3, [None] * 3]
    dln_b = [[None] * 3, [None] * 3]
    up = (dy,)
    for l in (1, 0):
        st = saved[l]
        dm = [None] * N_MOD

        def put_stats(stats, gate_idx, nxt):
            dm[gate_idx] = stats[:, 2, :]
            if nxt is not None:
                nxt[0][nxt[1]] = stats[:, 4, :]
                nxt[0][nxt[1] + 1] = stats[:, 3, :]

        dres, dys, stats = _ln_bwd(cfg, up, st["xhat3"], st["rstd3"], st["y3"], mods[l], 8, 0.5,
                                   ln_g_f[l, 2][None], "ln_bwd_fused" if len(up) > 1 else "ln_bwd_last")
        put_stats(stats, 8, None if len(up) == 1 else (dmod_next, 0))
        dln_g[l][2], dln_b[l][2] = stats[:, 0, :].sum(0), stats[:, 1, :].sum(0)
        dg, du, a_act, dxin = _ffn_bwd(cfg, dys, st["g3"], st["u3"], wg_f[l, 1], wu_f[l, 1], wd_f[l, 1], "ffn_bwd")
        g_ffn["ffn_w_gate"][l][1] = _matmul(st["xin3"], dg, "tn", F32, "ffn_dw_in")
        g_ffn["ffn_w_up"][l][1] = _matmul(st["xin3"], du, "tn", F32, "ffn_dw_in")
        g_ffn["ffn_w_down"][l][1] = _matmul(a_act, dys, "tn", F32, "ffn_dw_out")
        dres, dys, stats = _ln_bwd(cfg, (dres, dxin, st["h2"], mods[l], 7), st["xhat2"], st["rstd2"], st["y2"],
                                   mods[l], 5, 1.0, ln_g_f[l, 1][None], "ln_bwd_fused")
        put_stats(stats, 5, (dm, 6))
        dln_g[l][1], dln_b[l][1] = stats[:, 0, :].sum(0), stats[:, 1, :].sum(0)
        if l == 0:
            grads["mix_ab_w_out"] = _matmul(st["cat"], dys, "tn", F32, "mix_ab_dw_out")[None]
            dcat = _matmul(dys, w_ab_out, "nt", F32, "mix_ab_dcat")
            dq, dk, dv, dqc, dkc, dvc, dsink = _attn_bwd(cfg, st["p"], dcat, cos, sin, sink_rows, "attn_bwd")
            du_l, dpw_l, dps_l = _pool_bwd(st["p"], pool_w[0], pool_scale, dcat, n_lat, 0, 2, "pool_bwd_lat")
            du_c, dpw_c, dps_c = _pool_bwd(st["p"], pool_w[0], pool_scale, dcat, n_ctx, cfg.ctx_blk, 2, "pool_bwd_ctx")
            dp = jnp.concatenate([jnp.concatenate([dq, dk, dv, du_l], axis=1),
                                  jnp.concatenate([dqc, dkc, dvc, du_c], axis=1)], axis=0)
            grads["mix_ab_w_in"] = _matmul(st["xin2"], dp, "tn", F32, "mix_ab_dw_in")[None]
            dxin = _matmul(dp, w_ab_in, "nt", F32, "mix_ab_dx")
            grads["attn_sink"] = (dsink[0, :, 0] + dsink[1, :, 0])[None, :]
            grads["pool_w"] = (dpw_l + dpw_c)[None]
            grads["pool_scale"] = dps_l + dps_c
        else:
            grads["lru_w_out"] = _matmul(st["cat"], dys, "tn", F32, "lru_dw_out")[None]
            dz = _matmul(dys, w_lru_out, "nt", F32, "lru_dz")
            dgl, dul, dgc, duc, dwa, dwx, vec = _lru_bwd(cfg, st["p"], dz, lru_consts, "lru_bwd")
            dp = jnp.concatenate([jnp.concatenate([dgl, dul], axis=1), jnp.concatenate([dgc, duc], axis=1)], axis=0)
            grads["lru_w_in"] = _matmul(st["xin2"], dp, "tn", F32, "lru_dw_in")[None]
            dxin = _matmul(dp, w_lru_in, "nt", F32, "lru_dx")
            grads["lru_wa"], grads["lru_wx"] = dwa[None], dwx[None]
            vec_t = jnp.moveaxis(vec, 0, 1).reshape(16, D)
            grads["lru_ba"], grads["lru_bx"] = vec_t[None, 0:2], vec_t[None, 2:4]
            grads["lru_lambda"], grads["lru_conv_w"], grads["lru_conv_b"] = vec_t[None, 4:6], vec_t[None, 6:10], vec_t[10:11]
        dres, dys, stats = _ln_bwd(cfg, (dres, dxin, st["h1"], mods[l], 4), st["xhat1"], st["rstd1"], st["y1"],
                                   mods[l], 2, 0.5, ln_g_f[l, 0][None], "ln_bwd_fused")
        put_stats(stats, 2, (dm, 3))
        dln_g[l][0], dln_b[l][0] = stats[:, 0, :].sum(0), stats[:, 1, :].sum(0)
        dg, du, a_act, dxin = _ffn_bwd(cfg, dys, st["g1"], st["u1"], wg_f[l, 0], wu_f[l, 0], wd_f[l, 0], "ffn_bwd")
        g_ffn["ffn_w_gate"][l][0] = _matmul(st["xin1"], dg, "tn", F32, "ffn_dw_in")
        g_ffn["ffn_w_up"][l][0] = _matmul(st["xin1"], du, "tn", F32, "ffn_dw_in")
        g_ffn["ffn_w_down"][l][0] = _matmul(a_act, dys, "tn", F32, "ffn_dw_out")
        dmod[l] = dm
        dmod_next = dm
        up = (dres, dxin, st["h_in"], mods[l], 1)
    dh0, stats = _modulate_bwd(cfg, up[0], up[1], h0, mods[0], 1, "modulate_bwd")
    dmod[0][0], dmod[0][1] = stats[:, 4, :], stats[:, 3, :]
    grad_x = dh0[:cfg.t_lat].reshape(x.shape)

    dmod_mine = jnp.stack([jnp.stack(dmod[l], axis=1).reshape(3, N_MOD * D) for l in range(2)])
    dmod_all = _all_gather(dmod_mine.reshape(6 * N_MOD * D // 128, 128), "gather_dmod", True).reshape(N_DEV, 2, 3, N_MOD * D)
    dmod_sum = _sum_blocks(dmod_all.reshape(N_DEV, 6 * N_MOD * D // 128, 128), 0, 6 * N_MOD * D // 128,
                           6 * N_MOD * D // 128, "sum_dmod").reshape(2, 3, N_MOD * D)
    grads["b_mod"] = dmod_sum[:, 0] + dmod_sum[:, 1] + dmod_sum[:, 2]
    dmod_ex = jnp.moveaxis(dmod_all[:, :, 0:2, :], 1, 0).reshape(2, 2 * N_DEV, N_MOD * D)
    dm_rows = jnp.zeros((2, 32, N_MOD * D), F32).at[:, :16].set(dmod_ex).at[:, 16].set(dmod_sum[:, 2])
    dm_cols = lax.dynamic_slice_in_dim(dm_rows, me * mcols, mcols, axis=2).astype(BF16)
    grads["w_mod"] = jnp.stack([_matmul(s_rows, dm_cols[l], "tn", F32, "mod_dw", bn_cap=1280) for l in range(2)])
    ds_part = None
    for l in range(2):
        part = _matmul(dm_cols[l, 16:32], w_mod[l], "nt", F32, "mod_ds", bk_cap=1280)[0]
        ds_part = part if ds_part is None else ds_part + part

    for n in ("ffn_w_gate", "ffn_w_up", "ffn_w_down"):
        grads[n] = jnp.stack([jnp.stack(g_ffn[n][l]) for l in range(2)])
    col_sharded = ("ffn_w_gate", "ffn_w_up", "mix_ab_w_in", "lru_w_in")
    send_parts = []
    for n in big_names:
        blocks = _cols_to_blocks(grads[n]) if n in col_sharded else _rows_to_blocks(grads[n])
        send_parts.append(blocks.astype(BF16).reshape(N_DEV, -1, 1024))
    rep_names = ["lru_wa", "lru_wx", "pool_w"]
    rep_rows = [weights[n].size // 1024 // N_DEV for n in rep_names]
    for n in rep_names:
        send_parts.append(grads[n].astype(BF16).reshape(N_DEV, -1, 1024))
    recv = _exchange(jnp.concatenate(send_parts, axis=1), "exchange_grads")
    rep_r0 = sum(big_rows)
    rep_sum = _sum_blocks(recv, rep_r0, sum(rep_rows), 8, "sum_replicated")

    dln_g_f = jnp.stack([jnp.stack(dln_g[l]) for l in range(2)])
    dln_b_f = jnp.stack([jnp.stack(dln_b[l]) for l in range(2)])
    sink_pad = jnp.zeros((1, 128), F32).at[0, :8].set(grads["attn_sink"][0])
    part_list = [rep_sum.reshape(-1, 128), dln_g_f.reshape(-1, 128), dln_b_f.reshape(-1, 128),
                 grads["lru_conv_w"].reshape(-1, 128), grads["lru_conv_b"].reshape(-1, 128),
                 grads["lru_ba"].reshape(-1, 128), grads["lru_bx"].reshape(-1, 128),
                 grads["lru_lambda"].reshape(-1, 128), ds_part.reshape(-1, 128), sink_pad,
                 grads["pool_scale"].reshape(-1, 128)]
    part_rows = [a.shape[0] for a in part_list]
    pad = (-sum(part_rows)) % 8
    parts = jnp.concatenate(part_list + [jnp.zeros((pad, 128), F32)], axis=0)
    parts_all = _all_gather(parts, "gather_partials", True)
    n_rep = part_rows[0]
    rest = parts_all.shape[1] - n_rep
    small_sum = _sum_blocks(parts_all, 0, parts_all.shape[1], parts_all.shape[1], "sum_partials")[n_rep:]

    def take(idx):
        r0 = sum(part_rows[1:idx])
        return small_sum[r0:r0 + part_rows[idx]]

    rep_full = parts_all[:, :n_rep, :].reshape(N_DEV, sum(rep_rows), 1024)
    r = 0
    for n, rr in zip(rep_names, rep_rows):
        grads[n] = rep_full[:, r:r + rr, :].reshape(weights[n].shape)
        r += rr

    def my_cols(full, shp):
        w = shp[-1]
        return lax.dynamic_slice_in_dim(full, me * w, w, axis=full.ndim - 1)

    grads["ln_g"] = my_cols(take(1).reshape(2, 3, D), ln_g.shape)
    grads["ln_b"] = my_cols(take(2).reshape(2, 3, D), ln_b.shape)
    grads["lru_conv_w"] = my_cols(take(3).reshape(1, 4, D), lru_conv_w.shape)
    grads["lru_conv_b"] = my_cols(take(4).reshape(1, D), lru_conv_b.shape)
    grads["lru_ba"] = my_cols(take(5).reshape(1, 2, D), lru_ba.shape)
    grads["lru_bx"] = my_cols(take(6).reshape(1, 2, D), lru_bx.shape)
    grads["lru_lambda"] = my_cols(take(7).reshape(1, 2, D), lru_lambda.shape)
    sg = jax.nn.sigmoid(c_ctx)
    grads["c_ctx"] = take(8).reshape(D) * (sg * (1.0 + c_ctx * (1.0 - sg)))
    grads["attn_sink"] = take(9)[:, :8]
    grads["pool_scale"] = take(10).reshape(pool_scale.shape)

    delta, new_m, new_v = {}, {}, {}
    r0 = 0
    for n, rows in zip(big_names, big_rows):
        shp = weights[n].shape
        tr = 176 if rows % 176 == 0 and r0 % 176 == 0 else 32
        g2, d2, m2, v2 = _adamw_reduce(recv, r0, weights[n].reshape(rows, 1024), mom_m[n].reshape(rows, 1024),
                                       mom_v[n].reshape(rows, 1024), tr, "adamw_reduce")
        grads[n], delta[n], new_m[n], new_v[n] = g2.reshape(shp), d2.reshape(shp), m2.reshape(shp), v2.reshape(shp)
        r0 += rows
    for n in names:
        if n in delta:
            continue
        shp = weights[n].shape
        d2, m2, v2 = _adamw(_as2d(weights[n]), _as2d(grads[n].reshape(shp)), _as2d(mom_m[n]), _as2d(mom_v[n]), "adamw")
        grads[n], delta[n], new_m[n], new_v[n] = grads[n].reshape(shp), d2.reshape(shp), m2.reshape(shp), v2.reshape(shp)

    return (loss, grad_x, *[grads[n] for n in names], *[delta[n] for n in names],
            *[new_m[n] for n in names], *[new_v[n] for n in names])
```

```python
import functools
import math

import jax
import jax.numpy as jnp
from jax import lax
from jax.experimental import pallas as pl
from jax.experimental.pallas import tpu as pltpu

F32 = jnp.float32
BF16 = jnp.bfloat16
MESH = pl.DeviceIdType.MESH

D = 1024
N_MOD = 9
N_DEV = 8
HEAD_DIM = 64
ATT_HEADS = 8
KV_HEADS = 2
ATT_W = 512
BLK = 128
ATT_SCALE = HEAD_DIM ** -0.5
GRID_W = 64
ROPE_FREQS = HEAD_DIM // 4
ROPE_THETA = 10000.0
POOL_R = (1, 2, 4, 8)
LRU_C = 8.0
LN_EPS = 1e-5
NEG_INF = -1e30
ALPHA = 4.0 ** 0.25
LR, B1, B2, EPS, WD, STEP = 0.001, 0.9, 0.999, 1e-08, 0.01, 10
VMEM_LIMIT = 56 * 1024 * 1024
ROW_TILE = 512


def _params(sem=None):
    if sem is None:
        return pltpu.CompilerParams(vmem_limit_bytes=VMEM_LIMIT)
    return pltpu.CompilerParams(dimension_semantics=sem, vmem_limit_bytes=VMEM_LIMIT)


def _sigmoid(x):
    return 1.0 / (1.0 + jnp.exp(-x))


def _dot(a, b):
    return jnp.dot(a.astype(BF16), b.astype(BF16), preferred_element_type=F32)


def _dot_nt(a, b):
    return lax.dot_general(a.astype(BF16), b.astype(BF16), (((1,), (1,)), ((), ())), preferred_element_type=F32)


def _dot_tn(a, b):
    return lax.dot_general(a.astype(BF16), b.astype(BF16), (((0,), (0,)), ((), ())), preferred_element_type=F32)


def _pick(n, cap):
    best = None
    for m in range(128, min(n, cap) + 1, 128):
        if n % m == 0:
            best = m
    return n if best is None else best


def _chunks(width, step=256):
    out, c = [], 0
    while c < width:
        w = min(step, width - c)
        out.append((c, w))
        c += w
    return out


class _Cfg:
    def __init__(self, n_lat, n_ctx):
        self.n_lat, self.n_ctx = n_lat, n_ctx
        self.t_lat, self.t_ctx = 2 * n_lat, 2 * n_ctx
        self.T = self.t_lat + self.t_ctx
        self.tm = min(ROW_TILE, self.t_ctx)
        assert n_lat % self.tm == 0 and self.t_ctx % self.tm == 0 and n_lat >= 3 * BLK and n_ctx % BLK == 0
        self.nt = self.T // self.tm
        self.nlt = n_lat // self.tm
        self.ctx_blk = self.t_lat // n_ctx

    def seg(self, i):
        return jnp.minimum(i // self.nlt, 2)

    def first_of_seg(self, i):
        return jnp.where(i < 2 * self.nlt, i % self.nlt == 0, i == 2 * self.nlt)


def _modulate(cfg, h, mod, shift_idx, scale_idx, name):
    tm = cfg.tm

    def body(h_ref, mod_ref, o_ref):
        sh = mod_ref[shift_idx:shift_idx + 1, :]
        sc = mod_ref[scale_idx:scale_idx + 1, :]
        o_ref[...] = (h_ref[...] * (1.0 + sc) + sh).astype(BF16)

    return pl.pallas_call(
        body, grid=(cfg.nt,), name=name,
        in_specs=[pl.BlockSpec((tm, D), lambda i: (i, 0)),
                  pl.BlockSpec((None, N_MOD, D), lambda i: (cfg.seg(i), 0, 0))],
        out_specs=pl.BlockSpec((tm, D), lambda i: (i, 0)),
        out_shape=jax.ShapeDtypeStruct((cfg.T, D), BF16),
        compiler_params=_params(("parallel",)),
    )(h, mod)


def _ln_fwd(cfg, h, y, mod, gate_idx, coef, lng, lnb, mod_next, next_idx, name):
    tm = cfg.tm
    has_next = next_idx is not None

    def body(*refs):
        if has_next:
            h_ref, y_ref, mod_ref, g_ref, b_ref, modn_ref, hn_ref, xhat_ref, rstd_ref, xin_ref = refs
        else:
            h_ref, y_ref, mod_ref, g_ref, b_ref, hn_ref, xhat_ref, rstd_ref = refs
        gate = mod_ref[gate_idx:gate_idx + 1, :]
        z = ALPHA * h_ref[...] + (coef * gate) * y_ref[...]
        mu = jnp.mean(z, axis=-1, keepdims=True)
        zc = z - mu
        var = jnp.mean(zc * zc, axis=-1, keepdims=True)
        rstd = lax.rsqrt(var + LN_EPS)
        xhat = zc * rstd
        hn = xhat * g_ref[...] + b_ref[...]
        hn_ref[...] = hn
        xhat_ref[...] = xhat
        rstd_ref[...] = rstd
        if has_next:
            sh = modn_ref[next_idx[0]:next_idx[0] + 1, :]
            sc = modn_ref[next_idx[1]:next_idx[1] + 1, :]
            xin_ref[...] = (hn * (1.0 + sc) + sh).astype(BF16)

    row = pl.BlockSpec((tm, D), lambda i: (i, 0))
    modspec = pl.BlockSpec((None, N_MOD, D), lambda i: (cfg.seg(i), 0, 0))
    vec = pl.BlockSpec((1, D), lambda i: (0, 0))
    in_specs = [row, row, modspec, vec, vec]
    args = [h, y, mod, lng, lnb]
    out_specs = [row, row, pl.BlockSpec((tm, 1), lambda i: (i, 0))]
    out_shape = [jax.ShapeDtypeStruct((cfg.T, D), F32), jax.ShapeDtypeStruct((cfg.T, D), F32),
                 jax.ShapeDtypeStruct((cfg.T, 1), F32)]
    if has_next:
        in_specs.append(modspec)
        args.append(mod_next)
        out_specs.append(row)
        out_shape.append(jax.ShapeDtypeStruct((cfg.T, D), BF16))
    return pl.pallas_call(body, grid=(cfg.nt,), name=name, in_specs=in_specs, out_specs=out_specs,
                          out_shape=out_shape, compiler_params=_params(("parallel",)))(*args)


def _ln_bwd(cfg, up, xhat, rstd, y, mod, gate_idx, coef, lng, name):
    tm = cfg.tm
    fused = len(up) > 1
    scale_next = up[4] if fused else None

    def body(*refs):
        if fused:
            dres_n, dxin_n, hn_ref, modn_ref, xhat_ref, rstd_ref, y_ref, mod_ref, g_ref, dres_ref, dys_ref, st_ref = refs
        else:
            dhn_ref, xhat_ref, rstd_ref, y_ref, mod_ref, g_ref, dres_ref, dys_ref, st_ref = refs
        i = pl.program_id(0)

        @pl.when(cfg.first_of_seg(i))
        def _():
            st_ref[...] = jnp.zeros_like(st_ref)

        if fused:
            dxin = dxin_n[...]
            sc = modn_ref[scale_next:scale_next + 1, :]
            dhn = dres_n[...] + dxin * (1.0 + sc)
            st_ref[3:4, :] += jnp.sum(dxin * hn_ref[...], axis=0, keepdims=True)
            st_ref[4:5, :] += jnp.sum(dxin, axis=0, keepdims=True)
        else:
            dhn = dhn_ref[...]
        xhat = xhat_ref[...]
        gdh = dhn * g_ref[...]
        m1 = jnp.mean(gdh, axis=-1, keepdims=True)
        m2 = jnp.mean(gdh * xhat, axis=-1, keepdims=True)
        dz = rstd_ref[...] * (gdh - m1 - xhat * m2)
        gate = mod_ref[gate_idx:gate_idx + 1, :]
        dres_ref[...] = ALPHA * dz
        dys_ref[...] = ((coef * gate) * dz).astype(BF16)
        st_ref[0:1, :] += jnp.sum(dhn * xhat, axis=0, keepdims=True)
        st_ref[1:2, :] += jnp.sum(dhn, axis=0, keepdims=True)
        st_ref[2:3, :] += jnp.sum((coef * dz) * y_ref[...], axis=0, keepdims=True)

    row = pl.BlockSpec((tm, D), lambda i: (i, 0))
    modspec = pl.BlockSpec((None, N_MOD, D), lambda i: (cfg.seg(i), 0, 0))
    vec = pl.BlockSpec((1, D), lambda i: (0, 0))
    col = pl.BlockSpec((tm, 1), lambda i: (i, 0))
    if fused:
        in_specs = [row, row, row, modspec, row, col, row, modspec, vec]
        args = [up[0], up[1], up[2], up[3], xhat, rstd, y, mod, lng]
    else:
        in_specs = [row, row, col, row, modspec, vec]
        args = [up[0], xhat, rstd, y, mod, lng]
    return pl.pallas_call(
        body, grid=(cfg.nt,), name=name, in_specs=in_specs,
        out_specs=[row, row, pl.BlockSpec((None, 8, D), lambda i: (cfg.seg(i), 0, 0))],
        out_shape=[jax.ShapeDtypeStruct((cfg.T, D), F32), jax.ShapeDtypeStruct((cfg.T, D), BF16),
                   jax.ShapeDtypeStruct((3, 8, D), F32)],
        compiler_params=_params(("arbitrary",)))(*args)


def _modulate_bwd(cfg, dres, dxin, h, mod, scale_idx, name):
    tm = cfg.tm

    def body(dres_ref, dxin_ref, h_ref, mod_ref, dh_ref, st_ref):
        i = pl.program_id(0)

        @pl.when(cfg.first_of_seg(i))
        def _():
            st_ref[...] = jnp.zeros_like(st_ref)

        dxin = dxin_ref[...]
        sc = mod_ref[scale_idx:scale_idx + 1, :]
        dh_ref[...] = dres_ref[...] + dxin * (1.0 + sc)
        st_ref[3:4, :] += jnp.sum(dxin * h_ref[...], axis=0, keepdims=True)
        st_ref[4:5, :] += jnp.sum(dxin, axis=0, keepdims=True)

    row = pl.BlockSpec((tm, D), lambda i: (i, 0))
    return pl.pallas_call(
        body, grid=(cfg.nt,), name=name,
        in_specs=[row, row, row, pl.BlockSpec((None, N_MOD, D), lambda i: (cfg.seg(i), 0, 0))],
        out_specs=[row, pl.BlockSpec((None, 8, D), lambda i: (cfg.seg(i), 0, 0))],
        out_shape=[jax.ShapeDtypeStruct((cfg.T, D), F32), jax.ShapeDtypeStruct((3, 8, D), F32)],
        compiler_params=_params(("arbitrary",)))(dres, dxin, h, mod)


def _loss(cfg, h, target, name):
    tm = cfg.tm
    n_lt = 2 * cfg.nlt

    def body(h_ref, t_ref, dy_ref, l_ref):
        i = pl.program_id(0)

        @pl.when(i == 0)
        def _():
            l_ref[...] = jnp.zeros_like(l_ref)

        @pl.when(i < n_lt)
        def _():
            err = h_ref[...] - t_ref[...]
            dy_ref[...] = err * (1.0 / D)
            part = jnp.sum(jnp.sum(err * err, axis=1, keepdims=True), axis=0, keepdims=True) * (0.5 / D)
            l_ref[...] += jnp.broadcast_to(part, l_ref.shape)

        @pl.when(i >= n_lt)
        def _():
            dy_ref[...] = jnp.zeros_like(dy_ref)

    return pl.pallas_call(
        body, grid=(cfg.nt,), name=name,
        in_specs=[pl.BlockSpec((tm, D), lambda i: (i, 0)),
                  pl.BlockSpec((tm, D), lambda i: (jnp.minimum(i, n_lt - 1), 0))],
        out_specs=[pl.BlockSpec((tm, D), lambda i: (i, 0)), pl.BlockSpec((8, 128), lambda i: (0, 0))],
        out_shape=[jax.ShapeDtypeStruct((cfg.T, D), F32), jax.ShapeDtypeStruct((8, 128), F32)],
        compiler_params=_params(("arbitrary",)))(h, target)


def _matmul(a, b, mode, out_dtype, name, bm_cap=512, bn_cap=1408, bk_cap=1024):
    if mode == "nn":
        (M, K), N = a.shape, b.shape[1]
    elif mode == "nt":
        (M, K), N = a.shape, b.shape[0]
    else:
        (K, M), N = a.shape, b.shape[1]
    bm, bn, bk = _pick(M, bm_cap), _pick(N, bn_cap), _pick(K, bk_cap)
    nk = K // bk

    def body(a_ref, b_ref, o_ref, acc_ref):
        k = pl.program_id(2)
        if mode == "nn":
            part = _dot(a_ref[...], b_ref[...])
        elif mode == "nt":
            part = _dot_nt(a_ref[...], b_ref[...])
        else:
            part = _dot_tn(a_ref[...], b_ref[...])

        @pl.when(k == 0)
        def _():
            acc_ref[...] = part

        @pl.when(k > 0)
        def _():
            acc_ref[...] += part

        @pl.when(k == nk - 1)
        def _():
            o_ref[...] = acc_ref[...].astype(out_dtype)

    if mode == "nn":
        a_spec = pl.BlockSpec((bm, bk), lambda i, j, k: (i, k))
        b_spec = pl.BlockSpec((bk, bn), lambda i, j, k: (k, j))
    elif mode == "nt":
        a_spec = pl.BlockSpec((bm, bk), lambda i, j, k: (i, k))
        b_spec = pl.BlockSpec((bn, bk), lambda i, j, k: (j, k))
    else:
        a_spec = pl.BlockSpec((bk, bm), lambda i, j, k: (k, i))
        b_spec = pl.BlockSpec((bk, bn), lambda i, j, k: (k, j))
    return pl.pallas_call(
        body, grid=(M // bm, N // bn, nk), name=name, in_specs=[a_spec, b_spec],
        out_specs=pl.BlockSpec((bm, bn), lambda i, j, k: (i, j)),
        out_shape=jax.ShapeDtypeStruct((M, N), out_dtype),
        scratch_shapes=[pltpu.VMEM((bm, bn), F32)],
        compiler_params=_params(("parallel", "parallel", "arbitrary")))(a, b)


def _ffn_tile(T, cap):
    best = 256
    for t in range(256, cap + 1, 256):
        if T % t == 0:
            best = t
    return best


def _ffn_fwd(xin, wf, name):
    T = xin.shape[0]
    F = wf.shape[1]
    tm, tf = _ffn_tile(T, 768), F // 2
    assert tf % 128 == 0 and T % tm == 0

    def body(x_ref, wg_ref, wu_ref, wd_ref, g_ref, u_ref, y_ref):
        j = pl.program_id(1)
        x = x_ref[...]
        acc = None
        for c0, cw in _chunks(tf):
            g = _dot_nt(x, wg_ref[c0:c0 + cw, :])
            u = _dot_nt(x, wu_ref[c0:c0 + cw, :])
            g_ref[:, c0:c0 + cw] = g.astype(BF16)
            u_ref[:, c0:c0 + cw] = u.astype(BF16)
            part = _dot(g * _sigmoid(g) * u, wd_ref[c0:c0 + cw, :])
            acc = part if acc is None else acc + part

        @pl.when(j == 0)
        def _():
            y_ref[...] = acc

        @pl.when(j > 0)
        def _():
            y_ref[...] += acc

    return pl.pallas_call(
        body, grid=(T // tm, 2), name=name,
        in_specs=[pl.BlockSpec((tm, D), lambda i, j: (i, 0)),
                  pl.BlockSpec((None, tf, D), lambda i, j: (0, j, 0)),
                  pl.BlockSpec((None, tf, D), lambda i, j: (1, j, 0)),
                  pl.BlockSpec((None, tf, D), lambda i, j: (2, j, 0))],
        out_specs=[pl.BlockSpec((tm, tf), lambda i, j: (i, j)),
                   pl.BlockSpec((tm, tf), lambda i, j: (i, j)),
                   pl.BlockSpec((tm, D), lambda i, j: (i, 0))],
        out_shape=[jax.ShapeDtypeStruct((T, F), BF16), jax.ShapeDtypeStruct((T, F), BF16),
                   jax.ShapeDtypeStruct((T, D), F32)],
        compiler_params=_params(("parallel", "arbitrary")))(xin, wf, wf, wf)


def _ffn_bwd(dys, g, u, wf, name):
    T = dys.shape[0]
    F = wf.shape[1]
    tm, tf = _ffn_tile(T, 512), F // 2

    def body(dy_ref, g_ref, u_ref, wg_ref, wu_ref, wd_ref, dg_ref, du_ref, a_ref, dx_ref):
        j = pl.program_id(1)
        dy = dy_ref[...]
        acc = None
        for c0, cw in _chunks(tf):
            gg = g_ref[:, c0:c0 + cw].astype(F32)
            uu = u_ref[:, c0:c0 + cw].astype(F32)
            da = _dot_nt(dy, wd_ref[c0:c0 + cw, :])
            s = _sigmoid(gg)
            silu = gg * s
            a_ref[:, c0:c0 + cw] = (silu * uu).astype(BF16)
            du = (da * silu).astype(BF16)
            dg = (da * uu * (s * (1.0 + gg * (1.0 - s)))).astype(BF16)
            du_ref[:, c0:c0 + cw] = du
            dg_ref[:, c0:c0 + cw] = dg
            part = _dot(dg, wg_ref[c0:c0 + cw, :]) + _dot(du, wu_ref[c0:c0 + cw, :])
            acc = part if acc is None else acc + part

        @pl.when(j == 0)
        def _():
            dx_ref[...] = acc

        @pl.when(j > 0)
        def _():
            dx_ref[...] += acc

    blk = pl.BlockSpec((tm, tf), lambda i, j: (i, j))
    return pl.pallas_call(
        body, grid=(T // tm, 2), name=name,
        in_specs=[pl.BlockSpec((tm, D), lambda i, j: (i, 0)), blk, blk,
                  pl.BlockSpec((None, tf, D), lambda i, j: (0, j, 0)),
                  pl.BlockSpec((None, tf, D), lambda i, j: (1, j, 0)),
                  pl.BlockSpec((None, tf, D), lambda i, j: (2, j, 0))],
        out_specs=[blk, blk, blk, pl.BlockSpec((tm, D), lambda i, j: (i, 0))],
        out_shape=[jax.ShapeDtypeStruct((T, F), BF16), jax.ShapeDtypeStruct((T, F), BF16),
                   jax.ShapeDtypeStruct((T, F), BF16), jax.ShapeDtypeStruct((T, D), F32)],
        compiler_params=_params(("parallel", "arbitrary")))(dys, g, u, wf, wf, wf)


def _swap_halves(x):
    w = x.shape[1]
    lane = lax.broadcasted_iota(jnp.int32, (1, w), 1)
    return jnp.where((lane & 63) < 32, pltpu.roll(x, w - 32, 1), pltpu.roll(x, 32, 1))


def _rope(x, cos, sin):
    return x * cos + _swap_halves(x) * sin


def _rope_t(dy, cos, sin):
    return dy * cos + _swap_halves(dy * sin)


def _rope_tables(n_lat):
    rows = n_lat // GRID_W
    row = jnp.repeat(jnp.arange(rows, dtype=F32), GRID_W)
    col = jnp.tile(jnp.arange(GRID_W, dtype=F32), rows)
    inv = ROPE_THETA ** (-jnp.arange(ROPE_FREQS, dtype=F32) / ROPE_FREQS)
    ang = jnp.concatenate([row[:, None] * inv, col[:, None] * inv], axis=-1)
    cs, sn = jnp.cos(ang), jnp.sin(ang)
    cos = jnp.concatenate([cs, cs, cs, cs], axis=-1)
    sin = jnp.concatenate([-sn, sn, -sn, sn], axis=-1)
    return cos, sin


def _attn_specs(cfg):
    n_lat, n_ctx, cb = cfg.n_lat, cfg.n_ctx, cfg.ctx_blk
    return [pl.BlockSpec((n_lat, ATT_W), lambda e: (e, 0)),
            pl.BlockSpec((n_lat, 128), lambda e: (e, 4)),
            pl.BlockSpec((n_lat, 128), lambda e: (e, 5)),
            pl.BlockSpec((n_ctx, ATT_W), lambda e: (cb + e, 0)),
            pl.BlockSpec((n_ctx, 128), lambda e: (cb + e, 4)),
            pl.BlockSpec((n_ctx, 128), lambda e: (cb + e, 5)),
            pl.BlockSpec((n_lat, 128), lambda e: (0, 0)),
            pl.BlockSpec((n_lat, 128), lambda e: (0, 0)),
            pl.BlockSpec((8, 128), lambda e: (0, 0))]


def _attn_prepare(kh, kl, vl, kc, vc, ka, kb, va, vb, kca, kcb, vca, vcb):
    lane = lax.broadcasted_iota(jnp.int32, (1, 128), 1)
    own = (lane < 64) if kh == 0 else (lane >= 64)

    def split(x, ra, rb):
        mine = jnp.where(own, x, 0.0)
        other = pltpu.roll(mine, 64, 1)
        a, b = (mine, other) if kh == 0 else (other, mine)
        ra[...] = a.astype(BF16)
        rb[...] = b.astype(BF16)

    split(kl, ka, kb)
    split(vl, va, vb)
    split(kc, kca, kcb)
    split(vc, vca, vcb)


def _softmax_parts(s_list, sk):
    m = sk
    for s in s_list:
        m = jnp.maximum(m, jnp.max(s, axis=1, keepdims=True))
    es = [jnp.exp(s - m) for s in s_list]
    esk = jnp.exp(sk - m)
    den = esk
    for e in es:
        den = den + jnp.sum(e, axis=1, keepdims=True)
    inv = 1.0 / den
    return [e * inv for e in es], esk * inv


def _window(cfg, n):
    r0 = pl.multiple_of(n * BLK, BLK)
    start = pl.multiple_of(jnp.clip((n - 1) * BLK, 0, cfg.n_lat - 3 * BLK), BLK)
    qpos = r0 + lax.broadcasted_iota(jnp.int32, (BLK, 1), 0)
    kpos = start + lax.broadcasted_iota(jnp.int32, (1, 3 * BLK), 1)
    valid = jnp.abs(qpos - kpos) <= BLK
    return r0, start, valid


def _attn_fwd(cfg, p, cos, sin, sink_rows, name):
    n_lat, n_ctx = cfg.n_lat, cfg.n_ctx

    def body(q_ref, k_ref, v_ref, qc_ref, kc_ref, vc_ref, cos_ref, sin_ref, sink_ref, o_ref, oc_ref,
             qr, ka, kb, va, vb, kca, kcb, vca, vcb):
        cos_t, sin_t = cos_ref[...], sin_ref[...]
        for gq in range(4):
            qr[:, gq * 128:(gq + 1) * 128] = _rope(q_ref[:, gq * 128:(gq + 1) * 128], cos_t, sin_t).astype(BF16)
        kl = _rope(k_ref[...], cos_t, sin_t)
        for kh in range(KV_HEADS):
            _attn_prepare(kh, kl, v_ref[...], kc_ref[...], vc_ref[...], ka, kb, va, vb, kca, kcb, vca, vcb)

            def lat_block(n, carry):
                r0, start, valid = _window(cfg, n)
                win = pl.ds(start, 3 * BLK)
                for pr in range(2):
                    lanes = slice((kh * 2 + pr) * 128, (kh * 2 + pr + 1) * 128)
                    qp = qr[pl.ds(r0, BLK), lanes]
                    o = None
                    for half, (kw, kcx, vw, vcx) in enumerate(((ka, kca, va, vca), (kb, kcb, vb, vcb))):
                        head = kh * 4 + pr * 2 + half
                        s_w = jnp.where(valid, _dot_nt(qp, kw[win, :]) * ATT_SCALE, NEG_INF)
                        s_c = _dot_nt(qp, kcx[...]) * ATT_SCALE
                        (p_w, p_c), _ = _softmax_parts([s_w, s_c], sink_ref[head:head + 1, 0:1])
                        part = _dot(p_w, vw[win, :]) + _dot(p_c, vcx[...])
                        o = part if o is None else o + part
                    o_ref[pl.ds(r0, BLK), lanes] = o.astype(BF16)
                return carry

            lax.fori_loop(0, n_lat // BLK, lat_block, 0)
            for n in range(n_ctx // BLK):
                rows = slice(n * BLK, (n + 1) * BLK)
                for pr in range(2):
                    lanes = slice((kh * 2 + pr) * 128, (kh * 2 + pr + 1) * 128)
                    qp = qc_ref[rows, lanes]
                    o = None
                    for half, (kcx, vcx) in enumerate(((kca, vca), (kcb, vcb))):
                        head = kh * 4 + pr * 2 + half
                        s_c = _dot_nt(qp, kcx[...]) * ATT_SCALE
                        (p_c,), _ = _softmax_parts([s_c], sink_ref[head:head + 1, 0:1])
                        part = _dot(p_c, vcx[...])
                        o = part if o is None else o + part
                    oc_ref[rows, lanes] = o.astype(BF16)

    return pl.pallas_call(
        body, grid=(2,), name=name, in_specs=_attn_specs(cfg),
        out_specs=[pl.BlockSpec((n_lat, ATT_W), lambda e: (e, 0)), pl.BlockSpec((n_ctx, ATT_W), lambda e: (e, 0))],
        out_shape=[jax.ShapeDtypeStruct((cfg.t_lat, ATT_W), BF16), jax.ShapeDtypeStruct((cfg.t_ctx, ATT_W), BF16)],
        scratch_shapes=[pltpu.VMEM((n_lat, ATT_W), BF16)] + [pltpu.VMEM((n_lat, 128), BF16)] * 4
        + [pltpu.VMEM((n_ctx, 128), BF16)] * 4,
        compiler_params=_params(("parallel",)))(p, p, p, p, p, p, cos, sin, sink_rows)


def _attn_bwd(cfg, p, dcat, cos, sin, sink_rows, name):
    n_lat, n_ctx, cb = cfg.n_lat, cfg.n_ctx, cfg.ctx_blk

    def body(q_ref, k_ref, v_ref, qc_ref, kc_ref, vc_ref, cos_ref, sin_ref, sink_ref, do_ref, doc_ref,
             dq_ref, dk_ref, dv_ref, dqc_ref, dkc_ref, dvc_ref, dsink_ref,
             qr, ka, kb, va, vb, kca, kcb, vca, vcb, dqs, dka, dva, dkca, dvca):
        cos_t, sin_t = cos_ref[...], sin_ref[...]
        lane = lax.broadcasted_iota(jnp.int32, (1, 128), 1)
        lo = lane < 64
        for gq in range(4):
            qr[:, gq * 128:(gq + 1) * 128] = _rope(q_ref[:, gq * 128:(gq + 1) * 128], cos_t, sin_t).astype(BF16)
        kl = _rope(k_ref[...], cos_t, sin_t)
        dsink_ref[...] = jnp.zeros_like(dsink_ref)
        dka[...] = jnp.zeros_like(dka)
        dva[...] = jnp.zeros_like(dva)
        dkca[...] = jnp.zeros_like(dkca)
        dvca[...] = jnp.zeros_like(dvca)

        def halves(x):
            return jnp.where(lo, x, 0).astype(BF16), jnp.where(lo, 0, x).astype(BF16)

        for kh in range(KV_HEADS):
            _attn_prepare(kh, kl, v_ref[...], kc_ref[...], vc_ref[...], ka, kb, va, vb, kca, kcb, vca, vcb)

            def one_head(head, qp, q_half, do_p, do_half, kw, kcx, vw, vcx, win, valid):
                sk = sink_ref[head:head + 1, 0:1]
                s_list = [_dot_nt(qp, kcx[...]) * ATT_SCALE]
                if win is not None:
                    s_list.insert(0, jnp.where(valid, _dot_nt(qp, kw[win, :]) * ATT_SCALE, NEG_INF))
                probs, p_sink = _softmax_parts(s_list, sk)
                vals = [vcx[...]] if win is None else [vw[win, :], vcx[...]]
                dps = [_dot_nt(do_p, vv) for vv in vals]
                dr = None
                for pp, dp in zip(probs, dps):
                    t = jnp.sum(pp * dp, axis=1, keepdims=True)
                    dr = t if dr is None else dr + t
                dss = [(pp * (dp - dr) * ATT_SCALE).astype(BF16) for pp, dp in zip(probs, dps)]
                dsink_ref[head:head + 1, :] += jnp.broadcast_to(
                    jnp.sum(-p_sink * dr, axis=0, keepdims=True), (1, 128))
                p_c, ds_c = probs[-1], dss[-1]
                dq = _dot(ds_c, kcx[...])
                dkca[kh] += _dot_tn(ds_c, q_half)
                dvca[kh] += _dot_tn(p_c, do_half)
                if win is not None:
                    dq = dq + _dot(dss[0], kw[win, :])
                    dka[kh, win, :] += _dot_tn(dss[0], q_half)
                    dva[kh, win, :] += _dot_tn(probs[0], do_half)
                return dq

            def lat_block(n, carry):
                r0, start, valid = _window(cfg, n)
                win = pl.ds(start, 3 * BLK)
                for pr in range(2):
                    lanes = slice((kh * 2 + pr) * 128, (kh * 2 + pr + 1) * 128)
                    qp = qr[pl.ds(r0, BLK), lanes]
                    do_p = do_ref[pl.ds(r0, BLK), lanes]
                    q_h, do_h = halves(qp), halves(do_p)
                    dq = None
                    for half, (kw, kcx, vw, vcx) in enumerate(((ka, kca, va, vca), (kb, kcb, vb, vcb))):
                        part = one_head(kh * 4 + pr * 2 + half, qp, q_h[half], do_p, do_h[half],
                                        kw, kcx, vw, vcx, win, valid)
                        dq = part if dq is None else dq + part
                    dqs[pl.ds(r0, BLK), lanes] = dq
                return carry

            lax.fori_loop(0, n_lat // BLK, lat_block, 0)
            for n in range(n_ctx // BLK):
                rows = slice(n * BLK, (n + 1) * BLK)
                for pr in range(2):
                    lanes = slice((kh * 2 + pr) * 128, (kh * 2 + pr + 1) * 128)
                    qp = qc_ref[rows, lanes].astype(BF16)
                    do_p = doc_ref[rows, lanes]
                    q_h, do_h = halves(qp), halves(do_p)
                    dq = None
                    for half, (kcx, vcx) in enumerate(((kca, vca), (kcb, vcb))):
                        part = one_head(kh * 4 + pr * 2 + half, qp, q_h[half], do_p, do_h[half],
                                        None, kcx, None, vcx, None, None)
                        dq = part if dq is None else dq + part
                    dqc_ref[rows, lanes] = dq.astype(BF16)

        def fold(acc):
            r0 = acc[0] + pltpu.roll(acc[0], 64, 1)
            r1 = acc[1] + pltpu.roll(acc[1], 64, 1)
            return jnp.where(lo, r0, r1)

        for gq in range(4):
            sl = slice(gq * 128, (gq + 1) * 128)
            dq_ref[:, sl] = _rope_t(dqs[:, sl], cos_t, sin_t).astype(BF16)
        dk_ref[...] = _rope_t(fold(dka), cos_t, sin_t).astype(BF16)
        dv_ref[...] = fold(dva).astype(BF16)
        dkc_ref[...] = fold(dkca).astype(BF16)
        dvc_ref[...] = fold(dvca).astype(BF16)

    lat = lambda w: pl.BlockSpec((n_lat, w), lambda e: (e, 0))
    ctx = lambda w: pl.BlockSpec((n_ctx, w), lambda e: (e, 0))
    sd = jax.ShapeDtypeStruct
    return pl.pallas_call(
        body, grid=(2,), name=name,
        in_specs=_attn_specs(cfg) + [pl.BlockSpec((n_lat, ATT_W), lambda e: (e, 0)),
                                     pl.BlockSpec((n_ctx, ATT_W), lambda e: (cb + e, 0))],
        out_specs=[lat(ATT_W), lat(128), lat(128), ctx(ATT_W), ctx(128), ctx(128),
                   pl.BlockSpec((None, 8, 128), lambda e: (e, 0, 0))],
        out_shape=[sd((cfg.t_lat, ATT_W), BF16), sd((cfg.t_lat, 128), BF16), sd((cfg.t_lat, 128), BF16),
                   sd((cfg.t_ctx, ATT_W), BF16), sd((cfg.t_ctx, 128), BF16), sd((cfg.t_ctx, 128), BF16),
                   sd((2, 8, 128), F32)],
        scratch_shapes=[pltpu.VMEM((n_lat, ATT_W), BF16)] + [pltpu.VMEM((n_lat, 128), BF16)] * 4
        + [pltpu.VMEM((n_ctx, 128), BF16)] * 4
        + [pltpu.VMEM((n_lat, ATT_W), F32), pltpu.VMEM((2, n_lat, 128), F32), pltpu.VMEM((2, n_lat, 128), F32),
           pltpu.VMEM((2, n_ctx, 128), F32), pltpu.VMEM((2, n_ctx, 128), F32)],
        compiler_params=_params(("parallel",)))(p, p, p, p, p, p, cos, sin, sink_rows, dcat, dcat)


def _shift_down(x, k, row):
    return jnp.where(row >= k, pltpu.roll(x, k, 0), 0.0)


def _shift_up(x, k, row):
    n = x.shape[0]
    return jnp.where(row < n - k, pltpu.roll(x, n - k, 0), 0.0)


def _window_sum(x, r, row):
    below, above, k = x, x, 1
    while k < r:
        below = below + _shift_down(below, k, row)
        above = above + _shift_up(above, k, row)
        k *= 2
    return below + _shift_down(x, r, row) + _shift_up(above, 1, row)


def _inv_count(r, row, n):
    cnt = jnp.minimum(row + r, n - 1) + 1 - jnp.maximum(row - r, 0)
    return 1.0 / cnt.astype(F32)


def _pool_fwd(p, w, scale, n, blk0, n_seg, name):
    def body(u0, u1, u2, u3, w_ref, sc_ref, o_ref):
        row = lax.broadcasted_iota(jnp.int32, (n, 1), 0)
        for g, u_ref in enumerate((u0, u1, u2, u3)):
            u = u_ref[...]
            d = _window_sum(u, POOL_R[g], row) * _inv_count(POOL_R[g], row, n) - u
            o_ref[:, g * 128:(g + 1) * 128] = (_dot(d, w_ref[g]) * sc_ref[:, g * 128:(g + 1) * 128]).astype(BF16)

    return pl.pallas_call(
        body, grid=(n_seg,), name=name,
        in_specs=[pl.BlockSpec((n, 128), functools.partial(lambda g, e: (blk0 + e, 6 + g), g)) for g in range(4)]
        + [pl.BlockSpec((4, 128, 128), lambda e: (0, 0, 0)), pl.BlockSpec((1, 512), lambda e: (0, 0))],
        out_specs=pl.BlockSpec((n, 512), lambda e: (e, 0)),
        out_shape=jax.ShapeDtypeStruct((n_seg * n, 512), BF16),
        compiler_params=_params(("parallel",)))(p, p, p, p, w, scale)


def _pool_bwd(p, w, scale, dcat, n, blk0, n_seg, name):
    def body(u0, u1, u2, u3, w_ref, sc_ref, dp_ref, du_ref, dw_ref, dsc_ref):
        e = pl.program_id(0)

        @pl.when(e == 0)
        def _():
            dw_ref[...] = jnp.zeros_like(dw_ref)
            dsc_ref[...] = jnp.zeros_like(dsc_ref)

        row = lax.broadcasted_iota(jnp.int32, (n, 1), 0)
        for g, u_ref in enumerate((u0, u1, u2, u3)):
            sl = slice(g * 128, (g + 1) * 128)
            u = u_ref[...]
            inv = _inv_count(POOL_R[g], row, n)
            d = _window_sum(u, POOL_R[g], row) * inv - u
            dp = dp_ref[:, sl]
            dsc_ref[:, sl] += jnp.sum(dp * _dot(d, w_ref[g]), axis=0, keepdims=True)
            dyp = dp * sc_ref[:, sl]
            dw_ref[g] += _dot_tn(d, dyp)
            dd = _dot_nt(dyp, w_ref[g])
            du_ref[:, sl] = (_window_sum(dd * inv, POOL_R[g], row) - dd).astype(BF16)

    return pl.pallas_call(
        body, grid=(n_seg,), name=name,
        in_specs=[pl.BlockSpec((n, 128), functools.partial(lambda g, e: (blk0 + e, 6 + g), g)) for g in range(4)]
        + [pl.BlockSpec((4, 128, 128), lambda e: (0, 0, 0)), pl.BlockSpec((1, 512), lambda e: (0, 0)),
           pl.BlockSpec((n, 512), lambda e: (blk0 + e, 1))],
        out_specs=[pl.BlockSpec((n, 512), lambda e: (e, 0)),
                   pl.BlockSpec((4, 128, 128), lambda e: (0, 0, 0)), pl.BlockSpec((1, 512), lambda e: (0, 0))],
        out_shape=[jax.ShapeDtypeStruct((n_seg * n, 512), BF16), jax.ShapeDtypeStruct((4, 128, 128), F32),
                   jax.ShapeDtypeStruct((1, 512), F32)],
        compiler_params=_params(("arbitrary",)))(p, p, p, p, w, scale, dcat)


def _gelu(x):
    t = jnp.tanh(math.sqrt(2.0 / math.pi) * (x + 0.044715 * x * x * x))
    return 0.5 * x * (1.0 + t), t


def _gelu_grad(x, t):
    return 0.5 * (1.0 + t) + 0.5 * x * (1.0 - t * t) * (math.sqrt(2.0 / math.pi) * (1.0 + 3 * 0.044715 * x * x))


def _neg_expm1(x):
    series = -x * (1.0 + x * (0.5 + x * (1.0 / 6.0 + x * (1.0 / 24.0 + x * (1.0 / 120.0)))))
    return jnp.where(x > -0.05, series, 1.0 - jnp.exp(x))


def _softplus_neg(lam):
    x = -lam
    e = jnp.exp(-jnp.abs(x))
    log1p = jnp.where(e < 1e-2, e * (1.0 - e * (0.5 - e * (1.0 / 3.0))), jnp.log(1.0 + e))
    return jnp.maximum(x, 0.0) + log1p, -_sigmoid(x)


def _conv(u, w_ref, b_ref, row):
    return (b_ref[...] + _shift_down(u, 1, row) * w_ref[0:1, :] + u * w_ref[1:2, :]
            + _shift_up(u, 1, row) * w_ref[2:3, :] + _shift_up(u, 2, row) * w_ref[3:4, :])


def _lru_gates(uc, d, wa_ref, ba_ref, wx_ref, bx_ref, lam_ref):
    r = _sigmoid(_dot(uc, wa_ref[d]) + ba_ref[d:d + 1, :])
    gi = _sigmoid(_dot(uc, wx_ref[d]) + bx_ref[d:d + 1, :])
    sp, dsp = _softplus_neg(lam_ref[d:d + 1, :])
    la = (-LRU_C) * r * sp
    a = jnp.exp(la)
    sq = jnp.sqrt(_neg_expm1(2.0 * la))
    return r, gi, sp, dsp, a, sq


def _tile_scan(a, b, reverse):
    n = a.shape[0]
    row8 = lax.broadcasted_iota(jnp.int32, (n, 1), 0) & 7
    for k in (1, 2, 4):
        if reverse:
            m = row8 < 8 - k
            a_sh = jnp.where(m, pltpu.roll(a, n - k, 0), 1.0)
            b_sh = jnp.where(m, pltpu.roll(b, n - k, 0), 0.0)
        else:
            m = row8 >= k
            a_sh = jnp.where(m, pltpu.roll(a, k, 0), 1.0)
            b_sh = jnp.where(m, pltpu.roll(b, k, 0), 0.0)
        b = a * b_sh + b
        a = a * a_sh
    return a, b


def _carry_scan(a_ref, b_ref, n, reverse, carry):
    nt8 = n // 8

    def step(i, c):
        t = (nt8 - 1 - i) if reverse else i
        off = pl.multiple_of(t * 8, 8)
        h = a_ref[pl.ds(off, 8), :] * c + b_ref[pl.ds(off, 8), :]
        b_ref[pl.ds(off, 8), :] = h
        return h[0:1, :] if reverse else h[7:8, :]

    return lax.fori_loop(0, nt8, step, carry)


def _chain_scan(segs, reverse):
    carry = jnp.zeros((1, 128), F32)
    for a, b, a_ref, b_ref, n in segs:
        a2, b2 = _tile_scan(a, b, reverse)
        a_ref[...] = a2
        b_ref[...] = b2
        carry = _carry_scan(a_ref, b_ref, n, reverse, carry)


def _lru_specs(cfg):
    n_lat, n_ctx, cb = cfg.n_lat, cfg.n_ctx, cfg.ctx_blk
    return [pl.BlockSpec((n_lat, 128), lambda hb, e: (e, hb)),
            pl.BlockSpec((n_lat, 128), lambda hb, e: (e, 8 + hb)),
            pl.BlockSpec((n_ctx, 128), lambda hb, e: (cb + e, hb)),
            pl.BlockSpec((n_ctx, 128), lambda hb, e: (cb + e, 8 + hb)),
            pl.BlockSpec((4, 128), lambda hb, e: (0, hb)),
            pl.BlockSpec((1, 128), lambda hb, e: (0, hb)),
            pl.BlockSpec((2, None, 128, 128), lambda hb, e: (0, hb, 0, 0)),
            pl.BlockSpec((2, 128), lambda hb, e: (0, hb)),
            pl.BlockSpec((2, None, 128, 128), lambda hb, e: (0, hb, 0, 0)),
            pl.BlockSpec((2, 128), lambda hb, e: (0, hb)),
            pl.BlockSpec((2, 128), lambda hb, e: (0, hb))]


def _lru_fwd(cfg, p, consts, name):
    n_lat, n_ctx = cfg.n_lat, cfg.n_ctx

    def body(gl_ref, ul_ref, gc_ref, uc_ref, cw_ref, cb_ref, wa_ref, ba_ref, wx_ref, bx_ref, lam_ref,
             zl_ref, zc_ref, al, bl, ac, bc):
        row_l = lax.broadcasted_iota(jnp.int32, (n_lat, 1), 0)
        row_c = lax.broadcasted_iota(jnp.int32, (n_ctx, 1), 0)
        uc_l = _conv(ul_ref[...], cw_ref, cb_ref, row_l)
        uc_c = _conv(uc_ref[...], cw_ref, cb_ref, row_c)
        y_l = y_c = None
        for d in range(2):
            _, gi_l, _, _, a_l, sq_l = _lru_gates(uc_l, d, wa_ref, ba_ref, wx_ref, bx_ref, lam_ref)
            _, gi_c, _, _, a_c, sq_c = _lru_gates(uc_c, d, wa_ref, ba_ref, wx_ref, bx_ref, lam_ref)
            _chain_scan([(a_c, sq_c * (gi_c * uc_c), ac, bc, n_ctx), (a_l, sq_l * (gi_l * uc_l), al, bl, n_lat)],
                        reverse=(d == 1))
            y_l = bl[...] if y_l is None else y_l + bl[...]
            y_c = bc[...] if y_c is None else y_c + bc[...]
        zl_ref[...] = (_gelu(gl_ref[...])[0] * y_l).astype(BF16)
        zc_ref[...] = (_gelu(gc_ref[...])[0] * y_c).astype(BF16)

    return pl.pallas_call(
        body, grid=(8, 2), name=name, in_specs=_lru_specs(cfg),
        out_specs=[pl.BlockSpec((n_lat, 128), lambda hb, e: (e, hb)), pl.BlockSpec((n_ctx, 128), lambda hb, e: (e, hb))],
        out_shape=[jax.ShapeDtypeStruct((cfg.t_lat, D), BF16), jax.ShapeDtypeStruct((cfg.t_ctx, D), BF16)],
        scratch_shapes=[pltpu.VMEM((n_lat, 128), F32)] * 2 + [pltpu.VMEM((n_ctx, 128), F32)] * 2,
        compiler_params=_params(("parallel", "arbitrary")))(p, p, p, p, *consts)


def _lru_bwd(cfg, p, dz, consts, name):
    n_lat, n_ctx, cb = cfg.n_lat, cfg.n_ctx, cfg.ctx_blk

    def body(gl_ref, ul_ref, gc_ref, uc_ref, cw_ref, cb_ref, wa_ref, ba_ref, wx_ref, bx_ref, lam_ref,
             dzl_ref, dzc_ref, dgl_ref, dul_ref, dgc_ref, duc_ref, dwa_ref, dwx_ref, vec_ref,
             al, bl, ac, bc, hl, hc):
        e = pl.program_id(1)

        @pl.when(e == 0)
        def _():
            dwa_ref[...] = jnp.zeros_like(dwa_ref)
            dwx_ref[...] = jnp.zeros_like(dwx_ref)
            vec_ref[...] = jnp.zeros_like(vec_ref)

        row_l = lax.broadcasted_iota(jnp.int32, (n_lat, 1), 0)
        row_c = lax.broadcasted_iota(jnp.int32, (n_ctx, 1), 0)
        u_l, u_c = ul_ref[...], uc_ref[...]
        uc_l = _conv(u_l, cw_ref, cb_ref, row_l)
        uc_c = _conv(u_c, cw_ref, cb_ref, row_c)
        for d in range(2):
            _, gi_l, _, _, a_l, sq_l = _lru_gates(uc_l, d, wa_ref, ba_ref, wx_ref, bx_ref, lam_ref)
            _, gi_c, _, _, a_c, sq_c = _lru_gates(uc_c, d, wa_ref, ba_ref, wx_ref, bx_ref, lam_ref)
            _chain_scan([(a_c, sq_c * (gi_c * uc_c), ac, bc, n_ctx), (a_l, sq_l * (gi_l * uc_l), al, bl, n_lat)],
                        reverse=(d == 1))
            hl[d] = bl[...]
            hc[d] = bc[...]
        gel_l, t_l = _gelu(gl_ref[...])
        gel_c, t_c = _gelu(gc_ref[...])
        dz_l, dz_c = dzl_ref[...], dzc_ref[...]
        dgl_ref[...] = (dz_l * (hl[0] + hl[1]) * _gelu_grad(gl_ref[...], t_l)).astype(BF16)
        dgc_ref[...] = (dz_c * (hc[0] + hc[1]) * _gelu_grad(gc_ref[...], t_c)).astype(BF16)
        dy_l, dy_c = dz_l * gel_l, dz_c * gel_c
        duc_l = jnp.zeros((n_lat, 128), F32)
        duc_c = jnp.zeros((n_ctx, 128), F32)
        for d in range(2):
            r_l, gi_l, sp, dsp, a_l, sq_l = _lru_gates(uc_l, d, wa_ref, ba_ref, wx_ref, bx_ref, lam_ref)
            r_c, gi_c, _, _, a_c, sq_c = _lru_gates(uc_c, d, wa_ref, ba_ref, wx_ref, bx_ref, lam_ref)
            if d == 0:
                an_l = _shift_up(a_l, 1, row_l)
                an_c = jnp.where(row_c < n_ctx - 1, pltpu.roll(a_c, n_ctx - 1, 0), a_l[0:1, :])
            else:
                an_l = _shift_down(a_l, 1, row_l)
                an_c = jnp.where(row_c >= 1, pltpu.roll(a_c, 1, 0), a_l[n_lat - 1:n_lat, :])
            _chain_scan([(an_l, dy_l, al, bl, n_lat), (an_c, dy_c, ac, bc, n_ctx)], reverse=(d == 0))
            dsp_sum = jnp.zeros((1, 128), F32)
            for (dh, h, r, gi, a, sq, uc, seg) in ((bl[...], hl[d], r_l, gi_l, a_l, sq_l, uc_l, "l"),
                                                  (bc[...], hc[d], r_c, gi_c, a_c, sq_c, uc_c, "c")):
                b0 = sq * (gi * uc)
                t1 = dh * sq
                dla = dh * (h - b0) - (dh * gi * uc) * (a * a) / sq
                dzr = (dla * ((-LRU_C) * sp)) * r * (1.0 - r)
                dzi = (t1 * uc) * gi * (1.0 - gi)
                dsp_sum = dsp_sum + jnp.sum(dla * ((-LRU_C) * r), axis=0, keepdims=True)
                dwa_ref[d] += _dot_tn(uc, dzr)
                dwx_ref[d] += _dot_tn(uc, dzi)
                vec_ref[d:d + 1, :] += jnp.sum(dzr, axis=0, keepdims=True)
                vec_ref[2 + d:3 + d, :] += jnp.sum(dzi, axis=0, keepdims=True)
                duc = t1 * gi + _dot_nt(dzr, wa_ref[d]) + _dot_nt(dzi, wx_ref[d])
                if seg == "l":
                    duc_l = duc_l + duc
                else:
                    duc_c = duc_c + duc
            vec_ref[4 + d:5 + d, :] += dsp_sum * dsp
        for duc, u, row, du_ref in ((duc_l, u_l, row_l, dul_ref), (duc_c, u_c, row_c, duc_ref)):
            du_ref[...] = (_shift_up(duc, 1, row) * cw_ref[0:1, :] + duc * cw_ref[1:2, :]
                           + _shift_down(duc, 1, row) * cw_ref[2:3, :]
                           + _shift_down(duc, 2, row) * cw_ref[3:4, :]).astype(BF16)
            vec_ref[6:7, :] += jnp.sum(duc * _shift_down(u, 1, row), axis=0, keepdims=True)
            vec_ref[7:8, :] += jnp.sum(duc * u, axis=0, keepdims=True)
            vec_ref[8:9, :] += jnp.sum(duc * _shift_up(u, 1, row), axis=0, keepdims=True)
            vec_ref[9:10, :] += jnp.sum(duc * _shift_up(u, 2, row), axis=0, keepdims=True)
            vec_ref[10:11, :] += jnp.sum(duc, axis=0, keepdims=True)

    lat = pl.BlockSpec((n_lat, 128), lambda hb, e: (e, hb))
    ctx = pl.BlockSpec((n_ctx, 128), lambda hb, e: (e, hb))
    wspec = pl.BlockSpec((2, None, 128, 128), lambda hb, e: (0, hb, 0, 0))
    sd = jax.ShapeDtypeStruct
    return pl.pallas_call(
        body, grid=(8, 2), name=name,
        in_specs=_lru_specs(cfg) + [pl.BlockSpec((n_lat, 128), lambda hb, e: (e, hb)),
                                    pl.BlockSpec((n_ctx, 128), lambda hb, e: (cb + e, hb))],
        out_specs=[lat, lat, ctx, ctx, wspec, wspec, pl.BlockSpec((None, 16, 128), lambda hb, e: (hb, 0, 0))],
        out_shape=[sd((cfg.t_lat, D), BF16), sd((cfg.t_lat, D), BF16), sd((cfg.t_ctx, D), BF16), sd((cfg.t_ctx, D), BF16),
                   sd((2, 8, 128, 128), F32), sd((2, 8, 128, 128), F32), sd((8, 16, 128), F32)],
        scratch_shapes=[pltpu.VMEM((n_lat, 128), F32)] * 2 + [pltpu.VMEM((n_ctx, 128), F32)] * 2
        + [pltpu.VMEM((2, n_lat, 128), F32), pltpu.VMEM((2, n_ctx, 128), F32)],
        compiler_params=_params(("parallel", "arbitrary")))(p, p, p, p, *consts, dz, dz)


def _position():
    x, y, c = lax.axis_index("x"), lax.axis_index("y"), lax.axis_index("c")
    return x, y, c, 4 * x + 2 * y + c


def _peer(x, y, c, k):
    px = 1 - x if k & 4 else x
    py = 1 - y if k & 2 else y
    pc = 1 - c if k & 1 else c
    return (px, py, pc), 4 * px + 2 * py + pc


def _all_gather(v, name, in_vmem):
    def body(v_ref, o_ref, send_sems, recv_sems, local_sem):
        x, y, c, me = _position()
        mine = pltpu.make_async_copy(v_ref, o_ref.at[me], local_sem)
        mine.start()
        sends = []
        for k in range(1, N_DEV):
            peer, _ = _peer(x, y, c, k)
            cp = pltpu.make_async_remote_copy(src_ref=v_ref, dst_ref=o_ref.at[me], send_sem=send_sems.at[k - 1],
                                              recv_sem=recv_sems.at[k - 1], device_id=peer, device_id_type=MESH)
            cp.start()
            sends.append(cp)
        for k in range(1, N_DEV):
            peer, peer_lin = _peer(x, y, c, k)
            pltpu.make_async_remote_copy(src_ref=v_ref, dst_ref=o_ref.at[peer_lin], send_sem=send_sems.at[k - 1],
                                         recv_sem=recv_sems.at[k - 1], device_id=peer, device_id_type=MESH).wait_recv()
        for cp in sends:
            cp.wait_send()
        mine.wait()

    space = pltpu.VMEM if in_vmem else pl.ANY
    return pl.pallas_call(
        body, name=name,
        in_specs=[pl.BlockSpec(memory_space=space)], out_specs=pl.BlockSpec(memory_space=space),
        out_shape=jax.ShapeDtypeStruct((N_DEV,) + v.shape, v.dtype),
        scratch_shapes=[pltpu.SemaphoreType.DMA((N_DEV - 1,)), pltpu.SemaphoreType.DMA((N_DEV - 1,)),
                        pltpu.SemaphoreType.DMA],
        compiler_params=pltpu.CompilerParams(vmem_limit_bytes=VMEM_LIMIT))(v)


def _gather_group(srcs, name):
    n = len(srcs)

    def body(*refs):
        src_refs, out_refs = refs[:n], refs[n:2 * n]
        send_sems, recv_sems, local_sems = refs[2 * n:]
        x, y, c, me = _position()

        def rows_of(a, who):
            r = src_refs[a].shape[1]
            return out_refs[a].at[:, pl.ds(who * r, r), :]

        mine = [pltpu.make_async_copy(src_refs[a], rows_of(a, me), local_sems.at[a]) for a in range(n)]
        for cp in mine:
            cp.start()
        sends = []
        for k in range(1, N_DEV):
            peer, _ = _peer(x, y, c, k)
            for a in range(n):
                cp = pltpu.make_async_remote_copy(
                    src_ref=src_refs[a], dst_ref=rows_of(a, me), send_sem=send_sems.at[(k - 1) * n + a],
                    recv_sem=recv_sems.at[(k - 1) * n + a], device_id=peer, device_id_type=MESH)
                cp.start()
                sends.append(cp)
        for k in range(1, N_DEV):
            peer, peer_lin = _peer(x, y, c, k)
            for a in range(n):
                pltpu.make_async_remote_copy(
                    src_ref=src_refs[a], dst_ref=rows_of(a, peer_lin), send_sem=send_sems.at[(k - 1) * n + a],
                    recv_sem=recv_sems.at[(k - 1) * n + a], device_id=peer, device_id_type=MESH).wait_recv()
        for cp in sends:
            cp.wait_send()
        for cp in mine:
            cp.wait()

    hbm = pl.BlockSpec(memory_space=pl.ANY)
    return pl.pallas_call(
        body, name=name, in_specs=[hbm] * n, out_specs=[hbm] * n,
        out_shape=[jax.ShapeDtypeStruct((s.shape[0], N_DEV * s.shape[1], s.shape[2]), s.dtype) for s in srcs],
        scratch_shapes=[pltpu.SemaphoreType.DMA(((N_DEV - 1) * n,)), pltpu.SemaphoreType.DMA(((N_DEV - 1) * n,)),
                        pltpu.SemaphoreType.DMA((n,))],
        compiler_params=pltpu.CompilerParams(vmem_limit_bytes=VMEM_LIMIT))(*srcs)


def _exchange_group(grads, name):
    n = len(grads)

    def body(*refs):
        g_refs, out_refs = refs[:n], refs[n:2 * n]
        send_sems, recv_sems, local_sems = refs[2 * n:]
        x, y, c, me = _position()

        def rows_of(a, who):
            r = out_refs[a].shape[2]
            return g_refs[a].at[:, pl.ds(who * r, r), :]

        mine = [pltpu.make_async_copy(rows_of(a, me), out_refs[a].at[me], local_sems.at[a]) for a in range(n)]
        for cp in mine:
            cp.start()
        sends = []
        for k in range(1, N_DEV):
            peer, peer_lin = _peer(x, y, c, k)
            for a in range(n):
                cp = pltpu.make_async_remote_copy(
                    src_ref=rows_of(a, peer_lin), dst_ref=out_refs[a].at[me], send_sem=send_sems.at[(k - 1) * n + a],
                    recv_sem=recv_sems.at[(k - 1) * n + a], device_id=peer, device_id_type=MESH)
                cp.start()
                sends.append(cp)
        for k in range(1, N_DEV):
            peer, peer_lin = _peer(x, y, c, k)
            for a in range(n):
                pltpu.make_async_remote_copy(
                    src_ref=rows_of(a, peer_lin), dst_ref=out_refs[a].at[peer_lin],
                    send_sem=send_sems.at[(k - 1) * n + a], recv_sem=recv_sems.at[(k - 1) * n + a],
                    device_id=peer, device_id_type=MESH).wait_recv()
        for cp in sends:
            cp.wait_send()
        for cp in mine:
            cp.wait()

    hbm = pl.BlockSpec(memory_space=pl.ANY)
    return pl.pallas_call(
        body, name=name, in_specs=[hbm] * n, out_specs=[hbm] * n,
        out_shape=[jax.ShapeDtypeStruct((N_DEV, g.shape[0], g.shape[1] // N_DEV, g.shape[2]), g.dtype) for g in grads],
        scratch_shapes=[pltpu.SemaphoreType.DMA(((N_DEV - 1) * n,)), pltpu.SemaphoreType.DMA(((N_DEV - 1) * n,)),
                        pltpu.SemaphoreType.DMA((n,))],
        compiler_params=pltpu.CompilerParams(vmem_limit_bytes=VMEM_LIMIT))(*grads)


def _sum_blocks(v, name):
    k, rows, cols = v.shape
    tr = rows
    for cand in (rows, 512, 352, 256, 176, 128, 64, 32, 16):
        if rows % cand == 0 and k * cand * cols * v.dtype.itemsize <= 6 * 1024 * 1024:
            tr = cand
            break

    def body(v_ref, o_ref):
        acc = v_ref[0].astype(F32)
        for s in range(1, k):
            acc = acc + v_ref[s].astype(F32)
        o_ref[...] = acc

    return pl.pallas_call(
        body, grid=(rows // tr,), name=name,
        in_specs=[pl.BlockSpec((k, tr, cols), lambda i: (0, i, 0))],
        out_specs=pl.BlockSpec((tr, cols), lambda i: (i, 0)),
        out_shape=jax.ShapeDtypeStruct((rows, cols), F32),
        compiler_params=_params(("parallel",)))(v)


def _adam_math(w, g, m, v):
    m2 = B1 * m + (1.0 - B1) * g
    v2 = B2 * v + (1.0 - B2) * (g * g)
    m_hat = m2 / (1.0 - B1 ** STEP)
    v_hat = v2 / (1.0 - B2 ** STEP)
    return -LR * (m_hat / (jnp.sqrt(v_hat) + EPS) + WD * w), m2, v2


def _adamw(w, g, m, v, name):
    rows, cols = w.shape
    tr = rows
    for cand in (512, 256, 128, 64, 32, 16, 8):
        if rows % cand == 0 and cand * cols * 4 <= 2 * 1024 * 1024:
            tr = cand
            break

    def body(w_ref, g_ref, m_ref, v_ref, d_ref, m2_ref, v2_ref):
        d_ref[...], m2_ref[...], v2_ref[...] = _adam_math(w_ref[...], g_ref[...], m_ref[...], v_ref[...])

    blk = pl.BlockSpec((tr, cols), lambda i: (i, 0))
    return pl.pallas_call(
        body, grid=(rows // tr,), name=name, in_specs=[blk] * 4, out_specs=[blk] * 3,
        out_shape=[jax.ShapeDtypeStruct((rows, cols), F32)] * 3,
        compiler_params=_params(("parallel",)))(w, g, m, v)


def _as2d(a):
    n = a.size
    if n % 1024 == 0:
        return a.reshape(n // 1024, 1024)
    if n % 128 == 0:
        return a.reshape(n // 128, 128)
    return a.reshape(1, n)


def _blocks_to_cols(a):
    b = jnp.moveaxis(a, 0, -2)
    return b.reshape(b.shape[:-2] + (b.shape[-2] * b.shape[-1],))


def _pack_rows(parts):
    padded, offs, r = [], [], 0
    for p in parts:
        pad = (-p.shape[0]) % 8
        padded.append(jnp.pad(p, ((0, pad), (0, 0))) if pad else p)
        offs.append(r)
        r += p.shape[0] + pad
    return jnp.concatenate(padded, axis=0), offs


def _silu(x):
    return x * jax.nn.sigmoid(x)


def kernel(x, c, ctx, c_ctx, w_mod, b_mod, ln_g, ln_b, ffn_w_gate, ffn_w_up, ffn_w_down, mix_ab_w_in, attn_sink, pool_w, pool_scale, mix_ab_w_out, lru_w_in, lru_conv_w, lru_conv_b, lru_wa, lru_ba, lru_wx, lru_bx, lru_lambda, lru_w_out, loss_target, m_c_ctx, m_w_mod, m_b_mod, m_ln_g, m_ln_b, m_ffn_w_gate, m_ffn_w_up, m_ffn_w_down, m_mix_ab_w_in, m_attn_sink, m_pool_w, m_pool_scale, m_mix_ab_w_out, m_lru_w_in, m_lru_conv_w, m_lru_conv_b, m_lru_wa, m_lru_ba, m_lru_wx, m_lru_bx, m_lru_lambda, m_lru_w_out, v_c_ctx, v_w_mod, v_b_mod, v_ln_g, v_ln_b, v_ffn_w_gate, v_ffn_w_up, v_ffn_w_down, v_mix_ab_w_in, v_attn_sink, v_pool_w, v_pool_scale, v_mix_ab_w_out, v_lru_w_in, v_lru_conv_w, v_lru_conv_b, v_lru_wa, v_lru_ba, v_lru_wx, v_lru_bx, v_lru_lambda, v_lru_w_out):
    weights = dict(c_ctx=c_ctx, w_mod=w_mod, b_mod=b_mod, ln_g=ln_g, ln_b=ln_b, ffn_w_gate=ffn_w_gate,
                   ffn_w_up=ffn_w_up, ffn_w_down=ffn_w_down, mix_ab_w_in=mix_ab_w_in, attn_sink=attn_sink,
                   pool_w=pool_w, pool_scale=pool_scale, mix_ab_w_out=mix_ab_w_out, lru_w_in=lru_w_in,
                   lru_conv_w=lru_conv_w, lru_conv_b=lru_conv_b, lru_wa=lru_wa, lru_ba=lru_ba, lru_wx=lru_wx,
                   lru_bx=lru_bx, lru_lambda=lru_lambda, lru_w_out=lru_w_out)
    mom_m = dict(c_ctx=m_c_ctx, w_mod=m_w_mod, b_mod=m_b_mod, ln_g=m_ln_g, ln_b=m_ln_b, ffn_w_gate=m_ffn_w_gate,
                 ffn_w_up=m_ffn_w_up, ffn_w_down=m_ffn_w_down, mix_ab_w_in=m_mix_ab_w_in, attn_sink=m_attn_sink,
                 pool_w=m_pool_w, pool_scale=m_pool_scale, mix_ab_w_out=m_mix_ab_w_out, lru_w_in=m_lru_w_in,
                 lru_conv_w=m_lru_conv_w, lru_conv_b=m_lru_conv_b, lru_wa=m_lru_wa, lru_ba=m_lru_ba, lru_wx=m_lru_wx,
                 lru_bx=m_lru_bx, lru_lambda=m_lru_lambda, lru_w_out=m_lru_w_out)
    mom_v = dict(c_ctx=v_c_ctx, w_mod=v_w_mod, b_mod=v_b_mod, ln_g=v_ln_g, ln_b=v_ln_b, ffn_w_gate=v_ffn_w_gate,
                 ffn_w_up=v_ffn_w_up, ffn_w_down=v_ffn_w_down, mix_ab_w_in=v_mix_ab_w_in, attn_sink=v_attn_sink,
                 pool_w=v_pool_w, pool_scale=v_pool_scale, mix_ab_w_out=v_mix_ab_w_out, lru_w_in=v_lru_w_in,
                 lru_conv_w=v_lru_conv_w, lru_conv_b=v_lru_conv_b, lru_wa=v_lru_wa, lru_ba=v_lru_ba, lru_wx=v_lru_wx,
                 lru_bx=v_lru_bx, lru_lambda=v_lru_lambda, lru_w_out=v_lru_w_out)
    names = list(weights)

    n_lat, n_ctx = x.shape[1], ctx.shape[1]
    cfg = _Cfg(n_lat, n_ctx)
    _, _, _, me = _position()
    mcols = w_mod.shape[2]

    small_names = ["ln_g", "ln_b", "lru_conv_w", "lru_conv_b", "lru_ba", "lru_bx", "lru_lambda"]
    small, small_off = _pack_rows([c.reshape(-1, 128)] + [weights[n].reshape(-1, 128) for n in small_names])
    small_all = _all_gather(small, "gather_small", True)

    def small_full(idx, shp):
        rows = math.prod(shp) // 128
        return _blocks_to_cols(small_all[:, small_off[idx]:small_off[idx] + rows, :].reshape((N_DEV,) + shp))

    c_all = small_all[:, :2 * D // 128, :].reshape(2 * N_DEV, D)
    ln_g_f, ln_b_f = small_full(1, ln_g.shape), small_full(2, ln_b.shape)
    lru_consts = (small_full(3, lru_conv_w.shape)[0], small_full(4, lru_conv_b.shape), lru_wa[0],
                  small_full(5, lru_ba.shape)[0], lru_wx[0], small_full(6, lru_bx.shape)[0],
                  small_full(7, lru_lambda.shape)[0])

    s_rows = jnp.zeros((32, D), F32).at[:16].set(_silu(c_all)).at[16].set(_silu(c_ctx)).astype(BF16)
    mod_mine = jnp.stack([_matmul(s_rows, w_mod[l], "nn", F32, "mod_fwd", bn_cap=1280) for l in range(2)])
    mod_all = _all_gather(mod_mine.reshape(64, mcols), "gather_mod", True).reshape(N_DEV, 2, 32, mcols)
    mod_full = _blocks_to_cols(mod_all) + b_mod[:, None, :]
    ex0 = 2 * me
    mods = []
    for l in range(2):
        rows = jnp.stack([lax.dynamic_index_in_dim(mod_full[l], ex0, 0, False),
                          lax.dynamic_index_in_dim(mod_full[l], ex0 + 1, 0, False), mod_full[l, 16]])
        mods.append(rows.reshape(3, N_MOD, D))

    def t_bf16(w):
        return jnp.swapaxes(w, -1, -2).astype(BF16)

    wf = [[None, None], [None, None]]
    for l in range(2):
        for i in range(2):
            src = jnp.stack([t_bf16(ffn_w_gate[l, i]), t_bf16(ffn_w_up[l, i]), ffn_w_down[l, i].astype(BF16)])
            wf[l][i] = _gather_group([src], "gather_ffn")[0]
    w_ab_in_t, w_ab_out = [a[0] for a in _gather_group(
        [t_bf16(mix_ab_w_in), mix_ab_w_out.astype(BF16)], "gather_mix_ab")]
    w_lru_in_t, w_lru_out = [a[0] for a in _gather_group(
        [t_bf16(lru_w_in), lru_w_out.astype(BF16)], "gather_lru")]

    h0 = jnp.concatenate([x.reshape(cfg.t_lat, D), ctx.reshape(cfg.t_ctx, D)], axis=0)
    cos, sin = _rope_tables(n_lat)
    sink_rows = jnp.broadcast_to(attn_sink[0][:, None], (8, 128)).astype(F32)

    saved = []
    h = h0
    xin = _modulate(cfg, h0, mods[0], 0, 1, "modulate_in")
    for l in range(2):
        st = {"h_in": h, "xin1": xin}
        g1, u1, y1 = _ffn_fwd(xin, wf[l][0], "ffn_fwd")
        h1, xhat1, rstd1, xin2 = _ln_fwd(cfg, h, y1, mods[l], 2, 0.5, ln_g_f[l, 0][None], ln_b_f[l, 0][None],
                                          mods[l], (3, 4), "ln_fwd_a")
        st.update(g1=g1, u1=u1, y1=y1, h1=h1, xhat1=xhat1, rstd1=rstd1, xin2=xin2)
        if l == 0:
            p = _matmul(xin2, w_ab_in_t, "nt", F32, "mix_ab_in")
            att_l, att_c = _attn_fwd(cfg, p, cos, sin, sink_rows, "attn_fwd")
            pool_l = _pool_fwd(p, pool_w[0], pool_scale, n_lat, 0, 2, "pool_fwd_lat")
            pool_c = _pool_fwd(p, pool_w[0], pool_scale, n_ctx, cfg.ctx_blk, 2, "pool_fwd_ctx")
            cat = jnp.concatenate([jnp.concatenate([att_l, pool_l], axis=1),
                                   jnp.concatenate([att_c, pool_c], axis=1)], axis=0)
            y2 = _matmul(cat, w_ab_out, "nn", F32, "mix_ab_out")
        else:
            p = _matmul(xin2, w_lru_in_t, "nt", F32, "lru_in")
            z_l, z_c = _lru_fwd(cfg, p, lru_consts, "lru_fwd")
            cat = jnp.concatenate([z_l, z_c], axis=0)
            y2 = _matmul(cat, w_lru_out, "nn", F32, "lru_out")
        h2, xhat2, rstd2, xin3 = _ln_fwd(cfg, h1, y2, mods[l], 5, 1.0, ln_g_f[l, 1][None], ln_b_f[l, 1][None],
                                          mods[l], (6, 7), "ln_fwd_b")
        g3, u3, y3 = _ffn_fwd(xin3, wf[l][1], "ffn_fwd")
        if l == 0:
            h3, xhat3, rstd3, xin = _ln_fwd(cfg, h2, y3, mods[l], 8, 0.5, ln_g_f[l, 2][None], ln_b_f[l, 2][None],
                                            mods[1], (0, 1), "ln_fwd_a")
        else:
            h3, xhat3, rstd3 = _ln_fwd(cfg, h2, y3, mods[l], 8, 0.5, ln_g_f[l, 2][None], ln_b_f[l, 2][None],
                                       None, None, "ln_fwd_last")
        st.update(p=p, cat=cat, y2=y2, h2=h2, xhat2=xhat2, rstd2=rstd2, xin3=xin3, g3=g3, u3=u3, y3=y3,
                  xhat3=xhat3, rstd3=rstd3)
        saved.append(st)
        h = h3

    dy, loss_tile = _loss(cfg, h, loss_target.reshape(cfg.t_lat, D), "loss")
    loss = lax.psum(loss_tile[0, 0], ("x", "y", "c"))

    grads = {}
    dmod = [None, None]
    recv_ffn = [[None, None], [None, None]]
    dln_g = [[None] * 3, [None] * 3]
    dln_b = [[None] * 3, [None] * 3]

    def ffn_weight_grads(xin_b, dg, du, a_act, dys):
        parts = [_matmul(dg, xin_b, "tn", BF16, "ffn_dw", bm_cap=1408)[None],
                 _matmul(du, xin_b, "tn", BF16, "ffn_dw", bm_cap=1408)[None],
                 _matmul(a_act, dys, "tn", BF16, "ffn_dw", bm_cap=1408)[None]]
        return _exchange_group(parts, "exchange_ffn")

    up = (dy,)
    dmod_next = None
    for l in (1, 0):
        st = saved[l]
        dm = [None] * N_MOD

        def put_stats(stats, gate_idx, nxt):
            dm[gate_idx] = stats[:, 2, :]
            if nxt is not None:
                nxt[0][nxt[1]] = stats[:, 4, :]
                nxt[0][nxt[1] + 1] = stats[:, 3, :]

        dres, dys, stats = _ln_bwd(cfg, up, st["xhat3"], st["rstd3"], st["y3"], mods[l], 8, 0.5,
                                   ln_g_f[l, 2][None], "ln_bwd_fused" if len(up) > 1 else "ln_bwd_last")
        put_stats(stats, 8, None if len(up) == 1 else (dmod_next, 0))
        dln_g[l][2], dln_b[l][2] = stats[:, 0, :].sum(0), stats[:, 1, :].sum(0)
        dg, du, a_act, dxin = _ffn_bwd(dys, st["g3"], st["u3"], wf[l][1], "ffn_bwd")
        recv_ffn[l][1] = ffn_weight_grads(st["xin3"], dg, du, a_act, dys)
        dres, dys, stats = _ln_bwd(cfg, (dres, dxin, st["h2"], mods[l], 7), st["xhat2"], st["rstd2"], st["y2"],
                                   mods[l], 5, 1.0, ln_g_f[l, 1][None], "ln_bwd_fused")
        put_stats(stats, 5, (dm, 6))
        dln_g[l][1], dln_b[l][1] = stats[:, 0, :].sum(0), stats[:, 1, :].sum(0)
        if l == 0:
            dw_out = _matmul(st["cat"], dys, "tn", BF16, "mix_ab_dw_out")
            dcat = _matmul(dys, w_ab_out, "nt", F32, "mix_ab_dcat")
            dq, dk, dv, dqc, dkc, dvc, dsink = _attn_bwd(cfg, st["p"], dcat, cos, sin, sink_rows, "attn_bwd")
            du_l, dpw_l, dps_l = _pool_bwd(st["p"], pool_w[0], pool_scale, dcat, n_lat, 0, 2, "pool_bwd_lat")
            du_c, dpw_c, dps_c = _pool_bwd(st["p"], pool_w[0], pool_scale, dcat, n_ctx, cfg.ctx_blk, 2, "pool_bwd_ctx")
            dp = jnp.concatenate([jnp.concatenate([dq, dk, dv, du_l], axis=1),
                                  jnp.concatenate([dqc, dkc, dvc, du_c], axis=1)], axis=0)
            dw_in_t = _matmul(dp, st["xin2"], "tn", BF16, "mix_ab_dw_in", bm_cap=1280)
            dxin = _matmul(dp, w_ab_in_t, "nn", F32, "mix_ab_dx")
            recv_ab = _exchange_group([dw_in_t[None], dw_out[None], _as2d(dpw_l + dpw_c)[None]],
                                      "exchange_mix_ab")
            grads["attn_sink"] = (dsink[0, :, 0] + dsink[1, :, 0])[None, :]
            grads["pool_scale"] = dps_l + dps_c
        else:
            dw_out = _matmul(st["cat"], dys, "tn", BF16, "lru_dw_out")
            dz = _matmul(dys, w_lru_out, "nt", F32, "lru_dz")
            dgl, dul, dgc, duc, dwa, dwx, vec = _lru_bwd(cfg, st["p"], dz, lru_consts, "lru_bwd")
            dp = jnp.concatenate([jnp.concatenate([dgl, dul], axis=1), jnp.concatenate([dgc, duc], axis=1)], axis=0)
            dw_in_t = _matmul(dp, st["xin2"], "tn", BF16, "lru_dw_in", bm_cap=1024)
            dxin = _matmul(dp, w_lru_in_t, "nn", F32, "lru_dx")
            recv_lru = _exchange_group([dw_in_t[None], dw_out[None], _as2d(dwa)[None], _as2d(dwx)[None]], "exchange_lru")
            vec_t = jnp.moveaxis(vec, 0, 1).reshape(16, D)
            grads["lru_ba"], grads["lru_bx"] = vec_t[0:2], vec_t[2:4]
            grads["lru_lambda"], grads["lru_conv_w"], grads["lru_conv_b"] = vec_t[4:6], vec_t[6:10], vec_t[10:11]
        dres, dys, stats = _ln_bwd(cfg, (dres, dxin, st["h1"], mods[l], 4), st["xhat1"], st["rstd1"], st["y1"],
                                   mods[l], 2, 0.5, ln_g_f[l, 0][None], "ln_bwd_fused")
        put_stats(stats, 2, (dm, 3))
        dln_g[l][0], dln_b[l][0] = stats[:, 0, :].sum(0), stats[:, 1, :].sum(0)
        dg, du, a_act, dxin = _ffn_bwd(dys, st["g1"], st["u1"], wf[l][0], "ffn_bwd")
        recv_ffn[l][0] = ffn_weight_grads(st["xin1"], dg, du, a_act, dys)
        dmod[l] = dm
        dmod_next = dm
        up = (dres, dxin, st["h_in"], mods[l], 1)
    dh0, stats = _modulate_bwd(cfg, up[0], up[1], h0, mods[0], 1, "modulate_bwd")
    dmod[0][0], dmod[0][1] = stats[:, 4, :], stats[:, 3, :]
    grad_x = dh0[:cfg.t_lat].reshape(x.shape)

    dmod_mine = jnp.stack([jnp.stack(dmod[l], axis=1).reshape(3, N_MOD * D) for l in range(2)])
    n_dm = 6 * N_MOD * D // 128
    dmod_all = _all_gather(dmod_mine.reshape(n_dm, 128), "gather_dmod", True)
    dmod_sum = _sum_blocks(dmod_all, "sum_dmod").reshape(2, 3, N_MOD * D)
    dmod_all = dmod_all.reshape(N_DEV, 2, 3, N_MOD * D)
    grads["b_mod"] = dmod_sum[:, 0] + dmod_sum[:, 1] + dmod_sum[:, 2]
    dmod_ex = jnp.moveaxis(dmod_all[:, :, 0:2, :], 1, 0).reshape(2, 2 * N_DEV, N_MOD * D)
    dm_rows = jnp.zeros((2, 32, N_MOD * D), F32).at[:, :16].set(dmod_ex).at[:, 16].set(dmod_sum[:, 2])
    dm_cols = lax.dynamic_slice_in_dim(dm_rows, me * mcols, mcols, axis=2).astype(BF16)
    grads["w_mod"] = jnp.stack([_matmul(s_rows, dm_cols[l], "tn", F32, "mod_dw", bn_cap=1280) for l in range(2)])
    ds_part = None
    for l in range(2):
        part = _matmul(dm_cols[l, 16:32], w_mod[l], "nt", F32, "mod_ds", bk_cap=1280)[0]
        ds_part = part if ds_part is None else ds_part + part

    def shard_sum(recv, name):
        return _sum_blocks(recv.reshape(N_DEV, recv.shape[2], recv.shape[3]), name)

    gate_g = [[None, None], [None, None]]
    up_g = [[None, None], [None, None]]
    down_g = [[None, None], [None, None]]
    for l in range(2):
        for i in range(2):
            gt, ut, dn = [shard_sum(r, "sum_ffn") for r in recv_ffn[l][i]]
            gate_g[l][i], up_g[l][i], down_g[l][i] = gt.T, ut.T, dn
    grads["ffn_w_gate"] = jnp.stack([jnp.stack(gate_g[l]) for l in range(2)])
    grads["ffn_w_up"] = jnp.stack([jnp.stack(up_g[l]) for l in range(2)])
    grads["ffn_w_down"] = jnp.stack([jnp.stack(down_g[l]) for l in range(2)])
    grads["mix_ab_w_in"] = shard_sum(recv_ab[0], "sum_mix_in").T[None]
    grads["mix_ab_w_out"] = shard_sum(recv_ab[1], "sum_mix_out")[None]
    grads["lru_w_in"] = shard_sum(recv_lru[0], "sum_lru_in").T[None]
    grads["lru_w_out"] = shard_sum(recv_lru[1], "sum_lru_out")[None]
    rep_parts = [shard_sum(recv_lru[2], "sum_rep"), shard_sum(recv_lru[3], "sum_rep"), shard_sum(recv_ab[2], "sum_rep")]
    rep_names = ["lru_wa", "lru_wx", "pool_w"]

    dln_g_f = jnp.stack([jnp.stack(dln_g[l]) for l in range(2)])
    dln_b_f = jnp.stack([jnp.stack(dln_b[l]) for l in range(2)])
    sink_pad = jnp.zeros((1, 128), F32).at[0, :8].set(grads["attn_sink"][0])
    part_list = [p_.reshape(-1, 128) for p_ in rep_parts] + [
        dln_g_f.reshape(-1, 128), dln_b_f.reshape(-1, 128), grads["lru_conv_w"].reshape(-1, 128),
        grads["lru_conv_b"].reshape(-1, 128), grads["lru_ba"].reshape(-1, 128), grads["lru_bx"].reshape(-1, 128),
        grads["lru_lambda"].reshape(-1, 128), ds_part.reshape(-1, 128), sink_pad, grads["pool_scale"].reshape(-1, 128)]
    parts, part_off = _pack_rows(part_list)
    parts_all = _all_gather(parts, "gather_partials", True)
    parts_sum = _sum_blocks(parts_all, "sum_partials")

    for i, n in enumerate(rep_names):
        rows = part_list[i].shape[0]
        grads[n] = parts_all[:, part_off[i]:part_off[i] + rows, :].reshape(weights[n].shape)

    def take(idx):
        return parts_sum[part_off[idx]:part_off[idx] + part_list[idx].shape[0]]

    def my_cols(full, shp):
        w = shp[-1]
        return lax.dynamic_slice_in_dim(full, me * w, w, axis=full.ndim - 1)

    grads["ln_g"] = my_cols(take(3).reshape(2, 3, D), ln_g.shape)
    grads["ln_b"] = my_cols(take(4).reshape(2, 3, D), ln_b.shape)
    grads["lru_conv_w"] = my_cols(take(5).reshape(1, 4, D), lru_conv_w.shape)
    grads["lru_conv_b"] = my_cols(take(6).reshape(1, D), lru_conv_b.shape)
    grads["lru_ba"] = my_cols(take(7).reshape(1, 2, D), lru_ba.shape)
    grads["lru_bx"] = my_cols(take(8).reshape(1, 2, D), lru_bx.shape)
    grads["lru_lambda"] = my_cols(take(9).reshape(1, 2, D), lru_lambda.shape)
    sg = jax.nn.sigmoid(c_ctx)
    grads["c_ctx"] = take(10).reshape(D) * (sg * (1.0 + c_ctx * (1.0 - sg)))
    grads["attn_sink"] = take(11)[:, :8]
    grads["pool_scale"] = take(12).reshape(pool_scale.shape)

    delta, new_m, new_v = {}, {}, {}
    for n in names:
        shp = weights[n].shape
        grads[n] = grads[n].reshape(shp)
        d2, m2, v2 = _adamw(_as2d(weights[n]), _as2d(grads[n]), _as2d(mom_m[n]), _as2d(mom_v[n]), "adamw")
        delta[n], new_m[n], new_v[n] = d2.reshape(shp), m2.reshape(shp), v2.reshape(shp)

    return (loss, grad_x, *[grads[n] for n in names], *[delta[n] for n in names],
            *[new_m[n] for n in names], *[new_v[n] for n in names])
```

```python
import functools
import math

import jax
import jax.numpy as jnp
from jax import lax
from jax.experimental import pallas as pl
from jax.experimental.pallas import tpu as pltpu

F32 = jnp.float32
BF16 = jnp.bfloat16
MESH = pl.DeviceIdType.MESH

D = 1024
N_MOD = 9
N_DEV = 8
HEAD_DIM = 64
ATT_HEADS = 8
KV_HEADS = 2
ATT_W = 512
BLK = 128
ATT_SCALE = HEAD_DIM ** -0.5
GRID_W = 64
ROPE_FREQS = HEAD_DIM // 4
ROPE_THETA = 10000.0
POOL_R = (1, 2, 4, 8)
LRU_C = 8.0
LN_EPS = 1e-5
NEG_INF = -1e30
ALPHA = 4.0 ** 0.25
LR, B1, B2, EPS, WD, STEP = 0.001, 0.9, 0.999, 1e-08, 0.01, 10
VMEM_LIMIT = 56 * 1024 * 1024
ROW_TILE = 512


def _params(sem=None):
    if sem is None:
        return pltpu.CompilerParams(vmem_limit_bytes=VMEM_LIMIT)
    return pltpu.CompilerParams(dimension_semantics=sem, vmem_limit_bytes=VMEM_LIMIT)


def _sigmoid(x):
    return 1.0 / (1.0 + jnp.exp(-x))


def _dot(a, b):
    return jnp.dot(a.astype(BF16), b.astype(BF16), preferred_element_type=F32)


def _dot_nt(a, b):
    return lax.dot_general(a.astype(BF16), b.astype(BF16), (((1,), (1,)), ((), ())), preferred_element_type=F32)


def _dot_tn(a, b):
    return lax.dot_general(a.astype(BF16), b.astype(BF16), (((0,), (0,)), ((), ())), preferred_element_type=F32)


def _pick(n, cap):
    best = None
    for m in range(128, min(n, cap) + 1, 128):
        if n % m == 0:
            best = m
    return n if best is None else best


def _chunks(width, step=256):
    out, c = [], 0
    while c < width:
        w = min(step, width - c)
        out.append((c, w))
        c += w
    return out


class _Cfg:
    def __init__(self, n_lat, n_ctx):
        self.n_lat, self.n_ctx = n_lat, n_ctx
        self.t_lat, self.t_ctx = 2 * n_lat, 2 * n_ctx
        self.T = self.t_lat + self.t_ctx
        self.tm = min(ROW_TILE, self.t_ctx)
        assert n_lat % self.tm == 0 and self.t_ctx % self.tm == 0 and n_lat >= 3 * BLK and n_ctx % BLK == 0
        self.nt = self.T // self.tm
        self.nlt = n_lat // self.tm
        self.ctx_blk = self.t_lat // n_ctx

    def seg(self, i):
        return jnp.minimum(i // self.nlt, 2)

    def first_of_seg(self, i):
        return jnp.where(i < 2 * self.nlt, i % self.nlt == 0, i == 2 * self.nlt)


def _modulate(cfg, h, mod, shift_idx, scale_idx, name):
    tm = cfg.tm

    def body(h_ref, mod_ref, o_ref):
        sh = mod_ref[shift_idx:shift_idx + 1, :]
        sc = mod_ref[scale_idx:scale_idx + 1, :]
        o_ref[...] = (h_ref[...] * (1.0 + sc) + sh).astype(BF16)

    return pl.pallas_call(
        body, grid=(cfg.nt,), name=name,
        in_specs=[pl.BlockSpec((tm, D), lambda i: (i, 0)),
                  pl.BlockSpec((None, N_MOD, D), lambda i: (cfg.seg(i), 0, 0))],
        out_specs=pl.BlockSpec((tm, D), lambda i: (i, 0)),
        out_shape=jax.ShapeDtypeStruct((cfg.T, D), BF16),
        compiler_params=_params(("parallel",)),
    )(h, mod)


def _ln_fwd(cfg, h, y, mod, gate_idx, coef, lng, lnb, mod_next, next_idx, name):
    tm = cfg.tm
    has_next = next_idx is not None

    def body(*refs):
        if has_next:
            h_ref, y_ref, mod_ref, g_ref, b_ref, modn_ref, hn_ref, xhat_ref, rstd_ref, xin_ref = refs
        else:
            h_ref, y_ref, mod_ref, g_ref, b_ref, hn_ref, xhat_ref, rstd_ref = refs
        gate = mod_ref[gate_idx:gate_idx + 1, :]
        z = ALPHA * h_ref[...] + (coef * gate) * y_ref[...]
        mu = jnp.mean(z, axis=-1, keepdims=True)
        zc = z - mu
        var = jnp.mean(zc * zc, axis=-1, keepdims=True)
        rstd = lax.rsqrt(var + LN_EPS)
        xhat = zc * rstd
        hn = xhat * g_ref[...] + b_ref[...]
        hn_ref[...] = hn
        xhat_ref[...] = xhat
        rstd_ref[...] = rstd
        if has_next:
            sh = modn_ref[next_idx[0]:next_idx[0] + 1, :]
            sc = modn_ref[next_idx[1]:next_idx[1] + 1, :]
            xin_ref[...] = (hn * (1.0 + sc) + sh).astype(BF16)

    row = pl.BlockSpec((tm, D), lambda i: (i, 0))
    modspec = pl.BlockSpec((None, N_MOD, D), lambda i: (cfg.seg(i), 0, 0))
    vec = pl.BlockSpec((1, D), lambda i: (0, 0))
    in_specs = [row, row, modspec, vec, vec]
    args = [h, y, mod, lng, lnb]
    out_specs = [row, row, pl.BlockSpec((tm, 1), lambda i: (i, 0))]
    out_shape = [jax.ShapeDtypeStruct((cfg.T, D), F32), jax.ShapeDtypeStruct((cfg.T, D), F32),
                 jax.ShapeDtypeStruct((cfg.T, 1), F32)]
    if has_next:
        in_specs.append(modspec)
        args.append(mod_next)
        out_specs.append(row)
        out_shape.append(jax.ShapeDtypeStruct((cfg.T, D), BF16))
    return pl.pallas_call(body, grid=(cfg.nt,), name=name, in_specs=in_specs, out_specs=out_specs,
                          out_shape=out_shape, compiler_params=_params(("parallel",)))(*args)


def _ln_bwd(cfg, up, xhat, rstd, y, mod, gate_idx, coef, lng, name):
    tm = cfg.tm
    fused = len(up) > 1
    scale_next = up[4] if fused else None

    def body(*refs):
        if fused:
            dres_n, dxin_n, hn_ref, modn_ref, xhat_ref, rstd_ref, y_ref, mod_ref, g_ref, dres_ref, dys_ref, st_ref = refs
        else:
            dhn_ref, xhat_ref, rstd_ref, y_ref, mod_ref, g_ref, dres_ref, dys_ref, st_ref = refs
        i = pl.program_id(0)

        @pl.when(cfg.first_of_seg(i))
        def _():
            st_ref[...] = jnp.zeros_like(st_ref)

        if fused:
            dxin = dxin_n[...]
            sc = modn_ref[scale_next:scale_next + 1, :]
            dhn = dres_n[...] + dxin * (1.0 + sc)
            st_ref[3:4, :] += jnp.sum(dxin * hn_ref[...], axis=0, keepdims=True)
            st_ref[4:5, :] += jnp.sum(dxin, axis=0, keepdims=True)
        else:
            dhn = dhn_ref[...]
        xhat = xhat_ref[...]
        gdh = dhn * g_ref[...]
        m1 = jnp.mean(gdh, axis=-1, keepdims=True)
        m2 = jnp.mean(gdh * xhat, axis=-1, keepdims=True)
        dz = rstd_ref[...] * (gdh - m1 - xhat * m2)
        gate = mod_ref[gate_idx:gate_idx + 1, :]
        dres_ref[...] = ALPHA * dz
        dys_ref[...] = ((coef * gate) * dz).astype(BF16)
        st_ref[0:1, :] += jnp.sum(dhn * xhat, axis=0, keepdims=True)
        st_ref[1:2, :] += jnp.sum(dhn, axis=0, keepdims=True)
        st_ref[2:3, :] += jnp.sum((coef * dz) * y_ref[...], axis=0, keepdims=True)

    row = pl.BlockSpec((tm, D), lambda i: (i, 0))
    modspec = pl.BlockSpec((None, N_MOD, D), lambda i: (cfg.seg(i), 0, 0))
    vec = pl.BlockSpec((1, D), lambda i: (0, 0))
    col = pl.BlockSpec((tm, 1), lambda i: (i, 0))
    if fused:
        in_specs = [row, row, row, modspec, row, col, row, modspec, vec]
        args = [up[0], up[1], up[2], up[3], xhat, rstd, y, mod, lng]
    else:
        in_specs = [row, row, col, row, modspec, vec]
        args = [up[0], xhat, rstd, y, mod, lng]
    return pl.pallas_call(
        body, grid=(cfg.nt,), name=name, in_specs=in_specs,
        out_specs=[row, row, pl.BlockSpec((None, 8, D), lambda i: (cfg.seg(i), 0, 0))],
        out_shape=[jax.ShapeDtypeStruct((cfg.T, D), F32), jax.ShapeDtypeStruct((cfg.T, D), BF16),
                   jax.ShapeDtypeStruct((3, 8, D), F32)],
        compiler_params=_params(("arbitrary",)))(*args)


def _modulate_bwd(cfg, dres, dxin, h, mod, scale_idx, name):
    tm = cfg.tm

    def body(dres_ref, dxin_ref, h_ref, mod_ref, dh_ref, st_ref):
        i = pl.program_id(0)

        @pl.when(cfg.first_of_seg(i))
        def _():
            st_ref[...] = jnp.zeros_like(st_ref)

        dxin = dxin_ref[...]
        sc = mod_ref[scale_idx:scale_idx + 1, :]
        dh_ref[...] = dres_ref[...] + dxin * (1.0 + sc)
        st_ref[3:4, :] += jnp.sum(dxin * h_ref[...], axis=0, keepdims=True)
        st_ref[4:5, :] += jnp.sum(dxin, axis=0, keepdims=True)

    row = pl.BlockSpec((tm, D), lambda i: (i, 0))
    return pl.pallas_call(
        body, grid=(cfg.nt,), name=name,
        in_specs=[row, row, row, pl.BlockSpec((None, N_MOD, D), lambda i: (cfg.seg(i), 0, 0))],
        out_specs=[row, pl.BlockSpec((None, 8, D), lambda i: (cfg.seg(i), 0, 0))],
        out_shape=[jax.ShapeDtypeStruct((cfg.T, D), F32), jax.ShapeDtypeStruct((3, 8, D), F32)],
        compiler_params=_params(("arbitrary",)))(dres, dxin, h, mod)


def _loss(cfg, h, target, name):
    tm = cfg.tm
    n_lt = 2 * cfg.nlt

    def body(h_ref, t_ref, dy_ref, l_ref):
        i = pl.program_id(0)

        @pl.when(i == 0)
        def _():
            l_ref[...] = jnp.zeros_like(l_ref)

        @pl.when(i < n_lt)
        def _():
            err = h_ref[...] - t_ref[...]
            dy_ref[...] = err * (1.0 / D)
            part = jnp.sum(jnp.sum(err * err, axis=1, keepdims=True), axis=0, keepdims=True) * (0.5 / D)
            l_ref[...] += jnp.broadcast_to(part, l_ref.shape)

        @pl.when(i >= n_lt)
        def _():
            dy_ref[...] = jnp.zeros_like(dy_ref)

    return pl.pallas_call(
        body, grid=(cfg.nt,), name=name,
        in_specs=[pl.BlockSpec((tm, D), lambda i: (i, 0)),
                  pl.BlockSpec((tm, D), lambda i: (jnp.minimum(i, n_lt - 1), 0))],
        out_specs=[pl.BlockSpec((tm, D), lambda i: (i, 0)), pl.BlockSpec((8, 128), lambda i: (0, 0))],
        out_shape=[jax.ShapeDtypeStruct((cfg.T, D), F32), jax.ShapeDtypeStruct((8, 128), F32)],
        compiler_params=_params(("arbitrary",)))(h, target)


def _matmul(a, b, mode, out_dtype, name, bm_cap=512, bn_cap=1408, bk_cap=1024):
    if mode == "nn":
        (M, K), N = a.shape, b.shape[1]
    elif mode == "nt":
        (M, K), N = a.shape, b.shape[0]
    else:
        (K, M), N = a.shape, b.shape[1]
    bm, bn, bk = _pick(M, bm_cap), _pick(N, bn_cap), _pick(K, bk_cap)
    nk = K // bk

    def body(a_ref, b_ref, o_ref, acc_ref):
        k = pl.program_id(2)
        if mode == "nn":
            part = _dot(a_ref[...], b_ref[...])
        elif mode == "nt":
            part = _dot_nt(a_ref[...], b_ref[...])
        else:
            part = _dot_tn(a_ref[...], b_ref[...])

        @pl.when(k == 0)
        def _():
            acc_ref[...] = part

        @pl.when(k > 0)
        def _():
            acc_ref[...] += part

        @pl.when(k == nk - 1)
        def _():
            o_ref[...] = acc_ref[...].astype(out_dtype)

    if mode == "nn":
        a_spec = pl.BlockSpec((bm, bk), lambda i, j, k: (i, k))
        b_spec = pl.BlockSpec((bk, bn), lambda i, j, k: (k, j))
    elif mode == "nt":
        a_spec = pl.BlockSpec((bm, bk), lambda i, j, k: (i, k))
        b_spec = pl.BlockSpec((bn, bk), lambda i, j, k: (j, k))
    else:
        a_spec = pl.BlockSpec((bk, bm), lambda i, j, k: (k, i))
        b_spec = pl.BlockSpec((bk, bn), lambda i, j, k: (k, j))
    return pl.pallas_call(
        body, grid=(M // bm, N // bn, nk), name=name, in_specs=[a_spec, b_spec],
        out_specs=pl.BlockSpec((bm, bn), lambda i, j, k: (i, j)),
        out_shape=jax.ShapeDtypeStruct((M, N), out_dtype),
        scratch_shapes=[pltpu.VMEM((bm, bn), F32)],
        compiler_params=_params(("parallel", "parallel", "arbitrary")))(a, b)


def _ffn_tile(T, cap):
    best = 256
    for t in range(256, cap + 1, 256):
        if T % t == 0:
            best = t
    return best


def _ffn_fwd(xin, wf, name):
    T = xin.shape[0]
    F = wf.shape[1]
    tm, tf = _ffn_tile(T, 768), F // 2
    assert tf % 128 == 0 and T % tm == 0

    def body(x_ref, wg_ref, wu_ref, wd_ref, g_ref, u_ref, y_ref):
        j = pl.program_id(1)
        x = x_ref[...]
        acc = None
        for c0, cw in _chunks(tf):
            g = _dot_nt(x, wg_ref[c0:c0 + cw, :])
            u = _dot_nt(x, wu_ref[c0:c0 + cw, :])
            g_ref[:, c0:c0 + cw] = g.astype(BF16)
            u_ref[:, c0:c0 + cw] = u.astype(BF16)
            part = _dot(g * _sigmoid(g) * u, wd_ref[c0:c0 + cw, :])
            acc = part if acc is None else acc + part

        @pl.when(j == 0)
        def _():
            y_ref[...] = acc

        @pl.when(j > 0)
        def _():
            y_ref[...] += acc

    return pl.pallas_call(
        body, grid=(T // tm, 2), name=name,
        in_specs=[pl.BlockSpec((tm, D), lambda i, j: (i, 0)),
                  pl.BlockSpec((None, tf, D), lambda i, j: (0, j, 0)),
                  pl.BlockSpec((None, tf, D), lambda i, j: (1, j, 0)),
                  pl.BlockSpec((None, tf, D), lambda i, j: (2, j, 0))],
        out_specs=[pl.BlockSpec((tm, tf), lambda i, j: (i, j)),
                   pl.BlockSpec((tm, tf), lambda i, j: (i, j)),
                   pl.BlockSpec((tm, D), lambda i, j: (i, 0))],
        out_shape=[jax.ShapeDtypeStruct((T, F), BF16), jax.ShapeDtypeStruct((T, F), BF16),
                   jax.ShapeDtypeStruct((T, D), F32)],
        compiler_params=_params(("parallel", "arbitrary")))(xin, wf, wf, wf)


def _ffn_bwd(dys, g, u, wf, name):
    T = dys.shape[0]
    F = wf.shape[1]
    tm, tf = _ffn_tile(T, 512), F // 2

    def body(dy_ref, g_ref, u_ref, wg_ref, wu_ref, wd_ref, dg_ref, du_ref, a_ref, dx_ref):
        j = pl.program_id(1)
        dy = dy_ref[...]
        acc = None
        for c0, cw in _chunks(tf):
            gg = g_ref[:, c0:c0 + cw].astype(F32)
            uu = u_ref[:, c0:c0 + cw].astype(F32)
            da = _dot_nt(dy, wd_ref[c0:c0 + cw, :])
            s = _sigmoid(gg)
            silu = gg * s
            a_ref[:, c0:c0 + cw] = (silu * uu).astype(BF16)
            du = (da * silu).astype(BF16)
            dg = (da * uu * (s * (1.0 + gg * (1.0 - s)))).astype(BF16)
            du_ref[:, c0:c0 + cw] = du
            dg_ref[:, c0:c0 + cw] = dg
            part = _dot(dg, wg_ref[c0:c0 + cw, :]) + _dot(du, wu_ref[c0:c0 + cw, :])
            acc = part if acc is None else acc + part

        @pl.when(j == 0)
        def _():
            dx_ref[...] = acc

        @pl.when(j > 0)
        def _():
            dx_ref[...] += acc

    blk = pl.BlockSpec((tm, tf), lambda i, j: (i, j))
    return pl.pallas_call(
        body, grid=(T // tm, 2), name=name,
        in_specs=[pl.BlockSpec((tm, D), lambda i, j: (i, 0)), blk, blk,
                  pl.BlockSpec((None, tf, D), lambda i, j: (0, j, 0)),
                  pl.BlockSpec((None, tf, D), lambda i, j: (1, j, 0)),
                  pl.BlockSpec((None, tf, D), lambda i, j: (2, j, 0))],
        out_specs=[blk, blk, blk, pl.BlockSpec((tm, D), lambda i, j: (i, 0))],
        out_shape=[jax.ShapeDtypeStruct((T, F), BF16), jax.ShapeDtypeStruct((T, F), BF16),
                   jax.ShapeDtypeStruct((T, F), BF16), jax.ShapeDtypeStruct((T, D), F32)],
        compiler_params=_params(("parallel", "arbitrary")))(dys, g, u, wf, wf, wf)


def _swap_halves(x):
    w = x.shape[1]
    lane = lax.broadcasted_iota(jnp.int32, (1, w), 1)
    return jnp.where((lane & 63) < 32, pltpu.roll(x, w - 32, 1), pltpu.roll(x, 32, 1))


def _rope(x, cos, sin):
    return x * cos + _swap_halves(x) * sin


def _rope_t(dy, cos, sin):
    return dy * cos + _swap_halves(dy * sin)


def _rope_tables(n_lat):
    rows = n_lat // GRID_W
    row = jnp.repeat(jnp.arange(rows, dtype=F32), GRID_W)
    col = jnp.tile(jnp.arange(GRID_W, dtype=F32), rows)
    inv = ROPE_THETA ** (-jnp.arange(ROPE_FREQS, dtype=F32) / ROPE_FREQS)
    ang = jnp.concatenate([row[:, None] * inv, col[:, None] * inv], axis=-1)
    cs, sn = jnp.cos(ang), jnp.sin(ang)
    cos = jnp.concatenate([cs, cs, cs, cs], axis=-1)
    sin = jnp.concatenate([-sn, sn, -sn, sn], axis=-1)
    return cos, sin


def _attn_specs(cfg):
    n_lat, n_ctx, cb = cfg.n_lat, cfg.n_ctx, cfg.ctx_blk
    return [pl.BlockSpec((n_lat, ATT_W), lambda e: (e, 0)),
            pl.BlockSpec((n_lat, 128), lambda e: (e, 4)),
            pl.BlockSpec((n_lat, 128), lambda e: (e, 5)),
            pl.BlockSpec((n_ctx, ATT_W), lambda e: (cb + e, 0)),
            pl.BlockSpec((n_ctx, 128), lambda e: (cb + e, 4)),
            pl.BlockSpec((n_ctx, 128), lambda e: (cb + e, 5)),
            pl.BlockSpec((n_lat, 128), lambda e: (0, 0)),
            pl.BlockSpec((n_lat, 128), lambda e: (0, 0)),
            pl.BlockSpec((8, 128), lambda e: (0, 0))]


def _attn_prepare(kh, kl, vl, kc, vc, ka, kb, va, vb, kca, kcb, vca, vcb):
    lane = lax.broadcasted_iota(jnp.int32, (1, 128), 1)
    own = (lane < 64) if kh == 0 else (lane >= 64)

    def split(x, ra, rb):
        mine = jnp.where(own, x, 0.0)
        other = pltpu.roll(mine, 64, 1)
        a, b = (mine, other) if kh == 0 else (other, mine)
        ra[...] = a.astype(BF16)
        rb[...] = b.astype(BF16)

    split(kl, ka, kb)
    split(vl, va, vb)
    split(kc, kca, kcb)
    split(vc, vca, vcb)


def _softmax_parts(s_list, sk):
    m = sk
    for s in s_list:
        m = jnp.maximum(m, jnp.max(s, axis=1, keepdims=True))
    es = [jnp.exp(s - m) for s in s_list]
    esk = jnp.exp(sk - m)
    den = esk
    for e in es:
        den = den + jnp.sum(e, axis=1, keepdims=True)
    inv = 1.0 / den
    return [e * inv for e in es], esk * inv


def _window(cfg, n):
    r0 = pl.multiple_of(n * BLK, BLK)
    start = pl.multiple_of(jnp.clip((n - 1) * BLK, 0, cfg.n_lat - 3 * BLK), BLK)
    qpos = r0 + lax.broadcasted_iota(jnp.int32, (BLK, 1), 0)
    kpos = start + lax.broadcasted_iota(jnp.int32, (1, 3 * BLK), 1)
    valid = jnp.abs(qpos - kpos) <= BLK
    return r0, start, valid


def _attn_fwd(cfg, p, cos, sin, sink_rows, name):
    n_lat, n_ctx = cfg.n_lat, cfg.n_ctx

    def body(q_ref, k_ref, v_ref, qc_ref, kc_ref, vc_ref, cos_ref, sin_ref, sink_ref, o_ref, oc_ref,
             qr, ka, kb, va, vb, kca, kcb, vca, vcb):
        cos_t, sin_t = cos_ref[...], sin_ref[...]
        for gq in range(4):
            qr[:, gq * 128:(gq + 1) * 128] = _rope(q_ref[:, gq * 128:(gq + 1) * 128], cos_t, sin_t).astype(BF16)
        kl = _rope(k_ref[...], cos_t, sin_t)
        for kh in range(KV_HEADS):
            _attn_prepare(kh, kl, v_ref[...], kc_ref[...], vc_ref[...], ka, kb, va, vb, kca, kcb, vca, vcb)

            def lat_block(n, carry):
                r0, start, valid = _window(cfg, n)
                win = pl.ds(start, 3 * BLK)
                for pr in range(2):
                    lanes = slice((kh * 2 + pr) * 128, (kh * 2 + pr + 1) * 128)
                    qp = qr[pl.ds(r0, BLK), lanes]
                    o = None
                    for half, (kw, kcx, vw, vcx) in enumerate(((ka, kca, va, vca), (kb, kcb, vb, vcb))):
                        head = kh * 4 + pr * 2 + half
                        s_w = jnp.where(valid, _dot_nt(qp, kw[win, :]) * ATT_SCALE, NEG_INF)
                        s_c = _dot_nt(qp, kcx[...]) * ATT_SCALE
                        (p_w, p_c), _ = _softmax_parts([s_w, s_c], sink_ref[head:head + 1, 0:1])
                        part = _dot(p_w, vw[win, :]) + _dot(p_c, vcx[...])
                        o = part if o is None else o + part
                    o_ref[pl.ds(r0, BLK), lanes] = o.astype(BF16)
                return carry

            lax.fori_loop(0, n_lat // BLK, lat_block, 0)
            for n in range(n_ctx // BLK):
                rows = slice(n * BLK, (n + 1) * BLK)
                for pr in range(2):
                    lanes = slice((kh * 2 + pr) * 128, (kh * 2 + pr + 1) * 128)
                    qp = qc_ref[rows, lanes]
                    o = None
                    for half, (kcx, vcx) in enumerate(((kca, vca), (kcb, vcb))):
                        head = kh * 4 + pr * 2 + half
                        s_c = _dot_nt(qp, kcx[...]) * ATT_SCALE
                        (p_c,), _ = _softmax_parts([s_c], sink_ref[head:head + 1, 0:1])
                        part = _dot(p_c, vcx[...])
                        o = part if o is None else o + part
                    oc_ref[rows, lanes] = o.astype(BF16)

    return pl.pallas_call(
        body, grid=(2,), name=name, in_specs=_attn_specs(cfg),
        out_specs=[pl.BlockSpec((n_lat, ATT_W), lambda e: (e, 0)), pl.BlockSpec((n_ctx, ATT_W), lambda e: (e, 0))],
        out_shape=[jax.ShapeDtypeStruct((cfg.t_lat, ATT_W), BF16), jax.ShapeDtypeStruct((cfg.t_ctx, ATT_W), BF16)],
        scratch_shapes=[pltpu.VMEM((n_lat, ATT_W), BF16)] + [pltpu.VMEM((n_lat, 128), BF16)] * 4
        + [pltpu.VMEM((n_ctx, 128), BF16)] * 4,
        compiler_params=_params(("parallel",)))(p, p, p, p, p, p, cos, sin, sink_rows)


def _attn_bwd(cfg, p, dcat, cos, sin, sink_rows, name):
    n_lat, n_ctx, cb = cfg.n_lat, cfg.n_ctx, cfg.ctx_blk

    def body(q_ref, k_ref, v_ref, qc_ref, kc_ref, vc_ref, cos_ref, sin_ref, sink_ref, do_ref, doc_ref,
             dq_ref, dk_ref, dv_ref, dqc_ref, dkc_ref, dvc_ref, dsink_ref,
             qr, ka, kb, va, vb, kca, kcb, vca, vcb, dqs, dka, dva, dkca, dvca):
        cos_t, sin_t = cos_ref[...], sin_ref[...]
        lane = lax.broadcasted_iota(jnp.int32, (1, 128), 1)
        lo = lane < 64
        for gq in range(4):
            qr[:, gq * 128:(gq + 1) * 128] = _rope(q_ref[:, gq * 128:(gq + 1) * 128], cos_t, sin_t).astype(BF16)
        kl = _rope(k_ref[...], cos_t, sin_t)
        dsink_ref[...] = jnp.zeros_like(dsink_ref)
        dka[...] = jnp.zeros_like(dka)
        dva[...] = jnp.zeros_like(dva)
        dkca[...] = jnp.zeros_like(dkca)
        dvca[...] = jnp.zeros_like(dvca)

        def halves(x):
            return jnp.where(lo, x, 0).astype(BF16), jnp.where(lo, 0, x).astype(BF16)

        for kh in range(KV_HEADS):
            _attn_prepare(kh, kl, v_ref[...], kc_ref[...], vc_ref[...], ka, kb, va, vb, kca, kcb, vca, vcb)

            def one_head(head, qp, q_half, do_p, do_half, kw, kcx, vw, vcx, win, valid):
                sk = sink_ref[head:head + 1, 0:1]
                s_list = [_dot_nt(qp, kcx[...]) * ATT_SCALE]
                if win is not None:
                    s_list.insert(0, jnp.where(valid, _dot_nt(qp, kw[win, :]) * ATT_SCALE, NEG_INF))
                probs, p_sink = _softmax_parts(s_list, sk)
                vals = [vcx[...]] if win is None else [vw[win, :], vcx[...]]
                dps = [_dot_nt(do_p, vv) for vv in vals]
                dr = None
                for pp, dp in zip(probs, dps):
                    t = jnp.sum(pp * dp, axis=1, keepdims=True)
                    dr = t if dr is None else dr + t
                dss = [(pp * (dp - dr) * ATT_SCALE).astype(BF16) for pp, dp in zip(probs, dps)]
                dsink_ref[head:head + 1, :] += jnp.broadcast_to(
                    jnp.sum(-p_sink * dr, axis=0, keepdims=True), (1, 128))
                p_c, ds_c = probs[-1], dss[-1]
                dq = _dot(ds_c, kcx[...])
                dkca[kh] += _dot_tn(ds_c, q_half)
                dvca[kh] += _dot_tn(p_c, do_half)
                if win is not None:
                    dq = dq + _dot(dss[0], kw[win, :])
                    dka[kh, win, :] += _dot_tn(dss[0], q_half)
                    dva[kh, win, :] += _dot_tn(probs[0], do_half)
                return dq

            def lat_block(n, carry):
                r0, start, valid = _window(cfg, n)
                win = pl.ds(start, 3 * BLK)
                for pr in range(2):
                    lanes = slice((kh * 2 + pr) * 128, (kh * 2 + pr + 1) * 128)
                    qp = qr[pl.ds(r0, BLK), lanes]
                    do_p = do_ref[pl.ds(r0, BLK), lanes]
                    q_h, do_h = halves(qp), halves(do_p)
                    dq = None
                    for half, (kw, kcx, vw, vcx) in enumerate(((ka, kca, va, vca), (kb, kcb, vb, vcb))):
                        part = one_head(kh * 4 + pr * 2 + half, qp, q_h[half], do_p, do_h[half],
                                        kw, kcx, vw, vcx, win, valid)
                        dq = part if dq is None else dq + part
                    dqs[pl.ds(r0, BLK), lanes] = dq
                return carry

            lax.fori_loop(0, n_lat // BLK, lat_block, 0)
            for n in range(n_ctx // BLK):
                rows = slice(n * BLK, (n + 1) * BLK)
                for pr in range(2):
                    lanes = slice((kh * 2 + pr) * 128, (kh * 2 + pr + 1) * 128)
                    qp = qc_ref[rows, lanes].astype(BF16)
                    do_p = doc_ref[rows, lanes]
                    q_h, do_h = halves(qp), halves(do_p)
                    dq = None
                    for half, (kcx, vcx) in enumerate(((kca, vca), (kcb, vcb))):
                        part = one_head(kh * 4 + pr * 2 + half, qp, q_h[half], do_p, do_h[half],
                                        None, kcx, None, vcx, None, None)
                        dq = part if dq is None else dq + part
                    dqc_ref[rows, lanes] = dq.astype(BF16)

        def fold(acc):
            r0 = acc[0] + pltpu.roll(acc[0], 64, 1)
            r1 = acc[1] + pltpu.roll(acc[1], 64, 1)
            return jnp.where(lo, r0, r1)

        for gq in range(4):
            sl = slice(gq * 128, (gq + 1) * 128)
            dq_ref[:, sl] = _rope_t(dqs[:, sl], cos_t, sin_t).astype(BF16)
        dk_ref[...] = _rope_t(fold(dka), cos_t, sin_t).astype(BF16)
        dv_ref[...] = fold(dva).astype(BF16)
        dkc_ref[...] = fold(dkca).astype(BF16)
        dvc_ref[...] = fold(dvca).astype(BF16)

    lat = lambda w: pl.BlockSpec((n_lat, w), lambda e: (e, 0))
    ctx = lambda w: pl.BlockSpec((n_ctx, w), lambda e: (e, 0))
    sd = jax.ShapeDtypeStruct
    return pl.pallas_call(
        body, grid=(2,), name=name,
        in_specs=_attn_specs(cfg) + [pl.BlockSpec((n_lat, ATT_W), lambda e: (e, 0)),
                                     pl.BlockSpec((n_ctx, ATT_W), lambda e: (cb + e, 0))],
        out_specs=[lat(ATT_W), lat(128), lat(128), ctx(ATT_W), ctx(128), ctx(128),
                   pl.BlockSpec((None, 8, 128), lambda e: (e, 0, 0))],
        out_shape=[sd((cfg.t_lat, ATT_W), BF16), sd((cfg.t_lat, 128), BF16), sd((cfg.t_lat, 128), BF16),
                   sd((cfg.t_ctx, ATT_W), BF16), sd((cfg.t_ctx, 128), BF16), sd((cfg.t_ctx, 128), BF16),
                   sd((2, 8, 128), F32)],
        scratch_shapes=[pltpu.VMEM((n_lat, ATT_W), BF16)] + [pltpu.VMEM((n_lat, 128), BF16)] * 4
        + [pltpu.VMEM((n_ctx, 128), BF16)] * 4
        + [pltpu.VMEM((n_lat, ATT_W), F32), pltpu.VMEM((2, n_lat, 128), F32), pltpu.VMEM((2, n_lat, 128), F32),
           pltpu.VMEM((2, n_ctx, 128), F32), pltpu.VMEM((2, n_ctx, 128), F32)],
        compiler_params=_params(("parallel",)))(p, p, p, p, p, p, cos, sin, sink_rows, dcat, dcat)


def _shift_down(x, k, row):
    return jnp.where(row >= k, pltpu.roll(x, k, 0), 0.0)


def _shift_up(x, k, row):
    n = x.shape[0]
    return jnp.where(row < n - k, pltpu.roll(x, n - k, 0), 0.0)


def _window_sum(x, r, row):
    below, above, k = x, x, 1
    while k < r:
        below = below + _shift_down(below, k, row)
        above = above + _shift_up(above, k, row)
        k *= 2
    return below + _shift_down(x, r, row) + _shift_up(above, 1, row)


def _inv_count(r, row, n):
    cnt = jnp.minimum(row + r, n - 1) + 1 - jnp.maximum(row - r, 0)
    return 1.0 / cnt.astype(F32)


def _pool_fwd(p, w, scale, n, blk0, n_seg, name):
    def body(u0, u1, u2, u3, w_ref, sc_ref, o_ref):
        row = lax.broadcasted_iota(jnp.int32, (n, 1), 0)
        for g, u_ref in enumerate((u0, u1, u2, u3)):
            u = u_ref[...]
            d = _window_sum(u, POOL_R[g], row) * _inv_count(POOL_R[g], row, n) - u
            o_ref[:, g * 128:(g + 1) * 128] = (_dot(d, w_ref[g]) * sc_ref[:, g * 128:(g + 1) * 128]).astype(BF16)

    return pl.pallas_call(
        body, grid=(n_seg,), name=name,
        in_specs=[pl.BlockSpec((n, 128), functools.partial(lambda g, e: (blk0 + e, 6 + g), g)) for g in range(4)]
        + [pl.BlockSpec((4, 128, 128), lambda e: (0, 0, 0)), pl.BlockSpec((1, 512), lambda e: (0, 0))],
        out_specs=pl.BlockSpec((n, 512), lambda e: (e, 0)),
        out_shape=jax.ShapeDtypeStruct((n_seg * n, 512), BF16),
        compiler_params=_params(("parallel",)))(p, p, p, p, w, scale)


def _pool_bwd(p, w, scale, dcat, n, blk0, n_seg, name):
    def body(u0, u1, u2, u3, w_ref, sc_ref, dp_ref, du_ref, dw_ref, dsc_ref):
        e = pl.program_id(0)

        @pl.when(e == 0)
        def _():
            dw_ref[...] = jnp.zeros_like(dw_ref)
            dsc_ref[...] = jnp.zeros_like(dsc_ref)

        row = lax.broadcasted_iota(jnp.int32, (n, 1), 0)
        for g, u_ref in enumerate((u0, u1, u2, u3)):
            sl = slice(g * 128, (g + 1) * 128)
            u = u_ref[...]
            inv = _inv_count(POOL_R[g], row, n)
            d = _window_sum(u, POOL_R[g], row) * inv - u
            dp = dp_ref[:, sl]
            dsc_ref[:, sl] += jnp.sum(dp * _dot(d, w_ref[g]), axis=0, keepdims=True)
            dyp = dp * sc_ref[:, sl]
            dw_ref[g] += _dot_tn(d, dyp)
            dd = _dot_nt(dyp, w_ref[g])
            du_ref[:, sl] = (_window_sum(dd * inv, POOL_R[g], row) - dd).astype(BF16)

    return pl.pallas_call(
        body, grid=(n_seg,), name=name,
        in_specs=[pl.BlockSpec((n, 128), functools.partial(lambda g, e: (blk0 + e, 6 + g), g)) for g in range(4)]
        + [pl.BlockSpec((4, 128, 128), lambda e: (0, 0, 0)), pl.BlockSpec((1, 512), lambda e: (0, 0)),
           pl.BlockSpec((n, 512), lambda e: (blk0 + e, 1))],
        out_specs=[pl.BlockSpec((n, 512), lambda e: (e, 0)),
                   pl.BlockSpec((4, 128, 128), lambda e: (0, 0, 0)), pl.BlockSpec((1, 512), lambda e: (0, 0))],
        out_shape=[jax.ShapeDtypeStruct((n_seg * n, 512), BF16), jax.ShapeDtypeStruct((4, 128, 128), F32),
                   jax.ShapeDtypeStruct((1, 512), F32)],
        compiler_params=_params(("arbitrary",)))(p, p, p, p, w, scale, dcat)


def _gelu(x):
    t = jnp.tanh(math.sqrt(2.0 / math.pi) * (x + 0.044715 * x * x * x))
    return 0.5 * x * (1.0 + t), t


def _gelu_grad(x, t):
    return 0.5 * (1.0 + t) + 0.5 * x * (1.0 - t * t) * (math.sqrt(2.0 / math.pi) * (1.0 + 3 * 0.044715 * x * x))


def _neg_expm1(x):
    series = -x * (1.0 + x * (0.5 + x * (1.0 / 6.0 + x * (1.0 / 24.0 + x * (1.0 / 120.0)))))
    return jnp.where(x > -0.05, series, 1.0 - jnp.exp(x))


def _softplus_neg(lam):
    x = -lam
    e = jnp.exp(-jnp.abs(x))
    log1p = jnp.where(e < 1e-2, e * (1.0 - e * (0.5 - e * (1.0 / 3.0))), jnp.log(1.0 + e))
    return jnp.maximum(x, 0.0) + log1p, -_sigmoid(x)


def _conv(u, w_ref, b_ref, row):
    return (b_ref[...] + _shift_down(u, 1, row) * w_ref[0:1, :] + u * w_ref[1:2, :]
            + _shift_up(u, 1, row) * w_ref[2:3, :] + _shift_up(u, 2, row) * w_ref[3:4, :])


def _lru_gates(uc, d, wa_ref, ba_ref, wx_ref, bx_ref, lam_ref):
    r = _sigmoid(_dot(uc, wa_ref[d]) + ba_ref[d:d + 1, :])
    gi = _sigmoid(_dot(uc, wx_ref[d]) + bx_ref[d:d + 1, :])
    sp, dsp = _softplus_neg(lam_ref[d:d + 1, :])
    la = (-LRU_C) * r * sp
    a = jnp.exp(la)
    sq = jnp.sqrt(_neg_expm1(2.0 * la))
    return r, gi, sp, dsp, a, sq


def _tile_scan(a, b, reverse):
    n = a.shape[0]
    row8 = lax.broadcasted_iota(jnp.int32, (n, 1), 0) & 7
    for k in (1, 2, 4):
        if reverse:
            m = row8 < 8 - k
            a_sh = jnp.where(m, pltpu.roll(a, n - k, 0), 1.0)
            b_sh = jnp.where(m, pltpu.roll(b, n - k, 0), 0.0)
        else:
            m = row8 >= k
            a_sh = jnp.where(m, pltpu.roll(a, k, 0), 1.0)
            b_sh = jnp.where(m, pltpu.roll(b, k, 0), 0.0)
        b = a * b_sh + b
        a = a * a_sh
    return a, b


def _carry_scan(a_ref, b_ref, n, reverse, carry):
    nt8 = n // 8

    def step(i, c):
        t = (nt8 - 1 - i) if reverse else i
        off = pl.multiple_of(t * 8, 8)
        h = a_ref[pl.ds(off, 8), :] * c + b_ref[pl.ds(off, 8), :]
        b_ref[pl.ds(off, 8), :] = h
        return h[0:1, :] if reverse else h[7:8, :]

    return lax.fori_loop(0, nt8, step, carry)


def _chain_scan(segs, reverse):
    carry = jnp.zeros((1, 128), F32)
    for a, b, a_ref, b_ref, n in segs:
        a2, b2 = _tile_scan(a, b, reverse)
        a_ref[...] = a2
        b_ref[...] = b2
        carry = _carry_scan(a_ref, b_ref, n, reverse, carry)


def _lru_specs(cfg):
    n_lat, n_ctx, cb = cfg.n_lat, cfg.n_ctx, cfg.ctx_blk
    return [pl.BlockSpec((n_lat, 128), lambda hb, e: (e, hb)),
            pl.BlockSpec((n_lat, 128), lambda hb, e: (e, 8 + hb)),
            pl.BlockSpec((n_ctx, 128), lambda hb, e: (cb + e, hb)),
            pl.BlockSpec((n_ctx, 128), lambda hb, e: (cb + e, 8 + hb)),
            pl.BlockSpec((4, 128), lambda hb, e: (0, hb)),
            pl.BlockSpec((1, 128), lambda hb, e: (0, hb)),
            pl.BlockSpec((2, None, 128, 128), lambda hb, e: (0, hb, 0, 0)),
            pl.BlockSpec((2, 128), lambda hb, e: (0, hb)),
            pl.BlockSpec((2, None, 128, 128), lambda hb, e: (0, hb, 0, 0)),
            pl.BlockSpec((2, 128), lambda hb, e: (0, hb)),
            pl.BlockSpec((2, 128), lambda hb, e: (0, hb))]


def _lru_fwd(cfg, p, consts, name):
    n_lat, n_ctx = cfg.n_lat, cfg.n_ctx

    def body(gl_ref, ul_ref, gc_ref, uc_ref, cw_ref, cb_ref, wa_ref, ba_ref, wx_ref, bx_ref, lam_ref,
             zl_ref, zc_ref, al, bl, ac, bc):
        row_l = lax.broadcasted_iota(jnp.int32, (n_lat, 1), 0)
        row_c = lax.broadcasted_iota(jnp.int32, (n_ctx, 1), 0)
        uc_l = _conv(ul_ref[...], cw_ref, cb_ref, row_l)
        uc_c = _conv(uc_ref[...], cw_ref, cb_ref, row_c)
        y_l = y_c = None
        for d in range(2):
            _, gi_l, _, _, a_l, sq_l = _lru_gates(uc_l, d, wa_ref, ba_ref, wx_ref, bx_ref, lam_ref)
            _, gi_c, _, _, a_c, sq_c = _lru_gates(uc_c, d, wa_ref, ba_ref, wx_ref, bx_ref, lam_ref)
            _chain_scan([(a_c, sq_c * (gi_c * uc_c), ac, bc, n_ctx), (a_l, sq_l * (gi_l * uc_l), al, bl, n_lat)],
                        reverse=(d == 1))
            y_l = bl[...] if y_l is None else y_l + bl[...]
            y_c = bc[...] if y_c is None else y_c + bc[...]
        zl_ref[...] = (_gelu(gl_ref[...])[0] * y_l).astype(BF16)
        zc_ref[...] = (_gelu(gc_ref[...])[0] * y_c).astype(BF16)

    return pl.pallas_call(
        body, grid=(8, 2), name=name, in_specs=_lru_specs(cfg),
        out_specs=[pl.BlockSpec((n_lat, 128), lambda hb, e: (e, hb)), pl.BlockSpec((n_ctx, 128), lambda hb, e: (e, hb))],
        out_shape=[jax.ShapeDtypeStruct((cfg.t_lat, D), BF16), jax.ShapeDtypeStruct((cfg.t_ctx, D), BF16)],
        scratch_shapes=[pltpu.VMEM((n_lat, 128), F32)] * 2 + [pltpu.VMEM((n_ctx, 128), F32)] * 2,
        compiler_params=_params(("parallel", "arbitrary")))(p, p, p, p, *consts)


def _lru_bwd(cfg, p, dz, consts, name):
    n_lat, n_ctx, cb = cfg.n_lat, cfg.n_ctx, cfg.ctx_blk

    def body(gl_ref, ul_ref, gc_ref, uc_ref, cw_ref, cb_ref, wa_ref, ba_ref, wx_ref, bx_ref, lam_ref,
             dzl_ref, dzc_ref, dgl_ref, dul_ref, dgc_ref, duc_ref, dwa_ref, dwx_ref, vec_ref,
             al, bl, ac, bc, hl, hc):
        e = pl.program_id(1)

        @pl.when(e == 0)
        def _():
            dwa_ref[...] = jnp.zeros_like(dwa_ref)
            dwx_ref[...] = jnp.zeros_like(dwx_ref)
            vec_ref[...] = jnp.zeros_like(vec_ref)

        row_l = lax.broadcasted_iota(jnp.int32, (n_lat, 1), 0)
        row_c = lax.broadcasted_iota(jnp.int32, (n_ctx, 1), 0)
        u_l, u_c = ul_ref[...], uc_ref[...]
        uc_l = _conv(u_l, cw_ref, cb_ref, row_l)
        uc_c = _conv(u_c, cw_ref, cb_ref, row_c)
        for d in range(2):
            _, gi_l, _, _, a_l, sq_l = _lru_gates(uc_l, d, wa_ref, ba_ref, wx_ref, bx_ref, lam_ref)
            _, gi_c, _, _, a_c, sq_c = _lru_gates(uc_c, d, wa_ref, ba_ref, wx_ref, bx_ref, lam_ref)
            _chain_scan([(a_c, sq_c * (gi_c * uc_c), ac, bc, n_ctx), (a_l, sq_l * (gi_l * uc_l), al, bl, n_lat)],
                        reverse=(d == 1))
            hl[d] = bl[...]
            hc[d] = bc[...]
        gel_l, t_l = _gelu(gl_ref[...])
        gel_c, t_c = _gelu(gc_ref[...])
        dz_l, dz_c = dzl_ref[...], dzc_ref[...]
        dgl_ref[...] = (dz_l * (hl[0] + hl[1]) * _gelu_grad(gl_ref[...], t_l)).astype(BF16)
        dgc_ref[...] = (dz_c * (hc[0] + hc[1]) * _gelu_grad(gc_ref[...], t_c)).astype(BF16)
        dy_l, dy_c = dz_l * gel_l, dz_c * gel_c
        duc_l = jnp.zeros((n_lat, 128), F32)
        duc_c = jnp.zeros((n_ctx, 128), F32)
        for d in range(2):
            r_l, gi_l, sp, dsp, a_l, sq_l = _lru_gates(uc_l, d, wa_ref, ba_ref, wx_ref, bx_ref, lam_ref)
            r_c, gi_c, _, _, a_c, sq_c = _lru_gates(uc_c, d, wa_ref, ba_ref, wx_ref, bx_ref, lam_ref)
            if d == 0:
                an_l = _shift_up(a_l, 1, row_l)
                an_c = jnp.where(row_c < n_ctx - 1, pltpu.roll(a_c, n_ctx - 1, 0), a_l[0:1, :])
            else:
                an_l = _shift_down(a_l, 1, row_l)
                an_c = jnp.where(row_c >= 1, pltpu.roll(a_c, 1, 0), a_l[n_lat - 1:n_lat, :])
            _chain_scan([(an_l, dy_l, al, bl, n_lat), (an_c, dy_c, ac, bc, n_ctx)], reverse=(d == 0))
            dsp_sum = jnp.zeros((1, 128), F32)
            for (dh, h, r, gi, a, sq, uc, seg) in ((bl[...], hl[d], r_l, gi_l, a_l, sq_l, uc_l, "l"),
                                                  (bc[...], hc[d], r_c, gi_c, a_c, sq_c, uc_c, "c")):
                b0 = sq * (gi * uc)
                t1 = dh * sq
                dla = dh * (h - b0) - (dh * gi * uc) * (a * a) / sq
                dzr = (dla * ((-LRU_C) * sp)) * r * (1.0 - r)
                dzi = (t1 * uc) * gi * (1.0 - gi)
                dsp_sum = dsp_sum + jnp.sum(dla * ((-LRU_C) * r), axis=0, keepdims=True)
                dwa_ref[d] += _dot_tn(uc, dzr)
                dwx_ref[d] += _dot_tn(uc, dzi)
                vec_ref[d:d + 1, :] += jnp.sum(dzr, axis=0, keepdims=True)
                vec_ref[2 + d:3 + d, :] += jnp.sum(dzi, axis=0, keepdims=True)
                duc = t1 * gi + _dot_nt(dzr, wa_ref[d]) + _dot_nt(dzi, wx_ref[d])
                if seg == "l":
                    duc_l = duc_l + duc
                else:
                    duc_c = duc_c + duc
            vec_ref[4 + d:5 + d, :] += dsp_sum * dsp
        for duc, u, row, du_ref in ((duc_l, u_l, row_l, dul_ref), (duc_c, u_c, row_c, duc_ref)):
            du_ref[...] = (_shift_up(duc, 1, row) * cw_ref[0:1, :] + duc * cw_ref[1:2, :]
                           + _shift_down(duc, 1, row) * cw_ref[2:3, :]
                           + _shift_down(duc, 2, row) * cw_ref[3:4, :]).astype(BF16)
            vec_ref[6:7, :] += jnp.sum(duc * _shift_down(u, 1, row), axis=0, keepdims=True)
            vec_ref[7:8, :] += jnp.sum(duc * u, axis=0, keepdims=True)
            vec_ref[8:9, :] += jnp.sum(duc * _shift_up(u, 1, row), axis=0, keepdims=True)
            vec_ref[9:10, :] += jnp.sum(duc * _shift_up(u, 2, row), axis=0, keepdims=True)
            vec_ref[10:11, :] += jnp.sum(duc, axis=0, keepdims=True)

    lat = pl.BlockSpec((n_lat, 128), lambda hb, e: (e, hb))
    ctx = pl.BlockSpec((n_ctx, 128), lambda hb, e: (e, hb))
    wspec = pl.BlockSpec((2, None, 128, 128), lambda hb, e: (0, hb, 0, 0))
    sd = jax.ShapeDtypeStruct
    return pl.pallas_call(
        body, grid=(8, 2), name=name,
        in_specs=_lru_specs(cfg) + [pl.BlockSpec((n_lat, 128), lambda hb, e: (e, hb)),
                                    pl.BlockSpec((n_ctx, 128), lambda hb, e: (cb + e, hb))],
        out_specs=[lat, lat, ctx, ctx, wspec, wspec, pl.BlockSpec((None, 16, 128), lambda hb, e: (hb, 0, 0))],
        out_shape=[sd((cfg.t_lat, D), BF16), sd((cfg.t_lat, D), BF16), sd((cfg.t_ctx, D), BF16), sd((cfg.t_ctx, D), BF16),
                   sd((2, 8, 128, 128), F32), sd((2, 8, 128, 128), F32), sd((8, 16, 128), F32)],
        scratch_shapes=[pltpu.VMEM((n_lat, 128), F32)] * 2 + [pltpu.VMEM((n_ctx, 128), F32)] * 2
        + [pltpu.VMEM((2, n_lat, 128), F32), pltpu.VMEM((2, n_ctx, 128), F32)],
        compiler_params=_params(("parallel", "arbitrary")))(p, p, p, p, *consts, dz, dz)


def _position():
    x, y, c = lax.axis_index("x"), lax.axis_index("y"), lax.axis_index("c")
    return x, y, c, 4 * x + 2 * y + c


def _peer(x, y, c, k):
    px = 1 - x if k & 4 else x
    py = 1 - y if k & 2 else y
    pc = 1 - c if k & 1 else c
    return (px, py, pc), 4 * px + 2 * py + pc


def _all_gather(v, name, in_vmem):
    def body(v_ref, o_ref, send_sems, recv_sems, local_sem):
        x, y, c, me = _position()
        mine = pltpu.make_async_copy(v_ref, o_ref.at[me], local_sem)
        mine.start()
        sends = []
        for k in range(1, N_DEV):
            peer, _ = _peer(x, y, c, k)
            cp = pltpu.make_async_remote_copy(src_ref=v_ref, dst_ref=o_ref.at[me], send_sem=send_sems.at[k - 1],
                                              recv_sem=recv_sems.at[k - 1], device_id=peer, device_id_type=MESH)
            cp.start()
            sends.append(cp)
        for k in range(1, N_DEV):
            peer, peer_lin = _peer(x, y, c, k)
            pltpu.make_async_remote_copy(src_ref=v_ref, dst_ref=o_ref.at[peer_lin], send_sem=send_sems.at[k - 1],
                                         recv_sem=recv_sems.at[k - 1], device_id=peer, device_id_type=MESH).wait_recv()
        for cp in sends:
            cp.wait_send()
        mine.wait()

    space = pltpu.VMEM if in_vmem else pl.ANY
    return pl.pallas_call(
        body, name=name,
        in_specs=[pl.BlockSpec(memory_space=space)], out_specs=pl.BlockSpec(memory_space=space),
        out_shape=jax.ShapeDtypeStruct((N_DEV,) + v.shape, v.dtype),
        scratch_shapes=[pltpu.SemaphoreType.DMA((N_DEV - 1,)), pltpu.SemaphoreType.DMA((N_DEV - 1,)),
                        pltpu.SemaphoreType.DMA],
        compiler_params=pltpu.CompilerParams(vmem_limit_bytes=VMEM_LIMIT))(v)


_HBM = pl.BlockSpec(memory_space=pltpu.HBM)
_SEM = pl.BlockSpec(memory_space=pltpu.SEMAPHORE)
_EFFECT = pltpu.SideEffectType.DATAFLOW_SIDE_EFFECTING


def _push_start(src, land, block_of, name):
    def body(src_ref, land_ref, send_sem, recv_sem, src_thru, land_thru, token):
        x, y, c, me = _position()
        for k in range(1, N_DEV):
            peer, peer_lin = _peer(x, y, c, k)
            mine, there = block_of(src_ref, land_ref, me, peer_lin)
            pltpu.make_async_remote_copy(src_ref=mine, dst_ref=there, send_sem=send_sem, recv_sem=recv_sem,
                                         device_id=peer, device_id_type=MESH).start()
        token[...] = jnp.zeros_like(token)

    return pl.pallas_call(
        body, name=name,
        out_shape=(pltpu.SemaphoreType.DMA(()), pltpu.SemaphoreType.DMA(()), pltpu.HBM(src.shape, src.dtype),
                   pltpu.HBM(land.shape, land.dtype), jax.ShapeDtypeStruct((8, 128), F32)),
        in_specs=(_HBM, _HBM), out_specs=(_SEM, _SEM, _HBM, _HBM, pl.BlockSpec(memory_space=pltpu.VMEM)),
        input_output_aliases={0: 2, 1: 3},
        compiler_params=pltpu.CompilerParams(has_side_effects=_EFFECT),
    )(pltpu.with_memory_space_constraint(src, pltpu.HBM), pltpu.with_memory_space_constraint(land, pltpu.HBM))


def _push_wait(handle, seven_of, after, name):
    send_sem, recv_sem, src_thru, land_thru, _ = handle

    def body(src_ref, land_ref, send_sem, recv_sem, after_ref, src_dead, got_ref):
        x, y, c, _ = _position()
        seven = seven_of(land_ref)
        cp = pltpu.make_async_remote_copy(src_ref=seven, dst_ref=seven, send_sem=send_sem, recv_sem=recv_sem,
                                          device_id=(x, y, 1 - c), device_id_type=MESH)
        cp.wait_send()
        cp.wait_recv()

    return pl.pallas_call(
        body, name=name,
        out_shape=(pltpu.HBM(src_thru.shape, src_thru.dtype), pltpu.HBM(land_thru.shape, land_thru.dtype)),
        in_specs=(_HBM, _HBM, _SEM, _SEM, pl.BlockSpec(memory_space=pl.ANY)), out_specs=(_HBM, _HBM),
        input_output_aliases={0: 0, 1: 1},
        compiler_params=pltpu.CompilerParams(has_side_effects=_EFFECT),
    )(src_thru, land_thru, send_sem, recv_sem, after)[1]


def _gather_start(src, me, name):
    g, r, C = src.shape
    land = lax.dynamic_update_slice(lax.empty((g, N_DEV * r, C), src.dtype), src, (0, me * r, 0))
    return _push_start(src, land, lambda s, z, i, p: (s, z.at[:, pl.ds(i * r, r), :]), name)


def _gather_wait(handle, after, name):
    r = handle[2].shape[1]
    return _push_wait(handle, lambda z: z.at[:, pl.ds(0, (N_DEV - 1) * r), :], after, name)


def _exchange_start(grad, me, name):
    g, rows, C = grad.shape
    r = rows // N_DEV
    mine = lax.dynamic_slice_in_dim(grad, me * r, r, axis=1)[None]
    land = lax.dynamic_update_slice(lax.empty((N_DEV, g, r, C), grad.dtype), mine, (me, 0, 0, 0))
    return _push_start(grad, land, lambda s, z, i, p: (s.at[:, pl.ds(p * r, r), :], z.at[i]), name)


def _exchange_wait(handle, after, name):
    return _push_wait(handle, lambda z: z.at[pl.ds(0, N_DEV - 1)], after, name)


def _sum_blocks(v, name):
    k, rows, cols = v.shape
    tr = rows
    for cand in (rows, 512, 352, 256, 176, 128, 64, 32, 16):
        if rows % cand == 0 and k * cand * cols * v.dtype.itemsize <= 6 * 1024 * 1024:
            tr = cand
            break

    def body(v_ref, o_ref):
        acc = v_ref[0].astype(F32)
        for s in range(1, k):
            acc = acc + v_ref[s].astype(F32)
        o_ref[...] = acc

    return pl.pallas_call(
        body, grid=(rows // tr,), name=name,
        in_specs=[pl.BlockSpec((k, tr, cols), lambda i: (0, i, 0))],
        out_specs=pl.BlockSpec((tr, cols), lambda i: (i, 0)),
        out_shape=jax.ShapeDtypeStruct((rows, cols), F32),
        compiler_params=_params(("parallel",)))(v)


def _adam_math(w, g, m, v):
    m2 = B1 * m + (1.0 - B1) * g
    v2 = B2 * v + (1.0 - B2) * (g * g)
    m_hat = m2 / (1.0 - B1 ** STEP)
    v_hat = v2 / (1.0 - B2 ** STEP)
    return -LR * (m_hat / (jnp.sqrt(v_hat) + EPS) + WD * w), m2, v2


def _adamw(w, g, m, v, name):
    rows, cols = w.shape
    tr = rows
    for cand in (512, 256, 128, 64, 32, 16, 8):
        if rows % cand == 0 and cand * cols * 4 <= 2 * 1024 * 1024:
            tr = cand
            break

    def body(w_ref, g_ref, m_ref, v_ref, d_ref, m2_ref, v2_ref):
        d_ref[...], m2_ref[...], v2_ref[...] = _adam_math(w_ref[...], g_ref[...], m_ref[...], v_ref[...])

    blk = pl.BlockSpec((tr, cols), lambda i: (i, 0))
    return pl.pallas_call(
        body, grid=(rows // tr,), name=name, in_specs=[blk] * 4, out_specs=[blk] * 3,
        out_shape=[jax.ShapeDtypeStruct((rows, cols), F32)] * 3,
        compiler_params=_params(("parallel",)))(w, g, m, v)


def _as2d(a):
    n = a.size
    if n % 1024 == 0:
        return a.reshape(n // 1024, 1024)
    if n % 128 == 0:
        return a.reshape(n // 128, 128)
    return a.reshape(1, n)


def _blocks_to_cols(a):
    b = jnp.moveaxis(a, 0, -2)
    return b.reshape(b.shape[:-2] + (b.shape[-2] * b.shape[-1],))


def _pack_rows(parts):
    padded, offs, r = [], [], 0
    for p in parts:
        pad = (-p.shape[0]) % 8
        padded.append(jnp.pad(p, ((0, pad), (0, 0))) if pad else p)
        offs.append(r)
        r += p.shape[0] + pad
    return jnp.concatenate(padded, axis=0), offs


def _silu(x):
    return x * jax.nn.sigmoid(x)


def kernel(x, c, ctx, c_ctx, w_mod, b_mod, ln_g, ln_b, ffn_w_gate, ffn_w_up, ffn_w_down, mix_ab_w_in, attn_sink, pool_w, pool_scale, mix_ab_w_out, lru_w_in, lru_conv_w, lru_conv_b, lru_wa, lru_ba, lru_wx, lru_bx, lru_lambda, lru_w_out, loss_target, m_c_ctx, m_w_mod, m_b_mod, m_ln_g, m_ln_b, m_ffn_w_gate, m_ffn_w_up, m_ffn_w_down, m_mix_ab_w_in, m_attn_sink, m_pool_w, m_pool_scale, m_mix_ab_w_out, m_lru_w_in, m_lru_conv_w, m_lru_conv_b, m_lru_wa, m_lru_ba, m_lru_wx, m_lru_bx, m_lru_lambda, m_lru_w_out, v_c_ctx, v_w_mod, v_b_mod, v_ln_g, v_ln_b, v_ffn_w_gate, v_ffn_w_up, v_ffn_w_down, v_mix_ab_w_in, v_attn_sink, v_pool_w, v_pool_scale, v_mix_ab_w_out, v_lru_w_in, v_lru_conv_w, v_lru_conv_b, v_lru_wa, v_lru_ba, v_lru_wx, v_lru_bx, v_lru_lambda, v_lru_w_out):
    weights = dict(c_ctx=c_ctx, w_mod=w_mod, b_mod=b_mod, ln_g=ln_g, ln_b=ln_b, ffn_w_gate=ffn_w_gate,
                   ffn_w_up=ffn_w_up, ffn_w_down=ffn_w_down, mix_ab_w_in=mix_ab_w_in, attn_sink=attn_sink,
                   pool_w=pool_w, pool_scale=pool_scale, mix_ab_w_out=mix_ab_w_out, lru_w_in=lru_w_in,
                   lru_conv_w=lru_conv_w, lru_conv_b=lru_conv_b, lru_wa=lru_wa, lru_ba=lru_ba, lru_wx=lru_wx,
                   lru_bx=lru_bx, lru_lambda=lru_lambda, lru_w_out=lru_w_out)
    mom_m = dict(c_ctx=m_c_ctx, w_mod=m_w_mod, b_mod=m_b_mod, ln_g=m_ln_g, ln_b=m_ln_b, ffn_w_gate=m_ffn_w_gate,
                 ffn_w_up=m_ffn_w_up, ffn_w_down=m_ffn_w_down, mix_ab_w_in=m_mix_ab_w_in, attn_sink=m_attn_sink,
                 pool_w=m_pool_w, pool_scale=m_pool_scale, mix_ab_w_out=m_mix_ab_w_out, lru_w_in=m_lru_w_in,
                 lru_conv_w=m_lru_conv_w, lru_conv_b=m_lru_conv_b, lru_wa=m_lru_wa, lru_ba=m_lru_ba, lru_wx=m_lru_wx,
                 lru_bx=m_lru_bx, lru_lambda=m_lru_lambda, lru_w_out=m_lru_w_out)
    mom_v = dict(c_ctx=v_c_ctx, w_mod=v_w_mod, b_mod=v_b_mod, ln_g=v_ln_g, ln_b=v_ln_b, ffn_w_gate=v_ffn_w_gate,
                 ffn_w_up=v_ffn_w_up, ffn_w_down=v_ffn_w_down, mix_ab_w_in=v_mix_ab_w_in, attn_sink=v_attn_sink,
                 pool_w=v_pool_w, pool_scale=v_pool_scale, mix_ab_w_out=v_mix_ab_w_out, lru_w_in=v_lru_w_in,
                 lru_conv_w=v_lru_conv_w, lru_conv_b=v_lru_conv_b, lru_wa=v_lru_wa, lru_ba=v_lru_ba, lru_wx=v_lru_wx,
                 lru_bx=v_lru_bx, lru_lambda=v_lru_lambda, lru_w_out=v_lru_w_out)
    names = list(weights)

    n_lat, n_ctx = x.shape[1], ctx.shape[1]
    cfg = _Cfg(n_lat, n_ctx)
    _, _, _, me = _position()
    mcols = w_mod.shape[2]

    def t_bf16(w):
        return jnp.swapaxes(w, -1, -2).astype(BF16)

    tok = jnp.zeros((), F32)
    pending = {}
    for key, make_src in (
            ("ffn00", lambda: jnp.stack([t_bf16(ffn_w_gate[0, 0]), t_bf16(ffn_w_up[0, 0]), ffn_w_down[0, 0].astype(BF16)])),
            ("ab_in", lambda: t_bf16(mix_ab_w_in)), ("ab_out", lambda: mix_ab_w_out.astype(BF16)),
            ("ffn01", lambda: jnp.stack([t_bf16(ffn_w_gate[0, 1]), t_bf16(ffn_w_up[0, 1]), ffn_w_down[0, 1].astype(BF16)])),
            ("ffn10", lambda: jnp.stack([t_bf16(ffn_w_gate[1, 0]), t_bf16(ffn_w_up[1, 0]), ffn_w_down[1, 0].astype(BF16)])),
            ("lru_in", lambda: t_bf16(lru_w_in)), ("lru_out", lambda: lru_w_out.astype(BF16)),
            ("ffn11", lambda: jnp.stack([t_bf16(ffn_w_gate[1, 1]), t_bf16(ffn_w_up[1, 1]), ffn_w_down[1, 1].astype(BF16)]))):
        pending[key] = _gather_start(make_src() + tok.astype(BF16), me, "gather_start_" + key)
        tok = pending[key][4][0, 0]

    def weights_now(key, after):
        return _gather_wait(pending[key], after, "gather_wait_" + key)

    small_names = ["ln_g", "ln_b", "lru_conv_w", "lru_conv_b", "lru_ba", "lru_bx", "lru_lambda"]
    small, small_off = _pack_rows([(c + tok).reshape(-1, 128)] + [weights[n].reshape(-1, 128) for n in small_names])
    small_all = _all_gather(small, "gather_small", True)

    def small_full(idx, shp):
        rows = math.prod(shp) // 128
        return _blocks_to_cols(small_all[:, small_off[idx]:small_off[idx] + rows, :].reshape((N_DEV,) + shp))

    c_all = small_all[:, :2 * D // 128, :].reshape(2 * N_DEV, D)
    ln_g_f, ln_b_f = small_full(1, ln_g.shape), small_full(2, ln_b.shape)
    lru_consts = (small_full(3, lru_conv_w.shape)[0], small_full(4, lru_conv_b.shape), lru_wa[0],
                  small_full(5, lru_ba.shape)[0], lru_wx[0], small_full(6, lru_bx.shape)[0],
                  small_full(7, lru_lambda.shape)[0])

    s_rows = jnp.zeros((32, D), F32).at[:16].set(_silu(c_all)).at[16].set(_silu(c_ctx)).astype(BF16)
    mod_mine = jnp.stack([_matmul(s_rows, w_mod[l], "nn", F32, "mod_fwd", bn_cap=1280) for l in range(2)])
    mod_all = _all_gather(mod_mine.reshape(64, mcols), "gather_mod", True).reshape(N_DEV, 2, 32, mcols)
    mod_full = _blocks_to_cols(mod_all) + b_mod[:, None, :]
    ex0 = 2 * me
    mods = []
    for l in range(2):
        rows = jnp.stack([lax.dynamic_index_in_dim(mod_full[l], ex0, 0, False),
                          lax.dynamic_index_in_dim(mod_full[l], ex0 + 1, 0, False), mod_full[l, 16]])
        mods.append(rows.reshape(3, N_MOD, D))

    h0 = jnp.concatenate([x.reshape(cfg.t_lat, D), ctx.reshape(cfg.t_ctx, D)], axis=0)
    cos, sin = _rope_tables(n_lat)
    sink_rows = jnp.broadcast_to(attn_sink[0][:, None], (8, 128)).astype(F32)

    saved = []
    wf = [[None, None], [None, None]]
    h = h0
    xin = _modulate(cfg, h0, mods[0], 0, 1, "modulate_in")
    for l in range(2):
        st = {"h_in": h, "xin1": xin}
        wf[l][0] = weights_now("ffn%d0" % l, xin)
        g1, u1, y1 = _ffn_fwd(xin, wf[l][0], "ffn_fwd")
        h1, xhat1, rstd1, xin2 = _ln_fwd(cfg, h, y1, mods[l], 2, 0.5, ln_g_f[l, 0][None], ln_b_f[l, 0][None],
                                          mods[l], (3, 4), "ln_fwd_a")
        st.update(g1=g1, u1=u1, y1=y1, h1=h1, xhat1=xhat1, rstd1=rstd1, xin2=xin2)
        if l == 0:
            w_ab_in_t = weights_now("ab_in", xin2)[0]
            p = _matmul(xin2, w_ab_in_t, "nt", F32, "mix_ab_in")
            att_l, att_c = _attn_fwd(cfg, p, cos, sin, sink_rows, "attn_fwd")
            pool_l = _pool_fwd(p, pool_w[0], pool_scale, n_lat, 0, 2, "pool_fwd_lat")
            pool_c = _pool_fwd(p, pool_w[0], pool_scale, n_ctx, cfg.ctx_blk, 2, "pool_fwd_ctx")
            cat = jnp.concatenate([jnp.concatenate([att_l, pool_l], axis=1),
                                   jnp.concatenate([att_c, pool_c], axis=1)], axis=0)
            w_ab_out = weights_now("ab_out", cat)[0]
            y2 = _matmul(cat, w_ab_out, "nn", F32, "mix_ab_out")
        else:
            w_lru_in_t = weights_now("lru_in", xin2)[0]
            p = _matmul(xin2, w_lru_in_t, "nt", F32, "lru_in")
            z_l, z_c = _lru_fwd(cfg, p, lru_consts, "lru_fwd")
            cat = jnp.concatenate([z_l, z_c], axis=0)
            w_lru_out = weights_now("lru_out", cat)[0]
            y2 = _matmul(cat, w_lru_out, "nn", F32, "lru_out")
        h2, xhat2, rstd2, xin3 = _ln_fwd(cfg, h1, y2, mods[l], 5, 1.0, ln_g_f[l, 1][None], ln_b_f[l, 1][None],
                                          mods[l], (6, 7), "ln_fwd_b")
        wf[l][1] = weights_now("ffn%d1" % l, xin3)
        g3, u3, y3 = _ffn_fwd(xin3, wf[l][1], "ffn_fwd")
        if l == 0:
            h3, xhat3, rstd3, xin = _ln_fwd(cfg, h2, y3, mods[l], 8, 0.5, ln_g_f[l, 2][None], ln_b_f[l, 2][None],
                                            mods[1], (0, 1), "ln_fwd_a")
        else:
            h3, xhat3, rstd3 = _ln_fwd(cfg, h2, y3, mods[l], 8, 0.5, ln_g_f[l, 2][None], ln_b_f[l, 2][None],
                                       None, None, "ln_fwd_last")
        st.update(p=p, cat=cat, y2=y2, h2=h2, xhat2=xhat2, rstd2=rstd2, xin3=xin3, g3=g3, u3=u3, y3=y3,
                  xhat3=xhat3, rstd3=rstd3)
        saved.append(st)
        h = h3

    dy, loss_tile = _loss(cfg, h, loss_target.reshape(cfg.t_lat, D), "loss")
    loss = lax.psum(loss_tile[0, 0], ("x", "y", "c"))

    grads = {}
    dmod = [None, None]
    recv_ffn = [[None, None], [None, None]]
    dln_g = [[None] * 3, [None] * 3]
    dln_b = [[None] * 3, [None] * 3]

    def ffn_weight_grads(tag, xin_b, dg, du, a_act, dys):
        parts = [_matmul(dg, xin_b, "tn", BF16, "ffn_dw", bm_cap=1408)[None],
                 _matmul(du, xin_b, "tn", BF16, "ffn_dw", bm_cap=1408)[None],
                 _matmul(a_act, dys, "tn", BF16, "ffn_dw", bm_cap=1408)[None]]
        return [_exchange_start(part, me, "exchange_start_ffn%s_%d" % (tag, k)) for k, part in enumerate(parts)]

    def pin(handles):
        total = handles[0][4][0, 0]
        for hd in handles[1:]:
            total = total + hd[4][0, 0]
        return total

    up = (dy,)
    dmod_next = None
    last_sent = None
    for l in (1, 0):
        st = saved[l]
        dm = [None] * N_MOD

        def put_stats(stats, gate_idx, nxt):
            dm[gate_idx] = stats[:, 2, :]
            if nxt is not None:
                nxt[0][nxt[1]] = stats[:, 4, :]
                nxt[0][nxt[1] + 1] = stats[:, 3, :]

        lng3 = ln_g_f[l, 2][None] if last_sent is None else ln_g_f[l, 2][None] + pin(last_sent)
        dres, dys, stats = _ln_bwd(cfg, up, st["xhat3"], st["rstd3"], st["y3"], mods[l], 8, 0.5,
                                   lng3, "ln_bwd_fused" if len(up) > 1 else "ln_bwd_last")
        put_stats(stats, 8, None if len(up) == 1 else (dmod_next, 0))
        dln_g[l][2], dln_b[l][2] = stats[:, 0, :].sum(0), stats[:, 1, :].sum(0)
        dg, du, a_act, dxin = _ffn_bwd(dys, st["g3"], st["u3"], wf[l][1], "ffn_bwd")
        recv_ffn[l][1] = ffn_weight_grads("%d1" % l, st["xin3"], dg, du, a_act, dys)
        dres, dys, stats = _ln_bwd(cfg, (dres, dxin, st["h2"], mods[l], 7), st["xhat2"], st["rstd2"], st["y2"],
                                   mods[l], 5, 1.0, ln_g_f[l, 1][None] + pin(recv_ffn[l][1]), "ln_bwd_fused")
        put_stats(stats, 5, (dm, 6))
        dln_g[l][1], dln_b[l][1] = stats[:, 0, :].sum(0), stats[:, 1, :].sum(0)
        if l == 0:
            dw_out = _matmul(st["cat"], dys, "tn", BF16, "mix_ab_dw_out")
            dcat = _matmul(dys, w_ab_out, "nt", F32, "mix_ab_dcat")
            dq, dk, dv, dqc, dkc, dvc, dsink = _attn_bwd(cfg, st["p"], dcat, cos, sin, sink_rows, "attn_bwd")
            du_l, dpw_l, dps_l = _pool_bwd(st["p"], pool_w[0], pool_scale, dcat, n_lat, 0, 2, "pool_bwd_lat")
            du_c, dpw_c, dps_c = _pool_bwd(st["p"], pool_w[0], pool_scale, dcat, n_ctx, cfg.ctx_blk, 2, "pool_bwd_ctx")
            dp = jnp.concatenate([jnp.concatenate([dq, dk, dv, du_l], axis=1),
                                  jnp.concatenate([dqc, dkc, dvc, du_c], axis=1)], axis=0)
            dw_in_t = _matmul(dp, st["xin2"], "tn", BF16, "mix_ab_dw_in", bm_cap=1280)
            dxin = _matmul(dp, w_ab_in_t, "nn", F32, "mix_ab_dx")
            recv_mix = [_exchange_start(part, me, "exchange_start_mix_ab_%d" % k)
                        for k, part in enumerate((dw_in_t[None], dw_out[None], _as2d(dpw_l + dpw_c)[None]))]
            grads["attn_sink"] = (dsink[0, :, 0] + dsink[1, :, 0])[None, :]
            grads["pool_scale"] = dps_l + dps_c
        else:
            dw_out = _matmul(st["cat"], dys, "tn", BF16, "lru_dw_out")
            dz = _matmul(dys, w_lru_out, "nt", F32, "lru_dz")
            dgl, dul, dgc, duc, dwa, dwx, vec = _lru_bwd(cfg, st["p"], dz, lru_consts, "lru_bwd")
            dp = jnp.concatenate([jnp.concatenate([dgl, dul], axis=1), jnp.concatenate([dgc, duc], axis=1)], axis=0)
            dw_in_t = _matmul(dp, st["xin2"], "tn", BF16, "lru_dw_in", bm_cap=1024)
            dxin = _matmul(dp, w_lru_in_t, "nn", F32, "lru_dx")
            recv_mix = [_exchange_start(part, me, "exchange_start_lru_%d" % k)
                        for k, part in enumerate((dw_in_t[None], dw_out[None], _as2d(dwa)[None], _as2d(dwx)[None]))]
            vec_t = jnp.moveaxis(vec, 0, 1).reshape(16, D)
            grads["lru_ba"], grads["lru_bx"] = vec_t[0:2], vec_t[2:4]
            grads["lru_lambda"], grads["lru_conv_w"], grads["lru_conv_b"] = vec_t[4:6], vec_t[6:10], vec_t[10:11]
        if l == 0:
            recv_ab = recv_mix
        else:
            recv_lru = recv_mix
        dres, dys, stats = _ln_bwd(cfg, (dres, dxin, st["h1"], mods[l], 4), st["xhat1"], st["rstd1"], st["y1"],
                                   mods[l], 2, 0.5, ln_g_f[l, 0][None] + pin(recv_mix), "ln_bwd_fused")
        put_stats(stats, 2, (dm, 3))
        dln_g[l][0], dln_b[l][0] = stats[:, 0, :].sum(0), stats[:, 1, :].sum(0)
        dg, du, a_act, dxin = _ffn_bwd(dys, st["g1"], st["u1"], wf[l][0], "ffn_bwd")
        recv_ffn[l][0] = ffn_weight_grads("%d0" % l, st["xin1"], dg, du, a_act, dys)
        last_sent = recv_ffn[l][0]
        dmod[l] = dm
        dmod_next = dm
        up = (dres, dxin, st["h_in"], mods[l], 1)
    dh0, stats = _modulate_bwd(cfg, up[0], up[1], h0, mods[0] + pin(last_sent), 1, "modulate_bwd")
    dmod[0][0], dmod[0][1] = stats[:, 4, :], stats[:, 3, :]
    grad_x = dh0[:cfg.t_lat].reshape(x.shape)

    def arrived(handle, name):
        return _exchange_wait(handle, dh0, name)

    recv_ffn = [[[arrived(hd, "exchange_wait_ffn%d%d_%d" % (l, i, k)) for k, hd in enumerate(recv_ffn[l][i])]
                 for i in range(2)] for l in range(2)]
    recv_ab = [arrived(hd, "exchange_wait_mix_ab_%d" % k) for k, hd in enumerate(recv_ab)]
    recv_lru = [arrived(hd, "exchange_wait_lru_%d" % k) for k, hd in enumerate(recv_lru)]

    dmod_mine = jnp.stack([jnp.stack(dmod[l], axis=1).reshape(3, N_MOD * D) for l in range(2)])
    n_dm = 6 * N_MOD * D // 128
    dmod_all = _all_gather(dmod_mine.reshape(n_dm, 128), "gather_dmod", True)
    dmod_sum = _sum_blocks(dmod_all, "sum_dmod").reshape(2, 3, N_MOD * D)
    dmod_all = dmod_all.reshape(N_DEV, 2, 3, N_MOD * D)
    grads["b_mod"] = dmod_sum[:, 0] + dmod_sum[:, 1] + dmod_sum[:, 2]
    dmod_ex = jnp.moveaxis(dmod_all[:, :, 0:2, :], 1, 0).reshape(2, 2 * N_DEV, N_MOD * D)
    dm_rows = jnp.zeros((2, 32, N_MOD * D), F32).at[:, :16].set(dmod_ex).at[:, 16].set(dmod_sum[:, 2])
    dm_cols = lax.dynamic_slice_in_dim(dm_rows, me * mcols, mcols, axis=2).astype(BF16)
    grads["w_mod"] = jnp.stack([_matmul(s_rows, dm_cols[l], "tn", F32, "mod_dw", bn_cap=1280) for l in range(2)])
    ds_part = None
    for l in range(2):
        part = _matmul(dm_cols[l, 16:32], w_mod[l], "nt", F32, "mod_ds", bk_cap=1280)[0]
        ds_part = part if ds_part is None else ds_part + part

    def shard_sum(recv, name):
        return _sum_blocks(recv.reshape(N_DEV, recv.shape[2], recv.shape[3]), name)

    gate_g = [[None, None], [None, None]]
    up_g = [[None, None], [None, None]]
    down_g = [[None, None], [None, None]]
    for l in range(2):
        for i in range(2):
            gt, ut, dn = [shard_sum(r, "sum_ffn") for r in recv_ffn[l][i]]
            gate_g[l][i], up_g[l][i], down_g[l][i] = gt.T, ut.T, dn
    grads["ffn_w_gate"] = jnp.stack([jnp.stack(gate_g[l]) for l in range(2)])
    grads["ffn_w_up"] = jnp.stack([jnp.stack(up_g[l]) for l in range(2)])
    grads["ffn_w_down"] = jnp.stack([jnp.stack(down_g[l]) for l in range(2)])
    grads["mix_ab_w_in"] = shard_sum(recv_ab[0], "sum_mix_in").T[None]
    grads["mix_ab_w_out"] = shard_sum(recv_ab[1], "sum_mix_out")[None]
    grads["lru_w_in"] = shard_sum(recv_lru[0], "sum_lru_in").T[None]
    grads["lru_w_out"] = shard_sum(recv_lru[1], "sum_lru_out")[None]
    rep_parts = [shard_sum(recv_lru[2], "sum_rep"), shard_sum(recv_lru[3], "sum_rep"), shard_sum(recv_ab[2], "sum_rep")]
    rep_names = ["lru_wa", "lru_wx", "pool_w"]

    dln_g_f = jnp.stack([jnp.stack(dln_g[l]) for l in range(2)])
    dln_b_f = jnp.stack([jnp.stack(dln_b[l]) for l in range(2)])
    sink_pad = jnp.zeros((1, 128), F32).at[0, :8].set(grads["attn_sink"][0])
    part_list = [p_.reshape(-1, 128) for p_ in rep_parts] + [
        dln_g_f.reshape(-1, 128), dln_b_f.reshape(-1, 128), grads["lru_conv_w"].reshape(-1, 128),
        grads["lru_conv_b"].reshape(-1, 128), grads["lru_ba"].reshape(-1, 128), grads["lru_bx"].reshape(-1, 128),
        grads["lru_lambda"].reshape(-1, 128), ds_part.reshape(-1, 128), sink_pad, grads["pool_scale"].reshape(-1, 128)]
    parts, part_off = _pack_rows(part_list)
    parts_all = _all_gather(parts, "gather_partials", True)
    parts_sum = _sum_blocks(parts_all, "sum_partials")

    for i, n in enumerate(rep_names):
        rows = part_list[i].shape[0]
        grads[n] = parts_all[:, part_off[i]:part_off[i] + rows, :].reshape(weights[n].shape)

    def take(idx):
        return parts_sum[part_off[idx]:part_off[idx] + part_list[idx].shape[0]]

    def my_cols(full, shp):
        w = shp[-1]
        return lax.dynamic_slice_in_dim(full, me * w, w, axis=full.ndim - 1)

    grads["ln_g"] = my_cols(take(3).reshape(2, 3, D), ln_g.shape)
    grads["ln_b"] = my_cols(take(4).reshape(2, 3, D), ln_b.shape)
    grads["lru_conv_w"] = my_cols(take(5).reshape(1, 4, D), lru_conv_w.shape)
    grads["lru_conv_b"] = my_cols(take(6).reshape(1, D), lru_conv_b.shape)
    grads["lru_ba"] = my_cols(take(7).reshape(1, 2, D), lru_ba.shape)
    grads["lru_bx"] = my_cols(take(8).reshape(1, 2, D), lru_bx.shape)
    grads["lru_lambda"] = my_cols(take(9).reshape(1, 2, D), lru_lambda.shape)
    sg = jax.nn.sigmoid(c_ctx)
    grads["c_ctx"] = take(10).reshape(D) * (sg * (1.0 + c_ctx * (1.0 - sg)))
    grads["attn_sink"] = take(11)[:, :8]
    grads["pool_scale"] = take(12).reshape(pool_scale.shape)

    delta, new_m, new_v = {}, {}, {}
    for n in names:
        shp = weights[n].shape
        grads[n] = grads[n].reshape(shp)
        d2, m2, v2 = _adamw(_as2d(weights[n]), _as2d(grads[n]), _as2d(mom_m[n]), _as2d(mom_v[n]), "adamw")
        delta[n], new_m[n], new_v[n] = d2.reshape(shp), m2.reshape(shp), v2.reshape(shp)

    return (loss, grad_x, *[grads[n] for n in names], *[delta[n] for n in names],
            *[new_m[n] for n in names], *[new_v[n] for n in names])
```

```python
import functools
import math

import jax
import jax.numpy as jnp
from jax import lax
from jax.experimental import pallas as pl
from jax.experimental.pallas import tpu as pltpu

F32 = jnp.float32
BF16 = jnp.bfloat16
MESH = pl.DeviceIdType.MESH

D = 1024
N_MOD = 9
N_DEV = 8
HEAD_DIM = 64
ATT_HEADS = 8
KV_HEADS = 2
ATT_W = 512
BLK = 128
ATT_SCALE = HEAD_DIM ** -0.5
GRID_W = 64
ROPE_FREQS = HEAD_DIM // 4
ROPE_THETA = 10000.0
POOL_R = (1, 2, 4, 8)
LRU_C = 8.0
LN_EPS = 1e-5
NEG_INF = -1e30
ALPHA = 4.0 ** 0.25
LR, B1, B2, EPS, WD, STEP = 0.001, 0.9, 0.999, 1e-08, 0.01, 10
VMEM_LIMIT = 56 * 1024 * 1024
ROW_TILE = 512


def _params(sem=None):
    if sem is None:
        return pltpu.CompilerParams(vmem_limit_bytes=VMEM_LIMIT)
    return pltpu.CompilerParams(dimension_semantics=sem, vmem_limit_bytes=VMEM_LIMIT)


def _sigmoid(x):
    return 1.0 / (1.0 + jnp.exp(-x))


def _dot(a, b):
    return jnp.dot(a.astype(BF16), b.astype(BF16), preferred_element_type=F32)


def _dot_nt(a, b):
    return lax.dot_general(a.astype(BF16), b.astype(BF16), (((1,), (1,)), ((), ())), preferred_element_type=F32)


def _dot_tn(a, b):
    return lax.dot_general(a.astype(BF16), b.astype(BF16), (((0,), (0,)), ((), ())), preferred_element_type=F32)


def _pick(n, cap):
    best = None
    for m in range(128, min(n, cap) + 1, 128):
        if n % m == 0:
            best = m
    return n if best is None else best


def _chunks(width, step=256):
    out, c = [], 0
    while c < width:
        w = min(step, width - c)
        out.append((c, w))
        c += w
    return out


class _Cfg:
    def __init__(self, n_lat, n_ctx):
        self.n_lat, self.n_ctx = n_lat, n_ctx
        self.t_lat, self.t_ctx = 2 * n_lat, 2 * n_ctx
        self.T = self.t_lat + self.t_ctx
        self.tm = min(ROW_TILE, self.t_ctx)
        assert n_lat % self.tm == 0 and self.t_ctx % self.tm == 0 and n_lat >= 3 * BLK and n_ctx % BLK == 0
        self.nt = self.T // self.tm
        self.nlt = n_lat // self.tm
        self.ctx_blk = self.t_lat // n_ctx

    def seg(self, i):
        return jnp.minimum(i // self.nlt, 2)

    def first_of_seg(self, i):
        return jnp.where(i < 2 * self.nlt, i % self.nlt == 0, i == 2 * self.nlt)


def _modulate(cfg, h, mod, shift_idx, scale_idx, name):
    tm = cfg.tm

    def body(h_ref, mod_ref, o_ref):
        sh = mod_ref[shift_idx:shift_idx + 1, :]
        sc = mod_ref[scale_idx:scale_idx + 1, :]
        o_ref[...] = (h_ref[...] * (1.0 + sc) + sh).astype(BF16)

    return pl.pallas_call(
        body, grid=(cfg.nt,), name=name,
        in_specs=[pl.BlockSpec((tm, D), lambda i: (i, 0)),
                  pl.BlockSpec((None, N_MOD, D), lambda i: (cfg.seg(i), 0, 0))],
        out_specs=pl.BlockSpec((tm, D), lambda i: (i, 0)),
        out_shape=jax.ShapeDtypeStruct((cfg.T, D), BF16),
        compiler_params=_params(("parallel",)),
    )(h, mod)


def _ln_fwd(cfg, h, y, mod, gate_idx, coef, lng, lnb, mod_next, next_idx, name):
    tm = cfg.tm
    has_next = next_idx is not None

    def body(*refs):
        if has_next:
            h_ref, y_ref, mod_ref, g_ref, b_ref, modn_ref, hn_ref, xhat_ref, rstd_ref, xin_ref = refs
        else:
            h_ref, y_ref, mod_ref, g_ref, b_ref, hn_ref, xhat_ref, rstd_ref = refs
        gate = mod_ref[gate_idx:gate_idx + 1, :]
        z = ALPHA * h_ref[...] + (coef * gate) * y_ref[...]
        mu = jnp.mean(z, axis=-1, keepdims=True)
        zc = z - mu
        var = jnp.mean(zc * zc, axis=-1, keepdims=True)
        rstd = lax.rsqrt(var + LN_EPS)
        xhat = zc * rstd
        hn = xhat * g_ref[...] + b_ref[...]
        hn_ref[...] = hn
        xhat_ref[...] = xhat
        rstd_ref[...] = rstd
        if has_next:
            sh = modn_ref[next_idx[0]:next_idx[0] + 1, :]
            sc = modn_ref[next_idx[1]:next_idx[1] + 1, :]
            xin_ref[...] = (hn * (1.0 + sc) + sh).astype(BF16)

    row = pl.BlockSpec((tm, D), lambda i: (i, 0))
    modspec = pl.BlockSpec((None, N_MOD, D), lambda i: (cfg.seg(i), 0, 0))
    vec = pl.BlockSpec((1, D), lambda i: (0, 0))
    in_specs = [row, row, modspec, vec, vec]
    args = [h, y, mod, lng, lnb]
    out_specs = [row, row, pl.BlockSpec((tm, 1), lambda i: (i, 0))]
    out_shape = [jax.ShapeDtypeStruct((cfg.T, D), F32), jax.ShapeDtypeStruct((cfg.T, D), F32),
                 jax.ShapeDtypeStruct((cfg.T, 1), F32)]
    if has_next:
        in_specs.append(modspec)
        args.append(mod_next)
        out_specs.append(row)
        out_shape.append(jax.ShapeDtypeStruct((cfg.T, D), BF16))
    return pl.pallas_call(body, grid=(cfg.nt,), name=name, in_specs=in_specs, out_specs=out_specs,
                          out_shape=out_shape, compiler_params=_params(("parallel",)))(*args)


def _ln_bwd(cfg, up, xhat, rstd, y, mod, gate_idx, coef, lng, name):
    tm = cfg.tm
    fused = len(up) > 1
    scale_next = up[4] if fused else None

    def body(*refs):
        if fused:
            dres_n, dxin_n, hn_ref, modn_ref, xhat_ref, rstd_ref, y_ref, mod_ref, g_ref, dres_ref, dys_ref, st_ref = refs
        else:
            dhn_ref, xhat_ref, rstd_ref, y_ref, mod_ref, g_ref, dres_ref, dys_ref, st_ref = refs
        i = pl.program_id(0)

        @pl.when(cfg.first_of_seg(i))
        def _():
            st_ref[...] = jnp.zeros_like(st_ref)

        if fused:
            dxin = dxin_n[...]
            sc = modn_ref[scale_next:scale_next + 1, :]
            dhn = dres_n[...] + dxin * (1.0 + sc)
            st_ref[3:4, :] += jnp.sum(dxin * hn_ref[...], axis=0, keepdims=True)
            st_ref[4:5, :] += jnp.sum(dxin, axis=0, keepdims=True)
        else:
            dhn = dhn_ref[...]
        xhat = xhat_ref[...]
        gdh = dhn * g_ref[...]
        m1 = jnp.mean(gdh, axis=-1, keepdims=True)
        m2 = jnp.mean(gdh * xhat, axis=-1, keepdims=True)
        dz = rstd_ref[...] * (gdh - m1 - xhat * m2)
        gate = mod_ref[gate_idx:gate_idx + 1, :]
        dres_ref[...] = ALPHA * dz
        dys_ref[...] = ((coef * gate) * dz).astype(BF16)
        st_ref[0:1, :] += jnp.sum(dhn * xhat, axis=0, keepdims=True)
        st_ref[1:2, :] += jnp.sum(dhn, axis=0, keepdims=True)
        st_ref[2:3, :] += jnp.sum((coef * dz) * y_ref[...], axis=0, keepdims=True)

    row = pl.BlockSpec((tm, D), lambda i: (i, 0))
    modspec = pl.BlockSpec((None, N_MOD, D), lambda i: (cfg.seg(i), 0, 0))
    vec = pl.BlockSpec((1, D), lambda i: (0, 0))
    col = pl.BlockSpec((tm, 1), lambda i: (i, 0))
    if fused:
        in_specs = [row, row, row, modspec, row, col, row, modspec, vec]
        args = [up[0], up[1], up[2], up[3], xhat, rstd, y, mod, lng]
    else:
        in_specs = [row, row, col, row, modspec, vec]
        args = [up[0], xhat, rstd, y, mod, lng]
    return pl.pallas_call(
        body, grid=(cfg.nt,), name=name, in_specs=in_specs,
        out_specs=[row, row, pl.BlockSpec((None, 8, D), lambda i: (cfg.seg(i), 0, 0))],
        out_shape=[jax.ShapeDtypeStruct((cfg.T, D), F32), jax.ShapeDtypeStruct((cfg.T, D), BF16),
                   jax.ShapeDtypeStruct((3, 8, D), F32)],
        compiler_params=_params(("arbitrary",)))(*args)


def _modulate_bwd(cfg, dres, dxin, h, mod, scale_idx, name):
    tm = cfg.tm

    def body(dres_ref, dxin_ref, h_ref, mod_ref, dh_ref, st_ref):
        i = pl.program_id(0)

        @pl.when(cfg.first_of_seg(i))
        def _():
            st_ref[...] = jnp.zeros_like(st_ref)

        dxin = dxin_ref[...]
        sc = mod_ref[scale_idx:scale_idx + 1, :]
        dh_ref[...] = dres_ref[...] + dxin * (1.0 + sc)
        st_ref[3:4, :] += jnp.sum(dxin * h_ref[...], axis=0, keepdims=True)
        st_ref[4:5, :] += jnp.sum(dxin, axis=0, keepdims=True)

    row = pl.BlockSpec((tm, D), lambda i: (i, 0))
    return pl.pallas_call(
        body, grid=(cfg.nt,), name=name,
        in_specs=[row, row, row, pl.BlockSpec((None, N_MOD, D), lambda i: (cfg.seg(i), 0, 0))],
        out_specs=[row, pl.BlockSpec((None, 8, D), lambda i: (cfg.seg(i), 0, 0))],
        out_shape=[jax.ShapeDtypeStruct((cfg.T, D), F32), jax.ShapeDtypeStruct((3, 8, D), F32)],
        compiler_params=_params(("arbitrary",)))(dres, dxin, h, mod)


def _loss(cfg, h, target, name):
    tm = cfg.tm
    n_lt = 2 * cfg.nlt

    def body(h_ref, t_ref, dy_ref, l_ref):
        i = pl.program_id(0)

        @pl.when(i == 0)
        def _():
            l_ref[...] = jnp.zeros_like(l_ref)

        @pl.when(i < n_lt)
        def _():
            err = h_ref[...] - t_ref[...]
            dy_ref[...] = err * (1.0 / D)
            part = jnp.sum(jnp.sum(err * err, axis=1, keepdims=True), axis=0, keepdims=True) * (0.5 / D)
            l_ref[...] += jnp.broadcast_to(part, l_ref.shape)

        @pl.when(i >= n_lt)
        def _():
            dy_ref[...] = jnp.zeros_like(dy_ref)

    return pl.pallas_call(
        body, grid=(cfg.nt,), name=name,
        in_specs=[pl.BlockSpec((tm, D), lambda i: (i, 0)),
                  pl.BlockSpec((tm, D), lambda i: (jnp.minimum(i, n_lt - 1), 0))],
        out_specs=[pl.BlockSpec((tm, D), lambda i: (i, 0)), pl.BlockSpec((8, 128), lambda i: (0, 0))],
        out_shape=[jax.ShapeDtypeStruct((cfg.T, D), F32), jax.ShapeDtypeStruct((8, 128), F32)],
        compiler_params=_params(("arbitrary",)))(h, target)


def _matmul(a, b, mode, out_dtype, name, bm_cap=512, bn_cap=1408, bk_cap=1024):
    if mode == "nn":
        (M, K), N = a.shape, b.shape[1]
    elif mode == "nt":
        (M, K), N = a.shape, b.shape[0]
    else:
        (K, M), N = a.shape, b.shape[1]
    bm, bn, bk = _pick(M, bm_cap), _pick(N, bn_cap), _pick(K, bk_cap)
    nk = K // bk

    def body(a_ref, b_ref, o_ref, acc_ref):
        k = pl.program_id(2)
        if mode == "nn":
            part = _dot(a_ref[...], b_ref[...])
        elif mode == "nt":
            part = _dot_nt(a_ref[...], b_ref[...])
        else:
            part = _dot_tn(a_ref[...], b_ref[...])

        @pl.when(k == 0)
        def _():
            acc_ref[...] = part

        @pl.when(k > 0)
        def _():
            acc_ref[...] += part

        @pl.when(k == nk - 1)
        def _():
            o_ref[...] = acc_ref[...].astype(out_dtype)

    if mode == "nn":
        a_spec = pl.BlockSpec((bm, bk), lambda i, j, k: (i, k))
        b_spec = pl.BlockSpec((bk, bn), lambda i, j, k: (k, j))
    elif mode == "nt":
        a_spec = pl.BlockSpec((bm, bk), lambda i, j, k: (i, k))
        b_spec = pl.BlockSpec((bn, bk), lambda i, j, k: (j, k))
    else:
        a_spec = pl.BlockSpec((bk, bm), lambda i, j, k: (k, i))
        b_spec = pl.BlockSpec((bk, bn), lambda i, j, k: (k, j))
    return pl.pallas_call(
        body, grid=(M // bm, N // bn, nk), name=name, in_specs=[a_spec, b_spec],
        out_specs=pl.BlockSpec((bm, bn), lambda i, j, k: (i, j)),
        out_shape=jax.ShapeDtypeStruct((M, N), out_dtype),
        scratch_shapes=[pltpu.VMEM((bm, bn), F32)],
        compiler_params=_params(("parallel", "parallel", "arbitrary")))(a, b)


def _ffn_tile(T, cap):
    best = 256
    for t in range(256, cap + 1, 256):
        if T % t == 0:
            best = t
    return best


def _ffn_fwd(xin, wf, name):
    T = xin.shape[0]
    F = wf.shape[1]
    tm, tf = _ffn_tile(T, 768), F // 2
    assert tf % 128 == 0 and T % tm == 0

    def body(x_ref, wg_ref, wu_ref, wd_ref, g_ref, u_ref, y_ref):
        j = pl.program_id(1)
        x = x_ref[...]
        acc = None
        for c0, cw in _chunks(tf):
            g = _dot_nt(x, wg_ref[c0:c0 + cw, :])
            u = _dot_nt(x, wu_ref[c0:c0 + cw, :])
            g_ref[:, c0:c0 + cw] = g.astype(BF16)
            u_ref[:, c0:c0 + cw] = u.astype(BF16)
            part = _dot(g * _sigmoid(g) * u, wd_ref[c0:c0 + cw, :])
            acc = part if acc is None else acc + part

        @pl.when(j == 0)
        def _():
            y_ref[...] = acc

        @pl.when(j > 0)
        def _():
            y_ref[...] += acc

    return pl.pallas_call(
        body, grid=(T // tm, 2), name=name,
        in_specs=[pl.BlockSpec((tm, D), lambda i, j: (i, 0)),
                  pl.BlockSpec((None, tf, D), lambda i, j: (0, j, 0)),
                  pl.BlockSpec((None, tf, D), lambda i, j: (1, j, 0)),
                  pl.BlockSpec((None, tf, D), lambda i, j: (2, j, 0))],
        out_specs=[pl.BlockSpec((tm, tf), lambda i, j: (i, j)),
                   pl.BlockSpec((tm, tf), lambda i, j: (i, j)),
                   pl.BlockSpec((tm, D), lambda i, j: (i, 0))],
        out_shape=[jax.ShapeDtypeStruct((T, F), BF16), jax.ShapeDtypeStruct((T, F), BF16),
                   jax.ShapeDtypeStruct((T, D), F32)],
        compiler_params=_params(("parallel", "arbitrary")))(xin, wf, wf, wf)


def _ffn_bwd(dys, g, u, wf, name):
    T = dys.shape[0]
    F = wf.shape[1]
    tm, tf = _ffn_tile(T, 512), F // 2

    def body(dy_ref, g_ref, u_ref, wg_ref, wu_ref, wd_ref, dg_ref, du_ref, a_ref, dx_ref):
        j = pl.program_id(1)
        dy = dy_ref[...]
        acc = None
        for c0, cw in _chunks(tf):
            gg = g_ref[:, c0:c0 + cw].astype(F32)
            uu = u_ref[:, c0:c0 + cw].astype(F32)
            da = _dot_nt(dy, wd_ref[c0:c0 + cw, :])
            s = _sigmoid(gg)
            silu = gg * s
            a_ref[:, c0:c0 + cw] = (silu * uu).astype(BF16)
            du = (da * silu).astype(BF16)
            dg = (da * uu * (s * (1.0 + gg * (1.0 - s)))).astype(BF16)
            du_ref[:, c0:c0 + cw] = du
            dg_ref[:, c0:c0 + cw] = dg
            part = _dot(dg, wg_ref[c0:c0 + cw, :]) + _dot(du, wu_ref[c0:c0 + cw, :])
            acc = part if acc is None else acc + part

        @pl.when(j == 0)
        def _():
            dx_ref[...] = acc

        @pl.when(j > 0)
        def _():
            dx_ref[...] += acc

    blk = pl.BlockSpec((tm, tf), lambda i, j: (i, j))
    return pl.pallas_call(
        body, grid=(T // tm, 2), name=name,
        in_specs=[pl.BlockSpec((tm, D), lambda i, j: (i, 0)), blk, blk,
                  pl.BlockSpec((None, tf, D), lambda i, j: (0, j, 0)),
                  pl.BlockSpec((None, tf, D), lambda i, j: (1, j, 0)),
                  pl.BlockSpec((None, tf, D), lambda i, j: (2, j, 0))],
        out_specs=[blk, blk, blk, pl.BlockSpec((tm, D), lambda i, j: (i, 0))],
        out_shape=[jax.ShapeDtypeStruct((T, F), BF16), jax.ShapeDtypeStruct((T, F), BF16),
                   jax.ShapeDtypeStruct((T, F), BF16), jax.ShapeDtypeStruct((T, D), F32)],
        compiler_params=_params(("parallel", "arbitrary")))(dys, g, u, wf, wf, wf)


def _swap_halves(x):
    w = x.shape[1]
    lane = lax.broadcasted_iota(jnp.int32, (1, w), 1)
    return jnp.where((lane & 63) < 32, pltpu.roll(x, w - 32, 1), pltpu.roll(x, 32, 1))


def _rope(x, cos, sin):
    return x * cos + _swap_halves(x) * sin


def _rope_t(dy, cos, sin):
    return dy * cos + _swap_halves(dy * sin)


def _rope_tables(n_lat):
    rows = n_lat // GRID_W
    row = jnp.repeat(jnp.arange(rows, dtype=F32), GRID_W)
    col = jnp.tile(jnp.arange(GRID_W, dtype=F32), rows)
    inv = ROPE_THETA ** (-jnp.arange(ROPE_FREQS, dtype=F32) / ROPE_FREQS)
    ang = jnp.concatenate([row[:, None] * inv, col[:, None] * inv], axis=-1)
    cs, sn = jnp.cos(ang), jnp.sin(ang)
    cos = jnp.concatenate([cs, cs, cs, cs], axis=-1)
    sin = jnp.concatenate([-sn, sn, -sn, sn], axis=-1)
    return cos, sin


def _attn_specs(cfg):
    n_lat, n_ctx, cb = cfg.n_lat, cfg.n_ctx, cfg.ctx_blk
    return [pl.BlockSpec((n_lat, ATT_W), lambda e: (e, 0)),
            pl.BlockSpec((n_lat, 128), lambda e: (e, 4)),
            pl.BlockSpec((n_lat, 128), lambda e: (e, 5)),
            pl.BlockSpec((n_ctx, ATT_W), lambda e: (cb + e, 0)),
            pl.BlockSpec((n_ctx, 128), lambda e: (cb + e, 4)),
            pl.BlockSpec((n_ctx, 128), lambda e: (cb + e, 5)),
            pl.BlockSpec((n_lat, 128), lambda e: (0, 0)),
            pl.BlockSpec((n_lat, 128), lambda e: (0, 0)),
            pl.BlockSpec((8, 128), lambda e: (0, 0))]


def _attn_prepare(kh, kl, vl, kc, vc, ka, kb, va, vb, kca, kcb, vca, vcb):
    lane = lax.broadcasted_iota(jnp.int32, (1, 128), 1)
    own = (lane < 64) if kh == 0 else (lane >= 64)

    def split(x, ra, rb):
        mine = jnp.where(own, x, 0.0)
        other = pltpu.roll(mine, 64, 1)
        a, b = (mine, other) if kh == 0 else (other, mine)
        ra[...] = a.astype(BF16)
        rb[...] = b.astype(BF16)

    split(kl, ka, kb)
    split(vl, va, vb)
    split(kc, kca, kcb)
    split(vc, vca, vcb)


def _softmax_parts(s_list, sk):
    m = sk
    for s in s_list:
        m = jnp.maximum(m, jnp.max(s, axis=1, keepdims=True))
    es = [jnp.exp(s - m) for s in s_list]
    esk = jnp.exp(sk - m)
    den = esk
    for e in es:
        den = den + jnp.sum(e, axis=1, keepdims=True)
    inv = 1.0 / den
    return [e * inv for e in es], esk * inv


def _window(cfg, n):
    r0 = pl.multiple_of(n * BLK, BLK)
    start = pl.multiple_of(jnp.clip((n - 1) * BLK, 0, cfg.n_lat - 3 * BLK), BLK)
    qpos = r0 + lax.broadcasted_iota(jnp.int32, (BLK, 1), 0)
    kpos = start + lax.broadcasted_iota(jnp.int32, (1, 3 * BLK), 1)
    valid = jnp.abs(qpos - kpos) <= BLK
    return r0, start, valid


def _attn_fwd(cfg, p, cos, sin, sink_rows, name):
    n_lat, n_ctx = cfg.n_lat, cfg.n_ctx

    def body(q_ref, k_ref, v_ref, qc_ref, kc_ref, vc_ref, cos_ref, sin_ref, sink_ref, o_ref, oc_ref,
             qr, ka, kb, va, vb, kca, kcb, vca, vcb):
        cos_t, sin_t = cos_ref[...], sin_ref[...]
        for gq in range(4):
            qr[:, gq * 128:(gq + 1) * 128] = _rope(q_ref[:, gq * 128:(gq + 1) * 128], cos_t, sin_t).astype(BF16)
        kl = _rope(k_ref[...], cos_t, sin_t)
        for kh in range(KV_HEADS):
            _attn_prepare(kh, kl, v_ref[...], kc_ref[...], vc_ref[...], ka, kb, va, vb, kca, kcb, vca, vcb)

            def lat_block(n, carry):
                r0, start, valid = _window(cfg, n)
                win = pl.ds(start, 3 * BLK)
                for pr in range(2):
                    lanes = slice((kh * 2 + pr) * 128, (kh * 2 + pr + 1) * 128)
                    qp = qr[pl.ds(r0, BLK), lanes]
                    o = None
                    for half, (kw, kcx, vw, vcx) in enumerate(((ka, kca, va, vca), (kb, kcb, vb, vcb))):
                        head = kh * 4 + pr * 2 + half
                        s_w = jnp.where(valid, _dot_nt(qp, kw[win, :]) * ATT_SCALE, NEG_INF)
                        s_c = _dot_nt(qp, kcx[...]) * ATT_SCALE
                        (p_w, p_c), _ = _softmax_parts([s_w, s_c], sink_ref[head:head + 1, 0:1])
                        part = _dot(p_w, vw[win, :]) + _dot(p_c, vcx[...])
                        o = part if o is None else o + part
                    o_ref[pl.ds(r0, BLK), lanes] = o.astype(BF16)
                return carry

            lax.fori_loop(0, n_lat // BLK, lat_block, 0)
            for n in range(n_ctx // BLK):
                rows = slice(n * BLK, (n + 1) * BLK)
                for pr in range(2):
                    lanes = slice((kh * 2 + pr) * 128, (kh * 2 + pr + 1) * 128)
                    qp = qc_ref[rows, lanes]
                    o = None
                    for half, (kcx, vcx) in enumerate(((kca, vca), (kcb, vcb))):
                        head = kh * 4 + pr * 2 + half
                        s_c = _dot_nt(qp, kcx[...]) * ATT_SCALE
                        (p_c,), _ = _softmax_parts([s_c], sink_ref[head:head + 1, 0:1])
                        part = _dot(p_c, vcx[...])
                        o = part if o is None else o + part
                    oc_ref[rows, lanes] = o.astype(BF16)

    return pl.pallas_call(
        body, grid=(2,), name=name, in_specs=_attn_specs(cfg),
        out_specs=[pl.BlockSpec((n_lat, ATT_W), lambda e: (e, 0)), pl.BlockSpec((n_ctx, ATT_W), lambda e: (e, 0))],
        out_shape=[jax.ShapeDtypeStruct((cfg.t_lat, ATT_W), BF16), jax.ShapeDtypeStruct((cfg.t_ctx, ATT_W), BF16)],
        scratch_shapes=[pltpu.VMEM((n_lat, ATT_W), BF16)] + [pltpu.VMEM((n_lat, 128), BF16)] * 4
        + [pltpu.VMEM((n_ctx, 128), BF16)] * 4,
        compiler_params=_params(("parallel",)))(p, p, p, p, p, p, cos, sin, sink_rows)


def _attn_bwd(cfg, p, dcat, cos, sin, sink_rows, name):
    n_lat, n_ctx, cb = cfg.n_lat, cfg.n_ctx, cfg.ctx_blk

    def body(q_ref, k_ref, v_ref, qc_ref, kc_ref, vc_ref, cos_ref, sin_ref, sink_ref, do_ref, doc_ref,
             dq_ref, dk_ref, dv_ref, dqc_ref, dkc_ref, dvc_ref, dsink_ref,
             qr, ka, kb, va, vb, kca, kcb, vca, vcb, dqs, dka, dva, dkca, dvca):
        cos_t, sin_t = cos_ref[...], sin_ref[...]
        lane = lax.broadcasted_iota(jnp.int32, (1, 128), 1)
        lo = lane < 64
        for gq in range(4):
            qr[:, gq * 128:(gq + 1) * 128] = _rope(q_ref[:, gq * 128:(gq + 1) * 128], cos_t, sin_t).astype(BF16)
        kl = _rope(k_ref[...], cos_t, sin_t)
        dsink_ref[...] = jnp.zeros_like(dsink_ref)
        dka[...] = jnp.zeros_like(dka)
        dva[...] = jnp.zeros_like(dva)
        dkca[...] = jnp.zeros_like(dkca)
        dvca[...] = jnp.zeros_like(dvca)

        def halves(x):
            return jnp.where(lo, x, 0).astype(BF16), jnp.where(lo, 0, x).astype(BF16)

        for kh in range(KV_HEADS):
            _attn_prepare(kh, kl, v_ref[...], kc_ref[...], vc_ref[...], ka, kb, va, vb, kca, kcb, vca, vcb)

            def one_head(head, qp, q_half, do_p, do_half, kw, kcx, vw, vcx, win, valid):
                sk = sink_ref[head:head + 1, 0:1]
                s_list = [_dot_nt(qp, kcx[...]) * ATT_SCALE]
                if win is not None:
                    s_list.insert(0, jnp.where(valid, _dot_nt(qp, kw[win, :]) * ATT_SCALE, NEG_INF))
                probs, p_sink = _softmax_parts(s_list, sk)
                vals = [vcx[...]] if win is None else [vw[win, :], vcx[...]]
                dps = [_dot_nt(do_p, vv) for vv in vals]
                dr = None
                for pp, dp in zip(probs, dps):
                    t = jnp.sum(pp * dp, axis=1, keepdims=True)
                    dr = t if dr is None else dr + t
                dss = [(pp * (dp - dr) * ATT_SCALE).astype(BF16) for pp, dp in zip(probs, dps)]
                dsink_ref[head:head + 1, :] += jnp.broadcast_to(
                    jnp.sum(-p_sink * dr, axis=0, keepdims=True), (1, 128))
                p_c, ds_c = probs[-1], dss[-1]
                dq = _dot(ds_c, kcx[...])
                dkca[kh] += _dot_tn(ds_c, q_half)
                dvca[kh] += _dot_tn(p_c, do_half)
                if win is not None:
                    dq = dq + _dot(dss[0], kw[win, :])
                    dka[kh, win, :] += _dot_tn(dss[0], q_half)
                    dva[kh, win, :] += _dot_tn(probs[0], do_half)
                return dq

            def lat_block(n, carry):
                r0, start, valid = _window(cfg, n)
                win = pl.ds(start, 3 * BLK)
                for pr in range(2):
                    lanes = slice((kh * 2 + pr) * 128, (kh * 2 + pr + 1) * 128)
                    qp = qr[pl.ds(r0, BLK), lanes]
                    do_p = do_ref[pl.ds(r0, BLK), lanes]
                    q_h, do_h = halves(qp), halves(do_p)
                    dq = None
                    for half, (kw, kcx, vw, vcx) in enumerate(((ka, kca, va, vca), (kb, kcb, vb, vcb))):
                        part = one_head(kh * 4 + pr * 2 + half, qp, q_h[half], do_p, do_h[half],
                                        kw, kcx, vw, vcx, win, valid)
                        dq = part if dq is None else dq + part
                    dqs[pl.ds(r0, BLK), lanes] = dq
                return carry

            lax.fori_loop(0, n_lat // BLK, lat_block, 0)
            for n in range(n_ctx // BLK):
                rows = slice(n * BLK, (n + 1) * BLK)
                for pr in range(2):
                    lanes = slice((kh * 2 + pr) * 128, (kh * 2 + pr + 1) * 128)
                    qp = qc_ref[rows, lanes].astype(BF16)
                    do_p = doc_ref[rows, lanes]
                    q_h, do_h = halves(qp), halves(do_p)
                    dq = None
                    for half, (kcx, vcx) in enumerate(((kca, vca), (kcb, vcb))):
                        part = one_head(kh * 4 + pr * 2 + half, qp, q_h[half], do_p, do_h[half],
                                        None, kcx, None, vcx, None, None)
                        dq = part if dq is None else dq + part
                    dqc_ref[rows, lanes] = dq.astype(BF16)

        def fold(acc):
            r0 = acc[0] + pltpu.roll(acc[0], 64, 1)
            r1 = acc[1] + pltpu.roll(acc[1], 64, 1)
            return jnp.where(lo, r0, r1)

        for gq in range(4):
            sl = slice(gq * 128, (gq + 1) * 128)
            dq_ref[:, sl] = _rope_t(dqs[:, sl], cos_t, sin_t).astype(BF16)
        dk_ref[...] = _rope_t(fold(dka), cos_t, sin_t).astype(BF16)
        dv_ref[...] = fold(dva).astype(BF16)
        dkc_ref[...] = fold(dkca).astype(BF16)
        dvc_ref[...] = fold(dvca).astype(BF16)

    lat = lambda w: pl.BlockSpec((n_lat, w), lambda e: (e, 0))
    ctx = lambda w: pl.BlockSpec((n_ctx, w), lambda e: (e, 0))
    sd = jax.ShapeDtypeStruct
    return pl.pallas_call(
        body, grid=(2,), name=name,
        in_specs=_attn_specs(cfg) + [pl.BlockSpec((n_lat, ATT_W), lambda e: (e, 0)),
                                     pl.BlockSpec((n_ctx, ATT_W), lambda e: (cb + e, 0))],
        out_specs=[lat(ATT_W), lat(128), lat(128), ctx(ATT_W), ctx(128), ctx(128),
                   pl.BlockSpec((None, 8, 128), lambda e: (e, 0, 0))],
        out_shape=[sd((cfg.t_lat, ATT_W), BF16), sd((cfg.t_lat, 128), BF16), sd((cfg.t_lat, 128), BF16),
                   sd((cfg.t_ctx, ATT_W), BF16), sd((cfg.t_ctx, 128), BF16), sd((cfg.t_ctx, 128), BF16),
                   sd((2, 8, 128), F32)],
        scratch_shapes=[pltpu.VMEM((n_lat, ATT_W), BF16)] + [pltpu.VMEM((n_lat, 128), BF16)] * 4
        + [pltpu.VMEM((n_ctx, 128), BF16)] * 4
        + [pltpu.VMEM((n_lat, ATT_W), F32), pltpu.VMEM((2, n_lat, 128), F32), pltpu.VMEM((2, n_lat, 128), F32),
           pltpu.VMEM((2, n_ctx, 128), F32), pltpu.VMEM((2, n_ctx, 128), F32)],
        compiler_params=_params(("parallel",)))(p, p, p, p, p, p, cos, sin, sink_rows, dcat, dcat)


def _shift_down(x, k, row):
    return jnp.where(row >= k, pltpu.roll(x, k, 0), 0.0)


def _shift_up(x, k, row):
    n = x.shape[0]
    return jnp.where(row < n - k, pltpu.roll(x, n - k, 0), 0.0)


def _window_sum(x, r, row):
    below, above, k = x, x, 1
    while k < r:
        below = below + _shift_down(below, k, row)
        above = above + _shift_up(above, k, row)
        k *= 2
    return below + _shift_down(x, r, row) + _shift_up(above, 1, row)


def _inv_count(r, row, n):
    cnt = jnp.minimum(row + r, n - 1) + 1 - jnp.maximum(row - r, 0)
    return 1.0 / cnt.astype(F32)


def _pool_fwd(p, w, scale, n, blk0, n_seg, name):
    def body(u0, u1, u2, u3, w_ref, sc_ref, o_ref):
        row = lax.broadcasted_iota(jnp.int32, (n, 1), 0)
        for g, u_ref in enumerate((u0, u1, u2, u3)):
            u = u_ref[...]
            d = _window_sum(u, POOL_R[g], row) * _inv_count(POOL_R[g], row, n) - u
            o_ref[:, g * 128:(g + 1) * 128] = (_dot(d, w_ref[g]) * sc_ref[:, g * 128:(g + 1) * 128]).astype(BF16)

    return pl.pallas_call(
        body, grid=(n_seg,), name=name,
        in_specs=[pl.BlockSpec((n, 128), functools.partial(lambda g, e: (blk0 + e, 6 + g), g)) for g in range(4)]
        + [pl.BlockSpec((4, 128, 128), lambda e: (0, 0, 0)), pl.BlockSpec((1, 512), lambda e: (0, 0))],
        out_specs=pl.BlockSpec((n, 512), lambda e: (e, 0)),
        out_shape=jax.ShapeDtypeStruct((n_seg * n, 512), BF16),
        compiler_params=_params(("parallel",)))(p, p, p, p, w, scale)


def _pool_bwd(p, w, scale, dcat, n, blk0, n_seg, name):
    def body(u0, u1, u2, u3, w_ref, sc_ref, dp_ref, du_ref, dw_ref, dsc_ref):
        e = pl.program_id(0)

        @pl.when(e == 0)
        def _():
            dw_ref[...] = jnp.zeros_like(dw_ref)
            dsc_ref[...] = jnp.zeros_like(dsc_ref)

        row = lax.broadcasted_iota(jnp.int32, (n, 1), 0)
        for g, u_ref in enumerate((u0, u1, u2, u3)):
            sl = slice(g * 128, (g + 1) * 128)
            u = u_ref[...]
            inv = _inv_count(POOL_R[g], row, n)
            d = _window_sum(u, POOL_R[g], row) * inv - u
            dp = dp_ref[:, sl]
            dsc_ref[:, sl] += jnp.sum(dp * _dot(d, w_ref[g]), axis=0, keepdims=True)
            dyp = dp * sc_ref[:, sl]
            dw_ref[g] += _dot_tn(d, dyp)
            dd = _dot_nt(dyp, w_ref[g])
            du_ref[:, sl] = (_window_sum(dd * inv, POOL_R[g], row) - dd).astype(BF16)

    return pl.pallas_call(
        body, grid=(n_seg,), name=name,
        in_specs=[pl.BlockSpec((n, 128), functools.partial(lambda g, e: (blk0 + e, 6 + g), g)) for g in range(4)]
        + [pl.BlockSpec((4, 128, 128), lambda e: (0, 0, 0)), pl.BlockSpec((1, 512), lambda e: (0, 0)),
           pl.BlockSpec((n, 512), lambda e: (blk0 + e, 1))],
        out_specs=[pl.BlockSpec((n, 512), lambda e: (e, 0)),
                   pl.BlockSpec((4, 128, 128), lambda e: (0, 0, 0)), pl.BlockSpec((1, 512), lambda e: (0, 0))],
        out_shape=[jax.ShapeDtypeStruct((n_seg * n, 512), BF16), jax.ShapeDtypeStruct((4, 128, 128), F32),
                   jax.ShapeDtypeStruct((1, 512), F32)],
        compiler_params=_params(("arbitrary",)))(p, p, p, p, w, scale, dcat)


def _gelu(x):
    t = jnp.tanh(math.sqrt(2.0 / math.pi) * (x + 0.044715 * x * x * x))
    return 0.5 * x * (1.0 + t), t


def _gelu_grad(x, t):
    return 0.5 * (1.0 + t) + 0.5 * x * (1.0 - t * t) * (math.sqrt(2.0 / math.pi) * (1.0 + 3 * 0.044715 * x * x))


def _neg_expm1(x):
    series = -x * (1.0 + x * (0.5 + x * (1.0 / 6.0 + x * (1.0 / 24.0 + x * (1.0 / 120.0)))))
    return jnp.where(x > -0.05, series, 1.0 - jnp.exp(x))


def _softplus_neg(lam):
    x = -lam
    e = jnp.exp(-jnp.abs(x))
    log1p = jnp.where(e < 1e-2, e * (1.0 - e * (0.5 - e * (1.0 / 3.0))), jnp.log(1.0 + e))
    return jnp.maximum(x, 0.0) + log1p, -_sigmoid(x)


def _conv(u, w_ref, b_ref, row):
    return (b_ref[...] + _shift_down(u, 1, row) * w_ref[0:1, :] + u * w_ref[1:2, :]
            + _shift_up(u, 1, row) * w_ref[2:3, :] + _shift_up(u, 2, row) * w_ref[3:4, :])


def _lru_gates(uc, d, wa_ref, ba_ref, wx_ref, bx_ref, lam_ref):
    r = _sigmoid(_dot(uc, wa_ref[d]) + ba_ref[d:d + 1, :])
    gi = _sigmoid(_dot(uc, wx_ref[d]) + bx_ref[d:d + 1, :])
    sp, dsp = _softplus_neg(lam_ref[d:d + 1, :])
    la = (-LRU_C) * r * sp
    a = jnp.exp(la)
    sq = jnp.sqrt(_neg_expm1(2.0 * la))
    return r, gi, sp, dsp, a, sq


def _tile_scan(a, b, reverse):
    n = a.shape[0]
    row8 = lax.broadcasted_iota(jnp.int32, (n, 1), 0) & 7
    for k in (1, 2, 4):
        if reverse:
            m = row8 < 8 - k
            a_sh = jnp.where(m, pltpu.roll(a, n - k, 0), 1.0)
            b_sh = jnp.where(m, pltpu.roll(b, n - k, 0), 0.0)
        else:
            m = row8 >= k
            a_sh = jnp.where(m, pltpu.roll(a, k, 0), 1.0)
            b_sh = jnp.where(m, pltpu.roll(b, k, 0), 0.0)
        b = a * b_sh + b
        a = a * a_sh
    return a, b


def _carry_scan(a_ref, b_ref, n, reverse, carry):
    nt8 = n // 8

    def step(i, c):
        t = (nt8 - 1 - i) if reverse else i
        off = pl.multiple_of(t * 8, 8)
        h = a_ref[pl.ds(off, 8), :] * c + b_ref[pl.ds(off, 8), :]
        b_ref[pl.ds(off, 8), :] = h
        return h[0:1, :] if reverse else h[7:8, :]

    return lax.fori_loop(0, nt8, step, carry)


def _chain_scan(segs, reverse):
    carry = jnp.zeros((1, 128), F32)
    for a, b, a_ref, b_ref, n in segs:
        a2, b2 = _tile_scan(a, b, reverse)
        a_ref[...] = a2
        b_ref[...] = b2
        carry = _carry_scan(a_ref, b_ref, n, reverse, carry)


def _lru_specs(cfg):
    n_lat, n_ctx, cb = cfg.n_lat, cfg.n_ctx, cfg.ctx_blk
    return [pl.BlockSpec((n_lat, 128), lambda hb, e: (e, hb)),
            pl.BlockSpec((n_lat, 128), lambda hb, e: (e, 8 + hb)),
            pl.BlockSpec((n_ctx, 128), lambda hb, e: (cb + e, hb)),
            pl.BlockSpec((n_ctx, 128), lambda hb, e: (cb + e, 8 + hb)),
            pl.BlockSpec((4, 128), lambda hb, e: (0, hb)),
            pl.BlockSpec((1, 128), lambda hb, e: (0, hb)),
            pl.BlockSpec((2, None, 128, 128), lambda hb, e: (0, hb, 0, 0)),
            pl.BlockSpec((2, 128), lambda hb, e: (0, hb)),
            pl.BlockSpec((2, None, 128, 128), lambda hb, e: (0, hb, 0, 0)),
            pl.BlockSpec((2, 128), lambda hb, e: (0, hb)),
            pl.BlockSpec((2, 128), lambda hb, e: (0, hb))]


def _lru_fwd(cfg, p, consts, name):
    n_lat, n_ctx = cfg.n_lat, cfg.n_ctx

    def body(gl_ref, ul_ref, gc_ref, uc_ref, cw_ref, cb_ref, wa_ref, ba_ref, wx_ref, bx_ref, lam_ref,
             zl_ref, zc_ref, al, bl, ac, bc):
        row_l = lax.broadcasted_iota(jnp.int32, (n_lat, 1), 0)
        row_c = lax.broadcasted_iota(jnp.int32, (n_ctx, 1), 0)
        uc_l = _conv(ul_ref[...], cw_ref, cb_ref, row_l)
        uc_c = _conv(uc_ref[...], cw_ref, cb_ref, row_c)
        y_l = y_c = None
        for d in range(2):
            _, gi_l, _, _, a_l, sq_l = _lru_gates(uc_l, d, wa_ref, ba_ref, wx_ref, bx_ref, lam_ref)
            _, gi_c, _, _, a_c, sq_c = _lru_gates(uc_c, d, wa_ref, ba_ref, wx_ref, bx_ref, lam_ref)
            _chain_scan([(a_c, sq_c * (gi_c * uc_c), ac, bc, n_ctx), (a_l, sq_l * (gi_l * uc_l), al, bl, n_lat)],
                        reverse=(d == 1))
            y_l = bl[...] if y_l is None else y_l + bl[...]
            y_c = bc[...] if y_c is None else y_c + bc[...]
        zl_ref[...] = (_gelu(gl_ref[...])[0] * y_l).astype(BF16)
        zc_ref[...] = (_gelu(gc_ref[...])[0] * y_c).astype(BF16)

    return pl.pallas_call(
        body, grid=(8, 2), name=name, in_specs=_lru_specs(cfg),
        out_specs=[pl.BlockSpec((n_lat, 128), lambda hb, e: (e, hb)), pl.BlockSpec((n_ctx, 128), lambda hb, e: (e, hb))],
        out_shape=[jax.ShapeDtypeStruct((cfg.t_lat, D), BF16), jax.ShapeDtypeStruct((cfg.t_ctx, D), BF16)],
        scratch_shapes=[pltpu.VMEM((n_lat, 128), F32)] * 2 + [pltpu.VMEM((n_ctx, 128), F32)] * 2,
        compiler_params=_params(("parallel", "arbitrary")))(p, p, p, p, *consts)


def _lru_bwd(cfg, p, dz, consts, name):
    n_lat, n_ctx, cb = cfg.n_lat, cfg.n_ctx, cfg.ctx_blk

    def body(gl_ref, ul_ref, gc_ref, uc_ref, cw_ref, cb_ref, wa_ref, ba_ref, wx_ref, bx_ref, lam_ref,
             dzl_ref, dzc_ref, dgl_ref, dul_ref, dgc_ref, duc_ref, dwa_ref, dwx_ref, vec_ref,
             al, bl, ac, bc, hl, hc):
        e = pl.program_id(1)

        @pl.when(e == 0)
        def _():
            dwa_ref[...] = jnp.zeros_like(dwa_ref)
            dwx_ref[...] = jnp.zeros_like(dwx_ref)
            vec_ref[...] = jnp.zeros_like(vec_ref)

        row_l = lax.broadcasted_iota(jnp.int32, (n_lat, 1), 0)
        row_c = lax.broadcasted_iota(jnp.int32, (n_ctx, 1), 0)
        u_l, u_c = ul_ref[...], uc_ref[...]
        uc_l = _conv(u_l, cw_ref, cb_ref, row_l)
        uc_c = _conv(u_c, cw_ref, cb_ref, row_c)
        for d in range(2):
            _, gi_l, _, _, a_l, sq_l = _lru_gates(uc_l, d, wa_ref, ba_ref, wx_ref, bx_ref, lam_ref)
            _, gi_c, _, _, a_c, sq_c = _lru_gates(uc_c, d, wa_ref, ba_ref, wx_ref, bx_ref, lam_ref)
            _chain_scan([(a_c, sq_c * (gi_c * uc_c), ac, bc, n_ctx), (a_l, sq_l * (gi_l * uc_l), al, bl, n_lat)],
                        reverse=(d == 1))
            hl[d] = bl[...]
            hc[d] = bc[...]
        gel_l, t_l = _gelu(gl_ref[...])
        gel_c, t_c = _gelu(gc_ref[...])
        dz_l, dz_c = dzl_ref[...], dzc_ref[...]
        dgl_ref[...] = (dz_l * (hl[0] + hl[1]) * _gelu_grad(gl_ref[...], t_l)).astype(BF16)
        dgc_ref[...] = (dz_c * (hc[0] + hc[1]) * _gelu_grad(gc_ref[...], t_c)).astype(BF16)
        dy_l, dy_c = dz_l * gel_l, dz_c * gel_c
        duc_l = jnp.zeros((n_lat, 128), F32)
        duc_c = jnp.zeros((n_ctx, 128), F32)
        for d in range(2):
            r_l, gi_l, sp, dsp, a_l, sq_l = _lru_gates(uc_l, d, wa_ref, ba_ref, wx_ref, bx_ref, lam_ref)
            r_c, gi_c, _, _, a_c, sq_c = _lru_gates(uc_c, d, wa_ref, ba_ref, wx_ref, bx_ref, lam_ref)
            if d == 0:
                an_l = _shift_up(a_l, 1, row_l)
                an_c = jnp.where(row_c < n_ctx - 1, pltpu.roll(a_c, n_ctx - 1, 0), a_l[0:1, :])
            else:
                an_l = _shift_down(a_l, 1, row_l)
                an_c = jnp.where(row_c >= 1, pltpu.roll(a_c, 1, 0), a_l[n_lat - 1:n_lat, :])
            _chain_scan([(an_l, dy_l, al, bl, n_lat), (an_c, dy_c, ac, bc, n_ctx)], reverse=(d == 0))
            dsp_sum = jnp.zeros((1, 128), F32)
            for (dh, h, r, gi, a, sq, uc, seg) in ((bl[...], hl[d], r_l, gi_l, a_l, sq_l, uc_l, "l"),
                                                  (bc[...], hc[d], r_c, gi_c, a_c, sq_c, uc_c, "c")):
                b0 = sq * (gi * uc)
                t1 = dh * sq
                dla = dh * (h - b0) - (dh * gi * uc) * (a * a) / sq
                dzr = (dla * ((-LRU_C) * sp)) * r * (1.0 - r)
                dzi = (t1 * uc) * gi * (1.0 - gi)
                dsp_sum = dsp_sum + jnp.sum(dla * ((-LRU_C) * r), axis=0, keepdims=True)
                dwa_ref[d] += _dot_tn(uc, dzr)
                dwx_ref[d] += _dot_tn(uc, dzi)
                vec_ref[d:d + 1, :] += jnp.sum(dzr, axis=0, keepdims=True)
                vec_ref[2 + d:3 + d, :] += jnp.sum(dzi, axis=0, keepdims=True)
                duc = t1 * gi + _dot_nt(dzr, wa_ref[d]) + _dot_nt(dzi, wx_ref[d])
                if seg == "l":
                    duc_l = duc_l + duc
                else:
                    duc_c = duc_c + duc
            vec_ref[4 + d:5 + d, :] += dsp_sum * dsp
        for duc, u, row, du_ref in ((duc_l, u_l, row_l, dul_ref), (duc_c, u_c, row_c, duc_ref)):
            du_ref[...] = (_shift_up(duc, 1, row) * cw_ref[0:1, :] + duc * cw_ref[1:2, :]
                           + _shift_down(duc, 1, row) * cw_ref[2:3, :]
                           + _shift_down(duc, 2, row) * cw_ref[3:4, :]).astype(BF16)
            vec_ref[6:7, :] += jnp.sum(duc * _shift_down(u, 1, row), axis=0, keepdims=True)
            vec_ref[7:8, :] += jnp.sum(duc * u, axis=0, keepdims=True)
            vec_ref[8:9, :] += jnp.sum(duc * _shift_up(u, 1, row), axis=0, keepdims=True)
            vec_ref[9:10, :] += jnp.sum(duc * _shift_up(u, 2, row), axis=0, keepdims=True)
            vec_ref[10:11, :] += jnp.sum(duc, axis=0, keepdims=True)

    lat = pl.BlockSpec((n_lat, 128), lambda hb, e: (e, hb))
    ctx = pl.BlockSpec((n_ctx, 128), lambda hb, e: (e, hb))
    wspec = pl.BlockSpec((2, None, 128, 128), lambda hb, e: (0, hb, 0, 0))
    sd = jax.ShapeDtypeStruct
    return pl.pallas_call(
        body, grid=(8, 2), name=name,
        in_specs=_lru_specs(cfg) + [pl.BlockSpec((n_lat, 128), lambda hb, e: (e, hb)),
                                    pl.BlockSpec((n_ctx, 128), lambda hb, e: (cb + e, hb))],
        out_specs=[lat, lat, ctx, ctx, wspec, wspec, pl.BlockSpec((None, 16, 128), lambda hb, e: (hb, 0, 0))],
        out_shape=[sd((cfg.t_lat, D), BF16), sd((cfg.t_lat, D), BF16), sd((cfg.t_ctx, D), BF16), sd((cfg.t_ctx, D), BF16),
                   sd((2, 8, 128, 128), F32), sd((2, 8, 128, 128), F32), sd((8, 16, 128), F32)],
        scratch_shapes=[pltpu.VMEM((n_lat, 128), F32)] * 2 + [pltpu.VMEM((n_ctx, 128), F32)] * 2
        + [pltpu.VMEM((2, n_lat, 128), F32), pltpu.VMEM((2, n_ctx, 128), F32)],
        compiler_params=_params(("parallel", "arbitrary")))(p, p, p, p, *consts, dz, dz)


def _position():
    x, y, c = lax.axis_index("x"), lax.axis_index("y"), lax.axis_index("c")
    return x, y, c, 4 * x + 2 * y + c


def _peer(x, y, c, k):
    px = 1 - x if k & 4 else x
    py = 1 - y if k & 2 else y
    pc = 1 - c if k & 1 else c
    return (px, py, pc), 4 * px + 2 * py + pc


def _all_gather(v, name, in_vmem):
    def body(v_ref, o_ref, send_sems, recv_sems, local_sem):
        x, y, c, me = _position()
        mine = pltpu.make_async_copy(v_ref, o_ref.at[me], local_sem)
        mine.start()
        sends = []
        for k in range(1, N_DEV):
            peer, _ = _peer(x, y, c, k)
            cp = pltpu.make_async_remote_copy(src_ref=v_ref, dst_ref=o_ref.at[me], send_sem=send_sems.at[k - 1],
                                              recv_sem=recv_sems.at[k - 1], device_id=peer, device_id_type=MESH)
            cp.start()
            sends.append(cp)
        for k in range(1, N_DEV):
            peer, peer_lin = _peer(x, y, c, k)
            pltpu.make_async_remote_copy(src_ref=v_ref, dst_ref=o_ref.at[peer_lin], send_sem=send_sems.at[k - 1],
                                         recv_sem=recv_sems.at[k - 1], device_id=peer, device_id_type=MESH).wait_recv()
        for cp in sends:
            cp.wait_send()
        mine.wait()

    space = pltpu.VMEM if in_vmem else pl.ANY
    return pl.pallas_call(
        body, name=name,
        in_specs=[pl.BlockSpec(memory_space=space)], out_specs=pl.BlockSpec(memory_space=space),
        out_shape=jax.ShapeDtypeStruct((N_DEV,) + v.shape, v.dtype),
        scratch_shapes=[pltpu.SemaphoreType.DMA((N_DEV - 1,)), pltpu.SemaphoreType.DMA((N_DEV - 1,)),
                        pltpu.SemaphoreType.DMA],
        compiler_params=pltpu.CompilerParams(vmem_limit_bytes=VMEM_LIMIT))(v)


_HBM = pl.BlockSpec(memory_space=pltpu.HBM)
_SEM = pl.BlockSpec(memory_space=pltpu.SEMAPHORE)
_EFFECT = pltpu.SideEffectType.DATAFLOW_SIDE_EFFECTING


def _push_start(src, land, block_of, name):
    def body(src_ref, land_ref, send_sem, recv_sem, src_thru, land_thru, token):
        x, y, c, me = _position()
        for k in range(1, N_DEV):
            peer, peer_lin = _peer(x, y, c, k)
            mine, there = block_of(src_ref, land_ref, me, peer_lin)
            pltpu.make_async_remote_copy(src_ref=mine, dst_ref=there, send_sem=send_sem, recv_sem=recv_sem,
                                         device_id=peer, device_id_type=MESH).start()
        token[...] = jnp.zeros_like(token)

    return pl.pallas_call(
        body, name=name,
        out_shape=(pltpu.SemaphoreType.DMA(()), pltpu.SemaphoreType.DMA(()), pltpu.HBM(src.shape, src.dtype),
                   pltpu.HBM(land.shape, land.dtype), jax.ShapeDtypeStruct((8, 128), F32)),
        in_specs=(_HBM, _HBM), out_specs=(_SEM, _SEM, _HBM, _HBM, pl.BlockSpec(memory_space=pltpu.VMEM)),
        input_output_aliases={0: 2, 1: 3},
        compiler_params=pltpu.CompilerParams(has_side_effects=_EFFECT),
    )(pltpu.with_memory_space_constraint(src, pltpu.HBM), pltpu.with_memory_space_constraint(land, pltpu.HBM))


def _push_wait(handle, seven_of, after, name):
    send_sem, recv_sem, src_thru, land_thru, _ = handle

    def body(src_ref, land_ref, send_sem, recv_sem, after_ref, src_dead, got_ref):
        x, y, c, _ = _position()
        seven = seven_of(land_ref)
        cp = pltpu.make_async_remote_copy(src_ref=seven, dst_ref=seven, send_sem=send_sem, recv_sem=recv_sem,
                                          device_id=(x, y, 1 - c), device_id_type=MESH)
        cp.wait_send()
        cp.wait_recv()

    return pl.pallas_call(
        body, name=name,
        out_shape=(pltpu.HBM(src_thru.shape, src_thru.dtype), pltpu.HBM(land_thru.shape, land_thru.dtype)),
        in_specs=(_HBM, _HBM, _SEM, _SEM, pl.BlockSpec(memory_space=pl.ANY)), out_specs=(_HBM, _HBM),
        input_output_aliases={0: 0, 1: 1},
        compiler_params=pltpu.CompilerParams(has_side_effects=_EFFECT),
    )(src_thru, land_thru, send_sem, recv_sem, after)[1]


def _gather_start(src, me, name):
    g, r, C = src.shape
    land = lax.dynamic_update_slice(lax.empty((g, N_DEV * r, C), src.dtype), src, (0, me * r, 0))
    return _push_start(src, land, lambda s, z, i, p: (s, z.at[:, pl.ds(i * r, r), :]), name)


def _gather_wait(handle, after, name):
    r = handle[2].shape[1]
    return _push_wait(handle, lambda z: z.at[:, pl.ds(0, (N_DEV - 1) * r), :], after, name)


def _exchange_start(grad, me, name):
    g, rows, C = grad.shape
    r = rows // N_DEV
    mine = lax.dynamic_slice_in_dim(grad, me * r, r, axis=1)[None]
    land = lax.dynamic_update_slice(lax.empty((N_DEV, g, r, C), grad.dtype), mine, (me, 0, 0, 0))
    return _push_start(grad, land, lambda s, z, i, p: (s.at[:, pl.ds(p * r, r), :], z.at[i]), name)


def _exchange_wait(handle, after, name):
    return _push_wait(handle, lambda z: z.at[pl.ds(0, N_DEV - 1)], after, name)


def _sum_blocks(v, name):
    k, rows, cols = v.shape
    tr = rows
    for cand in (rows, 512, 352, 256, 176, 128, 64, 32, 16):
        if rows % cand == 0 and k * cand * cols * v.dtype.itemsize <= 6 * 1024 * 1024:
            tr = cand
            break

    def body(v_ref, o_ref):
        acc = v_ref[0].astype(F32)
        for s in range(1, k):
            acc = acc + v_ref[s].astype(F32)
        o_ref[...] = acc

    return pl.pallas_call(
        body, grid=(rows // tr,), name=name,
        in_specs=[pl.BlockSpec((k, tr, cols), lambda i: (0, i, 0))],
        out_specs=pl.BlockSpec((tr, cols), lambda i: (i, 0)),
        out_shape=jax.ShapeDtypeStruct((rows, cols), F32),
        compiler_params=_params(("parallel",)))(v)


def _adam_math(w, g, m, v):
    m2 = B1 * m + (1.0 - B1) * g
    v2 = B2 * v + (1.0 - B2) * (g * g)
    m_hat = m2 / (1.0 - B1 ** STEP)
    v_hat = v2 / (1.0 - B2 ** STEP)
    return -LR * (m_hat / (jnp.sqrt(v_hat) + EPS) + WD * w), m2, v2


def _adamw(w, g, m, v, name):
    shp = w.shape
    rows, cols = (shp[-2], shp[-1]) if len(shp) >= 2 else (1, shp[-1])
    lead = math.prod(shp[:-2]) if len(shp) > 2 else 1
    fits = [t for t in range(8, rows + 1, 8) if rows % t == 0 and t * cols * 4 <= 2 * 1024 * 1024]
    tr = max(fits) if fits else rows

    def body(w_ref, g_ref, m_ref, v_ref, d_ref, m2_ref, v2_ref):
        d_ref[...], m2_ref[...], v2_ref[...] = _adam_math(w_ref[...], g_ref[...], m_ref[...], v_ref[...])

    blk = pl.BlockSpec((None, tr, cols), lambda b, i: (b, i, 0))
    outs = pl.pallas_call(
        body, grid=(lead, rows // tr), name=name, in_specs=[blk] * 4, out_specs=[blk] * 3,
        out_shape=[jax.ShapeDtypeStruct((lead, rows, cols), F32)] * 3,
        compiler_params=_params(("parallel", "parallel")))(*[a.reshape(lead, rows, cols) for a in (w, g, m, v)])
    return [o.reshape(shp) for o in outs]


def _as2d(a):
    n = a.size
    if n % 1024 == 0:
        return a.reshape(n // 1024, 1024)
    if n % 128 == 0:
        return a.reshape(n // 128, 128)
    return a.reshape(1, n)


def _blocks_to_cols(a):
    b = jnp.moveaxis(a, 0, -2)
    return b.reshape(b.shape[:-2] + (b.shape[-2] * b.shape[-1],))


def _pack_rows(parts):
    padded, offs, r = [], [], 0
    for p in parts:
        pad = (-p.shape[0]) % 8
        padded.append(jnp.pad(p, ((0, pad), (0, 0))) if pad else p)
        offs.append(r)
        r += p.shape[0] + pad
    return jnp.concatenate(padded, axis=0), offs


def _silu(x):
    return x * jax.nn.sigmoid(x)


def kernel(x, c, ctx, c_ctx, w_mod, b_mod, ln_g, ln_b, ffn_w_gate, ffn_w_up, ffn_w_down, mix_ab_w_in, attn_sink, pool_w, pool_scale, mix_ab_w_out, lru_w_in, lru_conv_w, lru_conv_b, lru_wa, lru_ba, lru_wx, lru_bx, lru_lambda, lru_w_out, loss_target, m_c_ctx, m_w_mod, m_b_mod, m_ln_g, m_ln_b, m_ffn_w_gate, m_ffn_w_up, m_ffn_w_down, m_mix_ab_w_in, m_attn_sink, m_pool_w, m_pool_scale, m_mix_ab_w_out, m_lru_w_in, m_lru_conv_w, m_lru_conv_b, m_lru_wa, m_lru_ba, m_lru_wx, m_lru_bx, m_lru_lambda, m_lru_w_out, v_c_ctx, v_w_mod, v_b_mod, v_ln_g, v_ln_b, v_ffn_w_gate, v_ffn_w_up, v_ffn_w_down, v_mix_ab_w_in, v_attn_sink, v_pool_w, v_pool_scale, v_mix_ab_w_out, v_lru_w_in, v_lru_conv_w, v_lru_conv_b, v_lru_wa, v_lru_ba, v_lru_wx, v_lru_bx, v_lru_lambda, v_lru_w_out):
    weights = dict(c_ctx=c_ctx, w_mod=w_mod, b_mod=b_mod, ln_g=ln_g, ln_b=ln_b, ffn_w_gate=ffn_w_gate,
                   ffn_w_up=ffn_w_up, ffn_w_down=ffn_w_down, mix_ab_w_in=mix_ab_w_in, attn_sink=attn_sink,
                   pool_w=pool_w, pool_scale=pool_scale, mix_ab_w_out=mix_ab_w_out, lru_w_in=lru_w_in,
                   lru_conv_w=lru_conv_w, lru_conv_b=lru_conv_b, lru_wa=lru_wa, lru_ba=lru_ba, lru_wx=lru_wx,
                   lru_bx=lru_bx, lru_lambda=lru_lambda, lru_w_out=lru_w_out)
    mom_m = dict(c_ctx=m_c_ctx, w_mod=m_w_mod, b_mod=m_b_mod, ln_g=m_ln_g, ln_b=m_ln_b, ffn_w_gate=m_ffn_w_gate,
                 ffn_w_up=m_ffn_w_up, ffn_w_down=m_ffn_w_down, mix_ab_w_in=m_mix_ab_w_in, attn_sink=m_attn_sink,
                 pool_w=m_pool_w, pool_scale=m_pool_scale, mix_ab_w_out=m_mix_ab_w_out, lru_w_in=m_lru_w_in,
                 lru_conv_w=m_lru_conv_w, lru_conv_b=m_lru_conv_b, lru_wa=m_lru_wa, lru_ba=m_lru_ba, lru_wx=m_lru_wx,
                 lru_bx=m_lru_bx, lru_lambda=m_lru_lambda, lru_w_out=m_lru_w_out)
    mom_v = dict(c_ctx=v_c_ctx, w_mod=v_w_mod, b_mod=v_b_mod, ln_g=v_ln_g, ln_b=v_ln_b, ffn_w_gate=v_ffn_w_gate,
                 ffn_w_up=v_ffn_w_up, ffn_w_down=v_ffn_w_down, mix_ab_w_in=v_mix_ab_w_in, attn_sink=v_attn_sink,
                 pool_w=v_pool_w, pool_scale=v_pool_scale, mix_ab_w_out=v_mix_ab_w_out, lru_w_in=v_lru_w_in,
                 lru_conv_w=v_lru_conv_w, lru_conv_b=v_lru_conv_b, lru_wa=v_lru_wa, lru_ba=v_lru_ba, lru_wx=v_lru_wx,
                 lru_bx=v_lru_bx, lru_lambda=v_lru_lambda, lru_w_out=v_lru_w_out)
    names = list(weights)

    n_lat, n_ctx = x.shape[1], ctx.shape[1]
    cfg = _Cfg(n_lat, n_ctx)
    _, _, _, me = _position()
    mcols = w_mod.shape[2]

    def t_bf16(w):
        return jnp.swapaxes(w, -1, -2).astype(BF16)

    def ffn_src(l, i):
        return jnp.stack([t_bf16(ffn_w_gate[l, i]), t_bf16(ffn_w_up[l, i]), ffn_w_down[l, i].astype(BF16)])

    pending = {}

    def start_gathers(items, tok):
        for key, make_src in items:
            pending[key] = _gather_start(make_src() + tok.astype(BF16), me, "gather_start_" + key)
            tok = pending[key][4][0, 0]
        return tok

    def weights_now(key, after):
        return _gather_wait(pending[key], after, "gather_wait_" + key)

    tok = start_gathers([("ffn00", lambda: ffn_src(0, 0))], jnp.zeros((), F32))

    small_names = ["ln_g", "ln_b", "lru_conv_w", "lru_conv_b", "lru_ba", "lru_bx", "lru_lambda"]
    small, small_off = _pack_rows([(c + tok).reshape(-1, 128)] + [weights[n].reshape(-1, 128) for n in small_names])
    small_all = _all_gather(small, "gather_small", True)

    def small_full(idx, shp):
        rows = math.prod(shp) // 128
        return _blocks_to_cols(small_all[:, small_off[idx]:small_off[idx] + rows, :].reshape((N_DEV,) + shp))

    c_all = small_all[:, :2 * D // 128, :].reshape(2 * N_DEV, D)
    ln_g_f, ln_b_f = small_full(1, ln_g.shape), small_full(2, ln_b.shape)
    lru_consts = (small_full(3, lru_conv_w.shape)[0], small_full(4, lru_conv_b.shape), lru_wa[0],
                  small_full(5, lru_ba.shape)[0], lru_wx[0], small_full(6, lru_bx.shape)[0],
                  small_full(7, lru_lambda.shape)[0])

    s_rows = jnp.zeros((32, D), F32).at[:16].set(_silu(c_all)).at[16].set(_silu(c_ctx)).astype(BF16)
    mod_mine = jnp.stack([_matmul(s_rows, w_mod[l], "nn", F32, "mod_fwd", bn_cap=1280) for l in range(2)])
    mod_all = _all_gather(mod_mine.reshape(64, mcols), "gather_mod", True).reshape(N_DEV, 2, 32, mcols)
    tok = start_gathers([("ab_in", lambda: t_bf16(mix_ab_w_in)), ("ab_out", lambda: mix_ab_w_out.astype(BF16)),
                         ("ffn01", lambda: ffn_src(0, 1)), ("ffn10", lambda: ffn_src(1, 0)),
                         ("lru_in", lambda: t_bf16(lru_w_in)), ("lru_out", lambda: lru_w_out.astype(BF16)),
                         ("ffn11", lambda: ffn_src(1, 1))], mod_all[0, 0, 0, 0] * 0.0)
    mod_full = _blocks_to_cols(mod_all) + (b_mod[:, None, :] + tok)
    ex0 = 2 * me
    mods = []
    for l in range(2):
        rows = jnp.stack([lax.dynamic_index_in_dim(mod_full[l], ex0, 0, False),
                          lax.dynamic_index_in_dim(mod_full[l], ex0 + 1, 0, False), mod_full[l, 16]])
        mods.append(rows.reshape(3, N_MOD, D))

    h0 = jnp.concatenate([x.reshape(cfg.t_lat, D), ctx.reshape(cfg.t_ctx, D)], axis=0)
    cos, sin = _rope_tables(n_lat)
    sink_rows = jnp.broadcast_to(attn_sink[0][:, None], (8, 128)).astype(F32)

    saved = []
    wf = [[None, None], [None, None]]
    h = h0
    xin = _modulate(cfg, h0, mods[0], 0, 1, "modulate_in")
    for l in range(2):
        st = {"h_in": h, "xin1": xin}
        wf[l][0] = weights_now("ffn%d0" % l, xin)
        g1, u1, y1 = _ffn_fwd(xin, wf[l][0], "ffn_fwd")
        h1, xhat1, rstd1, xin2 = _ln_fwd(cfg, h, y1, mods[l], 2, 0.5, ln_g_f[l, 0][None], ln_b_f[l, 0][None],
                                          mods[l], (3, 4), "ln_fwd_a")
        st.update(g1=g1, u1=u1, y1=y1, h1=h1, xhat1=xhat1, rstd1=rstd1, xin2=xin2)
        if l == 0:
            w_ab_in_t = weights_now("ab_in", xin2)[0]
            p = _matmul(xin2, w_ab_in_t, "nt", F32, "mix_ab_in")
            att_l, att_c = _attn_fwd(cfg, p, cos, sin, sink_rows, "attn_fwd")
            pool_l = _pool_fwd(p, pool_w[0], pool_scale, n_lat, 0, 2, "pool_fwd_lat")
            pool_c = _pool_fwd(p, pool_w[0], pool_scale, n_ctx, cfg.ctx_blk, 2, "pool_fwd_ctx")
            cat = jnp.concatenate([jnp.concatenate([att_l, pool_l], axis=1),
                                   jnp.concatenate([att_c, pool_c], axis=1)], axis=0)
            w_ab_out = weights_now("ab_out", cat)[0]
            y2 = _matmul(cat, w_ab_out, "nn", F32, "mix_ab_out")
        else:
            w_lru_in_t = weights_now("lru_in", xin2)[0]
            p = _matmul(xin2, w_lru_in_t, "nt", F32, "lru_in")
            z_l, z_c = _lru_fwd(cfg, p, lru_consts, "lru_fwd")
            cat = jnp.concatenate([z_l, z_c], axis=0)
            w_lru_out = weights_now("lru_out", cat)[0]
            y2 = _matmul(cat, w_lru_out, "nn", F32, "lru_out")
        h2, xhat2, rstd2, xin3 = _ln_fwd(cfg, h1, y2, mods[l], 5, 1.0, ln_g_f[l, 1][None], ln_b_f[l, 1][None],
                                          mods[l], (6, 7), "ln_fwd_b")
        wf[l][1] = weights_now("ffn%d1" % l, xin3)
        g3, u3, y3 = _ffn_fwd(xin3, wf[l][1], "ffn_fwd")
        if l == 0:
            h3, xhat3, rstd3, xin = _ln_fwd(cfg, h2, y3, mods[l], 8, 0.5, ln_g_f[l, 2][None], ln_b_f[l, 2][None],
                                            mods[1], (0, 1), "ln_fwd_a")
        else:
            h3, xhat3, rstd3 = _ln_fwd(cfg, h2, y3, mods[l], 8, 0.5, ln_g_f[l, 2][None], ln_b_f[l, 2][None],
                                       None, None, "ln_fwd_last")
        st.update(p=p, cat=cat, y2=y2, h2=h2, xhat2=xhat2, rstd2=rstd2, xin3=xin3, g3=g3, u3=u3, y3=y3,
                  xhat3=xhat3, rstd3=rstd3)
        saved.append(st)
        h = h3

    dy, loss_tile = _loss(cfg, h, loss_target.reshape(cfg.t_lat, D), "loss")
    loss = lax.psum(loss_tile[0, 0], ("x", "y", "c"))

    grads = {}
    dmod = [None, None]
    recv_ffn = [[None, None], [None, None]]
    dln_g = [[None] * 3, [None] * 3]
    dln_b = [[None] * 3, [None] * 3]

    def ffn_weight_grads(tag, xin_b, dg, du, a_act, dys):
        parts = [_matmul(dg, xin_b, "tn", BF16, "ffn_dw", bm_cap=1408)[None],
                 _matmul(du, xin_b, "tn", BF16, "ffn_dw", bm_cap=1408)[None],
                 _matmul(a_act, dys, "tn", BF16, "ffn_dw", bm_cap=1408)[None]]
        return [_exchange_start(part, me, "exchange_start_ffn%s_%d" % (tag, k)) for k, part in enumerate(parts)]

    def pin(handles):
        total = handles[0][4][0, 0]
        for hd in handles[1:]:
            total = total + hd[4][0, 0]
        return total

    up = (dy,)
    dmod_next = None
    last_sent = None
    for l in (1, 0):
        st = saved[l]
        dm = [None] * N_MOD

        def put_stats(stats, gate_idx, nxt):
            dm[gate_idx] = stats[:, 2, :]
            if nxt is not None:
                nxt[0][nxt[1]] = stats[:, 4, :]
                nxt[0][nxt[1] + 1] = stats[:, 3, :]

        lng3 = ln_g_f[l, 2][None] if last_sent is None else ln_g_f[l, 2][None] + pin(last_sent)
        dres, dys, stats = _ln_bwd(cfg, up, st["xhat3"], st["rstd3"], st["y3"], mods[l], 8, 0.5,
                                   lng3, "ln_bwd_fused" if len(up) > 1 else "ln_bwd_last")
        put_stats(stats, 8, None if len(up) == 1 else (dmod_next, 0))
        dln_g[l][2], dln_b[l][2] = stats[:, 0, :].sum(0), stats[:, 1, :].sum(0)
        dg, du, a_act, dxin = _ffn_bwd(dys, st["g3"], st["u3"], wf[l][1], "ffn_bwd")
        recv_ffn[l][1] = ffn_weight_grads("%d1" % l, st["xin3"], dg, du, a_act, dys)
        dres, dys, stats = _ln_bwd(cfg, (dres, dxin, st["h2"], mods[l], 7), st["xhat2"], st["rstd2"], st["y2"],
                                   mods[l], 5, 1.0, ln_g_f[l, 1][None] + pin(recv_ffn[l][1]), "ln_bwd_fused")
        put_stats(stats, 5, (dm, 6))
        dln_g[l][1], dln_b[l][1] = stats[:, 0, :].sum(0), stats[:, 1, :].sum(0)
        if l == 0:
            dw_out = _matmul(st["cat"], dys, "tn", BF16, "mix_ab_dw_out")
            dcat = _matmul(dys, w_ab_out, "nt", F32, "mix_ab_dcat")
            dq, dk, dv, dqc, dkc, dvc, dsink = _attn_bwd(cfg, st["p"], dcat, cos, sin, sink_rows, "attn_bwd")
            du_l, dpw_l, dps_l = _pool_bwd(st["p"], pool_w[0], pool_scale, dcat, n_lat, 0, 2, "pool_bwd_lat")
            du_c, dpw_c, dps_c = _pool_bwd(st["p"], pool_w[0], pool_scale, dcat, n_ctx, cfg.ctx_blk, 2, "pool_bwd_ctx")
            dp = jnp.concatenate([jnp.concatenate([dq, dk, dv, du_l], axis=1),
                                  jnp.concatenate([dqc, dkc, dvc, du_c], axis=1)], axis=0)
            dw_in_t = _matmul(dp, st["xin2"], "tn", BF16, "mix_ab_dw_in", bm_cap=1280)
            dxin = _matmul(dp, w_ab_in_t, "nn", F32, "mix_ab_dx")
            recv_mix = [_exchange_start(part, me, "exchange_start_mix_ab_%d" % k)
                        for k, part in enumerate((dw_in_t[None], dw_out[None], _as2d(dpw_l + dpw_c)[None]))]
            grads["attn_sink"] = (dsink[0, :, 0] + dsink[1, :, 0])[None, :]
            grads["pool_scale"] = dps_l + dps_c
        else:
            dw_out = _matmul(st["cat"], dys, "tn", BF16, "lru_dw_out")
            dz = _matmul(dys, w_lru_out, "nt", F32, "lru_dz")
            dgl, dul, dgc, duc, dwa, dwx, vec = _lru_bwd(cfg, st["p"], dz, lru_consts, "lru_bwd")
            dp = jnp.concatenate([jnp.concatenate([dgl, dul], axis=1), jnp.concatenate([dgc, duc], axis=1)], axis=0)
            dw_in_t = _matmul(dp, st["xin2"], "tn", BF16, "lru_dw_in", bm_cap=1024)
            dxin = _matmul(dp, w_lru_in_t, "nn", F32, "lru_dx")
            recv_mix = [_exchange_start(part, me, "exchange_start_lru_%d" % k)
                        for k, part in enumerate((dw_in_t[None], dw_out[None], _as2d(dwa)[None], _as2d(dwx)[None]))]
            vec_t = jnp.moveaxis(vec, 0, 1).reshape(16, D)
            grads["lru_ba"], grads["lru_bx"] = vec_t[0:2], vec_t[2:4]
            grads["lru_lambda"], grads["lru_conv_w"], grads["lru_conv_b"] = vec_t[4:6], vec_t[6:10], vec_t[10:11]
        if l == 0:
            recv_ab = recv_mix
        else:
            recv_lru = recv_mix
        dres, dys, stats = _ln_bwd(cfg, (dres, dxin, st["h1"], mods[l], 4), st["xhat1"], st["rstd1"], st["y1"],
                                   mods[l], 2, 0.5, ln_g_f[l, 0][None] + pin(recv_mix), "ln_bwd_fused")
        put_stats(stats, 2, (dm, 3))
        dln_g[l][0], dln_b[l][0] = stats[:, 0, :].sum(0), stats[:, 1, :].sum(0)
        dg, du, a_act, dxin = _ffn_bwd(dys, st["g1"], st["u1"], wf[l][0], "ffn_bwd")
        recv_ffn[l][0] = ffn_weight_grads("%d0" % l, st["xin1"], dg, du, a_act, dys)
        last_sent = recv_ffn[l][0]
        dmod[l] = dm
        dmod_next = dm
        up = (dres, dxin, st["h_in"], mods[l], 1)
    dh0, stats = _modulate_bwd(cfg, up[0], up[1], h0, mods[0] + pin(last_sent), 1, "modulate_bwd")
    dmod[0][0], dmod[0][1] = stats[:, 4, :], stats[:, 3, :]
    grad_x = dh0[:cfg.t_lat].reshape(x.shape)

    def arrived(handle, name):
        return _exchange_wait(handle, dh0, name)

    recv_ffn = [[[arrived(hd, "exchange_wait_ffn%d%d_%d" % (l, i, k)) for k, hd in enumerate(recv_ffn[l][i])]
                 for i in range(2)] for l in range(2)]
    recv_ab = [arrived(hd, "exchange_wait_mix_ab_%d" % k) for k, hd in enumerate(recv_ab)]
    recv_lru = [arrived(hd, "exchange_wait_lru_%d" % k) for k, hd in enumerate(recv_lru)]

    dmod_mine = jnp.stack([jnp.stack(dmod[l], axis=1).reshape(3, N_MOD * D) for l in range(2)])
    n_dm = 6 * N_MOD * D // 128
    dmod_all = _all_gather(dmod_mine.reshape(n_dm, 128), "gather_dmod", True)
    dmod_sum = _sum_blocks(dmod_all, "sum_dmod").reshape(2, 3, N_MOD * D)
    dmod_all = dmod_all.reshape(N_DEV, 2, 3, N_MOD * D)
    grads["b_mod"] = dmod_sum[:, 0] + dmod_sum[:, 1] + dmod_sum[:, 2]
    dmod_ex = jnp.moveaxis(dmod_all[:, :, 0:2, :], 1, 0).reshape(2, 2 * N_DEV, N_MOD * D)
    dm_rows = jnp.zeros((2, 32, N_MOD * D), F32).at[:, :16].set(dmod_ex).at[:, 16].set(dmod_sum[:, 2])
    dm_cols = lax.dynamic_slice_in_dim(dm_rows, me * mcols, mcols, axis=2).astype(BF16)
    grads["w_mod"] = jnp.stack([_matmul(s_rows, dm_cols[l], "tn", F32, "mod_dw", bn_cap=1280) for l in range(2)])
    ds_part = None
    for l in range(2):
        part = _matmul(dm_cols[l, 16:32], w_mod[l], "nt", F32, "mod_ds", bk_cap=1280)[0]
        ds_part = part if ds_part is None else ds_part + part

    def shard_sum(recv, name):
        return _sum_blocks(recv.reshape(N_DEV, recv.shape[2], recv.shape[3]), name)

    gate_g = [[None, None], [None, None]]
    up_g = [[None, None], [None, None]]
    down_g = [[None, None], [None, None]]
    for l in range(2):
        for i in range(2):
            gt, ut, dn = [shard_sum(r, "sum_ffn") for r in recv_ffn[l][i]]
            gate_g[l][i], up_g[l][i], down_g[l][i] = gt.T, ut.T, dn
    grads["ffn_w_gate"] = jnp.stack([jnp.stack(gate_g[l]) for l in range(2)])
    grads["ffn_w_up"] = jnp.stack([jnp.stack(up_g[l]) for l in range(2)])
    grads["ffn_w_down"] = jnp.stack([jnp.stack(down_g[l]) for l in range(2)])
    grads["mix_ab_w_in"] = shard_sum(recv_ab[0], "sum_mix_in").T[None]
    grads["mix_ab_w_out"] = shard_sum(recv_ab[1], "sum_mix_out")[None]
    grads["lru_w_in"] = shard_sum(recv_lru[0], "sum_lru_in").T[None]
    grads["lru_w_out"] = shard_sum(recv_lru[1], "sum_lru_out")[None]
    rep_parts = [shard_sum(recv_lru[2], "sum_rep"), shard_sum(recv_lru[3], "sum_rep"), shard_sum(recv_ab[2], "sum_rep")]
    rep_names = ["lru_wa", "lru_wx", "pool_w"]

    dln_g_f = jnp.stack([jnp.stack(dln_g[l]) for l in range(2)])
    dln_b_f = jnp.stack([jnp.stack(dln_b[l]) for l in range(2)])
    sink_pad = jnp.zeros((1, 128), F32).at[0, :8].set(grads["attn_sink"][0])
    part_list = [p_.reshape(-1, 128) for p_ in rep_parts] + [
        dln_g_f.reshape(-1, 128), dln_b_f.reshape(-1, 128), grads["lru_conv_w"].reshape(-1, 128),
        grads["lru_conv_b"].reshape(-1, 128), grads["lru_ba"].reshape(-1, 128), grads["lru_bx"].reshape(-1, 128),
        grads["lru_lambda"].reshape(-1, 128), ds_part.reshape(-1, 128), sink_pad, grads["pool_scale"].reshape(-1, 128)]
    parts, part_off = _pack_rows(part_list)
    parts_all = _all_gather(parts, "gather_partials", True)
    parts_sum = _sum_blocks(parts_all, "sum_partials")

    for i, n in enumerate(rep_names):
        rows = part_list[i].shape[0]
        grads[n] = parts_all[:, part_off[i]:part_off[i] + rows, :].reshape(weights[n].shape)

    def take(idx):
        return parts_sum[part_off[idx]:part_off[idx] + part_list[idx].shape[0]]

    def my_cols(full, shp):
        w = shp[-1]
        return lax.dynamic_slice_in_dim(full, me * w, w, axis=full.ndim - 1)

    grads["ln_g"] = my_cols(take(3).reshape(2, 3, D), ln_g.shape)
    grads["ln_b"] = my_cols(take(4).reshape(2, 3, D), ln_b.shape)
    grads["lru_conv_w"] = my_cols(take(5).reshape(1, 4, D), lru_conv_w.shape)
    grads["lru_conv_b"] = my_cols(take(6).reshape(1, D), lru_conv_b.shape)
    grads["lru_ba"] = my_cols(take(7).reshape(1, 2, D), lru_ba.shape)
    grads["lru_bx"] = my_cols(take(8).reshape(1, 2, D), lru_bx.shape)
    grads["lru_lambda"] = my_cols(take(9).reshape(1, 2, D), lru_lambda.shape)
    sg = jax.nn.sigmoid(c_ctx)
    grads["c_ctx"] = take(10).reshape(D) * (sg * (1.0 + c_ctx * (1.0 - sg)))
    grads["attn_sink"] = take(11)[:, :8]
    grads["pool_scale"] = take(12).reshape(pool_scale.shape)

    delta, new_m, new_v = {}, {}, {}
    for n in names:
        shp = weights[n].shape
        grads[n] = grads[n].reshape(shp)
        delta[n], new_m[n], new_v[n] = _adamw(weights[n], grads[n], mom_m[n], mom_v[n], "adamw")

    return (loss, grad_x, *[grads[n] for n in names], *[delta[n] for n in names],
            *[new_m[n] for n in names], *[new_v[n] for n in names])
```

```python
import functools
import math

import jax
import jax.numpy as jnp
from jax import lax
from jax.experimental import pallas as pl
from jax.experimental.pallas import tpu as pltpu

F32 = jnp.float32
BF16 = jnp.bfloat16
MESH = pl.DeviceIdType.MESH

D = 1024
N_MOD = 9
N_DEV = 8
HEAD_DIM = 64
ATT_HEADS = 8
KV_HEADS = 2
ATT_W = 512
BLK = 128
ATT_SCALE = HEAD_DIM ** -0.5
GRID_W = 64
ROPE_FREQS = HEAD_DIM // 4
ROPE_THETA = 10000.0
POOL_R = (1, 2, 4, 8)
LRU_C = 8.0
LN_EPS = 1e-5
NEG_INF = -1e30
ALPHA = 4.0 ** 0.25
LR, B1, B2, EPS, WD, STEP = 0.001, 0.9, 0.999, 1e-08, 0.01, 10
VMEM_LIMIT = 56 * 1024 * 1024
ROW_TILE = 512


def _params(sem=None):
    if sem is None:
        return pltpu.CompilerParams(vmem_limit_bytes=VMEM_LIMIT)
    return pltpu.CompilerParams(dimension_semantics=sem, vmem_limit_bytes=VMEM_LIMIT)


def _sigmoid(x):
    return 0.5 * jnp.tanh(0.5 * x) + 0.5


def _dot(a, b):
    return jnp.dot(a.astype(BF16), b.astype(BF16), preferred_element_type=F32)


def _dot_nt(a, b):
    return lax.dot_general(a.astype(BF16), b.astype(BF16), (((1,), (1,)), ((), ())), preferred_element_type=F32)


def _dot_tn(a, b):
    return lax.dot_general(a.astype(BF16), b.astype(BF16), (((0,), (0,)), ((), ())), preferred_element_type=F32)


def _pick(n, cap):
    best = None
    for m in range(128, min(n, cap) + 1, 128):
        if n % m == 0:
            best = m
    return n if best is None else best


def _chunks(width, step=256):
    out, c = [], 0
    while c < width:
        w = min(step, width - c)
        out.append((c, w))
        c += w
    return out


class _Cfg:
    def __init__(self, n_lat, n_ctx):
        self.n_lat, self.n_ctx = n_lat, n_ctx
        self.t_lat, self.t_ctx = 2 * n_lat, 2 * n_ctx
        self.T = self.t_lat + self.t_ctx
        self.tm = min(ROW_TILE, self.t_ctx)
        assert n_lat % self.tm == 0 and self.t_ctx % self.tm == 0 and n_lat >= 3 * BLK and n_ctx % BLK == 0
        self.nt = self.T // self.tm
        self.nlt = n_lat // self.tm
        self.ctx_blk = self.t_lat // n_ctx

    def seg(self, i):
        return jnp.minimum(i // self.nlt, 2)

    def first_of_seg(self, i):
        return jnp.where(i < 2 * self.nlt, i % self.nlt == 0, i == 2 * self.nlt)


def _modulate(cfg, h, mod, shift_idx, scale_idx, name):
    tm = cfg.tm

    def body(h_ref, mod_ref, o_ref):
        sh = mod_ref[shift_idx:shift_idx + 1, :]
        sc = mod_ref[scale_idx:scale_idx + 1, :]
        o_ref[...] = (h_ref[...] * (1.0 + sc) + sh).astype(BF16)

    return pl.pallas_call(
        body, grid=(cfg.nt,), name=name,
        in_specs=[pl.BlockSpec((tm, D), lambda i: (i, 0)),
                  pl.BlockSpec((None, N_MOD, D), lambda i: (cfg.seg(i), 0, 0))],
        out_specs=pl.BlockSpec((tm, D), lambda i: (i, 0)),
        out_shape=jax.ShapeDtypeStruct((cfg.T, D), BF16),
        compiler_params=_params(("parallel",)),
    )(h, mod)


def _ln_fwd(cfg, h, y, mod, gate_idx, coef, lng, lnb, mod_next, next_idx, name):
    tm = cfg.tm
    has_next = next_idx is not None

    def body(*refs):
        if has_next:
            h_ref, y_ref, mod_ref, g_ref, b_ref, modn_ref, hn_ref, xhat_ref, rstd_ref, xin_ref = refs
        else:
            h_ref, y_ref, mod_ref, g_ref, b_ref, hn_ref, xhat_ref, rstd_ref = refs
        gate = mod_ref[gate_idx:gate_idx + 1, :]
        z = ALPHA * h_ref[...] + (coef * gate) * y_ref[...]
        mu = jnp.mean(z, axis=-1, keepdims=True)
        zc = z - mu
        var = jnp.mean(zc * zc, axis=-1, keepdims=True)
        rstd = lax.rsqrt(var + LN_EPS)
        xhat = zc * rstd
        hn = xhat * g_ref[...] + b_ref[...]
        hn_ref[...] = hn
        xhat_ref[...] = xhat
        rstd_ref[...] = rstd
        if has_next:
            sh = modn_ref[next_idx[0]:next_idx[0] + 1, :]
            sc = modn_ref[next_idx[1]:next_idx[1] + 1, :]
            xin_ref[...] = (hn * (1.0 + sc) + sh).astype(BF16)

    row = pl.BlockSpec((tm, D), lambda i: (i, 0))
    modspec = pl.BlockSpec((None, N_MOD, D), lambda i: (cfg.seg(i), 0, 0))
    vec = pl.BlockSpec((1, D), lambda i: (0, 0))
    in_specs = [row, row, modspec, vec, vec]
    args = [h, y, mod, lng, lnb]
    out_specs = [row, row, pl.BlockSpec((tm, 1), lambda i: (i, 0))]
    out_shape = [jax.ShapeDtypeStruct((cfg.T, D), F32), jax.ShapeDtypeStruct((cfg.T, D), F32),
                 jax.ShapeDtypeStruct((cfg.T, 1), F32)]
    if has_next:
        in_specs.append(modspec)
        args.append(mod_next)
        out_specs.append(row)
        out_shape.append(jax.ShapeDtypeStruct((cfg.T, D), BF16))
    return pl.pallas_call(body, grid=(cfg.nt,), name=name, in_specs=in_specs, out_specs=out_specs,
                          out_shape=out_shape, compiler_params=_params(("parallel",)))(*args)


def _ln_bwd(cfg, up, xhat, rstd, y, mod, gate_idx, coef, lng, name):
    tm = cfg.tm
    fused = len(up) > 1
    scale_next = up[4] if fused else None

    def body(*refs):
        if fused:
            dres_n, dxin_n, hn_ref, modn_ref, xhat_ref, rstd_ref, y_ref, mod_ref, g_ref, dres_ref, dys_ref, st_ref = refs
        else:
            dhn_ref, xhat_ref, rstd_ref, y_ref, mod_ref, g_ref, dres_ref, dys_ref, st_ref = refs
        i = pl.program_id(0)

        @pl.when(cfg.first_of_seg(i))
        def _():
            st_ref[...] = jnp.zeros_like(st_ref)

        if fused:
            dxin = dxin_n[...]
            sc = modn_ref[scale_next:scale_next + 1, :]
            dhn = dres_n[...] + dxin * (1.0 + sc)
            st_ref[3:4, :] += jnp.sum(dxin * hn_ref[...], axis=0, keepdims=True)
            st_ref[4:5, :] += jnp.sum(dxin, axis=0, keepdims=True)
        else:
            dhn = dhn_ref[...]
        xhat = xhat_ref[...]
        gdh = dhn * g_ref[...]
        m1 = jnp.mean(gdh, axis=-1, keepdims=True)
        m2 = jnp.mean(gdh * xhat, axis=-1, keepdims=True)
        dz = rstd_ref[...] * (gdh - m1 - xhat * m2)
        gate = mod_ref[gate_idx:gate_idx + 1, :]
        dres_ref[...] = ALPHA * dz
        dys_ref[...] = ((coef * gate) * dz).astype(BF16)
        st_ref[0:1, :] += jnp.sum(dhn * xhat, axis=0, keepdims=True)
        st_ref[1:2, :] += jnp.sum(dhn, axis=0, keepdims=True)
        st_ref[2:3, :] += jnp.sum((coef * dz) * y_ref[...], axis=0, keepdims=True)

    row = pl.BlockSpec((tm, D), lambda i: (i, 0))
    modspec = pl.BlockSpec((None, N_MOD, D), lambda i: (cfg.seg(i), 0, 0))
    vec = pl.BlockSpec((1, D), lambda i: (0, 0))
    col = pl.BlockSpec((tm, 1), lambda i: (i, 0))
    if fused:
        in_specs = [row, row, row, modspec, row, col, row, modspec, vec]
        args = [up[0], up[1], up[2], up[3], xhat, rstd, y, mod, lng]
    else:
        in_specs = [row, row, col, row, modspec, vec]
        args = [up[0], xhat, rstd, y, mod, lng]
    return pl.pallas_call(
        body, grid=(cfg.nt,), name=name, in_specs=in_specs,
        out_specs=[row, row, pl.BlockSpec((None, 8, D), lambda i: (cfg.seg(i), 0, 0))],
        out_shape=[jax.ShapeDtypeStruct((cfg.T, D), F32), jax.ShapeDtypeStruct((cfg.T, D), BF16),
                   jax.ShapeDtypeStruct((3, 8, D), F32)],
        compiler_params=_params(("arbitrary",)))(*args)


def _modulate_bwd(cfg, dres, dxin, h, mod, scale_idx, name):
    tm = cfg.tm

    def body(dres_ref, dxin_ref, h_ref, mod_ref, dh_ref, st_ref):
        i = pl.program_id(0)

        @pl.when(cfg.first_of_seg(i))
        def _():
            st_ref[...] = jnp.zeros_like(st_ref)

        dxin = dxin_ref[...]
        sc = mod_ref[scale_idx:scale_idx + 1, :]
        dh_ref[...] = dres_ref[...] + dxin * (1.0 + sc)
        st_ref[3:4, :] += jnp.sum(dxin * h_ref[...], axis=0, keepdims=True)
        st_ref[4:5, :] += jnp.sum(dxin, axis=0, keepdims=True)

    row = pl.BlockSpec((tm, D), lambda i: (i, 0))
    return pl.pallas_call(
        body, grid=(cfg.nt,), name=name,
        in_specs=[row, row, row, pl.BlockSpec((None, N_MOD, D), lambda i: (cfg.seg(i), 0, 0))],
        out_specs=[row, pl.BlockSpec((None, 8, D), lambda i: (cfg.seg(i), 0, 0))],
        out_shape=[jax.ShapeDtypeStruct((cfg.T, D), F32), jax.ShapeDtypeStruct((3, 8, D), F32)],
        compiler_params=_params(("arbitrary",)))(dres, dxin, h, mod)


def _loss(cfg, h, target, name):
    tm = cfg.tm
    n_lt = 2 * cfg.nlt

    def body(h_ref, t_ref, dy_ref, l_ref):
        i = pl.program_id(0)

        @pl.when(i == 0)
        def _():
            l_ref[...] = jnp.zeros_like(l_ref)

        @pl.when(i < n_lt)
        def _():
            err = h_ref[...] - t_ref[...]
            dy_ref[...] = err * (1.0 / D)
            part = jnp.sum(jnp.sum(err * err, axis=1, keepdims=True), axis=0, keepdims=True) * (0.5 / D)
            l_ref[...] += jnp.broadcast_to(part, l_ref.shape)

        @pl.when(i >= n_lt)
        def _():
            dy_ref[...] = jnp.zeros_like(dy_ref)

    return pl.pallas_call(
        body, grid=(cfg.nt,), name=name,
        in_specs=[pl.BlockSpec((tm, D), lambda i: (i, 0)),
                  pl.BlockSpec((tm, D), lambda i: (jnp.minimum(i, n_lt - 1), 0))],
        out_specs=[pl.BlockSpec((tm, D), lambda i: (i, 0)), pl.BlockSpec((8, 128), lambda i: (0, 0))],
        out_shape=[jax.ShapeDtypeStruct((cfg.T, D), F32), jax.ShapeDtypeStruct((8, 128), F32)],
        compiler_params=_params(("arbitrary",)))(h, target)


def _matmul(a, b, mode, out_dtype, name, bm_cap=512, bn_cap=1408, bk_cap=1024):
    if mode == "nn":
        (M, K), N = a.shape, b.shape[1]
    elif mode == "nt":
        (M, K), N = a.shape, b.shape[0]
    else:
        (K, M), N = a.shape, b.shape[1]
    bm, bn, bk = _pick(M, bm_cap), _pick(N, bn_cap), _pick(K, bk_cap)
    nk = K // bk

    def body(a_ref, b_ref, o_ref, acc_ref=None):
        k = pl.program_id(2)
        if mode == "nn":
            part = _dot(a_ref[...], b_ref[...])
        elif mode == "nt":
            part = _dot_nt(a_ref[...], b_ref[...])
        else:
            part = _dot_tn(a_ref[...], b_ref[...])
        if nk == 1:
            o_ref[...] = part.astype(out_dtype)
            return

        @pl.when(k == 0)
        def _():
            acc_ref[...] = part

        @pl.when((k > 0) & (k < nk - 1))
        def _():
            acc_ref[...] += part

        @pl.when(k == nk - 1)
        def _():
            o_ref[...] = (acc_ref[...] + part).astype(out_dtype)

    if mode == "nn":
        a_spec = pl.BlockSpec((bm, bk), lambda i, j, k: (i, k))
        b_spec = pl.BlockSpec((bk, bn), lambda i, j, k: (k, j))
    elif mode == "nt":
        a_spec = pl.BlockSpec((bm, bk), lambda i, j, k: (i, k))
        b_spec = pl.BlockSpec((bn, bk), lambda i, j, k: (j, k))
    else:
        a_spec = pl.BlockSpec((bk, bm), lambda i, j, k: (k, i))
        b_spec = pl.BlockSpec((bk, bn), lambda i, j, k: (k, j))
    return pl.pallas_call(
        body, grid=(M // bm, N // bn, nk), name=name, in_specs=[a_spec, b_spec],
        out_specs=pl.BlockSpec((bm, bn), lambda i, j, k: (i, j)),
        out_shape=jax.ShapeDtypeStruct((M, N), out_dtype),
        scratch_shapes=[pltpu.VMEM((bm, bn), F32)] if nk > 1 else [],
        compiler_params=_params(("parallel", "parallel", "arbitrary")))(a, b)


def _ffn_tile(T, cap):
    best = 256
    for t in range(256, cap + 1, 256):
        if T % t == 0:
            best = t
    return best


def _ffn_fwd(xin, wf, name):
    T = xin.shape[0]
    F = wf.shape[1]
    tm, tf = _ffn_tile(T, 768), F // 2
    assert tf % 128 == 0 and T % tm == 0

    def body(x_ref, wg_ref, wu_ref, wd_ref, g_ref, u_ref, y_ref):
        j = pl.program_id(1)
        x = x_ref[...]
        acc = None
        for c0, cw in _chunks(tf):
            g = _dot_nt(x, wg_ref[c0:c0 + cw, :])
            u = _dot_nt(x, wu_ref[c0:c0 + cw, :])
            g_ref[:, c0:c0 + cw] = g.astype(BF16)
            u_ref[:, c0:c0 + cw] = u.astype(BF16)
            part = _dot(g * _sigmoid(g) * u, wd_ref[c0:c0 + cw, :])
            acc = part if acc is None else acc + part

        @pl.when(j == 0)
        def _():
            y_ref[...] = acc

        @pl.when(j > 0)
        def _():
            y_ref[...] += acc

    return pl.pallas_call(
        body, grid=(T // tm, 2), name=name,
        in_specs=[pl.BlockSpec((tm, D), lambda i, j: (i, 0)),
                  pl.BlockSpec((None, tf, D), lambda i, j: (0, j, 0)),
                  pl.BlockSpec((None, tf, D), lambda i, j: (1, j, 0)),
                  pl.BlockSpec((None, tf, D), lambda i, j: (2, j, 0))],
        out_specs=[pl.BlockSpec((tm, tf), lambda i, j: (i, j)),
                   pl.BlockSpec((tm, tf), lambda i, j: (i, j)),
                   pl.BlockSpec((tm, D), lambda i, j: (i, 0))],
        out_shape=[jax.ShapeDtypeStruct((T, F), BF16), jax.ShapeDtypeStruct((T, F), BF16),
                   jax.ShapeDtypeStruct((T, D), F32)],
        compiler_params=_params(("parallel", "arbitrary")))(xin, wf, wf, wf)


def _ffn_bwd(dys, g, u, wf, name):
    T = dys.shape[0]
    F = wf.shape[1]
    tm, tf = _ffn_tile(T, 512), F // 2

    def body(dy_ref, g_ref, u_ref, wg_ref, wu_ref, wd_ref, dg_ref, du_ref, a_ref, dx_ref):
        j = pl.program_id(1)
        da_all = _dot_nt(dy_ref[...], wd_ref[...])
        for c0, cw in _chunks(tf):
            gg = g_ref[:, c0:c0 + cw].astype(F32)
            uu = u_ref[:, c0:c0 + cw].astype(F32)
            da = da_all[:, c0:c0 + cw]
            s = _sigmoid(gg)
            silu = gg * s
            a_ref[:, c0:c0 + cw] = (silu * uu).astype(BF16)
            du_ref[:, c0:c0 + cw] = (da * silu).astype(BF16)
            dg_ref[:, c0:c0 + cw] = (da * uu * (s * (1.0 + gg * (1.0 - s)))).astype(BF16)
        acc = _dot(dg_ref[...], wg_ref[...]) + _dot(du_ref[...], wu_ref[...])

        @pl.when(j == 0)
        def _():
            dx_ref[...] = acc

        @pl.when(j > 0)
        def _():
            dx_ref[...] += acc

    blk = pl.BlockSpec((tm, tf), lambda i, j: (i, j))
    return pl.pallas_call(
        body, grid=(T // tm, 2), name=name,
        in_specs=[pl.BlockSpec((tm, D), lambda i, j: (i, 0)), blk, blk,
                  pl.BlockSpec((None, tf, D), lambda i, j: (0, j, 0)),
                  pl.BlockSpec((None, tf, D), lambda i, j: (1, j, 0)),
                  pl.BlockSpec((None, tf, D), lambda i, j: (2, j, 0))],
        out_specs=[blk, blk, blk, pl.BlockSpec((tm, D), lambda i, j: (i, 0))],
        out_shape=[jax.ShapeDtypeStruct((T, F), BF16), jax.ShapeDtypeStruct((T, F), BF16),
                   jax.ShapeDtypeStruct((T, F), BF16), jax.ShapeDtypeStruct((T, D), F32)],
        compiler_params=_params(("parallel", "arbitrary")))(dys, g, u, wf, wf, wf)


def _swap_halves(x):
    w = x.shape[1]
    lane = lax.broadcasted_iota(jnp.int32, (1, w), 1)
    return jnp.where((lane & 63) < 32, pltpu.roll(x, w - 32, 1), pltpu.roll(x, 32, 1))


def _rope(x, cos, sin):
    return x * cos + _swap_halves(x) * sin


def _rope_t(dy, cos, sin):
    return dy * cos + _swap_halves(dy * sin)


def _rope_tables(n_lat):
    rows = n_lat // GRID_W
    row = jnp.repeat(jnp.arange(rows, dtype=F32), GRID_W)
    col = jnp.tile(jnp.arange(GRID_W, dtype=F32), rows)
    inv = ROPE_THETA ** (-jnp.arange(ROPE_FREQS, dtype=F32) / ROPE_FREQS)
    ang = jnp.concatenate([row[:, None] * inv, col[:, None] * inv], axis=-1)
    cs, sn = jnp.cos(ang), jnp.sin(ang)
    cos = jnp.concatenate([cs, cs, cs, cs], axis=-1)
    sin = jnp.concatenate([-sn, sn, -sn, sn], axis=-1)
    return cos, sin


def _attn_specs(cfg):
    n_lat, n_ctx, cb = cfg.n_lat, cfg.n_ctx, cfg.ctx_blk
    return [pl.BlockSpec((n_lat, ATT_W), lambda e: (e, 0)),
            pl.BlockSpec((n_lat, 128), lambda e: (e, 4)),
            pl.BlockSpec((n_lat, 128), lambda e: (e, 5)),
            pl.BlockSpec((n_ctx, ATT_W), lambda e: (cb + e, 0)),
            pl.BlockSpec((n_ctx, 128), lambda e: (cb + e, 4)),
            pl.BlockSpec((n_ctx, 128), lambda e: (cb + e, 5)),
            pl.BlockSpec((n_lat, 128), lambda e: (0, 0)),
            pl.BlockSpec((n_lat, 128), lambda e: (0, 0)),
            pl.BlockSpec((8, 128), lambda e: (0, 0))]


def _attn_prepare(kh, kl, vl, kc, vc, ka, kb, va, vb, kca, kcb, vca, vcb):
    lane = lax.broadcasted_iota(jnp.int32, (1, 128), 1)
    own = (lane < 64) if kh == 0 else (lane >= 64)

    def split(x, ra, rb):
        mine = jnp.where(own, x, 0.0)
        other = pltpu.roll(mine, 64, 1)
        a, b = (mine, other) if kh == 0 else (other, mine)
        ra[...] = a.astype(BF16)
        rb[...] = b.astype(BF16)

    split(kl, ka, kb)
    split(vl, va, vb)
    split(kc, kca, kcb)
    split(vc, vca, vcb)


def _softmax_parts(s_list, sk):
    m = sk
    for s in s_list:
        m = jnp.maximum(m, jnp.max(s, axis=1, keepdims=True))
    es = [jnp.exp(s - m) for s in s_list]
    esk = jnp.exp(sk - m)
    den = esk
    for e in es:
        den = den + jnp.sum(e, axis=1, keepdims=True)
    inv = 1.0 / den
    return [e * inv for e in es], esk * inv


def _window(cfg, n):
    r0 = pl.multiple_of(n * BLK, BLK)
    start = pl.multiple_of(jnp.clip((n - 1) * BLK, 0, cfg.n_lat - 3 * BLK), BLK)
    qpos = r0 + lax.broadcasted_iota(jnp.int32, (BLK, 1), 0)
    kpos = start + lax.broadcasted_iota(jnp.int32, (1, 3 * BLK), 1)
    valid = jnp.abs(qpos - kpos) <= BLK
    return r0, start, valid


def _attn_fwd(cfg, p, cos, sin, sink_rows, name):
    n_lat, n_ctx = cfg.n_lat, cfg.n_ctx

    def body(q_ref, k_ref, v_ref, qc_ref, kc_ref, vc_ref, cos_ref, sin_ref, sink_ref, o_ref, oc_ref,
             qr, ka, kb, va, vb, kca, kcb, vca, vcb):
        cos_t, sin_t = cos_ref[...], sin_ref[...]
        for gq in range(4):
            qr[:, gq * 128:(gq + 1) * 128] = _rope(q_ref[:, gq * 128:(gq + 1) * 128], cos_t, sin_t).astype(BF16)
        kl = _rope(k_ref[...], cos_t, sin_t)
        for kh in range(KV_HEADS):
            _attn_prepare(kh, kl, v_ref[...], kc_ref[...], vc_ref[...], ka, kb, va, vb, kca, kcb, vca, vcb)

            def lat_block(n, carry):
                r0, start, valid = _window(cfg, n)
                win = pl.ds(start, 3 * BLK)
                for pr in range(2):
                    lanes = slice((kh * 2 + pr) * 128, (kh * 2 + pr + 1) * 128)
                    qp = qr[pl.ds(r0, BLK), lanes]
                    o = None
                    for half, (kw, kcx, vw, vcx) in enumerate(((ka, kca, va, vca), (kb, kcb, vb, vcb))):
                        head = kh * 4 + pr * 2 + half
                        s_w = jnp.where(valid, _dot_nt(qp, kw[win, :]) * ATT_SCALE, NEG_INF)
                        s_c = _dot_nt(qp, kcx[...]) * ATT_SCALE
                        (p_w, p_c), _ = _softmax_parts([s_w, s_c], sink_ref[head:head + 1, 0:1])
                        part = _dot(p_w, vw[win, :]) + _dot(p_c, vcx[...])
                        o = part if o is None else o + part
                    o_ref[pl.ds(r0, BLK), lanes] = o.astype(BF16)
                return carry

            lax.fori_loop(0, n_lat // BLK, lat_block, 0)
            for n in range(n_ctx // BLK):
                rows = slice(n * BLK, (n + 1) * BLK)
                for pr in range(2):
                    lanes = slice((kh * 2 + pr) * 128, (kh * 2 + pr + 1) * 128)
                    qp = qc_ref[rows, lanes]
                    o = None
                    for half, (kcx, vcx) in enumerate(((kca, vca), (kcb, vcb))):
                        head = kh * 4 + pr * 2 + half
                        s_c = _dot_nt(qp, kcx[...]) * ATT_SCALE
                        (p_c,), _ = _softmax_parts([s_c], sink_ref[head:head + 1, 0:1])
                        part = _dot(p_c, vcx[...])
                        o = part if o is None else o + part
                    oc_ref[rows, lanes] = o.astype(BF16)

    return pl.pallas_call(
        body, grid=(2,), name=name, in_specs=_attn_specs(cfg),
        out_specs=[pl.BlockSpec((n_lat, ATT_W), lambda e: (e, 0)), pl.BlockSpec((n_ctx, ATT_W), lambda e: (e, 0))],
        out_shape=[jax.ShapeDtypeStruct((cfg.t_lat, ATT_W), BF16), jax.ShapeDtypeStruct((cfg.t_ctx, ATT_W), BF16)],
        scratch_shapes=[pltpu.VMEM((n_lat, ATT_W), BF16)] + [pltpu.VMEM((n_lat, 128), BF16)] * 4
        + [pltpu.VMEM((n_ctx, 128), BF16)] * 4,
        compiler_params=_params(("parallel",)))(p, p, p, p, p, p, cos, sin, sink_rows)


def _attn_bwd(cfg, p, dcat, cos, sin, sink_rows, name):
    n_lat, n_ctx, cb = cfg.n_lat, cfg.n_ctx, cfg.ctx_blk

    def body(q_ref, k_ref, v_ref, qc_ref, kc_ref, vc_ref, cos_ref, sin_ref, sink_ref, do_ref, doc_ref,
             dq_ref, dk_ref, dv_ref, dqc_ref, dkc_ref, dvc_ref, dsink_ref,
             qr, ka, kb, va, vb, kca, kcb, vca, vcb, dqs, dka, dva, dkca, dvca):
        cos_t, sin_t = cos_ref[...], sin_ref[...]
        lane = lax.broadcasted_iota(jnp.int32, (1, 128), 1)
        lo = lane < 64
        for gq in range(4):
            qr[:, gq * 128:(gq + 1) * 128] = _rope(q_ref[:, gq * 128:(gq + 1) * 128], cos_t, sin_t).astype(BF16)
        kl = _rope(k_ref[...], cos_t, sin_t)
        dsink_ref[...] = jnp.zeros_like(dsink_ref)
        dka[...] = jnp.zeros_like(dka)
        dva[...] = jnp.zeros_like(dva)
        dkca[...] = jnp.zeros_like(dkca)
        dvca[...] = jnp.zeros_like(dvca)

        def halves(x):
            return jnp.where(lo, x, 0).astype(BF16), jnp.where(lo, 0, x).astype(BF16)

        for kh in range(KV_HEADS):
            _attn_prepare(kh, kl, v_ref[...], kc_ref[...], vc_ref[...], ka, kb, va, vb, kca, kcb, vca, vcb)

            def one_head(head, qp, q_half, do_p, do_half, kw, kcx, vw, vcx, win, valid):
                sk = sink_ref[head:head + 1, 0:1]
                s_list = [_dot_nt(qp, kcx[...]) * ATT_SCALE]
                if win is not None:
                    s_list.insert(0, jnp.where(valid, _dot_nt(qp, kw[win, :]) * ATT_SCALE, NEG_INF))
                probs, p_sink = _softmax_parts(s_list, sk)
                vals = [vcx[...]] if win is None else [vw[win, :], vcx[...]]
                dps = [_dot_nt(do_p, vv) for vv in vals]
                dr = None
                for pp, dp in zip(probs, dps):
                    t = jnp.sum(pp * dp, axis=1, keepdims=True)
                    dr = t if dr is None else dr + t
                dss = [(pp * (dp - dr) * ATT_SCALE).astype(BF16) for pp, dp in zip(probs, dps)]
                dsink_ref[head:head + 1, :] += jnp.broadcast_to(
                    jnp.sum(-p_sink * dr, axis=0, keepdims=True), (1, 128))
                p_c, ds_c = probs[-1], dss[-1]
                dq = _dot(ds_c, kcx[...])
                dkca[kh] += _dot_tn(ds_c, q_half)
                dvca[kh] += _dot_tn(p_c, do_half)
                if win is not None:
                    dq = dq + _dot(dss[0], kw[win, :])
                    dka[kh, win, :] += _dot_tn(dss[0], q_half)
                    dva[kh, win, :] += _dot_tn(probs[0], do_half)
                return dq

            def lat_block(n, carry):
                r0, start, valid = _window(cfg, n)
                win = pl.ds(start, 3 * BLK)
                for pr in range(2):
                    lanes = slice((kh * 2 + pr) * 128, (kh * 2 + pr + 1) * 128)
                    qp = qr[pl.ds(r0, BLK), lanes]
                    do_p = do_ref[pl.ds(r0, BLK), lanes]
                    q_h, do_h = halves(qp), halves(do_p)
                    dq = None
                    for half, (kw, kcx, vw, vcx) in enumerate(((ka, kca, va, vca), (kb, kcb, vb, vcb))):
                        part = one_head(kh * 4 + pr * 2 + half, qp, q_h[half], do_p, do_h[half],
                                        kw, kcx, vw, vcx, win, valid)
                        dq = part if dq is None else dq + part
                    dqs[pl.ds(r0, BLK), lanes] = dq
                return carry

            lax.fori_loop(0, n_lat // BLK, lat_block, 0)
            for n in range(n_ctx // BLK):
                rows = slice(n * BLK, (n + 1) * BLK)
                for pr in range(2):
                    lanes = slice((kh * 2 + pr) * 128, (kh * 2 + pr + 1) * 128)
                    qp = qc_ref[rows, lanes].astype(BF16)
                    do_p = doc_ref[rows, lanes]
                    q_h, do_h = halves(qp), halves(do_p)
                    dq = None
                    for half, (kcx, vcx) in enumerate(((kca, vca), (kcb, vcb))):
                        part = one_head(kh * 4 + pr * 2 + half, qp, q_h[half], do_p, do_h[half],
                                        None, kcx, None, vcx, None, None)
                        dq = part if dq is None else dq + part
                    dqc_ref[rows, lanes] = dq.astype(BF16)

        def fold(acc):
            r0 = acc[0] + pltpu.roll(acc[0], 64, 1)
            r1 = acc[1] + pltpu.roll(acc[1], 64, 1)
            return jnp.where(lo, r0, r1)

        for gq in range(4):
            sl = slice(gq * 128, (gq + 1) * 128)
            dq_ref[:, sl] = _rope_t(dqs[:, sl], cos_t, sin_t).astype(BF16)
        dk_ref[...] = _rope_t(fold(dka), cos_t, sin_t).astype(BF16)
        dv_ref[...] = fold(dva).astype(BF16)
        dkc_ref[...] = fold(dkca).astype(BF16)
        dvc_ref[...] = fold(dvca).astype(BF16)

    lat = lambda w: pl.BlockSpec((n_lat, w), lambda e: (e, 0))
    ctx = lambda w: pl.BlockSpec((n_ctx, w), lambda e: (e, 0))
    sd = jax.ShapeDtypeStruct
    return pl.pallas_call(
        body, grid=(2,), name=name,
        in_specs=_attn_specs(cfg) + [pl.BlockSpec((n_lat, ATT_W), lambda e: (e, 0)),
                                     pl.BlockSpec((n_ctx, ATT_W), lambda e: (cb + e, 0))],
        out_specs=[lat(ATT_W), lat(128), lat(128), ctx(ATT_W), ctx(128), ctx(128),
                   pl.BlockSpec((None, 8, 128), lambda e: (e, 0, 0))],
        out_shape=[sd((cfg.t_lat, ATT_W), BF16), sd((cfg.t_lat, 128), BF16), sd((cfg.t_lat, 128), BF16),
                   sd((cfg.t_ctx, ATT_W), BF16), sd((cfg.t_ctx, 128), BF16), sd((cfg.t_ctx, 128), BF16),
                   sd((2, 8, 128), F32)],
        scratch_shapes=[pltpu.VMEM((n_lat, ATT_W), BF16)] + [pltpu.VMEM((n_lat, 128), BF16)] * 4
        + [pltpu.VMEM((n_ctx, 128), BF16)] * 4
        + [pltpu.VMEM((n_lat, ATT_W), F32), pltpu.VMEM((2, n_lat, 128), F32), pltpu.VMEM((2, n_lat, 128), F32),
           pltpu.VMEM((2, n_ctx, 128), F32), pltpu.VMEM((2, n_ctx, 128), F32)],
        compiler_params=_params(("parallel",)))(p, p, p, p, p, p, cos, sin, sink_rows, dcat, dcat)


def _shift_down(x, k, row):
    return jnp.where(row >= k, pltpu.roll(x, k, 0), 0.0)


def _shift_up(x, k, row):
    n = x.shape[0]
    return jnp.where(row < n - k, pltpu.roll(x, n - k, 0), 0.0)


def _window_sum(x, r, row):
    below, above, k = x, x, 1
    while k < r:
        below = below + _shift_down(below, k, row)
        above = above + _shift_up(above, k, row)
        k *= 2
    return below + _shift_down(x, r, row) + _shift_up(above, 1, row)


def _inv_count(r, row, n):
    cnt = jnp.minimum(row + r, n - 1) + 1 - jnp.maximum(row - r, 0)
    return 1.0 / cnt.astype(F32)


def _pool_fwd(p, w, scale, n, blk0, n_seg, name):
    def body(u0, u1, u2, u3, w_ref, sc_ref, o_ref):
        row = lax.broadcasted_iota(jnp.int32, (n, 1), 0)
        for g, u_ref in enumerate((u0, u1, u2, u3)):
            u = u_ref[...]
            d = _window_sum(u, POOL_R[g], row) * _inv_count(POOL_R[g], row, n) - u
            o_ref[:, g * 128:(g + 1) * 128] = (_dot(d, w_ref[g]) * sc_ref[:, g * 128:(g + 1) * 128]).astype(BF16)

    return pl.pallas_call(
        body, grid=(n_seg,), name=name,
        in_specs=[pl.BlockSpec((n, 128), functools.partial(lambda g, e: (blk0 + e, 6 + g), g)) for g in range(4)]
        + [pl.BlockSpec((4, 128, 128), lambda e: (0, 0, 0)), pl.BlockSpec((1, 512), lambda e: (0, 0))],
        out_specs=pl.BlockSpec((n, 512), lambda e: (e, 0)),
        out_shape=jax.ShapeDtypeStruct((n_seg * n, 512), BF16),
        compiler_params=_params(("parallel",)))(p, p, p, p, w, scale)


def _pool_bwd(p, w, scale, dcat, n, blk0, n_seg, name):
    def body(u0, u1, u2, u3, w_ref, sc_ref, dp_ref, du_ref, dw_ref, dsc_ref):
        e = pl.program_id(0)

        @pl.when(e == 0)
        def _():
            dw_ref[...] = jnp.zeros_like(dw_ref)
            dsc_ref[...] = jnp.zeros_like(dsc_ref)

        row = lax.broadcasted_iota(jnp.int32, (n, 1), 0)
        for g, u_ref in enumerate((u0, u1, u2, u3)):
            sl = slice(g * 128, (g + 1) * 128)
            u = u_ref[...]
            inv = _inv_count(POOL_R[g], row, n)
            d = _window_sum(u, POOL_R[g], row) * inv - u
            dp = dp_ref[:, sl]
            dsc_ref[:, sl] += jnp.sum(dp * _dot(d, w_ref[g]), axis=0, keepdims=True)
            dyp = dp * sc_ref[:, sl]
            dw_ref[g] += _dot_tn(d, dyp)
            dd = _dot_nt(dyp, w_ref[g])
            du_ref[:, sl] = (_window_sum(dd * inv, POOL_R[g], row) - dd).astype(BF16)

    return pl.pallas_call(
        body, grid=(n_seg,), name=name,
        in_specs=[pl.BlockSpec((n, 128), functools.partial(lambda g, e: (blk0 + e, 6 + g), g)) for g in range(4)]
        + [pl.BlockSpec((4, 128, 128), lambda e: (0, 0, 0)), pl.BlockSpec((1, 512), lambda e: (0, 0)),
           pl.BlockSpec((n, 512), lambda e: (blk0 + e, 1))],
        out_specs=[pl.BlockSpec((n, 512), lambda e: (e, 0)),
                   pl.BlockSpec((4, 128, 128), lambda e: (0, 0, 0)), pl.BlockSpec((1, 512), lambda e: (0, 0))],
        out_shape=[jax.ShapeDtypeStruct((n_seg * n, 512), BF16), jax.ShapeDtypeStruct((4, 128, 128), F32),
                   jax.ShapeDtypeStruct((1, 512), F32)],
        compiler_params=_params(("arbitrary",)))(p, p, p, p, w, scale, dcat)


def _gelu(x):
    t = jnp.tanh(math.sqrt(2.0 / math.pi) * (x + 0.044715 * x * x * x))
    return 0.5 * x * (1.0 + t), t


def _gelu_grad(x, t):
    return 0.5 * (1.0 + t) + 0.5 * x * (1.0 - t * t) * (math.sqrt(2.0 / math.pi) * (1.0 + 3 * 0.044715 * x * x))


def _neg_expm1(x):
    series = -x * (1.0 + x * (0.5 + x * (1.0 / 6.0 + x * (1.0 / 24.0 + x * (1.0 / 120.0)))))
    return jnp.where(x > -0.05, series, 1.0 - jnp.exp(x))


def _softplus_neg(lam):
    x = -lam
    e = jnp.exp(-jnp.abs(x))
    log1p = jnp.where(e < 1e-2, e * (1.0 - e * (0.5 - e * (1.0 / 3.0))), jnp.log(1.0 + e))
    return jnp.maximum(x, 0.0) + log1p, -_sigmoid(x)


def _conv(u, w_ref, b_ref, row):
    return (b_ref[...] + _shift_down(u, 1, row) * w_ref[0:1, :] + u * w_ref[1:2, :]
            + _shift_up(u, 1, row) * w_ref[2:3, :] + _shift_up(u, 2, row) * w_ref[3:4, :])


def _lru_gates(uc, d, wa_ref, ba_ref, wx_ref, bx_ref, lam_ref):
    r = _sigmoid(_dot(uc, wa_ref[d]) + ba_ref[d:d + 1, :])
    gi = _sigmoid(_dot(uc, wx_ref[d]) + bx_ref[d:d + 1, :])
    sp, dsp = _softplus_neg(lam_ref[d:d + 1, :])
    la = (-LRU_C) * r * sp
    a = jnp.exp(la)
    sq = jnp.sqrt(_neg_expm1(2.0 * la))
    return r, gi, sp, dsp, a, sq


def _tile_scan(a, b, reverse):
    n = a.shape[0]
    row8 = lax.broadcasted_iota(jnp.int32, (n, 1), 0) & 7
    for k in (1, 2, 4):
        if reverse:
            m = row8 < 8 - k
            a_sh = jnp.where(m, pltpu.roll(a, n - k, 0), 1.0)
            b_sh = jnp.where(m, pltpu.roll(b, n - k, 0), 0.0)
        else:
            m = row8 >= k
            a_sh = jnp.where(m, pltpu.roll(a, k, 0), 1.0)
            b_sh = jnp.where(m, pltpu.roll(b, k, 0), 0.0)
        b = a * b_sh + b
        a = a * a_sh
    return a, b


def _carry_scan(a_ref, b_ref, n, reverse, carry):
    nt8 = n // 8

    def step(i, c):
        t = (nt8 - 1 - i) if reverse else i
        off = pl.multiple_of(t * 8, 8)
        h = a_ref[pl.ds(off, 8), :] * c + b_ref[pl.ds(off, 8), :]
        b_ref[pl.ds(off, 8), :] = h
        return h[0:1, :] if reverse else h[7:8, :]

    return lax.fori_loop(0, nt8, step, carry)


def _chain_scan(segs, reverse):
    carry = jnp.zeros((1, 128), F32)
    for a, b, a_ref, b_ref, n in segs:
        a2, b2 = _tile_scan(a, b, reverse)
        a_ref[...] = a2
        b_ref[...] = b2
        carry = _carry_scan(a_ref, b_ref, n, reverse, carry)


def _lru_specs(cfg):
    n_lat, n_ctx, cb = cfg.n_lat, cfg.n_ctx, cfg.ctx_blk
    return [pl.BlockSpec((n_lat, 128), lambda hb, e: (e, hb)),
            pl.BlockSpec((n_lat, 128), lambda hb, e: (e, 8 + hb)),
            pl.BlockSpec((n_ctx, 128), lambda hb, e: (cb + e, hb)),
            pl.BlockSpec((n_ctx, 128), lambda hb, e: (cb + e, 8 + hb)),
            pl.BlockSpec((4, 128), lambda hb, e: (0, hb)),
            pl.BlockSpec((1, 128), lambda hb, e: (0, hb)),
            pl.BlockSpec((2, None, 128, 128), lambda hb, e: (0, hb, 0, 0)),
            pl.BlockSpec((2, 128), lambda hb, e: (0, hb)),
            pl.BlockSpec((2, None, 128, 128), lambda hb, e: (0, hb, 0, 0)),
            pl.BlockSpec((2, 128), lambda hb, e: (0, hb)),
            pl.BlockSpec((2, 128), lambda hb, e: (0, hb))]


def _lru_fwd(cfg, p, consts, name):
    n_lat, n_ctx = cfg.n_lat, cfg.n_ctx

    def body(gl_ref, ul_ref, gc_ref, uc_ref, cw_ref, cb_ref, wa_ref, ba_ref, wx_ref, bx_ref, lam_ref,
             zl_ref, zc_ref, al, bl, ac, bc):
        row_l = lax.broadcasted_iota(jnp.int32, (n_lat, 1), 0)
        row_c = lax.broadcasted_iota(jnp.int32, (n_ctx, 1), 0)
        uc_l = _conv(ul_ref[...], cw_ref, cb_ref, row_l)
        uc_c = _conv(uc_ref[...], cw_ref, cb_ref, row_c)
        y_l = y_c = None
        for d in range(2):
            _, gi_l, _, _, a_l, sq_l = _lru_gates(uc_l, d, wa_ref, ba_ref, wx_ref, bx_ref, lam_ref)
            _, gi_c, _, _, a_c, sq_c = _lru_gates(uc_c, d, wa_ref, ba_ref, wx_ref, bx_ref, lam_ref)
            _chain_scan([(a_c, sq_c * (gi_c * uc_c), ac, bc, n_ctx), (a_l, sq_l * (gi_l * uc_l), al, bl, n_lat)],
                        reverse=(d == 1))
            y_l = bl[...] if y_l is None else y_l + bl[...]
            y_c = bc[...] if y_c is None else y_c + bc[...]
        zl_ref[...] = (_gelu(gl_ref[...])[0] * y_l).astype(BF16)
        zc_ref[...] = (_gelu(gc_ref[...])[0] * y_c).astype(BF16)

    return pl.pallas_call(
        body, grid=(8, 2), name=name, in_specs=_lru_specs(cfg),
        out_specs=[pl.BlockSpec((n_lat, 128), lambda hb, e: (e, hb)), pl.BlockSpec((n_ctx, 128), lambda hb, e: (e, hb))],
        out_shape=[jax.ShapeDtypeStruct((cfg.t_lat, D), BF16), jax.ShapeDtypeStruct((cfg.t_ctx, D), BF16)],
        scratch_shapes=[pltpu.VMEM((n_lat, 128), F32)] * 2 + [pltpu.VMEM((n_ctx, 128), F32)] * 2,
        compiler_params=_params(("parallel", "arbitrary")))(p, p, p, p, *consts)


def _lru_bwd(cfg, p, dz, consts, name):
    n_lat, n_ctx, cb = cfg.n_lat, cfg.n_ctx, cfg.ctx_blk

    def body(gl_ref, ul_ref, gc_ref, uc_ref, cw_ref, cb_ref, wa_ref, ba_ref, wx_ref, bx_ref, lam_ref,
             dzl_ref, dzc_ref, dgl_ref, dul_ref, dgc_ref, duc_ref, dwa_ref, dwx_ref, vec_ref,
             al, bl, ac, bc, hl, hc, kept_l, kept_c):
        e = pl.program_id(1)

        @pl.when(e == 0)
        def _():
            dwa_ref[...] = jnp.zeros_like(dwa_ref)
            dwx_ref[...] = jnp.zeros_like(dwx_ref)
            vec_ref[...] = jnp.zeros_like(vec_ref)

        row_l = lax.broadcasted_iota(jnp.int32, (n_lat, 1), 0)
        row_c = lax.broadcasted_iota(jnp.int32, (n_ctx, 1), 0)
        u_l, u_c = ul_ref[...], uc_ref[...]
        uc_l = _conv(u_l, cw_ref, cb_ref, row_l)
        uc_c = _conv(u_c, cw_ref, cb_ref, row_c)
        for d in range(2):
            r_l, gi_l, _, _, a_l, sq_l = _lru_gates(uc_l, d, wa_ref, ba_ref, wx_ref, bx_ref, lam_ref)
            r_c, gi_c, _, _, a_c, sq_c = _lru_gates(uc_c, d, wa_ref, ba_ref, wx_ref, bx_ref, lam_ref)
            for i, (v_l, v_c) in enumerate(((r_l, r_c), (gi_l, gi_c), (a_l, a_c), (sq_l, sq_c))):
                kept_l[d, i] = v_l
                kept_c[d, i] = v_c
            _chain_scan([(a_c, sq_c * (gi_c * uc_c), ac, bc, n_ctx), (a_l, sq_l * (gi_l * uc_l), al, bl, n_lat)],
                        reverse=(d == 1))
            hl[d] = bl[...]
            hc[d] = bc[...]
        gel_l, t_l = _gelu(gl_ref[...])
        gel_c, t_c = _gelu(gc_ref[...])
        dz_l, dz_c = dzl_ref[...], dzc_ref[...]
        dgl_ref[...] = (dz_l * (hl[0] + hl[1]) * _gelu_grad(gl_ref[...], t_l)).astype(BF16)
        dgc_ref[...] = (dz_c * (hc[0] + hc[1]) * _gelu_grad(gc_ref[...], t_c)).astype(BF16)
        dy_l, dy_c = dz_l * gel_l, dz_c * gel_c
        duc_l = jnp.zeros((n_lat, 128), F32)
        duc_c = jnp.zeros((n_ctx, 128), F32)
        for d in range(2):
            r_l, gi_l, a_l, sq_l = [kept_l[d, i] for i in range(4)]
            r_c, gi_c, a_c, sq_c = [kept_c[d, i] for i in range(4)]
            sp, dsp = _softplus_neg(lam_ref[d:d + 1, :])
            if d == 0:
                an_l = _shift_up(a_l, 1, row_l)
                an_c = jnp.where(row_c < n_ctx - 1, pltpu.roll(a_c, n_ctx - 1, 0), a_l[0:1, :])
            else:
                an_l = _shift_down(a_l, 1, row_l)
                an_c = jnp.where(row_c >= 1, pltpu.roll(a_c, 1, 0), a_l[n_lat - 1:n_lat, :])
            _chain_scan([(an_l, dy_l, al, bl, n_lat), (an_c, dy_c, ac, bc, n_ctx)], reverse=(d == 0))
            dsp_sum = jnp.zeros((1, 128), F32)
            for (dh, h, r, gi, a, sq, uc, seg) in ((bl[...], hl[d], r_l, gi_l, a_l, sq_l, uc_l, "l"),
                                                  (bc[...], hc[d], r_c, gi_c, a_c, sq_c, uc_c, "c")):
                b0 = sq * (gi * uc)
                t1 = dh * sq
                dla = dh * (h - b0) - (dh * gi * uc) * (a * a) / sq
                dzr = (dla * ((-LRU_C) * sp)) * r * (1.0 - r)
                dzi = (t1 * uc) * gi * (1.0 - gi)
                dsp_sum = dsp_sum + jnp.sum(dla * ((-LRU_C) * r), axis=0, keepdims=True)
                dwa_ref[d] += _dot_tn(uc, dzr)
                dwx_ref[d] += _dot_tn(uc, dzi)
                vec_ref[d:d + 1, :] += jnp.sum(dzr, axis=0, keepdims=True)
                vec_ref[2 + d:3 + d, :] += jnp.sum(dzi, axis=0, keepdims=True)
                duc = t1 * gi + _dot_nt(dzr, wa_ref[d]) + _dot_nt(dzi, wx_ref[d])
                if seg == "l":
                    duc_l = duc_l + duc
                else:
                    duc_c = duc_c + duc
            vec_ref[4 + d:5 + d, :] += dsp_sum * dsp
        for duc, u, row, du_ref in ((duc_l, u_l, row_l, dul_ref), (duc_c, u_c, row_c, duc_ref)):
            du_ref[...] = (_shift_up(duc, 1, row) * cw_ref[0:1, :] + duc * cw_ref[1:2, :]
                           + _shift_down(duc, 1, row) * cw_ref[2:3, :]
                           + _shift_down(duc, 2, row) * cw_ref[3:4, :]).astype(BF16)
            vec_ref[6:7, :] += jnp.sum(duc * _shift_down(u, 1, row), axis=0, keepdims=True)
            vec_ref[7:8, :] += jnp.sum(duc * u, axis=0, keepdims=True)
            vec_ref[8:9, :] += jnp.sum(duc * _shift_up(u, 1, row), axis=0, keepdims=True)
            vec_ref[9:10, :] += jnp.sum(duc * _shift_up(u, 2, row), axis=0, keepdims=True)
            vec_ref[10:11, :] += jnp.sum(duc, axis=0, keepdims=True)

    lat = pl.BlockSpec((n_lat, 128), lambda hb, e: (e, hb))
    ctx = pl.BlockSpec((n_ctx, 128), lambda hb, e: (e, hb))
    wspec = pl.BlockSpec((2, None, 128, 128), lambda hb, e: (0, hb, 0, 0))
    sd = jax.ShapeDtypeStruct
    return pl.pallas_call(
        body, grid=(8, 2), name=name,
        in_specs=_lru_specs(cfg) + [pl.BlockSpec((n_lat, 128), lambda hb, e: (e, hb)),
                                    pl.BlockSpec((n_ctx, 128), lambda hb, e: (cb + e, hb))],
        out_specs=[lat, lat, ctx, ctx, wspec, wspec, pl.BlockSpec((None, 16, 128), lambda hb, e: (hb, 0, 0))],
        out_shape=[sd((cfg.t_lat, D), BF16), sd((cfg.t_lat, D), BF16), sd((cfg.t_ctx, D), BF16), sd((cfg.t_ctx, D), BF16),
                   sd((2, 8, 128, 128), F32), sd((2, 8, 128, 128), F32), sd((8, 16, 128), F32)],
        scratch_shapes=[pltpu.VMEM((n_lat, 128), F32)] * 2 + [pltpu.VMEM((n_ctx, 128), F32)] * 2
        + [pltpu.VMEM((2, n_lat, 128), F32), pltpu.VMEM((2, n_ctx, 128), F32),
           pltpu.VMEM((2, 4, n_lat, 128), F32), pltpu.VMEM((2, 4, n_ctx, 128), F32)],
        compiler_params=_params(("parallel", "arbitrary")))(p, p, p, p, *consts, dz, dz)


def _position():
    x, y, c = lax.axis_index("x"), lax.axis_index("y"), lax.axis_index("c")
    return x, y, c, 4 * x + 2 * y + c


def _peer(x, y, c, k):
    px = 1 - x if k & 4 else x
    py = 1 - y if k & 2 else y
    pc = 1 - c if k & 1 else c
    return (px, py, pc), 4 * px + 2 * py + pc


def _all_gather(v, name, in_vmem):
    def body(v_ref, o_ref, send_sems, recv_sems, local_sem):
        x, y, c, me = _position()
        mine = pltpu.make_async_copy(v_ref, o_ref.at[me], local_sem)
        mine.start()
        sends = []
        for k in range(1, N_DEV):
            peer, _ = _peer(x, y, c, k)
            cp = pltpu.make_async_remote_copy(src_ref=v_ref, dst_ref=o_ref.at[me], send_sem=send_sems.at[k - 1],
                                              recv_sem=recv_sems.at[k - 1], device_id=peer, device_id_type=MESH)
            cp.start()
            sends.append(cp)
        for k in range(1, N_DEV):
            peer, peer_lin = _peer(x, y, c, k)
            pltpu.make_async_remote_copy(src_ref=v_ref, dst_ref=o_ref.at[peer_lin], send_sem=send_sems.at[k - 1],
                                         recv_sem=recv_sems.at[k - 1], device_id=peer, device_id_type=MESH).wait_recv()
        for cp in sends:
            cp.wait_send()
        mine.wait()

    space = pltpu.VMEM if in_vmem else pl.ANY
    return pl.pallas_call(
        body, name=name,
        in_specs=[pl.BlockSpec(memory_space=space)], out_specs=pl.BlockSpec(memory_space=space),
        out_shape=jax.ShapeDtypeStruct((N_DEV,) + v.shape, v.dtype),
        scratch_shapes=[pltpu.SemaphoreType.DMA((N_DEV - 1,)), pltpu.SemaphoreType.DMA((N_DEV - 1,)),
                        pltpu.SemaphoreType.DMA],
        compiler_params=pltpu.CompilerParams(vmem_limit_bytes=VMEM_LIMIT))(v)


_HBM = pl.BlockSpec(memory_space=pltpu.HBM)
_SEM = pl.BlockSpec(memory_space=pltpu.SEMAPHORE)
_EFFECT = pltpu.SideEffectType.DATAFLOW_SIDE_EFFECTING


def _push_start(src, land, block_of, name):
    def body(src_ref, land_ref, send_sem, recv_sem, src_thru, land_thru, token):
        x, y, c, me = _position()
        for k in range(1, N_DEV):
            peer, peer_lin = _peer(x, y, c, k)
            mine, there = block_of(src_ref, land_ref, me, peer_lin)
            pltpu.make_async_remote_copy(src_ref=mine, dst_ref=there, send_sem=send_sem, recv_sem=recv_sem,
                                         device_id=peer, device_id_type=MESH).start()
        token[...] = jnp.zeros_like(token)

    return pl.pallas_call(
        body, name=name,
        out_shape=(pltpu.SemaphoreType.DMA(()), pltpu.SemaphoreType.DMA(()), pltpu.HBM(src.shape, src.dtype),
                   pltpu.HBM(land.shape, land.dtype), jax.ShapeDtypeStruct((8, 128), F32)),
        in_specs=(_HBM, _HBM), out_specs=(_SEM, _SEM, _HBM, _HBM, pl.BlockSpec(memory_space=pltpu.VMEM)),
        input_output_aliases={0: 2, 1: 3},
        compiler_params=pltpu.CompilerParams(has_side_effects=_EFFECT),
    )(pltpu.with_memory_space_constraint(src, pltpu.HBM), pltpu.with_memory_space_constraint(land, pltpu.HBM))


def _push_wait(handle, seven_of, after, name):
    send_sem, recv_sem, src_thru, land_thru, _ = handle

    def body(src_ref, land_ref, send_sem, recv_sem, after_ref, src_dead, got_ref):
        x, y, c, _ = _position()
        seven = seven_of(land_ref)
        cp = pltpu.make_async_remote_copy(src_ref=seven, dst_ref=seven, send_sem=send_sem, recv_sem=recv_sem,
                                          device_id=(x, y, 1 - c), device_id_type=MESH)
        cp.wait_send()
        cp.wait_recv()

    return pl.pallas_call(
        body, name=name,
        out_shape=(pltpu.HBM(src_thru.shape, src_thru.dtype), pltpu.HBM(land_thru.shape, land_thru.dtype)),
        in_specs=(_HBM, _HBM, _SEM, _SEM, pl.BlockSpec(memory_space=pl.ANY)), out_specs=(_HBM, _HBM),
        input_output_aliases={0: 0, 1: 1},
        compiler_params=pltpu.CompilerParams(has_side_effects=_EFFECT),
    )(src_thru, land_thru, send_sem, recv_sem, after)[1]


def _gather_start(src, me, name):
    g, r, C = src.shape
    land = lax.dynamic_update_slice(lax.empty((g, N_DEV * r, C), src.dtype), src, (0, me * r, 0))
    return _push_start(src, land, lambda s, z, i, p: (s, z.at[:, pl.ds(i * r, r), :]), name)


def _gather_wait(handle, after, name):
    r = handle[2].shape[1]
    return _push_wait(handle, lambda z: z.at[:, pl.ds(0, (N_DEV - 1) * r), :], after, name)


def _exchange_start(grad, me, name):
    g, rows, C = grad.shape
    r = rows // N_DEV
    mine = lax.dynamic_slice_in_dim(grad, me * r, r, axis=1)[None]
    land = lax.dynamic_update_slice(lax.empty((N_DEV, g, r, C), grad.dtype), mine, (me, 0, 0, 0))
    return _push_start(grad, land, lambda s, z, i, p: (s.at[:, pl.ds(p * r, r), :], z.at[i]), name)


def _exchange_wait(handle, after, name):
    return _push_wait(handle, lambda z: z.at[pl.ds(0, N_DEV - 1)], after, name)


def _sum_blocks(v, name):
    k, rows, cols = v.shape
    tr = rows
    for cand in (rows, 512, 352, 256, 176, 128, 64, 32, 16):
        if rows % cand == 0 and k * cand * cols * v.dtype.itemsize <= 6 * 1024 * 1024:
            tr = cand
            break

    def body(v_ref, o_ref):
        acc = v_ref[0].astype(F32)
        for s in range(1, k):
            acc = acc + v_ref[s].astype(F32)
        o_ref[...] = acc

    return pl.pallas_call(
        body, grid=(rows // tr,), name=name,
        in_specs=[pl.BlockSpec((k, tr, cols), lambda i: (0, i, 0))],
        out_specs=pl.BlockSpec((tr, cols), lambda i: (i, 0)),
        out_shape=jax.ShapeDtypeStruct((rows, cols), F32),
        compiler_params=_params(("parallel",)))(v)


def _adam_math(w, g, m, v):
    m2 = B1 * m + (1.0 - B1) * g
    v2 = B2 * v + (1.0 - B2) * (g * g)
    m_hat = m2 / (1.0 - B1 ** STEP)
    v_hat = v2 / (1.0 - B2 ** STEP)
    return -LR * (m_hat / (jnp.sqrt(v_hat) + EPS) + WD * w), m2, v2


def _adamw(w, g, m, v, name):
    shp = w.shape
    rows, cols = (shp[-2], shp[-1]) if len(shp) >= 2 else (1, shp[-1])
    lead = math.prod(shp[:-2]) if len(shp) > 2 else 1
    fits = [t for t in range(8, rows + 1, 8) if rows % t == 0 and t * cols * 4 <= 2 * 1024 * 1024]
    tr = max(fits) if fits else rows

    def body(w_ref, g_ref, m_ref, v_ref, d_ref, m2_ref, v2_ref):
        d_ref[...], m2_ref[...], v2_ref[...] = _adam_math(w_ref[...], g_ref[...], m_ref[...], v_ref[...])

    blk = pl.BlockSpec((None, tr, cols), lambda b, i: (b, i, 0))
    outs = pl.pallas_call(
        body, grid=(lead, rows // tr), name=name, in_specs=[blk] * 4, out_specs=[blk] * 3,
        out_shape=[jax.ShapeDtypeStruct((lead, rows, cols), F32)] * 3,
        compiler_params=_params(("parallel", "parallel")))(*[a.reshape(lead, rows, cols) for a in (w, g, m, v)])
    return [o.reshape(shp) for o in outs]


def _as2d(a):
    n = a.size
    if n % 1024 == 0:
        return a.reshape(n // 1024, 1024)
    if n % 128 == 0:
        return a.reshape(n // 128, 128)
    return a.reshape(1, n)


def _blocks_to_cols(a):
    b = jnp.moveaxis(a, 0, -2)
    return b.reshape(b.shape[:-2] + (b.shape[-2] * b.shape[-1],))


def _pack_rows(parts):
    padded, offs, r = [], [], 0
    for p in parts:
        pad = (-p.shape[0]) % 8
        padded.append(jnp.pad(p, ((0, pad), (0, 0))) if pad else p)
        offs.append(r)
        r += p.shape[0] + pad
    return jnp.concatenate(padded, axis=0), offs


def _silu(x):
    return x * jax.nn.sigmoid(x)


def kernel(x, c, ctx, c_ctx, w_mod, b_mod, ln_g, ln_b, ffn_w_gate, ffn_w_up, ffn_w_down, mix_ab_w_in, attn_sink, pool_w, pool_scale, mix_ab_w_out, lru_w_in, lru_conv_w, lru_conv_b, lru_wa, lru_ba, lru_wx, lru_bx, lru_lambda, lru_w_out, loss_target, m_c_ctx, m_w_mod, m_b_mod, m_ln_g, m_ln_b, m_ffn_w_gate, m_ffn_w_up, m_ffn_w_down, m_mix_ab_w_in, m_attn_sink, m_pool_w, m_pool_scale, m_mix_ab_w_out, m_lru_w_in, m_lru_conv_w, m_lru_conv_b, m_lru_wa, m_lru_ba, m_lru_wx, m_lru_bx, m_lru_lambda, m_lru_w_out, v_c_ctx, v_w_mod, v_b_mod, v_ln_g, v_ln_b, v_ffn_w_gate, v_ffn_w_up, v_ffn_w_down, v_mix_ab_w_in, v_attn_sink, v_pool_w, v_pool_scale, v_mix_ab_w_out, v_lru_w_in, v_lru_conv_w, v_lru_conv_b, v_lru_wa, v_lru_ba, v_lru_wx, v_lru_bx, v_lru_lambda, v_lru_w_out):
    weights = dict(c_ctx=c_ctx, w_mod=w_mod, b_mod=b_mod, ln_g=ln_g, ln_b=ln_b, ffn_w_gate=ffn_w_gate,
                   ffn_w_up=ffn_w_up, ffn_w_down=ffn_w_down, mix_ab_w_in=mix_ab_w_in, attn_sink=attn_sink,
                   pool_w=pool_w, pool_scale=pool_scale, mix_ab_w_out=mix_ab_w_out, lru_w_in=lru_w_in,
                   lru_conv_w=lru_conv_w, lru_conv_b=lru_conv_b, lru_wa=lru_wa, lru_ba=lru_ba, lru_wx=lru_wx,
                   lru_bx=lru_bx, lru_lambda=lru_lambda, lru_w_out=lru_w_out)
    mom_m = dict(c_ctx=m_c_ctx, w_mod=m_w_mod, b_mod=m_b_mod, ln_g=m_ln_g, ln_b=m_ln_b, ffn_w_gate=m_ffn_w_gate,
                 ffn_w_up=m_ffn_w_up, ffn_w_down=m_ffn_w_down, mix_ab_w_in=m_mix_ab_w_in, attn_sink=m_attn_sink,
                 pool_w=m_pool_w, pool_scale=m_pool_scale, mix_ab_w_out=m_mix_ab_w_out, lru_w_in=m_lru_w_in,
                 lru_conv_w=m_lru_conv_w, lru_conv_b=m_lru_conv_b, lru_wa=m_lru_wa, lru_ba=m_lru_ba, lru_wx=m_lru_wx,
                 lru_bx=m_lru_bx, lru_lambda=m_lru_lambda, lru_w_out=m_lru_w_out)
    mom_v = dict(c_ctx=v_c_ctx, w_mod=v_w_mod, b_mod=v_b_mod, ln_g=v_ln_g, ln_b=v_ln_b, ffn_w_gate=v_ffn_w_gate,
                 ffn_w_up=v_ffn_w_up, ffn_w_down=v_ffn_w_down, mix_ab_w_in=v_mix_ab_w_in, attn_sink=v_attn_sink,
                 pool_w=v_pool_w, pool_scale=v_pool_scale, mix_ab_w_out=v_mix_ab_w_out, lru_w_in=v_lru_w_in,
                 lru_conv_w=v_lru_conv_w, lru_conv_b=v_lru_conv_b, lru_wa=v_lru_wa, lru_ba=v_lru_ba, lru_wx=v_lru_wx,
                 lru_bx=v_lru_bx, lru_lambda=v_lru_lambda, lru_w_out=v_lru_w_out)
    names = list(weights)

    n_lat, n_ctx = x.shape[1], ctx.shape[1]
    cfg = _Cfg(n_lat, n_ctx)
    _, _, _, me = _position()
    mcols = w_mod.shape[2]

    def t_bf16(w):
        return jnp.swapaxes(w, -1, -2).astype(BF16)

    def ffn_src(l, i):
        return jnp.stack([t_bf16(ffn_w_gate[l, i]), t_bf16(ffn_w_up[l, i]), ffn_w_down[l, i].astype(BF16)])

    pending = {}

    def start_gathers(items, tok):
        for key, make_src in items:
            pending[key] = _gather_start(make_src() + tok.astype(BF16), me, "gather_start_" + key)
            tok = pending[key][4][0, 0]
        return tok

    def weights_now(key, after):
        return _gather_wait(pending[key], after, "gather_wait_" + key)

    tok = start_gathers([("ffn00", lambda: ffn_src(0, 0))], jnp.zeros((), F32))

    small_names = ["ln_g", "ln_b", "lru_conv_w", "lru_conv_b", "lru_ba", "lru_bx", "lru_lambda"]
    small, small_off = _pack_rows([(c + tok).reshape(-1, 128)] + [weights[n].reshape(-1, 128) for n in small_names])
    small_all = _all_gather(small, "gather_small", True)

    def small_full(idx, shp):
        rows = math.prod(shp) // 128
        return _blocks_to_cols(small_all[:, small_off[idx]:small_off[idx] + rows, :].reshape((N_DEV,) + shp))

    c_all = small_all[:, :2 * D // 128, :].reshape(2 * N_DEV, D)
    ln_g_f, ln_b_f = small_full(1, ln_g.shape), small_full(2, ln_b.shape)
    lru_consts = (small_full(3, lru_conv_w.shape)[0], small_full(4, lru_conv_b.shape), lru_wa[0],
                  small_full(5, lru_ba.shape)[0], lru_wx[0], small_full(6, lru_bx.shape)[0],
                  small_full(7, lru_lambda.shape)[0])

    s_rows = jnp.zeros((32, D), F32).at[:16].set(_silu(c_all)).at[16].set(_silu(c_ctx)).astype(BF16)
    mod_mine = jnp.stack([_matmul(s_rows, w_mod[l], "nn", F32, "mod_fwd", bn_cap=1280) for l in range(2)])
    mod_all = _all_gather(mod_mine.reshape(64, mcols), "gather_mod", True).reshape(N_DEV, 2, 32, mcols)
    tok = start_gathers([("ab_in", lambda: t_bf16(mix_ab_w_in)), ("ab_out", lambda: mix_ab_w_out.astype(BF16)),
                         ("ffn01", lambda: ffn_src(0, 1)), ("ffn10", lambda: ffn_src(1, 0)),
                         ("lru_in", lambda: t_bf16(lru_w_in)), ("lru_out", lambda: lru_w_out.astype(BF16)),
                         ("ffn11", lambda: ffn_src(1, 1))], mod_all[0, 0, 0, 0] * 0.0)
    mod_full = _blocks_to_cols(mod_all) + (b_mod[:, None, :] + tok)
    ex0 = 2 * me
    mods = []
    for l in range(2):
        rows = jnp.stack([lax.dynamic_index_in_dim(mod_full[l], ex0, 0, False),
                          lax.dynamic_index_in_dim(mod_full[l], ex0 + 1, 0, False), mod_full[l, 16]])
        mods.append(rows.reshape(3, N_MOD, D))

    h0 = jnp.concatenate([x.reshape(cfg.t_lat, D), ctx.reshape(cfg.t_ctx, D)], axis=0)
    cos, sin = _rope_tables(n_lat)
    sink_rows = jnp.broadcast_to(attn_sink[0][:, None], (8, 128)).astype(F32)

    saved = []
    wf = [[None, None], [None, None]]
    h = h0
    xin = _modulate(cfg, h0, mods[0], 0, 1, "modulate_in")
    for l in range(2):
        st = {"h_in": h, "xin1": xin}
        wf[l][0] = weights_now("ffn%d0" % l, xin)
        g1, u1, y1 = _ffn_fwd(xin, wf[l][0], "ffn_fwd")
        h1, xhat1, rstd1, xin2 = _ln_fwd(cfg, h, y1, mods[l], 2, 0.5, ln_g_f[l, 0][None], ln_b_f[l, 0][None],
                                          mods[l], (3, 4), "ln_fwd_a")
        st.update(g1=g1, u1=u1, y1=y1, h1=h1, xhat1=xhat1, rstd1=rstd1, xin2=xin2)
        if l == 0:
            w_ab_in_t = weights_now("ab_in", xin2)[0]
            p = _matmul(xin2, w_ab_in_t, "nt", F32, "mix_ab_in")
            att_l, att_c = _attn_fwd(cfg, p, cos, sin, sink_rows, "attn_fwd")
            pool_l = _pool_fwd(p, pool_w[0], pool_scale, n_lat, 0, 2, "pool_fwd_lat")
            pool_c = _pool_fwd(p, pool_w[0], pool_scale, n_ctx, cfg.ctx_blk, 2, "pool_fwd_ctx")
            cat = jnp.concatenate([jnp.concatenate([att_l, pool_l], axis=1),
                                   jnp.concatenate([att_c, pool_c], axis=1)], axis=0)
            w_ab_out = weights_now("ab_out", cat)[0]
            y2 = _matmul(cat, w_ab_out, "nn", F32, "mix_ab_out")
        else:
            w_lru_in_t = weights_now("lru_in", xin2)[0]
            p = _matmul(xin2, w_lru_in_t, "nt", F32, "lru_in")
            z_l, z_c = _lru_fwd(cfg, p, lru_consts, "lru_fwd")
            cat = jnp.concatenate([z_l, z_c], axis=0)
            w_lru_out = weights_now("lru_out", cat)[0]
            y2 = _matmul(cat, w_lru_out, "nn", F32, "lru_out")
        h2, xhat2, rstd2, xin3 = _ln_fwd(cfg, h1, y2, mods[l], 5, 1.0, ln_g_f[l, 1][None], ln_b_f[l, 1][None],
                                          mods[l], (6, 7), "ln_fwd_b")
        wf[l][1] = weights_now("ffn%d1" % l, xin3)
        g3, u3, y3 = _ffn_fwd(xin3, wf[l][1], "ffn_fwd")
        if l == 0:
            h3, xhat3, rstd3, xin = _ln_fwd(cfg, h2, y3, mods[l], 8, 0.5, ln_g_f[l, 2][None], ln_b_f[l, 2][None],
                                            mods[1], (0, 1), "ln_fwd_a")
        else:
            h3, xhat3, rstd3 = _ln_fwd(cfg, h2, y3, mods[l], 8, 0.5, ln_g_f[l, 2][None], ln_b_f[l, 2][None],
                                       None, None, "ln_fwd_last")
        st.update(p=p, cat=cat, y2=y2, h2=h2, xhat2=xhat2, rstd2=rstd2, xin3=xin3, g3=g3, u3=u3, y3=y3,
                  xhat3=xhat3, rstd3=rstd3)
        saved.append(st)
        h = h3

    dy, loss_tile = _loss(cfg, h, loss_target.reshape(cfg.t_lat, D), "loss")
    loss = lax.psum(loss_tile[0, 0], ("x", "y", "c"))

    grads = {}
    dmod = [None, None]
    recv_ffn = [[None, None], [None, None]]
    dln_g = [[None] * 3, [None] * 3]
    dln_b = [[None] * 3, [None] * 3]

    def ffn_weight_grads(tag, xin_b, dg, du, a_act, dys):
        parts = [_matmul(dg, xin_b, "tn", BF16, "ffn_dw", bm_cap=1408, bk_cap=2304)[None],
                 _matmul(du, xin_b, "tn", BF16, "ffn_dw", bm_cap=1408, bk_cap=2304)[None],
                 _matmul(a_act, dys, "tn", BF16, "ffn_dw", bm_cap=1408, bk_cap=2304)[None]]
        return [_exchange_start(part, me, "exchange_start_ffn%s_%d" % (tag, k)) for k, part in enumerate(parts)]

    def pin(handles):
        total = handles[0][4][0, 0]
        for hd in handles[1:]:
            total = total + hd[4][0, 0]
        return total

    up = (dy,)
    dmod_next = None
    last_sent = None
    for l in (1, 0):
        st = saved[l]
        dm = [None] * N_MOD

        def put_stats(stats, gate_idx, nxt):
            dm[gate_idx] = stats[:, 2, :]
            if nxt is not None:
                nxt[0][nxt[1]] = stats[:, 4, :]
                nxt[0][nxt[1] + 1] = stats[:, 3, :]

        lng3 = ln_g_f[l, 2][None] if last_sent is None else ln_g_f[l, 2][None] + pin(last_sent)
        dres, dys, stats = _ln_bwd(cfg, up, st["xhat3"], st["rstd3"], st["y3"], mods[l], 8, 0.5,
                                   lng3, "ln_bwd_fused" if len(up) > 1 else "ln_bwd_last")
        put_stats(stats, 8, None if len(up) == 1 else (dmod_next, 0))
        dln_g[l][2], dln_b[l][2] = stats[:, 0, :].sum(0), stats[:, 1, :].sum(0)
        dg, du, a_act, dxin = _ffn_bwd(dys, st["g3"], st["u3"], wf[l][1], "ffn_bwd")
        recv_ffn[l][1] = ffn_weight_grads("%d1" % l, st["xin3"], dg, du, a_act, dys)
        dres, dys, stats = _ln_bwd(cfg, (dres, dxin, st["h2"], mods[l], 7), st["xhat2"], st["rstd2"], st["y2"],
                                   mods[l], 5, 1.0, ln_g_f[l, 1][None] + pin(recv_ffn[l][1]), "ln_bwd_fused")
        put_stats(stats, 5, (dm, 6))
        dln_g[l][1], dln_b[l][1] = stats[:, 0, :].sum(0), stats[:, 1, :].sum(0)
        if l == 0:
            dw_out = _matmul(st["cat"], dys, "tn", BF16, "mix_ab_dw_out")
            dcat = _matmul(dys, w_ab_out, "nt", F32, "mix_ab_dcat")
            dq, dk, dv, dqc, dkc, dvc, dsink = _attn_bwd(cfg, st["p"], dcat, cos, sin, sink_rows, "attn_bwd")
            du_l, dpw_l, dps_l = _pool_bwd(st["p"], pool_w[0], pool_scale, dcat, n_lat, 0, 2, "pool_bwd_lat")
            du_c, dpw_c, dps_c = _pool_bwd(st["p"], pool_w[0], pool_scale, dcat, n_ctx, cfg.ctx_blk, 2, "pool_bwd_ctx")
            dp = jnp.concatenate([jnp.concatenate([dq, dk, dv, du_l], axis=1),
                                  jnp.concatenate([dqc, dkc, dvc, du_c], axis=1)], axis=0)
            dw_in_t = _matmul(dp, st["xin2"], "tn", BF16, "mix_ab_dw_in", bm_cap=1280)
            dxin = _matmul(dp, w_ab_in_t, "nn", F32, "mix_ab_dx")
            recv_mix = [_exchange_start(part, me, "exchange_start_mix_ab_%d" % k)
                        for k, part in enumerate((dw_in_t[None], dw_out[None], _as2d(dpw_l + dpw_c)[None]))]
            grads["attn_sink"] = (dsink[0, :, 0] + dsink[1, :, 0])[None, :]
            grads["pool_scale"] = dps_l + dps_c
        else:
            dw_out = _matmul(st["cat"], dys, "tn", BF16, "lru_dw_out")
            dz = _matmul(dys, w_lru_out, "nt", F32, "lru_dz")
            dgl, dul, dgc, duc, dwa, dwx, vec = _lru_bwd(cfg, st["p"], dz, lru_consts, "lru_bwd")
            dp = jnp.concatenate([jnp.concatenate([dgl, dul], axis=1), jnp.concatenate([dgc, duc], axis=1)], axis=0)
            dw_in_t = _matmul(dp, st["xin2"], "tn", BF16, "lru_dw_in", bm_cap=1024)
            dxin = _matmul(dp, w_lru_in_t, "nn", F32, "lru_dx")
            recv_mix = [_exchange_start(part, me, "exchange_start_lru_%d" % k)
                        for k, part in enumerate((dw_in_t[None], dw_out[None], _as2d(dwa)[None], _as2d(dwx)[None]))]
            vec_t = jnp.moveaxis(vec, 0, 1).reshape(16, D)
            grads["lru_ba"], grads["lru_bx"] = vec_t[0:2], vec_t[2:4]
            grads["lru_lambda"], grads["lru_conv_w"], grads["lru_conv_b"] = vec_t[4:6], vec_t[6:10], vec_t[10:11]
        if l == 0:
            recv_ab = recv_mix
        else:
            recv_lru = recv_mix
        dres, dys, stats = _ln_bwd(cfg, (dres, dxin, st["h1"], mods[l], 4), st["xhat1"], st["rstd1"], st["y1"],
                                   mods[l], 2, 0.5, ln_g_f[l, 0][None] + pin(recv_mix), "ln_bwd_fused")
        put_stats(stats, 2, (dm, 3))
        dln_g[l][0], dln_b[l][0] = stats[:, 0, :].sum(0), stats[:, 1, :].sum(0)
        dg, du, a_act, dxin = _ffn_bwd(dys, st["g1"], st["u1"], wf[l][0], "ffn_bwd")
        recv_ffn[l][0] = ffn_weight_grads("%d0" % l, st["xin1"], dg, du, a_act, dys)
        last_sent = recv_ffn[l][0]
        dmod[l] = dm
        dmod_next = dm
        up = (dres, dxin, st["h_in"], mods[l], 1)
    dh0, stats = _modulate_bwd(cfg, up[0], up[1], h0, mods[0] + pin(last_sent), 1, "modulate_bwd")
    dmod[0][0], dmod[0][1] = stats[:, 4, :], stats[:, 3, :]
    grad_x = dh0[:cfg.t_lat].reshape(x.shape)

    def arrived(handle, name):
        return _exchange_wait(handle, dh0, name)

    recv_ffn = [[[arrived(hd, "exchange_wait_ffn%d%d_%d" % (l, i, k)) for k, hd in enumerate(recv_ffn[l][i])]
                 for i in range(2)] for l in range(2)]
    recv_ab = [arrived(hd, "exchange_wait_mix_ab_%d" % k) for k, hd in enumerate(recv_ab)]
    recv_lru = [arrived(hd, "exchange_wait_lru_%d" % k) for k, hd in enumerate(recv_lru)]

    dmod_mine = jnp.stack([jnp.stack(dmod[l], axis=1).reshape(3, N_MOD * D) for l in range(2)])
    n_dm = 6 * N_MOD * D // 128
    dmod_all = _all_gather(dmod_mine.reshape(n_dm, 128), "gather_dmod", True)
    dmod_sum = _sum_blocks(dmod_all, "sum_dmod").reshape(2, 3, N_MOD * D)
    dmod_all = dmod_all.reshape(N_DEV, 2, 3, N_MOD * D)
    grads["b_mod"] = dmod_sum[:, 0] + dmod_sum[:, 1] + dmod_sum[:, 2]
    dmod_ex = jnp.moveaxis(dmod_all[:, :, 0:2, :], 1, 0).reshape(2, 2 * N_DEV, N_MOD * D)
    dm_rows = jnp.zeros((2, 32, N_MOD * D), F32).at[:, :16].set(dmod_ex).at[:, 16].set(dmod_sum[:, 2])
    dm_cols = lax.dynamic_slice_in_dim(dm_rows, me * mcols, mcols, axis=2).astype(BF16)
    grads["w_mod"] = jnp.stack([_matmul(s_rows, dm_cols[l], "tn", F32, "mod_dw", bn_cap=1280) for l in range(2)])
    ds_part = None
    for l in range(2):
        part = _matmul(dm_cols[l, 16:32], w_mod[l], "nt", F32, "mod_ds", bk_cap=1280)[0]
        ds_part = part if ds_part is None else ds_part + part

    def shard_sum(recv, name):
        return _sum_blocks(recv.reshape(N_DEV, recv.shape[2], recv.shape[3]), name)

    gate_g = [[None, None], [None, None]]
    up_g = [[None, None], [None, None]]
    down_g = [[None, None], [None, None]]
    for l in range(2):
        for i in range(2):
            gt, ut, dn = [shard_sum(r, "sum_ffn") for r in recv_ffn[l][i]]
            gate_g[l][i], up_g[l][i], down_g[l][i] = gt.T, ut.T, dn
    grads["ffn_w_gate"] = jnp.stack([jnp.stack(gate_g[l]) for l in range(2)])
    grads["ffn_w_up"] = jnp.stack([jnp.stack(up_g[l]) for l in range(2)])
    grads["ffn_w_down"] = jnp.stack([jnp.stack(down_g[l]) for l in range(2)])
    grads["mix_ab_w_in"] = shard_sum(recv_ab[0], "sum_mix_in").T[None]
    grads["mix_ab_w_out"] = shard_sum(recv_ab[1], "sum_mix_out")[None]
    grads["lru_w_in"] = shard_sum(recv_lru[0], "sum_lru_in").T[None]
    grads["lru_w_out"] = shard_sum(recv_lru[1], "sum_lru_out")[None]
    rep_parts = [shard_sum(recv_lru[2], "sum_rep"), shard_sum(recv_lru[3], "sum_rep"), shard_sum(recv_ab[2], "sum_rep")]
    rep_names = ["lru_wa", "lru_wx", "pool_w"]

    dln_g_f = jnp.stack([jnp.stack(dln_g[l]) for l in range(2)])
    dln_b_f = jnp.stack([jnp.stack(dln_b[l]) for l in range(2)])
    sink_pad = jnp.zeros((1, 128), F32).at[0, :8].set(grads["attn_sink"][0])
    part_list = [p_.reshape(-1, 128) for p_ in rep_parts] + [
        dln_g_f.reshape(-1, 128), dln_b_f.reshape(-1, 128), grads["lru_conv_w"].reshape(-1, 128),
        grads["lru_conv_b"].reshape(-1, 128), grads["lru_ba"].reshape(-1, 128), grads["lru_bx"].reshape(-1, 128),
        grads["lru_lambda"].reshape(-1, 128), ds_part.reshape(-1, 128), sink_pad, grads["pool_scale"].reshape(-1, 128)]
    parts, part_off = _pack_rows(part_list)
    parts_all = _all_gather(parts, "gather_partials", True)
    parts_sum = _sum_blocks(parts_all, "sum_partials")

    for i, n in enumerate(rep_names):
        rows = part_list[i].shape[0]
        grads[n] = parts_all[:, part_off[i]:part_off[i] + rows, :].reshape(weights[n].shape)

    def take(idx):
        return parts_sum[part_off[idx]:part_off[idx] + part_list[idx].shape[0]]

    def my_cols(full, shp):
        w = shp[-1]
        return lax.dynamic_slice_in_dim(full, me * w, w, axis=full.ndim - 1)

    grads["ln_g"] = my_cols(take(3).reshape(2, 3, D), ln_g.shape)
    grads["ln_b"] = my_cols(take(4).reshape(2, 3, D), ln_b.shape)
    grads["lru_conv_w"] = my_cols(take(5).reshape(1, 4, D), lru_conv_w.shape)
    grads["lru_conv_b"] = my_cols(take(6).reshape(1, D), lru_conv_b.shape)
    grads["lru_ba"] = my_cols(take(7).reshape(1, 2, D), lru_ba.shape)
    grads["lru_bx"] = my_cols(take(8).reshape(1, 2, D), lru_bx.shape)
    grads["lru_lambda"] = my_cols(take(9).reshape(1, 2, D), lru_lambda.shape)
    sg = jax.nn.sigmoid(c_ctx)
    grads["c_ctx"] = take(10).reshape(D) * (sg * (1.0 + c_ctx * (1.0 - sg)))
    grads["attn_sink"] = take(11)[:, :8]
    grads["pool_scale"] = take(12).reshape(pool_scale.shape)

    delta, new_m, new_v = {}, {}, {}
    for n in names:
        shp = weights[n].shape
        grads[n] = grads[n].reshape(shp)
        delta[n], new_m[n], new_v[n] = _adamw(weights[n], grads[n], mom_m[n], mom_v[n], "adamw")

    return (loss, grad_x, *[grads[n] for n in names], *[delta[n] for n in names],
            *[new_m[n] for n in names], *[new_v[n] for n in names])
```

```python
import functools
import math

import jax
import jax.numpy as jnp
from jax import lax
from jax.experimental import pallas as pl
from jax.experimental.pallas import tpu as pltpu

F32 = jnp.float32
BF16 = jnp.bfloat16
MESH = pl.DeviceIdType.MESH

D = 1024
N_MOD = 9
N_DEV = 8
HEAD_DIM = 64
ATT_HEADS = 8
KV_HEADS = 2
ATT_W = 512
BLK = 128
ATT_SCALE = HEAD_DIM ** -0.5
GRID_W = 64
ROPE_FREQS = HEAD_DIM // 4
ROPE_THETA = 10000.0
POOL_R = (1, 2, 4, 8)
LRU_C = 8.0
LN_EPS = 1e-5
NEG_INF = -1e30
ALPHA = 4.0 ** 0.25
LR, B1, B2, EPS, WD, STEP = 0.001, 0.9, 0.999, 1e-08, 0.01, 10
VMEM_LIMIT = 56 * 1024 * 1024
ROW_TILE = 512


def _params(sem=None):
    if sem is None:
        return pltpu.CompilerParams(vmem_limit_bytes=VMEM_LIMIT)
    return pltpu.CompilerParams(dimension_semantics=sem, vmem_limit_bytes=VMEM_LIMIT)


def _sigmoid(x):
    return 0.5 * jnp.tanh(0.5 * x) + 0.5


def _dot(a, b):
    return jnp.dot(a.astype(BF16), b.astype(BF16), preferred_element_type=F32)


def _dot_nt(a, b):
    return lax.dot_general(a.astype(BF16), b.astype(BF16), (((1,), (1,)), ((), ())), preferred_element_type=F32)


def _dot_tn(a, b):
    return lax.dot_general(a.astype(BF16), b.astype(BF16), (((0,), (0,)), ((), ())), preferred_element_type=F32)


def _pick(n, cap):
    best = None
    for m in range(128, min(n, cap) + 1, 128):
        if n % m == 0:
            best = m
    return n if best is None else best


def _chunks(width, step=256):
    out, c = [], 0
    while c < width:
        w = min(step, width - c)
        out.append((c, w))
        c += w
    return out


class _Cfg:
    def __init__(self, n_lat, n_ctx):
        self.n_lat, self.n_ctx = n_lat, n_ctx
        self.t_lat, self.t_ctx = 2 * n_lat, 2 * n_ctx
        self.T = self.t_lat + self.t_ctx
        self.tm = min(ROW_TILE, self.t_ctx)
        assert n_lat % self.tm == 0 and self.t_ctx % self.tm == 0 and n_lat >= 3 * BLK and n_ctx % BLK == 0
        self.nt = self.T // self.tm
        self.nlt = n_lat // self.tm
        self.ctx_blk = self.t_lat // n_ctx

    def seg(self, i):
        return jnp.minimum(i // self.nlt, 2)

    def first_of_seg(self, i):
        return jnp.where(i < 2 * self.nlt, i % self.nlt == 0, i == 2 * self.nlt)


def _modulate(cfg, h, mod, shift_idx, scale_idx, name):
    tm = cfg.tm

    def body(h_ref, mod_ref, o_ref):
        sh = mod_ref[shift_idx:shift_idx + 1, :]
        sc = mod_ref[scale_idx:scale_idx + 1, :]
        o_ref[...] = (h_ref[...] * (1.0 + sc) + sh).astype(BF16)

    return pl.pallas_call(
        body, grid=(cfg.nt,), name=name,
        in_specs=[pl.BlockSpec((tm, D), lambda i: (i, 0)),
                  pl.BlockSpec((None, N_MOD, D), lambda i: (cfg.seg(i), 0, 0))],
        out_specs=pl.BlockSpec((tm, D), lambda i: (i, 0)),
        out_shape=jax.ShapeDtypeStruct((cfg.T, D), BF16),
        compiler_params=_params(("parallel",)),
    )(h, mod)


def _ln_fwd(cfg, h, y, mod, gate_idx, coef, lng, lnb, mod_next, next_idx, name):
    tm = cfg.tm
    has_next = next_idx is not None

    def body(*refs):
        if has_next:
            h_ref, y_ref, mod_ref, g_ref, b_ref, modn_ref, hn_ref, xhat_ref, rstd_ref, xin_ref = refs
        else:
            h_ref, y_ref, mod_ref, g_ref, b_ref, hn_ref, xhat_ref, rstd_ref = refs
        gate = mod_ref[gate_idx:gate_idx + 1, :]
        z = ALPHA * h_ref[...] + (coef * gate) * y_ref[...]
        mu = jnp.mean(z, axis=-1, keepdims=True)
        zc = z - mu
        var = jnp.mean(zc * zc, axis=-1, keepdims=True)
        rstd = lax.rsqrt(var + LN_EPS)
        xhat = zc * rstd
        hn = xhat * g_ref[...] + b_ref[...]
        hn_ref[...] = hn
        xhat_ref[...] = xhat
        rstd_ref[...] = rstd
        if has_next:
            sh = modn_ref[next_idx[0]:next_idx[0] + 1, :]
            sc = modn_ref[next_idx[1]:next_idx[1] + 1, :]
            xin_ref[...] = (hn * (1.0 + sc) + sh).astype(BF16)

    row = pl.BlockSpec((tm, D), lambda i: (i, 0))
    modspec = pl.BlockSpec((None, N_MOD, D), lambda i: (cfg.seg(i), 0, 0))
    vec = pl.BlockSpec((1, D), lambda i: (0, 0))
    in_specs = [row, row, modspec, vec, vec]
    args = [h, y, mod, lng, lnb]
    out_specs = [row, row, pl.BlockSpec((tm, 1), lambda i: (i, 0))]
    out_shape = [jax.ShapeDtypeStruct((cfg.T, D), F32), jax.ShapeDtypeStruct((cfg.T, D), F32),
                 jax.ShapeDtypeStruct((cfg.T, 1), F32)]
    if has_next:
        in_specs.append(modspec)
        args.append(mod_next)
        out_specs.append(row)
        out_shape.append(jax.ShapeDtypeStruct((cfg.T, D), BF16))
    return pl.pallas_call(body, grid=(cfg.nt,), name=name, in_specs=in_specs, out_specs=out_specs,
                          out_shape=out_shape, compiler_params=_params(("parallel",)))(*args)


def _ln_bwd(cfg, up, xhat, rstd, y, mod, gate_idx, coef, lng, name):
    tm = cfg.tm
    fused = len(up) > 1
    scale_next = up[4] if fused else None

    def body(*refs):
        if fused:
            dres_n, dxin_n, hn_ref, modn_ref, xhat_ref, rstd_ref, y_ref, mod_ref, g_ref, dres_ref, dys_ref, st_ref = refs
        else:
            dhn_ref, xhat_ref, rstd_ref, y_ref, mod_ref, g_ref, dres_ref, dys_ref, st_ref = refs
        i = pl.program_id(0)

        @pl.when(cfg.first_of_seg(i))
        def _():
            st_ref[...] = jnp.zeros_like(st_ref)

        if fused:
            dxin = dxin_n[...]
            sc = modn_ref[scale_next:scale_next + 1, :]
            dhn = dres_n[...] + dxin * (1.0 + sc)
            st_ref[3:4, :] += jnp.sum(dxin * hn_ref[...], axis=0, keepdims=True)
            st_ref[4:5, :] += jnp.sum(dxin, axis=0, keepdims=True)
        else:
            dhn = dhn_ref[...]
        xhat = xhat_ref[...]
        gdh = dhn * g_ref[...]
        m1 = jnp.mean(gdh, axis=-1, keepdims=True)
        m2 = jnp.mean(gdh * xhat, axis=-1, keepdims=True)
        dz = rstd_ref[...] * (gdh - m1 - xhat * m2)
        gate = mod_ref[gate_idx:gate_idx + 1, :]
        dres_ref[...] = ALPHA * dz
        dys_ref[...] = ((coef * gate) * dz).astype(BF16)
        st_ref[0:1, :] += jnp.sum(dhn * xhat, axis=0, keepdims=True)
        st_ref[1:2, :] += jnp.sum(dhn, axis=0, keepdims=True)
        st_ref[2:3, :] += jnp.sum((coef * dz) * y_ref[...], axis=0, keepdims=True)

    row = pl.BlockSpec((tm, D), lambda i: (i, 0))
    modspec = pl.BlockSpec((None, N_MOD, D), lambda i: (cfg.seg(i), 0, 0))
    vec = pl.BlockSpec((1, D), lambda i: (0, 0))
    col = pl.BlockSpec((tm, 1), lambda i: (i, 0))
    if fused:
        in_specs = [row, row, row, modspec, row, col, row, modspec, vec]
        args = [up[0], up[1], up[2], up[3], xhat, rstd, y, mod, lng]
    else:
        in_specs = [row, row, col, row, modspec, vec]
        args = [up[0], xhat, rstd, y, mod, lng]
    return pl.pallas_call(
        body, grid=(cfg.nt,), name=name, in_specs=in_specs,
        out_specs=[row, row, pl.BlockSpec((None, 8, D), lambda i: (cfg.seg(i), 0, 0))],
        out_shape=[jax.ShapeDtypeStruct((cfg.T, D), F32), jax.ShapeDtypeStruct((cfg.T, D), BF16),
                   jax.ShapeDtypeStruct((3, 8, D), F32)],
        compiler_params=_params(("arbitrary",)))(*args)


def _modulate_bwd(cfg, dres, dxin, h, mod, scale_idx, name):
    tm = cfg.tm

    def body(dres_ref, dxin_ref, h_ref, mod_ref, dh_ref, st_ref):
        i = pl.program_id(0)

        @pl.when(cfg.first_of_seg(i))
        def _():
            st_ref[...] = jnp.zeros_like(st_ref)

        dxin = dxin_ref[...]
        sc = mod_ref[scale_idx:scale_idx + 1, :]
        dh_ref[...] = dres_ref[...] + dxin * (1.0 + sc)
        st_ref[3:4, :] += jnp.sum(dxin * h_ref[...], axis=0, keepdims=True)
        st_ref[4:5, :] += jnp.sum(dxin, axis=0, keepdims=True)

    row = pl.BlockSpec((tm, D), lambda i: (i, 0))
    return pl.pallas_call(
        body, grid=(cfg.nt,), name=name,
        in_specs=[row, row, row, pl.BlockSpec((None, N_MOD, D), lambda i: (cfg.seg(i), 0, 0))],
        out_specs=[row, pl.BlockSpec((None, 8, D), lambda i: (cfg.seg(i), 0, 0))],
        out_shape=[jax.ShapeDtypeStruct((cfg.T, D), F32), jax.ShapeDtypeStruct((3, 8, D), F32)],
        compiler_params=_params(("arbitrary",)))(dres, dxin, h, mod)


def _loss(cfg, h, target, name):
    tm = cfg.tm
    n_lt = 2 * cfg.nlt

    def body(h_ref, t_ref, dy_ref, l_ref):
        i = pl.program_id(0)

        @pl.when(i == 0)
        def _():
            l_ref[...] = jnp.zeros_like(l_ref)

        @pl.when(i < n_lt)
        def _():
            err = h_ref[...] - t_ref[...]
            dy_ref[...] = err * (1.0 / D)
            part = jnp.sum(jnp.sum(err * err, axis=1, keepdims=True), axis=0, keepdims=True) * (0.5 / D)
            l_ref[...] += jnp.broadcast_to(part, l_ref.shape)

        @pl.when(i >= n_lt)
        def _():
            dy_ref[...] = jnp.zeros_like(dy_ref)

    return pl.pallas_call(
        body, grid=(cfg.nt,), name=name,
        in_specs=[pl.BlockSpec((tm, D), lambda i: (i, 0)),
                  pl.BlockSpec((tm, D), lambda i: (jnp.minimum(i, n_lt - 1), 0))],
        out_specs=[pl.BlockSpec((tm, D), lambda i: (i, 0)), pl.BlockSpec((8, 128), lambda i: (0, 0))],
        out_shape=[jax.ShapeDtypeStruct((cfg.T, D), F32), jax.ShapeDtypeStruct((8, 128), F32)],
        compiler_params=_params(("arbitrary",)))(h, target)


def _matmul(a, b, mode, out_dtype, name, bm_cap=512, bn_cap=1408, bk_cap=1024):
    if mode == "nn":
        (M, K), N = a.shape, b.shape[1]
    elif mode == "nt":
        (M, K), N = a.shape, b.shape[0]
    else:
        (K, M), N = a.shape, b.shape[1]
    bm, bn, bk = _pick(M, bm_cap), _pick(N, bn_cap), _pick(K, bk_cap)
    nk = K // bk

    def body(a_ref, b_ref, o_ref, acc_ref=None):
        k = pl.program_id(2)
        if mode == "nn":
            part = _dot(a_ref[...], b_ref[...])
        elif mode == "nt":
            part = _dot_nt(a_ref[...], b_ref[...])
        else:
            part = _dot_tn(a_ref[...], b_ref[...])
        if nk == 1:
            o_ref[...] = part.astype(out_dtype)
            return

        @pl.when(k == 0)
        def _():
            acc_ref[...] = part

        @pl.when((k > 0) & (k < nk - 1))
        def _():
            acc_ref[...] += part

        @pl.when(k == nk - 1)
        def _():
            o_ref[...] = (acc_ref[...] + part).astype(out_dtype)

    if mode == "nn":
        a_spec = pl.BlockSpec((bm, bk), lambda i, j, k: (i, k))
        b_spec = pl.BlockSpec((bk, bn), lambda i, j, k: (k, j))
    elif mode == "nt":
        a_spec = pl.BlockSpec((bm, bk), lambda i, j, k: (i, k))
        b_spec = pl.BlockSpec((bn, bk), lambda i, j, k: (j, k))
    else:
        a_spec = pl.BlockSpec((bk, bm), lambda i, j, k: (k, i))
        b_spec = pl.BlockSpec((bk, bn), lambda i, j, k: (k, j))
    return pl.pallas_call(
        body, grid=(M // bm, N // bn, nk), name=name, in_specs=[a_spec, b_spec],
        out_specs=pl.BlockSpec((bm, bn), lambda i, j, k: (i, j)),
        out_shape=jax.ShapeDtypeStruct((M, N), out_dtype),
        scratch_shapes=[pltpu.VMEM((bm, bn), F32)] if nk > 1 else [],
        compiler_params=_params(("parallel", "parallel", "arbitrary")))(a, b)


def _ffn_tile(T, cap):
    best = 256
    for t in range(256, cap + 1, 256):
        if T % t == 0:
            best = t
    return best


def _ffn_fwd(xin, wf, name):
    T = xin.shape[0]
    F = wf.shape[1]
    tm, tf = _ffn_tile(T, 768), F // 2
    assert tf % 128 == 0 and T % tm == 0

    def body(x_ref, wg_ref, wu_ref, wd_ref, g_ref, u_ref, y_ref):
        j = pl.program_id(1)
        x = x_ref[...]
        acc = None
        for c0, cw in _chunks(tf):
            g = _dot_nt(x, wg_ref[c0:c0 + cw, :])
            u = _dot_nt(x, wu_ref[c0:c0 + cw, :])
            g_ref[:, c0:c0 + cw] = g.astype(BF16)
            u_ref[:, c0:c0 + cw] = u.astype(BF16)
            part = _dot(g * _sigmoid(g) * u, wd_ref[c0:c0 + cw, :])
            acc = part if acc is None else acc + part

        @pl.when(j == 0)
        def _():
            y_ref[...] = acc

        @pl.when(j > 0)
        def _():
            y_ref[...] += acc

    return pl.pallas_call(
        body, grid=(T // tm, 2), name=name,
        in_specs=[pl.BlockSpec((tm, D), lambda i, j: (i, 0)),
                  pl.BlockSpec((None, tf, D), lambda i, j: (0, j, 0)),
                  pl.BlockSpec((None, tf, D), lambda i, j: (1, j, 0)),
                  pl.BlockSpec((None, tf, D), lambda i, j: (2, j, 0))],
        out_specs=[pl.BlockSpec((tm, tf), lambda i, j: (i, j)),
                   pl.BlockSpec((tm, tf), lambda i, j: (i, j)),
                   pl.BlockSpec((tm, D), lambda i, j: (i, 0))],
        out_shape=[jax.ShapeDtypeStruct((T, F), BF16), jax.ShapeDtypeStruct((T, F), BF16),
                   jax.ShapeDtypeStruct((T, D), F32)],
        compiler_params=_params(("parallel", "arbitrary")))(xin, wf, wf, wf)


def _ffn_bwd(dys, g, u, wf, name):
    T = dys.shape[0]
    F = wf.shape[1]
    tm, tf = _ffn_tile(T, 512), F // 2

    def body(dy_ref, g_ref, u_ref, wg_ref, wu_ref, wd_ref, dg_ref, du_ref, a_ref, dx_ref):
        j = pl.program_id(1)
        da_all = _dot_nt(dy_ref[...], wd_ref[...])
        for c0, cw in _chunks(tf):
            gg = g_ref[:, c0:c0 + cw].astype(F32)
            uu = u_ref[:, c0:c0 + cw].astype(F32)
            da = da_all[:, c0:c0 + cw]
            s = _sigmoid(gg)
            silu = gg * s
            a_ref[:, c0:c0 + cw] = (silu * uu).astype(BF16)
            du_ref[:, c0:c0 + cw] = (da * silu).astype(BF16)
            dg_ref[:, c0:c0 + cw] = (da * uu * (s * (1.0 + gg * (1.0 - s)))).astype(BF16)
        acc = _dot(dg_ref[...], wg_ref[...]) + _dot(du_ref[...], wu_ref[...])

        @pl.when(j == 0)
        def _():
            dx_ref[...] = acc

        @pl.when(j > 0)
        def _():
            dx_ref[...] += acc

    blk = pl.BlockSpec((tm, tf), lambda i, j: (i, j))
    return pl.pallas_call(
        body, grid=(T // tm, 2), name=name,
        in_specs=[pl.BlockSpec((tm, D), lambda i, j: (i, 0)), blk, blk,
                  pl.BlockSpec((None, tf, D), lambda i, j: (0, j, 0)),
                  pl.BlockSpec((None, tf, D), lambda i, j: (1, j, 0)),
                  pl.BlockSpec((None, tf, D), lambda i, j: (2, j, 0))],
        out_specs=[blk, blk, blk, pl.BlockSpec((tm, D), lambda i, j: (i, 0))],
        out_shape=[jax.ShapeDtypeStruct((T, F), BF16), jax.ShapeDtypeStruct((T, F), BF16),
                   jax.ShapeDtypeStruct((T, F), BF16), jax.ShapeDtypeStruct((T, D), F32)],
        compiler_params=_params(("parallel", "arbitrary")))(dys, g, u, wf, wf, wf)


def _swap_halves(x):
    w = x.shape[1]
    lane = lax.broadcasted_iota(jnp.int32, (1, w), 1)
    return jnp.where((lane & 63) < 32, pltpu.roll(x, w - 32, 1), pltpu.roll(x, 32, 1))


def _rope(x, cos, sin):
    return x * cos + _swap_halves(x) * sin


def _rope_t(dy, cos, sin):
    return dy * cos + _swap_halves(dy * sin)


def _rope_tables(n_lat):
    rows = n_lat // GRID_W
    row = jnp.repeat(jnp.arange(rows, dtype=F32), GRID_W)
    col = jnp.tile(jnp.arange(GRID_W, dtype=F32), rows)
    inv = ROPE_THETA ** (-jnp.arange(ROPE_FREQS, dtype=F32) / ROPE_FREQS)
    ang = jnp.concatenate([row[:, None] * inv, col[:, None] * inv], axis=-1)
    cs, sn = jnp.cos(ang), jnp.sin(ang)
    cos = jnp.concatenate([cs, cs, cs, cs], axis=-1)
    sin = jnp.concatenate([-sn, sn, -sn, sn], axis=-1)
    return cos, sin


def _attn_specs(cfg):
    n_lat, n_ctx, cb = cfg.n_lat, cfg.n_ctx, cfg.ctx_blk
    return [pl.BlockSpec((n_lat, ATT_W), lambda e: (e, 0)),
            pl.BlockSpec((n_lat, 128), lambda e: (e, 4)),
            pl.BlockSpec((n_lat, 128), lambda e: (e, 5)),
            pl.BlockSpec((n_ctx, ATT_W), lambda e: (cb + e, 0)),
            pl.BlockSpec((n_ctx, 128), lambda e: (cb + e, 4)),
            pl.BlockSpec((n_ctx, 128), lambda e: (cb + e, 5)),
            pl.BlockSpec((n_lat, 128), lambda e: (0, 0)),
            pl.BlockSpec((n_lat, 128), lambda e: (0, 0)),
            pl.BlockSpec((8, 128), lambda e: (0, 0))]


def _attn_prepare(kh, kl, vl, kc, vc, ka, kb, va, vb, kca, kcb, vca, vcb):
    lane = lax.broadcasted_iota(jnp.int32, (1, 128), 1)
    own = (lane < 64) if kh == 0 else (lane >= 64)

    def split(x, ra, rb):
        mine = jnp.where(own, x, 0.0)
        other = pltpu.roll(mine, 64, 1)
        a, b = (mine, other) if kh == 0 else (other, mine)
        ra[...] = a.astype(BF16)
        rb[...] = b.astype(BF16)

    split(kl, ka, kb)
    split(vl, va, vb)
    split(kc, kca, kcb)
    split(vc, vca, vcb)


def _softmax_parts(s_list, sk):
    m = sk
    for s in s_list:
        m = jnp.maximum(m, jnp.max(s, axis=1, keepdims=True))
    es = [jnp.exp(s - m) for s in s_list]
    esk = jnp.exp(sk - m)
    den = esk
    for e in es:
        den = den + jnp.sum(e, axis=1, keepdims=True)
    inv = 1.0 / den
    return [e * inv for e in es], esk * inv


def _window(cfg, n):
    r0 = pl.multiple_of(n * BLK, BLK)
    start = pl.multiple_of(jnp.clip((n - 1) * BLK, 0, cfg.n_lat - 3 * BLK), BLK)
    qpos = r0 + lax.broadcasted_iota(jnp.int32, (BLK, 1), 0)
    kpos = start + lax.broadcasted_iota(jnp.int32, (1, 3 * BLK), 1)
    valid = jnp.abs(qpos - kpos) <= BLK
    return r0, start, valid


def _attn_fwd(cfg, p, cos, sin, sink_rows, name):
    n_lat, n_ctx = cfg.n_lat, cfg.n_ctx

    def body(q_ref, k_ref, v_ref, qc_ref, kc_ref, vc_ref, cos_ref, sin_ref, sink_ref, o_ref, oc_ref,
             qr, ka, kb, va, vb, kca, kcb, vca, vcb):
        cos_t, sin_t = cos_ref[...], sin_ref[...]
        for gq in range(4):
            qr[:, gq * 128:(gq + 1) * 128] = _rope(q_ref[:, gq * 128:(gq + 1) * 128], cos_t, sin_t).astype(BF16)
        kl = _rope(k_ref[...], cos_t, sin_t)
        for kh in range(KV_HEADS):
            _attn_prepare(kh, kl, v_ref[...], kc_ref[...], vc_ref[...], ka, kb, va, vb, kca, kcb, vca, vcb)

            def lat_block(n, carry):
                r0, start, valid = _window(cfg, n)
                win = pl.ds(start, 3 * BLK)
                for pr in range(2):
                    lanes = slice((kh * 2 + pr) * 128, (kh * 2 + pr + 1) * 128)
                    qp = qr[pl.ds(r0, BLK), lanes]
                    o = None
                    for half, (kw, kcx, vw, vcx) in enumerate(((ka, kca, va, vca), (kb, kcb, vb, vcb))):
                        head = kh * 4 + pr * 2 + half
                        s_w = jnp.where(valid, _dot_nt(qp, kw[win, :]) * ATT_SCALE, NEG_INF)
                        s_c = _dot_nt(qp, kcx[...]) * ATT_SCALE
                        (p_w, p_c), _ = _softmax_parts([s_w, s_c], sink_ref[head:head + 1, 0:1])
                        part = _dot(p_w, vw[win, :]) + _dot(p_c, vcx[...])
                        o = part if o is None else o + part
                    o_ref[pl.ds(r0, BLK), lanes] = o.astype(BF16)
                return carry

            lax.fori_loop(0, n_lat // BLK, lat_block, 0)
            for n in range(n_ctx // BLK):
                rows = slice(n * BLK, (n + 1) * BLK)
                for pr in range(2):
                    lanes = slice((kh * 2 + pr) * 128, (kh * 2 + pr + 1) * 128)
                    qp = qc_ref[rows, lanes]
                    o = None
                    for half, (kcx, vcx) in enumerate(((kca, vca), (kcb, vcb))):
                        head = kh * 4 + pr * 2 + half
                        s_c = _dot_nt(qp, kcx[...]) * ATT_SCALE
                        (p_c,), _ = _softmax_parts([s_c], sink_ref[head:head + 1, 0:1])
                        part = _dot(p_c, vcx[...])
                        o = part if o is None else o + part
                    oc_ref[rows, lanes] = o.astype(BF16)

    return pl.pallas_call(
        body, grid=(2,), name=name, in_specs=_attn_specs(cfg),
        out_specs=[pl.BlockSpec((n_lat, ATT_W), lambda e: (e, 0)), pl.BlockSpec((n_ctx, ATT_W), lambda e: (e, 0))],
        out_shape=[jax.ShapeDtypeStruct((cfg.t_lat, ATT_W), BF16), jax.ShapeDtypeStruct((cfg.t_ctx, ATT_W), BF16)],
        scratch_shapes=[pltpu.VMEM((n_lat, ATT_W), BF16)] + [pltpu.VMEM((n_lat, 128), BF16)] * 4
        + [pltpu.VMEM((n_ctx, 128), BF16)] * 4,
        compiler_params=_params(("parallel",)))(p, p, p, p, p, p, cos, sin, sink_rows)


def _attn_bwd(cfg, p, dcat, cos, sin, sink_rows, name):
    n_lat, n_ctx, cb = cfg.n_lat, cfg.n_ctx, cfg.ctx_blk

    def body(q_ref, k_ref, v_ref, qc_ref, kc_ref, vc_ref, cos_ref, sin_ref, sink_ref, do_ref, doc_ref,
             dq_ref, dk_ref, dv_ref, dqc_ref, dkc_ref, dvc_ref, dsink_ref,
             qr, ka, kb, va, vb, kca, kcb, vca, vcb, dqs, dka, dva, dkca, dvca):
        cos_t, sin_t = cos_ref[...], sin_ref[...]
        lane = lax.broadcasted_iota(jnp.int32, (1, 128), 1)
        lo = lane < 64
        for gq in range(4):
            qr[:, gq * 128:(gq + 1) * 128] = _rope(q_ref[:, gq * 128:(gq + 1) * 128], cos_t, sin_t).astype(BF16)
        kl = _rope(k_ref[...], cos_t, sin_t)
        dsink_ref[...] = jnp.zeros_like(dsink_ref)
        dka[...] = jnp.zeros_like(dka)
        dva[...] = jnp.zeros_like(dva)
        dkca[...] = jnp.zeros_like(dkca)
        dvca[...] = jnp.zeros_like(dvca)

        def halves(x):
            return jnp.where(lo, x, 0).astype(BF16), jnp.where(lo, 0, x).astype(BF16)

        for kh in range(KV_HEADS):
            _attn_prepare(kh, kl, v_ref[...], kc_ref[...], vc_ref[...], ka, kb, va, vb, kca, kcb, vca, vcb)

            def one_head(head, qp, q_half, do_p, do_half, kw, kcx, vw, vcx, win, valid):
                sk = sink_ref[head:head + 1, 0:1]
                s_list = [_dot_nt(qp, kcx[...]) * ATT_SCALE]
                if win is not None:
                    s_list.insert(0, jnp.where(valid, _dot_nt(qp, kw[win, :]) * ATT_SCALE, NEG_INF))
                probs, p_sink = _softmax_parts(s_list, sk)
                vals = [vcx[...]] if win is None else [vw[win, :], vcx[...]]
                dps = [_dot_nt(do_p, vv) for vv in vals]
                dr = None
                for pp, dp in zip(probs, dps):
                    t = jnp.sum(pp * dp, axis=1, keepdims=True)
                    dr = t if dr is None else dr + t
                dss = [(pp * (dp - dr) * ATT_SCALE).astype(BF16) for pp, dp in zip(probs, dps)]
                dsink_ref[head:head + 1, :] += jnp.broadcast_to(
                    jnp.sum(-p_sink * dr, axis=0, keepdims=True), (1, 128))
                p_c, ds_c = probs[-1], dss[-1]
                dq = _dot(ds_c, kcx[...])
                dkca[kh] += _dot_tn(ds_c, q_half)
                dvca[kh] += _dot_tn(p_c, do_half)
                if win is not None:
                    dq = dq + _dot(dss[0], kw[win, :])
                    dka[kh, win, :] += _dot_tn(dss[0], q_half)
                    dva[kh, win, :] += _dot_tn(probs[0], do_half)
                return dq

            def lat_block(n, carry):
                r0, start, valid = _window(cfg, n)
                win = pl.ds(start, 3 * BLK)
                for pr in range(2):
                    lanes = slice((kh * 2 + pr) * 128, (kh * 2 + pr + 1) * 128)
                    qp = qr[pl.ds(r0, BLK), lanes]
                    do_p = do_ref[pl.ds(r0, BLK), lanes]
                    q_h, do_h = halves(qp), halves(do_p)
                    dq = None
                    for half, (kw, kcx, vw, vcx) in enumerate(((ka, kca, va, vca), (kb, kcb, vb, vcb))):
                        part = one_head(kh * 4 + pr * 2 + half, qp, q_h[half], do_p, do_h[half],
                                        kw, kcx, vw, vcx, win, valid)
                        dq = part if dq is None else dq + part
                    dqs[pl.ds(r0, BLK), lanes] = dq
                return carry

            lax.fori_loop(0, n_lat // BLK, lat_block, 0)
            for n in range(n_ctx // BLK):
                rows = slice(n * BLK, (n + 1) * BLK)
                for pr in range(2):
                    lanes = slice((kh * 2 + pr) * 128, (kh * 2 + pr + 1) * 128)
                    qp = qc_ref[rows, lanes].astype(BF16)
                    do_p = doc_ref[rows, lanes]
                    q_h, do_h = halves(qp), halves(do_p)
                    dq = None
                    for half, (kcx, vcx) in enumerate(((kca, vca), (kcb, vcb))):
                        part = one_head(kh * 4 + pr * 2 + half, qp, q_h[half], do_p, do_h[half],
                                        None, kcx, None, vcx, None, None)
                        dq = part if dq is None else dq + part
                    dqc_ref[rows, lanes] = dq.astype(BF16)

        def fold(acc):
            r0 = acc[0] + pltpu.roll(acc[0], 64, 1)
            r1 = acc[1] + pltpu.roll(acc[1], 64, 1)
            return jnp.where(lo, r0, r1)

        for gq in range(4):
            sl = slice(gq * 128, (gq + 1) * 128)
            dq_ref[:, sl] = _rope_t(dqs[:, sl], cos_t, sin_t).astype(BF16)
        dk_ref[...] = _rope_t(fold(dka), cos_t, sin_t).astype(BF16)
        dv_ref[...] = fold(dva).astype(BF16)
        dkc_ref[...] = fold(dkca).astype(BF16)
        dvc_ref[...] = fold(dvca).astype(BF16)

    lat = lambda w: pl.BlockSpec((n_lat, w), lambda e: (e, 0))
    ctx = lambda w: pl.BlockSpec((n_ctx, w), lambda e: (e, 0))
    sd = jax.ShapeDtypeStruct
    return pl.pallas_call(
        body, grid=(2,), name=name,
        in_specs=_attn_specs(cfg) + [pl.BlockSpec((n_lat, ATT_W), lambda e: (e, 0)),
                                     pl.BlockSpec((n_ctx, ATT_W), lambda e: (cb + e, 0))],
        out_specs=[lat(ATT_W), lat(128), lat(128), ctx(ATT_W), ctx(128), ctx(128),
                   pl.BlockSpec((None, 8, 128), lambda e: (e, 0, 0))],
        out_shape=[sd((cfg.t_lat, ATT_W), BF16), sd((cfg.t_lat, 128), BF16), sd((cfg.t_lat, 128), BF16),
                   sd((cfg.t_ctx, ATT_W), BF16), sd((cfg.t_ctx, 128), BF16), sd((cfg.t_ctx, 128), BF16),
                   sd((2, 8, 128), F32)],
        scratch_shapes=[pltpu.VMEM((n_lat, ATT_W), BF16)] + [pltpu.VMEM((n_lat, 128), BF16)] * 4
        + [pltpu.VMEM((n_ctx, 128), BF16)] * 4
        + [pltpu.VMEM((n_lat, ATT_W), F32), pltpu.VMEM((2, n_lat, 128), F32), pltpu.VMEM((2, n_lat, 128), F32),
           pltpu.VMEM((2, n_ctx, 128), F32), pltpu.VMEM((2, n_ctx, 128), F32)],
        compiler_params=_params(("parallel",)))(p, p, p, p, p, p, cos, sin, sink_rows, dcat, dcat)


def _shift_down(x, k, row):
    return jnp.where(row >= k, pltpu.roll(x, k, 0), 0.0)


def _shift_up(x, k, row):
    n = x.shape[0]
    return jnp.where(row < n - k, pltpu.roll(x, n - k, 0), 0.0)


def _window_sum(x, r, row):
    below, above, k = x, x, 1
    while k < r:
        below = below + _shift_down(below, k, row)
        above = above + _shift_up(above, k, row)
        k *= 2
    return below + _shift_down(x, r, row) + _shift_up(above, 1, row)


def _inv_count(r, row, n):
    cnt = jnp.minimum(row + r, n - 1) + 1 - jnp.maximum(row - r, 0)
    return 1.0 / cnt.astype(F32)


def _pool_fwd(p, w, scale, n, blk0, n_seg, name):
    def body(u0, u1, u2, u3, w_ref, sc_ref, o_ref):
        row = lax.broadcasted_iota(jnp.int32, (n, 1), 0)
        for g, u_ref in enumerate((u0, u1, u2, u3)):
            u = u_ref[...]
            d = _window_sum(u, POOL_R[g], row) * _inv_count(POOL_R[g], row, n) - u
            o_ref[:, g * 128:(g + 1) * 128] = (_dot(d, w_ref[g]) * sc_ref[:, g * 128:(g + 1) * 128]).astype(BF16)

    return pl.pallas_call(
        body, grid=(n_seg,), name=name,
        in_specs=[pl.BlockSpec((n, 128), functools.partial(lambda g, e: (blk0 + e, 6 + g), g)) for g in range(4)]
        + [pl.BlockSpec((4, 128, 128), lambda e: (0, 0, 0)), pl.BlockSpec((1, 512), lambda e: (0, 0))],
        out_specs=pl.BlockSpec((n, 512), lambda e: (e, 0)),
        out_shape=jax.ShapeDtypeStruct((n_seg * n, 512), BF16),
        compiler_params=_params(("parallel",)))(p, p, p, p, w, scale)


def _pool_bwd(p, w, scale, dcat, n, blk0, n_seg, name):
    def body(u0, u1, u2, u3, w_ref, sc_ref, dp_ref, du_ref, dw_ref, dsc_ref):
        e = pl.program_id(0)

        @pl.when(e == 0)
        def _():
            dw_ref[...] = jnp.zeros_like(dw_ref)
            dsc_ref[...] = jnp.zeros_like(dsc_ref)

        row = lax.broadcasted_iota(jnp.int32, (n, 1), 0)
        for g, u_ref in enumerate((u0, u1, u2, u3)):
            sl = slice(g * 128, (g + 1) * 128)
            u = u_ref[...]
            inv = _inv_count(POOL_R[g], row, n)
            d = _window_sum(u, POOL_R[g], row) * inv - u
            dp = dp_ref[:, sl]
            dsc_ref[:, sl] += jnp.sum(dp * _dot(d, w_ref[g]), axis=0, keepdims=True)
            dyp = dp * sc_ref[:, sl]
            dw_ref[g] += _dot_tn(d, dyp)
            dd = _dot_nt(dyp, w_ref[g])
            du_ref[:, sl] = (_window_sum(dd * inv, POOL_R[g], row) - dd).astype(BF16)

    return pl.pallas_call(
        body, grid=(n_seg,), name=name,
        in_specs=[pl.BlockSpec((n, 128), functools.partial(lambda g, e: (blk0 + e, 6 + g), g)) for g in range(4)]
        + [pl.BlockSpec((4, 128, 128), lambda e: (0, 0, 0)), pl.BlockSpec((1, 512), lambda e: (0, 0)),
           pl.BlockSpec((n, 512), lambda e: (blk0 + e, 1))],
        out_specs=[pl.BlockSpec((n, 512), lambda e: (e, 0)),
                   pl.BlockSpec((4, 128, 128), lambda e: (0, 0, 0)), pl.BlockSpec((1, 512), lambda e: (0, 0))],
        out_shape=[jax.ShapeDtypeStruct((n_seg * n, 512), BF16), jax.ShapeDtypeStruct((4, 128, 128), F32),
                   jax.ShapeDtypeStruct((1, 512), F32)],
        compiler_params=_params(("arbitrary",)))(p, p, p, p, w, scale, dcat)


def _gelu(x):
    t = jnp.tanh(math.sqrt(2.0 / math.pi) * (x + 0.044715 * x * x * x))
    return 0.5 * x * (1.0 + t), t


def _gelu_grad(x, t):
    return 0.5 * (1.0 + t) + 0.5 * x * (1.0 - t * t) * (math.sqrt(2.0 / math.pi) * (1.0 + 3 * 0.044715 * x * x))


def _neg_expm1(x):
    series = -x * (1.0 + x * (0.5 + x * (1.0 / 6.0 + x * (1.0 / 24.0 + x * (1.0 / 120.0)))))
    return jnp.where(x > -0.05, series, 1.0 - jnp.exp(x))


def _softplus_neg(lam):
    x = -lam
    e = jnp.exp(-jnp.abs(x))
    log1p = jnp.where(e < 1e-2, e * (1.0 - e * (0.5 - e * (1.0 / 3.0))), jnp.log(1.0 + e))
    return jnp.maximum(x, 0.0) + log1p, -_sigmoid(x)


def _conv(u, w_ref, b_ref, row):
    return (b_ref[...] + _shift_down(u, 1, row) * w_ref[0:1, :] + u * w_ref[1:2, :]
            + _shift_up(u, 1, row) * w_ref[2:3, :] + _shift_up(u, 2, row) * w_ref[3:4, :])


def _lru_gates(uc, d, wa_ref, ba_ref, wx_ref, bx_ref, lam_ref):
    r = _sigmoid(_dot(uc, wa_ref[d]) + ba_ref[d:d + 1, :])
    gi = _sigmoid(_dot(uc, wx_ref[d]) + bx_ref[d:d + 1, :])
    sp, dsp = _softplus_neg(lam_ref[d:d + 1, :])
    la = (-LRU_C) * r * sp
    a = jnp.exp(la)
    sq = jnp.sqrt(_neg_expm1(2.0 * la))
    return r, gi, sp, dsp, a, sq


def _tile_scan(a_ref, b_ref, n, reverse):
    m = n // 8
    first = 7 if reverse else 0
    a_prev = a_ref[pl.ds(first, m, stride=8), :]
    b_prev = b_ref[pl.ds(first, m, stride=8), :]
    for j in (range(6, -1, -1) if reverse else range(1, 8)):
        rows = pl.ds(j, m, stride=8)
        aj = a_ref[rows, :]
        b_prev = aj * b_prev + b_ref[rows, :]
        a_prev = aj * a_prev
        b_ref[rows, :] = b_prev
        a_ref[rows, :] = a_prev


def _carry_scan(a_ref, b_ref, n, reverse, carry):
    nt8 = n // 8

    def step(i, c):
        t = (nt8 - 1 - i) if reverse else i
        off = pl.multiple_of(t * 8, 8)
        h = a_ref[pl.ds(off, 8), :] * c + b_ref[pl.ds(off, 8), :]
        b_ref[pl.ds(off, 8), :] = h
        return h[0:1, :] if reverse else h[7:8, :]

    return lax.fori_loop(0, nt8, step, carry)


def _chain_scan(segs, reverse):
    carry = jnp.zeros((1, 128), F32)
    for a, b, a_ref, b_ref, n in segs:
        a_ref[...] = a
        b_ref[...] = b
        _tile_scan(a_ref, b_ref, n, reverse)
        carry = _carry_scan(a_ref, b_ref, n, reverse, carry)


def _lru_specs(cfg):
    n_lat, n_ctx, cb = cfg.n_lat, cfg.n_ctx, cfg.ctx_blk
    return [pl.BlockSpec((n_lat, 128), lambda hb, e: (e, hb)),
            pl.BlockSpec((n_lat, 128), lambda hb, e: (e, 8 + hb)),
            pl.BlockSpec((n_ctx, 128), lambda hb, e: (cb + e, hb)),
            pl.BlockSpec((n_ctx, 128), lambda hb, e: (cb + e, 8 + hb)),
            pl.BlockSpec((4, 128), lambda hb, e: (0, hb)),
            pl.BlockSpec((1, 128), lambda hb, e: (0, hb)),
            pl.BlockSpec((2, None, 128, 128), lambda hb, e: (0, hb, 0, 0)),
            pl.BlockSpec((2, 128), lambda hb, e: (0, hb)),
            pl.BlockSpec((2, None, 128, 128), lambda hb, e: (0, hb, 0, 0)),
            pl.BlockSpec((2, 128), lambda hb, e: (0, hb)),
            pl.BlockSpec((2, 128), lambda hb, e: (0, hb))]


def _lru_fwd(cfg, p, consts, name):
    n_lat, n_ctx = cfg.n_lat, cfg.n_ctx

    def body(gl_ref, ul_ref, gc_ref, uc_ref, cw_ref, cb_ref, wa_ref, ba_ref, wx_ref, bx_ref, lam_ref,
             zl_ref, zc_ref, hl_ref, hc_ref, al, ac):
        row_l = lax.broadcasted_iota(jnp.int32, (n_lat, 1), 0)
        row_c = lax.broadcasted_iota(jnp.int32, (n_ctx, 1), 0)
        uc_l = _conv(ul_ref[...], cw_ref, cb_ref, row_l)
        uc_c = _conv(uc_ref[...], cw_ref, cb_ref, row_c)
        for d in range(2):
            _, gi_l, _, _, a_l, sq_l = _lru_gates(uc_l, d, wa_ref, ba_ref, wx_ref, bx_ref, lam_ref)
            _, gi_c, _, _, a_c, sq_c = _lru_gates(uc_c, d, wa_ref, ba_ref, wx_ref, bx_ref, lam_ref)
            _chain_scan([(a_c, sq_c * (gi_c * uc_c), ac, hc_ref.at[d], n_ctx),
                         (a_l, sq_l * (gi_l * uc_l), al, hl_ref.at[d], n_lat)], reverse=(d == 1))
        zl_ref[...] = (_gelu(gl_ref[...])[0] * (hl_ref[0] + hl_ref[1])).astype(BF16)
        zc_ref[...] = (_gelu(gc_ref[...])[0] * (hc_ref[0] + hc_ref[1])).astype(BF16)

    return pl.pallas_call(
        body, grid=(8, 2), name=name, in_specs=_lru_specs(cfg),
        out_specs=[pl.BlockSpec((n_lat, 128), lambda hb, e: (e, hb)), pl.BlockSpec((n_ctx, 128), lambda hb, e: (e, hb)),
                   pl.BlockSpec((2, n_lat, 128), lambda hb, e: (0, e, hb)),
                   pl.BlockSpec((2, n_ctx, 128), lambda hb, e: (0, e, hb))],
        out_shape=[jax.ShapeDtypeStruct((cfg.t_lat, D), BF16), jax.ShapeDtypeStruct((cfg.t_ctx, D), BF16),
                   jax.ShapeDtypeStruct((2, cfg.t_lat, D), F32), jax.ShapeDtypeStruct((2, cfg.t_ctx, D), F32)],
        scratch_shapes=[pltpu.VMEM((n_lat, 128), F32), pltpu.VMEM((n_ctx, 128), F32)],
        compiler_params=_params(("parallel", "arbitrary")))(p, p, p, p, *consts)


def _lru_bwd(cfg, p, dz, h_lat, h_ctx, consts, name):
    n_lat, n_ctx, cb = cfg.n_lat, cfg.n_ctx, cfg.ctx_blk

    def body(gl_ref, ul_ref, gc_ref, uc_ref, cw_ref, cb_ref, wa_ref, ba_ref, wx_ref, bx_ref, lam_ref,
             dzl_ref, dzc_ref, hl, hc, dgl_ref, dul_ref, dgc_ref, duc_ref, dwa_ref, dwx_ref, vec_ref,
             al, bl, ac, bc):
        e = pl.program_id(1)

        @pl.when(e == 0)
        def _():
            dwa_ref[...] = jnp.zeros_like(dwa_ref)
            dwx_ref[...] = jnp.zeros_like(dwx_ref)
            vec_ref[...] = jnp.zeros_like(vec_ref)

        row_l = lax.broadcasted_iota(jnp.int32, (n_lat, 1), 0)
        row_c = lax.broadcasted_iota(jnp.int32, (n_ctx, 1), 0)
        u_l, u_c = ul_ref[...], uc_ref[...]
        uc_l = _conv(u_l, cw_ref, cb_ref, row_l)
        uc_c = _conv(u_c, cw_ref, cb_ref, row_c)
        gel_l, t_l = _gelu(gl_ref[...])
        gel_c, t_c = _gelu(gc_ref[...])
        dz_l, dz_c = dzl_ref[...], dzc_ref[...]
        dgl_ref[...] = (dz_l * (hl[0] + hl[1]) * _gelu_grad(gl_ref[...], t_l)).astype(BF16)
        dgc_ref[...] = (dz_c * (hc[0] + hc[1]) * _gelu_grad(gc_ref[...], t_c)).astype(BF16)
        dy_l, dy_c = dz_l * gel_l, dz_c * gel_c
        duc_l = jnp.zeros((n_lat, 128), F32)
        duc_c = jnp.zeros((n_ctx, 128), F32)
        for d in range(2):
            r_l, gi_l, sp, dsp, a_l, sq_l = _lru_gates(uc_l, d, wa_ref, ba_ref, wx_ref, bx_ref, lam_ref)
            r_c, gi_c, _, _, a_c, sq_c = _lru_gates(uc_c, d, wa_ref, ba_ref, wx_ref, bx_ref, lam_ref)
            if d == 0:
                an_l = _shift_up(a_l, 1, row_l)
                an_c = jnp.where(row_c < n_ctx - 1, pltpu.roll(a_c, n_ctx - 1, 0), a_l[0:1, :])
            else:
                an_l = _shift_down(a_l, 1, row_l)
                an_c = jnp.where(row_c >= 1, pltpu.roll(a_c, 1, 0), a_l[n_lat - 1:n_lat, :])
            _chain_scan([(an_l, dy_l, al, bl, n_lat), (an_c, dy_c, ac, bc, n_ctx)], reverse=(d == 0))
            dsp_sum = jnp.zeros((1, 128), F32)
            for (dh, h, r, gi, a, sq, uc, seg) in ((bl[...], hl[d], r_l, gi_l, a_l, sq_l, uc_l, "l"),
                                                  (bc[...], hc[d], r_c, gi_c, a_c, sq_c, uc_c, "c")):
                b0 = sq * (gi * uc)
                t1 = dh * sq
                dla = dh * (h - b0) - (dh * gi * uc) * (a * a) / sq
                dzr = (dla * ((-LRU_C) * sp)) * r * (1.0 - r)
                dzi = (t1 * uc) * gi * (1.0 - gi)
                dsp_sum = dsp_sum + jnp.sum(dla * ((-LRU_C) * r), axis=0, keepdims=True)
                dwa_ref[d] += _dot_tn(uc, dzr)
                dwx_ref[d] += _dot_tn(uc, dzi)
                vec_ref[d:d + 1, :] += jnp.sum(dzr, axis=0, keepdims=True)
                vec_ref[2 + d:3 + d, :] += jnp.sum(dzi, axis=0, keepdims=True)
                duc = t1 * gi + _dot_nt(dzr, wa_ref[d]) + _dot_nt(dzi, wx_ref[d])
                if seg == "l":
                    duc_l = duc_l + duc
                else:
                    duc_c = duc_c + duc
            vec_ref[4 + d:5 + d, :] += dsp_sum * dsp
        for duc, u, row, du_ref in ((duc_l, u_l, row_l, dul_ref), (duc_c, u_c, row_c, duc_ref)):
            du_ref[...] = (_shift_up(duc, 1, row) * cw_ref[0:1, :] + duc * cw_ref[1:2, :]
                           + _shift_down(duc, 1, row) * cw_ref[2:3, :]
                           + _shift_down(duc, 2, row) * cw_ref[3:4, :]).astype(BF16)
            vec_ref[6:7, :] += jnp.sum(duc * _shift_down(u, 1, row), axis=0, keepdims=True)
            vec_ref[7:8, :] += jnp.sum(duc * u, axis=0, keepdims=True)
            vec_ref[8:9, :] += jnp.sum(duc * _shift_up(u, 1, row), axis=0, keepdims=True)
            vec_ref[9:10, :] += jnp.sum(duc * _shift_up(u, 2, row), axis=0, keepdims=True)
            vec_ref[10:11, :] += jnp.sum(duc, axis=0, keepdims=True)

    lat = pl.BlockSpec((n_lat, 128), lambda hb, e: (e, hb))
    ctx = pl.BlockSpec((n_ctx, 128), lambda hb, e: (e, hb))
    wspec = pl.BlockSpec((2, None, 128, 128), lambda hb, e: (0, hb, 0, 0))
    sd = jax.ShapeDtypeStruct
    return pl.pallas_call(
        body, grid=(8, 2), name=name,
        in_specs=_lru_specs(cfg) + [pl.BlockSpec((n_lat, 128), lambda hb, e: (e, hb)),
                                    pl.BlockSpec((n_ctx, 128), lambda hb, e: (cb + e, hb)),
                                    pl.BlockSpec((2, n_lat, 128), lambda hb, e: (0, e, hb)),
                                    pl.BlockSpec((2, n_ctx, 128), lambda hb, e: (0, e, hb))],
        out_specs=[lat, lat, ctx, ctx, wspec, wspec, pl.BlockSpec((None, 16, 128), lambda hb, e: (hb, 0, 0))],
        out_shape=[sd((cfg.t_lat, D), BF16), sd((cfg.t_lat, D), BF16), sd((cfg.t_ctx, D), BF16), sd((cfg.t_ctx, D), BF16),
                   sd((2, 8, 128, 128), F32), sd((2, 8, 128, 128), F32), sd((8, 16, 128), F32)],
        scratch_shapes=[pltpu.VMEM((n_lat, 128), F32)] * 2 + [pltpu.VMEM((n_ctx, 128), F32)] * 2,
        compiler_params=_params(("parallel", "arbitrary")))(p, p, p, p, *consts, dz, dz, h_lat, h_ctx)


def _position():
    x, y, c = lax.axis_index("x"), lax.axis_index("y"), lax.axis_index("c")
    return x, y, c, 4 * x + 2 * y + c


def _peer(x, y, c, k):
    px = 1 - x if k & 4 else x
    py = 1 - y if k & 2 else y
    pc = 1 - c if k & 1 else c
    return (px, py, pc), 4 * px + 2 * py + pc


def _all_gather(v, name, in_vmem):
    def body(v_ref, o_ref, send_sems, recv_sems, local_sem):
        x, y, c, me = _position()
        mine = pltpu.make_async_copy(v_ref, o_ref.at[me], local_sem)
        mine.start()
        sends = []
        for k in range(1, N_DEV):
            peer, _ = _peer(x, y, c, k)
            cp = pltpu.make_async_remote_copy(src_ref=v_ref, dst_ref=o_ref.at[me], send_sem=send_sems.at[k - 1],
                                              recv_sem=recv_sems.at[k - 1], device_id=peer, device_id_type=MESH)
            cp.start()
            sends.append(cp)
        for k in range(1, N_DEV):
            peer, peer_lin = _peer(x, y, c, k)
            pltpu.make_async_remote_copy(src_ref=v_ref, dst_ref=o_ref.at[peer_lin], send_sem=send_sems.at[k - 1],
                                         recv_sem=recv_sems.at[k - 1], device_id=peer, device_id_type=MESH).wait_recv()
        for cp in sends:
            cp.wait_send()
        mine.wait()

    space = pltpu.VMEM if in_vmem else pl.ANY
    return pl.pallas_call(
        body, name=name,
        in_specs=[pl.BlockSpec(memory_space=space)], out_specs=pl.BlockSpec(memory_space=space),
        out_shape=jax.ShapeDtypeStruct((N_DEV,) + v.shape, v.dtype),
        scratch_shapes=[pltpu.SemaphoreType.DMA((N_DEV - 1,)), pltpu.SemaphoreType.DMA((N_DEV - 1,)),
                        pltpu.SemaphoreType.DMA],
        compiler_params=pltpu.CompilerParams(vmem_limit_bytes=VMEM_LIMIT))(v)


_HBM = pl.BlockSpec(memory_space=pltpu.HBM)
_SEM = pl.BlockSpec(memory_space=pltpu.SEMAPHORE)
_EFFECT = pltpu.SideEffectType.DATAFLOW_SIDE_EFFECTING


def _push_start(src, land, block_of, name):
    def body(src_ref, land_ref, send_sem, recv_sem, src_thru, land_thru, token):
        x, y, c, me = _position()
        for k in range(1, N_DEV):
            peer, peer_lin = _peer(x, y, c, k)
            mine, there = block_of(src_ref, land_ref, me, peer_lin)
            pltpu.make_async_remote_copy(src_ref=mine, dst_ref=there, send_sem=send_sem, recv_sem=recv_sem,
                                         device_id=peer, device_id_type=MESH).start()
        token[...] = jnp.zeros_like(token)

    return pl.pallas_call(
        body, name=name,
        out_shape=(pltpu.SemaphoreType.DMA(()), pltpu.SemaphoreType.DMA(()), pltpu.HBM(src.shape, src.dtype),
                   pltpu.HBM(land.shape, land.dtype), jax.ShapeDtypeStruct((8, 128), F32)),
        in_specs=(_HBM, _HBM), out_specs=(_SEM, _SEM, _HBM, _HBM, pl.BlockSpec(memory_space=pltpu.VMEM)),
        input_output_aliases={0: 2, 1: 3},
        compiler_params=pltpu.CompilerParams(has_side_effects=_EFFECT),
    )(pltpu.with_memory_space_constraint(src, pltpu.HBM), pltpu.with_memory_space_constraint(land, pltpu.HBM))


def _push_wait(handle, seven_of, after, name):
    send_sem, recv_sem, src_thru, land_thru, _ = handle

    def body(src_ref, land_ref, send_sem, recv_sem, after_ref, src_dead, got_ref):
        x, y, c, _ = _position()
        seven = seven_of(land_ref)
        cp = pltpu.make_async_remote_copy(src_ref=seven, dst_ref=seven, send_sem=send_sem, recv_sem=recv_sem,
                                          device_id=(x, y, 1 - c), device_id_type=MESH)
        cp.wait_send()
        cp.wait_recv()

    return pl.pallas_call(
        body, name=name,
        out_shape=(pltpu.HBM(src_thru.shape, src_thru.dtype), pltpu.HBM(land_thru.shape, land_thru.dtype)),
        in_specs=(_HBM, _HBM, _SEM, _SEM, pl.BlockSpec(memory_space=pl.ANY)), out_specs=(_HBM, _HBM),
        input_output_aliases={0: 0, 1: 1},
        compiler_params=pltpu.CompilerParams(has_side_effects=_EFFECT),
    )(src_thru, land_thru, send_sem, recv_sem, after)[1]


def _gather_start(src, me, name):
    g, r, C = src.shape
    land = lax.dynamic_update_slice(lax.empty((g, N_DEV * r, C), src.dtype), src, (0, me * r, 0))
    return _push_start(src, land, lambda s, z, i, p: (s, z.at[:, pl.ds(i * r, r), :]), name)


def _gather_wait(handle, after, name):
    r = handle[2].shape[1]
    return _push_wait(handle, lambda z: z.at[:, pl.ds(0, (N_DEV - 1) * r), :], after, name)


def _exchange_start(grad, me, name):
    g, rows, C = grad.shape
    r = rows // N_DEV
    mine = lax.dynamic_slice_in_dim(grad, me * r, r, axis=1)[None]
    land = lax.dynamic_update_slice(lax.empty((N_DEV, g, r, C), grad.dtype), mine, (me, 0, 0, 0))
    return _push_start(grad, land, lambda s, z, i, p: (s.at[:, pl.ds(p * r, r), :], z.at[i]), name)


def _exchange_wait(handle, after, name):
    return _push_wait(handle, lambda z: z.at[pl.ds(0, N_DEV - 1)], after, name)


def _sum_blocks(v, name):
    k, rows, cols = v.shape
    tr = rows
    for cand in (rows, 512, 352, 256, 176, 128, 64, 32, 16):
        if rows % cand == 0 and k * cand * cols * v.dtype.itemsize <= 6 * 1024 * 1024:
            tr = cand
            break

    def body(v_ref, o_ref):
        acc = v_ref[0].astype(F32)
        for s in range(1, k):
            acc = acc + v_ref[s].astype(F32)
        o_ref[...] = acc

    return pl.pallas_call(
        body, grid=(rows // tr,), name=name,
        in_specs=[pl.BlockSpec((k, tr, cols), lambda i: (0, i, 0))],
        out_specs=pl.BlockSpec((tr, cols), lambda i: (i, 0)),
        out_shape=jax.ShapeDtypeStruct((rows, cols), F32),
        compiler_params=_params(("parallel",)))(v)


def _adam_math(w, g, m, v):
    m2 = B1 * m + (1.0 - B1) * g
    v2 = B2 * v + (1.0 - B2) * (g * g)
    m_hat = m2 / (1.0 - B1 ** STEP)
    v_hat = v2 / (1.0 - B2 ** STEP)
    return -LR * (m_hat / (jnp.sqrt(v_hat) + EPS) + WD * w), m2, v2


def _adamw(w, g, m, v, name):
    shp = w.shape
    rows, cols = (shp[-2], shp[-1]) if len(shp) >= 2 else (1, shp[-1])
    lead = math.prod(shp[:-2]) if len(shp) > 2 else 1
    fits = [t for t in range(8, rows + 1, 8) if rows % t == 0 and t * cols * 4 <= 2 * 1024 * 1024]
    tr = max(fits) if fits else rows

    def body(w_ref, g_ref, m_ref, v_ref, d_ref, m2_ref, v2_ref):
        d_ref[...], m2_ref[...], v2_ref[...] = _adam_math(w_ref[...], g_ref[...], m_ref[...], v_ref[...])

    blk = pl.BlockSpec((None, tr, cols), lambda b, i: (b, i, 0))
    outs = pl.pallas_call(
        body, grid=(lead, rows // tr), name=name, in_specs=[blk] * 4, out_specs=[blk] * 3,
        out_shape=[jax.ShapeDtypeStruct((lead, rows, cols), F32)] * 3,
        compiler_params=_params(("parallel", "parallel")))(*[a.reshape(lead, rows, cols) for a in (w, g, m, v)])
    return [o.reshape(shp) for o in outs]


def _as2d(a):
    n = a.size
    if n % 1024 == 0:
        return a.reshape(n // 1024, 1024)
    if n % 128 == 0:
        return a.reshape(n // 128, 128)
    return a.reshape(1, n)


def _blocks_to_cols(a):
    b = jnp.moveaxis(a, 0, -2)
    return b.reshape(b.shape[:-2] + (b.shape[-2] * b.shape[-1],))


def _pack_rows(parts):
    padded, offs, r = [], [], 0
    for p in parts:
        pad = (-p.shape[0]) % 8
        padded.append(jnp.pad(p, ((0, pad), (0, 0))) if pad else p)
        offs.append(r)
        r += p.shape[0] + pad
    return jnp.concatenate(padded, axis=0), offs


def _silu(x):
    return x * jax.nn.sigmoid(x)


def kernel(x, c, ctx, c_ctx, w_mod, b_mod, ln_g, ln_b, ffn_w_gate, ffn_w_up, ffn_w_down, mix_ab_w_in, attn_sink, pool_w, pool_scale, mix_ab_w_out, lru_w_in, lru_conv_w, lru_conv_b, lru_wa, lru_ba, lru_wx, lru_bx, lru_lambda, lru_w_out, loss_target, m_c_ctx, m_w_mod, m_b_mod, m_ln_g, m_ln_b, m_ffn_w_gate, m_ffn_w_up, m_ffn_w_down, m_mix_ab_w_in, m_attn_sink, m_pool_w, m_pool_scale, m_mix_ab_w_out, m_lru_w_in, m_lru_conv_w, m_lru_conv_b, m_lru_wa, m_lru_ba, m_lru_wx, m_lru_bx, m_lru_lambda, m_lru_w_out, v_c_ctx, v_w_mod, v_b_mod, v_ln_g, v_ln_b, v_ffn_w_gate, v_ffn_w_up, v_ffn_w_down, v_mix_ab_w_in, v_attn_sink, v_pool_w, v_pool_scale, v_mix_ab_w_out, v_lru_w_in, v_lru_conv_w, v_lru_conv_b, v_lru_wa, v_lru_ba, v_lru_wx, v_lru_bx, v_lru_lambda, v_lru_w_out):
    weights = dict(c_ctx=c_ctx, w_mod=w_mod, b_mod=b_mod, ln_g=ln_g, ln_b=ln_b, ffn_w_gate=ffn_w_gate,
                   ffn_w_up=ffn_w_up, ffn_w_down=ffn_w_down, mix_ab_w_in=mix_ab_w_in, attn_sink=attn_sink,
                   pool_w=pool_w, pool_scale=pool_scale, mix_ab_w_out=mix_ab_w_out, lru_w_in=lru_w_in,
                   lru_conv_w=lru_conv_w, lru_conv_b=lru_conv_b, lru_wa=lru_wa, lru_ba=lru_ba, lru_wx=lru_wx,
                   lru_bx=lru_bx, lru_lambda=lru_lambda, lru_w_out=lru_w_out)
    mom_m = dict(c_ctx=m_c_ctx, w_mod=m_w_mod, b_mod=m_b_mod, ln_g=m_ln_g, ln_b=m_ln_b, ffn_w_gate=m_ffn_w_gate,
                 ffn_w_up=m_ffn_w_up, ffn_w_down=m_ffn_w_down, mix_ab_w_in=m_mix_ab_w_in, attn_sink=m_attn_sink,
                 pool_w=m_pool_w, pool_scale=m_pool_scale, mix_ab_w_out=m_mix_ab_w_out, lru_w_in=m_lru_w_in,
                 lru_conv_w=m_lru_conv_w, lru_conv_b=m_lru_conv_b, lru_wa=m_lru_wa, lru_ba=m_lru_ba, lru_wx=m_lru_wx,
                 lru_bx=m_lru_bx, lru_lambda=m_lru_lambda, lru_w_out=m_lru_w_out)
    mom_v = dict(c_ctx=v_c_ctx, w_mod=v_w_mod, b_mod=v_b_mod, ln_g=v_ln_g, ln_b=v_ln_b, ffn_w_gate=v_ffn_w_gate,
                 ffn_w_up=v_ffn_w_up, ffn_w_down=v_ffn_w_down, mix_ab_w_in=v_mix_ab_w_in, attn_sink=v_attn_sink,
                 pool_w=v_pool_w, pool_scale=v_pool_scale, mix_ab_w_out=v_mix_ab_w_out, lru_w_in=v_lru_w_in,
                 lru_conv_w=v_lru_conv_w, lru_conv_b=v_lru_conv_b, lru_wa=v_lru_wa, lru_ba=v_lru_ba, lru_wx=v_lru_wx,
                 lru_bx=v_lru_bx, lru_lambda=v_lru_lambda, lru_w_out=v_lru_w_out)
    names = list(weights)

    n_lat, n_ctx = x.shape[1], ctx.shape[1]
    cfg = _Cfg(n_lat, n_ctx)
    _, _, _, me = _position()
    mcols = w_mod.shape[2]

    def t_bf16(w):
        return jnp.swapaxes(w, -1, -2).astype(BF16)

    def ffn_src(l, i):
        return jnp.stack([t_bf16(ffn_w_gate[l, i]), t_bf16(ffn_w_up[l, i]), ffn_w_down[l, i].astype(BF16)])

    pending = {}

    def start_gathers(items, tok):
        for key, make_src in items:
            pending[key] = _gather_start(make_src() + tok.astype(BF16), me, "gather_start_" + key)
            tok = pending[key][4][0, 0]
        return tok

    def weights_now(key, after):
        return _gather_wait(pending[key], after, "gather_wait_" + key)

    tok = start_gathers([("ffn00", lambda: ffn_src(0, 0))], jnp.zeros((), F32))

    small_names = ["ln_g", "ln_b", "lru_conv_w", "lru_conv_b", "lru_ba", "lru_bx", "lru_lambda"]
    small, small_off = _pack_rows([(c + tok).reshape(-1, 128)] + [weights[n].reshape(-1, 128) for n in small_names])
    small_all = _all_gather(small, "gather_small", True)

    def small_full(idx, shp):
        rows = math.prod(shp) // 128
        return _blocks_to_cols(small_all[:, small_off[idx]:small_off[idx] + rows, :].reshape((N_DEV,) + shp))

    c_all = small_all[:, :2 * D // 128, :].reshape(2 * N_DEV, D)
    ln_g_f, ln_b_f = small_full(1, ln_g.shape), small_full(2, ln_b.shape)
    lru_consts = (small_full(3, lru_conv_w.shape)[0], small_full(4, lru_conv_b.shape), lru_wa[0],
                  small_full(5, lru_ba.shape)[0], lru_wx[0], small_full(6, lru_bx.shape)[0],
                  small_full(7, lru_lambda.shape)[0])

    s_rows = jnp.zeros((32, D), F32).at[:16].set(_silu(c_all)).at[16].set(_silu(c_ctx)).astype(BF16)
    mod_mine = jnp.stack([_matmul(s_rows, w_mod[l], "nn", F32, "mod_fwd", bn_cap=1280) for l in range(2)])
    mod_all = _all_gather(mod_mine.reshape(64, mcols), "gather_mod", True).reshape(N_DEV, 2, 32, mcols)
    tok = start_gathers([("ab_in", lambda: t_bf16(mix_ab_w_in)), ("ab_out", lambda: mix_ab_w_out.astype(BF16)),
                         ("ffn01", lambda: ffn_src(0, 1)), ("ffn10", lambda: ffn_src(1, 0)),
                         ("lru_in", lambda: t_bf16(lru_w_in)), ("lru_out", lambda: lru_w_out.astype(BF16)),
                         ("ffn11", lambda: ffn_src(1, 1))], mod_all[0, 0, 0, 0] * 0.0)
    mod_full = _blocks_to_cols(mod_all) + (b_mod[:, None, :] + tok)
    ex0 = 2 * me
    mods = []
    for l in range(2):
        rows = jnp.stack([lax.dynamic_index_in_dim(mod_full[l], ex0, 0, False),
                          lax.dynamic_index_in_dim(mod_full[l], ex0 + 1, 0, False), mod_full[l, 16]])
        mods.append(rows.reshape(3, N_MOD, D))

    h0 = jnp.concatenate([x.reshape(cfg.t_lat, D), ctx.reshape(cfg.t_ctx, D)], axis=0)
    cos, sin = _rope_tables(n_lat)
    sink_rows = jnp.broadcast_to(attn_sink[0][:, None], (8, 128)).astype(F32)

    saved = []
    wf = [[None, None], [None, None]]
    h = h0
    xin = _modulate(cfg, h0, mods[0], 0, 1, "modulate_in")
    for l in range(2):
        st = {"h_in": h, "xin1": xin}
        wf[l][0] = weights_now("ffn%d0" % l, xin)
        g1, u1, y1 = _ffn_fwd(xin, wf[l][0], "ffn_fwd")
        h1, xhat1, rstd1, xin2 = _ln_fwd(cfg, h, y1, mods[l], 2, 0.5, ln_g_f[l, 0][None], ln_b_f[l, 0][None],
                                          mods[l], (3, 4), "ln_fwd_a")
        st.update(g1=g1, u1=u1, y1=y1, h1=h1, xhat1=xhat1, rstd1=rstd1, xin2=xin2)
        if l == 0:
            w_ab_in_t = weights_now("ab_in", xin2)[0]
            p = _matmul(xin2, w_ab_in_t, "nt", F32, "mix_ab_in")
            att_l, att_c = _attn_fwd(cfg, p, cos, sin, sink_rows, "attn_fwd")
            pool_l = _pool_fwd(p, pool_w[0], pool_scale, n_lat, 0, 2, "pool_fwd_lat")
            pool_c = _pool_fwd(p, pool_w[0], pool_scale, n_ctx, cfg.ctx_blk, 2, "pool_fwd_ctx")
            cat = jnp.concatenate([jnp.concatenate([att_l, pool_l], axis=1),
                                   jnp.concatenate([att_c, pool_c], axis=1)], axis=0)
            w_ab_out = weights_now("ab_out", cat)[0]
            y2 = _matmul(cat, w_ab_out, "nn", F32, "mix_ab_out")
        else:
            w_lru_in_t = weights_now("lru_in", xin2)[0]
            p = _matmul(xin2, w_lru_in_t, "nt", F32, "lru_in")
            z_l, z_c, st["h_lat"], st["h_ctx"] = _lru_fwd(cfg, p, lru_consts, "lru_fwd")
            cat = jnp.concatenate([z_l, z_c], axis=0)
            w_lru_out = weights_now("lru_out", cat)[0]
            y2 = _matmul(cat, w_lru_out, "nn", F32, "lru_out")
        h2, xhat2, rstd2, xin3 = _ln_fwd(cfg, h1, y2, mods[l], 5, 1.0, ln_g_f[l, 1][None], ln_b_f[l, 1][None],
                                          mods[l], (6, 7), "ln_fwd_b")
        wf[l][1] = weights_now("ffn%d1" % l, xin3)
        g3, u3, y3 = _ffn_fwd(xin3, wf[l][1], "ffn_fwd")
        if l == 0:
            h3, xhat3, rstd3, xin = _ln_fwd(cfg, h2, y3, mods[l], 8, 0.5, ln_g_f[l, 2][None], ln_b_f[l, 2][None],
                                            mods[1], (0, 1), "ln_fwd_a")
        else:
            h3, xhat3, rstd3 = _ln_fwd(cfg, h2, y3, mods[l], 8, 0.5, ln_g_f[l, 2][None], ln_b_f[l, 2][None],
                                       None, None, "ln_fwd_last")
        st.update(p=p, cat=cat, y2=y2, h2=h2, xhat2=xhat2, rstd2=rstd2, xin3=xin3, g3=g3, u3=u3, y3=y3,
                  xhat3=xhat3, rstd3=rstd3)
        saved.append(st)
        h = h3

    dy, loss_tile = _loss(cfg, h, loss_target.reshape(cfg.t_lat, D), "loss")
    loss = lax.psum(loss_tile[0, 0], ("x", "y", "c"))

    grads = {}
    dmod = [None, None]
    recv_ffn = [[None, None], [None, None]]
    dln_g = [[None] * 3, [None] * 3]
    dln_b = [[None] * 3, [None] * 3]

    def ffn_weight_grads(tag, xin_b, dg, du, a_act, dys):
        parts = [_matmul(dg, xin_b, "tn", BF16, "ffn_dw", bm_cap=1408, bk_cap=2304)[None],
                 _matmul(du, xin_b, "tn", BF16, "ffn_dw", bm_cap=1408, bk_cap=2304)[None],
                 _matmul(a_act, dys, "tn", BF16, "ffn_dw", bm_cap=1408, bk_cap=2304)[None]]
        return [_exchange_start(part, me, "exchange_start_ffn%s_%d" % (tag, k)) for k, part in enumerate(parts)]

    def pin(handles):
        total = handles[0][4][0, 0]
        for hd in handles[1:]:
            total = total + hd[4][0, 0]
        return total

    up = (dy,)
    dmod_next = None
    last_sent = None
    for l in (1, 0):
        st = saved[l]
        dm = [None] * N_MOD

        def put_stats(stats, gate_idx, nxt):
            dm[gate_idx] = stats[:, 2, :]
            if nxt is not None:
                nxt[0][nxt[1]] = stats[:, 4, :]
                nxt[0][nxt[1] + 1] = stats[:, 3, :]

        lng3 = ln_g_f[l, 2][None] if last_sent is None else ln_g_f[l, 2][None] + pin(last_sent)
        dres, dys, stats = _ln_bwd(cfg, up, st["xhat3"], st["rstd3"], st["y3"], mods[l], 8, 0.5,
                                   lng3, "ln_bwd_fused" if len(up) > 1 else "ln_bwd_last")
        put_stats(stats, 8, None if len(up) == 1 else (dmod_next, 0))
        dln_g[l][2], dln_b[l][2] = stats[:, 0, :].sum(0), stats[:, 1, :].sum(0)
        dg, du, a_act, dxin = _ffn_bwd(dys, st["g3"], st["u3"], wf[l][1], "ffn_bwd")
        recv_ffn[l][1] = ffn_weight_grads("%d1" % l, st["xin3"], dg, du, a_act, dys)
        dres, dys, stats = _ln_bwd(cfg, (dres, dxin, st["h2"], mods[l], 7), st["xhat2"], st["rstd2"], st["y2"],
                                   mods[l], 5, 1.0, ln_g_f[l, 1][None] + pin(recv_ffn[l][1]), "ln_bwd_fused")
        put_stats(stats, 5, (dm, 6))
        dln_g[l][1], dln_b[l][1] = stats[:, 0, :].sum(0), stats[:, 1, :].sum(0)
        if l == 0:
            dw_out = _matmul(st["cat"], dys, "tn", BF16, "mix_ab_dw_out")
            dcat = _matmul(dys, w_ab_out, "nt", F32, "mix_ab_dcat")
            dq, dk, dv, dqc, dkc, dvc, dsink = _attn_bwd(cfg, st["p"], dcat, cos, sin, sink_rows, "attn_bwd")
            du_l, dpw_l, dps_l = _pool_bwd(st["p"], pool_w[0], pool_scale, dcat, n_lat, 0, 2, "pool_bwd_lat")
            du_c, dpw_c, dps_c = _pool_bwd(st["p"], pool_w[0], pool_scale, dcat, n_ctx, cfg.ctx_blk, 2, "pool_bwd_ctx")
            dp = jnp.concatenate([jnp.concatenate([dq, dk, dv, du_l], axis=1),
                                  jnp.concatenate([dqc, dkc, dvc, du_c], axis=1)], axis=0)
            dw_in_t = _matmul(dp, st["xin2"], "tn", BF16, "mix_ab_dw_in", bm_cap=1280)
            dxin = _matmul(dp, w_ab_in_t, "nn", F32, "mix_ab_dx")
            recv_mix = [_exchange_start(part, me, "exchange_start_mix_ab_%d" % k)
                        for k, part in enumerate((dw_in_t[None], dw_out[None], _as2d(dpw_l + dpw_c)[None]))]
            grads["attn_sink"] = (dsink[0, :, 0] + dsink[1, :, 0])[None, :]
            grads["pool_scale"] = dps_l + dps_c
        else:
            dw_out = _matmul(st["cat"], dys, "tn", BF16, "lru_dw_out")
            dz = _matmul(dys, w_lru_out, "nt", F32, "lru_dz")
            dgl, dul, dgc, duc, dwa, dwx, vec = _lru_bwd(cfg, st["p"], dz, st["h_lat"], st["h_ctx"], lru_consts, "lru_bwd")
            dp = jnp.concatenate([jnp.concatenate([dgl, dul], axis=1), jnp.concatenate([dgc, duc], axis=1)], axis=0)
            dw_in_t = _matmul(dp, st["xin2"], "tn", BF16, "lru_dw_in", bm_cap=1024)
            dxin = _matmul(dp, w_lru_in_t, "nn", F32, "lru_dx")
            recv_mix = [_exchange_start(part, me, "exchange_start_lru_%d" % k)
                        for k, part in enumerate((dw_in_t[None], dw_out[None], _as2d(dwa)[None], _as2d(dwx)[None]))]
            vec_t = jnp.moveaxis(vec, 0, 1).reshape(16, D)
            grads["lru_ba"], grads["lru_bx"] = vec_t[0:2], vec_t[2:4]
            grads["lru_lambda"], grads["lru_conv_w"], grads["lru_conv_b"] = vec_t[4:6], vec_t[6:10], vec_t[10:11]
        if l == 0:
            recv_ab = recv_mix
        else:
            recv_lru = recv_mix
        dres, dys, stats = _ln_bwd(cfg, (dres, dxin, st["h1"], mods[l], 4), st["xhat1"], st["rstd1"], st["y1"],
                                   mods[l], 2, 0.5, ln_g_f[l, 0][None] + pin(recv_mix), "ln_bwd_fused")
        put_stats(stats, 2, (dm, 3))
        dln_g[l][0], dln_b[l][0] = stats[:, 0, :].sum(0), stats[:, 1, :].sum(0)
        dg, du, a_act, dxin = _ffn_bwd(dys, st["g1"], st["u1"], wf[l][0], "ffn_bwd")
        recv_ffn[l][0] = ffn_weight_grads("%d0" % l, st["xin1"], dg, du, a_act, dys)
        last_sent = recv_ffn[l][0]
        dmod[l] = dm
        dmod_next = dm
        up = (dres, dxin, st["h_in"], mods[l], 1)
    dh0, stats = _modulate_bwd(cfg, up[0], up[1], h0, mods[0] + pin(last_sent), 1, "modulate_bwd")
    dmod[0][0], dmod[0][1] = stats[:, 4, :], stats[:, 3, :]
    grad_x = dh0[:cfg.t_lat].reshape(x.shape)

    def arrived(handle, name):
        return _exchange_wait(handle, dh0, name)

    recv_ffn = [[[arrived(hd, "exchange_wait_ffn%d%d_%d" % (l, i, k)) for k, hd in enumerate(recv_ffn[l][i])]
                 for i in range(2)] for l in range(2)]
    recv_ab = [arrived(hd, "exchange_wait_mix_ab_%d" % k) for k, hd in enumerate(recv_ab)]
    recv_lru = [arrived(hd, "exchange_wait_lru_%d" % k) for k, hd in enumerate(recv_lru)]

    dmod_mine = jnp.stack([jnp.stack(dmod[l], axis=1).reshape(3, N_MOD * D) for l in range(2)])
    n_dm = 6 * N_MOD * D // 128
    dmod_all = _all_gather(dmod_mine.reshape(n_dm, 128), "gather_dmod", True)
    dmod_sum = _sum_blocks(dmod_all, "sum_dmod").reshape(2, 3, N_MOD * D)
    dmod_all = dmod_all.reshape(N_DEV, 2, 3, N_MOD * D)
    grads["b_mod"] = dmod_sum[:, 0] + dmod_sum[:, 1] + dmod_sum[:, 2]
    dmod_ex = jnp.moveaxis(dmod_all[:, :, 0:2, :], 1, 0).reshape(2, 2 * N_DEV, N_MOD * D)
    dm_rows = jnp.zeros((2, 32, N_MOD * D), F32).at[:, :16].set(dmod_ex).at[:, 16].set(dmod_sum[:, 2])
    dm_cols = lax.dynamic_slice_in_dim(dm_rows, me * mcols, mcols, axis=2).astype(BF16)
    grads["w_mod"] = jnp.stack([_matmul(s_rows, dm_cols[l], "tn", F32, "mod_dw", bn_cap=1280) for l in range(2)])
    ds_part = None
    for l in range(2):
        part = _matmul(dm_cols[l, 16:32], w_mod[l], "nt", F32, "mod_ds", bk_cap=1280)[0]
        ds_part = part if ds_part is None else ds_part + part

    def shard_sum(recv, name):
        return _sum_blocks(recv.reshape(N_DEV, recv.shape[2], recv.shape[3]), name)

    gate_g = [[None, None], [None, None]]
    up_g = [[None, None], [None, None]]
    down_g = [[None, None], [None, None]]
    for l in range(2):
        for i in range(2):
            gt, ut, dn = [shard_sum(r, "sum_ffn") for r in recv_ffn[l][i]]
            gate_g[l][i], up_g[l][i], down_g[l][i] = gt.T, ut.T, dn
    grads["ffn_w_gate"] = jnp.stack([jnp.stack(gate_g[l]) for l in range(2)])
    grads["ffn_w_up"] = jnp.stack([jnp.stack(up_g[l]) for l in range(2)])
    grads["ffn_w_down"] = jnp.stack([jnp.stack(down_g[l]) for l in range(2)])
    grads["mix_ab_w_in"] = shard_sum(recv_ab[0], "sum_mix_in").T[None]
    grads["mix_ab_w_out"] = shard_sum(recv_ab[1], "sum_mix_out")[None]
    grads["lru_w_in"] = shard_sum(recv_lru[0], "sum_lru_in").T[None]
    grads["lru_w_out"] = shard_sum(recv_lru[1], "sum_lru_out")[None]
    rep_parts = [shard_sum(recv_lru[2], "sum_rep"), shard_sum(recv_lru[3], "sum_rep"), shard_sum(recv_ab[2], "sum_rep")]
    rep_names = ["lru_wa", "lru_wx", "pool_w"]

    dln_g_f = jnp.stack([jnp.stack(dln_g[l]) for l in range(2)])
    dln_b_f = jnp.stack([jnp.stack(dln_b[l]) for l in range(2)])
    sink_pad = jnp.zeros((1, 128), F32).at[0, :8].set(grads["attn_sink"][0])
    part_list = [p_.reshape(-1, 128) for p_ in rep_parts] + [
        dln_g_f.reshape(-1, 128), dln_b_f.reshape(-1, 128), grads["lru_conv_w"].reshape(-1, 128),
        grads["lru_conv_b"].reshape(-1, 128), grads["lru_ba"].reshape(-1, 128), grads["lru_bx"].reshape(-1, 128),
        grads["lru_lambda"].reshape(-1, 128), ds_part.reshape(-1, 128), sink_pad, grads["pool_scale"].reshape(-1, 128)]
    parts, part_off = _pack_rows(part_list)
    parts_all = _all_gather(parts, "gather_partials", True)
    parts_sum = _sum_blocks(parts_all, "sum_partials")

    for i, n in enumerate(rep_names):
        rows = part_list[i].shape[0]
        grads[n] = parts_all[:, part_off[i]:part_off[i] + rows, :].reshape(weights[n].shape)

    def take(idx):
        return parts_sum[part_off[idx]:part_off[idx] + part_list[idx].shape[0]]

    def my_cols(full, shp):
        w = shp[-1]
        return lax.dynamic_slice_in_dim(full, me * w, w, axis=full.ndim - 1)

    grads["ln_g"] = my_cols(take(3).reshape(2, 3, D), ln_g.shape)
    grads["ln_b"] = my_cols(take(4).reshape(2, 3, D), ln_b.shape)
    grads["lru_conv_w"] = my_cols(take(5).reshape(1, 4, D), lru_conv_w.shape)
    grads["lru_conv_b"] = my_cols(take(6).reshape(1, D), lru_conv_b.shape)
    grads["lru_ba"] = my_cols(take(7).reshape(1, 2, D), lru_ba.shape)
    grads["lru_bx"] = my_cols(take(8).reshape(1, 2, D), lru_bx.shape)
    grads["lru_lambda"] = my_cols(take(9).reshape(1, 2, D), lru_lambda.shape)
    sg = jax.nn.sigmoid(c_ctx)
    grads["c_ctx"] = take(10).reshape(D) * (sg * (1.0 + c_ctx * (1.0 - sg)))
    grads["attn_sink"] = take(11)[:, :8]
    grads["pool_scale"] = take(12).reshape(pool_scale.shape)

    delta, new_m, new_v = {}, {}, {}
    for n in names:
        shp = weights[n].shape
        grads[n] = grads[n].reshape(shp)
        delta[n], new_m[n], new_v[n] = _adamw(weights[n], grads[n], mom_m[n], mom_v[n], "adamw")

    return (loss, grad_x, *[grads[n] for n in names], *[delta[n] for n in names],
            *[new_m[n] for n in names], *[new_v[n] for n in names])
```

```python
import functools
import math

import jax
import jax.numpy as jnp
from jax import lax
from jax.experimental import pallas as pl
from jax.experimental.pallas import tpu as pltpu

F32 = jnp.float32
BF16 = jnp.bfloat16
MESH = pl.DeviceIdType.MESH

D = 1024
N_MOD = 9
N_DEV = 8
HEAD_DIM = 64
ATT_HEADS = 8
KV_HEADS = 2
ATT_W = 512
BLK = 128
ATT_SCALE = HEAD_DIM ** -0.5
GRID_W = 64
ROPE_FREQS = HEAD_DIM // 4
ROPE_THETA = 10000.0
POOL_R = (1, 2, 4, 8)
LRU_C = 8.0
LN_EPS = 1e-5
NEG_INF = -1e30
ALPHA = 4.0 ** 0.25
LR, B1, B2, EPS, WD, STEP = 0.001, 0.9, 0.999, 1e-08, 0.01, 10
VMEM_LIMIT = 56 * 1024 * 1024
ROW_TILE = 512


def _params(sem=None):
    if sem is None:
        return pltpu.CompilerParams(vmem_limit_bytes=VMEM_LIMIT)
    return pltpu.CompilerParams(dimension_semantics=sem, vmem_limit_bytes=VMEM_LIMIT)


def _sigmoid(x):
    return 0.5 * jnp.tanh(0.5 * x) + 0.5


def _dot(a, b):
    return jnp.dot(a.astype(BF16), b.astype(BF16), preferred_element_type=F32)


def _dot_nt(a, b):
    return lax.dot_general(a.astype(BF16), b.astype(BF16), (((1,), (1,)), ((), ())), preferred_element_type=F32)


def _dot_tn(a, b):
    return lax.dot_general(a.astype(BF16), b.astype(BF16), (((0,), (0,)), ((), ())), preferred_element_type=F32)


def _pick(n, cap):
    best = None
    for m in range(128, min(n, cap) + 1, 128):
        if n % m == 0:
            best = m
    return n if best is None else best


def _chunks(width, step=256):
    out, c = [], 0
    while c < width:
        w = min(step, width - c)
        out.append((c, w))
        c += w
    return out


class _Cfg:
    def __init__(self, n_lat, n_ctx):
        self.n_lat, self.n_ctx = n_lat, n_ctx
        self.t_lat, self.t_ctx = 2 * n_lat, 2 * n_ctx
        self.T = self.t_lat + self.t_ctx
        self.tm = min(ROW_TILE, self.t_ctx)
        assert n_lat % self.tm == 0 and self.t_ctx % self.tm == 0 and n_lat >= 3 * BLK and n_ctx % BLK == 0
        self.nt = self.T // self.tm
        self.nlt = n_lat // self.tm
        self.ctx_blk = self.t_lat // n_ctx

    def seg(self, i):
        return jnp.minimum(i // self.nlt, 2)

    def first_of_seg(self, i):
        return jnp.where(i < 2 * self.nlt, i % self.nlt == 0, i == 2 * self.nlt)


def _modulate(cfg, h, mod, shift_idx, scale_idx, name):
    tm = cfg.tm

    def body(h_ref, mod_ref, o_ref):
        sh = mod_ref[shift_idx:shift_idx + 1, :]
        sc = mod_ref[scale_idx:scale_idx + 1, :]
        o_ref[...] = (h_ref[...] * (1.0 + sc) + sh).astype(BF16)

    return pl.pallas_call(
        body, grid=(cfg.nt,), name=name,
        in_specs=[pl.BlockSpec((tm, D), lambda i: (i, 0)),
                  pl.BlockSpec((None, N_MOD, D), lambda i: (cfg.seg(i), 0, 0))],
        out_specs=pl.BlockSpec((tm, D), lambda i: (i, 0)),
        out_shape=jax.ShapeDtypeStruct((cfg.T, D), BF16),
        compiler_params=_params(("parallel",)),
    )(h, mod)


def _ln_fwd(cfg, h, y, mod, gate_idx, coef, lng, lnb, mod_next, next_idx, name):
    tm = cfg.tm
    has_next = next_idx is not None

    def body(*refs):
        if has_next:
            h_ref, y_ref, mod_ref, g_ref, b_ref, modn_ref, hn_ref, xhat_ref, rstd_ref, xin_ref = refs
        else:
            h_ref, y_ref, mod_ref, g_ref, b_ref, hn_ref, xhat_ref, rstd_ref = refs
        gate = mod_ref[gate_idx:gate_idx + 1, :]
        z = ALPHA * h_ref[...] + (coef * gate) * y_ref[...].astype(F32)
        mu = jnp.mean(z, axis=-1, keepdims=True)
        zc = z - mu
        var = jnp.mean(zc * zc, axis=-1, keepdims=True)
        rstd = lax.rsqrt(var + LN_EPS)
        xhat = zc * rstd
        hn = xhat * g_ref[...] + b_ref[...]
        hn_ref[...] = hn
        xhat_ref[...] = xhat.astype(BF16)
        rstd_ref[...] = rstd
        if has_next:
            sh = modn_ref[next_idx[0]:next_idx[0] + 1, :]
            sc = modn_ref[next_idx[1]:next_idx[1] + 1, :]
            xin_ref[...] = (hn * (1.0 + sc) + sh).astype(BF16)

    row = pl.BlockSpec((tm, D), lambda i: (i, 0))
    modspec = pl.BlockSpec((None, N_MOD, D), lambda i: (cfg.seg(i), 0, 0))
    vec = pl.BlockSpec((1, D), lambda i: (0, 0))
    in_specs = [row, row, modspec, vec, vec]
    args = [h, y, mod, lng, lnb]
    out_specs = [row, row, pl.BlockSpec((tm, 1), lambda i: (i, 0))]
    out_shape = [jax.ShapeDtypeStruct((cfg.T, D), F32), jax.ShapeDtypeStruct((cfg.T, D), BF16),
                 jax.ShapeDtypeStruct((cfg.T, 1), F32)]
    if has_next:
        in_specs.append(modspec)
        args.append(mod_next)
        out_specs.append(row)
        out_shape.append(jax.ShapeDtypeStruct((cfg.T, D), BF16))
    return pl.pallas_call(body, grid=(cfg.nt,), name=name, in_specs=in_specs, out_specs=out_specs,
                          out_shape=out_shape, compiler_params=_params(("parallel",)))(*args)


def _ln_bwd(cfg, up, xhat, rstd, y, mod, gate_idx, coef, lng, name):
    tm = cfg.tm
    fused = len(up) > 1
    scale_next = up[4] if fused else None

    def body(*refs):
        if fused:
            dres_n, dxin_n, hn_ref, modn_ref, xhat_ref, rstd_ref, y_ref, mod_ref, g_ref, dres_ref, dys_ref, st_ref = refs
        else:
            dhn_ref, xhat_ref, rstd_ref, y_ref, mod_ref, g_ref, dres_ref, dys_ref, st_ref = refs
        i = pl.program_id(0)

        @pl.when(cfg.first_of_seg(i))
        def _():
            st_ref[...] = jnp.zeros_like(st_ref)

        if fused:
            dxin = dxin_n[...]
            sc = modn_ref[scale_next:scale_next + 1, :]
            dhn = dres_n[...] + dxin * (1.0 + sc)
            st_ref[3:4, :] += jnp.sum(dxin * hn_ref[...], axis=0, keepdims=True)
            st_ref[4:5, :] += jnp.sum(dxin, axis=0, keepdims=True)
        else:
            dhn = dhn_ref[...]
        xhat = xhat_ref[...].astype(F32)
        gdh = dhn * g_ref[...]
        m1 = jnp.mean(gdh, axis=-1, keepdims=True)
        m2 = jnp.mean(gdh * xhat, axis=-1, keepdims=True)
        dz = rstd_ref[...] * (gdh - m1 - xhat * m2)
        gate = mod_ref[gate_idx:gate_idx + 1, :]
        dres_ref[...] = ALPHA * dz
        dys_ref[...] = ((coef * gate) * dz).astype(BF16)
        st_ref[0:1, :] += jnp.sum(dhn * xhat, axis=0, keepdims=True)
        st_ref[1:2, :] += jnp.sum(dhn, axis=0, keepdims=True)
        st_ref[2:3, :] += jnp.sum((coef * dz) * y_ref[...].astype(F32), axis=0, keepdims=True)

    row = pl.BlockSpec((tm, D), lambda i: (i, 0))
    modspec = pl.BlockSpec((None, N_MOD, D), lambda i: (cfg.seg(i), 0, 0))
    vec = pl.BlockSpec((1, D), lambda i: (0, 0))
    col = pl.BlockSpec((tm, 1), lambda i: (i, 0))
    if fused:
        in_specs = [row, row, row, modspec, row, col, row, modspec, vec]
        args = [up[0], up[1], up[2], up[3], xhat, rstd, y, mod, lng]
    else:
        in_specs = [row, row, col, row, modspec, vec]
        args = [up[0], xhat, rstd, y, mod, lng]
    return pl.pallas_call(
        body, grid=(cfg.nt,), name=name, in_specs=in_specs,
        out_specs=[row, row, pl.BlockSpec((None, 8, D), lambda i: (cfg.seg(i), 0, 0))],
        out_shape=[jax.ShapeDtypeStruct((cfg.T, D), F32), jax.ShapeDtypeStruct((cfg.T, D), BF16),
                   jax.ShapeDtypeStruct((3, 8, D), F32)],
        compiler_params=_params(("arbitrary",)))(*args)


def _modulate_bwd(cfg, dres, dxin, h, mod, scale_idx, name):
    tm = cfg.tm

    def body(dres_ref, dxin_ref, h_ref, mod_ref, dh_ref, st_ref):
        i = pl.program_id(0)

        @pl.when(cfg.first_of_seg(i))
        def _():
            st_ref[...] = jnp.zeros_like(st_ref)

        dxin = dxin_ref[...]
        sc = mod_ref[scale_idx:scale_idx + 1, :]
        dh_ref[...] = dres_ref[...] + dxin * (1.0 + sc)
        st_ref[3:4, :] += jnp.sum(dxin * h_ref[...], axis=0, keepdims=True)
        st_ref[4:5, :] += jnp.sum(dxin, axis=0, keepdims=True)

    row = pl.BlockSpec((tm, D), lambda i: (i, 0))
    return pl.pallas_call(
        body, grid=(cfg.nt,), name=name,
        in_specs=[row, row, row, pl.BlockSpec((None, N_MOD, D), lambda i: (cfg.seg(i), 0, 0))],
        out_specs=[row, pl.BlockSpec((None, 8, D), lambda i: (cfg.seg(i), 0, 0))],
        out_shape=[jax.ShapeDtypeStruct((cfg.T, D), F32), jax.ShapeDtypeStruct((3, 8, D), F32)],
        compiler_params=_params(("arbitrary",)))(dres, dxin, h, mod)


def _loss(cfg, h, target, name):
    tm = cfg.tm
    n_lt = 2 * cfg.nlt

    def body(h_ref, t_ref, dy_ref, l_ref):
        i = pl.program_id(0)

        @pl.when(i == 0)
        def _():
            l_ref[...] = jnp.zeros_like(l_ref)

        @pl.when(i < n_lt)
        def _():
            err = h_ref[...] - t_ref[...]
            dy_ref[...] = err * (1.0 / D)
            part = jnp.sum(jnp.sum(err * err, axis=1, keepdims=True), axis=0, keepdims=True) * (0.5 / D)
            l_ref[...] += jnp.broadcast_to(part, l_ref.shape)

        @pl.when(i >= n_lt)
        def _():
            dy_ref[...] = jnp.zeros_like(dy_ref)

    return pl.pallas_call(
        body, grid=(cfg.nt,), name=name,
        in_specs=[pl.BlockSpec((tm, D), lambda i: (i, 0)),
                  pl.BlockSpec((tm, D), lambda i: (jnp.minimum(i, n_lt - 1), 0))],
        out_specs=[pl.BlockSpec((tm, D), lambda i: (i, 0)), pl.BlockSpec((8, 128), lambda i: (0, 0))],
        out_shape=[jax.ShapeDtypeStruct((cfg.T, D), F32), jax.ShapeDtypeStruct((8, 128), F32)],
        compiler_params=_params(("arbitrary",)))(h, target)


def _matmul(a, b, mode, out_dtype, name, bm_cap=512, bn_cap=1408, bk_cap=1024):
    if mode == "nn":
        (M, K), N = a.shape, b.shape[1]
    elif mode == "nt":
        (M, K), N = a.shape, b.shape[0]
    else:
        (K, M), N = a.shape, b.shape[1]
    bm, bn, bk = _pick(M, bm_cap), _pick(N, bn_cap), _pick(K, bk_cap)
    nk = K // bk

    def body(a_ref, b_ref, o_ref, acc_ref=None):
        k = pl.program_id(2)
        if mode == "nn":
            part = _dot(a_ref[...], b_ref[...])
        elif mode == "nt":
            part = _dot_nt(a_ref[...], b_ref[...])
        else:
            part = _dot_tn(a_ref[...], b_ref[...])
        if nk == 1:
            o_ref[...] = part.astype(out_dtype)
            return

        @pl.when(k == 0)
        def _():
            acc_ref[...] = part

        @pl.when((k > 0) & (k < nk - 1))
        def _():
            acc_ref[...] += part

        @pl.when(k == nk - 1)
        def _():
            o_ref[...] = (acc_ref[...] + part).astype(out_dtype)

    if mode == "nn":
        a_spec = pl.BlockSpec((bm, bk), lambda i, j, k: (i, k))
        b_spec = pl.BlockSpec((bk, bn), lambda i, j, k: (k, j))
    elif mode == "nt":
        a_spec = pl.BlockSpec((bm, bk), lambda i, j, k: (i, k))
        b_spec = pl.BlockSpec((bn, bk), lambda i, j, k: (j, k))
    else:
        a_spec = pl.BlockSpec((bk, bm), lambda i, j, k: (k, i))
        b_spec = pl.BlockSpec((bk, bn), lambda i, j, k: (k, j))
    return pl.pallas_call(
        body, grid=(M // bm, N // bn, nk), name=name, in_specs=[a_spec, b_spec],
        out_specs=pl.BlockSpec((bm, bn), lambda i, j, k: (i, j)),
        out_shape=jax.ShapeDtypeStruct((M, N), out_dtype),
        scratch_shapes=[pltpu.VMEM((bm, bn), F32)] if nk > 1 else [],
        compiler_params=_params(("parallel", "parallel", "arbitrary")))(a, b)


def _ffn_tile(T, cap):
    best = 256
    for t in range(256, cap + 1, 256):
        if T % t == 0:
            best = t
    return best


def _ffn_fwd(xin, wf, name):
    T = xin.shape[0]
    F = wf.shape[1]
    tm, tf = _ffn_tile(T, 768), F // 2
    assert tf % 128 == 0 and T % tm == 0

    def body(x_ref, wg_ref, wu_ref, wd_ref, g_ref, u_ref, y_ref, acc_ref):
        j = pl.program_id(1)
        x = x_ref[...]
        acc = None
        for c0, cw in _chunks(tf):
            g = _dot_nt(x, wg_ref[c0:c0 + cw, :])
            u = _dot_nt(x, wu_ref[c0:c0 + cw, :])
            g_ref[:, c0:c0 + cw] = g.astype(BF16)
            u_ref[:, c0:c0 + cw] = u.astype(BF16)
            part = _dot(g * _sigmoid(g) * u, wd_ref[c0:c0 + cw, :])
            acc = part if acc is None else acc + part

        @pl.when(j == 0)
        def _():
            acc_ref[...] = acc

        @pl.when(j == 1)
        def _():
            y_ref[...] = (acc_ref[...] + acc).astype(BF16)

    return pl.pallas_call(
        body, grid=(T // tm, 2), name=name,
        in_specs=[pl.BlockSpec((tm, D), lambda i, j: (i, 0)),
                  pl.BlockSpec((None, tf, D), lambda i, j: (0, j, 0)),
                  pl.BlockSpec((None, tf, D), lambda i, j: (1, j, 0)),
                  pl.BlockSpec((None, tf, D), lambda i, j: (2, j, 0))],
        out_specs=[pl.BlockSpec((tm, tf), lambda i, j: (i, j)),
                   pl.BlockSpec((tm, tf), lambda i, j: (i, j)),
                   pl.BlockSpec((tm, D), lambda i, j: (i, 0))],
        out_shape=[jax.ShapeDtypeStruct((T, F), BF16), jax.ShapeDtypeStruct((T, F), BF16),
                   jax.ShapeDtypeStruct((T, D), BF16)],
        scratch_shapes=[pltpu.VMEM((tm, D), F32)],
        compiler_params=_params(("parallel", "arbitrary")))(xin, wf, wf, wf)


def _ffn_bwd(dys, g, u, wf, name):
    T = dys.shape[0]
    F = wf.shape[1]
    tm, tf = _ffn_tile(T, 512), F // 2

    def body(dy_ref, g_ref, u_ref, wg_ref, wu_ref, wd_ref, dg_ref, du_ref, a_ref, dx_ref):
        j = pl.program_id(1)
        da_all = _dot_nt(dy_ref[...], wd_ref[...])
        for c0, cw in _chunks(tf):
            gg = g_ref[:, c0:c0 + cw].astype(F32)
            uu = u_ref[:, c0:c0 + cw].astype(F32)
            da = da_all[:, c0:c0 + cw]
            s = _sigmoid(gg)
            silu = gg * s
            a_ref[:, c0:c0 + cw] = (silu * uu).astype(BF16)
            du_ref[:, c0:c0 + cw] = (da * silu).astype(BF16)
            dg_ref[:, c0:c0 + cw] = (da * uu * (s * (1.0 + gg * (1.0 - s)))).astype(BF16)
        acc = _dot(dg_ref[...], wg_ref[...]) + _dot(du_ref[...], wu_ref[...])

        @pl.when(j == 0)
        def _():
            dx_ref[...] = acc

        @pl.when(j > 0)
        def _():
            dx_ref[...] += acc

    blk = pl.BlockSpec((tm, tf), lambda i, j: (i, j))
    return pl.pallas_call(
        body, grid=(T // tm, 2), name=name,
        in_specs=[pl.BlockSpec((tm, D), lambda i, j: (i, 0)), blk, blk,
                  pl.BlockSpec((None, tf, D), lambda i, j: (0, j, 0)),
                  pl.BlockSpec((None, tf, D), lambda i, j: (1, j, 0)),
                  pl.BlockSpec((None, tf, D), lambda i, j: (2, j, 0))],
        out_specs=[blk, blk, blk, pl.BlockSpec((tm, D), lambda i, j: (i, 0))],
        out_shape=[jax.ShapeDtypeStruct((T, F), BF16), jax.ShapeDtypeStruct((T, F), BF16),
                   jax.ShapeDtypeStruct((T, F), BF16), jax.ShapeDtypeStruct((T, D), F32)],
        compiler_params=_params(("parallel", "arbitrary")))(dys, g, u, wf, wf, wf)


def _swap_halves(x):
    w = x.shape[1]
    lane = lax.broadcasted_iota(jnp.int32, (1, w), 1)
    return jnp.where((lane & 63) < 32, pltpu.roll(x, w - 32, 1), pltpu.roll(x, 32, 1))


def _rope(x, cos, sin):
    return x * cos + _swap_halves(x) * sin


def _rope_t(dy, cos, sin):
    return dy * cos + _swap_halves(dy * sin)


def _rope_tables(n_lat):
    rows = n_lat // GRID_W
    row = jnp.repeat(jnp.arange(rows, dtype=F32), GRID_W)
    col = jnp.tile(jnp.arange(GRID_W, dtype=F32), rows)
    inv = ROPE_THETA ** (-jnp.arange(ROPE_FREQS, dtype=F32) / ROPE_FREQS)
    ang = jnp.concatenate([row[:, None] * inv, col[:, None] * inv], axis=-1)
    cs, sn = jnp.cos(ang), jnp.sin(ang)
    cos = jnp.concatenate([cs, cs, cs, cs], axis=-1)
    sin = jnp.concatenate([-sn, sn, -sn, sn], axis=-1)
    return cos, sin


def _attn_specs(cfg):
    n_lat, n_ctx, cb = cfg.n_lat, cfg.n_ctx, cfg.ctx_blk
    return [pl.BlockSpec((n_lat, ATT_W), lambda e: (e, 0)),
            pl.BlockSpec((n_lat, 128), lambda e: (e, 4)),
            pl.BlockSpec((n_lat, 128), lambda e: (e, 5)),
            pl.BlockSpec((n_ctx, ATT_W), lambda e: (cb + e, 0)),
            pl.BlockSpec((n_ctx, 128), lambda e: (cb + e, 4)),
            pl.BlockSpec((n_ctx, 128), lambda e: (cb + e, 5)),
            pl.BlockSpec((n_lat, 128), lambda e: (0, 0)),
            pl.BlockSpec((n_lat, 128), lambda e: (0, 0)),
            pl.BlockSpec((8, 128), lambda e: (0, 0))]


def _attn_prepare(kh, kl, vl, kc, vc, ka, kb, va, vb, kca, kcb, vca, vcb):
    lane = lax.broadcasted_iota(jnp.int32, (1, 128), 1)
    own = (lane < 64) if kh == 0 else (lane >= 64)

    def split(x, ra, rb):
        mine = jnp.where(own, x, 0.0)
        other = pltpu.roll(mine, 64, 1)
        a, b = (mine, other) if kh == 0 else (other, mine)
        ra[...] = a.astype(BF16)
        rb[...] = b.astype(BF16)

    split(kl, ka, kb)
    split(vl, va, vb)
    split(kc, kca, kcb)
    split(vc, vca, vcb)


def _softmax_parts(s_list, sk):
    m = sk
    for s in s_list:
        m = jnp.maximum(m, jnp.max(s, axis=1, keepdims=True))
    es = [jnp.exp(s - m) for s in s_list]
    esk = jnp.exp(sk - m)
    den = esk
    for e in es:
        den = den + jnp.sum(e, axis=1, keepdims=True)
    inv = 1.0 / den
    return [e * inv for e in es], esk * inv


def _window(cfg, n):
    r0 = pl.multiple_of(n * BLK, BLK)
    start = pl.multiple_of(jnp.clip((n - 1) * BLK, 0, cfg.n_lat - 3 * BLK), BLK)
    qpos = r0 + lax.broadcasted_iota(jnp.int32, (BLK, 1), 0)
    kpos = start + lax.broadcasted_iota(jnp.int32, (1, 3 * BLK), 1)
    valid = jnp.abs(qpos - kpos) <= BLK
    return r0, start, valid


def _attn_fwd(cfg, p, cos, sin, sink_rows, name):
    n_lat, n_ctx = cfg.n_lat, cfg.n_ctx

    def body(q_ref, k_ref, v_ref, qc_ref, kc_ref, vc_ref, cos_ref, sin_ref, sink_ref, o_ref, oc_ref,
             qr, ka, kb, va, vb, kca, kcb, vca, vcb):
        cos_t, sin_t = cos_ref[...], sin_ref[...]
        for gq in range(4):
            qr[:, gq * 128:(gq + 1) * 128] = _rope(q_ref[:, gq * 128:(gq + 1) * 128], cos_t, sin_t).astype(BF16)
        kl = _rope(k_ref[...], cos_t, sin_t)
        for kh in range(KV_HEADS):
            _attn_prepare(kh, kl, v_ref[...], kc_ref[...], vc_ref[...], ka, kb, va, vb, kca, kcb, vca, vcb)

            def lat_block(n, carry):
                r0, start, valid = _window(cfg, n)
                win = pl.ds(start, 3 * BLK)
                for pr in range(2):
                    lanes = slice((kh * 2 + pr) * 128, (kh * 2 + pr + 1) * 128)
                    qp = qr[pl.ds(r0, BLK), lanes]
                    o = None
                    for half, (kw, kcx, vw, vcx) in enumerate(((ka, kca, va, vca), (kb, kcb, vb, vcb))):
                        head = kh * 4 + pr * 2 + half
                        s_w = jnp.where(valid, _dot_nt(qp, kw[win, :]) * ATT_SCALE, NEG_INF)
                        s_c = _dot_nt(qp, kcx[...]) * ATT_SCALE
                        (p_w, p_c), _ = _softmax_parts([s_w, s_c], sink_ref[head:head + 1, 0:1])
                        part = _dot(p_w, vw[win, :]) + _dot(p_c, vcx[...])
                        o = part if o is None else o + part
                    o_ref[pl.ds(r0, BLK), lanes] = o.astype(BF16)
                return carry

            lax.fori_loop(0, n_lat // BLK, lat_block, 0)
            for n in range(n_ctx // BLK):
                rows = slice(n * BLK, (n + 1) * BLK)
                for pr in range(2):
                    lanes = slice((kh * 2 + pr) * 128, (kh * 2 + pr + 1) * 128)
                    qp = qc_ref[rows, lanes]
                    o = None
                    for half, (kcx, vcx) in enumerate(((kca, vca), (kcb, vcb))):
                        head = kh * 4 + pr * 2 + half
                        s_c = _dot_nt(qp, kcx[...]) * ATT_SCALE
                        (p_c,), _ = _softmax_parts([s_c], sink_ref[head:head + 1, 0:1])
                        part = _dot(p_c, vcx[...])
                        o = part if o is None else o + part
                    oc_ref[rows, lanes] = o.astype(BF16)

    return pl.pallas_call(
        body, grid=(2,), name=name, in_specs=_attn_specs(cfg),
        out_specs=[pl.BlockSpec((n_lat, ATT_W), lambda e: (e, 0)), pl.BlockSpec((n_ctx, ATT_W), lambda e: (e, 0))],
        out_shape=[jax.ShapeDtypeStruct((cfg.t_lat, ATT_W), BF16), jax.ShapeDtypeStruct((cfg.t_ctx, ATT_W), BF16)],
        scratch_shapes=[pltpu.VMEM((n_lat, ATT_W), BF16)] + [pltpu.VMEM((n_lat, 128), BF16)] * 4
        + [pltpu.VMEM((n_ctx, 128), BF16)] * 4,
        compiler_params=_params(("parallel",)))(p, p, p, p, p, p, cos, sin, sink_rows)


def _attn_bwd(cfg, p, dcat, cos, sin, sink_rows, name):
    n_lat, n_ctx, cb = cfg.n_lat, cfg.n_ctx, cfg.ctx_blk

    def body(q_ref, k_ref, v_ref, qc_ref, kc_ref, vc_ref, cos_ref, sin_ref, sink_ref, do_ref, doc_ref,
             dq_ref, dk_ref, dv_ref, dqc_ref, dkc_ref, dvc_ref, dsink_ref,
             qr, ka, kb, va, vb, kca, kcb, vca, vcb, dqs, dka, dva, dkca, dvca):
        cos_t, sin_t = cos_ref[...], sin_ref[...]
        lane = lax.broadcasted_iota(jnp.int32, (1, 128), 1)
        lo = lane < 64
        for gq in range(4):
            qr[:, gq * 128:(gq + 1) * 128] = _rope(q_ref[:, gq * 128:(gq + 1) * 128], cos_t, sin_t).astype(BF16)
        kl = _rope(k_ref[...], cos_t, sin_t)
        dsink_ref[...] = jnp.zeros_like(dsink_ref)
        dka[...] = jnp.zeros_like(dka)
        dva[...] = jnp.zeros_like(dva)
        dkca[...] = jnp.zeros_like(dkca)
        dvca[...] = jnp.zeros_like(dvca)

        def halves(x):
            return jnp.where(lo, x, 0).astype(BF16), jnp.where(lo, 0, x).astype(BF16)

        for kh in range(KV_HEADS):
            _attn_prepare(kh, kl, v_ref[...], kc_ref[...], vc_ref[...], ka, kb, va, vb, kca, kcb, vca, vcb)

            def one_head(head, qp, q_half, do_p, do_half, kw, kcx, vw, vcx, win, valid):
                sk = sink_ref[head:head + 1, 0:1]
                s_list = [_dot_nt(qp, kcx[...]) * ATT_SCALE]
                if win is not None:
                    s_list.insert(0, jnp.where(valid, _dot_nt(qp, kw[win, :]) * ATT_SCALE, NEG_INF))
                probs, p_sink = _softmax_parts(s_list, sk)
                vals = [vcx[...]] if win is None else [vw[win, :], vcx[...]]
                dps = [_dot_nt(do_p, vv) for vv in vals]
                dr = None
                for pp, dp in zip(probs, dps):
                    t = jnp.sum(pp * dp, axis=1, keepdims=True)
                    dr = t if dr is None else dr + t
                dss = [(pp * (dp - dr) * ATT_SCALE).astype(BF16) for pp, dp in zip(probs, dps)]
                dsink_ref[head:head + 1, :] += jnp.broadcast_to(
                    jnp.sum(-p_sink * dr, axis=0, keepdims=True), (1, 128))
                p_c, ds_c = probs[-1], dss[-1]
                dq = _dot(ds_c, kcx[...])
                dkca[kh] += _dot_tn(ds_c, q_half)
                dvca[kh] += _dot_tn(p_c, do_half)
                if win is not None:
                    dq = dq + _dot(dss[0], kw[win, :])
                    dka[kh, win, :] += _dot_tn(dss[0], q_half)
                    dva[kh, win, :] += _dot_tn(probs[0], do_half)
                return dq

            def lat_block(n, carry):
                r0, start, valid = _window(cfg, n)
                win = pl.ds(start, 3 * BLK)
                for pr in range(2):
                    lanes = slice((kh * 2 + pr) * 128, (kh * 2 + pr + 1) * 128)
                    qp = qr[pl.ds(r0, BLK), lanes]
                    do_p = do_ref[pl.ds(r0, BLK), lanes]
                    q_h, do_h = halves(qp), halves(do_p)
                    dq = None
                    for half, (kw, kcx, vw, vcx) in enumerate(((ka, kca, va, vca), (kb, kcb, vb, vcb))):
                        part = one_head(kh * 4 + pr * 2 + half, qp, q_h[half], do_p, do_h[half],
                                        kw, kcx, vw, vcx, win, valid)
                        dq = part if dq is None else dq + part
                    dqs[pl.ds(r0, BLK), lanes] = dq
                return carry

            lax.fori_loop(0, n_lat // BLK, lat_block, 0)
            for n in range(n_ctx // BLK):
                rows = slice(n * BLK, (n + 1) * BLK)
                for pr in range(2):
                    lanes = slice((kh * 2 + pr) * 128, (kh * 2 + pr + 1) * 128)
                    qp = qc_ref[rows, lanes].astype(BF16)
                    do_p = doc_ref[rows, lanes]
                    q_h, do_h = halves(qp), halves(do_p)
                    dq = None
                    for half, (kcx, vcx) in enumerate(((kca, vca), (kcb, vcb))):
                        part = one_head(kh * 4 + pr * 2 + half, qp, q_h[half], do_p, do_h[half],
                                        None, kcx, None, vcx, None, None)
                        dq = part if dq is None else dq + part
                    dqc_ref[rows, lanes] = dq.astype(BF16)

        def fold(acc):
            r0 = acc[0] + pltpu.roll(acc[0], 64, 1)
            r1 = acc[1] + pltpu.roll(acc[1], 64, 1)
            return jnp.where(lo, r0, r1)

        for gq in range(4):
            sl = slice(gq * 128, (gq + 1) * 128)
            dq_ref[:, sl] = _rope_t(dqs[:, sl], cos_t, sin_t).astype(BF16)
        dk_ref[...] = _rope_t(fold(dka), cos_t, sin_t).astype(BF16)
        dv_ref[...] = fold(dva).astype(BF16)
        dkc_ref[...] = fold(dkca).astype(BF16)
        dvc_ref[...] = fold(dvca).astype(BF16)

    lat = lambda w: pl.BlockSpec((n_lat, w), lambda e: (e, 0))
    ctx = lambda w: pl.BlockSpec((n_ctx, w), lambda e: (e, 0))
    sd = jax.ShapeDtypeStruct
    return pl.pallas_call(
        body, grid=(2,), name=name,
        in_specs=_attn_specs(cfg) + [pl.BlockSpec((n_lat, ATT_W), lambda e: (e, 0)),
                                     pl.BlockSpec((n_ctx, ATT_W), lambda e: (cb + e, 0))],
        out_specs=[lat(ATT_W), lat(128), lat(128), ctx(ATT_W), ctx(128), ctx(128),
                   pl.BlockSpec((None, 8, 128), lambda e: (e, 0, 0))],
        out_shape=[sd((cfg.t_lat, ATT_W), BF16), sd((cfg.t_lat, 128), BF16), sd((cfg.t_lat, 128), BF16),
                   sd((cfg.t_ctx, ATT_W), BF16), sd((cfg.t_ctx, 128), BF16), sd((cfg.t_ctx, 128), BF16),
                   sd((2, 8, 128), F32)],
        scratch_shapes=[pltpu.VMEM((n_lat, ATT_W), BF16)] + [pltpu.VMEM((n_lat, 128), BF16)] * 4
        + [pltpu.VMEM((n_ctx, 128), BF16)] * 4
        + [pltpu.VMEM((n_lat, ATT_W), F32), pltpu.VMEM((2, n_lat, 128), F32), pltpu.VMEM((2, n_lat, 128), F32),
           pltpu.VMEM((2, n_ctx, 128), F32), pltpu.VMEM((2, n_ctx, 128), F32)],
        compiler_params=_params(("parallel",)))(p, p, p, p, p, p, cos, sin, sink_rows, dcat, dcat)


def _shift_down(x, k, row):
    return jnp.where(row >= k, pltpu.roll(x, k, 0), 0.0)


def _shift_up(x, k, row):
    n = x.shape[0]
    return jnp.where(row < n - k, pltpu.roll(x, n - k, 0), 0.0)


def _window_sum(x, r, row):
    below, above, k = x, x, 1
    while k < r:
        below = below + _shift_down(below, k, row)
        above = above + _shift_up(above, k, row)
        k *= 2
    return below + _shift_down(x, r, row) + _shift_up(above, 1, row)


def _inv_count(r, row, n):
    cnt = jnp.minimum(row + r, n - 1) + 1 - jnp.maximum(row - r, 0)
    return 1.0 / cnt.astype(F32)


def _pool_fwd(p, w, scale, n, blk0, n_seg, name):
    def body(u0, u1, u2, u3, w_ref, sc_ref, o_ref):
        row = lax.broadcasted_iota(jnp.int32, (n, 1), 0)
        for g, u_ref in enumerate((u0, u1, u2, u3)):
            u = u_ref[...]
            d = _window_sum(u, POOL_R[g], row) * _inv_count(POOL_R[g], row, n) - u
            o_ref[:, g * 128:(g + 1) * 128] = (_dot(d, w_ref[g]) * sc_ref[:, g * 128:(g + 1) * 128]).astype(BF16)

    return pl.pallas_call(
        body, grid=(n_seg,), name=name,
        in_specs=[pl.BlockSpec((n, 128), functools.partial(lambda g, e: (blk0 + e, 6 + g), g)) for g in range(4)]
        + [pl.BlockSpec((4, 128, 128), lambda e: (0, 0, 0)), pl.BlockSpec((1, 512), lambda e: (0, 0))],
        out_specs=pl.BlockSpec((n, 512), lambda e: (e, 0)),
        out_shape=jax.ShapeDtypeStruct((n_seg * n, 512), BF16),
        compiler_params=_params(("parallel",)))(p, p, p, p, w, scale)


def _pool_bwd(p, w, scale, dcat, n, blk0, n_seg, name):
    def body(u0, u1, u2, u3, w_ref, sc_ref, dp_ref, du_ref, dw_ref, dsc_ref):
        e = pl.program_id(0)

        @pl.when(e == 0)
        def _():
            dw_ref[...] = jnp.zeros_like(dw_ref)
            dsc_ref[...] = jnp.zeros_like(dsc_ref)

        row = lax.broadcasted_iota(jnp.int32, (n, 1), 0)
        for g, u_ref in enumerate((u0, u1, u2, u3)):
            sl = slice(g * 128, (g + 1) * 128)
            u = u_ref[...]
            inv = _inv_count(POOL_R[g], row, n)
            d = _window_sum(u, POOL_R[g], row) * inv - u
            dp = dp_ref[:, sl]
            dsc_ref[:, sl] += jnp.sum(dp * _dot(d, w_ref[g]), axis=0, keepdims=True)
            dyp = dp * sc_ref[:, sl]
            dw_ref[g] += _dot_tn(d, dyp)
            dd = _dot_nt(dyp, w_ref[g])
            du_ref[:, sl] = (_window_sum(dd * inv, POOL_R[g], row) - dd).astype(BF16)

    return pl.pallas_call(
        body, grid=(n_seg,), name=name,
        in_specs=[pl.BlockSpec((n, 128), functools.partial(lambda g, e: (blk0 + e, 6 + g), g)) for g in range(4)]
        + [pl.BlockSpec((4, 128, 128), lambda e: (0, 0, 0)), pl.BlockSpec((1, 512), lambda e: (0, 0)),
           pl.BlockSpec((n, 512), lambda e: (blk0 + e, 1))],
        out_specs=[pl.BlockSpec((n, 512), lambda e: (e, 0)),
                   pl.BlockSpec((4, 128, 128), lambda e: (0, 0, 0)), pl.BlockSpec((1, 512), lambda e: (0, 0))],
        out_shape=[jax.ShapeDtypeStruct((n_seg * n, 512), BF16), jax.ShapeDtypeStruct((4, 128, 128), F32),
                   jax.ShapeDtypeStruct((1, 512), F32)],
        compiler_params=_params(("arbitrary",)))(p, p, p, p, w, scale, dcat)


def _gelu(x):
    t = jnp.tanh(math.sqrt(2.0 / math.pi) * (x + 0.044715 * x * x * x))
    return 0.5 * x * (1.0 + t), t


def _gelu_grad(x, t):
    return 0.5 * (1.0 + t) + 0.5 * x * (1.0 - t * t) * (math.sqrt(2.0 / math.pi) * (1.0 + 3 * 0.044715 * x * x))


def _neg_expm1(x):
    series = -x * (1.0 + x * (0.5 + x * (1.0 / 6.0 + x * (1.0 / 24.0 + x * (1.0 / 120.0)))))
    return jnp.where(x > -0.05, series, 1.0 - jnp.exp(x))


def _softplus_neg(lam):
    x = -lam
    e = jnp.exp(-jnp.abs(x))
    log1p = jnp.where(e < 1e-2, e * (1.0 - e * (0.5 - e * (1.0 / 3.0))), jnp.log(1.0 + e))
    return jnp.maximum(x, 0.0) + log1p, -_sigmoid(x)


def _conv(u, w_ref, b_ref, row):
    return (b_ref[...] + _shift_down(u, 1, row) * w_ref[0:1, :] + u * w_ref[1:2, :]
            + _shift_up(u, 1, row) * w_ref[2:3, :] + _shift_up(u, 2, row) * w_ref[3:4, :])


def _lru_gates(uc, d, wa_ref, ba_ref, wx_ref, bx_ref, lam_ref):
    r = _sigmoid(_dot(uc, wa_ref[d]) + ba_ref[d:d + 1, :])
    gi = _sigmoid(_dot(uc, wx_ref[d]) + bx_ref[d:d + 1, :])
    sp, dsp = _softplus_neg(lam_ref[d:d + 1, :])
    la = (-LRU_C) * r * sp
    a = jnp.exp(la)
    sq = jnp.sqrt(_neg_expm1(2.0 * la))
    return r, gi, sp, dsp, a, sq


def _tile_scan(a_ref, b_ref, n, reverse):
    m = n // 8
    first = 7 if reverse else 0
    a_prev = a_ref[pl.ds(first, m, stride=8), :]
    b_prev = b_ref[pl.ds(first, m, stride=8), :]
    for j in (range(6, -1, -1) if reverse else range(1, 8)):
        rows = pl.ds(j, m, stride=8)
        aj = a_ref[rows, :]
        b_prev = aj * b_prev + b_ref[rows, :]
        a_prev = aj * a_prev
        b_ref[rows, :] = b_prev
        a_ref[rows, :] = a_prev


def _carry_scan(a_ref, b_ref, n, reverse, carry):
    nt8 = n // 8

    def step(i, c):
        t = (nt8 - 1 - i) if reverse else i
        off = pl.multiple_of(t * 8, 8)
        h = a_ref[pl.ds(off, 8), :] * c + b_ref[pl.ds(off, 8), :]
        b_ref[pl.ds(off, 8), :] = h
        return h[0:1, :] if reverse else h[7:8, :]

    return lax.fori_loop(0, nt8, step, carry)


def _chain_scan(segs, reverse):
    carry = jnp.zeros((1, 128), F32)
    for a, b, a_ref, b_ref, n in segs:
        a_ref[...] = a
        b_ref[...] = b
        _tile_scan(a_ref, b_ref, n, reverse)
        carry = _carry_scan(a_ref, b_ref, n, reverse, carry)


def _lru_specs(cfg):
    n_lat, n_ctx, cb = cfg.n_lat, cfg.n_ctx, cfg.ctx_blk
    return [pl.BlockSpec((n_lat, 128), lambda hb, e: (e, hb)),
            pl.BlockSpec((n_lat, 128), lambda hb, e: (e, 8 + hb)),
            pl.BlockSpec((n_ctx, 128), lambda hb, e: (cb + e, hb)),
            pl.BlockSpec((n_ctx, 128), lambda hb, e: (cb + e, 8 + hb)),
            pl.BlockSpec((4, 128), lambda hb, e: (0, hb)),
            pl.BlockSpec((1, 128), lambda hb, e: (0, hb)),
            pl.BlockSpec((2, None, 128, 128), lambda hb, e: (0, hb, 0, 0)),
            pl.BlockSpec((2, 128), lambda hb, e: (0, hb)),
            pl.BlockSpec((2, None, 128, 128), lambda hb, e: (0, hb, 0, 0)),
            pl.BlockSpec((2, 128), lambda hb, e: (0, hb)),
            pl.BlockSpec((2, 128), lambda hb, e: (0, hb))]


def _lru_fwd(cfg, p, consts, name):
    n_lat, n_ctx = cfg.n_lat, cfg.n_ctx

    def body(gl_ref, ul_ref, gc_ref, uc_ref, cw_ref, cb_ref, wa_ref, ba_ref, wx_ref, bx_ref, lam_ref,
             zl_ref, zc_ref, hl_ref, hc_ref, al, ac):
        row_l = lax.broadcasted_iota(jnp.int32, (n_lat, 1), 0)
        row_c = lax.broadcasted_iota(jnp.int32, (n_ctx, 1), 0)
        uc_l = _conv(ul_ref[...], cw_ref, cb_ref, row_l)
        uc_c = _conv(uc_ref[...], cw_ref, cb_ref, row_c)
        for d in range(2):
            _, gi_l, _, _, a_l, sq_l = _lru_gates(uc_l, d, wa_ref, ba_ref, wx_ref, bx_ref, lam_ref)
            _, gi_c, _, _, a_c, sq_c = _lru_gates(uc_c, d, wa_ref, ba_ref, wx_ref, bx_ref, lam_ref)
            _chain_scan([(a_c, sq_c * (gi_c * uc_c), ac, hc_ref.at[d], n_ctx),
                         (a_l, sq_l * (gi_l * uc_l), al, hl_ref.at[d], n_lat)], reverse=(d == 1))
        zl_ref[...] = (_gelu(gl_ref[...])[0] * (hl_ref[0] + hl_ref[1])).astype(BF16)
        zc_ref[...] = (_gelu(gc_ref[...])[0] * (hc_ref[0] + hc_ref[1])).astype(BF16)

    return pl.pallas_call(
        body, grid=(8, 2), name=name, in_specs=_lru_specs(cfg),
        out_specs=[pl.BlockSpec((n_lat, 128), lambda hb, e: (e, hb)), pl.BlockSpec((n_ctx, 128), lambda hb, e: (e, hb)),
                   pl.BlockSpec((2, n_lat, 128), lambda hb, e: (0, e, hb)),
                   pl.BlockSpec((2, n_ctx, 128), lambda hb, e: (0, e, hb))],
        out_shape=[jax.ShapeDtypeStruct((cfg.t_lat, D), BF16), jax.ShapeDtypeStruct((cfg.t_ctx, D), BF16),
                   jax.ShapeDtypeStruct((2, cfg.t_lat, D), F32), jax.ShapeDtypeStruct((2, cfg.t_ctx, D), F32)],
        scratch_shapes=[pltpu.VMEM((n_lat, 128), F32), pltpu.VMEM((n_ctx, 128), F32)],
        compiler_params=_params(("parallel", "arbitrary")))(p, p, p, p, *consts)


def _lru_bwd(cfg, p, dz, h_lat, h_ctx, consts, name):
    n_lat, n_ctx, cb = cfg.n_lat, cfg.n_ctx, cfg.ctx_blk

    def body(gl_ref, ul_ref, gc_ref, uc_ref, cw_ref, cb_ref, wa_ref, ba_ref, wx_ref, bx_ref, lam_ref,
             dzl_ref, dzc_ref, hl, hc, dgl_ref, dul_ref, dgc_ref, duc_ref, dwa_ref, dwx_ref, vec_ref,
             al, bl, ac, bc):
        e = pl.program_id(1)

        @pl.when(e == 0)
        def _():
            dwa_ref[...] = jnp.zeros_like(dwa_ref)
            dwx_ref[...] = jnp.zeros_like(dwx_ref)
            vec_ref[...] = jnp.zeros_like(vec_ref)

        row_l = lax.broadcasted_iota(jnp.int32, (n_lat, 1), 0)
        row_c = lax.broadcasted_iota(jnp.int32, (n_ctx, 1), 0)
        u_l, u_c = ul_ref[...], uc_ref[...]
        uc_l = _conv(u_l, cw_ref, cb_ref, row_l)
        uc_c = _conv(u_c, cw_ref, cb_ref, row_c)
        gel_l, t_l = _gelu(gl_ref[...])
        gel_c, t_c = _gelu(gc_ref[...])
        dz_l, dz_c = dzl_ref[...], dzc_ref[...]
        dgl_ref[...] = (dz_l * (hl[0] + hl[1]) * _gelu_grad(gl_ref[...], t_l)).astype(BF16)
        dgc_ref[...] = (dz_c * (hc[0] + hc[1]) * _gelu_grad(gc_ref[...], t_c)).astype(BF16)
        dy_l, dy_c = dz_l * gel_l, dz_c * gel_c
        duc_l = jnp.zeros((n_lat, 128), F32)
        duc_c = jnp.zeros((n_ctx, 128), F32)
        for d in range(2):
            r_l, gi_l, sp, dsp, a_l, sq_l = _lru_gates(uc_l, d, wa_ref, ba_ref, wx_ref, bx_ref, lam_ref)
            r_c, gi_c, _, _, a_c, sq_c = _lru_gates(uc_c, d, wa_ref, ba_ref, wx_ref, bx_ref, lam_ref)
            if d == 0:
                an_l = _shift_up(a_l, 1, row_l)
                an_c = jnp.where(row_c < n_ctx - 1, pltpu.roll(a_c, n_ctx - 1, 0), a_l[0:1, :])
            else:
                an_l = _shift_down(a_l, 1, row_l)
                an_c = jnp.where(row_c >= 1, pltpu.roll(a_c, 1, 0), a_l[n_lat - 1:n_lat, :])
            _chain_scan([(an_l, dy_l, al, bl, n_lat), (an_c, dy_c, ac, bc, n_ctx)], reverse=(d == 0))
            dsp_sum = jnp.zeros((1, 128), F32)
            for (dh, h, r, gi, a, sq, uc, seg) in ((bl[...], hl[d], r_l, gi_l, a_l, sq_l, uc_l, "l"),
                                                  (bc[...], hc[d], r_c, gi_c, a_c, sq_c, uc_c, "c")):
                b0 = sq * (gi * uc)
                t1 = dh * sq
                dla = dh * (h - b0) - (dh * gi * uc) * (a * a) / sq
                dzr = (dla * ((-LRU_C) * sp)) * r * (1.0 - r)
                dzi = (t1 * uc) * gi * (1.0 - gi)
                dsp_sum = dsp_sum + jnp.sum(dla * ((-LRU_C) * r), axis=0, keepdims=True)
                dwa_ref[d] += _dot_tn(uc, dzr)
                dwx_ref[d] += _dot_tn(uc, dzi)
                vec_ref[d:d + 1, :] += jnp.sum(dzr, axis=0, keepdims=True)
                vec_ref[2 + d:3 + d, :] += jnp.sum(dzi, axis=0, keepdims=True)
                duc = t1 * gi + _dot_nt(dzr, wa_ref[d]) + _dot_nt(dzi, wx_ref[d])
                if seg == "l":
                    duc_l = duc_l + duc
                else:
                    duc_c = duc_c + duc
            vec_ref[4 + d:5 + d, :] += dsp_sum * dsp
        for duc, u, row, du_ref in ((duc_l, u_l, row_l, dul_ref), (duc_c, u_c, row_c, duc_ref)):
            du_ref[...] = (_shift_up(duc, 1, row) * cw_ref[0:1, :] + duc * cw_ref[1:2, :]
                           + _shift_down(duc, 1, row) * cw_ref[2:3, :]
                           + _shift_down(duc, 2, row) * cw_ref[3:4, :]).astype(BF16)
            vec_ref[6:7, :] += jnp.sum(duc * _shift_down(u, 1, row), axis=0, keepdims=True)
            vec_ref[7:8, :] += jnp.sum(duc * u, axis=0, keepdims=True)
            vec_ref[8:9, :] += jnp.sum(duc * _shift_up(u, 1, row), axis=0, keepdims=True)
            vec_ref[9:10, :] += jnp.sum(duc * _shift_up(u, 2, row), axis=0, keepdims=True)
            vec_ref[10:11, :] += jnp.sum(duc, axis=0, keepdims=True)

    lat = pl.BlockSpec((n_lat, 128), lambda hb, e: (e, hb))
    ctx = pl.BlockSpec((n_ctx, 128), lambda hb, e: (e, hb))
    wspec = pl.BlockSpec((2, None, 128, 128), lambda hb, e: (0, hb, 0, 0))
    sd = jax.ShapeDtypeStruct
    return pl.pallas_call(
        body, grid=(8, 2), name=name,
        in_specs=_lru_specs(cfg) + [pl.BlockSpec((n_lat, 128), lambda hb, e: (e, hb)),
                                    pl.BlockSpec((n_ctx, 128), lambda hb, e: (cb + e, hb)),
                                    pl.BlockSpec((2, n_lat, 128), lambda hb, e: (0, e, hb)),
                                    pl.BlockSpec((2, n_ctx, 128), lambda hb, e: (0, e, hb))],
        out_specs=[lat, lat, ctx, ctx, wspec, wspec, pl.BlockSpec((None, 16, 128), lambda hb, e: (hb, 0, 0))],
        out_shape=[sd((cfg.t_lat, D), BF16), sd((cfg.t_lat, D), BF16), sd((cfg.t_ctx, D), BF16), sd((cfg.t_ctx, D), BF16),
                   sd((2, 8, 128, 128), F32), sd((2, 8, 128, 128), F32), sd((8, 16, 128), F32)],
        scratch_shapes=[pltpu.VMEM((n_lat, 128), F32)] * 2 + [pltpu.VMEM((n_ctx, 128), F32)] * 2,
        compiler_params=_params(("parallel", "arbitrary")))(p, p, p, p, *consts, dz, dz, h_lat, h_ctx)


def _position():
    x, y, c = lax.axis_index("x"), lax.axis_index("y"), lax.axis_index("c")
    return x, y, c, 4 * x + 2 * y + c


def _peer(x, y, c, k):
    px = 1 - x if k & 4 else x
    py = 1 - y if k & 2 else y
    pc = 1 - c if k & 1 else c
    return (px, py, pc), 4 * px + 2 * py + pc


def _all_gather(v, name, in_vmem):
    def body(v_ref, o_ref, send_sems, recv_sems, local_sem):
        x, y, c, me = _position()
        mine = pltpu.make_async_copy(v_ref, o_ref.at[me], local_sem)
        mine.start()
        sends = []
        for k in range(1, N_DEV):
            peer, _ = _peer(x, y, c, k)
            cp = pltpu.make_async_remote_copy(src_ref=v_ref, dst_ref=o_ref.at[me], send_sem=send_sems.at[k - 1],
                                              recv_sem=recv_sems.at[k - 1], device_id=peer, device_id_type=MESH)
            cp.start()
            sends.append(cp)
        for k in range(1, N_DEV):
            peer, peer_lin = _peer(x, y, c, k)
            pltpu.make_async_remote_copy(src_ref=v_ref, dst_ref=o_ref.at[peer_lin], send_sem=send_sems.at[k - 1],
                                         recv_sem=recv_sems.at[k - 1], device_id=peer, device_id_type=MESH).wait_recv()
        for cp in sends:
            cp.wait_send()
        mine.wait()

    space = pltpu.VMEM if in_vmem else pl.ANY
    return pl.pallas_call(
        body, name=name,
        in_specs=[pl.BlockSpec(memory_space=space)], out_specs=pl.BlockSpec(memory_space=space),
        out_shape=jax.ShapeDtypeStruct((N_DEV,) + v.shape, v.dtype),
        scratch_shapes=[pltpu.SemaphoreType.DMA((N_DEV - 1,)), pltpu.SemaphoreType.DMA((N_DEV - 1,)),
                        pltpu.SemaphoreType.DMA],
        compiler_params=pltpu.CompilerParams(vmem_limit_bytes=VMEM_LIMIT))(v)


_HBM = pl.BlockSpec(memory_space=pltpu.HBM)
_SEM = pl.BlockSpec(memory_space=pltpu.SEMAPHORE)
_EFFECT = pltpu.SideEffectType.DATAFLOW_SIDE_EFFECTING


def _push_start(src, land, block_of, name):
    def body(src_ref, land_ref, send_sem, recv_sem, src_thru, land_thru, token):
        x, y, c, me = _position()
        for k in range(1, N_DEV):
            peer, peer_lin = _peer(x, y, c, k)
            mine, there = block_of(src_ref, land_ref, me, peer_lin)
            pltpu.make_async_remote_copy(src_ref=mine, dst_ref=there, send_sem=send_sem, recv_sem=recv_sem,
                                         device_id=peer, device_id_type=MESH).start()
        mine, here = block_of(src_ref, land_ref, me, me)
        pltpu.make_async_copy(mine, here, recv_sem).start()
        token[...] = jnp.zeros_like(token)

    return pl.pallas_call(
        body, name=name,
        out_shape=(pltpu.SemaphoreType.DMA(()), pltpu.SemaphoreType.DMA(()), pltpu.HBM(src.shape, src.dtype),
                   pltpu.HBM(land.shape, land.dtype), jax.ShapeDtypeStruct((8, 128), F32)),
        in_specs=(_HBM, _HBM), out_specs=(_SEM, _SEM, _HBM, _HBM, pl.BlockSpec(memory_space=pltpu.VMEM)),
        input_output_aliases={0: 2, 1: 3},
        compiler_params=pltpu.CompilerParams(has_side_effects=_EFFECT),
    )(pltpu.with_memory_space_constraint(src, pltpu.HBM), pltpu.with_memory_space_constraint(land, pltpu.HBM))


def _push_wait(handle, seven_of, after, name):
    send_sem, recv_sem, src_thru, land_thru, _ = handle

    def body(src_ref, land_ref, send_sem, recv_sem, after_ref, src_dead, got_ref):
        x, y, c, _ = _position()
        seven = seven_of(land_ref)
        pltpu.make_async_remote_copy(src_ref=seven, dst_ref=seven, send_sem=send_sem, recv_sem=recv_sem,
                                     device_id=(x, y, 1 - c), device_id_type=MESH).wait_send()
        pltpu.make_async_remote_copy(src_ref=land_ref, dst_ref=land_ref, send_sem=send_sem, recv_sem=recv_sem,
                                     device_id=(x, y, 1 - c), device_id_type=MESH).wait_recv()

    return pl.pallas_call(
        body, name=name,
        out_shape=(pltpu.HBM(src_thru.shape, src_thru.dtype), pltpu.HBM(land_thru.shape, land_thru.dtype)),
        in_specs=(_HBM, _HBM, _SEM, _SEM, pl.BlockSpec(memory_space=pl.ANY)), out_specs=(_HBM, _HBM),
        input_output_aliases={0: 0, 1: 1},
        compiler_params=pltpu.CompilerParams(has_side_effects=_EFFECT),
    )(src_thru, land_thru, send_sem, recv_sem, after)[1]


def _gather_start(src, name):
    g, r, C = src.shape
    land = lax.empty((g, N_DEV * r, C), src.dtype)
    return _push_start(src, land, lambda s, z, i, p: (s, z.at[:, pl.ds(i * r, r), :]), name)


def _gather_wait(handle, after, name):
    r = handle[2].shape[1]
    return _push_wait(handle, lambda z: z.at[:, pl.ds(0, (N_DEV - 1) * r), :], after, name)


def _exchange_start(grad, name):
    g, rows, C = grad.shape
    r = rows // N_DEV
    land = lax.empty((N_DEV, g, r, C), grad.dtype)
    return _push_start(grad, land, lambda s, z, i, p: (s.at[:, pl.ds(p * r, r), :], z.at[i]), name)


def _exchange_wait(handle, after, name):
    return _push_wait(handle, lambda z: z.at[pl.ds(0, N_DEV - 1)], after, name)


def _sum_blocks(v, name):
    k, rows, cols = v.shape
    tr = rows
    for cand in (rows, 512, 352, 256, 176, 128, 64, 32, 16):
        if rows % cand == 0 and k * cand * cols * v.dtype.itemsize <= 6 * 1024 * 1024:
            tr = cand
            break

    def body(v_ref, o_ref):
        acc = v_ref[0].astype(F32)
        for s in range(1, k):
            acc = acc + v_ref[s].astype(F32)
        o_ref[...] = acc

    return pl.pallas_call(
        body, grid=(rows // tr,), name=name,
        in_specs=[pl.BlockSpec((k, tr, cols), lambda i: (0, i, 0))],
        out_specs=pl.BlockSpec((tr, cols), lambda i: (i, 0)),
        out_shape=jax.ShapeDtypeStruct((rows, cols), F32),
        compiler_params=_params(("parallel",)))(v)


def _adam_math(w, g, m, v):
    m2 = B1 * m + (1.0 - B1) * g
    v2 = B2 * v + (1.0 - B2) * (g * g)
    m_hat = m2 / (1.0 - B1 ** STEP)
    v_hat = v2 / (1.0 - B2 ** STEP)
    return -LR * (m_hat / (jnp.sqrt(v_hat) + EPS) + WD * w), m2, v2


def _adamw(w, g, m, v, name):
    shp = w.shape
    rows, cols = (shp[-2], shp[-1]) if len(shp) >= 2 else (1, shp[-1])
    lead = math.prod(shp[:-2]) if len(shp) > 2 else 1
    fits = [t for t in range(8, rows + 1, 8) if rows % t == 0 and t * cols * 4 <= 2 * 1024 * 1024]
    tr = max(fits) if fits else rows

    def body(w_ref, g_ref, m_ref, v_ref, d_ref, m2_ref, v2_ref):
        d_ref[...], m2_ref[...], v2_ref[...] = _adam_math(w_ref[...], g_ref[...], m_ref[...], v_ref[...])

    blk = pl.BlockSpec((None, tr, cols), lambda b, i: (b, i, 0))
    outs = pl.pallas_call(
        body, grid=(lead, rows // tr), name=name, in_specs=[blk] * 4, out_specs=[blk] * 3,
        out_shape=[jax.ShapeDtypeStruct((lead, rows, cols), F32)] * 3,
        compiler_params=_params(("parallel", "parallel")))(*[a.reshape(lead, rows, cols) for a in (w, g, m, v)])
    return [o.reshape(shp) for o in outs]


def _as2d(a):
    n = a.size
    if n % 1024 == 0:
        return a.reshape(n // 1024, 1024)
    if n % 128 == 0:
        return a.reshape(n // 128, 128)
    return a.reshape(1, n)


def _blocks_to_cols(a):
    b = jnp.moveaxis(a, 0, -2)
    return b.reshape(b.shape[:-2] + (b.shape[-2] * b.shape[-1],))


def _pack_rows(parts):
    padded, offs, r = [], [], 0
    for p in parts:
        pad = (-p.shape[0]) % 8
        padded.append(jnp.pad(p, ((0, pad), (0, 0))) if pad else p)
        offs.append(r)
        r += p.shape[0] + pad
    return jnp.concatenate(padded, axis=0), offs


def _silu(x):
    return x * jax.nn.sigmoid(x)


def kernel(x, c, ctx, c_ctx, w_mod, b_mod, ln_g, ln_b, ffn_w_gate, ffn_w_up, ffn_w_down, mix_ab_w_in, attn_sink, pool_w, pool_scale, mix_ab_w_out, lru_w_in, lru_conv_w, lru_conv_b, lru_wa, lru_ba, lru_wx, lru_bx, lru_lambda, lru_w_out, loss_target, m_c_ctx, m_w_mod, m_b_mod, m_ln_g, m_ln_b, m_ffn_w_gate, m_ffn_w_up, m_ffn_w_down, m_mix_ab_w_in, m_attn_sink, m_pool_w, m_pool_scale, m_mix_ab_w_out, m_lru_w_in, m_lru_conv_w, m_lru_conv_b, m_lru_wa, m_lru_ba, m_lru_wx, m_lru_bx, m_lru_lambda, m_lru_w_out, v_c_ctx, v_w_mod, v_b_mod, v_ln_g, v_ln_b, v_ffn_w_gate, v_ffn_w_up, v_ffn_w_down, v_mix_ab_w_in, v_attn_sink, v_pool_w, v_pool_scale, v_mix_ab_w_out, v_lru_w_in, v_lru_conv_w, v_lru_conv_b, v_lru_wa, v_lru_ba, v_lru_wx, v_lru_bx, v_lru_lambda, v_lru_w_out):
    weights = dict(c_ctx=c_ctx, w_mod=w_mod, b_mod=b_mod, ln_g=ln_g, ln_b=ln_b, ffn_w_gate=ffn_w_gate,
                   ffn_w_up=ffn_w_up, ffn_w_down=ffn_w_down, mix_ab_w_in=mix_ab_w_in, attn_sink=attn_sink,
                   pool_w=pool_w, pool_scale=pool_scale, mix_ab_w_out=mix_ab_w_out, lru_w_in=lru_w_in,
                   lru_conv_w=lru_conv_w, lru_conv_b=lru_conv_b, lru_wa=lru_wa, lru_ba=lru_ba, lru_wx=lru_wx,
                   lru_bx=lru_bx, lru_lambda=lru_lambda, lru_w_out=lru_w_out)
    mom_m = dict(c_ctx=m_c_ctx, w_mod=m_w_mod, b_mod=m_b_mod, ln_g=m_ln_g, ln_b=m_ln_b, ffn_w_gate=m_ffn_w_gate,
                 ffn_w_up=m_ffn_w_up, ffn_w_down=m_ffn_w_down, mix_ab_w_in=m_mix_ab_w_in, attn_sink=m_attn_sink,
                 pool_w=m_pool_w, pool_scale=m_pool_scale, mix_ab_w_out=m_mix_ab_w_out, lru_w_in=m_lru_w_in,
                 lru_conv_w=m_lru_conv_w, lru_conv_b=m_lru_conv_b, lru_wa=m_lru_wa, lru_ba=m_lru_ba, lru_wx=m_lru_wx,
                 lru_bx=m_lru_bx, lru_lambda=m_lru_lambda, lru_w_out=m_lru_w_out)
    mom_v = dict(c_ctx=v_c_ctx, w_mod=v_w_mod, b_mod=v_b_mod, ln_g=v_ln_g, ln_b=v_ln_b, ffn_w_gate=v_ffn_w_gate,
                 ffn_w_up=v_ffn_w_up, ffn_w_down=v_ffn_w_down, mix_ab_w_in=v_mix_ab_w_in, attn_sink=v_attn_sink,
                 pool_w=v_pool_w, pool_scale=v_pool_scale, mix_ab_w_out=v_mix_ab_w_out, lru_w_in=v_lru_w_in,
                 lru_conv_w=v_lru_conv_w, lru_conv_b=v_lru_conv_b, lru_wa=v_lru_wa, lru_ba=v_lru_ba, lru_wx=v_lru_wx,
                 lru_bx=v_lru_bx, lru_lambda=v_lru_lambda, lru_w_out=v_lru_w_out)
    names = list(weights)

    n_lat, n_ctx = x.shape[1], ctx.shape[1]
    cfg = _Cfg(n_lat, n_ctx)
    _, _, _, me = _position()
    mcols = w_mod.shape[2]

    def t_bf16(w):
        return jnp.swapaxes(w, -1, -2).astype(BF16)

    def ffn_src(l, i):
        return jnp.stack([t_bf16(ffn_w_gate[l, i]), t_bf16(ffn_w_up[l, i]), ffn_w_down[l, i].astype(BF16)])

    pending = {}

    def start_gathers(items, tok):
        for key, make_src in items:
            pending[key] = _gather_start(make_src() + tok.astype(BF16), "gather_start_" + key)
            tok = pending[key][4][0, 0]
        return tok

    def weights_now(key, after):
        return _gather_wait(pending[key], after, "gather_wait_" + key)

    tok = start_gathers([("ffn00", lambda: ffn_src(0, 0))], jnp.zeros((), F32))

    small_names = ["ln_g", "ln_b", "lru_conv_w", "lru_conv_b", "lru_ba", "lru_bx", "lru_lambda"]
    small, small_off = _pack_rows([(c + tok).reshape(-1, 128)] + [weights[n].reshape(-1, 128) for n in small_names])
    small_all = _all_gather(small, "gather_small", True)

    def small_full(idx, shp):
        rows = math.prod(shp) // 128
        return _blocks_to_cols(small_all[:, small_off[idx]:small_off[idx] + rows, :].reshape((N_DEV,) + shp))

    c_all = small_all[:, :2 * D // 128, :].reshape(2 * N_DEV, D)
    ln_g_f, ln_b_f = small_full(1, ln_g.shape), small_full(2, ln_b.shape)
    lru_consts = (small_full(3, lru_conv_w.shape)[0], small_full(4, lru_conv_b.shape), lru_wa[0],
                  small_full(5, lru_ba.shape)[0], lru_wx[0], small_full(6, lru_bx.shape)[0],
                  small_full(7, lru_lambda.shape)[0])

    s_rows = jnp.zeros((32, D), F32).at[:16].set(_silu(c_all)).at[16].set(_silu(c_ctx)).astype(BF16)
    mod_mine = jnp.stack([_matmul(s_rows, w_mod[l], "nn", F32, "mod_fwd", bn_cap=1280) for l in range(2)])
    mod_all = _all_gather(mod_mine.reshape(64, mcols), "gather_mod", True).reshape(N_DEV, 2, 32, mcols)
    tok = start_gathers([("ab_in", lambda: t_bf16(mix_ab_w_in)), ("ab_out", lambda: mix_ab_w_out.astype(BF16)),
                         ("ffn01", lambda: ffn_src(0, 1)), ("ffn10", lambda: ffn_src(1, 0)),
                         ("lru_in", lambda: t_bf16(lru_w_in)), ("lru_out", lambda: lru_w_out.astype(BF16)),
                         ("ffn11", lambda: ffn_src(1, 1))], mod_all[0, 0, 0, 0] * 0.0)
    mod_full = _blocks_to_cols(mod_all) + (b_mod[:, None, :] + tok)
    ex0 = 2 * me
    mods = []
    for l in range(2):
        rows = jnp.stack([lax.dynamic_index_in_dim(mod_full[l], ex0, 0, False),
                          lax.dynamic_index_in_dim(mod_full[l], ex0 + 1, 0, False), mod_full[l, 16]])
        mods.append(rows.reshape(3, N_MOD, D))

    h0 = jnp.concatenate([x.reshape(cfg.t_lat, D), ctx.reshape(cfg.t_ctx, D)], axis=0)
    cos, sin = _rope_tables(n_lat)
    sink_rows = jnp.broadcast_to(attn_sink[0][:, None], (8, 128)).astype(F32)

    saved = []
    wf = [[None, None], [None, None]]
    h = h0
    xin = _modulate(cfg, h0, mods[0], 0, 1, "modulate_in")
    for l in range(2):
        st = {"h_in": h, "xin1": xin}
        wf[l][0] = weights_now("ffn%d0" % l, xin)
        g1, u1, y1 = _ffn_fwd(xin, wf[l][0], "ffn_fwd")
        h1, xhat1, rstd1, xin2 = _ln_fwd(cfg, h, y1, mods[l], 2, 0.5, ln_g_f[l, 0][None], ln_b_f[l, 0][None],
                                          mods[l], (3, 4), "ln_fwd_a")
        st.update(g1=g1, u1=u1, y1=y1, h1=h1, xhat1=xhat1, rstd1=rstd1, xin2=xin2)
        if l == 0:
            w_ab_in_t = weights_now("ab_in", xin2)[0]
            p = _matmul(xin2, w_ab_in_t, "nt", F32, "mix_ab_in")
            att_l, att_c = _attn_fwd(cfg, p, cos, sin, sink_rows, "attn_fwd")
            pool_l = _pool_fwd(p, pool_w[0], pool_scale, n_lat, 0, 2, "pool_fwd_lat")
            pool_c = _pool_fwd(p, pool_w[0], pool_scale, n_ctx, cfg.ctx_blk, 2, "pool_fwd_ctx")
            cat = jnp.concatenate([jnp.concatenate([att_l, pool_l], axis=1),
                                   jnp.concatenate([att_c, pool_c], axis=1)], axis=0)
            w_ab_out = weights_now("ab_out", cat)[0]
            y2 = _matmul(cat, w_ab_out, "nn", BF16, "mix_ab_out")
        else:
            w_lru_in_t = weights_now("lru_in", xin2)[0]
            p = _matmul(xin2, w_lru_in_t, "nt", F32, "lru_in")
            z_l, z_c, st["h_lat"], st["h_ctx"] = _lru_fwd(cfg, p, lru_consts, "lru_fwd")
            cat = jnp.concatenate([z_l, z_c], axis=0)
            w_lru_out = weights_now("lru_out", cat)[0]
            y2 = _matmul(cat, w_lru_out, "nn", BF16, "lru_out")
        h2, xhat2, rstd2, xin3 = _ln_fwd(cfg, h1, y2, mods[l], 5, 1.0, ln_g_f[l, 1][None], ln_b_f[l, 1][None],
                                          mods[l], (6, 7), "ln_fwd_b")
        wf[l][1] = weights_now("ffn%d1" % l, xin3)
        g3, u3, y3 = _ffn_fwd(xin3, wf[l][1], "ffn_fwd")
        if l == 0:
            h3, xhat3, rstd3, xin = _ln_fwd(cfg, h2, y3, mods[l], 8, 0.5, ln_g_f[l, 2][None], ln_b_f[l, 2][None],
                                            mods[1], (0, 1), "ln_fwd_a")
        else:
            h3, xhat3, rstd3 = _ln_fwd(cfg, h2, y3, mods[l], 8, 0.5, ln_g_f[l, 2][None], ln_b_f[l, 2][None],
                                       None, None, "ln_fwd_last")
        st.update(p=p, cat=cat, y2=y2, h2=h2, xhat2=xhat2, rstd2=rstd2, xin3=xin3, g3=g3, u3=u3, y3=y3,
                  xhat3=xhat3, rstd3=rstd3)
        saved.append(st)
        h = h3

    dy, loss_tile = _loss(cfg, h, loss_target.reshape(cfg.t_lat, D), "loss")
    loss = lax.psum(loss_tile[0, 0], ("x", "y", "c"))

    grads = {}
    dmod = [None, None]
    recv_ffn = [[None, None], [None, None]]
    dln_g = [[None] * 3, [None] * 3]
    dln_b = [[None] * 3, [None] * 3]

    def ffn_weight_grads(tag, xin_b, dg, du, a_act, dys):
        parts = [_matmul(dg, xin_b, "tn", BF16, "ffn_dw", bm_cap=1408, bk_cap=2304)[None],
                 _matmul(du, xin_b, "tn", BF16, "ffn_dw", bm_cap=1408, bk_cap=2304)[None],
                 _matmul(a_act, dys, "tn", BF16, "ffn_dw", bm_cap=1408, bk_cap=2304)[None]]
        return [_exchange_start(part,"exchange_start_ffn%s_%d" % (tag, k)) for k, part in enumerate(parts)]

    def pin(handles):
        total = handles[0][4][0, 0]
        for hd in handles[1:]:
            total = total + hd[4][0, 0]
        return total

    up = (dy,)
    dmod_next = None
    last_sent = None
    for l in (1, 0):
        st = saved[l]
        dm = [None] * N_MOD

        def put_stats(stats, gate_idx, nxt):
            dm[gate_idx] = stats[:, 2, :]
            if nxt is not None:
                nxt[0][nxt[1]] = stats[:, 4, :]
                nxt[0][nxt[1] + 1] = stats[:, 3, :]

        lng3 = ln_g_f[l, 2][None] if last_sent is None else ln_g_f[l, 2][None] + pin(last_sent)
        dres, dys, stats = _ln_bwd(cfg, up, st["xhat3"], st["rstd3"], st["y3"], mods[l], 8, 0.5,
                                   lng3, "ln_bwd_fused" if len(up) > 1 else "ln_bwd_last")
        put_stats(stats, 8, None if len(up) == 1 else (dmod_next, 0))
        dln_g[l][2], dln_b[l][2] = stats[:, 0, :].sum(0), stats[:, 1, :].sum(0)
        dg, du, a_act, dxin = _ffn_bwd(dys, st["g3"], st["u3"], wf[l][1], "ffn_bwd")
        recv_ffn[l][1] = ffn_weight_grads("%d1" % l, st["xin3"], dg, du, a_act, dys)
        dres, dys, stats = _ln_bwd(cfg, (dres, dxin, st["h2"], mods[l], 7), st["xhat2"], st["rstd2"], st["y2"],
                                   mods[l], 5, 1.0, ln_g_f[l, 1][None] + pin(recv_ffn[l][1]), "ln_bwd_fused")
        put_stats(stats, 5, (dm, 6))
        dln_g[l][1], dln_b[l][1] = stats[:, 0, :].sum(0), stats[:, 1, :].sum(0)
        if l == 0:
            dw_out = _matmul(st["cat"], dys, "tn", BF16, "mix_ab_dw_out")
            dcat = _matmul(dys, w_ab_out, "nt", F32, "mix_ab_dcat")
            dq, dk, dv, dqc, dkc, dvc, dsink = _attn_bwd(cfg, st["p"], dcat, cos, sin, sink_rows, "attn_bwd")
            du_l, dpw_l, dps_l = _pool_bwd(st["p"], pool_w[0], pool_scale, dcat, n_lat, 0, 2, "pool_bwd_lat")
            du_c, dpw_c, dps_c = _pool_bwd(st["p"], pool_w[0], pool_scale, dcat, n_ctx, cfg.ctx_blk, 2, "pool_bwd_ctx")
            dp = jnp.concatenate([jnp.concatenate([dq, dk, dv, du_l], axis=1),
                                  jnp.concatenate([dqc, dkc, dvc, du_c], axis=1)], axis=0)
            dw_in_t = _matmul(dp, st["xin2"], "tn", BF16, "mix_ab_dw_in", bm_cap=1280)
            dxin = _matmul(dp, w_ab_in_t, "nn", F32, "mix_ab_dx")
            recv_mix = [_exchange_start(part,"exchange_start_mix_ab_%d" % k)
                        for k, part in enumerate((dw_in_t[None], dw_out[None], _as2d(dpw_l + dpw_c)[None]))]
            grads["attn_sink"] = (dsink[0, :, 0] + dsink[1, :, 0])[None, :]
            grads["pool_scale"] = dps_l + dps_c
        else:
            dw_out = _matmul(st["cat"], dys, "tn", BF16, "lru_dw_out")
            dz = _matmul(dys, w_lru_out, "nt", F32, "lru_dz")
            dgl, dul, dgc, duc, dwa, dwx, vec = _lru_bwd(cfg, st["p"], dz, st["h_lat"], st["h_ctx"], lru_consts, "lru_bwd")
            dp = jnp.concatenate([jnp.concatenate([dgl, dul], axis=1), jnp.concatenate([dgc, duc], axis=1)], axis=0)
            dw_in_t = _matmul(dp, st["xin2"], "tn", BF16, "lru_dw_in", bm_cap=1024)
            dxin = _matmul(dp, w_lru_in_t, "nn", F32, "lru_dx")
            recv_mix = [_exchange_start(part,"exchange_start_lru_%d" % k)
                        for k, part in enumerate((dw_in_t[None], dw_out[None], _as2d(dwa)[None], _as2d(dwx)[None]))]
            vec_t = jnp.moveaxis(vec, 0, 1).reshape(16, D)
            grads["lru_ba"], grads["lru_bx"] = vec_t[0:2], vec_t[2:4]
            grads["lru_lambda"], grads["lru_conv_w"], grads["lru_conv_b"] = vec_t[4:6], vec_t[6:10], vec_t[10:11]
        if l == 0:
            recv_ab = recv_mix
        else:
            recv_lru = recv_mix
        dres, dys, stats = _ln_bwd(cfg, (dres, dxin, st["h1"], mods[l], 4), st["xhat1"], st["rstd1"], st["y1"],
                                   mods[l], 2, 0.5, ln_g_f[l, 0][None] + pin(recv_mix), "ln_bwd_fused")
        put_stats(stats, 2, (dm, 3))
        dln_g[l][0], dln_b[l][0] = stats[:, 0, :].sum(0), stats[:, 1, :].sum(0)
        dg, du, a_act, dxin = _ffn_bwd(dys, st["g1"], st["u1"], wf[l][0], "ffn_bwd")
        recv_ffn[l][0] = ffn_weight_grads("%d0" % l, st["xin1"], dg, du, a_act, dys)
        last_sent = recv_ffn[l][0]
        dmod[l] = dm
        dmod_next = dm
        up = (dres, dxin, st["h_in"], mods[l], 1)
    dh0, stats = _modulate_bwd(cfg, up[0], up[1], h0, mods[0] + pin(last_sent), 1, "modulate_bwd")
    dmod[0][0], dmod[0][1] = stats[:, 4, :], stats[:, 3, :]
    grad_x = dh0[:cfg.t_lat].reshape(x.shape)

    def arrived(handle, name):
        return _exchange_wait(handle, dh0, name)

    recv_ffn = [[[arrived(hd, "exchange_wait_ffn%d%d_%d" % (l, i, k)) for k, hd in enumerate(recv_ffn[l][i])]
                 for i in range(2)] for l in range(2)]
    recv_ab = [arrived(hd, "exchange_wait_mix_ab_%d" % k) for k, hd in enumerate(recv_ab)]
    recv_lru = [arrived(hd, "exchange_wait_lru_%d" % k) for k, hd in enumerate(recv_lru)]

    dmod_mine = jnp.stack([jnp.stack(dmod[l], axis=1).reshape(3, N_MOD * D) for l in range(2)])
    n_dm = 6 * N_MOD * D // 128
    dmod_all = _all_gather(dmod_mine.reshape(n_dm, 128), "gather_dmod", True)
    dmod_sum = _sum_blocks(dmod_all, "sum_dmod").reshape(2, 3, N_MOD * D)
    dmod_all = dmod_all.reshape(N_DEV, 2, 3, N_MOD * D)
    grads["b_mod"] = dmod_sum[:, 0] + dmod_sum[:, 1] + dmod_sum[:, 2]
    dmod_ex = jnp.moveaxis(dmod_all[:, :, 0:2, :], 1, 0).reshape(2, 2 * N_DEV, N_MOD * D)
    dm_rows = jnp.zeros((2, 32, N_MOD * D), F32).at[:, :16].set(dmod_ex).at[:, 16].set(dmod_sum[:, 2])
    dm_cols = lax.dynamic_slice_in_dim(dm_rows, me * mcols, mcols, axis=2).astype(BF16)
    grads["w_mod"] = jnp.stack([_matmul(s_rows, dm_cols[l], "tn", F32, "mod_dw", bn_cap=1280) for l in range(2)])
    ds_part = None
    for l in range(2):
        part = _matmul(dm_cols[l, 16:32], w_mod[l], "nt", F32, "mod_ds", bk_cap=1280)[0]
        ds_part = part if ds_part is None else ds_part + part

    def shard_sum(recv, name):
        return _sum_blocks(recv.reshape(N_DEV, recv.shape[2], recv.shape[3]), name)

    gate_g = [[None, None], [None, None]]
    up_g = [[None, None], [None, None]]
    down_g = [[None, None], [None, None]]
    for l in range(2):
        for i in range(2):
            gt, ut, dn = [shard_sum(r, "sum_ffn") for r in recv_ffn[l][i]]
            gate_g[l][i], up_g[l][i], down_g[l][i] = gt.T, ut.T, dn
    grads["ffn_w_gate"] = jnp.stack([jnp.stack(gate_g[l]) for l in range(2)])
    grads["ffn_w_up"] = jnp.stack([jnp.stack(up_g[l]) for l in range(2)])
    grads["ffn_w_down"] = jnp.stack([jnp.stack(down_g[l]) for l in range(2)])
    grads["mix_ab_w_in"] = shard_sum(recv_ab[0], "sum_mix_in").T[None]
    grads["mix_ab_w_out"] = shard_sum(recv_ab[1], "sum_mix_out")[None]
    grads["lru_w_in"] = shard_sum(recv_lru[0], "sum_lru_in").T[None]
    grads["lru_w_out"] = shard_sum(recv_lru[1], "sum_lru_out")[None]
    rep_parts = [shard_sum(recv_lru[2], "sum_rep"), shard_sum(recv_lru[3], "sum_rep"), shard_sum(recv_ab[2], "sum_rep")]
    rep_names = ["lru_wa", "lru_wx", "pool_w"]

    dln_g_f = jnp.stack([jnp.stack(dln_g[l]) for l in range(2)])
    dln_b_f = jnp.stack([jnp.stack(dln_b[l]) for l in range(2)])
    sink_pad = jnp.zeros((1, 128), F32).at[0, :8].set(grads["attn_sink"][0])
    part_list = [p_.reshape(-1, 128) for p_ in rep_parts] + [
        dln_g_f.reshape(-1, 128), dln_b_f.reshape(-1, 128), grads["lru_conv_w"].reshape(-1, 128),
        grads["lru_conv_b"].reshape(-1, 128), grads["lru_ba"].reshape(-1, 128), grads["lru_bx"].reshape(-1, 128),
        grads["lru_lambda"].reshape(-1, 128), ds_part.reshape(-1, 128), sink_pad, grads["pool_scale"].reshape(-1, 128)]
    parts, part_off = _pack_rows(part_list)
    parts_all = _all_gather(parts, "gather_partials", True)
    parts_sum = _sum_blocks(parts_all, "sum_partials")

    for i, n in enumerate(rep_names):
        rows = part_list[i].shape[0]
        grads[n] = parts_all[:, part_off[i]:part_off[i] + rows, :].reshape(weights[n].shape)

    def take(idx):
        return parts_sum[part_off[idx]:part_off[idx] + part_list[idx].shape[0]]

    def my_cols(full, shp):
        w = shp[-1]
        return lax.dynamic_slice_in_dim(full, me * w, w, axis=full.ndim - 1)

    grads["ln_g"] = my_cols(take(3).reshape(2, 3, D), ln_g.shape)
    grads["ln_b"] = my_cols(take(4).reshape(2, 3, D), ln_b.shape)
    grads["lru_conv_w"] = my_cols(take(5).reshape(1, 4, D), lru_conv_w.shape)
    grads["lru_conv_b"] = my_cols(take(6).reshape(1, D), lru_conv_b.shape)
    grads["lru_ba"] = my_cols(take(7).reshape(1, 2, D), lru_ba.shape)
    grads["lru_bx"] = my_cols(take(8).reshape(1, 2, D), lru_bx.shape)
    grads["lru_lambda"] = my_cols(take(9).reshape(1, 2, D), lru_lambda.shape)
    sg = jax.nn.sigmoid(c_ctx)
    grads["c_ctx"] = take(10).reshape(D) * (sg * (1.0 + c_ctx * (1.0 - sg)))
    grads["attn_sink"] = take(11)[:, :8]
    grads["pool_scale"] = take(12).reshape(pool_scale.shape)

    delta, new_m, new_v = {}, {}, {}
    for n in names:
        shp = weights[n].shape
        grads[n] = grads[n].reshape(shp)
        delta[n], new_m[n], new_v[n] = _adamw(weights[n], grads[n], mom_m[n], mom_v[n], "adamw")

    return (loss, grad_x, *[grads[n] for n in names], *[delta[n] for n in names],
            *[new_m[n] for n in names], *[new_v[n] for n in names])
```

```python
import functools
import math

import jax
import jax.numpy as jnp
from jax import lax
from jax.experimental import pallas as pl
from jax.experimental.pallas import tpu as pltpu

F32 = jnp.float32
BF16 = jnp.bfloat16
MESH = pl.DeviceIdType.MESH

D = 1024
N_MOD = 9
N_DEV = 8
HEAD_DIM = 64
ATT_HEADS = 8
KV_HEADS = 2
ATT_W = 512
BLK = 128
ATT_SCALE = HEAD_DIM ** -0.5
GRID_W = 64
ROPE_FREQS = HEAD_DIM // 4
ROPE_THETA = 10000.0
POOL_R = (1, 2, 4, 8)
LRU_C = 8.0
LN_EPS = 1e-5
NEG_INF = -1e30
ALPHA = 4.0 ** 0.25
LR, B1, B2, EPS, WD, STEP = 0.001, 0.9, 0.999, 1e-08, 0.01, 10
VMEM_LIMIT = 56 * 1024 * 1024
ROW_TILE = 512


def _params(sem=None):
    if sem is None:
        return pltpu.CompilerParams(vmem_limit_bytes=VMEM_LIMIT)
    return pltpu.CompilerParams(dimension_semantics=sem, vmem_limit_bytes=VMEM_LIMIT)


def _sigmoid(x):
    return 0.5 * jnp.tanh(0.5 * x) + 0.5


def _dot(a, b):
    return jnp.dot(a.astype(BF16), b.astype(BF16), preferred_element_type=F32)


def _dot_nt(a, b):
    return lax.dot_general(a.astype(BF16), b.astype(BF16), (((1,), (1,)), ((), ())), preferred_element_type=F32)


def _dot_tn(a, b):
    return lax.dot_general(a.astype(BF16), b.astype(BF16), (((0,), (0,)), ((), ())), preferred_element_type=F32)


def _pick(n, cap):
    best = None
    for m in range(128, min(n, cap) + 1, 128):
        if n % m == 0:
            best = m
    return n if best is None else best


def _chunks(width, step=256):
    out, c = [], 0
    while c < width:
        w = min(step, width - c)
        out.append((c, w))
        c += w
    return out


class _Cfg:
    def __init__(self, n_lat, n_ctx):
        self.n_lat, self.n_ctx = n_lat, n_ctx
        self.t_lat, self.t_ctx = 2 * n_lat, 2 * n_ctx
        self.T = self.t_lat + self.t_ctx
        self.tm = min(ROW_TILE, self.t_ctx)
        assert n_lat % self.tm == 0 and self.t_ctx % self.tm == 0 and n_lat >= 3 * BLK and n_ctx % BLK == 0
        self.nt = self.T // self.tm
        self.nlt = n_lat // self.tm
        self.ctx_blk = self.t_lat // n_ctx

    def seg(self, i):
        return jnp.minimum(i // self.nlt, 2)

    def first_of_seg(self, i):
        return jnp.where(i < 2 * self.nlt, i % self.nlt == 0, i == 2 * self.nlt)


def _modulate(cfg, h, mod, shift_idx, scale_idx, name):
    tm = cfg.tm

    def body(h_ref, mod_ref, o_ref):
        sh = mod_ref[shift_idx:shift_idx + 1, :]
        sc = mod_ref[scale_idx:scale_idx + 1, :]
        o_ref[...] = (h_ref[...] * (1.0 + sc) + sh).astype(BF16)

    return pl.pallas_call(
        body, grid=(cfg.nt,), name=name,
        in_specs=[pl.BlockSpec((tm, D), lambda i: (i, 0)),
                  pl.BlockSpec((None, N_MOD, D), lambda i: (cfg.seg(i), 0, 0))],
        out_specs=pl.BlockSpec((tm, D), lambda i: (i, 0)),
        out_shape=jax.ShapeDtypeStruct((cfg.T, D), BF16),
        compiler_params=_params(("parallel",)),
    )(h, mod)


def _ln_fwd(cfg, h, y, mod, gate_idx, coef, lng, lnb, mod_next, next_idx, name):
    tm = cfg.tm
    has_next = next_idx is not None

    def body(*refs):
        if has_next:
            h_ref, y_ref, mod_ref, g_ref, b_ref, modn_ref, hn_ref, xhat_ref, rstd_ref, xin_ref = refs
        else:
            h_ref, y_ref, mod_ref, g_ref, b_ref, hn_ref, xhat_ref, rstd_ref = refs
        gate = mod_ref[gate_idx:gate_idx + 1, :]
        z = ALPHA * h_ref[...] + (coef * gate) * y_ref[...].astype(F32)
        mu = jnp.mean(z, axis=-1, keepdims=True)
        zc = z - mu
        var = jnp.mean(zc * zc, axis=-1, keepdims=True)
        rstd = lax.rsqrt(var + LN_EPS)
        xhat = zc * rstd
        hn = xhat * g_ref[...] + b_ref[...]
        hn_ref[...] = hn
        xhat_ref[...] = xhat.astype(BF16)
        rstd_ref[...] = rstd
        if has_next:
            sh = modn_ref[next_idx[0]:next_idx[0] + 1, :]
            sc = modn_ref[next_idx[1]:next_idx[1] + 1, :]
            xin_ref[...] = (hn * (1.0 + sc) + sh).astype(BF16)

    row = pl.BlockSpec((tm, D), lambda i: (i, 0))
    modspec = pl.BlockSpec((None, N_MOD, D), lambda i: (cfg.seg(i), 0, 0))
    vec = pl.BlockSpec((1, D), lambda i: (0, 0))
    in_specs = [row, row, modspec, vec, vec]
    args = [h, y, mod, lng, lnb]
    out_specs = [row, row, pl.BlockSpec((tm, 1), lambda i: (i, 0))]
    out_shape = [jax.ShapeDtypeStruct((cfg.T, D), F32), jax.ShapeDtypeStruct((cfg.T, D), BF16),
                 jax.ShapeDtypeStruct((cfg.T, 1), F32)]
    if has_next:
        in_specs.append(modspec)
        args.append(mod_next)
        out_specs.append(row)
        out_shape.append(jax.ShapeDtypeStruct((cfg.T, D), BF16))
    return pl.pallas_call(body, grid=(cfg.nt,), name=name, in_specs=in_specs, out_specs=out_specs,
                          out_shape=out_shape, compiler_params=_params(("parallel",)))(*args)


def _ln_bwd(cfg, up, xhat, rstd, y, mod, gate_idx, coef, lng, name):
    tm = cfg.tm
    fused = len(up) > 1
    scale_next = up[4] if fused else None

    def body(*refs):
        if fused:
            dres_n, dxin_n, b_ref, modn_ref, xhat_ref, rstd_ref, y_ref, mod_ref, g_ref, dres_ref, dys_ref, st_ref = refs
        else:
            dhn_ref, xhat_ref, rstd_ref, y_ref, mod_ref, g_ref, dres_ref, dys_ref, st_ref = refs
        i = pl.program_id(0)

        @pl.when(cfg.first_of_seg(i))
        def _():
            st_ref[...] = jnp.zeros_like(st_ref)

        xhat = xhat_ref[...].astype(F32)
        if fused:
            dxin = dxin_n[...].astype(F32)
            sc = modn_ref[scale_next:scale_next + 1, :]
            dhn = dres_n[...] + dxin * (1.0 + sc)
            shift_sum = jnp.sum(dxin, axis=0, keepdims=True)
            st_ref[3:4, :] += g_ref[...] * jnp.sum(dxin * xhat, axis=0, keepdims=True) + b_ref[...] * shift_sum
            st_ref[4:5, :] += shift_sum
        else:
            dhn = dhn_ref[...]
        gdh = dhn * g_ref[...]
        m1 = jnp.mean(gdh, axis=-1, keepdims=True)
        m2 = jnp.mean(gdh * xhat, axis=-1, keepdims=True)
        dz = rstd_ref[...] * (gdh - m1 - xhat * m2)
        gate = mod_ref[gate_idx:gate_idx + 1, :]
        dres_ref[...] = ALPHA * dz
        dys_ref[...] = ((coef * gate) * dz).astype(BF16)
        st_ref[0:1, :] += jnp.sum(dhn * xhat, axis=0, keepdims=True)
        st_ref[1:2, :] += jnp.sum(dhn, axis=0, keepdims=True)
        st_ref[2:3, :] += jnp.sum((coef * dz) * y_ref[...].astype(F32), axis=0, keepdims=True)

    row = pl.BlockSpec((tm, D), lambda i: (i, 0))
    modspec = pl.BlockSpec((None, N_MOD, D), lambda i: (cfg.seg(i), 0, 0))
    vec = pl.BlockSpec((1, D), lambda i: (0, 0))
    col = pl.BlockSpec((tm, 1), lambda i: (i, 0))
    if fused:
        in_specs = [row, row, vec, modspec, row, col, row, modspec, vec]
        args = [up[0], up[1], up[2], up[3], xhat, rstd, y, mod, lng]
    else:
        in_specs = [row, row, col, row, modspec, vec]
        args = [up[0], xhat, rstd, y, mod, lng]
    return pl.pallas_call(
        body, grid=(cfg.nt,), name=name, in_specs=in_specs,
        out_specs=[row, row, pl.BlockSpec((None, 8, D), lambda i: (cfg.seg(i), 0, 0))],
        out_shape=[jax.ShapeDtypeStruct((cfg.T, D), F32), jax.ShapeDtypeStruct((cfg.T, D), BF16),
                   jax.ShapeDtypeStruct((3, 8, D), F32)],
        compiler_params=_params(("arbitrary",)))(*args)


def _modulate_bwd(cfg, dres, dxin, h, mod, scale_idx, name):
    tm = cfg.tm
    n_lt = 2 * cfg.nlt

    def body(dres_ref, dxin_ref, h_ref, mod_ref, dh_ref, st_ref):
        i = pl.program_id(0)

        @pl.when(cfg.first_of_seg(i))
        def _():
            st_ref[...] = jnp.zeros_like(st_ref)

        dxin = dxin_ref[...].astype(F32)
        sc = mod_ref[scale_idx:scale_idx + 1, :]

        @pl.when(i < n_lt)
        def _():
            dh_ref[...] = dres_ref[...] + dxin * (1.0 + sc)

        st_ref[3:4, :] += jnp.sum(dxin * h_ref[...], axis=0, keepdims=True)
        st_ref[4:5, :] += jnp.sum(dxin, axis=0, keepdims=True)

    row = pl.BlockSpec((tm, D), lambda i: (i, 0))
    return pl.pallas_call(
        body, grid=(cfg.nt,), name=name,
        in_specs=[row, row, row, pl.BlockSpec((None, N_MOD, D), lambda i: (cfg.seg(i), 0, 0))],
        out_specs=[pl.BlockSpec((tm, D), lambda i: (jnp.minimum(i, n_lt - 1), 0)),
                   pl.BlockSpec((None, 8, D), lambda i: (cfg.seg(i), 0, 0))],
        out_shape=[jax.ShapeDtypeStruct((cfg.t_lat, D), F32), jax.ShapeDtypeStruct((3, 8, D), F32)],
        compiler_params=_params(("arbitrary",)))(dres, dxin, h, mod)


def _loss(cfg, h, target, name):
    tm = cfg.tm
    n_lt = 2 * cfg.nlt

    def body(h_ref, t_ref, dy_ref, l_ref):
        i = pl.program_id(0)

        @pl.when(i == 0)
        def _():
            l_ref[...] = jnp.zeros_like(l_ref)

        @pl.when(i < n_lt)
        def _():
            err = h_ref[...] - t_ref[...]
            dy_ref[...] = err * (1.0 / D)
            part = jnp.sum(jnp.sum(err * err, axis=1, keepdims=True), axis=0, keepdims=True) * (0.5 / D)
            l_ref[...] += jnp.broadcast_to(part, l_ref.shape)

        @pl.when(i >= n_lt)
        def _():
            dy_ref[...] = jnp.zeros_like(dy_ref)

    return pl.pallas_call(
        body, grid=(cfg.nt,), name=name,
        in_specs=[pl.BlockSpec((tm, D), lambda i: (i, 0)),
                  pl.BlockSpec((tm, D), lambda i: (jnp.minimum(i, n_lt - 1), 0))],
        out_specs=[pl.BlockSpec((tm, D), lambda i: (i, 0)), pl.BlockSpec((8, 128), lambda i: (0, 0))],
        out_shape=[jax.ShapeDtypeStruct((cfg.T, D), F32), jax.ShapeDtypeStruct((8, 128), F32)],
        compiler_params=_params(("arbitrary",)))(h, target)


def _matmul(a, b, mode, out_dtype, name, bm_cap=512, bn_cap=1408, bk_cap=1024):
    if mode == "nn":
        (M, K), N = a.shape, b.shape[1]
    elif mode == "nt":
        (M, K), N = a.shape, b.shape[0]
    else:
        (K, M), N = a.shape, b.shape[1]
    bm, bn, bk = _pick(M, bm_cap), _pick(N, bn_cap), _pick(K, bk_cap)
    nk = K // bk

    def body(a_ref, b_ref, o_ref, acc_ref=None):
        k = pl.program_id(2)
        if mode == "nn":
            part = _dot(a_ref[...], b_ref[...])
        elif mode == "nt":
            part = _dot_nt(a_ref[...], b_ref[...])
        else:
            part = _dot_tn(a_ref[...], b_ref[...])
        if nk == 1:
            o_ref[...] = part.astype(out_dtype)
            return

        @pl.when(k == 0)
        def _():
            acc_ref[...] = part

        @pl.when((k > 0) & (k < nk - 1))
        def _():
            acc_ref[...] += part

        @pl.when(k == nk - 1)
        def _():
            o_ref[...] = (acc_ref[...] + part).astype(out_dtype)

    if mode == "nn":
        a_spec = pl.BlockSpec((bm, bk), lambda i, j, k: (i, k))
        b_spec = pl.BlockSpec((bk, bn), lambda i, j, k: (k, j))
    elif mode == "nt":
        a_spec = pl.BlockSpec((bm, bk), lambda i, j, k: (i, k))
        b_spec = pl.BlockSpec((bn, bk), lambda i, j, k: (j, k))
    else:
        a_spec = pl.BlockSpec((bk, bm), lambda i, j, k: (k, i))
        b_spec = pl.BlockSpec((bk, bn), lambda i, j, k: (k, j))
    return pl.pallas_call(
        body, grid=(M // bm, N // bn, nk), name=name, in_specs=[a_spec, b_spec],
        out_specs=pl.BlockSpec((bm, bn), lambda i, j, k: (i, j)),
        out_shape=jax.ShapeDtypeStruct((M, N), out_dtype),
        scratch_shapes=[pltpu.VMEM((bm, bn), F32)] if nk > 1 else [],
        compiler_params=_params(("parallel", "parallel", "arbitrary")))(a, b)


def _ffn_tile(T, cap):
    best = 256
    for t in range(256, cap + 1, 256):
        if T % t == 0:
            best = t
    return best


def _ffn_fwd(xin, wf, name):
    T = xin.shape[0]
    F = wf.shape[1]
    tm, tf = _ffn_tile(T, 768), F // 2
    assert tf % 128 == 0 and T % tm == 0

    def body(x_ref, wg_ref, wu_ref, wd_ref, g_ref, u_ref, y_ref, acc_ref):
        j = pl.program_id(1)
        x = x_ref[...]
        acc = None
        for c0, cw in _chunks(tf):
            g = _dot_nt(x, wg_ref[c0:c0 + cw, :])
            u = _dot_nt(x, wu_ref[c0:c0 + cw, :])
            g_ref[:, c0:c0 + cw] = g.astype(BF16)
            u_ref[:, c0:c0 + cw] = u.astype(BF16)
            part = _dot(g * _sigmoid(g) * u, wd_ref[c0:c0 + cw, :])
            acc = part if acc is None else acc + part

        @pl.when(j == 0)
        def _():
            acc_ref[...] = acc

        @pl.when(j == 1)
        def _():
            y_ref[...] = (acc_ref[...] + acc).astype(BF16)

    return pl.pallas_call(
        body, grid=(T // tm, 2), name=name,
        in_specs=[pl.BlockSpec((tm, D), lambda i, j: (i, 0)),
                  pl.BlockSpec((None, tf, D), lambda i, j: (0, j, 0)),
                  pl.BlockSpec((None, tf, D), lambda i, j: (1, j, 0)),
                  pl.BlockSpec((None, tf, D), lambda i, j: (2, j, 0))],
        out_specs=[pl.BlockSpec((tm, tf), lambda i, j: (i, j)),
                   pl.BlockSpec((tm, tf), lambda i, j: (i, j)),
                   pl.BlockSpec((tm, D), lambda i, j: (i, 0))],
        out_shape=[jax.ShapeDtypeStruct((T, F), BF16), jax.ShapeDtypeStruct((T, F), BF16),
                   jax.ShapeDtypeStruct((T, D), BF16)],
        scratch_shapes=[pltpu.VMEM((tm, D), F32)],
        compiler_params=_params(("parallel", "arbitrary")))(xin, wf, wf, wf)


def _ffn_bwd(dys, g, u, wf, name):
    T = dys.shape[0]
    F = wf.shape[1]
    tm, tf = _ffn_tile(T, 512), F // 2

    def body(dy_ref, g_ref, u_ref, wg_ref, wu_ref, wd_ref, dg_ref, du_ref, a_ref, dx_ref, acc_ref):
        j = pl.program_id(1)
        da_all = _dot_nt(dy_ref[...], wd_ref[...])
        for c0, cw in _chunks(tf):
            gg = g_ref[:, c0:c0 + cw].astype(F32)
            uu = u_ref[:, c0:c0 + cw].astype(F32)
            da = da_all[:, c0:c0 + cw]
            s = _sigmoid(gg)
            silu = gg * s
            a_ref[:, c0:c0 + cw] = (silu * uu).astype(BF16)
            du_ref[:, c0:c0 + cw] = (da * silu).astype(BF16)
            dg_ref[:, c0:c0 + cw] = (da * uu * (s * (1.0 + gg * (1.0 - s)))).astype(BF16)
        acc = _dot(dg_ref[...], wg_ref[...]) + _dot(du_ref[...], wu_ref[...])

        @pl.when(j == 0)
        def _():
            acc_ref[...] = acc

        @pl.when(j == 1)
        def _():
            dx_ref[...] = (acc_ref[...] + acc).astype(BF16)

    blk = pl.BlockSpec((tm, tf), lambda i, j: (i, j))
    return pl.pallas_call(
        body, grid=(T // tm, 2), name=name,
        in_specs=[pl.BlockSpec((tm, D), lambda i, j: (i, 0)), blk, blk,
                  pl.BlockSpec((None, tf, D), lambda i, j: (0, j, 0)),
                  pl.BlockSpec((None, tf, D), lambda i, j: (1, j, 0)),
                  pl.BlockSpec((None, tf, D), lambda i, j: (2, j, 0))],
        out_specs=[blk, blk, blk, pl.BlockSpec((tm, D), lambda i, j: (i, 0))],
        out_shape=[jax.ShapeDtypeStruct((T, F), BF16), jax.ShapeDtypeStruct((T, F), BF16),
                   jax.ShapeDtypeStruct((T, F), BF16), jax.ShapeDtypeStruct((T, D), BF16)],
        scratch_shapes=[pltpu.VMEM((tm, D), F32)],
        compiler_params=_params(("parallel", "arbitrary")))(dys, g, u, wf, wf, wf)


def _swap_halves(x):
    w = x.shape[1]
    lane = lax.broadcasted_iota(jnp.int32, (1, w), 1)
    return jnp.where((lane & 63) < 32, pltpu.roll(x, w - 32, 1), pltpu.roll(x, 32, 1))


def _rope(x, cos, sin):
    return x * cos + _swap_halves(x) * sin


def _rope_t(dy, cos, sin):
    return dy * cos + _swap_halves(dy * sin)


def _rope_tables(n_lat):
    rows = n_lat // GRID_W
    row = jnp.repeat(jnp.arange(rows, dtype=F32), GRID_W)
    col = jnp.tile(jnp.arange(GRID_W, dtype=F32), rows)
    inv = ROPE_THETA ** (-jnp.arange(ROPE_FREQS, dtype=F32) / ROPE_FREQS)
    ang = jnp.concatenate([row[:, None] * inv, col[:, None] * inv], axis=-1)
    cs, sn = jnp.cos(ang), jnp.sin(ang)
    cos = jnp.concatenate([cs, cs, cs, cs], axis=-1)
    sin = jnp.concatenate([-sn, sn, -sn, sn], axis=-1)
    return cos, sin


def _attn_specs(cfg):
    n_lat, n_ctx, cb = cfg.n_lat, cfg.n_ctx, cfg.ctx_blk
    return [pl.BlockSpec((n_lat, ATT_W), lambda e: (e, 0)),
            pl.BlockSpec((n_lat, 128), lambda e: (e, 4)),
            pl.BlockSpec((n_lat, 128), lambda e: (e, 5)),
            pl.BlockSpec((n_ctx, ATT_W), lambda e: (cb + e, 0)),
            pl.BlockSpec((n_ctx, 128), lambda e: (cb + e, 4)),
            pl.BlockSpec((n_ctx, 128), lambda e: (cb + e, 5)),
            pl.BlockSpec((n_lat, 128), lambda e: (0, 0)),
            pl.BlockSpec((n_lat, 128), lambda e: (0, 0)),
            pl.BlockSpec((8, 128), lambda e: (0, 0))]


def _attn_prepare(kh, kl, vl, kc, vc, ka, kb, va, vb, kca, kcb, vca, vcb):
    lane = lax.broadcasted_iota(jnp.int32, (1, 128), 1)
    own = (lane < 64) if kh == 0 else (lane >= 64)

    def split(x, ra, rb):
        mine = jnp.where(own, x, 0.0)
        other = pltpu.roll(mine, 64, 1)
        a, b = (mine, other) if kh == 0 else (other, mine)
        ra[...] = a.astype(BF16)
        rb[...] = b.astype(BF16)

    split(kl, ka, kb)
    split(vl, va, vb)
    split(kc, kca, kcb)
    split(vc, vca, vcb)


def _softmax_parts(s_list, sk):
    m = sk
    for s in s_list:
        m = jnp.maximum(m, jnp.max(s, axis=1, keepdims=True))
    es = [jnp.exp(s - m) for s in s_list]
    esk = jnp.exp(sk - m)
    den = esk
    for e in es:
        den = den + jnp.sum(e, axis=1, keepdims=True)
    inv = 1.0 / den
    return [e * inv for e in es], esk * inv


def _window(cfg, n):
    r0 = pl.multiple_of(n * BLK, BLK)
    start = pl.multiple_of(jnp.clip((n - 1) * BLK, 0, cfg.n_lat - 3 * BLK), BLK)
    qpos = r0 + lax.broadcasted_iota(jnp.int32, (BLK, 1), 0)
    kpos = start + lax.broadcasted_iota(jnp.int32, (1, 3 * BLK), 1)
    valid = jnp.abs(qpos - kpos) <= BLK
    return r0, start, valid


def _attn_fwd(cfg, p, cos, sin, sink_rows, name):
    n_lat, n_ctx = cfg.n_lat, cfg.n_ctx

    def body(q_ref, k_ref, v_ref, qc_ref, kc_ref, vc_ref, cos_ref, sin_ref, sink_ref, o_ref, oc_ref,
             qr, ka, kb, va, vb, kca, kcb, vca, vcb):
        cos_t, sin_t = cos_ref[...], sin_ref[...]
        for gq in range(4):
            qr[:, gq * 128:(gq + 1) * 128] = _rope(q_ref[:, gq * 128:(gq + 1) * 128], cos_t, sin_t).astype(BF16)
        kl = _rope(k_ref[...], cos_t, sin_t)
        for kh in range(KV_HEADS):
            _attn_prepare(kh, kl, v_ref[...], kc_ref[...], vc_ref[...], ka, kb, va, vb, kca, kcb, vca, vcb)

            def lat_block(n, carry):
                r0, start, valid = _window(cfg, n)
                win = pl.ds(start, 3 * BLK)
                for pr in range(2):
                    lanes = slice((kh * 2 + pr) * 128, (kh * 2 + pr + 1) * 128)
                    qp = qr[pl.ds(r0, BLK), lanes]
                    o = None
                    for half, (kw, kcx, vw, vcx) in enumerate(((ka, kca, va, vca), (kb, kcb, vb, vcb))):
                        head = kh * 4 + pr * 2 + half
                        s_w = jnp.where(valid, _dot_nt(qp, kw[win, :]) * ATT_SCALE, NEG_INF)
                        s_c = _dot_nt(qp, kcx[...]) * ATT_SCALE
                        (p_w, p_c), _ = _softmax_parts([s_w, s_c], sink_ref[head:head + 1, 0:1])
                        part = _dot(p_w, vw[win, :]) + _dot(p_c, vcx[...])
                        o = part if o is None else o + part
                    o_ref[pl.ds(r0, BLK), lanes] = o.astype(BF16)
                return carry

            lax.fori_loop(0, n_lat // BLK, lat_block, 0)
            for n in range(n_ctx // BLK):
                rows = slice(n * BLK, (n + 1) * BLK)
                for pr in range(2):
                    lanes = slice((kh * 2 + pr) * 128, (kh * 2 + pr + 1) * 128)
                    qp = qc_ref[rows, lanes]
                    o = None
                    for half, (kcx, vcx) in enumerate(((kca, vca), (kcb, vcb))):
                        head = kh * 4 + pr * 2 + half
                        s_c = _dot_nt(qp, kcx[...]) * ATT_SCALE
                        (p_c,), _ = _softmax_parts([s_c], sink_ref[head:head + 1, 0:1])
                        part = _dot(p_c, vcx[...])
                        o = part if o is None else o + part
                    oc_ref[rows, lanes] = o.astype(BF16)

    return pl.pallas_call(
        body, grid=(2,), name=name, in_specs=_attn_specs(cfg),
        out_specs=[pl.BlockSpec((n_lat, ATT_W), lambda e: (e, 0)), pl.BlockSpec((n_ctx, ATT_W), lambda e: (e, 0))],
        out_shape=[jax.ShapeDtypeStruct((cfg.t_lat, ATT_W), BF16), jax.ShapeDtypeStruct((cfg.t_ctx, ATT_W), BF16)],
        scratch_shapes=[pltpu.VMEM((n_lat, ATT_W), BF16)] + [pltpu.VMEM((n_lat, 128), BF16)] * 4
        + [pltpu.VMEM((n_ctx, 128), BF16)] * 4,
        compiler_params=_params(("parallel",)))(p, p, p, p, p, p, cos, sin, sink_rows)


def _attn_bwd(cfg, p, dcat, cos, sin, sink_rows, name):
    n_lat, n_ctx, cb = cfg.n_lat, cfg.n_ctx, cfg.ctx_blk

    def body(q_ref, k_ref, v_ref, qc_ref, kc_ref, vc_ref, cos_ref, sin_ref, sink_ref, do_ref, doc_ref,
             dq_ref, dk_ref, dv_ref, dqc_ref, dkc_ref, dvc_ref, dsink_ref,
             qr, ka, kb, va, vb, kca, kcb, vca, vcb, dqs, dka, dva, dkca, dvca):
        cos_t, sin_t = cos_ref[...], sin_ref[...]
        lane = lax.broadcasted_iota(jnp.int32, (1, 128), 1)
        lo = lane < 64
        for gq in range(4):
            qr[:, gq * 128:(gq + 1) * 128] = _rope(q_ref[:, gq * 128:(gq + 1) * 128], cos_t, sin_t).astype(BF16)
        kl = _rope(k_ref[...], cos_t, sin_t)
        dsink_ref[...] = jnp.zeros_like(dsink_ref)
        dka[...] = jnp.zeros_like(dka)
        dva[...] = jnp.zeros_like(dva)
        dkca[...] = jnp.zeros_like(dkca)
        dvca[...] = jnp.zeros_like(dvca)

        def halves(x):
            return jnp.where(lo, x, 0).astype(BF16), jnp.where(lo, 0, x).astype(BF16)

        for kh in range(KV_HEADS):
            _attn_prepare(kh, kl, v_ref[...], kc_ref[...], vc_ref[...], ka, kb, va, vb, kca, kcb, vca, vcb)

            def one_head(head, qp, q_half, do_p, do_half, kw, kcx, vw, vcx, win, valid):
                sk = sink_ref[head:head + 1, 0:1]
                s_list = [_dot_nt(qp, kcx[...]) * ATT_SCALE]
                if win is not None:
                    s_list.insert(0, jnp.where(valid, _dot_nt(qp, kw[win, :]) * ATT_SCALE, NEG_INF))
                probs, p_sink = _softmax_parts(s_list, sk)
                vals = [vcx[...]] if win is None else [vw[win, :], vcx[...]]
                dps = [_dot_nt(do_p, vv) for vv in vals]
                dr = None
                for pp, dp in zip(probs, dps):
                    t = jnp.sum(pp * dp, axis=1, keepdims=True)
                    dr = t if dr is None else dr + t
                dss = [(pp * (dp - dr) * ATT_SCALE).astype(BF16) for pp, dp in zip(probs, dps)]
                dsink_ref[head:head + 1, :] += jnp.broadcast_to(
                    jnp.sum(-p_sink * dr, axis=0, keepdims=True), (1, 128))
                p_c, ds_c = probs[-1], dss[-1]
                dq = _dot(ds_c, kcx[...])
                dkca[kh] += _dot_tn(ds_c, q_half)
                dvca[kh] += _dot_tn(p_c, do_half)
                if win is not None:
                    dq = dq + _dot(dss[0], kw[win, :])
                    dka[kh, win, :] += _dot_tn(dss[0], q_half)
                    dva[kh, win, :] += _dot_tn(probs[0], do_half)
                return dq

            def lat_block(n, carry):
                r0, start, valid = _window(cfg, n)
                win = pl.ds(start, 3 * BLK)
                for pr in range(2):
                    lanes = slice((kh * 2 + pr) * 128, (kh * 2 + pr + 1) * 128)
                    qp = qr[pl.ds(r0, BLK), lanes]
                    do_p = do_ref[pl.ds(r0, BLK), lanes]
                    q_h, do_h = halves(qp), halves(do_p)
                    dq = None
                    for half, (kw, kcx, vw, vcx) in enumerate(((ka, kca, va, vca), (kb, kcb, vb, vcb))):
                        part = one_head(kh * 4 + pr * 2 + half, qp, q_h[half], do_p, do_h[half],
                                        kw, kcx, vw, vcx, win, valid)
                        dq = part if dq is None else dq + part
                    dqs[pl.ds(r0, BLK), lanes] = dq
                return carry

            lax.fori_loop(0, n_lat // BLK, lat_block, 0)
            for n in range(n_ctx // BLK):
                rows = slice(n * BLK, (n + 1) * BLK)
                for pr in range(2):
                    lanes = slice((kh * 2 + pr) * 128, (kh * 2 + pr + 1) * 128)
                    qp = qc_ref[rows, lanes].astype(BF16)
                    do_p = doc_ref[rows, lanes]
                    q_h, do_h = halves(qp), halves(do_p)
                    dq = None
                    for half, (kcx, vcx) in enumerate(((kca, vca), (kcb, vcb))):
                        part = one_head(kh * 4 + pr * 2 + half, qp, q_h[half], do_p, do_h[half],
                                        None, kcx, None, vcx, None, None)
                        dq = part if dq is None else dq + part
                    dqc_ref[rows, lanes] = dq.astype(BF16)

        def fold(acc):
            r0 = acc[0] + pltpu.roll(acc[0], 64, 1)
            r1 = acc[1] + pltpu.roll(acc[1], 64, 1)
            return jnp.where(lo, r0, r1)

        for gq in range(4):
            sl = slice(gq * 128, (gq + 1) * 128)
            dq_ref[:, sl] = _rope_t(dqs[:, sl], cos_t, sin_t).astype(BF16)
        dk_ref[...] = _rope_t(fold(dka), cos_t, sin_t).astype(BF16)
        dv_ref[...] = fold(dva).astype(BF16)
        dkc_ref[...] = fold(dkca).astype(BF16)
        dvc_ref[...] = fold(dvca).astype(BF16)

    lat = lambda w: pl.BlockSpec((n_lat, w), lambda e: (e, 0))
    ctx = lambda w: pl.BlockSpec((n_ctx, w), lambda e: (e, 0))
    sd = jax.ShapeDtypeStruct
    return pl.pallas_call(
        body, grid=(2,), name=name,
        in_specs=_attn_specs(cfg) + [pl.BlockSpec((n_lat, ATT_W), lambda e: (e, 0)),
                                     pl.BlockSpec((n_ctx, ATT_W), lambda e: (cb + e, 0))],
        out_specs=[lat(ATT_W), lat(128), lat(128), ctx(ATT_W), ctx(128), ctx(128),
                   pl.BlockSpec((None, 8, 128), lambda e: (e, 0, 0))],
        out_shape=[sd((cfg.t_lat, ATT_W), BF16), sd((cfg.t_lat, 128), BF16), sd((cfg.t_lat, 128), BF16),
                   sd((cfg.t_ctx, ATT_W), BF16), sd((cfg.t_ctx, 128), BF16), sd((cfg.t_ctx, 128), BF16),
                   sd((2, 8, 128), F32)],
        scratch_shapes=[pltpu.VMEM((n_lat, ATT_W), BF16)] + [pltpu.VMEM((n_lat, 128), BF16)] * 4
        + [pltpu.VMEM((n_ctx, 128), BF16)] * 4
        + [pltpu.VMEM((n_lat, ATT_W), F32), pltpu.VMEM((2, n_lat, 128), F32), pltpu.VMEM((2, n_lat, 128), F32),
           pltpu.VMEM((2, n_ctx, 128), F32), pltpu.VMEM((2, n_ctx, 128), F32)],
        compiler_params=_params(("parallel",)))(p, p, p, p, p, p, cos, sin, sink_rows, dcat, dcat)


def _shift_down(x, k, row):
    return jnp.where(row >= k, pltpu.roll(x, k, 0), 0.0)


def _shift_up(x, k, row):
    n = x.shape[0]
    return jnp.where(row < n - k, pltpu.roll(x, n - k, 0), 0.0)


def _window_sum(x, r, row):
    below, above, k = x, x, 1
    while k < r:
        below = below + _shift_down(below, k, row)
        above = above + _shift_up(above, k, row)
        k *= 2
    return below + _shift_down(x, r, row) + _shift_up(above, 1, row)


def _inv_count(r, row, n):
    cnt = jnp.minimum(row + r, n - 1) + 1 - jnp.maximum(row - r, 0)
    return 1.0 / cnt.astype(F32)


def _pool_fwd(p, w, scale, n, blk0, n_seg, name):
    def body(u0, u1, u2, u3, w_ref, sc_ref, o_ref):
        row = lax.broadcasted_iota(jnp.int32, (n, 1), 0)
        for g, u_ref in enumerate((u0, u1, u2, u3)):
            u = u_ref[...]
            d = _window_sum(u, POOL_R[g], row) * _inv_count(POOL_R[g], row, n) - u
            o_ref[:, g * 128:(g + 1) * 128] = (_dot(d, w_ref[g]) * sc_ref[:, g * 128:(g + 1) * 128]).astype(BF16)

    return pl.pallas_call(
        body, grid=(n_seg,), name=name,
        in_specs=[pl.BlockSpec((n, 128), functools.partial(lambda g, e: (blk0 + e, 6 + g), g)) for g in range(4)]
        + [pl.BlockSpec((4, 128, 128), lambda e: (0, 0, 0)), pl.BlockSpec((1, 512), lambda e: (0, 0))],
        out_specs=pl.BlockSpec((n, 512), lambda e: (e, 0)),
        out_shape=jax.ShapeDtypeStruct((n_seg * n, 512), BF16),
        compiler_params=_params(("parallel",)))(p, p, p, p, w, scale)


def _pool_bwd(p, w, scale, dcat, n, blk0, n_seg, name):
    def body(u0, u1, u2, u3, w_ref, sc_ref, dp_ref, du_ref, dw_ref, dsc_ref):
        e = pl.program_id(0)

        @pl.when(e == 0)
        def _():
            dw_ref[...] = jnp.zeros_like(dw_ref)
            dsc_ref[...] = jnp.zeros_like(dsc_ref)

        row = lax.broadcasted_iota(jnp.int32, (n, 1), 0)
        for g, u_ref in enumerate((u0, u1, u2, u3)):
            sl = slice(g * 128, (g + 1) * 128)
            u = u_ref[...]
            inv = _inv_count(POOL_R[g], row, n)
            d = _window_sum(u, POOL_R[g], row) * inv - u
            dp = dp_ref[:, sl]
            dsc_ref[:, sl] += jnp.sum(dp * _dot(d, w_ref[g]), axis=0, keepdims=True)
            dyp = dp * sc_ref[:, sl]
            dw_ref[g] += _dot_tn(d, dyp)
            dd = _dot_nt(dyp, w_ref[g])
            du_ref[:, sl] = (_window_sum(dd * inv, POOL_R[g], row) - dd).astype(BF16)

    return pl.pallas_call(
        body, grid=(n_seg,), name=name,
        in_specs=[pl.BlockSpec((n, 128), functools.partial(lambda g, e: (blk0 + e, 6 + g), g)) for g in range(4)]
        + [pl.BlockSpec((4, 128, 128), lambda e: (0, 0, 0)), pl.BlockSpec((1, 512), lambda e: (0, 0)),
           pl.BlockSpec((n, 512), lambda e: (blk0 + e, 1))],
        out_specs=[pl.BlockSpec((n, 512), lambda e: (e, 0)),
                   pl.BlockSpec((4, 128, 128), lambda e: (0, 0, 0)), pl.BlockSpec((1, 512), lambda e: (0, 0))],
        out_shape=[jax.ShapeDtypeStruct((n_seg * n, 512), BF16), jax.ShapeDtypeStruct((4, 128, 128), F32),
                   jax.ShapeDtypeStruct((1, 512), F32)],
        compiler_params=_params(("arbitrary",)))(p, p, p, p, w, scale, dcat)


def _gelu(x):
    t = jnp.tanh(math.sqrt(2.0 / math.pi) * (x + 0.044715 * x * x * x))
    return 0.5 * x * (1.0 + t), t


def _gelu_grad(x, t):
    return 0.5 * (1.0 + t) + 0.5 * x * (1.0 - t * t) * (math.sqrt(2.0 / math.pi) * (1.0 + 3 * 0.044715 * x * x))


def _neg_expm1(x):
    series = -x * (1.0 + x * (0.5 + x * (1.0 / 6.0 + x * (1.0 / 24.0 + x * (1.0 / 120.0)))))
    return jnp.where(x > -0.05, series, 1.0 - jnp.exp(x))


def _softplus_neg(lam):
    x = -lam
    e = jnp.exp(-jnp.abs(x))
    log1p = jnp.where(e < 1e-2, e * (1.0 - e * (0.5 - e * (1.0 / 3.0))), jnp.log(1.0 + e))
    return jnp.maximum(x, 0.0) + log1p, -_sigmoid(x)


def _conv(u, w_ref, b_ref, row):
    return (b_ref[...] + _shift_down(u, 1, row) * w_ref[0:1, :] + u * w_ref[1:2, :]
            + _shift_up(u, 1, row) * w_ref[2:3, :] + _shift_up(u, 2, row) * w_ref[3:4, :])


def _lru_gates(uc, d, wa_ref, ba_ref, wx_ref, bx_ref, lam_ref):
    r = _sigmoid(_dot(uc, wa_ref[d]) + ba_ref[d:d + 1, :])
    gi = _sigmoid(_dot(uc, wx_ref[d]) + bx_ref[d:d + 1, :])
    sp, dsp = _softplus_neg(lam_ref[d:d + 1, :])
    la = (-LRU_C) * r * sp
    a = jnp.exp(la)
    sq = jnp.sqrt(_neg_expm1(2.0 * la))
    return r, gi, sp, dsp, a, sq


def _tile_scan(a_ref, b_ref, n, reverse):
    m = n // 8
    first = 7 if reverse else 0
    a_prev = a_ref[pl.ds(first, m, stride=8), :]
    b_prev = b_ref[pl.ds(first, m, stride=8), :]
    for j in (range(6, -1, -1) if reverse else range(1, 8)):
        rows = pl.ds(j, m, stride=8)
        aj = a_ref[rows, :]
        b_prev = aj * b_prev + b_ref[rows, :]
        a_prev = aj * a_prev
        b_ref[rows, :] = b_prev
        a_ref[rows, :] = a_prev


def _carry_scan(a_ref, b_ref, n, reverse, carry):
    nt8 = n // 8

    def step(i, c):
        t = (nt8 - 1 - i) if reverse else i
        off = pl.multiple_of(t * 8, 8)
        h = a_ref[pl.ds(off, 8), :] * c + b_ref[pl.ds(off, 8), :]
        b_ref[pl.ds(off, 8), :] = h
        return h[0:1, :] if reverse else h[7:8, :]

    return lax.fori_loop(0, nt8, step, carry)


def _chain_scan(segs, reverse):
    carry = jnp.zeros((1, 128), F32)
    for a, b, a_ref, b_ref, n in segs:
        a_ref[...] = a
        b_ref[...] = b
        _tile_scan(a_ref, b_ref, n, reverse)
        carry = _carry_scan(a_ref, b_ref, n, reverse, carry)


def _lru_specs(cfg):
    n_lat, n_ctx, cb = cfg.n_lat, cfg.n_ctx, cfg.ctx_blk
    return [pl.BlockSpec((n_lat, 128), lambda hb, e: (e, hb)),
            pl.BlockSpec((n_lat, 128), lambda hb, e: (e, 8 + hb)),
            pl.BlockSpec((n_ctx, 128), lambda hb, e: (cb + e, hb)),
            pl.BlockSpec((n_ctx, 128), lambda hb, e: (cb + e, 8 + hb)),
            pl.BlockSpec((4, 128), lambda hb, e: (0, hb)),
            pl.BlockSpec((1, 128), lambda hb, e: (0, hb)),
            pl.BlockSpec((2, None, 128, 128), lambda hb, e: (0, hb, 0, 0)),
            pl.BlockSpec((2, 128), lambda hb, e: (0, hb)),
            pl.BlockSpec((2, None, 128, 128), lambda hb, e: (0, hb, 0, 0)),
            pl.BlockSpec((2, 128), lambda hb, e: (0, hb)),
            pl.BlockSpec((2, 128), lambda hb, e: (0, hb))]


def _lru_fwd(cfg, p, consts, name):
    n_lat, n_ctx = cfg.n_lat, cfg.n_ctx

    def body(gl_ref, ul_ref, gc_ref, uc_ref, cw_ref, cb_ref, wa_ref, ba_ref, wx_ref, bx_ref, lam_ref,
             zl_ref, zc_ref, hl_ref, hc_ref, al, ac):
        row_l = lax.broadcasted_iota(jnp.int32, (n_lat, 1), 0)
        row_c = lax.broadcasted_iota(jnp.int32, (n_ctx, 1), 0)
        uc_l = _conv(ul_ref[...], cw_ref, cb_ref, row_l)
        uc_c = _conv(uc_ref[...], cw_ref, cb_ref, row_c)
        for d in range(2):
            _, gi_l, _, _, a_l, sq_l = _lru_gates(uc_l, d, wa_ref, ba_ref, wx_ref, bx_ref, lam_ref)
            _, gi_c, _, _, a_c, sq_c = _lru_gates(uc_c, d, wa_ref, ba_ref, wx_ref, bx_ref, lam_ref)
            _chain_scan([(a_c, sq_c * (gi_c * uc_c), ac, hc_ref.at[d], n_ctx),
                         (a_l, sq_l * (gi_l * uc_l), al, hl_ref.at[d], n_lat)], reverse=(d == 1))
        zl_ref[...] = (_gelu(gl_ref[...])[0] * (hl_ref[0] + hl_ref[1])).astype(BF16)
        zc_ref[...] = (_gelu(gc_ref[...])[0] * (hc_ref[0] + hc_ref[1])).astype(BF16)

    return pl.pallas_call(
        body, grid=(8, 2), name=name, in_specs=_lru_specs(cfg),
        out_specs=[pl.BlockSpec((n_lat, 128), lambda hb, e: (e, hb)), pl.BlockSpec((n_ctx, 128), lambda hb, e: (e, hb)),
                   pl.BlockSpec((2, n_lat, 128), lambda hb, e: (0, e, hb)),
                   pl.BlockSpec((2, n_ctx, 128), lambda hb, e: (0, e, hb))],
        out_shape=[jax.ShapeDtypeStruct((cfg.t_lat, D), BF16), jax.ShapeDtypeStruct((cfg.t_ctx, D), BF16),
                   jax.ShapeDtypeStruct((2, cfg.t_lat, D), F32), jax.ShapeDtypeStruct((2, cfg.t_ctx, D), F32)],
        scratch_shapes=[pltpu.VMEM((n_lat, 128), F32), pltpu.VMEM((n_ctx, 128), F32)],
        compiler_params=_params(("parallel", "arbitrary")))(p, p, p, p, *consts)


def _lru_bwd(cfg, p, dz, h_lat, h_ctx, consts, name):
    n_lat, n_ctx, cb = cfg.n_lat, cfg.n_ctx, cfg.ctx_blk

    def body(gl_ref, ul_ref, gc_ref, uc_ref, cw_ref, cb_ref, wa_ref, ba_ref, wx_ref, bx_ref, lam_ref,
             dzl_ref, dzc_ref, hl, hc, dgl_ref, dul_ref, dgc_ref, duc_ref, dwa_ref, dwx_ref, vec_ref,
             al, bl, ac, bc):
        e = pl.program_id(1)

        @pl.when(e == 0)
        def _():
            dwa_ref[...] = jnp.zeros_like(dwa_ref)
            dwx_ref[...] = jnp.zeros_like(dwx_ref)
            vec_ref[...] = jnp.zeros_like(vec_ref)

        row_l = lax.broadcasted_iota(jnp.int32, (n_lat, 1), 0)
        row_c = lax.broadcasted_iota(jnp.int32, (n_ctx, 1), 0)
        u_l, u_c = ul_ref[...], uc_ref[...]
        uc_l = _conv(u_l, cw_ref, cb_ref, row_l)
        uc_c = _conv(u_c, cw_ref, cb_ref, row_c)
        gel_l, t_l = _gelu(gl_ref[...])
        gel_c, t_c = _gelu(gc_ref[...])
        dz_l, dz_c = dzl_ref[...], dzc_ref[...]
        dgl_ref[...] = (dz_l * (hl[0] + hl[1]) * _gelu_grad(gl_ref[...], t_l)).astype(BF16)
        dgc_ref[...] = (dz_c * (hc[0] + hc[1]) * _gelu_grad(gc_ref[...], t_c)).astype(BF16)
        dy_l, dy_c = dz_l * gel_l, dz_c * gel_c
        duc_l = jnp.zeros((n_lat, 128), F32)
        duc_c = jnp.zeros((n_ctx, 128), F32)
        for d in range(2):
            r_l, gi_l, sp, dsp, a_l, sq_l = _lru_gates(uc_l, d, wa_ref, ba_ref, wx_ref, bx_ref, lam_ref)
            r_c, gi_c, _, _, a_c, sq_c = _lru_gates(uc_c, d, wa_ref, ba_ref, wx_ref, bx_ref, lam_ref)
            if d == 0:
                an_l = _shift_up(a_l, 1, row_l)
                an_c = jnp.where(row_c < n_ctx - 1, pltpu.roll(a_c, n_ctx - 1, 0), a_l[0:1, :])
            else:
                an_l = _shift_down(a_l, 1, row_l)
                an_c = jnp.where(row_c >= 1, pltpu.roll(a_c, 1, 0), a_l[n_lat - 1:n_lat, :])
            _chain_scan([(an_l, dy_l, al, bl, n_lat), (an_c, dy_c, ac, bc, n_ctx)], reverse=(d == 0))
            dsp_sum = jnp.zeros((1, 128), F32)
            for (dh, h, r, gi, a, sq, uc, seg) in ((bl[...], hl[d], r_l, gi_l, a_l, sq_l, uc_l, "l"),
                                                  (bc[...], hc[d], r_c, gi_c, a_c, sq_c, uc_c, "c")):
                b0 = sq * (gi * uc)
                t1 = dh * sq
                dla = dh * (h - b0) - (dh * gi * uc) * (a * a) / sq
                dzr = (dla * ((-LRU_C) * sp)) * r * (1.0 - r)
                dzi = (t1 * uc) * gi * (1.0 - gi)
                dsp_sum = dsp_sum + jnp.sum(dla * ((-LRU_C) * r), axis=0, keepdims=True)
                dwa_ref[d] += _dot_tn(uc, dzr)
                dwx_ref[d] += _dot_tn(uc, dzi)
                vec_ref[d:d + 1, :] += jnp.sum(dzr, axis=0, keepdims=True)
                vec_ref[2 + d:3 + d, :] += jnp.sum(dzi, axis=0, keepdims=True)
                duc = t1 * gi + _dot_nt(dzr, wa_ref[d]) + _dot_nt(dzi, wx_ref[d])
                if seg == "l":
                    duc_l = duc_l + duc
                else:
                    duc_c = duc_c + duc
            vec_ref[4 + d:5 + d, :] += dsp_sum * dsp
        for duc, u, row, du_ref in ((duc_l, u_l, row_l, dul_ref), (duc_c, u_c, row_c, duc_ref)):
            du_ref[...] = (_shift_up(duc, 1, row) * cw_ref[0:1, :] + duc * cw_ref[1:2, :]
                           + _shift_down(duc, 1, row) * cw_ref[2:3, :]
                           + _shift_down(duc, 2, row) * cw_ref[3:4, :]).astype(BF16)
            vec_ref[6:7, :] += jnp.sum(duc * _shift_down(u, 1, row), axis=0, keepdims=True)
            vec_ref[7:8, :] += jnp.sum(duc * u, axis=0, keepdims=True)
            vec_ref[8:9, :] += jnp.sum(duc * _shift_up(u, 1, row), axis=0, keepdims=True)
            vec_ref[9:10, :] += jnp.sum(duc * _shift_up(u, 2, row), axis=0, keepdims=True)
            vec_ref[10:11, :] += jnp.sum(duc, axis=0, keepdims=True)

    lat = pl.BlockSpec((n_lat, 128), lambda hb, e: (e, hb))
    ctx = pl.BlockSpec((n_ctx, 128), lambda hb, e: (e, hb))
    wspec = pl.BlockSpec((2, None, 128, 128), lambda hb, e: (0, hb, 0, 0))
    sd = jax.ShapeDtypeStruct
    return pl.pallas_call(
        body, grid=(8, 2), name=name,
        in_specs=_lru_specs(cfg) + [pl.BlockSpec((n_lat, 128), lambda hb, e: (e, hb)),
                                    pl.BlockSpec((n_ctx, 128), lambda hb, e: (cb + e, hb)),
                                    pl.BlockSpec((2, n_lat, 128), lambda hb, e: (0, e, hb)),
                                    pl.BlockSpec((2, n_ctx, 128), lambda hb, e: (0, e, hb))],
        out_specs=[lat, lat, ctx, ctx, wspec, wspec, pl.BlockSpec((None, 16, 128), lambda hb, e: (hb, 0, 0))],
        out_shape=[sd((cfg.t_lat, D), BF16), sd((cfg.t_lat, D), BF16), sd((cfg.t_ctx, D), BF16), sd((cfg.t_ctx, D), BF16),
                   sd((2, 8, 128, 128), F32), sd((2, 8, 128, 128), F32), sd((8, 16, 128), F32)],
        scratch_shapes=[pltpu.VMEM((n_lat, 128), F32)] * 2 + [pltpu.VMEM((n_ctx, 128), F32)] * 2,
        compiler_params=_params(("parallel", "arbitrary")))(p, p, p, p, *consts, dz, dz, h_lat, h_ctx)


def _position():
    x, y, c = lax.axis_index("x"), lax.axis_index("y"), lax.axis_index("c")
    return x, y, c, 4 * x + 2 * y + c


def _peer(x, y, c, k):
    px = 1 - x if k & 4 else x
    py = 1 - y if k & 2 else y
    pc = 1 - c if k & 1 else c
    return (px, py, pc), 4 * px + 2 * py + pc


def _all_gather(v, name, in_vmem):
    def body(v_ref, o_ref, send_sems, recv_sems, local_sem):
        x, y, c, me = _position()
        mine = pltpu.make_async_copy(v_ref, o_ref.at[me], local_sem)
        mine.start()
        sends = []
        for k in range(1, N_DEV):
            peer, _ = _peer(x, y, c, k)
            cp = pltpu.make_async_remote_copy(src_ref=v_ref, dst_ref=o_ref.at[me], send_sem=send_sems.at[k - 1],
                                              recv_sem=recv_sems.at[k - 1], device_id=peer, device_id_type=MESH)
            cp.start()
            sends.append(cp)
        for k in range(1, N_DEV):
            peer, peer_lin = _peer(x, y, c, k)
            pltpu.make_async_remote_copy(src_ref=v_ref, dst_ref=o_ref.at[peer_lin], send_sem=send_sems.at[k - 1],
                                         recv_sem=recv_sems.at[k - 1], device_id=peer, device_id_type=MESH).wait_recv()
        for cp in sends:
            cp.wait_send()
        mine.wait()

    space = pltpu.VMEM if in_vmem else pl.ANY
    return pl.pallas_call(
        body, name=name,
        in_specs=[pl.BlockSpec(memory_space=space)], out_specs=pl.BlockSpec(memory_space=space),
        out_shape=jax.ShapeDtypeStruct((N_DEV,) + v.shape, v.dtype),
        scratch_shapes=[pltpu.SemaphoreType.DMA((N_DEV - 1,)), pltpu.SemaphoreType.DMA((N_DEV - 1,)),
                        pltpu.SemaphoreType.DMA],
        compiler_params=pltpu.CompilerParams(vmem_limit_bytes=VMEM_LIMIT))(v)


_HBM = pl.BlockSpec(memory_space=pltpu.HBM)
_SEM = pl.BlockSpec(memory_space=pltpu.SEMAPHORE)
_EFFECT = pltpu.SideEffectType.DATAFLOW_SIDE_EFFECTING


ALL_PEERS = tuple(range(1, N_DEV))
SAME_CORE_AND_SIBLING = (1, 2, 4, 6)


def _push_start(src, land, block_of, name, relations=ALL_PEERS):
    def body(src_ref, land_ref, send_sem, recv_sem, src_thru, land_thru, token):
        x, y, c, me = _position()
        for k in relations:
            peer, peer_lin = _peer(x, y, c, k)
            mine, there = block_of(src_ref, land_ref, me, peer_lin)
            pltpu.make_async_remote_copy(src_ref=mine, dst_ref=there, send_sem=send_sem, recv_sem=recv_sem,
                                         device_id=peer, device_id_type=MESH).start()
        mine, here = block_of(src_ref, land_ref, me, me)
        pltpu.make_async_copy(mine, here, recv_sem).start()
        token[...] = jnp.zeros_like(token)

    return pl.pallas_call(
        body, name=name,
        out_shape=(pltpu.SemaphoreType.DMA(()), pltpu.SemaphoreType.DMA(()), pltpu.HBM(src.shape, src.dtype),
                   pltpu.HBM(land.shape, land.dtype), jax.ShapeDtypeStruct((8, 128), F32)),
        in_specs=(_HBM, _HBM), out_specs=(_SEM, _SEM, _HBM, _HBM, pl.BlockSpec(memory_space=pltpu.VMEM)),
        input_output_aliases={0: 2, 1: 3},
        compiler_params=pltpu.CompilerParams(has_side_effects=_EFFECT),
    )(pltpu.with_memory_space_constraint(src, pltpu.HBM), pltpu.with_memory_space_constraint(land, pltpu.HBM))


def _push_wait(handle, blocks_of, after, name, n_peers=N_DEV - 1):
    send_sem, recv_sem, src_thru, land_thru, _ = handle

    def body(src_ref, land_ref, send_sem, recv_sem, after_ref, src_dead, got_ref):
        x, y, c, _ = _position()
        sent, landed = blocks_of(land_ref, n_peers), blocks_of(land_ref, n_peers + 1)
        pltpu.make_async_remote_copy(src_ref=sent, dst_ref=sent, send_sem=send_sem, recv_sem=recv_sem,
                                     device_id=(x, y, 1 - c), device_id_type=MESH).wait_send()
        pltpu.make_async_remote_copy(src_ref=landed, dst_ref=landed, send_sem=send_sem, recv_sem=recv_sem,
                                     device_id=(x, y, 1 - c), device_id_type=MESH).wait_recv()

    return pl.pallas_call(
        body, name=name,
        out_shape=(pltpu.HBM(src_thru.shape, src_thru.dtype), pltpu.HBM(land_thru.shape, land_thru.dtype)),
        in_specs=(_HBM, _HBM, _SEM, _SEM, pl.BlockSpec(memory_space=pl.ANY)), out_specs=(_HBM, _HBM),
        input_output_aliases={0: 0, 1: 1},
        compiler_params=pltpu.CompilerParams(has_side_effects=_EFFECT),
    )(src_thru, land_thru, send_sem, recv_sem, after)[1]


def _gather_start(src, name, relations=ALL_PEERS):
    g, r, C = src.shape
    land = lax.empty((g, N_DEV * r, C), src.dtype)
    return _push_start(src, land, lambda s, z, i, p: (s, z.at[:, pl.ds(i * r, r), :]), name, relations)


def _gather_wait(handle, after, name, n_peers=N_DEV - 1):
    r = handle[2].shape[1]
    return _push_wait(handle, lambda z, n: z.at[:, pl.ds(0, n * r), :], after, name, n_peers)


def _relay_start(land, r, name):
    def body(land_ref, send_sem, recv_sem, land_thru, token):
        x, y, c, _ = _position()
        for k in (2, 4, 6):
            _, origin = _peer(x, y, c, k)
            rows = land_ref.at[:, pl.ds(origin * r, r), :]
            pltpu.make_async_remote_copy(src_ref=rows, dst_ref=rows, send_sem=send_sem, recv_sem=recv_sem,
                                         device_id=(x, y, 1 - c), device_id_type=MESH).start()
        token[...] = jnp.zeros_like(token)

    return pl.pallas_call(
        body, name=name,
        out_shape=(pltpu.SemaphoreType.DMA(()), pltpu.SemaphoreType.DMA(()), pltpu.HBM(land.shape, land.dtype),
                   jax.ShapeDtypeStruct((8, 128), F32)),
        in_specs=(_HBM,), out_specs=(_SEM, _SEM, _HBM, pl.BlockSpec(memory_space=pltpu.VMEM)),
        input_output_aliases={0: 2},
        compiler_params=pltpu.CompilerParams(has_side_effects=_EFFECT),
    )(pltpu.with_memory_space_constraint(land, pltpu.HBM))


def _relay_wait(handle, r, after, name):
    send_sem, recv_sem, land_thru, _ = handle

    def body(land_ref, send_sem, recv_sem, after_ref, got_ref):
        x, y, c, _ = _position()
        three = land_ref.at[:, pl.ds(0, 3 * r), :]
        cp = pltpu.make_async_remote_copy(src_ref=three, dst_ref=three, send_sem=send_sem, recv_sem=recv_sem,
                                          device_id=(x, y, 1 - c), device_id_type=MESH)
        cp.wait_send()
        cp.wait_recv()

    return pl.pallas_call(
        body, name=name, out_shape=(pltpu.HBM(land_thru.shape, land_thru.dtype),),
        in_specs=(_HBM, _SEM, _SEM, pl.BlockSpec(memory_space=pl.ANY)), out_specs=(_HBM,),
        input_output_aliases={0: 0},
        compiler_params=pltpu.CompilerParams(has_side_effects=_EFFECT),
    )(land_thru, send_sem, recv_sem, after)[0]


def _exchange_start(grad, name):
    g, rows, C = grad.shape
    r = rows // N_DEV
    land = lax.empty((N_DEV, g, r, C), grad.dtype)
    return _push_start(grad, land, lambda s, z, i, p: (s.at[:, pl.ds(p * r, r), :], z.at[i]), name)


def _exchange_wait(handle, after, name):
    return _push_wait(handle, lambda z, n: z.at[pl.ds(0, n)], after, name)


def _sum_blocks(v, name):
    k, rows, cols = v.shape
    tr = rows
    for cand in (rows, 512, 352, 256, 176, 128, 64, 32, 16):
        if rows % cand == 0 and k * cand * cols * v.dtype.itemsize <= 6 * 1024 * 1024:
            tr = cand
            break

    def body(v_ref, o_ref):
        acc = v_ref[0].astype(F32)
        for s in range(1, k):
            acc = acc + v_ref[s].astype(F32)
        o_ref[...] = acc

    return pl.pallas_call(
        body, grid=(rows // tr,), name=name,
        in_specs=[pl.BlockSpec((k, tr, cols), lambda i: (0, i, 0))],
        out_specs=pl.BlockSpec((tr, cols), lambda i: (i, 0)),
        out_shape=jax.ShapeDtypeStruct((rows, cols), F32),
        compiler_params=_params(("parallel",)))(v)


def _adam_math(w, g, m, v):
    m2 = B1 * m + (1.0 - B1) * g
    v2 = B2 * v + (1.0 - B2) * (g * g)
    m_hat = m2 / (1.0 - B1 ** STEP)
    v_hat = v2 / (1.0 - B2 ** STEP)
    return -LR * (m_hat / (jnp.sqrt(v_hat) + EPS) + WD * w), m2, v2


def _adamw(w, g, m, v, name):
    shp = w.shape
    rows, cols = (shp[-2], shp[-1]) if len(shp) >= 2 else (1, shp[-1])
    lead = math.prod(shp[:-2]) if len(shp) > 2 else 1
    fits = [t for t in range(8, rows + 1, 8) if rows % t == 0 and t * cols * 4 <= 2 * 1024 * 1024]
    tr = max(fits) if fits else rows

    def body(w_ref, g_ref, m_ref, v_ref, d_ref, m2_ref, v2_ref):
        d_ref[...], m2_ref[...], v2_ref[...] = _adam_math(w_ref[...], g_ref[...], m_ref[...], v_ref[...])

    blk = pl.BlockSpec((None, tr, cols), lambda b, i: (b, i, 0))
    outs = pl.pallas_call(
        body, grid=(lead, rows // tr), name=name, in_specs=[blk] * 4, out_specs=[blk] * 3,
        out_shape=[jax.ShapeDtypeStruct((lead, rows, cols), F32)] * 3,
        compiler_params=_params(("parallel", "parallel")))(*[a.reshape(lead, rows, cols) for a in (w, g, m, v)])
    return [o.reshape(shp) for o in outs]


def _as2d(a):
    n = a.size
    if n % 1024 == 0:
        return a.reshape(n // 1024, 1024)
    if n % 128 == 0:
        return a.reshape(n // 128, 128)
    return a.reshape(1, n)


def _blocks_to_cols(a):
    b = jnp.moveaxis(a, 0, -2)
    return b.reshape(b.shape[:-2] + (b.shape[-2] * b.shape[-1],))


def _pack_rows(parts):
    padded, offs, r = [], [], 0
    for p in parts:
        pad = (-p.shape[0]) % 8
        padded.append(jnp.pad(p, ((0, pad), (0, 0))) if pad else p)
        offs.append(r)
        r += p.shape[0] + pad
    return jnp.concatenate(padded, axis=0), offs


def _silu(x):
    return x * jax.nn.sigmoid(x)


def kernel(x, c, ctx, c_ctx, w_mod, b_mod, ln_g, ln_b, ffn_w_gate, ffn_w_up, ffn_w_down, mix_ab_w_in, attn_sink, pool_w, pool_scale, mix_ab_w_out, lru_w_in, lru_conv_w, lru_conv_b, lru_wa, lru_ba, lru_wx, lru_bx, lru_lambda, lru_w_out, loss_target, m_c_ctx, m_w_mod, m_b_mod, m_ln_g, m_ln_b, m_ffn_w_gate, m_ffn_w_up, m_ffn_w_down, m_mix_ab_w_in, m_attn_sink, m_pool_w, m_pool_scale, m_mix_ab_w_out, m_lru_w_in, m_lru_conv_w, m_lru_conv_b, m_lru_wa, m_lru_ba, m_lru_wx, m_lru_bx, m_lru_lambda, m_lru_w_out, v_c_ctx, v_w_mod, v_b_mod, v_ln_g, v_ln_b, v_ffn_w_gate, v_ffn_w_up, v_ffn_w_down, v_mix_ab_w_in, v_attn_sink, v_pool_w, v_pool_scale, v_mix_ab_w_out, v_lru_w_in, v_lru_conv_w, v_lru_conv_b, v_lru_wa, v_lru_ba, v_lru_wx, v_lru_bx, v_lru_lambda, v_lru_w_out):
    weights = dict(c_ctx=c_ctx, w_mod=w_mod, b_mod=b_mod, ln_g=ln_g, ln_b=ln_b, ffn_w_gate=ffn_w_gate,
                   ffn_w_up=ffn_w_up, ffn_w_down=ffn_w_down, mix_ab_w_in=mix_ab_w_in, attn_sink=attn_sink,
                   pool_w=pool_w, pool_scale=pool_scale, mix_ab_w_out=mix_ab_w_out, lru_w_in=lru_w_in,
                   lru_conv_w=lru_conv_w, lru_conv_b=lru_conv_b, lru_wa=lru_wa, lru_ba=lru_ba, lru_wx=lru_wx,
                   lru_bx=lru_bx, lru_lambda=lru_lambda, lru_w_out=lru_w_out)
    mom_m = dict(c_ctx=m_c_ctx, w_mod=m_w_mod, b_mod=m_b_mod, ln_g=m_ln_g, ln_b=m_ln_b, ffn_w_gate=m_ffn_w_gate,
                 ffn_w_up=m_ffn_w_up, ffn_w_down=m_ffn_w_down, mix_ab_w_in=m_mix_ab_w_in, attn_sink=m_attn_sink,
                 pool_w=m_pool_w, pool_scale=m_pool_scale, mix_ab_w_out=m_mix_ab_w_out, lru_w_in=m_lru_w_in,
                 lru_conv_w=m_lru_conv_w, lru_conv_b=m_lru_conv_b, lru_wa=m_lru_wa, lru_ba=m_lru_ba, lru_wx=m_lru_wx,
                 lru_bx=m_lru_bx, lru_lambda=m_lru_lambda, lru_w_out=m_lru_w_out)
    mom_v = dict(c_ctx=v_c_ctx, w_mod=v_w_mod, b_mod=v_b_mod, ln_g=v_ln_g, ln_b=v_ln_b, ffn_w_gate=v_ffn_w_gate,
                 ffn_w_up=v_ffn_w_up, ffn_w_down=v_ffn_w_down, mix_ab_w_in=v_mix_ab_w_in, attn_sink=v_attn_sink,
                 pool_w=v_pool_w, pool_scale=v_pool_scale, mix_ab_w_out=v_mix_ab_w_out, lru_w_in=v_lru_w_in,
                 lru_conv_w=v_lru_conv_w, lru_conv_b=v_lru_conv_b, lru_wa=v_lru_wa, lru_ba=v_lru_ba, lru_wx=v_lru_wx,
                 lru_bx=v_lru_bx, lru_lambda=v_lru_lambda, lru_w_out=v_lru_w_out)
    names = list(weights)

    n_lat, n_ctx = x.shape[1], ctx.shape[1]
    cfg = _Cfg(n_lat, n_ctx)
    _, _, _, me = _position()
    mcols = w_mod.shape[2]

    def t_bf16(w):
        return jnp.swapaxes(w, -1, -2).astype(BF16)

    def ffn_src(l, i):
        return jnp.stack([t_bf16(ffn_w_gate[l, i]), t_bf16(ffn_w_up[l, i]), ffn_w_down[l, i].astype(BF16)])

    pending = {}

    def start_gathers(items, tok):
        for key, make_src in items:
            pending[key] = _gather_start(make_src() + tok.astype(BF16), "gather_start_" + key)
            tok = pending[key][4][0, 0]
        return tok

    def weights_now(key, after):
        return _gather_wait(pending[key], after, "gather_wait_" + key)

    first = _gather_start(ffn_src(0, 0), "gather_start_ffn00", SAME_CORE_AND_SIBLING)
    tok = first[4][0, 0]

    small_names = ["ln_g", "ln_b", "lru_conv_w", "lru_conv_b", "lru_ba", "lru_bx", "lru_lambda"]
    small, small_off = _pack_rows([(c + tok).reshape(-1, 128)] + [weights[n].reshape(-1, 128) for n in small_names])
    small_all = _all_gather(small, "gather_small", True)

    def small_full(idx, shp):
        rows = math.prod(shp) // 128
        return _blocks_to_cols(small_all[:, small_off[idx]:small_off[idx] + rows, :].reshape((N_DEV,) + shp))

    c_all = small_all[:, :2 * D // 128, :].reshape(2 * N_DEV, D)
    ln_g_f, ln_b_f = small_full(1, ln_g.shape), small_full(2, ln_b.shape)
    lru_consts = (small_full(3, lru_conv_w.shape)[0], small_full(4, lru_conv_b.shape), lru_wa[0],
                  small_full(5, lru_ba.shape)[0], lru_wx[0], small_full(6, lru_bx.shape)[0],
                  small_full(7, lru_lambda.shape)[0])

    s_rows = jnp.zeros((32, D), F32).at[:16].set(_silu(c_all)).at[16].set(_silu(c_ctx)).astype(BF16)
    mod_mine = jnp.stack([_matmul(s_rows, w_mod[l], "nn", F32, "mod_fwd", bn_cap=1280) for l in range(2)])
    mod_all = _all_gather(mod_mine.reshape(64, mcols), "gather_mod", True).reshape(N_DEV, 2, 32, mcols)
    r_ffn = ffn_w_down.shape[2]
    relay = _relay_start(_gather_wait(first, mod_all, "gather_wait_ffn00", n_peers=len(SAME_CORE_AND_SIBLING)),
                         r_ffn, "gather_relay_start_ffn00")
    tok = start_gathers([("ab_in", lambda: t_bf16(mix_ab_w_in)), ("ab_out", lambda: mix_ab_w_out.astype(BF16)),
                         ("ffn01", lambda: ffn_src(0, 1)), ("ffn10", lambda: ffn_src(1, 0)),
                         ("lru_in", lambda: t_bf16(lru_w_in)), ("lru_out", lambda: lru_w_out.astype(BF16)),
                         ("ffn11", lambda: ffn_src(1, 1))], relay[3][0, 0])
    mod_full = _blocks_to_cols(mod_all) + (b_mod[:, None, :] + tok)
    ex0 = 2 * me
    mods = []
    for l in range(2):
        rows = jnp.stack([lax.dynamic_index_in_dim(mod_full[l], ex0, 0, False),
                          lax.dynamic_index_in_dim(mod_full[l], ex0 + 1, 0, False), mod_full[l, 16]])
        mods.append(rows.reshape(3, N_MOD, D))

    h0 = jnp.concatenate([x.reshape(cfg.t_lat, D), ctx.reshape(cfg.t_ctx, D)], axis=0)
    cos, sin = _rope_tables(n_lat)
    sink_rows = jnp.broadcast_to(attn_sink[0][:, None], (8, 128)).astype(F32)

    saved = []
    wf = [[None, None], [None, None]]
    h = h0
    xin = _modulate(cfg, h0, mods[0], 0, 1, "modulate_in")
    for l in range(2):
        st = {"h_in": h, "xin1": xin}
        wf[l][0] = (_relay_wait(relay, r_ffn, xin, "gather_relay_wait_ffn00") if l == 0
                    else weights_now("ffn10", xin))
        g1, u1, y1 = _ffn_fwd(xin, wf[l][0], "ffn_fwd")
        h1, xhat1, rstd1, xin2 = _ln_fwd(cfg, h, y1, mods[l], 2, 0.5, ln_g_f[l, 0][None], ln_b_f[l, 0][None],
                                          mods[l], (3, 4), "ln_fwd_a")
        st.update(g1=g1, u1=u1, y1=y1, h1=h1, xhat1=xhat1, rstd1=rstd1, xin2=xin2)
        if l == 0:
            w_ab_in_t = weights_now("ab_in", xin2)[0]
            p = _matmul(xin2, w_ab_in_t, "nt", F32, "mix_ab_in")
            att_l, att_c = _attn_fwd(cfg, p, cos, sin, sink_rows, "attn_fwd")
            pool_l = _pool_fwd(p, pool_w[0], pool_scale, n_lat, 0, 2, "pool_fwd_lat")
            pool_c = _pool_fwd(p, pool_w[0], pool_scale, n_ctx, cfg.ctx_blk, 2, "pool_fwd_ctx")
            cat = jnp.concatenate([jnp.concatenate([att_l, pool_l], axis=1),
                                   jnp.concatenate([att_c, pool_c], axis=1)], axis=0)
            w_ab_out = weights_now("ab_out", cat)[0]
            y2 = _matmul(cat, w_ab_out, "nn", BF16, "mix_ab_out")
        else:
            w_lru_in_t = weights_now("lru_in", xin2)[0]
            p = _matmul(xin2, w_lru_in_t, "nt", F32, "lru_in")
            z_l, z_c, st["h_lat"], st["h_ctx"] = _lru_fwd(cfg, p, lru_consts, "lru_fwd")
            cat = jnp.concatenate([z_l, z_c], axis=0)
            w_lru_out = weights_now("lru_out", cat)[0]
            y2 = _matmul(cat, w_lru_out, "nn", BF16, "lru_out")
        h2, xhat2, rstd2, xin3 = _ln_fwd(cfg, h1, y2, mods[l], 5, 1.0, ln_g_f[l, 1][None], ln_b_f[l, 1][None],
                                          mods[l], (6, 7), "ln_fwd_b")
        wf[l][1] = weights_now("ffn%d1" % l, xin3)
        g3, u3, y3 = _ffn_fwd(xin3, wf[l][1], "ffn_fwd")
        if l == 0:
            h3, xhat3, rstd3, xin = _ln_fwd(cfg, h2, y3, mods[l], 8, 0.5, ln_g_f[l, 2][None], ln_b_f[l, 2][None],
                                            mods[1], (0, 1), "ln_fwd_a")
        else:
            h3, xhat3, rstd3 = _ln_fwd(cfg, h2, y3, mods[l], 8, 0.5, ln_g_f[l, 2][None], ln_b_f[l, 2][None],
                                       None, None, "ln_fwd_last")
        st.update(p=p, cat=cat, y2=y2, h2=h2, xhat2=xhat2, rstd2=rstd2, xin3=xin3, g3=g3, u3=u3, y3=y3,
                  xhat3=xhat3, rstd3=rstd3)
        saved.append(st)
        h = h3

    dy, loss_tile = _loss(cfg, h, loss_target.reshape(cfg.t_lat, D), "loss")
    loss = lax.psum(loss_tile[0, 0], ("x", "y", "c"))

    grads = {}
    dmod = [None, None]
    recv_ffn = [[None, None], [None, None]]
    dln_g = [[None] * 3, [None] * 3]
    dln_b = [[None] * 3, [None] * 3]

    def ffn_weight_grads(tag, xin_b, dg, du, a_act, dys):
        parts = [_matmul(dg, xin_b, "tn", BF16, "ffn_dw", bm_cap=1408, bk_cap=2304)[None],
                 _matmul(du, xin_b, "tn", BF16, "ffn_dw", bm_cap=1408, bk_cap=2304)[None],
                 _matmul(a_act, dys, "tn", BF16, "ffn_dw", bm_cap=1408, bk_cap=2304)[None]]
        return [_exchange_start(part, "exchange_start_ffn%s_%d" % (tag, k)) for k, part in enumerate(parts)]

    def pin(handles):
        total = handles[0][4][0, 0]
        for hd in handles[1:]:
            total = total + hd[4][0, 0]
        return total

    up = (dy,)
    dmod_next = None
    last_sent = None
    for l in (1, 0):
        st = saved[l]
        dm = [None] * N_MOD

        def put_stats(stats, gate_idx, nxt):
            dm[gate_idx] = stats[:, 2, :]
            if nxt is not None:
                nxt[0][nxt[1]] = stats[:, 4, :]
                nxt[0][nxt[1] + 1] = stats[:, 3, :]

        lng3 = ln_g_f[l, 2][None] if last_sent is None else ln_g_f[l, 2][None] + pin(last_sent)
        if len(up) > 1:
            up = (up[0], up[1], ln_b_f[l, 2][None], up[3], up[4])
        dres, dys, stats = _ln_bwd(cfg, up, st["xhat3"], st["rstd3"], st["y3"], mods[l], 8, 0.5,
                                   lng3, "ln_bwd_fused" if len(up) > 1 else "ln_bwd_last")
        put_stats(stats, 8, None if len(up) == 1 else (dmod_next, 0))
        dln_g[l][2], dln_b[l][2] = stats[:, 0, :].sum(0), stats[:, 1, :].sum(0)
        dg, du, a_act, dxin = _ffn_bwd(dys, st["g3"], st["u3"], wf[l][1], "ffn_bwd")
        recv_ffn[l][1] = ffn_weight_grads("%d1" % l, st["xin3"], dg, du, a_act, dys)
        dres, dys, stats = _ln_bwd(cfg, (dres, dxin, ln_b_f[l, 1][None], mods[l], 7), st["xhat2"], st["rstd2"], st["y2"],
                                   mods[l], 5, 1.0, ln_g_f[l, 1][None] + pin(recv_ffn[l][1]), "ln_bwd_fused")
        put_stats(stats, 5, (dm, 6))
        dln_g[l][1], dln_b[l][1] = stats[:, 0, :].sum(0), stats[:, 1, :].sum(0)
        if l == 0:
            dw_out = _matmul(st["cat"], dys, "tn", BF16, "mix_ab_dw_out")
            dcat = _matmul(dys, w_ab_out, "nt", F32, "mix_ab_dcat")
            dq, dk, dv, dqc, dkc, dvc, dsink = _attn_bwd(cfg, st["p"], dcat, cos, sin, sink_rows, "attn_bwd")
            du_l, dpw_l, dps_l = _pool_bwd(st["p"], pool_w[0], pool_scale, dcat, n_lat, 0, 2, "pool_bwd_lat")
            du_c, dpw_c, dps_c = _pool_bwd(st["p"], pool_w[0], pool_scale, dcat, n_ctx, cfg.ctx_blk, 2, "pool_bwd_ctx")
            dp = jnp.concatenate([jnp.concatenate([dq, dk, dv, du_l], axis=1),
                                  jnp.concatenate([dqc, dkc, dvc, du_c], axis=1)], axis=0)
            dw_in_t = _matmul(dp, st["xin2"], "tn", BF16, "mix_ab_dw_in", bm_cap=1280)
            dxin = _matmul(dp, w_ab_in_t, "nn", BF16, "mix_ab_dx")
            recv_mix = [_exchange_start(part, "exchange_start_mix_ab_%d" % k)
                        for k, part in enumerate((dw_in_t[None], dw_out[None], _as2d(dpw_l + dpw_c)[None]))]
            grads["attn_sink"] = (dsink[0, :, 0] + dsink[1, :, 0])[None, :]
            grads["pool_scale"] = dps_l + dps_c
        else:
            dw_out = _matmul(st["cat"], dys, "tn", BF16, "lru_dw_out")
            dz = _matmul(dys, w_lru_out, "nt", F32, "lru_dz")
            dgl, dul, dgc, duc, dwa, dwx, vec = _lru_bwd(cfg, st["p"], dz, st["h_lat"], st["h_ctx"], lru_consts, "lru_bwd")
            dp = jnp.concatenate([jnp.concatenate([dgl, dul], axis=1), jnp.concatenate([dgc, duc], axis=1)], axis=0)
            dw_in_t = _matmul(dp, st["xin2"], "tn", BF16, "lru_dw_in", bm_cap=1024)
            dxin = _matmul(dp, w_lru_in_t, "nn", BF16, "lru_dx")
            recv_mix = [_exchange_start(part, "exchange_start_lru_%d" % k)
                        for k, part in enumerate((dw_in_t[None], dw_out[None], _as2d(dwa)[None], _as2d(dwx)[None]))]
            vec_t = jnp.moveaxis(vec, 0, 1).reshape(16, D)
            grads["lru_ba"], grads["lru_bx"] = vec_t[0:2], vec_t[2:4]
            grads["lru_lambda"], grads["lru_conv_w"], grads["lru_conv_b"] = vec_t[4:6], vec_t[6:10], vec_t[10:11]
        if l == 0:
            recv_ab = recv_mix
        else:
            recv_lru = recv_mix
        dres, dys, stats = _ln_bwd(cfg, (dres, dxin, ln_b_f[l, 0][None], mods[l], 4), st["xhat1"], st["rstd1"], st["y1"],
                                   mods[l], 2, 0.5, ln_g_f[l, 0][None] + pin(recv_mix), "ln_bwd_fused")
        put_stats(stats, 2, (dm, 3))
        dln_g[l][0], dln_b[l][0] = stats[:, 0, :].sum(0), stats[:, 1, :].sum(0)
        dg, du, a_act, dxin = _ffn_bwd(dys, st["g1"], st["u1"], wf[l][0], "ffn_bwd")
        recv_ffn[l][0] = ffn_weight_grads("%d0" % l, st["xin1"], dg, du, a_act, dys)
        last_sent = recv_ffn[l][0]
        dmod[l] = dm
        dmod_next = dm
        up = (dres, dxin, None, mods[l], 1)
    dh0, stats = _modulate_bwd(cfg, up[0], up[1], h0, mods[0] + pin(last_sent), 1, "modulate_bwd")
    dmod[0][0], dmod[0][1] = stats[:, 4, :], stats[:, 3, :]
    grad_x = dh0.reshape(x.shape)

    def arrived(handle, name):
        return _exchange_wait(handle, dh0, name)

    recv_ffn = [[[arrived(hd, "exchange_wait_ffn%d%d_%d" % (l, i, k)) for k, hd in enumerate(recv_ffn[l][i])]
                 for i in range(2)] for l in range(2)]
    recv_ab = [arrived(hd, "exchange_wait_mix_ab_%d" % k) for k, hd in enumerate(recv_ab)]
    recv_lru = [arrived(hd, "exchange_wait_lru_%d" % k) for k, hd in enumerate(recv_lru)]

    dmod_mine = jnp.stack([jnp.stack(dmod[l], axis=1).reshape(3, N_MOD * D) for l in range(2)])
    n_dm = 6 * N_MOD * D // 128
    dmod_all = _all_gather(dmod_mine.reshape(n_dm, 128), "gather_dmod", True)
    dmod_sum = _sum_blocks(dmod_all, "sum_dmod").reshape(2, 3, N_MOD * D)
    dmod_all = dmod_all.reshape(N_DEV, 2, 3, N_MOD * D)
    grads["b_mod"] = dmod_sum[:, 0] + dmod_sum[:, 1] + dmod_sum[:, 2]
    dmod_ex = jnp.moveaxis(dmod_all[:, :, 0:2, :], 1, 0).reshape(2, 2 * N_DEV, N_MOD * D)
    dm_rows = jnp.zeros((2, 32, N_MOD * D), F32).at[:, :16].set(dmod_ex).at[:, 16].set(dmod_sum[:, 2])
    dm_cols = lax.dynamic_slice_in_dim(dm_rows, me * mcols, mcols, axis=2).astype(BF16)
    grads["w_mod"] = jnp.stack([_matmul(s_rows, dm_cols[l], "tn", F32, "mod_dw", bn_cap=1280) for l in range(2)])
    ds_part = None
    for l in range(2):
        part = _matmul(dm_cols[l, 16:32], w_mod[l], "nt", F32, "mod_ds", bk_cap=1280)[0]
        ds_part = part if ds_part is None else ds_part + part

    def shard_sum(recv, name):
        return _sum_blocks(recv.reshape(N_DEV, recv.shape[2], recv.shape[3]), name)

    gate_g = [[None, None], [None, None]]
    up_g = [[None, None], [None, None]]
    down_g = [[None, None], [None, None]]
    for l in range(2):
        for i in range(2):
            gt, ut, dn = [shard_sum(r, "sum_ffn") for r in recv_ffn[l][i]]
            gate_g[l][i], up_g[l][i], down_g[l][i] = gt.T, ut.T, dn
    grads["ffn_w_gate"] = jnp.stack([jnp.stack(gate_g[l]) for l in range(2)])
    grads["ffn_w_up"] = jnp.stack([jnp.stack(up_g[l]) for l in range(2)])
    grads["ffn_w_down"] = jnp.stack([jnp.stack(down_g[l]) for l in range(2)])
    grads["mix_ab_w_in"] = shard_sum(recv_ab[0], "sum_mix_in").T[None]
    grads["mix_ab_w_out"] = shard_sum(recv_ab[1], "sum_mix_out")[None]
    grads["lru_w_in"] = shard_sum(recv_lru[0], "sum_lru_in").T[None]
    grads["lru_w_out"] = shard_sum(recv_lru[1], "sum_lru_out")[None]
    rep_parts = [shard_sum(recv_lru[2], "sum_rep"), shard_sum(recv_lru[3], "sum_rep"), shard_sum(recv_ab[2], "sum_rep")]
    rep_names = ["lru_wa", "lru_wx", "pool_w"]

    dln_g_f = jnp.stack([jnp.stack(dln_g[l]) for l in range(2)])
    dln_b_f = jnp.stack([jnp.stack(dln_b[l]) for l in range(2)])
    sink_pad = jnp.zeros((1, 128), F32).at[0, :8].set(grads["attn_sink"][0])
    part_list = [p_.reshape(-1, 128) for p_ in rep_parts] + [
        dln_g_f.reshape(-1, 128), dln_b_f.reshape(-1, 128), grads["lru_conv_w"].reshape(-1, 128),
        grads["lru_conv_b"].reshape(-1, 128), grads["lru_ba"].reshape(-1, 128), grads["lru_bx"].reshape(-1, 128),
        grads["lru_lambda"].reshape(-1, 128), ds_part.reshape(-1, 128), sink_pad, grads["pool_scale"].reshape(-1, 128)]
    parts, part_off = _pack_rows(part_list)
    parts_all = _all_gather(parts, "gather_partials", True)
    parts_sum = _sum_blocks(parts_all, "sum_partials")

    for i, n in enumerate(rep_names):
        rows = part_list[i].shape[0]
        grads[n] = parts_all[:, part_off[i]:part_off[i] + rows, :].reshape(weights[n].shape)

    def take(idx):
        return parts_sum[part_off[idx]:part_off[idx] + part_list[idx].shape[0]]

    def my_cols(full, shp):
        w = shp[-1]
        return lax.dynamic_slice_in_dim(full, me * w, w, axis=full.ndim - 1)

    grads["ln_g"] = my_cols(take(3).reshape(2, 3, D), ln_g.shape)
    grads["ln_b"] = my_cols(take(4).reshape(2, 3, D), ln_b.shape)
    grads["lru_conv_w"] = my_cols(take(5).reshape(1, 4, D), lru_conv_w.shape)
    grads["lru_conv_b"] = my_cols(take(6).reshape(1, D), lru_conv_b.shape)
    grads["lru_ba"] = my_cols(take(7).reshape(1, 2, D), lru_ba.shape)
    grads["lru_bx"] = my_cols(take(8).reshape(1, 2, D), lru_bx.shape)
    grads["lru_lambda"] = my_cols(take(9).reshape(1, 2, D), lru_lambda.shape)
    sg = jax.nn.sigmoid(c_ctx)
    grads["c_ctx"] = take(10).reshape(D) * (sg * (1.0 + c_ctx * (1.0 - sg)))
    grads["attn_sink"] = take(11)[:, :8]
    grads["pool_scale"] = take(12).reshape(pool_scale.shape)

    delta, new_m, new_v = {}, {}, {}
    for n in names:
        shp = weights[n].shape
        grads[n] = grads[n].reshape(shp)
        delta[n], new_m[n], new_v[n] = _adamw(weights[n], grads[n], mom_m[n], mom_v[n], "adamw")

    return (loss, grad_x, *[grads[n] for n in names], *[delta[n] for n in names],
            *[new_m[n] for n in names], *[new_v[n] for n in names])
```

```python
import functools
import math

import jax
import jax.numpy as jnp
from jax import lax
from jax.experimental import pallas as pl
from jax.experimental.pallas import tpu as pltpu

F32 = jnp.float32
BF16 = jnp.bfloat16
MESH = pl.DeviceIdType.MESH

D = 1024
N_MOD = 9
N_DEV = 8
HEAD_DIM = 64
ATT_HEADS = 8
KV_HEADS = 2
ATT_W = 512
BLK = 128
ATT_SCALE = HEAD_DIM ** -0.5
GRID_W = 64
ROPE_FREQS = HEAD_DIM // 4
ROPE_THETA = 10000.0
POOL_R = (1, 2, 4, 8)
LRU_C = 8.0
LN_EPS = 1e-5
NEG_INF = -1e30
ALPHA = 4.0 ** 0.25
LR, B1, B2, EPS, WD, STEP = 0.001, 0.9, 0.999, 1e-08, 0.01, 10
VMEM_LIMIT = 56 * 1024 * 1024
ROW_TILE = 512


def _params(sem=None):
    if sem is None:
        return pltpu.CompilerParams(vmem_limit_bytes=VMEM_LIMIT)
    return pltpu.CompilerParams(dimension_semantics=sem, vmem_limit_bytes=VMEM_LIMIT)


def _sigmoid(x):
    return 0.5 * jnp.tanh(0.5 * x) + 0.5


def _dot(a, b):
    return jnp.dot(a.astype(BF16), b.astype(BF16), preferred_element_type=F32)


def _dot_nt(a, b):
    return lax.dot_general(a.astype(BF16), b.astype(BF16), (((1,), (1,)), ((), ())), preferred_element_type=F32)


def _dot_tn(a, b):
    return lax.dot_general(a.astype(BF16), b.astype(BF16), (((0,), (0,)), ((), ())), preferred_element_type=F32)


def _pick(n, cap):
    best = None
    for m in range(128, min(n, cap) + 1, 128):
        if n % m == 0:
            best = m
    return n if best is None else best


def _chunks(width, step=256):
    out, c = [], 0
    while c < width:
        w = min(step, width - c)
        out.append((c, w))
        c += w
    return out


class _Cfg:
    def __init__(self, n_lat, n_ctx):
        self.n_lat, self.n_ctx = n_lat, n_ctx
        self.t_lat, self.t_ctx = 2 * n_lat, 2 * n_ctx
        self.T = self.t_lat + self.t_ctx
        self.tm = min(ROW_TILE, self.t_ctx)
        assert n_lat % self.tm == 0 and self.t_ctx % self.tm == 0 and n_lat >= 3 * BLK and n_ctx % BLK == 0
        self.nt = self.T // self.tm
        self.nlt = n_lat // self.tm
        self.ctx_blk = self.t_lat // n_ctx

    def seg(self, i):
        return jnp.minimum(i // self.nlt, 2)

    def first_of_seg(self, i):
        return jnp.where(i < 2 * self.nlt, i % self.nlt == 0, i == 2 * self.nlt)


def _modulate(cfg, h, mod, shift_idx, scale_idx, name):
    tm = cfg.tm

    def body(h_ref, mod_ref, o_ref):
        sh = mod_ref[shift_idx:shift_idx + 1, :]
        sc = mod_ref[scale_idx:scale_idx + 1, :]
        o_ref[...] = (h_ref[...] * (1.0 + sc) + sh).astype(BF16)

    return pl.pallas_call(
        body, grid=(cfg.nt,), name=name,
        in_specs=[pl.BlockSpec((tm, D), lambda i: (i, 0)),
                  pl.BlockSpec((None, N_MOD, D), lambda i: (cfg.seg(i), 0, 0))],
        out_specs=pl.BlockSpec((tm, D), lambda i: (i, 0)),
        out_shape=jax.ShapeDtypeStruct((cfg.T, D), BF16),
        compiler_params=_params(("parallel",)),
    )(h, mod)


def _ln_fwd(cfg, h, y, mod, gate_idx, coef, lng, lnb, mod_next, next_idx, name):
    tm = cfg.tm
    has_next = next_idx is not None

    def body(*refs):
        if has_next:
            h_ref, y_ref, mod_ref, g_ref, b_ref, modn_ref, hn_ref, xhat_ref, rstd_ref, xin_ref = refs
        else:
            h_ref, y_ref, mod_ref, g_ref, b_ref, hn_ref, xhat_ref, rstd_ref = refs
        gate = mod_ref[gate_idx:gate_idx + 1, :]
        z = ALPHA * h_ref[...] + (coef * gate) * y_ref[...].astype(F32)
        mu = jnp.mean(z, axis=-1, keepdims=True)
        zc = z - mu
        var = jnp.mean(zc * zc, axis=-1, keepdims=True)
        rstd = lax.rsqrt(var + LN_EPS)
        xhat = zc * rstd
        hn = xhat * g_ref[...] + b_ref[...]
        hn_ref[...] = hn
        xhat_ref[...] = xhat.astype(BF16)
        rstd_ref[...] = rstd
        if has_next:
            sh = modn_ref[next_idx[0]:next_idx[0] + 1, :]
            sc = modn_ref[next_idx[1]:next_idx[1] + 1, :]
            xin_ref[...] = (hn * (1.0 + sc) + sh).astype(BF16)

    row = pl.BlockSpec((tm, D), lambda i: (i, 0))
    modspec = pl.BlockSpec((None, N_MOD, D), lambda i: (cfg.seg(i), 0, 0))
    vec = pl.BlockSpec((1, D), lambda i: (0, 0))
    in_specs = [row, row, modspec, vec, vec]
    args = [h, y, mod, lng, lnb]
    out_specs = [row, row, pl.BlockSpec((tm, 1), lambda i: (i, 0))]
    out_shape = [jax.ShapeDtypeStruct((cfg.T, D), F32), jax.ShapeDtypeStruct((cfg.T, D), BF16),
                 jax.ShapeDtypeStruct((cfg.T, 1), F32)]
    if has_next:
        in_specs.append(modspec)
        args.append(mod_next)
        out_specs.append(row)
        out_shape.append(jax.ShapeDtypeStruct((cfg.T, D), BF16))
    return pl.pallas_call(body, grid=(cfg.nt,), name=name, in_specs=in_specs, out_specs=out_specs,
                          out_shape=out_shape, compiler_params=_params(("parallel",)))(*args)


def _ln_bwd(cfg, up, xhat, rstd, y, mod, gate_idx, coef, lng, name):
    tm = cfg.tm
    fused = len(up) > 1
    scale_next = up[4] if fused else None

    def body(*refs):
        if fused:
            dres_n, dxin_n, b_ref, modn_ref, xhat_ref, rstd_ref, y_ref, mod_ref, g_ref, dres_ref, dys_ref, st_ref = refs
        else:
            dhn_ref, xhat_ref, rstd_ref, y_ref, mod_ref, g_ref, dres_ref, dys_ref, st_ref = refs
        i = pl.program_id(0)

        @pl.when(cfg.first_of_seg(i))
        def _():
            st_ref[...] = jnp.zeros_like(st_ref)

        xhat = xhat_ref[...].astype(F32)
        if fused:
            dxin = dxin_n[...].astype(F32)
            sc = modn_ref[scale_next:scale_next + 1, :]
            dhn = dres_n[...] + dxin * (1.0 + sc)
            shift_sum = jnp.sum(dxin, axis=0, keepdims=True)
            st_ref[3:4, :] += g_ref[...] * jnp.sum(dxin * xhat, axis=0, keepdims=True) + b_ref[...] * shift_sum
            st_ref[4:5, :] += shift_sum
        else:
            dhn = dhn_ref[...]
        gdh = dhn * g_ref[...]
        m1 = jnp.mean(gdh, axis=-1, keepdims=True)
        m2 = jnp.mean(gdh * xhat, axis=-1, keepdims=True)
        dz = rstd_ref[...] * (gdh - m1 - xhat * m2)
        gate = mod_ref[gate_idx:gate_idx + 1, :]
        dres_ref[...] = ALPHA * dz
        dys_ref[...] = ((coef * gate) * dz).astype(BF16)
        st_ref[0:1, :] += jnp.sum(dhn * xhat, axis=0, keepdims=True)
        st_ref[1:2, :] += jnp.sum(dhn, axis=0, keepdims=True)
        st_ref[2:3, :] += jnp.sum((coef * dz) * y_ref[...].astype(F32), axis=0, keepdims=True)

    row = pl.BlockSpec((tm, D), lambda i: (i, 0))
    modspec = pl.BlockSpec((None, N_MOD, D), lambda i: (cfg.seg(i), 0, 0))
    vec = pl.BlockSpec((1, D), lambda i: (0, 0))
    col = pl.BlockSpec((tm, 1), lambda i: (i, 0))
    if fused:
        in_specs = [row, row, vec, modspec, row, col, row, modspec, vec]
        args = [up[0], up[1], up[2], up[3], xhat, rstd, y, mod, lng]
    else:
        in_specs = [row, row, col, row, modspec, vec]
        args = [up[0], xhat, rstd, y, mod, lng]
    return pl.pallas_call(
        body, grid=(cfg.nt,), name=name, in_specs=in_specs,
        out_specs=[row, row, pl.BlockSpec((None, 8, D), lambda i: (cfg.seg(i), 0, 0))],
        out_shape=[jax.ShapeDtypeStruct((cfg.T, D), F32), jax.ShapeDtypeStruct((cfg.T, D), BF16),
                   jax.ShapeDtypeStruct((3, 8, D), F32)],
        compiler_params=_params(("arbitrary",)))(*args)


def _modulate_bwd(cfg, dres, dxin, h, mod, scale_idx, name):
    tm = cfg.tm
    n_lt = 2 * cfg.nlt

    def body(dres_ref, dxin_ref, h_ref, mod_ref, dh_ref, st_ref):
        i = pl.program_id(0)

        @pl.when(cfg.first_of_seg(i))
        def _():
            st_ref[...] = jnp.zeros_like(st_ref)

        dxin = dxin_ref[...].astype(F32)
        sc = mod_ref[scale_idx:scale_idx + 1, :]

        @pl.when(i < n_lt)
        def _():
            dh_ref[...] = dres_ref[...] + dxin * (1.0 + sc)

        st_ref[3:4, :] += jnp.sum(dxin * h_ref[...], axis=0, keepdims=True)
        st_ref[4:5, :] += jnp.sum(dxin, axis=0, keepdims=True)

    row = pl.BlockSpec((tm, D), lambda i: (i, 0))
    return pl.pallas_call(
        body, grid=(cfg.nt,), name=name,
        in_specs=[row, row, row, pl.BlockSpec((None, N_MOD, D), lambda i: (cfg.seg(i), 0, 0))],
        out_specs=[pl.BlockSpec((tm, D), lambda i: (jnp.minimum(i, n_lt - 1), 0)),
                   pl.BlockSpec((None, 8, D), lambda i: (cfg.seg(i), 0, 0))],
        out_shape=[jax.ShapeDtypeStruct((cfg.t_lat, D), F32), jax.ShapeDtypeStruct((3, 8, D), F32)],
        compiler_params=_params(("arbitrary",)))(dres, dxin, h, mod)


def _loss(cfg, h, target, name):
    tm = cfg.tm
    n_lt = 2 * cfg.nlt

    def body(h_ref, t_ref, dy_ref, l_ref):
        i = pl.program_id(0)

        @pl.when(i == 0)
        def _():
            l_ref[...] = jnp.zeros_like(l_ref)

        @pl.when(i < n_lt)
        def _():
            err = h_ref[...] - t_ref[...]
            dy_ref[...] = err * (1.0 / D)
            part = jnp.sum(jnp.sum(err * err, axis=1, keepdims=True), axis=0, keepdims=True) * (0.5 / D)
            l_ref[...] += jnp.broadcast_to(part, l_ref.shape)

        @pl.when(i >= n_lt)
        def _():
            dy_ref[...] = jnp.zeros_like(dy_ref)

    return pl.pallas_call(
        body, grid=(cfg.nt,), name=name,
        in_specs=[pl.BlockSpec((tm, D), lambda i: (i, 0)),
                  pl.BlockSpec((tm, D), lambda i: (jnp.minimum(i, n_lt - 1), 0))],
        out_specs=[pl.BlockSpec((tm, D), lambda i: (i, 0)), pl.BlockSpec((8, 128), lambda i: (0, 0))],
        out_shape=[jax.ShapeDtypeStruct((cfg.T, D), F32), jax.ShapeDtypeStruct((8, 128), F32)],
        compiler_params=_params(("arbitrary",)))(h, target)


def _matmul(a, b, mode, out_dtype, name, bm_cap=512, bn_cap=1408, bk_cap=1024):
    if mode == "nn":
        (M, K), N = a.shape, b.shape[1]
    elif mode == "nt":
        (M, K), N = a.shape, b.shape[0]
    else:
        (K, M), N = a.shape, b.shape[1]
    bm, bn, bk = _pick(M, bm_cap), _pick(N, bn_cap), _pick(K, bk_cap)
    nk = K // bk

    def body(a_ref, b_ref, o_ref, acc_ref=None):
        k = pl.program_id(2)
        if mode == "nn":
            part = _dot(a_ref[...], b_ref[...])
        elif mode == "nt":
            part = _dot_nt(a_ref[...], b_ref[...])
        else:
            part = _dot_tn(a_ref[...], b_ref[...])
        if nk == 1:
            o_ref[...] = part.astype(out_dtype)
            return

        @pl.when(k == 0)
        def _():
            acc_ref[...] = part

        @pl.when((k > 0) & (k < nk - 1))
        def _():
            acc_ref[...] += part

        @pl.when(k == nk - 1)
        def _():
            o_ref[...] = (acc_ref[...] + part).astype(out_dtype)

    if mode == "nn":
        a_spec = pl.BlockSpec((bm, bk), lambda i, j, k: (i, k))
        b_spec = pl.BlockSpec((bk, bn), lambda i, j, k: (k, j))
    elif mode == "nt":
        a_spec = pl.BlockSpec((bm, bk), lambda i, j, k: (i, k))
        b_spec = pl.BlockSpec((bn, bk), lambda i, j, k: (j, k))
    else:
        a_spec = pl.BlockSpec((bk, bm), lambda i, j, k: (k, i))
        b_spec = pl.BlockSpec((bk, bn), lambda i, j, k: (k, j))
    return pl.pallas_call(
        body, grid=(M // bm, N // bn, nk), name=name, in_specs=[a_spec, b_spec],
        out_specs=pl.BlockSpec((bm, bn), lambda i, j, k: (i, j)),
        out_shape=jax.ShapeDtypeStruct((M, N), out_dtype),
        scratch_shapes=[pltpu.VMEM((bm, bn), F32)] if nk > 1 else [],
        compiler_params=_params(("parallel", "parallel", "arbitrary")))(a, b)


def _ffn_tile(T, cap):
    best = 256
    for t in range(256, cap + 1, 256):
        if T % t == 0:
            best = t
    return best


def _ffn_fwd(xin, wf, name):
    T = xin.shape[0]
    F = wf.shape[1]
    tm, tf = _ffn_tile(T, 768), F // 2
    assert tf % 128 == 0 and T % tm == 0

    def body(x_ref, wg_ref, wu_ref, wd_ref, g_ref, u_ref, y_ref, acc_ref):
        j = pl.program_id(1)
        x = x_ref[...]
        acc = None
        for c0, cw in _chunks(tf):
            g = _dot_nt(x, wg_ref[c0:c0 + cw, :])
            u = _dot_nt(x, wu_ref[c0:c0 + cw, :])
            g_ref[:, c0:c0 + cw] = g.astype(BF16)
            u_ref[:, c0:c0 + cw] = u.astype(BF16)
            part = _dot(g * _sigmoid(g) * u, wd_ref[c0:c0 + cw, :])
            acc = part if acc is None else acc + part

        @pl.when(j == 0)
        def _():
            acc_ref[...] = acc

        @pl.when(j == 1)
        def _():
            y_ref[...] = (acc_ref[...] + acc).astype(BF16)

    return pl.pallas_call(
        body, grid=(T // tm, 2), name=name,
        in_specs=[pl.BlockSpec((tm, D), lambda i, j: (i, 0)),
                  pl.BlockSpec((None, tf, D), lambda i, j: (0, j, 0)),
                  pl.BlockSpec((None, tf, D), lambda i, j: (1, j, 0)),
                  pl.BlockSpec((None, tf, D), lambda i, j: (2, j, 0))],
        out_specs=[pl.BlockSpec((tm, tf), lambda i, j: (i, j)),
                   pl.BlockSpec((tm, tf), lambda i, j: (i, j)),
                   pl.BlockSpec((tm, D), lambda i, j: (i, 0))],
        out_shape=[jax.ShapeDtypeStruct((T, F), BF16), jax.ShapeDtypeStruct((T, F), BF16),
                   jax.ShapeDtypeStruct((T, D), BF16)],
        scratch_shapes=[pltpu.VMEM((tm, D), F32)],
        compiler_params=_params(("parallel", "arbitrary")))(xin, wf, wf, wf)


def _ffn_bwd(dys, g, u, wf, name):
    T = dys.shape[0]
    F = wf.shape[1]
    tm, tf = _ffn_tile(T, 512), F // 2

    def body(dy_ref, g_ref, u_ref, wg_ref, wu_ref, wd_ref, dg_ref, du_ref, a_ref, dx_ref, acc_ref):
        j = pl.program_id(1)
        da_all = _dot_nt(dy_ref[...], wd_ref[...])
        for c0, cw in _chunks(tf):
            gg = g_ref[:, c0:c0 + cw].astype(F32)
            uu = u_ref[:, c0:c0 + cw].astype(F32)
            da = da_all[:, c0:c0 + cw]
            s = _sigmoid(gg)
            silu = gg * s
            a_ref[:, c0:c0 + cw] = (silu * uu).astype(BF16)
            du_ref[:, c0:c0 + cw] = (da * silu).astype(BF16)
            dg_ref[:, c0:c0 + cw] = (da * uu * (s * (1.0 + gg * (1.0 - s)))).astype(BF16)
        acc = _dot(dg_ref[...], wg_ref[...]) + _dot(du_ref[...], wu_ref[...])

        @pl.when(j == 0)
        def _():
            acc_ref[...] = acc

        @pl.when(j == 1)
        def _():
            dx_ref[...] = (acc_ref[...] + acc).astype(BF16)

    blk = pl.BlockSpec((tm, tf), lambda i, j: (i, j))
    return pl.pallas_call(
        body, grid=(T // tm, 2), name=name,
        in_specs=[pl.BlockSpec((tm, D), lambda i, j: (i, 0)), blk, blk,
                  pl.BlockSpec((None, tf, D), lambda i, j: (0, j, 0)),
                  pl.BlockSpec((None, tf, D), lambda i, j: (1, j, 0)),
                  pl.BlockSpec((None, tf, D), lambda i, j: (2, j, 0))],
        out_specs=[blk, blk, blk, pl.BlockSpec((tm, D), lambda i, j: (i, 0))],
        out_shape=[jax.ShapeDtypeStruct((T, F), BF16), jax.ShapeDtypeStruct((T, F), BF16),
                   jax.ShapeDtypeStruct((T, F), BF16), jax.ShapeDtypeStruct((T, D), BF16)],
        scratch_shapes=[pltpu.VMEM((tm, D), F32)],
        compiler_params=_params(("parallel", "arbitrary")))(dys, g, u, wf, wf, wf)


def _swap_halves(x):
    w = x.shape[1]
    lane = lax.broadcasted_iota(jnp.int32, (1, w), 1)
    return jnp.where((lane & 63) < 32, pltpu.roll(x, w - 32, 1), pltpu.roll(x, 32, 1))


def _rope(x, cos, sin):
    return x * cos + _swap_halves(x) * sin


def _rope_t(dy, cos, sin):
    return dy * cos + _swap_halves(dy * sin)


def _rope_tables(n_lat):
    rows = n_lat // GRID_W
    row = jnp.repeat(jnp.arange(rows, dtype=F32), GRID_W)
    col = jnp.tile(jnp.arange(GRID_W, dtype=F32), rows)
    inv = ROPE_THETA ** (-jnp.arange(ROPE_FREQS, dtype=F32) / ROPE_FREQS)
    ang = jnp.concatenate([row[:, None] * inv, col[:, None] * inv], axis=-1)
    cs, sn = jnp.cos(ang), jnp.sin(ang)
    cos = jnp.concatenate([cs, cs, cs, cs], axis=-1)
    sin = jnp.concatenate([-sn, sn, -sn, sn], axis=-1)
    return cos, sin


def _attn_specs(cfg):
    n_lat, n_ctx, cb = cfg.n_lat, cfg.n_ctx, cfg.ctx_blk
    return [pl.BlockSpec((n_lat, ATT_W), lambda e: (e, 0)),
            pl.BlockSpec((n_lat, 128), lambda e: (e, 4)),
            pl.BlockSpec((n_lat, 128), lambda e: (e, 5)),
            pl.BlockSpec((n_ctx, ATT_W), lambda e: (cb + e, 0)),
            pl.BlockSpec((n_ctx, 128), lambda e: (cb + e, 4)),
            pl.BlockSpec((n_ctx, 128), lambda e: (cb + e, 5)),
            pl.BlockSpec((n_lat, 128), lambda e: (0, 0)),
            pl.BlockSpec((n_lat, 128), lambda e: (0, 0)),
            pl.BlockSpec((8, 128), lambda e: (0, 0))]


def _attn_prepare(kh, kl, vl, kc, vc, ka, kb, va, vb, kca, kcb, vca, vcb):
    lane = lax.broadcasted_iota(jnp.int32, (1, 128), 1)
    own = (lane < 64) if kh == 0 else (lane >= 64)

    def split(x, ra, rb):
        mine = jnp.where(own, x, 0.0)
        other = pltpu.roll(mine, 64, 1)
        a, b = (mine, other) if kh == 0 else (other, mine)
        ra[...] = a.astype(BF16)
        rb[...] = b.astype(BF16)

    split(kl, ka, kb)
    split(vl, va, vb)
    split(kc, kca, kcb)
    split(vc, vca, vcb)


def _softmax_parts(s_list, sk):
    m = sk
    for s in s_list:
        m = jnp.maximum(m, jnp.max(s, axis=1, keepdims=True))
    es = [jnp.exp(s - m) for s in s_list]
    esk = jnp.exp(sk - m)
    den = esk
    for e in es:
        den = den + jnp.sum(e, axis=1, keepdims=True)
    inv = 1.0 / den
    return [e * inv for e in es], esk * inv


def _window(cfg, n):
    r0 = pl.multiple_of(n * BLK, BLK)
    start = pl.multiple_of(jnp.clip((n - 1) * BLK, 0, cfg.n_lat - 3 * BLK), BLK)
    qpos = r0 + lax.broadcasted_iota(jnp.int32, (BLK, 1), 0)
    kpos = start + lax.broadcasted_iota(jnp.int32, (1, 3 * BLK), 1)
    valid = jnp.abs(qpos - kpos) <= BLK
    return r0, start, valid


def _attn_fwd(cfg, p, cos, sin, sink_rows, name):
    n_lat, n_ctx = cfg.n_lat, cfg.n_ctx

    def body(q_ref, k_ref, v_ref, qc_ref, kc_ref, vc_ref, cos_ref, sin_ref, sink_ref, o_ref, oc_ref,
             qr, ka, kb, va, vb, kca, kcb, vca, vcb):
        cos_t, sin_t = cos_ref[...], sin_ref[...]
        for gq in range(4):
            qr[:, gq * 128:(gq + 1) * 128] = _rope(q_ref[:, gq * 128:(gq + 1) * 128], cos_t, sin_t).astype(BF16)
        kl = _rope(k_ref[...], cos_t, sin_t)
        for kh in range(KV_HEADS):
            _attn_prepare(kh, kl, v_ref[...], kc_ref[...], vc_ref[...], ka, kb, va, vb, kca, kcb, vca, vcb)

            def lat_block(n, carry):
                r0, start, valid = _window(cfg, n)
                win = pl.ds(start, 3 * BLK)
                for pr in range(2):
                    lanes = slice((kh * 2 + pr) * 128, (kh * 2 + pr + 1) * 128)
                    qp = qr[pl.ds(r0, BLK), lanes]
                    o = None
                    for half, (kw, kcx, vw, vcx) in enumerate(((ka, kca, va, vca), (kb, kcb, vb, vcb))):
                        head = kh * 4 + pr * 2 + half
                        s_w = jnp.where(valid, _dot_nt(qp, kw[win, :]) * ATT_SCALE, NEG_INF)
                        s_c = _dot_nt(qp, kcx[...]) * ATT_SCALE
                        (p_w, p_c), _ = _softmax_parts([s_w, s_c], sink_ref[head:head + 1, 0:1])
                        part = _dot(p_w, vw[win, :]) + _dot(p_c, vcx[...])
                        o = part if o is None else o + part
                    o_ref[pl.ds(r0, BLK), lanes] = o.astype(BF16)
                return carry

            lax.fori_loop(0, n_lat // BLK, lat_block, 0, unroll=2)
            for n in range(n_ctx // BLK):
                rows = slice(n * BLK, (n + 1) * BLK)
                for pr in range(2):
                    lanes = slice((kh * 2 + pr) * 128, (kh * 2 + pr + 1) * 128)
                    qp = qc_ref[rows, lanes]
                    o = None
                    for half, (kcx, vcx) in enumerate(((kca, vca), (kcb, vcb))):
                        head = kh * 4 + pr * 2 + half
                        s_c = _dot_nt(qp, kcx[...]) * ATT_SCALE
                        (p_c,), _ = _softmax_parts([s_c], sink_ref[head:head + 1, 0:1])
                        part = _dot(p_c, vcx[...])
                        o = part if o is None else o + part
                    oc_ref[rows, lanes] = o.astype(BF16)

    return pl.pallas_call(
        body, grid=(2,), name=name, in_specs=_attn_specs(cfg),
        out_specs=[pl.BlockSpec((n_lat, ATT_W), lambda e: (e, 0)), pl.BlockSpec((n_ctx, ATT_W), lambda e: (e, 0))],
        out_shape=[jax.ShapeDtypeStruct((cfg.t_lat, ATT_W), BF16), jax.ShapeDtypeStruct((cfg.t_ctx, ATT_W), BF16)],
        scratch_shapes=[pltpu.VMEM((n_lat, ATT_W), BF16)] + [pltpu.VMEM((n_lat, 128), BF16)] * 4
        + [pltpu.VMEM((n_ctx, 128), BF16)] * 4,
        compiler_params=_params(("parallel",)))(p, p, p, p, p, p, cos, sin, sink_rows)


def _attn_bwd(cfg, p, dcat, cos, sin, sink_rows, name):
    n_lat, n_ctx, cb = cfg.n_lat, cfg.n_ctx, cfg.ctx_blk

    def body(q_ref, k_ref, v_ref, qc_ref, kc_ref, vc_ref, cos_ref, sin_ref, sink_ref, do_ref, doc_ref,
             dq_ref, dk_ref, dv_ref, dqc_ref, dkc_ref, dvc_ref, dsink_ref,
             qr, ka, kb, va, vb, kca, kcb, vca, vcb, dqs, dka, dva, dkca, dvca):
        cos_t, sin_t = cos_ref[...], sin_ref[...]
        lane = lax.broadcasted_iota(jnp.int32, (1, 128), 1)
        lo = lane < 64
        for gq in range(4):
            qr[:, gq * 128:(gq + 1) * 128] = _rope(q_ref[:, gq * 128:(gq + 1) * 128], cos_t, sin_t).astype(BF16)
        kl = _rope(k_ref[...], cos_t, sin_t)
        dsink_ref[...] = jnp.zeros_like(dsink_ref)
        dka[...] = jnp.zeros_like(dka)
        dva[...] = jnp.zeros_like(dva)
        dkca[...] = jnp.zeros_like(dkca)
        dvca[...] = jnp.zeros_like(dvca)

        def halves(x):
            return jnp.where(lo, x, 0).astype(BF16), jnp.where(lo, 0, x).astype(BF16)

        for kh in range(KV_HEADS):
            _attn_prepare(kh, kl, v_ref[...], kc_ref[...], vc_ref[...], ka, kb, va, vb, kca, kcb, vca, vcb)

            def one_head(head, qp, q_half, do_p, do_half, kw, kcx, vw, vcx, win, valid):
                sk = sink_ref[head:head + 1, 0:1]
                s_list = [_dot_nt(qp, kcx[...]) * ATT_SCALE]
                if win is not None:
                    s_list.insert(0, jnp.where(valid, _dot_nt(qp, kw[win, :]) * ATT_SCALE, NEG_INF))
                probs, p_sink = _softmax_parts(s_list, sk)
                vals = [vcx[...]] if win is None else [vw[win, :], vcx[...]]
                dps = [_dot_nt(do_p, vv) for vv in vals]
                dr = None
                for pp, dp in zip(probs, dps):
                    t = jnp.sum(pp * dp, axis=1, keepdims=True)
                    dr = t if dr is None else dr + t
                dss = [(pp * (dp - dr) * ATT_SCALE).astype(BF16) for pp, dp in zip(probs, dps)]
                dsink_ref[head:head + 1, :] += jnp.broadcast_to(
                    jnp.sum(-p_sink * dr, axis=0, keepdims=True), (1, 128))
                p_c, ds_c = probs[-1], dss[-1]
                dq = _dot(ds_c, kcx[...])
                dkca[kh] += _dot_tn(ds_c, q_half)
                dvca[kh] += _dot_tn(p_c, do_half)
                if win is not None:
                    dq = dq + _dot(dss[0], kw[win, :])
                    dka[kh, win, :] += _dot_tn(dss[0], q_half)
                    dva[kh, win, :] += _dot_tn(probs[0], do_half)
                return dq

            def lat_block(n, carry):
                r0, start, valid = _window(cfg, n)
                win = pl.ds(start, 3 * BLK)
                for pr in range(2):
                    lanes = slice((kh * 2 + pr) * 128, (kh * 2 + pr + 1) * 128)
                    qp = qr[pl.ds(r0, BLK), lanes]
                    do_p = do_ref[pl.ds(r0, BLK), lanes]
                    q_h, do_h = halves(qp), halves(do_p)
                    dq = None
                    for half, (kw, kcx, vw, vcx) in enumerate(((ka, kca, va, vca), (kb, kcb, vb, vcb))):
                        part = one_head(kh * 4 + pr * 2 + half, qp, q_h[half], do_p, do_h[half],
                                        kw, kcx, vw, vcx, win, valid)
                        dq = part if dq is None else dq + part
                    dqs[pl.ds(r0, BLK), lanes] = dq
                return carry

            lax.fori_loop(0, n_lat // BLK, lat_block, 0, unroll=2)
            for n in range(n_ctx // BLK):
                rows = slice(n * BLK, (n + 1) * BLK)
                for pr in range(2):
                    lanes = slice((kh * 2 + pr) * 128, (kh * 2 + pr + 1) * 128)
                    qp = qc_ref[rows, lanes].astype(BF16)
                    do_p = doc_ref[rows, lanes]
                    q_h, do_h = halves(qp), halves(do_p)
                    dq = None
                    for half, (kcx, vcx) in enumerate(((kca, vca), (kcb, vcb))):
                        part = one_head(kh * 4 + pr * 2 + half, qp, q_h[half], do_p, do_h[half],
                                        None, kcx, None, vcx, None, None)
                        dq = part if dq is None else dq + part
                    dqc_ref[rows, lanes] = dq.astype(BF16)

        def fold(acc):
            r0 = acc[0] + pltpu.roll(acc[0], 64, 1)
            r1 = acc[1] + pltpu.roll(acc[1], 64, 1)
            return jnp.where(lo, r0, r1)

        for gq in range(4):
            sl = slice(gq * 128, (gq + 1) * 128)
            dq_ref[:, sl] = _rope_t(dqs[:, sl], cos_t, sin_t).astype(BF16)
        dk_ref[...] = _rope_t(fold(dka), cos_t, sin_t).astype(BF16)
        dv_ref[...] = fold(dva).astype(BF16)
        dkc_ref[...] = fold(dkca).astype(BF16)
        dvc_ref[...] = fold(dvca).astype(BF16)

    lat = lambda w: pl.BlockSpec((n_lat, w), lambda e: (e, 0))
    ctx = lambda w: pl.BlockSpec((n_ctx, w), lambda e: (e, 0))
    sd = jax.ShapeDtypeStruct
    return pl.pallas_call(
        body, grid=(2,), name=name,
        in_specs=_attn_specs(cfg) + [pl.BlockSpec((n_lat, ATT_W), lambda e: (e, 0)),
                                     pl.BlockSpec((n_ctx, ATT_W), lambda e: (cb + e, 0))],
        out_specs=[lat(ATT_W), lat(128), lat(128), ctx(ATT_W), ctx(128), ctx(128),
                   pl.BlockSpec((None, 8, 128), lambda e: (e, 0, 0))],
        out_shape=[sd((cfg.t_lat, ATT_W), BF16), sd((cfg.t_lat, 128), BF16), sd((cfg.t_lat, 128), BF16),
                   sd((cfg.t_ctx, ATT_W), BF16), sd((cfg.t_ctx, 128), BF16), sd((cfg.t_ctx, 128), BF16),
                   sd((2, 8, 128), F32)],
        scratch_shapes=[pltpu.VMEM((n_lat, ATT_W), BF16)] + [pltpu.VMEM((n_lat, 128), BF16)] * 4
        + [pltpu.VMEM((n_ctx, 128), BF16)] * 4
        + [pltpu.VMEM((n_lat, ATT_W), F32), pltpu.VMEM((2, n_lat, 128), F32), pltpu.VMEM((2, n_lat, 128), F32),
           pltpu.VMEM((2, n_ctx, 128), F32), pltpu.VMEM((2, n_ctx, 128), F32)],
        compiler_params=_params(("parallel",)))(p, p, p, p, p, p, cos, sin, sink_rows, dcat, dcat)


def _shift_down(x, k, row):
    return jnp.where(row >= k, pltpu.roll(x, k, 0), 0.0)


def _shift_up(x, k, row):
    n = x.shape[0]
    return jnp.where(row < n - k, pltpu.roll(x, n - k, 0), 0.0)


def _window_sum(x, r, row):
    below, above, k = x, x, 1
    while k < r:
        below = below + _shift_down(below, k, row)
        above = above + _shift_up(above, k, row)
        k *= 2
    return below + _shift_down(x, r, row) + _shift_up(above, 1, row)


def _inv_count(r, row, n):
    cnt = jnp.minimum(row + r, n - 1) + 1 - jnp.maximum(row - r, 0)
    return 1.0 / cnt.astype(F32)


def _pool_fwd(p, w, scale, n, blk0, n_seg, name):
    def body(u0, u1, u2, u3, w_ref, sc_ref, o_ref):
        row = lax.broadcasted_iota(jnp.int32, (n, 1), 0)
        for g, u_ref in enumerate((u0, u1, u2, u3)):
            u = u_ref[...]
            d = _window_sum(u, POOL_R[g], row) * _inv_count(POOL_R[g], row, n) - u
            o_ref[:, g * 128:(g + 1) * 128] = (_dot(d, w_ref[g]) * sc_ref[:, g * 128:(g + 1) * 128]).astype(BF16)

    return pl.pallas_call(
        body, grid=(n_seg,), name=name,
        in_specs=[pl.BlockSpec((n, 128), functools.partial(lambda g, e: (blk0 + e, 6 + g), g)) for g in range(4)]
        + [pl.BlockSpec((4, 128, 128), lambda e: (0, 0, 0)), pl.BlockSpec((1, 512), lambda e: (0, 0))],
        out_specs=pl.BlockSpec((n, 512), lambda e: (e, 0)),
        out_shape=jax.ShapeDtypeStruct((n_seg * n, 512), BF16),
        compiler_params=_params(("parallel",)))(p, p, p, p, w, scale)


def _pool_bwd(p, w, scale, dcat, n, blk0, n_seg, name):
    def body(u0, u1, u2, u3, w_ref, sc_ref, dp_ref, du_ref, dw_ref, dsc_ref):
        e = pl.program_id(0)

        @pl.when(e == 0)
        def _():
            dw_ref[...] = jnp.zeros_like(dw_ref)
            dsc_ref[...] = jnp.zeros_like(dsc_ref)

        row = lax.broadcasted_iota(jnp.int32, (n, 1), 0)
        for g, u_ref in enumerate((u0, u1, u2, u3)):
            sl = slice(g * 128, (g + 1) * 128)
            u = u_ref[...]
            inv = _inv_count(POOL_R[g], row, n)
            d = _window_sum(u, POOL_R[g], row) * inv - u
            dp = dp_ref[:, sl]
            dsc_ref[:, sl] += jnp.sum(dp * _dot(d, w_ref[g]), axis=0, keepdims=True)
            dyp = dp * sc_ref[:, sl]
            dw_ref[g] += _dot_tn(d, dyp)
            dd = _dot_nt(dyp, w_ref[g])
            du_ref[:, sl] = (_window_sum(dd * inv, POOL_R[g], row) - dd).astype(BF16)

    return pl.pallas_call(
        body, grid=(n_seg,), name=name,
        in_specs=[pl.BlockSpec((n, 128), functools.partial(lambda g, e: (blk0 + e, 6 + g), g)) for g in range(4)]
        + [pl.BlockSpec((4, 128, 128), lambda e: (0, 0, 0)), pl.BlockSpec((1, 512), lambda e: (0, 0)),
           pl.BlockSpec((n, 512), lambda e: (blk0 + e, 1))],
        out_specs=[pl.BlockSpec((n, 512), lambda e: (e, 0)),
                   pl.BlockSpec((4, 128, 128), lambda e: (0, 0, 0)), pl.BlockSpec((1, 512), lambda e: (0, 0))],
        out_shape=[jax.ShapeDtypeStruct((n_seg * n, 512), BF16), jax.ShapeDtypeStruct((4, 128, 128), F32),
                   jax.ShapeDtypeStruct((1, 512), F32)],
        compiler_params=_params(("arbitrary",)))(p, p, p, p, w, scale, dcat)


def _gelu(x):
    t = jnp.tanh(math.sqrt(2.0 / math.pi) * (x + 0.044715 * x * x * x))
    return 0.5 * x * (1.0 + t), t


def _gelu_grad(x, t):
    return 0.5 * (1.0 + t) + 0.5 * x * (1.0 - t * t) * (math.sqrt(2.0 / math.pi) * (1.0 + 3 * 0.044715 * x * x))


def _neg_expm1(x):
    series = -x * (1.0 + x * (0.5 + x * (1.0 / 6.0 + x * (1.0 / 24.0 + x * (1.0 / 120.0)))))
    return jnp.where(x > -0.05, series, 1.0 - jnp.exp(x))


def _softplus_neg(lam):
    x = -lam
    e = jnp.exp(-jnp.abs(x))
    log1p = jnp.where(e < 1e-2, e * (1.0 - e * (0.5 - e * (1.0 / 3.0))), jnp.log(1.0 + e))
    return jnp.maximum(x, 0.0) + log1p, -_sigmoid(x)


def _conv(u, w_ref, b_ref, row):
    return (b_ref[...] + _shift_down(u, 1, row) * w_ref[0:1, :] + u * w_ref[1:2, :]
            + _shift_up(u, 1, row) * w_ref[2:3, :] + _shift_up(u, 2, row) * w_ref[3:4, :])


def _lru_gates(uc, d, wa_ref, ba_ref, wx_ref, bx_ref, lam_ref):
    r = _sigmoid(_dot(uc, wa_ref[d]) + ba_ref[d:d + 1, :])
    gi = _sigmoid(_dot(uc, wx_ref[d]) + bx_ref[d:d + 1, :])
    sp, dsp = _softplus_neg(lam_ref[d:d + 1, :])
    la = (-LRU_C) * r * sp
    a = jnp.exp(la)
    sq = jnp.sqrt(_neg_expm1(2.0 * la))
    return r, gi, sp, dsp, a, sq


def _tile_scan(a_ref, b_ref, n, reverse):
    m = n // 8
    first = 7 if reverse else 0
    a_prev = a_ref[pl.ds(first, m, stride=8), :]
    b_prev = b_ref[pl.ds(first, m, stride=8), :]
    for j in (range(6, -1, -1) if reverse else range(1, 8)):
        rows = pl.ds(j, m, stride=8)
        aj = a_ref[rows, :]
        b_prev = aj * b_prev + b_ref[rows, :]
        a_prev = aj * a_prev
        b_ref[rows, :] = b_prev
        a_ref[rows, :] = a_prev


def _carry_scan(a_ref, b_ref, n, reverse, carry):
    nt8 = n // 8

    def step(i, c):
        t = (nt8 - 1 - i) if reverse else i
        off = pl.multiple_of(t * 8, 8)
        h = a_ref[pl.ds(off, 8), :] * c + b_ref[pl.ds(off, 8), :]
        b_ref[pl.ds(off, 8), :] = h
        return h[0:1, :] if reverse else h[7:8, :]

    return lax.fori_loop(0, nt8, step, carry, unroll=4)


def _chain_scan(segs, reverse):
    carry = jnp.zeros((1, 128), F32)
    for a, b, a_ref, b_ref, n in segs:
        a_ref[...] = a
        b_ref[...] = b
        _tile_scan(a_ref, b_ref, n, reverse)
        carry = _carry_scan(a_ref, b_ref, n, reverse, carry)


def _lru_specs(cfg):
    n_lat, n_ctx, cb = cfg.n_lat, cfg.n_ctx, cfg.ctx_blk
    return [pl.BlockSpec((n_lat, 128), lambda hb, e: (e, hb)),
            pl.BlockSpec((n_lat, 128), lambda hb, e: (e, 8 + hb)),
            pl.BlockSpec((n_ctx, 128), lambda hb, e: (cb + e, hb)),
            pl.BlockSpec((n_ctx, 128), lambda hb, e: (cb + e, 8 + hb)),
            pl.BlockSpec((4, 128), lambda hb, e: (0, hb)),
            pl.BlockSpec((1, 128), lambda hb, e: (0, hb)),
            pl.BlockSpec((2, None, 128, 128), lambda hb, e: (0, hb, 0, 0)),
            pl.BlockSpec((2, 128), lambda hb, e: (0, hb)),
            pl.BlockSpec((2, None, 128, 128), lambda hb, e: (0, hb, 0, 0)),
            pl.BlockSpec((2, 128), lambda hb, e: (0, hb)),
            pl.BlockSpec((2, 128), lambda hb, e: (0, hb))]


def _lru_fwd(cfg, p, consts, name):
    n_lat, n_ctx = cfg.n_lat, cfg.n_ctx

    def body(gl_ref, ul_ref, gc_ref, uc_ref, cw_ref, cb_ref, wa_ref, ba_ref, wx_ref, bx_ref, lam_ref,
             zl_ref, zc_ref, hl_ref, hc_ref, al, ac):
        row_l = lax.broadcasted_iota(jnp.int32, (n_lat, 1), 0)
        row_c = lax.broadcasted_iota(jnp.int32, (n_ctx, 1), 0)
        uc_l = _conv(ul_ref[...], cw_ref, cb_ref, row_l)
        uc_c = _conv(uc_ref[...], cw_ref, cb_ref, row_c)
        for d in range(2):
            _, gi_l, _, _, a_l, sq_l = _lru_gates(uc_l, d, wa_ref, ba_ref, wx_ref, bx_ref, lam_ref)
            _, gi_c, _, _, a_c, sq_c = _lru_gates(uc_c, d, wa_ref, ba_ref, wx_ref, bx_ref, lam_ref)
            _chain_scan([(a_c, sq_c * (gi_c * uc_c), ac, hc_ref.at[d], n_ctx),
                         (a_l, sq_l * (gi_l * uc_l), al, hl_ref.at[d], n_lat)], reverse=(d == 1))
        zl_ref[...] = (_gelu(gl_ref[...])[0] * (hl_ref[0] + hl_ref[1])).astype(BF16)
        zc_ref[...] = (_gelu(gc_ref[...])[0] * (hc_ref[0] + hc_ref[1])).astype(BF16)

    return pl.pallas_call(
        body, grid=(8, 2), name=name, in_specs=_lru_specs(cfg),
        out_specs=[pl.BlockSpec((n_lat, 128), lambda hb, e: (e, hb)), pl.BlockSpec((n_ctx, 128), lambda hb, e: (e, hb)),
                   pl.BlockSpec((2, n_lat, 128), lambda hb, e: (0, e, hb)),
                   pl.BlockSpec((2, n_ctx, 128), lambda hb, e: (0, e, hb))],
        out_shape=[jax.ShapeDtypeStruct((cfg.t_lat, D), BF16), jax.ShapeDtypeStruct((cfg.t_ctx, D), BF16),
                   jax.ShapeDtypeStruct((2, cfg.t_lat, D), F32), jax.ShapeDtypeStruct((2, cfg.t_ctx, D), F32)],
        scratch_shapes=[pltpu.VMEM((n_lat, 128), F32), pltpu.VMEM((n_ctx, 128), F32)],
        compiler_params=_params(("parallel", "arbitrary")))(p, p, p, p, *consts)


def _lru_bwd(cfg, p, dz, h_lat, h_ctx, consts, name):
    n_lat, n_ctx, cb = cfg.n_lat, cfg.n_ctx, cfg.ctx_blk

    def body(gl_ref, ul_ref, gc_ref, uc_ref, cw_ref, cb_ref, wa_ref, ba_ref, wx_ref, bx_ref, lam_ref,
             dzl_ref, dzc_ref, hl, hc, dgl_ref, dul_ref, dgc_ref, duc_ref, dwa_ref, dwx_ref, vec_ref,
             al, bl, ac, bc):
        e = pl.program_id(1)

        @pl.when(e == 0)
        def _():
            dwa_ref[...] = jnp.zeros_like(dwa_ref)
            dwx_ref[...] = jnp.zeros_like(dwx_ref)
            vec_ref[...] = jnp.zeros_like(vec_ref)

        row_l = lax.broadcasted_iota(jnp.int32, (n_lat, 1), 0)
        row_c = lax.broadcasted_iota(jnp.int32, (n_ctx, 1), 0)
        u_l, u_c = ul_ref[...], uc_ref[...]
        uc_l = _conv(u_l, cw_ref, cb_ref, row_l)
        uc_c = _conv(u_c, cw_ref, cb_ref, row_c)
        gel_l, t_l = _gelu(gl_ref[...])
        gel_c, t_c = _gelu(gc_ref[...])
        dz_l, dz_c = dzl_ref[...], dzc_ref[...]
        dgl_ref[...] = (dz_l * (hl[0] + hl[1]) * _gelu_grad(gl_ref[...], t_l)).astype(BF16)
        dgc_ref[...] = (dz_c * (hc[0] + hc[1]) * _gelu_grad(gc_ref[...], t_c)).astype(BF16)
        dy_l, dy_c = dz_l * gel_l, dz_c * gel_c
        duc_l = jnp.zeros((n_lat, 128), F32)
        duc_c = jnp.zeros((n_ctx, 128), F32)
        for d in range(2):
            r_l, gi_l, sp, dsp, a_l, sq_l = _lru_gates(uc_l, d, wa_ref, ba_ref, wx_ref, bx_ref, lam_ref)
            r_c, gi_c, _, _, a_c, sq_c = _lru_gates(uc_c, d, wa_ref, ba_ref, wx_ref, bx_ref, lam_ref)
            if d == 0:
                an_l = _shift_up(a_l, 1, row_l)
                an_c = jnp.where(row_c < n_ctx - 1, pltpu.roll(a_c, n_ctx - 1, 0), a_l[0:1, :])
            else:
                an_l = _shift_down(a_l, 1, row_l)
                an_c = jnp.where(row_c >= 1, pltpu.roll(a_c, 1, 0), a_l[n_lat - 1:n_lat, :])
            _chain_scan([(an_l, dy_l, al, bl, n_lat), (an_c, dy_c, ac, bc, n_ctx)], reverse=(d == 0))
            dsp_sum = jnp.zeros((1, 128), F32)
            for (dh, h, r, gi, a, sq, uc, seg) in ((bl[...], hl[d], r_l, gi_l, a_l, sq_l, uc_l, "l"),
                                                  (bc[...], hc[d], r_c, gi_c, a_c, sq_c, uc_c, "c")):
                b0 = sq * (gi * uc)
                t1 = dh * sq
                dla = dh * (h - b0) - (dh * gi * uc) * (a * a) / sq
                dzr = (dla * ((-LRU_C) * sp)) * r * (1.0 - r)
                dzi = (t1 * uc) * gi * (1.0 - gi)
                dsp_sum = dsp_sum + jnp.sum(dla * ((-LRU_C) * r), axis=0, keepdims=True)
                dwa_ref[d] += _dot_tn(uc, dzr)
                dwx_ref[d] += _dot_tn(uc, dzi)
                vec_ref[d:d + 1, :] += jnp.sum(dzr, axis=0, keepdims=True)
                vec_ref[2 + d:3 + d, :] += jnp.sum(dzi, axis=0, keepdims=True)
                duc = t1 * gi + _dot_nt(dzr, wa_ref[d]) + _dot_nt(dzi, wx_ref[d])
                if seg == "l":
                    duc_l = duc_l + duc
                else:
                    duc_c = duc_c + duc
            vec_ref[4 + d:5 + d, :] += dsp_sum * dsp
        for duc, u, row, du_ref in ((duc_l, u_l, row_l, dul_ref), (duc_c, u_c, row_c, duc_ref)):
            du_ref[...] = (_shift_up(duc, 1, row) * cw_ref[0:1, :] + duc * cw_ref[1:2, :]
                           + _shift_down(duc, 1, row) * cw_ref[2:3, :]
                           + _shift_down(duc, 2, row) * cw_ref[3:4, :]).astype(BF16)
            vec_ref[6:7, :] += jnp.sum(duc * _shift_down(u, 1, row), axis=0, keepdims=True)
            vec_ref[7:8, :] += jnp.sum(duc * u, axis=0, keepdims=True)
            vec_ref[8:9, :] += jnp.sum(duc * _shift_up(u, 1, row), axis=0, keepdims=True)
            vec_ref[9:10, :] += jnp.sum(duc * _shift_up(u, 2, row), axis=0, keepdims=True)
            vec_ref[10:11, :] += jnp.sum(duc, axis=0, keepdims=True)

    lat = pl.BlockSpec((n_lat, 128), lambda hb, e: (e, hb))
    ctx = pl.BlockSpec((n_ctx, 128), lambda hb, e: (e, hb))
    wspec = pl.BlockSpec((2, None, 128, 128), lambda hb, e: (0, hb, 0, 0))
    sd = jax.ShapeDtypeStruct
    return pl.pallas_call(
        body, grid=(8, 2), name=name,
        in_specs=_lru_specs(cfg) + [pl.BlockSpec((n_lat, 128), lambda hb, e: (e, hb)),
                                    pl.BlockSpec((n_ctx, 128), lambda hb, e: (cb + e, hb)),
                                    pl.BlockSpec((2, n_lat, 128), lambda hb, e: (0, e, hb)),
                                    pl.BlockSpec((2, n_ctx, 128), lambda hb, e: (0, e, hb))],
        out_specs=[lat, lat, ctx, ctx, wspec, wspec, pl.BlockSpec((None, 16, 128), lambda hb, e: (hb, 0, 0))],
        out_shape=[sd((cfg.t_lat, D), BF16), sd((cfg.t_lat, D), BF16), sd((cfg.t_ctx, D), BF16), sd((cfg.t_ctx, D), BF16),
                   sd((2, 8, 128, 128), F32), sd((2, 8, 128, 128), F32), sd((8, 16, 128), F32)],
        scratch_shapes=[pltpu.VMEM((n_lat, 128), F32)] * 2 + [pltpu.VMEM((n_ctx, 128), F32)] * 2,
        compiler_params=_params(("parallel", "arbitrary")))(p, p, p, p, *consts, dz, dz, h_lat, h_ctx)


def _position():
    x, y, c = lax.axis_index("x"), lax.axis_index("y"), lax.axis_index("c")
    return x, y, c, 4 * x + 2 * y + c


def _peer(x, y, c, k):
    px = 1 - x if k & 4 else x
    py = 1 - y if k & 2 else y
    pc = 1 - c if k & 1 else c
    return (px, py, pc), 4 * px + 2 * py + pc


def _all_gather(v, name, in_vmem):
    def body(v_ref, o_ref, send_sems, recv_sems, local_sem):
        x, y, c, me = _position()
        mine = pltpu.make_async_copy(v_ref, o_ref.at[me], local_sem)
        mine.start()
        sends = []
        for k in range(1, N_DEV):
            peer, _ = _peer(x, y, c, k)
            cp = pltpu.make_async_remote_copy(src_ref=v_ref, dst_ref=o_ref.at[me], send_sem=send_sems.at[k - 1],
                                              recv_sem=recv_sems.at[k - 1], device_id=peer, device_id_type=MESH)
            cp.start()
            sends.append(cp)
        for k in range(1, N_DEV):
            peer, peer_lin = _peer(x, y, c, k)
            pltpu.make_async_remote_copy(src_ref=v_ref, dst_ref=o_ref.at[peer_lin], send_sem=send_sems.at[k - 1],
                                         recv_sem=recv_sems.at[k - 1], device_id=peer, device_id_type=MESH).wait_recv()
        for cp in sends:
            cp.wait_send()
        mine.wait()

    space = pltpu.VMEM if in_vmem else pl.ANY
    return pl.pallas_call(
        body, name=name,
        in_specs=[pl.BlockSpec(memory_space=space)], out_specs=pl.BlockSpec(memory_space=space),
        out_shape=jax.ShapeDtypeStruct((N_DEV,) + v.shape, v.dtype),
        scratch_shapes=[pltpu.SemaphoreType.DMA((N_DEV - 1,)), pltpu.SemaphoreType.DMA((N_DEV - 1,)),
                        pltpu.SemaphoreType.DMA],
        compiler_params=pltpu.CompilerParams(vmem_limit_bytes=VMEM_LIMIT))(v)


_HBM = pl.BlockSpec(memory_space=pltpu.HBM)
_SEM = pl.BlockSpec(memory_space=pltpu.SEMAPHORE)
_EFFECT = pltpu.SideEffectType.DATAFLOW_SIDE_EFFECTING


ALL_PEERS = tuple(range(1, N_DEV))
SAME_CORE_AND_SIBLING = (1, 2, 4, 6)


def _push_start(src, land, block_of, name, relations=ALL_PEERS):
    def body(src_ref, land_ref, send_sem, recv_sem, src_thru, land_thru, token):
        x, y, c, me = _position()
        for k in relations:
            peer, peer_lin = _peer(x, y, c, k)
            mine, there = block_of(src_ref, land_ref, me, peer_lin)
            pltpu.make_async_remote_copy(src_ref=mine, dst_ref=there, send_sem=send_sem, recv_sem=recv_sem,
                                         device_id=peer, device_id_type=MESH).start()
        mine, here = block_of(src_ref, land_ref, me, me)
        pltpu.make_async_copy(mine, here, recv_sem).start()
        token[...] = jnp.zeros_like(token)

    return pl.pallas_call(
        body, name=name,
        out_shape=(pltpu.SemaphoreType.DMA(()), pltpu.SemaphoreType.DMA(()), pltpu.HBM(src.shape, src.dtype),
                   pltpu.HBM(land.shape, land.dtype), jax.ShapeDtypeStruct((8, 128), F32)),
        in_specs=(_HBM, _HBM), out_specs=(_SEM, _SEM, _HBM, _HBM, pl.BlockSpec(memory_space=pltpu.VMEM)),
        input_output_aliases={0: 2, 1: 3},
        compiler_params=pltpu.CompilerParams(has_side_effects=_EFFECT),
    )(pltpu.with_memory_space_constraint(src, pltpu.HBM), pltpu.with_memory_space_constraint(land, pltpu.HBM))


def _push_wait(handle, blocks_of, after, name, n_peers=N_DEV - 1):
    send_sem, recv_sem, src_thru, land_thru, _ = handle

    def body(src_ref, land_ref, send_sem, recv_sem, after_ref, src_dead, got_ref):
        x, y, c, _ = _position()
        sent, landed = blocks_of(land_ref, n_peers), blocks_of(land_ref, n_peers + 1)
        pltpu.make_async_remote_copy(src_ref=sent, dst_ref=sent, send_sem=send_sem, recv_sem=recv_sem,
                                     device_id=(x, y, 1 - c), device_id_type=MESH).wait_send()
        pltpu.make_async_remote_copy(src_ref=landed, dst_ref=landed, send_sem=send_sem, recv_sem=recv_sem,
                                     device_id=(x, y, 1 - c), device_id_type=MESH).wait_recv()

    return pl.pallas_call(
        body, name=name,
        out_shape=(pltpu.HBM(src_thru.shape, src_thru.dtype), pltpu.HBM(land_thru.shape, land_thru.dtype)),
        in_specs=(_HBM, _HBM, _SEM, _SEM, pl.BlockSpec(memory_space=pl.ANY)), out_specs=(_HBM, _HBM),
        input_output_aliases={0: 0, 1: 1},
        compiler_params=pltpu.CompilerParams(has_side_effects=_EFFECT),
    )(src_thru, land_thru, send_sem, recv_sem, after)[1]


def _gather_start(src, name, relations=ALL_PEERS):
    g, r, C = src.shape
    land = lax.empty((g, N_DEV * r, C), src.dtype)
    return _push_start(src, land, lambda s, z, i, p: (s, z.at[:, pl.ds(i * r, r), :]), name, relations)


def _gather_wait(handle, after, name, n_peers=N_DEV - 1):
    r = handle[2].shape[1]
    return _push_wait(handle, lambda z, n: z.at[:, pl.ds(0, n * r), :], after, name, n_peers)


def _relay_start(land, r, name):
    def body(land_ref, send_sem, recv_sem, land_thru, token):
        x, y, c, _ = _position()
        for k in (2, 4, 6):
            _, origin = _peer(x, y, c, k)
            rows = land_ref.at[:, pl.ds(origin * r, r), :]
            pltpu.make_async_remote_copy(src_ref=rows, dst_ref=rows, send_sem=send_sem, recv_sem=recv_sem,
                                         device_id=(x, y, 1 - c), device_id_type=MESH).start()
        token[...] = jnp.zeros_like(token)

    return pl.pallas_call(
        body, name=name,
        out_shape=(pltpu.SemaphoreType.DMA(()), pltpu.SemaphoreType.DMA(()), pltpu.HBM(land.shape, land.dtype),
                   jax.ShapeDtypeStruct((8, 128), F32)),
        in_specs=(_HBM,), out_specs=(_SEM, _SEM, _HBM, pl.BlockSpec(memory_space=pltpu.VMEM)),
        input_output_aliases={0: 2},
        compiler_params=pltpu.CompilerParams(has_side_effects=_EFFECT),
    )(pltpu.with_memory_space_constraint(land, pltpu.HBM))


def _relay_wait(handle, r, after, name):
    send_sem, recv_sem, land_thru, _ = handle

    def body(land_ref, send_sem, recv_sem, after_ref, got_ref):
        x, y, c, _ = _position()
        three = land_ref.at[:, pl.ds(0, 3 * r), :]
        cp = pltpu.make_async_remote_copy(src_ref=three, dst_ref=three, send_sem=send_sem, recv_sem=recv_sem,
                                          device_id=(x, y, 1 - c), device_id_type=MESH)
        cp.wait_send()
        cp.wait_recv()

    return pl.pallas_call(
        body, name=name, out_shape=(pltpu.HBM(land_thru.shape, land_thru.dtype),),
        in_specs=(_HBM, _SEM, _SEM, pl.BlockSpec(memory_space=pl.ANY)), out_specs=(_HBM,),
        input_output_aliases={0: 0},
        compiler_params=pltpu.CompilerParams(has_side_effects=_EFFECT),
    )(land_thru, send_sem, recv_sem, after)[0]


def _exchange_start(grad, name):
    g, rows, C = grad.shape
    r = rows // N_DEV
    land = lax.empty((N_DEV, g, r, C), grad.dtype)
    return _push_start(grad, land, lambda s, z, i, p: (s.at[:, pl.ds(p * r, r), :], z.at[i]), name)


def _exchange_wait(handle, after, name):
    return _push_wait(handle, lambda z, n: z.at[pl.ds(0, n)], after, name)


def _sum_blocks(v, name):
    k, rows, cols = v.shape
    tr = rows
    for cand in (rows, 512, 352, 256, 176, 128, 64, 32, 16):
        if rows % cand == 0 and k * cand * cols * v.dtype.itemsize <= 6 * 1024 * 1024:
            tr = cand
            break

    def body(v_ref, o_ref):
        acc = v_ref[0].astype(F32)
        for s in range(1, k):
            acc = acc + v_ref[s].astype(F32)
        o_ref[...] = acc

    return pl.pallas_call(
        body, grid=(rows // tr,), name=name,
        in_specs=[pl.BlockSpec((k, tr, cols), lambda i: (0, i, 0))],
        out_specs=pl.BlockSpec((tr, cols), lambda i: (i, 0)),
        out_shape=jax.ShapeDtypeStruct((rows, cols), F32),
        compiler_params=_params(("parallel",)))(v)


def _adam_math(w, g, m, v):
    m2 = B1 * m + (1.0 - B1) * g
    v2 = B2 * v + (1.0 - B2) * (g * g)
    m_hat = m2 / (1.0 - B1 ** STEP)
    v_hat = v2 / (1.0 - B2 ** STEP)
    return -LR * (m_hat / (jnp.sqrt(v_hat) + EPS) + WD * w), m2, v2


def _adamw(w, g, m, v, name, dep=None):
    shp = w.shape
    rows, cols = (shp[-2], shp[-1]) if len(shp) >= 2 else (1, shp[-1])
    lead = math.prod(shp[:-2]) if len(shp) > 2 else 1
    fits = [t for t in range(8, rows + 1, 8) if rows % t == 0 and t * cols * 4 <= 2 * 1024 * 1024]
    tr = max(fits) if fits else rows

    def body(w_ref, g_ref, m_ref, v_ref, *rest):
        d_ref, m2_ref, v2_ref = rest[-3:]
        d_ref[...], m2_ref[...], v2_ref[...] = _adam_math(w_ref[...], g_ref[...], m_ref[...], v_ref[...])

    blk = pl.BlockSpec((None, tr, cols), lambda b, i: (b, i, 0))
    extra = [] if dep is None else [dep]
    outs = pl.pallas_call(
        body, grid=(lead, rows // tr), name=name,
        in_specs=[blk] * 4 + [pl.BlockSpec(memory_space=pl.ANY)] * len(extra), out_specs=[blk] * 3,
        out_shape=[jax.ShapeDtypeStruct((lead, rows, cols), F32)] * 3,
        compiler_params=_params(("parallel", "parallel")))(*[a.reshape(lead, rows, cols) for a in (w, g, m, v)], *extra)
    return [o.reshape(shp) for o in outs]


def _as2d(a):
    n = a.size
    if n % 1024 == 0:
        return a.reshape(n // 1024, 1024)
    if n % 128 == 0:
        return a.reshape(n // 128, 128)
    return a.reshape(1, n)


def _blocks_to_cols(a):
    b = jnp.moveaxis(a, 0, -2)
    return b.reshape(b.shape[:-2] + (b.shape[-2] * b.shape[-1],))


def _pack_rows(parts):
    padded, offs, r = [], [], 0
    for p in parts:
        pad = (-p.shape[0]) % 8
        padded.append(jnp.pad(p, ((0, pad), (0, 0))) if pad else p)
        offs.append(r)
        r += p.shape[0] + pad
    return jnp.concatenate(padded, axis=0), offs


def _silu(x):
    return x * jax.nn.sigmoid(x)


def kernel(x, c, ctx, c_ctx, w_mod, b_mod, ln_g, ln_b, ffn_w_gate, ffn_w_up, ffn_w_down, mix_ab_w_in, attn_sink, pool_w, pool_scale, mix_ab_w_out, lru_w_in, lru_conv_w, lru_conv_b, lru_wa, lru_ba, lru_wx, lru_bx, lru_lambda, lru_w_out, loss_target, m_c_ctx, m_w_mod, m_b_mod, m_ln_g, m_ln_b, m_ffn_w_gate, m_ffn_w_up, m_ffn_w_down, m_mix_ab_w_in, m_attn_sink, m_pool_w, m_pool_scale, m_mix_ab_w_out, m_lru_w_in, m_lru_conv_w, m_lru_conv_b, m_lru_wa, m_lru_ba, m_lru_wx, m_lru_bx, m_lru_lambda, m_lru_w_out, v_c_ctx, v_w_mod, v_b_mod, v_ln_g, v_ln_b, v_ffn_w_gate, v_ffn_w_up, v_ffn_w_down, v_mix_ab_w_in, v_attn_sink, v_pool_w, v_pool_scale, v_mix_ab_w_out, v_lru_w_in, v_lru_conv_w, v_lru_conv_b, v_lru_wa, v_lru_ba, v_lru_wx, v_lru_bx, v_lru_lambda, v_lru_w_out):
    weights = dict(c_ctx=c_ctx, w_mod=w_mod, b_mod=b_mod, ln_g=ln_g, ln_b=ln_b, ffn_w_gate=ffn_w_gate,
                   ffn_w_up=ffn_w_up, ffn_w_down=ffn_w_down, mix_ab_w_in=mix_ab_w_in, attn_sink=attn_sink,
                   pool_w=pool_w, pool_scale=pool_scale, mix_ab_w_out=mix_ab_w_out, lru_w_in=lru_w_in,
                   lru_conv_w=lru_conv_w, lru_conv_b=lru_conv_b, lru_wa=lru_wa, lru_ba=lru_ba, lru_wx=lru_wx,
                   lru_bx=lru_bx, lru_lambda=lru_lambda, lru_w_out=lru_w_out)
    mom_m = dict(c_ctx=m_c_ctx, w_mod=m_w_mod, b_mod=m_b_mod, ln_g=m_ln_g, ln_b=m_ln_b, ffn_w_gate=m_ffn_w_gate,
                 ffn_w_up=m_ffn_w_up, ffn_w_down=m_ffn_w_down, mix_ab_w_in=m_mix_ab_w_in, attn_sink=m_attn_sink,
                 pool_w=m_pool_w, pool_scale=m_pool_scale, mix_ab_w_out=m_mix_ab_w_out, lru_w_in=m_lru_w_in,
                 lru_conv_w=m_lru_conv_w, lru_conv_b=m_lru_conv_b, lru_wa=m_lru_wa, lru_ba=m_lru_ba, lru_wx=m_lru_wx,
                 lru_bx=m_lru_bx, lru_lambda=m_lru_lambda, lru_w_out=m_lru_w_out)
    mom_v = dict(c_ctx=v_c_ctx, w_mod=v_w_mod, b_mod=v_b_mod, ln_g=v_ln_g, ln_b=v_ln_b, ffn_w_gate=v_ffn_w_gate,
                 ffn_w_up=v_ffn_w_up, ffn_w_down=v_ffn_w_down, mix_ab_w_in=v_mix_ab_w_in, attn_sink=v_attn_sink,
                 pool_w=v_pool_w, pool_scale=v_pool_scale, mix_ab_w_out=v_mix_ab_w_out, lru_w_in=v_lru_w_in,
                 lru_conv_w=v_lru_conv_w, lru_conv_b=v_lru_conv_b, lru_wa=v_lru_wa, lru_ba=v_lru_ba, lru_wx=v_lru_wx,
                 lru_bx=v_lru_bx, lru_lambda=v_lru_lambda, lru_w_out=v_lru_w_out)
    names = list(weights)

    n_lat, n_ctx = x.shape[1], ctx.shape[1]
    cfg = _Cfg(n_lat, n_ctx)
    _, _, _, me = _position()
    mcols = w_mod.shape[2]

    def t_bf16(w):
        return jnp.swapaxes(w, -1, -2).astype(BF16)

    def ffn_src(l, i):
        return jnp.stack([t_bf16(ffn_w_gate[l, i]), t_bf16(ffn_w_up[l, i]), ffn_w_down[l, i].astype(BF16)])

    pending = {}

    def start_gathers(items, tok):
        for key, make_src in items:
            pending[key] = _gather_start(make_src() + tok.astype(BF16), "gather_start_" + key)
            tok = pending[key][4][0, 0]
        return tok

    def weights_now(key, after):
        return _gather_wait(pending[key], after, "gather_wait_" + key)

    first = _gather_start(ffn_src(0, 0), "gather_start_ffn00", SAME_CORE_AND_SIBLING)
    tok = first[4][0, 0]

    small_names = ["ln_g", "ln_b", "lru_conv_w", "lru_conv_b", "lru_ba", "lru_bx", "lru_lambda"]
    small, small_off = _pack_rows([(c + tok).reshape(-1, 128)] + [weights[n].reshape(-1, 128) for n in small_names])
    small_all = _all_gather(small, "gather_small", True)

    def small_full(idx, shp):
        rows = math.prod(shp) // 128
        return _blocks_to_cols(small_all[:, small_off[idx]:small_off[idx] + rows, :].reshape((N_DEV,) + shp))

    c_all = small_all[:, :2 * D // 128, :].reshape(2 * N_DEV, D)
    ln_g_f, ln_b_f = small_full(1, ln_g.shape), small_full(2, ln_b.shape)
    lru_consts = (small_full(3, lru_conv_w.shape)[0], small_full(4, lru_conv_b.shape), lru_wa[0],
                  small_full(5, lru_ba.shape)[0], lru_wx[0], small_full(6, lru_bx.shape)[0],
                  small_full(7, lru_lambda.shape)[0])

    s_rows = jnp.zeros((32, D), F32).at[:16].set(_silu(c_all)).at[16].set(_silu(c_ctx)).astype(BF16)
    mod_mine = jnp.stack([_matmul(s_rows, w_mod[l], "nn", F32, "mod_fwd", bn_cap=1280) for l in range(2)])
    mod_all = _all_gather(mod_mine.reshape(64, mcols), "gather_mod", True).reshape(N_DEV, 2, 32, mcols)
    r_ffn = ffn_w_down.shape[2]
    relay = _relay_start(_gather_wait(first, mod_all, "gather_wait_ffn00", n_peers=len(SAME_CORE_AND_SIBLING)),
                         r_ffn, "gather_relay_start_ffn00")
    tok = start_gathers([("ab_in", lambda: t_bf16(mix_ab_w_in)), ("ab_out", lambda: mix_ab_w_out.astype(BF16)),
                         ("ffn01", lambda: ffn_src(0, 1)), ("ffn10", lambda: ffn_src(1, 0)),
                         ("lru_in", lambda: t_bf16(lru_w_in)), ("lru_out", lambda: lru_w_out.astype(BF16)),
                         ("ffn11", lambda: ffn_src(1, 1))], relay[3][0, 0])
    mod_full = _blocks_to_cols(mod_all) + (b_mod[:, None, :] + tok)
    ex0 = 2 * me
    mods = []
    for l in range(2):
        rows = jnp.stack([lax.dynamic_index_in_dim(mod_full[l], ex0, 0, False),
                          lax.dynamic_index_in_dim(mod_full[l], ex0 + 1, 0, False), mod_full[l, 16]])
        mods.append(rows.reshape(3, N_MOD, D))

    h0 = jnp.concatenate([x.reshape(cfg.t_lat, D), ctx.reshape(cfg.t_ctx, D)], axis=0)
    cos, sin = _rope_tables(n_lat)
    sink_rows = jnp.broadcast_to(attn_sink[0][:, None], (8, 128)).astype(F32)

    saved = []
    wf = [[None, None], [None, None]]
    h = h0
    xin = _modulate(cfg, h0, mods[0], 0, 1, "modulate_in")
    for l in range(2):
        st = {"h_in": h, "xin1": xin}
        wf[l][0] = (_relay_wait(relay, r_ffn, xin, "gather_relay_wait_ffn00") if l == 0
                    else weights_now("ffn10", xin))
        g1, u1, y1 = _ffn_fwd(xin, wf[l][0], "ffn_fwd")
        h1, xhat1, rstd1, xin2 = _ln_fwd(cfg, h, y1, mods[l], 2, 0.5, ln_g_f[l, 0][None], ln_b_f[l, 0][None],
                                          mods[l], (3, 4), "ln_fwd_a")
        st.update(g1=g1, u1=u1, y1=y1, h1=h1, xhat1=xhat1, rstd1=rstd1, xin2=xin2)
        if l == 0:
            w_ab_in_t = weights_now("ab_in", xin2)[0]
            p = _matmul(xin2, w_ab_in_t, "nt", F32, "mix_ab_in")
            att_l, att_c = _attn_fwd(cfg, p, cos, sin, sink_rows, "attn_fwd")
            pool_l = _pool_fwd(p, pool_w[0], pool_scale, n_lat, 0, 2, "pool_fwd_lat")
            pool_c = _pool_fwd(p, pool_w[0], pool_scale, n_ctx, cfg.ctx_blk, 2, "pool_fwd_ctx")
            cat = jnp.concatenate([jnp.concatenate([att_l, pool_l], axis=1),
                                   jnp.concatenate([att_c, pool_c], axis=1)], axis=0)
            w_ab_out = weights_now("ab_out", cat)[0]
            y2 = _matmul(cat, w_ab_out, "nn", BF16, "mix_ab_out")
        else:
            w_lru_in_t = weights_now("lru_in", xin2)[0]
            p = _matmul(xin2, w_lru_in_t, "nt", F32, "lru_in")
            z_l, z_c, st["h_lat"], st["h_ctx"] = _lru_fwd(cfg, p, lru_consts, "lru_fwd")
            cat = jnp.concatenate([z_l, z_c], axis=0)
            w_lru_out = weights_now("lru_out", cat)[0]
            y2 = _matmul(cat, w_lru_out, "nn", BF16, "lru_out")
        h2, xhat2, rstd2, xin3 = _ln_fwd(cfg, h1, y2, mods[l], 5, 1.0, ln_g_f[l, 1][None], ln_b_f[l, 1][None],
                                          mods[l], (6, 7), "ln_fwd_b")
        wf[l][1] = weights_now("ffn%d1" % l, xin3)
        g3, u3, y3 = _ffn_fwd(xin3, wf[l][1], "ffn_fwd")
        if l == 0:
            h3, xhat3, rstd3, xin = _ln_fwd(cfg, h2, y3, mods[l], 8, 0.5, ln_g_f[l, 2][None], ln_b_f[l, 2][None],
                                            mods[1], (0, 1), "ln_fwd_a")
        else:
            h3, xhat3, rstd3 = _ln_fwd(cfg, h2, y3, mods[l], 8, 0.5, ln_g_f[l, 2][None], ln_b_f[l, 2][None],
                                       None, None, "ln_fwd_last")
        st.update(p=p, cat=cat, y2=y2, h2=h2, xhat2=xhat2, rstd2=rstd2, xin3=xin3, g3=g3, u3=u3, y3=y3,
                  xhat3=xhat3, rstd3=rstd3)
        saved.append(st)
        h = h3

    dy, loss_tile = _loss(cfg, h, loss_target.reshape(cfg.t_lat, D), "loss")
    loss = lax.psum(loss_tile[0, 0], ("x", "y", "c"))

    grads = {}
    dmod = [None, None]
    recv_ffn = [[None, None], [None, None]]
    dln_g = [[None] * 3, [None] * 3]
    dln_b = [[None] * 3, [None] * 3]

    def ffn_weight_grads(tag, xin_b, dg, du, a_act, dys):
        parts = [_matmul(dg, xin_b, "tn", BF16, "ffn_dw", bm_cap=1408, bk_cap=2304)[None],
                 _matmul(du, xin_b, "tn", BF16, "ffn_dw", bm_cap=1408, bk_cap=2304)[None],
                 _matmul(a_act, dys, "tn", BF16, "ffn_dw", bm_cap=1408, bk_cap=2304)[None]]
        return [_exchange_start(part, "exchange_start_ffn%s_%d" % (tag, k)) for k, part in enumerate(parts)]

    def pin(handles):
        total = handles[0][4][0, 0]
        for hd in handles[1:]:
            total = total + hd[4][0, 0]
        return total

    up = (dy,)
    dmod_next = None
    last_sent = None
    for l in (1, 0):
        st = saved[l]
        dm = [None] * N_MOD

        def put_stats(stats, gate_idx, nxt):
            dm[gate_idx] = stats[:, 2, :]
            if nxt is not None:
                nxt[0][nxt[1]] = stats[:, 4, :]
                nxt[0][nxt[1] + 1] = stats[:, 3, :]

        lng3 = ln_g_f[l, 2][None] if last_sent is None else ln_g_f[l, 2][None] + pin(last_sent)
        if len(up) > 1:
            up = (up[0], up[1], ln_b_f[l, 2][None], up[3], up[4])
        dres, dys, stats = _ln_bwd(cfg, up, st["xhat3"], st["rstd3"], st["y3"], mods[l], 8, 0.5,
                                   lng3, "ln_bwd_fused" if len(up) > 1 else "ln_bwd_last")
        put_stats(stats, 8, None if len(up) == 1 else (dmod_next, 0))
        dln_g[l][2], dln_b[l][2] = stats[:, 0, :].sum(0), stats[:, 1, :].sum(0)
        dg, du, a_act, dxin = _ffn_bwd(dys, st["g3"], st["u3"], wf[l][1], "ffn_bwd")
        recv_ffn[l][1] = ffn_weight_grads("%d1" % l, st["xin3"], dg, du, a_act, dys)
        dres, dys, stats = _ln_bwd(cfg, (dres, dxin, ln_b_f[l, 1][None], mods[l], 7), st["xhat2"], st["rstd2"], st["y2"],
                                   mods[l], 5, 1.0, ln_g_f[l, 1][None] + pin(recv_ffn[l][1]), "ln_bwd_fused")
        put_stats(stats, 5, (dm, 6))
        dln_g[l][1], dln_b[l][1] = stats[:, 0, :].sum(0), stats[:, 1, :].sum(0)
        if l == 0:
            dw_out = _matmul(st["cat"], dys, "tn", BF16, "mix_ab_dw_out")
            dcat = _matmul(dys, w_ab_out, "nt", F32, "mix_ab_dcat")
            dq, dk, dv, dqc, dkc, dvc, dsink = _attn_bwd(cfg, st["p"], dcat, cos, sin, sink_rows, "attn_bwd")
            du_l, dpw_l, dps_l = _pool_bwd(st["p"], pool_w[0], pool_scale, dcat, n_lat, 0, 2, "pool_bwd_lat")
            du_c, dpw_c, dps_c = _pool_bwd(st["p"], pool_w[0], pool_scale, dcat, n_ctx, cfg.ctx_blk, 2, "pool_bwd_ctx")
            dp = jnp.concatenate([jnp.concatenate([dq, dk, dv, du_l], axis=1),
                                  jnp.concatenate([dqc, dkc, dvc, du_c], axis=1)], axis=0)
            dw_in_t = _matmul(dp, st["xin2"], "tn", BF16, "mix_ab_dw_in", bm_cap=1280)
            dxin = _matmul(dp, w_ab_in_t, "nn", BF16, "mix_ab_dx")
            recv_mix = [_exchange_start(part, "exchange_start_mix_ab_%d" % k)
                        for k, part in enumerate((dw_in_t[None], dw_out[None], _as2d(dpw_l + dpw_c)[None]))]
            grads["attn_sink"] = (dsink[0, :, 0] + dsink[1, :, 0])[None, :]
            grads["pool_scale"] = dps_l + dps_c
        else:
            dw_out = _matmul(st["cat"], dys, "tn", BF16, "lru_dw_out")
            dz = _matmul(dys, w_lru_out, "nt", F32, "lru_dz")
            dgl, dul, dgc, duc, dwa, dwx, vec = _lru_bwd(cfg, st["p"], dz, st["h_lat"], st["h_ctx"], lru_consts, "lru_bwd")
            dp = jnp.concatenate([jnp.concatenate([dgl, dul], axis=1), jnp.concatenate([dgc, duc], axis=1)], axis=0)
            dw_in_t = _matmul(dp, st["xin2"], "tn", BF16, "lru_dw_in", bm_cap=1024)
            dxin = _matmul(dp, w_lru_in_t, "nn", BF16, "lru_dx")
            recv_mix = [_exchange_start(part, "exchange_start_lru_%d" % k)
                        for k, part in enumerate((dw_in_t[None], dw_out[None], _as2d(dwa)[None], _as2d(dwx)[None]))]
            vec_t = jnp.moveaxis(vec, 0, 1).reshape(16, D)
            grads["lru_ba"], grads["lru_bx"] = vec_t[0:2], vec_t[2:4]
            grads["lru_lambda"], grads["lru_conv_w"], grads["lru_conv_b"] = vec_t[4:6], vec_t[6:10], vec_t[10:11]
        if l == 0:
            recv_ab = recv_mix
        else:
            recv_lru = recv_mix
        dres, dys, stats = _ln_bwd(cfg, (dres, dxin, ln_b_f[l, 0][None], mods[l], 4), st["xhat1"], st["rstd1"], st["y1"],
                                   mods[l], 2, 0.5, ln_g_f[l, 0][None] + pin(recv_mix), "ln_bwd_fused")
        put_stats(stats, 2, (dm, 3))
        dln_g[l][0], dln_b[l][0] = stats[:, 0, :].sum(0), stats[:, 1, :].sum(0)
        dg, du, a_act, dxin = _ffn_bwd(dys, st["g1"], st["u1"], wf[l][0], "ffn_bwd")
        recv_ffn[l][0] = ffn_weight_grads("%d0" % l, st["xin1"], dg, du, a_act, dys)
        last_sent = recv_ffn[l][0]
        dmod[l] = dm
        dmod_next = dm
        up = (dres, dxin, None, mods[l], 1)
    dh0, stats = _modulate_bwd(cfg, up[0], up[1], h0, mods[0] + pin(last_sent), 1, "modulate_bwd")
    dmod[0][0], dmod[0][1] = stats[:, 4, :], stats[:, 3, :]
    grad_x = dh0.reshape(x.shape)

    dmod_mine = jnp.stack([jnp.stack(dmod[l], axis=1).reshape(3, N_MOD * D) for l in range(2)])
    n_dm = 6 * N_MOD * D // 128
    dmod_sent = _gather_start(dmod_mine.reshape(1, n_dm, 128), "gather_start_dmod")

    def arrived(handle, name):
        return _exchange_wait(handle, dmod_sent[4], name)

    recv_ffn = [[[arrived(hd, "exchange_wait_ffn%d%d_%d" % (l, i, k)) for k, hd in enumerate(recv_ffn[l][i])]
                 for i in range(2)] for l in range(2)]
    recv_ab = [arrived(hd, "exchange_wait_mix_ab_%d" % k) for k, hd in enumerate(recv_ab)]
    recv_lru = [arrived(hd, "exchange_wait_lru_%d" % k) for k, hd in enumerate(recv_lru)]

    def shard_sum(recv, name):
        return _sum_blocks(recv.reshape(N_DEV, recv.shape[2], recv.shape[3]), name)

    gate_g = [[None, None], [None, None]]
    up_g = [[None, None], [None, None]]
    down_g = [[None, None], [None, None]]
    for l in range(2):
        for i in range(2):
            gt, ut, dn = [shard_sum(r, "sum_ffn") for r in recv_ffn[l][i]]
            gate_g[l][i], up_g[l][i], down_g[l][i] = gt.T, ut.T, dn
    grads["ffn_w_gate"] = jnp.stack([jnp.stack(gate_g[l]) for l in range(2)])
    grads["ffn_w_up"] = jnp.stack([jnp.stack(up_g[l]) for l in range(2)])
    grads["ffn_w_down"] = jnp.stack([jnp.stack(down_g[l]) for l in range(2)])
    grads["mix_ab_w_in"] = shard_sum(recv_ab[0], "sum_mix_in").T[None]
    grads["mix_ab_w_out"] = shard_sum(recv_ab[1], "sum_mix_out")[None]
    grads["lru_w_in"] = shard_sum(recv_lru[0], "sum_lru_in").T[None]
    grads["lru_w_out"] = shard_sum(recv_lru[1], "sum_lru_out")[None]
    rep_parts = [shard_sum(recv_lru[2], "sum_rep"), shard_sum(recv_lru[3], "sum_rep"), shard_sum(recv_ab[2], "sum_rep")]
    rep_names = ["lru_wa", "lru_wx", "pool_w"]

    dmod_all = _gather_wait(dmod_sent, rep_parts[2], "gather_wait_dmod").reshape(N_DEV, n_dm, 128)
    dmod_sum = _sum_blocks(dmod_all, "sum_dmod").reshape(2, 3, N_MOD * D)
    dmod_all = dmod_all.reshape(N_DEV, 2, 3, N_MOD * D)
    grads["b_mod"] = dmod_sum[:, 0] + dmod_sum[:, 1] + dmod_sum[:, 2]
    dmod_ex = jnp.moveaxis(dmod_all[:, :, 0:2, :], 1, 0).reshape(2, 2 * N_DEV, N_MOD * D)
    dm_rows = jnp.zeros((2, 32, N_MOD * D), F32).at[:, :16].set(dmod_ex).at[:, 16].set(dmod_sum[:, 2])
    dm_cols = lax.dynamic_slice_in_dim(dm_rows, me * mcols, mcols, axis=2).astype(BF16)
    grads["w_mod"] = jnp.stack([_matmul(s_rows, dm_cols[l], "tn", F32, "mod_dw", bn_cap=1280) for l in range(2)])
    ds_part = None
    for l in range(2):
        part = _matmul(dm_cols[l, 16:32], w_mod[l], "nt", F32, "mod_ds", bk_cap=1280)[0]
        ds_part = part if ds_part is None else ds_part + part

    dln_g_f = jnp.stack([jnp.stack(dln_g[l]) for l in range(2)])
    dln_b_f = jnp.stack([jnp.stack(dln_b[l]) for l in range(2)])
    sink_pad = jnp.zeros((1, 128), F32).at[0, :8].set(grads["attn_sink"][0])
    part_list = [p_.reshape(-1, 128) for p_ in rep_parts] + [
        dln_g_f.reshape(-1, 128), dln_b_f.reshape(-1, 128), grads["lru_conv_w"].reshape(-1, 128),
        grads["lru_conv_b"].reshape(-1, 128), grads["lru_ba"].reshape(-1, 128), grads["lru_bx"].reshape(-1, 128),
        grads["lru_lambda"].reshape(-1, 128), ds_part.reshape(-1, 128), sink_pad, grads["pool_scale"].reshape(-1, 128)]
    parts, part_off = _pack_rows(part_list)
    parts_sent = _gather_start(parts[None], "gather_start_partials")

    delta, new_m, new_v = {}, {}, {}
    for n in ("w_mod", "b_mod", "ffn_w_gate", "ffn_w_up", "ffn_w_down", "mix_ab_w_in", "mix_ab_w_out",
              "lru_w_in", "lru_w_out"):
        grads[n] = grads[n].reshape(weights[n].shape)
        delta[n], new_m[n], new_v[n] = _adamw(weights[n], grads[n], mom_m[n], mom_v[n], "adamw", dep=parts_sent[4])
    parts_all = _gather_wait(parts_sent, delta["lru_w_out"], "gather_wait_partials").reshape(N_DEV, parts.shape[0], 128)
    parts_sum = _sum_blocks(parts_all, "sum_partials")

    for i, n in enumerate(rep_names):
        rows = part_list[i].shape[0]
        grads[n] = parts_all[:, part_off[i]:part_off[i] + rows, :].reshape(weights[n].shape)

    def take(idx):
        return parts_sum[part_off[idx]:part_off[idx] + part_list[idx].shape[0]]

    def my_cols(full, shp):
        w = shp[-1]
        return lax.dynamic_slice_in_dim(full, me * w, w, axis=full.ndim - 1)

    grads["ln_g"] = my_cols(take(3).reshape(2, 3, D), ln_g.shape)
    grads["ln_b"] = my_cols(take(4).reshape(2, 3, D), ln_b.shape)
    grads["lru_conv_w"] = my_cols(take(5).reshape(1, 4, D), lru_conv_w.shape)
    grads["lru_conv_b"] = my_cols(take(6).reshape(1, D), lru_conv_b.shape)
    grads["lru_ba"] = my_cols(take(7).reshape(1, 2, D), lru_ba.shape)
    grads["lru_bx"] = my_cols(take(8).reshape(1, 2, D), lru_bx.shape)
    grads["lru_lambda"] = my_cols(take(9).reshape(1, 2, D), lru_lambda.shape)
    sg = jax.nn.sigmoid(c_ctx)
    grads["c_ctx"] = take(10).reshape(D) * (sg * (1.0 + c_ctx * (1.0 - sg)))
    grads["attn_sink"] = take(11)[:, :8]
    grads["pool_scale"] = take(12).reshape(pool_scale.shape)

    for n in names:
        if n in delta:
            continue
        grads[n] = grads[n].reshape(weights[n].shape)
        delta[n], new_m[n], new_v[n] = _adamw(weights[n], grads[n], mom_m[n], mom_v[n], "adamw")

    return (loss, grad_x, *[grads[n] for n in names], *[delta[n] for n in names],
            *[new_m[n] for n in names], *[new_v[n] for n in names])
```

```python
import functools
import math

import jax
import jax.numpy as jnp
from jax import lax
from jax.experimental import pallas as pl
from jax.experimental.pallas import tpu as pltpu

F32 = jnp.float32
BF16 = jnp.bfloat16
MESH = pl.DeviceIdType.MESH

D = 1024
N_MOD = 9
N_DEV = 8
HEAD_DIM = 64
ATT_HEADS = 8
KV_HEADS = 2
ATT_W = 512
BLK = 128
ATT_SCALE = HEAD_DIM ** -0.5
GRID_W = 64
ROPE_FREQS = HEAD_DIM // 4
ROPE_THETA = 10000.0
POOL_R = (1, 2, 4, 8)
LRU_C = 8.0
LN_EPS = 1e-5
NEG_INF = -1e30
ALPHA = 4.0 ** 0.25
LR, B1, B2, EPS, WD, STEP = 0.001, 0.9, 0.999, 1e-08, 0.01, 10
VMEM_LIMIT = 56 * 1024 * 1024
ROW_TILE = 512


def _params(sem=None):
    if sem is None:
        return pltpu.CompilerParams(vmem_limit_bytes=VMEM_LIMIT)
    return pltpu.CompilerParams(dimension_semantics=sem, vmem_limit_bytes=VMEM_LIMIT)


def _sigmoid(x):
    return 0.5 * jnp.tanh(0.5 * x) + 0.5


def _dot(a, b):
    return jnp.dot(a.astype(BF16), b.astype(BF16), preferred_element_type=F32)


def _dot_nt(a, b):
    return lax.dot_general(a.astype(BF16), b.astype(BF16), (((1,), (1,)), ((), ())), preferred_element_type=F32)


def _dot_tn(a, b):
    return lax.dot_general(a.astype(BF16), b.astype(BF16), (((0,), (0,)), ((), ())), preferred_element_type=F32)


def _pick(n, cap):
    best = None
    for m in range(128, min(n, cap) + 1, 128):
        if n % m == 0:
            best = m
    return n if best is None else best


def _chunks(width, step=256):
    out, c = [], 0
    while c < width:
        w = min(step, width - c)
        out.append((c, w))
        c += w
    return out


class _Cfg:
    def __init__(self, n_lat, n_ctx):
        self.n_lat, self.n_ctx = n_lat, n_ctx
        self.t_lat, self.t_ctx = 2 * n_lat, 2 * n_ctx
        self.T = self.t_lat + self.t_ctx
        self.tm = min(ROW_TILE, self.t_ctx)
        assert n_lat % self.tm == 0 and self.t_ctx % self.tm == 0 and n_lat >= 3 * BLK and n_ctx % BLK == 0
        self.nt = self.T // self.tm
        self.nlt = n_lat // self.tm
        self.ctx_blk = self.t_lat // n_ctx

    def seg(self, i):
        return jnp.minimum(i // self.nlt, 2)

    def first_of_seg(self, i):
        return jnp.where(i < 2 * self.nlt, i % self.nlt == 0, i == 2 * self.nlt)


def _modulate(cfg, h, mod, shift_idx, scale_idx, name):
    tm = cfg.tm

    def body(h_ref, mod_ref, o_ref):
        sh = mod_ref[shift_idx:shift_idx + 1, :]
        sc = mod_ref[scale_idx:scale_idx + 1, :]
        o_ref[...] = (h_ref[...] * (1.0 + sc) + sh).astype(BF16)

    return pl.pallas_call(
        body, grid=(cfg.nt,), name=name,
        in_specs=[pl.BlockSpec((tm, D), lambda i: (i, 0)),
                  pl.BlockSpec((None, N_MOD, D), lambda i: (cfg.seg(i), 0, 0))],
        out_specs=pl.BlockSpec((tm, D), lambda i: (i, 0)),
        out_shape=jax.ShapeDtypeStruct((cfg.T, D), BF16),
        compiler_params=_params(("parallel",)),
    )(h, mod)


def _ln_fwd(cfg, h, y, mod, gate_idx, coef, lng, lnb, mod_next, next_idx, name):
    tm = cfg.tm
    has_next = next_idx is not None

    def body(*refs):
        if has_next:
            h_ref, y_ref, mod_ref, g_ref, b_ref, modn_ref, hn_ref, xhat_ref, rstd_ref, xin_ref = refs
        else:
            h_ref, y_ref, mod_ref, g_ref, b_ref, hn_ref, xhat_ref, rstd_ref = refs
        gate = mod_ref[gate_idx:gate_idx + 1, :]
        z = ALPHA * h_ref[...] + (coef * gate) * y_ref[...].astype(F32)
        mu = jnp.mean(z, axis=-1, keepdims=True)
        zc = z - mu
        var = jnp.mean(zc * zc, axis=-1, keepdims=True)
        rstd = lax.rsqrt(var + LN_EPS)
        xhat = zc * rstd
        hn = xhat * g_ref[...] + b_ref[...]
        hn_ref[...] = hn
        xhat_ref[...] = xhat.astype(BF16)
        rstd_ref[...] = rstd
        if has_next:
            sh = modn_ref[next_idx[0]:next_idx[0] + 1, :]
            sc = modn_ref[next_idx[1]:next_idx[1] + 1, :]
            xin_ref[...] = (hn * (1.0 + sc) + sh).astype(BF16)

    row = pl.BlockSpec((tm, D), lambda i: (i, 0))
    modspec = pl.BlockSpec((None, N_MOD, D), lambda i: (cfg.seg(i), 0, 0))
    vec = pl.BlockSpec((1, D), lambda i: (0, 0))
    in_specs = [row, row, modspec, vec, vec]
    args = [h, y, mod, lng, lnb]
    out_specs = [row, row, pl.BlockSpec((tm, 1), lambda i: (i, 0))]
    out_shape = [jax.ShapeDtypeStruct((cfg.T, D), F32), jax.ShapeDtypeStruct((cfg.T, D), BF16),
                 jax.ShapeDtypeStruct((cfg.T, 1), F32)]
    if has_next:
        in_specs.append(modspec)
        args.append(mod_next)
        out_specs.append(row)
        out_shape.append(jax.ShapeDtypeStruct((cfg.T, D), BF16))
    return pl.pallas_call(body, grid=(cfg.nt,), name=name, in_specs=in_specs, out_specs=out_specs,
                          out_shape=out_shape, compiler_params=_params(("parallel",)))(*args)


def _ln_bwd(cfg, up, xhat, rstd, y, mod, gate_idx, coef, lng, name):
    tm = cfg.tm
    fused = len(up) > 1
    scale_next = up[4] if fused else None

    def body(*refs):
        if fused:
            dres_n, dxin_n, b_ref, modn_ref, xhat_ref, rstd_ref, y_ref, mod_ref, g_ref, dres_ref, dys_ref, st_ref = refs
        else:
            dhn_ref, xhat_ref, rstd_ref, y_ref, mod_ref, g_ref, dres_ref, dys_ref, st_ref = refs
        i = pl.program_id(0)

        @pl.when(cfg.first_of_seg(i))
        def _():
            st_ref[...] = jnp.zeros_like(st_ref)

        xhat = xhat_ref[...].astype(F32)
        if fused:
            dxin = dxin_n[...].astype(F32)
            sc = modn_ref[scale_next:scale_next + 1, :]
            dhn = dres_n[...] + dxin * (1.0 + sc)
            shift_sum = jnp.sum(dxin, axis=0, keepdims=True)
            st_ref[3:4, :] += g_ref[...] * jnp.sum(dxin * xhat, axis=0, keepdims=True) + b_ref[...] * shift_sum
            st_ref[4:5, :] += shift_sum
        else:
            dhn = dhn_ref[...]
        gdh = dhn * g_ref[...]
        m1 = jnp.mean(gdh, axis=-1, keepdims=True)
        m2 = jnp.mean(gdh * xhat, axis=-1, keepdims=True)
        dz = rstd_ref[...] * (gdh - m1 - xhat * m2)
        gate = mod_ref[gate_idx:gate_idx + 1, :]
        dres_ref[...] = ALPHA * dz
        dys_ref[...] = ((coef * gate) * dz).astype(BF16)
        st_ref[0:1, :] += jnp.sum(dhn * xhat, axis=0, keepdims=True)
        st_ref[1:2, :] += jnp.sum(dhn, axis=0, keepdims=True)
        st_ref[2:3, :] += jnp.sum((coef * dz) * y_ref[...].astype(F32), axis=0, keepdims=True)

    row = pl.BlockSpec((tm, D), lambda i: (i, 0))
    modspec = pl.BlockSpec((None, N_MOD, D), lambda i: (cfg.seg(i), 0, 0))
    vec = pl.BlockSpec((1, D), lambda i: (0, 0))
    col = pl.BlockSpec((tm, 1), lambda i: (i, 0))
    if fused:
        in_specs = [row, row, vec, modspec, row, col, row, modspec, vec]
        args = [up[0], up[1], up[2], up[3], xhat, rstd, y, mod, lng]
    else:
        in_specs = [row, row, col, row, modspec, vec]
        args = [up[0], xhat, rstd, y, mod, lng]
    return pl.pallas_call(
        body, grid=(cfg.nt,), name=name, in_specs=in_specs,
        out_specs=[row, row, pl.BlockSpec((None, 8, D), lambda i: (cfg.seg(i), 0, 0))],
        out_shape=[jax.ShapeDtypeStruct((cfg.T, D), F32), jax.ShapeDtypeStruct((cfg.T, D), BF16),
                   jax.ShapeDtypeStruct((3, 8, D), F32)],
        compiler_params=_params(("arbitrary",)))(*args)


def _modulate_bwd(cfg, dres, dxin, h, mod, scale_idx, name):
    tm = cfg.tm
    n_lt = 2 * cfg.nlt

    def body(dres_ref, dxin_ref, h_ref, mod_ref, dh_ref, st_ref):
        i = pl.program_id(0)

        @pl.when(cfg.first_of_seg(i))
        def _():
            st_ref[...] = jnp.zeros_like(st_ref)

        dxin = dxin_ref[...].astype(F32)
        sc = mod_ref[scale_idx:scale_idx + 1, :]

        @pl.when(i < n_lt)
        def _():
            dh_ref[...] = dres_ref[...] + dxin * (1.0 + sc)

        st_ref[3:4, :] += jnp.sum(dxin * h_ref[...], axis=0, keepdims=True)
        st_ref[4:5, :] += jnp.sum(dxin, axis=0, keepdims=True)

    row = pl.BlockSpec((tm, D), lambda i: (i, 0))
    return pl.pallas_call(
        body, grid=(cfg.nt,), name=name,
        in_specs=[row, row, row, pl.BlockSpec((None, N_MOD, D), lambda i: (cfg.seg(i), 0, 0))],
        out_specs=[pl.BlockSpec((tm, D), lambda i: (jnp.minimum(i, n_lt - 1), 0)),
                   pl.BlockSpec((None, 8, D), lambda i: (cfg.seg(i), 0, 0))],
        out_shape=[jax.ShapeDtypeStruct((cfg.t_lat, D), F32), jax.ShapeDtypeStruct((3, 8, D), F32)],
        compiler_params=_params(("arbitrary",)))(dres, dxin, h, mod)


def _loss(cfg, h, target, name):
    tm = cfg.tm
    n_lt = 2 * cfg.nlt

    def body(h_ref, t_ref, dy_ref, l_ref):
        i = pl.program_id(0)

        @pl.when(i == 0)
        def _():
            l_ref[...] = jnp.zeros_like(l_ref)

        @pl.when(i < n_lt)
        def _():
            err = h_ref[...] - t_ref[...]
            dy_ref[...] = err * (1.0 / D)
            part = jnp.sum(jnp.sum(err * err, axis=1, keepdims=True), axis=0, keepdims=True) * (0.5 / D)
            l_ref[...] += jnp.broadcast_to(part, l_ref.shape)

        @pl.when(i >= n_lt)
        def _():
            dy_ref[...] = jnp.zeros_like(dy_ref)

    return pl.pallas_call(
        body, grid=(cfg.nt,), name=name,
        in_specs=[pl.BlockSpec((tm, D), lambda i: (i, 0)),
                  pl.BlockSpec((tm, D), lambda i: (jnp.minimum(i, n_lt - 1), 0))],
        out_specs=[pl.BlockSpec((tm, D), lambda i: (i, 0)), pl.BlockSpec((8, 128), lambda i: (0, 0))],
        out_shape=[jax.ShapeDtypeStruct((cfg.T, D), F32), jax.ShapeDtypeStruct((8, 128), F32)],
        compiler_params=_params(("arbitrary",)))(h, target)


def _matmul(a, b, mode, out_dtype, name, bm_cap=512, bn_cap=1408, bk_cap=1024):
    if mode == "nn":
        (M, K), N = a.shape, b.shape[1]
    elif mode == "nt":
        (M, K), N = a.shape, b.shape[0]
    else:
        (K, M), N = a.shape, b.shape[1]
    bm, bn, bk = _pick(M, bm_cap), _pick(N, bn_cap), _pick(K, bk_cap)
    nk = K // bk

    def body(a_ref, b_ref, o_ref, acc_ref=None):
        k = pl.program_id(2)
        if mode == "nn":
            part = _dot(a_ref[...], b_ref[...])
        elif mode == "nt":
            part = _dot_nt(a_ref[...], b_ref[...])
        else:
            part = _dot_tn(a_ref[...], b_ref[...])
        if nk == 1:
            o_ref[...] = part.astype(out_dtype)
            return

        @pl.when(k == 0)
        def _():
            acc_ref[...] = part

        @pl.when((k > 0) & (k < nk - 1))
        def _():
            acc_ref[...] += part

        @pl.when(k == nk - 1)
        def _():
            o_ref[...] = (acc_ref[...] + part).astype(out_dtype)

    if mode == "nn":
        a_spec = pl.BlockSpec((bm, bk), lambda i, j, k: (i, k))
        b_spec = pl.BlockSpec((bk, bn), lambda i, j, k: (k, j))
    elif mode == "nt":
        a_spec = pl.BlockSpec((bm, bk), lambda i, j, k: (i, k))
        b_spec = pl.BlockSpec((bn, bk), lambda i, j, k: (j, k))
    else:
        a_spec = pl.BlockSpec((bk, bm), lambda i, j, k: (k, i))
        b_spec = pl.BlockSpec((bk, bn), lambda i, j, k: (k, j))
    return pl.pallas_call(
        body, grid=(M // bm, N // bn, nk), name=name, in_specs=[a_spec, b_spec],
        out_specs=pl.BlockSpec((bm, bn), lambda i, j, k: (i, j)),
        out_shape=jax.ShapeDtypeStruct((M, N), out_dtype),
        scratch_shapes=[pltpu.VMEM((bm, bn), F32)] if nk > 1 else [],
        compiler_params=_params(("parallel", "parallel", "arbitrary")))(a, b)


def _ffn_tile(T, cap):
    best = 256
    for t in range(256, cap + 1, 256):
        if T % t == 0:
            best = t
    return best


def _ffn_fwd(xin, wf, name):
    T = xin.shape[0]
    F = wf.shape[1]
    tm, tf = _ffn_tile(T, 768), F // 2
    assert tf % 128 == 0 and T % tm == 0

    def body(x_ref, wg_ref, wu_ref, wd_ref, g_ref, u_ref, y_ref, acc_ref):
        j = pl.program_id(1)
        x = x_ref[...]
        acc = None
        for c0, cw in _chunks(tf):
            g = _dot_nt(x, wg_ref[c0:c0 + cw, :])
            u = _dot_nt(x, wu_ref[c0:c0 + cw, :])
            g_ref[:, c0:c0 + cw] = g.astype(BF16)
            u_ref[:, c0:c0 + cw] = u.astype(BF16)
            part = _dot(g * _sigmoid(g) * u, wd_ref[c0:c0 + cw, :])
            acc = part if acc is None else acc + part

        @pl.when(j == 0)
        def _():
            acc_ref[...] = acc

        @pl.when(j == 1)
        def _():
            y_ref[...] = (acc_ref[...] + acc).astype(BF16)

    return pl.pallas_call(
        body, grid=(T // tm, 2), name=name,
        in_specs=[pl.BlockSpec((tm, D), lambda i, j: (i, 0)),
                  pl.BlockSpec((None, tf, D), lambda i, j: (0, j, 0)),
                  pl.BlockSpec((None, tf, D), lambda i, j: (1, j, 0)),
                  pl.BlockSpec((None, tf, D), lambda i, j: (2, j, 0))],
        out_specs=[pl.BlockSpec((tm, tf), lambda i, j: (i, j)),
                   pl.BlockSpec((tm, tf), lambda i, j: (i, j)),
                   pl.BlockSpec((tm, D), lambda i, j: (i, 0))],
        out_shape=[jax.ShapeDtypeStruct((T, F), BF16), jax.ShapeDtypeStruct((T, F), BF16),
                   jax.ShapeDtypeStruct((T, D), BF16)],
        scratch_shapes=[pltpu.VMEM((tm, D), F32)],
        compiler_params=_params(("parallel", "arbitrary")))(xin, wf, wf, wf)


def _ffn_bwd(dys, g, u, wf, name):
    T = dys.shape[0]
    F = wf.shape[1]
    tm, tf = _ffn_tile(T, 512), F // 2

    def body(dy_ref, g_ref, u_ref, wg_ref, wu_ref, wd_ref, dg_ref, du_ref, a_ref, dx_ref, acc_ref):
        j = pl.program_id(1)
        da_all = _dot_nt(dy_ref[...], wd_ref[...])
        for c0, cw in _chunks(tf):
            gg = g_ref[:, c0:c0 + cw].astype(F32)
            uu = u_ref[:, c0:c0 + cw].astype(F32)
            da = da_all[:, c0:c0 + cw]
            s = _sigmoid(gg)
            silu = gg * s
            a_ref[:, c0:c0 + cw] = (silu * uu).astype(BF16)
            du_ref[:, c0:c0 + cw] = (da * silu).astype(BF16)
            dg_ref[:, c0:c0 + cw] = (da * uu * (s * (1.0 + gg * (1.0 - s)))).astype(BF16)
        acc = _dot(dg_ref[...], wg_ref[...]) + _dot(du_ref[...], wu_ref[...])

        @pl.when(j == 0)
        def _():
            acc_ref[...] = acc

        @pl.when(j == 1)
        def _():
            dx_ref[...] = (acc_ref[...] + acc).astype(BF16)

    blk = pl.BlockSpec((tm, tf), lambda i, j: (i, j))
    return pl.pallas_call(
        body, grid=(T // tm, 2), name=name,
        in_specs=[pl.BlockSpec((tm, D), lambda i, j: (i, 0)), blk, blk,
                  pl.BlockSpec((None, tf, D), lambda i, j: (0, j, 0)),
                  pl.BlockSpec((None, tf, D), lambda i, j: (1, j, 0)),
                  pl.BlockSpec((None, tf, D), lambda i, j: (2, j, 0))],
        out_specs=[blk, blk, blk, pl.BlockSpec((tm, D), lambda i, j: (i, 0))],
        out_shape=[jax.ShapeDtypeStruct((T, F), BF16), jax.ShapeDtypeStruct((T, F), BF16),
                   jax.ShapeDtypeStruct((T, F), BF16), jax.ShapeDtypeStruct((T, D), BF16)],
        scratch_shapes=[pltpu.VMEM((tm, D), F32)],
        compiler_params=_params(("parallel", "arbitrary")))(dys, g, u, wf, wf, wf)


def _swap_halves(x):
    w = x.shape[1]
    lane = lax.broadcasted_iota(jnp.int32, (1, w), 1)
    return jnp.where((lane & 63) < 32, pltpu.roll(x, w - 32, 1), pltpu.roll(x, 32, 1))


def _rope(x, cos, sin):
    return x * cos + _swap_halves(x) * sin


def _rope_t(dy, cos, sin):
    return dy * cos + _swap_halves(dy * sin)


def _rope_tables(n_lat):
    rows = n_lat // GRID_W
    row = jnp.repeat(jnp.arange(rows, dtype=F32), GRID_W)
    col = jnp.tile(jnp.arange(GRID_W, dtype=F32), rows)
    inv = ROPE_THETA ** (-jnp.arange(ROPE_FREQS, dtype=F32) / ROPE_FREQS)
    ang = jnp.concatenate([row[:, None] * inv, col[:, None] * inv], axis=-1)
    cs, sn = jnp.cos(ang), jnp.sin(ang)
    cos = jnp.concatenate([cs, cs, cs, cs], axis=-1)
    sin = jnp.concatenate([-sn, sn, -sn, sn], axis=-1)
    return cos, sin


def _attn_specs(cfg):
    n_lat, n_ctx, cb = cfg.n_lat, cfg.n_ctx, cfg.ctx_blk
    return [pl.BlockSpec((n_lat, ATT_W), lambda e: (e, 0)),
            pl.BlockSpec((n_lat, 128), lambda e: (e, 4)),
            pl.BlockSpec((n_lat, 128), lambda e: (e, 5)),
            pl.BlockSpec((n_ctx, ATT_W), lambda e: (cb + e, 0)),
            pl.BlockSpec((n_ctx, 128), lambda e: (cb + e, 4)),
            pl.BlockSpec((n_ctx, 128), lambda e: (cb + e, 5)),
            pl.BlockSpec((n_lat, 128), lambda e: (0, 0)),
            pl.BlockSpec((n_lat, 128), lambda e: (0, 0)),
            pl.BlockSpec((8, 128), lambda e: (0, 0))]


def _attn_prepare(kh, kl, vl, kc, vc, ka, kb, va, vb, kca, kcb, vca, vcb):
    lane = lax.broadcasted_iota(jnp.int32, (1, 128), 1)
    own = (lane < 64) if kh == 0 else (lane >= 64)

    def split(x, ra, rb):
        mine = jnp.where(own, x, 0.0)
        other = pltpu.roll(mine, 64, 1)
        a, b = (mine, other) if kh == 0 else (other, mine)
        ra[...] = a.astype(BF16)
        rb[...] = b.astype(BF16)

    split(kl, ka, kb)
    split(vl, va, vb)
    split(kc, kca, kcb)
    split(vc, vca, vcb)


def _softmax_parts(s_list, sk):
    m = sk
    for s in s_list:
        m = jnp.maximum(m, jnp.max(s, axis=1, keepdims=True))
    es = [jnp.exp(s - m) for s in s_list]
    esk = jnp.exp(sk - m)
    den = esk
    for e in es:
        den = den + jnp.sum(e, axis=1, keepdims=True)
    inv = 1.0 / den
    return [e * inv for e in es], esk * inv


def _window(cfg, n):
    r0 = pl.multiple_of(n * BLK, BLK)
    start = pl.multiple_of(jnp.clip((n - 1) * BLK, 0, cfg.n_lat - 3 * BLK), BLK)
    qpos = r0 + lax.broadcasted_iota(jnp.int32, (BLK, 1), 0)
    kpos = start + lax.broadcasted_iota(jnp.int32, (1, 3 * BLK), 1)
    valid = jnp.abs(qpos - kpos) <= BLK
    return r0, start, valid


def _attn_fwd(cfg, p, cos, sin, sink_rows, name):
    n_lat, n_ctx = cfg.n_lat, cfg.n_ctx

    def body(q_ref, k_ref, v_ref, qc_ref, kc_ref, vc_ref, cos_ref, sin_ref, sink_ref, o_ref, oc_ref,
             qr, ka, kb, va, vb, kca, kcb, vca, vcb):
        cos_t, sin_t = cos_ref[...], sin_ref[...]
        for gq in range(4):
            qr[:, gq * 128:(gq + 1) * 128] = _rope(q_ref[:, gq * 128:(gq + 1) * 128], cos_t, sin_t).astype(BF16)
        kl = _rope(k_ref[...], cos_t, sin_t)
        for kh in range(KV_HEADS):
            _attn_prepare(kh, kl, v_ref[...], kc_ref[...], vc_ref[...], ka, kb, va, vb, kca, kcb, vca, vcb)

            def lat_block(n, carry):
                r0, start, valid = _window(cfg, n)
                win = pl.ds(start, 3 * BLK)
                lanes = [slice((kh * 2 + pr) * 128, (kh * 2 + pr + 1) * 128) for pr in range(2)]
                qps = [qr[pl.ds(r0, BLK), lanes[pr]] for pr in range(2)]
                kws, kcs = (ka[win, :], kb[win, :]), (kca[...], kcb[...])
                scores = [(jnp.where(valid, _dot_nt(qps[pr], kws[half]) * ATT_SCALE, NEG_INF),
                           _dot_nt(qps[pr], kcs[half]) * ATT_SCALE) for pr in range(2) for half in range(2)]
                probs = []
                for idx, (s_w, s_c) in enumerate(scores):
                    head = kh * 4 + idx
                    (p_w, p_c), _ = _softmax_parts([s_w, s_c], sink_ref[head:head + 1, 0:1])
                    probs.append((p_w.astype(BF16), p_c.astype(BF16)))
                vws, vcs = (va[win, :], vb[win, :]), (vca[...], vcb[...])
                for pr in range(2):
                    o = (_dot(probs[2 * pr][0], vws[0]) + _dot(probs[2 * pr][1], vcs[0])
                         + _dot(probs[2 * pr + 1][0], vws[1]) + _dot(probs[2 * pr + 1][1], vcs[1]))
                    o_ref[pl.ds(r0, BLK), lanes[pr]] = o.astype(BF16)
                return carry

            lax.fori_loop(0, n_lat // BLK, lat_block, 0, unroll=2)
            for n in range(n_ctx // BLK):
                rows = slice(n * BLK, (n + 1) * BLK)
                for pr in range(2):
                    lanes = slice((kh * 2 + pr) * 128, (kh * 2 + pr + 1) * 128)
                    qp = qc_ref[rows, lanes]
                    o = None
                    for half, (kcx, vcx) in enumerate(((kca, vca), (kcb, vcb))):
                        head = kh * 4 + pr * 2 + half
                        s_c = _dot_nt(qp, kcx[...]) * ATT_SCALE
                        (p_c,), _ = _softmax_parts([s_c], sink_ref[head:head + 1, 0:1])
                        part = _dot(p_c, vcx[...])
                        o = part if o is None else o + part
                    oc_ref[rows, lanes] = o.astype(BF16)

    return pl.pallas_call(
        body, grid=(2,), name=name, in_specs=_attn_specs(cfg),
        out_specs=[pl.BlockSpec((n_lat, ATT_W), lambda e: (e, 0)), pl.BlockSpec((n_ctx, ATT_W), lambda e: (e, 0))],
        out_shape=[jax.ShapeDtypeStruct((cfg.t_lat, ATT_W), BF16), jax.ShapeDtypeStruct((cfg.t_ctx, ATT_W), BF16)],
        scratch_shapes=[pltpu.VMEM((n_lat, ATT_W), BF16)] + [pltpu.VMEM((n_lat, 128), BF16)] * 4
        + [pltpu.VMEM((n_ctx, 128), BF16)] * 4,
        compiler_params=_params(("parallel",)))(p, p, p, p, p, p, cos, sin, sink_rows)


def _attn_bwd(cfg, p, dcat, cos, sin, sink_rows, name):
    n_lat, n_ctx, cb = cfg.n_lat, cfg.n_ctx, cfg.ctx_blk

    def body(q_ref, k_ref, v_ref, qc_ref, kc_ref, vc_ref, cos_ref, sin_ref, sink_ref, do_ref, doc_ref,
             dq_ref, dk_ref, dv_ref, dqc_ref, dkc_ref, dvc_ref, dsink_ref,
             qr, ka, kb, va, vb, kca, kcb, vca, vcb, dqs, dka, dva, dkca, dvca):
        cos_t, sin_t = cos_ref[...], sin_ref[...]
        lane = lax.broadcasted_iota(jnp.int32, (1, 128), 1)
        lo = lane < 64
        for gq in range(4):
            qr[:, gq * 128:(gq + 1) * 128] = _rope(q_ref[:, gq * 128:(gq + 1) * 128], cos_t, sin_t).astype(BF16)
        kl = _rope(k_ref[...], cos_t, sin_t)
        dsink_ref[...] = jnp.zeros_like(dsink_ref)
        dka[...] = jnp.zeros_like(dka)
        dva[...] = jnp.zeros_like(dva)
        dkca[...] = jnp.zeros_like(dkca)
        dvca[...] = jnp.zeros_like(dvca)

        def halves(x):
            return jnp.where(lo, x, 0).astype(BF16), jnp.where(lo, 0, x).astype(BF16)

        for kh in range(KV_HEADS):
            _attn_prepare(kh, kl, v_ref[...], kc_ref[...], vc_ref[...], ka, kb, va, vb, kca, kcb, vca, vcb)

            def one_head(head, qp, q_half, do_p, do_half, kw, kcx, vw, vcx, win, valid):
                sk = sink_ref[head:head + 1, 0:1]
                s_list = [_dot_nt(qp, kcx[...]) * ATT_SCALE]
                if win is not None:
                    s_list.insert(0, jnp.where(valid, _dot_nt(qp, kw[win, :]) * ATT_SCALE, NEG_INF))
                probs, p_sink = _softmax_parts(s_list, sk)
                vals = [vcx[...]] if win is None else [vw[win, :], vcx[...]]
                dps = [_dot_nt(do_p, vv) for vv in vals]
                dr = None
                for pp, dp in zip(probs, dps):
                    t = jnp.sum(pp * dp, axis=1, keepdims=True)
                    dr = t if dr is None else dr + t
                dss = [(pp * (dp - dr) * ATT_SCALE).astype(BF16) for pp, dp in zip(probs, dps)]
                dsink_ref[head:head + 1, :] += jnp.broadcast_to(
                    jnp.sum(-p_sink * dr, axis=0, keepdims=True), (1, 128))
                p_c, ds_c = probs[-1], dss[-1]
                dq = _dot(ds_c, kcx[...])
                dkca[kh] += _dot_tn(ds_c, q_half)
                dvca[kh] += _dot_tn(p_c, do_half)
                if win is not None:
                    dq = dq + _dot(dss[0], kw[win, :])
                    dka[kh, win, :] += _dot_tn(dss[0], q_half)
                    dva[kh, win, :] += _dot_tn(probs[0], do_half)
                return dq

            def lat_block(n, carry):
                r0, start, valid = _window(cfg, n)
                win = pl.ds(start, 3 * BLK)
                lanes = [slice((kh * 2 + pr) * 128, (kh * 2 + pr + 1) * 128) for pr in range(2)]
                qps = [qr[pl.ds(r0, BLK), lanes[pr]] for pr in range(2)]
                dops = [do_ref[pl.ds(r0, BLK), lanes[pr]].astype(BF16) for pr in range(2)]
                heads = [(pr, half) for pr in range(2) for half in range(2)]
                kws, kcs = (ka[win, :], kb[win, :]), (kca[...], kcb[...])
                vws, vcs = (va[win, :], vb[win, :]), (vca[...], vcb[...])
                soft = []
                for idx, (pr, half) in enumerate(heads):
                    s_w = jnp.where(valid, _dot_nt(qps[pr], kws[half]) * ATT_SCALE, NEG_INF)
                    s_c = _dot_nt(qps[pr], kcs[half]) * ATT_SCALE
                    soft.append(_softmax_parts([s_w, s_c], sink_ref[kh * 4 + idx:kh * 4 + idx + 1, 0:1]))
                dps = [(_dot_nt(dops[pr], vws[half]), _dot_nt(dops[pr], vcs[half])) for pr, half in heads]
                ds_w, ds_c, pb_w, pb_c = [], [], [], []
                for idx in range(4):
                    (p_w, p_c), p_sink = soft[idx]
                    dp_w, dp_c = dps[idx]
                    dr = jnp.sum(p_w * dp_w, axis=1, keepdims=True) + jnp.sum(p_c * dp_c, axis=1, keepdims=True)
                    ds_w.append((p_w * (dp_w - dr) * ATT_SCALE).astype(BF16))
                    ds_c.append((p_c * (dp_c - dr) * ATT_SCALE).astype(BF16))
                    pb_w.append(p_w.astype(BF16))
                    pb_c.append(p_c.astype(BF16))
                    head = kh * 4 + idx
                    dsink_ref[head:head + 1, :] += jnp.broadcast_to(
                        jnp.sum(-p_sink * dr, axis=0, keepdims=True), (1, 128))
                for pr in range(2):
                    dqs[pl.ds(r0, BLK), lanes[pr]] = (
                        _dot(ds_w[2 * pr], kws[0]) + _dot(ds_c[2 * pr], kcs[0])
                        + _dot(ds_w[2 * pr + 1], kws[1]) + _dot(ds_c[2 * pr + 1], kcs[1]))
                q_hs, do_hs = [halves(qp) for qp in qps], [halves(do_p) for do_p in dops]
                q_stack = jnp.concatenate([q_hs[pr][half] for pr, half in heads], axis=0)
                do_stack = jnp.concatenate([do_hs[pr][half] for pr, half in heads], axis=0)
                dka[kh, win, :] += _dot_tn(jnp.concatenate(ds_w, axis=0), q_stack)
                dva[kh, win, :] += _dot_tn(jnp.concatenate(pb_w, axis=0), do_stack)
                dkca[kh] += _dot_tn(jnp.concatenate(ds_c, axis=0), q_stack)
                dvca[kh] += _dot_tn(jnp.concatenate(pb_c, axis=0), do_stack)
                return carry

            lax.fori_loop(0, n_lat // BLK, lat_block, 0, unroll=2)
            for n in range(n_ctx // BLK):
                rows = slice(n * BLK, (n + 1) * BLK)
                for pr in range(2):
                    lanes = slice((kh * 2 + pr) * 128, (kh * 2 + pr + 1) * 128)
                    qp = qc_ref[rows, lanes].astype(BF16)
                    do_p = doc_ref[rows, lanes]
                    q_h, do_h = halves(qp), halves(do_p)
                    dq = None
                    for half, (kcx, vcx) in enumerate(((kca, vca), (kcb, vcb))):
                        part = one_head(kh * 4 + pr * 2 + half, qp, q_h[half], do_p, do_h[half],
                                        None, kcx, None, vcx, None, None)
                        dq = part if dq is None else dq + part
                    dqc_ref[rows, lanes] = dq.astype(BF16)

        def fold(acc):
            r0 = acc[0] + pltpu.roll(acc[0], 64, 1)
            r1 = acc[1] + pltpu.roll(acc[1], 64, 1)
            return jnp.where(lo, r0, r1)

        for gq in range(4):
            sl = slice(gq * 128, (gq + 1) * 128)
            dq_ref[:, sl] = _rope_t(dqs[:, sl], cos_t, sin_t).astype(BF16)
        dk_ref[...] = _rope_t(fold(dka), cos_t, sin_t).astype(BF16)
        dv_ref[...] = fold(dva).astype(BF16)
        dkc_ref[...] = fold(dkca).astype(BF16)
        dvc_ref[...] = fold(dvca).astype(BF16)

    lat = lambda w: pl.BlockSpec((n_lat, w), lambda e: (e, 0))
    ctx = lambda w: pl.BlockSpec((n_ctx, w), lambda e: (e, 0))
    sd = jax.ShapeDtypeStruct
    return pl.pallas_call(
        body, grid=(2,), name=name,
        in_specs=_attn_specs(cfg) + [pl.BlockSpec((n_lat, ATT_W), lambda e: (e, 0)),
                                     pl.BlockSpec((n_ctx, ATT_W), lambda e: (cb + e, 0))],
        out_specs=[lat(ATT_W), lat(128), lat(128), ctx(ATT_W), ctx(128), ctx(128),
                   pl.BlockSpec((None, 8, 128), lambda e: (e, 0, 0))],
        out_shape=[sd((cfg.t_lat, ATT_W), BF16), sd((cfg.t_lat, 128), BF16), sd((cfg.t_lat, 128), BF16),
                   sd((cfg.t_ctx, ATT_W), BF16), sd((cfg.t_ctx, 128), BF16), sd((cfg.t_ctx, 128), BF16),
                   sd((2, 8, 128), F32)],
        scratch_shapes=[pltpu.VMEM((n_lat, ATT_W), BF16)] + [pltpu.VMEM((n_lat, 128), BF16)] * 4
        + [pltpu.VMEM((n_ctx, 128), BF16)] * 4
        + [pltpu.VMEM((n_lat, ATT_W), F32), pltpu.VMEM((2, n_lat, 128), F32), pltpu.VMEM((2, n_lat, 128), F32),
           pltpu.VMEM((2, n_ctx, 128), F32), pltpu.VMEM((2, n_ctx, 128), F32)],
        compiler_params=_params(("parallel",)))(p, p, p, p, p, p, cos, sin, sink_rows, dcat, dcat)


def _shift_down(x, k, row):
    return jnp.where(row >= k, pltpu.roll(x, k, 0), 0.0)


def _shift_up(x, k, row):
    n = x.shape[0]
    return jnp.where(row < n - k, pltpu.roll(x, n - k, 0), 0.0)


def _window_sum(x, r, row):
    below, above, k = x, x, 1
    while k < r:
        below = below + _shift_down(below, k, row)
        above = above + _shift_up(above, k, row)
        k *= 2
    return below + _shift_down(x, r, row) + _shift_up(above, 1, row)


def _inv_count(r, row, n):
    cnt = jnp.minimum(row + r, n - 1) + 1 - jnp.maximum(row - r, 0)
    return 1.0 / cnt.astype(F32)


def _pool_fwd(p, w, scale, n, blk0, n_seg, name):
    def body(u0, u1, u2, u3, w_ref, sc_ref, o_ref):
        row = lax.broadcasted_iota(jnp.int32, (n, 1), 0)
        for g, u_ref in enumerate((u0, u1, u2, u3)):
            u = u_ref[...]
            d = _window_sum(u, POOL_R[g], row) * _inv_count(POOL_R[g], row, n) - u
            o_ref[:, g * 128:(g + 1) * 128] = (_dot(d, w_ref[g]) * sc_ref[:, g * 128:(g + 1) * 128]).astype(BF16)

    return pl.pallas_call(
        body, grid=(n_seg,), name=name,
        in_specs=[pl.BlockSpec((n, 128), functools.partial(lambda g, e: (blk0 + e, 6 + g), g)) for g in range(4)]
        + [pl.BlockSpec((4, 128, 128), lambda e: (0, 0, 0)), pl.BlockSpec((1, 512), lambda e: (0, 0))],
        out_specs=pl.BlockSpec((n, 512), lambda e: (e, 0)),
        out_shape=jax.ShapeDtypeStruct((n_seg * n, 512), BF16),
        compiler_params=_params(("parallel",)))(p, p, p, p, w, scale)


def _pool_bwd(p, w, scale, dcat, n, blk0, n_seg, name):
    def body(u0, u1, u2, u3, w_ref, sc_ref, dp_ref, du_ref, dw_ref, dsc_ref):
        e = pl.program_id(0)

        @pl.when(e == 0)
        def _():
            dw_ref[...] = jnp.zeros_like(dw_ref)
            dsc_ref[...] = jnp.zeros_like(dsc_ref)

        row = lax.broadcasted_iota(jnp.int32, (n, 1), 0)
        for g, u_ref in enumerate((u0, u1, u2, u3)):
            sl = slice(g * 128, (g + 1) * 128)
            u = u_ref[...]
            inv = _inv_count(POOL_R[g], row, n)
            d = _window_sum(u, POOL_R[g], row) * inv - u
            dp = dp_ref[:, sl]
            dsc_ref[:, sl] += jnp.sum(dp * _dot(d, w_ref[g]), axis=0, keepdims=True)
            dyp = dp * sc_ref[:, sl]
            dw_ref[g] += _dot_tn(d, dyp)
            dd = _dot_nt(dyp, w_ref[g])
            du_ref[:, sl] = (_window_sum(dd * inv, POOL_R[g], row) - dd).astype(BF16)

    return pl.pallas_call(
        body, grid=(n_seg,), name=name,
        in_specs=[pl.BlockSpec((n, 128), functools.partial(lambda g, e: (blk0 + e, 6 + g), g)) for g in range(4)]
        + [pl.BlockSpec((4, 128, 128), lambda e: (0, 0, 0)), pl.BlockSpec((1, 512), lambda e: (0, 0)),
           pl.BlockSpec((n, 512), lambda e: (blk0 + e, 1))],
        out_specs=[pl.BlockSpec((n, 512), lambda e: (e, 0)),
                   pl.BlockSpec((4, 128, 128), lambda e: (0, 0, 0)), pl.BlockSpec((1, 512), lambda e: (0, 0))],
        out_shape=[jax.ShapeDtypeStruct((n_seg * n, 512), BF16), jax.ShapeDtypeStruct((4, 128, 128), F32),
                   jax.ShapeDtypeStruct((1, 512), F32)],
        compiler_params=_params(("arbitrary",)))(p, p, p, p, w, scale, dcat)


def _gelu(x):
    t = jnp.tanh(math.sqrt(2.0 / math.pi) * (x + 0.044715 * x * x * x))
    return 0.5 * x * (1.0 + t), t


def _gelu_grad(x, t):
    return 0.5 * (1.0 + t) + 0.5 * x * (1.0 - t * t) * (math.sqrt(2.0 / math.pi) * (1.0 + 3 * 0.044715 * x * x))


def _neg_expm1(x):
    series = -x * (1.0 + x * (0.5 + x * (1.0 / 6.0 + x * (1.0 / 24.0 + x * (1.0 / 120.0)))))
    return jnp.where(x > -0.05, series, 1.0 - jnp.exp(x))


def _softplus_neg(lam):
    x = -lam
    e = jnp.exp(-jnp.abs(x))
    log1p = jnp.where(e < 1e-2, e * (1.0 - e * (0.5 - e * (1.0 / 3.0))), jnp.log(1.0 + e))
    return jnp.maximum(x, 0.0) + log1p, -_sigmoid(x)


def _conv(u, w_ref, b_ref, row):
    return (b_ref[...] + _shift_down(u, 1, row) * w_ref[0:1, :] + u * w_ref[1:2, :]
            + _shift_up(u, 1, row) * w_ref[2:3, :] + _shift_up(u, 2, row) * w_ref[3:4, :])


def _lru_gates(uc, d, wa_ref, ba_ref, wx_ref, bx_ref, lam_ref):
    r = _sigmoid(_dot(uc, wa_ref[d]) + ba_ref[d:d + 1, :])
    gi = _sigmoid(_dot(uc, wx_ref[d]) + bx_ref[d:d + 1, :])
    sp, dsp = _softplus_neg(lam_ref[d:d + 1, :])
    la = (-LRU_C) * r * sp
    a = jnp.exp(la)
    sq = jnp.sqrt(_neg_expm1(2.0 * la))
    return r, gi, sp, dsp, a, sq


def _tile_scan(a_ref, b_ref, n, reverse):
    m = n // 8
    first = 7 if reverse else 0
    a_prev = a_ref[pl.ds(first, m, stride=8), :]
    b_prev = b_ref[pl.ds(first, m, stride=8), :]
    for j in (range(6, -1, -1) if reverse else range(1, 8)):
        rows = pl.ds(j, m, stride=8)
        aj = a_ref[rows, :]
        b_prev = aj * b_prev + b_ref[rows, :]
        a_prev = aj * a_prev
        b_ref[rows, :] = b_prev
        a_ref[rows, :] = a_prev


def _carry_scan(a_ref, b_ref, n, reverse, carry):
    nt8 = n // 8

    def step(i, c):
        t = (nt8 - 1 - i) if reverse else i
        off = pl.multiple_of(t * 8, 8)
        h = a_ref[pl.ds(off, 8), :] * c + b_ref[pl.ds(off, 8), :]
        b_ref[pl.ds(off, 8), :] = h
        return h[0:1, :] if reverse else h[7:8, :]

    return lax.fori_loop(0, nt8, step, carry, unroll=4)


def _chain_scan(segs, reverse):
    carry = jnp.zeros((1, 128), F32)
    for a, b, a_ref, b_ref, n in segs:
        a_ref[...] = a
        b_ref[...] = b
        _tile_scan(a_ref, b_ref, n, reverse)
        carry = _carry_scan(a_ref, b_ref, n, reverse, carry)


def _lru_specs(cfg):
    n_lat, n_ctx, cb = cfg.n_lat, cfg.n_ctx, cfg.ctx_blk
    return [pl.BlockSpec((n_lat, 128), lambda hb, e: (e, hb)),
            pl.BlockSpec((n_lat, 128), lambda hb, e: (e, 8 + hb)),
            pl.BlockSpec((n_ctx, 128), lambda hb, e: (cb + e, hb)),
            pl.BlockSpec((n_ctx, 128), lambda hb, e: (cb + e, 8 + hb)),
            pl.BlockSpec((4, 128), lambda hb, e: (0, hb)),
            pl.BlockSpec((1, 128), lambda hb, e: (0, hb)),
            pl.BlockSpec((2, None, 128, 128), lambda hb, e: (0, hb, 0, 0)),
            pl.BlockSpec((2, 128), lambda hb, e: (0, hb)),
            pl.BlockSpec((2, None, 128, 128), lambda hb, e: (0, hb, 0, 0)),
            pl.BlockSpec((2, 128), lambda hb, e: (0, hb)),
            pl.BlockSpec((2, 128), lambda hb, e: (0, hb))]


def _lru_fwd(cfg, p, consts, name):
    n_lat, n_ctx = cfg.n_lat, cfg.n_ctx

    def body(gl_ref, ul_ref, gc_ref, uc_ref, cw_ref, cb_ref, wa_ref, ba_ref, wx_ref, bx_ref, lam_ref,
             zl_ref, zc_ref, hl_ref, hc_ref, al, ac):
        row_l = lax.broadcasted_iota(jnp.int32, (n_lat, 1), 0)
        row_c = lax.broadcasted_iota(jnp.int32, (n_ctx, 1), 0)
        uc_l = _conv(ul_ref[...], cw_ref, cb_ref, row_l)
        uc_c = _conv(uc_ref[...], cw_ref, cb_ref, row_c)
        for d in range(2):
            _, gi_l, _, _, a_l, sq_l = _lru_gates(uc_l, d, wa_ref, ba_ref, wx_ref, bx_ref, lam_ref)
            _, gi_c, _, _, a_c, sq_c = _lru_gates(uc_c, d, wa_ref, ba_ref, wx_ref, bx_ref, lam_ref)
            _chain_scan([(a_c, sq_c * (gi_c * uc_c), ac, hc_ref.at[d], n_ctx),
                         (a_l, sq_l * (gi_l * uc_l), al, hl_ref.at[d], n_lat)], reverse=(d == 1))
        zl_ref[...] = (_gelu(gl_ref[...])[0] * (hl_ref[0] + hl_ref[1])).astype(BF16)
        zc_ref[...] = (_gelu(gc_ref[...])[0] * (hc_ref[0] + hc_ref[1])).astype(BF16)

    return pl.pallas_call(
        body, grid=(8, 2), name=name, in_specs=_lru_specs(cfg),
        out_specs=[pl.BlockSpec((n_lat, 128), lambda hb, e: (e, hb)), pl.BlockSpec((n_ctx, 128), lambda hb, e: (e, hb)),
                   pl.BlockSpec((2, n_lat, 128), lambda hb, e: (0, e, hb)),
                   pl.BlockSpec((2, n_ctx, 128), lambda hb, e: (0, e, hb))],
        out_shape=[jax.ShapeDtypeStruct((cfg.t_lat, D), BF16), jax.ShapeDtypeStruct((cfg.t_ctx, D), BF16),
                   jax.ShapeDtypeStruct((2, cfg.t_lat, D), F32), jax.ShapeDtypeStruct((2, cfg.t_ctx, D), F32)],
        scratch_shapes=[pltpu.VMEM((n_lat, 128), F32), pltpu.VMEM((n_ctx, 128), F32)],
        compiler_params=_params(("parallel", "arbitrary")))(p, p, p, p, *consts)


def _lru_bwd(cfg, p, dz, h_lat, h_ctx, consts, name):
    n_lat, n_ctx, cb = cfg.n_lat, cfg.n_ctx, cfg.ctx_blk

    def body(gl_ref, ul_ref, gc_ref, uc_ref, cw_ref, cb_ref, wa_ref, ba_ref, wx_ref, bx_ref, lam_ref,
             dzl_ref, dzc_ref, hl, hc, dgl_ref, dul_ref, dgc_ref, duc_ref, dwa_ref, dwx_ref, vec_ref,
             al, bl, ac, bc):
        e = pl.program_id(1)

        @pl.when(e == 0)
        def _():
            dwa_ref[...] = jnp.zeros_like(dwa_ref)
            dwx_ref[...] = jnp.zeros_like(dwx_ref)
            vec_ref[...] = jnp.zeros_like(vec_ref)

        row_l = lax.broadcasted_iota(jnp.int32, (n_lat, 1), 0)
        row_c = lax.broadcasted_iota(jnp.int32, (n_ctx, 1), 0)
        u_l, u_c = ul_ref[...], uc_ref[...]
        uc_l = _conv(u_l, cw_ref, cb_ref, row_l)
        uc_c = _conv(u_c, cw_ref, cb_ref, row_c)
        gel_l, t_l = _gelu(gl_ref[...])
        gel_c, t_c = _gelu(gc_ref[...])
        dz_l, dz_c = dzl_ref[...], dzc_ref[...]
        dgl_ref[...] = (dz_l * (hl[0] + hl[1]) * _gelu_grad(gl_ref[...], t_l)).astype(BF16)
        dgc_ref[...] = (dz_c * (hc[0] + hc[1]) * _gelu_grad(gc_ref[...], t_c)).astype(BF16)
        dy_l, dy_c = dz_l * gel_l, dz_c * gel_c
        duc_l = jnp.zeros((n_lat, 128), F32)
        duc_c = jnp.zeros((n_ctx, 128), F32)
        for d in range(2):
            r_l, gi_l, sp, dsp, a_l, sq_l = _lru_gates(uc_l, d, wa_ref, ba_ref, wx_ref, bx_ref, lam_ref)
            r_c, gi_c, _, _, a_c, sq_c = _lru_gates(uc_c, d, wa_ref, ba_ref, wx_ref, bx_ref, lam_ref)
            if d == 0:
                an_l = _shift_up(a_l, 1, row_l)
                an_c = jnp.where(row_c < n_ctx - 1, pltpu.roll(a_c, n_ctx - 1, 0), a_l[0:1, :])
            else:
                an_l = _shift_down(a_l, 1, row_l)
                an_c = jnp.where(row_c >= 1, pltpu.roll(a_c, 1, 0), a_l[n_lat - 1:n_lat, :])
            _chain_scan([(an_l, dy_l, al, bl, n_lat), (an_c, dy_c, ac, bc, n_ctx)], reverse=(d == 0))
            dsp_sum = jnp.zeros((1, 128), F32)
            for (dh, h, r, gi, a, sq, uc, seg) in ((bl[...], hl[d], r_l, gi_l, a_l, sq_l, uc_l, "l"),
                                                  (bc[...], hc[d], r_c, gi_c, a_c, sq_c, uc_c, "c")):
                b0 = sq * (gi * uc)
                t1 = dh * sq
                dla = dh * (h - b0) - (dh * gi * uc) * (a * a) / sq
                dzr = (dla * ((-LRU_C) * sp)) * r * (1.0 - r)
                dzi = (t1 * uc) * gi * (1.0 - gi)
                dsp_sum = dsp_sum + jnp.sum(dla * ((-LRU_C) * r), axis=0, keepdims=True)
                dwa_ref[d] += _dot_tn(uc, dzr)
                dwx_ref[d] += _dot_tn(uc, dzi)
                vec_ref[d:d + 1, :] += jnp.sum(dzr, axis=0, keepdims=True)
                vec_ref[2 + d:3 + d, :] += jnp.sum(dzi, axis=0, keepdims=True)
                duc = t1 * gi + _dot_nt(dzr, wa_ref[d]) + _dot_nt(dzi, wx_ref[d])
                if seg == "l":
                    duc_l = duc_l + duc
                else:
                    duc_c = duc_c + duc
            vec_ref[4 + d:5 + d, :] += dsp_sum * dsp
        for duc, u, row, du_ref in ((duc_l, u_l, row_l, dul_ref), (duc_c, u_c, row_c, duc_ref)):
            du_ref[...] = (_shift_up(duc, 1, row) * cw_ref[0:1, :] + duc * cw_ref[1:2, :]
                           + _shift_down(duc, 1, row) * cw_ref[2:3, :]
                           + _shift_down(duc, 2, row) * cw_ref[3:4, :]).astype(BF16)
            vec_ref[6:7, :] += jnp.sum(duc * _shift_down(u, 1, row), axis=0, keepdims=True)
            vec_ref[7:8, :] += jnp.sum(duc * u, axis=0, keepdims=True)
            vec_ref[8:9, :] += jnp.sum(duc * _shift_up(u, 1, row), axis=0, keepdims=True)
            vec_ref[9:10, :] += jnp.sum(duc * _shift_up(u, 2, row), axis=0, keepdims=True)
            vec_ref[10:11, :] += jnp.sum(duc, axis=0, keepdims=True)

    lat = pl.BlockSpec((n_lat, 128), lambda hb, e: (e, hb))
    ctx = pl.BlockSpec((n_ctx, 128), lambda hb, e: (e, hb))
    wspec = pl.BlockSpec((2, None, 128, 128), lambda hb, e: (0, hb, 0, 0))
    sd = jax.ShapeDtypeStruct
    return pl.pallas_call(
        body, grid=(8, 2), name=name,
        in_specs=_lru_specs(cfg) + [pl.BlockSpec((n_lat, 128), lambda hb, e: (e, hb)),
                                    pl.BlockSpec((n_ctx, 128), lambda hb, e: (cb + e, hb)),
                                    pl.BlockSpec((2, n_lat, 128), lambda hb, e: (0, e, hb)),
                                    pl.BlockSpec((2, n_ctx, 128), lambda hb, e: (0, e, hb))],
        out_specs=[lat, lat, ctx, ctx, wspec, wspec, pl.BlockSpec((None, 16, 128), lambda hb, e: (hb, 0, 0))],
        out_shape=[sd((cfg.t_lat, D), BF16), sd((cfg.t_lat, D), BF16), sd((cfg.t_ctx, D), BF16), sd((cfg.t_ctx, D), BF16),
                   sd((2, 8, 128, 128), F32), sd((2, 8, 128, 128), F32), sd((8, 16, 128), F32)],
        scratch_shapes=[pltpu.VMEM((n_lat, 128), F32)] * 2 + [pltpu.VMEM((n_ctx, 128), F32)] * 2,
        compiler_params=_params(("parallel", "arbitrary")))(p, p, p, p, *consts, dz, dz, h_lat, h_ctx)


def _position():
    x, y, c = lax.axis_index("x"), lax.axis_index("y"), lax.axis_index("c")
    return x, y, c, 4 * x + 2 * y + c


def _peer(x, y, c, k):
    px = 1 - x if k & 4 else x
    py = 1 - y if k & 2 else y
    pc = 1 - c if k & 1 else c
    return (px, py, pc), 4 * px + 2 * py + pc


def _all_gather(v, name, in_vmem):
    def body(v_ref, o_ref, send_sems, recv_sems, local_sem):
        x, y, c, me = _position()
        mine = pltpu.make_async_copy(v_ref, o_ref.at[me], local_sem)
        mine.start()
        sends = []
        for k in range(1, N_DEV):
            peer, _ = _peer(x, y, c, k)
            cp = pltpu.make_async_remote_copy(src_ref=v_ref, dst_ref=o_ref.at[me], send_sem=send_sems.at[k - 1],
                                              recv_sem=recv_sems.at[k - 1], device_id=peer, device_id_type=MESH)
            cp.start()
            sends.append(cp)
        for k in range(1, N_DEV):
            peer, peer_lin = _peer(x, y, c, k)
            pltpu.make_async_remote_copy(src_ref=v_ref, dst_ref=o_ref.at[peer_lin], send_sem=send_sems.at[k - 1],
                                         recv_sem=recv_sems.at[k - 1], device_id=peer, device_id_type=MESH).wait_recv()
        for cp in sends:
            cp.wait_send()
        mine.wait()

    space = pltpu.VMEM if in_vmem else pl.ANY
    return pl.pallas_call(
        body, name=name,
        in_specs=[pl.BlockSpec(memory_space=space)], out_specs=pl.BlockSpec(memory_space=space),
        out_shape=jax.ShapeDtypeStruct((N_DEV,) + v.shape, v.dtype),
        scratch_shapes=[pltpu.SemaphoreType.DMA((N_DEV - 1,)), pltpu.SemaphoreType.DMA((N_DEV - 1,)),
                        pltpu.SemaphoreType.DMA],
        compiler_params=pltpu.CompilerParams(vmem_limit_bytes=VMEM_LIMIT))(v)


_HBM = pl.BlockSpec(memory_space=pltpu.HBM)
_SEM = pl.BlockSpec(memory_space=pltpu.SEMAPHORE)
_EFFECT = pltpu.SideEffectType.DATAFLOW_SIDE_EFFECTING


ALL_PEERS = tuple(range(1, N_DEV))
SAME_CORE_AND_SIBLING = (1, 2, 4, 6)


def _push_start(src, land, block_of, name, relations=ALL_PEERS):
    def body(src_ref, land_ref, send_sem, recv_sem, src_thru, land_thru, token):
        x, y, c, me = _position()
        for k in relations:
            peer, peer_lin = _peer(x, y, c, k)
            mine, there = block_of(src_ref, land_ref, me, peer_lin)
            pltpu.make_async_remote_copy(src_ref=mine, dst_ref=there, send_sem=send_sem, recv_sem=recv_sem,
                                         device_id=peer, device_id_type=MESH).start()
        mine, here = block_of(src_ref, land_ref, me, me)
        pltpu.make_async_copy(mine, here, recv_sem).start()
        token[...] = jnp.zeros_like(token)

    return pl.pallas_call(
        body, name=name,
        out_shape=(pltpu.SemaphoreType.DMA(()), pltpu.SemaphoreType.DMA(()), pltpu.HBM(src.shape, src.dtype),
                   pltpu.HBM(land.shape, land.dtype), jax.ShapeDtypeStruct((8, 128), F32)),
        in_specs=(_HBM, _HBM), out_specs=(_SEM, _SEM, _HBM, _HBM, pl.BlockSpec(memory_space=pltpu.VMEM)),
        input_output_aliases={0: 2, 1: 3},
        compiler_params=pltpu.CompilerParams(has_side_effects=_EFFECT),
    )(pltpu.with_memory_space_constraint(src, pltpu.HBM), pltpu.with_memory_space_constraint(land, pltpu.HBM))


def _push_wait(handle, blocks_of, after, name, n_peers=N_DEV - 1):
    send_sem, recv_sem, src_thru, land_thru, _ = handle

    def body(src_ref, land_ref, send_sem, recv_sem, after_ref, src_dead, got_ref):
        x, y, c, _ = _position()
        sent, landed = blocks_of(land_ref, n_peers), blocks_of(land_ref, n_peers + 1)
        pltpu.make_async_remote_copy(src_ref=sent, dst_ref=sent, send_sem=send_sem, recv_sem=recv_sem,
                                     device_id=(x, y, 1 - c), device_id_type=MESH).wait_send()
        pltpu.make_async_remote_copy(src_ref=landed, dst_ref=landed, send_sem=send_sem, recv_sem=recv_sem,
                                     device_id=(x, y, 1 - c), device_id_type=MESH).wait_recv()

    return pl.pallas_call(
        body, name=name,
        out_shape=(pltpu.HBM(src_thru.shape, src_thru.dtype), pltpu.HBM(land_thru.shape, land_thru.dtype)),
        in_specs=(_HBM, _HBM, _SEM, _SEM, pl.BlockSpec(memory_space=pl.ANY)), out_specs=(_HBM, _HBM),
        input_output_aliases={0: 0, 1: 1},
        compiler_params=pltpu.CompilerParams(has_side_effects=_EFFECT),
    )(src_thru, land_thru, send_sem, recv_sem, after)[1]


def _gather_start(src, name, relations=ALL_PEERS):
    g, r, C = src.shape
    land = lax.empty((g, N_DEV * r, C), src.dtype)
    return _push_start(src, land, lambda s, z, i, p: (s, z.at[:, pl.ds(i * r, r), :]), name, relations)


def _gather_wait(handle, after, name, n_peers=N_DEV - 1):
    r = handle[2].shape[1]
    return _push_wait(handle, lambda z, n: z.at[:, pl.ds(0, n * r), :], after, name, n_peers)


def _relay_start(land, r, name):
    def body(land_ref, send_sem, recv_sem, land_thru, token):
        x, y, c, _ = _position()
        for k in (2, 4, 6):
            _, origin = _peer(x, y, c, k)
            rows = land_ref.at[:, pl.ds(origin * r, r), :]
            pltpu.make_async_remote_copy(src_ref=rows, dst_ref=rows, send_sem=send_sem, recv_sem=recv_sem,
                                         device_id=(x, y, 1 - c), device_id_type=MESH).start()
        token[...] = jnp.zeros_like(token)

    return pl.pallas_call(
        body, name=name,
        out_shape=(pltpu.SemaphoreType.DMA(()), pltpu.SemaphoreType.DMA(()), pltpu.HBM(land.shape, land.dtype),
                   jax.ShapeDtypeStruct((8, 128), F32)),
        in_specs=(_HBM,), out_specs=(_SEM, _SEM, _HBM, pl.BlockSpec(memory_space=pltpu.VMEM)),
        input_output_aliases={0: 2},
        compiler_params=pltpu.CompilerParams(has_side_effects=_EFFECT),
    )(pltpu.with_memory_space_constraint(land, pltpu.HBM))


def _relay_wait(handle, r, after, name):
    send_sem, recv_sem, land_thru, _ = handle

    def body(land_ref, send_sem, recv_sem, after_ref, got_ref):
        x, y, c, _ = _position()
        three = land_ref.at[:, pl.ds(0, 3 * r), :]
        cp = pltpu.make_async_remote_copy(src_ref=three, dst_ref=three, send_sem=send_sem, recv_sem=recv_sem,
                                          device_id=(x, y, 1 - c), device_id_type=MESH)
        cp.wait_send()
        cp.wait_recv()

    return pl.pallas_call(
        body, name=name, out_shape=(pltpu.HBM(land_thru.shape, land_thru.dtype),),
        in_specs=(_HBM, _SEM, _SEM, pl.BlockSpec(memory_space=pl.ANY)), out_specs=(_HBM,),
        input_output_aliases={0: 0},
        compiler_params=pltpu.CompilerParams(has_side_effects=_EFFECT),
    )(land_thru, send_sem, recv_sem, after)[0]


def _exchange_start(grad, name):
    g, rows, C = grad.shape
    r = rows // N_DEV
    land = lax.empty((N_DEV, g, r, C), grad.dtype)
    return _push_start(grad, land, lambda s, z, i, p: (s.at[:, pl.ds(p * r, r), :], z.at[i]), name)


def _exchange_wait(handle, after, name):
    return _push_wait(handle, lambda z, n: z.at[pl.ds(0, n)], after, name)


def _sum_blocks(v, name):
    k, rows, cols = v.shape
    tr = rows
    for cand in (rows, 512, 352, 256, 176, 128, 64, 32, 16):
        if rows % cand == 0 and k * cand * cols * v.dtype.itemsize <= 6 * 1024 * 1024:
            tr = cand
            break

    def body(v_ref, o_ref):
        acc = v_ref[0].astype(F32)
        for s in range(1, k):
            acc = acc + v_ref[s].astype(F32)
        o_ref[...] = acc

    return pl.pallas_call(
        body, grid=(rows // tr,), name=name,
        in_specs=[pl.BlockSpec((k, tr, cols), lambda i: (0, i, 0))],
        out_specs=pl.BlockSpec((tr, cols), lambda i: (i, 0)),
        out_shape=jax.ShapeDtypeStruct((rows, cols), F32),
        compiler_params=_params(("parallel",)))(v)


def _adam_math(w, g, m, v):
    m2 = B1 * m + (1.0 - B1) * g
    v2 = B2 * v + (1.0 - B2) * (g * g)
    m_hat = m2 / (1.0 - B1 ** STEP)
    v_hat = v2 / (1.0 - B2 ** STEP)
    return -LR * (m_hat / (jnp.sqrt(v_hat) + EPS) + WD * w), m2, v2


def _adamw(w, g, m, v, name, dep=None):
    shp = w.shape
    rows, cols = (shp[-2], shp[-1]) if len(shp) >= 2 else (1, shp[-1])
    lead = math.prod(shp[:-2]) if len(shp) > 2 else 1
    fits = [t for t in range(8, rows + 1, 8) if rows % t == 0 and t * cols * 4 <= 2 * 1024 * 1024]
    tr = max(fits) if fits else rows

    def body(w_ref, g_ref, m_ref, v_ref, *rest):
        d_ref, m2_ref, v2_ref = rest[-3:]
        d_ref[...], m2_ref[...], v2_ref[...] = _adam_math(w_ref[...], g_ref[...], m_ref[...], v_ref[...])

    blk = pl.BlockSpec((None, tr, cols), lambda b, i: (b, i, 0))
    extra = [] if dep is None else [dep]
    outs = pl.pallas_call(
        body, grid=(lead, rows // tr), name=name,
        in_specs=[blk] * 4 + [pl.BlockSpec(memory_space=pl.ANY)] * len(extra), out_specs=[blk] * 3,
        out_shape=[jax.ShapeDtypeStruct((lead, rows, cols), F32)] * 3,
        compiler_params=_params(("parallel", "parallel")))(*[a.reshape(lead, rows, cols) for a in (w, g, m, v)], *extra)
    return [o.reshape(shp) for o in outs]


def _as2d(a):
    n = a.size
    if n % 1024 == 0:
        return a.reshape(n // 1024, 1024)
    if n % 128 == 0:
        return a.reshape(n // 128, 128)
    return a.reshape(1, n)


def _blocks_to_cols(a):
    b = jnp.moveaxis(a, 0, -2)
    return b.reshape(b.shape[:-2] + (b.shape[-2] * b.shape[-1],))


def _pack_rows(parts):
    padded, offs, r = [], [], 0
    for p in parts:
        pad = (-p.shape[0]) % 8
        padded.append(jnp.pad(p, ((0, pad), (0, 0))) if pad else p)
        offs.append(r)
        r += p.shape[0] + pad
    return jnp.concatenate(padded, axis=0), offs


def _silu(x):
    return x * jax.nn.sigmoid(x)


def kernel(x, c, ctx, c_ctx, w_mod, b_mod, ln_g, ln_b, ffn_w_gate, ffn_w_up, ffn_w_down, mix_ab_w_in, attn_sink, pool_w, pool_scale, mix_ab_w_out, lru_w_in, lru_conv_w, lru_conv_b, lru_wa, lru_ba, lru_wx, lru_bx, lru_lambda, lru_w_out, loss_target, m_c_ctx, m_w_mod, m_b_mod, m_ln_g, m_ln_b, m_ffn_w_gate, m_ffn_w_up, m_ffn_w_down, m_mix_ab_w_in, m_attn_sink, m_pool_w, m_pool_scale, m_mix_ab_w_out, m_lru_w_in, m_lru_conv_w, m_lru_conv_b, m_lru_wa, m_lru_ba, m_lru_wx, m_lru_bx, m_lru_lambda, m_lru_w_out, v_c_ctx, v_w_mod, v_b_mod, v_ln_g, v_ln_b, v_ffn_w_gate, v_ffn_w_up, v_ffn_w_down, v_mix_ab_w_in, v_attn_sink, v_pool_w, v_pool_scale, v_mix_ab_w_out, v_lru_w_in, v_lru_conv_w, v_lru_conv_b, v_lru_wa, v_lru_ba, v_lru_wx, v_lru_bx, v_lru_lambda, v_lru_w_out):
    weights = dict(c_ctx=c_ctx, w_mod=w_mod, b_mod=b_mod, ln_g=ln_g, ln_b=ln_b, ffn_w_gate=ffn_w_gate,
                   ffn_w_up=ffn_w_up, ffn_w_down=ffn_w_down, mix_ab_w_in=mix_ab_w_in, attn_sink=attn_sink,
                   pool_w=pool_w, pool_scale=pool_scale, mix_ab_w_out=mix_ab_w_out, lru_w_in=lru_w_in,
                   lru_conv_w=lru_conv_w, lru_conv_b=lru_conv_b, lru_wa=lru_wa, lru_ba=lru_ba, lru_wx=lru_wx,
                   lru_bx=lru_bx, lru_lambda=lru_lambda, lru_w_out=lru_w_out)
    mom_m = dict(c_ctx=m_c_ctx, w_mod=m_w_mod, b_mod=m_b_mod, ln_g=m_ln_g, ln_b=m_ln_b, ffn_w_gate=m_ffn_w_gate,
                 ffn_w_up=m_ffn_w_up, ffn_w_down=m_ffn_w_down, mix_ab_w_in=m_mix_ab_w_in, attn_sink=m_attn_sink,
                 pool_w=m_pool_w, pool_scale=m_pool_scale, mix_ab_w_out=m_mix_ab_w_out, lru_w_in=m_lru_w_in,
                 lru_conv_w=m_lru_conv_w, lru_conv_b=m_lru_conv_b, lru_wa=m_lru_wa, lru_ba=m_lru_ba, lru_wx=m_lru_wx,
                 lru_bx=m_lru_bx, lru_lambda=m_lru_lambda, lru_w_out=m_lru_w_out)
    mom_v = dict(c_ctx=v_c_ctx, w_mod=v_w_mod, b_mod=v_b_mod, ln_g=v_ln_g, ln_b=v_ln_b, ffn_w_gate=v_ffn_w_gate,
                 ffn_w_up=v_ffn_w_up, ffn_w_down=v_ffn_w_down, mix_ab_w_in=v_mix_ab_w_in, attn_sink=v_attn_sink,
                 pool_w=v_pool_w, pool_scale=v_pool_scale, mix_ab_w_out=v_mix_ab_w_out, lru_w_in=v_lru_w_in,
                 lru_conv_w=v_lru_conv_w, lru_conv_b=v_lru_conv_b, lru_wa=v_lru_wa, lru_ba=v_lru_ba, lru_wx=v_lru_wx,
                 lru_bx=v_lru_bx, lru_lambda=v_lru_lambda, lru_w_out=v_lru_w_out)
    names = list(weights)

    n_lat, n_ctx = x.shape[1], ctx.shape[1]
    cfg = _Cfg(n_lat, n_ctx)
    _, _, _, me = _position()
    mcols = w_mod.shape[2]

    def t_bf16(w):
        return jnp.swapaxes(w, -1, -2).astype(BF16)

    def ffn_src(l, i):
        return jnp.stack([t_bf16(ffn_w_gate[l, i]), t_bf16(ffn_w_up[l, i]), ffn_w_down[l, i].astype(BF16)])

    pending = {}

    def start_gathers(items, tok):
        for key, make_src in items:
            pending[key] = _gather_start(make_src() + tok.astype(BF16), "gather_start_" + key)
            tok = pending[key][4][0, 0]
        return tok

    def weights_now(key, after):
        return _gather_wait(pending[key], after, "gather_wait_" + key)

    first = _gather_start(ffn_src(0, 0), "gather_start_ffn00", SAME_CORE_AND_SIBLING)
    tok = first[4][0, 0]

    small_names = ["ln_g", "ln_b", "lru_conv_w", "lru_conv_b", "lru_ba", "lru_bx", "lru_lambda"]
    small, small_off = _pack_rows([(c + tok).reshape(-1, 128)] + [weights[n].reshape(-1, 128) for n in small_names])
    small_all = _all_gather(small, "gather_small", True)

    def small_full(idx, shp):
        rows = math.prod(shp) // 128
        return _blocks_to_cols(small_all[:, small_off[idx]:small_off[idx] + rows, :].reshape((N_DEV,) + shp))

    c_all = small_all[:, :2 * D // 128, :].reshape(2 * N_DEV, D)
    ln_g_f, ln_b_f = small_full(1, ln_g.shape), small_full(2, ln_b.shape)
    lru_consts = (small_full(3, lru_conv_w.shape)[0], small_full(4, lru_conv_b.shape), lru_wa[0],
                  small_full(5, lru_ba.shape)[0], lru_wx[0], small_full(6, lru_bx.shape)[0],
                  small_full(7, lru_lambda.shape)[0])

    s_rows = jnp.zeros((32, D), F32).at[:16].set(_silu(c_all)).at[16].set(_silu(c_ctx)).astype(BF16)
    mod_mine = jnp.stack([_matmul(s_rows, w_mod[l], "nn", F32, "mod_fwd", bn_cap=1280) for l in range(2)])
    mod_all = _all_gather(mod_mine.reshape(64, mcols), "gather_mod", True).reshape(N_DEV, 2, 32, mcols)
    r_ffn = ffn_w_down.shape[2]
    relay = _relay_start(_gather_wait(first, mod_all, "gather_wait_ffn00", n_peers=len(SAME_CORE_AND_SIBLING)),
                         r_ffn, "gather_relay_start_ffn00")
    tok = start_gathers([("ab_in", lambda: t_bf16(mix_ab_w_in)), ("ab_out", lambda: mix_ab_w_out.astype(BF16)),
                         ("ffn01", lambda: ffn_src(0, 1)), ("ffn10", lambda: ffn_src(1, 0)),
                         ("lru_in", lambda: t_bf16(lru_w_in)), ("lru_out", lambda: lru_w_out.astype(BF16)),
                         ("ffn11", lambda: ffn_src(1, 1))], relay[3][0, 0])
    mod_full = _blocks_to_cols(mod_all) + (b_mod[:, None, :] + tok)
    ex0 = 2 * me
    mods = []
    for l in range(2):
        rows = jnp.stack([lax.dynamic_index_in_dim(mod_full[l], ex0, 0, False),
                          lax.dynamic_index_in_dim(mod_full[l], ex0 + 1, 0, False), mod_full[l, 16]])
        mods.append(rows.reshape(3, N_MOD, D))

    h0 = jnp.concatenate([x.reshape(cfg.t_lat, D), ctx.reshape(cfg.t_ctx, D)], axis=0)
    cos, sin = _rope_tables(n_lat)
    sink_rows = jnp.broadcast_to(attn_sink[0][:, None], (8, 128)).astype(F32)

    saved = []
    wf = [[None, None], [None, None]]
    h = h0
    xin = _modulate(cfg, h0, mods[0], 0, 1, "modulate_in")
    for l in range(2):
        st = {"h_in": h, "xin1": xin}
        wf[l][0] = (_relay_wait(relay, r_ffn, xin, "gather_relay_wait_ffn00") if l == 0
                    else weights_now("ffn10", xin))
        g1, u1, y1 = _ffn_fwd(xin, wf[l][0], "ffn_fwd")
        h1, xhat1, rstd1, xin2 = _ln_fwd(cfg, h, y1, mods[l], 2, 0.5, ln_g_f[l, 0][None], ln_b_f[l, 0][None],
                                          mods[l], (3, 4), "ln_fwd_a")
        st.update(g1=g1, u1=u1, y1=y1, h1=h1, xhat1=xhat1, rstd1=rstd1, xin2=xin2)
        if l == 0:
            w_ab_in_t = weights_now("ab_in", xin2)[0]
            p = _matmul(xin2, w_ab_in_t, "nt", F32, "mix_ab_in")
            att_l, att_c = _attn_fwd(cfg, p, cos, sin, sink_rows, "attn_fwd")
            pool_l = _pool_fwd(p, pool_w[0], pool_scale, n_lat, 0, 2, "pool_fwd_lat")
            pool_c = _pool_fwd(p, pool_w[0], pool_scale, n_ctx, cfg.ctx_blk, 2, "pool_fwd_ctx")
            cat = jnp.concatenate([jnp.concatenate([att_l, pool_l], axis=1),
                                   jnp.concatenate([att_c, pool_c], axis=1)], axis=0)
            w_ab_out = weights_now("ab_out", cat)[0]
            y2 = _matmul(cat, w_ab_out, "nn", BF16, "mix_ab_out")
        else:
            w_lru_in_t = weights_now("lru_in", xin2)[0]
            p = _matmul(xin2, w_lru_in_t, "nt", F32, "lru_in")
            z_l, z_c, st["h_lat"], st["h_ctx"] = _lru_fwd(cfg, p, lru_consts, "lru_fwd")
            cat = jnp.concatenate([z_l, z_c], axis=0)
            w_lru_out = weights_now("lru_out", cat)[0]
            y2 = _matmul(cat, w_lru_out, "nn", BF16, "lru_out")
        h2, xhat2, rstd2, xin3 = _ln_fwd(cfg, h1, y2, mods[l], 5, 1.0, ln_g_f[l, 1][None], ln_b_f[l, 1][None],
                                          mods[l], (6, 7), "ln_fwd_b")
        wf[l][1] = weights_now("ffn%d1" % l, xin3)
        g3, u3, y3 = _ffn_fwd(xin3, wf[l][1], "ffn_fwd")
        if l == 0:
            h3, xhat3, rstd3, xin = _ln_fwd(cfg, h2, y3, mods[l], 8, 0.5, ln_g_f[l, 2][None], ln_b_f[l, 2][None],
                                            mods[1], (0, 1), "ln_fwd_a")
        else:
            h3, xhat3, rstd3 = _ln_fwd(cfg, h2, y3, mods[l], 8, 0.5, ln_g_f[l, 2][None], ln_b_f[l, 2][None],
                                       None, None, "ln_fwd_last")
        st.update(p=p, cat=cat, y2=y2, h2=h2, xhat2=xhat2, rstd2=rstd2, xin3=xin3, g3=g3, u3=u3, y3=y3,
                  xhat3=xhat3, rstd3=rstd3)
        saved.append(st)
        h = h3

    dy, loss_tile = _loss(cfg, h, loss_target.reshape(cfg.t_lat, D), "loss")
    loss = lax.psum(loss_tile[0, 0], ("x", "y", "c"))

    grads = {}
    dmod = [None, None]
    recv_ffn = [[None, None], [None, None]]
    dln_g = [[None] * 3, [None] * 3]
    dln_b = [[None] * 3, [None] * 3]

    def ffn_weight_grads(tag, xin_b, dg, du, a_act, dys):
        parts = [_matmul(dg, xin_b, "tn", BF16, "ffn_dw", bm_cap=1408, bk_cap=2304)[None],
                 _matmul(du, xin_b, "tn", BF16, "ffn_dw", bm_cap=1408, bk_cap=2304)[None],
                 _matmul(a_act, dys, "tn", BF16, "ffn_dw", bm_cap=1408, bk_cap=2304)[None]]
        return [_exchange_start(part, "exchange_start_ffn%s_%d" % (tag, k)) for k, part in enumerate(parts)]

    def pin(handles):
        total = handles[0][4][0, 0]
        for hd in handles[1:]:
            total = total + hd[4][0, 0]
        return total

    up = (dy,)
    dmod_next = None
    last_sent = None
    for l in (1, 0):
        st = saved[l]
        dm = [None] * N_MOD

        def put_stats(stats, gate_idx, nxt):
            dm[gate_idx] = stats[:, 2, :]
            if nxt is not None:
                nxt[0][nxt[1]] = stats[:, 4, :]
                nxt[0][nxt[1] + 1] = stats[:, 3, :]

        lng3 = ln_g_f[l, 2][None] if last_sent is None else ln_g_f[l, 2][None] + pin(last_sent)
        if len(up) > 1:
            up = (up[0], up[1], ln_b_f[l, 2][None], up[3], up[4])
        dres, dys, stats = _ln_bwd(cfg, up, st["xhat3"], st["rstd3"], st["y3"], mods[l], 8, 0.5,
                                   lng3, "ln_bwd_fused" if len(up) > 1 else "ln_bwd_last")
        put_stats(stats, 8, None if len(up) == 1 else (dmod_next, 0))
        dln_g[l][2], dln_b[l][2] = stats[:, 0, :].sum(0), stats[:, 1, :].sum(0)
        dg, du, a_act, dxin = _ffn_bwd(dys, st["g3"], st["u3"], wf[l][1], "ffn_bwd")
        recv_ffn[l][1] = ffn_weight_grads("%d1" % l, st["xin3"], dg, du, a_act, dys)
        dres, dys, stats = _ln_bwd(cfg, (dres, dxin, ln_b_f[l, 1][None], mods[l], 7), st["xhat2"], st["rstd2"], st["y2"],
                                   mods[l], 5, 1.0, ln_g_f[l, 1][None] + pin(recv_ffn[l][1]), "ln_bwd_fused")
        put_stats(stats, 5, (dm, 6))
        dln_g[l][1], dln_b[l][1] = stats[:, 0, :].sum(0), stats[:, 1, :].sum(0)
        if l == 0:
            dw_out = _matmul(st["cat"], dys, "tn", BF16, "mix_ab_dw_out")
            dcat = _matmul(dys, w_ab_out, "nt", F32, "mix_ab_dcat")
            dq, dk, dv, dqc, dkc, dvc, dsink = _attn_bwd(cfg, st["p"], dcat, cos, sin, sink_rows, "attn_bwd")
            du_l, dpw_l, dps_l = _pool_bwd(st["p"], pool_w[0], pool_scale, dcat, n_lat, 0, 2, "pool_bwd_lat")
            du_c, dpw_c, dps_c = _pool_bwd(st["p"], pool_w[0], pool_scale, dcat, n_ctx, cfg.ctx_blk, 2, "pool_bwd_ctx")
            dp = jnp.concatenate([jnp.concatenate([dq, dk, dv, du_l], axis=1),
                                  jnp.concatenate([dqc, dkc, dvc, du_c], axis=1)], axis=0)
            dw_in_t = _matmul(dp, st["xin2"], "tn", BF16, "mix_ab_dw_in", bm_cap=1280)
            dxin = _matmul(dp, w_ab_in_t, "nn", BF16, "mix_ab_dx")
            recv_mix = [_exchange_start(part, "exchange_start_mix_ab_%d" % k)
                        for k, part in enumerate((dw_in_t[None], dw_out[None], _as2d(dpw_l + dpw_c)[None]))]
            grads["attn_sink"] = (dsink[0, :, 0] + dsink[1, :, 0])[None, :]
            grads["pool_scale"] = dps_l + dps_c
        else:
            dw_out = _matmul(st["cat"], dys, "tn", BF16, "lru_dw_out")
            dz = _matmul(dys, w_lru_out, "nt", F32, "lru_dz")
            dgl, dul, dgc, duc, dwa, dwx, vec = _lru_bwd(cfg, st["p"], dz, st["h_lat"], st["h_ctx"], lru_consts, "lru_bwd")
            dp = jnp.concatenate([jnp.concatenate([dgl, dul], axis=1), jnp.concatenate([dgc, duc], axis=1)], axis=0)
            dw_in_t = _matmul(dp, st["xin2"], "tn", BF16, "lru_dw_in", bm_cap=1024)
            dxin = _matmul(dp, w_lru_in_t, "nn", BF16, "lru_dx")
            recv_mix = [_exchange_start(part, "exchange_start_lru_%d" % k)
                        for k, part in enumerate((dw_in_t[None], dw_out[None], _as2d(dwa)[None], _as2d(dwx)[None]))]
            vec_t = jnp.moveaxis(vec, 0, 1).reshape(16, D)
            grads["lru_ba"], grads["lru_bx"] = vec_t[0:2], vec_t[2:4]
            grads["lru_lambda"], grads["lru_conv_w"], grads["lru_conv_b"] = vec_t[4:6], vec_t[6:10], vec_t[10:11]
        if l == 0:
            recv_ab = recv_mix
        else:
            recv_lru = recv_mix
        dres, dys, stats = _ln_bwd(cfg, (dres, dxin, ln_b_f[l, 0][None], mods[l], 4), st["xhat1"], st["rstd1"], st["y1"],
                                   mods[l], 2, 0.5, ln_g_f[l, 0][None] + pin(recv_mix), "ln_bwd_fused")
        put_stats(stats, 2, (dm, 3))
        dln_g[l][0], dln_b[l][0] = stats[:, 0, :].sum(0), stats[:, 1, :].sum(0)
        dg, du, a_act, dxin = _ffn_bwd(dys, st["g1"], st["u1"], wf[l][0], "ffn_bwd")
        recv_ffn[l][0] = ffn_weight_grads("%d0" % l, st["xin1"], dg, du, a_act, dys)
        last_sent = recv_ffn[l][0]
        dmod[l] = dm
        dmod_next = dm
        up = (dres, dxin, None, mods[l], 1)
    dh0, stats = _modulate_bwd(cfg, up[0], up[1], h0, mods[0] + pin(last_sent), 1, "modulate_bwd")
    dmod[0][0], dmod[0][1] = stats[:, 4, :], stats[:, 3, :]
    grad_x = dh0.reshape(x.shape)

    dmod_mine = jnp.stack([jnp.stack(dmod[l], axis=1).reshape(3, N_MOD * D) for l in range(2)])
    n_dm = 6 * N_MOD * D // 128
    dmod_sent = _gather_start(dmod_mine.reshape(1, n_dm, 128), "gather_start_dmod")

    def arrived(handle, name):
        return _exchange_wait(handle, dmod_sent[4], name)

    recv_ffn = [[[arrived(hd, "exchange_wait_ffn%d%d_%d" % (l, i, k)) for k, hd in enumerate(recv_ffn[l][i])]
                 for i in range(2)] for l in range(2)]
    recv_ab = [arrived(hd, "exchange_wait_mix_ab_%d" % k) for k, hd in enumerate(recv_ab)]
    recv_lru = [arrived(hd, "exchange_wait_lru_%d" % k) for k, hd in enumerate(recv_lru)]

    def shard_sum(recv, name):
        return _sum_blocks(recv.reshape(N_DEV, recv.shape[2], recv.shape[3]), name)

    gate_g = [[None, None], [None, None]]
    up_g = [[None, None], [None, None]]
    down_g = [[None, None], [None, None]]
    for l in range(2):
        for i in range(2):
            gt, ut, dn = [shard_sum(r, "sum_ffn") for r in recv_ffn[l][i]]
            gate_g[l][i], up_g[l][i], down_g[l][i] = gt.T, ut.T, dn
    grads["ffn_w_gate"] = jnp.stack([jnp.stack(gate_g[l]) for l in range(2)])
    grads["ffn_w_up"] = jnp.stack([jnp.stack(up_g[l]) for l in range(2)])
    grads["ffn_w_down"] = jnp.stack([jnp.stack(down_g[l]) for l in range(2)])
    grads["mix_ab_w_in"] = shard_sum(recv_ab[0], "sum_mix_in").T[None]
    grads["mix_ab_w_out"] = shard_sum(recv_ab[1], "sum_mix_out")[None]
    grads["lru_w_in"] = shard_sum(recv_lru[0], "sum_lru_in").T[None]
    grads["lru_w_out"] = shard_sum(recv_lru[1], "sum_lru_out")[None]
    rep_parts = [shard_sum(recv_lru[2], "sum_rep"), shard_sum(recv_lru[3], "sum_rep"), shard_sum(recv_ab[2], "sum_rep")]
    rep_names = ["lru_wa", "lru_wx", "pool_w"]

    dmod_all = _gather_wait(dmod_sent, rep_parts[2], "gather_wait_dmod").reshape(N_DEV, n_dm, 128)
    dmod_sum = _sum_blocks(dmod_all, "sum_dmod").reshape(2, 3, N_MOD * D)
    dmod_all = dmod_all.reshape(N_DEV, 2, 3, N_MOD * D)
    grads["b_mod"] = dmod_sum[:, 0] + dmod_sum[:, 1] + dmod_sum[:, 2]
    dmod_ex = jnp.moveaxis(dmod_all[:, :, 0:2, :], 1, 0).reshape(2, 2 * N_DEV, N_MOD * D)
    dm_rows = jnp.zeros((2, 32, N_MOD * D), F32).at[:, :16].set(dmod_ex).at[:, 16].set(dmod_sum[:, 2])
    dm_cols = lax.dynamic_slice_in_dim(dm_rows, me * mcols, mcols, axis=2).astype(BF16)
    grads["w_mod"] = jnp.stack([_matmul(s_rows, dm_cols[l], "tn", F32, "mod_dw", bn_cap=1280) for l in range(2)])
    ds_part = None
    for l in range(2):
        part = _matmul(dm_cols[l, 16:32], w_mod[l], "nt", F32, "mod_ds", bk_cap=1280)[0]
        ds_part = part if ds_part is None else ds_part + part

    dln_g_f = jnp.stack([jnp.stack(dln_g[l]) for l in range(2)])
    dln_b_f = jnp.stack([jnp.stack(dln_b[l]) for l in range(2)])
    sink_pad = jnp.zeros((1, 128), F32).at[0, :8].set(grads["attn_sink"][0])
    part_list = [p_.reshape(-1, 128) for p_ in rep_parts] + [
        dln_g_f.reshape(-1, 128), dln_b_f.reshape(-1, 128), grads["lru_conv_w"].reshape(-1, 128),
        grads["lru_conv_b"].reshape(-1, 128), grads["lru_ba"].reshape(-1, 128), grads["lru_bx"].reshape(-1, 128),
        grads["lru_lambda"].reshape(-1, 128), ds_part.reshape(-1, 128), sink_pad, grads["pool_scale"].reshape(-1, 128)]
    parts, part_off = _pack_rows(part_list)
    parts_sent = _gather_start(parts[None], "gather_start_partials")

    delta, new_m, new_v = {}, {}, {}
    for n in ("w_mod", "b_mod", "ffn_w_gate", "ffn_w_up", "ffn_w_down", "mix_ab_w_in", "mix_ab_w_out",
              "lru_w_in", "lru_w_out"):
        grads[n] = grads[n].reshape(weights[n].shape)
        delta[n], new_m[n], new_v[n] = _adamw(weights[n], grads[n], mom_m[n], mom_v[n], "adamw", dep=parts_sent[4])
    parts_all = _gather_wait(parts_sent, delta["lru_w_out"], "gather_wait_partials").reshape(N_DEV, parts.shape[0], 128)
    parts_sum = _sum_blocks(parts_all, "sum_partials")

    for i, n in enumerate(rep_names):
        rows = part_list[i].shape[0]
        grads[n] = parts_all[:, part_off[i]:part_off[i] + rows, :].reshape(weights[n].shape)

    def take(idx):
        return parts_sum[part_off[idx]:part_off[idx] + part_list[idx].shape[0]]

    def my_cols(full, shp):
        w = shp[-1]
        return lax.dynamic_slice_in_dim(full, me * w, w, axis=full.ndim - 1)

    grads["ln_g"] = my_cols(take(3).reshape(2, 3, D), ln_g.shape)
    grads["ln_b"] = my_cols(take(4).reshape(2, 3, D), ln_b.shape)
    grads["lru_conv_w"] = my_cols(take(5).reshape(1, 4, D), lru_conv_w.shape)
    grads["lru_conv_b"] = my_cols(take(6).reshape(1, D), lru_conv_b.shape)
    grads["lru_ba"] = my_cols(take(7).reshape(1, 2, D), lru_ba.shape)
    grads["lru_bx"] = my_cols(take(8).reshape(1, 2, D), lru_bx.shape)
    grads["lru_lambda"] = my_cols(take(9).reshape(1, 2, D), lru_lambda.shape)
    sg = jax.nn.sigmoid(c_ctx)
    grads["c_ctx"] = take(10).reshape(D) * (sg * (1.0 + c_ctx * (1.0 - sg)))
    grads["attn_sink"] = take(11)[:, :8]
    grads["pool_scale"] = take(12).reshape(pool_scale.shape)

    for n in names:
        if n in delta:
            continue
        grads[n] = grads[n].reshape(weights[n].shape)
        delta[n], new_m[n], new_v[n] = _adamw(weights[n], grads[n], mom_m[n], mom_v[n], "adamw")

    return (loss, grad_x, *[grads[n] for n in names], *[delta[n] for n in names],
            *[new_m[n] for n in names], *[new_v[n] for n in names])
```

```python
import functools
import math

import jax
import jax.numpy as jnp
from jax import lax
from jax.experimental import pallas as pl
from jax.experimental.pallas import tpu as pltpu

F32 = jnp.float32
BF16 = jnp.bfloat16
MESH = pl.DeviceIdType.MESH

D = 1024
N_MOD = 9
N_DEV = 8
HEAD_DIM = 64
ATT_HEADS = 8
KV_HEADS = 2
ATT_W = 512
BLK = 128
ATT_SCALE = HEAD_DIM ** -0.5
GRID_W = 64
ROPE_FREQS = HEAD_DIM // 4
ROPE_THETA = 10000.0
POOL_R = (1, 2, 4, 8)
LRU_C = 8.0
LN_EPS = 1e-5
NEG_INF = -1e30
ALPHA = 4.0 ** 0.25
LR, B1, B2, EPS, WD, STEP = 0.001, 0.9, 0.999, 1e-08, 0.01, 10
VMEM_LIMIT = 56 * 1024 * 1024
ROW_TILE = 512


def _params(sem=None):
    if sem is None:
        return pltpu.CompilerParams(vmem_limit_bytes=VMEM_LIMIT)
    return pltpu.CompilerParams(dimension_semantics=sem, vmem_limit_bytes=VMEM_LIMIT)


def _sigmoid(x):
    return 0.5 * jnp.tanh(0.5 * x) + 0.5


def _dot(a, b):
    return jnp.dot(a.astype(BF16), b.astype(BF16), preferred_element_type=F32)


def _dot_nt(a, b):
    return lax.dot_general(a.astype(BF16), b.astype(BF16), (((1,), (1,)), ((), ())), preferred_element_type=F32)


def _dot_tn(a, b):
    return lax.dot_general(a.astype(BF16), b.astype(BF16), (((0,), (0,)), ((), ())), preferred_element_type=F32)


def _pick(n, cap):
    best = None
    for m in range(128, min(n, cap) + 1, 128):
        if n % m == 0:
            best = m
    return n if best is None else best


def _chunks(width, step=256):
    out, c = [], 0
    while c < width:
        w = min(step, width - c)
        out.append((c, w))
        c += w
    return out


class _Cfg:
    def __init__(self, n_lat, n_ctx):
        self.n_lat, self.n_ctx = n_lat, n_ctx
        self.t_lat, self.t_ctx = 2 * n_lat, 2 * n_ctx
        self.T = self.t_lat + self.t_ctx
        self.tm = min(ROW_TILE, self.t_ctx)
        assert n_lat % self.tm == 0 and self.t_ctx % self.tm == 0 and n_lat >= 3 * BLK and n_ctx % BLK == 0
        self.nt = self.T // self.tm
        self.nlt = n_lat // self.tm
        self.ctx_blk = self.t_lat // n_ctx

    def seg(self, i):
        return jnp.minimum(i // self.nlt, 2)

    def first_of_seg(self, i):
        return jnp.where(i < 2 * self.nlt, i % self.nlt == 0, i == 2 * self.nlt)


def _modulate(cfg, h, mod, shift_idx, scale_idx, name):
    tm = cfg.tm

    def body(h_ref, mod_ref, o_ref):
        sh = mod_ref[shift_idx:shift_idx + 1, :]
        sc = mod_ref[scale_idx:scale_idx + 1, :]
        o_ref[...] = (h_ref[...] * (1.0 + sc) + sh).astype(BF16)

    return pl.pallas_call(
        body, grid=(cfg.nt,), name=name,
        in_specs=[pl.BlockSpec((tm, D), lambda i: (i, 0)),
                  pl.BlockSpec((None, N_MOD, D), lambda i: (cfg.seg(i), 0, 0))],
        out_specs=pl.BlockSpec((tm, D), lambda i: (i, 0)),
        out_shape=jax.ShapeDtypeStruct((cfg.T, D), BF16),
        compiler_params=_params(("parallel",)),
    )(h, mod)


def _ln_fwd(cfg, h, y, mod, gate_idx, coef, lng, lnb, mod_next, next_idx, name):
    tm = cfg.tm
    has_next = next_idx is not None

    def body(*refs):
        if has_next:
            h_ref, y_ref, mod_ref, g_ref, b_ref, modn_ref, hn_ref, xhat_ref, rstd_ref, xin_ref = refs
        else:
            h_ref, y_ref, mod_ref, g_ref, b_ref, hn_ref, xhat_ref, rstd_ref = refs
        gate = mod_ref[gate_idx:gate_idx + 1, :]
        z = ALPHA * h_ref[...] + (coef * gate) * y_ref[...].astype(F32)
        mu = jnp.mean(z, axis=-1, keepdims=True)
        zc = z - mu
        var = jnp.mean(zc * zc, axis=-1, keepdims=True)
        rstd = lax.rsqrt(var + LN_EPS)
        xhat = zc * rstd
        hn = xhat * g_ref[...] + b_ref[...]
        hn_ref[...] = hn
        xhat_ref[...] = xhat.astype(BF16)
        rstd_ref[...] = rstd
        if has_next:
            sh = modn_ref[next_idx[0]:next_idx[0] + 1, :]
            sc = modn_ref[next_idx[1]:next_idx[1] + 1, :]
            xin_ref[...] = (hn * (1.0 + sc) + sh).astype(BF16)

    row = pl.BlockSpec((tm, D), lambda i: (i, 0))
    modspec = pl.BlockSpec((None, N_MOD, D), lambda i: (cfg.seg(i), 0, 0))
    vec = pl.BlockSpec((1, D), lambda i: (0, 0))
    in_specs = [row, row, modspec, vec, vec]
    args = [h, y, mod, lng, lnb]
    out_specs = [row, row, pl.BlockSpec((tm, 1), lambda i: (i, 0))]
    out_shape = [jax.ShapeDtypeStruct((cfg.T, D), F32), jax.ShapeDtypeStruct((cfg.T, D), BF16),
                 jax.ShapeDtypeStruct((cfg.T, 1), F32)]
    if has_next:
        in_specs.append(modspec)
        args.append(mod_next)
        out_specs.append(row)
        out_shape.append(jax.ShapeDtypeStruct((cfg.T, D), BF16))
    return pl.pallas_call(body, grid=(cfg.nt,), name=name, in_specs=in_specs, out_specs=out_specs,
                          out_shape=out_shape, compiler_params=_params(("parallel",)))(*args)


def _ln_bwd(cfg, up, xhat, rstd, y, mod, gate_idx, coef, lng, name):
    tm = cfg.tm
    fused = len(up) > 1
    scale_next = up[4] if fused else None

    def body(*refs):
        if fused:
            dres_n, dxin_n, b_ref, modn_ref, xhat_ref, rstd_ref, y_ref, mod_ref, g_ref, dres_ref, dys_ref, st_ref = refs
        else:
            dhn_ref, xhat_ref, rstd_ref, y_ref, mod_ref, g_ref, dres_ref, dys_ref, st_ref = refs
        i = pl.program_id(0)

        @pl.when(cfg.first_of_seg(i))
        def _():
            st_ref[...] = jnp.zeros_like(st_ref)

        xhat = xhat_ref[...].astype(F32)
        if fused:
            dxin = dxin_n[...].astype(F32)
            sc = modn_ref[scale_next:scale_next + 1, :]
            dhn = dres_n[...] + dxin * (1.0 + sc)
            shift_sum = jnp.sum(dxin, axis=0, keepdims=True)
            st_ref[3:4, :] += g_ref[...] * jnp.sum(dxin * xhat, axis=0, keepdims=True) + b_ref[...] * shift_sum
            st_ref[4:5, :] += shift_sum
        else:
            dhn = dhn_ref[...]
        gdh = dhn * g_ref[...]
        m1 = jnp.mean(gdh, axis=-1, keepdims=True)
        m2 = jnp.mean(gdh * xhat, axis=-1, keepdims=True)
        dz = rstd_ref[...] * (gdh - m1 - xhat * m2)
        gate = mod_ref[gate_idx:gate_idx + 1, :]
        dres_ref[...] = ALPHA * dz
        dys_ref[...] = ((coef * gate) * dz).astype(BF16)
        st_ref[0:1, :] += jnp.sum(dhn * xhat, axis=0, keepdims=True)
        st_ref[1:2, :] += jnp.sum(dhn, axis=0, keepdims=True)
        st_ref[2:3, :] += jnp.sum((coef * dz) * y_ref[...].astype(F32), axis=0, keepdims=True)

    row = pl.BlockSpec((tm, D), lambda i: (i, 0))
    modspec = pl.BlockSpec((None, N_MOD, D), lambda i: (cfg.seg(i), 0, 0))
    vec = pl.BlockSpec((1, D), lambda i: (0, 0))
    col = pl.BlockSpec((tm, 1), lambda i: (i, 0))
    if fused:
        in_specs = [row, row, vec, modspec, row, col, row, modspec, vec]
        args = [up[0], up[1], up[2], up[3], xhat, rstd, y, mod, lng]
    else:
        in_specs = [row, row, col, row, modspec, vec]
        args = [up[0], xhat, rstd, y, mod, lng]
    return pl.pallas_call(
        body, grid=(cfg.nt,), name=name, in_specs=in_specs,
        out_specs=[row, row, pl.BlockSpec((None, 8, D), lambda i: (cfg.seg(i), 0, 0))],
        out_shape=[jax.ShapeDtypeStruct((cfg.T, D), F32), jax.ShapeDtypeStruct((cfg.T, D), BF16),
                   jax.ShapeDtypeStruct((3, 8, D), F32)],
        compiler_params=_params(("arbitrary",)))(*args)


def _modulate_bwd(cfg, dres, dxin, h, mod, scale_idx, name):
    tm = cfg.tm
    n_lt = 2 * cfg.nlt

    def body(dres_ref, dxin_ref, h_ref, mod_ref, dh_ref, st_ref):
        i = pl.program_id(0)

        @pl.when(cfg.first_of_seg(i))
        def _():
            st_ref[...] = jnp.zeros_like(st_ref)

        dxin = dxin_ref[...].astype(F32)
        sc = mod_ref[scale_idx:scale_idx + 1, :]

        @pl.when(i < n_lt)
        def _():
            dh_ref[...] = dres_ref[...] + dxin * (1.0 + sc)

        st_ref[3:4, :] += jnp.sum(dxin * h_ref[...], axis=0, keepdims=True)
        st_ref[4:5, :] += jnp.sum(dxin, axis=0, keepdims=True)

    row = pl.BlockSpec((tm, D), lambda i: (i, 0))
    return pl.pallas_call(
        body, grid=(cfg.nt,), name=name,
        in_specs=[row, row, row, pl.BlockSpec((None, N_MOD, D), lambda i: (cfg.seg(i), 0, 0))],
        out_specs=[pl.BlockSpec((tm, D), lambda i: (jnp.minimum(i, n_lt - 1), 0)),
                   pl.BlockSpec((None, 8, D), lambda i: (cfg.seg(i), 0, 0))],
        out_shape=[jax.ShapeDtypeStruct((cfg.t_lat, D), F32), jax.ShapeDtypeStruct((3, 8, D), F32)],
        compiler_params=_params(("arbitrary",)))(dres, dxin, h, mod)


def _loss(cfg, h, target, name):
    tm = cfg.tm
    n_lt = 2 * cfg.nlt

    def body(h_ref, t_ref, dy_ref, l_ref):
        i = pl.program_id(0)

        @pl.when(i == 0)
        def _():
            l_ref[...] = jnp.zeros_like(l_ref)

        @pl.when(i < n_lt)
        def _():
            err = h_ref[...] - t_ref[...]
            dy_ref[...] = err * (1.0 / D)
            part = jnp.sum(jnp.sum(err * err, axis=1, keepdims=True), axis=0, keepdims=True) * (0.5 / D)
            l_ref[...] += jnp.broadcast_to(part, l_ref.shape)

        @pl.when(i >= n_lt)
        def _():
            dy_ref[...] = jnp.zeros_like(dy_ref)

    return pl.pallas_call(
        body, grid=(cfg.nt,), name=name,
        in_specs=[pl.BlockSpec((tm, D), lambda i: (i, 0)),
                  pl.BlockSpec((tm, D), lambda i: (jnp.minimum(i, n_lt - 1), 0))],
        out_specs=[pl.BlockSpec((tm, D), lambda i: (i, 0)), pl.BlockSpec((8, 128), lambda i: (0, 0))],
        out_shape=[jax.ShapeDtypeStruct((cfg.T, D), F32), jax.ShapeDtypeStruct((8, 128), F32)],
        compiler_params=_params(("arbitrary",)))(h, target)


def _matmul(a, b, mode, out_dtype, name, bm_cap=512, bn_cap=1408, bk_cap=1024):
    if mode == "nn":
        (M, K), N = a.shape, b.shape[1]
    elif mode == "nt":
        (M, K), N = a.shape, b.shape[0]
    else:
        (K, M), N = a.shape, b.shape[1]
    bm, bn, bk = _pick(M, bm_cap), _pick(N, bn_cap), _pick(K, bk_cap)
    nk = K // bk

    def body(a_ref, b_ref, o_ref, acc_ref=None):
        k = pl.program_id(2)
        if mode == "nn":
            part = _dot(a_ref[...], b_ref[...])
        elif mode == "nt":
            part = _dot_nt(a_ref[...], b_ref[...])
        else:
            part = _dot_tn(a_ref[...], b_ref[...])
        if nk == 1:
            o_ref[...] = part.astype(out_dtype)
            return

        @pl.when(k == 0)
        def _():
            acc_ref[...] = part

        @pl.when((k > 0) & (k < nk - 1))
        def _():
            acc_ref[...] += part

        @pl.when(k == nk - 1)
        def _():
            o_ref[...] = (acc_ref[...] + part).astype(out_dtype)

    if mode == "nn":
        a_spec = pl.BlockSpec((bm, bk), lambda i, j, k: (i, k))
        b_spec = pl.BlockSpec((bk, bn), lambda i, j, k: (k, j))
    elif mode == "nt":
        a_spec = pl.BlockSpec((bm, bk), lambda i, j, k: (i, k))
        b_spec = pl.BlockSpec((bn, bk), lambda i, j, k: (j, k))
    else:
        a_spec = pl.BlockSpec((bk, bm), lambda i, j, k: (k, i))
        b_spec = pl.BlockSpec((bk, bn), lambda i, j, k: (k, j))
    return pl.pallas_call(
        body, grid=(M // bm, N // bn, nk), name=name, in_specs=[a_spec, b_spec],
        out_specs=pl.BlockSpec((bm, bn), lambda i, j, k: (i, j)),
        out_shape=jax.ShapeDtypeStruct((M, N), out_dtype),
        scratch_shapes=[pltpu.VMEM((bm, bn), F32)] if nk > 1 else [],
        compiler_params=_params(("parallel", "parallel", "arbitrary")))(a, b)


def _ffn_tile(T, cap):
    best = 256
    for t in range(256, cap + 1, 256):
        if T % t == 0:
            best = t
    return best


def _ffn_fwd(xin, wf, name):
    T = xin.shape[0]
    F = wf.shape[1]
    tm, tf = _ffn_tile(T, 768), F // 2
    assert tf % 128 == 0 and T % tm == 0

    def body(x_ref, wg_ref, wu_ref, wd_ref, g_ref, u_ref, y_ref, acc_ref):
        j = pl.program_id(1)
        x = x_ref[...]
        acc = None
        for c0, cw in _chunks(tf):
            g = _dot_nt(x, wg_ref[c0:c0 + cw, :])
            u = _dot_nt(x, wu_ref[c0:c0 + cw, :])
            g_ref[:, c0:c0 + cw] = g.astype(BF16)
            u_ref[:, c0:c0 + cw] = u.astype(BF16)
            part = _dot(g * _sigmoid(g) * u, wd_ref[c0:c0 + cw, :])
            acc = part if acc is None else acc + part

        @pl.when(j == 0)
        def _():
            acc_ref[...] = acc

        @pl.when(j == 1)
        def _():
            y_ref[...] = (acc_ref[...] + acc).astype(BF16)

    return pl.pallas_call(
        body, grid=(T // tm, 2), name=name,
        in_specs=[pl.BlockSpec((tm, D), lambda i, j: (i, 0)),
                  pl.BlockSpec((None, tf, D), lambda i, j: (0, j, 0)),
                  pl.BlockSpec((None, tf, D), lambda i, j: (1, j, 0)),
                  pl.BlockSpec((None, tf, D), lambda i, j: (2, j, 0))],
        out_specs=[pl.BlockSpec((tm, tf), lambda i, j: (i, j)),
                   pl.BlockSpec((tm, tf), lambda i, j: (i, j)),
                   pl.BlockSpec((tm, D), lambda i, j: (i, 0))],
        out_shape=[jax.ShapeDtypeStruct((T, F), BF16), jax.ShapeDtypeStruct((T, F), BF16),
                   jax.ShapeDtypeStruct((T, D), BF16)],
        scratch_shapes=[pltpu.VMEM((tm, D), F32)],
        compiler_params=_params(("parallel", "arbitrary")))(xin, wf, wf, wf)


def _ffn_bwd(dys, g, u, wf, name):
    T = dys.shape[0]
    F = wf.shape[1]
    tm, tf = _ffn_tile(T, 512), F // 2

    def body(dy_ref, g_ref, u_ref, wg_ref, wu_ref, wd_ref, dg_ref, du_ref, a_ref, dx_ref, acc_ref):
        j = pl.program_id(1)
        da_all = _dot_nt(dy_ref[...], wd_ref[...])
        for c0, cw in _chunks(tf):
            gg = g_ref[:, c0:c0 + cw].astype(F32)
            uu = u_ref[:, c0:c0 + cw].astype(F32)
            da = da_all[:, c0:c0 + cw]
            s = _sigmoid(gg)
            silu = gg * s
            a_ref[:, c0:c0 + cw] = (silu * uu).astype(BF16)
            du_ref[:, c0:c0 + cw] = (da * silu).astype(BF16)
            dg_ref[:, c0:c0 + cw] = (da * uu * (s * (1.0 + gg * (1.0 - s)))).astype(BF16)
        acc = _dot(dg_ref[...], wg_ref[...]) + _dot(du_ref[...], wu_ref[...])

        @pl.when(j == 0)
        def _():
            acc_ref[...] = acc

        @pl.when(j == 1)
        def _():
            dx_ref[...] = (acc_ref[...] + acc).astype(BF16)

    blk = pl.BlockSpec((tm, tf), lambda i, j: (i, j))
    return pl.pallas_call(
        body, grid=(T // tm, 2), name=name,
        in_specs=[pl.BlockSpec((tm, D), lambda i, j: (i, 0)), blk, blk,
                  pl.BlockSpec((None, tf, D), lambda i, j: (0, j, 0)),
                  pl.BlockSpec((None, tf, D), lambda i, j: (1, j, 0)),
                  pl.BlockSpec((None, tf, D), lambda i, j: (2, j, 0))],
        out_specs=[blk, blk, blk, pl.BlockSpec((tm, D), lambda i, j: (i, 0))],
        out_shape=[jax.ShapeDtypeStruct((T, F), BF16), jax.ShapeDtypeStruct((T, F), BF16),
                   jax.ShapeDtypeStruct((T, F), BF16), jax.ShapeDtypeStruct((T, D), BF16)],
        scratch_shapes=[pltpu.VMEM((tm, D), F32)],
        compiler_params=_params(("parallel", "arbitrary")))(dys, g, u, wf, wf, wf)


def _swap_halves(x):
    w = x.shape[1]
    lane = lax.broadcasted_iota(jnp.int32, (1, w), 1)
    return jnp.where((lane & 63) < 32, pltpu.roll(x, w - 32, 1), pltpu.roll(x, 32, 1))


def _rope(x, cos, sin):
    return x * cos + _swap_halves(x) * sin


def _rope_t(dy, cos, sin):
    return dy * cos + _swap_halves(dy * sin)


def _rope_tables(n_lat):
    rows = n_lat // GRID_W
    row = jnp.repeat(jnp.arange(rows, dtype=F32), GRID_W)
    col = jnp.tile(jnp.arange(GRID_W, dtype=F32), rows)
    inv = ROPE_THETA ** (-jnp.arange(ROPE_FREQS, dtype=F32) / ROPE_FREQS)
    ang = jnp.concatenate([row[:, None] * inv, col[:, None] * inv], axis=-1)
    cs, sn = jnp.cos(ang), jnp.sin(ang)
    cos = jnp.concatenate([cs, cs, cs, cs], axis=-1)
    sin = jnp.concatenate([-sn, sn, -sn, sn], axis=-1)
    return cos, sin


def _attn_specs(cfg):
    n_lat, n_ctx, cb = cfg.n_lat, cfg.n_ctx, cfg.ctx_blk
    return [pl.BlockSpec((n_lat, ATT_W), lambda e: (e, 0)),
            pl.BlockSpec((n_lat, 128), lambda e: (e, 4)),
            pl.BlockSpec((n_lat, 128), lambda e: (e, 5)),
            pl.BlockSpec((n_ctx, ATT_W), lambda e: (cb + e, 0)),
            pl.BlockSpec((n_ctx, 128), lambda e: (cb + e, 4)),
            pl.BlockSpec((n_ctx, 128), lambda e: (cb + e, 5)),
            pl.BlockSpec((n_lat, 128), lambda e: (0, 0)),
            pl.BlockSpec((n_lat, 128), lambda e: (0, 0)),
            pl.BlockSpec((8, 128), lambda e: (0, 0))]


def _attn_prepare(kh, kl, vl, kc, vc, ka, kb, va, vb, kca, kcb, vca, vcb):
    lane = lax.broadcasted_iota(jnp.int32, (1, 128), 1)
    own = (lane < 64) if kh == 0 else (lane >= 64)

    def split(x, ra, rb):
        mine = jnp.where(own, x, 0.0)
        other = pltpu.roll(mine, 64, 1)
        a, b = (mine, other) if kh == 0 else (other, mine)
        ra[...] = a.astype(BF16)
        rb[...] = b.astype(BF16)

    split(kl, ka, kb)
    split(vl, va, vb)
    split(kc, kca, kcb)
    split(vc, vca, vcb)


def _softmax_parts(s_list, sk):
    m = sk
    for s in s_list:
        m = jnp.maximum(m, jnp.max(s, axis=1, keepdims=True))
    es = [jnp.exp(s - m) for s in s_list]
    esk = jnp.exp(sk - m)
    den = esk
    for e in es:
        den = den + jnp.sum(e, axis=1, keepdims=True)
    inv = 1.0 / den
    return [e * inv for e in es], esk * inv


def _window(cfg, n):
    r0 = pl.multiple_of(n * BLK, BLK)
    start = pl.multiple_of(jnp.clip((n - 1) * BLK, 0, cfg.n_lat - 3 * BLK), BLK)
    qpos = r0 + lax.broadcasted_iota(jnp.int32, (BLK, 1), 0)
    kpos = start + lax.broadcasted_iota(jnp.int32, (1, 3 * BLK), 1)
    valid = jnp.abs(qpos - kpos) <= BLK
    return r0, start, valid


def _attn_fwd(cfg, p, cos, sin, sink_rows, name):
    n_lat, n_ctx = cfg.n_lat, cfg.n_ctx

    def body(q_ref, k_ref, v_ref, qc_ref, kc_ref, vc_ref, cos_ref, sin_ref, sink_ref, o_ref, oc_ref,
             qr, ka, kb, va, vb, kca, kcb, vca, vcb):
        cos_t, sin_t = cos_ref[...], sin_ref[...]
        for gq in range(4):
            qr[:, gq * 128:(gq + 1) * 128] = _rope(q_ref[:, gq * 128:(gq + 1) * 128], cos_t, sin_t).astype(BF16)
        kl = _rope(k_ref[...], cos_t, sin_t)
        for kh in range(KV_HEADS):
            _attn_prepare(kh, kl, v_ref[...], kc_ref[...], vc_ref[...], ka, kb, va, vb, kca, kcb, vca, vcb)

            def lat_block(n, carry):
                r0, start, valid = _window(cfg, n)
                win = pl.ds(start, 3 * BLK)
                lanes = [slice((kh * 2 + pr) * 128, (kh * 2 + pr + 1) * 128) for pr in range(2)]
                qps = [qr[pl.ds(r0, BLK), lanes[pr]] for pr in range(2)]
                kws, kcs = (ka[win, :], kb[win, :]), (kca[...], kcb[...])
                scores = [(jnp.where(valid, _dot_nt(qps[pr], kws[half]) * ATT_SCALE, NEG_INF),
                           _dot_nt(qps[pr], kcs[half]) * ATT_SCALE) for pr in range(2) for half in range(2)]
                probs = []
                for idx, (s_w, s_c) in enumerate(scores):
                    head = kh * 4 + idx
                    (p_w, p_c), _ = _softmax_parts([s_w, s_c], sink_ref[head:head + 1, 0:1])
                    probs.append((p_w.astype(BF16), p_c.astype(BF16)))
                vws, vcs = (va[win, :], vb[win, :]), (vca[...], vcb[...])
                for pr in range(2):
                    o = (_dot(probs[2 * pr][0], vws[0]) + _dot(probs[2 * pr][1], vcs[0])
                         + _dot(probs[2 * pr + 1][0], vws[1]) + _dot(probs[2 * pr + 1][1], vcs[1]))
                    o_ref[pl.ds(r0, BLK), lanes[pr]] = o.astype(BF16)
                return carry

            lax.fori_loop(0, n_lat // BLK, lat_block, 0, unroll=2)
            for n in range(n_ctx // BLK):
                rows = slice(n * BLK, (n + 1) * BLK)
                for pr in range(2):
                    lanes = slice((kh * 2 + pr) * 128, (kh * 2 + pr + 1) * 128)
                    qp = qc_ref[rows, lanes]
                    o = None
                    for half, (kcx, vcx) in enumerate(((kca, vca), (kcb, vcb))):
                        head = kh * 4 + pr * 2 + half
                        s_c = _dot_nt(qp, kcx[...]) * ATT_SCALE
                        (p_c,), _ = _softmax_parts([s_c], sink_ref[head:head + 1, 0:1])
                        part = _dot(p_c, vcx[...])
                        o = part if o is None else o + part
                    oc_ref[rows, lanes] = o.astype(BF16)

    return pl.pallas_call(
        body, grid=(2,), name=name, in_specs=_attn_specs(cfg),
        out_specs=[pl.BlockSpec((n_lat, ATT_W), lambda e: (e, 0)), pl.BlockSpec((n_ctx, ATT_W), lambda e: (e, 0))],
        out_shape=[jax.ShapeDtypeStruct((cfg.t_lat, ATT_W), BF16), jax.ShapeDtypeStruct((cfg.t_ctx, ATT_W), BF16)],
        scratch_shapes=[pltpu.VMEM((n_lat, ATT_W), BF16)] + [pltpu.VMEM((n_lat, 128), BF16)] * 4
        + [pltpu.VMEM((n_ctx, 128), BF16)] * 4,
        compiler_params=_params(("parallel",)))(p, p, p, p, p, p, cos, sin, sink_rows)


def _attn_bwd(cfg, p, dcat, cos, sin, sink_rows, name):
    n_lat, n_ctx, cb = cfg.n_lat, cfg.n_ctx, cfg.ctx_blk

    def body(q_ref, k_ref, v_ref, qc_ref, kc_ref, vc_ref, cos_ref, sin_ref, sink_ref, do_ref, doc_ref,
             dq_ref, dk_ref, dv_ref, dqc_ref, dkc_ref, dvc_ref, dsink_ref,
             qr, ka, kb, va, vb, kca, kcb, vca, vcb, dqs, dka, dva, dkca, dvca):
        cos_t, sin_t = cos_ref[...], sin_ref[...]
        lane = lax.broadcasted_iota(jnp.int32, (1, 128), 1)
        lo = lane < 64
        for gq in range(4):
            qr[:, gq * 128:(gq + 1) * 128] = _rope(q_ref[:, gq * 128:(gq + 1) * 128], cos_t, sin_t).astype(BF16)
        kl = _rope(k_ref[...], cos_t, sin_t)
        dsink_ref[...] = jnp.zeros_like(dsink_ref)
        dka[...] = jnp.zeros_like(dka)
        dva[...] = jnp.zeros_like(dva)
        dkca[...] = jnp.zeros_like(dkca)
        dvca[...] = jnp.zeros_like(dvca)

        def halves(x):
            return jnp.where(lo, x, 0).astype(BF16), jnp.where(lo, 0, x).astype(BF16)

        for kh in range(KV_HEADS):
            _attn_prepare(kh, kl, v_ref[...], kc_ref[...], vc_ref[...], ka, kb, va, vb, kca, kcb, vca, vcb)

            def one_head(head, qp, q_half, do_p, do_half, kw, kcx, vw, vcx, win, valid):
                sk = sink_ref[head:head + 1, 0:1]
                s_list = [_dot_nt(qp, kcx[...]) * ATT_SCALE]
                if win is not None:
                    s_list.insert(0, jnp.where(valid, _dot_nt(qp, kw[win, :]) * ATT_SCALE, NEG_INF))
                probs, p_sink = _softmax_parts(s_list, sk)
                vals = [vcx[...]] if win is None else [vw[win, :], vcx[...]]
                dps = [_dot_nt(do_p, vv) for vv in vals]
                dr = None
                for pp, dp in zip(probs, dps):
                    t = jnp.sum(pp * dp, axis=1, keepdims=True)
                    dr = t if dr is None else dr + t
                dss = [(pp * (dp - dr) * ATT_SCALE).astype(BF16) for pp, dp in zip(probs, dps)]
                dsink_ref[head:head + 1, :] += jnp.broadcast_to(
                    jnp.sum(-p_sink * dr, axis=0, keepdims=True), (1, 128))
                p_c, ds_c = probs[-1], dss[-1]
                dq = _dot(ds_c, kcx[...])
                dkca[kh] += _dot_tn(ds_c, q_half)
                dvca[kh] += _dot_tn(p_c, do_half)
                if win is not None:
                    dq = dq + _dot(dss[0], kw[win, :])
                    dka[kh, win, :] += _dot_tn(dss[0], q_half)
                    dva[kh, win, :] += _dot_tn(probs[0], do_half)
                return dq

            def lat_block(n, carry):
                r0, start, valid = _window(cfg, n)
                win = pl.ds(start, 3 * BLK)
                lanes = [slice((kh * 2 + pr) * 128, (kh * 2 + pr + 1) * 128) for pr in range(2)]
                qps = [qr[pl.ds(r0, BLK), lanes[pr]] for pr in range(2)]
                dops = [do_ref[pl.ds(r0, BLK), lanes[pr]].astype(BF16) for pr in range(2)]
                heads = [(pr, half) for pr in range(2) for half in range(2)]
                kws, kcs = (ka[win, :], kb[win, :]), (kca[...], kcb[...])
                vws, vcs = (va[win, :], vb[win, :]), (vca[...], vcb[...])
                soft = []
                for idx, (pr, half) in enumerate(heads):
                    s_w = jnp.where(valid, _dot_nt(qps[pr], kws[half]) * ATT_SCALE, NEG_INF)
                    s_c = _dot_nt(qps[pr], kcs[half]) * ATT_SCALE
                    soft.append(_softmax_parts([s_w, s_c], sink_ref[kh * 4 + idx:kh * 4 + idx + 1, 0:1]))
                dps = [(_dot_nt(dops[pr], vws[half]), _dot_nt(dops[pr], vcs[half])) for pr, half in heads]
                ds_w, ds_c, pb_w, pb_c = [], [], [], []
                for idx in range(4):
                    (p_w, p_c), p_sink = soft[idx]
                    dp_w, dp_c = dps[idx]
                    dr = jnp.sum(p_w * dp_w, axis=1, keepdims=True) + jnp.sum(p_c * dp_c, axis=1, keepdims=True)
                    ds_w.append((p_w * (dp_w - dr) * ATT_SCALE).astype(BF16))
                    ds_c.append((p_c * (dp_c - dr) * ATT_SCALE).astype(BF16))
                    pb_w.append(p_w.astype(BF16))
                    pb_c.append(p_c.astype(BF16))
                    head = kh * 4 + idx
                    dsink_ref[head:head + 1, :] += jnp.broadcast_to(
                        jnp.sum(-p_sink * dr, axis=0, keepdims=True), (1, 128))
                for pr in range(2):
                    dqs[pl.ds(r0, BLK), lanes[pr]] = (
                        _dot(ds_w[2 * pr], kws[0]) + _dot(ds_c[2 * pr], kcs[0])
                        + _dot(ds_w[2 * pr + 1], kws[1]) + _dot(ds_c[2 * pr + 1], kcs[1]))
                q_hs, do_hs = [halves(qp) for qp in qps], [halves(do_p) for do_p in dops]
                q_stack = jnp.concatenate([q_hs[pr][half] for pr, half in heads], axis=0)
                do_stack = jnp.concatenate([do_hs[pr][half] for pr, half in heads], axis=0)
                dka[kh, win, :] += _dot_tn(jnp.concatenate(ds_w, axis=0), q_stack)
                dva[kh, win, :] += _dot_tn(jnp.concatenate(pb_w, axis=0), do_stack)
                dkca[kh] += _dot_tn(jnp.concatenate(ds_c, axis=0), q_stack)
                dvca[kh] += _dot_tn(jnp.concatenate(pb_c, axis=0), do_stack)
                return carry

            lax.fori_loop(0, n_lat // BLK, lat_block, 0, unroll=2)
            for n in range(n_ctx // BLK):
                rows = slice(n * BLK, (n + 1) * BLK)
                for pr in range(2):
                    lanes = slice((kh * 2 + pr) * 128, (kh * 2 + pr + 1) * 128)
                    qp = qc_ref[rows, lanes].astype(BF16)
                    do_p = doc_ref[rows, lanes]
                    q_h, do_h = halves(qp), halves(do_p)
                    dq = None
                    for half, (kcx, vcx) in enumerate(((kca, vca), (kcb, vcb))):
                        part = one_head(kh * 4 + pr * 2 + half, qp, q_h[half], do_p, do_h[half],
                                        None, kcx, None, vcx, None, None)
                        dq = part if dq is None else dq + part
                    dqc_ref[rows, lanes] = dq.astype(BF16)

        def fold(acc):
            r0 = acc[0] + pltpu.roll(acc[0], 64, 1)
            r1 = acc[1] + pltpu.roll(acc[1], 64, 1)
            return jnp.where(lo, r0, r1)

        for gq in range(4):
            sl = slice(gq * 128, (gq + 1) * 128)
            dq_ref[:, sl] = _rope_t(dqs[:, sl], cos_t, sin_t).astype(BF16)
        dk_ref[...] = _rope_t(fold(dka), cos_t, sin_t).astype(BF16)
        dv_ref[...] = fold(dva).astype(BF16)
        dkc_ref[...] = fold(dkca).astype(BF16)
        dvc_ref[...] = fold(dvca).astype(BF16)

    lat = lambda w: pl.BlockSpec((n_lat, w), lambda e: (e, 0))
    ctx = lambda w: pl.BlockSpec((n_ctx, w), lambda e: (e, 0))
    sd = jax.ShapeDtypeStruct
    return pl.pallas_call(
        body, grid=(2,), name=name,
        in_specs=_attn_specs(cfg) + [pl.BlockSpec((n_lat, ATT_W), lambda e: (e, 0)),
                                     pl.BlockSpec((n_ctx, ATT_W), lambda e: (cb + e, 0))],
        out_specs=[lat(ATT_W), lat(128), lat(128), ctx(ATT_W), ctx(128), ctx(128),
                   pl.BlockSpec((None, 8, 128), lambda e: (e, 0, 0))],
        out_shape=[sd((cfg.t_lat, ATT_W), BF16), sd((cfg.t_lat, 128), BF16), sd((cfg.t_lat, 128), BF16),
                   sd((cfg.t_ctx, ATT_W), BF16), sd((cfg.t_ctx, 128), BF16), sd((cfg.t_ctx, 128), BF16),
                   sd((2, 8, 128), F32)],
        scratch_shapes=[pltpu.VMEM((n_lat, ATT_W), BF16)] + [pltpu.VMEM((n_lat, 128), BF16)] * 4
        + [pltpu.VMEM((n_ctx, 128), BF16)] * 4
        + [pltpu.VMEM((n_lat, ATT_W), F32), pltpu.VMEM((2, n_lat, 128), F32), pltpu.VMEM((2, n_lat, 128), F32),
           pltpu.VMEM((2, n_ctx, 128), F32), pltpu.VMEM((2, n_ctx, 128), F32)],
        compiler_params=_params(("parallel",)))(p, p, p, p, p, p, cos, sin, sink_rows, dcat, dcat)


def _shift_down(x, k, row):
    return jnp.where(row >= k, pltpu.roll(x, k, 0), 0.0)


def _shift_up(x, k, row):
    n = x.shape[0]
    return jnp.where(row < n - k, pltpu.roll(x, n - k, 0), 0.0)


def _window_sum(x, r, row):
    below, above, k = x, x, 1
    while k < r:
        below = below + _shift_down(below, k, row)
        above = above + _shift_up(above, k, row)
        k *= 2
    return below + _shift_down(x, r, row) + _shift_up(above, 1, row)


def _inv_count(r, row, n):
    cnt = jnp.minimum(row + r, n - 1) + 1 - jnp.maximum(row - r, 0)
    return 1.0 / cnt.astype(F32)


def _pool_fwd(p, w, scale, n, blk0, n_seg, name):
    def body(u0, u1, u2, u3, w_ref, sc_ref, o_ref):
        row = lax.broadcasted_iota(jnp.int32, (n, 1), 0)
        for g, u_ref in enumerate((u0, u1, u2, u3)):
            u = u_ref[...]
            d = _window_sum(u, POOL_R[g], row) * _inv_count(POOL_R[g], row, n) - u
            o_ref[:, g * 128:(g + 1) * 128] = (_dot(d, w_ref[g]) * sc_ref[:, g * 128:(g + 1) * 128]).astype(BF16)

    return pl.pallas_call(
        body, grid=(n_seg,), name=name,
        in_specs=[pl.BlockSpec((n, 128), functools.partial(lambda g, e: (blk0 + e, 6 + g), g)) for g in range(4)]
        + [pl.BlockSpec((4, 128, 128), lambda e: (0, 0, 0)), pl.BlockSpec((1, 512), lambda e: (0, 0))],
        out_specs=pl.BlockSpec((n, 512), lambda e: (e, 0)),
        out_shape=jax.ShapeDtypeStruct((n_seg * n, 512), BF16),
        compiler_params=_params(("parallel",)))(p, p, p, p, w, scale)


def _pool_bwd(p, w, scale, dcat, n, blk0, n_seg, name):
    def body(u0, u1, u2, u3, w_ref, sc_ref, dp_ref, du_ref, dw_ref, dsc_ref):
        e = pl.program_id(0)

        @pl.when(e == 0)
        def _():
            dw_ref[...] = jnp.zeros_like(dw_ref)
            dsc_ref[...] = jnp.zeros_like(dsc_ref)

        row = lax.broadcasted_iota(jnp.int32, (n, 1), 0)
        for g, u_ref in enumerate((u0, u1, u2, u3)):
            sl = slice(g * 128, (g + 1) * 128)
            u = u_ref[...]
            inv = _inv_count(POOL_R[g], row, n)
            d = _window_sum(u, POOL_R[g], row) * inv - u
            dp = dp_ref[:, sl].astype(F32)
            dsc_ref[:, sl] += jnp.sum(dp * _dot(d, w_ref[g]), axis=0, keepdims=True)
            dyp = dp * sc_ref[:, sl]
            dw_ref[g] += _dot_tn(d, dyp)
            dd = _dot_nt(dyp, w_ref[g])
            du_ref[:, sl] = (_window_sum(dd * inv, POOL_R[g], row) - dd).astype(BF16)

    return pl.pallas_call(
        body, grid=(n_seg,), name=name,
        in_specs=[pl.BlockSpec((n, 128), functools.partial(lambda g, e: (blk0 + e, 6 + g), g)) for g in range(4)]
        + [pl.BlockSpec((4, 128, 128), lambda e: (0, 0, 0)), pl.BlockSpec((1, 512), lambda e: (0, 0)),
           pl.BlockSpec((n, 512), lambda e: (blk0 + e, 1))],
        out_specs=[pl.BlockSpec((n, 512), lambda e: (e, 0)),
                   pl.BlockSpec((4, 128, 128), lambda e: (0, 0, 0)), pl.BlockSpec((1, 512), lambda e: (0, 0))],
        out_shape=[jax.ShapeDtypeStruct((n_seg * n, 512), BF16), jax.ShapeDtypeStruct((4, 128, 128), F32),
                   jax.ShapeDtypeStruct((1, 512), F32)],
        compiler_params=_params(("arbitrary",)))(p, p, p, p, w, scale, dcat)


def _gelu(x):
    t = jnp.tanh(math.sqrt(2.0 / math.pi) * (x + 0.044715 * x * x * x))
    return 0.5 * x * (1.0 + t), t


def _gelu_grad(x, t):
    return 0.5 * (1.0 + t) + 0.5 * x * (1.0 - t * t) * (math.sqrt(2.0 / math.pi) * (1.0 + 3 * 0.044715 * x * x))


def _neg_expm1_twice(x):
    t = jnp.tanh(x)
    return (-2.0 * t) / (1.0 - t)


def _softplus_neg(lam):
    x = -lam
    e = jnp.exp(-jnp.abs(x))
    log1p = jnp.where(e < 1e-2, e * (1.0 - e * (0.5 - e * (1.0 / 3.0))), jnp.log(1.0 + e))
    return jnp.maximum(x, 0.0) + log1p, -_sigmoid(x)


def _conv(u, w_ref, b_ref, row):
    return (b_ref[...] + _shift_down(u, 1, row) * w_ref[0:1, :] + u * w_ref[1:2, :]
            + _shift_up(u, 1, row) * w_ref[2:3, :] + _shift_up(u, 2, row) * w_ref[3:4, :])


def _lru_gates(uc, d, wa_ref, ba_ref, wx_ref, bx_ref, lam_ref):
    r = _sigmoid(_dot(uc, wa_ref[d]) + ba_ref[d:d + 1, :])
    gi = _sigmoid(_dot(uc, wx_ref[d]) + bx_ref[d:d + 1, :])
    sp, dsp = _softplus_neg(lam_ref[d:d + 1, :])
    la = (-LRU_C) * r * sp
    a = jnp.exp(la)
    sq = jnp.sqrt(_neg_expm1_twice(la))
    return r, gi, sp, dsp, a, sq


def _tile_scan(a_ref, b_ref, n, reverse):
    m = n // 8
    first = 7 if reverse else 0
    a_prev = a_ref[pl.ds(first, m, stride=8), :]
    b_prev = b_ref[pl.ds(first, m, stride=8), :]
    for j in (range(6, -1, -1) if reverse else range(1, 8)):
        rows = pl.ds(j, m, stride=8)
        aj = a_ref[rows, :]
        b_prev = aj * b_prev + b_ref[rows, :]
        a_prev = aj * a_prev
        b_ref[rows, :] = b_prev
        a_ref[rows, :] = a_prev


def _carry_scan(a_ref, b_ref, n, reverse, carry):
    nt8 = n // 8

    def step(i, c):
        t = (nt8 - 1 - i) if reverse else i
        off = pl.multiple_of(t * 8, 8)
        h = a_ref[pl.ds(off, 8), :] * c + b_ref[pl.ds(off, 8), :]
        b_ref[pl.ds(off, 8), :] = h
        return h[0:1, :] if reverse else h[7:8, :]

    return lax.fori_loop(0, nt8, step, carry, unroll=4)


def _chain_scan(segs, reverse):
    carry = jnp.zeros((1, 128), F32)
    for a, b, a_ref, b_ref, n in segs:
        a_ref[...] = a
        b_ref[...] = b
        _tile_scan(a_ref, b_ref, n, reverse)
        carry = _carry_scan(a_ref, b_ref, n, reverse, carry)


def _lru_specs(cfg):
    n_lat, n_ctx, cb = cfg.n_lat, cfg.n_ctx, cfg.ctx_blk
    return [pl.BlockSpec((n_lat, 128), lambda hb, e: (e, hb)),
            pl.BlockSpec((n_lat, 128), lambda hb, e: (e, 8 + hb)),
            pl.BlockSpec((n_ctx, 128), lambda hb, e: (cb + e, hb)),
            pl.BlockSpec((n_ctx, 128), lambda hb, e: (cb + e, 8 + hb)),
            pl.BlockSpec((4, 128), lambda hb, e: (0, hb)),
            pl.BlockSpec((1, 128), lambda hb, e: (0, hb)),
            pl.BlockSpec((2, None, 128, 128), lambda hb, e: (0, hb, 0, 0)),
            pl.BlockSpec((2, 128), lambda hb, e: (0, hb)),
            pl.BlockSpec((2, None, 128, 128), lambda hb, e: (0, hb, 0, 0)),
            pl.BlockSpec((2, 128), lambda hb, e: (0, hb)),
            pl.BlockSpec((2, 128), lambda hb, e: (0, hb))]


def _lru_fwd(cfg, p, consts, name):
    n_lat, n_ctx = cfg.n_lat, cfg.n_ctx

    def body(gl_ref, ul_ref, gc_ref, uc_ref, cw_ref, cb_ref, wa_ref, ba_ref, wx_ref, bx_ref, lam_ref,
             zl_ref, zc_ref, hl_ref, hc_ref, al, ac):
        row_l = lax.broadcasted_iota(jnp.int32, (n_lat, 1), 0)
        row_c = lax.broadcasted_iota(jnp.int32, (n_ctx, 1), 0)
        uc_l = _conv(ul_ref[...], cw_ref, cb_ref, row_l)
        uc_c = _conv(uc_ref[...], cw_ref, cb_ref, row_c)
        for d in range(2):
            _, gi_l, _, _, a_l, sq_l = _lru_gates(uc_l, d, wa_ref, ba_ref, wx_ref, bx_ref, lam_ref)
            _, gi_c, _, _, a_c, sq_c = _lru_gates(uc_c, d, wa_ref, ba_ref, wx_ref, bx_ref, lam_ref)
            _chain_scan([(a_c, sq_c * (gi_c * uc_c), ac, hc_ref.at[d], n_ctx),
                         (a_l, sq_l * (gi_l * uc_l), al, hl_ref.at[d], n_lat)], reverse=(d == 1))
        zl_ref[...] = (_gelu(gl_ref[...])[0] * (hl_ref[0] + hl_ref[1])).astype(BF16)
        zc_ref[...] = (_gelu(gc_ref[...])[0] * (hc_ref[0] + hc_ref[1])).astype(BF16)

    return pl.pallas_call(
        body, grid=(8, 2), name=name, in_specs=_lru_specs(cfg),
        out_specs=[pl.BlockSpec((n_lat, 128), lambda hb, e: (e, hb)), pl.BlockSpec((n_ctx, 128), lambda hb, e: (e, hb)),
                   pl.BlockSpec((2, n_lat, 128), lambda hb, e: (0, e, hb)),
                   pl.BlockSpec((2, n_ctx, 128), lambda hb, e: (0, e, hb))],
        out_shape=[jax.ShapeDtypeStruct((cfg.t_lat, D), BF16), jax.ShapeDtypeStruct((cfg.t_ctx, D), BF16),
                   jax.ShapeDtypeStruct((2, cfg.t_lat, D), F32), jax.ShapeDtypeStruct((2, cfg.t_ctx, D), F32)],
        scratch_shapes=[pltpu.VMEM((n_lat, 128), F32), pltpu.VMEM((n_ctx, 128), F32)],
        compiler_params=_params(("parallel", "arbitrary")))(p, p, p, p, *consts)


def _lru_bwd(cfg, p, dz, h_lat, h_ctx, consts, name):
    n_lat, n_ctx, cb = cfg.n_lat, cfg.n_ctx, cfg.ctx_blk

    def body(gl_ref, ul_ref, gc_ref, uc_ref, cw_ref, cb_ref, wa_ref, ba_ref, wx_ref, bx_ref, lam_ref,
             dzl_ref, dzc_ref, hl, hc, dgl_ref, dul_ref, dgc_ref, duc_ref, dwa_ref, dwx_ref, vec_ref,
             al, bl, ac, bc):
        e = pl.program_id(1)

        @pl.when(e == 0)
        def _():
            dwa_ref[...] = jnp.zeros_like(dwa_ref)
            dwx_ref[...] = jnp.zeros_like(dwx_ref)
            vec_ref[...] = jnp.zeros_like(vec_ref)

        row_l = lax.broadcasted_iota(jnp.int32, (n_lat, 1), 0)
        row_c = lax.broadcasted_iota(jnp.int32, (n_ctx, 1), 0)
        u_l, u_c = ul_ref[...], uc_ref[...]
        uc_l = _conv(u_l, cw_ref, cb_ref, row_l)
        uc_c = _conv(u_c, cw_ref, cb_ref, row_c)
        gel_l, t_l = _gelu(gl_ref[...])
        gel_c, t_c = _gelu(gc_ref[...])
        dz_l, dz_c = dzl_ref[...].astype(F32), dzc_ref[...].astype(F32)
        dgl_ref[...] = (dz_l * (hl[0] + hl[1]) * _gelu_grad(gl_ref[...], t_l)).astype(BF16)
        dgc_ref[...] = (dz_c * (hc[0] + hc[1]) * _gelu_grad(gc_ref[...], t_c)).astype(BF16)
        dy_l, dy_c = dz_l * gel_l, dz_c * gel_c
        duc_l = jnp.zeros((n_lat, 128), F32)
        duc_c = jnp.zeros((n_ctx, 128), F32)
        for d in range(2):
            r_l, gi_l, sp, dsp, a_l, sq_l = _lru_gates(uc_l, d, wa_ref, ba_ref, wx_ref, bx_ref, lam_ref)
            r_c, gi_c, _, _, a_c, sq_c = _lru_gates(uc_c, d, wa_ref, ba_ref, wx_ref, bx_ref, lam_ref)
            if d == 0:
                an_l = _shift_up(a_l, 1, row_l)
                an_c = jnp.where(row_c < n_ctx - 1, pltpu.roll(a_c, n_ctx - 1, 0), a_l[0:1, :])
            else:
                an_l = _shift_down(a_l, 1, row_l)
                an_c = jnp.where(row_c >= 1, pltpu.roll(a_c, 1, 0), a_l[n_lat - 1:n_lat, :])
            _chain_scan([(an_l, dy_l, al, bl, n_lat), (an_c, dy_c, ac, bc, n_ctx)], reverse=(d == 0))
            dsp_sum = jnp.zeros((1, 128), F32)
            for (dh, h, r, gi, a, sq, uc, seg) in ((bl[...], hl[d], r_l, gi_l, a_l, sq_l, uc_l, "l"),
                                                  (bc[...], hc[d], r_c, gi_c, a_c, sq_c, uc_c, "c")):
                b0 = sq * (gi * uc)
                t1 = dh * sq
                dla = dh * (h - b0) - (dh * gi * uc) * (a * a) / sq
                dzr = (dla * ((-LRU_C) * sp)) * r * (1.0 - r)
                dzi = (t1 * uc) * gi * (1.0 - gi)
                dsp_sum = dsp_sum + jnp.sum(dla * ((-LRU_C) * r), axis=0, keepdims=True)
                dwa_ref[d] += _dot_tn(uc, dzr)
                dwx_ref[d] += _dot_tn(uc, dzi)
                vec_ref[d:d + 1, :] += jnp.sum(dzr, axis=0, keepdims=True)
                vec_ref[2 + d:3 + d, :] += jnp.sum(dzi, axis=0, keepdims=True)
                duc = t1 * gi + _dot_nt(dzr, wa_ref[d]) + _dot_nt(dzi, wx_ref[d])
                if seg == "l":
                    duc_l = duc_l + duc
                else:
                    duc_c = duc_c + duc
            vec_ref[4 + d:5 + d, :] += dsp_sum * dsp
        for duc, u, row, du_ref in ((duc_l, u_l, row_l, dul_ref), (duc_c, u_c, row_c, duc_ref)):
            du_ref[...] = (_shift_up(duc, 1, row) * cw_ref[0:1, :] + duc * cw_ref[1:2, :]
                           + _shift_down(duc, 1, row) * cw_ref[2:3, :]
                           + _shift_down(duc, 2, row) * cw_ref[3:4, :]).astype(BF16)
            vec_ref[6:7, :] += jnp.sum(duc * _shift_down(u, 1, row), axis=0, keepdims=True)
            vec_ref[7:8, :] += jnp.sum(duc * u, axis=0, keepdims=True)
            vec_ref[8:9, :] += jnp.sum(duc * _shift_up(u, 1, row), axis=0, keepdims=True)
            vec_ref[9:10, :] += jnp.sum(duc * _shift_up(u, 2, row), axis=0, keepdims=True)
            vec_ref[10:11, :] += jnp.sum(duc, axis=0, keepdims=True)

    lat = pl.BlockSpec((n_lat, 128), lambda hb, e: (e, hb))
    ctx = pl.BlockSpec((n_ctx, 128), lambda hb, e: (e, hb))
    wspec = pl.BlockSpec((2, None, 128, 128), lambda hb, e: (0, hb, 0, 0))
    sd = jax.ShapeDtypeStruct
    return pl.pallas_call(
        body, grid=(8, 2), name=name,
        in_specs=_lru_specs(cfg) + [pl.BlockSpec((n_lat, 128), lambda hb, e: (e, hb)),
                                    pl.BlockSpec((n_ctx, 128), lambda hb, e: (cb + e, hb)),
                                    pl.BlockSpec((2, n_lat, 128), lambda hb, e: (0, e, hb)),
                                    pl.BlockSpec((2, n_ctx, 128), lambda hb, e: (0, e, hb))],
        out_specs=[lat, lat, ctx, ctx, wspec, wspec, pl.BlockSpec((None, 16, 128), lambda hb, e: (hb, 0, 0))],
        out_shape=[sd((cfg.t_lat, D), BF16), sd((cfg.t_lat, D), BF16), sd((cfg.t_ctx, D), BF16), sd((cfg.t_ctx, D), BF16),
                   sd((2, 8, 128, 128), F32), sd((2, 8, 128, 128), F32), sd((8, 16, 128), F32)],
        scratch_shapes=[pltpu.VMEM((n_lat, 128), F32)] * 2 + [pltpu.VMEM((n_ctx, 128), F32)] * 2,
        compiler_params=_params(("parallel", "arbitrary")))(p, p, p, p, *consts, dz, dz, h_lat, h_ctx)


def _position():
    x, y, c = lax.axis_index("x"), lax.axis_index("y"), lax.axis_index("c")
    return x, y, c, 4 * x + 2 * y + c


def _peer(x, y, c, k):
    px = 1 - x if k & 4 else x
    py = 1 - y if k & 2 else y
    pc = 1 - c if k & 1 else c
    return (px, py, pc), 4 * px + 2 * py + pc


def _all_gather(v, name, in_vmem):
    def body(v_ref, o_ref, send_sems, recv_sems, local_sem):
        x, y, c, me = _position()
        mine = pltpu.make_async_copy(v_ref, o_ref.at[me], local_sem)
        mine.start()
        sends = []
        for k in range(1, N_DEV):
            peer, _ = _peer(x, y, c, k)
            cp = pltpu.make_async_remote_copy(src_ref=v_ref, dst_ref=o_ref.at[me], send_sem=send_sems.at[k - 1],
                                              recv_sem=recv_sems.at[k - 1], device_id=peer, device_id_type=MESH)
            cp.start()
            sends.append(cp)
        for k in range(1, N_DEV):
            peer, peer_lin = _peer(x, y, c, k)
            pltpu.make_async_remote_copy(src_ref=v_ref, dst_ref=o_ref.at[peer_lin], send_sem=send_sems.at[k - 1],
                                         recv_sem=recv_sems.at[k - 1], device_id=peer, device_id_type=MESH).wait_recv()
        for cp in sends:
            cp.wait_send()
        mine.wait()

    space = pltpu.VMEM if in_vmem else pl.ANY
    return pl.pallas_call(
        body, name=name,
        in_specs=[pl.BlockSpec(memory_space=space)], out_specs=pl.BlockSpec(memory_space=space),
        out_shape=jax.ShapeDtypeStruct((N_DEV,) + v.shape, v.dtype),
        scratch_shapes=[pltpu.SemaphoreType.DMA((N_DEV - 1,)), pltpu.SemaphoreType.DMA((N_DEV - 1,)),
                        pltpu.SemaphoreType.DMA],
        compiler_params=pltpu.CompilerParams(vmem_limit_bytes=VMEM_LIMIT))(v)


_HBM = pl.BlockSpec(memory_space=pltpu.HBM)
_SEM = pl.BlockSpec(memory_space=pltpu.SEMAPHORE)
_EFFECT = pltpu.SideEffectType.DATAFLOW_SIDE_EFFECTING


ALL_PEERS = tuple(range(1, N_DEV))
SAME_CORE_AND_SIBLING = (1, 2, 4, 6)


def _push_start(src, land, block_of, name, relations=ALL_PEERS):
    def body(src_ref, land_ref, send_sem, recv_sem, src_thru, land_thru, token):
        x, y, c, me = _position()
        for k in relations:
            peer, peer_lin = _peer(x, y, c, k)
            mine, there = block_of(src_ref, land_ref, me, peer_lin)
            pltpu.make_async_remote_copy(src_ref=mine, dst_ref=there, send_sem=send_sem, recv_sem=recv_sem,
                                         device_id=peer, device_id_type=MESH).start()
        mine, here = block_of(src_ref, land_ref, me, me)
        pltpu.make_async_copy(mine, here, recv_sem).start()
        token[...] = jnp.zeros_like(token)

    return pl.pallas_call(
        body, name=name,
        out_shape=(pltpu.SemaphoreType.DMA(()), pltpu.SemaphoreType.DMA(()), pltpu.HBM(src.shape, src.dtype),
                   pltpu.HBM(land.shape, land.dtype), jax.ShapeDtypeStruct((8, 128), F32)),
        in_specs=(_HBM, _HBM), out_specs=(_SEM, _SEM, _HBM, _HBM, pl.BlockSpec(memory_space=pltpu.VMEM)),
        input_output_aliases={0: 2, 1: 3},
        compiler_params=pltpu.CompilerParams(has_side_effects=_EFFECT),
    )(pltpu.with_memory_space_constraint(src, pltpu.HBM), pltpu.with_memory_space_constraint(land, pltpu.HBM))


def _push_wait(handle, blocks_of, after, name, n_peers=N_DEV - 1):
    send_sem, recv_sem, src_thru, land_thru, _ = handle

    def body(src_ref, land_ref, send_sem, recv_sem, after_ref, src_dead, got_ref):
        x, y, c, _ = _position()
        sent, landed = blocks_of(land_ref, n_peers), blocks_of(land_ref, n_peers + 1)
        pltpu.make_async_remote_copy(src_ref=sent, dst_ref=sent, send_sem=send_sem, recv_sem=recv_sem,
                                     device_id=(x, y, 1 - c), device_id_type=MESH).wait_send()
        pltpu.make_async_remote_copy(src_ref=landed, dst_ref=landed, send_sem=send_sem, recv_sem=recv_sem,
                                     device_id=(x, y, 1 - c), device_id_type=MESH).wait_recv()

    return pl.pallas_call(
        body, name=name,
        out_shape=(pltpu.HBM(src_thru.shape, src_thru.dtype), pltpu.HBM(land_thru.shape, land_thru.dtype)),
        in_specs=(_HBM, _HBM, _SEM, _SEM, pl.BlockSpec(memory_space=pl.ANY)), out_specs=(_HBM, _HBM),
        input_output_aliases={0: 0, 1: 1},
        compiler_params=pltpu.CompilerParams(has_side_effects=_EFFECT),
    )(src_thru, land_thru, send_sem, recv_sem, after)[1]


def _gather_start(src, name, relations=ALL_PEERS):
    g, r, C = src.shape
    land = lax.empty((g, N_DEV * r, C), src.dtype)
    return _push_start(src, land, lambda s, z, i, p: (s, z.at[:, pl.ds(i * r, r), :]), name, relations)


def _gather_wait(handle, after, name, n_peers=N_DEV - 1):
    r = handle[2].shape[1]
    return _push_wait(handle, lambda z, n: z.at[:, pl.ds(0, n * r), :], after, name, n_peers)


def _relay_start(land, r, name):
    def body(land_ref, send_sem, recv_sem, land_thru, token):
        x, y, c, _ = _position()
        for k in (2, 4, 6):
            _, origin = _peer(x, y, c, k)
            rows = land_ref.at[:, pl.ds(origin * r, r), :]
            pltpu.make_async_remote_copy(src_ref=rows, dst_ref=rows, send_sem=send_sem, recv_sem=recv_sem,
                                         device_id=(x, y, 1 - c), device_id_type=MESH).start()
        token[...] = jnp.zeros_like(token)

    return pl.pallas_call(
        body, name=name,
        out_shape=(pltpu.SemaphoreType.DMA(()), pltpu.SemaphoreType.DMA(()), pltpu.HBM(land.shape, land.dtype),
                   jax.ShapeDtypeStruct((8, 128), F32)),
        in_specs=(_HBM,), out_specs=(_SEM, _SEM, _HBM, pl.BlockSpec(memory_space=pltpu.VMEM)),
        input_output_aliases={0: 2},
        compiler_params=pltpu.CompilerParams(has_side_effects=_EFFECT),
    )(pltpu.with_memory_space_constraint(land, pltpu.HBM))


def _relay_wait(handle, r, after, name):
    send_sem, recv_sem, land_thru, _ = handle

    def body(land_ref, send_sem, recv_sem, after_ref, got_ref):
        x, y, c, _ = _position()
        three = land_ref.at[:, pl.ds(0, 3 * r), :]
        cp = pltpu.make_async_remote_copy(src_ref=three, dst_ref=three, send_sem=send_sem, recv_sem=recv_sem,
                                          device_id=(x, y, 1 - c), device_id_type=MESH)
        cp.wait_send()
        cp.wait_recv()

    return pl.pallas_call(
        body, name=name, out_shape=(pltpu.HBM(land_thru.shape, land_thru.dtype),),
        in_specs=(_HBM, _SEM, _SEM, pl.BlockSpec(memory_space=pl.ANY)), out_specs=(_HBM,),
        input_output_aliases={0: 0},
        compiler_params=pltpu.CompilerParams(has_side_effects=_EFFECT),
    )(land_thru, send_sem, recv_sem, after)[0]


def _exchange_start(grad, name):
    g, rows, C = grad.shape
    r = rows // N_DEV
    land = lax.empty((N_DEV, g, r, C), grad.dtype)
    return _push_start(grad, land, lambda s, z, i, p: (s.at[:, pl.ds(p * r, r), :], z.at[i]), name)


def _exchange_wait(handle, after, name):
    return _push_wait(handle, lambda z, n: z.at[pl.ds(0, n)], after, name)


def _sum_blocks(v, name):
    k, rows, cols = v.shape
    tr = rows
    for cand in (rows, 512, 352, 256, 176, 128, 64, 32, 16):
        if rows % cand == 0 and k * cand * cols * v.dtype.itemsize <= 6 * 1024 * 1024:
            tr = cand
            break

    def body(v_ref, o_ref):
        acc = v_ref[0].astype(F32)
        for s in range(1, k):
            acc = acc + v_ref[s].astype(F32)
        o_ref[...] = acc

    return pl.pallas_call(
        body, grid=(rows // tr,), name=name,
        in_specs=[pl.BlockSpec((k, tr, cols), lambda i: (0, i, 0))],
        out_specs=pl.BlockSpec((tr, cols), lambda i: (i, 0)),
        out_shape=jax.ShapeDtypeStruct((rows, cols), F32),
        compiler_params=_params(("parallel",)))(v)


def _adam_math(w, g, m, v):
    m2 = B1 * m + (1.0 - B1) * g
    v2 = B2 * v + (1.0 - B2) * (g * g)
    m_hat = m2 / (1.0 - B1 ** STEP)
    v_hat = v2 / (1.0 - B2 ** STEP)
    return -LR * (m_hat / (jnp.sqrt(v_hat) + EPS) + WD * w), m2, v2


def _adamw(w, g, m, v, name, dep=None):
    shp = w.shape
    rows, cols = (shp[-2], shp[-1]) if len(shp) >= 2 else (1, shp[-1])
    lead = math.prod(shp[:-2]) if len(shp) > 2 else 1
    fits = [t for t in range(8, rows + 1, 8) if rows % t == 0 and t * cols * 4 <= 2 * 1024 * 1024]
    tr = max(fits) if fits else rows

    def body(w_ref, g_ref, m_ref, v_ref, *rest):
        d_ref, m2_ref, v2_ref = rest[-3:]
        d_ref[...], m2_ref[...], v2_ref[...] = _adam_math(w_ref[...], g_ref[...], m_ref[...], v_ref[...])

    blk = pl.BlockSpec((None, tr, cols), lambda b, i: (b, i, 0))
    extra = [] if dep is None else [dep]
    outs = pl.pallas_call(
        body, grid=(lead, rows // tr), name=name,
        in_specs=[blk] * 4 + [pl.BlockSpec(memory_space=pl.ANY)] * len(extra), out_specs=[blk] * 3,
        out_shape=[jax.ShapeDtypeStruct((lead, rows, cols), F32)] * 3,
        compiler_params=_params(("parallel", "parallel")))(*[a.reshape(lead, rows, cols) for a in (w, g, m, v)], *extra)
    return [o.reshape(shp) for o in outs]


def _as2d(a):
    n = a.size
    if n % 1024 == 0:
        return a.reshape(n // 1024, 1024)
    if n % 128 == 0:
        return a.reshape(n // 128, 128)
    return a.reshape(1, n)


def _blocks_to_cols(a):
    b = jnp.moveaxis(a, 0, -2)
    return b.reshape(b.shape[:-2] + (b.shape[-2] * b.shape[-1],))


def _pack_rows(parts):
    padded, offs, r = [], [], 0
    for p in parts:
        pad = (-p.shape[0]) % 8
        padded.append(jnp.pad(p, ((0, pad), (0, 0))) if pad else p)
        offs.append(r)
        r += p.shape[0] + pad
    return jnp.concatenate(padded, axis=0), offs


def _silu(x):
    return x * jax.nn.sigmoid(x)


def kernel(x, c, ctx, c_ctx, w_mod, b_mod, ln_g, ln_b, ffn_w_gate, ffn_w_up, ffn_w_down, mix_ab_w_in, attn_sink, pool_w, pool_scale, mix_ab_w_out, lru_w_in, lru_conv_w, lru_conv_b, lru_wa, lru_ba, lru_wx, lru_bx, lru_lambda, lru_w_out, loss_target, m_c_ctx, m_w_mod, m_b_mod, m_ln_g, m_ln_b, m_ffn_w_gate, m_ffn_w_up, m_ffn_w_down, m_mix_ab_w_in, m_attn_sink, m_pool_w, m_pool_scale, m_mix_ab_w_out, m_lru_w_in, m_lru_conv_w, m_lru_conv_b, m_lru_wa, m_lru_ba, m_lru_wx, m_lru_bx, m_lru_lambda, m_lru_w_out, v_c_ctx, v_w_mod, v_b_mod, v_ln_g, v_ln_b, v_ffn_w_gate, v_ffn_w_up, v_ffn_w_down, v_mix_ab_w_in, v_attn_sink, v_pool_w, v_pool_scale, v_mix_ab_w_out, v_lru_w_in, v_lru_conv_w, v_lru_conv_b, v_lru_wa, v_lru_ba, v_lru_wx, v_lru_bx, v_lru_lambda, v_lru_w_out):
    weights = dict(c_ctx=c_ctx, w_mod=w_mod, b_mod=b_mod, ln_g=ln_g, ln_b=ln_b, ffn_w_gate=ffn_w_gate,
                   ffn_w_up=ffn_w_up, ffn_w_down=ffn_w_down, mix_ab_w_in=mix_ab_w_in, attn_sink=attn_sink,
                   pool_w=pool_w, pool_scale=pool_scale, mix_ab_w_out=mix_ab_w_out, lru_w_in=lru_w_in,
                   lru_conv_w=lru_conv_w, lru_conv_b=lru_conv_b, lru_wa=lru_wa, lru_ba=lru_ba, lru_wx=lru_wx,
                   lru_bx=lru_bx, lru_lambda=lru_lambda, lru_w_out=lru_w_out)
    mom_m = dict(c_ctx=m_c_ctx, w_mod=m_w_mod, b_mod=m_b_mod, ln_g=m_ln_g, ln_b=m_ln_b, ffn_w_gate=m_ffn_w_gate,
                 ffn_w_up=m_ffn_w_up, ffn_w_down=m_ffn_w_down, mix_ab_w_in=m_mix_ab_w_in, attn_sink=m_attn_sink,
                 pool_w=m_pool_w, pool_scale=m_pool_scale, mix_ab_w_out=m_mix_ab_w_out, lru_w_in=m_lru_w_in,
                 lru_conv_w=m_lru_conv_w, lru_conv_b=m_lru_conv_b, lru_wa=m_lru_wa, lru_ba=m_lru_ba, lru_wx=m_lru_wx,
                 lru_bx=m_lru_bx, lru_lambda=m_lru_lambda, lru_w_out=m_lru_w_out)
    mom_v = dict(c_ctx=v_c_ctx, w_mod=v_w_mod, b_mod=v_b_mod, ln_g=v_ln_g, ln_b=v_ln_b, ffn_w_gate=v_ffn_w_gate,
                 ffn_w_up=v_ffn_w_up, ffn_w_down=v_ffn_w_down, mix_ab_w_in=v_mix_ab_w_in, attn_sink=v_attn_sink,
                 pool_w=v_pool_w, pool_scale=v_pool_scale, mix_ab_w_out=v_mix_ab_w_out, lru_w_in=v_lru_w_in,
                 lru_conv_w=v_lru_conv_w, lru_conv_b=v_lru_conv_b, lru_wa=v_lru_wa, lru_ba=v_lru_ba, lru_wx=v_lru_wx,
                 lru_bx=v_lru_bx, lru_lambda=v_lru_lambda, lru_w_out=v_lru_w_out)
    names = list(weights)

    n_lat, n_ctx = x.shape[1], ctx.shape[1]
    cfg = _Cfg(n_lat, n_ctx)
    _, _, _, me = _position()
    mcols = w_mod.shape[2]

    def t_bf16(w):
        return jnp.swapaxes(w, -1, -2).astype(BF16)

    def ffn_src(l, i):
        return jnp.stack([t_bf16(ffn_w_gate[l, i]), t_bf16(ffn_w_up[l, i]), ffn_w_down[l, i].astype(BF16)])

    pending = {}

    def start_gathers(items, tok):
        for key, make_src in items:
            pending[key] = _gather_start(make_src() + tok.astype(BF16), "gather_start_" + key)
            tok = pending[key][4][0, 0]
        return tok

    def weights_now(key, after):
        return _gather_wait(pending[key], after, "gather_wait_" + key)

    first = _gather_start(ffn_src(0, 0), "gather_start_ffn00", SAME_CORE_AND_SIBLING)
    tok = first[4][0, 0]

    small_names = ["ln_g", "ln_b", "lru_conv_w", "lru_conv_b", "lru_ba", "lru_bx", "lru_lambda"]
    small, small_off = _pack_rows([(c + tok).reshape(-1, 128)] + [weights[n].reshape(-1, 128) for n in small_names])
    small_all = _all_gather(small, "gather_small", True)

    def small_full(idx, shp):
        rows = math.prod(shp) // 128
        return _blocks_to_cols(small_all[:, small_off[idx]:small_off[idx] + rows, :].reshape((N_DEV,) + shp))

    c_all = small_all[:, :2 * D // 128, :].reshape(2 * N_DEV, D)
    ln_g_f, ln_b_f = small_full(1, ln_g.shape), small_full(2, ln_b.shape)
    lru_consts = (small_full(3, lru_conv_w.shape)[0], small_full(4, lru_conv_b.shape), lru_wa[0],
                  small_full(5, lru_ba.shape)[0], lru_wx[0], small_full(6, lru_bx.shape)[0],
                  small_full(7, lru_lambda.shape)[0])

    s_rows = jnp.zeros((32, D), F32).at[:16].set(_silu(c_all)).at[16].set(_silu(c_ctx)).astype(BF16)
    mod_mine = jnp.stack([_matmul(s_rows, w_mod[l], "nn", F32, "mod_fwd", bn_cap=1280) for l in range(2)])
    mod_all = _all_gather(mod_mine.reshape(64, mcols), "gather_mod", True).reshape(N_DEV, 2, 32, mcols)
    r_ffn = ffn_w_down.shape[2]
    relay = _relay_start(_gather_wait(first, mod_all, "gather_wait_ffn00", n_peers=len(SAME_CORE_AND_SIBLING)),
                         r_ffn, "gather_relay_start_ffn00")
    tok = start_gathers([("ab_in", lambda: t_bf16(mix_ab_w_in)), ("ab_out", lambda: mix_ab_w_out.astype(BF16)),
                         ("ffn01", lambda: ffn_src(0, 1)), ("ffn10", lambda: ffn_src(1, 0)),
                         ("lru_in", lambda: t_bf16(lru_w_in)), ("lru_out", lambda: lru_w_out.astype(BF16)),
                         ("ffn11", lambda: ffn_src(1, 1))], relay[3][0, 0])
    mod_full = _blocks_to_cols(mod_all) + (b_mod[:, None, :] + tok)
    ex0 = 2 * me
    mods = []
    for l in range(2):
        rows = jnp.stack([lax.dynamic_index_in_dim(mod_full[l], ex0, 0, False),
                          lax.dynamic_index_in_dim(mod_full[l], ex0 + 1, 0, False), mod_full[l, 16]])
        mods.append(rows.reshape(3, N_MOD, D))

    h0 = jnp.concatenate([x.reshape(cfg.t_lat, D), ctx.reshape(cfg.t_ctx, D)], axis=0)
    cos, sin = _rope_tables(n_lat)
    sink_rows = jnp.broadcast_to(attn_sink[0][:, None], (8, 128)).astype(F32)

    saved = []
    wf = [[None, None], [None, None]]
    h = h0
    xin = _modulate(cfg, h0, mods[0], 0, 1, "modulate_in")
    for l in range(2):
        st = {"h_in": h, "xin1": xin}
        wf[l][0] = (_relay_wait(relay, r_ffn, xin, "gather_relay_wait_ffn00") if l == 0
                    else weights_now("ffn10", xin))
        g1, u1, y1 = _ffn_fwd(xin, wf[l][0], "ffn_fwd")
        h1, xhat1, rstd1, xin2 = _ln_fwd(cfg, h, y1, mods[l], 2, 0.5, ln_g_f[l, 0][None], ln_b_f[l, 0][None],
                                          mods[l], (3, 4), "ln_fwd_a")
        st.update(g1=g1, u1=u1, y1=y1, h1=h1, xhat1=xhat1, rstd1=rstd1, xin2=xin2)
        if l == 0:
            w_ab_in_t = weights_now("ab_in", xin2)[0]
            p = _matmul(xin2, w_ab_in_t, "nt", F32, "mix_ab_in")
            att_l, att_c = _attn_fwd(cfg, p, cos, sin, sink_rows, "attn_fwd")
            pool_l = _pool_fwd(p, pool_w[0], pool_scale, n_lat, 0, 2, "pool_fwd_lat")
            pool_c = _pool_fwd(p, pool_w[0], pool_scale, n_ctx, cfg.ctx_blk, 2, "pool_fwd_ctx")
            cat = jnp.concatenate([jnp.concatenate([att_l, pool_l], axis=1),
                                   jnp.concatenate([att_c, pool_c], axis=1)], axis=0)
            w_ab_out = weights_now("ab_out", cat)[0]
            y2 = _matmul(cat, w_ab_out, "nn", BF16, "mix_ab_out")
        else:
            w_lru_in_t = weights_now("lru_in", xin2)[0]
            p = _matmul(xin2, w_lru_in_t, "nt", F32, "lru_in")
            z_l, z_c, st["h_lat"], st["h_ctx"] = _lru_fwd(cfg, p, lru_consts, "lru_fwd")
            cat = jnp.concatenate([z_l, z_c], axis=0)
            w_lru_out = weights_now("lru_out", cat)[0]
            y2 = _matmul(cat, w_lru_out, "nn", BF16, "lru_out")
        h2, xhat2, rstd2, xin3 = _ln_fwd(cfg, h1, y2, mods[l], 5, 1.0, ln_g_f[l, 1][None], ln_b_f[l, 1][None],
                                          mods[l], (6, 7), "ln_fwd_b")
        wf[l][1] = weights_now("ffn%d1" % l, xin3)
        g3, u3, y3 = _ffn_fwd(xin3, wf[l][1], "ffn_fwd")
        if l == 0:
            h3, xhat3, rstd3, xin = _ln_fwd(cfg, h2, y3, mods[l], 8, 0.5, ln_g_f[l, 2][None], ln_b_f[l, 2][None],
                                            mods[1], (0, 1), "ln_fwd_a")
        else:
            h3, xhat3, rstd3 = _ln_fwd(cfg, h2, y3, mods[l], 8, 0.5, ln_g_f[l, 2][None], ln_b_f[l, 2][None],
                                       None, None, "ln_fwd_last")
        st.update(p=p, cat=cat, y2=y2, h2=h2, xhat2=xhat2, rstd2=rstd2, xin3=xin3, g3=g3, u3=u3, y3=y3,
                  xhat3=xhat3, rstd3=rstd3)
        saved.append(st)
        h = h3

    dy, loss_tile = _loss(cfg, h, loss_target.reshape(cfg.t_lat, D), "loss")
    loss = lax.psum(loss_tile[0, 0], ("x", "y", "c"))

    grads = {}
    dmod = [None, None]
    recv_ffn = [[None, None], [None, None]]
    dln_g = [[None] * 3, [None] * 3]
    dln_b = [[None] * 3, [None] * 3]

    def ffn_weight_grads(tag, xin_b, dg, du, a_act, dys):
        parts = [_matmul(dg, xin_b, "tn", BF16, "ffn_dw", bm_cap=1408, bk_cap=2304)[None],
                 _matmul(du, xin_b, "tn", BF16, "ffn_dw", bm_cap=1408, bk_cap=2304)[None],
                 _matmul(a_act, dys, "tn", BF16, "ffn_dw", bm_cap=1408, bk_cap=2304)[None]]
        return [_exchange_start(part, "exchange_start_ffn%s_%d" % (tag, k)) for k, part in enumerate(parts)]

    def pin(handles):
        total = handles[0][4][0, 0]
        for hd in handles[1:]:
            total = total + hd[4][0, 0]
        return total

    up = (dy,)
    dmod_next = None
    last_sent = None
    for l in (1, 0):
        st = saved[l]
        dm = [None] * N_MOD

        def put_stats(stats, gate_idx, nxt):
            dm[gate_idx] = stats[:, 2, :]
            if nxt is not None:
                nxt[0][nxt[1]] = stats[:, 4, :]
                nxt[0][nxt[1] + 1] = stats[:, 3, :]

        lng3 = ln_g_f[l, 2][None] if last_sent is None else ln_g_f[l, 2][None] + pin(last_sent)
        if len(up) > 1:
            up = (up[0], up[1], ln_b_f[l, 2][None], up[3], up[4])
        dres, dys, stats = _ln_bwd(cfg, up, st["xhat3"], st["rstd3"], st["y3"], mods[l], 8, 0.5,
                                   lng3, "ln_bwd_fused" if len(up) > 1 else "ln_bwd_last")
        put_stats(stats, 8, None if len(up) == 1 else (dmod_next, 0))
        dln_g[l][2], dln_b[l][2] = stats[:, 0, :].sum(0), stats[:, 1, :].sum(0)
        dg, du, a_act, dxin = _ffn_bwd(dys, st["g3"], st["u3"], wf[l][1], "ffn_bwd")
        recv_ffn[l][1] = ffn_weight_grads("%d1" % l, st["xin3"], dg, du, a_act, dys)
        dres, dys, stats = _ln_bwd(cfg, (dres, dxin, ln_b_f[l, 1][None], mods[l], 7), st["xhat2"], st["rstd2"], st["y2"],
                                   mods[l], 5, 1.0, ln_g_f[l, 1][None] + pin(recv_ffn[l][1]), "ln_bwd_fused")
        put_stats(stats, 5, (dm, 6))
        dln_g[l][1], dln_b[l][1] = stats[:, 0, :].sum(0), stats[:, 1, :].sum(0)
        if l == 0:
            dw_out = _matmul(st["cat"], dys, "tn", BF16, "mix_ab_dw_out")
            dcat = _matmul(dys, w_ab_out, "nt", BF16, "mix_ab_dcat")
            dq, dk, dv, dqc, dkc, dvc, dsink = _attn_bwd(cfg, st["p"], dcat, cos, sin, sink_rows, "attn_bwd")
            du_l, dpw_l, dps_l = _pool_bwd(st["p"], pool_w[0], pool_scale, dcat, n_lat, 0, 2, "pool_bwd_lat")
            du_c, dpw_c, dps_c = _pool_bwd(st["p"], pool_w[0], pool_scale, dcat, n_ctx, cfg.ctx_blk, 2, "pool_bwd_ctx")
            dp = jnp.concatenate([jnp.concatenate([dq, dk, dv, du_l], axis=1),
                                  jnp.concatenate([dqc, dkc, dvc, du_c], axis=1)], axis=0)
            dw_in_t = _matmul(dp, st["xin2"], "tn", BF16, "mix_ab_dw_in", bm_cap=1280)
            dxin = _matmul(dp, w_ab_in_t, "nn", BF16, "mix_ab_dx")
            recv_mix = [_exchange_start(part, "exchange_start_mix_ab_%d" % k)
                        for k, part in enumerate((dw_in_t[None], dw_out[None], _as2d(dpw_l + dpw_c)[None]))]
            grads["attn_sink"] = (dsink[0, :, 0] + dsink[1, :, 0])[None, :]
            grads["pool_scale"] = dps_l + dps_c
        else:
            dw_out = _matmul(st["cat"], dys, "tn", BF16, "lru_dw_out")
            dz = _matmul(dys, w_lru_out, "nt", BF16, "lru_dz")
            dgl, dul, dgc, duc, dwa, dwx, vec = _lru_bwd(cfg, st["p"], dz, st["h_lat"], st["h_ctx"], lru_consts, "lru_bwd")
            dp = jnp.concatenate([jnp.concatenate([dgl, dul], axis=1), jnp.concatenate([dgc, duc], axis=1)], axis=0)
            dw_in_t = _matmul(dp, st["xin2"], "tn", BF16, "lru_dw_in", bm_cap=1024)
            dxin = _matmul(dp, w_lru_in_t, "nn", BF16, "lru_dx")
            recv_mix = [_exchange_start(part, "exchange_start_lru_%d" % k)
                        for k, part in enumerate((dw_in_t[None], dw_out[None], _as2d(dwa)[None], _as2d(dwx)[None]))]
            vec_t = jnp.moveaxis(vec, 0, 1).reshape(16, D)
            grads["lru_ba"], grads["lru_bx"] = vec_t[0:2], vec_t[2:4]
            grads["lru_lambda"], grads["lru_conv_w"], grads["lru_conv_b"] = vec_t[4:6], vec_t[6:10], vec_t[10:11]
        if l == 0:
            recv_ab = recv_mix
        else:
            recv_lru = recv_mix
        dres, dys, stats = _ln_bwd(cfg, (dres, dxin, ln_b_f[l, 0][None], mods[l], 4), st["xhat1"], st["rstd1"], st["y1"],
                                   mods[l], 2, 0.5, ln_g_f[l, 0][None] + pin(recv_mix), "ln_bwd_fused")
        put_stats(stats, 2, (dm, 3))
        dln_g[l][0], dln_b[l][0] = stats[:, 0, :].sum(0), stats[:, 1, :].sum(0)
        dg, du, a_act, dxin = _ffn_bwd(dys, st["g1"], st["u1"], wf[l][0], "ffn_bwd")
        recv_ffn[l][0] = ffn_weight_grads("%d0" % l, st["xin1"], dg, du, a_act, dys)
        last_sent = recv_ffn[l][0]
        dmod[l] = dm
        dmod_next = dm
        up = (dres, dxin, None, mods[l], 1)
    dh0, stats = _modulate_bwd(cfg, up[0], up[1], h0, mods[0] + pin(last_sent), 1, "modulate_bwd")
    dmod[0][0], dmod[0][1] = stats[:, 4, :], stats[:, 3, :]
    grad_x = dh0.reshape(x.shape)

    dmod_mine = jnp.stack([jnp.stack(dmod[l], axis=1).reshape(3, N_MOD * D) for l in range(2)])
    n_dm = 6 * N_MOD * D // 128
    dmod_sent = _gather_start(dmod_mine.reshape(1, n_dm, 128), "gather_start_dmod")

    def arrived(handle, name):
        return _exchange_wait(handle, dmod_sent[4], name)

    recv_ffn = [[[arrived(hd, "exchange_wait_ffn%d%d_%d" % (l, i, k)) for k, hd in enumerate(recv_ffn[l][i])]
                 for i in range(2)] for l in range(2)]
    recv_ab = [arrived(hd, "exchange_wait_mix_ab_%d" % k) for k, hd in enumerate(recv_ab)]
    recv_lru = [arrived(hd, "exchange_wait_lru_%d" % k) for k, hd in enumerate(recv_lru)]

    def shard_sum(recv, name):
        return _sum_blocks(recv.reshape(N_DEV, recv.shape[2], recv.shape[3]), name)

    gate_g = [[None, None], [None, None]]
    up_g = [[None, None], [None, None]]
    down_g = [[None, None], [None, None]]
    for l in range(2):
        for i in range(2):
            gt, ut, dn = [shard_sum(r, "sum_ffn") for r in recv_ffn[l][i]]
            gate_g[l][i], up_g[l][i], down_g[l][i] = gt.T, ut.T, dn
    grads["ffn_w_gate"] = jnp.stack([jnp.stack(gate_g[l]) for l in range(2)])
    grads["ffn_w_up"] = jnp.stack([jnp.stack(up_g[l]) for l in range(2)])
    grads["ffn_w_down"] = jnp.stack([jnp.stack(down_g[l]) for l in range(2)])
    grads["mix_ab_w_in"] = shard_sum(recv_ab[0], "sum_mix_in").T[None]
    grads["mix_ab_w_out"] = shard_sum(recv_ab[1], "sum_mix_out")[None]
    grads["lru_w_in"] = shard_sum(recv_lru[0], "sum_lru_in").T[None]
    grads["lru_w_out"] = shard_sum(recv_lru[1], "sum_lru_out")[None]
    rep_parts = [shard_sum(recv_lru[2], "sum_rep"), shard_sum(recv_lru[3], "sum_rep"), shard_sum(recv_ab[2], "sum_rep")]
    rep_names = ["lru_wa", "lru_wx", "pool_w"]

    dmod_all = _gather_wait(dmod_sent, rep_parts[2], "gather_wait_dmod").reshape(N_DEV, n_dm, 128)
    dmod_sum = _sum_blocks(dmod_all, "sum_dmod").reshape(2, 3, N_MOD * D)
    dmod_all = dmod_all.reshape(N_DEV, 2, 3, N_MOD * D)
    grads["b_mod"] = dmod_sum[:, 0] + dmod_sum[:, 1] + dmod_sum[:, 2]
    dmod_ex = jnp.moveaxis(dmod_all[:, :, 0:2, :], 1, 0).reshape(2, 2 * N_DEV, N_MOD * D)
    dm_rows = jnp.zeros((2, 32, N_MOD * D), F32).at[:, :16].set(dmod_ex).at[:, 16].set(dmod_sum[:, 2])
    dm_cols = lax.dynamic_slice_in_dim(dm_rows, me * mcols, mcols, axis=2).astype(BF16)
    grads["w_mod"] = jnp.stack([_matmul(s_rows, dm_cols[l], "tn", F32, "mod_dw", bn_cap=1280) for l in range(2)])
    ds_part = None
    for l in range(2):
        part = _matmul(dm_cols[l, 16:32], w_mod[l], "nt", F32, "mod_ds", bk_cap=1280)[0]
        ds_part = part if ds_part is None else ds_part + part

    dln_g_f = jnp.stack([jnp.stack(dln_g[l]) for l in range(2)])
    dln_b_f = jnp.stack([jnp.stack(dln_b[l]) for l in range(2)])
    sink_pad = jnp.zeros((1, 128), F32).at[0, :8].set(grads["attn_sink"][0])
    part_list = [p_.reshape(-1, 128) for p_ in rep_parts] + [
        dln_g_f.reshape(-1, 128), dln_b_f.reshape(-1, 128), grads["lru_conv_w"].reshape(-1, 128),
        grads["lru_conv_b"].reshape(-1, 128), grads["lru_ba"].reshape(-1, 128), grads["lru_bx"].reshape(-1, 128),
        grads["lru_lambda"].reshape(-1, 128), ds_part.reshape(-1, 128), sink_pad, grads["pool_scale"].reshape(-1, 128)]
    parts, part_off = _pack_rows(part_list)
    parts_sent = _gather_start(parts[None], "gather_start_partials")

    delta, new_m, new_v = {}, {}, {}
    for n in ("w_mod", "b_mod", "ffn_w_gate", "ffn_w_up", "ffn_w_down", "mix_ab_w_in", "mix_ab_w_out",
              "lru_w_in", "lru_w_out"):
        grads[n] = grads[n].reshape(weights[n].shape)
        delta[n], new_m[n], new_v[n] = _adamw(weights[n], grads[n], mom_m[n], mom_v[n], "adamw", dep=parts_sent[4])
    parts_all = _gather_wait(parts_sent, delta["lru_w_out"], "gather_wait_partials").reshape(N_DEV, parts.shape[0], 128)
    parts_sum = _sum_blocks(parts_all, "sum_partials")

    for i, n in enumerate(rep_names):
        rows = part_list[i].shape[0]
        grads[n] = parts_all[:, part_off[i]:part_off[i] + rows, :].reshape(weights[n].shape)

    def take(idx):
        return parts_sum[part_off[idx]:part_off[idx] + part_list[idx].shape[0]]

    def my_cols(full, shp):
        w = shp[-1]
        return lax.dynamic_slice_in_dim(full, me * w, w, axis=full.ndim - 1)

    grads["ln_g"] = my_cols(take(3).reshape(2, 3, D), ln_g.shape)
    grads["ln_b"] = my_cols(take(4).reshape(2, 3, D), ln_b.shape)
    grads["lru_conv_w"] = my_cols(take(5).reshape(1, 4, D), lru_conv_w.shape)
    grads["lru_conv_b"] = my_cols(take(6).reshape(1, D), lru_conv_b.shape)
    grads["lru_ba"] = my_cols(take(7).reshape(1, 2, D), lru_ba.shape)
    grads["lru_bx"] = my_cols(take(8).reshape(1, 2, D), lru_bx.shape)
    grads["lru_lambda"] = my_cols(take(9).reshape(1, 2, D), lru_lambda.shape)
    sg = jax.nn.sigmoid(c_ctx)
    grads["c_ctx"] = take(10).reshape(D) * (sg * (1.0 + c_ctx * (1.0 - sg)))
    grads["attn_sink"] = take(11)[:, :8]
    grads["pool_scale"] = take(12).reshape(pool_scale.shape)

    for n in names:
        if n in delta:
            continue
        grads[n] = grads[n].reshape(weights[n].shape)
        delta[n], new_m[n], new_v[n] = _adamw(weights[n], grads[n], mom_m[n], mom_v[n], "adamw")

    return (loss, grad_x, *[grads[n] for n in names], *[delta[n] for n in names],
            *[new_m[n] for n in names], *[new_v[n] for n in names])
```

```python
import functools
import math

import jax
import jax.numpy as jnp
from jax import lax
from jax.experimental import pallas as pl
from jax.experimental.pallas import tpu as pltpu

F32 = jnp.float32
BF16 = jnp.bfloat16
MESH = pl.DeviceIdType.MESH

D = 1024
N_MOD = 9
N_DEV = 8
HEAD_DIM = 64
ATT_HEADS = 8
KV_HEADS = 2
ATT_W = 512
BLK = 128
ATT_SCALE = HEAD_DIM ** -0.5
GRID_W = 64
ROPE_FREQS = HEAD_DIM // 4
ROPE_THETA = 10000.0
POOL_R = (1, 2, 4, 8)
LRU_C = 8.0
LN_EPS = 1e-5
NEG_INF = -1e30
ALPHA = 4.0 ** 0.25
LR, B1, B2, EPS, WD, STEP = 0.001, 0.9, 0.999, 1e-08, 0.01, 10
VMEM_LIMIT = 56 * 1024 * 1024
ROW_TILE = 512


def _params(sem=None):
    if sem is None:
        return pltpu.CompilerParams(vmem_limit_bytes=VMEM_LIMIT)
    return pltpu.CompilerParams(dimension_semantics=sem, vmem_limit_bytes=VMEM_LIMIT)


def _sigmoid(x):
    return 0.5 * jnp.tanh(0.5 * x) + 0.5


def _dot(a, b):
    return jnp.dot(a.astype(BF16), b.astype(BF16), preferred_element_type=F32)


def _dot_nt(a, b):
    return lax.dot_general(a.astype(BF16), b.astype(BF16), (((1,), (1,)), ((), ())), preferred_element_type=F32)


def _dot_tn(a, b):
    return lax.dot_general(a.astype(BF16), b.astype(BF16), (((0,), (0,)), ((), ())), preferred_element_type=F32)


def _pick(n, cap):
    best = None
    for m in range(128, min(n, cap) + 1, 128):
        if n % m == 0:
            best = m
    return n if best is None else best


def _chunks(width, step=256):
    out, c = [], 0
    while c < width:
        w = min(step, width - c)
        out.append((c, w))
        c += w
    return out


class _Cfg:
    def __init__(self, n_lat, n_ctx):
        self.n_lat, self.n_ctx = n_lat, n_ctx
        self.t_lat, self.t_ctx = 2 * n_lat, 2 * n_ctx
        self.T = self.t_lat + self.t_ctx
        self.tm = min(ROW_TILE, self.t_ctx)
        assert n_lat % self.tm == 0 and self.t_ctx % self.tm == 0 and n_lat >= 3 * BLK and n_ctx % BLK == 0
        self.nt = self.T // self.tm
        self.nlt = n_lat // self.tm
        self.ctx_blk = self.t_lat // n_ctx

    def seg(self, i):
        return jnp.minimum(i // self.nlt, 2)

    def first_of_seg(self, i):
        return jnp.where(i < 2 * self.nlt, i % self.nlt == 0, i == 2 * self.nlt)


def _modulate(cfg, h, mod, shift_idx, scale_idx, name):
    tm = cfg.tm

    def body(h_ref, mod_ref, o_ref):
        sh = mod_ref[shift_idx:shift_idx + 1, :]
        sc = mod_ref[scale_idx:scale_idx + 1, :]
        o_ref[...] = (h_ref[...] * (1.0 + sc) + sh).astype(BF16)

    return pl.pallas_call(
        body, grid=(cfg.nt,), name=name,
        in_specs=[pl.BlockSpec((tm, D), lambda i: (i, 0)),
                  pl.BlockSpec((None, N_MOD, D), lambda i: (cfg.seg(i), 0, 0))],
        out_specs=pl.BlockSpec((tm, D), lambda i: (i, 0)),
        out_shape=jax.ShapeDtypeStruct((cfg.T, D), BF16),
        compiler_params=_params(("parallel",)),
    )(h, mod)


def _ln_fwd(cfg, h, y, mod, gate_idx, coef, lng, lnb, mod_next, next_idx, name):
    tm = cfg.tm
    has_next = next_idx is not None

    def body(*refs):
        if has_next:
            h_ref, y_ref, mod_ref, g_ref, b_ref, modn_ref, hn_ref, xhat_ref, rstd_ref, xin_ref = refs
        else:
            h_ref, y_ref, mod_ref, g_ref, b_ref, hn_ref, xhat_ref, rstd_ref = refs
        gate = mod_ref[gate_idx:gate_idx + 1, :]
        z = ALPHA * h_ref[...] + (coef * gate) * y_ref[...].astype(F32)
        mu = jnp.mean(z, axis=-1, keepdims=True)
        zc = z - mu
        var = jnp.mean(zc * zc, axis=-1, keepdims=True)
        rstd = lax.rsqrt(var + LN_EPS)
        xhat = zc * rstd
        hn = xhat * g_ref[...] + b_ref[...]
        hn_ref[...] = hn
        xhat_ref[...] = xhat.astype(BF16)
        rstd_ref[...] = rstd
        if has_next:
            sh = modn_ref[next_idx[0]:next_idx[0] + 1, :]
            sc = modn_ref[next_idx[1]:next_idx[1] + 1, :]
            xin_ref[...] = (hn * (1.0 + sc) + sh).astype(BF16)

    row = pl.BlockSpec((tm, D), lambda i: (i, 0))
    modspec = pl.BlockSpec((None, N_MOD, D), lambda i: (cfg.seg(i), 0, 0))
    vec = pl.BlockSpec((1, D), lambda i: (0, 0))
    in_specs = [row, row, modspec, vec, vec]
    args = [h, y, mod, lng, lnb]
    out_specs = [row, row, pl.BlockSpec((tm, 1), lambda i: (i, 0))]
    out_shape = [jax.ShapeDtypeStruct((cfg.T, D), F32), jax.ShapeDtypeStruct((cfg.T, D), BF16),
                 jax.ShapeDtypeStruct((cfg.T, 1), F32)]
    if has_next:
        in_specs.append(modspec)
        args.append(mod_next)
        out_specs.append(row)
        out_shape.append(jax.ShapeDtypeStruct((cfg.T, D), BF16))
    return pl.pallas_call(body, grid=(cfg.nt,), name=name, in_specs=in_specs, out_specs=out_specs,
                          out_shape=out_shape, compiler_params=_params(("parallel",)))(*args)


def _ln_bwd(cfg, up, xhat, rstd, y, mod, gate_idx, coef, lng, name):
    tm = cfg.tm
    fused = len(up) > 1
    scale_next = up[4] if fused else None

    def body(*refs):
        if fused:
            dres_n, dxin_n, b_ref, modn_ref, xhat_ref, rstd_ref, y_ref, mod_ref, g_ref, dres_ref, dys_ref, st_ref = refs
        else:
            dhn_ref, xhat_ref, rstd_ref, y_ref, mod_ref, g_ref, dres_ref, dys_ref, st_ref = refs
        i = pl.program_id(0)

        @pl.when(cfg.first_of_seg(i))
        def _():
            st_ref[...] = jnp.zeros_like(st_ref)

        xhat = xhat_ref[...].astype(F32)
        if fused:
            dxin = dxin_n[...].astype(F32)
            sc = modn_ref[scale_next:scale_next + 1, :]
            dhn = dres_n[...] + dxin * (1.0 + sc)
            shift_sum = jnp.sum(dxin, axis=0, keepdims=True)
            st_ref[3:4, :] += g_ref[...] * jnp.sum(dxin * xhat, axis=0, keepdims=True) + b_ref[...] * shift_sum
            st_ref[4:5, :] += shift_sum
        else:
            dhn = dhn_ref[...]
        gdh = dhn * g_ref[...]
        m1 = jnp.mean(gdh, axis=-1, keepdims=True)
        m2 = jnp.mean(gdh * xhat, axis=-1, keepdims=True)
        dz = rstd_ref[...] * (gdh - m1 - xhat * m2)
        gate = mod_ref[gate_idx:gate_idx + 1, :]
        dres_ref[...] = ALPHA * dz
        dys_ref[...] = ((coef * gate) * dz).astype(BF16)
        st_ref[0:1, :] += jnp.sum(dhn * xhat, axis=0, keepdims=True)
        st_ref[1:2, :] += jnp.sum(dhn, axis=0, keepdims=True)
        st_ref[2:3, :] += jnp.sum((coef * dz) * y_ref[...].astype(F32), axis=0, keepdims=True)

    row = pl.BlockSpec((tm, D), lambda i: (i, 0))
    modspec = pl.BlockSpec((None, N_MOD, D), lambda i: (cfg.seg(i), 0, 0))
    vec = pl.BlockSpec((1, D), lambda i: (0, 0))
    col = pl.BlockSpec((tm, 1), lambda i: (i, 0))
    if fused:
        in_specs = [row, row, vec, modspec, row, col, row, modspec, vec]
        args = [up[0], up[1], up[2], up[3], xhat, rstd, y, mod, lng]
    else:
        in_specs = [row, row, col, row, modspec, vec]
        args = [up[0], xhat, rstd, y, mod, lng]
    return pl.pallas_call(
        body, grid=(cfg.nt,), name=name, in_specs=in_specs,
        out_specs=[row, row, pl.BlockSpec((None, 8, D), lambda i: (cfg.seg(i), 0, 0))],
        out_shape=[jax.ShapeDtypeStruct((cfg.T, D), F32), jax.ShapeDtypeStruct((cfg.T, D), BF16),
                   jax.ShapeDtypeStruct((3, 8, D), F32)],
        compiler_params=_params(("arbitrary",)))(*args)


def _modulate_bwd(cfg, dres, dxin, h, mod, scale_idx, name):
    tm = cfg.tm
    n_lt = 2 * cfg.nlt

    def body(dres_ref, dxin_ref, h_ref, mod_ref, dh_ref, st_ref):
        i = pl.program_id(0)

        @pl.when(cfg.first_of_seg(i))
        def _():
            st_ref[...] = jnp.zeros_like(st_ref)

        dxin = dxin_ref[...].astype(F32)
        sc = mod_ref[scale_idx:scale_idx + 1, :]

        @pl.when(i < n_lt)
        def _():
            dh_ref[...] = dres_ref[...] + dxin * (1.0 + sc)

        st_ref[3:4, :] += jnp.sum(dxin * h_ref[...], axis=0, keepdims=True)
        st_ref[4:5, :] += jnp.sum(dxin, axis=0, keepdims=True)

    row = pl.BlockSpec((tm, D), lambda i: (i, 0))
    return pl.pallas_call(
        body, grid=(cfg.nt,), name=name,
        in_specs=[row, row, row, pl.BlockSpec((None, N_MOD, D), lambda i: (cfg.seg(i), 0, 0))],
        out_specs=[pl.BlockSpec((tm, D), lambda i: (jnp.minimum(i, n_lt - 1), 0)),
                   pl.BlockSpec((None, 8, D), lambda i: (cfg.seg(i), 0, 0))],
        out_shape=[jax.ShapeDtypeStruct((cfg.t_lat, D), F32), jax.ShapeDtypeStruct((3, 8, D), F32)],
        compiler_params=_params(("arbitrary",)))(dres, dxin, h, mod)


def _loss(cfg, h, target, name):
    tm = cfg.tm
    n_lt = 2 * cfg.nlt

    def body(h_ref, t_ref, dy_ref, l_ref):
        i = pl.program_id(0)

        @pl.when(i == 0)
        def _():
            l_ref[...] = jnp.zeros_like(l_ref)

        @pl.when(i < n_lt)
        def _():
            err = h_ref[...] - t_ref[...]
            dy_ref[...] = err * (1.0 / D)
            part = jnp.sum(jnp.sum(err * err, axis=1, keepdims=True), axis=0, keepdims=True) * (0.5 / D)
            l_ref[...] += jnp.broadcast_to(part, l_ref.shape)

        @pl.when(i >= n_lt)
        def _():
            dy_ref[...] = jnp.zeros_like(dy_ref)

    return pl.pallas_call(
        body, grid=(cfg.nt,), name=name,
        in_specs=[pl.BlockSpec((tm, D), lambda i: (i, 0)),
                  pl.BlockSpec((tm, D), lambda i: (jnp.minimum(i, n_lt - 1), 0))],
        out_specs=[pl.BlockSpec((tm, D), lambda i: (i, 0)), pl.BlockSpec((8, 128), lambda i: (0, 0))],
        out_shape=[jax.ShapeDtypeStruct((cfg.T, D), F32), jax.ShapeDtypeStruct((8, 128), F32)],
        compiler_params=_params(("arbitrary",)))(h, target)


def _matmul(a, b, mode, out_dtype, name, bm_cap=512, bn_cap=1408, bk_cap=1024):
    if mode == "nn":
        (M, K), N = a.shape, b.shape[1]
    elif mode == "nt":
        (M, K), N = a.shape, b.shape[0]
    else:
        (K, M), N = a.shape, b.shape[1]
    bm, bn, bk = _pick(M, bm_cap), _pick(N, bn_cap), _pick(K, bk_cap)
    nk = K // bk

    def body(a_ref, b_ref, o_ref, acc_ref=None):
        k = pl.program_id(2)
        if mode == "nn":
            part = _dot(a_ref[...], b_ref[...])
        elif mode == "nt":
            part = _dot_nt(a_ref[...], b_ref[...])
        else:
            part = _dot_tn(a_ref[...], b_ref[...])
        if nk == 1:
            o_ref[...] = part.astype(out_dtype)
            return

        @pl.when(k == 0)
        def _():
            acc_ref[...] = part

        @pl.when((k > 0) & (k < nk - 1))
        def _():
            acc_ref[...] += part

        @pl.when(k == nk - 1)
        def _():
            o_ref[...] = (acc_ref[...] + part).astype(out_dtype)

    if mode == "nn":
        a_spec = pl.BlockSpec((bm, bk), lambda i, j, k: (i, k))
        b_spec = pl.BlockSpec((bk, bn), lambda i, j, k: (k, j))
    elif mode == "nt":
        a_spec = pl.BlockSpec((bm, bk), lambda i, j, k: (i, k))
        b_spec = pl.BlockSpec((bn, bk), lambda i, j, k: (j, k))
    else:
        a_spec = pl.BlockSpec((bk, bm), lambda i, j, k: (k, i))
        b_spec = pl.BlockSpec((bk, bn), lambda i, j, k: (k, j))
    return pl.pallas_call(
        body, grid=(M // bm, N // bn, nk), name=name, in_specs=[a_spec, b_spec],
        out_specs=pl.BlockSpec((bm, bn), lambda i, j, k: (i, j)),
        out_shape=jax.ShapeDtypeStruct((M, N), out_dtype),
        scratch_shapes=[pltpu.VMEM((bm, bn), F32)] if nk > 1 else [],
        compiler_params=_params(("parallel", "parallel", "arbitrary")))(a, b)


def _ffn_tile(T, cap):
    best = 256
    for t in range(256, cap + 1, 256):
        if T % t == 0:
            best = t
    return best


def _ffn_fwd(xin, wf, name):
    T = xin.shape[0]
    F = wf.shape[1]
    tm, tf = _ffn_tile(T, 768), F // 2
    assert tf % 128 == 0 and T % tm == 0

    def body(x_ref, wg_ref, wu_ref, wd_ref, g_ref, u_ref, y_ref, acc_ref):
        j = pl.program_id(1)
        x = x_ref[...]
        acc = None
        for c0, cw in _chunks(tf):
            g = _dot_nt(x, wg_ref[c0:c0 + cw, :])
            u = _dot_nt(x, wu_ref[c0:c0 + cw, :])
            g_ref[:, c0:c0 + cw] = g.astype(BF16)
            u_ref[:, c0:c0 + cw] = u.astype(BF16)
            part = _dot(g * _sigmoid(g) * u, wd_ref[c0:c0 + cw, :])
            acc = part if acc is None else acc + part

        @pl.when(j == 0)
        def _():
            acc_ref[...] = acc

        @pl.when(j == 1)
        def _():
            y_ref[...] = (acc_ref[...] + acc).astype(BF16)

    return pl.pallas_call(
        body, grid=(T // tm, 2), name=name,
        in_specs=[pl.BlockSpec((tm, D), lambda i, j: (i, 0)),
                  pl.BlockSpec((None, tf, D), lambda i, j: (0, j, 0)),
                  pl.BlockSpec((None, tf, D), lambda i, j: (1, j, 0)),
                  pl.BlockSpec((None, tf, D), lambda i, j: (2, j, 0))],
        out_specs=[pl.BlockSpec((tm, tf), lambda i, j: (i, j)),
                   pl.BlockSpec((tm, tf), lambda i, j: (i, j)),
                   pl.BlockSpec((tm, D), lambda i, j: (i, 0))],
        out_shape=[jax.ShapeDtypeStruct((T, F), BF16), jax.ShapeDtypeStruct((T, F), BF16),
                   jax.ShapeDtypeStruct((T, D), BF16)],
        scratch_shapes=[pltpu.VMEM((tm, D), F32)],
        compiler_params=_params(("parallel", "arbitrary")))(xin, wf, wf, wf)


def _ffn_bwd(dys, g, u, wf, name):
    T = dys.shape[0]
    F = wf.shape[1]
    tm, tf = _ffn_tile(T, 512), F // 2

    def body(dy_ref, g_ref, u_ref, wg_ref, wu_ref, wd_ref, dg_ref, du_ref, a_ref, dx_ref, acc_ref):
        j = pl.program_id(1)
        da_all = _dot_nt(dy_ref[...], wd_ref[...])
        for c0, cw in _chunks(tf):
            gg = g_ref[:, c0:c0 + cw].astype(F32)
            uu = u_ref[:, c0:c0 + cw].astype(F32)
            da = da_all[:, c0:c0 + cw]
            s = _sigmoid(gg)
            silu = gg * s
            a_ref[:, c0:c0 + cw] = (silu * uu).astype(BF16)
            du_ref[:, c0:c0 + cw] = (da * silu).astype(BF16)
            dg_ref[:, c0:c0 + cw] = (da * uu * (s * (1.0 + gg * (1.0 - s)))).astype(BF16)
        acc = _dot(dg_ref[...], wg_ref[...]) + _dot(du_ref[...], wu_ref[...])

        @pl.when(j == 0)
        def _():
            acc_ref[...] = acc

        @pl.when(j == 1)
        def _():
            dx_ref[...] = (acc_ref[...] + acc).astype(BF16)

    blk = pl.BlockSpec((tm, tf), lambda i, j: (i, j))
    return pl.pallas_call(
        body, grid=(T // tm, 2), name=name,
        in_specs=[pl.BlockSpec((tm, D), lambda i, j: (i, 0)), blk, blk,
                  pl.BlockSpec((None, tf, D), lambda i, j: (0, j, 0)),
                  pl.BlockSpec((None, tf, D), lambda i, j: (1, j, 0)),
                  pl.BlockSpec((None, tf, D), lambda i, j: (2, j, 0))],
        out_specs=[blk, blk, blk, pl.BlockSpec((tm, D), lambda i, j: (i, 0))],
        out_shape=[jax.ShapeDtypeStruct((T, F), BF16), jax.ShapeDtypeStruct((T, F), BF16),
                   jax.ShapeDtypeStruct((T, F), BF16), jax.ShapeDtypeStruct((T, D), BF16)],
        scratch_shapes=[pltpu.VMEM((tm, D), F32)],
        compiler_params=_params(("parallel", "arbitrary")))(dys, g, u, wf, wf, wf)


def _swap_halves(x):
    w = x.shape[1]
    lane = lax.broadcasted_iota(jnp.int32, (1, w), 1)
    return jnp.where((lane & 63) < 32, pltpu.roll(x, w - 32, 1), pltpu.roll(x, 32, 1))


def _rope(x, cos, sin):
    return x * cos + _swap_halves(x) * sin


def _rope_t(dy, cos, sin):
    return dy * cos + _swap_halves(dy * sin)


def _rope_tables(n_lat):
    rows = n_lat // GRID_W
    row = jnp.repeat(jnp.arange(rows, dtype=F32), GRID_W)
    col = jnp.tile(jnp.arange(GRID_W, dtype=F32), rows)
    inv = ROPE_THETA ** (-jnp.arange(ROPE_FREQS, dtype=F32) / ROPE_FREQS)
    ang = jnp.concatenate([row[:, None] * inv, col[:, None] * inv], axis=-1)
    cs, sn = jnp.cos(ang), jnp.sin(ang)
    cos = jnp.concatenate([cs, cs, cs, cs], axis=-1)
    sin = jnp.concatenate([-sn, sn, -sn, sn], axis=-1)
    return cos, sin


def _attn_specs(cfg):
    n_lat, n_ctx, cb = cfg.n_lat, cfg.n_ctx, cfg.ctx_blk
    return [pl.BlockSpec((n_lat, ATT_W), lambda e: (e, 0)),
            pl.BlockSpec((n_lat, 128), lambda e: (e, 4)),
            pl.BlockSpec((n_lat, 128), lambda e: (e, 5)),
            pl.BlockSpec((n_ctx, ATT_W), lambda e: (cb + e, 0)),
            pl.BlockSpec((n_ctx, 128), lambda e: (cb + e, 4)),
            pl.BlockSpec((n_ctx, 128), lambda e: (cb + e, 5)),
            pl.BlockSpec((n_lat, 128), lambda e: (0, 0)),
            pl.BlockSpec((n_lat, 128), lambda e: (0, 0)),
            pl.BlockSpec((8, 128), lambda e: (0, 0))]


def _attn_prepare(kh, kl, vl, kc, vc, ka, kb, va, vb, kca, kcb, vca, vcb):
    lane = lax.broadcasted_iota(jnp.int32, (1, 128), 1)
    own = (lane < 64) if kh == 0 else (lane >= 64)

    def split(x, ra, rb):
        mine = jnp.where(own, x, 0.0)
        other = pltpu.roll(mine, 64, 1)
        a, b = (mine, other) if kh == 0 else (other, mine)
        ra[...] = a.astype(BF16)
        rb[...] = b.astype(BF16)

    split(kl, ka, kb)
    split(vl, va, vb)
    split(kc, kca, kcb)
    split(vc, vca, vcb)


def _softmax_parts(s_list, sk):
    m = sk
    for s in s_list:
        m = jnp.maximum(m, jnp.max(s, axis=1, keepdims=True))
    es = [jnp.exp(s - m) for s in s_list]
    esk = jnp.exp(sk - m)
    den = esk
    for e in es:
        den = den + jnp.sum(e, axis=1, keepdims=True)
    inv = 1.0 / den
    return [e * inv for e in es], esk * inv


def _window(cfg, n):
    r0 = pl.multiple_of(n * BLK, BLK)
    start = pl.multiple_of(jnp.clip((n - 1) * BLK, 0, cfg.n_lat - 3 * BLK), BLK)
    qpos = r0 + lax.broadcasted_iota(jnp.int32, (BLK, 1), 0)
    kpos = start + lax.broadcasted_iota(jnp.int32, (1, 3 * BLK), 1)
    valid = jnp.abs(qpos - kpos) <= BLK
    return r0, start, valid


def _attn_fwd(cfg, p, cos, sin, sink_rows, name):
    n_lat, n_ctx = cfg.n_lat, cfg.n_ctx

    def body(q_ref, k_ref, v_ref, qc_ref, kc_ref, vc_ref, cos_ref, sin_ref, sink_ref, o_ref, oc_ref,
             qr, ka, kb, va, vb, kca, kcb, vca, vcb):
        cos_t, sin_t = cos_ref[...], sin_ref[...]
        for gq in range(4):
            qr[:, gq * 128:(gq + 1) * 128] = _rope(q_ref[:, gq * 128:(gq + 1) * 128].astype(F32), cos_t, sin_t).astype(BF16)
        kl = _rope(k_ref[...].astype(F32), cos_t, sin_t)
        for kh in range(KV_HEADS):
            _attn_prepare(kh, kl, v_ref[...].astype(F32), kc_ref[...].astype(F32), vc_ref[...].astype(F32),
                          ka, kb, va, vb, kca, kcb, vca, vcb)

            def lat_block(n, carry):
                r0, start, valid = _window(cfg, n)
                win = pl.ds(start, 3 * BLK)
                lanes = [slice((kh * 2 + pr) * 128, (kh * 2 + pr + 1) * 128) for pr in range(2)]
                qps = [qr[pl.ds(r0, BLK), lanes[pr]] for pr in range(2)]
                kws, kcs = (ka[win, :], kb[win, :]), (kca[...], kcb[...])
                scores = [(jnp.where(valid, _dot_nt(qps[pr], kws[half]) * ATT_SCALE, NEG_INF),
                           _dot_nt(qps[pr], kcs[half]) * ATT_SCALE) for pr in range(2) for half in range(2)]
                probs = []
                for idx, (s_w, s_c) in enumerate(scores):
                    head = kh * 4 + idx
                    (p_w, p_c), _ = _softmax_parts([s_w, s_c], sink_ref[head:head + 1, 0:1])
                    probs.append((p_w.astype(BF16), p_c.astype(BF16)))
                vws, vcs = (va[win, :], vb[win, :]), (vca[...], vcb[...])
                for pr in range(2):
                    o = (_dot(probs[2 * pr][0], vws[0]) + _dot(probs[2 * pr][1], vcs[0])
                         + _dot(probs[2 * pr + 1][0], vws[1]) + _dot(probs[2 * pr + 1][1], vcs[1]))
                    o_ref[pl.ds(r0, BLK), lanes[pr]] = o.astype(BF16)
                return carry

            lax.fori_loop(0, n_lat // BLK, lat_block, 0, unroll=2)
            for n in range(n_ctx // BLK):
                rows = slice(n * BLK, (n + 1) * BLK)
                for pr in range(2):
                    lanes = slice((kh * 2 + pr) * 128, (kh * 2 + pr + 1) * 128)
                    qp = qc_ref[rows, lanes]
                    o = None
                    for half, (kcx, vcx) in enumerate(((kca, vca), (kcb, vcb))):
                        head = kh * 4 + pr * 2 + half
                        s_c = _dot_nt(qp, kcx[...]) * ATT_SCALE
                        (p_c,), _ = _softmax_parts([s_c], sink_ref[head:head + 1, 0:1])
                        part = _dot(p_c, vcx[...])
                        o = part if o is None else o + part
                    oc_ref[rows, lanes] = o.astype(BF16)

    return pl.pallas_call(
        body, grid=(2,), name=name, in_specs=_attn_specs(cfg),
        out_specs=[pl.BlockSpec((n_lat, ATT_W), lambda e: (e, 0)), pl.BlockSpec((n_ctx, ATT_W), lambda e: (e, 0))],
        out_shape=[jax.ShapeDtypeStruct((cfg.t_lat, ATT_W), BF16), jax.ShapeDtypeStruct((cfg.t_ctx, ATT_W), BF16)],
        scratch_shapes=[pltpu.VMEM((n_lat, ATT_W), BF16)] + [pltpu.VMEM((n_lat, 128), BF16)] * 4
        + [pltpu.VMEM((n_ctx, 128), BF16)] * 4,
        compiler_params=_params(("parallel",)))(p, p, p, p, p, p, cos, sin, sink_rows)


def _attn_bwd(cfg, p, dcat, cos, sin, sink_rows, name):
    n_lat, n_ctx, cb = cfg.n_lat, cfg.n_ctx, cfg.ctx_blk

    def body(q_ref, k_ref, v_ref, qc_ref, kc_ref, vc_ref, cos_ref, sin_ref, sink_ref, do_ref, doc_ref,
             dq_ref, dk_ref, dv_ref, dqc_ref, dkc_ref, dvc_ref, dsink_ref,
             qr, ka, kb, va, vb, kca, kcb, vca, vcb, dqs, dka, dva, dkca, dvca):
        cos_t, sin_t = cos_ref[...], sin_ref[...]
        lane = lax.broadcasted_iota(jnp.int32, (1, 128), 1)
        lo = lane < 64
        for gq in range(4):
            qr[:, gq * 128:(gq + 1) * 128] = _rope(q_ref[:, gq * 128:(gq + 1) * 128].astype(F32), cos_t, sin_t).astype(BF16)
        kl = _rope(k_ref[...].astype(F32), cos_t, sin_t)
        dsink_ref[...] = jnp.zeros_like(dsink_ref)
        dka[...] = jnp.zeros_like(dka)
        dva[...] = jnp.zeros_like(dva)
        dkca[...] = jnp.zeros_like(dkca)
        dvca[...] = jnp.zeros_like(dvca)

        def halves(x):
            return jnp.where(lo, x, 0).astype(BF16), jnp.where(lo, 0, x).astype(BF16)

        for kh in range(KV_HEADS):
            _attn_prepare(kh, kl, v_ref[...].astype(F32), kc_ref[...].astype(F32), vc_ref[...].astype(F32),
                          ka, kb, va, vb, kca, kcb, vca, vcb)

            def one_head(head, qp, q_half, do_p, do_half, kw, kcx, vw, vcx, win, valid):
                sk = sink_ref[head:head + 1, 0:1]
                s_list = [_dot_nt(qp, kcx[...]) * ATT_SCALE]
                if win is not None:
                    s_list.insert(0, jnp.where(valid, _dot_nt(qp, kw[win, :]) * ATT_SCALE, NEG_INF))
                probs, p_sink = _softmax_parts(s_list, sk)
                vals = [vcx[...]] if win is None else [vw[win, :], vcx[...]]
                dps = [_dot_nt(do_p, vv) for vv in vals]
                dr = None
                for pp, dp in zip(probs, dps):
                    t = jnp.sum(pp * dp, axis=1, keepdims=True)
                    dr = t if dr is None else dr + t
                dss = [(pp * (dp - dr) * ATT_SCALE).astype(BF16) for pp, dp in zip(probs, dps)]
                dsink_ref[head:head + 1, :] += jnp.broadcast_to(
                    jnp.sum(-p_sink * dr, axis=0, keepdims=True), (1, 128))
                p_c, ds_c = probs[-1], dss[-1]
                dq = _dot(ds_c, kcx[...])
                dkca[kh] += _dot_tn(ds_c, q_half)
                dvca[kh] += _dot_tn(p_c, do_half)
                if win is not None:
                    dq = dq + _dot(dss[0], kw[win, :])
                    dka[kh, win, :] += _dot_tn(dss[0], q_half)
                    dva[kh, win, :] += _dot_tn(probs[0], do_half)
                return dq

            def lat_block(n, carry):
                r0, start, valid = _window(cfg, n)
                win = pl.ds(start, 3 * BLK)
                lanes = [slice((kh * 2 + pr) * 128, (kh * 2 + pr + 1) * 128) for pr in range(2)]
                qps = [qr[pl.ds(r0, BLK), lanes[pr]] for pr in range(2)]
                dops = [do_ref[pl.ds(r0, BLK), lanes[pr]].astype(BF16) for pr in range(2)]
                heads = [(pr, half) for pr in range(2) for half in range(2)]
                kws, kcs = (ka[win, :], kb[win, :]), (kca[...], kcb[...])
                vws, vcs = (va[win, :], vb[win, :]), (vca[...], vcb[...])
                soft = []
                for idx, (pr, half) in enumerate(heads):
                    s_w = jnp.where(valid, _dot_nt(qps[pr], kws[half]) * ATT_SCALE, NEG_INF)
                    s_c = _dot_nt(qps[pr], kcs[half]) * ATT_SCALE
                    soft.append(_softmax_parts([s_w, s_c], sink_ref[kh * 4 + idx:kh * 4 + idx + 1, 0:1]))
                dps = [(_dot_nt(dops[pr], vws[half]), _dot_nt(dops[pr], vcs[half])) for pr, half in heads]
                ds_w, ds_c, pb_w, pb_c = [], [], [], []
                for idx in range(4):
                    (p_w, p_c), p_sink = soft[idx]
                    dp_w, dp_c = dps[idx]
                    dr = jnp.sum(p_w * dp_w, axis=1, keepdims=True) + jnp.sum(p_c * dp_c, axis=1, keepdims=True)
                    ds_w.append((p_w * (dp_w - dr) * ATT_SCALE).astype(BF16))
                    ds_c.append((p_c * (dp_c - dr) * ATT_SCALE).astype(BF16))
                    pb_w.append(p_w.astype(BF16))
                    pb_c.append(p_c.astype(BF16))
                    head = kh * 4 + idx
                    dsink_ref[head:head + 1, :] += jnp.broadcast_to(
                        jnp.sum(-p_sink * dr, axis=0, keepdims=True), (1, 128))
                for pr in range(2):
                    dqs[pl.ds(r0, BLK), lanes[pr]] = (
                        _dot(ds_w[2 * pr], kws[0]) + _dot(ds_c[2 * pr], kcs[0])
                        + _dot(ds_w[2 * pr + 1], kws[1]) + _dot(ds_c[2 * pr + 1], kcs[1]))
                q_hs, do_hs = [halves(qp) for qp in qps], [halves(do_p) for do_p in dops]
                q_stack = jnp.concatenate([q_hs[pr][half] for pr, half in heads], axis=0)
                do_stack = jnp.concatenate([do_hs[pr][half] for pr, half in heads], axis=0)
                dka[kh, win, :] += _dot_tn(jnp.concatenate(ds_w, axis=0), q_stack)
                dva[kh, win, :] += _dot_tn(jnp.concatenate(pb_w, axis=0), do_stack)
                dkca[kh] += _dot_tn(jnp.concatenate(ds_c, axis=0), q_stack)
                dvca[kh] += _dot_tn(jnp.concatenate(pb_c, axis=0), do_stack)
                return carry

            lax.fori_loop(0, n_lat // BLK, lat_block, 0, unroll=2)
            for n in range(n_ctx // BLK):
                rows = slice(n * BLK, (n + 1) * BLK)
                for pr in range(2):
                    lanes = slice((kh * 2 + pr) * 128, (kh * 2 + pr + 1) * 128)
                    qp = qc_ref[rows, lanes].astype(BF16)
                    do_p = doc_ref[rows, lanes]
                    q_h, do_h = halves(qp), halves(do_p)
                    dq = None
                    for half, (kcx, vcx) in enumerate(((kca, vca), (kcb, vcb))):
                        part = one_head(kh * 4 + pr * 2 + half, qp, q_h[half], do_p, do_h[half],
                                        None, kcx, None, vcx, None, None)
                        dq = part if dq is None else dq + part
                    dqc_ref[rows, lanes] = dq.astype(BF16)

        def fold(acc):
            r0 = acc[0] + pltpu.roll(acc[0], 64, 1)
            r1 = acc[1] + pltpu.roll(acc[1], 64, 1)
            return jnp.where(lo, r0, r1)

        for gq in range(4):
            sl = slice(gq * 128, (gq + 1) * 128)
            dq_ref[:, sl] = _rope_t(dqs[:, sl], cos_t, sin_t).astype(BF16)
        dk_ref[...] = _rope_t(fold(dka), cos_t, sin_t).astype(BF16)
        dv_ref[...] = fold(dva).astype(BF16)
        dkc_ref[...] = fold(dkca).astype(BF16)
        dvc_ref[...] = fold(dvca).astype(BF16)

    lat = lambda w: pl.BlockSpec((n_lat, w), lambda e: (e, 0))
    ctx = lambda w: pl.BlockSpec((n_ctx, w), lambda e: (e, 0))
    sd = jax.ShapeDtypeStruct
    return pl.pallas_call(
        body, grid=(2,), name=name,
        in_specs=_attn_specs(cfg) + [pl.BlockSpec((n_lat, ATT_W), lambda e: (e, 0)),
                                     pl.BlockSpec((n_ctx, ATT_W), lambda e: (cb + e, 0))],
        out_specs=[lat(ATT_W), lat(128), lat(128), ctx(ATT_W), ctx(128), ctx(128),
                   pl.BlockSpec((None, 8, 128), lambda e: (e, 0, 0))],
        out_shape=[sd((cfg.t_lat, ATT_W), BF16), sd((cfg.t_lat, 128), BF16), sd((cfg.t_lat, 128), BF16),
                   sd((cfg.t_ctx, ATT_W), BF16), sd((cfg.t_ctx, 128), BF16), sd((cfg.t_ctx, 128), BF16),
                   sd((2, 8, 128), F32)],
        scratch_shapes=[pltpu.VMEM((n_lat, ATT_W), BF16)] + [pltpu.VMEM((n_lat, 128), BF16)] * 4
        + [pltpu.VMEM((n_ctx, 128), BF16)] * 4
        + [pltpu.VMEM((n_lat, ATT_W), F32), pltpu.VMEM((2, n_lat, 128), F32), pltpu.VMEM((2, n_lat, 128), F32),
           pltpu.VMEM((2, n_ctx, 128), F32), pltpu.VMEM((2, n_ctx, 128), F32)],
        compiler_params=_params(("parallel",)))(p, p, p, p, p, p, cos, sin, sink_rows, dcat, dcat)


def _shift_down(x, k, row):
    return jnp.where(row >= k, pltpu.roll(x, k, 0), 0.0)


def _shift_up(x, k, row):
    n = x.shape[0]
    return jnp.where(row < n - k, pltpu.roll(x, n - k, 0), 0.0)


def _window_sum(x, r, row):
    below, above, k = x, x, 1
    while k < r:
        below = below + _shift_down(below, k, row)
        above = above + _shift_up(above, k, row)
        k *= 2
    return below + _shift_down(x, r, row) + _shift_up(above, 1, row)


def _inv_count(r, row, n):
    cnt = jnp.minimum(row + r, n - 1) + 1 - jnp.maximum(row - r, 0)
    return 1.0 / cnt.astype(F32)


def _pool_fwd(p, w, scale, n, blk0, n_seg, name):
    def body(u0, u1, u2, u3, w_ref, sc_ref, o_ref):
        row = lax.broadcasted_iota(jnp.int32, (n, 1), 0)
        for g, u_ref in enumerate((u0, u1, u2, u3)):
            u = u_ref[...].astype(F32)
            d = _window_sum(u, POOL_R[g], row) * _inv_count(POOL_R[g], row, n) - u
            o_ref[:, g * 128:(g + 1) * 128] = (_dot(d, w_ref[g]) * sc_ref[:, g * 128:(g + 1) * 128]).astype(BF16)

    return pl.pallas_call(
        body, grid=(n_seg,), name=name,
        in_specs=[pl.BlockSpec((n, 128), functools.partial(lambda g, e: (blk0 + e, 6 + g), g)) for g in range(4)]
        + [pl.BlockSpec((4, 128, 128), lambda e: (0, 0, 0)), pl.BlockSpec((1, 512), lambda e: (0, 0))],
        out_specs=pl.BlockSpec((n, 512), lambda e: (e, 0)),
        out_shape=jax.ShapeDtypeStruct((n_seg * n, 512), BF16),
        compiler_params=_params(("parallel",)))(p, p, p, p, w, scale)


def _pool_bwd(p, w, scale, dcat, n, blk0, n_seg, name):
    def body(u0, u1, u2, u3, w_ref, sc_ref, dp_ref, du_ref, dw_ref, dsc_ref):
        e = pl.program_id(0)

        @pl.when(e == 0)
        def _():
            dw_ref[...] = jnp.zeros_like(dw_ref)
            dsc_ref[...] = jnp.zeros_like(dsc_ref)

        row = lax.broadcasted_iota(jnp.int32, (n, 1), 0)
        for g, u_ref in enumerate((u0, u1, u2, u3)):
            sl = slice(g * 128, (g + 1) * 128)
            u = u_ref[...].astype(F32)
            inv = _inv_count(POOL_R[g], row, n)
            d = _window_sum(u, POOL_R[g], row) * inv - u
            dp = dp_ref[:, sl].astype(F32)
            dsc_ref[:, sl] += jnp.sum(dp * _dot(d, w_ref[g]), axis=0, keepdims=True)
            dyp = dp * sc_ref[:, sl]
            dw_ref[g] += _dot_tn(d, dyp)
            dd = _dot_nt(dyp, w_ref[g])
            du_ref[:, sl] = (_window_sum(dd * inv, POOL_R[g], row) - dd).astype(BF16)

    return pl.pallas_call(
        body, grid=(n_seg,), name=name,
        in_specs=[pl.BlockSpec((n, 128), functools.partial(lambda g, e: (blk0 + e, 6 + g), g)) for g in range(4)]
        + [pl.BlockSpec((4, 128, 128), lambda e: (0, 0, 0)), pl.BlockSpec((1, 512), lambda e: (0, 0)),
           pl.BlockSpec((n, 512), lambda e: (blk0 + e, 1))],
        out_specs=[pl.BlockSpec((n, 512), lambda e: (e, 0)),
                   pl.BlockSpec((4, 128, 128), lambda e: (0, 0, 0)), pl.BlockSpec((1, 512), lambda e: (0, 0))],
        out_shape=[jax.ShapeDtypeStruct((n_seg * n, 512), BF16), jax.ShapeDtypeStruct((4, 128, 128), F32),
                   jax.ShapeDtypeStruct((1, 512), F32)],
        compiler_params=_params(("arbitrary",)))(p, p, p, p, w, scale, dcat)


def _gelu(x):
    t = jnp.tanh(math.sqrt(2.0 / math.pi) * (x + 0.044715 * x * x * x))
    return 0.5 * x * (1.0 + t), t


def _gelu_grad(x, t):
    return 0.5 * (1.0 + t) + 0.5 * x * (1.0 - t * t) * (math.sqrt(2.0 / math.pi) * (1.0 + 3 * 0.044715 * x * x))


def _neg_expm1_twice(x):
    t = jnp.tanh(x)
    return (-2.0 * t) / (1.0 - t)


def _softplus_neg(lam):
    x = -lam
    e = jnp.exp(-jnp.abs(x))
    log1p = jnp.where(e < 1e-2, e * (1.0 - e * (0.5 - e * (1.0 / 3.0))), jnp.log(1.0 + e))
    return jnp.maximum(x, 0.0) + log1p, -_sigmoid(x)


def _conv(u, w_ref, b_ref, row):
    return (b_ref[...] + _shift_down(u, 1, row) * w_ref[0:1, :] + u * w_ref[1:2, :]
            + _shift_up(u, 1, row) * w_ref[2:3, :] + _shift_up(u, 2, row) * w_ref[3:4, :])


def _lru_gates(uc, d, wa_ref, ba_ref, wx_ref, bx_ref, lam_ref):
    r = _sigmoid(_dot(uc, wa_ref[d]) + ba_ref[d:d + 1, :])
    gi = _sigmoid(_dot(uc, wx_ref[d]) + bx_ref[d:d + 1, :])
    sp, dsp = _softplus_neg(lam_ref[d:d + 1, :])
    la = (-LRU_C) * r * sp
    a = jnp.exp(la)
    sq = jnp.sqrt(_neg_expm1_twice(la))
    return r, gi, sp, dsp, a, sq


def _tile_scan(a_ref, b_ref, n, reverse):
    m = n // 8
    first = 7 if reverse else 0
    a_prev = a_ref[pl.ds(first, m, stride=8), :]
    b_prev = b_ref[pl.ds(first, m, stride=8), :]
    for j in (range(6, -1, -1) if reverse else range(1, 8)):
        rows = pl.ds(j, m, stride=8)
        aj = a_ref[rows, :]
        b_prev = aj * b_prev + b_ref[rows, :]
        a_prev = aj * a_prev
        b_ref[rows, :] = b_prev
        a_ref[rows, :] = a_prev


def _carry_scan(a_ref, b_ref, n, reverse, carry):
    nt8 = n // 8

    def step(i, c):
        t = (nt8 - 1 - i) if reverse else i
        off = pl.multiple_of(t * 8, 8)
        h = a_ref[pl.ds(off, 8), :] * c + b_ref[pl.ds(off, 8), :]
        b_ref[pl.ds(off, 8), :] = h
        return h[0:1, :] if reverse else h[7:8, :]

    return lax.fori_loop(0, nt8, step, carry, unroll=4)


def _chain_scan(segs, reverse):
    carry = jnp.zeros((1, 128), F32)
    for a, b, a_ref, b_ref, n in segs:
        a_ref[...] = a
        b_ref[...] = b
        _tile_scan(a_ref, b_ref, n, reverse)
        carry = _carry_scan(a_ref, b_ref, n, reverse, carry)


def _lru_specs(cfg):
    n_lat, n_ctx, cb = cfg.n_lat, cfg.n_ctx, cfg.ctx_blk
    return [pl.BlockSpec((n_lat, 128), lambda hb, e: (e, hb)),
            pl.BlockSpec((n_lat, 128), lambda hb, e: (e, 8 + hb)),
            pl.BlockSpec((n_ctx, 128), lambda hb, e: (cb + e, hb)),
            pl.BlockSpec((n_ctx, 128), lambda hb, e: (cb + e, 8 + hb)),
            pl.BlockSpec((4, 128), lambda hb, e: (0, hb)),
            pl.BlockSpec((1, 128), lambda hb, e: (0, hb)),
            pl.BlockSpec((2, None, 128, 128), lambda hb, e: (0, hb, 0, 0)),
            pl.BlockSpec((2, 128), lambda hb, e: (0, hb)),
            pl.BlockSpec((2, None, 128, 128), lambda hb, e: (0, hb, 0, 0)),
            pl.BlockSpec((2, 128), lambda hb, e: (0, hb)),
            pl.BlockSpec((2, 128), lambda hb, e: (0, hb))]


def _lru_fwd(cfg, p, consts, name):
    n_lat, n_ctx = cfg.n_lat, cfg.n_ctx

    def body(gl_ref, ul_ref, gc_ref, uc_ref, cw_ref, cb_ref, wa_ref, ba_ref, wx_ref, bx_ref, lam_ref,
             zl_ref, zc_ref, hl_ref, hc_ref, al, ac):
        row_l = lax.broadcasted_iota(jnp.int32, (n_lat, 1), 0)
        row_c = lax.broadcasted_iota(jnp.int32, (n_ctx, 1), 0)
        uc_l = _conv(ul_ref[...].astype(F32), cw_ref, cb_ref, row_l)
        uc_c = _conv(uc_ref[...].astype(F32), cw_ref, cb_ref, row_c)
        for d in range(2):
            _, gi_l, _, _, a_l, sq_l = _lru_gates(uc_l, d, wa_ref, ba_ref, wx_ref, bx_ref, lam_ref)
            _, gi_c, _, _, a_c, sq_c = _lru_gates(uc_c, d, wa_ref, ba_ref, wx_ref, bx_ref, lam_ref)
            _chain_scan([(a_c, sq_c * (gi_c * uc_c), ac, hc_ref.at[d], n_ctx),
                         (a_l, sq_l * (gi_l * uc_l), al, hl_ref.at[d], n_lat)], reverse=(d == 1))
        zl_ref[...] = (_gelu(gl_ref[...].astype(F32))[0] * (hl_ref[0] + hl_ref[1])).astype(BF16)
        zc_ref[...] = (_gelu(gc_ref[...].astype(F32))[0] * (hc_ref[0] + hc_ref[1])).astype(BF16)

    return pl.pallas_call(
        body, grid=(8, 2), name=name, in_specs=_lru_specs(cfg),
        out_specs=[pl.BlockSpec((n_lat, 128), lambda hb, e: (e, hb)), pl.BlockSpec((n_ctx, 128), lambda hb, e: (e, hb)),
                   pl.BlockSpec((2, n_lat, 128), lambda hb, e: (0, e, hb)),
                   pl.BlockSpec((2, n_ctx, 128), lambda hb, e: (0, e, hb))],
        out_shape=[jax.ShapeDtypeStruct((cfg.t_lat, D), BF16), jax.ShapeDtypeStruct((cfg.t_ctx, D), BF16),
                   jax.ShapeDtypeStruct((2, cfg.t_lat, D), F32), jax.ShapeDtypeStruct((2, cfg.t_ctx, D), F32)],
        scratch_shapes=[pltpu.VMEM((n_lat, 128), F32), pltpu.VMEM((n_ctx, 128), F32)],
        compiler_params=_params(("parallel", "arbitrary")))(p, p, p, p, *consts)


def _lru_bwd(cfg, p, dz, h_lat, h_ctx, consts, name):
    n_lat, n_ctx, cb = cfg.n_lat, cfg.n_ctx, cfg.ctx_blk

    def body(gl_ref, ul_ref, gc_ref, uc_ref, cw_ref, cb_ref, wa_ref, ba_ref, wx_ref, bx_ref, lam_ref,
             dzl_ref, dzc_ref, hl, hc, dgl_ref, dul_ref, dgc_ref, duc_ref, dwa_ref, dwx_ref, vec_ref,
             al, bl, ac, bc):
        e = pl.program_id(1)

        @pl.when(e == 0)
        def _():
            dwa_ref[...] = jnp.zeros_like(dwa_ref)
            dwx_ref[...] = jnp.zeros_like(dwx_ref)
            vec_ref[...] = jnp.zeros_like(vec_ref)

        row_l = lax.broadcasted_iota(jnp.int32, (n_lat, 1), 0)
        row_c = lax.broadcasted_iota(jnp.int32, (n_ctx, 1), 0)
        u_l, u_c = ul_ref[...].astype(F32), uc_ref[...].astype(F32)
        uc_l = _conv(u_l, cw_ref, cb_ref, row_l)
        uc_c = _conv(u_c, cw_ref, cb_ref, row_c)
        gel_l, t_l = _gelu(gl_ref[...].astype(F32))
        gel_c, t_c = _gelu(gc_ref[...].astype(F32))
        dz_l, dz_c = dzl_ref[...].astype(F32), dzc_ref[...].astype(F32)
        dgl_ref[...] = (dz_l * (hl[0] + hl[1]) * _gelu_grad(gl_ref[...].astype(F32), t_l)).astype(BF16)
        dgc_ref[...] = (dz_c * (hc[0] + hc[1]) * _gelu_grad(gc_ref[...].astype(F32), t_c)).astype(BF16)
        dy_l, dy_c = dz_l * gel_l, dz_c * gel_c
        duc_l = jnp.zeros((n_lat, 128), F32)
        duc_c = jnp.zeros((n_ctx, 128), F32)
        for d in range(2):
            r_l, gi_l, sp, dsp, a_l, sq_l = _lru_gates(uc_l, d, wa_ref, ba_ref, wx_ref, bx_ref, lam_ref)
            r_c, gi_c, _, _, a_c, sq_c = _lru_gates(uc_c, d, wa_ref, ba_ref, wx_ref, bx_ref, lam_ref)
            if d == 0:
                an_l = _shift_up(a_l, 1, row_l)
                an_c = jnp.where(row_c < n_ctx - 1, pltpu.roll(a_c, n_ctx - 1, 0), a_l[0:1, :])
            else:
                an_l = _shift_down(a_l, 1, row_l)
                an_c = jnp.where(row_c >= 1, pltpu.roll(a_c, 1, 0), a_l[n_lat - 1:n_lat, :])
            _chain_scan([(an_l, dy_l, al, bl, n_lat), (an_c, dy_c, ac, bc, n_ctx)], reverse=(d == 0))
            dsp_sum = jnp.zeros((1, 128), F32)
            for (dh, h, r, gi, a, sq, uc, seg) in ((bl[...], hl[d], r_l, gi_l, a_l, sq_l, uc_l, "l"),
                                                  (bc[...], hc[d], r_c, gi_c, a_c, sq_c, uc_c, "c")):
                b0 = sq * (gi * uc)
                t1 = dh * sq
                dla = dh * (h - b0) - (dh * gi * uc) * (a * a) / sq
                dzr = (dla * ((-LRU_C) * sp)) * r * (1.0 - r)
                dzi = (t1 * uc) * gi * (1.0 - gi)
                dsp_sum = dsp_sum + jnp.sum(dla * ((-LRU_C) * r), axis=0, keepdims=True)
                dwa_ref[d] += _dot_tn(uc, dzr)
                dwx_ref[d] += _dot_tn(uc, dzi)
                vec_ref[d:d + 1, :] += jnp.sum(dzr, axis=0, keepdims=True)
                vec_ref[2 + d:3 + d, :] += jnp.sum(dzi, axis=0, keepdims=True)
                duc = t1 * gi + _dot_nt(dzr, wa_ref[d]) + _dot_nt(dzi, wx_ref[d])
                if seg == "l":
                    duc_l = duc_l + duc
                else:
                    duc_c = duc_c + duc
            vec_ref[4 + d:5 + d, :] += dsp_sum * dsp
        for duc, u, row, du_ref in ((duc_l, u_l, row_l, dul_ref), (duc_c, u_c, row_c, duc_ref)):
            du_ref[...] = (_shift_up(duc, 1, row) * cw_ref[0:1, :] + duc * cw_ref[1:2, :]
                           + _shift_down(duc, 1, row) * cw_ref[2:3, :]
                           + _shift_down(duc, 2, row) * cw_ref[3:4, :]).astype(BF16)
            vec_ref[6:7, :] += jnp.sum(duc * _shift_down(u, 1, row), axis=0, keepdims=True)
            vec_ref[7:8, :] += jnp.sum(duc * u, axis=0, keepdims=True)
            vec_ref[8:9, :] += jnp.sum(duc * _shift_up(u, 1, row), axis=0, keepdims=True)
            vec_ref[9:10, :] += jnp.sum(duc * _shift_up(u, 2, row), axis=0, keepdims=True)
            vec_ref[10:11, :] += jnp.sum(duc, axis=0, keepdims=True)

    lat = pl.BlockSpec((n_lat, 128), lambda hb, e: (e, hb))
    ctx = pl.BlockSpec((n_ctx, 128), lambda hb, e: (e, hb))
    wspec = pl.BlockSpec((2, None, 128, 128), lambda hb, e: (0, hb, 0, 0))
    sd = jax.ShapeDtypeStruct
    return pl.pallas_call(
        body, grid=(8, 2), name=name,
        in_specs=_lru_specs(cfg) + [pl.BlockSpec((n_lat, 128), lambda hb, e: (e, hb)),
                                    pl.BlockSpec((n_ctx, 128), lambda hb, e: (cb + e, hb)),
                                    pl.BlockSpec((2, n_lat, 128), lambda hb, e: (0, e, hb)),
                                    pl.BlockSpec((2, n_ctx, 128), lambda hb, e: (0, e, hb))],
        out_specs=[lat, lat, ctx, ctx, wspec, wspec, pl.BlockSpec((None, 16, 128), lambda hb, e: (hb, 0, 0))],
        out_shape=[sd((cfg.t_lat, D), BF16), sd((cfg.t_lat, D), BF16), sd((cfg.t_ctx, D), BF16), sd((cfg.t_ctx, D), BF16),
                   sd((2, 8, 128, 128), F32), sd((2, 8, 128, 128), F32), sd((8, 16, 128), F32)],
        scratch_shapes=[pltpu.VMEM((n_lat, 128), F32)] * 2 + [pltpu.VMEM((n_ctx, 128), F32)] * 2,
        compiler_params=_params(("parallel", "arbitrary")))(p, p, p, p, *consts, dz, dz, h_lat, h_ctx)


def _position():
    x, y, c = lax.axis_index("x"), lax.axis_index("y"), lax.axis_index("c")
    return x, y, c, 4 * x + 2 * y + c


def _peer(x, y, c, k):
    px = 1 - x if k & 4 else x
    py = 1 - y if k & 2 else y
    pc = 1 - c if k & 1 else c
    return (px, py, pc), 4 * px + 2 * py + pc


def _all_gather(v, name, in_vmem):
    def body(v_ref, o_ref, send_sems, recv_sems, local_sem):
        x, y, c, me = _position()
        mine = pltpu.make_async_copy(v_ref, o_ref.at[me], local_sem)
        mine.start()
        sends = []
        for k in range(1, N_DEV):
            peer, _ = _peer(x, y, c, k)
            cp = pltpu.make_async_remote_copy(src_ref=v_ref, dst_ref=o_ref.at[me], send_sem=send_sems.at[k - 1],
                                              recv_sem=recv_sems.at[k - 1], device_id=peer, device_id_type=MESH)
            cp.start()
            sends.append(cp)
        for k in range(1, N_DEV):
            peer, peer_lin = _peer(x, y, c, k)
            pltpu.make_async_remote_copy(src_ref=v_ref, dst_ref=o_ref.at[peer_lin], send_sem=send_sems.at[k - 1],
                                         recv_sem=recv_sems.at[k - 1], device_id=peer, device_id_type=MESH).wait_recv()
        for cp in sends:
            cp.wait_send()
        mine.wait()

    space = pltpu.VMEM if in_vmem else pl.ANY
    return pl.pallas_call(
        body, name=name,
        in_specs=[pl.BlockSpec(memory_space=space)], out_specs=pl.BlockSpec(memory_space=space),
        out_shape=jax.ShapeDtypeStruct((N_DEV,) + v.shape, v.dtype),
        scratch_shapes=[pltpu.SemaphoreType.DMA((N_DEV - 1,)), pltpu.SemaphoreType.DMA((N_DEV - 1,)),
                        pltpu.SemaphoreType.DMA],
        compiler_params=pltpu.CompilerParams(vmem_limit_bytes=VMEM_LIMIT))(v)


_HBM = pl.BlockSpec(memory_space=pltpu.HBM)
_SEM = pl.BlockSpec(memory_space=pltpu.SEMAPHORE)
_EFFECT = pltpu.SideEffectType.DATAFLOW_SIDE_EFFECTING


ALL_PEERS = tuple(range(1, N_DEV))
SAME_CORE_AND_SIBLING = (1, 2, 4, 6)


def _push_start(src, land, block_of, name, relations=ALL_PEERS):
    def body(src_ref, land_ref, send_sem, recv_sem, src_thru, land_thru, token):
        x, y, c, me = _position()
        for k in relations:
            peer, peer_lin = _peer(x, y, c, k)
            mine, there = block_of(src_ref, land_ref, me, peer_lin)
            pltpu.make_async_remote_copy(src_ref=mine, dst_ref=there, send_sem=send_sem, recv_sem=recv_sem,
                                         device_id=peer, device_id_type=MESH).start()
        mine, here = block_of(src_ref, land_ref, me, me)
        pltpu.make_async_copy(mine, here, recv_sem).start()
        token[...] = jnp.zeros_like(token)

    return pl.pallas_call(
        body, name=name,
        out_shape=(pltpu.SemaphoreType.DMA(()), pltpu.SemaphoreType.DMA(()), pltpu.HBM(src.shape, src.dtype),
                   pltpu.HBM(land.shape, land.dtype), jax.ShapeDtypeStruct((8, 128), F32)),
        in_specs=(_HBM, _HBM), out_specs=(_SEM, _SEM, _HBM, _HBM, pl.BlockSpec(memory_space=pltpu.VMEM)),
        input_output_aliases={0: 2, 1: 3},
        compiler_params=pltpu.CompilerParams(has_side_effects=_EFFECT),
    )(pltpu.with_memory_space_constraint(src, pltpu.HBM), pltpu.with_memory_space_constraint(land, pltpu.HBM))


def _push_wait(handle, blocks_of, after, name, n_peers=N_DEV - 1):
    send_sem, recv_sem, src_thru, land_thru, _ = handle

    def body(src_ref, land_ref, send_sem, recv_sem, after_ref, src_dead, got_ref):
        x, y, c, _ = _position()
        sent, landed = blocks_of(land_ref, n_peers), blocks_of(land_ref, n_peers + 1)
        pltpu.make_async_remote_copy(src_ref=sent, dst_ref=sent, send_sem=send_sem, recv_sem=recv_sem,
                                     device_id=(x, y, 1 - c), device_id_type=MESH).wait_send()
        pltpu.make_async_remote_copy(src_ref=landed, dst_ref=landed, send_sem=send_sem, recv_sem=recv_sem,
                                     device_id=(x, y, 1 - c), device_id_type=MESH).wait_recv()

    return pl.pallas_call(
        body, name=name,
        out_shape=(pltpu.HBM(src_thru.shape, src_thru.dtype), pltpu.HBM(land_thru.shape, land_thru.dtype)),
        in_specs=(_HBM, _HBM, _SEM, _SEM, pl.BlockSpec(memory_space=pl.ANY)), out_specs=(_HBM, _HBM),
        input_output_aliases={0: 0, 1: 1},
        compiler_params=pltpu.CompilerParams(has_side_effects=_EFFECT),
    )(src_thru, land_thru, send_sem, recv_sem, after)[1]


def _gather_start(src, name, relations=ALL_PEERS):
    g, r, C = src.shape
    land = lax.empty((g, N_DEV * r, C), src.dtype)
    return _push_start(src, land, lambda s, z, i, p: (s, z.at[:, pl.ds(i * r, r), :]), name, relations)


def _gather_wait(handle, after, name, n_peers=N_DEV - 1):
    r = handle[2].shape[1]
    return _push_wait(handle, lambda z, n: z.at[:, pl.ds(0, n * r), :], after, name, n_peers)


def _relay_start(land, r, name):
    def body(land_ref, send_sem, recv_sem, land_thru, token):
        x, y, c, _ = _position()
        for k in (2, 4, 6):
            _, origin = _peer(x, y, c, k)
            rows = land_ref.at[:, pl.ds(origin * r, r), :]
            pltpu.make_async_remote_copy(src_ref=rows, dst_ref=rows, send_sem=send_sem, recv_sem=recv_sem,
                                         device_id=(x, y, 1 - c), device_id_type=MESH).start()
        token[...] = jnp.zeros_like(token)

    return pl.pallas_call(
        body, name=name,
        out_shape=(pltpu.SemaphoreType.DMA(()), pltpu.SemaphoreType.DMA(()), pltpu.HBM(land.shape, land.dtype),
                   jax.ShapeDtypeStruct((8, 128), F32)),
        in_specs=(_HBM,), out_specs=(_SEM, _SEM, _HBM, pl.BlockSpec(memory_space=pltpu.VMEM)),
        input_output_aliases={0: 2},
        compiler_params=pltpu.CompilerParams(has_side_effects=_EFFECT),
    )(pltpu.with_memory_space_constraint(land, pltpu.HBM))


def _relay_wait(handle, r, after, name):
    send_sem, recv_sem, land_thru, _ = handle

    def body(land_ref, send_sem, recv_sem, after_ref, got_ref):
        x, y, c, _ = _position()
        three = land_ref.at[:, pl.ds(0, 3 * r), :]
        cp = pltpu.make_async_remote_copy(src_ref=three, dst_ref=three, send_sem=send_sem, recv_sem=recv_sem,
                                          device_id=(x, y, 1 - c), device_id_type=MESH)
        cp.wait_send()
        cp.wait_recv()

    return pl.pallas_call(
        body, name=name, out_shape=(pltpu.HBM(land_thru.shape, land_thru.dtype),),
        in_specs=(_HBM, _SEM, _SEM, pl.BlockSpec(memory_space=pl.ANY)), out_specs=(_HBM,),
        input_output_aliases={0: 0},
        compiler_params=pltpu.CompilerParams(has_side_effects=_EFFECT),
    )(land_thru, send_sem, recv_sem, after)[0]


def _exchange_start(grad, name):
    g, rows, C = grad.shape
    r = rows // N_DEV
    land = lax.empty((N_DEV, g, r, C), grad.dtype)
    return _push_start(grad, land, lambda s, z, i, p: (s.at[:, pl.ds(p * r, r), :], z.at[i]), name)


def _exchange_wait(handle, after, name):
    return _push_wait(handle, lambda z, n: z.at[pl.ds(0, n)], after, name)


def _sum_blocks(v, name):
    k, rows, cols = v.shape
    tr = rows
    for cand in (rows, 512, 352, 256, 176, 128, 64, 32, 16):
        if rows % cand == 0 and k * cand * cols * v.dtype.itemsize <= 6 * 1024 * 1024:
            tr = cand
            break

    def body(v_ref, o_ref):
        acc = v_ref[0].astype(F32)
        for s in range(1, k):
            acc = acc + v_ref[s].astype(F32)
        o_ref[...] = acc

    return pl.pallas_call(
        body, grid=(rows // tr,), name=name,
        in_specs=[pl.BlockSpec((k, tr, cols), lambda i: (0, i, 0))],
        out_specs=pl.BlockSpec((tr, cols), lambda i: (i, 0)),
        out_shape=jax.ShapeDtypeStruct((rows, cols), F32),
        compiler_params=_params(("parallel",)))(v)


def _adam_math(w, g, m, v):
    m2 = B1 * m + (1.0 - B1) * g
    v2 = B2 * v + (1.0 - B2) * (g * g)
    m_hat = m2 / (1.0 - B1 ** STEP)
    v_hat = v2 / (1.0 - B2 ** STEP)
    return -LR * (m_hat / (jnp.sqrt(v_hat) + EPS) + WD * w), m2, v2


def _adamw(w, g, m, v, name, dep=None):
    shp = w.shape
    rows, cols = (shp[-2], shp[-1]) if len(shp) >= 2 else (1, shp[-1])
    lead = math.prod(shp[:-2]) if len(shp) > 2 else 1
    fits = [t for t in range(8, rows + 1, 8) if rows % t == 0 and t * cols * 4 <= 2 * 1024 * 1024]
    tr = max(fits) if fits else rows

    def body(w_ref, g_ref, m_ref, v_ref, *rest):
        d_ref, m2_ref, v2_ref = rest[-3:]
        d_ref[...], m2_ref[...], v2_ref[...] = _adam_math(w_ref[...], g_ref[...], m_ref[...], v_ref[...])

    blk = pl.BlockSpec((None, tr, cols), lambda b, i: (b, i, 0))
    extra = [] if dep is None else [dep]
    outs = pl.pallas_call(
        body, grid=(lead, rows // tr), name=name,
        in_specs=[blk] * 4 + [pl.BlockSpec(memory_space=pl.ANY)] * len(extra), out_specs=[blk] * 3,
        out_shape=[jax.ShapeDtypeStruct((lead, rows, cols), F32)] * 3,
        compiler_params=_params(("parallel", "parallel")))(*[a.reshape(lead, rows, cols) for a in (w, g, m, v)], *extra)
    return [o.reshape(shp) for o in outs]


def _as2d(a):
    n = a.size
    if n % 1024 == 0:
        return a.reshape(n // 1024, 1024)
    if n % 128 == 0:
        return a.reshape(n // 128, 128)
    return a.reshape(1, n)


def _blocks_to_cols(a):
    b = jnp.moveaxis(a, 0, -2)
    return b.reshape(b.shape[:-2] + (b.shape[-2] * b.shape[-1],))


def _pack_rows(parts):
    padded, offs, r = [], [], 0
    for p in parts:
        pad = (-p.shape[0]) % 8
        padded.append(jnp.pad(p, ((0, pad), (0, 0))) if pad else p)
        offs.append(r)
        r += p.shape[0] + pad
    return jnp.concatenate(padded, axis=0), offs


def _silu(x):
    return x * jax.nn.sigmoid(x)


def kernel(x, c, ctx, c_ctx, w_mod, b_mod, ln_g, ln_b, ffn_w_gate, ffn_w_up, ffn_w_down, mix_ab_w_in, attn_sink, pool_w, pool_scale, mix_ab_w_out, lru_w_in, lru_conv_w, lru_conv_b, lru_wa, lru_ba, lru_wx, lru_bx, lru_lambda, lru_w_out, loss_target, m_c_ctx, m_w_mod, m_b_mod, m_ln_g, m_ln_b, m_ffn_w_gate, m_ffn_w_up, m_ffn_w_down, m_mix_ab_w_in, m_attn_sink, m_pool_w, m_pool_scale, m_mix_ab_w_out, m_lru_w_in, m_lru_conv_w, m_lru_conv_b, m_lru_wa, m_lru_ba, m_lru_wx, m_lru_bx, m_lru_lambda, m_lru_w_out, v_c_ctx, v_w_mod, v_b_mod, v_ln_g, v_ln_b, v_ffn_w_gate, v_ffn_w_up, v_ffn_w_down, v_mix_ab_w_in, v_attn_sink, v_pool_w, v_pool_scale, v_mix_ab_w_out, v_lru_w_in, v_lru_conv_w, v_lru_conv_b, v_lru_wa, v_lru_ba, v_lru_wx, v_lru_bx, v_lru_lambda, v_lru_w_out):
    weights = dict(c_ctx=c_ctx, w_mod=w_mod, b_mod=b_mod, ln_g=ln_g, ln_b=ln_b, ffn_w_gate=ffn_w_gate,
                   ffn_w_up=ffn_w_up, ffn_w_down=ffn_w_down, mix_ab_w_in=mix_ab_w_in, attn_sink=attn_sink,
                   pool_w=pool_w, pool_scale=pool_scale, mix_ab_w_out=mix_ab_w_out, lru_w_in=lru_w_in,
                   lru_conv_w=lru_conv_w, lru_conv_b=lru_conv_b, lru_wa=lru_wa, lru_ba=lru_ba, lru_wx=lru_wx,
                   lru_bx=lru_bx, lru_lambda=lru_lambda, lru_w_out=lru_w_out)
    mom_m = dict(c_ctx=m_c_ctx, w_mod=m_w_mod, b_mod=m_b_mod, ln_g=m_ln_g, ln_b=m_ln_b, ffn_w_gate=m_ffn_w_gate,
                 ffn_w_up=m_ffn_w_up, ffn_w_down=m_ffn_w_down, mix_ab_w_in=m_mix_ab_w_in, attn_sink=m_attn_sink,
                 pool_w=m_pool_w, pool_scale=m_pool_scale, mix_ab_w_out=m_mix_ab_w_out, lru_w_in=m_lru_w_in,
                 lru_conv_w=m_lru_conv_w, lru_conv_b=m_lru_conv_b, lru_wa=m_lru_wa, lru_ba=m_lru_ba, lru_wx=m_lru_wx,
                 lru_bx=m_lru_bx, lru_lambda=m_lru_lambda, lru_w_out=m_lru_w_out)
    mom_v = dict(c_ctx=v_c_ctx, w_mod=v_w_mod, b_mod=v_b_mod, ln_g=v_ln_g, ln_b=v_ln_b, ffn_w_gate=v_ffn_w_gate,
                 ffn_w_up=v_ffn_w_up, ffn_w_down=v_ffn_w_down, mix_ab_w_in=v_mix_ab_w_in, attn_sink=v_attn_sink,
                 pool_w=v_pool_w, pool_scale=v_pool_scale, mix_ab_w_out=v_mix_ab_w_out, lru_w_in=v_lru_w_in,
                 lru_conv_w=v_lru_conv_w, lru_conv_b=v_lru_conv_b, lru_wa=v_lru_wa, lru_ba=v_lru_ba, lru_wx=v_lru_wx,
                 lru_bx=v_lru_bx, lru_lambda=v_lru_lambda, lru_w_out=v_lru_w_out)
    names = list(weights)

    n_lat, n_ctx = x.shape[1], ctx.shape[1]
    cfg = _Cfg(n_lat, n_ctx)
    _, _, _, me = _position()
    mcols = w_mod.shape[2]

    def t_bf16(w):
        return jnp.swapaxes(w, -1, -2).astype(BF16)

    def ffn_src(l, i):
        return jnp.stack([t_bf16(ffn_w_gate[l, i]), t_bf16(ffn_w_up[l, i]), ffn_w_down[l, i].astype(BF16)])

    pending = {}

    def start_gathers(items, tok):
        for key, make_src in items:
            pending[key] = _gather_start(make_src() + tok.astype(BF16), "gather_start_" + key)
            tok = pending[key][4][0, 0]
        return tok

    def weights_now(key, after):
        return _gather_wait(pending[key], after, "gather_wait_" + key)

    first = _gather_start(ffn_src(0, 0), "gather_start_ffn00", SAME_CORE_AND_SIBLING)
    tok = first[4][0, 0]

    small_names = ["ln_g", "ln_b", "lru_conv_w", "lru_conv_b", "lru_ba", "lru_bx", "lru_lambda"]
    small, small_off = _pack_rows([(c + tok).reshape(-1, 128)] + [weights[n].reshape(-1, 128) for n in small_names])
    small_all = _all_gather(small, "gather_small", True)

    def small_full(idx, shp):
        rows = math.prod(shp) // 128
        return _blocks_to_cols(small_all[:, small_off[idx]:small_off[idx] + rows, :].reshape((N_DEV,) + shp))

    c_all = small_all[:, :2 * D // 128, :].reshape(2 * N_DEV, D)
    ln_g_f, ln_b_f = small_full(1, ln_g.shape), small_full(2, ln_b.shape)
    lru_consts = (small_full(3, lru_conv_w.shape)[0], small_full(4, lru_conv_b.shape), lru_wa[0],
                  small_full(5, lru_ba.shape)[0], lru_wx[0], small_full(6, lru_bx.shape)[0],
                  small_full(7, lru_lambda.shape)[0])

    s_rows = jnp.zeros((32, D), F32).at[:16].set(_silu(c_all)).at[16].set(_silu(c_ctx)).astype(BF16)
    mod_mine = jnp.stack([_matmul(s_rows, w_mod[l], "nn", F32, "mod_fwd", bn_cap=1280) for l in range(2)])
    mod_all = _all_gather(mod_mine.reshape(64, mcols), "gather_mod", True).reshape(N_DEV, 2, 32, mcols)
    r_ffn = ffn_w_down.shape[2]
    relay = _relay_start(_gather_wait(first, mod_all, "gather_wait_ffn00", n_peers=len(SAME_CORE_AND_SIBLING)),
                         r_ffn, "gather_relay_start_ffn00")
    tok = start_gathers([("ab_in", lambda: t_bf16(mix_ab_w_in)), ("ab_out", lambda: mix_ab_w_out.astype(BF16)),
                         ("ffn01", lambda: ffn_src(0, 1)), ("ffn10", lambda: ffn_src(1, 0)),
                         ("lru_in", lambda: t_bf16(lru_w_in)), ("lru_out", lambda: lru_w_out.astype(BF16)),
                         ("ffn11", lambda: ffn_src(1, 1))], relay[3][0, 0])
    mod_full = _blocks_to_cols(mod_all) + (b_mod[:, None, :] + tok)
    ex0 = 2 * me
    mods = []
    for l in range(2):
        rows = jnp.stack([lax.dynamic_index_in_dim(mod_full[l], ex0, 0, False),
                          lax.dynamic_index_in_dim(mod_full[l], ex0 + 1, 0, False), mod_full[l, 16]])
        mods.append(rows.reshape(3, N_MOD, D))

    h0 = jnp.concatenate([x.reshape(cfg.t_lat, D), ctx.reshape(cfg.t_ctx, D)], axis=0)
    cos, sin = _rope_tables(n_lat)
    sink_rows = jnp.broadcast_to(attn_sink[0][:, None], (8, 128)).astype(F32)

    saved = []
    wf = [[None, None], [None, None]]
    h = h0
    xin = _modulate(cfg, h0, mods[0], 0, 1, "modulate_in")
    for l in range(2):
        st = {"h_in": h, "xin1": xin}
        wf[l][0] = (_relay_wait(relay, r_ffn, xin, "gather_relay_wait_ffn00") if l == 0
                    else weights_now("ffn10", xin))
        g1, u1, y1 = _ffn_fwd(xin, wf[l][0], "ffn_fwd")
        h1, xhat1, rstd1, xin2 = _ln_fwd(cfg, h, y1, mods[l], 2, 0.5, ln_g_f[l, 0][None], ln_b_f[l, 0][None],
                                          mods[l], (3, 4), "ln_fwd_a")
        st.update(g1=g1, u1=u1, y1=y1, h1=h1, xhat1=xhat1, rstd1=rstd1, xin2=xin2)
        if l == 0:
            w_ab_in_t = weights_now("ab_in", xin2)[0]
            p = _matmul(xin2, w_ab_in_t, "nt", BF16, "mix_ab_in")
            att_l, att_c = _attn_fwd(cfg, p, cos, sin, sink_rows, "attn_fwd")
            pool_l = _pool_fwd(p, pool_w[0], pool_scale, n_lat, 0, 2, "pool_fwd_lat")
            pool_c = _pool_fwd(p, pool_w[0], pool_scale, n_ctx, cfg.ctx_blk, 2, "pool_fwd_ctx")
            cat = jnp.concatenate([jnp.concatenate([att_l, pool_l], axis=1),
                                   jnp.concatenate([att_c, pool_c], axis=1)], axis=0)
            w_ab_out = weights_now("ab_out", cat)[0]
            y2 = _matmul(cat, w_ab_out, "nn", BF16, "mix_ab_out")
        else:
            w_lru_in_t = weights_now("lru_in", xin2)[0]
            p = _matmul(xin2, w_lru_in_t, "nt", BF16, "lru_in")
            z_l, z_c, st["h_lat"], st["h_ctx"] = _lru_fwd(cfg, p, lru_consts, "lru_fwd")
            cat = jnp.concatenate([z_l, z_c], axis=0)
            w_lru_out = weights_now("lru_out", cat)[0]
            y2 = _matmul(cat, w_lru_out, "nn", BF16, "lru_out")
        h2, xhat2, rstd2, xin3 = _ln_fwd(cfg, h1, y2, mods[l], 5, 1.0, ln_g_f[l, 1][None], ln_b_f[l, 1][None],
                                          mods[l], (6, 7), "ln_fwd_b")
        wf[l][1] = weights_now("ffn%d1" % l, xin3)
        g3, u3, y3 = _ffn_fwd(xin3, wf[l][1], "ffn_fwd")
        if l == 0:
            h3, xhat3, rstd3, xin = _ln_fwd(cfg, h2, y3, mods[l], 8, 0.5, ln_g_f[l, 2][None], ln_b_f[l, 2][None],
                                            mods[1], (0, 1), "ln_fwd_a")
        else:
            h3, xhat3, rstd3 = _ln_fwd(cfg, h2, y3, mods[l], 8, 0.5, ln_g_f[l, 2][None], ln_b_f[l, 2][None],
                                       None, None, "ln_fwd_last")
        st.update(p=p, cat=cat, y2=y2, h2=h2, xhat2=xhat2, rstd2=rstd2, xin3=xin3, g3=g3, u3=u3, y3=y3,
                  xhat3=xhat3, rstd3=rstd3)
        saved.append(st)
        h = h3

    dy, loss_tile = _loss(cfg, h, loss_target.reshape(cfg.t_lat, D), "loss")
    loss = lax.psum(loss_tile[0, 0], ("x", "y", "c"))

    grads = {}
    dmod = [None, None]
    recv_ffn = [[None, None], [None, None]]
    dln_g = [[None] * 3, [None] * 3]
    dln_b = [[None] * 3, [None] * 3]

    def ffn_weight_grads(tag, xin_b, dg, du, a_act, dys):
        parts = [_matmul(dg, xin_b, "tn", BF16, "ffn_dw", bm_cap=1408, bk_cap=2304)[None],
                 _matmul(du, xin_b, "tn", BF16, "ffn_dw", bm_cap=1408, bk_cap=2304)[None],
                 _matmul(a_act, dys, "tn", BF16, "ffn_dw", bm_cap=1408, bk_cap=2304)[None]]
        return [_exchange_start(part, "exchange_start_ffn%s_%d" % (tag, k)) for k, part in enumerate(parts)]

    def pin(handles):
        total = handles[0][4][0, 0]
        for hd in handles[1:]:
            total = total + hd[4][0, 0]
        return total

    up = (dy,)
    dmod_next = None
    last_sent = None
    for l in (1, 0):
        st = saved[l]
        dm = [None] * N_MOD

        def put_stats(stats, gate_idx, nxt):
            dm[gate_idx] = stats[:, 2, :]
            if nxt is not None:
                nxt[0][nxt[1]] = stats[:, 4, :]
                nxt[0][nxt[1] + 1] = stats[:, 3, :]

        lng3 = ln_g_f[l, 2][None] if last_sent is None else ln_g_f[l, 2][None] + pin(last_sent)
        if len(up) > 1:
            up = (up[0], up[1], ln_b_f[l, 2][None], up[3], up[4])
        dres, dys, stats = _ln_bwd(cfg, up, st["xhat3"], st["rstd3"], st["y3"], mods[l], 8, 0.5,
                                   lng3, "ln_bwd_fused" if len(up) > 1 else "ln_bwd_last")
        put_stats(stats, 8, None if len(up) == 1 else (dmod_next, 0))
        dln_g[l][2], dln_b[l][2] = stats[:, 0, :].sum(0), stats[:, 1, :].sum(0)
        dg, du, a_act, dxin = _ffn_bwd(dys, st["g3"], st["u3"], wf[l][1], "ffn_bwd")
        recv_ffn[l][1] = ffn_weight_grads("%d1" % l, st["xin3"], dg, du, a_act, dys)
        dres, dys, stats = _ln_bwd(cfg, (dres, dxin, ln_b_f[l, 1][None], mods[l], 7), st["xhat2"], st["rstd2"], st["y2"],
                                   mods[l], 5, 1.0, ln_g_f[l, 1][None] + pin(recv_ffn[l][1]), "ln_bwd_fused")
        put_stats(stats, 5, (dm, 6))
        dln_g[l][1], dln_b[l][1] = stats[:, 0, :].sum(0), stats[:, 1, :].sum(0)
        if l == 0:
            dw_out = _matmul(st["cat"], dys, "tn", BF16, "mix_ab_dw_out")
            dcat = _matmul(dys, w_ab_out, "nt", BF16, "mix_ab_dcat")
            dq, dk, dv, dqc, dkc, dvc, dsink = _attn_bwd(cfg, st["p"], dcat, cos, sin, sink_rows, "attn_bwd")
            du_l, dpw_l, dps_l = _pool_bwd(st["p"], pool_w[0], pool_scale, dcat, n_lat, 0, 2, "pool_bwd_lat")
            du_c, dpw_c, dps_c = _pool_bwd(st["p"], pool_w[0], pool_scale, dcat, n_ctx, cfg.ctx_blk, 2, "pool_bwd_ctx")
            dp = jnp.concatenate([jnp.concatenate([dq, dk, dv, du_l], axis=1),
                                  jnp.concatenate([dqc, dkc, dvc, du_c], axis=1)], axis=0)
            dw_in_t = _matmul(dp, st["xin2"], "tn", BF16, "mix_ab_dw_in", bm_cap=1280)
            dxin = _matmul(dp, w_ab_in_t, "nn", BF16, "mix_ab_dx")
            recv_mix = [_exchange_start(part, "exchange_start_mix_ab_%d" % k)
                        for k, part in enumerate((dw_in_t[None], dw_out[None], _as2d(dpw_l + dpw_c)[None]))]
            grads["attn_sink"] = (dsink[0, :, 0] + dsink[1, :, 0])[None, :]
            grads["pool_scale"] = dps_l + dps_c
        else:
            dw_out = _matmul(st["cat"], dys, "tn", BF16, "lru_dw_out")
            dz = _matmul(dys, w_lru_out, "nt", BF16, "lru_dz")
            dgl, dul, dgc, duc, dwa, dwx, vec = _lru_bwd(cfg, st["p"], dz, st["h_lat"], st["h_ctx"], lru_consts, "lru_bwd")
            dp = jnp.concatenate([jnp.concatenate([dgl, dul], axis=1), jnp.concatenate([dgc, duc], axis=1)], axis=0)
            dw_in_t = _matmul(dp, st["xin2"], "tn", BF16, "lru_dw_in", bm_cap=1024)
            dxin = _matmul(dp, w_lru_in_t, "nn", BF16, "lru_dx")
            recv_mix = [_exchange_start(part, "exchange_start_lru_%d" % k)
                        for k, part in enumerate((dw_in_t[None], dw_out[None], _as2d(dwa)[None], _as2d(dwx)[None]))]
            vec_t = jnp.moveaxis(vec, 0, 1).reshape(16, D)
            grads["lru_ba"], grads["lru_bx"] = vec_t[0:2], vec_t[2:4]
            grads["lru_lambda"], grads["lru_conv_w"], grads["lru_conv_b"] = vec_t[4:6], vec_t[6:10], vec_t[10:11]
        if l == 0:
            recv_ab = recv_mix
        else:
            recv_lru = recv_mix
        dres, dys, stats = _ln_bwd(cfg, (dres, dxin, ln_b_f[l, 0][None], mods[l], 4), st["xhat1"], st["rstd1"], st["y1"],
                                   mods[l], 2, 0.5, ln_g_f[l, 0][None] + pin(recv_mix), "ln_bwd_fused")
        put_stats(stats, 2, (dm, 3))
        dln_g[l][0], dln_b[l][0] = stats[:, 0, :].sum(0), stats[:, 1, :].sum(0)
        dg, du, a_act, dxin = _ffn_bwd(dys, st["g1"], st["u1"], wf[l][0], "ffn_bwd")
        recv_ffn[l][0] = ffn_weight_grads("%d0" % l, st["xin1"], dg, du, a_act, dys)
        last_sent = recv_ffn[l][0]
        dmod[l] = dm
        dmod_next = dm
        up = (dres, dxin, None, mods[l], 1)
    dh0, stats = _modulate_bwd(cfg, up[0], up[1], h0, mods[0] + pin(last_sent), 1, "modulate_bwd")
    dmod[0][0], dmod[0][1] = stats[:, 4, :], stats[:, 3, :]
    grad_x = dh0.reshape(x.shape)

    dmod_mine = jnp.stack([jnp.stack(dmod[l], axis=1).reshape(3, N_MOD * D) for l in range(2)])
    n_dm = 6 * N_MOD * D // 128
    dmod_sent = _gather_start(dmod_mine.reshape(1, n_dm, 128), "gather_start_dmod")

    def arrived(handle, name):
        return _exchange_wait(handle, dmod_sent[4], name)

    recv_ffn = [[[arrived(hd, "exchange_wait_ffn%d%d_%d" % (l, i, k)) for k, hd in enumerate(recv_ffn[l][i])]
                 for i in range(2)] for l in range(2)]
    recv_ab = [arrived(hd, "exchange_wait_mix_ab_%d" % k) for k, hd in enumerate(recv_ab)]
    recv_lru = [arrived(hd, "exchange_wait_lru_%d" % k) for k, hd in enumerate(recv_lru)]

    def shard_sum(recv, name):
        return _sum_blocks(recv.reshape(N_DEV, recv.shape[2], recv.shape[3]), name)

    gate_g = [[None, None], [None, None]]
    up_g = [[None, None], [None, None]]
    down_g = [[None, None], [None, None]]
    for l in range(2):
        for i in range(2):
            gt, ut, dn = [shard_sum(r, "sum_ffn") for r in recv_ffn[l][i]]
            gate_g[l][i], up_g[l][i], down_g[l][i] = gt.T, ut.T, dn
    grads["ffn_w_gate"] = jnp.stack([jnp.stack(gate_g[l]) for l in range(2)])
    grads["ffn_w_up"] = jnp.stack([jnp.stack(up_g[l]) for l in range(2)])
    grads["ffn_w_down"] = jnp.stack([jnp.stack(down_g[l]) for l in range(2)])
    grads["mix_ab_w_in"] = shard_sum(recv_ab[0], "sum_mix_in").T[None]
    grads["mix_ab_w_out"] = shard_sum(recv_ab[1], "sum_mix_out")[None]
    grads["lru_w_in"] = shard_sum(recv_lru[0], "sum_lru_in").T[None]
    grads["lru_w_out"] = shard_sum(recv_lru[1], "sum_lru_out")[None]
    rep_parts = [shard_sum(recv_lru[2], "sum_rep"), shard_sum(recv_lru[3], "sum_rep"), shard_sum(recv_ab[2], "sum_rep")]
    rep_names = ["lru_wa", "lru_wx", "pool_w"]

    dmod_all = _gather_wait(dmod_sent, rep_parts[2], "gather_wait_dmod").reshape(N_DEV, n_dm, 128)
    dmod_sum = _sum_blocks(dmod_all, "sum_dmod").reshape(2, 3, N_MOD * D)
    dmod_all = dmod_all.reshape(N_DEV, 2, 3, N_MOD * D)
    grads["b_mod"] = dmod_sum[:, 0] + dmod_sum[:, 1] + dmod_sum[:, 2]
    dmod_ex = jnp.moveaxis(dmod_all[:, :, 0:2, :], 1, 0).reshape(2, 2 * N_DEV, N_MOD * D)
    dm_rows = jnp.zeros((2, 32, N_MOD * D), F32).at[:, :16].set(dmod_ex).at[:, 16].set(dmod_sum[:, 2])
    dm_cols = lax.dynamic_slice_in_dim(dm_rows, me * mcols, mcols, axis=2).astype(BF16)
    grads["w_mod"] = jnp.stack([_matmul(s_rows, dm_cols[l], "tn", F32, "mod_dw", bn_cap=1280) for l in range(2)])
    ds_part = None
    for l in range(2):
        part = _matmul(dm_cols[l, 16:32], w_mod[l], "nt", F32, "mod_ds", bk_cap=1280)[0]
        ds_part = part if ds_part is None else ds_part + part

    dln_g_f = jnp.stack([jnp.stack(dln_g[l]) for l in range(2)])
    dln_b_f = jnp.stack([jnp.stack(dln_b[l]) for l in range(2)])
    sink_pad = jnp.zeros((1, 128), F32).at[0, :8].set(grads["attn_sink"][0])
    part_list = [p_.reshape(-1, 128) for p_ in rep_parts] + [
        dln_g_f.reshape(-1, 128), dln_b_f.reshape(-1, 128), grads["lru_conv_w"].reshape(-1, 128),
        grads["lru_conv_b"].reshape(-1, 128), grads["lru_ba"].reshape(-1, 128), grads["lru_bx"].reshape(-1, 128),
        grads["lru_lambda"].reshape(-1, 128), ds_part.reshape(-1, 128), sink_pad, grads["pool_scale"].reshape(-1, 128)]
    parts, part_off = _pack_rows(part_list)
    parts_sent = _gather_start(parts[None], "gather_start_partials")

    delta, new_m, new_v = {}, {}, {}
    for n in ("w_mod", "b_mod", "ffn_w_gate", "ffn_w_up", "ffn_w_down", "mix_ab_w_in", "mix_ab_w_out",
              "lru_w_in", "lru_w_out"):
        grads[n] = grads[n].reshape(weights[n].shape)
        delta[n], new_m[n], new_v[n] = _adamw(weights[n], grads[n], mom_m[n], mom_v[n], "adamw", dep=parts_sent[4])
    parts_all = _gather_wait(parts_sent, delta["lru_w_out"], "gather_wait_partials").reshape(N_DEV, parts.shape[0], 128)
    parts_sum = _sum_blocks(parts_all, "sum_partials")

    for i, n in enumerate(rep_names):
        rows = part_list[i].shape[0]
        grads[n] = parts_all[:, part_off[i]:part_off[i] + rows, :].reshape(weights[n].shape)

    def take(idx):
        return parts_sum[part_off[idx]:part_off[idx] + part_list[idx].shape[0]]

    def my_cols(full, shp):
        w = shp[-1]
        return lax.dynamic_slice_in_dim(full, me * w, w, axis=full.ndim - 1)

    grads["ln_g"] = my_cols(take(3).reshape(2, 3, D), ln_g.shape)
    grads["ln_b"] = my_cols(take(4).reshape(2, 3, D), ln_b.shape)
    grads["lru_conv_w"] = my_cols(take(5).reshape(1, 4, D), lru_conv_w.shape)
    grads["lru_conv_b"] = my_cols(take(6).reshape(1, D), lru_conv_b.shape)
    grads["lru_ba"] = my_cols(take(7).reshape(1, 2, D), lru_ba.shape)
    grads["lru_bx"] = my_cols(take(8).reshape(1, 2, D), lru_bx.shape)
    grads["lru_lambda"] = my_cols(take(9).reshape(1, 2, D), lru_lambda.shape)
    sg = jax.nn.sigmoid(c_ctx)
    grads["c_ctx"] = take(10).reshape(D) * (sg * (1.0 + c_ctx * (1.0 - sg)))
    grads["attn_sink"] = take(11)[:, :8]
    grads["pool_scale"] = take(12).reshape(pool_scale.shape)

    for n in names:
        if n in delta:
            continue
        grads[n] = grads[n].reshape(weights[n].shape)
        delta[n], new_m[n], new_v[n] = _adamw(weights[n], grads[n], mom_m[n], mom_v[n], "adamw")

    return (loss, grad_x, *[grads[n] for n in names], *[delta[n] for n in names],
            *[new_m[n] for n in names], *[new_v[n] for n in names])
```

```python
import functools
import math

import jax
import jax.numpy as jnp
from jax import lax
from jax.experimental import pallas as pl
from jax.experimental.pallas import tpu as pltpu

F32 = jnp.float32
BF16 = jnp.bfloat16
MESH = pl.DeviceIdType.MESH

D = 1024
N_MOD = 9
N_DEV = 8
HEAD_DIM = 64
ATT_HEADS = 8
KV_HEADS = 2
ATT_W = 512
BLK = 128
ATT_SCALE = HEAD_DIM ** -0.5
GRID_W = 64
ROPE_FREQS = HEAD_DIM // 4
ROPE_THETA = 10000.0
POOL_R = (1, 2, 4, 8)
LRU_C = 8.0
LN_EPS = 1e-5
NEG_INF = -1e30
ALPHA = 4.0 ** 0.25
LR, B1, B2, EPS, WD, STEP = 0.001, 0.9, 0.999, 1e-08, 0.01, 10
VMEM_LIMIT = 56 * 1024 * 1024
ROW_TILE = 512


def _params(sem=None):
    if sem is None:
        return pltpu.CompilerParams(vmem_limit_bytes=VMEM_LIMIT)
    return pltpu.CompilerParams(dimension_semantics=sem, vmem_limit_bytes=VMEM_LIMIT)


def _sigmoid(x):
    return 0.5 * jnp.tanh(0.5 * x) + 0.5


def _dot(a, b):
    return jnp.dot(a.astype(BF16), b.astype(BF16), preferred_element_type=F32)


def _dot_nt(a, b):
    return lax.dot_general(a.astype(BF16), b.astype(BF16), (((1,), (1,)), ((), ())), preferred_element_type=F32)


def _dot_tn(a, b):
    return lax.dot_general(a.astype(BF16), b.astype(BF16), (((0,), (0,)), ((), ())), preferred_element_type=F32)


def _pick(n, cap):
    best = None
    for m in range(128, min(n, cap) + 1, 128):
        if n % m == 0:
            best = m
    return n if best is None else best


def _chunks(width, step=256):
    out, c = [], 0
    while c < width:
        w = min(step, width - c)
        out.append((c, w))
        c += w
    return out


class _Cfg:
    def __init__(self, n_lat, n_ctx):
        self.n_lat, self.n_ctx = n_lat, n_ctx
        self.t_lat, self.t_ctx = 2 * n_lat, 2 * n_ctx
        self.T = self.t_lat + self.t_ctx
        self.tm = min(ROW_TILE, self.t_ctx)
        assert n_lat % self.tm == 0 and self.t_ctx % self.tm == 0 and n_lat >= 3 * BLK and n_ctx % BLK == 0
        self.nt = self.T // self.tm
        self.nlt = n_lat // self.tm
        self.ctx_blk = self.t_lat // n_ctx

    def seg(self, i):
        return jnp.minimum(i // self.nlt, 2)

    def first_of_seg(self, i):
        return jnp.where(i < 2 * self.nlt, i % self.nlt == 0, i == 2 * self.nlt)


def _modulate(cfg, h, mod, shift_idx, scale_idx, name):
    tm = cfg.tm

    def body(h_ref, mod_ref, o_ref):
        sh = mod_ref[shift_idx:shift_idx + 1, :]
        sc = mod_ref[scale_idx:scale_idx + 1, :]
        o_ref[...] = (h_ref[...] * (1.0 + sc) + sh).astype(BF16)

    return pl.pallas_call(
        body, grid=(cfg.nt,), name=name,
        in_specs=[pl.BlockSpec((tm, D), lambda i: (i, 0)),
                  pl.BlockSpec((None, N_MOD, D), lambda i: (cfg.seg(i), 0, 0))],
        out_specs=pl.BlockSpec((tm, D), lambda i: (i, 0)),
        out_shape=jax.ShapeDtypeStruct((cfg.T, D), BF16),
        compiler_params=_params(("parallel",)),
    )(h, mod)


def _ln_fwd(cfg, h, y, mod, gate_idx, coef, lng, lnb, mod_next, next_idx, name):
    tm = cfg.tm
    has_next = next_idx is not None

    def body(*refs):
        if has_next:
            h_ref, y_ref, mod_ref, g_ref, b_ref, modn_ref, hn_ref, xhat_ref, rstd_ref, xin_ref = refs
        else:
            h_ref, y_ref, mod_ref, g_ref, b_ref, hn_ref, xhat_ref, rstd_ref = refs
        gate = mod_ref[gate_idx:gate_idx + 1, :]
        z = ALPHA * h_ref[...] + (coef * gate) * y_ref[...].astype(F32)
        mu = jnp.mean(z, axis=-1, keepdims=True)
        zc = z - mu
        var = jnp.mean(zc * zc, axis=-1, keepdims=True)
        rstd = lax.rsqrt(var + LN_EPS)
        xhat = zc * rstd
        hn = xhat * g_ref[...] + b_ref[...]
        hn_ref[...] = hn
        xhat_ref[...] = xhat.astype(BF16)
        rstd_ref[...] = rstd
        if has_next:
            sh = modn_ref[next_idx[0]:next_idx[0] + 1, :]
            sc = modn_ref[next_idx[1]:next_idx[1] + 1, :]
            xin_ref[...] = (hn * (1.0 + sc) + sh).astype(BF16)

    row = pl.BlockSpec((tm, D), lambda i: (i, 0))
    modspec = pl.BlockSpec((None, N_MOD, D), lambda i: (cfg.seg(i), 0, 0))
    vec = pl.BlockSpec((1, D), lambda i: (0, 0))
    in_specs = [row, row, modspec, vec, vec]
    args = [h, y, mod, lng, lnb]
    out_specs = [row, row, pl.BlockSpec((tm, 1), lambda i: (i, 0))]
    out_shape = [jax.ShapeDtypeStruct((cfg.T, D), F32), jax.ShapeDtypeStruct((cfg.T, D), BF16),
                 jax.ShapeDtypeStruct((cfg.T, 1), F32)]
    if has_next:
        in_specs.append(modspec)
        args.append(mod_next)
        out_specs.append(row)
        out_shape.append(jax.ShapeDtypeStruct((cfg.T, D), BF16))
    return pl.pallas_call(body, grid=(cfg.nt,), name=name, in_specs=in_specs, out_specs=out_specs,
                          out_shape=out_shape, compiler_params=_params(("parallel",)))(*args)


def _ln_bwd(cfg, up, xhat, rstd, y, mod, gate_idx, coef, lng, name):
    tm = cfg.tm
    fused = len(up) > 1
    scale_next = up[4] if fused else None

    def body(*refs):
        if fused:
            dres_n, dxin_n, b_ref, modn_ref, xhat_ref, rstd_ref, y_ref, mod_ref, g_ref, dres_ref, dys_ref, st_ref = refs
        else:
            dhn_ref, xhat_ref, rstd_ref, y_ref, mod_ref, g_ref, dres_ref, dys_ref, st_ref = refs
        i = pl.program_id(0)

        @pl.when(cfg.first_of_seg(i))
        def _():
            st_ref[...] = jnp.zeros_like(st_ref)

        xhat = xhat_ref[...].astype(F32)
        if fused:
            dxin = dxin_n[...].astype(F32)
            sc = modn_ref[scale_next:scale_next + 1, :]
            dhn = dres_n[...] + dxin * (1.0 + sc)
            shift_sum = jnp.sum(dxin, axis=0, keepdims=True)
            st_ref[3:4, :] += g_ref[...] * jnp.sum(dxin * xhat, axis=0, keepdims=True) + b_ref[...] * shift_sum
            st_ref[4:5, :] += shift_sum
        else:
            dhn = dhn_ref[...]
        gdh = dhn * g_ref[...]
        m1 = jnp.mean(gdh, axis=-1, keepdims=True)
        m2 = jnp.mean(gdh * xhat, axis=-1, keepdims=True)
        dz = rstd_ref[...] * (gdh - m1 - xhat * m2)
        gate = mod_ref[gate_idx:gate_idx + 1, :]
        dres_ref[...] = ALPHA * dz
        dys_ref[...] = ((coef * gate) * dz).astype(BF16)
        st_ref[0:1, :] += jnp.sum(dhn * xhat, axis=0, keepdims=True)
        st_ref[1:2, :] += jnp.sum(dhn, axis=0, keepdims=True)
        st_ref[2:3, :] += jnp.sum((coef * dz) * y_ref[...].astype(F32), axis=0, keepdims=True)

    row = pl.BlockSpec((tm, D), lambda i: (i, 0))
    modspec = pl.BlockSpec((None, N_MOD, D), lambda i: (cfg.seg(i), 0, 0))
    vec = pl.BlockSpec((1, D), lambda i: (0, 0))
    col = pl.BlockSpec((tm, 1), lambda i: (i, 0))
    if fused:
        in_specs = [row, row, vec, modspec, row, col, row, modspec, vec]
        args = [up[0], up[1], up[2], up[3], xhat, rstd, y, mod, lng]
    else:
        in_specs = [row, row, col, row, modspec, vec]
        args = [up[0], xhat, rstd, y, mod, lng]
    return pl.pallas_call(
        body, grid=(cfg.nt,), name=name, in_specs=in_specs,
        out_specs=[row, row, pl.BlockSpec((None, 8, D), lambda i: (cfg.seg(i), 0, 0))],
        out_shape=[jax.ShapeDtypeStruct((cfg.T, D), F32), jax.ShapeDtypeStruct((cfg.T, D), BF16),
                   jax.ShapeDtypeStruct((3, 8, D), F32)],
        compiler_params=_params(("arbitrary",)))(*args)


def _modulate_bwd(cfg, dres, dxin, h, mod, scale_idx, name):
    tm = cfg.tm
    n_lt = 2 * cfg.nlt

    def body(dres_ref, dxin_ref, h_ref, mod_ref, dh_ref, st_ref):
        i = pl.program_id(0)

        @pl.when(cfg.first_of_seg(i))
        def _():
            st_ref[...] = jnp.zeros_like(st_ref)

        dxin = dxin_ref[...].astype(F32)
        sc = mod_ref[scale_idx:scale_idx + 1, :]

        @pl.when(i < n_lt)
        def _():
            dh_ref[...] = dres_ref[...] + dxin * (1.0 + sc)

        st_ref[3:4, :] += jnp.sum(dxin * h_ref[...], axis=0, keepdims=True)
        st_ref[4:5, :] += jnp.sum(dxin, axis=0, keepdims=True)

    row = pl.BlockSpec((tm, D), lambda i: (i, 0))
    return pl.pallas_call(
        body, grid=(cfg.nt,), name=name,
        in_specs=[row, row, row, pl.BlockSpec((None, N_MOD, D), lambda i: (cfg.seg(i), 0, 0))],
        out_specs=[pl.BlockSpec((tm, D), lambda i: (jnp.minimum(i, n_lt - 1), 0)),
                   pl.BlockSpec((None, 8, D), lambda i: (cfg.seg(i), 0, 0))],
        out_shape=[jax.ShapeDtypeStruct((cfg.t_lat, D), F32), jax.ShapeDtypeStruct((3, 8, D), F32)],
        compiler_params=_params(("arbitrary",)))(dres, dxin, h, mod)


def _loss(cfg, h, target, name):
    tm = cfg.tm
    n_lt = 2 * cfg.nlt

    def body(h_ref, t_ref, dy_ref, l_ref):
        i = pl.program_id(0)

        @pl.when(i == 0)
        def _():
            l_ref[...] = jnp.zeros_like(l_ref)

        @pl.when(i < n_lt)
        def _():
            err = h_ref[...] - t_ref[...]
            dy_ref[...] = err * (1.0 / D)
            part = jnp.sum(jnp.sum(err * err, axis=1, keepdims=True), axis=0, keepdims=True) * (0.5 / D)
            l_ref[...] += jnp.broadcast_to(part, l_ref.shape)

        @pl.when(i >= n_lt)
        def _():
            dy_ref[...] = jnp.zeros_like(dy_ref)

    return pl.pallas_call(
        body, grid=(cfg.nt,), name=name,
        in_specs=[pl.BlockSpec((tm, D), lambda i: (i, 0)),
                  pl.BlockSpec((tm, D), lambda i: (jnp.minimum(i, n_lt - 1), 0))],
        out_specs=[pl.BlockSpec((tm, D), lambda i: (i, 0)), pl.BlockSpec((8, 128), lambda i: (0, 0))],
        out_shape=[jax.ShapeDtypeStruct((cfg.T, D), F32), jax.ShapeDtypeStruct((8, 128), F32)],
        compiler_params=_params(("arbitrary",)))(h, target)


def _matmul(a, b, mode, out_dtype, name, bm_cap=1536, bn_cap=1408, bk_cap=1024):
    if mode == "nn":
        (M, K), N = a.shape, b.shape[1]
    elif mode == "nt":
        (M, K), N = a.shape, b.shape[0]
    else:
        (K, M), N = a.shape, b.shape[1]
    bm, bn, bk = _pick(M, bm_cap), _pick(N, bn_cap), _pick(K, bk_cap)
    nk = K // bk

    def body(a_ref, b_ref, o_ref, acc_ref=None):
        k = pl.program_id(2)
        if mode == "nn":
            part = _dot(a_ref[...], b_ref[...])
        elif mode == "nt":
            part = _dot_nt(a_ref[...], b_ref[...])
        else:
            part = _dot_tn(a_ref[...], b_ref[...])
        if nk == 1:
            o_ref[...] = part.astype(out_dtype)
            return

        @pl.when(k == 0)
        def _():
            acc_ref[...] = part

        @pl.when((k > 0) & (k < nk - 1))
        def _():
            acc_ref[...] += part

        @pl.when(k == nk - 1)
        def _():
            o_ref[...] = (acc_ref[...] + part).astype(out_dtype)

    if mode == "nn":
        a_spec = pl.BlockSpec((bm, bk), lambda i, j, k: (i, k))
        b_spec = pl.BlockSpec((bk, bn), lambda i, j, k: (k, j))
    elif mode == "nt":
        a_spec = pl.BlockSpec((bm, bk), lambda i, j, k: (i, k))
        b_spec = pl.BlockSpec((bn, bk), lambda i, j, k: (j, k))
    else:
        a_spec = pl.BlockSpec((bk, bm), lambda i, j, k: (k, i))
        b_spec = pl.BlockSpec((bk, bn), lambda i, j, k: (k, j))
    return pl.pallas_call(
        body, grid=(M // bm, N // bn, nk), name=name, in_specs=[a_spec, b_spec],
        out_specs=pl.BlockSpec((bm, bn), lambda i, j, k: (i, j)),
        out_shape=jax.ShapeDtypeStruct((M, N), out_dtype),
        scratch_shapes=[pltpu.VMEM((bm, bn), F32)] if nk > 1 else [],
        compiler_params=_params(("parallel", "parallel", "arbitrary")))(a, b)


def _ffn_tile(T, cap):
    best = 256
    for t in range(256, cap + 1, 256):
        if T % t == 0:
            best = t
    return best


def _ffn_fwd(xin, wf, name):
    T = xin.shape[0]
    F = wf.shape[1]
    tm, tf = _ffn_tile(T, 768), F // 2
    assert tf % 128 == 0 and T % tm == 0

    def body(x_ref, wg_ref, wu_ref, wd_ref, g_ref, u_ref, y_ref, acc_ref):
        j = pl.program_id(1)
        x = x_ref[...]
        acc = None
        for c0, cw in _chunks(tf):
            g = _dot_nt(x, wg_ref[c0:c0 + cw, :])
            u = _dot_nt(x, wu_ref[c0:c0 + cw, :])
            g_ref[:, c0:c0 + cw] = g.astype(BF16)
            u_ref[:, c0:c0 + cw] = u.astype(BF16)
            part = _dot(g * _sigmoid(g) * u, wd_ref[c0:c0 + cw, :])
            acc = part if acc is None else acc + part

        @pl.when(j == 0)
        def _():
            acc_ref[...] = acc

        @pl.when(j == 1)
        def _():
            y_ref[...] = (acc_ref[...] + acc).astype(BF16)

    return pl.pallas_call(
        body, grid=(T // tm, 2), name=name,
        in_specs=[pl.BlockSpec((tm, D), lambda i, j: (i, 0)),
                  pl.BlockSpec((None, tf, D), lambda i, j: (0, j, 0)),
                  pl.BlockSpec((None, tf, D), lambda i, j: (1, j, 0)),
                  pl.BlockSpec((None, tf, D), lambda i, j: (2, j, 0))],
        out_specs=[pl.BlockSpec((tm, tf), lambda i, j: (i, j)),
                   pl.BlockSpec((tm, tf), lambda i, j: (i, j)),
                   pl.BlockSpec((tm, D), lambda i, j: (i, 0))],
        out_shape=[jax.ShapeDtypeStruct((T, F), BF16), jax.ShapeDtypeStruct((T, F), BF16),
                   jax.ShapeDtypeStruct((T, D), BF16)],
        scratch_shapes=[pltpu.VMEM((tm, D), F32)],
        compiler_params=_params(("parallel", "arbitrary")))(xin, wf, wf, wf)


def _ffn_bwd(dys, g, u, wf, name):
    T = dys.shape[0]
    F = wf.shape[1]
    tm, tf = _ffn_tile(T, 512), F // 2

    def body(dy_ref, g_ref, u_ref, wg_ref, wu_ref, wd_ref, dg_ref, du_ref, a_ref, dx_ref, acc_ref):
        j = pl.program_id(1)
        da_all = _dot_nt(dy_ref[...], wd_ref[...])
        for c0, cw in _chunks(tf):
            gg = g_ref[:, c0:c0 + cw].astype(F32)
            uu = u_ref[:, c0:c0 + cw].astype(F32)
            da = da_all[:, c0:c0 + cw]
            s = _sigmoid(gg)
            silu = gg * s
            a_ref[:, c0:c0 + cw] = (silu * uu).astype(BF16)
            du_ref[:, c0:c0 + cw] = (da * silu).astype(BF16)
            dg_ref[:, c0:c0 + cw] = (da * uu * (s * (1.0 + gg * (1.0 - s)))).astype(BF16)
        acc = _dot(dg_ref[...], wg_ref[...]) + _dot(du_ref[...], wu_ref[...])

        @pl.when(j == 0)
        def _():
            acc_ref[...] = acc

        @pl.when(j == 1)
        def _():
            dx_ref[...] = (acc_ref[...] + acc).astype(BF16)

    blk = pl.BlockSpec((tm, tf), lambda i, j: (i, j))
    return pl.pallas_call(
        body, grid=(T // tm, 2), name=name,
        in_specs=[pl.BlockSpec((tm, D), lambda i, j: (i, 0)), blk, blk,
                  pl.BlockSpec((None, tf, D), lambda i, j: (0, j, 0)),
                  pl.BlockSpec((None, tf, D), lambda i, j: (1, j, 0)),
                  pl.BlockSpec((None, tf, D), lambda i, j: (2, j, 0))],
        out_specs=[blk, blk, blk, pl.BlockSpec((tm, D), lambda i, j: (i, 0))],
        out_shape=[jax.ShapeDtypeStruct((T, F), BF16), jax.ShapeDtypeStruct((T, F), BF16),
                   jax.ShapeDtypeStruct((T, F), BF16), jax.ShapeDtypeStruct((T, D), BF16)],
        scratch_shapes=[pltpu.VMEM((tm, D), F32)],
        compiler_params=_params(("parallel", "arbitrary")))(dys, g, u, wf, wf, wf)


def _ffn_dw(dg, du, a_act, xin, dys, name):
    T, F = dg.shape
    bm, bk = _pick(F, 1408), _pick(T, 1152)
    ni, nk = F // bm, T // bk
    assert nk >= 2

    def body(dg_ref, du_ref, a_ref, x_ref, dy_ref, o_ref, acc_ref):
        s, k = pl.program_id(0), pl.program_id(2)

        def product(lhs_ref, rhs_ref):
            part = _dot_tn(lhs_ref[...], rhs_ref[...])

            @pl.when(k == 0)
            def _():
                acc_ref[...] = part

            @pl.when((k > 0) & (k < nk - 1))
            def _():
                acc_ref[...] += part

            @pl.when(k == nk - 1)
            def _():
                o_ref[...] = (acc_ref[...] + part).astype(BF16)

        for which, (lhs_ref, rhs_ref) in enumerate(((dg_ref, x_ref), (du_ref, x_ref), (a_ref, dy_ref))):
            pl.when(s == which)(functools.partial(product, lhs_ref, rhs_ref))

    def lhs_spec(which):
        def index(s, i, k):
            before, after = s < which, s > which
            return (jnp.where(before, 0, jnp.where(after, nk - 1, k)), jnp.where(before, 0, jnp.where(after, ni - 1, i)))
        return pl.BlockSpec((bk, bm), index)

    def rhs_spec(first, last):
        def index(s, i, k):
            return (jnp.where(s < first, 0, jnp.where(s > last, nk - 1, k)), 0)
        return pl.BlockSpec((bk, D), index)

    return pl.pallas_call(
        body, grid=(3, ni, nk), name=name,
        in_specs=[lhs_spec(0), lhs_spec(1), lhs_spec(2), rhs_spec(0, 1), rhs_spec(2, 2)],
        out_specs=pl.BlockSpec((None, bm, D), lambda s, i, k: (s, i, 0)),
        out_shape=jax.ShapeDtypeStruct((3, F, D), BF16),
        scratch_shapes=[pltpu.VMEM((bm, D), F32)],
        compiler_params=_params(("arbitrary", "arbitrary", "arbitrary")))(dg, du, a_act, xin, dys)


def _swap_halves(x):
    w = x.shape[1]
    lane = lax.broadcasted_iota(jnp.int32, (1, w), 1)
    return jnp.where((lane & 63) < 32, pltpu.roll(x, w - 32, 1), pltpu.roll(x, 32, 1))


def _rope(x, cos, sin):
    return x * cos + _swap_halves(x) * sin


def _rope_t(dy, cos, sin):
    return dy * cos + _swap_halves(dy * sin)


def _rope_tables(n_lat):
    rows = n_lat // GRID_W
    row = jnp.repeat(jnp.arange(rows, dtype=F32), GRID_W)
    col = jnp.tile(jnp.arange(GRID_W, dtype=F32), rows)
    inv = ROPE_THETA ** (-jnp.arange(ROPE_FREQS, dtype=F32) / ROPE_FREQS)
    ang = jnp.concatenate([row[:, None] * inv, col[:, None] * inv], axis=-1)
    cs, sn = jnp.cos(ang), jnp.sin(ang)
    cos = jnp.concatenate([cs, cs, cs, cs], axis=-1)
    sin = jnp.concatenate([-sn, sn, -sn, sn], axis=-1)
    return cos, sin


def _attn_specs(cfg):
    n_lat, n_ctx, cb = cfg.n_lat, cfg.n_ctx, cfg.ctx_blk
    return [pl.BlockSpec((n_lat, ATT_W), lambda e: (e, 0)),
            pl.BlockSpec((n_lat, 128), lambda e: (e, 4)),
            pl.BlockSpec((n_lat, 128), lambda e: (e, 5)),
            pl.BlockSpec((n_ctx, ATT_W), lambda e: (cb + e, 0)),
            pl.BlockSpec((n_ctx, 128), lambda e: (cb + e, 4)),
            pl.BlockSpec((n_ctx, 128), lambda e: (cb + e, 5)),
            pl.BlockSpec((n_lat, 128), lambda e: (0, 0)),
            pl.BlockSpec((n_lat, 128), lambda e: (0, 0)),
            pl.BlockSpec((8, 128), lambda e: (0, 0))]


def _attn_prepare(kh, kl, vl, kc, vc, ka, kb, va, vb, kca, kcb, vca, vcb):
    lane = lax.broadcasted_iota(jnp.int32, (1, 128), 1)
    own = (lane < 64) if kh == 0 else (lane >= 64)

    def split(x, ra, rb):
        mine = jnp.where(own, x, 0.0)
        other = pltpu.roll(mine, 64, 1)
        a, b = (mine, other) if kh == 0 else (other, mine)
        ra[...] = a.astype(BF16)
        rb[...] = b.astype(BF16)

    split(kl, ka, kb)
    split(vl, va, vb)
    split(kc, kca, kcb)
    split(vc, vca, vcb)


def _softmax_parts(s_list, sk):
    m = sk
    for s in s_list:
        m = jnp.maximum(m, jnp.max(s, axis=1, keepdims=True))
    es = [jnp.exp(s - m) for s in s_list]
    esk = jnp.exp(sk - m)
    den = esk
    for e in es:
        den = den + jnp.sum(e, axis=1, keepdims=True)
    inv = 1.0 / den
    return [e * inv for e in es], esk * inv


def _window(cfg, n):
    r0 = pl.multiple_of(n * BLK, BLK)
    start = pl.multiple_of(jnp.clip((n - 1) * BLK, 0, cfg.n_lat - 3 * BLK), BLK)
    qpos = r0 + lax.broadcasted_iota(jnp.int32, (BLK, 1), 0)
    kpos = start + lax.broadcasted_iota(jnp.int32, (1, 3 * BLK), 1)
    valid = jnp.abs(qpos - kpos) <= BLK
    return r0, start, valid


def _attn_fwd(cfg, p, cos, sin, sink_rows, name):
    n_lat, n_ctx = cfg.n_lat, cfg.n_ctx

    def body(q_ref, k_ref, v_ref, qc_ref, kc_ref, vc_ref, cos_ref, sin_ref, sink_ref, o_ref, oc_ref,
             qr, ka, kb, va, vb, kca, kcb, vca, vcb):
        cos_t, sin_t = cos_ref[...], sin_ref[...]
        for gq in range(4):
            qr[:, gq * 128:(gq + 1) * 128] = _rope(q_ref[:, gq * 128:(gq + 1) * 128].astype(F32), cos_t, sin_t).astype(BF16)
        kl = _rope(k_ref[...].astype(F32), cos_t, sin_t)
        for kh in range(KV_HEADS):
            _attn_prepare(kh, kl, v_ref[...].astype(F32), kc_ref[...].astype(F32), vc_ref[...].astype(F32),
                          ka, kb, va, vb, kca, kcb, vca, vcb)

            def lat_block(n, carry):
                r0, start, valid = _window(cfg, n)
                win = pl.ds(start, 3 * BLK)
                lanes = [slice((kh * 2 + pr) * 128, (kh * 2 + pr + 1) * 128) for pr in range(2)]
                qps = [qr[pl.ds(r0, BLK), lanes[pr]] for pr in range(2)]
                kws, kcs = (ka[win, :], kb[win, :]), (kca[...], kcb[...])
                scores = [(jnp.where(valid, _dot_nt(qps[pr], kws[half]) * ATT_SCALE, NEG_INF),
                           _dot_nt(qps[pr], kcs[half]) * ATT_SCALE) for pr in range(2) for half in range(2)]
                probs = []
                for idx, (s_w, s_c) in enumerate(scores):
                    head = kh * 4 + idx
                    (p_w, p_c), _ = _softmax_parts([s_w, s_c], sink_ref[head:head + 1, 0:1])
                    probs.append((p_w.astype(BF16), p_c.astype(BF16)))
                vws, vcs = (va[win, :], vb[win, :]), (vca[...], vcb[...])
                for pr in range(2):
                    o = (_dot(probs[2 * pr][0], vws[0]) + _dot(probs[2 * pr][1], vcs[0])
                         + _dot(probs[2 * pr + 1][0], vws[1]) + _dot(probs[2 * pr + 1][1], vcs[1]))
                    o_ref[pl.ds(r0, BLK), lanes[pr]] = o.astype(BF16)
                return carry

            lax.fori_loop(0, n_lat // BLK, lat_block, 0, unroll=2)
            for n in range(n_ctx // BLK):
                rows = slice(n * BLK, (n + 1) * BLK)
                for pr in range(2):
                    lanes = slice((kh * 2 + pr) * 128, (kh * 2 + pr + 1) * 128)
                    qp = qc_ref[rows, lanes]
                    o = None
                    for half, (kcx, vcx) in enumerate(((kca, vca), (kcb, vcb))):
                        head = kh * 4 + pr * 2 + half
                        s_c = _dot_nt(qp, kcx[...]) * ATT_SCALE
                        (p_c,), _ = _softmax_parts([s_c], sink_ref[head:head + 1, 0:1])
                        part = _dot(p_c, vcx[...])
                        o = part if o is None else o + part
                    oc_ref[rows, lanes] = o.astype(BF16)

    return pl.pallas_call(
        body, grid=(2,), name=name, in_specs=_attn_specs(cfg),
        out_specs=[pl.BlockSpec((n_lat, ATT_W), lambda e: (e, 0)), pl.BlockSpec((n_ctx, ATT_W), lambda e: (e, 0))],
        out_shape=[jax.ShapeDtypeStruct((cfg.t_lat, ATT_W), BF16), jax.ShapeDtypeStruct((cfg.t_ctx, ATT_W), BF16)],
        scratch_shapes=[pltpu.VMEM((n_lat, ATT_W), BF16)] + [pltpu.VMEM((n_lat, 128), BF16)] * 4
        + [pltpu.VMEM((n_ctx, 128), BF16)] * 4,
        compiler_params=_params(("parallel",)))(p, p, p, p, p, p, cos, sin, sink_rows)


def _attn_bwd(cfg, p, dcat, cos, sin, sink_rows, name):
    n_lat, n_ctx, cb = cfg.n_lat, cfg.n_ctx, cfg.ctx_blk

    def body(q_ref, k_ref, v_ref, qc_ref, kc_ref, vc_ref, cos_ref, sin_ref, sink_ref, do_ref, doc_ref,
             dq_ref, dk_ref, dv_ref, dqc_ref, dkc_ref, dvc_ref, dsink_ref,
             qr, ka, kb, va, vb, kca, kcb, vca, vcb, dqs, dka, dva, dkca, dvca):
        cos_t, sin_t = cos_ref[...], sin_ref[...]
        lane = lax.broadcasted_iota(jnp.int32, (1, 128), 1)
        lo = lane < 64
        for gq in range(4):
            qr[:, gq * 128:(gq + 1) * 128] = _rope(q_ref[:, gq * 128:(gq + 1) * 128].astype(F32), cos_t, sin_t).astype(BF16)
        kl = _rope(k_ref[...].astype(F32), cos_t, sin_t)
        dsink_ref[...] = jnp.zeros_like(dsink_ref)
        dka[...] = jnp.zeros_like(dka)
        dva[...] = jnp.zeros_like(dva)
        dkca[...] = jnp.zeros_like(dkca)
        dvca[...] = jnp.zeros_like(dvca)

        def halves(x):
            return jnp.where(lo, x, 0).astype(BF16), jnp.where(lo, 0, x).astype(BF16)

        for kh in range(KV_HEADS):
            _attn_prepare(kh, kl, v_ref[...].astype(F32), kc_ref[...].astype(F32), vc_ref[...].astype(F32),
                          ka, kb, va, vb, kca, kcb, vca, vcb)

            def one_head(head, qp, q_half, do_p, do_half, kw, kcx, vw, vcx, win, valid):
                sk = sink_ref[head:head + 1, 0:1]
                s_list = [_dot_nt(qp, kcx[...]) * ATT_SCALE]
                if win is not None:
                    s_list.insert(0, jnp.where(valid, _dot_nt(qp, kw[win, :]) * ATT_SCALE, NEG_INF))
                probs, p_sink = _softmax_parts(s_list, sk)
                vals = [vcx[...]] if win is None else [vw[win, :], vcx[...]]
                dps = [_dot_nt(do_p, vv) for vv in vals]
                dr = None
                for pp, dp in zip(probs, dps):
                    t = jnp.sum(pp * dp, axis=1, keepdims=True)
                    dr = t if dr is None else dr + t
                dss = [(pp * (dp - dr) * ATT_SCALE).astype(BF16) for pp, dp in zip(probs, dps)]
                dsink_ref[head:head + 1, :] += jnp.broadcast_to(
                    jnp.sum(-p_sink * dr, axis=0, keepdims=True), (1, 128))
                p_c, ds_c = probs[-1], dss[-1]
                dq = _dot(ds_c, kcx[...])
                dkca[kh] += _dot_tn(ds_c, q_half)
                dvca[kh] += _dot_tn(p_c, do_half)
                if win is not None:
                    dq = dq + _dot(dss[0], kw[win, :])
                    dka[kh, win, :] += _dot_tn(dss[0], q_half)
                    dva[kh, win, :] += _dot_tn(probs[0], do_half)
                return dq

            def lat_block(n, carry):
                r0, start, valid = _window(cfg, n)
                win = pl.ds(start, 3 * BLK)
                lanes = [slice((kh * 2 + pr) * 128, (kh * 2 + pr + 1) * 128) for pr in range(2)]
                qps = [qr[pl.ds(r0, BLK), lanes[pr]] for pr in range(2)]
                dops = [do_ref[pl.ds(r0, BLK), lanes[pr]].astype(BF16) for pr in range(2)]
                heads = [(pr, half) for pr in range(2) for half in range(2)]
                kws, kcs = (ka[win, :], kb[win, :]), (kca[...], kcb[...])
                vws, vcs = (va[win, :], vb[win, :]), (vca[...], vcb[...])
                soft = []
                for idx, (pr, half) in enumerate(heads):
                    s_w = jnp.where(valid, _dot_nt(qps[pr], kws[half]) * ATT_SCALE, NEG_INF)
                    s_c = _dot_nt(qps[pr], kcs[half]) * ATT_SCALE
                    soft.append(_softmax_parts([s_w, s_c], sink_ref[kh * 4 + idx:kh * 4 + idx + 1, 0:1]))
                dps = [(_dot_nt(dops[pr], vws[half]), _dot_nt(dops[pr], vcs[half])) for pr, half in heads]
                ds_w, ds_c, pb_w, pb_c = [], [], [], []
                for idx in range(4):
                    (p_w, p_c), p_sink = soft[idx]
                    dp_w, dp_c = dps[idx]
                    dr = jnp.sum(p_w * dp_w, axis=1, keepdims=True) + jnp.sum(p_c * dp_c, axis=1, keepdims=True)
                    ds_w.append((p_w * (dp_w - dr) * ATT_SCALE).astype(BF16))
                    ds_c.append((p_c * (dp_c - dr) * ATT_SCALE).astype(BF16))
                    pb_w.append(p_w.astype(BF16))
                    pb_c.append(p_c.astype(BF16))
                    head = kh * 4 + idx
                    dsink_ref[head:head + 1, :] += jnp.broadcast_to(
                        jnp.sum(-p_sink * dr, axis=0, keepdims=True), (1, 128))
                for pr in range(2):
                    dqs[pl.ds(r0, BLK), lanes[pr]] = (
                        _dot(ds_w[2 * pr], kws[0]) + _dot(ds_c[2 * pr], kcs[0])
                        + _dot(ds_w[2 * pr + 1], kws[1]) + _dot(ds_c[2 * pr + 1], kcs[1]))
                q_hs, do_hs = [halves(qp) for qp in qps], [halves(do_p) for do_p in dops]
                q_stack = jnp.concatenate([q_hs[pr][half] for pr, half in heads], axis=0)
                do_stack = jnp.concatenate([do_hs[pr][half] for pr, half in heads], axis=0)
                dka[kh, win, :] += _dot_tn(jnp.concatenate(ds_w, axis=0), q_stack)
                dva[kh, win, :] += _dot_tn(jnp.concatenate(pb_w, axis=0), do_stack)
                dkca[kh] += _dot_tn(jnp.concatenate(ds_c, axis=0), q_stack)
                dvca[kh] += _dot_tn(jnp.concatenate(pb_c, axis=0), do_stack)
                return carry

            lax.fori_loop(0, n_lat // BLK, lat_block, 0, unroll=2)
            for n in range(n_ctx // BLK):
                rows = slice(n * BLK, (n + 1) * BLK)
                for pr in range(2):
                    lanes = slice((kh * 2 + pr) * 128, (kh * 2 + pr + 1) * 128)
                    qp = qc_ref[rows, lanes].astype(BF16)
                    do_p = doc_ref[rows, lanes]
                    q_h, do_h = halves(qp), halves(do_p)
                    dq = None
                    for half, (kcx, vcx) in enumerate(((kca, vca), (kcb, vcb))):
                        part = one_head(kh * 4 + pr * 2 + half, qp, q_h[half], do_p, do_h[half],
                                        None, kcx, None, vcx, None, None)
                        dq = part if dq is None else dq + part
                    dqc_ref[rows, lanes] = dq.astype(BF16)

        def fold(acc):
            r0 = acc[0] + pltpu.roll(acc[0], 64, 1)
            r1 = acc[1] + pltpu.roll(acc[1], 64, 1)
            return jnp.where(lo, r0, r1)

        for gq in range(4):
            sl = slice(gq * 128, (gq + 1) * 128)
            dq_ref[:, sl] = _rope_t(dqs[:, sl], cos_t, sin_t).astype(BF16)
        dk_ref[...] = _rope_t(fold(dka), cos_t, sin_t).astype(BF16)
        dv_ref[...] = fold(dva).astype(BF16)
        dkc_ref[...] = fold(dkca).astype(BF16)
        dvc_ref[...] = fold(dvca).astype(BF16)

    lat = lambda w: pl.BlockSpec((n_lat, w), lambda e: (e, 0))
    ctx = lambda w: pl.BlockSpec((n_ctx, w), lambda e: (e, 0))
    sd = jax.ShapeDtypeStruct
    return pl.pallas_call(
        body, grid=(2,), name=name,
        in_specs=_attn_specs(cfg) + [pl.BlockSpec((n_lat, ATT_W), lambda e: (e, 0)),
                                     pl.BlockSpec((n_ctx, ATT_W), lambda e: (cb + e, 0))],
        out_specs=[lat(ATT_W), lat(128), lat(128), ctx(ATT_W), ctx(128), ctx(128),
                   pl.BlockSpec((None, 8, 128), lambda e: (e, 0, 0))],
        out_shape=[sd((cfg.t_lat, ATT_W), BF16), sd((cfg.t_lat, 128), BF16), sd((cfg.t_lat, 128), BF16),
                   sd((cfg.t_ctx, ATT_W), BF16), sd((cfg.t_ctx, 128), BF16), sd((cfg.t_ctx, 128), BF16),
                   sd((2, 8, 128), F32)],
        scratch_shapes=[pltpu.VMEM((n_lat, ATT_W), BF16)] + [pltpu.VMEM((n_lat, 128), BF16)] * 4
        + [pltpu.VMEM((n_ctx, 128), BF16)] * 4
        + [pltpu.VMEM((n_lat, ATT_W), F32), pltpu.VMEM((2, n_lat, 128), F32), pltpu.VMEM((2, n_lat, 128), F32),
           pltpu.VMEM((2, n_ctx, 128), F32), pltpu.VMEM((2, n_ctx, 128), F32)],
        compiler_params=_params(("parallel",)))(p, p, p, p, p, p, cos, sin, sink_rows, dcat, dcat)


def _shift_down(x, k, row):
    return jnp.where(row >= k, pltpu.roll(x, k, 0), 0.0)


def _shift_up(x, k, row):
    n = x.shape[0]
    return jnp.where(row < n - k, pltpu.roll(x, n - k, 0), 0.0)


def _window_sum(x, r, row):
    below, above, k = x, x, 1
    while k < r:
        below = below + _shift_down(below, k, row)
        above = above + _shift_up(above, k, row)
        k *= 2
    return below + _shift_down(x, r, row) + _shift_up(above, 1, row)


def _inv_count(r, row, n):
    cnt = jnp.minimum(row + r, n - 1) + 1 - jnp.maximum(row - r, 0)
    return 1.0 / cnt.astype(F32)


def _pool_fwd(p, w, scale, n, blk0, n_seg, name):
    def body(u0, u1, u2, u3, w_ref, sc_ref, o_ref):
        row = lax.broadcasted_iota(jnp.int32, (n, 1), 0)
        for g, u_ref in enumerate((u0, u1, u2, u3)):
            u = u_ref[...].astype(F32)
            d = _window_sum(u, POOL_R[g], row) * _inv_count(POOL_R[g], row, n) - u
            o_ref[:, g * 128:(g + 1) * 128] = (_dot(d, w_ref[g]) * sc_ref[:, g * 128:(g + 1) * 128]).astype(BF16)

    return pl.pallas_call(
        body, grid=(n_seg,), name=name,
        in_specs=[pl.BlockSpec((n, 128), functools.partial(lambda g, e: (blk0 + e, 6 + g), g)) for g in range(4)]
        + [pl.BlockSpec((4, 128, 128), lambda e: (0, 0, 0)), pl.BlockSpec((1, 512), lambda e: (0, 0))],
        out_specs=pl.BlockSpec((n, 512), lambda e: (e, 0)),
        out_shape=jax.ShapeDtypeStruct((n_seg * n, 512), BF16),
        compiler_params=_params(("parallel",)))(p, p, p, p, w, scale)


def _pool_bwd(p, w, scale, dcat, n, blk0, n_seg, name):
    def body(u0, u1, u2, u3, w_ref, sc_ref, dp_ref, du_ref, dw_ref, dsc_ref):
        e = pl.program_id(0)

        @pl.when(e == 0)
        def _():
            dw_ref[...] = jnp.zeros_like(dw_ref)
            dsc_ref[...] = jnp.zeros_like(dsc_ref)

        row = lax.broadcasted_iota(jnp.int32, (n, 1), 0)
        for g, u_ref in enumerate((u0, u1, u2, u3)):
            sl = slice(g * 128, (g + 1) * 128)
            u = u_ref[...].astype(F32)
            inv = _inv_count(POOL_R[g], row, n)
            d = _window_sum(u, POOL_R[g], row) * inv - u
            dp = dp_ref[:, sl].astype(F32)
            dsc_ref[:, sl] += jnp.sum(dp * _dot(d, w_ref[g]), axis=0, keepdims=True)
            dyp = dp * sc_ref[:, sl]
            dw_ref[g] += _dot_tn(d, dyp)
            dd = _dot_nt(dyp, w_ref[g])
            du_ref[:, sl] = (_window_sum(dd * inv, POOL_R[g], row) - dd).astype(BF16)

    return pl.pallas_call(
        body, grid=(n_seg,), name=name,
        in_specs=[pl.BlockSpec((n, 128), functools.partial(lambda g, e: (blk0 + e, 6 + g), g)) for g in range(4)]
        + [pl.BlockSpec((4, 128, 128), lambda e: (0, 0, 0)), pl.BlockSpec((1, 512), lambda e: (0, 0)),
           pl.BlockSpec((n, 512), lambda e: (blk0 + e, 1))],
        out_specs=[pl.BlockSpec((n, 512), lambda e: (e, 0)),
                   pl.BlockSpec((4, 128, 128), lambda e: (0, 0, 0)), pl.BlockSpec((1, 512), lambda e: (0, 0))],
        out_shape=[jax.ShapeDtypeStruct((n_seg * n, 512), BF16), jax.ShapeDtypeStruct((4, 128, 128), F32),
                   jax.ShapeDtypeStruct((1, 512), F32)],
        compiler_params=_params(("arbitrary",)))(p, p, p, p, w, scale, dcat)


def _gelu(x):
    t = jnp.tanh(math.sqrt(2.0 / math.pi) * (x + 0.044715 * x * x * x))
    return 0.5 * x * (1.0 + t), t


def _gelu_grad(x, t):
    return 0.5 * (1.0 + t) + 0.5 * x * (1.0 - t * t) * (math.sqrt(2.0 / math.pi) * (1.0 + 3 * 0.044715 * x * x))


def _neg_expm1_twice(x):
    t = jnp.tanh(x)
    return (-2.0 * t) / (1.0 - t)


def _softplus_neg(lam):
    x = -lam
    e = jnp.exp(-jnp.abs(x))
    log1p = jnp.where(e < 1e-2, e * (1.0 - e * (0.5 - e * (1.0 / 3.0))), jnp.log(1.0 + e))
    return jnp.maximum(x, 0.0) + log1p, -_sigmoid(x)


def _conv(u, w_ref, b_ref, row):
    return (b_ref[...] + _shift_down(u, 1, row) * w_ref[0:1, :] + u * w_ref[1:2, :]
            + _shift_up(u, 1, row) * w_ref[2:3, :] + _shift_up(u, 2, row) * w_ref[3:4, :])


def _lru_gates(uc, d, wa_ref, ba_ref, wx_ref, bx_ref, lam_ref):
    r = _sigmoid(_dot(uc, wa_ref[d]) + ba_ref[d:d + 1, :])
    gi = _sigmoid(_dot(uc, wx_ref[d]) + bx_ref[d:d + 1, :])
    sp, dsp = _softplus_neg(lam_ref[d:d + 1, :])
    la = (-LRU_C) * r * sp
    a = jnp.exp(la)
    sq = jnp.sqrt(_neg_expm1_twice(la))
    return r, gi, sp, dsp, a, sq


def _tile_scan(a_ref, b_ref, n, reverse):
    m = n // 8
    first = 7 if reverse else 0
    a_prev = a_ref[pl.ds(first, m, stride=8), :]
    b_prev = b_ref[pl.ds(first, m, stride=8), :]
    for j in (range(6, -1, -1) if reverse else range(1, 8)):
        rows = pl.ds(j, m, stride=8)
        aj = a_ref[rows, :]
        b_prev = aj * b_prev + b_ref[rows, :]
        a_prev = aj * a_prev
        b_ref[rows, :] = b_prev
        a_ref[rows, :] = a_prev


def _carry_scan(a_ref, b_ref, n, reverse, carry):
    nt8 = n // 8

    def step(i, c):
        t = (nt8 - 1 - i) if reverse else i
        off = pl.multiple_of(t * 8, 8)
        h = a_ref[pl.ds(off, 8), :] * c + b_ref[pl.ds(off, 8), :]
        b_ref[pl.ds(off, 8), :] = h
        return h[0:1, :] if reverse else h[7:8, :]

    return lax.fori_loop(0, nt8, step, carry, unroll=4)


def _chain_scan(segs, reverse):
    carry = jnp.zeros((1, 128), F32)
    for a, b, a_ref, b_ref, n in segs:
        a_ref[...] = a
        b_ref[...] = b
        _tile_scan(a_ref, b_ref, n, reverse)
        carry = _carry_scan(a_ref, b_ref, n, reverse, carry)


def _lru_specs(cfg):
    n_lat, n_ctx, cb = cfg.n_lat, cfg.n_ctx, cfg.ctx_blk
    return [pl.BlockSpec((n_lat, 128), lambda hb, e: (e, hb)),
            pl.BlockSpec((n_lat, 128), lambda hb, e: (e, 8 + hb)),
            pl.BlockSpec((n_ctx, 128), lambda hb, e: (cb + e, hb)),
            pl.BlockSpec((n_ctx, 128), lambda hb, e: (cb + e, 8 + hb)),
            pl.BlockSpec((4, 128), lambda hb, e: (0, hb)),
            pl.BlockSpec((1, 128), lambda hb, e: (0, hb)),
            pl.BlockSpec((2, None, 128, 128), lambda hb, e: (0, hb, 0, 0)),
            pl.BlockSpec((2, 128), lambda hb, e: (0, hb)),
            pl.BlockSpec((2, None, 128, 128), lambda hb, e: (0, hb, 0, 0)),
            pl.BlockSpec((2, 128), lambda hb, e: (0, hb)),
            pl.BlockSpec((2, 128), lambda hb, e: (0, hb))]


def _lru_fwd(cfg, p, consts, name):
    n_lat, n_ctx = cfg.n_lat, cfg.n_ctx

    def body(gl_ref, ul_ref, gc_ref, uc_ref, cw_ref, cb_ref, wa_ref, ba_ref, wx_ref, bx_ref, lam_ref,
             zl_ref, zc_ref, hl_ref, hc_ref, al, ac):
        row_l = lax.broadcasted_iota(jnp.int32, (n_lat, 1), 0)
        row_c = lax.broadcasted_iota(jnp.int32, (n_ctx, 1), 0)
        uc_l = _conv(ul_ref[...].astype(F32), cw_ref, cb_ref, row_l)
        uc_c = _conv(uc_ref[...].astype(F32), cw_ref, cb_ref, row_c)
        for d in range(2):
            _, gi_l, _, _, a_l, sq_l = _lru_gates(uc_l, d, wa_ref, ba_ref, wx_ref, bx_ref, lam_ref)
            _, gi_c, _, _, a_c, sq_c = _lru_gates(uc_c, d, wa_ref, ba_ref, wx_ref, bx_ref, lam_ref)
            _chain_scan([(a_c, sq_c * (gi_c * uc_c), ac, hc_ref.at[d], n_ctx),
                         (a_l, sq_l * (gi_l * uc_l), al, hl_ref.at[d], n_lat)], reverse=(d == 1))
        zl_ref[...] = (_gelu(gl_ref[...].astype(F32))[0] * (hl_ref[0] + hl_ref[1])).astype(BF16)
        zc_ref[...] = (_gelu(gc_ref[...].astype(F32))[0] * (hc_ref[0] + hc_ref[1])).astype(BF16)

    return pl.pallas_call(
        body, grid=(8, 2), name=name, in_specs=_lru_specs(cfg),
        out_specs=[pl.BlockSpec((n_lat, 128), lambda hb, e: (e, hb)), pl.BlockSpec((n_ctx, 128), lambda hb, e: (e, hb)),
                   pl.BlockSpec((2, n_lat, 128), lambda hb, e: (0, e, hb)),
                   pl.BlockSpec((2, n_ctx, 128), lambda hb, e: (0, e, hb))],
        out_shape=[jax.ShapeDtypeStruct((cfg.t_lat, D), BF16), jax.ShapeDtypeStruct((cfg.t_ctx, D), BF16),
                   jax.ShapeDtypeStruct((2, cfg.t_lat, D), F32), jax.ShapeDtypeStruct((2, cfg.t_ctx, D), F32)],
        scratch_shapes=[pltpu.VMEM((n_lat, 128), F32), pltpu.VMEM((n_ctx, 128), F32)],
        compiler_params=_params(("parallel", "arbitrary")))(p, p, p, p, *consts)


def _lru_bwd(cfg, p, dz, h_lat, h_ctx, consts, name):
    n_lat, n_ctx, cb = cfg.n_lat, cfg.n_ctx, cfg.ctx_blk

    def body(gl_ref, ul_ref, gc_ref, uc_ref, cw_ref, cb_ref, wa_ref, ba_ref, wx_ref, bx_ref, lam_ref,
             dzl_ref, dzc_ref, hl, hc, dgl_ref, dul_ref, dgc_ref, duc_ref, dwa_ref, dwx_ref, vec_ref,
             al, bl, ac, bc):
        e = pl.program_id(1)

        @pl.when(e == 0)
        def _():
            dwa_ref[...] = jnp.zeros_like(dwa_ref)
            dwx_ref[...] = jnp.zeros_like(dwx_ref)
            vec_ref[...] = jnp.zeros_like(vec_ref)

        row_l = lax.broadcasted_iota(jnp.int32, (n_lat, 1), 0)
        row_c = lax.broadcasted_iota(jnp.int32, (n_ctx, 1), 0)
        u_l, u_c = ul_ref[...].astype(F32), uc_ref[...].astype(F32)
        uc_l = _conv(u_l, cw_ref, cb_ref, row_l)
        uc_c = _conv(u_c, cw_ref, cb_ref, row_c)
        gel_l, t_l = _gelu(gl_ref[...].astype(F32))
        gel_c, t_c = _gelu(gc_ref[...].astype(F32))
        dz_l, dz_c = dzl_ref[...].astype(F32), dzc_ref[...].astype(F32)
        dgl_ref[...] = (dz_l * (hl[0] + hl[1]) * _gelu_grad(gl_ref[...].astype(F32), t_l)).astype(BF16)
        dgc_ref[...] = (dz_c * (hc[0] + hc[1]) * _gelu_grad(gc_ref[...].astype(F32), t_c)).astype(BF16)
        dy_l, dy_c = dz_l * gel_l, dz_c * gel_c
        duc_l = jnp.zeros((n_lat, 128), F32)
        duc_c = jnp.zeros((n_ctx, 128), F32)
        for d in range(2):
            r_l, gi_l, sp, dsp, a_l, sq_l = _lru_gates(uc_l, d, wa_ref, ba_ref, wx_ref, bx_ref, lam_ref)
            r_c, gi_c, _, _, a_c, sq_c = _lru_gates(uc_c, d, wa_ref, ba_ref, wx_ref, bx_ref, lam_ref)
            if d == 0:
                an_l = _shift_up(a_l, 1, row_l)
                an_c = jnp.where(row_c < n_ctx - 1, pltpu.roll(a_c, n_ctx - 1, 0), a_l[0:1, :])
            else:
                an_l = _shift_down(a_l, 1, row_l)
                an_c = jnp.where(row_c >= 1, pltpu.roll(a_c, 1, 0), a_l[n_lat - 1:n_lat, :])
            _chain_scan([(an_l, dy_l, al, bl, n_lat), (an_c, dy_c, ac, bc, n_ctx)], reverse=(d == 0))
            dsp_sum = jnp.zeros((1, 128), F32)
            for (dh, h, r, gi, a, sq, uc, seg) in ((bl[...], hl[d], r_l, gi_l, a_l, sq_l, uc_l, "l"),
                                                  (bc[...], hc[d], r_c, gi_c, a_c, sq_c, uc_c, "c")):
                b0 = sq * (gi * uc)
                t1 = dh * sq
                dla = dh * (h - b0) - (dh * gi * uc) * (a * a) / sq
                dzr = (dla * ((-LRU_C) * sp)) * r * (1.0 - r)
                dzi = (t1 * uc) * gi * (1.0 - gi)
                dsp_sum = dsp_sum + jnp.sum(dla * ((-LRU_C) * r), axis=0, keepdims=True)
                dwa_ref[d] += _dot_tn(uc, dzr)
                dwx_ref[d] += _dot_tn(uc, dzi)
                vec_ref[d:d + 1, :] += jnp.sum(dzr, axis=0, keepdims=True)
                vec_ref[2 + d:3 + d, :] += jnp.sum(dzi, axis=0, keepdims=True)
                duc = t1 * gi + _dot_nt(dzr, wa_ref[d]) + _dot_nt(dzi, wx_ref[d])
                if seg == "l":
                    duc_l = duc_l + duc
                else:
                    duc_c = duc_c + duc
            vec_ref[4 + d:5 + d, :] += dsp_sum * dsp
        for duc, u, row, du_ref in ((duc_l, u_l, row_l, dul_ref), (duc_c, u_c, row_c, duc_ref)):
            du_ref[...] = (_shift_up(duc, 1, row) * cw_ref[0:1, :] + duc * cw_ref[1:2, :]
                           + _shift_down(duc, 1, row) * cw_ref[2:3, :]
                           + _shift_down(duc, 2, row) * cw_ref[3:4, :]).astype(BF16)
            vec_ref[6:7, :] += jnp.sum(duc * _shift_down(u, 1, row), axis=0, keepdims=True)
            vec_ref[7:8, :] += jnp.sum(duc * u, axis=0, keepdims=True)
            vec_ref[8:9, :] += jnp.sum(duc * _shift_up(u, 1, row), axis=0, keepdims=True)
            vec_ref[9:10, :] += jnp.sum(duc * _shift_up(u, 2, row), axis=0, keepdims=True)
            vec_ref[10:11, :] += jnp.sum(duc, axis=0, keepdims=True)

    lat = pl.BlockSpec((n_lat, 128), lambda hb, e: (e, hb))
    ctx = pl.BlockSpec((n_ctx, 128), lambda hb, e: (e, hb))
    wspec = pl.BlockSpec((2, None, 128, 128), lambda hb, e: (0, hb, 0, 0))
    sd = jax.ShapeDtypeStruct
    return pl.pallas_call(
        body, grid=(8, 2), name=name,
        in_specs=_lru_specs(cfg) + [pl.BlockSpec((n_lat, 128), lambda hb, e: (e, hb)),
                                    pl.BlockSpec((n_ctx, 128), lambda hb, e: (cb + e, hb)),
                                    pl.BlockSpec((2, n_lat, 128), lambda hb, e: (0, e, hb)),
                                    pl.BlockSpec((2, n_ctx, 128), lambda hb, e: (0, e, hb))],
        out_specs=[lat, lat, ctx, ctx, wspec, wspec, pl.BlockSpec((None, 16, 128), lambda hb, e: (hb, 0, 0))],
        out_shape=[sd((cfg.t_lat, D), BF16), sd((cfg.t_lat, D), BF16), sd((cfg.t_ctx, D), BF16), sd((cfg.t_ctx, D), BF16),
                   sd((2, 8, 128, 128), F32), sd((2, 8, 128, 128), F32), sd((8, 16, 128), F32)],
        scratch_shapes=[pltpu.VMEM((n_lat, 128), F32)] * 2 + [pltpu.VMEM((n_ctx, 128), F32)] * 2,
        compiler_params=_params(("parallel", "arbitrary")))(p, p, p, p, *consts, dz, dz, h_lat, h_ctx)


def _position():
    x, y, c = lax.axis_index("x"), lax.axis_index("y"), lax.axis_index("c")
    return x, y, c, 4 * x + 2 * y + c


def _peer(x, y, c, k):
    px = 1 - x if k & 4 else x
    py = 1 - y if k & 2 else y
    pc = 1 - c if k & 1 else c
    return (px, py, pc), 4 * px + 2 * py + pc


def _all_gather(v, name, in_vmem):
    def body(v_ref, o_ref, send_sems, recv_sems, local_sem):
        x, y, c, me = _position()
        mine = pltpu.make_async_copy(v_ref, o_ref.at[me], local_sem)
        mine.start()
        sends = []
        for k in range(1, N_DEV):
            peer, _ = _peer(x, y, c, k)
            cp = pltpu.make_async_remote_copy(src_ref=v_ref, dst_ref=o_ref.at[me], send_sem=send_sems.at[k - 1],
                                              recv_sem=recv_sems.at[k - 1], device_id=peer, device_id_type=MESH)
            cp.start()
            sends.append(cp)
        for k in range(1, N_DEV):
            peer, peer_lin = _peer(x, y, c, k)
            pltpu.make_async_remote_copy(src_ref=v_ref, dst_ref=o_ref.at[peer_lin], send_sem=send_sems.at[k - 1],
                                         recv_sem=recv_sems.at[k - 1], device_id=peer, device_id_type=MESH).wait_recv()
        for cp in sends:
            cp.wait_send()
        mine.wait()

    space = pltpu.VMEM if in_vmem else pl.ANY
    return pl.pallas_call(
        body, name=name,
        in_specs=[pl.BlockSpec(memory_space=space)], out_specs=pl.BlockSpec(memory_space=space),
        out_shape=jax.ShapeDtypeStruct((N_DEV,) + v.shape, v.dtype),
        scratch_shapes=[pltpu.SemaphoreType.DMA((N_DEV - 1,)), pltpu.SemaphoreType.DMA((N_DEV - 1,)),
                        pltpu.SemaphoreType.DMA],
        compiler_params=pltpu.CompilerParams(vmem_limit_bytes=VMEM_LIMIT))(v)


_HBM = pl.BlockSpec(memory_space=pltpu.HBM)
_SEM = pl.BlockSpec(memory_space=pltpu.SEMAPHORE)
_EFFECT = pltpu.SideEffectType.DATAFLOW_SIDE_EFFECTING


ALL_PEERS = tuple(range(1, N_DEV))
SAME_CORE_AND_SIBLING = (1, 2, 4, 6)


def _push_start(src, land, block_of, name, relations=ALL_PEERS):
    def body(src_ref, land_ref, send_sem, recv_sem, src_thru, land_thru, token):
        x, y, c, me = _position()
        for k in relations:
            peer, peer_lin = _peer(x, y, c, k)
            mine, there = block_of(src_ref, land_ref, me, peer_lin)
            pltpu.make_async_remote_copy(src_ref=mine, dst_ref=there, send_sem=send_sem, recv_sem=recv_sem,
                                         device_id=peer, device_id_type=MESH).start()
        mine, here = block_of(src_ref, land_ref, me, me)
        pltpu.make_async_copy(mine, here, recv_sem).start()
        token[...] = jnp.zeros_like(token)

    return pl.pallas_call(
        body, name=name,
        out_shape=(pltpu.SemaphoreType.DMA(()), pltpu.SemaphoreType.DMA(()), pltpu.HBM(src.shape, src.dtype),
                   pltpu.HBM(land.shape, land.dtype), jax.ShapeDtypeStruct((8, 128), F32)),
        in_specs=(_HBM, _HBM), out_specs=(_SEM, _SEM, _HBM, _HBM, pl.BlockSpec(memory_space=pltpu.VMEM)),
        input_output_aliases={0: 2, 1: 3},
        compiler_params=pltpu.CompilerParams(has_side_effects=_EFFECT),
    )(pltpu.with_memory_space_constraint(src, pltpu.HBM), pltpu.with_memory_space_constraint(land, pltpu.HBM))


def _push_wait(handle, blocks_of, after, name, n_peers=N_DEV - 1):
    send_sem, recv_sem, src_thru, land_thru, _ = handle

    def body(src_ref, land_ref, send_sem, recv_sem, after_ref, src_dead, got_ref):
        x, y, c, _ = _position()
        sent, landed = blocks_of(land_ref, n_peers), blocks_of(land_ref, n_peers + 1)
        pltpu.make_async_remote_copy(src_ref=sent, dst_ref=sent, send_sem=send_sem, recv_sem=recv_sem,
                                     device_id=(x, y, 1 - c), device_id_type=MESH).wait_send()
        pltpu.make_async_remote_copy(src_ref=landed, dst_ref=landed, send_sem=send_sem, recv_sem=recv_sem,
                                     device_id=(x, y, 1 - c), device_id_type=MESH).wait_recv()

    return pl.pallas_call(
        body, name=name,
        out_shape=(pltpu.HBM(src_thru.shape, src_thru.dtype), pltpu.HBM(land_thru.shape, land_thru.dtype)),
        in_specs=(_HBM, _HBM, _SEM, _SEM, pl.BlockSpec(memory_space=pl.ANY)), out_specs=(_HBM, _HBM),
        input_output_aliases={0: 0, 1: 1},
        compiler_params=pltpu.CompilerParams(has_side_effects=_EFFECT),
    )(src_thru, land_thru, send_sem, recv_sem, after)[1]


def _gather_start(src, name, relations=ALL_PEERS):
    g, r, C = src.shape
    land = lax.empty((g, N_DEV * r, C), src.dtype)
    return _push_start(src, land, lambda s, z, i, p: (s, z.at[:, pl.ds(i * r, r), :]), name, relations)


def _gather_wait(handle, after, name, n_peers=N_DEV - 1):
    r = handle[2].shape[1]
    return _push_wait(handle, lambda z, n: z.at[:, pl.ds(0, n * r), :], after, name, n_peers)


def _relay_start(land, r, name):
    def body(land_ref, send_sem, recv_sem, land_thru, token):
        x, y, c, _ = _position()
        for k in (2, 4, 6):
            _, origin = _peer(x, y, c, k)
            rows = land_ref.at[:, pl.ds(origin * r, r), :]
            pltpu.make_async_remote_copy(src_ref=rows, dst_ref=rows, send_sem=send_sem, recv_sem=recv_sem,
                                         device_id=(x, y, 1 - c), device_id_type=MESH).start()
        token[...] = jnp.zeros_like(token)

    return pl.pallas_call(
        body, name=name,
        out_shape=(pltpu.SemaphoreType.DMA(()), pltpu.SemaphoreType.DMA(()), pltpu.HBM(land.shape, land.dtype),
                   jax.ShapeDtypeStruct((8, 128), F32)),
        in_specs=(_HBM,), out_specs=(_SEM, _SEM, _HBM, pl.BlockSpec(memory_space=pltpu.VMEM)),
        input_output_aliases={0: 2},
        compiler_params=pltpu.CompilerParams(has_side_effects=_EFFECT),
    )(pltpu.with_memory_space_constraint(land, pltpu.HBM))


def _relay_wait(handle, r, after, name):
    send_sem, recv_sem, land_thru, _ = handle

    def body(land_ref, send_sem, recv_sem, after_ref, got_ref):
        x, y, c, _ = _position()
        three = land_ref.at[:, pl.ds(0, 3 * r), :]
        cp = pltpu.make_async_remote_copy(src_ref=three, dst_ref=three, send_sem=send_sem, recv_sem=recv_sem,
                                          device_id=(x, y, 1 - c), device_id_type=MESH)
        cp.wait_send()
        cp.wait_recv()

    return pl.pallas_call(
        body, name=name, out_shape=(pltpu.HBM(land_thru.shape, land_thru.dtype),),
        in_specs=(_HBM, _SEM, _SEM, pl.BlockSpec(memory_space=pl.ANY)), out_specs=(_HBM,),
        input_output_aliases={0: 0},
        compiler_params=pltpu.CompilerParams(has_side_effects=_EFFECT),
    )(land_thru, send_sem, recv_sem, after)[0]


def _exchange_start(grad, name):
    g, rows, C = grad.shape
    r = rows // N_DEV
    land = lax.empty((N_DEV, g, r, C), grad.dtype)
    return _push_start(grad, land, lambda s, z, i, p: (s.at[:, pl.ds(p * r, r), :], z.at[i]), name)


def _exchange_wait(handle, after, name):
    return _push_wait(handle, lambda z, n: z.at[pl.ds(0, n)], after, name)


def _sum_blocks(v, name):
    k, rows, cols = v.shape
    tr = rows
    for cand in (rows, 512, 352, 256, 176, 128, 64, 32, 16):
        if rows % cand == 0 and k * cand * cols * v.dtype.itemsize <= 6 * 1024 * 1024:
            tr = cand
            break

    def body(v_ref, o_ref):
        acc = v_ref[0].astype(F32)
        for s in range(1, k):
            acc = acc + v_ref[s].astype(F32)
        o_ref[...] = acc

    return pl.pallas_call(
        body, grid=(rows // tr,), name=name,
        in_specs=[pl.BlockSpec((k, tr, cols), lambda i: (0, i, 0))],
        out_specs=pl.BlockSpec((tr, cols), lambda i: (i, 0)),
        out_shape=jax.ShapeDtypeStruct((rows, cols), F32),
        compiler_params=_params(("parallel",)))(v)


def _adam_math(w, g, m, v):
    m2 = B1 * m + (1.0 - B1) * g
    v2 = B2 * v + (1.0 - B2) * (g * g)
    m_hat = m2 / (1.0 - B1 ** STEP)
    v_hat = v2 / (1.0 - B2 ** STEP)
    return -LR * (m_hat / (jnp.sqrt(v_hat) + EPS) + WD * w), m2, v2


def _adamw(w, g, m, v, name, dep=None):
    shp = w.shape
    rows, cols = (shp[-2], shp[-1]) if len(shp) >= 2 else (1, shp[-1])
    lead = math.prod(shp[:-2]) if len(shp) > 2 else 1
    fits = [t for t in range(8, rows + 1, 8) if rows % t == 0 and t * cols * 4 <= 2 * 1024 * 1024]
    tr = max(fits) if fits else rows

    def body(w_ref, g_ref, m_ref, v_ref, *rest):
        d_ref, m2_ref, v2_ref = rest[-3:]
        d_ref[...], m2_ref[...], v2_ref[...] = _adam_math(w_ref[...], g_ref[...], m_ref[...], v_ref[...])

    blk = pl.BlockSpec((None, tr, cols), lambda b, i: (b, i, 0))
    extra = [] if dep is None else [dep]
    outs = pl.pallas_call(
        body, grid=(lead, rows // tr), name=name,
        in_specs=[blk] * 4 + [pl.BlockSpec(memory_space=pl.ANY)] * len(extra), out_specs=[blk] * 3,
        out_shape=[jax.ShapeDtypeStruct((lead, rows, cols), F32)] * 3,
        compiler_params=_params(("parallel", "parallel")))(*[a.reshape(lead, rows, cols) for a in (w, g, m, v)], *extra)
    return [o.reshape(shp) for o in outs]


def _as2d(a):
    n = a.size
    if n % 1024 == 0:
        return a.reshape(n // 1024, 1024)
    if n % 128 == 0:
        return a.reshape(n // 128, 128)
    return a.reshape(1, n)


def _blocks_to_cols(a):
    b = jnp.moveaxis(a, 0, -2)
    return b.reshape(b.shape[:-2] + (b.shape[-2] * b.shape[-1],))


def _pack_rows(parts):
    padded, offs, r = [], [], 0
    for p in parts:
        pad = (-p.shape[0]) % 8
        padded.append(jnp.pad(p, ((0, pad), (0, 0))) if pad else p)
        offs.append(r)
        r += p.shape[0] + pad
    return jnp.concatenate(padded, axis=0), offs


def _silu(x):
    return x * jax.nn.sigmoid(x)


def kernel(x, c, ctx, c_ctx, w_mod, b_mod, ln_g, ln_b, ffn_w_gate, ffn_w_up, ffn_w_down, mix_ab_w_in, attn_sink, pool_w, pool_scale, mix_ab_w_out, lru_w_in, lru_conv_w, lru_conv_b, lru_wa, lru_ba, lru_wx, lru_bx, lru_lambda, lru_w_out, loss_target, m_c_ctx, m_w_mod, m_b_mod, m_ln_g, m_ln_b, m_ffn_w_gate, m_ffn_w_up, m_ffn_w_down, m_mix_ab_w_in, m_attn_sink, m_pool_w, m_pool_scale, m_mix_ab_w_out, m_lru_w_in, m_lru_conv_w, m_lru_conv_b, m_lru_wa, m_lru_ba, m_lru_wx, m_lru_bx, m_lru_lambda, m_lru_w_out, v_c_ctx, v_w_mod, v_b_mod, v_ln_g, v_ln_b, v_ffn_w_gate, v_ffn_w_up, v_ffn_w_down, v_mix_ab_w_in, v_attn_sink, v_pool_w, v_pool_scale, v_mix_ab_w_out, v_lru_w_in, v_lru_conv_w, v_lru_conv_b, v_lru_wa, v_lru_ba, v_lru_wx, v_lru_bx, v_lru_lambda, v_lru_w_out):
    weights = dict(c_ctx=c_ctx, w_mod=w_mod, b_mod=b_mod, ln_g=ln_g, ln_b=ln_b, ffn_w_gate=ffn_w_gate,
                   ffn_w_up=ffn_w_up, ffn_w_down=ffn_w_down, mix_ab_w_in=mix_ab_w_in, attn_sink=attn_sink,
                   pool_w=pool_w, pool_scale=pool_scale, mix_ab_w_out=mix_ab_w_out, lru_w_in=lru_w_in,
                   lru_conv_w=lru_conv_w, lru_conv_b=lru_conv_b, lru_wa=lru_wa, lru_ba=lru_ba, lru_wx=lru_wx,
                   lru_bx=lru_bx, lru_lambda=lru_lambda, lru_w_out=lru_w_out)
    mom_m = dict(c_ctx=m_c_ctx, w_mod=m_w_mod, b_mod=m_b_mod, ln_g=m_ln_g, ln_b=m_ln_b, ffn_w_gate=m_ffn_w_gate,
                 ffn_w_up=m_ffn_w_up, ffn_w_down=m_ffn_w_down, mix_ab_w_in=m_mix_ab_w_in, attn_sink=m_attn_sink,
                 pool_w=m_pool_w, pool_scale=m_pool_scale, mix_ab_w_out=m_mix_ab_w_out, lru_w_in=m_lru_w_in,
                 lru_conv_w=m_lru_conv_w, lru_conv_b=m_lru_conv_b, lru_wa=m_lru_wa, lru_ba=m_lru_ba, lru_wx=m_lru_wx,
                 lru_bx=m_lru_bx, lru_lambda=m_lru_lambda, lru_w_out=m_lru_w_out)
    mom_v = dict(c_ctx=v_c_ctx, w_mod=v_w_mod, b_mod=v_b_mod, ln_g=v_ln_g, ln_b=v_ln_b, ffn_w_gate=v_ffn_w_gate,
                 ffn_w_up=v_ffn_w_up, ffn_w_down=v_ffn_w_down, mix_ab_w_in=v_mix_ab_w_in, attn_sink=v_attn_sink,
                 pool_w=v_pool_w, pool_scale=v_pool_scale, mix_ab_w_out=v_mix_ab_w_out, lru_w_in=v_lru_w_in,
                 lru_conv_w=v_lru_conv_w, lru_conv_b=v_lru_conv_b, lru_wa=v_lru_wa, lru_ba=v_lru_ba, lru_wx=v_lru_wx,
                 lru_bx=v_lru_bx, lru_lambda=v_lru_lambda, lru_w_out=v_lru_w_out)
    names = list(weights)

    n_lat, n_ctx = x.shape[1], ctx.shape[1]
    cfg = _Cfg(n_lat, n_ctx)
    _, _, _, me = _position()
    mcols = w_mod.shape[2]

    def t_bf16(w):
        return jnp.swapaxes(w, -1, -2).astype(BF16)

    def ffn_src(l, i):
        return jnp.stack([t_bf16(ffn_w_gate[l, i]), t_bf16(ffn_w_up[l, i]), ffn_w_down[l, i].astype(BF16)])

    pending = {}

    def start_gathers(items, tok):
        for key, make_src in items:
            pending[key] = _gather_start(make_src() + tok.astype(BF16), "gather_start_" + key)
            tok = pending[key][4][0, 0]
        return tok

    def weights_now(key, after):
        return _gather_wait(pending[key], after, "gather_wait_" + key)

    first = _gather_start(ffn_src(0, 0), "gather_start_ffn00", SAME_CORE_AND_SIBLING)
    tok = first[4][0, 0]

    small_names = ["ln_g", "ln_b", "lru_conv_w", "lru_conv_b", "lru_ba", "lru_bx", "lru_lambda"]
    small, small_off = _pack_rows([(c + tok).reshape(-1, 128)] + [weights[n].reshape(-1, 128) for n in small_names])
    small_all = _all_gather(small, "gather_small", True)

    def small_full(idx, shp):
        rows = math.prod(shp) // 128
        return _blocks_to_cols(small_all[:, small_off[idx]:small_off[idx] + rows, :].reshape((N_DEV,) + shp))

    c_all = small_all[:, :2 * D // 128, :].reshape(2 * N_DEV, D)
    ln_g_f, ln_b_f = small_full(1, ln_g.shape), small_full(2, ln_b.shape)
    lru_consts = (small_full(3, lru_conv_w.shape)[0], small_full(4, lru_conv_b.shape), lru_wa[0],
                  small_full(5, lru_ba.shape)[0], lru_wx[0], small_full(6, lru_bx.shape)[0],
                  small_full(7, lru_lambda.shape)[0])

    s_rows = jnp.zeros((32, D), F32).at[:16].set(_silu(c_all)).at[16].set(_silu(c_ctx)).astype(BF16)
    mod_mine = jnp.stack([_matmul(s_rows, w_mod[l], "nn", F32, "mod_fwd", bn_cap=1280) for l in range(2)])
    mod_all = _all_gather(mod_mine.reshape(64, mcols), "gather_mod", True).reshape(N_DEV, 2, 32, mcols)
    r_ffn = ffn_w_down.shape[2]
    relay = _relay_start(_gather_wait(first, mod_all, "gather_wait_ffn00", n_peers=len(SAME_CORE_AND_SIBLING)),
                         r_ffn, "gather_relay_start_ffn00")
    tok = start_gathers([("ab_in", lambda: t_bf16(mix_ab_w_in)), ("ab_out", lambda: mix_ab_w_out.astype(BF16)),
                         ("ffn01", lambda: ffn_src(0, 1)), ("ffn10", lambda: ffn_src(1, 0)),
                         ("lru_in", lambda: t_bf16(lru_w_in)), ("lru_out", lambda: lru_w_out.astype(BF16)),
                         ("ffn11", lambda: ffn_src(1, 1))], relay[3][0, 0])
    mod_full = _blocks_to_cols(mod_all) + (b_mod[:, None, :] + tok)
    ex0 = 2 * me
    mods = []
    for l in range(2):
        rows = jnp.stack([lax.dynamic_index_in_dim(mod_full[l], ex0, 0, False),
                          lax.dynamic_index_in_dim(mod_full[l], ex0 + 1, 0, False), mod_full[l, 16]])
        mods.append(rows.reshape(3, N_MOD, D))

    h0 = jnp.concatenate([x.reshape(cfg.t_lat, D), ctx.reshape(cfg.t_ctx, D)], axis=0)
    cos, sin = _rope_tables(n_lat)
    sink_rows = jnp.broadcast_to(attn_sink[0][:, None], (8, 128)).astype(F32)

    saved = []
    wf = [[None, None], [None, None]]
    h = h0
    xin = _modulate(cfg, h0, mods[0], 0, 1, "modulate_in")
    for l in range(2):
        st = {"h_in": h, "xin1": xin}
        wf[l][0] = (_relay_wait(relay, r_ffn, xin, "gather_relay_wait_ffn00") if l == 0
                    else weights_now("ffn10", xin))
        g1, u1, y1 = _ffn_fwd(xin, wf[l][0], "ffn_fwd")
        h1, xhat1, rstd1, xin2 = _ln_fwd(cfg, h, y1, mods[l], 2, 0.5, ln_g_f[l, 0][None], ln_b_f[l, 0][None],
                                          mods[l], (3, 4), "ln_fwd_a")
        st.update(g1=g1, u1=u1, y1=y1, h1=h1, xhat1=xhat1, rstd1=rstd1, xin2=xin2)
        if l == 0:
            w_ab_in_t = weights_now("ab_in", xin2)[0]
            p = _matmul(xin2, w_ab_in_t, "nt", BF16, "mix_ab_in")
            att_l, att_c = _attn_fwd(cfg, p, cos, sin, sink_rows, "attn_fwd")
            pool_l = _pool_fwd(p, pool_w[0], pool_scale, n_lat, 0, 2, "pool_fwd_lat")
            pool_c = _pool_fwd(p, pool_w[0], pool_scale, n_ctx, cfg.ctx_blk, 2, "pool_fwd_ctx")
            cat = jnp.concatenate([jnp.concatenate([att_l, pool_l], axis=1),
                                   jnp.concatenate([att_c, pool_c], axis=1)], axis=0)
            w_ab_out = weights_now("ab_out", cat)[0]
            y2 = _matmul(cat, w_ab_out, "nn", BF16, "mix_ab_out")
        else:
            w_lru_in_t = weights_now("lru_in", xin2)[0]
            p = _matmul(xin2, w_lru_in_t, "nt", BF16, "lru_in")
            z_l, z_c, st["h_lat"], st["h_ctx"] = _lru_fwd(cfg, p, lru_consts, "lru_fwd")
            cat = jnp.concatenate([z_l, z_c], axis=0)
            w_lru_out = weights_now("lru_out", cat)[0]
            y2 = _matmul(cat, w_lru_out, "nn", BF16, "lru_out")
        h2, xhat2, rstd2, xin3 = _ln_fwd(cfg, h1, y2, mods[l], 5, 1.0, ln_g_f[l, 1][None], ln_b_f[l, 1][None],
                                          mods[l], (6, 7), "ln_fwd_b")
        wf[l][1] = weights_now("ffn%d1" % l, xin3)
        g3, u3, y3 = _ffn_fwd(xin3, wf[l][1], "ffn_fwd")
        if l == 0:
            h3, xhat3, rstd3, xin = _ln_fwd(cfg, h2, y3, mods[l], 8, 0.5, ln_g_f[l, 2][None], ln_b_f[l, 2][None],
                                            mods[1], (0, 1), "ln_fwd_a")
        else:
            h3, xhat3, rstd3 = _ln_fwd(cfg, h2, y3, mods[l], 8, 0.5, ln_g_f[l, 2][None], ln_b_f[l, 2][None],
                                       None, None, "ln_fwd_last")
        st.update(p=p, cat=cat, y2=y2, h2=h2, xhat2=xhat2, rstd2=rstd2, xin3=xin3, g3=g3, u3=u3, y3=y3,
                  xhat3=xhat3, rstd3=rstd3)
        saved.append(st)
        h = h3

    dy, loss_tile = _loss(cfg, h, loss_target.reshape(cfg.t_lat, D), "loss")
    loss = lax.psum(loss_tile[0, 0], ("x", "y", "c"))

    grads = {}
    dmod = [None, None]
    recv_ffn = [[None, None], [None, None]]
    dln_g = [[None] * 3, [None] * 3]
    dln_b = [[None] * 3, [None] * 3]

    def ffn_weight_grads(tag, xin_b, dg, du, a_act, dys):
        return [_exchange_start(_ffn_dw(dg, du, a_act, xin_b, dys, "ffn_dw"), "exchange_start_ffn" + tag)]

    def pin(handles):
        total = handles[0][4][0, 0]
        for hd in handles[1:]:
            total = total + hd[4][0, 0]
        return total

    up = (dy,)
    dmod_next = None
    last_sent = None
    for l in (1, 0):
        st = saved[l]
        dm = [None] * N_MOD

        def put_stats(stats, gate_idx, nxt):
            dm[gate_idx] = stats[:, 2, :]
            if nxt is not None:
                nxt[0][nxt[1]] = stats[:, 4, :]
                nxt[0][nxt[1] + 1] = stats[:, 3, :]

        lng3 = ln_g_f[l, 2][None] if last_sent is None else ln_g_f[l, 2][None] + pin(last_sent)
        if len(up) > 1:
            up = (up[0], up[1], ln_b_f[l, 2][None], up[3], up[4])
        dres, dys, stats = _ln_bwd(cfg, up, st["xhat3"], st["rstd3"], st["y3"], mods[l], 8, 0.5,
                                   lng3, "ln_bwd_fused" if len(up) > 1 else "ln_bwd_last")
        put_stats(stats, 8, None if len(up) == 1 else (dmod_next, 0))
        dln_g[l][2], dln_b[l][2] = stats[:, 0, :].sum(0), stats[:, 1, :].sum(0)
        dg, du, a_act, dxin = _ffn_bwd(dys, st["g3"], st["u3"], wf[l][1], "ffn_bwd")
        recv_ffn[l][1] = ffn_weight_grads("%d1" % l, st["xin3"], dg, du, a_act, dys)
        dres, dys, stats = _ln_bwd(cfg, (dres, dxin, ln_b_f[l, 1][None], mods[l], 7), st["xhat2"], st["rstd2"], st["y2"],
                                   mods[l], 5, 1.0, ln_g_f[l, 1][None] + pin(recv_ffn[l][1]), "ln_bwd_fused")
        put_stats(stats, 5, (dm, 6))
        dln_g[l][1], dln_b[l][1] = stats[:, 0, :].sum(0), stats[:, 1, :].sum(0)
        if l == 0:
            dw_out = _matmul(st["cat"], dys, "tn", BF16, "mix_ab_dw_out")
            dcat = _matmul(dys, w_ab_out, "nt", BF16, "mix_ab_dcat")
            dq, dk, dv, dqc, dkc, dvc, dsink = _attn_bwd(cfg, st["p"], dcat, cos, sin, sink_rows, "attn_bwd")
            du_l, dpw_l, dps_l = _pool_bwd(st["p"], pool_w[0], pool_scale, dcat, n_lat, 0, 2, "pool_bwd_lat")
            du_c, dpw_c, dps_c = _pool_bwd(st["p"], pool_w[0], pool_scale, dcat, n_ctx, cfg.ctx_blk, 2, "pool_bwd_ctx")
            dp = jnp.concatenate([jnp.concatenate([dq, dk, dv, du_l], axis=1),
                                  jnp.concatenate([dqc, dkc, dvc, du_c], axis=1)], axis=0)
            dw_in_t = _matmul(dp, st["xin2"], "tn", BF16, "mix_ab_dw_in", bm_cap=1280)
            dxin = _matmul(dp, w_ab_in_t, "nn", BF16, "mix_ab_dx")
            recv_mix = [_exchange_start(part, "exchange_start_mix_ab_%d" % k)
                        for k, part in enumerate((dw_in_t[None], dw_out[None], _as2d(dpw_l + dpw_c)[None]))]
            grads["attn_sink"] = (dsink[0, :, 0] + dsink[1, :, 0])[None, :]
            grads["pool_scale"] = dps_l + dps_c
        else:
            dw_out = _matmul(st["cat"], dys, "tn", BF16, "lru_dw_out")
            dz = _matmul(dys, w_lru_out, "nt", BF16, "lru_dz")
            dgl, dul, dgc, duc, dwa, dwx, vec = _lru_bwd(cfg, st["p"], dz, st["h_lat"], st["h_ctx"], lru_consts, "lru_bwd")
            dp = jnp.concatenate([jnp.concatenate([dgl, dul], axis=1), jnp.concatenate([dgc, duc], axis=1)], axis=0)
            dw_in_t = _matmul(dp, st["xin2"], "tn", BF16, "lru_dw_in", bm_cap=1024)
            dxin = _matmul(dp, w_lru_in_t, "nn", BF16, "lru_dx")
            recv_mix = [_exchange_start(part, "exchange_start_lru_%d" % k)
                        for k, part in enumerate((dw_in_t[None], dw_out[None], _as2d(dwa)[None], _as2d(dwx)[None]))]
            vec_t = jnp.moveaxis(vec, 0, 1).reshape(16, D)
            grads["lru_ba"], grads["lru_bx"] = vec_t[0:2], vec_t[2:4]
            grads["lru_lambda"], grads["lru_conv_w"], grads["lru_conv_b"] = vec_t[4:6], vec_t[6:10], vec_t[10:11]
        if l == 0:
            recv_ab = recv_mix
        else:
            recv_lru = recv_mix
        dres, dys, stats = _ln_bwd(cfg, (dres, dxin, ln_b_f[l, 0][None], mods[l], 4), st["xhat1"], st["rstd1"], st["y1"],
                                   mods[l], 2, 0.5, ln_g_f[l, 0][None] + pin(recv_mix), "ln_bwd_fused")
        put_stats(stats, 2, (dm, 3))
        dln_g[l][0], dln_b[l][0] = stats[:, 0, :].sum(0), stats[:, 1, :].sum(0)
        dg, du, a_act, dxin = _ffn_bwd(dys, st["g1"], st["u1"], wf[l][0], "ffn_bwd")
        recv_ffn[l][0] = ffn_weight_grads("%d0" % l, st["xin1"], dg, du, a_act, dys)
        last_sent = recv_ffn[l][0]
        dmod[l] = dm
        dmod_next = dm
        up = (dres, dxin, None, mods[l], 1)
    dh0, stats = _modulate_bwd(cfg, up[0], up[1], h0, mods[0] + pin(last_sent), 1, "modulate_bwd")
    dmod[0][0], dmod[0][1] = stats[:, 4, :], stats[:, 3, :]
    grad_x = dh0.reshape(x.shape)

    dmod_mine = jnp.stack([jnp.stack(dmod[l], axis=1).reshape(3, N_MOD * D) for l in range(2)])
    n_dm = 6 * N_MOD * D // 128
    dmod_sent = _gather_start(dmod_mine.reshape(1, n_dm, 128), "gather_start_dmod")

    def arrived(handle, name):
        return _exchange_wait(handle, dmod_sent[4], name)

    recv_ffn = [[arrived(recv_ffn[l][i][0], "exchange_wait_ffn%d%d" % (l, i)) for i in range(2)] for l in range(2)]
    recv_ab = [arrived(hd, "exchange_wait_mix_ab_%d" % k) for k, hd in enumerate(recv_ab)]
    recv_lru = [arrived(hd, "exchange_wait_lru_%d" % k) for k, hd in enumerate(recv_lru)]

    def shard_sum(recv, name):
        return _sum_blocks(recv.reshape(N_DEV, recv.shape[2], recv.shape[3]), name)

    gate_g = [[None, None], [None, None]]
    up_g = [[None, None], [None, None]]
    down_g = [[None, None], [None, None]]
    for l in range(2):
        for i in range(2):
            recv = recv_ffn[l][i]
            r_ffn3 = recv.shape[1] * recv.shape[2]
            gt, ut, dn = _sum_blocks(recv.reshape(N_DEV, r_ffn3, D), "sum_ffn").reshape(recv.shape[1:])
            gate_g[l][i], up_g[l][i], down_g[l][i] = gt.T, ut.T, dn
    grads["ffn_w_gate"] = jnp.stack([jnp.stack(gate_g[l]) for l in range(2)])
    grads["ffn_w_up"] = jnp.stack([jnp.stack(up_g[l]) for l in range(2)])
    grads["ffn_w_down"] = jnp.stack([jnp.stack(down_g[l]) for l in range(2)])
    grads["mix_ab_w_in"] = shard_sum(recv_ab[0], "sum_mix_in").T[None]
    grads["mix_ab_w_out"] = shard_sum(recv_ab[1], "sum_mix_out")[None]
    grads["lru_w_in"] = shard_sum(recv_lru[0], "sum_lru_in").T[None]
    grads["lru_w_out"] = shard_sum(recv_lru[1], "sum_lru_out")[None]
    rep_parts = [shard_sum(recv_lru[2], "sum_rep"), shard_sum(recv_lru[3], "sum_rep"), shard_sum(recv_ab[2], "sum_rep")]
    rep_names = ["lru_wa", "lru_wx", "pool_w"]

    dmod_all = _gather_wait(dmod_sent, rep_parts[2], "gather_wait_dmod").reshape(N_DEV, n_dm, 128)
    dmod_sum = _sum_blocks(dmod_all, "sum_dmod").reshape(2, 3, N_MOD * D)
    dmod_all = dmod_all.reshape(N_DEV, 2, 3, N_MOD * D)
    grads["b_mod"] = dmod_sum[:, 0] + dmod_sum[:, 1] + dmod_sum[:, 2]
    dmod_ex = jnp.moveaxis(dmod_all[:, :, 0:2, :], 1, 0).reshape(2, 2 * N_DEV, N_MOD * D)
    dm_rows = jnp.zeros((2, 32, N_MOD * D), F32).at[:, :16].set(dmod_ex).at[:, 16].set(dmod_sum[:, 2])
    dm_cols = lax.dynamic_slice_in_dim(dm_rows, me * mcols, mcols, axis=2).astype(BF16)
    grads["w_mod"] = jnp.stack([_matmul(s_rows, dm_cols[l], "tn", F32, "mod_dw", bn_cap=1280) for l in range(2)])
    ds_part = None
    for l in range(2):
        part = _matmul(dm_cols[l, 16:32], w_mod[l], "nt", F32, "mod_ds", bk_cap=1280)[0]
        ds_part = part if ds_part is None else ds_part + part

    dln_g_f = jnp.stack([jnp.stack(dln_g[l]) for l in range(2)])
    dln_b_f = jnp.stack([jnp.stack(dln_b[l]) for l in range(2)])
    sink_pad = jnp.zeros((1, 128), F32).at[0, :8].set(grads["attn_sink"][0])
    part_list = [p_.reshape(-1, 128) for p_ in rep_parts] + [
        dln_g_f.reshape(-1, 128), dln_b_f.reshape(-1, 128), grads["lru_conv_w"].reshape(-1, 128),
        grads["lru_conv_b"].reshape(-1, 128), grads["lru_ba"].reshape(-1, 128), grads["lru_bx"].reshape(-1, 128),
        grads["lru_lambda"].reshape(-1, 128), ds_part.reshape(-1, 128), sink_pad, grads["pool_scale"].reshape(-1, 128)]
    parts, part_off = _pack_rows(part_list)
    parts_sent = _gather_start(parts[None], "gather_start_partials")

    delta, new_m, new_v = {}, {}, {}
    for n in ("w_mod", "b_mod", "ffn_w_gate", "ffn_w_up", "ffn_w_down", "mix_ab_w_in", "mix_ab_w_out",
              "lru_w_in", "lru_w_out"):
        grads[n] = grads[n].reshape(weights[n].shape)
        delta[n], new_m[n], new_v[n] = _adamw(weights[n], grads[n], mom_m[n], mom_v[n], "adamw", dep=parts_sent[4])
    parts_all = _gather_wait(parts_sent, delta["lru_w_out"], "gather_wait_partials").reshape(N_DEV, parts.shape[0], 128)
    parts_sum = _sum_blocks(parts_all, "sum_partials")

    for i, n in enumerate(rep_names):
        rows = part_list[i].shape[0]
        grads[n] = parts_all[:, part_off[i]:part_off[i] + rows, :].reshape(weights[n].shape)

    def take(idx):
        return parts_sum[part_off[idx]:part_off[idx] + part_list[idx].shape[0]]

    def my_cols(full, shp):
        w = shp[-1]
        return lax.dynamic_slice_in_dim(full, me * w, w, axis=full.ndim - 1)

    grads["ln_g"] = my_cols(take(3).reshape(2, 3, D), ln_g.shape)
    grads["ln_b"] = my_cols(take(4).reshape(2, 3, D), ln_b.shape)
    grads["lru_conv_w"] = my_cols(take(5).reshape(1, 4, D), lru_conv_w.shape)
    grads["lru_conv_b"] = my_cols(take(6).reshape(1, D), lru_conv_b.shape)
    grads["lru_ba"] = my_cols(take(7).reshape(1, 2, D), lru_ba.shape)
    grads["lru_bx"] = my_cols(take(8).reshape(1, 2, D), lru_bx.shape)
    grads["lru_lambda"] = my_cols(take(9).reshape(1, 2, D), lru_lambda.shape)
    sg = jax.nn.sigmoid(c_ctx)
    grads["c_ctx"] = take(10).reshape(D) * (sg * (1.0 + c_ctx * (1.0 - sg)))
    grads["attn_sink"] = take(11)[:, :8]
    grads["pool_scale"] = take(12).reshape(pool_scale.shape)

    for n in names:
        if n in delta:
            continue
        grads[n] = grads[n].reshape(weights[n].shape)
        delta[n], new_m[n], new_v[n] = _adamw(weights[n], grads[n], mom_m[n], mom_v[n], "adamw")

    return (loss, grad_x, *[grads[n] for n in names], *[delta[n] for n in names],
            *[new_m[n] for n in names], *[new_v[n] for n in names])
```

```python
import functools
import math

import jax
import jax.numpy as jnp
from jax import lax
from jax.experimental import pallas as pl
from jax.experimental.pallas import tpu as pltpu

F32 = jnp.float32
BF16 = jnp.bfloat16
MESH = pl.DeviceIdType.MESH

D = 1024
N_MOD = 9
N_DEV = 8
HEAD_DIM = 64
ATT_HEADS = 8
KV_HEADS = 2
ATT_W = 512
BLK = 128
ATT_SCALE = HEAD_DIM ** -0.5
GRID_W = 64
ROPE_FREQS = HEAD_DIM // 4
ROPE_THETA = 10000.0
POOL_R = (1, 2, 4, 8)
LRU_C = 8.0
LN_EPS = 1e-5
NEG_INF = -1e30
ALPHA = 4.0 ** 0.25
LR, B1, B2, EPS, WD, STEP = 0.001, 0.9, 0.999, 1e-08, 0.01, 10
VMEM_LIMIT = 56 * 1024 * 1024
ROW_TILE = 512


def _params(sem=None):
    if sem is None:
        return pltpu.CompilerParams(vmem_limit_bytes=VMEM_LIMIT)
    return pltpu.CompilerParams(dimension_semantics=sem, vmem_limit_bytes=VMEM_LIMIT)


def _sigmoid(x):
    return 0.5 * jnp.tanh(0.5 * x) + 0.5


def _dot(a, b):
    return jnp.dot(a.astype(BF16), b.astype(BF16), preferred_element_type=F32)


def _dot_nt(a, b):
    return lax.dot_general(a.astype(BF16), b.astype(BF16), (((1,), (1,)), ((), ())), preferred_element_type=F32)


def _dot_tn(a, b):
    return lax.dot_general(a.astype(BF16), b.astype(BF16), (((0,), (0,)), ((), ())), preferred_element_type=F32)


def _pick(n, cap):
    best = None
    for m in range(128, min(n, cap) + 1, 128):
        if n % m == 0:
            best = m
    return n if best is None else best


def _chunks(width, step=256):
    out, c = [], 0
    while c < width:
        w = min(step, width - c)
        out.append((c, w))
        c += w
    return out


class _Cfg:
    def __init__(self, n_lat, n_ctx):
        self.n_lat, self.n_ctx = n_lat, n_ctx
        self.t_lat, self.t_ctx = 2 * n_lat, 2 * n_ctx
        self.T = self.t_lat + self.t_ctx
        self.tm = min(ROW_TILE, self.t_ctx)
        assert n_lat % self.tm == 0 and self.t_ctx % self.tm == 0 and n_lat >= 3 * BLK and n_ctx % BLK == 0
        self.nt = self.T // self.tm
        self.nlt = n_lat // self.tm
        self.ctx_blk = self.t_lat // n_ctx

    def seg(self, i):
        return jnp.minimum(i // self.nlt, 2)

    def first_of_seg(self, i):
        return jnp.where(i < 2 * self.nlt, i % self.nlt == 0, i == 2 * self.nlt)


def _modulate(cfg, h, mod, shift_idx, scale_idx, name):
    tm = cfg.tm

    def body(h_ref, mod_ref, o_ref):
        sh = mod_ref[shift_idx:shift_idx + 1, :]
        sc = mod_ref[scale_idx:scale_idx + 1, :]
        o_ref[...] = (h_ref[...] * (1.0 + sc) + sh).astype(BF16)

    return pl.pallas_call(
        body, grid=(cfg.nt,), name=name,
        in_specs=[pl.BlockSpec((tm, D), lambda i: (i, 0)),
                  pl.BlockSpec((None, N_MOD, D), lambda i: (cfg.seg(i), 0, 0))],
        out_specs=pl.BlockSpec((tm, D), lambda i: (i, 0)),
        out_shape=jax.ShapeDtypeStruct((cfg.T, D), BF16),
        compiler_params=_params(("parallel",)),
    )(h, mod)


def _ln_fwd(cfg, h, y, mod, gate_idx, coef, lng, lnb, mod_next, next_idx, name):
    tm = cfg.tm
    has_next = next_idx is not None

    def body(*refs):
        if has_next:
            h_ref, y_ref, mod_ref, g_ref, b_ref, modn_ref, hn_ref, xhat_ref, rstd_ref, xin_ref = refs
        else:
            h_ref, y_ref, mod_ref, g_ref, b_ref, hn_ref, xhat_ref, rstd_ref = refs
        gate = mod_ref[gate_idx:gate_idx + 1, :]
        z = ALPHA * h_ref[...] + (coef * gate) * y_ref[...].astype(F32)
        mu = jnp.mean(z, axis=-1, keepdims=True)
        zc = z - mu
        var = jnp.mean(zc * zc, axis=-1, keepdims=True)
        rstd = lax.rsqrt(var + LN_EPS)
        xhat = zc * rstd
        hn = xhat * g_ref[...] + b_ref[...]
        hn_ref[...] = hn
        xhat_ref[...] = xhat.astype(BF16)
        rstd_ref[...] = rstd
        if has_next:
            sh = modn_ref[next_idx[0]:next_idx[0] + 1, :]
            sc = modn_ref[next_idx[1]:next_idx[1] + 1, :]
            xin_ref[...] = (hn * (1.0 + sc) + sh).astype(BF16)

    row = pl.BlockSpec((tm, D), lambda i: (i, 0))
    modspec = pl.BlockSpec((None, N_MOD, D), lambda i: (cfg.seg(i), 0, 0))
    vec = pl.BlockSpec((1, D), lambda i: (0, 0))
    in_specs = [row, row, modspec, vec, vec]
    args = [h, y, mod, lng, lnb]
    out_specs = [row, row, pl.BlockSpec((tm, 1), lambda i: (i, 0))]
    out_shape = [jax.ShapeDtypeStruct((cfg.T, D), F32), jax.ShapeDtypeStruct((cfg.T, D), BF16),
                 jax.ShapeDtypeStruct((cfg.T, 1), F32)]
    if has_next:
        in_specs.append(modspec)
        args.append(mod_next)
        out_specs.append(row)
        out_shape.append(jax.ShapeDtypeStruct((cfg.T, D), BF16))
    return pl.pallas_call(body, grid=(cfg.nt,), name=name, in_specs=in_specs, out_specs=out_specs,
                          out_shape=out_shape, compiler_params=_params(("parallel",)))(*args)


def _ln_bwd(cfg, up, xhat, rstd, y, mod, gate_idx, coef, lng, name):
    tm = cfg.tm
    fused = len(up) > 1
    scale_next = up[4] if fused else None

    def body(*refs):
        if fused:
            dres_n, dxin_n, b_ref, modn_ref, xhat_ref, rstd_ref, y_ref, mod_ref, g_ref, dres_ref, dys_ref, st_ref = refs
        else:
            dhn_ref, xhat_ref, rstd_ref, y_ref, mod_ref, g_ref, dres_ref, dys_ref, st_ref = refs
        i = pl.program_id(0)

        @pl.when(cfg.first_of_seg(i))
        def _():
            st_ref[...] = jnp.zeros_like(st_ref)

        xhat = xhat_ref[...].astype(F32)
        if fused:
            dxin = dxin_n[...].astype(F32)
            sc = modn_ref[scale_next:scale_next + 1, :]
            dhn = dres_n[...] + dxin * (1.0 + sc)
            shift_sum = jnp.sum(dxin, axis=0, keepdims=True)
            st_ref[3:4, :] += g_ref[...] * jnp.sum(dxin * xhat, axis=0, keepdims=True) + b_ref[...] * shift_sum
            st_ref[4:5, :] += shift_sum
        else:
            dhn = dhn_ref[...]
        gdh = dhn * g_ref[...]
        m1 = jnp.mean(gdh, axis=-1, keepdims=True)
        m2 = jnp.mean(gdh * xhat, axis=-1, keepdims=True)
        dz = rstd_ref[...] * (gdh - m1 - xhat * m2)
        gate = mod_ref[gate_idx:gate_idx + 1, :]
        dres_ref[...] = ALPHA * dz
        dys_ref[...] = ((coef * gate) * dz).astype(BF16)
        st_ref[0:1, :] += jnp.sum(dhn * xhat, axis=0, keepdims=True)
        st_ref[1:2, :] += jnp.sum(dhn, axis=0, keepdims=True)
        st_ref[2:3, :] += jnp.sum((coef * dz) * y_ref[...].astype(F32), axis=0, keepdims=True)

    row = pl.BlockSpec((tm, D), lambda i: (i, 0))
    modspec = pl.BlockSpec((None, N_MOD, D), lambda i: (cfg.seg(i), 0, 0))
    vec = pl.BlockSpec((1, D), lambda i: (0, 0))
    col = pl.BlockSpec((tm, 1), lambda i: (i, 0))
    if fused:
        in_specs = [row, row, vec, modspec, row, col, row, modspec, vec]
        args = [up[0], up[1], up[2], up[3], xhat, rstd, y, mod, lng]
    else:
        in_specs = [row, row, col, row, modspec, vec]
        args = [up[0], xhat, rstd, y, mod, lng]
    return pl.pallas_call(
        body, grid=(cfg.nt,), name=name, in_specs=in_specs,
        out_specs=[row, row, pl.BlockSpec((None, 8, D), lambda i: (cfg.seg(i), 0, 0))],
        out_shape=[jax.ShapeDtypeStruct((cfg.T, D), F32), jax.ShapeDtypeStruct((cfg.T, D), BF16),
                   jax.ShapeDtypeStruct((3, 8, D), F32)],
        compiler_params=_params(("arbitrary",)))(*args)


def _modulate_bwd(cfg, dres, dxin, h, mod, scale_idx, name):
    tm = cfg.tm
    n_lt = 2 * cfg.nlt

    def body(dres_ref, dxin_ref, h_ref, mod_ref, dh_ref, st_ref):
        i = pl.program_id(0)

        @pl.when(cfg.first_of_seg(i))
        def _():
            st_ref[...] = jnp.zeros_like(st_ref)

        dxin = dxin_ref[...].astype(F32)
        sc = mod_ref[scale_idx:scale_idx + 1, :]

        @pl.when(i < n_lt)
        def _():
            dh_ref[...] = dres_ref[...] + dxin * (1.0 + sc)

        st_ref[3:4, :] += jnp.sum(dxin * h_ref[...], axis=0, keepdims=True)
        st_ref[4:5, :] += jnp.sum(dxin, axis=0, keepdims=True)

    row = pl.BlockSpec((tm, D), lambda i: (i, 0))
    return pl.pallas_call(
        body, grid=(cfg.nt,), name=name,
        in_specs=[row, row, row, pl.BlockSpec((None, N_MOD, D), lambda i: (cfg.seg(i), 0, 0))],
        out_specs=[pl.BlockSpec((tm, D), lambda i: (jnp.minimum(i, n_lt - 1), 0)),
                   pl.BlockSpec((None, 8, D), lambda i: (cfg.seg(i), 0, 0))],
        out_shape=[jax.ShapeDtypeStruct((cfg.t_lat, D), F32), jax.ShapeDtypeStruct((3, 8, D), F32)],
        compiler_params=_params(("arbitrary",)))(dres, dxin, h, mod)


def _loss(cfg, h, target, name):
    tm = cfg.tm
    n_lt = 2 * cfg.nlt

    def body(h_ref, t_ref, dy_ref, l_ref):
        i = pl.program_id(0)

        @pl.when(i == 0)
        def _():
            l_ref[...] = jnp.zeros_like(l_ref)

        @pl.when(i < n_lt)
        def _():
            err = h_ref[...] - t_ref[...]
            dy_ref[...] = err * (1.0 / D)
            part = jnp.sum(jnp.sum(err * err, axis=1, keepdims=True), axis=0, keepdims=True) * (0.5 / D)
            l_ref[...] += jnp.broadcast_to(part, l_ref.shape)

        @pl.when(i >= n_lt)
        def _():
            dy_ref[...] = jnp.zeros_like(dy_ref)

    return pl.pallas_call(
        body, grid=(cfg.nt,), name=name,
        in_specs=[pl.BlockSpec((tm, D), lambda i: (i, 0)),
                  pl.BlockSpec((tm, D), lambda i: (jnp.minimum(i, n_lt - 1), 0))],
        out_specs=[pl.BlockSpec((tm, D), lambda i: (i, 0)), pl.BlockSpec((8, 128), lambda i: (0, 0))],
        out_shape=[jax.ShapeDtypeStruct((cfg.T, D), F32), jax.ShapeDtypeStruct((8, 128), F32)],
        compiler_params=_params(("arbitrary",)))(h, target)


def _matmul(a, b, mode, out_dtype, name, bm_cap=1536, bn_cap=1408, bk_cap=1024, dep=None):
    if mode == "nn":
        (M, K), N = a.shape, b.shape[1]
    elif mode == "nt":
        (M, K), N = a.shape, b.shape[0]
    else:
        (K, M), N = a.shape, b.shape[1]
    bm, bn, bk = _pick(M, bm_cap), _pick(N, bn_cap), _pick(K, bk_cap)
    nk = K // bk

    def body(a_ref, b_ref, *rest):
        o_ref = rest[0] if dep is None else rest[1]
        acc_ref = rest[-1]
        k = pl.program_id(2)
        if mode == "nn":
            part = _dot(a_ref[...], b_ref[...])
        elif mode == "nt":
            part = _dot_nt(a_ref[...], b_ref[...])
        else:
            part = _dot_tn(a_ref[...], b_ref[...])
        if nk == 1:
            o_ref[...] = part.astype(out_dtype)
            return

        @pl.when(k == 0)
        def _():
            acc_ref[...] = part

        @pl.when((k > 0) & (k < nk - 1))
        def _():
            acc_ref[...] += part

        @pl.when(k == nk - 1)
        def _():
            o_ref[...] = (acc_ref[...] + part).astype(out_dtype)

    if mode == "nn":
        a_spec = pl.BlockSpec((bm, bk), lambda i, j, k: (i, k))
        b_spec = pl.BlockSpec((bk, bn), lambda i, j, k: (k, j))
    elif mode == "nt":
        a_spec = pl.BlockSpec((bm, bk), lambda i, j, k: (i, k))
        b_spec = pl.BlockSpec((bn, bk), lambda i, j, k: (j, k))
    else:
        a_spec = pl.BlockSpec((bk, bm), lambda i, j, k: (k, i))
        b_spec = pl.BlockSpec((bk, bn), lambda i, j, k: (k, j))
    return pl.pallas_call(
        body, grid=(M // bm, N // bn, nk), name=name,
        in_specs=[a_spec, b_spec] + ([] if dep is None else [pl.BlockSpec(memory_space=pl.ANY)]),
        out_specs=pl.BlockSpec((bm, bn), lambda i, j, k: (i, j)),
        out_shape=jax.ShapeDtypeStruct((M, N), out_dtype),
        scratch_shapes=[pltpu.VMEM((bm, bn), F32)] if nk > 1 else [],
        compiler_params=_params(("parallel", "parallel", "arbitrary")))(a, b, *([] if dep is None else [dep]))


def _ffn_tile(T, cap):
    best = 256
    for t in range(256, cap + 1, 256):
        if T % t == 0:
            best = t
    return best


def _ffn_fwd(xin, wf, name):
    T = xin.shape[0]
    F = wf.shape[1]
    tm, tf = _ffn_tile(T, 768), F // 2
    assert tf % 128 == 0 and T % tm == 0

    def body(x_ref, wg_ref, wu_ref, wd_ref, g_ref, u_ref, y_ref, acc_ref):
        j = pl.program_id(1)
        x = x_ref[...]
        acc = None
        for c0, cw in _chunks(tf):
            g = _dot_nt(x, wg_ref[c0:c0 + cw, :])
            u = _dot_nt(x, wu_ref[c0:c0 + cw, :])
            g_ref[:, c0:c0 + cw] = g.astype(BF16)
            u_ref[:, c0:c0 + cw] = u.astype(BF16)
            part = _dot(g * _sigmoid(g) * u, wd_ref[c0:c0 + cw, :])
            acc = part if acc is None else acc + part

        @pl.when(j == 0)
        def _():
            acc_ref[...] = acc

        @pl.when(j == 1)
        def _():
            y_ref[...] = (acc_ref[...] + acc).astype(BF16)

    return pl.pallas_call(
        body, grid=(T // tm, 2), name=name,
        in_specs=[pl.BlockSpec((tm, D), lambda i, j: (i, 0)),
                  pl.BlockSpec((None, tf, D), lambda i, j: (0, j, 0)),
                  pl.BlockSpec((None, tf, D), lambda i, j: (1, j, 0)),
                  pl.BlockSpec((None, tf, D), lambda i, j: (2, j, 0))],
        out_specs=[pl.BlockSpec((tm, tf), lambda i, j: (i, j)),
                   pl.BlockSpec((tm, tf), lambda i, j: (i, j)),
                   pl.BlockSpec((tm, D), lambda i, j: (i, 0))],
        out_shape=[jax.ShapeDtypeStruct((T, F), BF16), jax.ShapeDtypeStruct((T, F), BF16),
                   jax.ShapeDtypeStruct((T, D), BF16)],
        scratch_shapes=[pltpu.VMEM((tm, D), F32)],
        compiler_params=_params(("parallel", "arbitrary")))(xin, wf, wf, wf)


def _ffn_bwd(dys, g, u, wf, name):
    T = dys.shape[0]
    F = wf.shape[1]
    tm, tf = _ffn_tile(T, 512), F // 2

    def body(dy_ref, g_ref, u_ref, wg_ref, wu_ref, wd_ref, dg_ref, du_ref, a_ref, dx_ref, acc_ref):
        j = pl.program_id(1)
        da_all = _dot_nt(dy_ref[...], wd_ref[...])
        for c0, cw in _chunks(tf):
            gg = g_ref[:, c0:c0 + cw].astype(F32)
            uu = u_ref[:, c0:c0 + cw].astype(F32)
            da = da_all[:, c0:c0 + cw]
            s = _sigmoid(gg)
            silu = gg * s
            a_ref[:, c0:c0 + cw] = (silu * uu).astype(BF16)
            du_ref[:, c0:c0 + cw] = (da * silu).astype(BF16)
            dg_ref[:, c0:c0 + cw] = (da * uu * (s * (1.0 + gg * (1.0 - s)))).astype(BF16)
        acc = _dot(dg_ref[...], wg_ref[...]) + _dot(du_ref[...], wu_ref[...])

        @pl.when(j == 0)
        def _():
            acc_ref[...] = acc

        @pl.when(j == 1)
        def _():
            dx_ref[...] = (acc_ref[...] + acc).astype(BF16)

    blk = pl.BlockSpec((tm, tf), lambda i, j: (i, j))
    return pl.pallas_call(
        body, grid=(T // tm, 2), name=name,
        in_specs=[pl.BlockSpec((tm, D), lambda i, j: (i, 0)), blk, blk,
                  pl.BlockSpec((None, tf, D), lambda i, j: (0, j, 0)),
                  pl.BlockSpec((None, tf, D), lambda i, j: (1, j, 0)),
                  pl.BlockSpec((None, tf, D), lambda i, j: (2, j, 0))],
        out_specs=[blk, blk, blk, pl.BlockSpec((tm, D), lambda i, j: (i, 0))],
        out_shape=[jax.ShapeDtypeStruct((T, F), BF16), jax.ShapeDtypeStruct((T, F), BF16),
                   jax.ShapeDtypeStruct((T, F), BF16), jax.ShapeDtypeStruct((T, D), BF16)],
        scratch_shapes=[pltpu.VMEM((tm, D), F32)],
        compiler_params=_params(("parallel", "arbitrary")))(dys, g, u, wf, wf, wf)


def _swap_halves(x):
    w = x.shape[1]
    lane = lax.broadcasted_iota(jnp.int32, (1, w), 1)
    return jnp.where((lane & 63) < 32, pltpu.roll(x, w - 32, 1), pltpu.roll(x, 32, 1))


def _rope(x, cos, sin):
    return x * cos + _swap_halves(x) * sin


def _rope_t(dy, cos, sin):
    return dy * cos + _swap_halves(dy * sin)


def _rope_tables(n_lat):
    rows = n_lat // GRID_W
    row = jnp.repeat(jnp.arange(rows, dtype=F32), GRID_W)
    col = jnp.tile(jnp.arange(GRID_W, dtype=F32), rows)
    inv = ROPE_THETA ** (-jnp.arange(ROPE_FREQS, dtype=F32) / ROPE_FREQS)
    ang = jnp.concatenate([row[:, None] * inv, col[:, None] * inv], axis=-1)
    cs, sn = jnp.cos(ang), jnp.sin(ang)
    cos = jnp.concatenate([cs, cs, cs, cs], axis=-1)
    sin = jnp.concatenate([-sn, sn, -sn, sn], axis=-1)
    return cos, sin


def _attn_specs(cfg):
    n_lat, n_ctx, cb = cfg.n_lat, cfg.n_ctx, cfg.ctx_blk
    return [pl.BlockSpec((n_lat, ATT_W), lambda e: (e, 0)),
            pl.BlockSpec((n_lat, 128), lambda e: (e, 4)),
            pl.BlockSpec((n_lat, 128), lambda e: (e, 5)),
            pl.BlockSpec((n_ctx, ATT_W), lambda e: (cb + e, 0)),
            pl.BlockSpec((n_ctx, 128), lambda e: (cb + e, 4)),
            pl.BlockSpec((n_ctx, 128), lambda e: (cb + e, 5)),
            pl.BlockSpec((n_lat, 128), lambda e: (0, 0)),
            pl.BlockSpec((n_lat, 128), lambda e: (0, 0)),
            pl.BlockSpec((8, 128), lambda e: (0, 0))]


def _attn_prepare(kh, kl, vl, kc, vc, ka, kb, va, vb, kca, kcb, vca, vcb):
    lane = lax.broadcasted_iota(jnp.int32, (1, 128), 1)
    own = (lane < 64) if kh == 0 else (lane >= 64)

    def split(x, ra, rb):
        mine = jnp.where(own, x, 0.0)
        other = pltpu.roll(mine, 64, 1)
        a, b = (mine, other) if kh == 0 else (other, mine)
        ra[...] = a.astype(BF16)
        rb[...] = b.astype(BF16)

    split(kl, ka, kb)
    split(vl, va, vb)
    split(kc, kca, kcb)
    split(vc, vca, vcb)


def _softmax_parts(s_list, sk):
    m = sk
    for s in s_list:
        m = jnp.maximum(m, jnp.max(s, axis=1, keepdims=True))
    es = [jnp.exp(s - m) for s in s_list]
    esk = jnp.exp(sk - m)
    den = esk
    for e in es:
        den = den + jnp.sum(e, axis=1, keepdims=True)
    inv = 1.0 / den
    return [e * inv for e in es], esk * inv


def _window(cfg, n):
    r0 = pl.multiple_of(n * BLK, BLK)
    start = pl.multiple_of(jnp.clip((n - 1) * BLK, 0, cfg.n_lat - 3 * BLK), BLK)
    qpos = r0 + lax.broadcasted_iota(jnp.int32, (BLK, 1), 0)
    kpos = start + lax.broadcasted_iota(jnp.int32, (1, 3 * BLK), 1)
    valid = jnp.abs(qpos - kpos) <= BLK
    return r0, start, valid


def _attn_fwd(cfg, p, cos, sin, sink_rows, name):
    n_lat, n_ctx = cfg.n_lat, cfg.n_ctx

    def body(q_ref, k_ref, v_ref, qc_ref, kc_ref, vc_ref, cos_ref, sin_ref, sink_ref, o_ref, oc_ref,
             qr, ka, kb, va, vb, kca, kcb, vca, vcb):
        cos_t, sin_t = cos_ref[...], sin_ref[...]
        for gq in range(4):
            qr[:, gq * 128:(gq + 1) * 128] = _rope(q_ref[:, gq * 128:(gq + 1) * 128].astype(F32), cos_t, sin_t).astype(BF16)
        kl = _rope(k_ref[...].astype(F32), cos_t, sin_t)
        for kh in range(KV_HEADS):
            _attn_prepare(kh, kl, v_ref[...].astype(F32), kc_ref[...].astype(F32), vc_ref[...].astype(F32),
                          ka, kb, va, vb, kca, kcb, vca, vcb)

            def lat_block(n, carry):
                r0, start, valid = _window(cfg, n)
                win = pl.ds(start, 3 * BLK)
                lanes = [slice((kh * 2 + pr) * 128, (kh * 2 + pr + 1) * 128) for pr in range(2)]
                qps = [qr[pl.ds(r0, BLK), lanes[pr]] for pr in range(2)]
                kws, kcs = (ka[win, :], kb[win, :]), (kca[...], kcb[...])
                scores = [(jnp.where(valid, _dot_nt(qps[pr], kws[half]) * ATT_SCALE, NEG_INF),
                           _dot_nt(qps[pr], kcs[half]) * ATT_SCALE) for pr in range(2) for half in range(2)]
                probs = []
                for idx, (s_w, s_c) in enumerate(scores):
                    head = kh * 4 + idx
                    (p_w, p_c), _ = _softmax_parts([s_w, s_c], sink_ref[head:head + 1, 0:1])
                    probs.append((p_w.astype(BF16), p_c.astype(BF16)))
                vws, vcs = (va[win, :], vb[win, :]), (vca[...], vcb[...])
                for pr in range(2):
                    o = (_dot(probs[2 * pr][0], vws[0]) + _dot(probs[2 * pr][1], vcs[0])
                         + _dot(probs[2 * pr + 1][0], vws[1]) + _dot(probs[2 * pr + 1][1], vcs[1]))
                    o_ref[pl.ds(r0, BLK), lanes[pr]] = o.astype(BF16)
                return carry

            lax.fori_loop(0, n_lat // BLK, lat_block, 0, unroll=2)
            for n in range(n_ctx // BLK):
                rows = slice(n * BLK, (n + 1) * BLK)
                for pr in range(2):
                    lanes = slice((kh * 2 + pr) * 128, (kh * 2 + pr + 1) * 128)
                    qp = qc_ref[rows, lanes]
                    o = None
                    for half, (kcx, vcx) in enumerate(((kca, vca), (kcb, vcb))):
                        head = kh * 4 + pr * 2 + half
                        s_c = _dot_nt(qp, kcx[...]) * ATT_SCALE
                        (p_c,), _ = _softmax_parts([s_c], sink_ref[head:head + 1, 0:1])
                        part = _dot(p_c, vcx[...])
                        o = part if o is None else o + part
                    oc_ref[rows, lanes] = o.astype(BF16)

    return pl.pallas_call(
        body, grid=(2,), name=name, in_specs=_attn_specs(cfg),
        out_specs=[pl.BlockSpec((n_lat, ATT_W), lambda e: (e, 0)), pl.BlockSpec((n_ctx, ATT_W), lambda e: (e, 0))],
        out_shape=[jax.ShapeDtypeStruct((cfg.t_lat, ATT_W), BF16), jax.ShapeDtypeStruct((cfg.t_ctx, ATT_W), BF16)],
        scratch_shapes=[pltpu.VMEM((n_lat, ATT_W), BF16)] + [pltpu.VMEM((n_lat, 128), BF16)] * 4
        + [pltpu.VMEM((n_ctx, 128), BF16)] * 4,
        compiler_params=_params(("parallel",)))(p, p, p, p, p, p, cos, sin, sink_rows)


def _attn_bwd(cfg, p, dcat, cos, sin, sink_rows, name):
    n_lat, n_ctx, cb = cfg.n_lat, cfg.n_ctx, cfg.ctx_blk

    def body(q_ref, k_ref, v_ref, qc_ref, kc_ref, vc_ref, cos_ref, sin_ref, sink_ref, do_ref, doc_ref,
             dq_ref, dk_ref, dv_ref, dqc_ref, dkc_ref, dvc_ref, dsink_ref,
             qr, ka, kb, va, vb, kca, kcb, vca, vcb, dqs, dka, dva, dkca, dvca):
        cos_t, sin_t = cos_ref[...], sin_ref[...]
        lane = lax.broadcasted_iota(jnp.int32, (1, 128), 1)
        lo = lane < 64
        for gq in range(4):
            qr[:, gq * 128:(gq + 1) * 128] = _rope(q_ref[:, gq * 128:(gq + 1) * 128].astype(F32), cos_t, sin_t).astype(BF16)
        kl = _rope(k_ref[...].astype(F32), cos_t, sin_t)
        dsink_ref[...] = jnp.zeros_like(dsink_ref)
        dka[...] = jnp.zeros_like(dka)
        dva[...] = jnp.zeros_like(dva)
        dkca[...] = jnp.zeros_like(dkca)
        dvca[...] = jnp.zeros_like(dvca)

        def halves(x):
            return jnp.where(lo, x, 0).astype(BF16), jnp.where(lo, 0, x).astype(BF16)

        for kh in range(KV_HEADS):
            _attn_prepare(kh, kl, v_ref[...].astype(F32), kc_ref[...].astype(F32), vc_ref[...].astype(F32),
                          ka, kb, va, vb, kca, kcb, vca, vcb)

            def one_head(head, qp, q_half, do_p, do_half, kw, kcx, vw, vcx, win, valid):
                sk = sink_ref[head:head + 1, 0:1]
                s_list = [_dot_nt(qp, kcx[...]) * ATT_SCALE]
                if win is not None:
                    s_list.insert(0, jnp.where(valid, _dot_nt(qp, kw[win, :]) * ATT_SCALE, NEG_INF))
                probs, p_sink = _softmax_parts(s_list, sk)
                vals = [vcx[...]] if win is None else [vw[win, :], vcx[...]]
                dps = [_dot_nt(do_p, vv) for vv in vals]
                dr = None
                for pp, dp in zip(probs, dps):
                    t = jnp.sum(pp * dp, axis=1, keepdims=True)
                    dr = t if dr is None else dr + t
                dss = [(pp * (dp - dr) * ATT_SCALE).astype(BF16) for pp, dp in zip(probs, dps)]
                dsink_ref[head:head + 1, :] += jnp.broadcast_to(
                    jnp.sum(-p_sink * dr, axis=0, keepdims=True), (1, 128))
                p_c, ds_c = probs[-1], dss[-1]
                dq = _dot(ds_c, kcx[...])
                dkca[kh] += _dot_tn(ds_c, q_half)
                dvca[kh] += _dot_tn(p_c, do_half)
                if win is not None:
                    dq = dq + _dot(dss[0], kw[win, :])
                    dka[kh, win, :] += _dot_tn(dss[0], q_half)
                    dva[kh, win, :] += _dot_tn(probs[0], do_half)
                return dq

            def lat_block(n, carry):
                r0, start, valid = _window(cfg, n)
                win = pl.ds(start, 3 * BLK)
                lanes = [slice((kh * 2 + pr) * 128, (kh * 2 + pr + 1) * 128) for pr in range(2)]
                qps = [qr[pl.ds(r0, BLK), lanes[pr]] for pr in range(2)]
                dops = [do_ref[pl.ds(r0, BLK), lanes[pr]].astype(BF16) for pr in range(2)]
                heads = [(pr, half) for pr in range(2) for half in range(2)]
                kws, kcs = (ka[win, :], kb[win, :]), (kca[...], kcb[...])
                vws, vcs = (va[win, :], vb[win, :]), (vca[...], vcb[...])
                soft = []
                for idx, (pr, half) in enumerate(heads):
                    s_w = jnp.where(valid, _dot_nt(qps[pr], kws[half]) * ATT_SCALE, NEG_INF)
                    s_c = _dot_nt(qps[pr], kcs[half]) * ATT_SCALE
                    soft.append(_softmax_parts([s_w, s_c], sink_ref[kh * 4 + idx:kh * 4 + idx + 1, 0:1]))
                dps = [(_dot_nt(dops[pr], vws[half]), _dot_nt(dops[pr], vcs[half])) for pr, half in heads]
                ds_w, ds_c, pb_w, pb_c = [], [], [], []
                for idx in range(4):
                    (p_w, p_c), p_sink = soft[idx]
                    dp_w, dp_c = dps[idx]
                    dr = jnp.sum(p_w * dp_w, axis=1, keepdims=True) + jnp.sum(p_c * dp_c, axis=1, keepdims=True)
                    ds_w.append((p_w * (dp_w - dr) * ATT_SCALE).astype(BF16))
                    ds_c.append((p_c * (dp_c - dr) * ATT_SCALE).astype(BF16))
                    pb_w.append(p_w.astype(BF16))
                    pb_c.append(p_c.astype(BF16))
                    head = kh * 4 + idx
                    dsink_ref[head:head + 1, :] += jnp.broadcast_to(
                        jnp.sum(-p_sink * dr, axis=0, keepdims=True), (1, 128))
                for pr in range(2):
                    dqs[pl.ds(r0, BLK), lanes[pr]] = (
                        _dot(ds_w[2 * pr], kws[0]) + _dot(ds_c[2 * pr], kcs[0])
                        + _dot(ds_w[2 * pr + 1], kws[1]) + _dot(ds_c[2 * pr + 1], kcs[1]))
                q_hs, do_hs = [halves(qp) for qp in qps], [halves(do_p) for do_p in dops]
                q_stack = jnp.concatenate([q_hs[pr][half] for pr, half in heads], axis=0)
                do_stack = jnp.concatenate([do_hs[pr][half] for pr, half in heads], axis=0)
                dka[kh, win, :] += _dot_tn(jnp.concatenate(ds_w, axis=0), q_stack)
                dva[kh, win, :] += _dot_tn(jnp.concatenate(pb_w, axis=0), do_stack)
                dkca[kh] += _dot_tn(jnp.concatenate(ds_c, axis=0), q_stack)
                dvca[kh] += _dot_tn(jnp.concatenate(pb_c, axis=0), do_stack)
                return carry

            lax.fori_loop(0, n_lat // BLK, lat_block, 0, unroll=2)
            for n in range(n_ctx // BLK):
                rows = slice(n * BLK, (n + 1) * BLK)
                for pr in range(2):
                    lanes = slice((kh * 2 + pr) * 128, (kh * 2 + pr + 1) * 128)
                    qp = qc_ref[rows, lanes].astype(BF16)
                    do_p = doc_ref[rows, lanes]
                    q_h, do_h = halves(qp), halves(do_p)
                    dq = None
                    for half, (kcx, vcx) in enumerate(((kca, vca), (kcb, vcb))):
                        part = one_head(kh * 4 + pr * 2 + half, qp, q_h[half], do_p, do_h[half],
                                        None, kcx, None, vcx, None, None)
                        dq = part if dq is None else dq + part
                    dqc_ref[rows, lanes] = dq.astype(BF16)

        def fold(acc):
            r0 = acc[0] + pltpu.roll(acc[0], 64, 1)
            r1 = acc[1] + pltpu.roll(acc[1], 64, 1)
            return jnp.where(lo, r0, r1)

        for gq in range(4):
            sl = slice(gq * 128, (gq + 1) * 128)
            dq_ref[:, sl] = _rope_t(dqs[:, sl], cos_t, sin_t).astype(BF16)
        dk_ref[...] = _rope_t(fold(dka), cos_t, sin_t).astype(BF16)
        dv_ref[...] = fold(dva).astype(BF16)
        dkc_ref[...] = fold(dkca).astype(BF16)
        dvc_ref[...] = fold(dvca).astype(BF16)

    lat = lambda w: pl.BlockSpec((n_lat, w), lambda e: (e, 0))
    ctx = lambda w: pl.BlockSpec((n_ctx, w), lambda e: (e, 0))
    sd = jax.ShapeDtypeStruct
    return pl.pallas_call(
        body, grid=(2,), name=name,
        in_specs=_attn_specs(cfg) + [pl.BlockSpec((n_lat, ATT_W), lambda e: (e, 0)),
                                     pl.BlockSpec((n_ctx, ATT_W), lambda e: (cb + e, 0))],
        out_specs=[lat(ATT_W), lat(128), lat(128), ctx(ATT_W), ctx(128), ctx(128),
                   pl.BlockSpec((None, 8, 128), lambda e: (e, 0, 0))],
        out_shape=[sd((cfg.t_lat, ATT_W), BF16), sd((cfg.t_lat, 128), BF16), sd((cfg.t_lat, 128), BF16),
                   sd((cfg.t_ctx, ATT_W), BF16), sd((cfg.t_ctx, 128), BF16), sd((cfg.t_ctx, 128), BF16),
                   sd((2, 8, 128), F32)],
        scratch_shapes=[pltpu.VMEM((n_lat, ATT_W), BF16)] + [pltpu.VMEM((n_lat, 128), BF16)] * 4
        + [pltpu.VMEM((n_ctx, 128), BF16)] * 4
        + [pltpu.VMEM((n_lat, ATT_W), F32), pltpu.VMEM((2, n_lat, 128), F32), pltpu.VMEM((2, n_lat, 128), F32),
           pltpu.VMEM((2, n_ctx, 128), F32), pltpu.VMEM((2, n_ctx, 128), F32)],
        compiler_params=_params(("parallel",)))(p, p, p, p, p, p, cos, sin, sink_rows, dcat, dcat)


def _shift_down(x, k, row):
    return jnp.where(row >= k, pltpu.roll(x, k, 0), 0.0)


def _shift_up(x, k, row):
    n = x.shape[0]
    return jnp.where(row < n - k, pltpu.roll(x, n - k, 0), 0.0)


def _window_sum(x, r, row):
    below, above, k = x, x, 1
    while k < r:
        below = below + _shift_down(below, k, row)
        above = above + _shift_up(above, k, row)
        k *= 2
    return below + _shift_down(x, r, row) + _shift_up(above, 1, row)


def _inv_count(r, row, n):
    cnt = jnp.minimum(row + r, n - 1) + 1 - jnp.maximum(row - r, 0)
    return 1.0 / cnt.astype(F32)


def _pool_fwd(p, w, scale, n, blk0, n_seg, name):
    def body(u0, u1, u2, u3, w_ref, sc_ref, o_ref):
        row = lax.broadcasted_iota(jnp.int32, (n, 1), 0)
        for g, u_ref in enumerate((u0, u1, u2, u3)):
            u = u_ref[...].astype(F32)
            d = _window_sum(u, POOL_R[g], row) * _inv_count(POOL_R[g], row, n) - u
            o_ref[:, g * 128:(g + 1) * 128] = (_dot(d, w_ref[g]) * sc_ref[:, g * 128:(g + 1) * 128]).astype(BF16)

    return pl.pallas_call(
        body, grid=(n_seg,), name=name,
        in_specs=[pl.BlockSpec((n, 128), functools.partial(lambda g, e: (blk0 + e, 6 + g), g)) for g in range(4)]
        + [pl.BlockSpec((4, 128, 128), lambda e: (0, 0, 0)), pl.BlockSpec((1, 512), lambda e: (0, 0))],
        out_specs=pl.BlockSpec((n, 512), lambda e: (e, 0)),
        out_shape=jax.ShapeDtypeStruct((n_seg * n, 512), BF16),
        compiler_params=_params(("parallel",)))(p, p, p, p, w, scale)


def _pool_bwd(p, w, scale, dcat, n, blk0, n_seg, name):
    def body(u0, u1, u2, u3, w_ref, sc_ref, dp_ref, du_ref, dw_ref, dsc_ref):
        e = pl.program_id(0)

        @pl.when(e == 0)
        def _():
            dw_ref[...] = jnp.zeros_like(dw_ref)
            dsc_ref[...] = jnp.zeros_like(dsc_ref)

        row = lax.broadcasted_iota(jnp.int32, (n, 1), 0)
        for g, u_ref in enumerate((u0, u1, u2, u3)):
            sl = slice(g * 128, (g + 1) * 128)
            u = u_ref[...].astype(F32)
            inv = _inv_count(POOL_R[g], row, n)
            d = _window_sum(u, POOL_R[g], row) * inv - u
            dp = dp_ref[:, sl].astype(F32)
            dsc_ref[:, sl] += jnp.sum(dp * _dot(d, w_ref[g]), axis=0, keepdims=True)
            dyp = dp * sc_ref[:, sl]
            dw_ref[g] += _dot_tn(d, dyp)
            dd = _dot_nt(dyp, w_ref[g])
            du_ref[:, sl] = (_window_sum(dd * inv, POOL_R[g], row) - dd).astype(BF16)

    return pl.pallas_call(
        body, grid=(n_seg,), name=name,
        in_specs=[pl.BlockSpec((n, 128), functools.partial(lambda g, e: (blk0 + e, 6 + g), g)) for g in range(4)]
        + [pl.BlockSpec((4, 128, 128), lambda e: (0, 0, 0)), pl.BlockSpec((1, 512), lambda e: (0, 0)),
           pl.BlockSpec((n, 512), lambda e: (blk0 + e, 1))],
        out_specs=[pl.BlockSpec((n, 512), lambda e: (e, 0)),
                   pl.BlockSpec((4, 128, 128), lambda e: (0, 0, 0)), pl.BlockSpec((1, 512), lambda e: (0, 0))],
        out_shape=[jax.ShapeDtypeStruct((n_seg * n, 512), BF16), jax.ShapeDtypeStruct((4, 128, 128), F32),
                   jax.ShapeDtypeStruct((1, 512), F32)],
        compiler_params=_params(("arbitrary",)))(p, p, p, p, w, scale, dcat)


def _gelu(x):
    t = jnp.tanh(math.sqrt(2.0 / math.pi) * (x + 0.044715 * x * x * x))
    return 0.5 * x * (1.0 + t), t


def _gelu_grad(x, t):
    return 0.5 * (1.0 + t) + 0.5 * x * (1.0 - t * t) * (math.sqrt(2.0 / math.pi) * (1.0 + 3 * 0.044715 * x * x))


def _neg_expm1_twice(x):
    t = jnp.tanh(x)
    return (-2.0 * t) / (1.0 - t)


def _softplus_neg(lam):
    x = -lam
    e = jnp.exp(-jnp.abs(x))
    log1p = jnp.where(e < 1e-2, e * (1.0 - e * (0.5 - e * (1.0 / 3.0))), jnp.log(1.0 + e))
    return jnp.maximum(x, 0.0) + log1p, -_sigmoid(x)


def _conv(u, w_ref, b_ref, row):
    return (b_ref[...] + _shift_down(u, 1, row) * w_ref[0:1, :] + u * w_ref[1:2, :]
            + _shift_up(u, 1, row) * w_ref[2:3, :] + _shift_up(u, 2, row) * w_ref[3:4, :])


def _lru_gates(uc, d, wa_ref, ba_ref, wx_ref, bx_ref, lam_ref):
    r = _sigmoid(_dot(uc, wa_ref[d]) + ba_ref[d:d + 1, :])
    gi = _sigmoid(_dot(uc, wx_ref[d]) + bx_ref[d:d + 1, :])
    sp, dsp = _softplus_neg(lam_ref[d:d + 1, :])
    la = (-LRU_C) * r * sp
    a = jnp.exp(la)
    sq = jnp.sqrt(_neg_expm1_twice(la))
    return r, gi, sp, dsp, a, sq


def _tile_scan(a_ref, b_ref, n, reverse):
    m = n // 8
    first = 7 if reverse else 0
    a_prev = a_ref[pl.ds(first, m, stride=8), :]
    b_prev = b_ref[pl.ds(first, m, stride=8), :]
    for j in (range(6, -1, -1) if reverse else range(1, 8)):
        rows = pl.ds(j, m, stride=8)
        aj = a_ref[rows, :]
        b_prev = aj * b_prev + b_ref[rows, :]
        a_prev = aj * a_prev
        b_ref[rows, :] = b_prev
        a_ref[rows, :] = a_prev


def _carry_scan(a_ref, b_ref, n, reverse, carry):
    nt8 = n // 8

    def step(i, c):
        t = (nt8 - 1 - i) if reverse else i
        off = pl.multiple_of(t * 8, 8)
        h = a_ref[pl.ds(off, 8), :] * c + b_ref[pl.ds(off, 8), :]
        b_ref[pl.ds(off, 8), :] = h
        return h[0:1, :] if reverse else h[7:8, :]

    return lax.fori_loop(0, nt8, step, carry, unroll=4)


def _chain_scan(segs, reverse):
    carry = jnp.zeros((1, 128), F32)
    for a, b, a_ref, b_ref, n in segs:
        a_ref[...] = a
        b_ref[...] = b
        _tile_scan(a_ref, b_ref, n, reverse)
        carry = _carry_scan(a_ref, b_ref, n, reverse, carry)


def _lru_specs(cfg):
    n_lat, n_ctx, cb = cfg.n_lat, cfg.n_ctx, cfg.ctx_blk
    return [pl.BlockSpec((n_lat, 128), lambda hb, e: (e, hb)),
            pl.BlockSpec((n_lat, 128), lambda hb, e: (e, 8 + hb)),
            pl.BlockSpec((n_ctx, 128), lambda hb, e: (cb + e, hb)),
            pl.BlockSpec((n_ctx, 128), lambda hb, e: (cb + e, 8 + hb)),
            pl.BlockSpec((4, 128), lambda hb, e: (0, hb)),
            pl.BlockSpec((1, 128), lambda hb, e: (0, hb)),
            pl.BlockSpec((2, None, 128, 128), lambda hb, e: (0, hb, 0, 0)),
            pl.BlockSpec((2, 128), lambda hb, e: (0, hb)),
            pl.BlockSpec((2, None, 128, 128), lambda hb, e: (0, hb, 0, 0)),
            pl.BlockSpec((2, 128), lambda hb, e: (0, hb)),
            pl.BlockSpec((2, 128), lambda hb, e: (0, hb))]


def _lru_fwd(cfg, p, consts, name):
    n_lat, n_ctx = cfg.n_lat, cfg.n_ctx

    def body(gl_ref, ul_ref, gc_ref, uc_ref, cw_ref, cb_ref, wa_ref, ba_ref, wx_ref, bx_ref, lam_ref,
             zl_ref, zc_ref, hl_ref, hc_ref, al, ac):
        row_l = lax.broadcasted_iota(jnp.int32, (n_lat, 1), 0)
        row_c = lax.broadcasted_iota(jnp.int32, (n_ctx, 1), 0)
        uc_l = _conv(ul_ref[...].astype(F32), cw_ref, cb_ref, row_l)
        uc_c = _conv(uc_ref[...].astype(F32), cw_ref, cb_ref, row_c)
        for d in range(2):
            _, gi_l, _, _, a_l, sq_l = _lru_gates(uc_l, d, wa_ref, ba_ref, wx_ref, bx_ref, lam_ref)
            _, gi_c, _, _, a_c, sq_c = _lru_gates(uc_c, d, wa_ref, ba_ref, wx_ref, bx_ref, lam_ref)
            _chain_scan([(a_c, sq_c * (gi_c * uc_c), ac, hc_ref.at[d], n_ctx),
                         (a_l, sq_l * (gi_l * uc_l), al, hl_ref.at[d], n_lat)], reverse=(d == 1))
        zl_ref[...] = (_gelu(gl_ref[...].astype(F32))[0] * (hl_ref[0] + hl_ref[1])).astype(BF16)
        zc_ref[...] = (_gelu(gc_ref[...].astype(F32))[0] * (hc_ref[0] + hc_ref[1])).astype(BF16)

    return pl.pallas_call(
        body, grid=(8, 2), name=name, in_specs=_lru_specs(cfg),
        out_specs=[pl.BlockSpec((n_lat, 128), lambda hb, e: (e, hb)), pl.BlockSpec((n_ctx, 128), lambda hb, e: (e, hb)),
                   pl.BlockSpec((2, n_lat, 128), lambda hb, e: (0, e, hb)),
                   pl.BlockSpec((2, n_ctx, 128), lambda hb, e: (0, e, hb))],
        out_shape=[jax.ShapeDtypeStruct((cfg.t_lat, D), BF16), jax.ShapeDtypeStruct((cfg.t_ctx, D), BF16),
                   jax.ShapeDtypeStruct((2, cfg.t_lat, D), F32), jax.ShapeDtypeStruct((2, cfg.t_ctx, D), F32)],
        scratch_shapes=[pltpu.VMEM((n_lat, 128), F32), pltpu.VMEM((n_ctx, 128), F32)],
        compiler_params=_params(("parallel", "arbitrary")))(p, p, p, p, *consts)


def _lru_bwd(cfg, p, dz, h_lat, h_ctx, consts, name):
    n_lat, n_ctx, cb = cfg.n_lat, cfg.n_ctx, cfg.ctx_blk

    def body(gl_ref, ul_ref, gc_ref, uc_ref, cw_ref, cb_ref, wa_ref, ba_ref, wx_ref, bx_ref, lam_ref,
             dzl_ref, dzc_ref, hl, hc, dgl_ref, dul_ref, dgc_ref, duc_ref, dwa_ref, dwx_ref, vec_ref,
             al, bl, ac, bc):
        e = pl.program_id(1)

        @pl.when(e == 0)
        def _():
            dwa_ref[...] = jnp.zeros_like(dwa_ref)
            dwx_ref[...] = jnp.zeros_like(dwx_ref)
            vec_ref[...] = jnp.zeros_like(vec_ref)

        row_l = lax.broadcasted_iota(jnp.int32, (n_lat, 1), 0)
        row_c = lax.broadcasted_iota(jnp.int32, (n_ctx, 1), 0)
        u_l, u_c = ul_ref[...].astype(F32), uc_ref[...].astype(F32)
        uc_l = _conv(u_l, cw_ref, cb_ref, row_l)
        uc_c = _conv(u_c, cw_ref, cb_ref, row_c)
        gel_l, t_l = _gelu(gl_ref[...].astype(F32))
        gel_c, t_c = _gelu(gc_ref[...].astype(F32))
        dz_l, dz_c = dzl_ref[...].astype(F32), dzc_ref[...].astype(F32)
        dgl_ref[...] = (dz_l * (hl[0] + hl[1]) * _gelu_grad(gl_ref[...].astype(F32), t_l)).astype(BF16)
        dgc_ref[...] = (dz_c * (hc[0] + hc[1]) * _gelu_grad(gc_ref[...].astype(F32), t_c)).astype(BF16)
        dy_l, dy_c = dz_l * gel_l, dz_c * gel_c
        duc_l = jnp.zeros((n_lat, 128), F32)
        duc_c = jnp.zeros((n_ctx, 128), F32)
        for d in range(2):
            r_l, gi_l, sp, dsp, a_l, sq_l = _lru_gates(uc_l, d, wa_ref, ba_ref, wx_ref, bx_ref, lam_ref)
            r_c, gi_c, _, _, a_c, sq_c = _lru_gates(uc_c, d, wa_ref, ba_ref, wx_ref, bx_ref, lam_ref)
            if d == 0:
                an_l = _shift_up(a_l, 1, row_l)
                an_c = jnp.where(row_c < n_ctx - 1, pltpu.roll(a_c, n_ctx - 1, 0), a_l[0:1, :])
            else:
                an_l = _shift_down(a_l, 1, row_l)
                an_c = jnp.where(row_c >= 1, pltpu.roll(a_c, 1, 0), a_l[n_lat - 1:n_lat, :])
            _chain_scan([(an_l, dy_l, al, bl, n_lat), (an_c, dy_c, ac, bc, n_ctx)], reverse=(d == 0))
            dsp_sum = jnp.zeros((1, 128), F32)
            for (dh, h, r, gi, a, sq, uc, seg) in ((bl[...], hl[d], r_l, gi_l, a_l, sq_l, uc_l, "l"),
                                                  (bc[...], hc[d], r_c, gi_c, a_c, sq_c, uc_c, "c")):
                b0 = sq * (gi * uc)
                t1 = dh * sq
                dla = dh * (h - b0) - (dh * gi * uc) * (a * a) / sq
                dzr = (dla * ((-LRU_C) * sp)) * r * (1.0 - r)
                dzi = (t1 * uc) * gi * (1.0 - gi)
                dsp_sum = dsp_sum + jnp.sum(dla * ((-LRU_C) * r), axis=0, keepdims=True)
                dwa_ref[d] += _dot_tn(uc, dzr)
                dwx_ref[d] += _dot_tn(uc, dzi)
                vec_ref[d:d + 1, :] += jnp.sum(dzr, axis=0, keepdims=True)
                vec_ref[2 + d:3 + d, :] += jnp.sum(dzi, axis=0, keepdims=True)
                duc = t1 * gi + _dot_nt(dzr, wa_ref[d]) + _dot_nt(dzi, wx_ref[d])
                if seg == "l":
                    duc_l = duc_l + duc
                else:
                    duc_c = duc_c + duc
            vec_ref[4 + d:5 + d, :] += dsp_sum * dsp
        for duc, u, row, du_ref in ((duc_l, u_l, row_l, dul_ref), (duc_c, u_c, row_c, duc_ref)):
            du_ref[...] = (_shift_up(duc, 1, row) * cw_ref[0:1, :] + duc * cw_ref[1:2, :]
                           + _shift_down(duc, 1, row) * cw_ref[2:3, :]
                           + _shift_down(duc, 2, row) * cw_ref[3:4, :]).astype(BF16)
            vec_ref[6:7, :] += jnp.sum(duc * _shift_down(u, 1, row), axis=0, keepdims=True)
            vec_ref[7:8, :] += jnp.sum(duc * u, axis=0, keepdims=True)
            vec_ref[8:9, :] += jnp.sum(duc * _shift_up(u, 1, row), axis=0, keepdims=True)
            vec_ref[9:10, :] += jnp.sum(duc * _shift_up(u, 2, row), axis=0, keepdims=True)
            vec_ref[10:11, :] += jnp.sum(duc, axis=0, keepdims=True)

    lat = pl.BlockSpec((n_lat, 128), lambda hb, e: (e, hb))
    ctx = pl.BlockSpec((n_ctx, 128), lambda hb, e: (e, hb))
    wspec = pl.BlockSpec((2, None, 128, 128), lambda hb, e: (0, hb, 0, 0))
    sd = jax.ShapeDtypeStruct
    return pl.pallas_call(
        body, grid=(8, 2), name=name,
        in_specs=_lru_specs(cfg) + [pl.BlockSpec((n_lat, 128), lambda hb, e: (e, hb)),
                                    pl.BlockSpec((n_ctx, 128), lambda hb, e: (cb + e, hb)),
                                    pl.BlockSpec((2, n_lat, 128), lambda hb, e: (0, e, hb)),
                                    pl.BlockSpec((2, n_ctx, 128), lambda hb, e: (0, e, hb))],
        out_specs=[lat, lat, ctx, ctx, wspec, wspec, pl.BlockSpec((None, 16, 128), lambda hb, e: (hb, 0, 0))],
        out_shape=[sd((cfg.t_lat, D), BF16), sd((cfg.t_lat, D), BF16), sd((cfg.t_ctx, D), BF16), sd((cfg.t_ctx, D), BF16),
                   sd((2, 8, 128, 128), F32), sd((2, 8, 128, 128), F32), sd((8, 16, 128), F32)],
        scratch_shapes=[pltpu.VMEM((n_lat, 128), F32)] * 2 + [pltpu.VMEM((n_ctx, 128), F32)] * 2,
        compiler_params=_params(("parallel", "arbitrary")))(p, p, p, p, *consts, dz, dz, h_lat, h_ctx)


def _position():
    x, y, c = lax.axis_index("x"), lax.axis_index("y"), lax.axis_index("c")
    return x, y, c, 4 * x + 2 * y + c


def _peer(x, y, c, k):
    px = 1 - x if k & 4 else x
    py = 1 - y if k & 2 else y
    pc = 1 - c if k & 1 else c
    return (px, py, pc), 4 * px + 2 * py + pc


def _all_gather(v, name, in_vmem):
    def body(v_ref, o_ref, send_sems, recv_sems, local_sem):
        x, y, c, me = _position()
        mine = pltpu.make_async_copy(v_ref, o_ref.at[me], local_sem)
        mine.start()
        sends = []
        for k in range(1, N_DEV):
            peer, _ = _peer(x, y, c, k)
            cp = pltpu.make_async_remote_copy(src_ref=v_ref, dst_ref=o_ref.at[me], send_sem=send_sems.at[k - 1],
                                              recv_sem=recv_sems.at[k - 1], device_id=peer, device_id_type=MESH)
            cp.start()
            sends.append(cp)
        for k in range(1, N_DEV):
            peer, peer_lin = _peer(x, y, c, k)
            pltpu.make_async_remote_copy(src_ref=v_ref, dst_ref=o_ref.at[peer_lin], send_sem=send_sems.at[k - 1],
                                         recv_sem=recv_sems.at[k - 1], device_id=peer, device_id_type=MESH).wait_recv()
        for cp in sends:
            cp.wait_send()
        mine.wait()

    space = pltpu.VMEM if in_vmem else pl.ANY
    return pl.pallas_call(
        body, name=name,
        in_specs=[pl.BlockSpec(memory_space=space)], out_specs=pl.BlockSpec(memory_space=space),
        out_shape=jax.ShapeDtypeStruct((N_DEV,) + v.shape, v.dtype),
        scratch_shapes=[pltpu.SemaphoreType.DMA((N_DEV - 1,)), pltpu.SemaphoreType.DMA((N_DEV - 1,)),
                        pltpu.SemaphoreType.DMA],
        compiler_params=pltpu.CompilerParams(vmem_limit_bytes=VMEM_LIMIT))(v)


_HBM = pl.BlockSpec(memory_space=pltpu.HBM)
_SEM = pl.BlockSpec(memory_space=pltpu.SEMAPHORE)
_EFFECT = pltpu.SideEffectType.DATAFLOW_SIDE_EFFECTING


ALL_PEERS = tuple(range(1, N_DEV))
SAME_CORE_AND_SIBLING = (1, 2, 4, 6)


def _push_start(src, land, block_of, name, relations=ALL_PEERS):
    def body(src_ref, land_ref, send_sem, recv_sem, src_thru, land_thru, token):
        x, y, c, me = _position()
        for k in relations:
            peer, peer_lin = _peer(x, y, c, k)
            mine, there = block_of(src_ref, land_ref, me, peer_lin)
            pltpu.make_async_remote_copy(src_ref=mine, dst_ref=there, send_sem=send_sem, recv_sem=recv_sem,
                                         device_id=peer, device_id_type=MESH).start()
        mine, here = block_of(src_ref, land_ref, me, me)
        pltpu.make_async_copy(mine, here, recv_sem).start()
        token[...] = jnp.zeros_like(token)

    return pl.pallas_call(
        body, name=name,
        out_shape=(pltpu.SemaphoreType.DMA(()), pltpu.SemaphoreType.DMA(()), pltpu.HBM(src.shape, src.dtype),
                   pltpu.HBM(land.shape, land.dtype), jax.ShapeDtypeStruct((8, 128), F32)),
        in_specs=(_HBM, _HBM), out_specs=(_SEM, _SEM, _HBM, _HBM, pl.BlockSpec(memory_space=pltpu.VMEM)),
        input_output_aliases={0: 2, 1: 3},
        compiler_params=pltpu.CompilerParams(has_side_effects=_EFFECT),
    )(pltpu.with_memory_space_constraint(src, pltpu.HBM), pltpu.with_memory_space_constraint(land, pltpu.HBM))


def _push_wait(handle, blocks_of, after, name, n_peers=N_DEV - 1):
    send_sem, recv_sem, src_thru, land_thru, _ = handle

    def body(src_ref, land_ref, send_sem, recv_sem, after_ref, src_dead, got_ref):
        x, y, c, _ = _position()
        sent, landed = blocks_of(land_ref, n_peers), blocks_of(land_ref, n_peers + 1)
        pltpu.make_async_remote_copy(src_ref=sent, dst_ref=sent, send_sem=send_sem, recv_sem=recv_sem,
                                     device_id=(x, y, 1 - c), device_id_type=MESH).wait_send()
        pltpu.make_async_remote_copy(src_ref=landed, dst_ref=landed, send_sem=send_sem, recv_sem=recv_sem,
                                     device_id=(x, y, 1 - c), device_id_type=MESH).wait_recv()

    return pl.pallas_call(
        body, name=name,
        out_shape=(pltpu.HBM(src_thru.shape, src_thru.dtype), pltpu.HBM(land_thru.shape, land_thru.dtype)),
        in_specs=(_HBM, _HBM, _SEM, _SEM, pl.BlockSpec(memory_space=pl.ANY)), out_specs=(_HBM, _HBM),
        input_output_aliases={0: 0, 1: 1},
        compiler_params=pltpu.CompilerParams(has_side_effects=_EFFECT),
    )(src_thru, land_thru, send_sem, recv_sem, after)[1]


def _gather_start(src, name, relations=ALL_PEERS):
    g, r, C = src.shape
    land = lax.empty((g, N_DEV * r, C), src.dtype)
    return _push_start(src, land, lambda s, z, i, p: (s, z.at[:, pl.ds(i * r, r), :]), name, relations)


def _gather_wait(handle, after, name, n_peers=N_DEV - 1):
    r = handle[2].shape[1]
    return _push_wait(handle, lambda z, n: z.at[:, pl.ds(0, n * r), :], after, name, n_peers)


def _relay_start(land, r, name):
    def body(land_ref, send_sem, recv_sem, land_thru, token):
        x, y, c, _ = _position()
        for k in (2, 4, 6):
            _, origin = _peer(x, y, c, k)
            rows = land_ref.at[:, pl.ds(origin * r, r), :]
            pltpu.make_async_remote_copy(src_ref=rows, dst_ref=rows, send_sem=send_sem, recv_sem=recv_sem,
                                         device_id=(x, y, 1 - c), device_id_type=MESH).start()
        token[...] = jnp.zeros_like(token)

    return pl.pallas_call(
        body, name=name,
        out_shape=(pltpu.SemaphoreType.DMA(()), pltpu.SemaphoreType.DMA(()), pltpu.HBM(land.shape, land.dtype),
                   jax.ShapeDtypeStruct((8, 128), F32)),
        in_specs=(_HBM,), out_specs=(_SEM, _SEM, _HBM, pl.BlockSpec(memory_space=pltpu.VMEM)),
        input_output_aliases={0: 2},
        compiler_params=pltpu.CompilerParams(has_side_effects=_EFFECT),
    )(pltpu.with_memory_space_constraint(land, pltpu.HBM))


def _relay_wait(handle, r, after, name):
    send_sem, recv_sem, land_thru, _ = handle

    def body(land_ref, send_sem, recv_sem, after_ref, got_ref):
        x, y, c, _ = _position()
        three = land_ref.at[:, pl.ds(0, 3 * r), :]
        cp = pltpu.make_async_remote_copy(src_ref=three, dst_ref=three, send_sem=send_sem, recv_sem=recv_sem,
                                          device_id=(x, y, 1 - c), device_id_type=MESH)
        cp.wait_send()
        cp.wait_recv()

    return pl.pallas_call(
        body, name=name, out_shape=(pltpu.HBM(land_thru.shape, land_thru.dtype),),
        in_specs=(_HBM, _SEM, _SEM, pl.BlockSpec(memory_space=pl.ANY)), out_specs=(_HBM,),
        input_output_aliases={0: 0},
        compiler_params=pltpu.CompilerParams(has_side_effects=_EFFECT),
    )(land_thru, send_sem, recv_sem, after)[0]


def _exchange_start(grad, name):
    g, rows, C = grad.shape
    r = rows // N_DEV
    land = lax.empty((N_DEV, g, r, C), grad.dtype)
    return _push_start(grad, land, lambda s, z, i, p: (s.at[:, pl.ds(p * r, r), :], z.at[i]), name)


def _exchange_wait(handle, after, name):
    return _push_wait(handle, lambda z, n: z.at[pl.ds(0, n)], after, name)


def _sum_blocks(v, name):
    k, rows, cols = v.shape
    tr = rows
    for cand in (rows, 512, 352, 256, 176, 128, 64, 32, 16):
        if rows % cand == 0 and k * cand * cols * v.dtype.itemsize <= 6 * 1024 * 1024:
            tr = cand
            break

    def body(v_ref, o_ref):
        acc = v_ref[0].astype(F32)
        for s in range(1, k):
            acc = acc + v_ref[s].astype(F32)
        o_ref[...] = acc

    return pl.pallas_call(
        body, grid=(rows // tr,), name=name,
        in_specs=[pl.BlockSpec((k, tr, cols), lambda i: (0, i, 0))],
        out_specs=pl.BlockSpec((tr, cols), lambda i: (i, 0)),
        out_shape=jax.ShapeDtypeStruct((rows, cols), F32),
        compiler_params=_params(("parallel",)))(v)


def _adam_math(w, g, m, v):
    m2 = B1 * m + (1.0 - B1) * g
    v2 = B2 * v + (1.0 - B2) * (g * g)
    m_hat = m2 / (1.0 - B1 ** STEP)
    v_hat = v2 / (1.0 - B2 ** STEP)
    return -LR * (m_hat / (jnp.sqrt(v_hat) + EPS) + WD * w), m2, v2


def _adamw(w, g, m, v, name, dep=None):
    shp = w.shape
    rows, cols = (shp[-2], shp[-1]) if len(shp) >= 2 else (1, shp[-1])
    lead = math.prod(shp[:-2]) if len(shp) > 2 else 1
    fits = [t for t in range(8, rows + 1, 8) if rows % t == 0 and t * cols * 4 <= 2 * 1024 * 1024]
    tr = max(fits) if fits else rows

    def body(w_ref, g_ref, m_ref, v_ref, *rest):
        d_ref, m2_ref, v2_ref = rest[-3:]
        d_ref[...], m2_ref[...], v2_ref[...] = _adam_math(w_ref[...], g_ref[...], m_ref[...], v_ref[...])

    blk = pl.BlockSpec((None, tr, cols), lambda b, i: (b, i, 0))
    extra = [] if dep is None else [dep]
    outs = pl.pallas_call(
        body, grid=(lead, rows // tr), name=name,
        in_specs=[blk] * 4 + [pl.BlockSpec(memory_space=pl.ANY)] * len(extra), out_specs=[blk] * 3,
        out_shape=[jax.ShapeDtypeStruct((lead, rows, cols), F32)] * 3,
        compiler_params=_params(("parallel", "parallel")))(*[a.reshape(lead, rows, cols) for a in (w, g, m, v)], *extra)
    return [o.reshape(shp) for o in outs]


def _as2d(a):
    n = a.size
    if n % 1024 == 0:
        return a.reshape(n // 1024, 1024)
    if n % 128 == 0:
        return a.reshape(n // 128, 128)
    return a.reshape(1, n)


def _blocks_to_cols(a):
    b = jnp.moveaxis(a, 0, -2)
    return b.reshape(b.shape[:-2] + (b.shape[-2] * b.shape[-1],))


def _pack_rows(parts):
    padded, offs, r = [], [], 0
    for p in parts:
        pad = (-p.shape[0]) % 8
        padded.append(jnp.pad(p, ((0, pad), (0, 0))) if pad else p)
        offs.append(r)
        r += p.shape[0] + pad
    return jnp.concatenate(padded, axis=0), offs


def _silu(x):
    return x * jax.nn.sigmoid(x)


def kernel(x, c, ctx, c_ctx, w_mod, b_mod, ln_g, ln_b, ffn_w_gate, ffn_w_up, ffn_w_down, mix_ab_w_in, attn_sink, pool_w, pool_scale, mix_ab_w_out, lru_w_in, lru_conv_w, lru_conv_b, lru_wa, lru_ba, lru_wx, lru_bx, lru_lambda, lru_w_out, loss_target, m_c_ctx, m_w_mod, m_b_mod, m_ln_g, m_ln_b, m_ffn_w_gate, m_ffn_w_up, m_ffn_w_down, m_mix_ab_w_in, m_attn_sink, m_pool_w, m_pool_scale, m_mix_ab_w_out, m_lru_w_in, m_lru_conv_w, m_lru_conv_b, m_lru_wa, m_lru_ba, m_lru_wx, m_lru_bx, m_lru_lambda, m_lru_w_out, v_c_ctx, v_w_mod, v_b_mod, v_ln_g, v_ln_b, v_ffn_w_gate, v_ffn_w_up, v_ffn_w_down, v_mix_ab_w_in, v_attn_sink, v_pool_w, v_pool_scale, v_mix_ab_w_out, v_lru_w_in, v_lru_conv_w, v_lru_conv_b, v_lru_wa, v_lru_ba, v_lru_wx, v_lru_bx, v_lru_lambda, v_lru_w_out):
    weights = dict(c_ctx=c_ctx, w_mod=w_mod, b_mod=b_mod, ln_g=ln_g, ln_b=ln_b, ffn_w_gate=ffn_w_gate,
                   ffn_w_up=ffn_w_up, ffn_w_down=ffn_w_down, mix_ab_w_in=mix_ab_w_in, attn_sink=attn_sink,
                   pool_w=pool_w, pool_scale=pool_scale, mix_ab_w_out=mix_ab_w_out, lru_w_in=lru_w_in,
                   lru_conv_w=lru_conv_w, lru_conv_b=lru_conv_b, lru_wa=lru_wa, lru_ba=lru_ba, lru_wx=lru_wx,
                   lru_bx=lru_bx, lru_lambda=lru_lambda, lru_w_out=lru_w_out)
    mom_m = dict(c_ctx=m_c_ctx, w_mod=m_w_mod, b_mod=m_b_mod, ln_g=m_ln_g, ln_b=m_ln_b, ffn_w_gate=m_ffn_w_gate,
                 ffn_w_up=m_ffn_w_up, ffn_w_down=m_ffn_w_down, mix_ab_w_in=m_mix_ab_w_in, attn_sink=m_attn_sink,
                 pool_w=m_pool_w, pool_scale=m_pool_scale, mix_ab_w_out=m_mix_ab_w_out, lru_w_in=m_lru_w_in,
                 lru_conv_w=m_lru_conv_w, lru_conv_b=m_lru_conv_b, lru_wa=m_lru_wa, lru_ba=m_lru_ba, lru_wx=m_lru_wx,
                 lru_bx=m_lru_bx, lru_lambda=m_lru_lambda, lru_w_out=m_lru_w_out)
    mom_v = dict(c_ctx=v_c_ctx, w_mod=v_w_mod, b_mod=v_b_mod, ln_g=v_ln_g, ln_b=v_ln_b, ffn_w_gate=v_ffn_w_gate,
                 ffn_w_up=v_ffn_w_up, ffn_w_down=v_ffn_w_down, mix_ab_w_in=v_mix_ab_w_in, attn_sink=v_attn_sink,
                 pool_w=v_pool_w, pool_scale=v_pool_scale, mix_ab_w_out=v_mix_ab_w_out, lru_w_in=v_lru_w_in,
                 lru_conv_w=v_lru_conv_w, lru_conv_b=v_lru_conv_b, lru_wa=v_lru_wa, lru_ba=v_lru_ba, lru_wx=v_lru_wx,
                 lru_bx=v_lru_bx, lru_lambda=v_lru_lambda, lru_w_out=v_lru_w_out)
    names = list(weights)

    n_lat, n_ctx = x.shape[1], ctx.shape[1]
    cfg = _Cfg(n_lat, n_ctx)
    _, _, _, me = _position()
    mcols = w_mod.shape[2]

    def t_bf16(w):
        return jnp.swapaxes(w, -1, -2).astype(BF16)

    def ffn_src(l, i):
        return jnp.stack([t_bf16(ffn_w_gate[l, i]), t_bf16(ffn_w_up[l, i]), ffn_w_down[l, i].astype(BF16)])

    pending = {}

    def start_gathers(items, tok):
        for key, make_src in items:
            pending[key] = _gather_start(make_src() + tok.astype(BF16), "gather_start_" + key)
            tok = pending[key][4][0, 0]
        return tok

    def weights_now(key, after):
        return _gather_wait(pending[key], after, "gather_wait_" + key)

    first = _gather_start(ffn_src(0, 0), "gather_start_ffn00", SAME_CORE_AND_SIBLING)
    tok = first[4][0, 0]

    small_names = ["ln_g", "ln_b", "lru_conv_w", "lru_conv_b", "lru_ba", "lru_bx", "lru_lambda"]
    small, small_off = _pack_rows([(c + tok).reshape(-1, 128)] + [weights[n].reshape(-1, 128) for n in small_names])
    small_all = _all_gather(small, "gather_small", True)

    def small_full(idx, shp):
        rows = math.prod(shp) // 128
        return _blocks_to_cols(small_all[:, small_off[idx]:small_off[idx] + rows, :].reshape((N_DEV,) + shp))

    c_all = small_all[:, :2 * D // 128, :].reshape(2 * N_DEV, D)
    ln_g_f, ln_b_f = small_full(1, ln_g.shape), small_full(2, ln_b.shape)
    lru_consts = (small_full(3, lru_conv_w.shape)[0], small_full(4, lru_conv_b.shape), lru_wa[0],
                  small_full(5, lru_ba.shape)[0], lru_wx[0], small_full(6, lru_bx.shape)[0],
                  small_full(7, lru_lambda.shape)[0])

    s_rows = jnp.zeros((32, D), F32).at[:16].set(_silu(c_all)).at[16].set(_silu(c_ctx)).astype(BF16)
    mod_mine = jnp.stack([_matmul(s_rows, w_mod[l], "nn", F32, "mod_fwd", bn_cap=1280) for l in range(2)])
    mod_all = _all_gather(mod_mine.reshape(64, mcols), "gather_mod", True).reshape(N_DEV, 2, 32, mcols)
    r_ffn = ffn_w_down.shape[2]
    relay = _relay_start(_gather_wait(first, mod_all, "gather_wait_ffn00", n_peers=len(SAME_CORE_AND_SIBLING)),
                         r_ffn, "gather_relay_start_ffn00")
    tok = start_gathers([("ab_in", lambda: t_bf16(mix_ab_w_in)), ("ab_out", lambda: mix_ab_w_out.astype(BF16)),
                         ("ffn01", lambda: ffn_src(0, 1)), ("ffn10", lambda: ffn_src(1, 0)),
                         ("lru_in", lambda: t_bf16(lru_w_in)), ("lru_out", lambda: lru_w_out.astype(BF16)),
                         ("ffn11", lambda: ffn_src(1, 1))], relay[3][0, 0])
    mod_full = _blocks_to_cols(mod_all) + (b_mod[:, None, :] + tok)
    ex0 = 2 * me
    mods = []
    for l in range(2):
        rows = jnp.stack([lax.dynamic_index_in_dim(mod_full[l], ex0, 0, False),
                          lax.dynamic_index_in_dim(mod_full[l], ex0 + 1, 0, False), mod_full[l, 16]])
        mods.append(rows.reshape(3, N_MOD, D))

    h0 = jnp.concatenate([x.reshape(cfg.t_lat, D), ctx.reshape(cfg.t_ctx, D)], axis=0)
    cos, sin = _rope_tables(n_lat)
    sink_rows = jnp.broadcast_to(attn_sink[0][:, None], (8, 128)).astype(F32)

    saved = []
    wf = [[None, None], [None, None]]
    h = h0
    xin = _modulate(cfg, h0, mods[0], 0, 1, "modulate_in")
    for l in range(2):
        st = {"h_in": h, "xin1": xin}
        wf[l][0] = (_relay_wait(relay, r_ffn, xin, "gather_relay_wait_ffn00") if l == 0
                    else weights_now("ffn10", xin))
        g1, u1, y1 = _ffn_fwd(xin, wf[l][0], "ffn_fwd")
        h1, xhat1, rstd1, xin2 = _ln_fwd(cfg, h, y1, mods[l], 2, 0.5, ln_g_f[l, 0][None], ln_b_f[l, 0][None],
                                          mods[l], (3, 4), "ln_fwd_a")
        st.update(g1=g1, u1=u1, y1=y1, h1=h1, xhat1=xhat1, rstd1=rstd1, xin2=xin2)
        if l == 0:
            w_ab_in_t = weights_now("ab_in", xin2)[0]
            p = _matmul(xin2, w_ab_in_t, "nt", BF16, "mix_ab_in")
            att_l, att_c = _attn_fwd(cfg, p, cos, sin, sink_rows, "attn_fwd")
            pool_l = _pool_fwd(p, pool_w[0], pool_scale, n_lat, 0, 2, "pool_fwd_lat")
            pool_c = _pool_fwd(p, pool_w[0], pool_scale, n_ctx, cfg.ctx_blk, 2, "pool_fwd_ctx")
            cat = jnp.concatenate([jnp.concatenate([att_l, pool_l], axis=1),
                                   jnp.concatenate([att_c, pool_c], axis=1)], axis=0)
            w_ab_out = weights_now("ab_out", cat)[0]
            y2 = _matmul(cat, w_ab_out, "nn", BF16, "mix_ab_out")
        else:
            w_lru_in_t = weights_now("lru_in", xin2)[0]
            p = _matmul(xin2, w_lru_in_t, "nt", BF16, "lru_in")
            z_l, z_c, st["h_lat"], st["h_ctx"] = _lru_fwd(cfg, p, lru_consts, "lru_fwd")
            cat = jnp.concatenate([z_l, z_c], axis=0)
            w_lru_out = weights_now("lru_out", cat)[0]
            y2 = _matmul(cat, w_lru_out, "nn", BF16, "lru_out")
        h2, xhat2, rstd2, xin3 = _ln_fwd(cfg, h1, y2, mods[l], 5, 1.0, ln_g_f[l, 1][None], ln_b_f[l, 1][None],
                                          mods[l], (6, 7), "ln_fwd_b")
        wf[l][1] = weights_now("ffn%d1" % l, xin3)
        g3, u3, y3 = _ffn_fwd(xin3, wf[l][1], "ffn_fwd")
        if l == 0:
            h3, xhat3, rstd3, xin = _ln_fwd(cfg, h2, y3, mods[l], 8, 0.5, ln_g_f[l, 2][None], ln_b_f[l, 2][None],
                                            mods[1], (0, 1), "ln_fwd_a")
        else:
            h3, xhat3, rstd3 = _ln_fwd(cfg, h2, y3, mods[l], 8, 0.5, ln_g_f[l, 2][None], ln_b_f[l, 2][None],
                                       None, None, "ln_fwd_last")
        st.update(p=p, cat=cat, y2=y2, h2=h2, xhat2=xhat2, rstd2=rstd2, xin3=xin3, g3=g3, u3=u3, y3=y3,
                  xhat3=xhat3, rstd3=rstd3)
        saved.append(st)
        h = h3

    dy, loss_tile = _loss(cfg, h, loss_target.reshape(cfg.t_lat, D), "loss")
    loss = lax.psum(loss_tile[0, 0], ("x", "y", "c"))

    grads = {}
    dmod = [None, None]
    recv_ffn = [[None, None], [None, None]]
    dln_g = [[None] * 3, [None] * 3]
    dln_b = [[None] * 3, [None] * 3]

    def ffn_weight_grads(tag, xin_b, dg, du, a_act, dys, dep=None):
        handles = []
        for k, (lhs, rhs) in enumerate(((dg, xin_b), (du, xin_b), (a_act, dys))):
            part = _matmul(lhs, rhs, "tn", BF16, "ffn_dw", bm_cap=1408, bk_cap=2304, dep=dep)[None]
            handles.append(_exchange_start(part, "exchange_start_ffn%s_%d" % (tag, k)))
        return handles

    def pin(handles):
        total = handles[0][4][0, 0]
        for hd in handles[1:]:
            total = total + hd[4][0, 0]
        return total

    up = (dy,)
    dmod_next = None
    last_sent = None
    for l in (1, 0):
        st = saved[l]
        dm = [None] * N_MOD

        def put_stats(stats, gate_idx, nxt):
            dm[gate_idx] = stats[:, 2, :]
            if nxt is not None:
                nxt[0][nxt[1]] = stats[:, 4, :]
                nxt[0][nxt[1] + 1] = stats[:, 3, :]

        lng3 = ln_g_f[l, 2][None] if last_sent is None else ln_g_f[l, 2][None] + pin(last_sent)
        if len(up) > 1:
            up = (up[0], up[1], ln_b_f[l, 2][None], up[3], up[4])
        dres, dys, stats = _ln_bwd(cfg, up, st["xhat3"], st["rstd3"], st["y3"], mods[l], 8, 0.5,
                                   lng3, "ln_bwd_fused" if len(up) > 1 else "ln_bwd_last")
        put_stats(stats, 8, None if len(up) == 1 else (dmod_next, 0))
        dln_g[l][2], dln_b[l][2] = stats[:, 0, :].sum(0), stats[:, 1, :].sum(0)
        dg, du, a_act, dxin = _ffn_bwd(dys, st["g3"], st["u3"], wf[l][1], "ffn_bwd")
        recv_ffn[l][1] = ffn_weight_grads("%d1" % l, st["xin3"], dg, du, a_act, dys)
        dres, dys, stats = _ln_bwd(cfg, (dres, dxin, ln_b_f[l, 1][None], mods[l], 7), st["xhat2"], st["rstd2"], st["y2"],
                                   mods[l], 5, 1.0, ln_g_f[l, 1][None] + pin(recv_ffn[l][1]), "ln_bwd_fused")
        put_stats(stats, 5, (dm, 6))
        dln_g[l][1], dln_b[l][1] = stats[:, 0, :].sum(0), stats[:, 1, :].sum(0)
        if l == 0:
            dw_out = _matmul(st["cat"], dys, "tn", BF16, "mix_ab_dw_out")
            dcat = _matmul(dys, w_ab_out, "nt", BF16, "mix_ab_dcat")
            dq, dk, dv, dqc, dkc, dvc, dsink = _attn_bwd(cfg, st["p"], dcat, cos, sin, sink_rows, "attn_bwd")
            du_l, dpw_l, dps_l = _pool_bwd(st["p"], pool_w[0], pool_scale, dcat, n_lat, 0, 2, "pool_bwd_lat")
            du_c, dpw_c, dps_c = _pool_bwd(st["p"], pool_w[0], pool_scale, dcat, n_ctx, cfg.ctx_blk, 2, "pool_bwd_ctx")
            dp = jnp.concatenate([jnp.concatenate([dq, dk, dv, du_l], axis=1),
                                  jnp.concatenate([dqc, dkc, dvc, du_c], axis=1)], axis=0)
            dw_in_t = _matmul(dp, st["xin2"], "tn", BF16, "mix_ab_dw_in", bm_cap=1280)
            dxin = _matmul(dp, w_ab_in_t, "nn", BF16, "mix_ab_dx")
            recv_mix = [_exchange_start(part, "exchange_start_mix_ab_%d" % k)
                        for k, part in enumerate((dw_in_t[None], dw_out[None], _as2d(dpw_l + dpw_c)[None]))]
            grads["attn_sink"] = (dsink[0, :, 0] + dsink[1, :, 0])[None, :]
            grads["pool_scale"] = dps_l + dps_c
        else:
            dw_out = _matmul(st["cat"], dys, "tn", BF16, "lru_dw_out")
            dz = _matmul(dys, w_lru_out, "nt", BF16, "lru_dz")
            dgl, dul, dgc, duc, dwa, dwx, vec = _lru_bwd(cfg, st["p"], dz, st["h_lat"], st["h_ctx"], lru_consts, "lru_bwd")
            dp = jnp.concatenate([jnp.concatenate([dgl, dul], axis=1), jnp.concatenate([dgc, duc], axis=1)], axis=0)
            dw_in_t = _matmul(dp, st["xin2"], "tn", BF16, "lru_dw_in", bm_cap=1024)
            dxin = _matmul(dp, w_lru_in_t, "nn", BF16, "lru_dx")
            recv_mix = [_exchange_start(part, "exchange_start_lru_%d" % k)
                        for k, part in enumerate((dw_in_t[None], dw_out[None], _as2d(dwa)[None], _as2d(dwx)[None]))]
            vec_t = jnp.moveaxis(vec, 0, 1).reshape(16, D)
            grads["lru_ba"], grads["lru_bx"] = vec_t[0:2], vec_t[2:4]
            grads["lru_lambda"], grads["lru_conv_w"], grads["lru_conv_b"] = vec_t[4:6], vec_t[6:10], vec_t[10:11]
        if l == 0:
            recv_ab = recv_mix
        else:
            recv_lru = recv_mix
        dres, dys, stats = _ln_bwd(cfg, (dres, dxin, ln_b_f[l, 0][None], mods[l], 4), st["xhat1"], st["rstd1"], st["y1"],
                                   mods[l], 2, 0.5, ln_g_f[l, 0][None] + pin(recv_mix), "ln_bwd_fused")
        put_stats(stats, 2, (dm, 3))
        dln_g[l][0], dln_b[l][0] = stats[:, 0, :].sum(0), stats[:, 1, :].sum(0)
        dg, du, a_act, dxin = _ffn_bwd(dys, st["g1"], st["u1"], wf[l][0], "ffn_bwd")
        if l == 1:
            recv_ffn[1][0] = ffn_weight_grads("10", st["xin1"], dg, du, a_act, dys)
            last_sent = recv_ffn[1][0]
        dmod[l] = dm
        dmod_next = dm
        up = (dres, dxin, None, mods[l], 1)
    dh0, stats = _modulate_bwd(cfg, up[0], up[1], h0, mods[0], 1, "modulate_bwd")
    dmod[0][0], dmod[0][1] = stats[:, 4, :], stats[:, 3, :]
    grad_x = dh0.reshape(x.shape)

    dmod_mine = jnp.stack([jnp.stack(dmod[l], axis=1).reshape(3, N_MOD * D) for l in range(2)])
    n_dm = 6 * N_MOD * D // 128
    dmod_sent = _gather_start(dmod_mine.reshape(1, n_dm, 128), "gather_start_dmod")
    last_ffn = ffn_weight_grads("00", saved[0]["xin1"], dg, du, a_act, dys, dep=dmod_sent[4])

    def arrived(handle, after, name):
        return _exchange_wait(handle, after, name)

    after_start = last_ffn[2][4]
    for l, i in ((1, 1), (1, 0), (0, 1)):
        recv_ffn[l][i] = [arrived(hd, after_start, "exchange_wait_ffn%d%d_%d" % (l, i, k))
                          for k, hd in enumerate(recv_ffn[l][i])]
    recv_ab = [arrived(hd, after_start, "exchange_wait_mix_ab_%d" % k) for k, hd in enumerate(recv_ab)]
    recv_lru = [arrived(hd, after_start, "exchange_wait_lru_%d" % k) for k, hd in enumerate(recv_lru)]

    def shard_sum(recv, name):
        return _sum_blocks(recv.reshape(N_DEV, recv.shape[2], recv.shape[3]), name)

    gate_g = [[None, None], [None, None]]
    up_g = [[None, None], [None, None]]
    down_g = [[None, None], [None, None]]
    def ffn_sums(l, i):
        gt, ut, dn = [shard_sum(r, "sum_ffn") for r in recv_ffn[l][i]]
        gate_g[l][i], up_g[l][i], down_g[l][i] = gt.T, ut.T, dn

    for l, i in ((1, 1), (1, 0), (0, 1)):
        ffn_sums(l, i)
    grads["mix_ab_w_in"] = shard_sum(recv_ab[0], "sum_mix_in").T[None]
    grads["mix_ab_w_out"] = shard_sum(recv_ab[1], "sum_mix_out")[None]
    grads["lru_w_in"] = shard_sum(recv_lru[0], "sum_lru_in").T[None]
    grads["lru_w_out"] = shard_sum(recv_lru[1], "sum_lru_out")[None]
    rep_parts = [shard_sum(recv_lru[2], "sum_rep"), shard_sum(recv_lru[3], "sum_rep"), shard_sum(recv_ab[2], "sum_rep")]
    rep_names = ["lru_wa", "lru_wx", "pool_w"]

    dmod_all = _gather_wait(dmod_sent, rep_parts[2], "gather_wait_dmod").reshape(N_DEV, n_dm, 128)
    dmod_sum = _sum_blocks(dmod_all, "sum_dmod").reshape(2, 3, N_MOD * D)
    dmod_all = dmod_all.reshape(N_DEV, 2, 3, N_MOD * D)
    grads["b_mod"] = dmod_sum[:, 0] + dmod_sum[:, 1] + dmod_sum[:, 2]
    dmod_ex = jnp.moveaxis(dmod_all[:, :, 0:2, :], 1, 0).reshape(2, 2 * N_DEV, N_MOD * D)
    dm_rows = jnp.zeros((2, 32, N_MOD * D), F32).at[:, :16].set(dmod_ex).at[:, 16].set(dmod_sum[:, 2])
    dm_cols = lax.dynamic_slice_in_dim(dm_rows, me * mcols, mcols, axis=2).astype(BF16)
    grads["w_mod"] = jnp.stack([_matmul(s_rows, dm_cols[l], "tn", F32, "mod_dw", bn_cap=1280) for l in range(2)])
    ds_part = None
    for l in range(2):
        part = _matmul(dm_cols[l, 16:32], w_mod[l], "nt", F32, "mod_ds", bk_cap=1280)[0]
        ds_part = part if ds_part is None else ds_part + part

    dln_g_f = jnp.stack([jnp.stack(dln_g[l]) for l in range(2)])
    dln_b_f = jnp.stack([jnp.stack(dln_b[l]) for l in range(2)])
    sink_pad = jnp.zeros((1, 128), F32).at[0, :8].set(grads["attn_sink"][0])
    part_list = [p_.reshape(-1, 128) for p_ in rep_parts] + [
        dln_g_f.reshape(-1, 128), dln_b_f.reshape(-1, 128), grads["lru_conv_w"].reshape(-1, 128),
        grads["lru_conv_b"].reshape(-1, 128), grads["lru_ba"].reshape(-1, 128), grads["lru_bx"].reshape(-1, 128),
        grads["lru_lambda"].reshape(-1, 128), ds_part.reshape(-1, 128), sink_pad, grads["pool_scale"].reshape(-1, 128)]
    parts, part_off = _pack_rows(part_list)
    parts_sent = _gather_start(parts[None], "gather_start_partials")

    delta, new_m, new_v = {}, {}, {}
    for n in ("w_mod", "b_mod", "mix_ab_w_in", "mix_ab_w_out", "lru_w_in", "lru_w_out"):
        grads[n] = grads[n].reshape(weights[n].shape)
        delta[n], new_m[n], new_v[n] = _adamw(weights[n], grads[n], mom_m[n], mom_v[n], "adamw", dep=parts_sent[4])
    parts_all = _gather_wait(parts_sent, delta["lru_w_out"], "gather_wait_partials").reshape(N_DEV, parts.shape[0], 128)
    parts_sum = _sum_blocks(parts_all, "sum_partials")

    for i, n in enumerate(rep_names):
        rows = part_list[i].shape[0]
        grads[n] = parts_all[:, part_off[i]:part_off[i] + rows, :].reshape(weights[n].shape)

    def take(idx):
        return parts_sum[part_off[idx]:part_off[idx] + part_list[idx].shape[0]]

    def my_cols(full, shp):
        w = shp[-1]
        return lax.dynamic_slice_in_dim(full, me * w, w, axis=full.ndim - 1)

    grads["ln_g"] = my_cols(take(3).reshape(2, 3, D), ln_g.shape)
    grads["ln_b"] = my_cols(take(4).reshape(2, 3, D), ln_b.shape)
    grads["lru_conv_w"] = my_cols(take(5).reshape(1, 4, D), lru_conv_w.shape)
    grads["lru_conv_b"] = my_cols(take(6).reshape(1, D), lru_conv_b.shape)
    grads["lru_ba"] = my_cols(take(7).reshape(1, 2, D), lru_ba.shape)
    grads["lru_bx"] = my_cols(take(8).reshape(1, 2, D), lru_bx.shape)
    grads["lru_lambda"] = my_cols(take(9).reshape(1, 2, D), lru_lambda.shape)
    sg = jax.nn.sigmoid(c_ctx)
    grads["c_ctx"] = take(10).reshape(D) * (sg * (1.0 + c_ctx * (1.0 - sg)))
    grads["attn_sink"] = take(11)[:, :8]
    grads["pool_scale"] = take(12).reshape(pool_scale.shape)

    ffn_names = ("ffn_w_gate", "ffn_w_up", "ffn_w_down")
    last_delta = delta["lru_w_out"]
    for n in names:
        if n in delta or n in ffn_names:
            continue
        grads[n] = grads[n].reshape(weights[n].shape)
        delta[n], new_m[n], new_v[n] = _adamw(weights[n], grads[n], mom_m[n], mom_v[n], "adamw")
        last_delta = delta[n]
    recv_ffn[0][0] = [arrived(hd, last_delta, "exchange_wait_ffn00_%d" % k) for k, hd in enumerate(last_ffn)]
    ffn_sums(0, 0)
    grads["ffn_w_gate"] = jnp.stack([jnp.stack(gate_g[l]) for l in range(2)])
    grads["ffn_w_up"] = jnp.stack([jnp.stack(up_g[l]) for l in range(2)])
    grads["ffn_w_down"] = jnp.stack([jnp.stack(down_g[l]) for l in range(2)])
    for n in ffn_names:
        delta[n], new_m[n], new_v[n] = _adamw(weights[n], grads[n], mom_m[n], mom_v[n], "adamw")

    return (loss, grad_x, *[grads[n] for n in names], *[delta[n] for n in names],
            *[new_m[n] for n in names], *[new_v[n] for n in names])
```

```python
import functools
import math

import jax
import jax.numpy as jnp
from jax import lax
from jax.experimental import pallas as pl
from jax.experimental.pallas import tpu as pltpu

F32 = jnp.float32
BF16 = jnp.bfloat16
MESH = pl.DeviceIdType.MESH

D = 1024
N_MOD = 9
N_DEV = 8
HEAD_DIM = 64
ATT_HEADS = 8
KV_HEADS = 2
ATT_W = 512
BLK = 128
ATT_SCALE = HEAD_DIM ** -0.5
GRID_W = 64
ROPE_FREQS = HEAD_DIM // 4
ROPE_THETA = 10000.0
POOL_R = (1, 2, 4, 8)
LRU_C = 8.0
LN_EPS = 1e-5
NEG_INF = -1e30
ALPHA = 4.0 ** 0.25
LR, B1, B2, EPS, WD, STEP = 0.001, 0.9, 0.999, 1e-08, 0.01, 10
VMEM_LIMIT = 56 * 1024 * 1024
ROW_TILE = 512


def _params(sem=None):
    if sem is None:
        return pltpu.CompilerParams(vmem_limit_bytes=VMEM_LIMIT)
    return pltpu.CompilerParams(dimension_semantics=sem, vmem_limit_bytes=VMEM_LIMIT)


def _sigmoid(x):
    return 0.5 * jnp.tanh(0.5 * x) + 0.5


def _dot(a, b):
    return jnp.dot(a.astype(BF16), b.astype(BF16), preferred_element_type=F32)


def _dot_nt(a, b):
    return lax.dot_general(a.astype(BF16), b.astype(BF16), (((1,), (1,)), ((), ())), preferred_element_type=F32)


def _dot_tn(a, b):
    return lax.dot_general(a.astype(BF16), b.astype(BF16), (((0,), (0,)), ((), ())), preferred_element_type=F32)


def _pick(n, cap):
    best = None
    for m in range(128, min(n, cap) + 1, 128):
        if n % m == 0:
            best = m
    return n if best is None else best


def _chunks(width, step=256):
    out, c = [], 0
    while c < width:
        w = min(step, width - c)
        out.append((c, w))
        c += w
    return out


class _Cfg:
    def __init__(self, n_lat, n_ctx):
        self.n_lat, self.n_ctx = n_lat, n_ctx
        self.t_lat, self.t_ctx = 2 * n_lat, 2 * n_ctx
        self.T = self.t_lat + self.t_ctx
        self.tm = min(ROW_TILE, self.t_ctx)
        assert n_lat % self.tm == 0 and self.t_ctx % self.tm == 0 and n_lat >= 3 * BLK and n_ctx % BLK == 0
        self.nt = self.T // self.tm
        self.nlt = n_lat // self.tm
        self.ctx_blk = self.t_lat // n_ctx

    def seg(self, i):
        return jnp.minimum(i // self.nlt, 2)

    def first_of_seg(self, i):
        return jnp.where(i < 2 * self.nlt, i % self.nlt == 0, i == 2 * self.nlt)


def _modulate(cfg, h, mod, shift_idx, scale_idx, name):
    tm = cfg.tm

    def body(h_ref, mod_ref, o_ref):
        sh = mod_ref[shift_idx:shift_idx + 1, :]
        sc = mod_ref[scale_idx:scale_idx + 1, :]
        o_ref[...] = (h_ref[...] * (1.0 + sc) + sh).astype(BF16)

    return pl.pallas_call(
        body, grid=(cfg.nt,), name=name,
        in_specs=[pl.BlockSpec((tm, D), lambda i: (i, 0)),
                  pl.BlockSpec((None, N_MOD, D), lambda i: (cfg.seg(i), 0, 0))],
        out_specs=pl.BlockSpec((tm, D), lambda i: (i, 0)),
        out_shape=jax.ShapeDtypeStruct((cfg.T, D), BF16),
        compiler_params=_params(("parallel",)),
    )(h, mod)


def _ln_fwd(cfg, h, y, mod, gate_idx, coef, lng, lnb, mod_next, next_idx, name):
    tm = cfg.tm
    has_next = next_idx is not None

    def body(*refs):
        if has_next:
            h_ref, y_ref, mod_ref, g_ref, b_ref, modn_ref, hn_ref, xhat_ref, rstd_ref, xin_ref = refs
        else:
            h_ref, y_ref, mod_ref, g_ref, b_ref, hn_ref, xhat_ref, rstd_ref = refs
        gate = mod_ref[gate_idx:gate_idx + 1, :]
        z = ALPHA * h_ref[...] + (coef * gate) * y_ref[...].astype(F32)
        mu = jnp.mean(z, axis=-1, keepdims=True)
        zc = z - mu
        var = jnp.mean(zc * zc, axis=-1, keepdims=True)
        rstd = lax.rsqrt(var + LN_EPS)
        xhat = zc * rstd
        hn = xhat * g_ref[...] + b_ref[...]
        hn_ref[...] = hn
        xhat_ref[...] = xhat.astype(BF16)
        rstd_ref[...] = rstd
        if has_next:
            sh = modn_ref[next_idx[0]:next_idx[0] + 1, :]
            sc = modn_ref[next_idx[1]:next_idx[1] + 1, :]
            xin_ref[...] = (hn * (1.0 + sc) + sh).astype(BF16)

    row = pl.BlockSpec((tm, D), lambda i: (i, 0))
    modspec = pl.BlockSpec((None, N_MOD, D), lambda i: (cfg.seg(i), 0, 0))
    vec = pl.BlockSpec((1, D), lambda i: (0, 0))
    in_specs = [row, row, modspec, vec, vec]
    args = [h, y, mod, lng, lnb]
    out_specs = [row, row, pl.BlockSpec((tm, 1), lambda i: (i, 0))]
    out_shape = [jax.ShapeDtypeStruct((cfg.T, D), F32), jax.ShapeDtypeStruct((cfg.T, D), BF16),
                 jax.ShapeDtypeStruct((cfg.T, 1), F32)]
    if has_next:
        in_specs.append(modspec)
        args.append(mod_next)
        out_specs.append(row)
        out_shape.append(jax.ShapeDtypeStruct((cfg.T, D), BF16))
    return pl.pallas_call(body, grid=(cfg.nt,), name=name, in_specs=in_specs, out_specs=out_specs,
                          out_shape=out_shape, compiler_params=_params(("parallel",)))(*args)


def _ln_bwd(cfg, up, xhat, rstd, y, mod, gate_idx, coef, lng, name):
    tm = cfg.tm
    fused = len(up) > 1
    scale_next = up[4] if fused else None

    def body(*refs):
        if fused:
            dres_n, dxin_n, b_ref, modn_ref, xhat_ref, rstd_ref, y_ref, mod_ref, g_ref, dres_ref, dys_ref, st_ref = refs
        else:
            dhn_ref, xhat_ref, rstd_ref, y_ref, mod_ref, g_ref, dres_ref, dys_ref, st_ref = refs
        i = pl.program_id(0)

        @pl.when(cfg.first_of_seg(i))
        def _():
            st_ref[...] = jnp.zeros_like(st_ref)

        xhat = xhat_ref[...].astype(F32)
        if fused:
            dxin = dxin_n[...].astype(F32)
            sc = modn_ref[scale_next:scale_next + 1, :]
            dhn = dres_n[...] + dxin * (1.0 + sc)
            shift_sum = jnp.sum(dxin, axis=0, keepdims=True)
            st_ref[3:4, :] += g_ref[...] * jnp.sum(dxin * xhat, axis=0, keepdims=True) + b_ref[...] * shift_sum
            st_ref[4:5, :] += shift_sum
        else:
            dhn = dhn_ref[...]
        gdh = dhn * g_ref[...]
        m1 = jnp.mean(gdh, axis=-1, keepdims=True)
        m2 = jnp.mean(gdh * xhat, axis=-1, keepdims=True)
        dz = rstd_ref[...] * (gdh - m1 - xhat * m2)
        gate = mod_ref[gate_idx:gate_idx + 1, :]
        dres_ref[...] = ALPHA * dz
        dys_ref[...] = ((coef * gate) * dz).astype(BF16)
        st_ref[0:1, :] += jnp.sum(dhn * xhat, axis=0, keepdims=True)
        st_ref[1:2, :] += jnp.sum(dhn, axis=0, keepdims=True)
        st_ref[2:3, :] += jnp.sum((coef * dz) * y_ref[...].astype(F32), axis=0, keepdims=True)

    row = pl.BlockSpec((tm, D), lambda i: (i, 0))
    modspec = pl.BlockSpec((None, N_MOD, D), lambda i: (cfg.seg(i), 0, 0))
    vec = pl.BlockSpec((1, D), lambda i: (0, 0))
    col = pl.BlockSpec((tm, 1), lambda i: (i, 0))
    if fused:
        in_specs = [row, row, vec, modspec, row, col, row, modspec, vec]
        args = [up[0], up[1], up[2], up[3], xhat, rstd, y, mod, lng]
    else:
        in_specs = [row, row, col, row, modspec, vec]
        args = [up[0], xhat, rstd, y, mod, lng]
    return pl.pallas_call(
        body, grid=(cfg.nt,), name=name, in_specs=in_specs,
        out_specs=[row, row, pl.BlockSpec((None, 8, D), lambda i: (cfg.seg(i), 0, 0))],
        out_shape=[jax.ShapeDtypeStruct((cfg.T, D), F32), jax.ShapeDtypeStruct((cfg.T, D), BF16),
                   jax.ShapeDtypeStruct((3, 8, D), F32)],
        compiler_params=_params(("arbitrary",)))(*args)


def _modulate_bwd(cfg, dres, dxin, h, mod, scale_idx, name):
    tm = cfg.tm
    n_lt = 2 * cfg.nlt

    def body(dres_ref, dxin_ref, h_ref, mod_ref, dh_ref, st_ref):
        i = pl.program_id(0)

        @pl.when(cfg.first_of_seg(i))
        def _():
            st_ref[...] = jnp.zeros_like(st_ref)

        dxin = dxin_ref[...].astype(F32)
        sc = mod_ref[scale_idx:scale_idx + 1, :]

        @pl.when(i < n_lt)
        def _():
            dh_ref[...] = dres_ref[...] + dxin * (1.0 + sc)

        st_ref[3:4, :] += jnp.sum(dxin * h_ref[...], axis=0, keepdims=True)
        st_ref[4:5, :] += jnp.sum(dxin, axis=0, keepdims=True)

    row = pl.BlockSpec((tm, D), lambda i: (i, 0))
    return pl.pallas_call(
        body, grid=(cfg.nt,), name=name,
        in_specs=[row, row, row, pl.BlockSpec((None, N_MOD, D), lambda i: (cfg.seg(i), 0, 0))],
        out_specs=[pl.BlockSpec((tm, D), lambda i: (jnp.minimum(i, n_lt - 1), 0)),
                   pl.BlockSpec((None, 8, D), lambda i: (cfg.seg(i), 0, 0))],
        out_shape=[jax.ShapeDtypeStruct((cfg.t_lat, D), F32), jax.ShapeDtypeStruct((3, 8, D), F32)],
        compiler_params=_params(("arbitrary",)))(dres, dxin, h, mod)


def _loss(cfg, h, target, name):
    tm = cfg.tm
    n_lt = 2 * cfg.nlt

    def body(h_ref, t_ref, dy_ref, l_ref):
        i = pl.program_id(0)

        @pl.when(i == 0)
        def _():
            l_ref[...] = jnp.zeros_like(l_ref)

        @pl.when(i < n_lt)
        def _():
            err = h_ref[...] - t_ref[...]
            dy_ref[...] = err * (1.0 / D)
            part = jnp.sum(jnp.sum(err * err, axis=1, keepdims=True), axis=0, keepdims=True) * (0.5 / D)
            l_ref[...] += jnp.broadcast_to(part, l_ref.shape)

        @pl.when(i >= n_lt)
        def _():
            dy_ref[...] = jnp.zeros_like(dy_ref)

    return pl.pallas_call(
        body, grid=(cfg.nt,), name=name,
        in_specs=[pl.BlockSpec((tm, D), lambda i: (i, 0)),
                  pl.BlockSpec((tm, D), lambda i: (jnp.minimum(i, n_lt - 1), 0))],
        out_specs=[pl.BlockSpec((tm, D), lambda i: (i, 0)), pl.BlockSpec((8, 128), lambda i: (0, 0))],
        out_shape=[jax.ShapeDtypeStruct((cfg.T, D), F32), jax.ShapeDtypeStruct((8, 128), F32)],
        compiler_params=_params(("arbitrary",)))(h, target)


def _matmul(a, b, mode, out_dtype, name, bm_cap=1536, bn_cap=1408, bk_cap=1024, dep=None):
    if mode == "nn":
        (M, K), N = a.shape, b.shape[1]
    elif mode == "nt":
        (M, K), N = a.shape, b.shape[0]
    else:
        (K, M), N = a.shape, b.shape[1]
    bm, bn, bk = _pick(M, bm_cap), _pick(N, bn_cap), _pick(K, bk_cap)
    nk = K // bk

    def body(a_ref, b_ref, *rest):
        o_ref = rest[0] if dep is None else rest[1]
        acc_ref = rest[-1]
        k = pl.program_id(2)
        if mode == "nn":
            part = _dot(a_ref[...], b_ref[...])
        elif mode == "nt":
            part = _dot_nt(a_ref[...], b_ref[...])
        else:
            part = _dot_tn(a_ref[...], b_ref[...])
        if nk == 1:
            o_ref[...] = part.astype(out_dtype)
            return

        @pl.when(k == 0)
        def _():
            acc_ref[...] = part

        @pl.when((k > 0) & (k < nk - 1))
        def _():
            acc_ref[...] += part

        @pl.when(k == nk - 1)
        def _():
            o_ref[...] = (acc_ref[...] + part).astype(out_dtype)

    if mode == "nn":
        a_spec = pl.BlockSpec((bm, bk), lambda i, j, k: (i, k))
        b_spec = pl.BlockSpec((bk, bn), lambda i, j, k: (k, j))
    elif mode == "nt":
        a_spec = pl.BlockSpec((bm, bk), lambda i, j, k: (i, k))
        b_spec = pl.BlockSpec((bn, bk), lambda i, j, k: (j, k))
    else:
        a_spec = pl.BlockSpec((bk, bm), lambda i, j, k: (k, i))
        b_spec = pl.BlockSpec((bk, bn), lambda i, j, k: (k, j))
    return pl.pallas_call(
        body, grid=(M // bm, N // bn, nk), name=name,
        in_specs=[a_spec, b_spec] + ([] if dep is None else [pl.BlockSpec(memory_space=pl.ANY)]),
        out_specs=pl.BlockSpec((bm, bn), lambda i, j, k: (i, j)),
        out_shape=jax.ShapeDtypeStruct((M, N), out_dtype),
        scratch_shapes=[pltpu.VMEM((bm, bn), F32)] if nk > 1 else [],
        compiler_params=_params(("parallel", "parallel", "arbitrary")))(a, b, *([] if dep is None else [dep]))


def _ffn_tile(T, cap):
    best = 256
    for t in range(256, cap + 1, 256):
        if T % t == 0:
            best = t
    return best


def _ffn_fwd(xin, wf, name):
    T = xin.shape[0]
    F = wf.shape[1]
    tm, tf = _ffn_tile(T, 768), F // 2
    assert tf % 128 == 0 and T % tm == 0

    def body(x_ref, wg_ref, wu_ref, wd_ref, g_ref, u_ref, y_ref, acc_ref):
        j = pl.program_id(1)
        x = x_ref[...]
        acc = None
        for c0, cw in _chunks(tf):
            g = _dot_nt(x, wg_ref[c0:c0 + cw, :])
            u = _dot_nt(x, wu_ref[c0:c0 + cw, :])
            g_ref[:, c0:c0 + cw] = g.astype(BF16)
            u_ref[:, c0:c0 + cw] = u.astype(BF16)
            part = _dot(g * _sigmoid(g) * u, wd_ref[c0:c0 + cw, :])
            acc = part if acc is None else acc + part

        @pl.when(j == 0)
        def _():
            acc_ref[...] = acc

        @pl.when(j == 1)
        def _():
            y_ref[...] = (acc_ref[...] + acc).astype(BF16)

    return pl.pallas_call(
        body, grid=(T // tm, 2), name=name,
        in_specs=[pl.BlockSpec((tm, D), lambda i, j: (i, 0)),
                  pl.BlockSpec((None, tf, D), lambda i, j: (0, j, 0)),
                  pl.BlockSpec((None, tf, D), lambda i, j: (1, j, 0)),
                  pl.BlockSpec((None, tf, D), lambda i, j: (2, j, 0))],
        out_specs=[pl.BlockSpec((tm, tf), lambda i, j: (i, j)),
                   pl.BlockSpec((tm, tf), lambda i, j: (i, j)),
                   pl.BlockSpec((tm, D), lambda i, j: (i, 0))],
        out_shape=[jax.ShapeDtypeStruct((T, F), BF16), jax.ShapeDtypeStruct((T, F), BF16),
                   jax.ShapeDtypeStruct((T, D), BF16)],
        scratch_shapes=[pltpu.VMEM((tm, D), F32)],
        compiler_params=_params(("parallel", "arbitrary")))(xin, wf, wf, wf)


def _ffn_bwd(dys, g, u, wf, name):
    T = dys.shape[0]
    F = wf.shape[1]
    tm, tf = _ffn_tile(T, 512), F // 2

    def body(dy_ref, g_ref, u_ref, wg_ref, wu_ref, wd_ref, dg_ref, du_ref, a_ref, dx_ref, acc_ref):
        j = pl.program_id(1)
        da_all = _dot_nt(dy_ref[...], wd_ref[...])
        for c0, cw in _chunks(tf):
            gg = g_ref[:, c0:c0 + cw].astype(F32)
            uu = u_ref[:, c0:c0 + cw].astype(F32)
            da = da_all[:, c0:c0 + cw]
            s = _sigmoid(gg)
            silu = gg * s
            a_ref[:, c0:c0 + cw] = (silu * uu).astype(BF16)
            du_ref[:, c0:c0 + cw] = (da * silu).astype(BF16)
            dg_ref[:, c0:c0 + cw] = (da * uu * (s * (1.0 + gg * (1.0 - s)))).astype(BF16)
        acc = _dot(dg_ref[...], wg_ref[...]) + _dot(du_ref[...], wu_ref[...])

        @pl.when(j == 0)
        def _():
            acc_ref[...] = acc

        @pl.when(j == 1)
        def _():
            dx_ref[...] = (acc_ref[...] + acc).astype(BF16)

    blk = pl.BlockSpec((tm, tf), lambda i, j: (i, j))
    return pl.pallas_call(
        body, grid=(T // tm, 2), name=name,
        in_specs=[pl.BlockSpec((tm, D), lambda i, j: (i, 0)), blk, blk,
                  pl.BlockSpec((None, tf, D), lambda i, j: (0, j, 0)),
                  pl.BlockSpec((None, tf, D), lambda i, j: (1, j, 0)),
                  pl.BlockSpec((None, tf, D), lambda i, j: (2, j, 0))],
        out_specs=[blk, blk, blk, pl.BlockSpec((tm, D), lambda i, j: (i, 0))],
        out_shape=[jax.ShapeDtypeStruct((T, F), BF16), jax.ShapeDtypeStruct((T, F), BF16),
                   jax.ShapeDtypeStruct((T, F), BF16), jax.ShapeDtypeStruct((T, D), BF16)],
        scratch_shapes=[pltpu.VMEM((tm, D), F32)],
        compiler_params=_params(("parallel", "arbitrary")))(dys, g, u, wf, wf, wf)


def _swap_halves(x):
    w = x.shape[1]
    lane = lax.broadcasted_iota(jnp.int32, (1, w), 1)
    return jnp.where((lane & 63) < 32, pltpu.roll(x, w - 32, 1), pltpu.roll(x, 32, 1))


def _rope(x, cos, sin):
    return x * cos + _swap_halves(x) * sin


def _rope_t(dy, cos, sin):
    return dy * cos + _swap_halves(dy * sin)


def _rope_tables(n_lat):
    rows = n_lat // GRID_W
    row = jnp.repeat(jnp.arange(rows, dtype=F32), GRID_W)
    col = jnp.tile(jnp.arange(GRID_W, dtype=F32), rows)
    inv = ROPE_THETA ** (-jnp.arange(ROPE_FREQS, dtype=F32) / ROPE_FREQS)
    ang = jnp.concatenate([row[:, None] * inv, col[:, None] * inv], axis=-1)
    cs, sn = jnp.cos(ang), jnp.sin(ang)
    cos = jnp.concatenate([cs, cs, cs, cs], axis=-1)
    sin = jnp.concatenate([-sn, sn, -sn, sn], axis=-1)
    return cos, sin


def _attn_specs(cfg):
    n_lat, n_ctx, cb = cfg.n_lat, cfg.n_ctx, cfg.ctx_blk
    return [pl.BlockSpec((n_lat, ATT_W), lambda e: (e, 0)),
            pl.BlockSpec((n_lat, 128), lambda e: (e, 4)),
            pl.BlockSpec((n_lat, 128), lambda e: (e, 5)),
            pl.BlockSpec((n_ctx, ATT_W), lambda e: (cb + e, 0)),
            pl.BlockSpec((n_ctx, 128), lambda e: (cb + e, 4)),
            pl.BlockSpec((n_ctx, 128), lambda e: (cb + e, 5)),
            pl.BlockSpec((n_lat, 128), lambda e: (0, 0)),
            pl.BlockSpec((n_lat, 128), lambda e: (0, 0)),
            pl.BlockSpec((8, 128), lambda e: (0, 0))]


def _attn_prepare(kh, kl, vl, kc, vc, ka, kb, va, vb, kca, kcb, vca, vcb):
    lane = lax.broadcasted_iota(jnp.int32, (1, 128), 1)
    own = (lane < 64) if kh == 0 else (lane >= 64)

    def split(x, ra, rb):
        mine = jnp.where(own, x, 0.0)
        other = pltpu.roll(mine, 64, 1)
        a, b = (mine, other) if kh == 0 else (other, mine)
        ra[...] = a.astype(BF16)
        rb[...] = b.astype(BF16)

    split(kl, ka, kb)
    split(vl, va, vb)
    split(kc, kca, kcb)
    split(vc, vca, vcb)


def _softmax_parts(s_list, sk):
    m = sk
    for s in s_list:
        m = jnp.maximum(m, jnp.max(s, axis=1, keepdims=True))
    es = [jnp.exp(s - m) for s in s_list]
    esk = jnp.exp(sk - m)
    den = esk
    for e in es:
        den = den + jnp.sum(e, axis=1, keepdims=True)
    inv = 1.0 / den
    return [e * inv for e in es], esk * inv


def _window(cfg, n):
    r0 = pl.multiple_of(n * BLK, BLK)
    start = pl.multiple_of(jnp.clip((n - 1) * BLK, 0, cfg.n_lat - 3 * BLK), BLK)
    qpos = r0 + lax.broadcasted_iota(jnp.int32, (BLK, 1), 0)
    kpos = start + lax.broadcasted_iota(jnp.int32, (1, 3 * BLK), 1)
    valid = jnp.abs(qpos - kpos) <= BLK
    return r0, start, valid


def _attn_fwd(cfg, p, cos, sin, sink_rows, name):
    n_lat, n_ctx = cfg.n_lat, cfg.n_ctx

    def body(q_ref, k_ref, v_ref, qc_ref, kc_ref, vc_ref, cos_ref, sin_ref, sink_ref, o_ref, oc_ref,
             qr, ka, kb, va, vb, kca, kcb, vca, vcb):
        cos_t, sin_t = cos_ref[...], sin_ref[...]
        for gq in range(4):
            qr[:, gq * 128:(gq + 1) * 128] = _rope(q_ref[:, gq * 128:(gq + 1) * 128].astype(F32), cos_t, sin_t).astype(BF16)
        kl = _rope(k_ref[...].astype(F32), cos_t, sin_t)
        for kh in range(KV_HEADS):
            _attn_prepare(kh, kl, v_ref[...].astype(F32), kc_ref[...].astype(F32), vc_ref[...].astype(F32),
                          ka, kb, va, vb, kca, kcb, vca, vcb)

            def lat_block(n, carry):
                r0, start, valid = _window(cfg, n)
                win = pl.ds(start, 3 * BLK)
                lanes = [slice((kh * 2 + pr) * 128, (kh * 2 + pr + 1) * 128) for pr in range(2)]
                qps = [qr[pl.ds(r0, BLK), lanes[pr]] for pr in range(2)]
                kws, kcs = (ka[win, :], kb[win, :]), (kca[...], kcb[...])
                scores = [(jnp.where(valid, _dot_nt(qps[pr], kws[half]) * ATT_SCALE, NEG_INF),
                           _dot_nt(qps[pr], kcs[half]) * ATT_SCALE) for pr in range(2) for half in range(2)]
                probs = []
                for idx, (s_w, s_c) in enumerate(scores):
                    head = kh * 4 + idx
                    (p_w, p_c), _ = _softmax_parts([s_w, s_c], sink_ref[head:head + 1, 0:1])
                    probs.append((p_w.astype(BF16), p_c.astype(BF16)))
                vws, vcs = (va[win, :], vb[win, :]), (vca[...], vcb[...])
                for pr in range(2):
                    o = (_dot(probs[2 * pr][0], vws[0]) + _dot(probs[2 * pr][1], vcs[0])
                         + _dot(probs[2 * pr + 1][0], vws[1]) + _dot(probs[2 * pr + 1][1], vcs[1]))
                    o_ref[pl.ds(r0, BLK), lanes[pr]] = o.astype(BF16)
                return carry

            lax.fori_loop(0, n_lat // BLK, lat_block, 0, unroll=2)
            for n in range(n_ctx // BLK):
                rows = slice(n * BLK, (n + 1) * BLK)
                for pr in range(2):
                    lanes = slice((kh * 2 + pr) * 128, (kh * 2 + pr + 1) * 128)
                    qp = qc_ref[rows, lanes]
                    o = None
                    for half, (kcx, vcx) in enumerate(((kca, vca), (kcb, vcb))):
                        head = kh * 4 + pr * 2 + half
                        s_c = _dot_nt(qp, kcx[...]) * ATT_SCALE
                        (p_c,), _ = _softmax_parts([s_c], sink_ref[head:head + 1, 0:1])
                        part = _dot(p_c, vcx[...])
                        o = part if o is None else o + part
                    oc_ref[rows, lanes] = o.astype(BF16)

    return pl.pallas_call(
        body, grid=(2,), name=name, in_specs=_attn_specs(cfg),
        out_specs=[pl.BlockSpec((n_lat, ATT_W), lambda e: (e, 0)), pl.BlockSpec((n_ctx, ATT_W), lambda e: (e, 0))],
        out_shape=[jax.ShapeDtypeStruct((cfg.t_lat, ATT_W), BF16), jax.ShapeDtypeStruct((cfg.t_ctx, ATT_W), BF16)],
        scratch_shapes=[pltpu.VMEM((n_lat, ATT_W), BF16)] + [pltpu.VMEM((n_lat, 128), BF16)] * 4
        + [pltpu.VMEM((n_ctx, 128), BF16)] * 4,
        compiler_params=_params(("parallel",)))(p, p, p, p, p, p, cos, sin, sink_rows)


def _attn_bwd(cfg, p, dcat, cos, sin, sink_rows, name):
    n_lat, n_ctx, cb = cfg.n_lat, cfg.n_ctx, cfg.ctx_blk

    def body(q_ref, k_ref, v_ref, qc_ref, kc_ref, vc_ref, cos_ref, sin_ref, sink_ref, do_ref, doc_ref,
             dq_ref, dk_ref, dv_ref, dqc_ref, dkc_ref, dvc_ref, dsink_ref,
             qr, ka, kb, va, vb, kca, kcb, vca, vcb, dqs, dka, dva, dkca, dvca):
        cos_t, sin_t = cos_ref[...], sin_ref[...]
        lane = lax.broadcasted_iota(jnp.int32, (1, 128), 1)
        lo = lane < 64
        for gq in range(4):
            qr[:, gq * 128:(gq + 1) * 128] = _rope(q_ref[:, gq * 128:(gq + 1) * 128].astype(F32), cos_t, sin_t).astype(BF16)
        kl = _rope(k_ref[...].astype(F32), cos_t, sin_t)
        dsink_ref[...] = jnp.zeros_like(dsink_ref)
        dka[...] = jnp.zeros_like(dka)
        dva[...] = jnp.zeros_like(dva)
        dkca[...] = jnp.zeros_like(dkca)
        dvca[...] = jnp.zeros_like(dvca)

        def halves(x):
            return jnp.where(lo, x, 0).astype(BF16), jnp.where(lo, 0, x).astype(BF16)

        for kh in range(KV_HEADS):
            _attn_prepare(kh, kl, v_ref[...].astype(F32), kc_ref[...].astype(F32), vc_ref[...].astype(F32),
                          ka, kb, va, vb, kca, kcb, vca, vcb)

            def one_head(head, qp, q_half, do_p, do_half, kw, kcx, vw, vcx, win, valid):
                sk = sink_ref[head:head + 1, 0:1]
                s_list = [_dot_nt(qp, kcx[...]) * ATT_SCALE]
                if win is not None:
                    s_list.insert(0, jnp.where(valid, _dot_nt(qp, kw[win, :]) * ATT_SCALE, NEG_INF))
                probs, p_sink = _softmax_parts(s_list, sk)
                vals = [vcx[...]] if win is None else [vw[win, :], vcx[...]]
                dps = [_dot_nt(do_p, vv) for vv in vals]
                dr = None
                for pp, dp in zip(probs, dps):
                    t = jnp.sum(pp * dp, axis=1, keepdims=True)
                    dr = t if dr is None else dr + t
                dss = [(pp * (dp - dr) * ATT_SCALE).astype(BF16) for pp, dp in zip(probs, dps)]
                dsink_ref[head:head + 1, :] += jnp.broadcast_to(
                    jnp.sum(-p_sink * dr, axis=0, keepdims=True), (1, 128))
                p_c, ds_c = probs[-1], dss[-1]
                dq = _dot(ds_c, kcx[...])
                dkca[kh] += _dot_tn(ds_c, q_half)
                dvca[kh] += _dot_tn(p_c, do_half)
                if win is not None:
                    dq = dq + _dot(dss[0], kw[win, :])
                    dka[kh, win, :] += _dot_tn(dss[0], q_half)
                    dva[kh, win, :] += _dot_tn(probs[0], do_half)
                return dq

            def lat_block(n, carry):
                r0, start, valid = _window(cfg, n)
                win = pl.ds(start, 3 * BLK)
                lanes = [slice((kh * 2 + pr) * 128, (kh * 2 + pr + 1) * 128) for pr in range(2)]
                qps = [qr[pl.ds(r0, BLK), lanes[pr]] for pr in range(2)]
                dops = [do_ref[pl.ds(r0, BLK), lanes[pr]].astype(BF16) for pr in range(2)]
                heads = [(pr, half) for pr in range(2) for half in range(2)]
                kws, kcs = (ka[win, :], kb[win, :]), (kca[...], kcb[...])
                vws, vcs = (va[win, :], vb[win, :]), (vca[...], vcb[...])
                soft = []
                for idx, (pr, half) in enumerate(heads):
                    s_w = jnp.where(valid, _dot_nt(qps[pr], kws[half]) * ATT_SCALE, NEG_INF)
                    s_c = _dot_nt(qps[pr], kcs[half]) * ATT_SCALE
                    soft.append(_softmax_parts([s_w, s_c], sink_ref[kh * 4 + idx:kh * 4 + idx + 1, 0:1]))
                dps = [(_dot_nt(dops[pr], vws[half]), _dot_nt(dops[pr], vcs[half])) for pr, half in heads]
                ds_w, ds_c, pb_w, pb_c = [], [], [], []
                for idx in range(4):
                    (p_w, p_c), p_sink = soft[idx]
                    dp_w, dp_c = dps[idx]
                    dr = jnp.sum(p_w * dp_w, axis=1, keepdims=True) + jnp.sum(p_c * dp_c, axis=1, keepdims=True)
                    ds_w.append((p_w * (dp_w - dr) * ATT_SCALE).astype(BF16))
                    ds_c.append((p_c * (dp_c - dr) * ATT_SCALE).astype(BF16))
                    pb_w.append(p_w.astype(BF16))
                    pb_c.append(p_c.astype(BF16))
                    head = kh * 4 + idx
                    dsink_ref[head:head + 1, :] += jnp.broadcast_to(
                        jnp.sum(-p_sink * dr, axis=0, keepdims=True), (1, 128))
                for pr in range(2):
                    dqs[pl.ds(r0, BLK), lanes[pr]] = (
                        _dot(ds_w[2 * pr], kws[0]) + _dot(ds_c[2 * pr], kcs[0])
                        + _dot(ds_w[2 * pr + 1], kws[1]) + _dot(ds_c[2 * pr + 1], kcs[1]))
                q_hs, do_hs = [halves(qp) for qp in qps], [halves(do_p) for do_p in dops]
                q_stack = jnp.concatenate([q_hs[pr][half] for pr, half in heads], axis=0)
                do_stack = jnp.concatenate([do_hs[pr][half] for pr, half in heads], axis=0)
                dka[kh, win, :] += _dot_tn(jnp.concatenate(ds_w, axis=0), q_stack)
                dva[kh, win, :] += _dot_tn(jnp.concatenate(pb_w, axis=0), do_stack)
                dkca[kh] += _dot_tn(jnp.concatenate(ds_c, axis=0), q_stack)
                dvca[kh] += _dot_tn(jnp.concatenate(pb_c, axis=0), do_stack)
                return carry

            lax.fori_loop(0, n_lat // BLK, lat_block, 0, unroll=2)
            for n in range(n_ctx // BLK):
                rows = slice(n * BLK, (n + 1) * BLK)
                for pr in range(2):
                    lanes = slice((kh * 2 + pr) * 128, (kh * 2 + pr + 1) * 128)
                    qp = qc_ref[rows, lanes].astype(BF16)
                    do_p = doc_ref[rows, lanes]
                    q_h, do_h = halves(qp), halves(do_p)
                    dq = None
                    for half, (kcx, vcx) in enumerate(((kca, vca), (kcb, vcb))):
                        part = one_head(kh * 4 + pr * 2 + half, qp, q_h[half], do_p, do_h[half],
                                        None, kcx, None, vcx, None, None)
                        dq = part if dq is None else dq + part
                    dqc_ref[rows, lanes] = dq.astype(BF16)

        def fold(acc):
            r0 = acc[0] + pltpu.roll(acc[0], 64, 1)
            r1 = acc[1] + pltpu.roll(acc[1], 64, 1)
            return jnp.where(lo, r0, r1)

        for gq in range(4):
            sl = slice(gq * 128, (gq + 1) * 128)
            dq_ref[:, sl] = _rope_t(dqs[:, sl], cos_t, sin_t).astype(BF16)
        dk_ref[...] = _rope_t(fold(dka), cos_t, sin_t).astype(BF16)
        dv_ref[...] = fold(dva).astype(BF16)
        dkc_ref[...] = fold(dkca).astype(BF16)
        dvc_ref[...] = fold(dvca).astype(BF16)

    lat = lambda w: pl.BlockSpec((n_lat, w), lambda e: (e, 0))
    ctx = lambda w: pl.BlockSpec((n_ctx, w), lambda e: (e, 0))
    sd = jax.ShapeDtypeStruct
    return pl.pallas_call(
        body, grid=(2,), name=name,
        in_specs=_attn_specs(cfg) + [pl.BlockSpec((n_lat, ATT_W), lambda e: (e, 0)),
                                     pl.BlockSpec((n_ctx, ATT_W), lambda e: (cb + e, 0))],
        out_specs=[lat(ATT_W), lat(128), lat(128), ctx(ATT_W), ctx(128), ctx(128),
                   pl.BlockSpec((None, 8, 128), lambda e: (e, 0, 0))],
        out_shape=[sd((cfg.t_lat, ATT_W), BF16), sd((cfg.t_lat, 128), BF16), sd((cfg.t_lat, 128), BF16),
                   sd((cfg.t_ctx, ATT_W), BF16), sd((cfg.t_ctx, 128), BF16), sd((cfg.t_ctx, 128), BF16),
                   sd((2, 8, 128), F32)],
        scratch_shapes=[pltpu.VMEM((n_lat, ATT_W), BF16)] + [pltpu.VMEM((n_lat, 128), BF16)] * 4
        + [pltpu.VMEM((n_ctx, 128), BF16)] * 4
        + [pltpu.VMEM((n_lat, ATT_W), F32), pltpu.VMEM((2, n_lat, 128), F32), pltpu.VMEM((2, n_lat, 128), F32),
           pltpu.VMEM((2, n_ctx, 128), F32), pltpu.VMEM((2, n_ctx, 128), F32)],
        compiler_params=_params(("parallel",)))(p, p, p, p, p, p, cos, sin, sink_rows, dcat, dcat)


def _shift_down(x, k, row):
    return jnp.where(row >= k, pltpu.roll(x, k, 0), 0.0)


def _shift_up(x, k, row):
    n = x.shape[0]
    return jnp.where(row < n - k, pltpu.roll(x, n - k, 0), 0.0)


def _window_sum(x, r, row):
    below, above, k = x, x, 1
    while k < r:
        below = below + _shift_down(below, k, row)
        above = above + _shift_up(above, k, row)
        k *= 2
    return below + _shift_down(x, r, row) + _shift_up(above, 1, row)


def _inv_count(r, row, n):
    cnt = jnp.minimum(row + r, n - 1) + 1 - jnp.maximum(row - r, 0)
    return 1.0 / cnt.astype(F32)


def _pool_fwd(p, w, scale, n, blk0, n_seg, name):
    def body(u0, u1, u2, u3, w_ref, sc_ref, o_ref):
        row = lax.broadcasted_iota(jnp.int32, (n, 1), 0)
        for g, u_ref in enumerate((u0, u1, u2, u3)):
            u = u_ref[...].astype(F32)
            d = _window_sum(u, POOL_R[g], row) * _inv_count(POOL_R[g], row, n) - u
            o_ref[:, g * 128:(g + 1) * 128] = (_dot(d, w_ref[g]) * sc_ref[:, g * 128:(g + 1) * 128]).astype(BF16)

    return pl.pallas_call(
        body, grid=(n_seg,), name=name,
        in_specs=[pl.BlockSpec((n, 128), functools.partial(lambda g, e: (blk0 + e, 6 + g), g)) for g in range(4)]
        + [pl.BlockSpec((4, 128, 128), lambda e: (0, 0, 0)), pl.BlockSpec((1, 512), lambda e: (0, 0))],
        out_specs=pl.BlockSpec((n, 512), lambda e: (e, 0)),
        out_shape=jax.ShapeDtypeStruct((n_seg * n, 512), BF16),
        compiler_params=_params(("parallel",)))(p, p, p, p, w, scale)


def _pool_bwd(p, w, scale, dcat, n, blk0, n_seg, name):
    def body(u0, u1, u2, u3, w_ref, sc_ref, dp_ref, du_ref, dw_ref, dsc_ref):
        e = pl.program_id(0)

        @pl.when(e == 0)
        def _():
            dw_ref[...] = jnp.zeros_like(dw_ref)
            dsc_ref[...] = jnp.zeros_like(dsc_ref)

        row = lax.broadcasted_iota(jnp.int32, (n, 1), 0)
        for g, u_ref in enumerate((u0, u1, u2, u3)):
            sl = slice(g * 128, (g + 1) * 128)
            u = u_ref[...].astype(F32)
            inv = _inv_count(POOL_R[g], row, n)
            d = _window_sum(u, POOL_R[g], row) * inv - u
            dp = dp_ref[:, sl].astype(F32)
            dsc_ref[:, sl] += jnp.sum(dp * _dot(d, w_ref[g]), axis=0, keepdims=True)
            dyp = dp * sc_ref[:, sl]
            dw_ref[g] += _dot_tn(d, dyp)
            dd = _dot_nt(dyp, w_ref[g])
            du_ref[:, sl] = (_window_sum(dd * inv, POOL_R[g], row) - dd).astype(BF16)

    return pl.pallas_call(
        body, grid=(n_seg,), name=name,
        in_specs=[pl.BlockSpec((n, 128), functools.partial(lambda g, e: (blk0 + e, 6 + g), g)) for g in range(4)]
        + [pl.BlockSpec((4, 128, 128), lambda e: (0, 0, 0)), pl.BlockSpec((1, 512), lambda e: (0, 0)),
           pl.BlockSpec((n, 512), lambda e: (blk0 + e, 1))],
        out_specs=[pl.BlockSpec((n, 512), lambda e: (e, 0)),
                   pl.BlockSpec((4, 128, 128), lambda e: (0, 0, 0)), pl.BlockSpec((1, 512), lambda e: (0, 0))],
        out_shape=[jax.ShapeDtypeStruct((n_seg * n, 512), BF16), jax.ShapeDtypeStruct((4, 128, 128), F32),
                   jax.ShapeDtypeStruct((1, 512), F32)],
        compiler_params=_params(("arbitrary",)))(p, p, p, p, w, scale, dcat)


def _gelu(x):
    t = jnp.tanh(math.sqrt(2.0 / math.pi) * (x + 0.044715 * x * x * x))
    return 0.5 * x * (1.0 + t), t


def _gelu_grad(x, t):
    return 0.5 * (1.0 + t) + 0.5 * x * (1.0 - t * t) * (math.sqrt(2.0 / math.pi) * (1.0 + 3 * 0.044715 * x * x))


def _neg_expm1_twice(x):
    t = jnp.tanh(x)
    return (-2.0 * t) / (1.0 - t)


def _softplus_neg(lam):
    x = -lam
    e = jnp.exp(-jnp.abs(x))
    log1p = jnp.where(e < 1e-2, e * (1.0 - e * (0.5 - e * (1.0 / 3.0))), jnp.log(1.0 + e))
    return jnp.maximum(x, 0.0) + log1p, -_sigmoid(x)


def _conv(u, w_ref, b_ref, row):
    return (b_ref[...] + _shift_down(u, 1, row) * w_ref[0:1, :] + u * w_ref[1:2, :]
            + _shift_up(u, 1, row) * w_ref[2:3, :] + _shift_up(u, 2, row) * w_ref[3:4, :])


def _lru_gates(uc, d, wa_ref, ba_ref, wx_ref, bx_ref, lam_ref):
    r = _sigmoid(_dot(uc, wa_ref[d]) + ba_ref[d:d + 1, :])
    gi = _sigmoid(_dot(uc, wx_ref[d]) + bx_ref[d:d + 1, :])
    sp, dsp = _softplus_neg(lam_ref[d:d + 1, :])
    la = (-LRU_C) * r * sp
    a = jnp.exp(la)
    sq = jnp.sqrt(_neg_expm1_twice(la))
    return r, gi, sp, dsp, a, sq


def _tile_scan(a_ref, b_ref, n, reverse):
    m = n // 8
    first = 7 if reverse else 0
    a_prev = a_ref[pl.ds(first, m, stride=8), :]
    b_prev = b_ref[pl.ds(first, m, stride=8), :]
    for j in (range(6, -1, -1) if reverse else range(1, 8)):
        rows = pl.ds(j, m, stride=8)
        aj = a_ref[rows, :]
        b_prev = aj * b_prev + b_ref[rows, :]
        a_prev = aj * a_prev
        b_ref[rows, :] = b_prev
        a_ref[rows, :] = a_prev


def _carry_scan(a_ref, b_ref, n, reverse, carry):
    nt8 = n // 8

    def step(i, c):
        t = (nt8 - 1 - i) if reverse else i
        off = pl.multiple_of(t * 8, 8)
        h = a_ref[pl.ds(off, 8), :] * c + b_ref[pl.ds(off, 8), :]
        b_ref[pl.ds(off, 8), :] = h
        return h[0:1, :] if reverse else h[7:8, :]

    return lax.fori_loop(0, nt8, step, carry, unroll=4)


def _chain_scan(segs, reverse):
    carry = jnp.zeros((1, 128), F32)
    for a, b, a_ref, b_ref, n in segs:
        a_ref[...] = a
        b_ref[...] = b
        _tile_scan(a_ref, b_ref, n, reverse)
        carry = _carry_scan(a_ref, b_ref, n, reverse, carry)


def _lru_specs(cfg):
    n_lat, n_ctx, cb = cfg.n_lat, cfg.n_ctx, cfg.ctx_blk
    return [pl.BlockSpec((n_lat, 128), lambda hb, e: (e, hb)),
            pl.BlockSpec((n_lat, 128), lambda hb, e: (e, 8 + hb)),
            pl.BlockSpec((n_ctx, 128), lambda hb, e: (cb + e, hb)),
            pl.BlockSpec((n_ctx, 128), lambda hb, e: (cb + e, 8 + hb)),
            pl.BlockSpec((4, 128), lambda hb, e: (0, hb)),
            pl.BlockSpec((1, 128), lambda hb, e: (0, hb)),
            pl.BlockSpec((2, None, 128, 128), lambda hb, e: (0, hb, 0, 0)),
            pl.BlockSpec((2, 128), lambda hb, e: (0, hb)),
            pl.BlockSpec((2, None, 128, 128), lambda hb, e: (0, hb, 0, 0)),
            pl.BlockSpec((2, 128), lambda hb, e: (0, hb)),
            pl.BlockSpec((2, 128), lambda hb, e: (0, hb))]


def _lru_fwd(cfg, p, consts, name):
    n_lat, n_ctx = cfg.n_lat, cfg.n_ctx

    def body(gl_ref, ul_ref, gc_ref, uc_ref, cw_ref, cb_ref, wa_ref, ba_ref, wx_ref, bx_ref, lam_ref,
             zl_ref, zc_ref, hl_ref, hc_ref, al, ac):
        row_l = lax.broadcasted_iota(jnp.int32, (n_lat, 1), 0)
        row_c = lax.broadcasted_iota(jnp.int32, (n_ctx, 1), 0)
        uc_l = _conv(ul_ref[...].astype(F32), cw_ref, cb_ref, row_l)
        uc_c = _conv(uc_ref[...].astype(F32), cw_ref, cb_ref, row_c)
        for d in range(2):
            _, gi_l, _, _, a_l, sq_l = _lru_gates(uc_l, d, wa_ref, ba_ref, wx_ref, bx_ref, lam_ref)
            _, gi_c, _, _, a_c, sq_c = _lru_gates(uc_c, d, wa_ref, ba_ref, wx_ref, bx_ref, lam_ref)
            _chain_scan([(a_c, sq_c * (gi_c * uc_c), ac, hc_ref.at[d], n_ctx),
                         (a_l, sq_l * (gi_l * uc_l), al, hl_ref.at[d], n_lat)], reverse=(d == 1))
        zl_ref[...] = (_gelu(gl_ref[...].astype(F32))[0] * (hl_ref[0] + hl_ref[1])).astype(BF16)
        zc_ref[...] = (_gelu(gc_ref[...].astype(F32))[0] * (hc_ref[0] + hc_ref[1])).astype(BF16)

    return pl.pallas_call(
        body, grid=(8, 2), name=name, in_specs=_lru_specs(cfg),
        out_specs=[pl.BlockSpec((n_lat, 128), lambda hb, e: (e, hb)), pl.BlockSpec((n_ctx, 128), lambda hb, e: (e, hb)),
                   pl.BlockSpec((2, n_lat, 128), lambda hb, e: (0, e, hb)),
                   pl.BlockSpec((2, n_ctx, 128), lambda hb, e: (0, e, hb))],
        out_shape=[jax.ShapeDtypeStruct((cfg.t_lat, D), BF16), jax.ShapeDtypeStruct((cfg.t_ctx, D), BF16),
                   jax.ShapeDtypeStruct((2, cfg.t_lat, D), F32), jax.ShapeDtypeStruct((2, cfg.t_ctx, D), F32)],
        scratch_shapes=[pltpu.VMEM((n_lat, 128), F32), pltpu.VMEM((n_ctx, 128), F32)],
        compiler_params=_params(("parallel", "arbitrary")))(p, p, p, p, *consts)


def _lru_bwd(cfg, p, dz, h_lat, h_ctx, consts, name):
    n_lat, n_ctx, cb = cfg.n_lat, cfg.n_ctx, cfg.ctx_blk

    def body(gl_ref, ul_ref, gc_ref, uc_ref, cw_ref, cb_ref, wa_ref, ba_ref, wx_ref, bx_ref, lam_ref,
             dzl_ref, dzc_ref, hl, hc, dgl_ref, dul_ref, dgc_ref, duc_ref, dwa_ref, dwx_ref, vec_ref,
             al, bl, ac, bc):
        e = pl.program_id(1)

        @pl.when(e == 0)
        def _():
            dwa_ref[...] = jnp.zeros_like(dwa_ref)
            dwx_ref[...] = jnp.zeros_like(dwx_ref)
            vec_ref[...] = jnp.zeros_like(vec_ref)

        row_l = lax.broadcasted_iota(jnp.int32, (n_lat, 1), 0)
        row_c = lax.broadcasted_iota(jnp.int32, (n_ctx, 1), 0)
        u_l, u_c = ul_ref[...].astype(F32), uc_ref[...].astype(F32)
        uc_l = _conv(u_l, cw_ref, cb_ref, row_l)
        uc_c = _conv(u_c, cw_ref, cb_ref, row_c)
        gel_l, t_l = _gelu(gl_ref[...].astype(F32))
        gel_c, t_c = _gelu(gc_ref[...].astype(F32))
        dz_l, dz_c = dzl_ref[...].astype(F32), dzc_ref[...].astype(F32)
        dgl_ref[...] = (dz_l * (hl[0] + hl[1]) * _gelu_grad(gl_ref[...].astype(F32), t_l)).astype(BF16)
        dgc_ref[...] = (dz_c * (hc[0] + hc[1]) * _gelu_grad(gc_ref[...].astype(F32), t_c)).astype(BF16)
        dy_l, dy_c = dz_l * gel_l, dz_c * gel_c
        duc_l = jnp.zeros((n_lat, 128), F32)
        duc_c = jnp.zeros((n_ctx, 128), F32)
        for d in range(2):
            r_l, gi_l, sp, dsp, a_l, sq_l = _lru_gates(uc_l, d, wa_ref, ba_ref, wx_ref, bx_ref, lam_ref)
            r_c, gi_c, _, _, a_c, sq_c = _lru_gates(uc_c, d, wa_ref, ba_ref, wx_ref, bx_ref, lam_ref)
            if d == 0:
                an_l = _shift_up(a_l, 1, row_l)
                an_c = jnp.where(row_c < n_ctx - 1, pltpu.roll(a_c, n_ctx - 1, 0), a_l[0:1, :])
            else:
                an_l = _shift_down(a_l, 1, row_l)
                an_c = jnp.where(row_c >= 1, pltpu.roll(a_c, 1, 0), a_l[n_lat - 1:n_lat, :])
            _chain_scan([(an_l, dy_l, al, bl, n_lat), (an_c, dy_c, ac, bc, n_ctx)], reverse=(d == 0))
            dsp_sum = jnp.zeros((1, 128), F32)
            for (dh, h, r, gi, a, sq, uc, seg) in ((bl[...], hl[d], r_l, gi_l, a_l, sq_l, uc_l, "l"),
                                                  (bc[...], hc[d], r_c, gi_c, a_c, sq_c, uc_c, "c")):
                b0 = sq * (gi * uc)
                t1 = dh * sq
                dla = dh * (h - b0) - (dh * gi * uc) * (a * a) / sq
                dzr = (dla * ((-LRU_C) * sp)) * r * (1.0 - r)
                dzi = (t1 * uc) * gi * (1.0 - gi)
                dsp_sum = dsp_sum + jnp.sum(dla * ((-LRU_C) * r), axis=0, keepdims=True)
                dwa_ref[d] += _dot_tn(uc, dzr)
                dwx_ref[d] += _dot_tn(uc, dzi)
                vec_ref[d:d + 1, :] += jnp.sum(dzr, axis=0, keepdims=True)
                vec_ref[2 + d:3 + d, :] += jnp.sum(dzi, axis=0, keepdims=True)
                duc = t1 * gi + _dot_nt(dzr, wa_ref[d]) + _dot_nt(dzi, wx_ref[d])
                if seg == "l":
                    duc_l = duc_l + duc
                else:
                    duc_c = duc_c + duc
            vec_ref[4 + d:5 + d, :] += dsp_sum * dsp
        for duc, u, row, du_ref in ((duc_l, u_l, row_l, dul_ref), (duc_c, u_c, row_c, duc_ref)):
            du_ref[...] = (_shift_up(duc, 1, row) * cw_ref[0:1, :] + duc * cw_ref[1:2, :]
                           + _shift_down(duc, 1, row) * cw_ref[2:3, :]
                           + _shift_down(duc, 2, row) * cw_ref[3:4, :]).astype(BF16)
            vec_ref[6:7, :] += jnp.sum(duc * _shift_down(u, 1, row), axis=0, keepdims=True)
            vec_ref[7:8, :] += jnp.sum(duc * u, axis=0, keepdims=True)
            vec_ref[8:9, :] += jnp.sum(duc * _shift_up(u, 1, row), axis=0, keepdims=True)
            vec_ref[9:10, :] += jnp.sum(duc * _shift_up(u, 2, row), axis=0, keepdims=True)
            vec_ref[10:11, :] += jnp.sum(duc, axis=0, keepdims=True)

    lat = pl.BlockSpec((n_lat, 128), lambda hb, e: (e, hb))
    ctx = pl.BlockSpec((n_ctx, 128), lambda hb, e: (e, hb))
    wspec = pl.BlockSpec((2, None, 128, 128), lambda hb, e: (0, hb, 0, 0))
    sd = jax.ShapeDtypeStruct
    return pl.pallas_call(
        body, grid=(8, 2), name=name,
        in_specs=_lru_specs(cfg) + [pl.BlockSpec((n_lat, 128), lambda hb, e: (e, hb)),
                                    pl.BlockSpec((n_ctx, 128), lambda hb, e: (cb + e, hb)),
                                    pl.BlockSpec((2, n_lat, 128), lambda hb, e: (0, e, hb)),
                                    pl.BlockSpec((2, n_ctx, 128), lambda hb, e: (0, e, hb))],
        out_specs=[lat, lat, ctx, ctx, wspec, wspec, pl.BlockSpec((None, 16, 128), lambda hb, e: (hb, 0, 0))],
        out_shape=[sd((cfg.t_lat, D), BF16), sd((cfg.t_lat, D), BF16), sd((cfg.t_ctx, D), BF16), sd((cfg.t_ctx, D), BF16),
                   sd((2, 8, 128, 128), F32), sd((2, 8, 128, 128), F32), sd((8, 16, 128), F32)],
        scratch_shapes=[pltpu.VMEM((n_lat, 128), F32)] * 2 + [pltpu.VMEM((n_ctx, 128), F32)] * 2,
        compiler_params=_params(("parallel", "arbitrary")))(p, p, p, p, *consts, dz, dz, h_lat, h_ctx)


def _position():
    x, y, c = lax.axis_index("x"), lax.axis_index("y"), lax.axis_index("c")
    return x, y, c, 4 * x + 2 * y + c


def _peer(x, y, c, k):
    px = 1 - x if k & 4 else x
    py = 1 - y if k & 2 else y
    pc = 1 - c if k & 1 else c
    return (px, py, pc), 4 * px + 2 * py + pc


def _all_gather(v, name, in_vmem):
    def body(v_ref, o_ref, send_sems, recv_sems, local_sem):
        x, y, c, me = _position()
        mine = pltpu.make_async_copy(v_ref, o_ref.at[me], local_sem)
        mine.start()
        sends = []
        for k in range(1, N_DEV):
            peer, _ = _peer(x, y, c, k)
            cp = pltpu.make_async_remote_copy(src_ref=v_ref, dst_ref=o_ref.at[me], send_sem=send_sems.at[k - 1],
                                              recv_sem=recv_sems.at[k - 1], device_id=peer, device_id_type=MESH)
            cp.start()
            sends.append(cp)
        for k in range(1, N_DEV):
            peer, peer_lin = _peer(x, y, c, k)
            pltpu.make_async_remote_copy(src_ref=v_ref, dst_ref=o_ref.at[peer_lin], send_sem=send_sems.at[k - 1],
                                         recv_sem=recv_sems.at[k - 1], device_id=peer, device_id_type=MESH).wait_recv()
        for cp in sends:
            cp.wait_send()
        mine.wait()

    space = pltpu.VMEM if in_vmem else pl.ANY
    return pl.pallas_call(
        body, name=name,
        in_specs=[pl.BlockSpec(memory_space=space)], out_specs=pl.BlockSpec(memory_space=space),
        out_shape=jax.ShapeDtypeStruct((N_DEV,) + v.shape, v.dtype),
        scratch_shapes=[pltpu.SemaphoreType.DMA((N_DEV - 1,)), pltpu.SemaphoreType.DMA((N_DEV - 1,)),
                        pltpu.SemaphoreType.DMA],
        compiler_params=pltpu.CompilerParams(vmem_limit_bytes=VMEM_LIMIT))(v)


_HBM = pl.BlockSpec(memory_space=pltpu.HBM)
_SEM = pl.BlockSpec(memory_space=pltpu.SEMAPHORE)
_EFFECT = pltpu.SideEffectType.DATAFLOW_SIDE_EFFECTING


ALL_PEERS = tuple(range(1, N_DEV))
SAME_CORE_AND_SIBLING = (1, 2, 4, 6)


def _push_start(src, land, block_of, name, relations=ALL_PEERS):
    def body(src_ref, land_ref, send_sem, recv_sem, src_thru, land_thru, token):
        x, y, c, me = _position()
        for k in relations:
            peer, peer_lin = _peer(x, y, c, k)
            mine, there = block_of(src_ref, land_ref, me, peer_lin)
            pltpu.make_async_remote_copy(src_ref=mine, dst_ref=there, send_sem=send_sem, recv_sem=recv_sem,
                                         device_id=peer, device_id_type=MESH).start()
        mine, here = block_of(src_ref, land_ref, me, me)
        pltpu.make_async_copy(mine, here, recv_sem).start()
        token[...] = jnp.zeros_like(token)

    return pl.pallas_call(
        body, name=name,
        out_shape=(pltpu.SemaphoreType.DMA(()), pltpu.SemaphoreType.DMA(()), pltpu.HBM(src.shape, src.dtype),
                   pltpu.HBM(land.shape, land.dtype), jax.ShapeDtypeStruct((8, 128), F32)),
        in_specs=(_HBM, _HBM), out_specs=(_SEM, _SEM, _HBM, _HBM, pl.BlockSpec(memory_space=pltpu.VMEM)),
        input_output_aliases={0: 2, 1: 3},
        compiler_params=pltpu.CompilerParams(has_side_effects=_EFFECT),
    )(pltpu.with_memory_space_constraint(src, pltpu.HBM), pltpu.with_memory_space_constraint(land, pltpu.HBM))


def _push_wait(handle, blocks_of, after, name, n_peers=N_DEV - 1):
    send_sem, recv_sem, src_thru, land_thru, _ = handle

    def body(src_ref, land_ref, send_sem, recv_sem, after_ref, src_dead, got_ref):
        x, y, c, _ = _position()
        sent, landed = blocks_of(land_ref, n_peers), blocks_of(land_ref, n_peers + 1)
        pltpu.make_async_remote_copy(src_ref=sent, dst_ref=sent, send_sem=send_sem, recv_sem=recv_sem,
                                     device_id=(x, y, 1 - c), device_id_type=MESH).wait_send()
        pltpu.make_async_remote_copy(src_ref=landed, dst_ref=landed, send_sem=send_sem, recv_sem=recv_sem,
                                     device_id=(x, y, 1 - c), device_id_type=MESH).wait_recv()

    return pl.pallas_call(
        body, name=name,
        out_shape=(pltpu.HBM(src_thru.shape, src_thru.dtype), pltpu.HBM(land_thru.shape, land_thru.dtype)),
        in_specs=(_HBM, _HBM, _SEM, _SEM, pl.BlockSpec(memory_space=pl.ANY)), out_specs=(_HBM, _HBM),
        input_output_aliases={0: 0, 1: 1},
        compiler_params=pltpu.CompilerParams(has_side_effects=_EFFECT),
    )(src_thru, land_thru, send_sem, recv_sem, after)[1]


def _gather_start(src, name, relations=ALL_PEERS):
    g, r, C = src.shape
    land = lax.empty((g, N_DEV * r, C), src.dtype)
    return _push_start(src, land, lambda s, z, i, p: (s, z.at[:, pl.ds(i * r, r), :]), name, relations)


def _gather_wait(handle, after, name, n_peers=N_DEV - 1):
    r = handle[2].shape[1]
    return _push_wait(handle, lambda z, n: z.at[:, pl.ds(0, n * r), :], after, name, n_peers)


def _relay_start(land, r, name):
    def body(land_ref, send_sem, recv_sem, land_thru, token):
        x, y, c, _ = _position()
        for k in (2, 4, 6):
            _, origin = _peer(x, y, c, k)
            rows = land_ref.at[:, pl.ds(origin * r, r), :]
            pltpu.make_async_remote_copy(src_ref=rows, dst_ref=rows, send_sem=send_sem, recv_sem=recv_sem,
                                         device_id=(x, y, 1 - c), device_id_type=MESH).start()
        token[...] = jnp.zeros_like(token)

    return pl.pallas_call(
        body, name=name,
        out_shape=(pltpu.SemaphoreType.DMA(()), pltpu.SemaphoreType.DMA(()), pltpu.HBM(land.shape, land.dtype),
                   jax.ShapeDtypeStruct((8, 128), F32)),
        in_specs=(_HBM,), out_specs=(_SEM, _SEM, _HBM, pl.BlockSpec(memory_space=pltpu.VMEM)),
        input_output_aliases={0: 2},
        compiler_params=pltpu.CompilerParams(has_side_effects=_EFFECT),
    )(pltpu.with_memory_space_constraint(land, pltpu.HBM))


def _relay_wait(handle, r, after, name):
    send_sem, recv_sem, land_thru, _ = handle

    def body(land_ref, send_sem, recv_sem, after_ref, got_ref):
        x, y, c, _ = _position()
        three = land_ref.at[:, pl.ds(0, 3 * r), :]
        cp = pltpu.make_async_remote_copy(src_ref=three, dst_ref=three, send_sem=send_sem, recv_sem=recv_sem,
                                          device_id=(x, y, 1 - c), device_id_type=MESH)
        cp.wait_send()
        cp.wait_recv()

    return pl.pallas_call(
        body, name=name, out_shape=(pltpu.HBM(land_thru.shape, land_thru.dtype),),
        in_specs=(_HBM, _SEM, _SEM, pl.BlockSpec(memory_space=pl.ANY)), out_specs=(_HBM,),
        input_output_aliases={0: 0},
        compiler_params=pltpu.CompilerParams(has_side_effects=_EFFECT),
    )(land_thru, send_sem, recv_sem, after)[0]


def _exchange_start(grad, name):
    g, rows, C = grad.shape
    r = rows // N_DEV
    land = lax.empty((N_DEV, g, r, C), grad.dtype)
    return _push_start(grad, land, lambda s, z, i, p: (s.at[:, pl.ds(p * r, r), :], z.at[i]), name)


def _exchange_wait(handle, after, name):
    return _push_wait(handle, lambda z, n: z.at[pl.ds(0, n)], after, name)


def _sum_blocks(v, name):
    k, rows, cols = v.shape
    tr = rows
    for cand in (rows, 512, 352, 256, 176, 128, 64, 32, 16):
        if rows % cand == 0 and k * cand * cols * v.dtype.itemsize <= 6 * 1024 * 1024:
            tr = cand
            break

    def body(v_ref, o_ref):
        acc = v_ref[0].astype(F32)
        for s in range(1, k):
            acc = acc + v_ref[s].astype(F32)
        o_ref[...] = acc

    return pl.pallas_call(
        body, grid=(rows // tr,), name=name,
        in_specs=[pl.BlockSpec((k, tr, cols), lambda i: (0, i, 0))],
        out_specs=pl.BlockSpec((tr, cols), lambda i: (i, 0)),
        out_shape=jax.ShapeDtypeStruct((rows, cols), F32),
        compiler_params=_params(("parallel",)))(v)


def _adam_math(w, g, m, v):
    m2 = B1 * m + (1.0 - B1) * g
    v2 = B2 * v + (1.0 - B2) * (g * g)
    m_hat = m2 / (1.0 - B1 ** STEP)
    v_hat = v2 / (1.0 - B2 ** STEP)
    return -LR * (m_hat / (jnp.sqrt(v_hat) + EPS) + WD * w), m2, v2


def _adamw(w, g, m, v, name, dep=None):
    shp = w.shape
    rows, cols = (shp[-2], shp[-1]) if len(shp) >= 2 else (1, shp[-1])
    lead = math.prod(shp[:-2]) if len(shp) > 2 else 1
    fits = [t for t in range(8, rows + 1, 8) if rows % t == 0 and t * cols * 4 <= 2 * 1024 * 1024]
    tr = max(fits) if fits else rows

    def body(w_ref, g_ref, m_ref, v_ref, *rest):
        d_ref, m2_ref, v2_ref = rest[-3:]
        d_ref[...], m2_ref[...], v2_ref[...] = _adam_math(w_ref[...], g_ref[...], m_ref[...], v_ref[...])

    blk = pl.BlockSpec((None, tr, cols), lambda b, i: (b, i, 0))
    extra = [] if dep is None else [dep]
    outs = pl.pallas_call(
        body, grid=(lead, rows // tr), name=name,
        in_specs=[blk] * 4 + [pl.BlockSpec(memory_space=pl.ANY)] * len(extra), out_specs=[blk] * 3,
        out_shape=[jax.ShapeDtypeStruct((lead, rows, cols), F32)] * 3,
        compiler_params=_params(("parallel", "parallel")))(*[a.reshape(lead, rows, cols) for a in (w, g, m, v)], *extra)
    return [o.reshape(shp) for o in outs]


def _as2d(a):
    n = a.size
    if n % 1024 == 0:
        return a.reshape(n // 1024, 1024)
    if n % 128 == 0:
        return a.reshape(n // 128, 128)
    return a.reshape(1, n)


def _blocks_to_cols(a):
    b = jnp.moveaxis(a, 0, -2)
    return b.reshape(b.shape[:-2] + (b.shape[-2] * b.shape[-1],))


def _pack_rows(parts):
    padded, offs, r = [], [], 0
    for p in parts:
        pad = (-p.shape[0]) % 8
        padded.append(jnp.pad(p, ((0, pad), (0, 0))) if pad else p)
        offs.append(r)
        r += p.shape[0] + pad
    return jnp.concatenate(padded, axis=0), offs


def _silu(x):
    return x * jax.nn.sigmoid(x)


def kernel(x, c, ctx, c_ctx, w_mod, b_mod, ln_g, ln_b, ffn_w_gate, ffn_w_up, ffn_w_down, mix_ab_w_in, attn_sink, pool_w, pool_scale, mix_ab_w_out, lru_w_in, lru_conv_w, lru_conv_b, lru_wa, lru_ba, lru_wx, lru_bx, lru_lambda, lru_w_out, loss_target, m_c_ctx, m_w_mod, m_b_mod, m_ln_g, m_ln_b, m_ffn_w_gate, m_ffn_w_up, m_ffn_w_down, m_mix_ab_w_in, m_attn_sink, m_pool_w, m_pool_scale, m_mix_ab_w_out, m_lru_w_in, m_lru_conv_w, m_lru_conv_b, m_lru_wa, m_lru_ba, m_lru_wx, m_lru_bx, m_lru_lambda, m_lru_w_out, v_c_ctx, v_w_mod, v_b_mod, v_ln_g, v_ln_b, v_ffn_w_gate, v_ffn_w_up, v_ffn_w_down, v_mix_ab_w_in, v_attn_sink, v_pool_w, v_pool_scale, v_mix_ab_w_out, v_lru_w_in, v_lru_conv_w, v_lru_conv_b, v_lru_wa, v_lru_ba, v_lru_wx, v_lru_bx, v_lru_lambda, v_lru_w_out):
    weights = dict(c_ctx=c_ctx, w_mod=w_mod, b_mod=b_mod, ln_g=ln_g, ln_b=ln_b, ffn_w_gate=ffn_w_gate,
                   ffn_w_up=ffn_w_up, ffn_w_down=ffn_w_down, mix_ab_w_in=mix_ab_w_in, attn_sink=attn_sink,
                   pool_w=pool_w, pool_scale=pool_scale, mix_ab_w_out=mix_ab_w_out, lru_w_in=lru_w_in,
                   lru_conv_w=lru_conv_w, lru_conv_b=lru_conv_b, lru_wa=lru_wa, lru_ba=lru_ba, lru_wx=lru_wx,
                   lru_bx=lru_bx, lru_lambda=lru_lambda, lru_w_out=lru_w_out)
    mom_m = dict(c_ctx=m_c_ctx, w_mod=m_w_mod, b_mod=m_b_mod, ln_g=m_ln_g, ln_b=m_ln_b, ffn_w_gate=m_ffn_w_gate,
                 ffn_w_up=m_ffn_w_up, ffn_w_down=m_ffn_w_down, mix_ab_w_in=m_mix_ab_w_in, attn_sink=m_attn_sink,
                 pool_w=m_pool_w, pool_scale=m_pool_scale, mix_ab_w_out=m_mix_ab_w_out, lru_w_in=m_lru_w_in,
                 lru_conv_w=m_lru_conv_w, lru_conv_b=m_lru_conv_b, lru_wa=m_lru_wa, lru_ba=m_lru_ba, lru_wx=m_lru_wx,
                 lru_bx=m_lru_bx, lru_lambda=m_lru_lambda, lru_w_out=m_lru_w_out)
    mom_v = dict(c_ctx=v_c_ctx, w_mod=v_w_mod, b_mod=v_b_mod, ln_g=v_ln_g, ln_b=v_ln_b, ffn_w_gate=v_ffn_w_gate,
                 ffn_w_up=v_ffn_w_up, ffn_w_down=v_ffn_w_down, mix_ab_w_in=v_mix_ab_w_in, attn_sink=v_attn_sink,
                 pool_w=v_pool_w, pool_scale=v_pool_scale, mix_ab_w_out=v_mix_ab_w_out, lru_w_in=v_lru_w_in,
                 lru_conv_w=v_lru_conv_w, lru_conv_b=v_lru_conv_b, lru_wa=v_lru_wa, lru_ba=v_lru_ba, lru_wx=v_lru_wx,
                 lru_bx=v_lru_bx, lru_lambda=v_lru_lambda, lru_w_out=v_lru_w_out)
    names = list(weights)

    n_lat, n_ctx = x.shape[1], ctx.shape[1]
    cfg = _Cfg(n_lat, n_ctx)
    _, _, _, me = _position()
    mcols = w_mod.shape[2]

    def t_bf16(w):
        return jnp.swapaxes(w, -1, -2).astype(BF16)

    def ffn_src(l, i):
        return jnp.stack([t_bf16(ffn_w_gate[l, i]), t_bf16(ffn_w_up[l, i]), ffn_w_down[l, i].astype(BF16)])

    pending = {}

    def start_gathers(items, tok):
        for key, make_src in items:
            pending[key] = _gather_start(make_src() + tok.astype(BF16), "gather_start_" + key)
            tok = pending[key][4][0, 0]
        return tok

    def weights_now(key, after):
        return _gather_wait(pending[key], after, "gather_wait_" + key)

    first = _gather_start(ffn_src(0, 0), "gather_start_ffn00", SAME_CORE_AND_SIBLING)
    tok = first[4][0, 0]

    small_names = ["ln_g", "ln_b", "lru_conv_w", "lru_conv_b", "lru_ba", "lru_bx", "lru_lambda"]
    small, small_off = _pack_rows([(c + tok).reshape(-1, 128)] + [weights[n].reshape(-1, 128) for n in small_names])
    small_all = _all_gather(small, "gather_small", True)

    def small_full(idx, shp):
        rows = math.prod(shp) // 128
        return _blocks_to_cols(small_all[:, small_off[idx]:small_off[idx] + rows, :].reshape((N_DEV,) + shp))

    c_all = small_all[:, :2 * D // 128, :].reshape(2 * N_DEV, D)
    ln_g_f, ln_b_f = small_full(1, ln_g.shape), small_full(2, ln_b.shape)
    lru_consts = (small_full(3, lru_conv_w.shape)[0], small_full(4, lru_conv_b.shape), lru_wa[0],
                  small_full(5, lru_ba.shape)[0], lru_wx[0], small_full(6, lru_bx.shape)[0],
                  small_full(7, lru_lambda.shape)[0])

    s_rows = jnp.zeros((32, D), F32).at[:16].set(_silu(c_all)).at[16].set(_silu(c_ctx)).astype(BF16)
    mod_mine = jnp.stack([_matmul(s_rows, w_mod[l], "nn", F32, "mod_fwd", bn_cap=1280) for l in range(2)])
    mod_all = _all_gather(mod_mine.reshape(64, mcols), "gather_mod", True).reshape(N_DEV, 2, 32, mcols)
    r_ffn = ffn_w_down.shape[2]
    relay = _relay_start(_gather_wait(first, mod_all, "gather_wait_ffn00", n_peers=len(SAME_CORE_AND_SIBLING)),
                         r_ffn, "gather_relay_start_ffn00")
    tok = start_gathers([("ab_in", lambda: t_bf16(mix_ab_w_in)), ("ab_out", lambda: mix_ab_w_out.astype(BF16)),
                         ("ffn01", lambda: ffn_src(0, 1)), ("ffn10", lambda: ffn_src(1, 0)),
                         ("lru_in", lambda: t_bf16(lru_w_in)), ("lru_out", lambda: lru_w_out.astype(BF16)),
                         ("ffn11", lambda: ffn_src(1, 1))], relay[3][0, 0])
    mod_full = _blocks_to_cols(mod_all) + (b_mod[:, None, :] + tok)
    ex0 = 2 * me
    mods = []
    for l in range(2):
        rows = jnp.stack([lax.dynamic_index_in_dim(mod_full[l], ex0, 0, False),
                          lax.dynamic_index_in_dim(mod_full[l], ex0 + 1, 0, False), mod_full[l, 16]])
        mods.append(rows.reshape(3, N_MOD, D))

    h0 = jnp.concatenate([x.reshape(cfg.t_lat, D), ctx.reshape(cfg.t_ctx, D)], axis=0)
    cos, sin = _rope_tables(n_lat)
    sink_rows = jnp.broadcast_to(attn_sink[0][:, None], (8, 128)).astype(F32)

    saved = []
    wf = [[None, None], [None, None]]
    h = h0
    xin = _modulate(cfg, h0, mods[0], 0, 1, "modulate_in")
    for l in range(2):
        st = {"h_in": h, "xin1": xin}
        wf[l][0] = (_relay_wait(relay, r_ffn, xin, "gather_relay_wait_ffn00") if l == 0
                    else weights_now("ffn10", xin))
        g1, u1, y1 = _ffn_fwd(xin, wf[l][0], "ffn_fwd")
        h1, xhat1, rstd1, xin2 = _ln_fwd(cfg, h, y1, mods[l], 2, 0.5, ln_g_f[l, 0][None], ln_b_f[l, 0][None],
                                          mods[l], (3, 4), "ln_fwd_a")
        st.update(g1=g1, u1=u1, y1=y1, h1=h1, xhat1=xhat1, rstd1=rstd1, xin2=xin2)
        if l == 0:
            w_ab_in_t = weights_now("ab_in", xin2)[0]
            p = _matmul(xin2, w_ab_in_t, "nt", BF16, "mix_ab_in")
            att_l, att_c = _attn_fwd(cfg, p, cos, sin, sink_rows, "attn_fwd")
            pool_l = _pool_fwd(p, pool_w[0], pool_scale, n_lat, 0, 2, "pool_fwd_lat")
            pool_c = _pool_fwd(p, pool_w[0], pool_scale, n_ctx, cfg.ctx_blk, 2, "pool_fwd_ctx")
            cat = jnp.concatenate([jnp.concatenate([att_l, pool_l], axis=1),
                                   jnp.concatenate([att_c, pool_c], axis=1)], axis=0)
            w_ab_out = weights_now("ab_out", cat)[0]
            y2 = _matmul(cat, w_ab_out, "nn", BF16, "mix_ab_out")
        else:
            w_lru_in_t = weights_now("lru_in", xin2)[0]
            p = _matmul(xin2, w_lru_in_t, "nt", BF16, "lru_in")
            z_l, z_c, st["h_lat"], st["h_ctx"] = _lru_fwd(cfg, p, lru_consts, "lru_fwd")
            cat = jnp.concatenate([z_l, z_c], axis=0)
            w_lru_out = weights_now("lru_out", cat)[0]
            y2 = _matmul(cat, w_lru_out, "nn", BF16, "lru_out")
        h2, xhat2, rstd2, xin3 = _ln_fwd(cfg, h1, y2, mods[l], 5, 1.0, ln_g_f[l, 1][None], ln_b_f[l, 1][None],
                                          mods[l], (6, 7), "ln_fwd_b")
        wf[l][1] = weights_now("ffn%d1" % l, xin3)
        g3, u3, y3 = _ffn_fwd(xin3, wf[l][1], "ffn_fwd")
        if l == 0:
            h3, xhat3, rstd3, xin = _ln_fwd(cfg, h2, y3, mods[l], 8, 0.5, ln_g_f[l, 2][None], ln_b_f[l, 2][None],
                                            mods[1], (0, 1), "ln_fwd_a")
        else:
            h3, xhat3, rstd3 = _ln_fwd(cfg, h2, y3, mods[l], 8, 0.5, ln_g_f[l, 2][None], ln_b_f[l, 2][None],
                                       None, None, "ln_fwd_last")
        st.update(p=p, cat=cat, y2=y2, h2=h2, xhat2=xhat2, rstd2=rstd2, xin3=xin3, g3=g3, u3=u3, y3=y3,
                  xhat3=xhat3, rstd3=rstd3)
        saved.append(st)
        h = h3

    dy, loss_tile = _loss(cfg, h, loss_target.reshape(cfg.t_lat, D), "loss")
    loss = lax.psum(loss_tile[0, 0], ("x", "y", "c"))

    grads = {}
    dmod = [None, None]
    recv_ffn = [[None, None], [None, None]]
    dln_g = [[None] * 3, [None] * 3]
    dln_b = [[None] * 3, [None] * 3]

    def ffn_weight_grads(tag, xin_b, dg, du, a_act, dys, dep=None):
        handles = []
        for k, (lhs, rhs) in enumerate(((dg, xin_b), (du, xin_b), (a_act, dys))):
            part = _matmul(lhs, rhs, "tn", BF16, "ffn_dw", bm_cap=1408, bk_cap=2304, dep=dep)[None]
            handles.append(_exchange_start(part, "exchange_start_ffn%s_%d" % (tag, k)))
        return handles

    def pin(handles):
        total = handles[0][4][0, 0]
        for hd in handles[1:]:
            total = total + hd[4][0, 0]
        return total

    up = (dy,)
    dmod_next = None
    last_sent = None
    for l in (1, 0):
        st = saved[l]
        dm = [None] * N_MOD

        def put_stats(stats, gate_idx, nxt):
            dm[gate_idx] = stats[:, 2, :]
            if nxt is not None:
                nxt[0][nxt[1]] = stats[:, 4, :]
                nxt[0][nxt[1] + 1] = stats[:, 3, :]

        lng3 = ln_g_f[l, 2][None] if last_sent is None else ln_g_f[l, 2][None] + pin(last_sent)
        if len(up) > 1:
            up = (up[0], up[1], ln_b_f[l, 2][None], up[3], up[4])
        dres, dys, stats = _ln_bwd(cfg, up, st["xhat3"], st["rstd3"], st["y3"], mods[l], 8, 0.5,
                                   lng3, "ln_bwd_fused" if len(up) > 1 else "ln_bwd_last")
        put_stats(stats, 8, None if len(up) == 1 else (dmod_next, 0))
        dln_g[l][2], dln_b[l][2] = stats[:, 0, :].sum(0), stats[:, 1, :].sum(0)
        dg, du, a_act, dxin = _ffn_bwd(dys, st["g3"], st["u3"], wf[l][1], "ffn_bwd")
        recv_ffn[l][1] = ffn_weight_grads("%d1" % l, st["xin3"], dg, du, a_act, dys)
        dres, dys, stats = _ln_bwd(cfg, (dres, dxin, ln_b_f[l, 1][None], mods[l], 7), st["xhat2"], st["rstd2"], st["y2"],
                                   mods[l], 5, 1.0, ln_g_f[l, 1][None] + pin(recv_ffn[l][1]), "ln_bwd_fused")
        put_stats(stats, 5, (dm, 6))
        dln_g[l][1], dln_b[l][1] = stats[:, 0, :].sum(0), stats[:, 1, :].sum(0)
        if l == 0:
            dw_out = _matmul(st["cat"], dys, "tn", BF16, "mix_ab_dw_out")
            dcat = _matmul(dys, w_ab_out, "nt", BF16, "mix_ab_dcat")
            dq, dk, dv, dqc, dkc, dvc, dsink = _attn_bwd(cfg, st["p"], dcat, cos, sin, sink_rows, "attn_bwd")
            du_l, dpw_l, dps_l = _pool_bwd(st["p"], pool_w[0], pool_scale, dcat, n_lat, 0, 2, "pool_bwd_lat")
            du_c, dpw_c, dps_c = _pool_bwd(st["p"], pool_w[0], pool_scale, dcat, n_ctx, cfg.ctx_blk, 2, "pool_bwd_ctx")
            dp = jnp.concatenate([jnp.concatenate([dq, dk, dv, du_l], axis=1),
                                  jnp.concatenate([dqc, dkc, dvc, du_c], axis=1)], axis=0)
            dw_in_t = _matmul(dp, st["xin2"], "tn", BF16, "mix_ab_dw_in", bm_cap=1280)
            dxin = _matmul(dp, w_ab_in_t, "nn", BF16, "mix_ab_dx")
            recv_mix = [_exchange_start(part, "exchange_start_mix_ab_%d" % k)
                        for k, part in enumerate((dw_in_t[None], dw_out[None], _as2d(dpw_l + dpw_c)[None]))]
            grads["attn_sink"] = (dsink[0, :, 0] + dsink[1, :, 0])[None, :]
            grads["pool_scale"] = dps_l + dps_c
        else:
            dw_out = _matmul(st["cat"], dys, "tn", BF16, "lru_dw_out")
            dz = _matmul(dys, w_lru_out, "nt", BF16, "lru_dz")
            dgl, dul, dgc, duc, dwa, dwx, vec = _lru_bwd(cfg, st["p"], dz, st["h_lat"], st["h_ctx"], lru_consts, "lru_bwd")
            dp = jnp.concatenate([jnp.concatenate([dgl, dul], axis=1), jnp.concatenate([dgc, duc], axis=1)], axis=0)
            dw_in_t = _matmul(dp, st["xin2"], "tn", BF16, "lru_dw_in", bm_cap=1024)
            dxin = _matmul(dp, w_lru_in_t, "nn", BF16, "lru_dx")
            recv_mix = [_exchange_start(part, "exchange_start_lru_%d" % k)
                        for k, part in enumerate((dw_in_t[None], dw_out[None], _as2d(dwa)[None], _as2d(dwx)[None]))]
            vec_t = jnp.moveaxis(vec, 0, 1).reshape(16, D)
            grads["lru_ba"], grads["lru_bx"] = vec_t[0:2], vec_t[2:4]
            grads["lru_lambda"], grads["lru_conv_w"], grads["lru_conv_b"] = vec_t[4:6], vec_t[6:10], vec_t[10:11]
        if l == 0:
            recv_ab = recv_mix
        else:
            recv_lru = recv_mix
        dres, dys, stats = _ln_bwd(cfg, (dres, dxin, ln_b_f[l, 0][None], mods[l], 4), st["xhat1"], st["rstd1"], st["y1"],
                                   mods[l], 2, 0.5, ln_g_f[l, 0][None] + pin(recv_mix), "ln_bwd_fused")
        put_stats(stats, 2, (dm, 3))
        dln_g[l][0], dln_b[l][0] = stats[:, 0, :].sum(0), stats[:, 1, :].sum(0)
        dg, du, a_act, dxin = _ffn_bwd(dys, st["g1"], st["u1"], wf[l][0], "ffn_bwd")
        recv_ffn[l][0] = ffn_weight_grads("%d0" % l, st["xin1"], dg, du, a_act, dys)
        last_sent = recv_ffn[l][0]
        dmod[l] = dm
        dmod_next = dm
        up = (dres, dxin, None, mods[l], 1)
    dh0, stats = _modulate_bwd(cfg, up[0], up[1], h0, mods[0] + pin(last_sent), 1, "modulate_bwd")
    dmod[0][0], dmod[0][1] = stats[:, 4, :], stats[:, 3, :]
    grad_x = dh0.reshape(x.shape)

    dmod_mine = jnp.stack([jnp.stack(dmod[l], axis=1).reshape(3, N_MOD * D) for l in range(2)])
    n_dm = 6 * N_MOD * D // 128
    dmod_sent = _gather_start(dmod_mine.reshape(1, n_dm, 128), "gather_start_dmod")
    last_ffn = recv_ffn[0][0]

    def arrived(handle, after, name):
        return _exchange_wait(handle, after, name)

    after_start = dmod_sent[4]
    for l, i in ((1, 1), (1, 0), (0, 1)):
        recv_ffn[l][i] = [arrived(hd, after_start, "exchange_wait_ffn%d%d_%d" % (l, i, k))
                          for k, hd in enumerate(recv_ffn[l][i])]
    recv_ab = [arrived(hd, after_start, "exchange_wait_mix_ab_%d" % k) for k, hd in enumerate(recv_ab)]
    recv_lru = [arrived(hd, after_start, "exchange_wait_lru_%d" % k) for k, hd in enumerate(recv_lru)]

    def shard_sum(recv, name):
        return _sum_blocks(recv.reshape(N_DEV, recv.shape[2], recv.shape[3]), name)

    gate_g = [[None, None], [None, None]]
    up_g = [[None, None], [None, None]]
    down_g = [[None, None], [None, None]]
    def ffn_sums(l, i):
        gt, ut, dn = [shard_sum(r, "sum_ffn") for r in recv_ffn[l][i]]
        gate_g[l][i], up_g[l][i], down_g[l][i] = gt.T, ut.T, dn

    for l, i in ((1, 1), (1, 0), (0, 1)):
        ffn_sums(l, i)
    grads["mix_ab_w_in"] = shard_sum(recv_ab[0], "sum_mix_in").T[None]
    grads["mix_ab_w_out"] = shard_sum(recv_ab[1], "sum_mix_out")[None]
    grads["lru_w_in"] = shard_sum(recv_lru[0], "sum_lru_in").T[None]
    grads["lru_w_out"] = shard_sum(recv_lru[1], "sum_lru_out")[None]
    rep_parts = [shard_sum(recv_lru[2], "sum_rep"), shard_sum(recv_lru[3], "sum_rep"), shard_sum(recv_ab[2], "sum_rep")]
    rep_names = ["lru_wa", "lru_wx", "pool_w"]

    dmod_all = _gather_wait(dmod_sent, rep_parts[2], "gather_wait_dmod").reshape(N_DEV, n_dm, 128)
    dmod_sum = _sum_blocks(dmod_all, "sum_dmod").reshape(2, 3, N_MOD * D)
    dmod_all = dmod_all.reshape(N_DEV, 2, 3, N_MOD * D)
    grads["b_mod"] = dmod_sum[:, 0] + dmod_sum[:, 1] + dmod_sum[:, 2]
    dmod_ex = jnp.moveaxis(dmod_all[:, :, 0:2, :], 1, 0).reshape(2, 2 * N_DEV, N_MOD * D)
    dm_rows = jnp.zeros((2, 32, N_MOD * D), F32).at[:, :16].set(dmod_ex).at[:, 16].set(dmod_sum[:, 2])
    dm_cols = lax.dynamic_slice_in_dim(dm_rows, me * mcols, mcols, axis=2).astype(BF16)
    grads["w_mod"] = jnp.stack([_matmul(s_rows, dm_cols[l], "tn", F32, "mod_dw", bn_cap=1280) for l in range(2)])
    ds_part = None
    for l in range(2):
        part = _matmul(dm_cols[l, 16:32], w_mod[l], "nt", F32, "mod_ds", bk_cap=1280)[0]
        ds_part = part if ds_part is None else ds_part + part

    dln_g_f = jnp.stack([jnp.stack(dln_g[l]) for l in range(2)])
    dln_b_f = jnp.stack([jnp.stack(dln_b[l]) for l in range(2)])
    sink_pad = jnp.zeros((1, 128), F32).at[0, :8].set(grads["attn_sink"][0])
    part_list = [p_.reshape(-1, 128) for p_ in rep_parts] + [
        dln_g_f.reshape(-1, 128), dln_b_f.reshape(-1, 128), grads["lru_conv_w"].reshape(-1, 128),
        grads["lru_conv_b"].reshape(-1, 128), grads["lru_ba"].reshape(-1, 128), grads["lru_bx"].reshape(-1, 128),
        grads["lru_lambda"].reshape(-1, 128), ds_part.reshape(-1, 128), sink_pad, grads["pool_scale"].reshape(-1, 128)]
    parts, part_off = _pack_rows(part_list)
    parts_sent = _gather_start(parts[None], "gather_start_partials")

    delta, new_m, new_v = {}, {}, {}
    for n in ("w_mod", "b_mod", "mix_ab_w_in", "mix_ab_w_out", "lru_w_in", "lru_w_out"):
        grads[n] = grads[n].reshape(weights[n].shape)
        delta[n], new_m[n], new_v[n] = _adamw(weights[n], grads[n], mom_m[n], mom_v[n], "adamw", dep=parts_sent[4])
    parts_all = _gather_wait(parts_sent, delta["lru_w_out"], "gather_wait_partials").reshape(N_DEV, parts.shape[0], 128)
    parts_sum = _sum_blocks(parts_all, "sum_partials")

    for i, n in enumerate(rep_names):
        rows = part_list[i].shape[0]
        grads[n] = parts_all[:, part_off[i]:part_off[i] + rows, :].reshape(weights[n].shape)

    def take(idx):
        return parts_sum[part_off[idx]:part_off[idx] + part_list[idx].shape[0]]

    def my_cols(full, shp):
        w = shp[-1]
        return lax.dynamic_slice_in_dim(full, me * w, w, axis=full.ndim - 1)

    grads["ln_g"] = my_cols(take(3).reshape(2, 3, D), ln_g.shape)
    grads["ln_b"] = my_cols(take(4).reshape(2, 3, D), ln_b.shape)
    grads["lru_conv_w"] = my_cols(take(5).reshape(1, 4, D), lru_conv_w.shape)
    grads["lru_conv_b"] = my_cols(take(6).reshape(1, D), lru_conv_b.shape)
    grads["lru_ba"] = my_cols(take(7).reshape(1, 2, D), lru_ba.shape)
    grads["lru_bx"] = my_cols(take(8).reshape(1, 2, D), lru_bx.shape)
    grads["lru_lambda"] = my_cols(take(9).reshape(1, 2, D), lru_lambda.shape)
    sg = jax.nn.sigmoid(c_ctx)
    grads["c_ctx"] = take(10).reshape(D) * (sg * (1.0 + c_ctx * (1.0 - sg)))
    grads["attn_sink"] = take(11)[:, :8]
    grads["pool_scale"] = take(12).reshape(pool_scale.shape)

    ffn_names = ("ffn_w_gate", "ffn_w_up", "ffn_w_down")
    last_delta = delta["lru_w_out"]
    for n in names:
        if n in delta or n in ffn_names:
            continue
        grads[n] = grads[n].reshape(weights[n].shape)
        delta[n], new_m[n], new_v[n] = _adamw(weights[n], grads[n], mom_m[n], mom_v[n], "adamw")
        last_delta = delta[n]
    recv_ffn[0][0] = [arrived(hd, last_delta, "exchange_wait_ffn00_%d" % k) for k, hd in enumerate(last_ffn)]
    ffn_sums(0, 0)
    grads["ffn_w_gate"] = jnp.stack([jnp.stack(gate_g[l]) for l in range(2)])
    grads["ffn_w_up"] = jnp.stack([jnp.stack(up_g[l]) for l in range(2)])
    grads["ffn_w_down"] = jnp.stack([jnp.stack(down_g[l]) for l in range(2)])
    for n in ffn_names:
        delta[n], new_m[n], new_v[n] = _adamw(weights[n], grads[n], mom_m[n], mom_v[n], "adamw")

    return (loss, grad_x, *[grads[n] for n in names], *[delta[n] for n in names],
            *[new_m[n] for n in names], *[new_v[n] for n in names])
```

```python
import functools
import math

import jax
import jax.numpy as jnp
from jax import lax
from jax.experimental import pallas as pl
from jax.experimental.pallas import tpu as pltpu

F32 = jnp.float32
BF16 = jnp.bfloat16
MESH = pl.DeviceIdType.MESH

D = 1024
N_MOD = 9
N_DEV = 8
HEAD_DIM = 64
ATT_HEADS = 8
KV_HEADS = 2
ATT_W = 512
BLK = 128
ATT_SCALE = HEAD_DIM ** -0.5
GRID_W = 64
ROPE_FREQS = HEAD_DIM // 4
ROPE_THETA = 10000.0
POOL_R = (1, 2, 4, 8)
LRU_C = 8.0
LN_EPS = 1e-5
NEG_INF = -1e30
ALPHA = 4.0 ** 0.25
LR, B1, B2, EPS, WD, STEP = 0.001, 0.9, 0.999, 1e-08, 0.01, 10
VMEM_LIMIT = 56 * 1024 * 1024
ROW_TILE = 512


def _params(sem=None):
    if sem is None:
        return pltpu.CompilerParams(vmem_limit_bytes=VMEM_LIMIT)
    return pltpu.CompilerParams(dimension_semantics=sem, vmem_limit_bytes=VMEM_LIMIT)


def _sigmoid(x):
    return 0.5 * jnp.tanh(0.5 * x) + 0.5


def _dot(a, b):
    return jnp.dot(a.astype(BF16), b.astype(BF16), preferred_element_type=F32)


def _dot_nt(a, b):
    return lax.dot_general(a.astype(BF16), b.astype(BF16), (((1,), (1,)), ((), ())), preferred_element_type=F32)


def _dot_tn(a, b):
    return lax.dot_general(a.astype(BF16), b.astype(BF16), (((0,), (0,)), ((), ())), preferred_element_type=F32)


def _pick(n, cap):
    best = None
    for m in range(128, min(n, cap) + 1, 128):
        if n % m == 0:
            best = m
    return n if best is None else best


def _chunks(width, step=256):
    out, c = [], 0
    while c < width:
        w = min(step, width - c)
        out.append((c, w))
        c += w
    return out


class _Cfg:
    def __init__(self, n_lat, n_ctx):
        self.n_lat, self.n_ctx = n_lat, n_ctx
        self.t_lat, self.t_ctx = 2 * n_lat, 2 * n_ctx
        self.T = self.t_lat + self.t_ctx
        self.tm = min(ROW_TILE, self.t_ctx)
        assert n_lat % self.tm == 0 and self.t_ctx % self.tm == 0 and n_lat >= 3 * BLK and n_ctx % BLK == 0
        self.nt = self.T // self.tm
        self.nlt = n_lat // self.tm
        self.ctx_blk = self.t_lat // n_ctx

    def seg(self, i):
        return jnp.minimum(i // self.nlt, 2)

    def first_of_seg(self, i):
        return jnp.where(i < 2 * self.nlt, i % self.nlt == 0, i == 2 * self.nlt)


def _modulate(cfg, h, mod, shift_idx, scale_idx, name):
    tm = cfg.tm

    def body(h_ref, mod_ref, o_ref):
        sh = mod_ref[shift_idx:shift_idx + 1, :]
        sc = mod_ref[scale_idx:scale_idx + 1, :]
        o_ref[...] = (h_ref[...] * (1.0 + sc) + sh).astype(BF16)

    return pl.pallas_call(
        body, grid=(cfg.nt,), name=name,
        in_specs=[pl.BlockSpec((tm, D), lambda i: (i, 0)),
                  pl.BlockSpec((None, N_MOD, D), lambda i: (cfg.seg(i), 0, 0))],
        out_specs=pl.BlockSpec((tm, D), lambda i: (i, 0)),
        out_shape=jax.ShapeDtypeStruct((cfg.T, D), BF16),
        compiler_params=_params(("parallel",)),
    )(h, mod)


def _ln_fwd(cfg, h, y, mod, gate_idx, coef, lng, lnb, mod_next, next_idx, name):
    tm = cfg.tm
    has_next = next_idx is not None

    def body(*refs):
        if has_next:
            h_ref, y_ref, mod_ref, g_ref, b_ref, modn_ref, hn_ref, xhat_ref, rstd_ref, xin_ref = refs
        else:
            h_ref, y_ref, mod_ref, g_ref, b_ref, hn_ref, xhat_ref, rstd_ref = refs
        gate = mod_ref[gate_idx:gate_idx + 1, :]
        z = ALPHA * h_ref[...] + (coef * gate) * y_ref[...].astype(F32)
        mu = jnp.mean(z, axis=-1, keepdims=True)
        zc = z - mu
        var = jnp.mean(zc * zc, axis=-1, keepdims=True)
        rstd = lax.rsqrt(var + LN_EPS)
        xhat = zc * rstd
        hn = xhat * g_ref[...] + b_ref[...]
        hn_ref[...] = hn
        xhat_ref[...] = xhat.astype(BF16)
        rstd_ref[...] = rstd
        if has_next:
            sh = modn_ref[next_idx[0]:next_idx[0] + 1, :]
            sc = modn_ref[next_idx[1]:next_idx[1] + 1, :]
            xin_ref[...] = (hn * (1.0 + sc) + sh).astype(BF16)

    row = pl.BlockSpec((tm, D), lambda i: (i, 0))
    modspec = pl.BlockSpec((None, N_MOD, D), lambda i: (cfg.seg(i), 0, 0))
    vec = pl.BlockSpec((1, D), lambda i: (0, 0))
    in_specs = [row, row, modspec, vec, vec]
    args = [h, y, mod, lng, lnb]
    out_specs = [row, row, pl.BlockSpec((tm, 1), lambda i: (i, 0))]
    out_shape = [jax.ShapeDtypeStruct((cfg.T, D), F32), jax.ShapeDtypeStruct((cfg.T, D), BF16),
                 jax.ShapeDtypeStruct((cfg.T, 1), F32)]
    if has_next:
        in_specs.append(modspec)
        args.append(mod_next)
        out_specs.append(row)
        out_shape.append(jax.ShapeDtypeStruct((cfg.T, D), BF16))
    return pl.pallas_call(body, grid=(cfg.nt,), name=name, in_specs=in_specs, out_specs=out_specs,
                          out_shape=out_shape, compiler_params=_params(("parallel",)))(*args)


def _ln_bwd(cfg, up, xhat, rstd, y, mod, gate_idx, coef, lng, name):
    tm = cfg.tm
    fused = len(up) > 1
    scale_next = up[4] if fused else None

    def body(*refs):
        if fused:
            dres_n, dxin_n, b_ref, modn_ref, xhat_ref, rstd_ref, y_ref, mod_ref, g_ref, dres_ref, dys_ref, st_ref = refs
        else:
            dhn_ref, xhat_ref, rstd_ref, y_ref, mod_ref, g_ref, dres_ref, dys_ref, st_ref = refs
        i = pl.program_id(0)

        @pl.when(cfg.first_of_seg(i))
        def _():
            st_ref[...] = jnp.zeros_like(st_ref)

        xhat = xhat_ref[...].astype(F32)
        if fused:
            dxin = dxin_n[...].astype(F32)
            sc = modn_ref[scale_next:scale_next + 1, :]
            dhn = dres_n[...] + dxin * (1.0 + sc)
            shift_sum = jnp.sum(dxin, axis=0, keepdims=True)
            st_ref[3:4, :] += g_ref[...] * jnp.sum(dxin * xhat, axis=0, keepdims=True) + b_ref[...] * shift_sum
            st_ref[4:5, :] += shift_sum
        else:
            dhn = dhn_ref[...]
        gdh = dhn * g_ref[...]
        m1 = jnp.mean(gdh, axis=-1, keepdims=True)
        m2 = jnp.mean(gdh * xhat, axis=-1, keepdims=True)
        dz = rstd_ref[...] * (gdh - m1 - xhat * m2)
        gate = mod_ref[gate_idx:gate_idx + 1, :]
        dres_ref[...] = ALPHA * dz
        dys_ref[...] = ((coef * gate) * dz).astype(BF16)
        st_ref[0:1, :] += jnp.sum(dhn * xhat, axis=0, keepdims=True)
        st_ref[1:2, :] += jnp.sum(dhn, axis=0, keepdims=True)
        st_ref[2:3, :] += jnp.sum((coef * dz) * y_ref[...].astype(F32), axis=0, keepdims=True)

    row = pl.BlockSpec((tm, D), lambda i: (i, 0))
    modspec = pl.BlockSpec((None, N_MOD, D), lambda i: (cfg.seg(i), 0, 0))
    vec = pl.BlockSpec((1, D), lambda i: (0, 0))
    col = pl.BlockSpec((tm, 1), lambda i: (i, 0))
    if fused:
        in_specs = [row, row, vec, modspec, row, col, row, modspec, vec]
        args = [up[0], up[1], up[2], up[3], xhat, rstd, y, mod, lng]
    else:
        in_specs = [row, row, col, row, modspec, vec]
        args = [up[0], xhat, rstd, y, mod, lng]
    return pl.pallas_call(
        body, grid=(cfg.nt,), name=name, in_specs=in_specs,
        out_specs=[row, row, pl.BlockSpec((None, 8, D), lambda i: (cfg.seg(i), 0, 0))],
        out_shape=[jax.ShapeDtypeStruct((cfg.T, D), F32), jax.ShapeDtypeStruct((cfg.T, D), BF16),
                   jax.ShapeDtypeStruct((3, 8, D), F32)],
        compiler_params=_params(("arbitrary",)))(*args)


def _modulate_bwd(cfg, dres, dxin, h, mod, scale_idx, name):
    tm = cfg.tm
    n_lt = 2 * cfg.nlt

    def body(dres_ref, dxin_ref, h_ref, mod_ref, dh_ref, st_ref):
        i = pl.program_id(0)

        @pl.when(cfg.first_of_seg(i))
        def _():
            st_ref[...] = jnp.zeros_like(st_ref)

        dxin = dxin_ref[...].astype(F32)
        sc = mod_ref[scale_idx:scale_idx + 1, :]

        @pl.when(i < n_lt)
        def _():
            dh_ref[...] = dres_ref[...] + dxin * (1.0 + sc)

        st_ref[3:4, :] += jnp.sum(dxin * h_ref[...], axis=0, keepdims=True)
        st_ref[4:5, :] += jnp.sum(dxin, axis=0, keepdims=True)

    row = pl.BlockSpec((tm, D), lambda i: (i, 0))
    return pl.pallas_call(
        body, grid=(cfg.nt,), name=name,
        in_specs=[row, row, row, pl.BlockSpec((None, N_MOD, D), lambda i: (cfg.seg(i), 0, 0))],
        out_specs=[pl.BlockSpec((tm, D), lambda i: (jnp.minimum(i, n_lt - 1), 0)),
                   pl.BlockSpec((None, 8, D), lambda i: (cfg.seg(i), 0, 0))],
        out_shape=[jax.ShapeDtypeStruct((cfg.t_lat, D), F32), jax.ShapeDtypeStruct((3, 8, D), F32)],
        compiler_params=_params(("arbitrary",)))(dres, dxin, h, mod)


def _loss(cfg, h, target, name):
    tm = cfg.tm
    n_lt = 2 * cfg.nlt

    def body(h_ref, t_ref, dy_ref, l_ref):
        i = pl.program_id(0)

        @pl.when(i == 0)
        def _():
            l_ref[...] = jnp.zeros_like(l_ref)

        @pl.when(i < n_lt)
        def _():
            err = h_ref[...] - t_ref[...]
            dy_ref[...] = err * (1.0 / D)
            part = jnp.sum(jnp.sum(err * err, axis=1, keepdims=True), axis=0, keepdims=True) * (0.5 / D)
            l_ref[...] += jnp.broadcast_to(part, l_ref.shape)

        @pl.when(i >= n_lt)
        def _():
            dy_ref[...] = jnp.zeros_like(dy_ref)

    return pl.pallas_call(
        body, grid=(cfg.nt,), name=name,
        in_specs=[pl.BlockSpec((tm, D), lambda i: (i, 0)),
                  pl.BlockSpec((tm, D), lambda i: (jnp.minimum(i, n_lt - 1), 0))],
        out_specs=[pl.BlockSpec((tm, D), lambda i: (i, 0)), pl.BlockSpec((8, 128), lambda i: (0, 0))],
        out_shape=[jax.ShapeDtypeStruct((cfg.T, D), F32), jax.ShapeDtypeStruct((8, 128), F32)],
        compiler_params=_params(("arbitrary",)))(h, target)


def _matmul(a, b, mode, out_dtype, name, bm_cap=1536, bn_cap=1408, bk_cap=1024):
    if mode == "nn":
        (M, K), N = a.shape, b.shape[1]
    elif mode == "nt":
        (M, K), N = a.shape, b.shape[0]
    else:
        (K, M), N = a.shape, b.shape[1]
    bm, bn, bk = _pick(M, bm_cap), _pick(N, bn_cap), _pick(K, bk_cap)
    nk = K // bk

    def body(a_ref, b_ref, o_ref, acc_ref=None):
        k = pl.program_id(2)
        if mode == "nn":
            part = _dot(a_ref[...], b_ref[...])
        elif mode == "nt":
            part = _dot_nt(a_ref[...], b_ref[...])
        else:
            part = _dot_tn(a_ref[...], b_ref[...])
        if nk == 1:
            o_ref[...] = part.astype(out_dtype)
            return

        @pl.when(k == 0)
        def _():
            acc_ref[...] = part

        @pl.when((k > 0) & (k < nk - 1))
        def _():
            acc_ref[...] += part

        @pl.when(k == nk - 1)
        def _():
            o_ref[...] = (acc_ref[...] + part).astype(out_dtype)

    if mode == "nn":
        a_spec = pl.BlockSpec((bm, bk), lambda i, j, k: (i, k))
        b_spec = pl.BlockSpec((bk, bn), lambda i, j, k: (k, j))
    elif mode == "nt":
        a_spec = pl.BlockSpec((bm, bk), lambda i, j, k: (i, k))
        b_spec = pl.BlockSpec((bn, bk), lambda i, j, k: (j, k))
    else:
        a_spec = pl.BlockSpec((bk, bm), lambda i, j, k: (k, i))
        b_spec = pl.BlockSpec((bk, bn), lambda i, j, k: (k, j))
    return pl.pallas_call(
        body, grid=(M // bm, N // bn, nk), name=name, in_specs=[a_spec, b_spec],
        out_specs=pl.BlockSpec((bm, bn), lambda i, j, k: (i, j)),
        out_shape=jax.ShapeDtypeStruct((M, N), out_dtype),
        scratch_shapes=[pltpu.VMEM((bm, bn), F32)] if nk > 1 else [],
        compiler_params=_params(("parallel", "parallel", "arbitrary")))(a, b)


def _ffn_tile(T, cap):
    best = 256
    for t in range(256, cap + 1, 256):
        if T % t == 0:
            best = t
    return best


def _ffn_fwd(xin, wf, name):
    T = xin.shape[0]
    F = wf.shape[1]
    tm, tf = _ffn_tile(T, 768), F // 2
    assert tf % 128 == 0 and T % tm == 0

    def body(x_ref, wg_ref, wu_ref, wd_ref, g_ref, u_ref, y_ref, acc_ref):
        j = pl.program_id(1)
        x = x_ref[...]
        acc = None
        for c0, cw in _chunks(tf):
            g = _dot_nt(x, wg_ref[c0:c0 + cw, :])
            u = _dot_nt(x, wu_ref[c0:c0 + cw, :])
            g_ref[:, c0:c0 + cw] = g.astype(BF16)
            u_ref[:, c0:c0 + cw] = u.astype(BF16)
            part = _dot(g * _sigmoid(g) * u, wd_ref[c0:c0 + cw, :])
            acc = part if acc is None else acc + part

        @pl.when(j == 0)
        def _():
            acc_ref[...] = acc

        @pl.when(j == 1)
        def _():
            y_ref[...] = (acc_ref[...] + acc).astype(BF16)

    return pl.pallas_call(
        body, grid=(T // tm, 2), name=name,
        in_specs=[pl.BlockSpec((tm, D), lambda i, j: (i, 0)),
                  pl.BlockSpec((None, tf, D), lambda i, j: (0, j, 0)),
                  pl.BlockSpec((None, tf, D), lambda i, j: (1, j, 0)),
                  pl.BlockSpec((None, tf, D), lambda i, j: (2, j, 0))],
        out_specs=[pl.BlockSpec((tm, tf), lambda i, j: (i, j)),
                   pl.BlockSpec((tm, tf), lambda i, j: (i, j)),
                   pl.BlockSpec((tm, D), lambda i, j: (i, 0))],
        out_shape=[jax.ShapeDtypeStruct((T, F), BF16), jax.ShapeDtypeStruct((T, F), BF16),
                   jax.ShapeDtypeStruct((T, D), BF16)],
        scratch_shapes=[pltpu.VMEM((tm, D), F32)],
        compiler_params=_params(("parallel", "arbitrary")))(xin, wf, wf, wf)


def _ffn_bwd(dys, g, u, wf, name):
    T = dys.shape[0]
    F = wf.shape[1]
    tm, tf = _ffn_tile(T, 512), F // 2

    def body(dy_ref, g_ref, u_ref, wg_ref, wu_ref, wd_ref, dg_ref, du_ref, a_ref, dx_ref, acc_ref):
        j = pl.program_id(1)
        da_all = _dot_nt(dy_ref[...], wd_ref[...])
        for c0, cw in _chunks(tf):
            gg = g_ref[:, c0:c0 + cw].astype(F32)
            uu = u_ref[:, c0:c0 + cw].astype(F32)
            da = da_all[:, c0:c0 + cw]
            s = _sigmoid(gg)
            silu = gg * s
            a_ref[:, c0:c0 + cw] = (silu * uu).astype(BF16)
            du_ref[:, c0:c0 + cw] = (da * silu).astype(BF16)
            dg_ref[:, c0:c0 + cw] = (da * uu * (s * (1.0 + gg * (1.0 - s)))).astype(BF16)
        acc = _dot(dg_ref[...], wg_ref[...]) + _dot(du_ref[...], wu_ref[...])

        @pl.when(j == 0)
        def _():
            acc_ref[...] = acc

        @pl.when(j == 1)
        def _():
            dx_ref[...] = (acc_ref[...] + acc).astype(BF16)

    blk = pl.BlockSpec((tm, tf), lambda i, j: (i, j))
    return pl.pallas_call(
        body, grid=(T // tm, 2), name=name,
        in_specs=[pl.BlockSpec((tm, D), lambda i, j: (i, 0)), blk, blk,
                  pl.BlockSpec((None, tf, D), lambda i, j: (0, j, 0)),
                  pl.BlockSpec((None, tf, D), lambda i, j: (1, j, 0)),
                  pl.BlockSpec((None, tf, D), lambda i, j: (2, j, 0))],
        out_specs=[blk, blk, blk, pl.BlockSpec((tm, D), lambda i, j: (i, 0))],
        out_shape=[jax.ShapeDtypeStruct((T, F), BF16), jax.ShapeDtypeStruct((T, F), BF16),
                   jax.ShapeDtypeStruct((T, F), BF16), jax.ShapeDtypeStruct((T, D), BF16)],
        scratch_shapes=[pltpu.VMEM((tm, D), F32)],
        compiler_params=_params(("parallel", "arbitrary")))(dys, g, u, wf, wf, wf)


def _swap_halves(x):
    w = x.shape[1]
    lane = lax.broadcasted_iota(jnp.int32, (1, w), 1)
    return jnp.where((lane & 63) < 32, pltpu.roll(x, w - 32, 1), pltpu.roll(x, 32, 1))


def _rope(x, cos, sin):
    return x * cos + _swap_halves(x) * sin


def _rope_t(dy, cos, sin):
    return dy * cos + _swap_halves(dy * sin)


def _rope_tables(n_lat):
    rows = n_lat // GRID_W
    row = jnp.repeat(jnp.arange(rows, dtype=F32), GRID_W)
    col = jnp.tile(jnp.arange(GRID_W, dtype=F32), rows)
    inv = ROPE_THETA ** (-jnp.arange(ROPE_FREQS, dtype=F32) / ROPE_FREQS)
    ang = jnp.concatenate([row[:, None] * inv, col[:, None] * inv], axis=-1)
    cs, sn = jnp.cos(ang), jnp.sin(ang)
    cos = jnp.concatenate([cs, cs, cs, cs], axis=-1)
    sin = jnp.concatenate([-sn, sn, -sn, sn], axis=-1)
    return cos, sin


def _attn_specs(cfg):
    n_lat, n_ctx, cb = cfg.n_lat, cfg.n_ctx, cfg.ctx_blk
    return [pl.BlockSpec((n_lat, ATT_W), lambda e: (e, 0)),
            pl.BlockSpec((n_lat, 128), lambda e: (e, 4)),
            pl.BlockSpec((n_lat, 128), lambda e: (e, 5)),
            pl.BlockSpec((n_ctx, ATT_W), lambda e: (cb + e, 0)),
            pl.BlockSpec((n_ctx, 128), lambda e: (cb + e, 4)),
            pl.BlockSpec((n_ctx, 128), lambda e: (cb + e, 5)),
            pl.BlockSpec((n_lat, 128), lambda e: (0, 0)),
            pl.BlockSpec((n_lat, 128), lambda e: (0, 0)),
            pl.BlockSpec((8, 128), lambda e: (0, 0))]


def _attn_prepare(kh, kl, vl, kc, vc, ka, kb, va, vb, kca, kcb, vca, vcb):
    lane = lax.broadcasted_iota(jnp.int32, (1, 128), 1)
    own = (lane < 64) if kh == 0 else (lane >= 64)

    def split(x, ra, rb):
        mine = jnp.where(own, x, 0.0)
        other = pltpu.roll(mine, 64, 1)
        a, b = (mine, other) if kh == 0 else (other, mine)
        ra[...] = a.astype(BF16)
        rb[...] = b.astype(BF16)

    split(kl, ka, kb)
    split(vl, va, vb)
    split(kc, kca, kcb)
    split(vc, vca, vcb)


def _softmax_parts(s_list, sk):
    m = sk
    for s in s_list:
        m = jnp.maximum(m, jnp.max(s, axis=1, keepdims=True))
    es = [jnp.exp(s - m) for s in s_list]
    esk = jnp.exp(sk - m)
    den = esk
    for e in es:
        den = den + jnp.sum(e, axis=1, keepdims=True)
    inv = 1.0 / den
    return [e * inv for e in es], esk * inv


def _window(cfg, n):
    r0 = pl.multiple_of(n * BLK, BLK)
    start = pl.multiple_of(jnp.clip((n - 1) * BLK, 0, cfg.n_lat - 3 * BLK), BLK)
    qpos = r0 + lax.broadcasted_iota(jnp.int32, (BLK, 1), 0)
    kpos = start + lax.broadcasted_iota(jnp.int32, (1, 3 * BLK), 1)
    valid = jnp.abs(qpos - kpos) <= BLK
    return r0, start, valid


def _attn_fwd(cfg, p, cos, sin, sink_rows, name):
    n_lat, n_ctx = cfg.n_lat, cfg.n_ctx

    def body(q_ref, k_ref, v_ref, qc_ref, kc_ref, vc_ref, cos_ref, sin_ref, sink_ref, o_ref, oc_ref,
             qr, ka, kb, va, vb, kca, kcb, vca, vcb):
        cos_t, sin_t = cos_ref[...], sin_ref[...]
        for gq in range(4):
            qr[:, gq * 128:(gq + 1) * 128] = _rope(q_ref[:, gq * 128:(gq + 1) * 128].astype(F32), cos_t, sin_t).astype(BF16)
        kl = _rope(k_ref[...].astype(F32), cos_t, sin_t)
        for kh in range(KV_HEADS):
            _attn_prepare(kh, kl, v_ref[...].astype(F32), kc_ref[...].astype(F32), vc_ref[...].astype(F32),
                          ka, kb, va, vb, kca, kcb, vca, vcb)

            def lat_block(n, carry):
                r0, start, valid = _window(cfg, n)
                win = pl.ds(start, 3 * BLK)
                lanes = [slice((kh * 2 + pr) * 128, (kh * 2 + pr + 1) * 128) for pr in range(2)]
                qps = [qr[pl.ds(r0, BLK), lanes[pr]] for pr in range(2)]
                kws, kcs = (ka[win, :], kb[win, :]), (kca[...], kcb[...])
                scores = [(jnp.where(valid, _dot_nt(qps[pr], kws[half]) * ATT_SCALE, NEG_INF),
                           _dot_nt(qps[pr], kcs[half]) * ATT_SCALE) for pr in range(2) for half in range(2)]
                probs = []
                for idx, (s_w, s_c) in enumerate(scores):
                    head = kh * 4 + idx
                    (p_w, p_c), _ = _softmax_parts([s_w, s_c], sink_ref[head:head + 1, 0:1])
                    probs.append((p_w.astype(BF16), p_c.astype(BF16)))
                vws, vcs = (va[win, :], vb[win, :]), (vca[...], vcb[...])
                for pr in range(2):
                    o = (_dot(probs[2 * pr][0], vws[0]) + _dot(probs[2 * pr][1], vcs[0])
                         + _dot(probs[2 * pr + 1][0], vws[1]) + _dot(probs[2 * pr + 1][1], vcs[1]))
                    o_ref[pl.ds(r0, BLK), lanes[pr]] = o.astype(BF16)
                return carry

            lax.fori_loop(0, n_lat // BLK, lat_block, 0, unroll=2)
            for n in range(n_ctx // BLK):
                rows = slice(n * BLK, (n + 1) * BLK)
                for pr in range(2):
                    lanes = slice((kh * 2 + pr) * 128, (kh * 2 + pr + 1) * 128)
                    qp = qc_ref[rows, lanes]
                    o = None
                    for half, (kcx, vcx) in enumerate(((kca, vca), (kcb, vcb))):
                        head = kh * 4 + pr * 2 + half
                        s_c = _dot_nt(qp, kcx[...]) * ATT_SCALE
                        (p_c,), _ = _softmax_parts([s_c], sink_ref[head:head + 1, 0:1])
                        part = _dot(p_c, vcx[...])
                        o = part if o is None else o + part
                    oc_ref[rows, lanes] = o.astype(BF16)

    return pl.pallas_call(
        body, grid=(2,), name=name, in_specs=_attn_specs(cfg),
        out_specs=[pl.BlockSpec((n_lat, ATT_W), lambda e: (e, 0)), pl.BlockSpec((n_ctx, ATT_W), lambda e: (e, 0))],
        out_shape=[jax.ShapeDtypeStruct((cfg.t_lat, ATT_W), BF16), jax.ShapeDtypeStruct((cfg.t_ctx, ATT_W), BF16)],
        scratch_shapes=[pltpu.VMEM((n_lat, ATT_W), BF16)] + [pltpu.VMEM((n_lat, 128), BF16)] * 4
        + [pltpu.VMEM((n_ctx, 128), BF16)] * 4,
        compiler_params=_params(("parallel",)))(p, p, p, p, p, p, cos, sin, sink_rows)


def _attn_bwd(cfg, p, dcat, cos, sin, sink_rows, name):
    n_lat, n_ctx, cb = cfg.n_lat, cfg.n_ctx, cfg.ctx_blk

    def body(q_ref, k_ref, v_ref, qc_ref, kc_ref, vc_ref, cos_ref, sin_ref, sink_ref, do_ref, doc_ref,
             dq_ref, dk_ref, dv_ref, dqc_ref, dkc_ref, dvc_ref, dsink_ref,
             qr, ka, kb, va, vb, kca, kcb, vca, vcb, dqs, dka, dva, dkca, dvca):
        cos_t, sin_t = cos_ref[...], sin_ref[...]
        lane = lax.broadcasted_iota(jnp.int32, (1, 128), 1)
        lo = lane < 64
        for gq in range(4):
            qr[:, gq * 128:(gq + 1) * 128] = _rope(q_ref[:, gq * 128:(gq + 1) * 128].astype(F32), cos_t, sin_t).astype(BF16)
        kl = _rope(k_ref[...].astype(F32), cos_t, sin_t)
        dsink_ref[...] = jnp.zeros_like(dsink_ref)
        dka[...] = jnp.zeros_like(dka)
        dva[...] = jnp.zeros_like(dva)
        dkca[...] = jnp.zeros_like(dkca)
        dvca[...] = jnp.zeros_like(dvca)

        def halves(x):
            return jnp.where(lo, x, 0).astype(BF16), jnp.where(lo, 0, x).astype(BF16)

        for kh in range(KV_HEADS):
            _attn_prepare(kh, kl, v_ref[...].astype(F32), kc_ref[...].astype(F32), vc_ref[...].astype(F32),
                          ka, kb, va, vb, kca, kcb, vca, vcb)

            def one_head(head, qp, q_half, do_p, do_half, kw, kcx, vw, vcx, win, valid):
                sk = sink_ref[head:head + 1, 0:1]
                s_list = [_dot_nt(qp, kcx[...]) * ATT_SCALE]
                if win is not None:
                    s_list.insert(0, jnp.where(valid, _dot_nt(qp, kw[win, :]) * ATT_SCALE, NEG_INF))
                probs, p_sink = _softmax_parts(s_list, sk)
                vals = [vcx[...]] if win is None else [vw[win, :], vcx[...]]
                dps = [_dot_nt(do_p, vv) for vv in vals]
                dr = None
                for pp, dp in zip(probs, dps):
                    t = jnp.sum(pp * dp, axis=1, keepdims=True)
                    dr = t if dr is None else dr + t
                dss = [(pp * (dp - dr) * ATT_SCALE).astype(BF16) for pp, dp in zip(probs, dps)]
                dsink_ref[head:head + 1, :] += jnp.broadcast_to(
                    jnp.sum(-p_sink * dr, axis=0, keepdims=True), (1, 128))
                p_c, ds_c = probs[-1], dss[-1]
                dq = _dot(ds_c, kcx[...])
                dkca[kh] += _dot_tn(ds_c, q_half)
                dvca[kh] += _dot_tn(p_c, do_half)
                if win is not None:
                    dq = dq + _dot(dss[0], kw[win, :])
                    dka[kh, win, :] += _dot_tn(dss[0], q_half)
                    dva[kh, win, :] += _dot_tn(probs[0], do_half)
                return dq

            def lat_block(n, carry):
                r0, start, valid = _window(cfg, n)
                win = pl.ds(start, 3 * BLK)
                lanes = [slice((kh * 2 + pr) * 128, (kh * 2 + pr + 1) * 128) for pr in range(2)]
                qps = [qr[pl.ds(r0, BLK), lanes[pr]] for pr in range(2)]
                dops = [do_ref[pl.ds(r0, BLK), lanes[pr]].astype(BF16) for pr in range(2)]
                heads = [(pr, half) for pr in range(2) for half in range(2)]
                kws, kcs = (ka[win, :], kb[win, :]), (kca[...], kcb[...])
                vws, vcs = (va[win, :], vb[win, :]), (vca[...], vcb[...])
                soft = []
                for idx, (pr, half) in enumerate(heads):
                    s_w = jnp.where(valid, _dot_nt(qps[pr], kws[half]) * ATT_SCALE, NEG_INF)
                    s_c = _dot_nt(qps[pr], kcs[half]) * ATT_SCALE
                    soft.append(_softmax_parts([s_w, s_c], sink_ref[kh * 4 + idx:kh * 4 + idx + 1, 0:1]))
                dps = [(_dot_nt(dops[pr], vws[half]), _dot_nt(dops[pr], vcs[half])) for pr, half in heads]
                ds_w, ds_c, pb_w, pb_c = [], [], [], []
                for idx in range(4):
                    (p_w, p_c), p_sink = soft[idx]
                    dp_w, dp_c = dps[idx]
                    dr = jnp.sum(p_w * dp_w, axis=1, keepdims=True) + jnp.sum(p_c * dp_c, axis=1, keepdims=True)
                    ds_w.append((p_w * (dp_w - dr) * ATT_SCALE).astype(BF16))
                    ds_c.append((p_c * (dp_c - dr) * ATT_SCALE).astype(BF16))
                    pb_w.append(p_w.astype(BF16))
                    pb_c.append(p_c.astype(BF16))
                    head = kh * 4 + idx
                    dsink_ref[head:head + 1, :] += jnp.broadcast_to(
                        jnp.sum(-p_sink * dr, axis=0, keepdims=True), (1, 128))
                for pr in range(2):
                    dqs[pl.ds(r0, BLK), lanes[pr]] = (
                        _dot(ds_w[2 * pr], kws[0]) + _dot(ds_c[2 * pr], kcs[0])
                        + _dot(ds_w[2 * pr + 1], kws[1]) + _dot(ds_c[2 * pr + 1], kcs[1]))
                q_hs, do_hs = [halves(qp) for qp in qps], [halves(do_p) for do_p in dops]
                q_stack = jnp.concatenate([q_hs[pr][half] for pr, half in heads], axis=0)
                do_stack = jnp.concatenate([do_hs[pr][half] for pr, half in heads], axis=0)
                dka[kh, win, :] += _dot_tn(jnp.concatenate(ds_w, axis=0), q_stack)
                dva[kh, win, :] += _dot_tn(jnp.concatenate(pb_w, axis=0), do_stack)
                dkca[kh] += _dot_tn(jnp.concatenate(ds_c, axis=0), q_stack)
                dvca[kh] += _dot_tn(jnp.concatenate(pb_c, axis=0), do_stack)
                return carry

            lax.fori_loop(0, n_lat // BLK, lat_block, 0, unroll=2)
            for n in range(n_ctx // BLK):
                rows = slice(n * BLK, (n + 1) * BLK)
                for pr in range(2):
                    lanes = slice((kh * 2 + pr) * 128, (kh * 2 + pr + 1) * 128)
                    qp = qc_ref[rows, lanes].astype(BF16)
                    do_p = doc_ref[rows, lanes]
                    q_h, do_h = halves(qp), halves(do_p)
                    dq = None
                    for half, (kcx, vcx) in enumerate(((kca, vca), (kcb, vcb))):
                        part = one_head(kh * 4 + pr * 2 + half, qp, q_h[half], do_p, do_h[half],
                                        None, kcx, None, vcx, None, None)
                        dq = part if dq is None else dq + part
                    dqc_ref[rows, lanes] = dq.astype(BF16)

        def fold(acc):
            r0 = acc[0] + pltpu.roll(acc[0], 64, 1)
            r1 = acc[1] + pltpu.roll(acc[1], 64, 1)
            return jnp.where(lo, r0, r1)

        for gq in range(4):
            sl = slice(gq * 128, (gq + 1) * 128)
            dq_ref[:, sl] = _rope_t(dqs[:, sl], cos_t, sin_t).astype(BF16)
        dk_ref[...] = _rope_t(fold(dka), cos_t, sin_t).astype(BF16)
        dv_ref[...] = fold(dva).astype(BF16)
        dkc_ref[...] = fold(dkca).astype(BF16)
        dvc_ref[...] = fold(dvca).astype(BF16)

    lat = lambda w: pl.BlockSpec((n_lat, w), lambda e: (e, 0))
    ctx = lambda w: pl.BlockSpec((n_ctx, w), lambda e: (e, 0))
    sd = jax.ShapeDtypeStruct
    return pl.pallas_call(
        body, grid=(2,), name=name,
        in_specs=_attn_specs(cfg) + [pl.BlockSpec((n_lat, ATT_W), lambda e: (e, 0)),
                                     pl.BlockSpec((n_ctx, ATT_W), lambda e: (cb + e, 0))],
        out_specs=[lat(ATT_W), lat(128), lat(128), ctx(ATT_W), ctx(128), ctx(128),
                   pl.BlockSpec((None, 8, 128), lambda e: (e, 0, 0))],
        out_shape=[sd((cfg.t_lat, ATT_W), BF16), sd((cfg.t_lat, 128), BF16), sd((cfg.t_lat, 128), BF16),
                   sd((cfg.t_ctx, ATT_W), BF16), sd((cfg.t_ctx, 128), BF16), sd((cfg.t_ctx, 128), BF16),
                   sd((2, 8, 128), F32)],
        scratch_shapes=[pltpu.VMEM((n_lat, ATT_W), BF16)] + [pltpu.VMEM((n_lat, 128), BF16)] * 4
        + [pltpu.VMEM((n_ctx, 128), BF16)] * 4
        + [pltpu.VMEM((n_lat, ATT_W), F32), pltpu.VMEM((2, n_lat, 128), F32), pltpu.VMEM((2, n_lat, 128), F32),
           pltpu.VMEM((2, n_ctx, 128), F32), pltpu.VMEM((2, n_ctx, 128), F32)],
        compiler_params=_params(("parallel",)))(p, p, p, p, p, p, cos, sin, sink_rows, dcat, dcat)


def _shift_down(x, k, row):
    return jnp.where(row >= k, pltpu.roll(x, k, 0), 0.0)


def _shift_up(x, k, row):
    n = x.shape[0]
    return jnp.where(row < n - k, pltpu.roll(x, n - k, 0), 0.0)


def _window_sum(x, r, row):
    below, above, k = x, x, 1
    while k < r:
        below = below + _shift_down(below, k, row)
        above = above + _shift_up(above, k, row)
        k *= 2
    return below + _shift_down(x, r, row) + _shift_up(above, 1, row)


def _inv_count(r, row, n):
    cnt = jnp.minimum(row + r, n - 1) + 1 - jnp.maximum(row - r, 0)
    return 1.0 / cnt.astype(F32)


def _pool_fwd(p, w, scale, n, blk0, n_seg, name):
    def body(u0, u1, u2, u3, w_ref, sc_ref, o_ref):
        row = lax.broadcasted_iota(jnp.int32, (n, 1), 0)
        for g, u_ref in enumerate((u0, u1, u2, u3)):
            u = u_ref[...].astype(F32)
            d = _window_sum(u, POOL_R[g], row) * _inv_count(POOL_R[g], row, n) - u
            o_ref[:, g * 128:(g + 1) * 128] = (_dot(d, w_ref[g]) * sc_ref[:, g * 128:(g + 1) * 128]).astype(BF16)

    return pl.pallas_call(
        body, grid=(n_seg,), name=name,
        in_specs=[pl.BlockSpec((n, 128), functools.partial(lambda g, e: (blk0 + e, 6 + g), g)) for g in range(4)]
        + [pl.BlockSpec((4, 128, 128), lambda e: (0, 0, 0)), pl.BlockSpec((1, 512), lambda e: (0, 0))],
        out_specs=pl.BlockSpec((n, 512), lambda e: (e, 0)),
        out_shape=jax.ShapeDtypeStruct((n_seg * n, 512), BF16),
        compiler_params=_params(("parallel",)))(p, p, p, p, w, scale)


def _pool_bwd(p, w, scale, dcat, n, blk0, n_seg, name):
    def body(u0, u1, u2, u3, w_ref, sc_ref, dp_ref, du_ref, dw_ref, dsc_ref):
        e = pl.program_id(0)

        @pl.when(e == 0)
        def _():
            dw_ref[...] = jnp.zeros_like(dw_ref)
            dsc_ref[...] = jnp.zeros_like(dsc_ref)

        row = lax.broadcasted_iota(jnp.int32, (n, 1), 0)
        for g, u_ref in enumerate((u0, u1, u2, u3)):
            sl = slice(g * 128, (g + 1) * 128)
            u = u_ref[...].astype(F32)
            inv = _inv_count(POOL_R[g], row, n)
            d = _window_sum(u, POOL_R[g], row) * inv - u
            dp = dp_ref[:, sl].astype(F32)
            dsc_ref[:, sl] += jnp.sum(dp * _dot(d, w_ref[g]), axis=0, keepdims=True)
            dyp = dp * sc_ref[:, sl]
            dw_ref[g] += _dot_tn(d, dyp)
            dd = _dot_nt(dyp, w_ref[g])
            du_ref[:, sl] = (_window_sum(dd * inv, POOL_R[g], row) - dd).astype(BF16)

    return pl.pallas_call(
        body, grid=(n_seg,), name=name,
        in_specs=[pl.BlockSpec((n, 128), functools.partial(lambda g, e: (blk0 + e, 6 + g), g)) for g in range(4)]
        + [pl.BlockSpec((4, 128, 128), lambda e: (0, 0, 0)), pl.BlockSpec((1, 512), lambda e: (0, 0)),
           pl.BlockSpec((n, 512), lambda e: (blk0 + e, 1))],
        out_specs=[pl.BlockSpec((n, 512), lambda e: (e, 0)),
                   pl.BlockSpec((4, 128, 128), lambda e: (0, 0, 0)), pl.BlockSpec((1, 512), lambda e: (0, 0))],
        out_shape=[jax.ShapeDtypeStruct((n_seg * n, 512), BF16), jax.ShapeDtypeStruct((4, 128, 128), F32),
                   jax.ShapeDtypeStruct((1, 512), F32)],
        compiler_params=_params(("arbitrary",)))(p, p, p, p, w, scale, dcat)


def _gelu(x):
    t = jnp.tanh(math.sqrt(2.0 / math.pi) * (x + 0.044715 * x * x * x))
    return 0.5 * x * (1.0 + t), t


def _gelu_grad(x, t):
    return 0.5 * (1.0 + t) + 0.5 * x * (1.0 - t * t) * (math.sqrt(2.0 / math.pi) * (1.0 + 3 * 0.044715 * x * x))


def _neg_expm1_twice(x):
    t = jnp.tanh(x)
    return (-2.0 * t) / (1.0 - t)


def _softplus_neg(lam):
    x = -lam
    e = jnp.exp(-jnp.abs(x))
    log1p = jnp.where(e < 1e-2, e * (1.0 - e * (0.5 - e * (1.0 / 3.0))), jnp.log(1.0 + e))
    return jnp.maximum(x, 0.0) + log1p, -_sigmoid(x)


def _conv(u, w_ref, b_ref, row):
    return (b_ref[...] + _shift_down(u, 1, row) * w_ref[0:1, :] + u * w_ref[1:2, :]
            + _shift_up(u, 1, row) * w_ref[2:3, :] + _shift_up(u, 2, row) * w_ref[3:4, :])


def _lru_gates(uc, d, wa_ref, ba_ref, wx_ref, bx_ref, lam_ref):
    r = _sigmoid(_dot(uc, wa_ref[d]) + ba_ref[d:d + 1, :])
    gi = _sigmoid(_dot(uc, wx_ref[d]) + bx_ref[d:d + 1, :])
    sp, dsp = _softplus_neg(lam_ref[d:d + 1, :])
    la = (-LRU_C) * r * sp
    a = jnp.exp(la)
    sq = jnp.sqrt(_neg_expm1_twice(la))
    return r, gi, sp, dsp, a, sq


def _tile_scan(a_ref, b_ref, n, reverse):
    m = n // 8
    first = 7 if reverse else 0
    a_prev = a_ref[pl.ds(first, m, stride=8), :]
    b_prev = b_ref[pl.ds(first, m, stride=8), :]
    for j in (range(6, -1, -1) if reverse else range(1, 8)):
        rows = pl.ds(j, m, stride=8)
        aj = a_ref[rows, :]
        b_prev = aj * b_prev + b_ref[rows, :]
        a_prev = aj * a_prev
        b_ref[rows, :] = b_prev
        a_ref[rows, :] = a_prev


def _carry_scan(a_ref, b_ref, n, reverse, carry):
    nt8 = n // 8

    def step(i, c):
        t = (nt8 - 1 - i) if reverse else i
        off = pl.multiple_of(t * 8, 8)
        h = a_ref[pl.ds(off, 8), :] * c + b_ref[pl.ds(off, 8), :]
        b_ref[pl.ds(off, 8), :] = h
        return h[0:1, :] if reverse else h[7:8, :]

    return lax.fori_loop(0, nt8, step, carry, unroll=4)


def _chain_scan(segs, reverse):
    carry = jnp.zeros((1, 128), F32)
    for a, b, a_ref, b_ref, n in segs:
        a_ref[...] = a
        b_ref[...] = b
        _tile_scan(a_ref, b_ref, n, reverse)
        carry = _carry_scan(a_ref, b_ref, n, reverse, carry)


def _lru_specs(cfg):
    n_lat, n_ctx, cb = cfg.n_lat, cfg.n_ctx, cfg.ctx_blk
    return [pl.BlockSpec((n_lat, 128), lambda hb, e: (e, hb)),
            pl.BlockSpec((n_lat, 128), lambda hb, e: (e, 8 + hb)),
            pl.BlockSpec((n_ctx, 128), lambda hb, e: (cb + e, hb)),
            pl.BlockSpec((n_ctx, 128), lambda hb, e: (cb + e, 8 + hb)),
            pl.BlockSpec((4, 128), lambda hb, e: (0, hb)),
            pl.BlockSpec((1, 128), lambda hb, e: (0, hb)),
            pl.BlockSpec((2, None, 128, 128), lambda hb, e: (0, hb, 0, 0)),
            pl.BlockSpec((2, 128), lambda hb, e: (0, hb)),
            pl.BlockSpec((2, None, 128, 128), lambda hb, e: (0, hb, 0, 0)),
            pl.BlockSpec((2, 128), lambda hb, e: (0, hb)),
            pl.BlockSpec((2, 128), lambda hb, e: (0, hb))]


def _lru_fwd(cfg, p, consts, name):
    n_lat, n_ctx = cfg.n_lat, cfg.n_ctx

    def body(gl_ref, ul_ref, gc_ref, uc_ref, cw_ref, cb_ref, wa_ref, ba_ref, wx_ref, bx_ref, lam_ref,
             zl_ref, zc_ref, hl_ref, hc_ref, al, ac):
        row_l = lax.broadcasted_iota(jnp.int32, (n_lat, 1), 0)
        row_c = lax.broadcasted_iota(jnp.int32, (n_ctx, 1), 0)
        uc_l = _conv(ul_ref[...].astype(F32), cw_ref, cb_ref, row_l)
        uc_c = _conv(uc_ref[...].astype(F32), cw_ref, cb_ref, row_c)
        for d in range(2):
            _, gi_l, _, _, a_l, sq_l = _lru_gates(uc_l, d, wa_ref, ba_ref, wx_ref, bx_ref, lam_ref)
            _, gi_c, _, _, a_c, sq_c = _lru_gates(uc_c, d, wa_ref, ba_ref, wx_ref, bx_ref, lam_ref)
            _chain_scan([(a_c, sq_c * (gi_c * uc_c), ac, hc_ref.at[d], n_ctx),
                         (a_l, sq_l * (gi_l * uc_l), al, hl_ref.at[d], n_lat)], reverse=(d == 1))
        zl_ref[...] = (_gelu(gl_ref[...].astype(F32))[0] * (hl_ref[0] + hl_ref[1])).astype(BF16)
        zc_ref[...] = (_gelu(gc_ref[...].astype(F32))[0] * (hc_ref[0] + hc_ref[1])).astype(BF16)

    return pl.pallas_call(
        body, grid=(8, 2), name=name, in_specs=_lru_specs(cfg),
        out_specs=[pl.BlockSpec((n_lat, 128), lambda hb, e: (e, hb)), pl.BlockSpec((n_ctx, 128), lambda hb, e: (e, hb)),
                   pl.BlockSpec((2, n_lat, 128), lambda hb, e: (0, e, hb)),
                   pl.BlockSpec((2, n_ctx, 128), lambda hb, e: (0, e, hb))],
        out_shape=[jax.ShapeDtypeStruct((cfg.t_lat, D), BF16), jax.ShapeDtypeStruct((cfg.t_ctx, D), BF16),
                   jax.ShapeDtypeStruct((2, cfg.t_lat, D), F32), jax.ShapeDtypeStruct((2, cfg.t_ctx, D), F32)],
        scratch_shapes=[pltpu.VMEM((n_lat, 128), F32), pltpu.VMEM((n_ctx, 128), F32)],
        compiler_params=_params(("parallel", "arbitrary")))(p, p, p, p, *consts)


def _lru_bwd(cfg, p, dz, h_lat, h_ctx, consts, name):
    n_lat, n_ctx, cb = cfg.n_lat, cfg.n_ctx, cfg.ctx_blk

    def body(gl_ref, ul_ref, gc_ref, uc_ref, cw_ref, cb_ref, wa_ref, ba_ref, wx_ref, bx_ref, lam_ref,
             dzl_ref, dzc_ref, hl, hc, dgl_ref, dul_ref, dgc_ref, duc_ref, dwa_ref, dwx_ref, vec_ref,
             al, bl, ac, bc):
        e = pl.program_id(1)

        @pl.when(e == 0)
        def _():
            dwa_ref[...] = jnp.zeros_like(dwa_ref)
            dwx_ref[...] = jnp.zeros_like(dwx_ref)
            vec_ref[...] = jnp.zeros_like(vec_ref)

        row_l = lax.broadcasted_iota(jnp.int32, (n_lat, 1), 0)
        row_c = lax.broadcasted_iota(jnp.int32, (n_ctx, 1), 0)
        u_l, u_c = ul_ref[...].astype(F32), uc_ref[...].astype(F32)
        uc_l = _conv(u_l, cw_ref, cb_ref, row_l)
        uc_c = _conv(u_c, cw_ref, cb_ref, row_c)
        gel_l, t_l = _gelu(gl_ref[...].astype(F32))
        gel_c, t_c = _gelu(gc_ref[...].astype(F32))
        dz_l, dz_c = dzl_ref[...].astype(F32), dzc_ref[...].astype(F32)
        dgl_ref[...] = (dz_l * (hl[0] + hl[1]) * _gelu_grad(gl_ref[...].astype(F32), t_l)).astype(BF16)
        dgc_ref[...] = (dz_c * (hc[0] + hc[1]) * _gelu_grad(gc_ref[...].astype(F32), t_c)).astype(BF16)
        dy_l, dy_c = dz_l * gel_l, dz_c * gel_c
        duc_l = jnp.zeros((n_lat, 128), F32)
        duc_c = jnp.zeros((n_ctx, 128), F32)
        for d in range(2):
            r_l, gi_l, sp, dsp, a_l, sq_l = _lru_gates(uc_l, d, wa_ref, ba_ref, wx_ref, bx_ref, lam_ref)
            r_c, gi_c, _, _, a_c, sq_c = _lru_gates(uc_c, d, wa_ref, ba_ref, wx_ref, bx_ref, lam_ref)
            if d == 0:
                an_l = _shift_up(a_l, 1, row_l)
                an_c = jnp.where(row_c < n_ctx - 1, pltpu.roll(a_c, n_ctx - 1, 0), a_l[0:1, :])
            else:
                an_l = _shift_down(a_l, 1, row_l)
                an_c = jnp.where(row_c >= 1, pltpu.roll(a_c, 1, 0), a_l[n_lat - 1:n_lat, :])
            _chain_scan([(an_l, dy_l, al, bl, n_lat), (an_c, dy_c, ac, bc, n_ctx)], reverse=(d == 0))
            dsp_sum = jnp.zeros((1, 128), F32)
            for (dh, h, r, gi, a, sq, uc, seg) in ((bl[...], hl[d], r_l, gi_l, a_l, sq_l, uc_l, "l"),
                                                  (bc[...], hc[d], r_c, gi_c, a_c, sq_c, uc_c, "c")):
                b0 = sq * (gi * uc)
                t1 = dh * sq
                dla = dh * (h - b0) - (dh * gi * uc) * (a * a) / sq
                dzr = (dla * ((-LRU_C) * sp)) * r * (1.0 - r)
                dzi = (t1 * uc) * gi * (1.0 - gi)
                dsp_sum = dsp_sum + jnp.sum(dla * ((-LRU_C) * r), axis=0, keepdims=True)
                dwa_ref[d] += _dot_tn(uc, dzr)
                dwx_ref[d] += _dot_tn(uc, dzi)
                vec_ref[d:d + 1, :] += jnp.sum(dzr, axis=0, keepdims=True)
                vec_ref[2 + d:3 + d, :] += jnp.sum(dzi, axis=0, keepdims=True)
                duc = t1 * gi + _dot_nt(dzr, wa_ref[d]) + _dot_nt(dzi, wx_ref[d])
                if seg == "l":
                    duc_l = duc_l + duc
                else:
                    duc_c = duc_c + duc
            vec_ref[4 + d:5 + d, :] += dsp_sum * dsp
        for duc, u, row, du_ref in ((duc_l, u_l, row_l, dul_ref), (duc_c, u_c, row_c, duc_ref)):
            du_ref[...] = (_shift_up(duc, 1, row) * cw_ref[0:1, :] + duc * cw_ref[1:2, :]
                           + _shift_down(duc, 1, row) * cw_ref[2:3, :]
                           + _shift_down(duc, 2, row) * cw_ref[3:4, :]).astype(BF16)
            vec_ref[6:7, :] += jnp.sum(duc * _shift_down(u, 1, row), axis=0, keepdims=True)
            vec_ref[7:8, :] += jnp.sum(duc * u, axis=0, keepdims=True)
            vec_ref[8:9, :] += jnp.sum(duc * _shift_up(u, 1, row), axis=0, keepdims=True)
            vec_ref[9:10, :] += jnp.sum(duc * _shift_up(u, 2, row), axis=0, keepdims=True)
            vec_ref[10:11, :] += jnp.sum(duc, axis=0, keepdims=True)

    lat = pl.BlockSpec((n_lat, 128), lambda hb, e: (e, hb))
    ctx = pl.BlockSpec((n_ctx, 128), lambda hb, e: (e, hb))
    wspec = pl.BlockSpec((2, None, 128, 128), lambda hb, e: (0, hb, 0, 0))
    sd = jax.ShapeDtypeStruct
    return pl.pallas_call(
        body, grid=(8, 2), name=name,
        in_specs=_lru_specs(cfg) + [pl.BlockSpec((n_lat, 128), lambda hb, e: (e, hb)),
                                    pl.BlockSpec((n_ctx, 128), lambda hb, e: (cb + e, hb)),
                                    pl.BlockSpec((2, n_lat, 128), lambda hb, e: (0, e, hb)),
                                    pl.BlockSpec((2, n_ctx, 128), lambda hb, e: (0, e, hb))],
        out_specs=[lat, lat, ctx, ctx, wspec, wspec, pl.BlockSpec((None, 16, 128), lambda hb, e: (hb, 0, 0))],
        out_shape=[sd((cfg.t_lat, D), BF16), sd((cfg.t_lat, D), BF16), sd((cfg.t_ctx, D), BF16), sd((cfg.t_ctx, D), BF16),
                   sd((2, 8, 128, 128), F32), sd((2, 8, 128, 128), F32), sd((8, 16, 128), F32)],
        scratch_shapes=[pltpu.VMEM((n_lat, 128), F32)] * 2 + [pltpu.VMEM((n_ctx, 128), F32)] * 2,
        compiler_params=_params(("parallel", "arbitrary")))(p, p, p, p, *consts, dz, dz, h_lat, h_ctx)


def _position():
    x, y, c = lax.axis_index("x"), lax.axis_index("y"), lax.axis_index("c")
    return x, y, c, 4 * x + 2 * y + c


def _peer(x, y, c, k):
    px = 1 - x if k & 4 else x
    py = 1 - y if k & 2 else y
    pc = 1 - c if k & 1 else c
    return (px, py, pc), 4 * px + 2 * py + pc


def _all_gather(v, name, in_vmem):
    def body(v_ref, o_ref, send_sems, recv_sems, local_sem):
        x, y, c, me = _position()
        mine = pltpu.make_async_copy(v_ref, o_ref.at[me], local_sem)
        mine.start()
        sends = []
        for k in range(1, N_DEV):
            peer, _ = _peer(x, y, c, k)
            cp = pltpu.make_async_remote_copy(src_ref=v_ref, dst_ref=o_ref.at[me], send_sem=send_sems.at[k - 1],
                                              recv_sem=recv_sems.at[k - 1], device_id=peer, device_id_type=MESH)
            cp.start()
            sends.append(cp)
        for k in range(1, N_DEV):
            peer, peer_lin = _peer(x, y, c, k)
            pltpu.make_async_remote_copy(src_ref=v_ref, dst_ref=o_ref.at[peer_lin], send_sem=send_sems.at[k - 1],
                                         recv_sem=recv_sems.at[k - 1], device_id=peer, device_id_type=MESH).wait_recv()
        for cp in sends:
            cp.wait_send()
        mine.wait()

    space = pltpu.VMEM if in_vmem else pl.ANY
    return pl.pallas_call(
        body, name=name,
        in_specs=[pl.BlockSpec(memory_space=space)], out_specs=pl.BlockSpec(memory_space=space),
        out_shape=jax.ShapeDtypeStruct((N_DEV,) + v.shape, v.dtype),
        scratch_shapes=[pltpu.SemaphoreType.DMA((N_DEV - 1,)), pltpu.SemaphoreType.DMA((N_DEV - 1,)),
                        pltpu.SemaphoreType.DMA],
        compiler_params=pltpu.CompilerParams(vmem_limit_bytes=VMEM_LIMIT))(v)


_HBM = pl.BlockSpec(memory_space=pltpu.HBM)
_SEM = pl.BlockSpec(memory_space=pltpu.SEMAPHORE)
_EFFECT = pltpu.SideEffectType.DATAFLOW_SIDE_EFFECTING


ALL_PEERS = tuple(range(1, N_DEV))
SAME_CORE_AND_SIBLING = (1, 2, 4, 6)


def _push_start(src, land, block_of, name, relations=ALL_PEERS):
    def body(src_ref, land_ref, send_sem, recv_sem, src_thru, land_thru, token):
        x, y, c, me = _position()
        for k in relations:
            peer, peer_lin = _peer(x, y, c, k)
            mine, there = block_of(src_ref, land_ref, me, peer_lin)
            pltpu.make_async_remote_copy(src_ref=mine, dst_ref=there, send_sem=send_sem, recv_sem=recv_sem,
                                         device_id=peer, device_id_type=MESH).start()
        mine, here = block_of(src_ref, land_ref, me, me)
        pltpu.make_async_copy(mine, here, recv_sem).start()
        token[...] = jnp.zeros_like(token)

    return pl.pallas_call(
        body, name=name,
        out_shape=(pltpu.SemaphoreType.DMA(()), pltpu.SemaphoreType.DMA(()), pltpu.HBM(src.shape, src.dtype),
                   pltpu.HBM(land.shape, land.dtype), jax.ShapeDtypeStruct((8, 128), F32)),
        in_specs=(_HBM, _HBM), out_specs=(_SEM, _SEM, _HBM, _HBM, pl.BlockSpec(memory_space=pltpu.VMEM)),
        input_output_aliases={0: 2, 1: 3},
        compiler_params=pltpu.CompilerParams(has_side_effects=_EFFECT),
    )(pltpu.with_memory_space_constraint(src, pltpu.HBM), pltpu.with_memory_space_constraint(land, pltpu.HBM))


def _push_wait(handle, blocks_of, after, name, n_peers=N_DEV - 1):
    send_sem, recv_sem, src_thru, land_thru, _ = handle

    def body(src_ref, land_ref, send_sem, recv_sem, after_ref, src_dead, got_ref):
        x, y, c, _ = _position()
        sent, landed = blocks_of(land_ref, n_peers), blocks_of(land_ref, n_peers + 1)
        pltpu.make_async_remote_copy(src_ref=sent, dst_ref=sent, send_sem=send_sem, recv_sem=recv_sem,
                                     device_id=(x, y, 1 - c), device_id_type=MESH).wait_send()
        pltpu.make_async_remote_copy(src_ref=landed, dst_ref=landed, send_sem=send_sem, recv_sem=recv_sem,
                                     device_id=(x, y, 1 - c), device_id_type=MESH).wait_recv()

    return pl.pallas_call(
        body, name=name,
        out_shape=(pltpu.HBM(src_thru.shape, src_thru.dtype), pltpu.HBM(land_thru.shape, land_thru.dtype)),
        in_specs=(_HBM, _HBM, _SEM, _SEM, pl.BlockSpec(memory_space=pl.ANY)), out_specs=(_HBM, _HBM),
        input_output_aliases={0: 0, 1: 1},
        compiler_params=pltpu.CompilerParams(has_side_effects=_EFFECT),
    )(src_thru, land_thru, send_sem, recv_sem, after)[1]


def _gather_start(src, name, relations=ALL_PEERS):
    g, r, C = src.shape
    land = lax.empty((g, N_DEV * r, C), src.dtype)
    return _push_start(src, land, lambda s, z, i, p: (s, z.at[:, pl.ds(i * r, r), :]), name, relations)


def _gather_wait(handle, after, name, n_peers=N_DEV - 1):
    r = handle[2].shape[1]
    return _push_wait(handle, lambda z, n: z.at[:, pl.ds(0, n * r), :], after, name, n_peers)


def _relay_start(land, r, name):
    def body(land_ref, send_sem, recv_sem, land_thru, token):
        x, y, c, _ = _position()
        for k in (2, 4, 6):
            _, origin = _peer(x, y, c, k)
            rows = land_ref.at[:, pl.ds(origin * r, r), :]
            pltpu.make_async_remote_copy(src_ref=rows, dst_ref=rows, send_sem=send_sem, recv_sem=recv_sem,
                                         device_id=(x, y, 1 - c), device_id_type=MESH).start()
        token[...] = jnp.zeros_like(token)

    return pl.pallas_call(
        body, name=name,
        out_shape=(pltpu.SemaphoreType.DMA(()), pltpu.SemaphoreType.DMA(()), pltpu.HBM(land.shape, land.dtype),
                   jax.ShapeDtypeStruct((8, 128), F32)),
        in_specs=(_HBM,), out_specs=(_SEM, _SEM, _HBM, pl.BlockSpec(memory_space=pltpu.VMEM)),
        input_output_aliases={0: 2},
        compiler_params=pltpu.CompilerParams(has_side_effects=_EFFECT),
    )(pltpu.with_memory_space_constraint(land, pltpu.HBM))


def _relay_wait(handle, r, after, name):
    send_sem, recv_sem, land_thru, _ = handle

    def body(land_ref, send_sem, recv_sem, after_ref, got_ref):
        x, y, c, _ = _position()
        three = land_ref.at[:, pl.ds(0, 3 * r), :]
        cp = pltpu.make_async_remote_copy(src_ref=three, dst_ref=three, send_sem=send_sem, recv_sem=recv_sem,
                                          device_id=(x, y, 1 - c), device_id_type=MESH)
        cp.wait_send()
        cp.wait_recv()

    return pl.pallas_call(
        body, name=name, out_shape=(pltpu.HBM(land_thru.shape, land_thru.dtype),),
        in_specs=(_HBM, _SEM, _SEM, pl.BlockSpec(memory_space=pl.ANY)), out_specs=(_HBM,),
        input_output_aliases={0: 0},
        compiler_params=pltpu.CompilerParams(has_side_effects=_EFFECT),
    )(land_thru, send_sem, recv_sem, after)[0]


def _exchange_start(grad, name):
    g, rows, C = grad.shape
    r = rows // N_DEV
    land = lax.empty((N_DEV, g, r, C), grad.dtype)
    return _push_start(grad, land, lambda s, z, i, p: (s.at[:, pl.ds(p * r, r), :], z.at[i]), name)


def _exchange_wait(handle, after, name):
    return _push_wait(handle, lambda z, n: z.at[pl.ds(0, n)], after, name)


def _sum_blocks(v, name):
    k, rows, cols = v.shape
    tr = rows
    for cand in (rows, 512, 352, 256, 176, 128, 64, 32, 16):
        if rows % cand == 0 and k * cand * cols * v.dtype.itemsize <= 6 * 1024 * 1024:
            tr = cand
            break

    def body(v_ref, o_ref):
        acc = v_ref[0].astype(F32)
        for s in range(1, k):
            acc = acc + v_ref[s].astype(F32)
        o_ref[...] = acc

    return pl.pallas_call(
        body, grid=(rows // tr,), name=name,
        in_specs=[pl.BlockSpec((k, tr, cols), lambda i: (0, i, 0))],
        out_specs=pl.BlockSpec((tr, cols), lambda i: (i, 0)),
        out_shape=jax.ShapeDtypeStruct((rows, cols), F32),
        compiler_params=_params(("parallel",)))(v)


def _adam_math(w, g, m, v):
    m2 = B1 * m + (1.0 - B1) * g
    v2 = B2 * v + (1.0 - B2) * (g * g)
    m_hat = m2 / (1.0 - B1 ** STEP)
    v_hat = v2 / (1.0 - B2 ** STEP)
    return -LR * (m_hat / (jnp.sqrt(v_hat) + EPS) + WD * w), m2, v2


def _adamw(w, g, m, v, name, dep=None):
    shp = w.shape
    rows, cols = (shp[-2], shp[-1]) if len(shp) >= 2 else (1, shp[-1])
    lead = math.prod(shp[:-2]) if len(shp) > 2 else 1
    fits = [t for t in range(8, rows + 1, 8) if rows % t == 0 and t * cols * 4 <= 2 * 1024 * 1024]
    tr = max(fits) if fits else rows

    def body(w_ref, g_ref, m_ref, v_ref, *rest):
        d_ref, m2_ref, v2_ref = rest[-3:]
        d_ref[...], m2_ref[...], v2_ref[...] = _adam_math(w_ref[...], g_ref[...], m_ref[...], v_ref[...])

    blk = pl.BlockSpec((None, tr, cols), lambda b, i: (b, i, 0))
    extra = [] if dep is None else [dep]
    outs = pl.pallas_call(
        body, grid=(lead, rows // tr), name=name,
        in_specs=[blk] * 4 + [pl.BlockSpec(memory_space=pl.ANY)] * len(extra), out_specs=[blk] * 3,
        out_shape=[jax.ShapeDtypeStruct((lead, rows, cols), F32)] * 3,
        compiler_params=_params(("parallel", "parallel")))(*[a.reshape(lead, rows, cols) for a in (w, g, m, v)], *extra)
    return [o.reshape(shp) for o in outs]


def _as2d(a):
    n = a.size
    if n % 1024 == 0:
        return a.reshape(n // 1024, 1024)
    if n % 128 == 0:
        return a.reshape(n // 128, 128)
    return a.reshape(1, n)


def _blocks_to_cols(a):
    b = jnp.moveaxis(a, 0, -2)
    return b.reshape(b.shape[:-2] + (b.shape[-2] * b.shape[-1],))


def _pack_rows(parts):
    padded, offs, r = [], [], 0
    for p in parts:
        pad = (-p.shape[0]) % 8
        padded.append(jnp.pad(p, ((0, pad), (0, 0))) if pad else p)
        offs.append(r)
        r += p.shape[0] + pad
    return jnp.concatenate(padded, axis=0), offs


def _silu(x):
    return x * jax.nn.sigmoid(x)


def kernel(x, c, ctx, c_ctx, w_mod, b_mod, ln_g, ln_b, ffn_w_gate, ffn_w_up, ffn_w_down, mix_ab_w_in, attn_sink, pool_w, pool_scale, mix_ab_w_out, lru_w_in, lru_conv_w, lru_conv_b, lru_wa, lru_ba, lru_wx, lru_bx, lru_lambda, lru_w_out, loss_target, m_c_ctx, m_w_mod, m_b_mod, m_ln_g, m_ln_b, m_ffn_w_gate, m_ffn_w_up, m_ffn_w_down, m_mix_ab_w_in, m_attn_sink, m_pool_w, m_pool_scale, m_mix_ab_w_out, m_lru_w_in, m_lru_conv_w, m_lru_conv_b, m_lru_wa, m_lru_ba, m_lru_wx, m_lru_bx, m_lru_lambda, m_lru_w_out, v_c_ctx, v_w_mod, v_b_mod, v_ln_g, v_ln_b, v_ffn_w_gate, v_ffn_w_up, v_ffn_w_down, v_mix_ab_w_in, v_attn_sink, v_pool_w, v_pool_scale, v_mix_ab_w_out, v_lru_w_in, v_lru_conv_w, v_lru_conv_b, v_lru_wa, v_lru_ba, v_lru_wx, v_lru_bx, v_lru_lambda, v_lru_w_out):
    weights = dict(c_ctx=c_ctx, w_mod=w_mod, b_mod=b_mod, ln_g=ln_g, ln_b=ln_b, ffn_w_gate=ffn_w_gate,
                   ffn_w_up=ffn_w_up, ffn_w_down=ffn_w_down, mix_ab_w_in=mix_ab_w_in, attn_sink=attn_sink,
                   pool_w=pool_w, pool_scale=pool_scale, mix_ab_w_out=mix_ab_w_out, lru_w_in=lru_w_in,
                   lru_conv_w=lru_conv_w, lru_conv_b=lru_conv_b, lru_wa=lru_wa, lru_ba=lru_ba, lru_wx=lru_wx,
                   lru_bx=lru_bx, lru_lambda=lru_lambda, lru_w_out=lru_w_out)
    mom_m = dict(c_ctx=m_c_ctx, w_mod=m_w_mod, b_mod=m_b_mod, ln_g=m_ln_g, ln_b=m_ln_b, ffn_w_gate=m_ffn_w_gate,
                 ffn_w_up=m_ffn_w_up, ffn_w_down=m_ffn_w_down, mix_ab_w_in=m_mix_ab_w_in, attn_sink=m_attn_sink,
                 pool_w=m_pool_w, pool_scale=m_pool_scale, mix_ab_w_out=m_mix_ab_w_out, lru_w_in=m_lru_w_in,
                 lru_conv_w=m_lru_conv_w, lru_conv_b=m_lru_conv_b, lru_wa=m_lru_wa, lru_ba=m_lru_ba, lru_wx=m_lru_wx,
                 lru_bx=m_lru_bx, lru_lambda=m_lru_lambda, lru_w_out=m_lru_w_out)
    mom_v = dict(c_ctx=v_c_ctx, w_mod=v_w_mod, b_mod=v_b_mod, ln_g=v_ln_g, ln_b=v_ln_b, ffn_w_gate=v_ffn_w_gate,
                 ffn_w_up=v_ffn_w_up, ffn_w_down=v_ffn_w_down, mix_ab_w_in=v_mix_ab_w_in, attn_sink=v_attn_sink,
                 pool_w=v_pool_w, pool_scale=v_pool_scale, mix_ab_w_out=v_mix_ab_w_out, lru_w_in=v_lru_w_in,
                 lru_conv_w=v_lru_conv_w, lru_conv_b=v_lru_conv_b, lru_wa=v_lru_wa, lru_ba=v_lru_ba, lru_wx=v_lru_wx,
                 lru_bx=v_lru_bx, lru_lambda=v_lru_lambda, lru_w_out=v_lru_w_out)
    names = list(weights)

    n_lat, n_ctx = x.shape[1], ctx.shape[1]
    cfg = _Cfg(n_lat, n_ctx)
    _, _, _, me = _position()
    mcols = w_mod.shape[2]

    def t_bf16(w):
        return jnp.swapaxes(w, -1, -2).astype(BF16)

    def ffn_src(l, i):
        return jnp.stack([t_bf16(ffn_w_gate[l, i]), t_bf16(ffn_w_up[l, i]), ffn_w_down[l, i].astype(BF16)])

    pending = {}

    def start_gathers(items, tok):
        for key, make_src in items:
            pending[key] = _gather_start(make_src() + tok.astype(BF16), "gather_start_" + key)
            tok = pending[key][4][0, 0]
        return tok

    def weights_now(key, after):
        return _gather_wait(pending[key], after, "gather_wait_" + key)

    first = _gather_start(ffn_src(0, 0), "gather_start_ffn00", SAME_CORE_AND_SIBLING)
    tok = first[4][0, 0]

    small_names = ["ln_g", "ln_b", "lru_conv_w", "lru_conv_b", "lru_ba", "lru_bx", "lru_lambda"]
    small, small_off = _pack_rows([(c + tok).reshape(-1, 128)] + [weights[n].reshape(-1, 128) for n in small_names])
    small_all = _all_gather(small, "gather_small", True)

    def small_full(idx, shp):
        rows = math.prod(shp) // 128
        return _blocks_to_cols(small_all[:, small_off[idx]:small_off[idx] + rows, :].reshape((N_DEV,) + shp))

    c_all = small_all[:, :2 * D // 128, :].reshape(2 * N_DEV, D)
    ln_g_f, ln_b_f = small_full(1, ln_g.shape), small_full(2, ln_b.shape)
    lru_consts = (small_full(3, lru_conv_w.shape)[0], small_full(4, lru_conv_b.shape), lru_wa[0],
                  small_full(5, lru_ba.shape)[0], lru_wx[0], small_full(6, lru_bx.shape)[0],
                  small_full(7, lru_lambda.shape)[0])

    s_rows = jnp.zeros((32, D), F32).at[:16].set(_silu(c_all)).at[16].set(_silu(c_ctx)).astype(BF16)
    mod_mine = jnp.stack([_matmul(s_rows, w_mod[l], "nn", F32, "mod_fwd", bn_cap=1280) for l in range(2)])
    mod_all = _all_gather(mod_mine.reshape(64, mcols), "gather_mod", True).reshape(N_DEV, 2, 32, mcols)
    r_ffn = ffn_w_down.shape[2]
    relay = _relay_start(_gather_wait(first, mod_all, "gather_wait_ffn00", n_peers=len(SAME_CORE_AND_SIBLING)),
                         r_ffn, "gather_relay_start_ffn00")
    tok = start_gathers([("ab_in", lambda: t_bf16(mix_ab_w_in)), ("ab_out", lambda: mix_ab_w_out.astype(BF16)),
                         ("ffn01", lambda: ffn_src(0, 1)), ("ffn10", lambda: ffn_src(1, 0)),
                         ("lru_in", lambda: t_bf16(lru_w_in)), ("lru_out", lambda: lru_w_out.astype(BF16)),
                         ("ffn11", lambda: ffn_src(1, 1))], relay[3][0, 0])
    mod_full = _blocks_to_cols(mod_all) + (b_mod[:, None, :] + tok)
    ex0 = 2 * me
    mods = []
    for l in range(2):
        rows = jnp.stack([lax.dynamic_index_in_dim(mod_full[l], ex0, 0, False),
                          lax.dynamic_index_in_dim(mod_full[l], ex0 + 1, 0, False), mod_full[l, 16]])
        mods.append(rows.reshape(3, N_MOD, D))

    h0 = jnp.concatenate([x.reshape(cfg.t_lat, D), ctx.reshape(cfg.t_ctx, D)], axis=0)
    cos, sin = _rope_tables(n_lat)
    sink_rows = jnp.broadcast_to(attn_sink[0][:, None], (8, 128)).astype(F32)

    saved = []
    wf = [[None, None], [None, None]]
    h = h0
    xin = _modulate(cfg, h0, mods[0], 0, 1, "modulate_in")
    for l in range(2):
        st = {"h_in": h, "xin1": xin}
        wf[l][0] = (_relay_wait(relay, r_ffn, xin, "gather_relay_wait_ffn00") if l == 0
                    else weights_now("ffn10", xin))
        g1, u1, y1 = _ffn_fwd(xin, wf[l][0], "ffn_fwd")
        h1, xhat1, rstd1, xin2 = _ln_fwd(cfg, h, y1, mods[l], 2, 0.5, ln_g_f[l, 0][None], ln_b_f[l, 0][None],
                                          mods[l], (3, 4), "ln_fwd_a")
        st.update(g1=g1, u1=u1, y1=y1, h1=h1, xhat1=xhat1, rstd1=rstd1, xin2=xin2)
        if l == 0:
            w_ab_in_t = weights_now("ab_in", xin2)[0]
            p = _matmul(xin2, w_ab_in_t, "nt", BF16, "mix_ab_in")
            att_l, att_c = _attn_fwd(cfg, p, cos, sin, sink_rows, "attn_fwd")
            pool_l = _pool_fwd(p, pool_w[0], pool_scale, n_lat, 0, 2, "pool_fwd_lat")
            pool_c = _pool_fwd(p, pool_w[0], pool_scale, n_ctx, cfg.ctx_blk, 2, "pool_fwd_ctx")
            cat = jnp.concatenate([jnp.concatenate([att_l, pool_l], axis=1),
                                   jnp.concatenate([att_c, pool_c], axis=1)], axis=0)
            w_ab_out = weights_now("ab_out", cat)[0]
            y2 = _matmul(cat, w_ab_out, "nn", BF16, "mix_ab_out")
        else:
            w_lru_in_t = weights_now("lru_in", xin2)[0]
            p = _matmul(xin2, w_lru_in_t, "nt", BF16, "lru_in")
            z_l, z_c, st["h_lat"], st["h_ctx"] = _lru_fwd(cfg, p, lru_consts, "lru_fwd")
            cat = jnp.concatenate([z_l, z_c], axis=0)
            w_lru_out = weights_now("lru_out", cat)[0]
            y2 = _matmul(cat, w_lru_out, "nn", BF16, "lru_out")
        h2, xhat2, rstd2, xin3 = _ln_fwd(cfg, h1, y2, mods[l], 5, 1.0, ln_g_f[l, 1][None], ln_b_f[l, 1][None],
                                          mods[l], (6, 7), "ln_fwd_b")
        wf[l][1] = weights_now("ffn%d1" % l, xin3)
        g3, u3, y3 = _ffn_fwd(xin3, wf[l][1], "ffn_fwd")
        if l == 0:
            h3, xhat3, rstd3, xin = _ln_fwd(cfg, h2, y3, mods[l], 8, 0.5, ln_g_f[l, 2][None], ln_b_f[l, 2][None],
                                            mods[1], (0, 1), "ln_fwd_a")
        else:
            h3, xhat3, rstd3 = _ln_fwd(cfg, h2, y3, mods[l], 8, 0.5, ln_g_f[l, 2][None], ln_b_f[l, 2][None],
                                       None, None, "ln_fwd_last")
        st.update(p=p, cat=cat, y2=y2, h2=h2, xhat2=xhat2, rstd2=rstd2, xin3=xin3, g3=g3, u3=u3, y3=y3,
                  xhat3=xhat3, rstd3=rstd3)
        saved.append(st)
        h = h3

    dy, loss_tile = _loss(cfg, h, loss_target.reshape(cfg.t_lat, D), "loss")
    loss = lax.psum(loss_tile[0, 0], ("x", "y", "c"))

    grads = {}
    dmod = [None, None]
    recv_ffn = [[None, None], [None, None]]
    dln_g = [[None] * 3, [None] * 3]
    dln_b = [[None] * 3, [None] * 3]

    def ffn_weight_grads(tag, xin_b, dg, du, a_act, dys):
        handles = []
        for k, (lhs, rhs) in enumerate(((dg, xin_b), (du, xin_b), (a_act, dys))):
            part = _matmul(lhs, rhs, "tn", BF16, "ffn_dw", bm_cap=1408, bk_cap=2304)[None]
            handles.append(_exchange_start(part, "exchange_start_ffn%s_%d" % (tag, k)))
        return handles

    def pin(handles):
        total = handles[0][4][0, 0]
        for hd in handles[1:]:
            total = total + hd[4][0, 0]
        return total

    up = (dy,)
    dmod_next = None
    last_sent = None
    for l in (1, 0):
        st = saved[l]
        dm = [None] * N_MOD

        def put_stats(stats, gate_idx, nxt):
            dm[gate_idx] = stats[:, 2, :]
            if nxt is not None:
                nxt[0][nxt[1]] = stats[:, 4, :]
                nxt[0][nxt[1] + 1] = stats[:, 3, :]

        lng3 = ln_g_f[l, 2][None] if last_sent is None else ln_g_f[l, 2][None] + pin(last_sent)
        if len(up) > 1:
            up = (up[0], up[1], ln_b_f[l, 2][None], up[3], up[4])
        dres, dys, stats = _ln_bwd(cfg, up, st["xhat3"], st["rstd3"], st["y3"], mods[l], 8, 0.5,
                                   lng3, "ln_bwd_fused" if len(up) > 1 else "ln_bwd_last")
        put_stats(stats, 8, None if len(up) == 1 else (dmod_next, 0))
        dln_g[l][2], dln_b[l][2] = stats[:, 0, :].sum(0), stats[:, 1, :].sum(0)
        dg, du, a_act, dxin = _ffn_bwd(dys, st["g3"], st["u3"], wf[l][1], "ffn_bwd")
        recv_ffn[l][1] = ffn_weight_grads("%d1" % l, st["xin3"], dg, du, a_act, dys)
        dres, dys, stats = _ln_bwd(cfg, (dres, dxin, ln_b_f[l, 1][None], mods[l], 7), st["xhat2"], st["rstd2"], st["y2"],
                                   mods[l], 5, 1.0, ln_g_f[l, 1][None] + pin(recv_ffn[l][1]), "ln_bwd_fused")
        put_stats(stats, 5, (dm, 6))
        dln_g[l][1], dln_b[l][1] = stats[:, 0, :].sum(0), stats[:, 1, :].sum(0)
        if l == 0:
            dw_out = _matmul(st["cat"], dys, "tn", BF16, "mix_ab_dw_out")
            dcat = _matmul(dys, w_ab_out, "nt", BF16, "mix_ab_dcat")
            dq, dk, dv, dqc, dkc, dvc, dsink = _attn_bwd(cfg, st["p"], dcat, cos, sin, sink_rows, "attn_bwd")
            du_l, dpw_l, dps_l = _pool_bwd(st["p"], pool_w[0], pool_scale, dcat, n_lat, 0, 2, "pool_bwd_lat")
            du_c, dpw_c, dps_c = _pool_bwd(st["p"], pool_w[0], pool_scale, dcat, n_ctx, cfg.ctx_blk, 2, "pool_bwd_ctx")
            dp = jnp.concatenate([jnp.concatenate([dq, dk, dv, du_l], axis=1),
                                  jnp.concatenate([dqc, dkc, dvc, du_c], axis=1)], axis=0)
            dw_in_t = _matmul(dp, st["xin2"], "tn", BF16, "mix_ab_dw_in", bm_cap=1280)
            dxin = _matmul(dp, w_ab_in_t, "nn", BF16, "mix_ab_dx")
            recv_mix = [_exchange_start(part, "exchange_start_mix_ab_%d" % k)
                        for k, part in enumerate((dw_in_t[None], dw_out[None], _as2d(dpw_l + dpw_c)[None]))]
            grads["attn_sink"] = (dsink[0, :, 0] + dsink[1, :, 0])[None, :]
            grads["pool_scale"] = dps_l + dps_c
        else:
            dw_out = _matmul(st["cat"], dys, "tn", BF16, "lru_dw_out")
            dz = _matmul(dys, w_lru_out, "nt", BF16, "lru_dz")
            dgl, dul, dgc, duc, dwa, dwx, vec = _lru_bwd(cfg, st["p"], dz, st["h_lat"], st["h_ctx"], lru_consts, "lru_bwd")
            dp = jnp.concatenate([jnp.concatenate([dgl, dul], axis=1), jnp.concatenate([dgc, duc], axis=1)], axis=0)
            dw_in_t = _matmul(dp, st["xin2"], "tn", BF16, "lru_dw_in", bm_cap=1024)
            dxin = _matmul(dp, w_lru_in_t, "nn", BF16, "lru_dx")
            recv_mix = [_exchange_start(part, "exchange_start_lru_%d" % k)
                        for k, part in enumerate((dw_in_t[None], dw_out[None], _as2d(dwa)[None], _as2d(dwx)[None]))]
            vec_t = jnp.moveaxis(vec, 0, 1).reshape(16, D)
            grads["lru_ba"], grads["lru_bx"] = vec_t[0:2], vec_t[2:4]
            grads["lru_lambda"], grads["lru_conv_w"], grads["lru_conv_b"] = vec_t[4:6], vec_t[6:10], vec_t[10:11]
        if l == 0:
            recv_ab = recv_mix
        else:
            recv_lru = recv_mix
        dres, dys, stats = _ln_bwd(cfg, (dres, dxin, ln_b_f[l, 0][None], mods[l], 4), st["xhat1"], st["rstd1"], st["y1"],
                                   mods[l], 2, 0.5, ln_g_f[l, 0][None] + pin(recv_mix), "ln_bwd_fused")
        put_stats(stats, 2, (dm, 3))
        dln_g[l][0], dln_b[l][0] = stats[:, 0, :].sum(0), stats[:, 1, :].sum(0)
        dg, du, a_act, dxin = _ffn_bwd(dys, st["g1"], st["u1"], wf[l][0], "ffn_bwd")
        recv_ffn[l][0] = ffn_weight_grads("%d0" % l, st["xin1"], dg, du, a_act, dys)
        last_sent = recv_ffn[l][0]
        dmod[l] = dm
        dmod_next = dm
        up = (dres, dxin, None, mods[l], 1)
    dh0, stats = _modulate_bwd(cfg, up[0], up[1], h0, mods[0] + pin(last_sent), 1, "modulate_bwd")
    dmod[0][0], dmod[0][1] = stats[:, 4, :], stats[:, 3, :]
    grad_x = dh0.reshape(x.shape)

    dmod_mine = jnp.stack([jnp.stack(dmod[l], axis=1).reshape(3, N_MOD * D) for l in range(2)])
    n_dm = 6 * N_MOD * D // 128
    dmod_sent = _gather_start(dmod_mine.reshape(1, n_dm, 128), "gather_start_dmod")
    last_ffn = recv_ffn[0][0]

    def arrived(handle, after, name):
        return _exchange_wait(handle, after, name)

    after_start = dmod_sent[4]
    for l, i in ((1, 1), (1, 0), (0, 1)):
        recv_ffn[l][i] = [arrived(hd, after_start, "exchange_wait_ffn%d%d_%d" % (l, i, k))
                          for k, hd in enumerate(recv_ffn[l][i])]
    recv_ab = [arrived(hd, after_start, "exchange_wait_mix_ab_%d" % k) for k, hd in enumerate(recv_ab)]
    recv_lru = [arrived(hd, after_start, "exchange_wait_lru_%d" % k) for k, hd in enumerate(recv_lru)]

    def shard_sum(recv, name):
        return _sum_blocks(recv.reshape(N_DEV, recv.shape[2], recv.shape[3]), name)

    gate_g = [[None, None], [None, None]]
    up_g = [[None, None], [None, None]]
    down_g = [[None, None], [None, None]]
    def ffn_sums(l, i):
        gt, ut, dn = [shard_sum(r, "sum_ffn") for r in recv_ffn[l][i]]
        gate_g[l][i], up_g[l][i], down_g[l][i] = gt.T, ut.T, dn

    for l, i in ((1, 1), (1, 0), (0, 1)):
        ffn_sums(l, i)
    grads["mix_ab_w_in"] = shard_sum(recv_ab[0], "sum_mix_in").T[None]
    grads["mix_ab_w_out"] = shard_sum(recv_ab[1], "sum_mix_out")[None]
    grads["lru_w_in"] = shard_sum(recv_lru[0], "sum_lru_in").T[None]
    grads["lru_w_out"] = shard_sum(recv_lru[1], "sum_lru_out")[None]
    rep_parts = [shard_sum(recv_lru[2], "sum_rep"), shard_sum(recv_lru[3], "sum_rep"), shard_sum(recv_ab[2], "sum_rep")]
    rep_names = ["lru_wa", "lru_wx", "pool_w"]

    delta, new_m, new_v = {}, {}, {}
    for n in ("mix_ab_w_in", "mix_ab_w_out", "lru_w_in", "lru_w_out"):
        delta[n], new_m[n], new_v[n] = _adamw(weights[n], grads[n], mom_m[n], mom_v[n], "adamw", dep=rep_parts[2])

    dmod_all = _gather_wait(dmod_sent, delta["lru_w_out"], "gather_wait_dmod").reshape(N_DEV, n_dm, 128)
    dmod_sum = _sum_blocks(dmod_all, "sum_dmod").reshape(2, 3, N_MOD * D)
    dmod_all = dmod_all.reshape(N_DEV, 2, 3, N_MOD * D)
    grads["b_mod"] = dmod_sum[:, 0] + dmod_sum[:, 1] + dmod_sum[:, 2]
    dmod_ex = jnp.moveaxis(dmod_all[:, :, 0:2, :], 1, 0).reshape(2, 2 * N_DEV, N_MOD * D)
    dm_rows = jnp.zeros((2, 32, N_MOD * D), F32).at[:, :16].set(dmod_ex).at[:, 16].set(dmod_sum[:, 2])
    dm_cols = lax.dynamic_slice_in_dim(dm_rows, me * mcols, mcols, axis=2).astype(BF16)
    grads["w_mod"] = jnp.stack([_matmul(s_rows, dm_cols[l], "tn", F32, "mod_dw", bn_cap=1280) for l in range(2)])
    ds_part = None
    for l in range(2):
        part = _matmul(dm_cols[l, 16:32], w_mod[l], "nt", F32, "mod_ds", bk_cap=1280)[0]
        ds_part = part if ds_part is None else ds_part + part

    dln_g_f = jnp.stack([jnp.stack(dln_g[l]) for l in range(2)])
    dln_b_f = jnp.stack([jnp.stack(dln_b[l]) for l in range(2)])
    sink_pad = jnp.zeros((1, 128), F32).at[0, :8].set(grads["attn_sink"][0])
    part_list = [p_.reshape(-1, 128) for p_ in rep_parts] + [
        dln_g_f.reshape(-1, 128), dln_b_f.reshape(-1, 128), grads["lru_conv_w"].reshape(-1, 128),
        grads["lru_conv_b"].reshape(-1, 128), grads["lru_ba"].reshape(-1, 128), grads["lru_bx"].reshape(-1, 128),
        grads["lru_lambda"].reshape(-1, 128), ds_part.reshape(-1, 128), sink_pad, grads["pool_scale"].reshape(-1, 128)]
    parts, part_off = _pack_rows(part_list)
    parts_sent = _gather_start(parts[None], "gather_start_partials")

    for n in ("w_mod", "b_mod"):
        grads[n] = grads[n].reshape(weights[n].shape)
        delta[n], new_m[n], new_v[n] = _adamw(weights[n], grads[n], mom_m[n], mom_v[n], "adamw", dep=parts_sent[4])
    parts_all = _gather_wait(parts_sent, delta["b_mod"], "gather_wait_partials").reshape(N_DEV, parts.shape[0], 128)
    parts_sum = _sum_blocks(parts_all, "sum_partials")

    for i, n in enumerate(rep_names):
        rows = part_list[i].shape[0]
        grads[n] = parts_all[:, part_off[i]:part_off[i] + rows, :].reshape(weights[n].shape)

    def take(idx):
        return parts_sum[part_off[idx]:part_off[idx] + part_list[idx].shape[0]]

    def my_cols(full, shp):
        w = shp[-1]
        return lax.dynamic_slice_in_dim(full, me * w, w, axis=full.ndim - 1)

    grads["ln_g"] = my_cols(take(3).reshape(2, 3, D), ln_g.shape)
    grads["ln_b"] = my_cols(take(4).reshape(2, 3, D), ln_b.shape)
    grads["lru_conv_w"] = my_cols(take(5).reshape(1, 4, D), lru_conv_w.shape)
    grads["lru_conv_b"] = my_cols(take(6).reshape(1, D), lru_conv_b.shape)
    grads["lru_ba"] = my_cols(take(7).reshape(1, 2, D), lru_ba.shape)
    grads["lru_bx"] = my_cols(take(8).reshape(1, 2, D), lru_bx.shape)
    grads["lru_lambda"] = my_cols(take(9).reshape(1, 2, D), lru_lambda.shape)
    sg = jax.nn.sigmoid(c_ctx)
    grads["c_ctx"] = take(10).reshape(D) * (sg * (1.0 + c_ctx * (1.0 - sg)))
    grads["attn_sink"] = take(11)[:, :8]
    grads["pool_scale"] = take(12).reshape(pool_scale.shape)

    ffn_names = ("ffn_w_gate", "ffn_w_up", "ffn_w_down")
    last_delta = delta["lru_w_out"]
    for n in names:
        if n in delta or n in ffn_names:
            continue
        grads[n] = grads[n].reshape(weights[n].shape)
        delta[n], new_m[n], new_v[n] = _adamw(weights[n], grads[n], mom_m[n], mom_v[n], "adamw")
        last_delta = delta[n]
    recv_ffn[0][0] = [arrived(hd, last_delta, "exchange_wait_ffn00_%d" % k) for k, hd in enumerate(last_ffn)]
    ffn_sums(0, 0)
    grads["ffn_w_gate"] = jnp.stack([jnp.stack(gate_g[l]) for l in range(2)])
    grads["ffn_w_up"] = jnp.stack([jnp.stack(up_g[l]) for l in range(2)])
    grads["ffn_w_down"] = jnp.stack([jnp.stack(down_g[l]) for l in range(2)])
    for n in ffn_names:
        delta[n], new_m[n], new_v[n] = _adamw(weights[n], grads[n], mom_m[n], mom_v[n], "adamw")

    return (loss, grad_x, *[grads[n] for n in names], *[delta[n] for n in names],
            *[new_m[n] for n in names], *[new_v[n] for n in names])
```

```python
import functools
import math

import jax
import jax.numpy as jnp
from jax import lax
from jax.experimental import pallas as pl
from jax.experimental.pallas import tpu as pltpu

F32 = jnp.float32
BF16 = jnp.bfloat16
MESH = pl.DeviceIdType.MESH

D = 1024
N_MOD = 9
N_DEV = 8
HEAD_DIM = 64
ATT_HEADS = 8
KV_HEADS = 2
ATT_W = 512
BLK = 128
ATT_SCALE = HEAD_DIM ** -0.5
GRID_W = 64
ROPE_FREQS = HEAD_DIM // 4
ROPE_THETA = 10000.0
POOL_R = (1, 2, 4, 8)
LRU_C = 8.0
LN_EPS = 1e-5
NEG_INF = -1e30
ALPHA = 4.0 ** 0.25
LR, B1, B2, EPS, WD, STEP = 0.001, 0.9, 0.999, 1e-08, 0.01, 10
VMEM_LIMIT = 56 * 1024 * 1024
ROW_TILE = 512


def _params(sem=None):
    if sem is None:
        return pltpu.CompilerParams(vmem_limit_bytes=VMEM_LIMIT)
    return pltpu.CompilerParams(dimension_semantics=sem, vmem_limit_bytes=VMEM_LIMIT)


def _sigmoid(x):
    return 0.5 * jnp.tanh(0.5 * x) + 0.5


def _dot(a, b):
    return jnp.dot(a.astype(BF16), b.astype(BF16), preferred_element_type=F32)


def _dot_nt(a, b):
    return lax.dot_general(a.astype(BF16), b.astype(BF16), (((1,), (1,)), ((), ())), preferred_element_type=F32)


def _dot_tn(a, b):
    return lax.dot_general(a.astype(BF16), b.astype(BF16), (((0,), (0,)), ((), ())), preferred_element_type=F32)


def _pick(n, cap):
    best = None
    for m in range(128, min(n, cap) + 1, 128):
        if n % m == 0:
            best = m
    return n if best is None else best


def _chunks(width, step=256):
    out, c = [], 0
    while c < width:
        w = min(step, width - c)
        out.append((c, w))
        c += w
    return out


class _Cfg:
    def __init__(self, n_lat, n_ctx):
        self.n_lat, self.n_ctx = n_lat, n_ctx
        self.t_lat, self.t_ctx = 2 * n_lat, 2 * n_ctx
        self.T = self.t_lat + self.t_ctx
        self.tm = min(ROW_TILE, self.t_ctx)
        assert n_lat % self.tm == 0 and self.t_ctx % self.tm == 0 and n_lat >= 3 * BLK and n_ctx % BLK == 0
        self.nt = self.T // self.tm
        self.nlt = n_lat // self.tm
        self.ctx_blk = self.t_lat // n_ctx

    def seg(self, i):
        return jnp.minimum(i // self.nlt, 2)

    def first_of_seg(self, i):
        return jnp.where(i < 2 * self.nlt, i % self.nlt == 0, i == 2 * self.nlt)


def _modulate(cfg, h, mod, shift_idx, scale_idx, name):
    tm = cfg.tm

    def body(h_ref, mod_ref, o_ref):
        sh = mod_ref[shift_idx:shift_idx + 1, :]
        sc = mod_ref[scale_idx:scale_idx + 1, :]
        o_ref[...] = (h_ref[...] * (1.0 + sc) + sh).astype(BF16)

    return pl.pallas_call(
        body, grid=(cfg.nt,), name=name,
        in_specs=[pl.BlockSpec((tm, D), lambda i: (i, 0)),
                  pl.BlockSpec((None, N_MOD, D), lambda i: (cfg.seg(i), 0, 0))],
        out_specs=pl.BlockSpec((tm, D), lambda i: (i, 0)),
        out_shape=jax.ShapeDtypeStruct((cfg.T, D), BF16),
        compiler_params=_params(("parallel",)),
    )(h, mod)


def _ln_fwd(cfg, h, y, mod, gate_idx, coef, lng, lnb, mod_next, next_idx, name):
    tm = cfg.tm
    has_next = next_idx is not None

    def body(*refs):
        if has_next:
            h_ref, y_ref, mod_ref, g_ref, b_ref, modn_ref, hn_ref, xhat_ref, rstd_ref, xin_ref = refs
        else:
            h_ref, y_ref, mod_ref, g_ref, b_ref, hn_ref, xhat_ref, rstd_ref = refs
        gate = mod_ref[gate_idx:gate_idx + 1, :]
        z = ALPHA * h_ref[...] + (coef * gate) * y_ref[...].astype(F32)
        mu = jnp.mean(z, axis=-1, keepdims=True)
        zc = z - mu
        var = jnp.mean(zc * zc, axis=-1, keepdims=True)
        rstd = lax.rsqrt(var + LN_EPS)
        xhat = zc * rstd
        hn = xhat * g_ref[...] + b_ref[...]
        hn_ref[...] = hn
        xhat_ref[...] = xhat.astype(BF16)
        rstd_ref[...] = rstd
        if has_next:
            sh = modn_ref[next_idx[0]:next_idx[0] + 1, :]
            sc = modn_ref[next_idx[1]:next_idx[1] + 1, :]
            xin_ref[...] = (hn * (1.0 + sc) + sh).astype(BF16)

    row = pl.BlockSpec((tm, D), lambda i: (i, 0))
    modspec = pl.BlockSpec((None, N_MOD, D), lambda i: (cfg.seg(i), 0, 0))
    vec = pl.BlockSpec((1, D), lambda i: (0, 0))
    in_specs = [row, row, modspec, vec, vec]
    args = [h, y, mod, lng, lnb]
    out_specs = [row, row, pl.BlockSpec((tm, 1), lambda i: (i, 0))]
    out_shape = [jax.ShapeDtypeStruct((cfg.T, D), F32), jax.ShapeDtypeStruct((cfg.T, D), BF16),
                 jax.ShapeDtypeStruct((cfg.T, 1), F32)]
    if has_next:
        in_specs.append(modspec)
        args.append(mod_next)
        out_specs.append(row)
        out_shape.append(jax.ShapeDtypeStruct((cfg.T, D), BF16))
    return pl.pallas_call(body, grid=(cfg.nt,), name=name, in_specs=in_specs, out_specs=out_specs,
                          out_shape=out_shape, compiler_params=_params(("parallel",)))(*args)


def _ln_bwd(cfg, up, xhat, rstd, y, mod, gate_idx, coef, lng, name):
    tm = cfg.tm
    fused = len(up) > 1
    scale_next = up[4] if fused else None

    def body(*refs):
        if fused:
            dres_n, dxin_n, b_ref, modn_ref, xhat_ref, rstd_ref, y_ref, mod_ref, g_ref, dres_ref, dys_ref, st_ref = refs
        else:
            dhn_ref, xhat_ref, rstd_ref, y_ref, mod_ref, g_ref, dres_ref, dys_ref, st_ref = refs
        i = pl.program_id(0)

        @pl.when(cfg.first_of_seg(i))
        def _():
            st_ref[...] = jnp.zeros_like(st_ref)

        xhat = xhat_ref[...].astype(F32)
        if fused:
            dxin = dxin_n[...].astype(F32)
            sc = modn_ref[scale_next:scale_next + 1, :]
            dhn = dres_n[...] + dxin * (1.0 + sc)
            shift_sum = jnp.sum(dxin, axis=0, keepdims=True)
            st_ref[3:4, :] += g_ref[...] * jnp.sum(dxin * xhat, axis=0, keepdims=True) + b_ref[...] * shift_sum
            st_ref[4:5, :] += shift_sum
        else:
            dhn = dhn_ref[...]
        gdh = dhn * g_ref[...]
        m1 = jnp.mean(gdh, axis=-1, keepdims=True)
        m2 = jnp.mean(gdh * xhat, axis=-1, keepdims=True)
        dz = rstd_ref[...] * (gdh - m1 - xhat * m2)
        gate = mod_ref[gate_idx:gate_idx + 1, :]
        dres_ref[...] = ALPHA * dz
        dys_ref[...] = ((coef * gate) * dz).astype(BF16)
        st_ref[0:1, :] += jnp.sum(dhn * xhat, axis=0, keepdims=True)
        st_ref[1:2, :] += jnp.sum(dhn, axis=0, keepdims=True)
        st_ref[2:3, :] += jnp.sum((coef * dz) * y_ref[...].astype(F32), axis=0, keepdims=True)

    row = pl.BlockSpec((tm, D), lambda i: (i, 0))
    modspec = pl.BlockSpec((None, N_MOD, D), lambda i: (cfg.seg(i), 0, 0))
    vec = pl.BlockSpec((1, D), lambda i: (0, 0))
    col = pl.BlockSpec((tm, 1), lambda i: (i, 0))
    if fused:
        in_specs = [row, row, vec, modspec, row, col, row, modspec, vec]
        args = [up[0], up[1], up[2], up[3], xhat, rstd, y, mod, lng]
    else:
        in_specs = [row, row, col, row, modspec, vec]
        args = [up[0], xhat, rstd, y, mod, lng]
    return pl.pallas_call(
        body, grid=(cfg.nt,), name=name, in_specs=in_specs,
        out_specs=[row, row, pl.BlockSpec((None, 8, D), lambda i: (cfg.seg(i), 0, 0))],
        out_shape=[jax.ShapeDtypeStruct((cfg.T, D), F32), jax.ShapeDtypeStruct((cfg.T, D), BF16),
                   jax.ShapeDtypeStruct((3, 8, D), F32)],
        compiler_params=_params(("arbitrary",)))(*args)


def _modulate_bwd(cfg, dres, dxin, h, mod, scale_idx, name):
    tm = cfg.tm
    n_lt = 2 * cfg.nlt

    def body(dres_ref, dxin_ref, h_ref, mod_ref, dh_ref, st_ref):
        i = pl.program_id(0)

        @pl.when(cfg.first_of_seg(i))
        def _():
            st_ref[...] = jnp.zeros_like(st_ref)

        dxin = dxin_ref[...].astype(F32)
        sc = mod_ref[scale_idx:scale_idx + 1, :]

        @pl.when(i < n_lt)
        def _():
            dh_ref[...] = dres_ref[...] + dxin * (1.0 + sc)

        st_ref[3:4, :] += jnp.sum(dxin * h_ref[...], axis=0, keepdims=True)
        st_ref[4:5, :] += jnp.sum(dxin, axis=0, keepdims=True)

    row = pl.BlockSpec((tm, D), lambda i: (i, 0))
    return pl.pallas_call(
        body, grid=(cfg.nt,), name=name,
        in_specs=[row, row, row, pl.BlockSpec((None, N_MOD, D), lambda i: (cfg.seg(i), 0, 0))],
        out_specs=[pl.BlockSpec((tm, D), lambda i: (jnp.minimum(i, n_lt - 1), 0)),
                   pl.BlockSpec((None, 8, D), lambda i: (cfg.seg(i), 0, 0))],
        out_shape=[jax.ShapeDtypeStruct((cfg.t_lat, D), F32), jax.ShapeDtypeStruct((3, 8, D), F32)],
        compiler_params=_params(("arbitrary",)))(dres, dxin, h, mod)


def _loss(cfg, h, target, name):
    tm = cfg.tm
    n_lt = 2 * cfg.nlt

    def body(h_ref, t_ref, dy_ref, l_ref):
        i = pl.program_id(0)

        @pl.when(i == 0)
        def _():
            l_ref[...] = jnp.zeros_like(l_ref)

        @pl.when(i < n_lt)
        def _():
            err = h_ref[...] - t_ref[...]
            dy_ref[...] = err * (1.0 / D)
            part = jnp.sum(jnp.sum(err * err, axis=1, keepdims=True), axis=0, keepdims=True) * (0.5 / D)
            l_ref[...] += jnp.broadcast_to(part, l_ref.shape)

        @pl.when(i >= n_lt)
        def _():
            dy_ref[...] = jnp.zeros_like(dy_ref)

    return pl.pallas_call(
        body, grid=(cfg.nt,), name=name,
        in_specs=[pl.BlockSpec((tm, D), lambda i: (i, 0)),
                  pl.BlockSpec((tm, D), lambda i: (jnp.minimum(i, n_lt - 1), 0))],
        out_specs=[pl.BlockSpec((tm, D), lambda i: (i, 0)), pl.BlockSpec((8, 128), lambda i: (0, 0))],
        out_shape=[jax.ShapeDtypeStruct((cfg.T, D), F32), jax.ShapeDtypeStruct((8, 128), F32)],
        compiler_params=_params(("arbitrary",)))(h, target)


def _matmul(a, b, mode, out_dtype, name, bm_cap=1536, bn_cap=1408, bk_cap=1024):
    if mode == "nn":
        (M, K), N = a.shape, b.shape[1]
    elif mode == "nt":
        (M, K), N = a.shape, b.shape[0]
    else:
        (K, M), N = a.shape, b.shape[1]
    bm, bn, bk = _pick(M, bm_cap), _pick(N, bn_cap), _pick(K, bk_cap)
    nk = K // bk

    def body(a_ref, b_ref, o_ref, acc_ref=None):
        k = pl.program_id(2)
        if mode == "nn":
            part = _dot(a_ref[...], b_ref[...])
        elif mode == "nt":
            part = _dot_nt(a_ref[...], b_ref[...])
        else:
            part = _dot_tn(a_ref[...], b_ref[...])
        if nk == 1:
            o_ref[...] = part.astype(out_dtype)
            return

        @pl.when(k == 0)
        def _():
            acc_ref[...] = part

        @pl.when((k > 0) & (k < nk - 1))
        def _():
            acc_ref[...] += part

        @pl.when(k == nk - 1)
        def _():
            o_ref[...] = (acc_ref[...] + part).astype(out_dtype)

    if mode == "nn":
        a_spec = pl.BlockSpec((bm, bk), lambda i, j, k: (i, k))
        b_spec = pl.BlockSpec((bk, bn), lambda i, j, k: (k, j))
    elif mode == "nt":
        a_spec = pl.BlockSpec((bm, bk), lambda i, j, k: (i, k))
        b_spec = pl.BlockSpec((bn, bk), lambda i, j, k: (j, k))
    else:
        a_spec = pl.BlockSpec((bk, bm), lambda i, j, k: (k, i))
        b_spec = pl.BlockSpec((bk, bn), lambda i, j, k: (k, j))
    return pl.pallas_call(
        body, grid=(M // bm, N // bn, nk), name=name, in_specs=[a_spec, b_spec],
        out_specs=pl.BlockSpec((bm, bn), lambda i, j, k: (i, j)),
        out_shape=jax.ShapeDtypeStruct((M, N), out_dtype),
        scratch_shapes=[pltpu.VMEM((bm, bn), F32)] if nk > 1 else [],
        compiler_params=_params(("parallel", "parallel", "arbitrary")))(a, b)


def _ffn_tile(T, cap):
    best = 256
    for t in range(256, cap + 1, 256):
        if T % t == 0:
            best = t
    return best


def _ffn_fwd(xin, wf, name):
    T = xin.shape[0]
    F = wf.shape[1]
    tm, tf = _ffn_tile(T, 768), F // 2
    assert tf % 128 == 0 and T % tm == 0

    def body(x_ref, wg_ref, wu_ref, wd_ref, g_ref, u_ref, y_ref, acc_ref):
        j = pl.program_id(1)
        x = x_ref[...]
        acc = None
        for c0, cw in _chunks(tf):
            g = _dot_nt(x, wg_ref[c0:c0 + cw, :])
            u = _dot_nt(x, wu_ref[c0:c0 + cw, :])
            g_ref[:, c0:c0 + cw] = g.astype(BF16)
            u_ref[:, c0:c0 + cw] = u.astype(BF16)
            part = _dot(g * _sigmoid(g) * u, wd_ref[c0:c0 + cw, :])
            acc = part if acc is None else acc + part

        @pl.when(j == 0)
        def _():
            acc_ref[...] = acc

        @pl.when(j == 1)
        def _():
            y_ref[...] = (acc_ref[...] + acc).astype(BF16)

    return pl.pallas_call(
        body, grid=(T // tm, 2), name=name,
        in_specs=[pl.BlockSpec((tm, D), lambda i, j: (i, 0)),
                  pl.BlockSpec((None, tf, D), lambda i, j: (0, j, 0)),
                  pl.BlockSpec((None, tf, D), lambda i, j: (1, j, 0)),
                  pl.BlockSpec((None, tf, D), lambda i, j: (2, j, 0))],
        out_specs=[pl.BlockSpec((tm, tf), lambda i, j: (i, j)),
                   pl.BlockSpec((tm, tf), lambda i, j: (i, j)),
                   pl.BlockSpec((tm, D), lambda i, j: (i, 0))],
        out_shape=[jax.ShapeDtypeStruct((T, F), BF16), jax.ShapeDtypeStruct((T, F), BF16),
                   jax.ShapeDtypeStruct((T, D), BF16)],
        scratch_shapes=[pltpu.VMEM((tm, D), F32)],
        compiler_params=_params(("parallel", "arbitrary")))(xin, wf, wf, wf)


def _ffn_bwd(dys, g, u, wf, name):
    T = dys.shape[0]
    F = wf.shape[1]
    tm, tf = _ffn_tile(T, 512), F // 2

    def body(dy_ref, g_ref, u_ref, wg_ref, wu_ref, wd_ref, dg_ref, du_ref, a_ref, dx_ref, acc_ref):
        j = pl.program_id(1)
        da_all = _dot_nt(dy_ref[...], wd_ref[...])
        for c0, cw in _chunks(tf):
            gg = g_ref[:, c0:c0 + cw].astype(F32)
            uu = u_ref[:, c0:c0 + cw].astype(F32)
            da = da_all[:, c0:c0 + cw]
            s = _sigmoid(gg)
            silu = gg * s
            a_ref[:, c0:c0 + cw] = (silu * uu).astype(BF16)
            du_ref[:, c0:c0 + cw] = (da * silu).astype(BF16)
            dg_ref[:, c0:c0 + cw] = (da * uu * (s * (1.0 + gg * (1.0 - s)))).astype(BF16)
        acc = _dot(dg_ref[...], wg_ref[...]) + _dot(du_ref[...], wu_ref[...])

        @pl.when(j == 0)
        def _():
            acc_ref[...] = acc

        @pl.when(j == 1)
        def _():
            dx_ref[...] = (acc_ref[...] + acc).astype(BF16)

    blk = pl.BlockSpec((tm, tf), lambda i, j: (i, j))
    return pl.pallas_call(
        body, grid=(T // tm, 2), name=name,
        in_specs=[pl.BlockSpec((tm, D), lambda i, j: (i, 0)), blk, blk,
                  pl.BlockSpec((None, tf, D), lambda i, j: (0, j, 0)),
                  pl.BlockSpec((None, tf, D), lambda i, j: (1, j, 0)),
                  pl.BlockSpec((None, tf, D), lambda i, j: (2, j, 0))],
        out_specs=[blk, blk, blk, pl.BlockSpec((tm, D), lambda i, j: (i, 0))],
        out_shape=[jax.ShapeDtypeStruct((T, F), BF16), jax.ShapeDtypeStruct((T, F), BF16),
                   jax.ShapeDtypeStruct((T, F), BF16), jax.ShapeDtypeStruct((T, D), BF16)],
        scratch_shapes=[pltpu.VMEM((tm, D), F32)],
        compiler_params=_params(("parallel", "arbitrary")))(dys, g, u, wf, wf, wf)


def _swap_halves(x):
    w = x.shape[1]
    lane = lax.broadcasted_iota(jnp.int32, (1, w), 1)
    return jnp.where((lane & 63) < 32, pltpu.roll(x, w - 32, 1), pltpu.roll(x, 32, 1))


def _rope(x, cos, sin):
    return x * cos + _swap_halves(x) * sin


def _rope_t(dy, cos, sin):
    return dy * cos + _swap_halves(dy * sin)


def _rope_tables(n_lat):
    rows = n_lat // GRID_W
    row = jnp.repeat(jnp.arange(rows, dtype=F32), GRID_W)
    col = jnp.tile(jnp.arange(GRID_W, dtype=F32), rows)
    inv = ROPE_THETA ** (-jnp.arange(ROPE_FREQS, dtype=F32) / ROPE_FREQS)
    ang = jnp.concatenate([row[:, None] * inv, col[:, None] * inv], axis=-1)
    cs, sn = jnp.cos(ang), jnp.sin(ang)
    cos = jnp.concatenate([cs, cs, cs, cs], axis=-1)
    sin = jnp.concatenate([-sn, sn, -sn, sn], axis=-1)
    return cos, sin


def _attn_specs(cfg):
    n_lat, n_ctx, cb = cfg.n_lat, cfg.n_ctx, cfg.ctx_blk
    return [pl.BlockSpec((n_lat, ATT_W), lambda e: (e, 0)),
            pl.BlockSpec((n_lat, 128), lambda e: (e, 4)),
            pl.BlockSpec((n_lat, 128), lambda e: (e, 5)),
            pl.BlockSpec((n_ctx, ATT_W), lambda e: (cb + e, 0)),
            pl.BlockSpec((n_ctx, 128), lambda e: (cb + e, 4)),
            pl.BlockSpec((n_ctx, 128), lambda e: (cb + e, 5)),
            pl.BlockSpec((n_lat, 128), lambda e: (0, 0)),
            pl.BlockSpec((n_lat, 128), lambda e: (0, 0)),
            pl.BlockSpec((8, 128), lambda e: (0, 0))]


def _attn_prepare(kh, kl, vl, kc, vc, ka, kb, va, vb, kca, kcb, vca, vcb):
    lane = lax.broadcasted_iota(jnp.int32, (1, 128), 1)
    own = (lane < 64) if kh == 0 else (lane >= 64)

    def split(x, ra, rb):
        mine = jnp.where(own, x, 0.0)
        other = pltpu.roll(mine, 64, 1)
        a, b = (mine, other) if kh == 0 else (other, mine)
        ra[...] = a.astype(BF16)
        rb[...] = b.astype(BF16)

    split(kl, ka, kb)
    split(vl, va, vb)
    split(kc, kca, kcb)
    split(vc, vca, vcb)


def _softmax_parts(s_list, sk):
    m = sk
    for s in s_list:
        m = jnp.maximum(m, jnp.max(s, axis=1, keepdims=True))
    es = [jnp.exp(s - m) for s in s_list]
    esk = jnp.exp(sk - m)
    den = esk
    for e in es:
        den = den + jnp.sum(e, axis=1, keepdims=True)
    inv = 1.0 / den
    return [e * inv for e in es], esk * inv


def _window(cfg, n):
    r0 = pl.multiple_of(n * BLK, BLK)
    start = pl.multiple_of(jnp.clip((n - 1) * BLK, 0, cfg.n_lat - 3 * BLK), BLK)
    qpos = r0 + lax.broadcasted_iota(jnp.int32, (BLK, 1), 0)
    kpos = start + lax.broadcasted_iota(jnp.int32, (1, 3 * BLK), 1)
    valid = jnp.abs(qpos - kpos) <= BLK
    return r0, start, valid


def _attn_fwd(cfg, p, cos, sin, sink_rows, name):
    n_lat, n_ctx = cfg.n_lat, cfg.n_ctx

    def body(q_ref, k_ref, v_ref, qc_ref, kc_ref, vc_ref, cos_ref, sin_ref, sink_ref, o_ref, oc_ref,
             qr, ka, kb, va, vb, kca, kcb, vca, vcb):
        cos_t, sin_t = cos_ref[...], sin_ref[...]
        for gq in range(4):
            qr[:, gq * 128:(gq + 1) * 128] = _rope(q_ref[:, gq * 128:(gq + 1) * 128].astype(F32), cos_t, sin_t).astype(BF16)
        kl = _rope(k_ref[...].astype(F32), cos_t, sin_t)
        for kh in range(KV_HEADS):
            _attn_prepare(kh, kl, v_ref[...].astype(F32), kc_ref[...].astype(F32), vc_ref[...].astype(F32),
                          ka, kb, va, vb, kca, kcb, vca, vcb)

            def lat_block(n, carry):
                r0, start, valid = _window(cfg, n)
                win = pl.ds(start, 3 * BLK)
                lanes = [slice((kh * 2 + pr) * 128, (kh * 2 + pr + 1) * 128) for pr in range(2)]
                qps = [qr[pl.ds(r0, BLK), lanes[pr]] for pr in range(2)]
                kws, kcs = (ka[win, :], kb[win, :]), (kca[...], kcb[...])
                scores = [(jnp.where(valid, _dot_nt(qps[pr], kws[half]) * ATT_SCALE, NEG_INF),
                           _dot_nt(qps[pr], kcs[half]) * ATT_SCALE) for pr in range(2) for half in range(2)]
                probs = []
                for idx, (s_w, s_c) in enumerate(scores):
                    head = kh * 4 + idx
                    (p_w, p_c), _ = _softmax_parts([s_w, s_c], sink_ref[head:head + 1, 0:1])
                    probs.append((p_w.astype(BF16), p_c.astype(BF16)))
                vws, vcs = (va[win, :], vb[win, :]), (vca[...], vcb[...])
                for pr in range(2):
                    o = (_dot(probs[2 * pr][0], vws[0]) + _dot(probs[2 * pr][1], vcs[0])
                         + _dot(probs[2 * pr + 1][0], vws[1]) + _dot(probs[2 * pr + 1][1], vcs[1]))
                    o_ref[pl.ds(r0, BLK), lanes[pr]] = o.astype(BF16)
                return carry

            lax.fori_loop(0, n_lat // BLK, lat_block, 0, unroll=2)
            for n in range(n_ctx // BLK):
                rows = slice(n * BLK, (n + 1) * BLK)
                for pr in range(2):
                    lanes = slice((kh * 2 + pr) * 128, (kh * 2 + pr + 1) * 128)
                    qp = qc_ref[rows, lanes]
                    o = None
                    for half, (kcx, vcx) in enumerate(((kca, vca), (kcb, vcb))):
                        head = kh * 4 + pr * 2 + half
                        s_c = _dot_nt(qp, kcx[...]) * ATT_SCALE
                        (p_c,), _ = _softmax_parts([s_c], sink_ref[head:head + 1, 0:1])
                        part = _dot(p_c, vcx[...])
                        o = part if o is None else o + part
                    oc_ref[rows, lanes] = o.astype(BF16)

    return pl.pallas_call(
        body, grid=(2,), name=name, in_specs=_attn_specs(cfg),
        out_specs=[pl.BlockSpec((n_lat, ATT_W), lambda e: (e, 0)), pl.BlockSpec((n_ctx, ATT_W), lambda e: (e, 0))],
        out_shape=[jax.ShapeDtypeStruct((cfg.t_lat, ATT_W), BF16), jax.ShapeDtypeStruct((cfg.t_ctx, ATT_W), BF16)],
        scratch_shapes=[pltpu.VMEM((n_lat, ATT_W), BF16)] + [pltpu.VMEM((n_lat, 128), BF16)] * 4
        + [pltpu.VMEM((n_ctx, 128), BF16)] * 4,
        compiler_params=_params(("parallel",)))(p, p, p, p, p, p, cos, sin, sink_rows)


def _attn_bwd(cfg, p, dcat, cos, sin, sink_rows, name):
    n_lat, n_ctx, cb = cfg.n_lat, cfg.n_ctx, cfg.ctx_blk

    def body(q_ref, k_ref, v_ref, qc_ref, kc_ref, vc_ref, cos_ref, sin_ref, sink_ref, do_ref, doc_ref,
             dq_ref, dk_ref, dv_ref, dqc_ref, dkc_ref, dvc_ref, dsink_ref,
             qr, ka, kb, va, vb, kca, kcb, vca, vcb, dqs, dka, dva, dkca, dvca):
        cos_t, sin_t = cos_ref[...], sin_ref[...]
        lane = lax.broadcasted_iota(jnp.int32, (1, 128), 1)
        lo = lane < 64
        for gq in range(4):
            qr[:, gq * 128:(gq + 1) * 128] = _rope(q_ref[:, gq * 128:(gq + 1) * 128].astype(F32), cos_t, sin_t).astype(BF16)
        kl = _rope(k_ref[...].astype(F32), cos_t, sin_t)
        dsink_ref[...] = jnp.zeros_like(dsink_ref)
        dka[...] = jnp.zeros_like(dka)
        dva[...] = jnp.zeros_like(dva)
        dkca[...] = jnp.zeros_like(dkca)
        dvca[...] = jnp.zeros_like(dvca)

        def halves(x):
            return jnp.where(lo, x, 0).astype(BF16), jnp.where(lo, 0, x).astype(BF16)

        for kh in range(KV_HEADS):
            _attn_prepare(kh, kl, v_ref[...].astype(F32), kc_ref[...].astype(F32), vc_ref[...].astype(F32),
                          ka, kb, va, vb, kca, kcb, vca, vcb)

            def one_head(head, qp, q_half, do_p, do_half, kw, kcx, vw, vcx, win, valid):
                sk = sink_ref[head:head + 1, 0:1]
                s_list = [_dot_nt(qp, kcx[...]) * ATT_SCALE]
                if win is not None:
                    s_list.insert(0, jnp.where(valid, _dot_nt(qp, kw[win, :]) * ATT_SCALE, NEG_INF))
                probs, p_sink = _softmax_parts(s_list, sk)
                vals = [vcx[...]] if win is None else [vw[win, :], vcx[...]]
                dps = [_dot_nt(do_p, vv) for vv in vals]
                dr = None
                for pp, dp in zip(probs, dps):
                    t = jnp.sum(pp * dp, axis=1, keepdims=True)
                    dr = t if dr is None else dr + t
                dss = [(pp * (dp - dr) * ATT_SCALE).astype(BF16) for pp, dp in zip(probs, dps)]
                dsink_ref[head:head + 1, :] += jnp.broadcast_to(
                    jnp.sum(-p_sink * dr, axis=0, keepdims=True), (1, 128))
                p_c, ds_c = probs[-1], dss[-1]
                dq = _dot(ds_c, kcx[...])
                dkca[kh] += _dot_tn(ds_c, q_half)
                dvca[kh] += _dot_tn(p_c, do_half)
                if win is not None:
                    dq = dq + _dot(dss[0], kw[win, :])
                    dka[kh, win, :] += _dot_tn(dss[0], q_half)
                    dva[kh, win, :] += _dot_tn(probs[0], do_half)
                return dq

            def lat_block(n, carry):
                r0, start, valid = _window(cfg, n)
                win = pl.ds(start, 3 * BLK)
                lanes = [slice((kh * 2 + pr) * 128, (kh * 2 + pr + 1) * 128) for pr in range(2)]
                qps = [qr[pl.ds(r0, BLK), lanes[pr]] for pr in range(2)]
                dops = [do_ref[pl.ds(r0, BLK), lanes[pr]].astype(BF16) for pr in range(2)]
                heads = [(pr, half) for pr in range(2) for half in range(2)]
                kws, kcs = (ka[win, :], kb[win, :]), (kca[...], kcb[...])
                vws, vcs = (va[win, :], vb[win, :]), (vca[...], vcb[...])
                soft = []
                for idx, (pr, half) in enumerate(heads):
                    s_w = jnp.where(valid, _dot_nt(qps[pr], kws[half]) * ATT_SCALE, NEG_INF)
                    s_c = _dot_nt(qps[pr], kcs[half]) * ATT_SCALE
                    soft.append(_softmax_parts([s_w, s_c], sink_ref[kh * 4 + idx:kh * 4 + idx + 1, 0:1]))
                dps = [(_dot_nt(dops[pr], vws[half]), _dot_nt(dops[pr], vcs[half])) for pr, half in heads]
                ds_w, ds_c, pb_w, pb_c = [], [], [], []
                for idx in range(4):
                    (p_w, p_c), p_sink = soft[idx]
                    dp_w, dp_c = dps[idx]
                    dr = jnp.sum(p_w * dp_w, axis=1, keepdims=True) + jnp.sum(p_c * dp_c, axis=1, keepdims=True)
                    ds_w.append((p_w * (dp_w - dr) * ATT_SCALE).astype(BF16))
                    ds_c.append((p_c * (dp_c - dr) * ATT_SCALE).astype(BF16))
                    pb_w.append(p_w.astype(BF16))
                    pb_c.append(p_c.astype(BF16))
                    head = kh * 4 + idx
                    dsink_ref[head:head + 1, :] += jnp.broadcast_to(
                        jnp.sum(-p_sink * dr, axis=0, keepdims=True), (1, 128))
                for pr in range(2):
                    dqs[pl.ds(r0, BLK), lanes[pr]] = (
                        _dot(ds_w[2 * pr], kws[0]) + _dot(ds_c[2 * pr], kcs[0])
                        + _dot(ds_w[2 * pr + 1], kws[1]) + _dot(ds_c[2 * pr + 1], kcs[1]))
                q_hs, do_hs = [halves(qp) for qp in qps], [halves(do_p) for do_p in dops]
                q_stack = jnp.concatenate([q_hs[pr][half] for pr, half in heads], axis=0)
                do_stack = jnp.concatenate([do_hs[pr][half] for pr, half in heads], axis=0)
                dka[kh, win, :] += _dot_tn(jnp.concatenate(ds_w, axis=0), q_stack)
                dva[kh, win, :] += _dot_tn(jnp.concatenate(pb_w, axis=0), do_stack)
                dkca[kh] += _dot_tn(jnp.concatenate(ds_c, axis=0), q_stack)
                dvca[kh] += _dot_tn(jnp.concatenate(pb_c, axis=0), do_stack)
                return carry

            lax.fori_loop(0, n_lat // BLK, lat_block, 0, unroll=2)
            for n in range(n_ctx // BLK):
                rows = slice(n * BLK, (n + 1) * BLK)
                for pr in range(2):
                    lanes = slice((kh * 2 + pr) * 128, (kh * 2 + pr + 1) * 128)
                    qp = qc_ref[rows, lanes].astype(BF16)
                    do_p = doc_ref[rows, lanes]
                    q_h, do_h = halves(qp), halves(do_p)
                    dq = None
                    for half, (kcx, vcx) in enumerate(((kca, vca), (kcb, vcb))):
                        part = one_head(kh * 4 + pr * 2 + half, qp, q_h[half], do_p, do_h[half],
                                        None, kcx, None, vcx, None, None)
                        dq = part if dq is None else dq + part
                    dqc_ref[rows, lanes] = dq.astype(BF16)

        def fold(acc):
            r0 = acc[0] + pltpu.roll(acc[0], 64, 1)
            r1 = acc[1] + pltpu.roll(acc[1], 64, 1)
            return jnp.where(lo, r0, r1)

        for gq in range(4):
            sl = slice(gq * 128, (gq + 1) * 128)
            dq_ref[:, sl] = _rope_t(dqs[:, sl], cos_t, sin_t).astype(BF16)
        dk_ref[...] = _rope_t(fold(dka), cos_t, sin_t).astype(BF16)
        dv_ref[...] = fold(dva).astype(BF16)
        dkc_ref[...] = fold(dkca).astype(BF16)
        dvc_ref[...] = fold(dvca).astype(BF16)

    lat = lambda w: pl.BlockSpec((n_lat, w), lambda e: (e, 0))
    ctx = lambda w: pl.BlockSpec((n_ctx, w), lambda e: (e, 0))
    sd = jax.ShapeDtypeStruct
    return pl.pallas_call(
        body, grid=(2,), name=name,
        in_specs=_attn_specs(cfg) + [pl.BlockSpec((n_lat, ATT_W), lambda e: (e, 0)),
                                     pl.BlockSpec((n_ctx, ATT_W), lambda e: (cb + e, 0))],
        out_specs=[lat(ATT_W), lat(128), lat(128), ctx(ATT_W), ctx(128), ctx(128),
                   pl.BlockSpec((None, 8, 128), lambda e: (e, 0, 0))],
        out_shape=[sd((cfg.t_lat, ATT_W), BF16), sd((cfg.t_lat, 128), BF16), sd((cfg.t_lat, 128), BF16),
                   sd((cfg.t_ctx, ATT_W), BF16), sd((cfg.t_ctx, 128), BF16), sd((cfg.t_ctx, 128), BF16),
                   sd((2, 8, 128), F32)],
        scratch_shapes=[pltpu.VMEM((n_lat, ATT_W), BF16)] + [pltpu.VMEM((n_lat, 128), BF16)] * 4
        + [pltpu.VMEM((n_ctx, 128), BF16)] * 4
        + [pltpu.VMEM((n_lat, ATT_W), F32), pltpu.VMEM((2, n_lat, 128), F32), pltpu.VMEM((2, n_lat, 128), F32),
           pltpu.VMEM((2, n_ctx, 128), F32), pltpu.VMEM((2, n_ctx, 128), F32)],
        compiler_params=_params(("parallel",)))(p, p, p, p, p, p, cos, sin, sink_rows, dcat, dcat)


def _shift_down(x, k, row):
    return jnp.where(row >= k, pltpu.roll(x, k, 0), 0.0)


def _shift_up(x, k, row):
    n = x.shape[0]
    return jnp.where(row < n - k, pltpu.roll(x, n - k, 0), 0.0)


def _window_sum(x, r, row):
    below, above, k = x, x, 1
    while k < r:
        below = below + _shift_down(below, k, row)
        above = above + _shift_up(above, k, row)
        k *= 2
    return below + _shift_down(x, r, row) + _shift_up(above, 1, row)


def _inv_count(r, row, n):
    cnt = jnp.minimum(row + r, n - 1) + 1 - jnp.maximum(row - r, 0)
    return 1.0 / cnt.astype(F32)


def _pool_fwd(p, w, scale, n, blk0, n_seg, name):
    def body(u0, u1, u2, u3, w_ref, sc_ref, o_ref):
        row = lax.broadcasted_iota(jnp.int32, (n, 1), 0)
        for g, u_ref in enumerate((u0, u1, u2, u3)):
            u = u_ref[...].astype(F32)
            d = _window_sum(u, POOL_R[g], row) * _inv_count(POOL_R[g], row, n) - u
            o_ref[:, g * 128:(g + 1) * 128] = (_dot(d, w_ref[g]) * sc_ref[:, g * 128:(g + 1) * 128]).astype(BF16)

    return pl.pallas_call(
        body, grid=(n_seg,), name=name,
        in_specs=[pl.BlockSpec((n, 128), functools.partial(lambda g, e: (blk0 + e, 6 + g), g)) for g in range(4)]
        + [pl.BlockSpec((4, 128, 128), lambda e: (0, 0, 0)), pl.BlockSpec((1, 512), lambda e: (0, 0))],
        out_specs=pl.BlockSpec((n, 512), lambda e: (e, 0)),
        out_shape=jax.ShapeDtypeStruct((n_seg * n, 512), BF16),
        compiler_params=_params(("parallel",)))(p, p, p, p, w, scale)


def _pool_bwd(p, w, scale, dcat, n, blk0, n_seg, name):
    def body(u0, u1, u2, u3, w_ref, sc_ref, dp_ref, du_ref, dw_ref, dsc_ref):
        e = pl.program_id(0)

        @pl.when(e == 0)
        def _():
            dw_ref[...] = jnp.zeros_like(dw_ref)
            dsc_ref[...] = jnp.zeros_like(dsc_ref)

        row = lax.broadcasted_iota(jnp.int32, (n, 1), 0)
        for g, u_ref in enumerate((u0, u1, u2, u3)):
            sl = slice(g * 128, (g + 1) * 128)
            u = u_ref[...].astype(F32)
            inv = _inv_count(POOL_R[g], row, n)
            d = _window_sum(u, POOL_R[g], row) * inv - u
            dp = dp_ref[:, sl].astype(F32)
            dsc_ref[:, sl] += jnp.sum(dp * _dot(d, w_ref[g]), axis=0, keepdims=True)
            dyp = dp * sc_ref[:, sl]
            dw_ref[g] += _dot_tn(d, dyp)
            dd = _dot_nt(dyp, w_ref[g])
            du_ref[:, sl] = (_window_sum(dd * inv, POOL_R[g], row) - dd).astype(BF16)

    return pl.pallas_call(
        body, grid=(n_seg,), name=name,
        in_specs=[pl.BlockSpec((n, 128), functools.partial(lambda g, e: (blk0 + e, 6 + g), g)) for g in range(4)]
        + [pl.BlockSpec((4, 128, 128), lambda e: (0, 0, 0)), pl.BlockSpec((1, 512), lambda e: (0, 0)),
           pl.BlockSpec((n, 512), lambda e: (blk0 + e, 1))],
        out_specs=[pl.BlockSpec((n, 512), lambda e: (e, 0)),
                   pl.BlockSpec((4, 128, 128), lambda e: (0, 0, 0)), pl.BlockSpec((1, 512), lambda e: (0, 0))],
        out_shape=[jax.ShapeDtypeStruct((n_seg * n, 512), BF16), jax.ShapeDtypeStruct((4, 128, 128), F32),
                   jax.ShapeDtypeStruct((1, 512), F32)],
        compiler_params=_params(("arbitrary",)))(p, p, p, p, w, scale, dcat)


def _gelu(x):
    t = jnp.tanh(math.sqrt(2.0 / math.pi) * (x + 0.044715 * x * x * x))
    return 0.5 * x * (1.0 + t), t


def _gelu_grad(x, t):
    return 0.5 * (1.0 + t) + 0.5 * x * (1.0 - t * t) * (math.sqrt(2.0 / math.pi) * (1.0 + 3 * 0.044715 * x * x))


def _neg_expm1_twice(x):
    t = jnp.tanh(x)
    return (-2.0 * t) / (1.0 - t)


def _softplus_neg(lam):
    x = -lam
    e = jnp.exp(-jnp.abs(x))
    log1p = jnp.where(e < 1e-2, e * (1.0 - e * (0.5 - e * (1.0 / 3.0))), jnp.log(1.0 + e))
    return jnp.maximum(x, 0.0) + log1p, -_sigmoid(x)


def _conv(u, w_ref, b_ref, row):
    return (b_ref[...] + _shift_down(u, 1, row) * w_ref[0:1, :] + u * w_ref[1:2, :]
            + _shift_up(u, 1, row) * w_ref[2:3, :] + _shift_up(u, 2, row) * w_ref[3:4, :])


def _lru_gates(uc, d, wa_ref, ba_ref, wx_ref, bx_ref, lam_ref):
    r = _sigmoid(_dot(uc, wa_ref[d]) + ba_ref[d:d + 1, :])
    gi = _sigmoid(_dot(uc, wx_ref[d]) + bx_ref[d:d + 1, :])
    sp, dsp = _softplus_neg(lam_ref[d:d + 1, :])
    la = (-LRU_C) * r * sp
    a = jnp.exp(la)
    sq = jnp.sqrt(_neg_expm1_twice(la))
    return r, gi, sp, dsp, a, sq


def _tile_scan(a_ref, b_ref, n, reverse):
    m = n // 8
    first = 7 if reverse else 0
    a_prev = a_ref[pl.ds(first, m, stride=8), :]
    b_prev = b_ref[pl.ds(first, m, stride=8), :]
    for j in (range(6, -1, -1) if reverse else range(1, 8)):
        rows = pl.ds(j, m, stride=8)
        aj = a_ref[rows, :]
        b_prev = aj * b_prev + b_ref[rows, :]
        a_prev = aj * a_prev
        b_ref[rows, :] = b_prev
        a_ref[rows, :] = a_prev


def _carry_scan(a_ref, b_ref, n, reverse, carry):
    nt8 = n // 8

    def step(i, c):
        t = (nt8 - 1 - i) if reverse else i
        off = pl.multiple_of(t * 8, 8)
        h = a_ref[pl.ds(off, 8), :] * c + b_ref[pl.ds(off, 8), :]
        b_ref[pl.ds(off, 8), :] = h
        return h[0:1, :] if reverse else h[7:8, :]

    return lax.fori_loop(0, nt8, step, carry, unroll=4)


def _chain_scan(segs, reverse):
    carry = jnp.zeros((1, 128), F32)
    for a, b, a_ref, b_ref, n in segs:
        a_ref[...] = a
        b_ref[...] = b
        _tile_scan(a_ref, b_ref, n, reverse)
        carry = _carry_scan(a_ref, b_ref, n, reverse, carry)


def _lru_specs(cfg):
    n_lat, n_ctx, cb = cfg.n_lat, cfg.n_ctx, cfg.ctx_blk
    return [pl.BlockSpec((n_lat, 128), lambda hb, e: (e, hb)),
            pl.BlockSpec((n_lat, 128), lambda hb, e: (e, 8 + hb)),
            pl.BlockSpec((n_ctx, 128), lambda hb, e: (cb + e, hb)),
            pl.BlockSpec((n_ctx, 128), lambda hb, e: (cb + e, 8 + hb)),
            pl.BlockSpec((4, 128), lambda hb, e: (0, hb)),
            pl.BlockSpec((1, 128), lambda hb, e: (0, hb)),
            pl.BlockSpec((2, None, 128, 128), lambda hb, e: (0, hb, 0, 0)),
            pl.BlockSpec((2, 128), lambda hb, e: (0, hb)),
            pl.BlockSpec((2, None, 128, 128), lambda hb, e: (0, hb, 0, 0)),
            pl.BlockSpec((2, 128), lambda hb, e: (0, hb)),
            pl.BlockSpec((2, 128), lambda hb, e: (0, hb))]


def _lru_fwd(cfg, p, consts, name):
    n_lat, n_ctx = cfg.n_lat, cfg.n_ctx

    def body(gl_ref, ul_ref, gc_ref, uc_ref, cw_ref, cb_ref, wa_ref, ba_ref, wx_ref, bx_ref, lam_ref,
             zl_ref, zc_ref, hl_ref, hc_ref, al, ac):
        row_l = lax.broadcasted_iota(jnp.int32, (n_lat, 1), 0)
        row_c = lax.broadcasted_iota(jnp.int32, (n_ctx, 1), 0)
        uc_l = _conv(ul_ref[...].astype(F32), cw_ref, cb_ref, row_l)
        uc_c = _conv(uc_ref[...].astype(F32), cw_ref, cb_ref, row_c)
        for d in range(2):
            _, gi_l, _, _, a_l, sq_l = _lru_gates(uc_l, d, wa_ref, ba_ref, wx_ref, bx_ref, lam_ref)
            _, gi_c, _, _, a_c, sq_c = _lru_gates(uc_c, d, wa_ref, ba_ref, wx_ref, bx_ref, lam_ref)
            _chain_scan([(a_c, sq_c * (gi_c * uc_c), ac, hc_ref.at[d], n_ctx),
                         (a_l, sq_l * (gi_l * uc_l), al, hl_ref.at[d], n_lat)], reverse=(d == 1))
        zl_ref[...] = (_gelu(gl_ref[...].astype(F32))[0] * (hl_ref[0] + hl_ref[1])).astype(BF16)
        zc_ref[...] = (_gelu(gc_ref[...].astype(F32))[0] * (hc_ref[0] + hc_ref[1])).astype(BF16)

    return pl.pallas_call(
        body, grid=(8, 2), name=name, in_specs=_lru_specs(cfg),
        out_specs=[pl.BlockSpec((n_lat, 128), lambda hb, e: (e, hb)), pl.BlockSpec((n_ctx, 128), lambda hb, e: (e, hb)),
                   pl.BlockSpec((2, n_lat, 128), lambda hb, e: (0, e, hb)),
                   pl.BlockSpec((2, n_ctx, 128), lambda hb, e: (0, e, hb))],
        out_shape=[jax.ShapeDtypeStruct((cfg.t_lat, D), BF16), jax.ShapeDtypeStruct((cfg.t_ctx, D), BF16),
                   jax.ShapeDtypeStruct((2, cfg.t_lat, D), F32), jax.ShapeDtypeStruct((2, cfg.t_ctx, D), F32)],
        scratch_shapes=[pltpu.VMEM((n_lat, 128), F32), pltpu.VMEM((n_ctx, 128), F32)],
        compiler_params=_params(("parallel", "arbitrary")))(p, p, p, p, *consts)


def _lru_bwd(cfg, p, dz, h_lat, h_ctx, consts, name):
    n_lat, n_ctx, cb = cfg.n_lat, cfg.n_ctx, cfg.ctx_blk

    def body(gl_ref, ul_ref, gc_ref, uc_ref, cw_ref, cb_ref, wa_ref, ba_ref, wx_ref, bx_ref, lam_ref,
             dzl_ref, dzc_ref, hl, hc, dgl_ref, dul_ref, dgc_ref, duc_ref, dwa_ref, dwx_ref, vec_ref,
             al, bl, ac, bc):
        e = pl.program_id(1)

        @pl.when(e == 0)
        def _():
            dwa_ref[...] = jnp.zeros_like(dwa_ref)
            dwx_ref[...] = jnp.zeros_like(dwx_ref)
            vec_ref[...] = jnp.zeros_like(vec_ref)

        row_l = lax.broadcasted_iota(jnp.int32, (n_lat, 1), 0)
        row_c = lax.broadcasted_iota(jnp.int32, (n_ctx, 1), 0)
        u_l, u_c = ul_ref[...].astype(F32), uc_ref[...].astype(F32)
        uc_l = _conv(u_l, cw_ref, cb_ref, row_l)
        uc_c = _conv(u_c, cw_ref, cb_ref, row_c)
        gel_l, t_l = _gelu(gl_ref[...].astype(F32))
        gel_c, t_c = _gelu(gc_ref[...].astype(F32))
        dz_l, dz_c = dzl_ref[...].astype(F32), dzc_ref[...].astype(F32)
        dgl_ref[...] = (dz_l * (hl[0] + hl[1]) * _gelu_grad(gl_ref[...].astype(F32), t_l)).astype(BF16)
        dgc_ref[...] = (dz_c * (hc[0] + hc[1]) * _gelu_grad(gc_ref[...].astype(F32), t_c)).astype(BF16)
        dy_l, dy_c = dz_l * gel_l, dz_c * gel_c
        duc_l = jnp.zeros((n_lat, 128), F32)
        duc_c = jnp.zeros((n_ctx, 128), F32)
        for d in range(2):
            r_l, gi_l, sp, dsp, a_l, sq_l = _lru_gates(uc_l, d, wa_ref, ba_ref, wx_ref, bx_ref, lam_ref)
            r_c, gi_c, _, _, a_c, sq_c = _lru_gates(uc_c, d, wa_ref, ba_ref, wx_ref, bx_ref, lam_ref)
            if d == 0:
                an_l = _shift_up(a_l, 1, row_l)
                an_c = jnp.where(row_c < n_ctx - 1, pltpu.roll(a_c, n_ctx - 1, 0), a_l[0:1, :])
            else:
                an_l = _shift_down(a_l, 1, row_l)
                an_c = jnp.where(row_c >= 1, pltpu.roll(a_c, 1, 0), a_l[n_lat - 1:n_lat, :])
            _chain_scan([(an_l, dy_l, al, bl, n_lat), (an_c, dy_c, ac, bc, n_ctx)], reverse=(d == 0))
            dsp_sum = jnp.zeros((1, 128), F32)
            for (dh, h, r, gi, a, sq, uc, seg) in ((bl[...], hl[d], r_l, gi_l, a_l, sq_l, uc_l, "l"),
                                                  (bc[...], hc[d], r_c, gi_c, a_c, sq_c, uc_c, "c")):
                b0 = sq * (gi * uc)
                t1 = dh * sq
                dla = dh * (h - b0) - (dh * gi * uc) * (a * a) / sq
                dzr = (dla * ((-LRU_C) * sp)) * r * (1.0 - r)
                dzi = (t1 * uc) * gi * (1.0 - gi)
                dsp_sum = dsp_sum + jnp.sum(dla * ((-LRU_C) * r), axis=0, keepdims=True)
                dwa_ref[d] += _dot_tn(uc, dzr)
                dwx_ref[d] += _dot_tn(uc, dzi)
                vec_ref[d:d + 1, :] += jnp.sum(dzr, axis=0, keepdims=True)
                vec_ref[2 + d:3 + d, :] += jnp.sum(dzi, axis=0, keepdims=True)
                duc = t1 * gi + _dot_nt(dzr, wa_ref[d]) + _dot_nt(dzi, wx_ref[d])
                if seg == "l":
                    duc_l = duc_l + duc
                else:
                    duc_c = duc_c + duc
            vec_ref[4 + d:5 + d, :] += dsp_sum * dsp
        for duc, u, row, du_ref in ((duc_l, u_l, row_l, dul_ref), (duc_c, u_c, row_c, duc_ref)):
            du_ref[...] = (_shift_up(duc, 1, row) * cw_ref[0:1, :] + duc * cw_ref[1:2, :]
                           + _shift_down(duc, 1, row) * cw_ref[2:3, :]
                           + _shift_down(duc, 2, row) * cw_ref[3:4, :]).astype(BF16)
            vec_ref[6:7, :] += jnp.sum(duc * _shift_down(u, 1, row), axis=0, keepdims=True)
            vec_ref[7:8, :] += jnp.sum(duc * u, axis=0, keepdims=True)
            vec_ref[8:9, :] += jnp.sum(duc * _shift_up(u, 1, row), axis=0, keepdims=True)
            vec_ref[9:10, :] += jnp.sum(duc * _shift_up(u, 2, row), axis=0, keepdims=True)
            vec_ref[10:11, :] += jnp.sum(duc, axis=0, keepdims=True)

    lat = pl.BlockSpec((n_lat, 128), lambda hb, e: (e, hb))
    ctx = pl.BlockSpec((n_ctx, 128), lambda hb, e: (e, hb))
    wspec = pl.BlockSpec((2, None, 128, 128), lambda hb, e: (0, hb, 0, 0))
    sd = jax.ShapeDtypeStruct
    return pl.pallas_call(
        body, grid=(8, 2), name=name,
        in_specs=_lru_specs(cfg) + [pl.BlockSpec((n_lat, 128), lambda hb, e: (e, hb)),
                                    pl.BlockSpec((n_ctx, 128), lambda hb, e: (cb + e, hb)),
                                    pl.BlockSpec((2, n_lat, 128), lambda hb, e: (0, e, hb)),
                                    pl.BlockSpec((2, n_ctx, 128), lambda hb, e: (0, e, hb))],
        out_specs=[lat, lat, ctx, ctx, wspec, wspec, pl.BlockSpec((None, 16, 128), lambda hb, e: (hb, 0, 0))],
        out_shape=[sd((cfg.t_lat, D), BF16), sd((cfg.t_lat, D), BF16), sd((cfg.t_ctx, D), BF16), sd((cfg.t_ctx, D), BF16),
                   sd((2, 8, 128, 128), F32), sd((2, 8, 128, 128), F32), sd((8, 16, 128), F32)],
        scratch_shapes=[pltpu.VMEM((n_lat, 128), F32)] * 2 + [pltpu.VMEM((n_ctx, 128), F32)] * 2,
        compiler_params=_params(("parallel", "arbitrary")))(p, p, p, p, *consts, dz, dz, h_lat, h_ctx)


def _position():
    x, y, c = lax.axis_index("x"), lax.axis_index("y"), lax.axis_index("c")
    return x, y, c, 4 * x + 2 * y + c


def _peer(x, y, c, k):
    px = 1 - x if k & 4 else x
    py = 1 - y if k & 2 else y
    pc = 1 - c if k & 1 else c
    return (px, py, pc), 4 * px + 2 * py + pc


def _all_gather(v, name, in_vmem):
    def body(v_ref, o_ref, send_sems, recv_sems, local_sem):
        x, y, c, me = _position()
        mine = pltpu.make_async_copy(v_ref, o_ref.at[me], local_sem)
        mine.start()
        sends = []
        for k in range(1, N_DEV):
            peer, _ = _peer(x, y, c, k)
            cp = pltpu.make_async_remote_copy(src_ref=v_ref, dst_ref=o_ref.at[me], send_sem=send_sems.at[k - 1],
                                              recv_sem=recv_sems.at[k - 1], device_id=peer, device_id_type=MESH)
            cp.start()
            sends.append(cp)
        for k in range(1, N_DEV):
            peer, peer_lin = _peer(x, y, c, k)
            pltpu.make_async_remote_copy(src_ref=v_ref, dst_ref=o_ref.at[peer_lin], send_sem=send_sems.at[k - 1],
                                         recv_sem=recv_sems.at[k - 1], device_id=peer, device_id_type=MESH).wait_recv()
        for cp in sends:
            cp.wait_send()
        mine.wait()

    space = pltpu.VMEM if in_vmem else pl.ANY
    return pl.pallas_call(
        body, name=name,
        in_specs=[pl.BlockSpec(memory_space=space)], out_specs=pl.BlockSpec(memory_space=space),
        out_shape=jax.ShapeDtypeStruct((N_DEV,) + v.shape, v.dtype),
        scratch_shapes=[pltpu.SemaphoreType.DMA((N_DEV - 1,)), pltpu.SemaphoreType.DMA((N_DEV - 1,)),
                        pltpu.SemaphoreType.DMA],
        compiler_params=pltpu.CompilerParams(vmem_limit_bytes=VMEM_LIMIT))(v)


_HBM = pl.BlockSpec(memory_space=pltpu.HBM)
_SEM = pl.BlockSpec(memory_space=pltpu.SEMAPHORE)
_EFFECT = pltpu.SideEffectType.DATAFLOW_SIDE_EFFECTING


ALL_PEERS = tuple(range(1, N_DEV))
SAME_CORE_AND_SIBLING = (1, 2, 4, 6)


def _push_start(src, land, block_of, name, relations=ALL_PEERS):
    def body(src_ref, land_ref, send_sem, recv_sem, src_thru, land_thru, token):
        x, y, c, me = _position()
        for k in relations:
            peer, peer_lin = _peer(x, y, c, k)
            mine, there = block_of(src_ref, land_ref, me, peer_lin)
            pltpu.make_async_remote_copy(src_ref=mine, dst_ref=there, send_sem=send_sem, recv_sem=recv_sem,
                                         device_id=peer, device_id_type=MESH).start()
        mine, here = block_of(src_ref, land_ref, me, me)
        pltpu.make_async_copy(mine, here, recv_sem).start()
        token[...] = jnp.zeros_like(token)

    return pl.pallas_call(
        body, name=name,
        out_shape=(pltpu.SemaphoreType.DMA(()), pltpu.SemaphoreType.DMA(()), pltpu.HBM(src.shape, src.dtype),
                   pltpu.HBM(land.shape, land.dtype), jax.ShapeDtypeStruct((8, 128), F32)),
        in_specs=(_HBM, _HBM), out_specs=(_SEM, _SEM, _HBM, _HBM, pl.BlockSpec(memory_space=pltpu.VMEM)),
        input_output_aliases={0: 2, 1: 3},
        compiler_params=pltpu.CompilerParams(has_side_effects=_EFFECT),
    )(pltpu.with_memory_space_constraint(src, pltpu.HBM), pltpu.with_memory_space_constraint(land, pltpu.HBM))


def _push_wait(handle, blocks_of, after, name, n_peers=N_DEV - 1):
    send_sem, recv_sem, src_thru, land_thru, _ = handle

    def body(src_ref, land_ref, send_sem, recv_sem, after_ref, src_dead, got_ref):
        x, y, c, _ = _position()
        sent, landed = blocks_of(land_ref, n_peers), blocks_of(land_ref, n_peers + 1)
        pltpu.make_async_remote_copy(src_ref=sent, dst_ref=sent, send_sem=send_sem, recv_sem=recv_sem,
                                     device_id=(x, y, 1 - c), device_id_type=MESH).wait_send()
        pltpu.make_async_remote_copy(src_ref=landed, dst_ref=landed, send_sem=send_sem, recv_sem=recv_sem,
                                     device_id=(x, y, 1 - c), device_id_type=MESH).wait_recv()

    return pl.pallas_call(
        body, name=name,
        out_shape=(pltpu.HBM(src_thru.shape, src_thru.dtype), pltpu.HBM(land_thru.shape, land_thru.dtype)),
        in_specs=(_HBM, _HBM, _SEM, _SEM, pl.BlockSpec(memory_space=pl.ANY)), out_specs=(_HBM, _HBM),
        input_output_aliases={0: 0, 1: 1},
        compiler_params=pltpu.CompilerParams(has_side_effects=_EFFECT),
    )(src_thru, land_thru, send_sem, recv_sem, after)[1]


def _gather_start(src, name, relations=ALL_PEERS):
    g, r, C = src.shape
    land = lax.empty((g, N_DEV * r, C), src.dtype)
    return _push_start(src, land, lambda s, z, i, p: (s, z.at[:, pl.ds(i * r, r), :]), name, relations)


def _gather_wait(handle, after, name, n_peers=N_DEV - 1):
    r = handle[2].shape[1]
    return _push_wait(handle, lambda z, n: z.at[:, pl.ds(0, n * r), :], after, name, n_peers)


def _relay_start(land, r, name):
    def body(land_ref, send_sem, recv_sem, land_thru, token):
        x, y, c, _ = _position()
        for k in (2, 4, 6):
            _, origin = _peer(x, y, c, k)
            rows = land_ref.at[:, pl.ds(origin * r, r), :]
            pltpu.make_async_remote_copy(src_ref=rows, dst_ref=rows, send_sem=send_sem, recv_sem=recv_sem,
                                         device_id=(x, y, 1 - c), device_id_type=MESH).start()
        token[...] = jnp.zeros_like(token)

    return pl.pallas_call(
        body, name=name,
        out_shape=(pltpu.SemaphoreType.DMA(()), pltpu.SemaphoreType.DMA(()), pltpu.HBM(land.shape, land.dtype),
                   jax.ShapeDtypeStruct((8, 128), F32)),
        in_specs=(_HBM,), out_specs=(_SEM, _SEM, _HBM, pl.BlockSpec(memory_space=pltpu.VMEM)),
        input_output_aliases={0: 2},
        compiler_params=pltpu.CompilerParams(has_side_effects=_EFFECT),
    )(pltpu.with_memory_space_constraint(land, pltpu.HBM))


def _relay_wait(handle, r, after, name):
    send_sem, recv_sem, land_thru, _ = handle

    def body(land_ref, send_sem, recv_sem, after_ref, got_ref):
        x, y, c, _ = _position()
        three = land_ref.at[:, pl.ds(0, 3 * r), :]
        cp = pltpu.make_async_remote_copy(src_ref=three, dst_ref=three, send_sem=send_sem, recv_sem=recv_sem,
                                          device_id=(x, y, 1 - c), device_id_type=MESH)
        cp.wait_send()
        cp.wait_recv()

    return pl.pallas_call(
        body, name=name, out_shape=(pltpu.HBM(land_thru.shape, land_thru.dtype),),
        in_specs=(_HBM, _SEM, _SEM, pl.BlockSpec(memory_space=pl.ANY)), out_specs=(_HBM,),
        input_output_aliases={0: 0},
        compiler_params=pltpu.CompilerParams(has_side_effects=_EFFECT),
    )(land_thru, send_sem, recv_sem, after)[0]


def _exchange_start(grad, name):
    g, rows, C = grad.shape
    r = rows // N_DEV
    land = lax.empty((N_DEV, g, r, C), grad.dtype)
    return _push_start(grad, land, lambda s, z, i, p: (s.at[:, pl.ds(p * r, r), :], z.at[i]), name)


def _exchange_wait(handle, after, name):
    return _push_wait(handle, lambda z, n: z.at[pl.ds(0, n)], after, name)


def _sum_blocks(v, name):
    k, rows, cols = v.shape
    tr = rows
    for cand in (rows, 512, 352, 256, 176, 128, 64, 32, 16):
        if rows % cand == 0 and k * cand * cols * v.dtype.itemsize <= 6 * 1024 * 1024:
            tr = cand
            break

    def body(v_ref, o_ref):
        acc = v_ref[0].astype(F32)
        for s in range(1, k):
            acc = acc + v_ref[s].astype(F32)
        o_ref[...] = acc

    return pl.pallas_call(
        body, grid=(rows // tr,), name=name,
        in_specs=[pl.BlockSpec((k, tr, cols), lambda i: (0, i, 0))],
        out_specs=pl.BlockSpec((tr, cols), lambda i: (i, 0)),
        out_shape=jax.ShapeDtypeStruct((rows, cols), F32),
        compiler_params=_params(("parallel",)))(v)


def _adam_math(w, g, m, v):
    m2 = B1 * m + (1.0 - B1) * g
    v2 = B2 * v + (1.0 - B2) * (g * g)
    m_hat = m2 / (1.0 - B1 ** STEP)
    v_hat = v2 / (1.0 - B2 ** STEP)
    return -LR * (m_hat / (jnp.sqrt(v_hat) + EPS) + WD * w), m2, v2


def _adamw(w, g, m, v, name, dep=None):
    shp = w.shape
    rows, cols = (shp[-2], shp[-1]) if len(shp) >= 2 else (1, shp[-1])
    lead = math.prod(shp[:-2]) if len(shp) > 2 else 1
    fits = [t for t in range(8, rows + 1, 8) if rows % t == 0 and t * cols * 4 <= 2 * 1024 * 1024]
    tr = max(fits) if fits else rows

    def body(w_ref, g_ref, m_ref, v_ref, *rest):
        d_ref, m2_ref, v2_ref = rest[-3:]
        d_ref[...], m2_ref[...], v2_ref[...] = _adam_math(w_ref[...], g_ref[...], m_ref[...], v_ref[...])

    blk = pl.BlockSpec((None, tr, cols), lambda b, i: (b, i, 0))
    extra = [] if dep is None else [dep]
    outs = pl.pallas_call(
        body, grid=(lead, rows // tr), name=name,
        in_specs=[blk] * 4 + [pl.BlockSpec(memory_space=pl.ANY)] * len(extra), out_specs=[blk] * 3,
        out_shape=[jax.ShapeDtypeStruct((lead, rows, cols), F32)] * 3,
        compiler_params=_params(("parallel", "parallel")))(*[a.reshape(lead, rows, cols) for a in (w, g, m, v)], *extra)
    return [o.reshape(shp) for o in outs]


def _as2d(a):
    n = a.size
    if n % 1024 == 0:
        return a.reshape(n // 1024, 1024)
    if n % 128 == 0:
        return a.reshape(n // 128, 128)
    return a.reshape(1, n)


def _blocks_to_cols(a):
    b = jnp.moveaxis(a, 0, -2)
    return b.reshape(b.shape[:-2] + (b.shape[-2] * b.shape[-1],))


def _pack_rows(parts):
    padded, offs, r = [], [], 0
    for p in parts:
        pad = (-p.shape[0]) % 8
        padded.append(jnp.pad(p, ((0, pad), (0, 0))) if pad else p)
        offs.append(r)
        r += p.shape[0] + pad
    return jnp.concatenate(padded, axis=0), offs


def _silu(x):
    return x * jax.nn.sigmoid(x)


def kernel(x, c, ctx, c_ctx, w_mod, b_mod, ln_g, ln_b, ffn_w_gate, ffn_w_up, ffn_w_down, mix_ab_w_in, attn_sink, pool_w, pool_scale, mix_ab_w_out, lru_w_in, lru_conv_w, lru_conv_b, lru_wa, lru_ba, lru_wx, lru_bx, lru_lambda, lru_w_out, loss_target, m_c_ctx, m_w_mod, m_b_mod, m_ln_g, m_ln_b, m_ffn_w_gate, m_ffn_w_up, m_ffn_w_down, m_mix_ab_w_in, m_attn_sink, m_pool_w, m_pool_scale, m_mix_ab_w_out, m_lru_w_in, m_lru_conv_w, m_lru_conv_b, m_lru_wa, m_lru_ba, m_lru_wx, m_lru_bx, m_lru_lambda, m_lru_w_out, v_c_ctx, v_w_mod, v_b_mod, v_ln_g, v_ln_b, v_ffn_w_gate, v_ffn_w_up, v_ffn_w_down, v_mix_ab_w_in, v_attn_sink, v_pool_w, v_pool_scale, v_mix_ab_w_out, v_lru_w_in, v_lru_conv_w, v_lru_conv_b, v_lru_wa, v_lru_ba, v_lru_wx, v_lru_bx, v_lru_lambda, v_lru_w_out):
    weights = dict(c_ctx=c_ctx, w_mod=w_mod, b_mod=b_mod, ln_g=ln_g, ln_b=ln_b, ffn_w_gate=ffn_w_gate,
                   ffn_w_up=ffn_w_up, ffn_w_down=ffn_w_down, mix_ab_w_in=mix_ab_w_in, attn_sink=attn_sink,
                   pool_w=pool_w, pool_scale=pool_scale, mix_ab_w_out=mix_ab_w_out, lru_w_in=lru_w_in,
                   lru_conv_w=lru_conv_w, lru_conv_b=lru_conv_b, lru_wa=lru_wa, lru_ba=lru_ba, lru_wx=lru_wx,
                   lru_bx=lru_bx, lru_lambda=lru_lambda, lru_w_out=lru_w_out)
    mom_m = dict(c_ctx=m_c_ctx, w_mod=m_w_mod, b_mod=m_b_mod, ln_g=m_ln_g, ln_b=m_ln_b, ffn_w_gate=m_ffn_w_gate,
                 ffn_w_up=m_ffn_w_up, ffn_w_down=m_ffn_w_down, mix_ab_w_in=m_mix_ab_w_in, attn_sink=m_attn_sink,
                 pool_w=m_pool_w, pool_scale=m_pool_scale, mix_ab_w_out=m_mix_ab_w_out, lru_w_in=m_lru_w_in,
                 lru_conv_w=m_lru_conv_w, lru_conv_b=m_lru_conv_b, lru_wa=m_lru_wa, lru_ba=m_lru_ba, lru_wx=m_lru_wx,
                 lru_bx=m_lru_bx, lru_lambda=m_lru_lambda, lru_w_out=m_lru_w_out)
    mom_v = dict(c_ctx=v_c_ctx, w_mod=v_w_mod, b_mod=v_b_mod, ln_g=v_ln_g, ln_b=v_ln_b, ffn_w_gate=v_ffn_w_gate,
                 ffn_w_up=v_ffn_w_up, ffn_w_down=v_ffn_w_down, mix_ab_w_in=v_mix_ab_w_in, attn_sink=v_attn_sink,
                 pool_w=v_pool_w, pool_scale=v_pool_scale, mix_ab_w_out=v_mix_ab_w_out, lru_w_in=v_lru_w_in,
                 lru_conv_w=v_lru_conv_w, lru_conv_b=v_lru_conv_b, lru_wa=v_lru_wa, lru_ba=v_lru_ba, lru_wx=v_lru_wx,
                 lru_bx=v_lru_bx, lru_lambda=v_lru_lambda, lru_w_out=v_lru_w_out)
    names = list(weights)

    n_lat, n_ctx = x.shape[1], ctx.shape[1]
    cfg = _Cfg(n_lat, n_ctx)
    _, _, _, me = _position()
    mcols = w_mod.shape[2]

    def t_bf16(w):
        return jnp.swapaxes(w, -1, -2).astype(BF16)

    def ffn_src(l, i):
        return jnp.stack([t_bf16(ffn_w_gate[l, i]), t_bf16(ffn_w_up[l, i]), ffn_w_down[l, i].astype(BF16)])

    pending = {}

    def start_gathers(items, tok):
        for key, make_src in items:
            pending[key] = _gather_start(make_src() + tok.astype(BF16), "gather_start_" + key)
            tok = pending[key][4][0, 0]
        return tok

    def weights_now(key, after):
        return _gather_wait(pending[key], after, "gather_wait_" + key)

    first = _gather_start(ffn_src(0, 0), "gather_start_ffn00", SAME_CORE_AND_SIBLING)
    tok = first[4][0, 0]

    small_names = ["ln_g", "ln_b", "lru_conv_w", "lru_conv_b", "lru_ba", "lru_bx", "lru_lambda"]
    small, small_off = _pack_rows([(c + tok).reshape(-1, 128)] + [weights[n].reshape(-1, 128) for n in small_names])
    small_all = _all_gather(small, "gather_small", True)

    def small_full(idx, shp):
        rows = math.prod(shp) // 128
        return _blocks_to_cols(small_all[:, small_off[idx]:small_off[idx] + rows, :].reshape((N_DEV,) + shp))

    c_all = small_all[:, :2 * D // 128, :].reshape(2 * N_DEV, D)
    ln_g_f, ln_b_f = small_full(1, ln_g.shape), small_full(2, ln_b.shape)
    lru_consts = (small_full(3, lru_conv_w.shape)[0], small_full(4, lru_conv_b.shape), lru_wa[0],
                  small_full(5, lru_ba.shape)[0], lru_wx[0], small_full(6, lru_bx.shape)[0],
                  small_full(7, lru_lambda.shape)[0])

    s_rows = jnp.zeros((32, D), F32).at[:16].set(_silu(c_all)).at[16].set(_silu(c_ctx)).astype(BF16)
    mod_mine = jnp.stack([_matmul(s_rows, w_mod[l], "nn", F32, "mod_fwd", bn_cap=1280) for l in range(2)])
    mod_all = _all_gather(mod_mine.reshape(64, mcols), "gather_mod", True).reshape(N_DEV, 2, 32, mcols)
    r_ffn = ffn_w_down.shape[2]
    relay = _relay_start(_gather_wait(first, mod_all, "gather_wait_ffn00", n_peers=len(SAME_CORE_AND_SIBLING)),
                         r_ffn, "gather_relay_start_ffn00")
    tok = start_gathers([("ab_in", lambda: t_bf16(mix_ab_w_in)), ("ab_out", lambda: mix_ab_w_out.astype(BF16)),
                         ("ffn01", lambda: ffn_src(0, 1)), ("ffn10", lambda: ffn_src(1, 0)),
                         ("lru_in", lambda: t_bf16(lru_w_in)), ("lru_out", lambda: lru_w_out.astype(BF16)),
                         ("ffn11", lambda: ffn_src(1, 1))], relay[3][0, 0])
    mod_full = _blocks_to_cols(mod_all) + (b_mod[:, None, :] + tok)
    ex0 = 2 * me
    mods = []
    for l in range(2):
        rows = jnp.stack([lax.dynamic_index_in_dim(mod_full[l], ex0, 0, False),
                          lax.dynamic_index_in_dim(mod_full[l], ex0 + 1, 0, False), mod_full[l, 16]])
        mods.append(rows.reshape(3, N_MOD, D))

    h0 = jnp.concatenate([x.reshape(cfg.t_lat, D), ctx.reshape(cfg.t_ctx, D)], axis=0)
    cos, sin = _rope_tables(n_lat)
    sink_rows = jnp.broadcast_to(attn_sink[0][:, None], (8, 128)).astype(F32)

    saved = []
    wf = [[None, None], [None, None]]
    h = h0
    xin = _modulate(cfg, h0, mods[0], 0, 1, "modulate_in")
    for l in range(2):
        st = {"h_in": h, "xin1": xin}
        wf[l][0] = (_relay_wait(relay, r_ffn, xin, "gather_relay_wait_ffn00") if l == 0
                    else weights_now("ffn10", xin))
        g1, u1, y1 = _ffn_fwd(xin, wf[l][0], "ffn_fwd")
        h1, xhat1, rstd1, xin2 = _ln_fwd(cfg, h, y1, mods[l], 2, 0.5, ln_g_f[l, 0][None], ln_b_f[l, 0][None],
                                          mods[l], (3, 4), "ln_fwd_a")
        st.update(g1=g1, u1=u1, y1=y1, h1=h1, xhat1=xhat1, rstd1=rstd1, xin2=xin2)
        if l == 0:
            w_ab_in_t = weights_now("ab_in", xin2)[0]
            p = _matmul(xin2, w_ab_in_t, "nt", BF16, "mix_ab_in")
            att_l, att_c = _attn_fwd(cfg, p, cos, sin, sink_rows, "attn_fwd")
            pool_l = _pool_fwd(p, pool_w[0], pool_scale, n_lat, 0, 2, "pool_fwd_lat")
            pool_c = _pool_fwd(p, pool_w[0], pool_scale, n_ctx, cfg.ctx_blk, 2, "pool_fwd_ctx")
            cat = jnp.concatenate([jnp.concatenate([att_l, pool_l], axis=1),
                                   jnp.concatenate([att_c, pool_c], axis=1)], axis=0)
            w_ab_out = weights_now("ab_out", cat)[0]
            y2 = _matmul(cat, w_ab_out, "nn", BF16, "mix_ab_out")
        else:
            w_lru_in_t = weights_now("lru_in", xin2)[0]
            p = _matmul(xin2, w_lru_in_t, "nt", BF16, "lru_in")
            z_l, z_c, st["h_lat"], st["h_ctx"] = _lru_fwd(cfg, p, lru_consts, "lru_fwd")
            cat = jnp.concatenate([z_l, z_c], axis=0)
            w_lru_out = weights_now("lru_out", cat)[0]
            y2 = _matmul(cat, w_lru_out, "nn", BF16, "lru_out")
        h2, xhat2, rstd2, xin3 = _ln_fwd(cfg, h1, y2, mods[l], 5, 1.0, ln_g_f[l, 1][None], ln_b_f[l, 1][None],
                                          mods[l], (6, 7), "ln_fwd_b")
        wf[l][1] = weights_now("ffn%d1" % l, xin3)
        g3, u3, y3 = _ffn_fwd(xin3, wf[l][1], "ffn_fwd")
        if l == 0:
            h3, xhat3, rstd3, xin = _ln_fwd(cfg, h2, y3, mods[l], 8, 0.5, ln_g_f[l, 2][None], ln_b_f[l, 2][None],
                                            mods[1], (0, 1), "ln_fwd_a")
        else:
            h3, xhat3, rstd3 = _ln_fwd(cfg, h2, y3, mods[l], 8, 0.5, ln_g_f[l, 2][None], ln_b_f[l, 2][None],
                                       None, None, "ln_fwd_last")
        st.update(p=p, cat=cat, y2=y2, h2=h2, xhat2=xhat2, rstd2=rstd2, xin3=xin3, g3=g3, u3=u3, y3=y3,
                  xhat3=xhat3, rstd3=rstd3)
        saved.append(st)
        h = h3

    dy, loss_tile = _loss(cfg, h, loss_target.reshape(cfg.t_lat, D), "loss")
    loss = lax.psum(loss_tile[0, 0], ("x", "y", "c"))

    grads = {}
    dmod = [None, None]
    recv_ffn = [[None, None], [None, None]]
    dln_g = [[None] * 3, [None] * 3]
    dln_b = [[None] * 3, [None] * 3]

    def ffn_weight_grads(tag, xin_b, dg, du, a_act, dys):
        handles = []
        for k, (lhs, rhs) in enumerate(((dg, xin_b), (du, xin_b), (a_act, dys))):
            part = _matmul(lhs, rhs, "tn", BF16, "ffn_dw", bm_cap=1408, bk_cap=2304)[None]
            handles.append(_exchange_start(part, "exchange_start_ffn%s_%d" % (tag, k)))
        return handles

    def pin(handles):
        total = handles[0][4][0, 0]
        for hd in handles[1:]:
            total = total + hd[4][0, 0]
        return total

    up = (dy,)
    dmod_next = None
    last_sent = None
    for l in (1, 0):
        st = saved[l]
        dm = [None] * N_MOD

        def put_stats(stats, gate_idx, nxt):
            dm[gate_idx] = stats[:, 2, :]
            if nxt is not None:
                nxt[0][nxt[1]] = stats[:, 4, :]
                nxt[0][nxt[1] + 1] = stats[:, 3, :]

        lng3 = ln_g_f[l, 2][None] if last_sent is None else ln_g_f[l, 2][None] + pin(last_sent)
        if len(up) > 1:
            up = (up[0], up[1], ln_b_f[l, 2][None], up[3], up[4])
        dres, dys, stats = _ln_bwd(cfg, up, st["xhat3"], st["rstd3"], st["y3"], mods[l], 8, 0.5,
                                   lng3, "ln_bwd_fused" if len(up) > 1 else "ln_bwd_last")
        put_stats(stats, 8, None if len(up) == 1 else (dmod_next, 0))
        dln_g[l][2], dln_b[l][2] = stats[:, 0, :].sum(0), stats[:, 1, :].sum(0)
        dg, du, a_act, dxin = _ffn_bwd(dys, st["g3"], st["u3"], wf[l][1], "ffn_bwd")
        recv_ffn[l][1] = ffn_weight_grads("%d1" % l, st["xin3"], dg, du, a_act, dys)
        dres, dys, stats = _ln_bwd(cfg, (dres, dxin, ln_b_f[l, 1][None], mods[l], 7), st["xhat2"], st["rstd2"], st["y2"],
                                   mods[l], 5, 1.0, ln_g_f[l, 1][None] + pin(recv_ffn[l][1]), "ln_bwd_fused")
        put_stats(stats, 5, (dm, 6))
        dln_g[l][1], dln_b[l][1] = stats[:, 0, :].sum(0), stats[:, 1, :].sum(0)
        if l == 0:
            dw_out = _matmul(st["cat"], dys, "tn", BF16, "mix_ab_dw_out")
            dcat = _matmul(dys, w_ab_out, "nt", BF16, "mix_ab_dcat")
            dq, dk, dv, dqc, dkc, dvc, dsink = _attn_bwd(cfg, st["p"], dcat, cos, sin, sink_rows, "attn_bwd")
            du_l, dpw_l, dps_l = _pool_bwd(st["p"], pool_w[0], pool_scale, dcat, n_lat, 0, 2, "pool_bwd_lat")
            du_c, dpw_c, dps_c = _pool_bwd(st["p"], pool_w[0], pool_scale, dcat, n_ctx, cfg.ctx_blk, 2, "pool_bwd_ctx")
            dp = jnp.concatenate([jnp.concatenate([dq, dk, dv, du_l], axis=1),
                                  jnp.concatenate([dqc, dkc, dvc, du_c], axis=1)], axis=0)
            dw_in_t = _matmul(dp, st["xin2"], "tn", BF16, "mix_ab_dw_in", bm_cap=1280)
            dxin = _matmul(dp, w_ab_in_t, "nn", BF16, "mix_ab_dx")
            recv_mix = [_exchange_start(part, "exchange_start_mix_ab_%d" % k)
                        for k, part in enumerate((dw_in_t[None], dw_out[None], _as2d(dpw_l + dpw_c)[None]))]
            grads["attn_sink"] = (dsink[0, :, 0] + dsink[1, :, 0])[None, :]
            grads["pool_scale"] = dps_l + dps_c
        else:
            dw_out = _matmul(st["cat"], dys, "tn", BF16, "lru_dw_out")
            dz = _matmul(dys, w_lru_out, "nt", BF16, "lru_dz")
            dgl, dul, dgc, duc, dwa, dwx, vec = _lru_bwd(cfg, st["p"], dz, st["h_lat"], st["h_ctx"], lru_consts, "lru_bwd")
            dp = jnp.concatenate([jnp.concatenate([dgl, dul], axis=1), jnp.concatenate([dgc, duc], axis=1)], axis=0)
            dw_in_t = _matmul(dp, st["xin2"], "tn", BF16, "lru_dw_in", bm_cap=1024)
            dxin = _matmul(dp, w_lru_in_t, "nn", BF16, "lru_dx")
            recv_mix = [_exchange_start(part, "exchange_start_lru_%d" % k)
                        for k, part in enumerate((dw_in_t[None], dw_out[None], _as2d(dwa)[None], _as2d(dwx)[None]))]
            vec_t = jnp.moveaxis(vec, 0, 1).reshape(16, D)
            grads["lru_ba"], grads["lru_bx"] = vec_t[0:2], vec_t[2:4]
            grads["lru_lambda"], grads["lru_conv_w"], grads["lru_conv_b"] = vec_t[4:6], vec_t[6:10], vec_t[10:11]
        if l == 0:
            recv_ab = recv_mix
        else:
            recv_lru = recv_mix
        dres, dys, stats = _ln_bwd(cfg, (dres, dxin, ln_b_f[l, 0][None], mods[l], 4), st["xhat1"], st["rstd1"], st["y1"],
                                   mods[l], 2, 0.5, ln_g_f[l, 0][None] + pin(recv_mix), "ln_bwd_fused")
        put_stats(stats, 2, (dm, 3))
        dln_g[l][0], dln_b[l][0] = stats[:, 0, :].sum(0), stats[:, 1, :].sum(0)
        dg, du, a_act, dxin = _ffn_bwd(dys, st["g1"], st["u1"], wf[l][0], "ffn_bwd")
        recv_ffn[l][0] = ffn_weight_grads("%d0" % l, st["xin1"], dg, du, a_act, dys)
        last_sent = recv_ffn[l][0]
        dmod[l] = dm
        dmod_next = dm
        up = (dres, dxin, None, mods[l], 1)
    dh0, stats = _modulate_bwd(cfg, up[0], up[1], h0, mods[0] + pin(last_sent), 1, "modulate_bwd")
    dmod[0][0], dmod[0][1] = stats[:, 4, :], stats[:, 3, :]
    grad_x = dh0.reshape(x.shape)

    dmod_mine = jnp.stack([jnp.stack(dmod[l], axis=1).reshape(3, N_MOD * D) for l in range(2)])
    n_dm = 6 * N_MOD * D // 128
    dmod_sent = _gather_start(dmod_mine.reshape(1, n_dm, 128), "gather_start_dmod")

    def arrived(handle, name):
        return _exchange_wait(handle, dmod_sent[4], name)

    recv_ffn = [[[arrived(hd, "exchange_wait_ffn%d%d_%d" % (l, i, k)) for k, hd in enumerate(recv_ffn[l][i])]
                 for i in range(2)] for l in range(2)]
    recv_ab = [arrived(hd, "exchange_wait_mix_ab_%d" % k) for k, hd in enumerate(recv_ab)]
    recv_lru = [arrived(hd, "exchange_wait_lru_%d" % k) for k, hd in enumerate(recv_lru)]

    def shard_sum(recv, name):
        return _sum_blocks(recv.reshape(N_DEV, recv.shape[2], recv.shape[3]), name)

    gate_g = [[None, None], [None, None]]
    up_g = [[None, None], [None, None]]
    down_g = [[None, None], [None, None]]
    for l in range(2):
        for i in range(2):
            gt, ut, dn = [shard_sum(r, "sum_ffn") for r in recv_ffn[l][i]]
            gate_g[l][i], up_g[l][i], down_g[l][i] = gt.T, ut.T, dn
    grads["ffn_w_gate"] = jnp.stack([jnp.stack(gate_g[l]) for l in range(2)])
    grads["ffn_w_up"] = jnp.stack([jnp.stack(up_g[l]) for l in range(2)])
    grads["ffn_w_down"] = jnp.stack([jnp.stack(down_g[l]) for l in range(2)])
    grads["mix_ab_w_in"] = shard_sum(recv_ab[0], "sum_mix_in").T[None]
    grads["mix_ab_w_out"] = shard_sum(recv_ab[1], "sum_mix_out")[None]
    grads["lru_w_in"] = shard_sum(recv_lru[0], "sum_lru_in").T[None]
    grads["lru_w_out"] = shard_sum(recv_lru[1], "sum_lru_out")[None]
    rep_parts = [shard_sum(recv_lru[2], "sum_rep"), shard_sum(recv_lru[3], "sum_rep"), shard_sum(recv_ab[2], "sum_rep")]
    rep_names = ["lru_wa", "lru_wx", "pool_w"]

    dmod_all = _gather_wait(dmod_sent, rep_parts[2], "gather_wait_dmod").reshape(N_DEV, n_dm, 128)
    dmod_sum = _sum_blocks(dmod_all, "sum_dmod").reshape(2, 3, N_MOD * D)
    dmod_all = dmod_all.reshape(N_DEV, 2, 3, N_MOD * D)
    grads["b_mod"] = dmod_sum[:, 0] + dmod_sum[:, 1] + dmod_sum[:, 2]
    dmod_ex = jnp.moveaxis(dmod_all[:, :, 0:2, :], 1, 0).reshape(2, 2 * N_DEV, N_MOD * D)
    dm_rows = jnp.zeros((2, 32, N_MOD * D), F32).at[:, :16].set(dmod_ex).at[:, 16].set(dmod_sum[:, 2])
    dm_cols = lax.dynamic_slice_in_dim(dm_rows, me * mcols, mcols, axis=2).astype(BF16)
    grads["w_mod"] = jnp.stack([_matmul(s_rows, dm_cols[l], "tn", F32, "mod_dw", bn_cap=1280) for l in range(2)])
    ds_part = None
    for l in range(2):
        part = _matmul(dm_cols[l, 16:32], w_mod[l], "nt", F32, "mod_ds", bk_cap=1280)[0]
        ds_part = part if ds_part is None else ds_part + part

    dln_g_f = jnp.stack([jnp.stack(dln_g[l]) for l in range(2)])
    dln_b_f = jnp.stack([jnp.stack(dln_b[l]) for l in range(2)])
    sink_pad = jnp.zeros((1, 128), F32).at[0, :8].set(grads["attn_sink"][0])
    part_list = [p_.reshape(-1, 128) for p_ in rep_parts] + [
        dln_g_f.reshape(-1, 128), dln_b_f.reshape(-1, 128), grads["lru_conv_w"].reshape(-1, 128),
        grads["lru_conv_b"].reshape(-1, 128), grads["lru_ba"].reshape(-1, 128), grads["lru_bx"].reshape(-1, 128),
        grads["lru_lambda"].reshape(-1, 128), ds_part.reshape(-1, 128), sink_pad, grads["pool_scale"].reshape(-1, 128)]
    parts, part_off = _pack_rows(part_list)
    parts_sent = _gather_start(parts[None], "gather_start_partials")

    delta, new_m, new_v = {}, {}, {}
    for n in ("w_mod", "b_mod", "ffn_w_gate", "ffn_w_up", "ffn_w_down", "mix_ab_w_in", "mix_ab_w_out",
              "lru_w_in", "lru_w_out"):
        grads[n] = grads[n].reshape(weights[n].shape)
        delta[n], new_m[n], new_v[n] = _adamw(weights[n], grads[n], mom_m[n], mom_v[n], "adamw", dep=parts_sent[4])
    parts_all = _gather_wait(parts_sent, delta["lru_w_out"], "gather_wait_partials").reshape(N_DEV, parts.shape[0], 128)
    parts_sum = _sum_blocks(parts_all, "sum_partials")

    for i, n in enumerate(rep_names):
        rows = part_list[i].shape[0]
        grads[n] = parts_all[:, part_off[i]:part_off[i] + rows, :].reshape(weights[n].shape)

    def take(idx):
        return parts_sum[part_off[idx]:part_off[idx] + part_list[idx].shape[0]]

    def my_cols(full, shp):
        w = shp[-1]
        return lax.dynamic_slice_in_dim(full, me * w, w, axis=full.ndim - 1)

    grads["ln_g"] = my_cols(take(3).reshape(2, 3, D), ln_g.shape)
    grads["ln_b"] = my_cols(take(4).reshape(2, 3, D), ln_b.shape)
    grads["lru_conv_w"] = my_cols(take(5).reshape(1, 4, D), lru_conv_w.shape)
    grads["lru_conv_b"] = my_cols(take(6).reshape(1, D), lru_conv_b.shape)
    grads["lru_ba"] = my_cols(take(7).reshape(1, 2, D), lru_ba.shape)
    grads["lru_bx"] = my_cols(take(8).reshape(1, 2, D), lru_bx.shape)
    grads["lru_lambda"] = my_cols(take(9).reshape(1, 2, D), lru_lambda.shape)
    sg = jax.nn.sigmoid(c_ctx)
    grads["c_ctx"] = take(10).reshape(D) * (sg * (1.0 + c_ctx * (1.0 - sg)))
    grads["attn_sink"] = take(11)[:, :8]
    grads["pool_scale"] = take(12).reshape(pool_scale.shape)

    for n in names:
        if n in delta:
            continue
        grads[n] = grads[n].reshape(weights[n].shape)
        delta[n], new_m[n], new_v[n] = _adamw(weights[n], grads[n], mom_m[n], mom_v[n], "adamw")

    return (loss, grad_x, *[grads[n] for n in names], *[delta[n] for n in names],
            *[new_m[n] for n in names], *[new_v[n] for n in names])
```

```python
import functools
import math

import jax
import jax.numpy as jnp
from jax import lax
from jax.experimental import pallas as pl
from jax.experimental.pallas import tpu as pltpu

F32 = jnp.float32
BF16 = jnp.bfloat16
MESH = pl.DeviceIdType.MESH

D = 1024
N_MOD = 9
N_DEV = 8
HEAD_DIM = 64
ATT_HEADS = 8
KV_HEADS = 2
ATT_W = 512
BLK = 128
ATT_SCALE = HEAD_DIM ** -0.5
GRID_W = 64
ROPE_FREQS = HEAD_DIM // 4
ROPE_THETA = 10000.0
POOL_R = (1, 2, 4, 8)
LRU_C = 8.0
LN_EPS = 1e-5
NEG_INF = -1e30
ALPHA = 4.0 ** 0.25
LR, B1, B2, EPS, WD, STEP = 0.001, 0.9, 0.999, 1e-08, 0.01, 10
VMEM_LIMIT = 56 * 1024 * 1024
ROW_TILE = 512


def _params(sem=None):
    if sem is None:
        return pltpu.CompilerParams(vmem_limit_bytes=VMEM_LIMIT)
    return pltpu.CompilerParams(dimension_semantics=sem, vmem_limit_bytes=VMEM_LIMIT)


def _sigmoid(x):
    return 0.5 * jnp.tanh(0.5 * x) + 0.5


def _dot(a, b):
    return jnp.dot(a.astype(BF16), b.astype(BF16), preferred_element_type=F32)


def _dot_nt(a, b):
    return lax.dot_general(a.astype(BF16), b.astype(BF16), (((1,), (1,)), ((), ())), preferred_element_type=F32)


def _dot_tn(a, b):
    return lax.dot_general(a.astype(BF16), b.astype(BF16), (((0,), (0,)), ((), ())), preferred_element_type=F32)


def _pick(n, cap):
    best = None
    for m in range(128, min(n, cap) + 1, 128):
        if n % m == 0:
            best = m
    return n if best is None else best


def _chunks(width, step=256):
    out, c = [], 0
    while c < width:
        w = min(step, width - c)
        out.append((c, w))
        c += w
    return out


class _Cfg:
    def __init__(self, n_lat, n_ctx):
        self.n_lat, self.n_ctx = n_lat, n_ctx
        self.t_lat, self.t_ctx = 2 * n_lat, 2 * n_ctx
        self.T = self.t_lat + self.t_ctx
        self.tm = min(ROW_TILE, self.t_ctx)
        assert n_lat % self.tm == 0 and self.t_ctx % self.tm == 0 and n_lat >= 3 * BLK and n_ctx % BLK == 0
        self.nt = self.T // self.tm
        self.nlt = n_lat // self.tm
        self.ctx_blk = self.t_lat // n_ctx

    def seg(self, i):
        return jnp.minimum(i // self.nlt, 2)

    def first_of_seg(self, i):
        return jnp.where(i < 2 * self.nlt, i % self.nlt == 0, i == 2 * self.nlt)


def _modulate(cfg, h, mod, shift_idx, scale_idx, name):
    tm = cfg.tm

    def body(h_ref, mod_ref, o_ref):
        sh = mod_ref[shift_idx:shift_idx + 1, :]
        sc = mod_ref[scale_idx:scale_idx + 1, :]
        o_ref[...] = (h_ref[...] * (1.0 + sc) + sh).astype(BF16)

    return pl.pallas_call(
        body, grid=(cfg.nt,), name=name,
        in_specs=[pl.BlockSpec((tm, D), lambda i: (i, 0)),
                  pl.BlockSpec((None, N_MOD, D), lambda i: (cfg.seg(i), 0, 0))],
        out_specs=pl.BlockSpec((tm, D), lambda i: (i, 0)),
        out_shape=jax.ShapeDtypeStruct((cfg.T, D), BF16),
        compiler_params=_params(("parallel",)),
    )(h, mod)


def _ln_fwd(cfg, h, y, mod, gate_idx, coef, lng, lnb, mod_next, next_idx, name):
    tm = cfg.tm
    has_next = next_idx is not None

    def body(*refs):
        if has_next:
            h_ref, y_ref, mod_ref, g_ref, b_ref, modn_ref, hn_ref, xhat_ref, rstd_ref, xin_ref = refs
        else:
            h_ref, y_ref, mod_ref, g_ref, b_ref, hn_ref, xhat_ref, rstd_ref = refs
        gate = mod_ref[gate_idx:gate_idx + 1, :]
        z = ALPHA * h_ref[...] + (coef * gate) * y_ref[...].astype(F32)
        mu = jnp.mean(z, axis=-1, keepdims=True)
        zc = z - mu
        var = jnp.mean(zc * zc, axis=-1, keepdims=True)
        rstd = lax.rsqrt(var + LN_EPS)
        xhat = zc * rstd
        hn = xhat * g_ref[...] + b_ref[...]
        hn_ref[...] = hn
        xhat_ref[...] = xhat.astype(BF16)
        rstd_ref[...] = rstd
        if has_next:
            sh = modn_ref[next_idx[0]:next_idx[0] + 1, :]
            sc = modn_ref[next_idx[1]:next_idx[1] + 1, :]
            xin_ref[...] = (hn * (1.0 + sc) + sh).astype(BF16)

    row = pl.BlockSpec((tm, D), lambda i: (i, 0))
    modspec = pl.BlockSpec((None, N_MOD, D), lambda i: (cfg.seg(i), 0, 0))
    vec = pl.BlockSpec((1, D), lambda i: (0, 0))
    in_specs = [row, row, modspec, vec, vec]
    args = [h, y, mod, lng, lnb]
    out_specs = [row, row, pl.BlockSpec((tm, 1), lambda i: (i, 0))]
    out_shape = [jax.ShapeDtypeStruct((cfg.T, D), F32), jax.ShapeDtypeStruct((cfg.T, D), BF16),
                 jax.ShapeDtypeStruct((cfg.T, 1), F32)]
    if has_next:
        in_specs.append(modspec)
        args.append(mod_next)
        out_specs.append(row)
        out_shape.append(jax.ShapeDtypeStruct((cfg.T, D), BF16))
    return pl.pallas_call(body, grid=(cfg.nt,), name=name, in_specs=in_specs, out_specs=out_specs,
                          out_shape=out_shape, compiler_params=_params(("parallel",)))(*args)


def _ln_bwd(cfg, up, xhat, rstd, y, mod, gate_idx, coef, lng, name):
    tm = cfg.tm
    fused = len(up) > 1
    scale_next = up[4] if fused else None

    def body(*refs):
        if fused:
            dres_n, dxin_n, b_ref, modn_ref, xhat_ref, rstd_ref, y_ref, mod_ref, g_ref, dres_ref, dys_ref, st_ref = refs
        else:
            dhn_ref, xhat_ref, rstd_ref, y_ref, mod_ref, g_ref, dres_ref, dys_ref, st_ref = refs
        i = pl.program_id(0)

        @pl.when(cfg.first_of_seg(i))
        def _():
            st_ref[...] = jnp.zeros_like(st_ref)

        xhat = xhat_ref[...].astype(F32)
        if fused:
            dxin = dxin_n[...].astype(F32)
            sc = modn_ref[scale_next:scale_next + 1, :]
            dhn = dres_n[...] + dxin * (1.0 + sc)
            shift_sum = jnp.sum(dxin, axis=0, keepdims=True)
            st_ref[3:4, :] += g_ref[...] * jnp.sum(dxin * xhat, axis=0, keepdims=True) + b_ref[...] * shift_sum
            st_ref[4:5, :] += shift_sum
        else:
            dhn = dhn_ref[...]
        gdh = dhn * g_ref[...]
        m1 = jnp.mean(gdh, axis=-1, keepdims=True)
        m2 = jnp.mean(gdh * xhat, axis=-1, keepdims=True)
        dz = rstd_ref[...] * (gdh - m1 - xhat * m2)
        gate = mod_ref[gate_idx:gate_idx + 1, :]
        dres_ref[...] = ALPHA * dz
        dys_ref[...] = ((coef * gate) * dz).astype(BF16)
        st_ref[0:1, :] += jnp.sum(dhn * xhat, axis=0, keepdims=True)
        st_ref[1:2, :] += jnp.sum(dhn, axis=0, keepdims=True)
        st_ref[2:3, :] += jnp.sum((coef * dz) * y_ref[...].astype(F32), axis=0, keepdims=True)

    row = pl.BlockSpec((tm, D), lambda i: (i, 0))
    modspec = pl.BlockSpec((None, N_MOD, D), lambda i: (cfg.seg(i), 0, 0))
    vec = pl.BlockSpec((1, D), lambda i: (0, 0))
    col = pl.BlockSpec((tm, 1), lambda i: (i, 0))
    if fused:
        in_specs = [row, row, vec, modspec, row, col, row, modspec, vec]
        args = [up[0], up[1], up[2], up[3], xhat, rstd, y, mod, lng]
    else:
        in_specs = [row, row, col, row, modspec, vec]
        args = [up[0], xhat, rstd, y, mod, lng]
    return pl.pallas_call(
        body, grid=(cfg.nt,), name=name, in_specs=in_specs,
        out_specs=[row, row, pl.BlockSpec((None, 8, D), lambda i: (cfg.seg(i), 0, 0))],
        out_shape=[jax.ShapeDtypeStruct((cfg.T, D), F32), jax.ShapeDtypeStruct((cfg.T, D), BF16),
                   jax.ShapeDtypeStruct((3, 8, D), F32)],
        compiler_params=_params(("arbitrary",)))(*args)


def _modulate_bwd(cfg, dres, dxin, h, mod, scale_idx, name):
    tm = cfg.tm
    n_lt = 2 * cfg.nlt

    def body(dres_ref, dxin_ref, h_ref, mod_ref, dh_ref, st_ref):
        i = pl.program_id(0)

        @pl.when(cfg.first_of_seg(i))
        def _():
            st_ref[...] = jnp.zeros_like(st_ref)

        dxin = dxin_ref[...].astype(F32)
        sc = mod_ref[scale_idx:scale_idx + 1, :]

        @pl.when(i < n_lt)
        def _():
            dh_ref[...] = dres_ref[...] + dxin * (1.0 + sc)

        st_ref[3:4, :] += jnp.sum(dxin * h_ref[...], axis=0, keepdims=True)
        st_ref[4:5, :] += jnp.sum(dxin, axis=0, keepdims=True)

    row = pl.BlockSpec((tm, D), lambda i: (i, 0))
    return pl.pallas_call(
        body, grid=(cfg.nt,), name=name,
        in_specs=[row, row, row, pl.BlockSpec((None, N_MOD, D), lambda i: (cfg.seg(i), 0, 0))],
        out_specs=[pl.BlockSpec((tm, D), lambda i: (jnp.minimum(i, n_lt - 1), 0)),
                   pl.BlockSpec((None, 8, D), lambda i: (cfg.seg(i), 0, 0))],
        out_shape=[jax.ShapeDtypeStruct((cfg.t_lat, D), F32), jax.ShapeDtypeStruct((3, 8, D), F32)],
        compiler_params=_params(("arbitrary",)))(dres, dxin, h, mod)


def _loss(cfg, h, target, name):
    tm = cfg.tm
    n_lt = 2 * cfg.nlt

    def body(h_ref, t_ref, dy_ref, l_ref):
        i = pl.program_id(0)

        @pl.when(i == 0)
        def _():
            l_ref[...] = jnp.zeros_like(l_ref)

        @pl.when(i < n_lt)
        def _():
            err = h_ref[...] - t_ref[...]
            dy_ref[...] = err * (1.0 / D)
            part = jnp.sum(jnp.sum(err * err, axis=1, keepdims=True), axis=0, keepdims=True) * (0.5 / D)
            l_ref[...] += jnp.broadcast_to(part, l_ref.shape)

        @pl.when(i >= n_lt)
        def _():
            dy_ref[...] = jnp.zeros_like(dy_ref)

    return pl.pallas_call(
        body, grid=(cfg.nt,), name=name,
        in_specs=[pl.BlockSpec((tm, D), lambda i: (i, 0)),
                  pl.BlockSpec((tm, D), lambda i: (jnp.minimum(i, n_lt - 1), 0))],
        out_specs=[pl.BlockSpec((tm, D), lambda i: (i, 0)), pl.BlockSpec((8, 128), lambda i: (0, 0))],
        out_shape=[jax.ShapeDtypeStruct((cfg.T, D), F32), jax.ShapeDtypeStruct((8, 128), F32)],
        compiler_params=_params(("arbitrary",)))(h, target)


def _matmul(a, b, mode, out_dtype, name, bm_cap=2304, bn_cap=1408, bk_cap=1024):
    if mode == "nn":
        (M, K), N = a.shape, b.shape[1]
    elif mode == "nt":
        (M, K), N = a.shape, b.shape[0]
    else:
        (K, M), N = a.shape, b.shape[1]
    bm, bn, bk = _pick(M, bm_cap), _pick(N, bn_cap), _pick(K, bk_cap)
    nk = K // bk

    def body(a_ref, b_ref, o_ref, acc_ref=None):
        k = pl.program_id(2)
        if mode == "nn":
            part = _dot(a_ref[...], b_ref[...])
        elif mode == "nt":
            part = _dot_nt(a_ref[...], b_ref[...])
        else:
            part = _dot_tn(a_ref[...], b_ref[...])
        if nk == 1:
            o_ref[...] = part.astype(out_dtype)
            return

        @pl.when(k == 0)
        def _():
            acc_ref[...] = part

        @pl.when((k > 0) & (k < nk - 1))
        def _():
            acc_ref[...] += part

        @pl.when(k == nk - 1)
        def _():
            o_ref[...] = (acc_ref[...] + part).astype(out_dtype)

    if mode == "nn":
        a_spec = pl.BlockSpec((bm, bk), lambda i, j, k: (i, k))
        b_spec = pl.BlockSpec((bk, bn), lambda i, j, k: (k, j))
    elif mode == "nt":
        a_spec = pl.BlockSpec((bm, bk), lambda i, j, k: (i, k))
        b_spec = pl.BlockSpec((bn, bk), lambda i, j, k: (j, k))
    else:
        a_spec = pl.BlockSpec((bk, bm), lambda i, j, k: (k, i))
        b_spec = pl.BlockSpec((bk, bn), lambda i, j, k: (k, j))
    return pl.pallas_call(
        body, grid=(M // bm, N // bn, nk), name=name, in_specs=[a_spec, b_spec],
        out_specs=pl.BlockSpec((bm, bn), lambda i, j, k: (i, j)),
        out_shape=jax.ShapeDtypeStruct((M, N), out_dtype),
        scratch_shapes=[pltpu.VMEM((bm, bn), F32)] if nk > 1 else [],
        compiler_params=_params(("parallel", "parallel", "arbitrary")))(a, b)


def _ffn_tile(T, cap):
    best = 256
    for t in range(256, cap + 1, 256):
        if T % t == 0:
            best = t
    return best


def _ffn_fwd(xin, wf, name):
    T = xin.shape[0]
    F = wf.shape[1]
    tm, tf = _ffn_tile(T, 768), F // 2
    assert tf % 128 == 0 and T % tm == 0

    def body(x_ref, wg_ref, wu_ref, wd_ref, g_ref, u_ref, y_ref, acc_ref):
        j = pl.program_id(1)
        x = x_ref[...]
        acc = None
        for c0, cw in _chunks(tf):
            g = _dot_nt(x, wg_ref[c0:c0 + cw, :])
            u = _dot_nt(x, wu_ref[c0:c0 + cw, :])
            g_ref[:, c0:c0 + cw] = g.astype(BF16)
            u_ref[:, c0:c0 + cw] = u.astype(BF16)
            part = _dot(g * _sigmoid(g) * u, wd_ref[c0:c0 + cw, :])
            acc = part if acc is None else acc + part

        @pl.when(j == 0)
        def _():
            acc_ref[...] = acc

        @pl.when(j == 1)
        def _():
            y_ref[...] = (acc_ref[...] + acc).astype(BF16)

    return pl.pallas_call(
        body, grid=(T // tm, 2), name=name,
        in_specs=[pl.BlockSpec((tm, D), lambda i, j: (i, 0)),
                  pl.BlockSpec((None, tf, D), lambda i, j: (0, j, 0)),
                  pl.BlockSpec((None, tf, D), lambda i, j: (1, j, 0)),
                  pl.BlockSpec((None, tf, D), lambda i, j: (2, j, 0))],
        out_specs=[pl.BlockSpec((tm, tf), lambda i, j: (i, j)),
                   pl.BlockSpec((tm, tf), lambda i, j: (i, j)),
                   pl.BlockSpec((tm, D), lambda i, j: (i, 0))],
        out_shape=[jax.ShapeDtypeStruct((T, F), BF16), jax.ShapeDtypeStruct((T, F), BF16),
                   jax.ShapeDtypeStruct((T, D), BF16)],
        scratch_shapes=[pltpu.VMEM((tm, D), F32)],
        compiler_params=_params(("parallel", "arbitrary")))(xin, wf, wf, wf)


def _ffn_bwd(dys, g, u, wf, name):
    T = dys.shape[0]
    F = wf.shape[1]
    tm, tf = _ffn_tile(T, 512), F // 2

    def body(dy_ref, g_ref, u_ref, wg_ref, wu_ref, wd_ref, dg_ref, du_ref, a_ref, dx_ref, acc_ref):
        j = pl.program_id(1)
        da_all = _dot_nt(dy_ref[...], wd_ref[...])
        for c0, cw in _chunks(tf):
            gg = g_ref[:, c0:c0 + cw].astype(F32)
            uu = u_ref[:, c0:c0 + cw].astype(F32)
            da = da_all[:, c0:c0 + cw]
            s = _sigmoid(gg)
            silu = gg * s
            a_ref[:, c0:c0 + cw] = (silu * uu).astype(BF16)
            du_ref[:, c0:c0 + cw] = (da * silu).astype(BF16)
            dg_ref[:, c0:c0 + cw] = (da * uu * (s * (1.0 + gg * (1.0 - s)))).astype(BF16)
        acc = _dot(dg_ref[...], wg_ref[...]) + _dot(du_ref[...], wu_ref[...])

        @pl.when(j == 0)
        def _():
            acc_ref[...] = acc

        @pl.when(j == 1)
        def _():
            dx_ref[...] = (acc_ref[...] + acc).astype(BF16)

    blk = pl.BlockSpec((tm, tf), lambda i, j: (i, j))
    return pl.pallas_call(
        body, grid=(T // tm, 2), name=name,
        in_specs=[pl.BlockSpec((tm, D), lambda i, j: (i, 0)), blk, blk,
                  pl.BlockSpec((None, tf, D), lambda i, j: (0, j, 0)),
                  pl.BlockSpec((None, tf, D), lambda i, j: (1, j, 0)),
                  pl.BlockSpec((None, tf, D), lambda i, j: (2, j, 0))],
        out_specs=[blk, blk, blk, pl.BlockSpec((tm, D), lambda i, j: (i, 0))],
        out_shape=[jax.ShapeDtypeStruct((T, F), BF16), jax.ShapeDtypeStruct((T, F), BF16),
                   jax.ShapeDtypeStruct((T, F), BF16), jax.ShapeDtypeStruct((T, D), BF16)],
        scratch_shapes=[pltpu.VMEM((tm, D), F32)],
        compiler_params=_params(("parallel", "arbitrary")))(dys, g, u, wf, wf, wf)


def _swap_halves(x):
    w = x.shape[1]
    lane = lax.broadcasted_iota(jnp.int32, (1, w), 1)
    return jnp.where((lane & 63) < 32, pltpu.roll(x, w - 32, 1), pltpu.roll(x, 32, 1))


def _rope(x, cos, sin):
    return x * cos + _swap_halves(x) * sin


def _rope_t(dy, cos, sin):
    return dy * cos + _swap_halves(dy * sin)


def _rope_tables(n_lat):
    rows = n_lat // GRID_W
    row = jnp.repeat(jnp.arange(rows, dtype=F32), GRID_W)
    col = jnp.tile(jnp.arange(GRID_W, dtype=F32), rows)
    inv = ROPE_THETA ** (-jnp.arange(ROPE_FREQS, dtype=F32) / ROPE_FREQS)
    ang = jnp.concatenate([row[:, None] * inv, col[:, None] * inv], axis=-1)
    cs, sn = jnp.cos(ang), jnp.sin(ang)
    cos = jnp.concatenate([cs, cs, cs, cs], axis=-1)
    sin = jnp.concatenate([-sn, sn, -sn, sn], axis=-1)
    return cos, sin


def _attn_specs(cfg):
    n_lat, n_ctx, cb = cfg.n_lat, cfg.n_ctx, cfg.ctx_blk
    return [pl.BlockSpec((n_lat, ATT_W), lambda e: (e, 0)),
            pl.BlockSpec((n_lat, 128), lambda e: (e, 4)),
            pl.BlockSpec((n_lat, 128), lambda e: (e, 5)),
            pl.BlockSpec((n_ctx, ATT_W), lambda e: (cb + e, 0)),
            pl.BlockSpec((n_ctx, 128), lambda e: (cb + e, 4)),
            pl.BlockSpec((n_ctx, 128), lambda e: (cb + e, 5)),
            pl.BlockSpec((n_lat, 128), lambda e: (0, 0)),
            pl.BlockSpec((n_lat, 128), lambda e: (0, 0)),
            pl.BlockSpec((8, 128), lambda e: (0, 0))]


def _attn_prepare(kh, kl, vl, kc, vc, ka, kb, va, vb, kca, kcb, vca, vcb):
    lane = lax.broadcasted_iota(jnp.int32, (1, 128), 1)
    own = (lane < 64) if kh == 0 else (lane >= 64)

    def split(x, ra, rb):
        mine = jnp.where(own, x, 0.0)
        other = pltpu.roll(mine, 64, 1)
        a, b = (mine, other) if kh == 0 else (other, mine)
        ra[...] = a.astype(BF16)
        rb[...] = b.astype(BF16)

    split(kl, ka, kb)
    split(vl, va, vb)
    split(kc, kca, kcb)
    split(vc, vca, vcb)


def _softmax_parts(s_list, sk):
    m = sk
    for s in s_list:
        m = jnp.maximum(m, jnp.max(s, axis=1, keepdims=True))
    es = [jnp.exp(s - m) for s in s_list]
    esk = jnp.exp(sk - m)
    den = esk
    for e in es:
        den = den + jnp.sum(e, axis=1, keepdims=True)
    inv = 1.0 / den
    return [e * inv for e in es], esk * inv


def _window(cfg, n):
    r0 = pl.multiple_of(n * BLK, BLK)
    start = pl.multiple_of(jnp.clip((n - 1) * BLK, 0, cfg.n_lat - 3 * BLK), BLK)
    qpos = r0 + lax.broadcasted_iota(jnp.int32, (BLK, 1), 0)
    kpos = start + lax.broadcasted_iota(jnp.int32, (1, 3 * BLK), 1)
    valid = jnp.abs(qpos - kpos) <= BLK
    return r0, start, valid


def _attn_fwd(cfg, p, cos, sin, sink_rows, name):
    n_lat, n_ctx = cfg.n_lat, cfg.n_ctx

    def body(q_ref, k_ref, v_ref, qc_ref, kc_ref, vc_ref, cos_ref, sin_ref, sink_ref, o_ref, oc_ref,
             qr, ka, kb, va, vb, kca, kcb, vca, vcb):
        cos_t, sin_t = cos_ref[...], sin_ref[...]
        for gq in range(4):
            qr[:, gq * 128:(gq + 1) * 128] = _rope(q_ref[:, gq * 128:(gq + 1) * 128].astype(F32), cos_t, sin_t).astype(BF16)
        kl = _rope(k_ref[...].astype(F32), cos_t, sin_t)
        for kh in range(KV_HEADS):
            _attn_prepare(kh, kl, v_ref[...].astype(F32), kc_ref[...].astype(F32), vc_ref[...].astype(F32),
                          ka, kb, va, vb, kca, kcb, vca, vcb)

            def lat_block(n, carry):
                r0, start, valid = _window(cfg, n)
                win = pl.ds(start, 3 * BLK)
                lanes = [slice((kh * 2 + pr) * 128, (kh * 2 + pr + 1) * 128) for pr in range(2)]
                qps = [qr[pl.ds(r0, BLK), lanes[pr]] for pr in range(2)]
                kws, kcs = (ka[win, :], kb[win, :]), (kca[...], kcb[...])
                scores = [(jnp.where(valid, _dot_nt(qps[pr], kws[half]) * ATT_SCALE, NEG_INF),
                           _dot_nt(qps[pr], kcs[half]) * ATT_SCALE) for pr in range(2) for half in range(2)]
                probs = []
                for idx, (s_w, s_c) in enumerate(scores):
                    head = kh * 4 + idx
                    (p_w, p_c), _ = _softmax_parts([s_w, s_c], sink_ref[head:head + 1, 0:1])
                    probs.append((p_w.astype(BF16), p_c.astype(BF16)))
                vws, vcs = (va[win, :], vb[win, :]), (vca[...], vcb[...])
                for pr in range(2):
                    o = (_dot(probs[2 * pr][0], vws[0]) + _dot(probs[2 * pr][1], vcs[0])
                         + _dot(probs[2 * pr + 1][0], vws[1]) + _dot(probs[2 * pr + 1][1], vcs[1]))
                    o_ref[pl.ds(r0, BLK), lanes[pr]] = o.astype(BF16)
                return carry

            lax.fori_loop(0, n_lat // BLK, lat_block, 0, unroll=2)
            for n in range(n_ctx // BLK):
                rows = slice(n * BLK, (n + 1) * BLK)
                for pr in range(2):
                    lanes = slice((kh * 2 + pr) * 128, (kh * 2 + pr + 1) * 128)
                    qp = qc_ref[rows, lanes]
                    o = None
                    for half, (kcx, vcx) in enumerate(((kca, vca), (kcb, vcb))):
                        head = kh * 4 + pr * 2 + half
                        s_c = _dot_nt(qp, kcx[...]) * ATT_SCALE
                        (p_c,), _ = _softmax_parts([s_c], sink_ref[head:head + 1, 0:1])
                        part = _dot(p_c, vcx[...])
                        o = part if o is None else o + part
                    oc_ref[rows, lanes] = o.astype(BF16)

    return pl.pallas_call(
        body, grid=(2,), name=name, in_specs=_attn_specs(cfg),
        out_specs=[pl.BlockSpec((n_lat, ATT_W), lambda e: (e, 0)), pl.BlockSpec((n_ctx, ATT_W), lambda e: (e, 0))],
        out_shape=[jax.ShapeDtypeStruct((cfg.t_lat, ATT_W), BF16), jax.ShapeDtypeStruct((cfg.t_ctx, ATT_W), BF16)],
        scratch_shapes=[pltpu.VMEM((n_lat, ATT_W), BF16)] + [pltpu.VMEM((n_lat, 128), BF16)] * 4
        + [pltpu.VMEM((n_ctx, 128), BF16)] * 4,
        compiler_params=_params(("parallel",)))(p, p, p, p, p, p, cos, sin, sink_rows)


def _attn_bwd(cfg, p, dcat, cos, sin, sink_rows, name):
    n_lat, n_ctx, cb = cfg.n_lat, cfg.n_ctx, cfg.ctx_blk

    def body(q_ref, k_ref, v_ref, qc_ref, kc_ref, vc_ref, cos_ref, sin_ref, sink_ref, do_ref, doc_ref,
             dq_ref, dk_ref, dv_ref, dqc_ref, dkc_ref, dvc_ref, dsink_ref,
             qr, ka, kb, va, vb, kca, kcb, vca, vcb, dqs, dka, dva, dkca, dvca):
        cos_t, sin_t = cos_ref[...], sin_ref[...]
        lane = lax.broadcasted_iota(jnp.int32, (1, 128), 1)
        lo = lane < 64
        for gq in range(4):
            qr[:, gq * 128:(gq + 1) * 128] = _rope(q_ref[:, gq * 128:(gq + 1) * 128].astype(F32), cos_t, sin_t).astype(BF16)
        kl = _rope(k_ref[...].astype(F32), cos_t, sin_t)
        dsink_ref[...] = jnp.zeros_like(dsink_ref)
        dka[...] = jnp.zeros_like(dka)
        dva[...] = jnp.zeros_like(dva)
        dkca[...] = jnp.zeros_like(dkca)
        dvca[...] = jnp.zeros_like(dvca)

        def halves(x):
            return jnp.where(lo, x, 0).astype(BF16), jnp.where(lo, 0, x).astype(BF16)

        for kh in range(KV_HEADS):
            _attn_prepare(kh, kl, v_ref[...].astype(F32), kc_ref[...].astype(F32), vc_ref[...].astype(F32),
                          ka, kb, va, vb, kca, kcb, vca, vcb)

            def one_head(head, qp, q_half, do_p, do_half, kw, kcx, vw, vcx, win, valid):
                sk = sink_ref[head:head + 1, 0:1]
                s_list = [_dot_nt(qp, kcx[...]) * ATT_SCALE]
                if win is not None:
                    s_list.insert(0, jnp.where(valid, _dot_nt(qp, kw[win, :]) * ATT_SCALE, NEG_INF))
                probs, p_sink = _softmax_parts(s_list, sk)
                vals = [vcx[...]] if win is None else [vw[win, :], vcx[...]]
                dps = [_dot_nt(do_p, vv) for vv in vals]
                dr = None
                for pp, dp in zip(probs, dps):
                    t = jnp.sum(pp * dp, axis=1, keepdims=True)
                    dr = t if dr is None else dr + t
                dss = [(pp * (dp - dr) * ATT_SCALE).astype(BF16) for pp, dp in zip(probs, dps)]
                dsink_ref[head:head + 1, :] += jnp.broadcast_to(
                    jnp.sum(-p_sink * dr, axis=0, keepdims=True), (1, 128))
                p_c, ds_c = probs[-1], dss[-1]
                dq = _dot(ds_c, kcx[...])
                dkca[kh] += _dot_tn(ds_c, q_half)
                dvca[kh] += _dot_tn(p_c, do_half)
                if win is not None:
                    dq = dq + _dot(dss[0], kw[win, :])
                    dka[kh, win, :] += _dot_tn(dss[0], q_half)
                    dva[kh, win, :] += _dot_tn(probs[0], do_half)
                return dq

            def lat_block(n, carry):
                r0, start, valid = _window(cfg, n)
                win = pl.ds(start, 3 * BLK)
                lanes = [slice((kh * 2 + pr) * 128, (kh * 2 + pr + 1) * 128) for pr in range(2)]
                qps = [qr[pl.ds(r0, BLK), lanes[pr]] for pr in range(2)]
                dops = [do_ref[pl.ds(r0, BLK), lanes[pr]].astype(BF16) for pr in range(2)]
                heads = [(pr, half) for pr in range(2) for half in range(2)]
                kws, kcs = (ka[win, :], kb[win, :]), (kca[...], kcb[...])
                vws, vcs = (va[win, :], vb[win, :]), (vca[...], vcb[...])
                soft = []
                for idx, (pr, half) in enumerate(heads):
                    s_w = jnp.where(valid, _dot_nt(qps[pr], kws[half]) * ATT_SCALE, NEG_INF)
                    s_c = _dot_nt(qps[pr], kcs[half]) * ATT_SCALE
                    soft.append(_softmax_parts([s_w, s_c], sink_ref[kh * 4 + idx:kh * 4 + idx + 1, 0:1]))
                dps = [(_dot_nt(dops[pr], vws[half]), _dot_nt(dops[pr], vcs[half])) for pr, half in heads]
                ds_w, ds_c, pb_w, pb_c = [], [], [], []
                for idx in range(4):
                    (p_w, p_c), p_sink = soft[idx]
                    dp_w, dp_c = dps[idx]
                    dr = jnp.sum(p_w * dp_w, axis=1, keepdims=True) + jnp.sum(p_c * dp_c, axis=1, keepdims=True)
                    ds_w.append((p_w * (dp_w - dr) * ATT_SCALE).astype(BF16))
                    ds_c.append((p_c * (dp_c - dr) * ATT_SCALE).astype(BF16))
                    pb_w.append(p_w.astype(BF16))
                    pb_c.append(p_c.astype(BF16))
                    head = kh * 4 + idx
                    dsink_ref[head:head + 1, :] += jnp.broadcast_to(
                        jnp.sum(-p_sink * dr, axis=0, keepdims=True), (1, 128))
                for pr in range(2):
                    dqs[pl.ds(r0, BLK), lanes[pr]] = (
                        _dot(ds_w[2 * pr], kws[0]) + _dot(ds_c[2 * pr], kcs[0])
                        + _dot(ds_w[2 * pr + 1], kws[1]) + _dot(ds_c[2 * pr + 1], kcs[1]))
                q_hs, do_hs = [halves(qp) for qp in qps], [halves(do_p) for do_p in dops]
                q_stack = jnp.concatenate([q_hs[pr][half] for pr, half in heads], axis=0)
                do_stack = jnp.concatenate([do_hs[pr][half] for pr, half in heads], axis=0)
                dka[kh, win, :] += _dot_tn(jnp.concatenate(ds_w, axis=0), q_stack)
                dva[kh, win, :] += _dot_tn(jnp.concatenate(pb_w, axis=0), do_stack)
                dkca[kh] += _dot_tn(jnp.concatenate(ds_c, axis=0), q_stack)
                dvca[kh] += _dot_tn(jnp.concatenate(pb_c, axis=0), do_stack)
                return carry

            lax.fori_loop(0, n_lat // BLK, lat_block, 0, unroll=2)
            for n in range(n_ctx // BLK):
                rows = slice(n * BLK, (n + 1) * BLK)
                for pr in range(2):
                    lanes = slice((kh * 2 + pr) * 128, (kh * 2 + pr + 1) * 128)
                    qp = qc_ref[rows, lanes].astype(BF16)
                    do_p = doc_ref[rows, lanes]
                    q_h, do_h = halves(qp), halves(do_p)
                    dq = None
                    for half, (kcx, vcx) in enumerate(((kca, vca), (kcb, vcb))):
                        part = one_head(kh * 4 + pr * 2 + half, qp, q_h[half], do_p, do_h[half],
                                        None, kcx, None, vcx, None, None)
                        dq = part if dq is None else dq + part
                    dqc_ref[rows, lanes] = dq.astype(BF16)

        def fold(acc):
            r0 = acc[0] + pltpu.roll(acc[0], 64, 1)
            r1 = acc[1] + pltpu.roll(acc[1], 64, 1)
            return jnp.where(lo, r0, r1)

        for gq in range(4):
            sl = slice(gq * 128, (gq + 1) * 128)
            dq_ref[:, sl] = _rope_t(dqs[:, sl], cos_t, sin_t).astype(BF16)
        dk_ref[...] = _rope_t(fold(dka), cos_t, sin_t).astype(BF16)
        dv_ref[...] = fold(dva).astype(BF16)
        dkc_ref[...] = fold(dkca).astype(BF16)
        dvc_ref[...] = fold(dvca).astype(BF16)

    lat = lambda w: pl.BlockSpec((n_lat, w), lambda e: (e, 0))
    ctx = lambda w: pl.BlockSpec((n_ctx, w), lambda e: (e, 0))
    sd = jax.ShapeDtypeStruct
    return pl.pallas_call(
        body, grid=(2,), name=name,
        in_specs=_attn_specs(cfg) + [pl.BlockSpec((n_lat, ATT_W), lambda e: (e, 0)),
                                     pl.BlockSpec((n_ctx, ATT_W), lambda e: (cb + e, 0))],
        out_specs=[lat(ATT_W), lat(128), lat(128), ctx(ATT_W), ctx(128), ctx(128),
                   pl.BlockSpec((None, 8, 128), lambda e: (e, 0, 0))],
        out_shape=[sd((cfg.t_lat, ATT_W), BF16), sd((cfg.t_lat, 128), BF16), sd((cfg.t_lat, 128), BF16),
                   sd((cfg.t_ctx, ATT_W), BF16), sd((cfg.t_ctx, 128), BF16), sd((cfg.t_ctx, 128), BF16),
                   sd((2, 8, 128), F32)],
        scratch_shapes=[pltpu.VMEM((n_lat, ATT_W), BF16)] + [pltpu.VMEM((n_lat, 128), BF16)] * 4
        + [pltpu.VMEM((n_ctx, 128), BF16)] * 4
        + [pltpu.VMEM((n_lat, ATT_W), F32), pltpu.VMEM((2, n_lat, 128), F32), pltpu.VMEM((2, n_lat, 128), F32),
           pltpu.VMEM((2, n_ctx, 128), F32), pltpu.VMEM((2, n_ctx, 128), F32)],
        compiler_params=_params(("parallel",)))(p, p, p, p, p, p, cos, sin, sink_rows, dcat, dcat)


def _shift_down(x, k, row):
    return jnp.where(row >= k, pltpu.roll(x, k, 0), 0.0)


def _shift_up(x, k, row):
    n = x.shape[0]
    return jnp.where(row < n - k, pltpu.roll(x, n - k, 0), 0.0)


def _window_sum(x, r, row):
    below, above, k = x, x, 1
    while k < r:
        below = below + _shift_down(below, k, row)
        above = above + _shift_up(above, k, row)
        k *= 2
    return below + _shift_down(x, r, row) + _shift_up(above, 1, row)


def _inv_count(r, row, n):
    cnt = jnp.minimum(row + r, n - 1) + 1 - jnp.maximum(row - r, 0)
    return 1.0 / cnt.astype(F32)


def _pool_fwd(p, w, scale, n, blk0, n_seg, name):
    def body(u0, u1, u2, u3, w_ref, sc_ref, o_ref):
        row = lax.broadcasted_iota(jnp.int32, (n, 1), 0)
        for g, u_ref in enumerate((u0, u1, u2, u3)):
            u = u_ref[...].astype(F32)
            d = _window_sum(u, POOL_R[g], row) * _inv_count(POOL_R[g], row, n) - u
            o_ref[:, g * 128:(g + 1) * 128] = (_dot(d, w_ref[g]) * sc_ref[:, g * 128:(g + 1) * 128]).astype(BF16)

    return pl.pallas_call(
        body, grid=(n_seg,), name=name,
        in_specs=[pl.BlockSpec((n, 128), functools.partial(lambda g, e: (blk0 + e, 6 + g), g)) for g in range(4)]
        + [pl.BlockSpec((4, 128, 128), lambda e: (0, 0, 0)), pl.BlockSpec((1, 512), lambda e: (0, 0))],
        out_specs=pl.BlockSpec((n, 512), lambda e: (e, 0)),
        out_shape=jax.ShapeDtypeStruct((n_seg * n, 512), BF16),
        compiler_params=_params(("parallel",)))(p, p, p, p, w, scale)


def _pool_bwd(p, w, scale, dcat, n, blk0, n_seg, name):
    def body(u0, u1, u2, u3, w_ref, sc_ref, dp_ref, du_ref, dw_ref, dsc_ref):
        e = pl.program_id(0)

        @pl.when(e == 0)
        def _():
            dw_ref[...] = jnp.zeros_like(dw_ref)
            dsc_ref[...] = jnp.zeros_like(dsc_ref)

        row = lax.broadcasted_iota(jnp.int32, (n, 1), 0)
        for g, u_ref in enumerate((u0, u1, u2, u3)):
            sl = slice(g * 128, (g + 1) * 128)
            u = u_ref[...].astype(F32)
            inv = _inv_count(POOL_R[g], row, n)
            d = _window_sum(u, POOL_R[g], row) * inv - u
            dp = dp_ref[:, sl].astype(F32)
            dsc_ref[:, sl] += jnp.sum(dp * _dot(d, w_ref[g]), axis=0, keepdims=True)
            dyp = dp * sc_ref[:, sl]
            dw_ref[g] += _dot_tn(d, dyp)
            dd = _dot_nt(dyp, w_ref[g])
            du_ref[:, sl] = (_window_sum(dd * inv, POOL_R[g], row) - dd).astype(BF16)

    return pl.pallas_call(
        body, grid=(n_seg,), name=name,
        in_specs=[pl.BlockSpec((n, 128), functools.partial(lambda g, e: (blk0 + e, 6 + g), g)) for g in range(4)]
        + [pl.BlockSpec((4, 128, 128), lambda e: (0, 0, 0)), pl.BlockSpec((1, 512), lambda e: (0, 0)),
           pl.BlockSpec((n, 512), lambda e: (blk0 + e, 1))],
        out_specs=[pl.BlockSpec((n, 512), lambda e: (e, 0)),
                   pl.BlockSpec((4, 128, 128), lambda e: (0, 0, 0)), pl.BlockSpec((1, 512), lambda e: (0, 0))],
        out_shape=[jax.ShapeDtypeStruct((n_seg * n, 512), BF16), jax.ShapeDtypeStruct((4, 128, 128), F32),
                   jax.ShapeDtypeStruct((1, 512), F32)],
        compiler_params=_params(("arbitrary",)))(p, p, p, p, w, scale, dcat)


def _gelu(x):
    t = jnp.tanh(math.sqrt(2.0 / math.pi) * (x + 0.044715 * x * x * x))
    return 0.5 * x * (1.0 + t), t


def _gelu_grad(x, t):
    return 0.5 * (1.0 + t) + 0.5 * x * (1.0 - t * t) * (math.sqrt(2.0 / math.pi) * (1.0 + 3 * 0.044715 * x * x))


def _neg_expm1_twice(x):
    t = jnp.tanh(x)
    return (-2.0 * t) / (1.0 - t)


def _softplus_neg(lam):
    x = -lam
    e = jnp.exp(-jnp.abs(x))
    log1p = jnp.where(e < 1e-2, e * (1.0 - e * (0.5 - e * (1.0 / 3.0))), jnp.log(1.0 + e))
    return jnp.maximum(x, 0.0) + log1p, -_sigmoid(x)


def _conv(u, w_ref, b_ref, row):
    return (b_ref[...] + _shift_down(u, 1, row) * w_ref[0:1, :] + u * w_ref[1:2, :]
            + _shift_up(u, 1, row) * w_ref[2:3, :] + _shift_up(u, 2, row) * w_ref[3:4, :])


def _lru_gates(uc, d, wa_ref, ba_ref, wx_ref, bx_ref, lam_ref):
    r = _sigmoid(_dot(uc, wa_ref[d]) + ba_ref[d:d + 1, :])
    gi = _sigmoid(_dot(uc, wx_ref[d]) + bx_ref[d:d + 1, :])
    sp, dsp = _softplus_neg(lam_ref[d:d + 1, :])
    la = (-LRU_C) * r * sp
    a = jnp.exp(la)
    sq = jnp.sqrt(_neg_expm1_twice(la))
    return r, gi, sp, dsp, a, sq


def _tile_scan(a_ref, b_ref, n, reverse):
    m = n // 8
    first = 7 if reverse else 0
    a_prev = a_ref[pl.ds(first, m, stride=8), :]
    b_prev = b_ref[pl.ds(first, m, stride=8), :]
    for j in (range(6, -1, -1) if reverse else range(1, 8)):
        rows = pl.ds(j, m, stride=8)
        aj = a_ref[rows, :]
        b_prev = aj * b_prev + b_ref[rows, :]
        a_prev = aj * a_prev
        b_ref[rows, :] = b_prev
        a_ref[rows, :] = a_prev


def _carry_scan(a_ref, b_ref, n, reverse, carry):
    nt8 = n // 8

    def step(i, c):
        t = (nt8 - 1 - i) if reverse else i
        off = pl.multiple_of(t * 8, 8)
        h = a_ref[pl.ds(off, 8), :] * c + b_ref[pl.ds(off, 8), :]
        b_ref[pl.ds(off, 8), :] = h
        return h[0:1, :] if reverse else h[7:8, :]

    return lax.fori_loop(0, nt8, step, carry, unroll=8)


def _chain_scan(segs, reverse):
    carry = jnp.zeros((1, 128), F32)
    for a, b, a_ref, b_ref, n in segs:
        a_ref[...] = a
        b_ref[...] = b
        _tile_scan(a_ref, b_ref, n, reverse)
        carry = _carry_scan(a_ref, b_ref, n, reverse, carry)


def _lru_specs(cfg):
    n_lat, n_ctx, cb = cfg.n_lat, cfg.n_ctx, cfg.ctx_blk
    return [pl.BlockSpec((n_lat, 128), lambda hb, e: (e, hb)),
            pl.BlockSpec((n_lat, 128), lambda hb, e: (e, 8 + hb)),
            pl.BlockSpec((n_ctx, 128), lambda hb, e: (cb + e, hb)),
            pl.BlockSpec((n_ctx, 128), lambda hb, e: (cb + e, 8 + hb)),
            pl.BlockSpec((4, 128), lambda hb, e: (0, hb)),
            pl.BlockSpec((1, 128), lambda hb, e: (0, hb)),
            pl.BlockSpec((2, None, 128, 128), lambda hb, e: (0, hb, 0, 0)),
            pl.BlockSpec((2, 128), lambda hb, e: (0, hb)),
            pl.BlockSpec((2, None, 128, 128), lambda hb, e: (0, hb, 0, 0)),
            pl.BlockSpec((2, 128), lambda hb, e: (0, hb)),
            pl.BlockSpec((2, 128), lambda hb, e: (0, hb))]


def _lru_fwd(cfg, p, consts, name):
    n_lat, n_ctx = cfg.n_lat, cfg.n_ctx

    def body(gl_ref, ul_ref, gc_ref, uc_ref, cw_ref, cb_ref, wa_ref, ba_ref, wx_ref, bx_ref, lam_ref,
             zl_ref, zc_ref, hl_ref, hc_ref, al, ac):
        row_l = lax.broadcasted_iota(jnp.int32, (n_lat, 1), 0)
        row_c = lax.broadcasted_iota(jnp.int32, (n_ctx, 1), 0)
        uc_l = _conv(ul_ref[...].astype(F32), cw_ref, cb_ref, row_l)
        uc_c = _conv(uc_ref[...].astype(F32), cw_ref, cb_ref, row_c)
        for d in range(2):
            _, gi_l, _, _, a_l, sq_l = _lru_gates(uc_l, d, wa_ref, ba_ref, wx_ref, bx_ref, lam_ref)
            _, gi_c, _, _, a_c, sq_c = _lru_gates(uc_c, d, wa_ref, ba_ref, wx_ref, bx_ref, lam_ref)
            _chain_scan([(a_c, sq_c * (gi_c * uc_c), ac, hc_ref.at[d], n_ctx),
                         (a_l, sq_l * (gi_l * uc_l), al, hl_ref.at[d], n_lat)], reverse=(d == 1))
        zl_ref[...] = (_gelu(gl_ref[...].astype(F32))[0] * (hl_ref[0] + hl_ref[1])).astype(BF16)
        zc_ref[...] = (_gelu(gc_ref[...].astype(F32))[0] * (hc_ref[0] + hc_ref[1])).astype(BF16)

    return pl.pallas_call(
        body, grid=(8, 2), name=name, in_specs=_lru_specs(cfg),
        out_specs=[pl.BlockSpec((n_lat, 128), lambda hb, e: (e, hb)), pl.BlockSpec((n_ctx, 128), lambda hb, e: (e, hb)),
                   pl.BlockSpec((2, n_lat, 128), lambda hb, e: (0, e, hb)),
                   pl.BlockSpec((2, n_ctx, 128), lambda hb, e: (0, e, hb))],
        out_shape=[jax.ShapeDtypeStruct((cfg.t_lat, D), BF16), jax.ShapeDtypeStruct((cfg.t_ctx, D), BF16),
                   jax.ShapeDtypeStruct((2, cfg.t_lat, D), F32), jax.ShapeDtypeStruct((2, cfg.t_ctx, D), F32)],
        scratch_shapes=[pltpu.VMEM((n_lat, 128), F32), pltpu.VMEM((n_ctx, 128), F32)],
        compiler_params=_params(("parallel", "arbitrary")))(p, p, p, p, *consts)


def _lru_bwd(cfg, p, dz, h_lat, h_ctx, consts, name):
    n_lat, n_ctx, cb = cfg.n_lat, cfg.n_ctx, cfg.ctx_blk

    def body(gl_ref, ul_ref, gc_ref, uc_ref, cw_ref, cb_ref, wa_ref, ba_ref, wx_ref, bx_ref, lam_ref,
             dzl_ref, dzc_ref, hl, hc, dgl_ref, dul_ref, dgc_ref, duc_ref, dwa_ref, dwx_ref, vec_ref,
             al, bl, ac, bc):
        e = pl.program_id(1)

        @pl.when(e == 0)
        def _():
            dwa_ref[...] = jnp.zeros_like(dwa_ref)
            dwx_ref[...] = jnp.zeros_like(dwx_ref)
            vec_ref[...] = jnp.zeros_like(vec_ref)

        row_l = lax.broadcasted_iota(jnp.int32, (n_lat, 1), 0)
        row_c = lax.broadcasted_iota(jnp.int32, (n_ctx, 1), 0)
        u_l, u_c = ul_ref[...].astype(F32), uc_ref[...].astype(F32)
        uc_l = _conv(u_l, cw_ref, cb_ref, row_l)
        uc_c = _conv(u_c, cw_ref, cb_ref, row_c)
        gel_l, t_l = _gelu(gl_ref[...].astype(F32))
        gel_c, t_c = _gelu(gc_ref[...].astype(F32))
        dz_l, dz_c = dzl_ref[...].astype(F32), dzc_ref[...].astype(F32)
        dgl_ref[...] = (dz_l * (hl[0] + hl[1]) * _gelu_grad(gl_ref[...].astype(F32), t_l)).astype(BF16)
        dgc_ref[...] = (dz_c * (hc[0] + hc[1]) * _gelu_grad(gc_ref[...].astype(F32), t_c)).astype(BF16)
        dy_l, dy_c = dz_l * gel_l, dz_c * gel_c
        duc_l = jnp.zeros((n_lat, 128), F32)
        duc_c = jnp.zeros((n_ctx, 128), F32)
        for d in range(2):
            r_l, gi_l, sp, dsp, a_l, sq_l = _lru_gates(uc_l, d, wa_ref, ba_ref, wx_ref, bx_ref, lam_ref)
            r_c, gi_c, _, _, a_c, sq_c = _lru_gates(uc_c, d, wa_ref, ba_ref, wx_ref, bx_ref, lam_ref)
            if d == 0:
                an_l = _shift_up(a_l, 1, row_l)
                an_c = jnp.where(row_c < n_ctx - 1, pltpu.roll(a_c, n_ctx - 1, 0), a_l[0:1, :])
            else:
                an_l = _shift_down(a_l, 1, row_l)
                an_c = jnp.where(row_c >= 1, pltpu.roll(a_c, 1, 0), a_l[n_lat - 1:n_lat, :])
            _chain_scan([(an_l, dy_l, al, bl, n_lat), (an_c, dy_c, ac, bc, n_ctx)], reverse=(d == 0))
            dsp_sum = jnp.zeros((1, 128), F32)
            for (dh, h, r, gi, a, sq, uc, seg) in ((bl[...], hl[d], r_l, gi_l, a_l, sq_l, uc_l, "l"),
                                                  (bc[...], hc[d], r_c, gi_c, a_c, sq_c, uc_c, "c")):
                b0 = sq * (gi * uc)
                t1 = dh * sq
                dla = dh * (h - b0) - (dh * gi * uc) * (a * a) / sq
                dzr = (dla * ((-LRU_C) * sp)) * r * (1.0 - r)
                dzi = (t1 * uc) * gi * (1.0 - gi)
                dsp_sum = dsp_sum + jnp.sum(dla * ((-LRU_C) * r), axis=0, keepdims=True)
                dwa_ref[d] += _dot_tn(uc, dzr)
                dwx_ref[d] += _dot_tn(uc, dzi)
                vec_ref[d:d + 1, :] += jnp.sum(dzr, axis=0, keepdims=True)
                vec_ref[2 + d:3 + d, :] += jnp.sum(dzi, axis=0, keepdims=True)
                duc = t1 * gi + _dot_nt(dzr, wa_ref[d]) + _dot_nt(dzi, wx_ref[d])
                if seg == "l":
                    duc_l = duc_l + duc
                else:
                    duc_c = duc_c + duc
            vec_ref[4 + d:5 + d, :] += dsp_sum * dsp
        for duc, u, row, du_ref in ((duc_l, u_l, row_l, dul_ref), (duc_c, u_c, row_c, duc_ref)):
            du_ref[...] = (_shift_up(duc, 1, row) * cw_ref[0:1, :] + duc * cw_ref[1:2, :]
                           + _shift_down(duc, 1, row) * cw_ref[2:3, :]
                           + _shift_down(duc, 2, row) * cw_ref[3:4, :]).astype(BF16)
            vec_ref[6:7, :] += jnp.sum(duc * _shift_down(u, 1, row), axis=0, keepdims=True)
            vec_ref[7:8, :] += jnp.sum(duc * u, axis=0, keepdims=True)
            vec_ref[8:9, :] += jnp.sum(duc * _shift_up(u, 1, row), axis=0, keepdims=True)
            vec_ref[9:10, :] += jnp.sum(duc * _shift_up(u, 2, row), axis=0, keepdims=True)
            vec_ref[10:11, :] += jnp.sum(duc, axis=0, keepdims=True)

    lat = pl.BlockSpec((n_lat, 128), lambda hb, e: (e, hb))
    ctx = pl.BlockSpec((n_ctx, 128), lambda hb, e: (e, hb))
    wspec = pl.BlockSpec((2, None, 128, 128), lambda hb, e: (0, hb, 0, 0))
    sd = jax.ShapeDtypeStruct
    return pl.pallas_call(
        body, grid=(8, 2), name=name,
        in_specs=_lru_specs(cfg) + [pl.BlockSpec((n_lat, 128), lambda hb, e: (e, hb)),
                                    pl.BlockSpec((n_ctx, 128), lambda hb, e: (cb + e, hb)),
                                    pl.BlockSpec((2, n_lat, 128), lambda hb, e: (0, e, hb)),
                                    pl.BlockSpec((2, n_ctx, 128), lambda hb, e: (0, e, hb))],
        out_specs=[lat, lat, ctx, ctx, wspec, wspec, pl.BlockSpec((None, 16, 128), lambda hb, e: (hb, 0, 0))],
        out_shape=[sd((cfg.t_lat, D), BF16), sd((cfg.t_lat, D), BF16), sd((cfg.t_ctx, D), BF16), sd((cfg.t_ctx, D), BF16),
                   sd((2, 8, 128, 128), F32), sd((2, 8, 128, 128), F32), sd((8, 16, 128), F32)],
        scratch_shapes=[pltpu.VMEM((n_lat, 128), F32)] * 2 + [pltpu.VMEM((n_ctx, 128), F32)] * 2,
        compiler_params=_params(("parallel", "arbitrary")))(p, p, p, p, *consts, dz, dz, h_lat, h_ctx)


def _position():
    x, y, c = lax.axis_index("x"), lax.axis_index("y"), lax.axis_index("c")
    return x, y, c, 4 * x + 2 * y + c


def _peer(x, y, c, k):
    px = 1 - x if k & 4 else x
    py = 1 - y if k & 2 else y
    pc = 1 - c if k & 1 else c
    return (px, py, pc), 4 * px + 2 * py + pc


def _all_gather(v, name, in_vmem):
    def body(v_ref, o_ref, send_sems, recv_sems, local_sem):
        x, y, c, me = _position()
        mine = pltpu.make_async_copy(v_ref, o_ref.at[me], local_sem)
        mine.start()
        sends = []
        for k in range(1, N_DEV):
            peer, _ = _peer(x, y, c, k)
            cp = pltpu.make_async_remote_copy(src_ref=v_ref, dst_ref=o_ref.at[me], send_sem=send_sems.at[k - 1],
                                              recv_sem=recv_sems.at[k - 1], device_id=peer, device_id_type=MESH)
            cp.start()
            sends.append(cp)
        for k in range(1, N_DEV):
            peer, peer_lin = _peer(x, y, c, k)
            pltpu.make_async_remote_copy(src_ref=v_ref, dst_ref=o_ref.at[peer_lin], send_sem=send_sems.at[k - 1],
                                         recv_sem=recv_sems.at[k - 1], device_id=peer, device_id_type=MESH).wait_recv()
        for cp in sends:
            cp.wait_send()
        mine.wait()

    space = pltpu.VMEM if in_vmem else pl.ANY
    return pl.pallas_call(
        body, name=name,
        in_specs=[pl.BlockSpec(memory_space=space)], out_specs=pl.BlockSpec(memory_space=space),
        out_shape=jax.ShapeDtypeStruct((N_DEV,) + v.shape, v.dtype),
        scratch_shapes=[pltpu.SemaphoreType.DMA((N_DEV - 1,)), pltpu.SemaphoreType.DMA((N_DEV - 1,)),
                        pltpu.SemaphoreType.DMA],
        compiler_params=pltpu.CompilerParams(vmem_limit_bytes=VMEM_LIMIT))(v)


_HBM = pl.BlockSpec(memory_space=pltpu.HBM)
_SEM = pl.BlockSpec(memory_space=pltpu.SEMAPHORE)
_EFFECT = pltpu.SideEffectType.DATAFLOW_SIDE_EFFECTING


ALL_PEERS = tuple(range(1, N_DEV))
SAME_CORE_AND_SIBLING = (1, 2, 4, 6)


def _push_start(src, land, block_of, name, relations=ALL_PEERS):
    def body(src_ref, land_ref, send_sem, recv_sem, src_thru, land_thru, token):
        x, y, c, me = _position()
        for k in relations:
            peer, peer_lin = _peer(x, y, c, k)
            mine, there = block_of(src_ref, land_ref, me, peer_lin)
            pltpu.make_async_remote_copy(src_ref=mine, dst_ref=there, send_sem=send_sem, recv_sem=recv_sem,
                                         device_id=peer, device_id_type=MESH).start()
        mine, here = block_of(src_ref, land_ref, me, me)
        pltpu.make_async_copy(mine, here, recv_sem).start()
        token[...] = jnp.zeros_like(token)

    return pl.pallas_call(
        body, name=name,
        out_shape=(pltpu.SemaphoreType.DMA(()), pltpu.SemaphoreType.DMA(()), pltpu.HBM(src.shape, src.dtype),
                   pltpu.HBM(land.shape, land.dtype), jax.ShapeDtypeStruct((8, 128), F32)),
        in_specs=(_HBM, _HBM), out_specs=(_SEM, _SEM, _HBM, _HBM, pl.BlockSpec(memory_space=pltpu.VMEM)),
        input_output_aliases={0: 2, 1: 3},
        compiler_params=pltpu.CompilerParams(has_side_effects=_EFFECT),
    )(pltpu.with_memory_space_constraint(src, pltpu.HBM), pltpu.with_memory_space_constraint(land, pltpu.HBM))


def _push_wait(handle, blocks_of, after, name, n_peers=N_DEV - 1):
    send_sem, recv_sem, src_thru, land_thru, _ = handle

    def body(src_ref, land_ref, send_sem, recv_sem, after_ref, src_dead, got_ref):
        x, y, c, _ = _position()
        sent, landed = blocks_of(land_ref, n_peers), blocks_of(land_ref, n_peers + 1)
        pltpu.make_async_remote_copy(src_ref=sent, dst_ref=sent, send_sem=send_sem, recv_sem=recv_sem,
                                     device_id=(x, y, 1 - c), device_id_type=MESH).wait_send()
        pltpu.make_async_remote_copy(src_ref=landed, dst_ref=landed, send_sem=send_sem, recv_sem=recv_sem,
                                     device_id=(x, y, 1 - c), device_id_type=MESH).wait_recv()

    return pl.pallas_call(
        body, name=name,
        out_shape=(pltpu.HBM(src_thru.shape, src_thru.dtype), pltpu.HBM(land_thru.shape, land_thru.dtype)),
        in_specs=(_HBM, _HBM, _SEM, _SEM, pl.BlockSpec(memory_space=pl.ANY)), out_specs=(_HBM, _HBM),
        input_output_aliases={0: 0, 1: 1},
        compiler_params=pltpu.CompilerParams(has_side_effects=_EFFECT),
    )(src_thru, land_thru, send_sem, recv_sem, after)[1]


def _gather_start(src, name, relations=ALL_PEERS):
    g, r, C = src.shape
    land = lax.empty((g, N_DEV * r, C), src.dtype)
    return _push_start(src, land, lambda s, z, i, p: (s, z.at[:, pl.ds(i * r, r), :]), name, relations)


def _gather_wait(handle, after, name, n_peers=N_DEV - 1):
    r = handle[2].shape[1]
    return _push_wait(handle, lambda z, n: z.at[:, pl.ds(0, n * r), :], after, name, n_peers)


def _relay_start(land, r, name):
    def body(land_ref, send_sem, recv_sem, land_thru, token):
        x, y, c, _ = _position()
        for k in (2, 4, 6):
            _, origin = _peer(x, y, c, k)
            rows = land_ref.at[:, pl.ds(origin * r, r), :]
            pltpu.make_async_remote_copy(src_ref=rows, dst_ref=rows, send_sem=send_sem, recv_sem=recv_sem,
                                         device_id=(x, y, 1 - c), device_id_type=MESH).start()
        token[...] = jnp.zeros_like(token)

    return pl.pallas_call(
        body, name=name,
        out_shape=(pltpu.SemaphoreType.DMA(()), pltpu.SemaphoreType.DMA(()), pltpu.HBM(land.shape, land.dtype),
                   jax.ShapeDtypeStruct((8, 128), F32)),
        in_specs=(_HBM,), out_specs=(_SEM, _SEM, _HBM, pl.BlockSpec(memory_space=pltpu.VMEM)),
        input_output_aliases={0: 2},
        compiler_params=pltpu.CompilerParams(has_side_effects=_EFFECT),
    )(pltpu.with_memory_space_constraint(land, pltpu.HBM))


def _relay_wait(handle, r, after, name):
    send_sem, recv_sem, land_thru, _ = handle

    def body(land_ref, send_sem, recv_sem, after_ref, got_ref):
        x, y, c, _ = _position()
        three = land_ref.at[:, pl.ds(0, 3 * r), :]
        cp = pltpu.make_async_remote_copy(src_ref=three, dst_ref=three, send_sem=send_sem, recv_sem=recv_sem,
                                          device_id=(x, y, 1 - c), device_id_type=MESH)
        cp.wait_send()
        cp.wait_recv()

    return pl.pallas_call(
        body, name=name, out_shape=(pltpu.HBM(land_thru.shape, land_thru.dtype),),
        in_specs=(_HBM, _SEM, _SEM, pl.BlockSpec(memory_space=pl.ANY)), out_specs=(_HBM,),
        input_output_aliases={0: 0},
        compiler_params=pltpu.CompilerParams(has_side_effects=_EFFECT),
    )(land_thru, send_sem, recv_sem, after)[0]


def _exchange_start(grad, name):
    g, rows, C = grad.shape
    r = rows // N_DEV
    land = lax.empty((N_DEV, g, r, C), grad.dtype)
    return _push_start(grad, land, lambda s, z, i, p: (s.at[:, pl.ds(p * r, r), :], z.at[i]), name)


def _exchange_wait(handle, after, name):
    return _push_wait(handle, lambda z, n: z.at[pl.ds(0, n)], after, name)


def _sum_blocks(v, name):
    k, rows, cols = v.shape
    tr = rows
    for cand in (rows, 512, 352, 256, 176, 128, 64, 32, 16):
        if rows % cand == 0 and k * cand * cols * v.dtype.itemsize <= 6 * 1024 * 1024:
            tr = cand
            break

    def body(v_ref, o_ref):
        acc = v_ref[0].astype(F32)
        for s in range(1, k):
            acc = acc + v_ref[s].astype(F32)
        o_ref[...] = acc

    return pl.pallas_call(
        body, grid=(rows // tr,), name=name,
        in_specs=[pl.BlockSpec((k, tr, cols), lambda i: (0, i, 0))],
        out_specs=pl.BlockSpec((tr, cols), lambda i: (i, 0)),
        out_shape=jax.ShapeDtypeStruct((rows, cols), F32),
        compiler_params=_params(("parallel",)))(v)


def _adam_math(w, g, m, v):
    m2 = B1 * m + (1.0 - B1) * g
    v2 = B2 * v + (1.0 - B2) * (g * g)
    m_hat = m2 / (1.0 - B1 ** STEP)
    v_hat = v2 / (1.0 - B2 ** STEP)
    return -LR * (m_hat / (jnp.sqrt(v_hat) + EPS) + WD * w), m2, v2


def _adamw(w, g, m, v, name, dep=None):
    shp = w.shape
    rows, cols = (shp[-2], shp[-1]) if len(shp) >= 2 else (1, shp[-1])
    lead = math.prod(shp[:-2]) if len(shp) > 2 else 1
    fits = [t for t in range(8, rows + 1, 8) if rows % t == 0 and t * cols * 4 <= 2 * 1024 * 1024]
    tr = max(fits) if fits else rows

    def body(w_ref, g_ref, m_ref, v_ref, *rest):
        d_ref, m2_ref, v2_ref = rest[-3:]
        d_ref[...], m2_ref[...], v2_ref[...] = _adam_math(w_ref[...], g_ref[...], m_ref[...], v_ref[...])

    blk = pl.BlockSpec((None, tr, cols), lambda b, i: (b, i, 0))
    extra = [] if dep is None else [dep]
    outs = pl.pallas_call(
        body, grid=(lead, rows // tr), name=name,
        in_specs=[blk] * 4 + [pl.BlockSpec(memory_space=pl.ANY)] * len(extra), out_specs=[blk] * 3,
        out_shape=[jax.ShapeDtypeStruct((lead, rows, cols), F32)] * 3,
        compiler_params=_params(("parallel", "parallel")))(*[a.reshape(lead, rows, cols) for a in (w, g, m, v)], *extra)
    return [o.reshape(shp) for o in outs]


def _as2d(a):
    n = a.size
    if n % 1024 == 0:
        return a.reshape(n // 1024, 1024)
    if n % 128 == 0:
        return a.reshape(n // 128, 128)
    return a.reshape(1, n)


def _blocks_to_cols(a):
    b = jnp.moveaxis(a, 0, -2)
    return b.reshape(b.shape[:-2] + (b.shape[-2] * b.shape[-1],))


def _pack_rows(parts):
    padded, offs, r = [], [], 0
    for p in parts:
        pad = (-p.shape[0]) % 8
        padded.append(jnp.pad(p, ((0, pad), (0, 0))) if pad else p)
        offs.append(r)
        r += p.shape[0] + pad
    return jnp.concatenate(padded, axis=0), offs


def _silu(x):
    return x * jax.nn.sigmoid(x)


def kernel(x, c, ctx, c_ctx, w_mod, b_mod, ln_g, ln_b, ffn_w_gate, ffn_w_up, ffn_w_down, mix_ab_w_in, attn_sink, pool_w, pool_scale, mix_ab_w_out, lru_w_in, lru_conv_w, lru_conv_b, lru_wa, lru_ba, lru_wx, lru_bx, lru_lambda, lru_w_out, loss_target, m_c_ctx, m_w_mod, m_b_mod, m_ln_g, m_ln_b, m_ffn_w_gate, m_ffn_w_up, m_ffn_w_down, m_mix_ab_w_in, m_attn_sink, m_pool_w, m_pool_scale, m_mix_ab_w_out, m_lru_w_in, m_lru_conv_w, m_lru_conv_b, m_lru_wa, m_lru_ba, m_lru_wx, m_lru_bx, m_lru_lambda, m_lru_w_out, v_c_ctx, v_w_mod, v_b_mod, v_ln_g, v_ln_b, v_ffn_w_gate, v_ffn_w_up, v_ffn_w_down, v_mix_ab_w_in, v_attn_sink, v_pool_w, v_pool_scale, v_mix_ab_w_out, v_lru_w_in, v_lru_conv_w, v_lru_conv_b, v_lru_wa, v_lru_ba, v_lru_wx, v_lru_bx, v_lru_lambda, v_lru_w_out):
    weights = dict(c_ctx=c_ctx, w_mod=w_mod, b_mod=b_mod, ln_g=ln_g, ln_b=ln_b, ffn_w_gate=ffn_w_gate,
                   ffn_w_up=ffn_w_up, ffn_w_down=ffn_w_down, mix_ab_w_in=mix_ab_w_in, attn_sink=attn_sink,
                   pool_w=pool_w, pool_scale=pool_scale, mix_ab_w_out=mix_ab_w_out, lru_w_in=lru_w_in,
                   lru_conv_w=lru_conv_w, lru_conv_b=lru_conv_b, lru_wa=lru_wa, lru_ba=lru_ba, lru_wx=lru_wx,
                   lru_bx=lru_bx, lru_lambda=lru_lambda, lru_w_out=lru_w_out)
    mom_m = dict(c_ctx=m_c_ctx, w_mod=m_w_mod, b_mod=m_b_mod, ln_g=m_ln_g, ln_b=m_ln_b, ffn_w_gate=m_ffn_w_gate,
                 ffn_w_up=m_ffn_w_up, ffn_w_down=m_ffn_w_down, mix_ab_w_in=m_mix_ab_w_in, attn_sink=m_attn_sink,
                 pool_w=m_pool_w, pool_scale=m_pool_scale, mix_ab_w_out=m_mix_ab_w_out, lru_w_in=m_lru_w_in,
                 lru_conv_w=m_lru_conv_w, lru_conv_b=m_lru_conv_b, lru_wa=m_lru_wa, lru_ba=m_lru_ba, lru_wx=m_lru_wx,
                 lru_bx=m_lru_bx, lru_lambda=m_lru_lambda, lru_w_out=m_lru_w_out)
    mom_v = dict(c_ctx=v_c_ctx, w_mod=v_w_mod, b_mod=v_b_mod, ln_g=v_ln_g, ln_b=v_ln_b, ffn_w_gate=v_ffn_w_gate,
                 ffn_w_up=v_ffn_w_up, ffn_w_down=v_ffn_w_down, mix_ab_w_in=v_mix_ab_w_in, attn_sink=v_attn_sink,
                 pool_w=v_pool_w, pool_scale=v_pool_scale, mix_ab_w_out=v_mix_ab_w_out, lru_w_in=v_lru_w_in,
                 lru_conv_w=v_lru_conv_w, lru_conv_b=v_lru_conv_b, lru_wa=v_lru_wa, lru_ba=v_lru_ba, lru_wx=v_lru_wx,
                 lru_bx=v_lru_bx, lru_lambda=v_lru_lambda, lru_w_out=v_lru_w_out)
    names = list(weights)

    n_lat, n_ctx = x.shape[1], ctx.shape[1]
    cfg = _Cfg(n_lat, n_ctx)
    _, _, _, me = _position()
    mcols = w_mod.shape[2]

    def t_bf16(w):
        return jnp.swapaxes(w, -1, -2).astype(BF16)

    def ffn_src(l, i):
        return jnp.stack([t_bf16(ffn_w_gate[l, i]), t_bf16(ffn_w_up[l, i]), ffn_w_down[l, i].astype(BF16)])

    pending = {}

    def start_gathers(items, tok):
        for key, make_src in items:
            pending[key] = _gather_start(make_src() + tok.astype(BF16), "gather_start_" + key)
            tok = pending[key][4][0, 0]
        return tok

    def weights_now(key, after):
        return _gather_wait(pending[key], after, "gather_wait_" + key)

    first = _gather_start(ffn_src(0, 0), "gather_start_ffn00", SAME_CORE_AND_SIBLING)
    tok = first[4][0, 0]

    small_names = ["ln_g", "ln_b", "lru_conv_w", "lru_conv_b", "lru_ba", "lru_bx", "lru_lambda"]
    small, small_off = _pack_rows([(c + tok).reshape(-1, 128)] + [weights[n].reshape(-1, 128) for n in small_names])
    small_all = _all_gather(small, "gather_small", True)

    def small_full(idx, shp):
        rows = math.prod(shp) // 128
        return _blocks_to_cols(small_all[:, small_off[idx]:small_off[idx] + rows, :].reshape((N_DEV,) + shp))

    c_all = small_all[:, :2 * D // 128, :].reshape(2 * N_DEV, D)
    ln_g_f, ln_b_f = small_full(1, ln_g.shape), small_full(2, ln_b.shape)
    lru_consts = (small_full(3, lru_conv_w.shape)[0], small_full(4, lru_conv_b.shape), lru_wa[0],
                  small_full(5, lru_ba.shape)[0], lru_wx[0], small_full(6, lru_bx.shape)[0],
                  small_full(7, lru_lambda.shape)[0])

    s_rows = jnp.zeros((32, D), F32).at[:16].set(_silu(c_all)).at[16].set(_silu(c_ctx)).astype(BF16)
    mod_mine = jnp.stack([_matmul(s_rows, w_mod[l], "nn", F32, "mod_fwd", bn_cap=1280) for l in range(2)])
    mod_all = _all_gather(mod_mine.reshape(64, mcols), "gather_mod", True).reshape(N_DEV, 2, 32, mcols)
    r_ffn = ffn_w_down.shape[2]
    relay = _relay_start(_gather_wait(first, mod_all, "gather_wait_ffn00", n_peers=len(SAME_CORE_AND_SIBLING)),
                         r_ffn, "gather_relay_start_ffn00")
    tok = start_gathers([("ab_in", lambda: t_bf16(mix_ab_w_in)), ("ab_out", lambda: mix_ab_w_out.astype(BF16)),
                         ("ffn01", lambda: ffn_src(0, 1)), ("ffn10", lambda: ffn_src(1, 0)),
                         ("lru_in", lambda: t_bf16(lru_w_in)), ("lru_out", lambda: lru_w_out.astype(BF16)),
                         ("ffn11", lambda: ffn_src(1, 1))], relay[3][0, 0])
    mod_full = _blocks_to_cols(mod_all) + (b_mod[:, None, :] + tok)
    ex0 = 2 * me
    mods = []
    for l in range(2):
        rows = jnp.stack([lax.dynamic_index_in_dim(mod_full[l], ex0, 0, False),
                          lax.dynamic_index_in_dim(mod_full[l], ex0 + 1, 0, False), mod_full[l, 16]])
        mods.append(rows.reshape(3, N_MOD, D))

    h0 = jnp.concatenate([x.reshape(cfg.t_lat, D), ctx.reshape(cfg.t_ctx, D)], axis=0)
    cos, sin = _rope_tables(n_lat)
    sink_rows = jnp.broadcast_to(attn_sink[0][:, None], (8, 128)).astype(F32)

    saved = []
    wf = [[None, None], [None, None]]
    h = h0
    xin = _modulate(cfg, h0, mods[0], 0, 1, "modulate_in")
    for l in range(2):
        st = {"h_in": h, "xin1": xin}
        wf[l][0] = (_relay_wait(relay, r_ffn, xin, "gather_relay_wait_ffn00") if l == 0
                    else weights_now("ffn10", xin))
        g1, u1, y1 = _ffn_fwd(xin, wf[l][0], "ffn_fwd")
        h1, xhat1, rstd1, xin2 = _ln_fwd(cfg, h, y1, mods[l], 2, 0.5, ln_g_f[l, 0][None], ln_b_f[l, 0][None],
                                          mods[l], (3, 4), "ln_fwd_a")
        st.update(g1=g1, u1=u1, y1=y1, h1=h1, xhat1=xhat1, rstd1=rstd1, xin2=xin2)
        if l == 0:
            w_ab_in_t = weights_now("ab_in", xin2)[0]
            p = _matmul(xin2, w_ab_in_t, "nt", BF16, "mix_ab_in")
            att_l, att_c = _attn_fwd(cfg, p, cos, sin, sink_rows, "attn_fwd")
            pool_l = _pool_fwd(p, pool_w[0], pool_scale, n_lat, 0, 2, "pool_fwd_lat")
            pool_c = _pool_fwd(p, pool_w[0], pool_scale, n_ctx, cfg.ctx_blk, 2, "pool_fwd_ctx")
            cat = jnp.concatenate([jnp.concatenate([att_l, pool_l], axis=1),
                                   jnp.concatenate([att_c, pool_c], axis=1)], axis=0)
            w_ab_out = weights_now("ab_out", cat)[0]
            y2 = _matmul(cat, w_ab_out, "nn", BF16, "mix_ab_out")
        else:
            w_lru_in_t = weights_now("lru_in", xin2)[0]
            p = _matmul(xin2, w_lru_in_t, "nt", BF16, "lru_in")
            z_l, z_c, st["h_lat"], st["h_ctx"] = _lru_fwd(cfg, p, lru_consts, "lru_fwd")
            cat = jnp.concatenate([z_l, z_c], axis=0)
            w_lru_out = weights_now("lru_out", cat)[0]
            y2 = _matmul(cat, w_lru_out, "nn", BF16, "lru_out")
        h2, xhat2, rstd2, xin3 = _ln_fwd(cfg, h1, y2, mods[l], 5, 1.0, ln_g_f[l, 1][None], ln_b_f[l, 1][None],
                                          mods[l], (6, 7), "ln_fwd_b")
        wf[l][1] = weights_now("ffn%d1" % l, xin3)
        g3, u3, y3 = _ffn_fwd(xin3, wf[l][1], "ffn_fwd")
        if l == 0:
            h3, xhat3, rstd3, xin = _ln_fwd(cfg, h2, y3, mods[l], 8, 0.5, ln_g_f[l, 2][None], ln_b_f[l, 2][None],
                                            mods[1], (0, 1), "ln_fwd_a")
        else:
            h3, xhat3, rstd3 = _ln_fwd(cfg, h2, y3, mods[l], 8, 0.5, ln_g_f[l, 2][None], ln_b_f[l, 2][None],
                                       None, None, "ln_fwd_last")
        st.update(p=p, cat=cat, y2=y2, h2=h2, xhat2=xhat2, rstd2=rstd2, xin3=xin3, g3=g3, u3=u3, y3=y3,
                  xhat3=xhat3, rstd3=rstd3)
        saved.append(st)
        h = h3

    dy, loss_tile = _loss(cfg, h, loss_target.reshape(cfg.t_lat, D), "loss")
    loss = lax.psum(loss_tile[0, 0], ("x", "y", "c"))

    grads = {}
    dmod = [None, None]
    recv_ffn = [[None, None], [None, None]]
    dln_g = [[None] * 3, [None] * 3]
    dln_b = [[None] * 3, [None] * 3]

    def ffn_weight_grads(tag, xin_b, dg, du, a_act, dys):
        handles = []
        for k, (lhs, rhs) in enumerate(((dg, xin_b), (du, xin_b), (a_act, dys))):
            part = _matmul(lhs, rhs, "tn", BF16, "ffn_dw", bm_cap=1408, bk_cap=2304)[None]
            handles.append(_exchange_start(part, "exchange_start_ffn%s_%d" % (tag, k)))
        return handles

    def pin(handles):
        total = handles[0][4][0, 0]
        for hd in handles[1:]:
            total = total + hd[4][0, 0]
        return total

    up = (dy,)
    dmod_next = None
    last_sent = None
    for l in (1, 0):
        st = saved[l]
        dm = [None] * N_MOD

        def put_stats(stats, gate_idx, nxt):
            dm[gate_idx] = stats[:, 2, :]
            if nxt is not None:
                nxt[0][nxt[1]] = stats[:, 4, :]
                nxt[0][nxt[1] + 1] = stats[:, 3, :]

        lng3 = ln_g_f[l, 2][None] if last_sent is None else ln_g_f[l, 2][None] + pin(last_sent)
        if len(up) > 1:
            up = (up[0], up[1], ln_b_f[l, 2][None], up[3], up[4])
        dres, dys, stats = _ln_bwd(cfg, up, st["xhat3"], st["rstd3"], st["y3"], mods[l], 8, 0.5,
                                   lng3, "ln_bwd_fused" if len(up) > 1 else "ln_bwd_last")
        put_stats(stats, 8, None if len(up) == 1 else (dmod_next, 0))
        dln_g[l][2], dln_b[l][2] = stats[:, 0, :].sum(0), stats[:, 1, :].sum(0)
        dg, du, a_act, dxin = _ffn_bwd(dys, st["g3"], st["u3"], wf[l][1], "ffn_bwd")
        recv_ffn[l][1] = ffn_weight_grads("%d1" % l, st["xin3"], dg, du, a_act, dys)
        dres, dys, stats = _ln_bwd(cfg, (dres, dxin, ln_b_f[l, 1][None], mods[l], 7), st["xhat2"], st["rstd2"], st["y2"],
                                   mods[l], 5, 1.0, ln_g_f[l, 1][None] + pin(recv_ffn[l][1]), "ln_bwd_fused")
        put_stats(stats, 5, (dm, 6))
        dln_g[l][1], dln_b[l][1] = stats[:, 0, :].sum(0), stats[:, 1, :].sum(0)
        if l == 0:
            dw_out = _matmul(st["cat"], dys, "tn", BF16, "mix_ab_dw_out")
            dcat = _matmul(dys, w_ab_out, "nt", BF16, "mix_ab_dcat")
            dq, dk, dv, dqc, dkc, dvc, dsink = _attn_bwd(cfg, st["p"], dcat, cos, sin, sink_rows, "attn_bwd")
            du_l, dpw_l, dps_l = _pool_bwd(st["p"], pool_w[0], pool_scale, dcat, n_lat, 0, 2, "pool_bwd_lat")
            du_c, dpw_c, dps_c = _pool_bwd(st["p"], pool_w[0], pool_scale, dcat, n_ctx, cfg.ctx_blk, 2, "pool_bwd_ctx")
            dp = jnp.concatenate([jnp.concatenate([dq, dk, dv, du_l], axis=1),
                                  jnp.concatenate([dqc, dkc, dvc, du_c], axis=1)], axis=0)
            dw_in_t = _matmul(dp, st["xin2"], "tn", BF16, "mix_ab_dw_in", bm_cap=1280)
            dxin = _matmul(dp, w_ab_in_t, "nn", BF16, "mix_ab_dx")
            recv_mix = [_exchange_start(part, "exchange_start_mix_ab_%d" % k)
                        for k, part in enumerate((dw_in_t[None], dw_out[None], _as2d(dpw_l + dpw_c)[None]))]
            grads["attn_sink"] = (dsink[0, :, 0] + dsink[1, :, 0])[None, :]
            grads["pool_scale"] = dps_l + dps_c
        else:
            dw_out = _matmul(st["cat"], dys, "tn", BF16, "lru_dw_out")
            dz = _matmul(dys, w_lru_out, "nt", BF16, "lru_dz")
            dgl, dul, dgc, duc, dwa, dwx, vec = _lru_bwd(cfg, st["p"], dz, st["h_lat"], st["h_ctx"], lru_consts, "lru_bwd")
            dp = jnp.concatenate([jnp.concatenate([dgl, dul], axis=1), jnp.concatenate([dgc, duc], axis=1)], axis=0)
            dw_in_t = _matmul(dp, st["xin2"], "tn", BF16, "lru_dw_in", bm_cap=1024)
            dxin = _matmul(dp, w_lru_in_t, "nn", BF16, "lru_dx")
            recv_mix = [_exchange_start(part, "exchange_start_lru_%d" % k)
                        for k, part in enumerate((dw_in_t[None], dw_out[None], _as2d(dwa)[None], _as2d(dwx)[None]))]
            vec_t = jnp.moveaxis(vec, 0, 1).reshape(16, D)
            grads["lru_ba"], grads["lru_bx"] = vec_t[0:2], vec_t[2:4]
            grads["lru_lambda"], grads["lru_conv_w"], grads["lru_conv_b"] = vec_t[4:6], vec_t[6:10], vec_t[10:11]
        if l == 0:
            recv_ab = recv_mix
        else:
            recv_lru = recv_mix
        dres, dys, stats = _ln_bwd(cfg, (dres, dxin, ln_b_f[l, 0][None], mods[l], 4), st["xhat1"], st["rstd1"], st["y1"],
                                   mods[l], 2, 0.5, ln_g_f[l, 0][None] + pin(recv_mix), "ln_bwd_fused")
        put_stats(stats, 2, (dm, 3))
        dln_g[l][0], dln_b[l][0] = stats[:, 0, :].sum(0), stats[:, 1, :].sum(0)
        dg, du, a_act, dxin = _ffn_bwd(dys, st["g1"], st["u1"], wf[l][0], "ffn_bwd")
        recv_ffn[l][0] = ffn_weight_grads("%d0" % l, st["xin1"], dg, du, a_act, dys)
        last_sent = recv_ffn[l][0]
        dmod[l] = dm
        dmod_next = dm
        up = (dres, dxin, None, mods[l], 1)
    dh0, stats = _modulate_bwd(cfg, up[0], up[1], h0, mods[0] + pin(last_sent), 1, "modulate_bwd")
    dmod[0][0], dmod[0][1] = stats[:, 4, :], stats[:, 3, :]
    grad_x = dh0.reshape(x.shape)

    dmod_mine = jnp.stack([jnp.stack(dmod[l], axis=1).reshape(3, N_MOD * D) for l in range(2)])
    n_dm = 6 * N_MOD * D // 128
    dmod_sent = _gather_start(dmod_mine.reshape(1, n_dm, 128), "gather_start_dmod")

    def arrived(handle, name):
        return _exchange_wait(handle, dmod_sent[4], name)

    recv_ffn = [[[arrived(hd, "exchange_wait_ffn%d%d_%d" % (l, i, k)) for k, hd in enumerate(recv_ffn[l][i])]
                 for i in range(2)] for l in range(2)]
    recv_ab = [arrived(hd, "exchange_wait_mix_ab_%d" % k) for k, hd in enumerate(recv_ab)]
    recv_lru = [arrived(hd, "exchange_wait_lru_%d" % k) for k, hd in enumerate(recv_lru)]

    def shard_sum(recv, name):
        return _sum_blocks(recv.reshape(N_DEV, recv.shape[2], recv.shape[3]), name)

    gate_g = [[None, None], [None, None]]
    up_g = [[None, None], [None, None]]
    down_g = [[None, None], [None, None]]
    for l in range(2):
        for i in range(2):
            gt, ut, dn = [shard_sum(r, "sum_ffn") for r in recv_ffn[l][i]]
            gate_g[l][i], up_g[l][i], down_g[l][i] = gt.T, ut.T, dn
    grads["ffn_w_gate"] = jnp.stack([jnp.stack(gate_g[l]) for l in range(2)])
    grads["ffn_w_up"] = jnp.stack([jnp.stack(up_g[l]) for l in range(2)])
    grads["ffn_w_down"] = jnp.stack([jnp.stack(down_g[l]) for l in range(2)])
    grads["mix_ab_w_in"] = shard_sum(recv_ab[0], "sum_mix_in").T[None]
    grads["mix_ab_w_out"] = shard_sum(recv_ab[1], "sum_mix_out")[None]
    grads["lru_w_in"] = shard_sum(recv_lru[0], "sum_lru_in").T[None]
    grads["lru_w_out"] = shard_sum(recv_lru[1], "sum_lru_out")[None]
    rep_parts = [shard_sum(recv_lru[2], "sum_rep"), shard_sum(recv_lru[3], "sum_rep"), shard_sum(recv_ab[2], "sum_rep")]
    rep_names = ["lru_wa", "lru_wx", "pool_w"]

    dmod_all = _gather_wait(dmod_sent, rep_parts[2], "gather_wait_dmod").reshape(N_DEV, n_dm, 128)
    dmod_sum = _sum_blocks(dmod_all, "sum_dmod").reshape(2, 3, N_MOD * D)
    dmod_all = dmod_all.reshape(N_DEV, 2, 3, N_MOD * D)
    grads["b_mod"] = dmod_sum[:, 0] + dmod_sum[:, 1] + dmod_sum[:, 2]
    dmod_ex = jnp.moveaxis(dmod_all[:, :, 0:2, :], 1, 0).reshape(2, 2 * N_DEV, N_MOD * D)
    dm_rows = jnp.zeros((2, 32, N_MOD * D), F32).at[:, :16].set(dmod_ex).at[:, 16].set(dmod_sum[:, 2])
    dm_cols = lax.dynamic_slice_in_dim(dm_rows, me * mcols, mcols, axis=2).astype(BF16)
    grads["w_mod"] = jnp.stack([_matmul(s_rows, dm_cols[l], "tn", F32, "mod_dw", bn_cap=1280) for l in range(2)])
    ds_part = None
    for l in range(2):
        part = _matmul(dm_cols[l, 16:32], w_mod[l], "nt", F32, "mod_ds", bk_cap=1280)[0]
        ds_part = part if ds_part is None else ds_part + part

    dln_g_f = jnp.stack([jnp.stack(dln_g[l]) for l in range(2)])
    dln_b_f = jnp.stack([jnp.stack(dln_b[l]) for l in range(2)])
    sink_pad = jnp.zeros((1, 128), F32).at[0, :8].set(grads["attn_sink"][0])
    part_list = [p_.reshape(-1, 128) for p_ in rep_parts] + [
        dln_g_f.reshape(-1, 128), dln_b_f.reshape(-1, 128), grads["lru_conv_w"].reshape(-1, 128),
        grads["lru_conv_b"].reshape(-1, 128), grads["lru_ba"].reshape(-1, 128), grads["lru_bx"].reshape(-1, 128),
        grads["lru_lambda"].reshape(-1, 128), ds_part.reshape(-1, 128), sink_pad, grads["pool_scale"].reshape(-1, 128)]
    parts, part_off = _pack_rows(part_list)
    parts_sent = _gather_start(parts[None], "gather_start_partials")

    delta, new_m, new_v = {}, {}, {}
    for n in ("w_mod", "b_mod", "ffn_w_gate", "ffn_w_up", "ffn_w_down", "mix_ab_w_in", "mix_ab_w_out",
              "lru_w_in", "lru_w_out"):
        grads[n] = grads[n].reshape(weights[n].shape)
        delta[n], new_m[n], new_v[n] = _adamw(weights[n], grads[n], mom_m[n], mom_v[n], "adamw", dep=parts_sent[4])
    parts_all = _gather_wait(parts_sent, delta["lru_w_out"], "gather_wait_partials").reshape(N_DEV, parts.shape[0], 128)
    parts_sum = _sum_blocks(parts_all, "sum_partials")

    for i, n in enumerate(rep_names):
        rows = part_list[i].shape[0]
        grads[n] = parts_all[:, part_off[i]:part_off[i] + rows, :].reshape(weights[n].shape)

    def take(idx):
        return parts_sum[part_off[idx]:part_off[idx] + part_list[idx].shape[0]]

    def my_cols(full, shp):
        w = shp[-1]
        return lax.dynamic_slice_in_dim(full, me * w, w, axis=full.ndim - 1)

    grads["ln_g"] = my_cols(take(3).reshape(2, 3, D), ln_g.shape)
    grads["ln_b"] = my_cols(take(4).reshape(2, 3, D), ln_b.shape)
    grads["lru_conv_w"] = my_cols(take(5).reshape(1, 4, D), lru_conv_w.shape)
    grads["lru_conv_b"] = my_cols(take(6).reshape(1, D), lru_conv_b.shape)
    grads["lru_ba"] = my_cols(take(7).reshape(1, 2, D), lru_ba.shape)
    grads["lru_bx"] = my_cols(take(8).reshape(1, 2, D), lru_bx.shape)
    grads["lru_lambda"] = my_cols(take(9).reshape(1, 2, D), lru_lambda.shape)
    sg = jax.nn.sigmoid(c_ctx)
    grads["c_ctx"] = take(10).reshape(D) * (sg * (1.0 + c_ctx * (1.0 - sg)))
    grads["attn_sink"] = take(11)[:, :8]
    grads["pool_scale"] = take(12).reshape(pool_scale.shape)

    for n in names:
        if n in delta:
            continue
        grads[n] = grads[n].reshape(weights[n].shape)
        delta[n], new_m[n], new_v[n] = _adamw(weights[n], grads[n], mom_m[n], mom_v[n], "adamw")

    return (loss, grad_x, *[grads[n] for n in names], *[delta[n] for n in names],
            *[new_m[n] for n in names], *[new_v[n] for n in names])
```

```python
import functools
import math

import jax
import jax.numpy as jnp
from jax import lax
from jax.experimental import pallas as pl
from jax.experimental.pallas import tpu as pltpu

F32 = jnp.float32
BF16 = jnp.bfloat16
MESH = pl.DeviceIdType.MESH

D = 1024
N_MOD = 9
N_DEV = 8
HEAD_DIM = 64
ATT_HEADS = 8
KV_HEADS = 2
ATT_W = 512
BLK = 128
ATT_SCALE = HEAD_DIM ** -0.5
GRID_W = 64
ROPE_FREQS = HEAD_DIM // 4
ROPE_THETA = 10000.0
POOL_R = (1, 2, 4, 8)
LRU_C = 8.0
LN_EPS = 1e-5
NEG_INF = -1e30
ALPHA = 4.0 ** 0.25
LR, B1, B2, EPS, WD, STEP = 0.001, 0.9, 0.999, 1e-08, 0.01, 10
VMEM_LIMIT = 56 * 1024 * 1024
ROW_TILE = 512


def _params(sem=None):
    if sem is None:
        return pltpu.CompilerParams(vmem_limit_bytes=VMEM_LIMIT)
    return pltpu.CompilerParams(dimension_semantics=sem, vmem_limit_bytes=VMEM_LIMIT)


def _sigmoid(x):
    return 0.5 * jnp.tanh(0.5 * x) + 0.5


def _dot(a, b):
    return jnp.dot(a.astype(BF16), b.astype(BF16), preferred_element_type=F32)


def _dot_nt(a, b):
    return lax.dot_general(a.astype(BF16), b.astype(BF16), (((1,), (1,)), ((), ())), preferred_element_type=F32)


def _dot_tn(a, b):
    return lax.dot_general(a.astype(BF16), b.astype(BF16), (((0,), (0,)), ((), ())), preferred_element_type=F32)


def _pick(n, cap):
    best = None
    for m in range(128, min(n, cap) + 1, 128):
        if n % m == 0:
            best = m
    return n if best is None else best


def _chunks(width, step=256):
    out, c = [], 0
    while c < width:
        w = min(step, width - c)
        out.append((c, w))
        c += w
    return out


class _Cfg:
    def __init__(self, n_lat, n_ctx):
        self.n_lat, self.n_ctx = n_lat, n_ctx
        self.t_lat, self.t_ctx = 2 * n_lat, 2 * n_ctx
        self.T = self.t_lat + self.t_ctx
        self.tm = min(ROW_TILE, self.t_ctx)
        assert n_lat % self.tm == 0 and self.t_ctx % self.tm == 0 and n_lat >= 3 * BLK and n_ctx % BLK == 0
        self.nt = self.T // self.tm
        self.nlt = n_lat // self.tm
        self.ctx_blk = self.t_lat // n_ctx

    def seg(self, i):
        return jnp.minimum(i // self.nlt, 2)

    def first_of_seg(self, i):
        return jnp.where(i < 2 * self.nlt, i % self.nlt == 0, i == 2 * self.nlt)


def _modulate(cfg, h, mod, shift_idx, scale_idx, name):
    tm = cfg.tm

    def body(h_ref, mod_ref, o_ref):
        sh = mod_ref[shift_idx:shift_idx + 1, :]
        sc = mod_ref[scale_idx:scale_idx + 1, :]
        o_ref[...] = (h_ref[...] * (1.0 + sc) + sh).astype(BF16)

    return pl.pallas_call(
        body, grid=(cfg.nt,), name=name,
        in_specs=[pl.BlockSpec((tm, D), lambda i: (i, 0)),
                  pl.BlockSpec((None, N_MOD, D), lambda i: (cfg.seg(i), 0, 0))],
        out_specs=pl.BlockSpec((tm, D), lambda i: (i, 0)),
        out_shape=jax.ShapeDtypeStruct((cfg.T, D), BF16),
        compiler_params=_params(("parallel",)),
    )(h, mod)


def _ln_fwd(cfg, h, y, mod, gate_idx, coef, lng, lnb, mod_next, next_idx, name):
    tm = cfg.tm
    has_next = next_idx is not None

    def body(*refs):
        if has_next:
            h_ref, y_ref, mod_ref, g_ref, b_ref, modn_ref, hn_ref, xhat_ref, rstd_ref, xin_ref = refs
        else:
            h_ref, y_ref, mod_ref, g_ref, b_ref, hn_ref, xhat_ref, rstd_ref = refs
        gate = mod_ref[gate_idx:gate_idx + 1, :]
        z = ALPHA * h_ref[...] + (coef * gate) * y_ref[...].astype(F32)
        mu = jnp.mean(z, axis=-1, keepdims=True)
        zc = z - mu
        var = jnp.mean(zc * zc, axis=-1, keepdims=True)
        rstd = lax.rsqrt(var + LN_EPS)
        xhat = zc * rstd
        hn = xhat * g_ref[...] + b_ref[...]
        hn_ref[...] = hn
        xhat_ref[...] = xhat.astype(BF16)
        rstd_ref[...] = rstd
        if has_next:
            sh = modn_ref[next_idx[0]:next_idx[0] + 1, :]
            sc = modn_ref[next_idx[1]:next_idx[1] + 1, :]
            xin_ref[...] = (hn * (1.0 + sc) + sh).astype(BF16)

    row = pl.BlockSpec((tm, D), lambda i: (i, 0))
    modspec = pl.BlockSpec((None, N_MOD, D), lambda i: (cfg.seg(i), 0, 0))
    vec = pl.BlockSpec((1, D), lambda i: (0, 0))
    in_specs = [row, row, modspec, vec, vec]
    args = [h, y, mod, lng, lnb]
    out_specs = [row, row, pl.BlockSpec((tm, 1), lambda i: (i, 0))]
    out_shape = [jax.ShapeDtypeStruct((cfg.T, D), F32), jax.ShapeDtypeStruct((cfg.T, D), BF16),
                 jax.ShapeDtypeStruct((cfg.T, 1), F32)]
    if has_next:
        in_specs.append(modspec)
        args.append(mod_next)
        out_specs.append(row)
        out_shape.append(jax.ShapeDtypeStruct((cfg.T, D), BF16))
    return pl.pallas_call(body, grid=(cfg.nt,), name=name, in_specs=in_specs, out_specs=out_specs,
                          out_shape=out_shape, compiler_params=_params(("parallel",)))(*args)


def _ln_bwd(cfg, up, xhat, rstd, y, mod, gate_idx, coef, lng, name):
    tm = cfg.tm
    fused = len(up) > 1
    scale_next = up[4] if fused else None

    def body(*refs):
        if fused:
            dres_n, dxin_n, b_ref, modn_ref, xhat_ref, rstd_ref, y_ref, mod_ref, g_ref, dres_ref, dys_ref, st_ref = refs
        else:
            dhn_ref, xhat_ref, rstd_ref, y_ref, mod_ref, g_ref, dres_ref, dys_ref, st_ref = refs
        i = pl.program_id(0)

        @pl.when(cfg.first_of_seg(i))
        def _():
            st_ref[...] = jnp.zeros_like(st_ref)

        xhat = xhat_ref[...].astype(F32)
        if fused:
            dxin = dxin_n[...].astype(F32)
            sc = modn_ref[scale_next:scale_next + 1, :]
            dhn = dres_n[...] + dxin * (1.0 + sc)
            shift_sum = jnp.sum(dxin, axis=0, keepdims=True)
            st_ref[3:4, :] += g_ref[...] * jnp.sum(dxin * xhat, axis=0, keepdims=True) + b_ref[...] * shift_sum
            st_ref[4:5, :] += shift_sum
        else:
            dhn = dhn_ref[...]
        gdh = dhn * g_ref[...]
        m1 = jnp.mean(gdh, axis=-1, keepdims=True)
        m2 = jnp.mean(gdh * xhat, axis=-1, keepdims=True)
        dz = rstd_ref[...] * (gdh - m1 - xhat * m2)
        gate = mod_ref[gate_idx:gate_idx + 1, :]
        dres_ref[...] = ALPHA * dz
        dys_ref[...] = ((coef * gate) * dz).astype(BF16)
        st_ref[0:1, :] += jnp.sum(dhn * xhat, axis=0, keepdims=True)
        st_ref[1:2, :] += jnp.sum(dhn, axis=0, keepdims=True)
        st_ref[2:3, :] += jnp.sum((coef * dz) * y_ref[...].astype(F32), axis=0, keepdims=True)

    row = pl.BlockSpec((tm, D), lambda i: (i, 0))
    modspec = pl.BlockSpec((None, N_MOD, D), lambda i: (cfg.seg(i), 0, 0))
    vec = pl.BlockSpec((1, D), lambda i: (0, 0))
    col = pl.BlockSpec((tm, 1), lambda i: (i, 0))
    if fused:
        in_specs = [row, row, vec, modspec, row, col, row, modspec, vec]
        args = [up[0], up[1], up[2], up[3], xhat, rstd, y, mod, lng]
    else:
        in_specs = [row, row, col, row, modspec, vec]
        args = [up[0], xhat, rstd, y, mod, lng]
    return pl.pallas_call(
        body, grid=(cfg.nt,), name=name, in_specs=in_specs,
        out_specs=[row, row, pl.BlockSpec((None, 8, D), lambda i: (cfg.seg(i), 0, 0))],
        out_shape=[jax.ShapeDtypeStruct((cfg.T, D), F32), jax.ShapeDtypeStruct((cfg.T, D), BF16),
                   jax.ShapeDtypeStruct((3, 8, D), F32)],
        compiler_params=_params(("arbitrary",)))(*args)


def _modulate_bwd(cfg, dres, dxin, h, mod, scale_idx, name):
    tm = cfg.tm
    n_lt = 2 * cfg.nlt

    def body(dres_ref, dxin_ref, h_ref, mod_ref, dh_ref, st_ref):
        i = pl.program_id(0)

        @pl.when(cfg.first_of_seg(i))
        def _():
            st_ref[...] = jnp.zeros_like(st_ref)

        dxin = dxin_ref[...].astype(F32)
        sc = mod_ref[scale_idx:scale_idx + 1, :]

        @pl.when(i < n_lt)
        def _():
            dh_ref[...] = dres_ref[...] + dxin * (1.0 + sc)

        st_ref[3:4, :] += jnp.sum(dxin * h_ref[...], axis=0, keepdims=True)
        st_ref[4:5, :] += jnp.sum(dxin, axis=0, keepdims=True)

    row = pl.BlockSpec((tm, D), lambda i: (i, 0))
    return pl.pallas_call(
        body, grid=(cfg.nt,), name=name,
        in_specs=[row, row, row, pl.BlockSpec((None, N_MOD, D), lambda i: (cfg.seg(i), 0, 0))],
        out_specs=[pl.BlockSpec((tm, D), lambda i: (jnp.minimum(i, n_lt - 1), 0)),
                   pl.BlockSpec((None, 8, D), lambda i: (cfg.seg(i), 0, 0))],
        out_shape=[jax.ShapeDtypeStruct((cfg.t_lat, D), F32), jax.ShapeDtypeStruct((3, 8, D), F32)],
        compiler_params=_params(("arbitrary",)))(dres, dxin, h, mod)


def _loss(cfg, h, target, name):
    tm = cfg.tm
    n_lt = 2 * cfg.nlt

    def body(h_ref, t_ref, dy_ref, l_ref):
        i = pl.program_id(0)

        @pl.when(i == 0)
        def _():
            l_ref[...] = jnp.zeros_like(l_ref)

        @pl.when(i < n_lt)
        def _():
            err = h_ref[...] - t_ref[...]
            dy_ref[...] = err * (1.0 / D)
            part = jnp.sum(jnp.sum(err * err, axis=1, keepdims=True), axis=0, keepdims=True) * (0.5 / D)
            l_ref[...] += jnp.broadcast_to(part, l_ref.shape)

        @pl.when(i >= n_lt)
        def _():
            dy_ref[...] = jnp.zeros_like(dy_ref)

    return pl.pallas_call(
        body, grid=(cfg.nt,), name=name,
        in_specs=[pl.BlockSpec((tm, D), lambda i: (i, 0)),
                  pl.BlockSpec((tm, D), lambda i: (jnp.minimum(i, n_lt - 1), 0))],
        out_specs=[pl.BlockSpec((tm, D), lambda i: (i, 0)), pl.BlockSpec((8, 128), lambda i: (0, 0))],
        out_shape=[jax.ShapeDtypeStruct((cfg.T, D), F32), jax.ShapeDtypeStruct((8, 128), F32)],
        compiler_params=_params(("arbitrary",)))(h, target)


def _matmul(a, b, mode, out_dtype, name, bm_cap=1536, bn_cap=1408, bk_cap=1024):
    if mode == "nn":
        (M, K), N = a.shape, b.shape[1]
    elif mode == "nt":
        (M, K), N = a.shape, b.shape[0]
    else:
        (K, M), N = a.shape, b.shape[1]
    bm, bn, bk = _pick(M, bm_cap), _pick(N, bn_cap), _pick(K, bk_cap)
    nk = K // bk

    def body(a_ref, b_ref, o_ref, acc_ref=None):
        k = pl.program_id(2)
        if mode == "nn":
            part = _dot(a_ref[...], b_ref[...])
        elif mode == "nt":
            part = _dot_nt(a_ref[...], b_ref[...])
        else:
            part = _dot_tn(a_ref[...], b_ref[...])
        if nk == 1:
            o_ref[...] = part.astype(out_dtype)
            return

        @pl.when(k == 0)
        def _():
            acc_ref[...] = part

        @pl.when((k > 0) & (k < nk - 1))
        def _():
            acc_ref[...] += part

        @pl.when(k == nk - 1)
        def _():
            o_ref[...] = (acc_ref[...] + part).astype(out_dtype)

    if mode == "nn":
        a_spec = pl.BlockSpec((bm, bk), lambda i, j, k: (i, k))
        b_spec = pl.BlockSpec((bk, bn), lambda i, j, k: (k, j))
    elif mode == "nt":
        a_spec = pl.BlockSpec((bm, bk), lambda i, j, k: (i, k))
        b_spec = pl.BlockSpec((bn, bk), lambda i, j, k: (j, k))
    else:
        a_spec = pl.BlockSpec((bk, bm), lambda i, j, k: (k, i))
        b_spec = pl.BlockSpec((bk, bn), lambda i, j, k: (k, j))
    return pl.pallas_call(
        body, grid=(M // bm, N // bn, nk), name=name, in_specs=[a_spec, b_spec],
        out_specs=pl.BlockSpec((bm, bn), lambda i, j, k: (i, j)),
        out_shape=jax.ShapeDtypeStruct((M, N), out_dtype),
        scratch_shapes=[pltpu.VMEM((bm, bn), F32)] if nk > 1 else [],
        compiler_params=_params(("parallel", "parallel", "arbitrary")))(a, b)


def _ffn_tile(T, cap):
    best = 128
    for t in range(128, cap + 1, 128):
        if T % t == 0:
            best = t
    return best


def _ffn_fwd(xin, wf, name):
    T = xin.shape[0]
    F = wf.shape[1]
    tm, tf = _ffn_tile(T, 1152), F // 2
    assert tf % 128 == 0 and T % tm == 0

    def body(x_ref, wg_ref, wu_ref, wd_ref, g_ref, u_ref, y_ref, acc_ref):
        j = pl.program_id(1)
        x = x_ref[...]
        acc = None
        for c0, cw in _chunks(tf):
            g = _dot_nt(x, wg_ref[c0:c0 + cw, :])
            u = _dot_nt(x, wu_ref[c0:c0 + cw, :])
            g_ref[:, c0:c0 + cw] = g.astype(BF16)
            u_ref[:, c0:c0 + cw] = u.astype(BF16)
            part = _dot(g * _sigmoid(g) * u, wd_ref[c0:c0 + cw, :])
            acc = part if acc is None else acc + part

        @pl.when(j == 0)
        def _():
            acc_ref[...] = acc

        @pl.when(j == 1)
        def _():
            y_ref[...] = (acc_ref[...] + acc).astype(BF16)

    return pl.pallas_call(
        body, grid=(T // tm, 2), name=name,
        in_specs=[pl.BlockSpec((tm, D), lambda i, j: (i, 0)),
                  pl.BlockSpec((None, tf, D), lambda i, j: (0, j, 0)),
                  pl.BlockSpec((None, tf, D), lambda i, j: (1, j, 0)),
                  pl.BlockSpec((None, tf, D), lambda i, j: (2, j, 0))],
        out_specs=[pl.BlockSpec((tm, tf), lambda i, j: (i, j)),
                   pl.BlockSpec((tm, tf), lambda i, j: (i, j)),
                   pl.BlockSpec((tm, D), lambda i, j: (i, 0))],
        out_shape=[jax.ShapeDtypeStruct((T, F), BF16), jax.ShapeDtypeStruct((T, F), BF16),
                   jax.ShapeDtypeStruct((T, D), BF16)],
        scratch_shapes=[pltpu.VMEM((tm, D), F32)],
        compiler_params=_params(("parallel", "arbitrary")))(xin, wf, wf, wf)


def _ffn_bwd(dys, g, u, wf, name):
    T = dys.shape[0]
    F = wf.shape[1]
    tm, tf = _ffn_tile(T, 512), F // 2

    def body(dy_ref, g_ref, u_ref, wg_ref, wu_ref, wd_ref, dg_ref, du_ref, a_ref, dx_ref, acc_ref):
        j = pl.program_id(1)
        da_all = _dot_nt(dy_ref[...], wd_ref[...])
        for c0, cw in _chunks(tf):
            gg = g_ref[:, c0:c0 + cw].astype(F32)
            uu = u_ref[:, c0:c0 + cw].astype(F32)
            da = da_all[:, c0:c0 + cw]
            s = _sigmoid(gg)
            silu = gg * s
            a_ref[:, c0:c0 + cw] = (silu * uu).astype(BF16)
            du_ref[:, c0:c0 + cw] = (da * silu).astype(BF16)
            dg_ref[:, c0:c0 + cw] = (da * uu * (s * (1.0 + gg * (1.0 - s)))).astype(BF16)
        acc = _dot(dg_ref[...], wg_ref[...]) + _dot(du_ref[...], wu_ref[...])

        @pl.when(j == 0)
        def _():
            acc_ref[...] = acc

        @pl.when(j == 1)
        def _():
            dx_ref[...] = (acc_ref[...] + acc).astype(BF16)

    blk = pl.BlockSpec((tm, tf), lambda i, j: (i, j))
    return pl.pallas_call(
        body, grid=(T // tm, 2), name=name,
        in_specs=[pl.BlockSpec((tm, D), lambda i, j: (i, 0)), blk, blk,
                  pl.BlockSpec((None, tf, D), lambda i, j: (0, j, 0)),
                  pl.BlockSpec((None, tf, D), lambda i, j: (1, j, 0)),
                  pl.BlockSpec((None, tf, D), lambda i, j: (2, j, 0))],
        out_specs=[blk, blk, blk, pl.BlockSpec((tm, D), lambda i, j: (i, 0))],
        out_shape=[jax.ShapeDtypeStruct((T, F), BF16), jax.ShapeDtypeStruct((T, F), BF16),
                   jax.ShapeDtypeStruct((T, F), BF16), jax.ShapeDtypeStruct((T, D), BF16)],
        scratch_shapes=[pltpu.VMEM((tm, D), F32)],
        compiler_params=_params(("parallel", "arbitrary")))(dys, g, u, wf, wf, wf)


def _swap_halves(x):
    w = x.shape[1]
    lane = lax.broadcasted_iota(jnp.int32, (1, w), 1)
    return jnp.where((lane & 63) < 32, pltpu.roll(x, w - 32, 1), pltpu.roll(x, 32, 1))


def _rope(x, cos, sin):
    return x * cos + _swap_halves(x) * sin


def _rope_t(dy, cos, sin):
    return dy * cos + _swap_halves(dy * sin)


def _rope_tables(n_lat):
    rows = n_lat // GRID_W
    row = jnp.repeat(jnp.arange(rows, dtype=F32), GRID_W)
    col = jnp.tile(jnp.arange(GRID_W, dtype=F32), rows)
    inv = ROPE_THETA ** (-jnp.arange(ROPE_FREQS, dtype=F32) / ROPE_FREQS)
    ang = jnp.concatenate([row[:, None] * inv, col[:, None] * inv], axis=-1)
    cs, sn = jnp.cos(ang), jnp.sin(ang)
    cos = jnp.concatenate([cs, cs, cs, cs], axis=-1)
    sin = jnp.concatenate([-sn, sn, -sn, sn], axis=-1)
    return cos, sin


def _attn_specs(cfg):
    n_lat, n_ctx, cb = cfg.n_lat, cfg.n_ctx, cfg.ctx_blk
    return [pl.BlockSpec((n_lat, ATT_W), lambda e: (e, 0)),
            pl.BlockSpec((n_lat, 128), lambda e: (e, 4)),
            pl.BlockSpec((n_lat, 128), lambda e: (e, 5)),
            pl.BlockSpec((n_ctx, ATT_W), lambda e: (cb + e, 0)),
            pl.BlockSpec((n_ctx, 128), lambda e: (cb + e, 4)),
            pl.BlockSpec((n_ctx, 128), lambda e: (cb + e, 5)),
            pl.BlockSpec((n_lat, 128), lambda e: (0, 0)),
            pl.BlockSpec((n_lat, 128), lambda e: (0, 0)),
            pl.BlockSpec((8, 128), lambda e: (0, 0))]


def _attn_prepare(kh, kl, vl, kc, vc, ka, kb, va, vb, kca, kcb, vca, vcb):
    lane = lax.broadcasted_iota(jnp.int32, (1, 128), 1)
    own = (lane < 64) if kh == 0 else (lane >= 64)

    def split(x, ra, rb):
        mine = jnp.where(own, x, 0.0)
        other = pltpu.roll(mine, 64, 1)
        a, b = (mine, other) if kh == 0 else (other, mine)
        ra[...] = a.astype(BF16)
        rb[...] = b.astype(BF16)

    split(kl, ka, kb)
    split(vl, va, vb)
    split(kc, kca, kcb)
    split(vc, vca, vcb)


def _softmax_parts(s_list, sk):
    m = sk
    for s in s_list:
        m = jnp.maximum(m, jnp.max(s, axis=1, keepdims=True))
    es = [jnp.exp(s - m) for s in s_list]
    esk = jnp.exp(sk - m)
    den = esk
    for e in es:
        den = den + jnp.sum(e, axis=1, keepdims=True)
    inv = 1.0 / den
    return [e * inv for e in es], esk * inv


def _window(cfg, n):
    r0 = pl.multiple_of(n * BLK, BLK)
    start = pl.multiple_of(jnp.clip((n - 1) * BLK, 0, cfg.n_lat - 3 * BLK), BLK)
    qpos = r0 + lax.broadcasted_iota(jnp.int32, (BLK, 1), 0)
    kpos = start + lax.broadcasted_iota(jnp.int32, (1, 3 * BLK), 1)
    valid = jnp.abs(qpos - kpos) <= BLK
    return r0, start, valid


def _attn_fwd(cfg, p, cos, sin, sink_rows, name):
    n_lat, n_ctx = cfg.n_lat, cfg.n_ctx

    def body(q_ref, k_ref, v_ref, qc_ref, kc_ref, vc_ref, cos_ref, sin_ref, sink_ref, o_ref, oc_ref,
             qr, ka, kb, va, vb, kca, kcb, vca, vcb):
        cos_t, sin_t = cos_ref[...], sin_ref[...]
        for gq in range(4):
            qr[:, gq * 128:(gq + 1) * 128] = _rope(q_ref[:, gq * 128:(gq + 1) * 128].astype(F32), cos_t, sin_t).astype(BF16)
        kl = _rope(k_ref[...].astype(F32), cos_t, sin_t)
        for kh in range(KV_HEADS):
            _attn_prepare(kh, kl, v_ref[...].astype(F32), kc_ref[...].astype(F32), vc_ref[...].astype(F32),
                          ka, kb, va, vb, kca, kcb, vca, vcb)

            def lat_block(n, carry):
                r0, start, valid = _window(cfg, n)
                win = pl.ds(start, 3 * BLK)
                lanes = [slice((kh * 2 + pr) * 128, (kh * 2 + pr + 1) * 128) for pr in range(2)]
                qps = [qr[pl.ds(r0, BLK), lanes[pr]] for pr in range(2)]
                kws, kcs = (ka[win, :], kb[win, :]), (kca[...], kcb[...])
                scores = [(jnp.where(valid, _dot_nt(qps[pr], kws[half]) * ATT_SCALE, NEG_INF),
                           _dot_nt(qps[pr], kcs[half]) * ATT_SCALE) for pr in range(2) for half in range(2)]
                probs = []
                for idx, (s_w, s_c) in enumerate(scores):
                    head = kh * 4 + idx
                    (p_w, p_c), _ = _softmax_parts([s_w, s_c], sink_ref[head:head + 1, 0:1])
                    probs.append((p_w.astype(BF16), p_c.astype(BF16)))
                vws, vcs = (va[win, :], vb[win, :]), (vca[...], vcb[...])
                for pr in range(2):
                    o = (_dot(probs[2 * pr][0], vws[0]) + _dot(probs[2 * pr][1], vcs[0])
                         + _dot(probs[2 * pr + 1][0], vws[1]) + _dot(probs[2 * pr + 1][1], vcs[1]))
                    o_ref[pl.ds(r0, BLK), lanes[pr]] = o.astype(BF16)
                return carry

            lax.fori_loop(0, n_lat // BLK, lat_block, 0, unroll=2)
            for n in range(n_ctx // BLK):
                rows = slice(n * BLK, (n + 1) * BLK)
                for pr in range(2):
                    lanes = slice((kh * 2 + pr) * 128, (kh * 2 + pr + 1) * 128)
                    qp = qc_ref[rows, lanes]
                    o = None
                    for half, (kcx, vcx) in enumerate(((kca, vca), (kcb, vcb))):
                        head = kh * 4 + pr * 2 + half
                        s_c = _dot_nt(qp, kcx[...]) * ATT_SCALE
                        (p_c,), _ = _softmax_parts([s_c], sink_ref[head:head + 1, 0:1])
                        part = _dot(p_c, vcx[...])
                        o = part if o is None else o + part
                    oc_ref[rows, lanes] = o.astype(BF16)

    return pl.pallas_call(
        body, grid=(2,), name=name, in_specs=_attn_specs(cfg),
        out_specs=[pl.BlockSpec((n_lat, ATT_W), lambda e: (e, 0)), pl.BlockSpec((n_ctx, ATT_W), lambda e: (e, 0))],
        out_shape=[jax.ShapeDtypeStruct((cfg.t_lat, ATT_W), BF16), jax.ShapeDtypeStruct((cfg.t_ctx, ATT_W), BF16)],
        scratch_shapes=[pltpu.VMEM((n_lat, ATT_W), BF16)] + [pltpu.VMEM((n_lat, 128), BF16)] * 4
        + [pltpu.VMEM((n_ctx, 128), BF16)] * 4,
        compiler_params=_params(("parallel",)))(p, p, p, p, p, p, cos, sin, sink_rows)


def _attn_bwd(cfg, p, dcat, cos, sin, sink_rows, name):
    n_lat, n_ctx, cb = cfg.n_lat, cfg.n_ctx, cfg.ctx_blk

    def body(q_ref, k_ref, v_ref, qc_ref, kc_ref, vc_ref, cos_ref, sin_ref, sink_ref, do_ref, doc_ref,
             dq_ref, dk_ref, dv_ref, dqc_ref, dkc_ref, dvc_ref, dsink_ref,
             qr, ka, kb, va, vb, kca, kcb, vca, vcb, dqs, dka, dva, dkca, dvca):
        cos_t, sin_t = cos_ref[...], sin_ref[...]
        lane = lax.broadcasted_iota(jnp.int32, (1, 128), 1)
        lo = lane < 64
        for gq in range(4):
            qr[:, gq * 128:(gq + 1) * 128] = _rope(q_ref[:, gq * 128:(gq + 1) * 128].astype(F32), cos_t, sin_t).astype(BF16)
        kl = _rope(k_ref[...].astype(F32), cos_t, sin_t)
        dsink_ref[...] = jnp.zeros_like(dsink_ref)
        dka[...] = jnp.zeros_like(dka)
        dva[...] = jnp.zeros_like(dva)
        dkca[...] = jnp.zeros_like(dkca)
        dvca[...] = jnp.zeros_like(dvca)

        def halves(x):
            return jnp.where(lo, x, 0).astype(BF16), jnp.where(lo, 0, x).astype(BF16)

        for kh in range(KV_HEADS):
            _attn_prepare(kh, kl, v_ref[...].astype(F32), kc_ref[...].astype(F32), vc_ref[...].astype(F32),
                          ka, kb, va, vb, kca, kcb, vca, vcb)

            def one_head(head, qp, q_half, do_p, do_half, kw, kcx, vw, vcx, win, valid):
                sk = sink_ref[head:head + 1, 0:1]
                s_list = [_dot_nt(qp, kcx[...]) * ATT_SCALE]
                if win is not None:
                    s_list.insert(0, jnp.where(valid, _dot_nt(qp, kw[win, :]) * ATT_SCALE, NEG_INF))
                probs, p_sink = _softmax_parts(s_list, sk)
                vals = [vcx[...]] if win is None else [vw[win, :], vcx[...]]
                dps = [_dot_nt(do_p, vv) for vv in vals]
                dr = None
                for pp, dp in zip(probs, dps):
                    t = jnp.sum(pp * dp, axis=1, keepdims=True)
                    dr = t if dr is None else dr + t
                dss = [(pp * (dp - dr) * ATT_SCALE).astype(BF16) for pp, dp in zip(probs, dps)]
                dsink_ref[head:head + 1, :] += jnp.broadcast_to(
                    jnp.sum(-p_sink * dr, axis=0, keepdims=True), (1, 128))
                p_c, ds_c = probs[-1], dss[-1]
                dq = _dot(ds_c, kcx[...])
                dkca[kh] += _dot_tn(ds_c, q_half)
                dvca[kh] += _dot_tn(p_c, do_half)
                if win is not None:
                    dq = dq + _dot(dss[0], kw[win, :])
                    dka[kh, win, :] += _dot_tn(dss[0], q_half)
                    dva[kh, win, :] += _dot_tn(probs[0], do_half)
                return dq

            def lat_block(n, carry):
                r0, start, valid = _window(cfg, n)
                win = pl.ds(start, 3 * BLK)
                lanes = [slice((kh * 2 + pr) * 128, (kh * 2 + pr + 1) * 128) for pr in range(2)]
                qps = [qr[pl.ds(r0, BLK), lanes[pr]] for pr in range(2)]
                dops = [do_ref[pl.ds(r0, BLK), lanes[pr]].astype(BF16) for pr in range(2)]
                heads = [(pr, half) for pr in range(2) for half in range(2)]
                kws, kcs = (ka[win, :], kb[win, :]), (kca[...], kcb[...])
                vws, vcs = (va[win, :], vb[win, :]), (vca[...], vcb[...])
                soft = []
                for idx, (pr, half) in enumerate(heads):
                    s_w = jnp.where(valid, _dot_nt(qps[pr], kws[half]) * ATT_SCALE, NEG_INF)
                    s_c = _dot_nt(qps[pr], kcs[half]) * ATT_SCALE
                    soft.append(_softmax_parts([s_w, s_c], sink_ref[kh * 4 + idx:kh * 4 + idx + 1, 0:1]))
                dps = [(_dot_nt(dops[pr], vws[half]), _dot_nt(dops[pr], vcs[half])) for pr, half in heads]
                ds_w, ds_c, pb_w, pb_c = [], [], [], []
                for idx in range(4):
                    (p_w, p_c), p_sink = soft[idx]
                    dp_w, dp_c = dps[idx]
                    dr = jnp.sum(p_w * dp_w, axis=1, keepdims=True) + jnp.sum(p_c * dp_c, axis=1, keepdims=True)
                    ds_w.append((p_w * (dp_w - dr) * ATT_SCALE).astype(BF16))
                    ds_c.append((p_c * (dp_c - dr) * ATT_SCALE).astype(BF16))
                    pb_w.append(p_w.astype(BF16))
                    pb_c.append(p_c.astype(BF16))
                    head = kh * 4 + idx
                    dsink_ref[head:head + 1, :] += jnp.broadcast_to(
                        jnp.sum(-p_sink * dr, axis=0, keepdims=True), (1, 128))
                for pr in range(2):
                    dqs[pl.ds(r0, BLK), lanes[pr]] = (
                        _dot(ds_w[2 * pr], kws[0]) + _dot(ds_c[2 * pr], kcs[0])
                        + _dot(ds_w[2 * pr + 1], kws[1]) + _dot(ds_c[2 * pr + 1], kcs[1]))
                q_hs, do_hs = [halves(qp) for qp in qps], [halves(do_p) for do_p in dops]
                q_stack = jnp.concatenate([q_hs[pr][half] for pr, half in heads], axis=0)
                do_stack = jnp.concatenate([do_hs[pr][half] for pr, half in heads], axis=0)
                dka[kh, win, :] += _dot_tn(jnp.concatenate(ds_w, axis=0), q_stack)
                dva[kh, win, :] += _dot_tn(jnp.concatenate(pb_w, axis=0), do_stack)
                dkca[kh] += _dot_tn(jnp.concatenate(ds_c, axis=0), q_stack)
                dvca[kh] += _dot_tn(jnp.concatenate(pb_c, axis=0), do_stack)
                return carry

            lax.fori_loop(0, n_lat // BLK, lat_block, 0, unroll=2)
            for n in range(n_ctx // BLK):
                rows = slice(n * BLK, (n + 1) * BLK)
                for pr in range(2):
                    lanes = slice((kh * 2 + pr) * 128, (kh * 2 + pr + 1) * 128)
                    qp = qc_ref[rows, lanes].astype(BF16)
                    do_p = doc_ref[rows, lanes]
                    q_h, do_h = halves(qp), halves(do_p)
                    dq = None
                    for half, (kcx, vcx) in enumerate(((kca, vca), (kcb, vcb))):
                        part = one_head(kh * 4 + pr * 2 + half, qp, q_h[half], do_p, do_h[half],
                                        None, kcx, None, vcx, None, None)
                        dq = part if dq is None else dq + part
                    dqc_ref[rows, lanes] = dq.astype(BF16)

        def fold(acc):
            r0 = acc[0] + pltpu.roll(acc[0], 64, 1)
            r1 = acc[1] + pltpu.roll(acc[1], 64, 1)
            return jnp.where(lo, r0, r1)

        for gq in range(4):
            sl = slice(gq * 128, (gq + 1) * 128)
            dq_ref[:, sl] = _rope_t(dqs[:, sl], cos_t, sin_t).astype(BF16)
        dk_ref[...] = _rope_t(fold(dka), cos_t, sin_t).astype(BF16)
        dv_ref[...] = fold(dva).astype(BF16)
        dkc_ref[...] = fold(dkca).astype(BF16)
        dvc_ref[...] = fold(dvca).astype(BF16)

    lat = lambda w: pl.BlockSpec((n_lat, w), lambda e: (e, 0))
    ctx = lambda w: pl.BlockSpec((n_ctx, w), lambda e: (e, 0))
    sd = jax.ShapeDtypeStruct
    return pl.pallas_call(
        body, grid=(2,), name=name,
        in_specs=_attn_specs(cfg) + [pl.BlockSpec((n_lat, ATT_W), lambda e: (e, 0)),
                                     pl.BlockSpec((n_ctx, ATT_W), lambda e: (cb + e, 0))],
        out_specs=[lat(ATT_W), lat(128), lat(128), ctx(ATT_W), ctx(128), ctx(128),
                   pl.BlockSpec((None, 8, 128), lambda e: (e, 0, 0))],
        out_shape=[sd((cfg.t_lat, ATT_W), BF16), sd((cfg.t_lat, 128), BF16), sd((cfg.t_lat, 128), BF16),
                   sd((cfg.t_ctx, ATT_W), BF16), sd((cfg.t_ctx, 128), BF16), sd((cfg.t_ctx, 128), BF16),
                   sd((2, 8, 128), F32)],
        scratch_shapes=[pltpu.VMEM((n_lat, ATT_W), BF16)] + [pltpu.VMEM((n_lat, 128), BF16)] * 4
        + [pltpu.VMEM((n_ctx, 128), BF16)] * 4
        + [pltpu.VMEM((n_lat, ATT_W), F32), pltpu.VMEM((2, n_lat, 128), F32), pltpu.VMEM((2, n_lat, 128), F32),
           pltpu.VMEM((2, n_ctx, 128), F32), pltpu.VMEM((2, n_ctx, 128), F32)],
        compiler_params=_params(("parallel",)))(p, p, p, p, p, p, cos, sin, sink_rows, dcat, dcat)


def _shift_down(x, k, row):
    return jnp.where(row >= k, pltpu.roll(x, k, 0), 0.0)


def _shift_up(x, k, row):
    n = x.shape[0]
    return jnp.where(row < n - k, pltpu.roll(x, n - k, 0), 0.0)


def _window_sum(x, r, row):
    below, above, k = x, x, 1
    while k < r:
        below = below + _shift_down(below, k, row)
        above = above + _shift_up(above, k, row)
        k *= 2
    return below + _shift_down(x, r, row) + _shift_up(above, 1, row)


def _inv_count(r, row, n):
    cnt = jnp.minimum(row + r, n - 1) + 1 - jnp.maximum(row - r, 0)
    return 1.0 / cnt.astype(F32)


def _pool_fwd(p, w, scale, n, blk0, n_seg, name):
    def body(u0, u1, u2, u3, w_ref, sc_ref, o_ref):
        row = lax.broadcasted_iota(jnp.int32, (n, 1), 0)
        for g, u_ref in enumerate((u0, u1, u2, u3)):
            u = u_ref[...].astype(F32)
            d = _window_sum(u, POOL_R[g], row) * _inv_count(POOL_R[g], row, n) - u
            o_ref[:, g * 128:(g + 1) * 128] = (_dot(d, w_ref[g]) * sc_ref[:, g * 128:(g + 1) * 128]).astype(BF16)

    return pl.pallas_call(
        body, grid=(n_seg,), name=name,
        in_specs=[pl.BlockSpec((n, 128), functools.partial(lambda g, e: (blk0 + e, 6 + g), g)) for g in range(4)]
        + [pl.BlockSpec((4, 128, 128), lambda e: (0, 0, 0)), pl.BlockSpec((1, 512), lambda e: (0, 0))],
        out_specs=pl.BlockSpec((n, 512), lambda e: (e, 0)),
        out_shape=jax.ShapeDtypeStruct((n_seg * n, 512), BF16),
        compiler_params=_params(("parallel",)))(p, p, p, p, w, scale)


def _pool_bwd(p, w, scale, dcat, n, blk0, n_seg, name):
    def body(u0, u1, u2, u3, w_ref, sc_ref, dp_ref, du_ref, dw_ref, dsc_ref):
        e = pl.program_id(0)

        @pl.when(e == 0)
        def _():
            dw_ref[...] = jnp.zeros_like(dw_ref)
            dsc_ref[...] = jnp.zeros_like(dsc_ref)

        row = lax.broadcasted_iota(jnp.int32, (n, 1), 0)
        for g, u_ref in enumerate((u0, u1, u2, u3)):
            sl = slice(g * 128, (g + 1) * 128)
            u = u_ref[...].astype(F32)
            inv = _inv_count(POOL_R[g], row, n)
            d = _window_sum(u, POOL_R[g], row) * inv - u
            dp = dp_ref[:, sl].astype(F32)
            dsc_ref[:, sl] += jnp.sum(dp * _dot(d, w_ref[g]), axis=0, keepdims=True)
            dyp = dp * sc_ref[:, sl]
            dw_ref[g] += _dot_tn(d, dyp)
            dd = _dot_nt(dyp, w_ref[g])
            du_ref[:, sl] = (_window_sum(dd * inv, POOL_R[g], row) - dd).astype(BF16)

    return pl.pallas_call(
        body, grid=(n_seg,), name=name,
        in_specs=[pl.BlockSpec((n, 128), functools.partial(lambda g, e: (blk0 + e, 6 + g), g)) for g in range(4)]
        + [pl.BlockSpec((4, 128, 128), lambda e: (0, 0, 0)), pl.BlockSpec((1, 512), lambda e: (0, 0)),
           pl.BlockSpec((n, 512), lambda e: (blk0 + e, 1))],
        out_specs=[pl.BlockSpec((n, 512), lambda e: (e, 0)),
                   pl.BlockSpec((4, 128, 128), lambda e: (0, 0, 0)), pl.BlockSpec((1, 512), lambda e: (0, 0))],
        out_shape=[jax.ShapeDtypeStruct((n_seg * n, 512), BF16), jax.ShapeDtypeStruct((4, 128, 128), F32),
                   jax.ShapeDtypeStruct((1, 512), F32)],
        compiler_params=_params(("arbitrary",)))(p, p, p, p, w, scale, dcat)


def _gelu(x):
    t = jnp.tanh(math.sqrt(2.0 / math.pi) * (x + 0.044715 * x * x * x))
    return 0.5 * x * (1.0 + t), t


def _gelu_grad(x, t):
    return 0.5 * (1.0 + t) + 0.5 * x * (1.0 - t * t) * (math.sqrt(2.0 / math.pi) * (1.0 + 3 * 0.044715 * x * x))


def _neg_expm1_twice(x):
    t = jnp.tanh(x)
    return (-2.0 * t) / (1.0 - t)


def _softplus_neg(lam):
    x = -lam
    e = jnp.exp(-jnp.abs(x))
    log1p = jnp.where(e < 1e-2, e * (1.0 - e * (0.5 - e * (1.0 / 3.0))), jnp.log(1.0 + e))
    return jnp.maximum(x, 0.0) + log1p, -_sigmoid(x)


def _conv(u, w_ref, b_ref, row):
    return (b_ref[...] + _shift_down(u, 1, row) * w_ref[0:1, :] + u * w_ref[1:2, :]
            + _shift_up(u, 1, row) * w_ref[2:3, :] + _shift_up(u, 2, row) * w_ref[3:4, :])


def _lru_gates(uc, d, wa_ref, ba_ref, wx_ref, bx_ref, lam_ref):
    r = _sigmoid(_dot(uc, wa_ref[d]) + ba_ref[d:d + 1, :])
    gi = _sigmoid(_dot(uc, wx_ref[d]) + bx_ref[d:d + 1, :])
    sp, dsp = _softplus_neg(lam_ref[d:d + 1, :])
    la = (-LRU_C) * r * sp
    a = jnp.exp(la)
    sq = jnp.sqrt(_neg_expm1_twice(la))
    return r, gi, sp, dsp, a, sq


def _tile_scan(a_ref, b_ref, n, reverse):
    m = n // 8
    first = 7 if reverse else 0
    a_prev = a_ref[pl.ds(first, m, stride=8), :]
    b_prev = b_ref[pl.ds(first, m, stride=8), :]
    for j in (range(6, -1, -1) if reverse else range(1, 8)):
        rows = pl.ds(j, m, stride=8)
        aj = a_ref[rows, :]
        b_prev = aj * b_prev + b_ref[rows, :]
        a_prev = aj * a_prev
        b_ref[rows, :] = b_prev
        a_ref[rows, :] = a_prev


def _carry_scan(a_ref, b_ref, n, reverse, carry):
    nt8 = n // 8

    def step(i, c):
        t = (nt8 - 1 - i) if reverse else i
        off = pl.multiple_of(t * 8, 8)
        h = a_ref[pl.ds(off, 8), :] * c + b_ref[pl.ds(off, 8), :]
        b_ref[pl.ds(off, 8), :] = h
        return h[0:1, :] if reverse else h[7:8, :]

    return lax.fori_loop(0, nt8, step, carry, unroll=4)


def _chain_scan(segs, reverse):
    carry = jnp.zeros((1, 128), F32)
    for a, b, a_ref, b_ref, n in segs:
        a_ref[...] = a
        b_ref[...] = b
        _tile_scan(a_ref, b_ref, n, reverse)
        carry = _carry_scan(a_ref, b_ref, n, reverse, carry)


def _lru_specs(cfg):
    n_lat, n_ctx, cb = cfg.n_lat, cfg.n_ctx, cfg.ctx_blk
    return [pl.BlockSpec((n_lat, 128), lambda hb, e: (e, hb)),
            pl.BlockSpec((n_lat, 128), lambda hb, e: (e, 8 + hb)),
            pl.BlockSpec((n_ctx, 128), lambda hb, e: (cb + e, hb)),
            pl.BlockSpec((n_ctx, 128), lambda hb, e: (cb + e, 8 + hb)),
            pl.BlockSpec((4, 128), lambda hb, e: (0, hb)),
            pl.BlockSpec((1, 128), lambda hb, e: (0, hb)),
            pl.BlockSpec((2, None, 128, 128), lambda hb, e: (0, hb, 0, 0)),
            pl.BlockSpec((2, 128), lambda hb, e: (0, hb)),
            pl.BlockSpec((2, None, 128, 128), lambda hb, e: (0, hb, 0, 0)),
            pl.BlockSpec((2, 128), lambda hb, e: (0, hb)),
            pl.BlockSpec((2, 128), lambda hb, e: (0, hb))]


def _lru_fwd(cfg, p, consts, name):
    n_lat, n_ctx = cfg.n_lat, cfg.n_ctx

    def body(gl_ref, ul_ref, gc_ref, uc_ref, cw_ref, cb_ref, wa_ref, ba_ref, wx_ref, bx_ref, lam_ref,
             zl_ref, zc_ref, hl_ref, hc_ref, al, ac):
        row_l = lax.broadcasted_iota(jnp.int32, (n_lat, 1), 0)
        row_c = lax.broadcasted_iota(jnp.int32, (n_ctx, 1), 0)
        uc_l = _conv(ul_ref[...].astype(F32), cw_ref, cb_ref, row_l)
        uc_c = _conv(uc_ref[...].astype(F32), cw_ref, cb_ref, row_c)
        for d in range(2):
            _, gi_l, _, _, a_l, sq_l = _lru_gates(uc_l, d, wa_ref, ba_ref, wx_ref, bx_ref, lam_ref)
            _, gi_c, _, _, a_c, sq_c = _lru_gates(uc_c, d, wa_ref, ba_ref, wx_ref, bx_ref, lam_ref)
            _chain_scan([(a_c, sq_c * (gi_c * uc_c), ac, hc_ref.at[d], n_ctx),
                         (a_l, sq_l * (gi_l * uc_l), al, hl_ref.at[d], n_lat)], reverse=(d == 1))
        zl_ref[...] = (_gelu(gl_ref[...].astype(F32))[0] * (hl_ref[0] + hl_ref[1])).astype(BF16)
        zc_ref[...] = (_gelu(gc_ref[...].astype(F32))[0] * (hc_ref[0] + hc_ref[1])).astype(BF16)

    return pl.pallas_call(
        body, grid=(8, 2), name=name, in_specs=_lru_specs(cfg),
        out_specs=[pl.BlockSpec((n_lat, 128), lambda hb, e: (e, hb)), pl.BlockSpec((n_ctx, 128), lambda hb, e: (e, hb)),
                   pl.BlockSpec((2, n_lat, 128), lambda hb, e: (0, e, hb)),
                   pl.BlockSpec((2, n_ctx, 128), lambda hb, e: (0, e, hb))],
        out_shape=[jax.ShapeDtypeStruct((cfg.t_lat, D), BF16), jax.ShapeDtypeStruct((cfg.t_ctx, D), BF16),
                   jax.ShapeDtypeStruct((2, cfg.t_lat, D), F32), jax.ShapeDtypeStruct((2, cfg.t_ctx, D), F32)],
        scratch_shapes=[pltpu.VMEM((n_lat, 128), F32), pltpu.VMEM((n_ctx, 128), F32)],
        compiler_params=_params(("parallel", "arbitrary")))(p, p, p, p, *consts)


def _lru_bwd(cfg, p, dz, h_lat, h_ctx, consts, name):
    n_lat, n_ctx, cb = cfg.n_lat, cfg.n_ctx, cfg.ctx_blk

    def body(gl_ref, ul_ref, gc_ref, uc_ref, cw_ref, cb_ref, wa_ref, ba_ref, wx_ref, bx_ref, lam_ref,
             dzl_ref, dzc_ref, hl, hc, dgl_ref, dul_ref, dgc_ref, duc_ref, dwa_ref, dwx_ref, vec_ref,
             al, bl, ac, bc):
        e = pl.program_id(1)

        @pl.when(e == 0)
        def _():
            dwa_ref[...] = jnp.zeros_like(dwa_ref)
            dwx_ref[...] = jnp.zeros_like(dwx_ref)
            vec_ref[...] = jnp.zeros_like(vec_ref)

        row_l = lax.broadcasted_iota(jnp.int32, (n_lat, 1), 0)
        row_c = lax.broadcasted_iota(jnp.int32, (n_ctx, 1), 0)
        u_l, u_c = ul_ref[...].astype(F32), uc_ref[...].astype(F32)
        uc_l = _conv(u_l, cw_ref, cb_ref, row_l)
        uc_c = _conv(u_c, cw_ref, cb_ref, row_c)
        gel_l, t_l = _gelu(gl_ref[...].astype(F32))
        gel_c, t_c = _gelu(gc_ref[...].astype(F32))
        dz_l, dz_c = dzl_ref[...].astype(F32), dzc_ref[...].astype(F32)
        dgl_ref[...] = (dz_l * (hl[0] + hl[1]) * _gelu_grad(gl_ref[...].astype(F32), t_l)).astype(BF16)
        dgc_ref[...] = (dz_c * (hc[0] + hc[1]) * _gelu_grad(gc_ref[...].astype(F32), t_c)).astype(BF16)
        dy_l, dy_c = dz_l * gel_l, dz_c * gel_c
        duc_l = jnp.zeros((n_lat, 128), F32)
        duc_c = jnp.zeros((n_ctx, 128), F32)
        for d in range(2):
            r_l, gi_l, sp, dsp, a_l, sq_l = _lru_gates(uc_l, d, wa_ref, ba_ref, wx_ref, bx_ref, lam_ref)
            r_c, gi_c, _, _, a_c, sq_c = _lru_gates(uc_c, d, wa_ref, ba_ref, wx_ref, bx_ref, lam_ref)
            if d == 0:
                an_l = _shift_up(a_l, 1, row_l)
                an_c = jnp.where(row_c < n_ctx - 1, pltpu.roll(a_c, n_ctx - 1, 0), a_l[0:1, :])
            else:
                an_l = _shift_down(a_l, 1, row_l)
                an_c = jnp.where(row_c >= 1, pltpu.roll(a_c, 1, 0), a_l[n_lat - 1:n_lat, :])
            _chain_scan([(an_l, dy_l, al, bl, n_lat), (an_c, dy_c, ac, bc, n_ctx)], reverse=(d == 0))
            dsp_sum = jnp.zeros((1, 128), F32)
            for (dh, h, r, gi, a, sq, uc, seg) in ((bl[...], hl[d], r_l, gi_l, a_l, sq_l, uc_l, "l"),
                                                  (bc[...], hc[d], r_c, gi_c, a_c, sq_c, uc_c, "c")):
                b0 = sq * (gi * uc)
                t1 = dh * sq
                dla = dh * (h - b0) - (dh * gi * uc) * (a * a) / sq
                dzr = (dla * ((-LRU_C) * sp)) * r * (1.0 - r)
                dzi = (t1 * uc) * gi * (1.0 - gi)
                dsp_sum = dsp_sum + jnp.sum(dla * ((-LRU_C) * r), axis=0, keepdims=True)
                dwa_ref[d] += _dot_tn(uc, dzr)
                dwx_ref[d] += _dot_tn(uc, dzi)
                vec_ref[d:d + 1, :] += jnp.sum(dzr, axis=0, keepdims=True)
                vec_ref[2 + d:3 + d, :] += jnp.sum(dzi, axis=0, keepdims=True)
                duc = t1 * gi + _dot_nt(dzr, wa_ref[d]) + _dot_nt(dzi, wx_ref[d])
                if seg == "l":
                    duc_l = duc_l + duc
                else:
                    duc_c = duc_c + duc
            vec_ref[4 + d:5 + d, :] += dsp_sum * dsp
        for duc, u, row, du_ref in ((duc_l, u_l, row_l, dul_ref), (duc_c, u_c, row_c, duc_ref)):
            du_ref[...] = (_shift_up(duc, 1, row) * cw_ref[0:1, :] + duc * cw_ref[1:2, :]
                           + _shift_down(duc, 1, row) * cw_ref[2:3, :]
                           + _shift_down(duc, 2, row) * cw_ref[3:4, :]).astype(BF16)
            vec_ref[6:7, :] += jnp.sum(duc * _shift_down(u, 1, row), axis=0, keepdims=True)
            vec_ref[7:8, :] += jnp.sum(duc * u, axis=0, keepdims=True)
            vec_ref[8:9, :] += jnp.sum(duc * _shift_up(u, 1, row), axis=0, keepdims=True)
            vec_ref[9:10, :] += jnp.sum(duc * _shift_up(u, 2, row), axis=0, keepdims=True)
            vec_ref[10:11, :] += jnp.sum(duc, axis=0, keepdims=True)

    lat = pl.BlockSpec((n_lat, 128), lambda hb, e: (e, hb))
    ctx = pl.BlockSpec((n_ctx, 128), lambda hb, e: (e, hb))
    wspec = pl.BlockSpec((2, None, 128, 128), lambda hb, e: (0, hb, 0, 0))
    sd = jax.ShapeDtypeStruct
    return pl.pallas_call(
        body, grid=(8, 2), name=name,
        in_specs=_lru_specs(cfg) + [pl.BlockSpec((n_lat, 128), lambda hb, e: (e, hb)),
                                    pl.BlockSpec((n_ctx, 128), lambda hb, e: (cb + e, hb)),
                                    pl.BlockSpec((2, n_lat, 128), lambda hb, e: (0, e, hb)),
                                    pl.BlockSpec((2, n_ctx, 128), lambda hb, e: (0, e, hb))],
        out_specs=[lat, lat, ctx, ctx, wspec, wspec, pl.BlockSpec((None, 16, 128), lambda hb, e: (hb, 0, 0))],
        out_shape=[sd((cfg.t_lat, D), BF16), sd((cfg.t_lat, D), BF16), sd((cfg.t_ctx, D), BF16), sd((cfg.t_ctx, D), BF16),
                   sd((2, 8, 128, 128), F32), sd((2, 8, 128, 128), F32), sd((8, 16, 128), F32)],
        scratch_shapes=[pltpu.VMEM((n_lat, 128), F32)] * 2 + [pltpu.VMEM((n_ctx, 128), F32)] * 2,
        compiler_params=_params(("parallel", "arbitrary")))(p, p, p, p, *consts, dz, dz, h_lat, h_ctx)


def _position():
    x, y, c = lax.axis_index("x"), lax.axis_index("y"), lax.axis_index("c")
    return x, y, c, 4 * x + 2 * y + c


def _peer(x, y, c, k):
    px = 1 - x if k & 4 else x
    py = 1 - y if k & 2 else y
    pc = 1 - c if k & 1 else c
    return (px, py, pc), 4 * px + 2 * py + pc


def _all_gather(v, name, in_vmem):
    def body(v_ref, o_ref, send_sems, recv_sems, local_sem):
        x, y, c, me = _position()
        mine = pltpu.make_async_copy(v_ref, o_ref.at[me], local_sem)
        mine.start()
        sends = []
        for k in range(1, N_DEV):
            peer, _ = _peer(x, y, c, k)
            cp = pltpu.make_async_remote_copy(src_ref=v_ref, dst_ref=o_ref.at[me], send_sem=send_sems.at[k - 1],
                                              recv_sem=recv_sems.at[k - 1], device_id=peer, device_id_type=MESH)
            cp.start()
            sends.append(cp)
        for k in range(1, N_DEV):
            peer, peer_lin = _peer(x, y, c, k)
            pltpu.make_async_remote_copy(src_ref=v_ref, dst_ref=o_ref.at[peer_lin], send_sem=send_sems.at[k - 1],
                                         recv_sem=recv_sems.at[k - 1], device_id=peer, device_id_type=MESH).wait_recv()
        for cp in sends:
            cp.wait_send()
        mine.wait()

    space = pltpu.VMEM if in_vmem else pl.ANY
    return pl.pallas_call(
        body, name=name,
        in_specs=[pl.BlockSpec(memory_space=space)], out_specs=pl.BlockSpec(memory_space=space),
        out_shape=jax.ShapeDtypeStruct((N_DEV,) + v.shape, v.dtype),
        scratch_shapes=[pltpu.SemaphoreType.DMA((N_DEV - 1,)), pltpu.SemaphoreType.DMA((N_DEV - 1,)),
                        pltpu.SemaphoreType.DMA],
        compiler_params=pltpu.CompilerParams(vmem_limit_bytes=VMEM_LIMIT))(v)


_HBM = pl.BlockSpec(memory_space=pltpu.HBM)
_SEM = pl.BlockSpec(memory_space=pltpu.SEMAPHORE)
_EFFECT = pltpu.SideEffectType.DATAFLOW_SIDE_EFFECTING


ALL_PEERS = tuple(range(1, N_DEV))
SAME_CORE_AND_SIBLING = (1, 2, 4, 6)


def _push_start(src, land, block_of, name, relations=ALL_PEERS):
    def body(src_ref, land_ref, send_sem, recv_sem, src_thru, land_thru, token):
        x, y, c, me = _position()
        for k in relations:
            peer, peer_lin = _peer(x, y, c, k)
            mine, there = block_of(src_ref, land_ref, me, peer_lin)
            pltpu.make_async_remote_copy(src_ref=mine, dst_ref=there, send_sem=send_sem, recv_sem=recv_sem,
                                         device_id=peer, device_id_type=MESH).start()
        mine, here = block_of(src_ref, land_ref, me, me)
        pltpu.make_async_copy(mine, here, recv_sem).start()
        token[...] = jnp.zeros_like(token)

    return pl.pallas_call(
        body, name=name,
        out_shape=(pltpu.SemaphoreType.DMA(()), pltpu.SemaphoreType.DMA(()), pltpu.HBM(src.shape, src.dtype),
                   pltpu.HBM(land.shape, land.dtype), jax.ShapeDtypeStruct((8, 128), F32)),
        in_specs=(_HBM, _HBM), out_specs=(_SEM, _SEM, _HBM, _HBM, pl.BlockSpec(memory_space=pltpu.VMEM)),
        input_output_aliases={0: 2, 1: 3},
        compiler_params=pltpu.CompilerParams(has_side_effects=_EFFECT),
    )(pltpu.with_memory_space_constraint(src, pltpu.HBM), pltpu.with_memory_space_constraint(land, pltpu.HBM))


def _push_wait(handle, blocks_of, after, name, n_peers=N_DEV - 1):
    send_sem, recv_sem, src_thru, land_thru, _ = handle

    def body(src_ref, land_ref, send_sem, recv_sem, after_ref, src_dead, got_ref):
        x, y, c, _ = _position()
        sent, landed = blocks_of(land_ref, n_peers), blocks_of(land_ref, n_peers + 1)
        pltpu.make_async_remote_copy(src_ref=sent, dst_ref=sent, send_sem=send_sem, recv_sem=recv_sem,
                                     device_id=(x, y, 1 - c), device_id_type=MESH).wait_send()
        pltpu.make_async_remote_copy(src_ref=landed, dst_ref=landed, send_sem=send_sem, recv_sem=recv_sem,
                                     device_id=(x, y, 1 - c), device_id_type=MESH).wait_recv()

    return pl.pallas_call(
        body, name=name,
        out_shape=(pltpu.HBM(src_thru.shape, src_thru.dtype), pltpu.HBM(land_thru.shape, land_thru.dtype)),
        in_specs=(_HBM, _HBM, _SEM, _SEM, pl.BlockSpec(memory_space=pl.ANY)), out_specs=(_HBM, _HBM),
        input_output_aliases={0: 0, 1: 1},
        compiler_params=pltpu.CompilerParams(has_side_effects=_EFFECT),
    )(src_thru, land_thru, send_sem, recv_sem, after)[1]


def _gather_start(src, name, relations=ALL_PEERS):
    g, r, C = src.shape
    land = lax.empty((g, N_DEV * r, C), src.dtype)
    return _push_start(src, land, lambda s, z, i, p: (s, z.at[:, pl.ds(i * r, r), :]), name, relations)


def _gather_wait(handle, after, name, n_peers=N_DEV - 1):
    r = handle[2].shape[1]
    return _push_wait(handle, lambda z, n: z.at[:, pl.ds(0, n * r), :], after, name, n_peers)


def _relay_start(land, r, name):
    def body(land_ref, send_sem, recv_sem, land_thru, token):
        x, y, c, _ = _position()
        for k in (2, 4, 6):
            _, origin = _peer(x, y, c, k)
            rows = land_ref.at[:, pl.ds(origin * r, r), :]
            pltpu.make_async_remote_copy(src_ref=rows, dst_ref=rows, send_sem=send_sem, recv_sem=recv_sem,
                                         device_id=(x, y, 1 - c), device_id_type=MESH).start()
        token[...] = jnp.zeros_like(token)

    return pl.pallas_call(
        body, name=name,
        out_shape=(pltpu.SemaphoreType.DMA(()), pltpu.SemaphoreType.DMA(()), pltpu.HBM(land.shape, land.dtype),
                   jax.ShapeDtypeStruct((8, 128), F32)),
        in_specs=(_HBM,), out_specs=(_SEM, _SEM, _HBM, pl.BlockSpec(memory_space=pltpu.VMEM)),
        input_output_aliases={0: 2},
        compiler_params=pltpu.CompilerParams(has_side_effects=_EFFECT),
    )(pltpu.with_memory_space_constraint(land, pltpu.HBM))


def _relay_wait(handle, r, after, name):
    send_sem, recv_sem, land_thru, _ = handle

    def body(land_ref, send_sem, recv_sem, after_ref, got_ref):
        x, y, c, _ = _position()
        three = land_ref.at[:, pl.ds(0, 3 * r), :]
        cp = pltpu.make_async_remote_copy(src_ref=three, dst_ref=three, send_sem=send_sem, recv_sem=recv_sem,
                                          device_id=(x, y, 1 - c), device_id_type=MESH)
        cp.wait_send()
        cp.wait_recv()

    return pl.pallas_call(
        body, name=name, out_shape=(pltpu.HBM(land_thru.shape, land_thru.dtype),),
        in_specs=(_HBM, _SEM, _SEM, pl.BlockSpec(memory_space=pl.ANY)), out_specs=(_HBM,),
        input_output_aliases={0: 0},
        compiler_params=pltpu.CompilerParams(has_side_effects=_EFFECT),
    )(land_thru, send_sem, recv_sem, after)[0]


def _exchange_start(grad, name):
    g, rows, C = grad.shape
    r = rows // N_DEV
    land = lax.empty((N_DEV, g, r, C), grad.dtype)
    return _push_start(grad, land, lambda s, z, i, p: (s.at[:, pl.ds(p * r, r), :], z.at[i]), name)


def _exchange_wait(handle, after, name):
    return _push_wait(handle, lambda z, n: z.at[pl.ds(0, n)], after, name)


def _sum_blocks(v, name):
    k, rows, cols = v.shape
    tr = rows
    for cand in (rows, 512, 352, 256, 176, 128, 64, 32, 16):
        if rows % cand == 0 and k * cand * cols * v.dtype.itemsize <= 6 * 1024 * 1024:
            tr = cand
            break

    def body(v_ref, o_ref):
        acc = v_ref[0].astype(F32)
        for s in range(1, k):
            acc = acc + v_ref[s].astype(F32)
        o_ref[...] = acc

    return pl.pallas_call(
        body, grid=(rows // tr,), name=name,
        in_specs=[pl.BlockSpec((k, tr, cols), lambda i: (0, i, 0))],
        out_specs=pl.BlockSpec((tr, cols), lambda i: (i, 0)),
        out_shape=jax.ShapeDtypeStruct((rows, cols), F32),
        compiler_params=_params(("parallel",)))(v)


def _adam_math(w, g, m, v):
    m2 = B1 * m + (1.0 - B1) * g
    v2 = B2 * v + (1.0 - B2) * (g * g)
    m_hat = m2 / (1.0 - B1 ** STEP)
    v_hat = v2 / (1.0 - B2 ** STEP)
    return -LR * (m_hat / (jnp.sqrt(v_hat) + EPS) + WD * w), m2, v2


def _adamw(w, g, m, v, name, dep=None):
    shp = w.shape
    rows, cols = (shp[-2], shp[-1]) if len(shp) >= 2 else (1, shp[-1])
    lead = math.prod(shp[:-2]) if len(shp) > 2 else 1
    fits = [t for t in range(8, rows + 1, 8) if rows % t == 0 and t * cols * 4 <= 2 * 1024 * 1024]
    tr = max(fits) if fits else rows

    def body(w_ref, g_ref, m_ref, v_ref, *rest):
        d_ref, m2_ref, v2_ref = rest[-3:]
        d_ref[...], m2_ref[...], v2_ref[...] = _adam_math(w_ref[...], g_ref[...], m_ref[...], v_ref[...])

    blk = pl.BlockSpec((None, tr, cols), lambda b, i: (b, i, 0))
    extra = [] if dep is None else [dep]
    outs = pl.pallas_call(
        body, grid=(lead, rows // tr), name=name,
        in_specs=[blk] * 4 + [pl.BlockSpec(memory_space=pl.ANY)] * len(extra), out_specs=[blk] * 3,
        out_shape=[jax.ShapeDtypeStruct((lead, rows, cols), F32)] * 3,
        compiler_params=_params(("parallel", "parallel")))(*[a.reshape(lead, rows, cols) for a in (w, g, m, v)], *extra)
    return [o.reshape(shp) for o in outs]


def _as2d(a):
    n = a.size
    if n % 1024 == 0:
        return a.reshape(n // 1024, 1024)
    if n % 128 == 0:
        return a.reshape(n // 128, 128)
    return a.reshape(1, n)


def _blocks_to_cols(a):
    b = jnp.moveaxis(a, 0, -2)
    return b.reshape(b.shape[:-2] + (b.shape[-2] * b.shape[-1],))


def _pack_rows(parts):
    padded, offs, r = [], [], 0
    for p in parts:
        pad = (-p.shape[0]) % 8
        padded.append(jnp.pad(p, ((0, pad), (0, 0))) if pad else p)
        offs.append(r)
        r += p.shape[0] + pad
    return jnp.concatenate(padded, axis=0), offs


def _silu(x):
    return x * jax.nn.sigmoid(x)


def kernel(x, c, ctx, c_ctx, w_mod, b_mod, ln_g, ln_b, ffn_w_gate, ffn_w_up, ffn_w_down, mix_ab_w_in, attn_sink, pool_w, pool_scale, mix_ab_w_out, lru_w_in, lru_conv_w, lru_conv_b, lru_wa, lru_ba, lru_wx, lru_bx, lru_lambda, lru_w_out, loss_target, m_c_ctx, m_w_mod, m_b_mod, m_ln_g, m_ln_b, m_ffn_w_gate, m_ffn_w_up, m_ffn_w_down, m_mix_ab_w_in, m_attn_sink, m_pool_w, m_pool_scale, m_mix_ab_w_out, m_lru_w_in, m_lru_conv_w, m_lru_conv_b, m_lru_wa, m_lru_ba, m_lru_wx, m_lru_bx, m_lru_lambda, m_lru_w_out, v_c_ctx, v_w_mod, v_b_mod, v_ln_g, v_ln_b, v_ffn_w_gate, v_ffn_w_up, v_ffn_w_down, v_mix_ab_w_in, v_attn_sink, v_pool_w, v_pool_scale, v_mix_ab_w_out, v_lru_w_in, v_lru_conv_w, v_lru_conv_b, v_lru_wa, v_lru_ba, v_lru_wx, v_lru_bx, v_lru_lambda, v_lru_w_out):
    weights = dict(c_ctx=c_ctx, w_mod=w_mod, b_mod=b_mod, ln_g=ln_g, ln_b=ln_b, ffn_w_gate=ffn_w_gate,
                   ffn_w_up=ffn_w_up, ffn_w_down=ffn_w_down, mix_ab_w_in=mix_ab_w_in, attn_sink=attn_sink,
                   pool_w=pool_w, pool_scale=pool_scale, mix_ab_w_out=mix_ab_w_out, lru_w_in=lru_w_in,
                   lru_conv_w=lru_conv_w, lru_conv_b=lru_conv_b, lru_wa=lru_wa, lru_ba=lru_ba, lru_wx=lru_wx,
                   lru_bx=lru_bx, lru_lambda=lru_lambda, lru_w_out=lru_w_out)
    mom_m = dict(c_ctx=m_c_ctx, w_mod=m_w_mod, b_mod=m_b_mod, ln_g=m_ln_g, ln_b=m_ln_b, ffn_w_gate=m_ffn_w_gate,
                 ffn_w_up=m_ffn_w_up, ffn_w_down=m_ffn_w_down, mix_ab_w_in=m_mix_ab_w_in, attn_sink=m_attn_sink,
                 pool_w=m_pool_w, pool_scale=m_pool_scale, mix_ab_w_out=m_mix_ab_w_out, lru_w_in=m_lru_w_in,
                 lru_conv_w=m_lru_conv_w, lru_conv_b=m_lru_conv_b, lru_wa=m_lru_wa, lru_ba=m_lru_ba, lru_wx=m_lru_wx,
                 lru_bx=m_lru_bx, lru_lambda=m_lru_lambda, lru_w_out=m_lru_w_out)
    mom_v = dict(c_ctx=v_c_ctx, w_mod=v_w_mod, b_mod=v_b_mod, ln_g=v_ln_g, ln_b=v_ln_b, ffn_w_gate=v_ffn_w_gate,
                 ffn_w_up=v_ffn_w_up, ffn_w_down=v_ffn_w_down, mix_ab_w_in=v_mix_ab_w_in, attn_sink=v_attn_sink,
                 pool_w=v_pool_w, pool_scale=v_pool_scale, mix_ab_w_out=v_mix_ab_w_out, lru_w_in=v_lru_w_in,
                 lru_conv_w=v_lru_conv_w, lru_conv_b=v_lru_conv_b, lru_wa=v_lru_wa, lru_ba=v_lru_ba, lru_wx=v_lru_wx,
                 lru_bx=v_lru_bx, lru_lambda=v_lru_lambda, lru_w_out=v_lru_w_out)
    names = list(weights)

    n_lat, n_ctx = x.shape[1], ctx.shape[1]
    cfg = _Cfg(n_lat, n_ctx)
    _, _, _, me = _position()
    mcols = w_mod.shape[2]

    def t_bf16(w):
        return jnp.swapaxes(w, -1, -2).astype(BF16)

    def ffn_src(l, i):
        return jnp.stack([t_bf16(ffn_w_gate[l, i]), t_bf16(ffn_w_up[l, i]), ffn_w_down[l, i].astype(BF16)])

    pending = {}

    def start_gathers(items, tok):
        for key, make_src in items:
            pending[key] = _gather_start(make_src() + tok.astype(BF16), "gather_start_" + key)
            tok = pending[key][4][0, 0]
        return tok

    def weights_now(key, after):
        return _gather_wait(pending[key], after, "gather_wait_" + key)

    first = _gather_start(ffn_src(0, 0), "gather_start_ffn00", SAME_CORE_AND_SIBLING)
    tok = first[4][0, 0]

    small_names = ["ln_g", "ln_b", "lru_conv_w", "lru_conv_b", "lru_ba", "lru_bx", "lru_lambda"]
    small, small_off = _pack_rows([(c + tok).reshape(-1, 128)] + [weights[n].reshape(-1, 128) for n in small_names])
    small_all = _all_gather(small, "gather_small", True)

    def small_full(idx, shp):
        rows = math.prod(shp) // 128
        return _blocks_to_cols(small_all[:, small_off[idx]:small_off[idx] + rows, :].reshape((N_DEV,) + shp))

    c_all = small_all[:, :2 * D // 128, :].reshape(2 * N_DEV, D)
    ln_g_f, ln_b_f = small_full(1, ln_g.shape), small_full(2, ln_b.shape)
    lru_consts = (small_full(3, lru_conv_w.shape)[0], small_full(4, lru_conv_b.shape), lru_wa[0],
                  small_full(5, lru_ba.shape)[0], lru_wx[0], small_full(6, lru_bx.shape)[0],
                  small_full(7, lru_lambda.shape)[0])

    s_rows = jnp.zeros((32, D), F32).at[:16].set(_silu(c_all)).at[16].set(_silu(c_ctx)).astype(BF16)
    mod_mine = jnp.stack([_matmul(s_rows, w_mod[l], "nn", F32, "mod_fwd", bn_cap=1280) for l in range(2)])
    mod_all = _all_gather(mod_mine.reshape(64, mcols), "gather_mod", True).reshape(N_DEV, 2, 32, mcols)
    r_ffn = ffn_w_down.shape[2]
    relay = _relay_start(_gather_wait(first, mod_all, "gather_wait_ffn00", n_peers=len(SAME_CORE_AND_SIBLING)),
                         r_ffn, "gather_relay_start_ffn00")
    tok = start_gathers([("ab_in", lambda: t_bf16(mix_ab_w_in)), ("ab_out", lambda: mix_ab_w_out.astype(BF16)),
                         ("ffn01", lambda: ffn_src(0, 1)), ("ffn10", lambda: ffn_src(1, 0)),
                         ("lru_in", lambda: t_bf16(lru_w_in)), ("lru_out", lambda: lru_w_out.astype(BF16)),
                         ("ffn11", lambda: ffn_src(1, 1))], relay[3][0, 0])
    mod_full = _blocks_to_cols(mod_all) + (b_mod[:, None, :] + tok)
    ex0 = 2 * me
    mods = []
    for l in range(2):
        rows = jnp.stack([lax.dynamic_index_in_dim(mod_full[l], ex0, 0, False),
                          lax.dynamic_index_in_dim(mod_full[l], ex0 + 1, 0, False), mod_full[l, 16]])
        mods.append(rows.reshape(3, N_MOD, D))

    h0 = jnp.concatenate([x.reshape(cfg.t_lat, D), ctx.reshape(cfg.t_ctx, D)], axis=0)
    cos, sin = _rope_tables(n_lat)
    sink_rows = jnp.broadcast_to(attn_sink[0][:, None], (8, 128)).astype(F32)

    saved = []
    wf = [[None, None], [None, None]]
    h = h0
    xin = _modulate(cfg, h0, mods[0], 0, 1, "modulate_in")
    for l in range(2):
        st = {"h_in": h, "xin1": xin}
        wf[l][0] = (_relay_wait(relay, r_ffn, xin, "gather_relay_wait_ffn00") if l == 0
                    else weights_now("ffn10", xin))
        g1, u1, y1 = _ffn_fwd(xin, wf[l][0], "ffn_fwd")
        h1, xhat1, rstd1, xin2 = _ln_fwd(cfg, h, y1, mods[l], 2, 0.5, ln_g_f[l, 0][None], ln_b_f[l, 0][None],
                                          mods[l], (3, 4), "ln_fwd_a")
        st.update(g1=g1, u1=u1, y1=y1, h1=h1, xhat1=xhat1, rstd1=rstd1, xin2=xin2)
        if l == 0:
            w_ab_in_t = weights_now("ab_in", xin2)[0]
            p = _matmul(xin2, w_ab_in_t, "nt", BF16, "mix_ab_in")
            att_l, att_c = _attn_fwd(cfg, p, cos, sin, sink_rows, "attn_fwd")
            pool_l = _pool_fwd(p, pool_w[0], pool_scale, n_lat, 0, 2, "pool_fwd_lat")
            pool_c = _pool_fwd(p, pool_w[0], pool_scale, n_ctx, cfg.ctx_blk, 2, "pool_fwd_ctx")
            cat = jnp.concatenate([jnp.concatenate([att_l, pool_l], axis=1),
                                   jnp.concatenate([att_c, pool_c], axis=1)], axis=0)
            w_ab_out = weights_now("ab_out", cat)[0]
            y2 = _matmul(cat, w_ab_out, "nn", BF16, "mix_ab_out")
        else:
            w_lru_in_t = weights_now("lru_in", xin2)[0]
            p = _matmul(xin2, w_lru_in_t, "nt", BF16, "lru_in")
            z_l, z_c, st["h_lat"], st["h_ctx"] = _lru_fwd(cfg, p, lru_consts, "lru_fwd")
            cat = jnp.concatenate([z_l, z_c], axis=0)
            w_lru_out = weights_now("lru_out", cat)[0]
            y2 = _matmul(cat, w_lru_out, "nn", BF16, "lru_out")
        h2, xhat2, rstd2, xin3 = _ln_fwd(cfg, h1, y2, mods[l], 5, 1.0, ln_g_f[l, 1][None], ln_b_f[l, 1][None],
                                          mods[l], (6, 7), "ln_fwd_b")
        wf[l][1] = weights_now("ffn%d1" % l, xin3)
        g3, u3, y3 = _ffn_fwd(xin3, wf[l][1], "ffn_fwd")
        if l == 0:
            h3, xhat3, rstd3, xin = _ln_fwd(cfg, h2, y3, mods[l], 8, 0.5, ln_g_f[l, 2][None], ln_b_f[l, 2][None],
                                            mods[1], (0, 1), "ln_fwd_a")
        else:
            h3, xhat3, rstd3 = _ln_fwd(cfg, h2, y3, mods[l], 8, 0.5, ln_g_f[l, 2][None], ln_b_f[l, 2][None],
                                       None, None, "ln_fwd_last")
        st.update(p=p, cat=cat, y2=y2, h2=h2, xhat2=xhat2, rstd2=rstd2, xin3=xin3, g3=g3, u3=u3, y3=y3,
                  xhat3=xhat3, rstd3=rstd3)
        saved.append(st)
        h = h3

    dy, loss_tile = _loss(cfg, h, loss_target.reshape(cfg.t_lat, D), "loss")
    loss = lax.psum(loss_tile[0, 0], ("x", "y", "c"))

    grads = {}
    dmod = [None, None]
    recv_ffn = [[None, None], [None, None]]
    dln_g = [[None] * 3, [None] * 3]
    dln_b = [[None] * 3, [None] * 3]

    def ffn_weight_grads(tag, xin_b, dg, du, a_act, dys):
        handles = []
        for k, (lhs, rhs) in enumerate(((dg, xin_b), (du, xin_b), (a_act, dys))):
            part = _matmul(lhs, rhs, "tn", BF16, "ffn_dw", bm_cap=1408, bk_cap=2304)[None]
            handles.append(_exchange_start(part, "exchange_start_ffn%s_%d" % (tag, k)))
        return handles

    def pin(handles):
        total = handles[0][4][0, 0]
        for hd in handles[1:]:
            total = total + hd[4][0, 0]
        return total

    up = (dy,)
    dmod_next = None
    last_sent = None
    for l in (1, 0):
        st = saved[l]
        dm = [None] * N_MOD

        def put_stats(stats, gate_idx, nxt):
            dm[gate_idx] = stats[:, 2, :]
            if nxt is not None:
                nxt[0][nxt[1]] = stats[:, 4, :]
                nxt[0][nxt[1] + 1] = stats[:, 3, :]

        lng3 = ln_g_f[l, 2][None] if last_sent is None else ln_g_f[l, 2][None] + pin(last_sent)
        if len(up) > 1:
            up = (up[0], up[1], ln_b_f[l, 2][None], up[3], up[4])
        dres, dys, stats = _ln_bwd(cfg, up, st["xhat3"], st["rstd3"], st["y3"], mods[l], 8, 0.5,
                                   lng3, "ln_bwd_fused" if len(up) > 1 else "ln_bwd_last")
        put_stats(stats, 8, None if len(up) == 1 else (dmod_next, 0))
        dln_g[l][2], dln_b[l][2] = stats[:, 0, :].sum(0), stats[:, 1, :].sum(0)
        dg, du, a_act, dxin = _ffn_bwd(dys, st["g3"], st["u3"], wf[l][1], "ffn_bwd")
        recv_ffn[l][1] = ffn_weight_grads("%d1" % l, st["xin3"], dg, du, a_act, dys)
        dres, dys, stats = _ln_bwd(cfg, (dres, dxin, ln_b_f[l, 1][None], mods[l], 7), st["xhat2"], st["rstd2"], st["y2"],
                                   mods[l], 5, 1.0, ln_g_f[l, 1][None] + pin(recv_ffn[l][1]), "ln_bwd_fused")
        put_stats(stats, 5, (dm, 6))
        dln_g[l][1], dln_b[l][1] = stats[:, 0, :].sum(0), stats[:, 1, :].sum(0)
        if l == 0:
            dw_out = _matmul(st["cat"], dys, "tn", BF16, "mix_ab_dw_out")
            dcat = _matmul(dys, w_ab_out, "nt", BF16, "mix_ab_dcat")
            dq, dk, dv, dqc, dkc, dvc, dsink = _attn_bwd(cfg, st["p"], dcat, cos, sin, sink_rows, "attn_bwd")
            du_l, dpw_l, dps_l = _pool_bwd(st["p"], pool_w[0], pool_scale, dcat, n_lat, 0, 2, "pool_bwd_lat")
            du_c, dpw_c, dps_c = _pool_bwd(st["p"], pool_w[0], pool_scale, dcat, n_ctx, cfg.ctx_blk, 2, "pool_bwd_ctx")
            dp = jnp.concatenate([jnp.concatenate([dq, dk, dv, du_l], axis=1),
                                  jnp.concatenate([dqc, dkc, dvc, du_c], axis=1)], axis=0)
            dw_in_t = _matmul(dp, st["xin2"], "tn", BF16, "mix_ab_dw_in", bm_cap=1280)
            dxin = _matmul(dp, w_ab_in_t, "nn", BF16, "mix_ab_dx")
            recv_mix = [_exchange_start(part, "exchange_start_mix_ab_%d" % k)
                        for k, part in enumerate((dw_in_t[None], dw_out[None], _as2d(dpw_l + dpw_c)[None]))]
            grads["attn_sink"] = (dsink[0, :, 0] + dsink[1, :, 0])[None, :]
            grads["pool_scale"] = dps_l + dps_c
        else:
            dw_out = _matmul(st["cat"], dys, "tn", BF16, "lru_dw_out")
            dz = _matmul(dys, w_lru_out, "nt", BF16, "lru_dz")
            dgl, dul, dgc, duc, dwa, dwx, vec = _lru_bwd(cfg, st["p"], dz, st["h_lat"], st["h_ctx"], lru_consts, "lru_bwd")
            dp = jnp.concatenate([jnp.concatenate([dgl, dul], axis=1), jnp.concatenate([dgc, duc], axis=1)], axis=0)
            dw_in_t = _matmul(dp, st["xin2"], "tn", BF16, "lru_dw_in", bm_cap=1024)
            dxin = _matmul(dp, w_lru_in_t, "nn", BF16, "lru_dx")
            recv_mix = [_exchange_start(part, "exchange_start_lru_%d" % k)
                        for k, part in enumerate((dw_in_t[None], dw_out[None], _as2d(dwa)[None], _as2d(dwx)[None]))]
            vec_t = jnp.moveaxis(vec, 0, 1).reshape(16, D)
            grads["lru_ba"], grads["lru_bx"] = vec_t[0:2], vec_t[2:4]
            grads["lru_lambda"], grads["lru_conv_w"], grads["lru_conv_b"] = vec_t[4:6], vec_t[6:10], vec_t[10:11]
        if l == 0:
            recv_ab = recv_mix
        else:
            recv_lru = recv_mix
        dres, dys, stats = _ln_bwd(cfg, (dres, dxin, ln_b_f[l, 0][None], mods[l], 4), st["xhat1"], st["rstd1"], st["y1"],
                                   mods[l], 2, 0.5, ln_g_f[l, 0][None] + pin(recv_mix), "ln_bwd_fused")
        put_stats(stats, 2, (dm, 3))
        dln_g[l][0], dln_b[l][0] = stats[:, 0, :].sum(0), stats[:, 1, :].sum(0)
        dg, du, a_act, dxin = _ffn_bwd(dys, st["g1"], st["u1"], wf[l][0], "ffn_bwd")
        recv_ffn[l][0] = ffn_weight_grads("%d0" % l, st["xin1"], dg, du, a_act, dys)
        last_sent = recv_ffn[l][0]
        dmod[l] = dm
        dmod_next = dm
        up = (dres, dxin, None, mods[l], 1)
    dh0, stats = _modulate_bwd(cfg, up[0], up[1], h0, mods[0] + pin(last_sent), 1, "modulate_bwd")
    dmod[0][0], dmod[0][1] = stats[:, 4, :], stats[:, 3, :]
    grad_x = dh0.reshape(x.shape)

    dmod_mine = jnp.stack([jnp.stack(dmod[l], axis=1).reshape(3, N_MOD * D) for l in range(2)])
    n_dm = 6 * N_MOD * D // 128
    dmod_sent = _gather_start(dmod_mine.reshape(1, n_dm, 128), "gather_start_dmod")

    def arrived(handle, name):
        return _exchange_wait(handle, dmod_sent[4], name)

    recv_ffn = [[[arrived(hd, "exchange_wait_ffn%d%d_%d" % (l, i, k)) for k, hd in enumerate(recv_ffn[l][i])]
                 for i in range(2)] for l in range(2)]
    recv_ab = [arrived(hd, "exchange_wait_mix_ab_%d" % k) for k, hd in enumerate(recv_ab)]
    recv_lru = [arrived(hd, "exchange_wait_lru_%d" % k) for k, hd in enumerate(recv_lru)]

    def shard_sum(recv, name):
        return _sum_blocks(recv.reshape(N_DEV, recv.shape[2], recv.shape[3]), name)

    gate_g = [[None, None], [None, None]]
    up_g = [[None, None], [None, None]]
    down_g = [[None, None], [None, None]]
    for l in range(2):
        for i in range(2):
            gt, ut, dn = [shard_sum(r, "sum_ffn") for r in recv_ffn[l][i]]
            gate_g[l][i], up_g[l][i], down_g[l][i] = gt.T, ut.T, dn
    grads["ffn_w_gate"] = jnp.stack([jnp.stack(gate_g[l]) for l in range(2)])
    grads["ffn_w_up"] = jnp.stack([jnp.stack(up_g[l]) for l in range(2)])
    grads["ffn_w_down"] = jnp.stack([jnp.stack(down_g[l]) for l in range(2)])
    grads["mix_ab_w_in"] = shard_sum(recv_ab[0], "sum_mix_in").T[None]
    grads["mix_ab_w_out"] = shard_sum(recv_ab[1], "sum_mix_out")[None]
    grads["lru_w_in"] = shard_sum(recv_lru[0], "sum_lru_in").T[None]
    grads["lru_w_out"] = shard_sum(recv_lru[1], "sum_lru_out")[None]
    rep_parts = [shard_sum(recv_lru[2], "sum_rep"), shard_sum(recv_lru[3], "sum_rep"), shard_sum(recv_ab[2], "sum_rep")]
    rep_names = ["lru_wa", "lru_wx", "pool_w"]

    dmod_all = _gather_wait(dmod_sent, rep_parts[2], "gather_wait_dmod").reshape(N_DEV, n_dm, 128)
    dmod_sum = _sum_blocks(dmod_all, "sum_dmod").reshape(2, 3, N_MOD * D)
    dmod_all = dmod_all.reshape(N_DEV, 2, 3, N_MOD * D)
    grads["b_mod"] = dmod_sum[:, 0] + dmod_sum[:, 1] + dmod_sum[:, 2]
    dmod_ex = jnp.moveaxis(dmod_all[:, :, 0:2, :], 1, 0).reshape(2, 2 * N_DEV, N_MOD * D)
    dm_rows = jnp.zeros((2, 32, N_MOD * D), F32).at[:, :16].set(dmod_ex).at[:, 16].set(dmod_sum[:, 2])
    dm_cols = lax.dynamic_slice_in_dim(dm_rows, me * mcols, mcols, axis=2).astype(BF16)
    grads["w_mod"] = jnp.stack([_matmul(s_rows, dm_cols[l], "tn", F32, "mod_dw", bn_cap=1280) for l in range(2)])
    ds_part = None
    for l in range(2):
        part = _matmul(dm_cols[l, 16:32], w_mod[l], "nt", F32, "mod_ds", bk_cap=1280)[0]
        ds_part = part if ds_part is None else ds_part + part

    dln_g_f = jnp.stack([jnp.stack(dln_g[l]) for l in range(2)])
    dln_b_f = jnp.stack([jnp.stack(dln_b[l]) for l in range(2)])
    sink_pad = jnp.zeros((1, 128), F32).at[0, :8].set(grads["attn_sink"][0])
    part_list = [p_.reshape(-1, 128) for p_ in rep_parts] + [
        dln_g_f.reshape(-1, 128), dln_b_f.reshape(-1, 128), grads["lru_conv_w"].reshape(-1, 128),
        grads["lru_conv_b"].reshape(-1, 128), grads["lru_ba"].reshape(-1, 128), grads["lru_bx"].reshape(-1, 128),
        grads["lru_lambda"].reshape(-1, 128), ds_part.reshape(-1, 128), sink_pad, grads["pool_scale"].reshape(-1, 128)]
    parts, part_off = _pack_rows(part_list)
    parts_sent = _gather_start(parts[None], "gather_start_partials")

    delta, new_m, new_v = {}, {}, {}
    for n in ("w_mod", "b_mod", "ffn_w_gate", "ffn_w_up", "ffn_w_down", "mix_ab_w_in", "mix_ab_w_out",
              "lru_w_in", "lru_w_out"):
        grads[n] = grads[n].reshape(weights[n].shape)
        delta[n], new_m[n], new_v[n] = _adamw(weights[n], grads[n], mom_m[n], mom_v[n], "adamw", dep=parts_sent[4])
    parts_all = _gather_wait(parts_sent, delta["lru_w_out"], "gather_wait_partials").reshape(N_DEV, parts.shape[0], 128)
    parts_sum = _sum_blocks(parts_all, "sum_partials")

    for i, n in enumerate(rep_names):
        rows = part_list[i].shape[0]
        grads[n] = parts_all[:, part_off[i]:part_off[i] + rows, :].reshape(weights[n].shape)

    def take(idx):
        return parts_sum[part_off[idx]:part_off[idx] + part_list[idx].shape[0]]

    def my_cols(full, shp):
        w = shp[-1]
        return lax.dynamic_slice_in_dim(full, me * w, w, axis=full.ndim - 1)

    grads["ln_g"] = my_cols(take(3).reshape(2, 3, D), ln_g.shape)
    grads["ln_b"] = my_cols(take(4).reshape(2, 3, D), ln_b.shape)
    grads["lru_conv_w"] = my_cols(take(5).reshape(1, 4, D), lru_conv_w.shape)
    grads["lru_conv_b"] = my_cols(take(6).reshape(1, D), lru_conv_b.shape)
    grads["lru_ba"] = my_cols(take(7).reshape(1, 2, D), lru_ba.shape)
    grads["lru_bx"] = my_cols(take(8).reshape(1, 2, D), lru_bx.shape)
    grads["lru_lambda"] = my_cols(take(9).reshape(1, 2, D), lru_lambda.shape)
    sg = jax.nn.sigmoid(c_ctx)
    grads["c_ctx"] = take(10).reshape(D) * (sg * (1.0 + c_ctx * (1.0 - sg)))
    grads["attn_sink"] = take(11)[:, :8]
    grads["pool_scale"] = take(12).reshape(pool_scale.shape)

    for n in names:
        if n in delta:
            continue
        grads[n] = grads[n].reshape(weights[n].shape)
        delta[n], new_m[n], new_v[n] = _adamw(weights[n], grads[n], mom_m[n], mom_v[n], "adamw")

    return (loss, grad_x, *[grads[n] for n in names], *[delta[n] for n in names],
            *[new_m[n] for n in names], *[new_v[n] for n in names])
```

```python
import functools
import math

import jax
import jax.numpy as jnp
from jax import lax
from jax.experimental import pallas as pl
from jax.experimental.pallas import tpu as pltpu

F32 = jnp.float32
BF16 = jnp.bfloat16
MESH = pl.DeviceIdType.MESH

D = 1024
N_MOD = 9
N_DEV = 8
HEAD_DIM = 64
ATT_HEADS = 8
KV_HEADS = 2
ATT_W = 512
BLK = 128
ATT_SCALE = HEAD_DIM ** -0.5
GRID_W = 64
ROPE_FREQS = HEAD_DIM // 4
ROPE_THETA = 10000.0
POOL_R = (1, 2, 4, 8)
LRU_C = 8.0
LN_EPS = 1e-5
NEG_INF = -1e30
ALPHA = 4.0 ** 0.25
LR, B1, B2, EPS, WD, STEP = 0.001, 0.9, 0.999, 1e-08, 0.01, 10
VMEM_LIMIT = 56 * 1024 * 1024
ROW_TILE = 512


def _params(sem=None):
    if sem is None:
        return pltpu.CompilerParams(vmem_limit_bytes=VMEM_LIMIT)
    return pltpu.CompilerParams(dimension_semantics=sem, vmem_limit_bytes=VMEM_LIMIT)


def _sigmoid(x):
    return 0.5 * jnp.tanh(0.5 * x) + 0.5


def _dot(a, b):
    return jnp.dot(a.astype(BF16), b.astype(BF16), preferred_element_type=F32)


def _dot_nt(a, b):
    return lax.dot_general(a.astype(BF16), b.astype(BF16), (((1,), (1,)), ((), ())), preferred_element_type=F32)


def _dot_tn(a, b):
    return lax.dot_general(a.astype(BF16), b.astype(BF16), (((0,), (0,)), ((), ())), preferred_element_type=F32)


def _pick(n, cap):
    best = None
    for m in range(128, min(n, cap) + 1, 128):
        if n % m == 0:
            best = m
    return n if best is None else best


def _chunks(width, step=256):
    out, c = [], 0
    while c < width:
        w = min(step, width - c)
        out.append((c, w))
        c += w
    return out


class _Cfg:
    def __init__(self, n_lat, n_ctx):
        self.n_lat, self.n_ctx = n_lat, n_ctx
        self.t_lat, self.t_ctx = 2 * n_lat, 2 * n_ctx
        self.T = self.t_lat + self.t_ctx
        self.tm = min(ROW_TILE, self.t_ctx)
        assert n_lat % self.tm == 0 and self.t_ctx % self.tm == 0 and n_lat >= 3 * BLK and n_ctx % BLK == 0
        self.nt = self.T // self.tm
        self.nlt = n_lat // self.tm
        self.ctx_blk = self.t_lat // n_ctx

    def seg(self, i):
        return jnp.minimum(i // self.nlt, 2)

    def first_of_seg(self, i):
        return jnp.where(i < 2 * self.nlt, i % self.nlt == 0, i == 2 * self.nlt)


def _modulate(cfg, h, mod, shift_idx, scale_idx, name):
    tm = cfg.tm

    def body(h_ref, mod_ref, o_ref):
        sh = mod_ref[shift_idx:shift_idx + 1, :]
        sc = mod_ref[scale_idx:scale_idx + 1, :]
        o_ref[...] = (h_ref[...] * (1.0 + sc) + sh).astype(BF16)

    return pl.pallas_call(
        body, grid=(cfg.nt,), name=name,
        in_specs=[pl.BlockSpec((tm, D), lambda i: (i, 0)),
                  pl.BlockSpec((None, N_MOD, D), lambda i: (cfg.seg(i), 0, 0))],
        out_specs=pl.BlockSpec((tm, D), lambda i: (i, 0)),
        out_shape=jax.ShapeDtypeStruct((cfg.T, D), BF16),
        compiler_params=_params(("parallel",)),
    )(h, mod)


def _ln_fwd(cfg, h, y, mod, gate_idx, coef, lng, lnb, mod_next, next_idx, name):
    tm = cfg.tm
    has_next = next_idx is not None

    def body(*refs):
        if has_next:
            h_ref, y_ref, mod_ref, g_ref, b_ref, modn_ref, hn_ref, xhat_ref, rstd_ref, xin_ref = refs
        else:
            h_ref, y_ref, mod_ref, g_ref, b_ref, hn_ref, xhat_ref, rstd_ref = refs
        gate = mod_ref[gate_idx:gate_idx + 1, :]
        z = ALPHA * h_ref[...] + (coef * gate) * y_ref[...].astype(F32)
        mu = jnp.mean(z, axis=-1, keepdims=True)
        zc = z - mu
        var = jnp.mean(zc * zc, axis=-1, keepdims=True)
        rstd = lax.rsqrt(var + LN_EPS)
        xhat = zc * rstd
        hn = xhat * g_ref[...] + b_ref[...]
        hn_ref[...] = hn
        xhat_ref[...] = xhat.astype(BF16)
        rstd_ref[...] = rstd
        if has_next:
            sh = modn_ref[next_idx[0]:next_idx[0] + 1, :]
            sc = modn_ref[next_idx[1]:next_idx[1] + 1, :]
            xin_ref[...] = (hn * (1.0 + sc) + sh).astype(BF16)

    row = pl.BlockSpec((tm, D), lambda i: (i, 0))
    modspec = pl.BlockSpec((None, N_MOD, D), lambda i: (cfg.seg(i), 0, 0))
    vec = pl.BlockSpec((1, D), lambda i: (0, 0))
    in_specs = [row, row, modspec, vec, vec]
    args = [h, y, mod, lng, lnb]
    out_specs = [row, row, pl.BlockSpec((tm, 1), lambda i: (i, 0))]
    out_shape = [jax.ShapeDtypeStruct((cfg.T, D), F32), jax.ShapeDtypeStruct((cfg.T, D), BF16),
                 jax.ShapeDtypeStruct((cfg.T, 1), F32)]
    if has_next:
        in_specs.append(modspec)
        args.append(mod_next)
        out_specs.append(row)
        out_shape.append(jax.ShapeDtypeStruct((cfg.T, D), BF16))
    return pl.pallas_call(body, grid=(cfg.nt,), name=name, in_specs=in_specs, out_specs=out_specs,
                          out_shape=out_shape, compiler_params=_params(("parallel",)))(*args)


def _ln_bwd(cfg, up, xhat, rstd, y, mod, gate_idx, coef, lng, name):
    tm = cfg.tm
    fused = len(up) > 1
    scale_next = up[4] if fused else None

    def body(*refs):
        if fused:
            dres_n, dxin_n, b_ref, modn_ref, xhat_ref, rstd_ref, y_ref, mod_ref, g_ref, dres_ref, dys_ref, st_ref = refs
        else:
            dhn_ref, xhat_ref, rstd_ref, y_ref, mod_ref, g_ref, dres_ref, dys_ref, st_ref = refs
        i = pl.program_id(0)

        @pl.when(cfg.first_of_seg(i))
        def _():
            st_ref[...] = jnp.zeros_like(st_ref)

        xhat = xhat_ref[...].astype(F32)
        if fused:
            dxin = dxin_n[...].astype(F32)
            sc = modn_ref[scale_next:scale_next + 1, :]
            dhn = dres_n[...] + dxin * (1.0 + sc)
            shift_sum = jnp.sum(dxin, axis=0, keepdims=True)
            st_ref[3:4, :] += g_ref[...] * jnp.sum(dxin * xhat, axis=0, keepdims=True) + b_ref[...] * shift_sum
            st_ref[4:5, :] += shift_sum
        else:
            dhn = dhn_ref[...]
        gdh = dhn * g_ref[...]
        m1 = jnp.mean(gdh, axis=-1, keepdims=True)
        m2 = jnp.mean(gdh * xhat, axis=-1, keepdims=True)
        dz = rstd_ref[...] * (gdh - m1 - xhat * m2)
        gate = mod_ref[gate_idx:gate_idx + 1, :]
        dres_ref[...] = ALPHA * dz
        dys_ref[...] = ((coef * gate) * dz).astype(BF16)
        st_ref[0:1, :] += jnp.sum(dhn * xhat, axis=0, keepdims=True)
        st_ref[1:2, :] += jnp.sum(dhn, axis=0, keepdims=True)
        st_ref[2:3, :] += jnp.sum((coef * dz) * y_ref[...].astype(F32), axis=0, keepdims=True)

    row = pl.BlockSpec((tm, D), lambda i: (i, 0))
    modspec = pl.BlockSpec((None, N_MOD, D), lambda i: (cfg.seg(i), 0, 0))
    vec = pl.BlockSpec((1, D), lambda i: (0, 0))
    col = pl.BlockSpec((tm, 1), lambda i: (i, 0))
    if fused:
        in_specs = [row, row, vec, modspec, row, col, row, modspec, vec]
        args = [up[0], up[1], up[2], up[3], xhat, rstd, y, mod, lng]
    else:
        in_specs = [row, row, col, row, modspec, vec]
        args = [up[0], xhat, rstd, y, mod, lng]
    return pl.pallas_call(
        body, grid=(cfg.nt,), name=name, in_specs=in_specs,
        out_specs=[row, row, pl.BlockSpec((None, 8, D), lambda i: (cfg.seg(i), 0, 0))],
        out_shape=[jax.ShapeDtypeStruct((cfg.T, D), F32), jax.ShapeDtypeStruct((cfg.T, D), BF16),
                   jax.ShapeDtypeStruct((3, 8, D), F32)],
        compiler_params=_params(("arbitrary",)))(*args)


def _modulate_bwd(cfg, dres, dxin, h, mod, scale_idx, name):
    tm = cfg.tm
    n_lt = 2 * cfg.nlt

    def body(dres_ref, dxin_ref, h_ref, mod_ref, dh_ref, st_ref):
        i = pl.program_id(0)

        @pl.when(cfg.first_of_seg(i))
        def _():
            st_ref[...] = jnp.zeros_like(st_ref)

        dxin = dxin_ref[...].astype(F32)
        sc = mod_ref[scale_idx:scale_idx + 1, :]

        @pl.when(i < n_lt)
        def _():
            dh_ref[...] = dres_ref[...] + dxin * (1.0 + sc)

        st_ref[3:4, :] += jnp.sum(dxin * h_ref[...], axis=0, keepdims=True)
        st_ref[4:5, :] += jnp.sum(dxin, axis=0, keepdims=True)

    row = pl.BlockSpec((tm, D), lambda i: (i, 0))
    return pl.pallas_call(
        body, grid=(cfg.nt,), name=name,
        in_specs=[row, row, row, pl.BlockSpec((None, N_MOD, D), lambda i: (cfg.seg(i), 0, 0))],
        out_specs=[pl.BlockSpec((tm, D), lambda i: (jnp.minimum(i, n_lt - 1), 0)),
                   pl.BlockSpec((None, 8, D), lambda i: (cfg.seg(i), 0, 0))],
        out_shape=[jax.ShapeDtypeStruct((cfg.t_lat, D), F32), jax.ShapeDtypeStruct((3, 8, D), F32)],
        compiler_params=_params(("arbitrary",)))(dres, dxin, h, mod)


def _loss(cfg, h, target, name):
    tm = cfg.tm
    n_lt = 2 * cfg.nlt

    def body(h_ref, t_ref, dy_ref, l_ref):
        i = pl.program_id(0)

        @pl.when(i == 0)
        def _():
            l_ref[...] = jnp.zeros_like(l_ref)

        @pl.when(i < n_lt)
        def _():
            err = h_ref[...] - t_ref[...]
            dy_ref[...] = err * (1.0 / D)
            part = jnp.sum(jnp.sum(err * err, axis=1, keepdims=True), axis=0, keepdims=True) * (0.5 / D)
            l_ref[...] += jnp.broadcast_to(part, l_ref.shape)

        @pl.when(i >= n_lt)
        def _():
            dy_ref[...] = jnp.zeros_like(dy_ref)

    return pl.pallas_call(
        body, grid=(cfg.nt,), name=name,
        in_specs=[pl.BlockSpec((tm, D), lambda i: (i, 0)),
                  pl.BlockSpec((tm, D), lambda i: (jnp.minimum(i, n_lt - 1), 0))],
        out_specs=[pl.BlockSpec((tm, D), lambda i: (i, 0)), pl.BlockSpec((8, 128), lambda i: (0, 0))],
        out_shape=[jax.ShapeDtypeStruct((cfg.T, D), F32), jax.ShapeDtypeStruct((8, 128), F32)],
        compiler_params=_params(("arbitrary",)))(h, target)


def _matmul(a, b, mode, out_dtype, name, bm_cap=1536, bn_cap=1408, bk_cap=1024):
    if mode == "nn":
        (M, K), N = a.shape, b.shape[1]
    elif mode == "nt":
        (M, K), N = a.shape, b.shape[0]
    else:
        (K, M), N = a.shape, b.shape[1]
    bm, bn, bk = _pick(M, bm_cap), _pick(N, bn_cap), _pick(K, bk_cap)
    nk = K // bk

    def body(a_ref, b_ref, o_ref, acc_ref=None):
        k = pl.program_id(2)
        if mode == "nn":
            part = _dot(a_ref[...], b_ref[...])
        elif mode == "nt":
            part = _dot_nt(a_ref[...], b_ref[...])
        else:
            part = _dot_tn(a_ref[...], b_ref[...])
        if nk == 1:
            o_ref[...] = part.astype(out_dtype)
            return

        @pl.when(k == 0)
        def _():
            acc_ref[...] = part

        @pl.when((k > 0) & (k < nk - 1))
        def _():
            acc_ref[...] += part

        @pl.when(k == nk - 1)
        def _():
            o_ref[...] = (acc_ref[...] + part).astype(out_dtype)

    if mode == "nn":
        a_spec = pl.BlockSpec((bm, bk), lambda i, j, k: (i, k))
        b_spec = pl.BlockSpec((bk, bn), lambda i, j, k: (k, j))
    elif mode == "nt":
        a_spec = pl.BlockSpec((bm, bk), lambda i, j, k: (i, k))
        b_spec = pl.BlockSpec((bn, bk), lambda i, j, k: (j, k))
    else:
        a_spec = pl.BlockSpec((bk, bm), lambda i, j, k: (k, i))
        b_spec = pl.BlockSpec((bk, bn), lambda i, j, k: (k, j))
    return pl.pallas_call(
        body, grid=(M // bm, N // bn, nk), name=name, in_specs=[a_spec, b_spec],
        out_specs=pl.BlockSpec((bm, bn), lambda i, j, k: (i, j)),
        out_shape=jax.ShapeDtypeStruct((M, N), out_dtype),
        scratch_shapes=[pltpu.VMEM((bm, bn), F32)] if nk > 1 else [],
        compiler_params=_params(("parallel", "parallel", "arbitrary")))(a, b)


def _ffn_tile(T, cap):
    best = 256
    for t in range(256, cap + 1, 256):
        if T % t == 0:
            best = t
    return best


def _ffn_fwd(xin, wf, name):
    T = xin.shape[0]
    F = wf.shape[1]
    tm, tf = _ffn_tile(T, 768), F // 2
    assert tf % 128 == 0 and T % tm == 0

    def body(x_ref, wg_ref, wu_ref, wd_ref, g_ref, u_ref, y_ref, acc_ref):
        j = pl.program_id(1)
        x = x_ref[...]
        acc = None
        for c0, cw in _chunks(tf):
            g = _dot_nt(x, wg_ref[c0:c0 + cw, :])
            u = _dot_nt(x, wu_ref[c0:c0 + cw, :])
            g_ref[:, c0:c0 + cw] = g.astype(BF16)
            u_ref[:, c0:c0 + cw] = u.astype(BF16)
            part = _dot(g * _sigmoid(g) * u, wd_ref[c0:c0 + cw, :])
            acc = part if acc is None else acc + part

        @pl.when(j == 0)
        def _():
            acc_ref[...] = acc

        @pl.when(j == 1)
        def _():
            y_ref[...] = (acc_ref[...] + acc).astype(BF16)

    return pl.pallas_call(
        body, grid=(T // tm, 2), name=name,
        in_specs=[pl.BlockSpec((tm, D), lambda i, j: (i, 0)),
                  pl.BlockSpec((None, tf, D), lambda i, j: (0, j, 0)),
                  pl.BlockSpec((None, tf, D), lambda i, j: (1, j, 0)),
                  pl.BlockSpec((None, tf, D), lambda i, j: (2, j, 0))],
        out_specs=[pl.BlockSpec((tm, tf), lambda i, j: (i, j)),
                   pl.BlockSpec((tm, tf), lambda i, j: (i, j)),
                   pl.BlockSpec((tm, D), lambda i, j: (i, 0))],
        out_shape=[jax.ShapeDtypeStruct((T, F), BF16), jax.ShapeDtypeStruct((T, F), BF16),
                   jax.ShapeDtypeStruct((T, D), BF16)],
        scratch_shapes=[pltpu.VMEM((tm, D), F32)],
        compiler_params=_params(("parallel", "arbitrary")))(xin, wf, wf, wf)


def _ffn_bwd(dys, g, u, wf, name):
    T = dys.shape[0]
    F = wf.shape[1]
    tm, tf = _ffn_tile(T, 512), F // 2

    def body(dy_ref, g_ref, u_ref, wg_ref, wu_ref, wd_ref, dg_ref, du_ref, a_ref, dx_ref, acc_ref):
        j = pl.program_id(1)
        da_all = _dot_nt(dy_ref[...], wd_ref[...])
        for c0, cw in _chunks(tf):
            gg = g_ref[:, c0:c0 + cw].astype(F32)
            uu = u_ref[:, c0:c0 + cw].astype(F32)
            da = da_all[:, c0:c0 + cw]
            s = _sigmoid(gg)
            silu = gg * s
            a_ref[:, c0:c0 + cw] = (silu * uu).astype(BF16)
            du_ref[:, c0:c0 + cw] = (da * silu).astype(BF16)
            dg_ref[:, c0:c0 + cw] = (da * uu * (s * (1.0 + gg * (1.0 - s)))).astype(BF16)
        acc = _dot(dg_ref[...], wg_ref[...]) + _dot(du_ref[...], wu_ref[...])

        @pl.when(j == 0)
        def _():
            acc_ref[...] = acc

        @pl.when(j == 1)
        def _():
            dx_ref[...] = (acc_ref[...] + acc).astype(BF16)

    blk = pl.BlockSpec((tm, tf), lambda i, j: (i, j))
    return pl.pallas_call(
        body, grid=(T // tm, 2), name=name,
        in_specs=[pl.BlockSpec((tm, D), lambda i, j: (i, 0)), blk, blk,
                  pl.BlockSpec((None, tf, D), lambda i, j: (0, j, 0)),
                  pl.BlockSpec((None, tf, D), lambda i, j: (1, j, 0)),
                  pl.BlockSpec((None, tf, D), lambda i, j: (2, j, 0))],
        out_specs=[blk, blk, blk, pl.BlockSpec((tm, D), lambda i, j: (i, 0))],
        out_shape=[jax.ShapeDtypeStruct((T, F), BF16), jax.ShapeDtypeStruct((T, F), BF16),
                   jax.ShapeDtypeStruct((T, F), BF16), jax.ShapeDtypeStruct((T, D), BF16)],
        scratch_shapes=[pltpu.VMEM((tm, D), F32)],
        compiler_params=_params(("parallel", "arbitrary")))(dys, g, u, wf, wf, wf)


def _swap_halves(x):
    w = x.shape[1]
    lane = lax.broadcasted_iota(jnp.int32, (1, w), 1)
    return jnp.where((lane & 63) < 32, pltpu.roll(x, w - 32, 1), pltpu.roll(x, 32, 1))


def _rope(x, cos, sin):
    return x * cos + _swap_halves(x) * sin


def _rope_t(dy, cos, sin):
    return dy * cos + _swap_halves(dy * sin)


def _rope_tables(n_lat):
    rows = n_lat // GRID_W
    row = jnp.repeat(jnp.arange(rows, dtype=F32), GRID_W)
    col = jnp.tile(jnp.arange(GRID_W, dtype=F32), rows)
    inv = ROPE_THETA ** (-jnp.arange(ROPE_FREQS, dtype=F32) / ROPE_FREQS)
    ang = jnp.concatenate([row[:, None] * inv, col[:, None] * inv], axis=-1)
    cs, sn = jnp.cos(ang), jnp.sin(ang)
    cos = jnp.concatenate([cs, cs, cs, cs], axis=-1)
    sin = jnp.concatenate([-sn, sn, -sn, sn], axis=-1)
    return cos, sin


def _attn_specs(cfg):
    n_lat, n_ctx, cb = cfg.n_lat, cfg.n_ctx, cfg.ctx_blk
    return [pl.BlockSpec((n_lat, ATT_W), lambda e: (e, 0)),
            pl.BlockSpec((n_lat, 128), lambda e: (e, 4)),
            pl.BlockSpec((n_lat, 128), lambda e: (e, 5)),
            pl.BlockSpec((n_ctx, ATT_W), lambda e: (cb + e, 0)),
            pl.BlockSpec((n_ctx, 128), lambda e: (cb + e, 4)),
            pl.BlockSpec((n_ctx, 128), lambda e: (cb + e, 5)),
            pl.BlockSpec((n_lat, 128), lambda e: (0, 0)),
            pl.BlockSpec((n_lat, 128), lambda e: (0, 0)),
            pl.BlockSpec((8, 128), lambda e: (0, 0))]


def _attn_prepare(kh, kl, vl, kc, vc, ka, kb, va, vb, kca, kcb, vca, vcb):
    lane = lax.broadcasted_iota(jnp.int32, (1, 128), 1)
    own = (lane < 64) if kh == 0 else (lane >= 64)

    def split(x, ra, rb):
        mine = jnp.where(own, x, 0.0)
        other = pltpu.roll(mine, 64, 1)
        a, b = (mine, other) if kh == 0 else (other, mine)
        ra[...] = a.astype(BF16)
        rb[...] = b.astype(BF16)

    split(kl, ka, kb)
    split(vl, va, vb)
    split(kc, kca, kcb)
    split(vc, vca, vcb)


def _softmax_parts(s_list, sk):
    m = sk
    for s in s_list:
        m = jnp.maximum(m, jnp.max(s, axis=1, keepdims=True))
    es = [jnp.exp(s - m) for s in s_list]
    esk = jnp.exp(sk - m)
    den = esk
    for e in es:
        den = den + jnp.sum(e, axis=1, keepdims=True)
    inv = 1.0 / den
    return [e * inv for e in es], esk * inv


def _window(cfg, n):
    r0 = pl.multiple_of(n * BLK, BLK)
    start = pl.multiple_of(jnp.clip((n - 1) * BLK, 0, cfg.n_lat - 3 * BLK), BLK)
    qpos = r0 + lax.broadcasted_iota(jnp.int32, (BLK, 1), 0)
    kpos = start + lax.broadcasted_iota(jnp.int32, (1, 3 * BLK), 1)
    valid = jnp.abs(qpos - kpos) <= BLK
    return r0, start, valid


def _attn_fwd(cfg, p, cos, sin, sink_rows, name):
    n_lat, n_ctx = cfg.n_lat, cfg.n_ctx

    def body(q_ref, k_ref, v_ref, qc_ref, kc_ref, vc_ref, cos_ref, sin_ref, sink_ref, o_ref, oc_ref,
             qr, ka, kb, va, vb, kca, kcb, vca, vcb):
        cos_t, sin_t = cos_ref[...], sin_ref[...]
        for gq in range(4):
            qr[:, gq * 128:(gq + 1) * 128] = _rope(q_ref[:, gq * 128:(gq + 1) * 128].astype(F32), cos_t, sin_t).astype(BF16)
        kl = _rope(k_ref[...].astype(F32), cos_t, sin_t)
        for kh in range(KV_HEADS):
            _attn_prepare(kh, kl, v_ref[...].astype(F32), kc_ref[...].astype(F32), vc_ref[...].astype(F32),
                          ka, kb, va, vb, kca, kcb, vca, vcb)

            def lat_block(n, carry):
                r0, start, valid = _window(cfg, n)
                win = pl.ds(start, 3 * BLK)
                lanes = [slice((kh * 2 + pr) * 128, (kh * 2 + pr + 1) * 128) for pr in range(2)]
                qps = [qr[pl.ds(r0, BLK), lanes[pr]] for pr in range(2)]
                kws, kcs = (ka[win, :], kb[win, :]), (kca[...], kcb[...])
                scores = [(jnp.where(valid, _dot_nt(qps[pr], kws[half]) * ATT_SCALE, NEG_INF),
                           _dot_nt(qps[pr], kcs[half]) * ATT_SCALE) for pr in range(2) for half in range(2)]
                probs = []
                for idx, (s_w, s_c) in enumerate(scores):
                    head = kh * 4 + idx
                    (p_w, p_c), _ = _softmax_parts([s_w, s_c], sink_ref[head:head + 1, 0:1])
                    probs.append((p_w.astype(BF16), p_c.astype(BF16)))
                vws, vcs = (va[win, :], vb[win, :]), (vca[...], vcb[...])
                for pr in range(2):
                    o = (_dot(probs[2 * pr][0], vws[0]) + _dot(probs[2 * pr][1], vcs[0])
                         + _dot(probs[2 * pr + 1][0], vws[1]) + _dot(probs[2 * pr + 1][1], vcs[1]))
                    o_ref[pl.ds(r0, BLK), lanes[pr]] = o.astype(BF16)
                return carry

            lax.fori_loop(0, n_lat // BLK, lat_block, 0)
            for n in range(n_ctx // BLK):
                rows = slice(n * BLK, (n + 1) * BLK)
                for pr in range(2):
                    lanes = slice((kh * 2 + pr) * 128, (kh * 2 + pr + 1) * 128)
                    qp = qc_ref[rows, lanes]
                    o = None
                    for half, (kcx, vcx) in enumerate(((kca, vca), (kcb, vcb))):
                        head = kh * 4 + pr * 2 + half
                        s_c = _dot_nt(qp, kcx[...]) * ATT_SCALE
                        (p_c,), _ = _softmax_parts([s_c], sink_ref[head:head + 1, 0:1])
                        part = _dot(p_c, vcx[...])
                        o = part if o is None else o + part
                    oc_ref[rows, lanes] = o.astype(BF16)

    return pl.pallas_call(
        body, grid=(2,), name=name, in_specs=_attn_specs(cfg),
        out_specs=[pl.BlockSpec((n_lat, ATT_W), lambda e: (e, 0)), pl.BlockSpec((n_ctx, ATT_W), lambda e: (e, 0))],
        out_shape=[jax.ShapeDtypeStruct((cfg.t_lat, ATT_W), BF16), jax.ShapeDtypeStruct((cfg.t_ctx, ATT_W), BF16)],
        scratch_shapes=[pltpu.VMEM((n_lat, ATT_W), BF16)] + [pltpu.VMEM((n_lat, 128), BF16)] * 4
        + [pltpu.VMEM((n_ctx, 128), BF16)] * 4,
        compiler_params=_params(("parallel",)))(p, p, p, p, p, p, cos, sin, sink_rows)


def _attn_bwd(cfg, p, dcat, cos, sin, sink_rows, name):
    n_lat, n_ctx, cb = cfg.n_lat, cfg.n_ctx, cfg.ctx_blk

    def body(q_ref, k_ref, v_ref, qc_ref, kc_ref, vc_ref, cos_ref, sin_ref, sink_ref, do_ref, doc_ref,
             dq_ref, dk_ref, dv_ref, dqc_ref, dkc_ref, dvc_ref, dsink_ref,
             qr, ka, kb, va, vb, kca, kcb, vca, vcb, dqs, dka, dva, dkca, dvca):
        cos_t, sin_t = cos_ref[...], sin_ref[...]
        lane = lax.broadcasted_iota(jnp.int32, (1, 128), 1)
        lo = lane < 64
        for gq in range(4):
            qr[:, gq * 128:(gq + 1) * 128] = _rope(q_ref[:, gq * 128:(gq + 1) * 128].astype(F32), cos_t, sin_t).astype(BF16)
        kl = _rope(k_ref[...].astype(F32), cos_t, sin_t)
        dsink_ref[...] = jnp.zeros_like(dsink_ref)
        dka[...] = jnp.zeros_like(dka)
        dva[...] = jnp.zeros_like(dva)
        dkca[...] = jnp.zeros_like(dkca)
        dvca[...] = jnp.zeros_like(dvca)

        def halves(x):
            return jnp.where(lo, x, 0).astype(BF16), jnp.where(lo, 0, x).astype(BF16)

        for kh in range(KV_HEADS):
            _attn_prepare(kh, kl, v_ref[...].astype(F32), kc_ref[...].astype(F32), vc_ref[...].astype(F32),
                          ka, kb, va, vb, kca, kcb, vca, vcb)

            def one_head(head, qp, q_half, do_p, do_half, kw, kcx, vw, vcx, win, valid):
                sk = sink_ref[head:head + 1, 0:1]
                s_list = [_dot_nt(qp, kcx[...]) * ATT_SCALE]
                if win is not None:
                    s_list.insert(0, jnp.where(valid, _dot_nt(qp, kw[win, :]) * ATT_SCALE, NEG_INF))
                probs, p_sink = _softmax_parts(s_list, sk)
                vals = [vcx[...]] if win is None else [vw[win, :], vcx[...]]
                dps = [_dot_nt(do_p, vv) for vv in vals]
                dr = None
                for pp, dp in zip(probs, dps):
                    t = jnp.sum(pp * dp, axis=1, keepdims=True)
                    dr = t if dr is None else dr + t
                dss = [(pp * (dp - dr) * ATT_SCALE).astype(BF16) for pp, dp in zip(probs, dps)]
                dsink_ref[head:head + 1, :] += jnp.broadcast_to(
                    jnp.sum(-p_sink * dr, axis=0, keepdims=True), (1, 128))
                p_c, ds_c = probs[-1], dss[-1]
                dq = _dot(ds_c, kcx[...])
                dkca[kh] += _dot_tn(ds_c, q_half)
                dvca[kh] += _dot_tn(p_c, do_half)
                if win is not None:
                    dq = dq + _dot(dss[0], kw[win, :])
                    dka[kh, win, :] += _dot_tn(dss[0], q_half)
                    dva[kh, win, :] += _dot_tn(probs[0], do_half)
                return dq

            def lat_block(n, carry):
                r0, start, valid = _window(cfg, n)
                win = pl.ds(start, 3 * BLK)
                lanes = [slice((kh * 2 + pr) * 128, (kh * 2 + pr + 1) * 128) for pr in range(2)]
                qps = [qr[pl.ds(r0, BLK), lanes[pr]] for pr in range(2)]
                dops = [do_ref[pl.ds(r0, BLK), lanes[pr]].astype(BF16) for pr in range(2)]
                heads = [(pr, half) for pr in range(2) for half in range(2)]
                kws, kcs = (ka[win, :], kb[win, :]), (kca[...], kcb[...])
                vws, vcs = (va[win, :], vb[win, :]), (vca[...], vcb[...])
                soft = []
                for idx, (pr, half) in enumerate(heads):
                    s_w = jnp.where(valid, _dot_nt(qps[pr], kws[half]) * ATT_SCALE, NEG_INF)
                    s_c = _dot_nt(qps[pr], kcs[half]) * ATT_SCALE
                    soft.append(_softmax_parts([s_w, s_c], sink_ref[kh * 4 + idx:kh * 4 + idx + 1, 0:1]))
                dps = [(_dot_nt(dops[pr], vws[half]), _dot_nt(dops[pr], vcs[half])) for pr, half in heads]
                ds_w, ds_c, pb_w, pb_c = [], [], [], []
                for idx in range(4):
                    (p_w, p_c), p_sink = soft[idx]
                    dp_w, dp_c = dps[idx]
                    dr = jnp.sum(p_w * dp_w, axis=1, keepdims=True) + jnp.sum(p_c * dp_c, axis=1, keepdims=True)
                    ds_w.append((p_w * (dp_w - dr) * ATT_SCALE).astype(BF16))
                    ds_c.append((p_c * (dp_c - dr) * ATT_SCALE).astype(BF16))
                    pb_w.append(p_w.astype(BF16))
                    pb_c.append(p_c.astype(BF16))
                    head = kh * 4 + idx
                    dsink_ref[head:head + 1, :] += jnp.broadcast_to(
                        jnp.sum(-p_sink * dr, axis=0, keepdims=True), (1, 128))
                for pr in range(2):
                    dqs[pl.ds(r0, BLK), lanes[pr]] = (
                        _dot(ds_w[2 * pr], kws[0]) + _dot(ds_c[2 * pr], kcs[0])
                        + _dot(ds_w[2 * pr + 1], kws[1]) + _dot(ds_c[2 * pr + 1], kcs[1]))
                q_hs, do_hs = [halves(qp) for qp in qps], [halves(do_p) for do_p in dops]
                q_stack = jnp.concatenate([q_hs[pr][half] for pr, half in heads], axis=0)
                do_stack = jnp.concatenate([do_hs[pr][half] for pr, half in heads], axis=0)
                dka[kh, win, :] += _dot_tn(jnp.concatenate(ds_w, axis=0), q_stack)
                dva[kh, win, :] += _dot_tn(jnp.concatenate(pb_w, axis=0), do_stack)
                dkca[kh] += _dot_tn(jnp.concatenate(ds_c, axis=0), q_stack)
                dvca[kh] += _dot_tn(jnp.concatenate(pb_c, axis=0), do_stack)
                return carry

            lax.fori_loop(0, n_lat // BLK, lat_block, 0)
            for n in range(n_ctx // BLK):
                rows = slice(n * BLK, (n + 1) * BLK)
                for pr in range(2):
                    lanes = slice((kh * 2 + pr) * 128, (kh * 2 + pr + 1) * 128)
                    qp = qc_ref[rows, lanes].astype(BF16)
                    do_p = doc_ref[rows, lanes]
                    q_h, do_h = halves(qp), halves(do_p)
                    dq = None
                    for half, (kcx, vcx) in enumerate(((kca, vca), (kcb, vcb))):
                        part = one_head(kh * 4 + pr * 2 + half, qp, q_h[half], do_p, do_h[half],
                                        None, kcx, None, vcx, None, None)
                        dq = part if dq is None else dq + part
                    dqc_ref[rows, lanes] = dq.astype(BF16)

        def fold(acc):
            r0 = acc[0] + pltpu.roll(acc[0], 64, 1)
            r1 = acc[1] + pltpu.roll(acc[1], 64, 1)
            return jnp.where(lo, r0, r1)

        for gq in range(4):
            sl = slice(gq * 128, (gq + 1) * 128)
            dq_ref[:, sl] = _rope_t(dqs[:, sl], cos_t, sin_t).astype(BF16)
        dk_ref[...] = _rope_t(fold(dka), cos_t, sin_t).astype(BF16)
        dv_ref[...] = fold(dva).astype(BF16)
        dkc_ref[...] = fold(dkca).astype(BF16)
        dvc_ref[...] = fold(dvca).astype(BF16)

    lat = lambda w: pl.BlockSpec((n_lat, w), lambda e: (e, 0))
    ctx = lambda w: pl.BlockSpec((n_ctx, w), lambda e: (e, 0))
    sd = jax.ShapeDtypeStruct
    return pl.pallas_call(
        body, grid=(2,), name=name,
        in_specs=_attn_specs(cfg) + [pl.BlockSpec((n_lat, ATT_W), lambda e: (e, 0)),
                                     pl.BlockSpec((n_ctx, ATT_W), lambda e: (cb + e, 0))],
        out_specs=[lat(ATT_W), lat(128), lat(128), ctx(ATT_W), ctx(128), ctx(128),
                   pl.BlockSpec((None, 8, 128), lambda e: (e, 0, 0))],
        out_shape=[sd((cfg.t_lat, ATT_W), BF16), sd((cfg.t_lat, 128), BF16), sd((cfg.t_lat, 128), BF16),
                   sd((cfg.t_ctx, ATT_W), BF16), sd((cfg.t_ctx, 128), BF16), sd((cfg.t_ctx, 128), BF16),
                   sd((2, 8, 128), F32)],
        scratch_shapes=[pltpu.VMEM((n_lat, ATT_W), BF16)] + [pltpu.VMEM((n_lat, 128), BF16)] * 4
        + [pltpu.VMEM((n_ctx, 128), BF16)] * 4
        + [pltpu.VMEM((n_lat, ATT_W), F32), pltpu.VMEM((2, n_lat, 128), F32), pltpu.VMEM((2, n_lat, 128), F32),
           pltpu.VMEM((2, n_ctx, 128), F32), pltpu.VMEM((2, n_ctx, 128), F32)],
        compiler_params=_params(("parallel",)))(p, p, p, p, p, p, cos, sin, sink_rows, dcat, dcat)


def _shift_down(x, k, row):
    return jnp.where(row >= k, pltpu.roll(x, k, 0), 0.0)


def _shift_up(x, k, row):
    n = x.shape[0]
    return jnp.where(row < n - k, pltpu.roll(x, n - k, 0), 0.0)


def _window_sum(x, r, row):
    below, above, k = x, x, 1
    while k < r:
        below = below + _shift_down(below, k, row)
        above = above + _shift_up(above, k, row)
        k *= 2
    return below + _shift_down(x, r, row) + _shift_up(above, 1, row)


def _inv_count(r, row, n):
    cnt = jnp.minimum(row + r, n - 1) + 1 - jnp.maximum(row - r, 0)
    return 1.0 / cnt.astype(F32)


def _pool_fwd(p, w, scale, n, blk0, n_seg, name):
    def body(u0, u1, u2, u3, w_ref, sc_ref, o_ref):
        row = lax.broadcasted_iota(jnp.int32, (n, 1), 0)
        for g, u_ref in enumerate((u0, u1, u2, u3)):
            u = u_ref[...].astype(F32)
            d = _window_sum(u, POOL_R[g], row) * _inv_count(POOL_R[g], row, n) - u
            o_ref[:, g * 128:(g + 1) * 128] = (_dot(d, w_ref[g]) * sc_ref[:, g * 128:(g + 1) * 128]).astype(BF16)

    return pl.pallas_call(
        body, grid=(n_seg,), name=name,
        in_specs=[pl.BlockSpec((n, 128), functools.partial(lambda g, e: (blk0 + e, 6 + g), g)) for g in range(4)]
        + [pl.BlockSpec((4, 128, 128), lambda e: (0, 0, 0)), pl.BlockSpec((1, 512), lambda e: (0, 0))],
        out_specs=pl.BlockSpec((n, 512), lambda e: (e, 0)),
        out_shape=jax.ShapeDtypeStruct((n_seg * n, 512), BF16),
        compiler_params=_params(("parallel",)))(p, p, p, p, w, scale)


def _pool_bwd(p, w, scale, dcat, n, blk0, n_seg, name):
    def body(u0, u1, u2, u3, w_ref, sc_ref, dp_ref, du_ref, dw_ref, dsc_ref):
        e = pl.program_id(0)

        @pl.when(e == 0)
        def _():
            dw_ref[...] = jnp.zeros_like(dw_ref)
            dsc_ref[...] = jnp.zeros_like(dsc_ref)

        row = lax.broadcasted_iota(jnp.int32, (n, 1), 0)
        for g, u_ref in enumerate((u0, u1, u2, u3)):
            sl = slice(g * 128, (g + 1) * 128)
            u = u_ref[...].astype(F32)
            inv = _inv_count(POOL_R[g], row, n)
            d = _window_sum(u, POOL_R[g], row) * inv - u
            dp = dp_ref[:, sl].astype(F32)
            dsc_ref[:, sl] += jnp.sum(dp * _dot(d, w_ref[g]), axis=0, keepdims=True)
            dyp = dp * sc_ref[:, sl]
            dw_ref[g] += _dot_tn(d, dyp)
            dd = _dot_nt(dyp, w_ref[g])
            du_ref[:, sl] = (_window_sum(dd * inv, POOL_R[g], row) - dd).astype(BF16)

    return pl.pallas_call(
        body, grid=(n_seg,), name=name,
        in_specs=[pl.BlockSpec((n, 128), functools.partial(lambda g, e: (blk0 + e, 6 + g), g)) for g in range(4)]
        + [pl.BlockSpec((4, 128, 128), lambda e: (0, 0, 0)), pl.BlockSpec((1, 512), lambda e: (0, 0)),
           pl.BlockSpec((n, 512), lambda e: (blk0 + e, 1))],
        out_specs=[pl.BlockSpec((n, 512), lambda e: (e, 0)),
                   pl.BlockSpec((4, 128, 128), lambda e: (0, 0, 0)), pl.BlockSpec((1, 512), lambda e: (0, 0))],
        out_shape=[jax.ShapeDtypeStruct((n_seg * n, 512), BF16), jax.ShapeDtypeStruct((4, 128, 128), F32),
                   jax.ShapeDtypeStruct((1, 512), F32)],
        compiler_params=_params(("arbitrary",)))(p, p, p, p, w, scale, dcat)


def _gelu(x):
    t = jnp.tanh(math.sqrt(2.0 / math.pi) * (x + 0.044715 * x * x * x))
    return 0.5 * x * (1.0 + t), t


def _gelu_grad(x, t):
    return 0.5 * (1.0 + t) + 0.5 * x * (1.0 - t * t) * (math.sqrt(2.0 / math.pi) * (1.0 + 3 * 0.044715 * x * x))


def _neg_expm1_twice(x):
    t = jnp.tanh(x)
    return (-2.0 * t) / (1.0 - t)


def _softplus_neg(lam):
    x = -lam
    e = jnp.exp(-jnp.abs(x))
    log1p = jnp.where(e < 1e-2, e * (1.0 - e * (0.5 - e * (1.0 / 3.0))), jnp.log(1.0 + e))
    return jnp.maximum(x, 0.0) + log1p, -_sigmoid(x)


def _conv(u, w_ref, b_ref, row):
    return (b_ref[...] + _shift_down(u, 1, row) * w_ref[0:1, :] + u * w_ref[1:2, :]
            + _shift_up(u, 1, row) * w_ref[2:3, :] + _shift_up(u, 2, row) * w_ref[3:4, :])


def _lru_gates(uc, d, wa_ref, ba_ref, wx_ref, bx_ref, lam_ref):
    r = _sigmoid(_dot(uc, wa_ref[d]) + ba_ref[d:d + 1, :])
    gi = _sigmoid(_dot(uc, wx_ref[d]) + bx_ref[d:d + 1, :])
    sp, dsp = _softplus_neg(lam_ref[d:d + 1, :])
    la = (-LRU_C) * r * sp
    a = jnp.exp(la)
    sq = jnp.sqrt(_neg_expm1_twice(la))
    return r, gi, sp, dsp, a, sq


def _tile_scan(a_ref, b_ref, n, reverse):
    m = n // 8
    first = 7 if reverse else 0
    a_prev = a_ref[pl.ds(first, m, stride=8), :]
    b_prev = b_ref[pl.ds(first, m, stride=8), :]
    for j in (range(6, -1, -1) if reverse else range(1, 8)):
        rows = pl.ds(j, m, stride=8)
        aj = a_ref[rows, :]
        b_prev = aj * b_prev + b_ref[rows, :]
        a_prev = aj * a_prev
        b_ref[rows, :] = b_prev
        a_ref[rows, :] = a_prev


def _carry_scan(a_ref, b_ref, n, reverse, carry):
    nt8 = n // 8

    def step(i, c):
        t = (nt8 - 1 - i) if reverse else i
        off = pl.multiple_of(t * 8, 8)
        h = a_ref[pl.ds(off, 8), :] * c + b_ref[pl.ds(off, 8), :]
        b_ref[pl.ds(off, 8), :] = h
        return h[0:1, :] if reverse else h[7:8, :]

    return lax.fori_loop(0, nt8, step, carry, unroll=4)


def _chain_scan(segs, reverse):
    carry = jnp.zeros((1, 128), F32)
    for a, b, a_ref, b_ref, n in segs:
        a_ref[...] = a
        b_ref[...] = b
        _tile_scan(a_ref, b_ref, n, reverse)
        carry = _carry_scan(a_ref, b_ref, n, reverse, carry)


def _lru_specs(cfg):
    n_lat, n_ctx, cb = cfg.n_lat, cfg.n_ctx, cfg.ctx_blk
    return [pl.BlockSpec((n_lat, 128), lambda hb, e: (e, hb)),
            pl.BlockSpec((n_lat, 128), lambda hb, e: (e, 8 + hb)),
            pl.BlockSpec((n_ctx, 128), lambda hb, e: (cb + e, hb)),
            pl.BlockSpec((n_ctx, 128), lambda hb, e: (cb + e, 8 + hb)),
            pl.BlockSpec((4, 128), lambda hb, e: (0, hb)),
            pl.BlockSpec((1, 128), lambda hb, e: (0, hb)),
            pl.BlockSpec((2, None, 128, 128), lambda hb, e: (0, hb, 0, 0)),
            pl.BlockSpec((2, 128), lambda hb, e: (0, hb)),
            pl.BlockSpec((2, None, 128, 128), lambda hb, e: (0, hb, 0, 0)),
            pl.BlockSpec((2, 128), lambda hb, e: (0, hb)),
            pl.BlockSpec((2, 128), lambda hb, e: (0, hb))]


def _lru_fwd(cfg, p, consts, name):
    n_lat, n_ctx = cfg.n_lat, cfg.n_ctx

    def body(gl_ref, ul_ref, gc_ref, uc_ref, cw_ref, cb_ref, wa_ref, ba_ref, wx_ref, bx_ref, lam_ref,
             zl_ref, zc_ref, hl_ref, hc_ref, al, ac):
        row_l = lax.broadcasted_iota(jnp.int32, (n_lat, 1), 0)
        row_c = lax.broadcasted_iota(jnp.int32, (n_ctx, 1), 0)
        uc_l = _conv(ul_ref[...].astype(F32), cw_ref, cb_ref, row_l)
        uc_c = _conv(uc_ref[...].astype(F32), cw_ref, cb_ref, row_c)
        for d in range(2):
            _, gi_l, _, _, a_l, sq_l = _lru_gates(uc_l, d, wa_ref, ba_ref, wx_ref, bx_ref, lam_ref)
            _, gi_c, _, _, a_c, sq_c = _lru_gates(uc_c, d, wa_ref, ba_ref, wx_ref, bx_ref, lam_ref)
            _chain_scan([(a_c, sq_c * (gi_c * uc_c), ac, hc_ref.at[d], n_ctx),
                         (a_l, sq_l * (gi_l * uc_l), al, hl_ref.at[d], n_lat)], reverse=(d == 1))
        zl_ref[...] = (_gelu(gl_ref[...].astype(F32))[0] * (hl_ref[0] + hl_ref[1])).astype(BF16)
        zc_ref[...] = (_gelu(gc_ref[...].astype(F32))[0] * (hc_ref[0] + hc_ref[1])).astype(BF16)

    return pl.pallas_call(
        body, grid=(8, 2), name=name, in_specs=_lru_specs(cfg),
        out_specs=[pl.BlockSpec((n_lat, 128), lambda hb, e: (e, hb)), pl.BlockSpec((n_ctx, 128), lambda hb, e: (e, hb)),
                   pl.BlockSpec((2, n_lat, 128), lambda hb, e: (0, e, hb)),
                   pl.BlockSpec((2, n_ctx, 128), lambda hb, e: (0, e, hb))],
        out_shape=[jax.ShapeDtypeStruct((cfg.t_lat, D), BF16), jax.ShapeDtypeStruct((cfg.t_ctx, D), BF16),
                   jax.ShapeDtypeStruct((2, cfg.t_lat, D), F32), jax.ShapeDtypeStruct((2, cfg.t_ctx, D), F32)],
        scratch_shapes=[pltpu.VMEM((n_lat, 128), F32), pltpu.VMEM((n_ctx, 128), F32)],
        compiler_params=_params(("parallel", "arbitrary")))(p, p, p, p, *consts)


def _lru_bwd(cfg, p, dz, h_lat, h_ctx, consts, name):
    n_lat, n_ctx, cb = cfg.n_lat, cfg.n_ctx, cfg.ctx_blk

    def body(gl_ref, ul_ref, gc_ref, uc_ref, cw_ref, cb_ref, wa_ref, ba_ref, wx_ref, bx_ref, lam_ref,
             dzl_ref, dzc_ref, hl, hc, dgl_ref, dul_ref, dgc_ref, duc_ref, dwa_ref, dwx_ref, vec_ref,
             al, bl, ac, bc):
        e = pl.program_id(1)

        @pl.when(e == 0)
        def _():
            dwa_ref[...] = jnp.zeros_like(dwa_ref)
            dwx_ref[...] = jnp.zeros_like(dwx_ref)
            vec_ref[...] = jnp.zeros_like(vec_ref)

        row_l = lax.broadcasted_iota(jnp.int32, (n_lat, 1), 0)
        row_c = lax.broadcasted_iota(jnp.int32, (n_ctx, 1), 0)
        u_l, u_c = ul_ref[...].astype(F32), uc_ref[...].astype(F32)
        uc_l = _conv(u_l, cw_ref, cb_ref, row_l)
        uc_c = _conv(u_c, cw_ref, cb_ref, row_c)
        gel_l, t_l = _gelu(gl_ref[...].astype(F32))
        gel_c, t_c = _gelu(gc_ref[...].astype(F32))
        dz_l, dz_c = dzl_ref[...].astype(F32), dzc_ref[...].astype(F32)
        dgl_ref[...] = (dz_l * (hl[0] + hl[1]) * _gelu_grad(gl_ref[...].astype(F32), t_l)).astype(BF16)
        dgc_ref[...] = (dz_c * (hc[0] + hc[1]) * _gelu_grad(gc_ref[...].astype(F32), t_c)).astype(BF16)
        dy_l, dy_c = dz_l * gel_l, dz_c * gel_c
        duc_l = jnp.zeros((n_lat, 128), F32)
        duc_c = jnp.zeros((n_ctx, 128), F32)
        for d in range(2):
            r_l, gi_l, sp, dsp, a_l, sq_l = _lru_gates(uc_l, d, wa_ref, ba_ref, wx_ref, bx_ref, lam_ref)
            r_c, gi_c, _, _, a_c, sq_c = _lru_gates(uc_c, d, wa_ref, ba_ref, wx_ref, bx_ref, lam_ref)
            if d == 0:
                an_l = _shift_up(a_l, 1, row_l)
                an_c = jnp.where(row_c < n_ctx - 1, pltpu.roll(a_c, n_ctx - 1, 0), a_l[0:1, :])
            else:
                an_l = _shift_down(a_l, 1, row_l)
                an_c = jnp.where(row_c >= 1, pltpu.roll(a_c, 1, 0), a_l[n_lat - 1:n_lat, :])
            _chain_scan([(an_l, dy_l, al, bl, n_lat), (an_c, dy_c, ac, bc, n_ctx)], reverse=(d == 0))
            dsp_sum = jnp.zeros((1, 128), F32)
            for (dh, h, r, gi, a, sq, uc, seg) in ((bl[...], hl[d], r_l, gi_l, a_l, sq_l, uc_l, "l"),
                                                  (bc[...], hc[d], r_c, gi_c, a_c, sq_c, uc_c, "c")):
                b0 = sq * (gi * uc)
                t1 = dh * sq
                dla = dh * (h - b0) - (dh * gi * uc) * (a * a) / sq
                dzr = (dla * ((-LRU_C) * sp)) * r * (1.0 - r)
                dzi = (t1 * uc) * gi * (1.0 - gi)
                dsp_sum = dsp_sum + jnp.sum(dla * ((-LRU_C) * r), axis=0, keepdims=True)
                dwa_ref[d] += _dot_tn(uc, dzr)
                dwx_ref[d] += _dot_tn(uc, dzi)
                vec_ref[d:d + 1, :] += jnp.sum(dzr, axis=0, keepdims=True)
                vec_ref[2 + d:3 + d, :] += jnp.sum(dzi, axis=0, keepdims=True)
                duc = t1 * gi + _dot_nt(dzr, wa_ref[d]) + _dot_nt(dzi, wx_ref[d])
                if seg == "l":
                    duc_l = duc_l + duc
                else:
                    duc_c = duc_c + duc
            vec_ref[4 + d:5 + d, :] += dsp_sum * dsp
        for duc, u, row, du_ref in ((duc_l, u_l, row_l, dul_ref), (duc_c, u_c, row_c, duc_ref)):
            du_ref[...] = (_shift_up(duc, 1, row) * cw_ref[0:1, :] + duc * cw_ref[1:2, :]
                           + _shift_down(duc, 1, row) * cw_ref[2:3, :]
                           + _shift_down(duc, 2, row) * cw_ref[3:4, :]).astype(BF16)
            vec_ref[6:7, :] += jnp.sum(duc * _shift_down(u, 1, row), axis=0, keepdims=True)
            vec_ref[7:8, :] += jnp.sum(duc * u, axis=0, keepdims=True)
            vec_ref[8:9, :] += jnp.sum(duc * _shift_up(u, 1, row), axis=0, keepdims=True)
            vec_ref[9:10, :] += jnp.sum(duc * _shift_up(u, 2, row), axis=0, keepdims=True)
            vec_ref[10:11, :] += jnp.sum(duc, axis=0, keepdims=True)

    lat = pl.BlockSpec((n_lat, 128), lambda hb, e: (e, hb))
    ctx = pl.BlockSpec((n_ctx, 128), lambda hb, e: (e, hb))
    wspec = pl.BlockSpec((2, None, 128, 128), lambda hb, e: (0, hb, 0, 0))
    sd = jax.ShapeDtypeStruct
    return pl.pallas_call(
        body, grid=(8, 2), name=name,
        in_specs=_lru_specs(cfg) + [pl.BlockSpec((n_lat, 128), lambda hb, e: (e, hb)),
                                    pl.BlockSpec((n_ctx, 128), lambda hb, e: (cb + e, hb)),
                                    pl.BlockSpec((2, n_lat, 128), lambda hb, e: (0, e, hb)),
                                    pl.BlockSpec((2, n_ctx, 128), lambda hb, e: (0, e, hb))],
        out_specs=[lat, lat, ctx, ctx, wspec, wspec, pl.BlockSpec((None, 16, 128), lambda hb, e: (hb, 0, 0))],
        out_shape=[sd((cfg.t_lat, D), BF16), sd((cfg.t_lat, D), BF16), sd((cfg.t_ctx, D), BF16), sd((cfg.t_ctx, D), BF16),
                   sd((2, 8, 128, 128), F32), sd((2, 8, 128, 128), F32), sd((8, 16, 128), F32)],
        scratch_shapes=[pltpu.VMEM((n_lat, 128), F32)] * 2 + [pltpu.VMEM((n_ctx, 128), F32)] * 2,
        compiler_params=_params(("parallel", "arbitrary")))(p, p, p, p, *consts, dz, dz, h_lat, h_ctx)


def _position():
    x, y, c = lax.axis_index("x"), lax.axis_index("y"), lax.axis_index("c")
    return x, y, c, 4 * x + 2 * y + c


def _peer(x, y, c, k):
    px = 1 - x if k & 4 else x
    py = 1 - y if k & 2 else y
    pc = 1 - c if k & 1 else c
    return (px, py, pc), 4 * px + 2 * py + pc


def _all_gather(v, name, in_vmem):
    def body(v_ref, o_ref, send_sems, recv_sems, local_sem):
        x, y, c, me = _position()
        mine = pltpu.make_async_copy(v_ref, o_ref.at[me], local_sem)
        mine.start()
        sends = []
        for k in range(1, N_DEV):
            peer, _ = _peer(x, y, c, k)
            cp = pltpu.make_async_remote_copy(src_ref=v_ref, dst_ref=o_ref.at[me], send_sem=send_sems.at[k - 1],
                                              recv_sem=recv_sems.at[k - 1], device_id=peer, device_id_type=MESH)
            cp.start()
            sends.append(cp)
        for k in range(1, N_DEV):
            peer, peer_lin = _peer(x, y, c, k)
            pltpu.make_async_remote_copy(src_ref=v_ref, dst_ref=o_ref.at[peer_lin], send_sem=send_sems.at[k - 1],
                                         recv_sem=recv_sems.at[k - 1], device_id=peer, device_id_type=MESH).wait_recv()
        for cp in sends:
            cp.wait_send()
        mine.wait()

    space = pltpu.VMEM if in_vmem else pl.ANY
    return pl.pallas_call(
        body, name=name,
        in_specs=[pl.BlockSpec(memory_space=space)], out_specs=pl.BlockSpec(memory_space=space),
        out_shape=jax.ShapeDtypeStruct((N_DEV,) + v.shape, v.dtype),
        scratch_shapes=[pltpu.SemaphoreType.DMA((N_DEV - 1,)), pltpu.SemaphoreType.DMA((N_DEV - 1,)),
                        pltpu.SemaphoreType.DMA],
        compiler_params=pltpu.CompilerParams(vmem_limit_bytes=VMEM_LIMIT))(v)


_HBM = pl.BlockSpec(memory_space=pltpu.HBM)
_SEM = pl.BlockSpec(memory_space=pltpu.SEMAPHORE)
_EFFECT = pltpu.SideEffectType.DATAFLOW_SIDE_EFFECTING


ALL_PEERS = tuple(range(1, N_DEV))
SAME_CORE_AND_SIBLING = (1, 2, 4, 6)


def _push_start(src, land, block_of, name, relations=ALL_PEERS):
    def body(src_ref, land_ref, send_sem, recv_sem, src_thru, land_thru, token):
        x, y, c, me = _position()
        for k in relations:
            peer, peer_lin = _peer(x, y, c, k)
            mine, there = block_of(src_ref, land_ref, me, peer_lin)
            pltpu.make_async_remote_copy(src_ref=mine, dst_ref=there, send_sem=send_sem, recv_sem=recv_sem,
                                         device_id=peer, device_id_type=MESH).start()
        mine, here = block_of(src_ref, land_ref, me, me)
        pltpu.make_async_copy(mine, here, recv_sem).start()
        token[...] = jnp.zeros_like(token)

    return pl.pallas_call(
        body, name=name,
        out_shape=(pltpu.SemaphoreType.DMA(()), pltpu.SemaphoreType.DMA(()), pltpu.HBM(src.shape, src.dtype),
                   pltpu.HBM(land.shape, land.dtype), jax.ShapeDtypeStruct((8, 128), F32)),
        in_specs=(_HBM, _HBM), out_specs=(_SEM, _SEM, _HBM, _HBM, pl.BlockSpec(memory_space=pltpu.VMEM)),
        input_output_aliases={0: 2, 1: 3},
        compiler_params=pltpu.CompilerParams(has_side_effects=_EFFECT),
    )(pltpu.with_memory_space_constraint(src, pltpu.HBM), pltpu.with_memory_space_constraint(land, pltpu.HBM))


def _push_wait(handle, blocks_of, after, name, n_peers=N_DEV - 1):
    send_sem, recv_sem, src_thru, land_thru, _ = handle

    def body(src_ref, land_ref, send_sem, recv_sem, after_ref, src_dead, got_ref):
        x, y, c, _ = _position()
        sent, landed = blocks_of(land_ref, n_peers), blocks_of(land_ref, n_peers + 1)
        pltpu.make_async_remote_copy(src_ref=sent, dst_ref=sent, send_sem=send_sem, recv_sem=recv_sem,
                                     device_id=(x, y, 1 - c), device_id_type=MESH).wait_send()
        pltpu.make_async_remote_copy(src_ref=landed, dst_ref=landed, send_sem=send_sem, recv_sem=recv_sem,
                                     device_id=(x, y, 1 - c), device_id_type=MESH).wait_recv()

    return pl.pallas_call(
        body, name=name,
        out_shape=(pltpu.HBM(src_thru.shape, src_thru.dtype), pltpu.HBM(land_thru.shape, land_thru.dtype)),
        in_specs=(_HBM, _HBM, _SEM, _SEM, pl.BlockSpec(memory_space=pl.ANY)), out_specs=(_HBM, _HBM),
        input_output_aliases={0: 0, 1: 1},
        compiler_params=pltpu.CompilerParams(has_side_effects=_EFFECT),
    )(src_thru, land_thru, send_sem, recv_sem, after)[1]


def _gather_start(src, name, relations=ALL_PEERS):
    g, r, C = src.shape
    land = lax.empty((g, N_DEV * r, C), src.dtype)
    return _push_start(src, land, lambda s, z, i, p: (s, z.at[:, pl.ds(i * r, r), :]), name, relations)


def _gather_wait(handle, after, name, n_peers=N_DEV - 1):
    r = handle[2].shape[1]
    return _push_wait(handle, lambda z, n: z.at[:, pl.ds(0, n * r), :], after, name, n_peers)


def _relay_start(land, r, name):
    def body(land_ref, send_sem, recv_sem, land_thru, token):
        x, y, c, _ = _position()
        for k in (2, 4, 6):
            _, origin = _peer(x, y, c, k)
            rows = land_ref.at[:, pl.ds(origin * r, r), :]
            pltpu.make_async_remote_copy(src_ref=rows, dst_ref=rows, send_sem=send_sem, recv_sem=recv_sem,
                                         device_id=(x, y, 1 - c), device_id_type=MESH).start()
        token[...] = jnp.zeros_like(token)

    return pl.pallas_call(
        body, name=name,
        out_shape=(pltpu.SemaphoreType.DMA(()), pltpu.SemaphoreType.DMA(()), pltpu.HBM(land.shape, land.dtype),
                   jax.ShapeDtypeStruct((8, 128), F32)),
        in_specs=(_HBM,), out_specs=(_SEM, _SEM, _HBM, pl.BlockSpec(memory_space=pltpu.VMEM)),
        input_output_aliases={0: 2},
        compiler_params=pltpu.CompilerParams(has_side_effects=_EFFECT),
    )(pltpu.with_memory_space_constraint(land, pltpu.HBM))


def _relay_wait(handle, r, after, name):
    send_sem, recv_sem, land_thru, _ = handle

    def body(land_ref, send_sem, recv_sem, after_ref, got_ref):
        x, y, c, _ = _position()
        three = land_ref.at[:, pl.ds(0, 3 * r), :]
        cp = pltpu.make_async_remote_copy(src_ref=three, dst_ref=three, send_sem=send_sem, recv_sem=recv_sem,
                                          device_id=(x, y, 1 - c), device_id_type=MESH)
        cp.wait_send()
        cp.wait_recv()

    return pl.pallas_call(
        body, name=name, out_shape=(pltpu.HBM(land_thru.shape, land_thru.dtype),),
        in_specs=(_HBM, _SEM, _SEM, pl.BlockSpec(memory_space=pl.ANY)), out_specs=(_HBM,),
        input_output_aliases={0: 0},
        compiler_params=pltpu.CompilerParams(has_side_effects=_EFFECT),
    )(land_thru, send_sem, recv_sem, after)[0]


def _exchange_start(grad, name):
    g, rows, C = grad.shape
    r = rows // N_DEV
    land = lax.empty((N_DEV, g, r, C), grad.dtype)
    return _push_start(grad, land, lambda s, z, i, p: (s.at[:, pl.ds(p * r, r), :], z.at[i]), name)


def _exchange_wait(handle, after, name):
    return _push_wait(handle, lambda z, n: z.at[pl.ds(0, n)], after, name)


def _sum_blocks(v, name):
    k, rows, cols = v.shape
    tr = rows
    for cand in (rows, 512, 352, 256, 176, 128, 64, 32, 16):
        if rows % cand == 0 and k * cand * cols * v.dtype.itemsize <= 6 * 1024 * 1024:
            tr = cand
            break

    def body(v_ref, o_ref):
        acc = v_ref[0].astype(F32)
        for s in range(1, k):
            acc = acc + v_ref[s].astype(F32)
        o_ref[...] = acc

    return pl.pallas_call(
        body, grid=(rows // tr,), name=name,
        in_specs=[pl.BlockSpec((k, tr, cols), lambda i: (0, i, 0))],
        out_specs=pl.BlockSpec((tr, cols), lambda i: (i, 0)),
        out_shape=jax.ShapeDtypeStruct((rows, cols), F32),
        compiler_params=_params(("parallel",)))(v)


def _adam_math(w, g, m, v):
    m2 = B1 * m + (1.0 - B1) * g
    v2 = B2 * v + (1.0 - B2) * (g * g)
    m_hat = m2 / (1.0 - B1 ** STEP)
    v_hat = v2 / (1.0 - B2 ** STEP)
    return -LR * (m_hat / (jnp.sqrt(v_hat) + EPS) + WD * w), m2, v2


def _adamw(w, g, m, v, name, dep=None):
    shp = w.shape
    rows, cols = (shp[-2], shp[-1]) if len(shp) >= 2 else (1, shp[-1])
    lead = math.prod(shp[:-2]) if len(shp) > 2 else 1
    fits = [t for t in range(8, rows + 1, 8) if rows % t == 0 and t * cols * 4 <= 2 * 1024 * 1024]
    tr = max(fits) if fits else rows

    def body(w_ref, g_ref, m_ref, v_ref, *rest):
        d_ref, m2_ref, v2_ref = rest[-3:]
        d_ref[...], m2_ref[...], v2_ref[...] = _adam_math(w_ref[...], g_ref[...], m_ref[...], v_ref[...])

    blk = pl.BlockSpec((None, tr, cols), lambda b, i: (b, i, 0))
    extra = [] if dep is None else [dep]
    outs = pl.pallas_call(
        body, grid=(lead, rows // tr), name=name,
        in_specs=[blk] * 4 + [pl.BlockSpec(memory_space=pl.ANY)] * len(extra), out_specs=[blk] * 3,
        out_shape=[jax.ShapeDtypeStruct((lead, rows, cols), F32)] * 3,
        compiler_params=_params(("parallel", "parallel")))(*[a.reshape(lead, rows, cols) for a in (w, g, m, v)], *extra)
    return [o.reshape(shp) for o in outs]


def _as2d(a):
    n = a.size
    if n % 1024 == 0:
        return a.reshape(n // 1024, 1024)
    if n % 128 == 0:
        return a.reshape(n // 128, 128)
    return a.reshape(1, n)


def _blocks_to_cols(a):
    b = jnp.moveaxis(a, 0, -2)
    return b.reshape(b.shape[:-2] + (b.shape[-2] * b.shape[-1],))


def _pack_rows(parts):
    padded, offs, r = [], [], 0
    for p in parts:
        pad = (-p.shape[0]) % 8
        padded.append(jnp.pad(p, ((0, pad), (0, 0))) if pad else p)
        offs.append(r)
        r += p.shape[0] + pad
    return jnp.concatenate(padded, axis=0), offs


def _silu(x):
    return x * jax.nn.sigmoid(x)


def kernel(x, c, ctx, c_ctx, w_mod, b_mod, ln_g, ln_b, ffn_w_gate, ffn_w_up, ffn_w_down, mix_ab_w_in, attn_sink, pool_w, pool_scale, mix_ab_w_out, lru_w_in, lru_conv_w, lru_conv_b, lru_wa, lru_ba, lru_wx, lru_bx, lru_lambda, lru_w_out, loss_target, m_c_ctx, m_w_mod, m_b_mod, m_ln_g, m_ln_b, m_ffn_w_gate, m_ffn_w_up, m_ffn_w_down, m_mix_ab_w_in, m_attn_sink, m_pool_w, m_pool_scale, m_mix_ab_w_out, m_lru_w_in, m_lru_conv_w, m_lru_conv_b, m_lru_wa, m_lru_ba, m_lru_wx, m_lru_bx, m_lru_lambda, m_lru_w_out, v_c_ctx, v_w_mod, v_b_mod, v_ln_g, v_ln_b, v_ffn_w_gate, v_ffn_w_up, v_ffn_w_down, v_mix_ab_w_in, v_attn_sink, v_pool_w, v_pool_scale, v_mix_ab_w_out, v_lru_w_in, v_lru_conv_w, v_lru_conv_b, v_lru_wa, v_lru_ba, v_lru_wx, v_lru_bx, v_lru_lambda, v_lru_w_out):
    weights = dict(c_ctx=c_ctx, w_mod=w_mod, b_mod=b_mod, ln_g=ln_g, ln_b=ln_b, ffn_w_gate=ffn_w_gate,
                   ffn_w_up=ffn_w_up, ffn_w_down=ffn_w_down, mix_ab_w_in=mix_ab_w_in, attn_sink=attn_sink,
                   pool_w=pool_w, pool_scale=pool_scale, mix_ab_w_out=mix_ab_w_out, lru_w_in=lru_w_in,
                   lru_conv_w=lru_conv_w, lru_conv_b=lru_conv_b, lru_wa=lru_wa, lru_ba=lru_ba, lru_wx=lru_wx,
                   lru_bx=lru_bx, lru_lambda=lru_lambda, lru_w_out=lru_w_out)
    mom_m = dict(c_ctx=m_c_ctx, w_mod=m_w_mod, b_mod=m_b_mod, ln_g=m_ln_g, ln_b=m_ln_b, ffn_w_gate=m_ffn_w_gate,
                 ffn_w_up=m_ffn_w_up, ffn_w_down=m_ffn_w_down, mix_ab_w_in=m_mix_ab_w_in, attn_sink=m_attn_sink,
                 pool_w=m_pool_w, pool_scale=m_pool_scale, mix_ab_w_out=m_mix_ab_w_out, lru_w_in=m_lru_w_in,
                 lru_conv_w=m_lru_conv_w, lru_conv_b=m_lru_conv_b, lru_wa=m_lru_wa, lru_ba=m_lru_ba, lru_wx=m_lru_wx,
                 lru_bx=m_lru_bx, lru_lambda=m_lru_lambda, lru_w_out=m_lru_w_out)
    mom_v = dict(c_ctx=v_c_ctx, w_mod=v_w_mod, b_mod=v_b_mod, ln_g=v_ln_g, ln_b=v_ln_b, ffn_w_gate=v_ffn_w_gate,
                 ffn_w_up=v_ffn_w_up, ffn_w_down=v_ffn_w_down, mix_ab_w_in=v_mix_ab_w_in, attn_sink=v_attn_sink,
                 pool_w=v_pool_w, pool_scale=v_pool_scale, mix_ab_w_out=v_mix_ab_w_out, lru_w_in=v_lru_w_in,
                 lru_conv_w=v_lru_conv_w, lru_conv_b=v_lru_conv_b, lru_wa=v_lru_wa, lru_ba=v_lru_ba, lru_wx=v_lru_wx,
                 lru_bx=v_lru_bx, lru_lambda=v_lru_lambda, lru_w_out=v_lru_w_out)
    names = list(weights)

    n_lat, n_ctx = x.shape[1], ctx.shape[1]
    cfg = _Cfg(n_lat, n_ctx)
    _, _, _, me = _position()
    mcols = w_mod.shape[2]

    def t_bf16(w):
        return jnp.swapaxes(w, -1, -2).astype(BF16)

    def ffn_src(l, i):
        return jnp.stack([t_bf16(ffn_w_gate[l, i]), t_bf16(ffn_w_up[l, i]), ffn_w_down[l, i].astype(BF16)])

    pending = {}

    def start_gathers(items, tok):
        for key, make_src in items:
            pending[key] = _gather_start(make_src() + tok.astype(BF16), "gather_start_" + key)
            tok = pending[key][4][0, 0]
        return tok

    def weights_now(key, after):
        return _gather_wait(pending[key], after, "gather_wait_" + key)

    first = _gather_start(ffn_src(0, 0), "gather_start_ffn00", SAME_CORE_AND_SIBLING)
    tok = first[4][0, 0]

    small_names = ["ln_g", "ln_b", "lru_conv_w", "lru_conv_b", "lru_ba", "lru_bx", "lru_lambda"]
    small, small_off = _pack_rows([(c + tok).reshape(-1, 128)] + [weights[n].reshape(-1, 128) for n in small_names])
    small_all = _all_gather(small, "gather_small", True)

    def small_full(idx, shp):
        rows = math.prod(shp) // 128
        return _blocks_to_cols(small_all[:, small_off[idx]:small_off[idx] + rows, :].reshape((N_DEV,) + shp))

    c_all = small_all[:, :2 * D // 128, :].reshape(2 * N_DEV, D)
    ln_g_f, ln_b_f = small_full(1, ln_g.shape), small_full(2, ln_b.shape)
    lru_consts = (small_full(3, lru_conv_w.shape)[0], small_full(4, lru_conv_b.shape), lru_wa[0],
                  small_full(5, lru_ba.shape)[0], lru_wx[0], small_full(6, lru_bx.shape)[0],
                  small_full(7, lru_lambda.shape)[0])

    s_rows = jnp.zeros((32, D), F32).at[:16].set(_silu(c_all)).at[16].set(_silu(c_ctx)).astype(BF16)
    mod_mine = jnp.stack([_matmul(s_rows, w_mod[l], "nn", F32, "mod_fwd", bn_cap=1280) for l in range(2)])
    mod_all = _all_gather(mod_mine.reshape(64, mcols), "gather_mod", True).reshape(N_DEV, 2, 32, mcols)
    r_ffn = ffn_w_down.shape[2]
    relay = _relay_start(_gather_wait(first, mod_all, "gather_wait_ffn00", n_peers=len(SAME_CORE_AND_SIBLING)),
                         r_ffn, "gather_relay_start_ffn00")
    tok = start_gathers([("ab_in", lambda: t_bf16(mix_ab_w_in)), ("ab_out", lambda: mix_ab_w_out.astype(BF16)),
                         ("ffn01", lambda: ffn_src(0, 1)), ("ffn10", lambda: ffn_src(1, 0)),
                         ("lru_in", lambda: t_bf16(lru_w_in)), ("lru_out", lambda: lru_w_out.astype(BF16)),
                         ("ffn11", lambda: ffn_src(1, 1))], relay[3][0, 0])
    mod_full = _blocks_to_cols(mod_all) + (b_mod[:, None, :] + tok)
    ex0 = 2 * me
    mods = []
    for l in range(2):
        rows = jnp.stack([lax.dynamic_index_in_dim(mod_full[l], ex0, 0, False),
                          lax.dynamic_index_in_dim(mod_full[l], ex0 + 1, 0, False), mod_full[l, 16]])
        mods.append(rows.reshape(3, N_MOD, D))

    h0 = jnp.concatenate([x.reshape(cfg.t_lat, D), ctx.reshape(cfg.t_ctx, D)], axis=0)
    cos, sin = _rope_tables(n_lat)
    sink_rows = jnp.broadcast_to(attn_sink[0][:, None], (8, 128)).astype(F32)

    saved = []
    wf = [[None, None], [None, None]]
    h = h0
    xin = _modulate(cfg, h0, mods[0], 0, 1, "modulate_in")
    for l in range(2):
        st = {"h_in": h, "xin1": xin}
        wf[l][0] = (_relay_wait(relay, r_ffn, xin, "gather_relay_wait_ffn00") if l == 0
                    else weights_now("ffn10", xin))
        g1, u1, y1 = _ffn_fwd(xin, wf[l][0], "ffn_fwd")
        h1, xhat1, rstd1, xin2 = _ln_fwd(cfg, h, y1, mods[l], 2, 0.5, ln_g_f[l, 0][None], ln_b_f[l, 0][None],
                                          mods[l], (3, 4), "ln_fwd_a")
        st.update(g1=g1, u1=u1, y1=y1, h1=h1, xhat1=xhat1, rstd1=rstd1, xin2=xin2)
        if l == 0:
            w_ab_in_t = weights_now("ab_in", xin2)[0]
            p = _matmul(xin2, w_ab_in_t, "nt", BF16, "mix_ab_in")
            att_l, att_c = _attn_fwd(cfg, p, cos, sin, sink_rows, "attn_fwd")
            pool_l = _pool_fwd(p, pool_w[0], pool_scale, n_lat, 0, 2, "pool_fwd_lat")
            pool_c = _pool_fwd(p, pool_w[0], pool_scale, n_ctx, cfg.ctx_blk, 2, "pool_fwd_ctx")
            cat = jnp.concatenate([jnp.concatenate([att_l, pool_l], axis=1),
                                   jnp.concatenate([att_c, pool_c], axis=1)], axis=0)
            w_ab_out = weights_now("ab_out", cat)[0]
            y2 = _matmul(cat, w_ab_out, "nn", BF16, "mix_ab_out")
        else:
            w_lru_in_t = weights_now("lru_in", xin2)[0]
            p = _matmul(xin2, w_lru_in_t, "nt", BF16, "lru_in")
            z_l, z_c, st["h_lat"], st["h_ctx"] = _lru_fwd(cfg, p, lru_consts, "lru_fwd")
            cat = jnp.concatenate([z_l, z_c], axis=0)
            w_lru_out = weights_now("lru_out", cat)[0]
            y2 = _matmul(cat, w_lru_out, "nn", BF16, "lru_out")
        h2, xhat2, rstd2, xin3 = _ln_fwd(cfg, h1, y2, mods[l], 5, 1.0, ln_g_f[l, 1][None], ln_b_f[l, 1][None],
                                          mods[l], (6, 7), "ln_fwd_b")
        wf[l][1] = weights_now("ffn%d1" % l, xin3)
        g3, u3, y3 = _ffn_fwd(xin3, wf[l][1], "ffn_fwd")
        if l == 0:
            h3, xhat3, rstd3, xin = _ln_fwd(cfg, h2, y3, mods[l], 8, 0.5, ln_g_f[l, 2][None], ln_b_f[l, 2][None],
                                            mods[1], (0, 1), "ln_fwd_a")
        else:
            h3, xhat3, rstd3 = _ln_fwd(cfg, h2, y3, mods[l], 8, 0.5, ln_g_f[l, 2][None], ln_b_f[l, 2][None],
                                       None, None, "ln_fwd_last")
        st.update(p=p, cat=cat, y2=y2, h2=h2, xhat2=xhat2, rstd2=rstd2, xin3=xin3, g3=g3, u3=u3, y3=y3,
                  xhat3=xhat3, rstd3=rstd3)
        saved.append(st)
        h = h3

    dy, loss_tile = _loss(cfg, h, loss_target.reshape(cfg.t_lat, D), "loss")
    loss = lax.psum(loss_tile[0, 0], ("x", "y", "c"))

    grads = {}
    dmod = [None, None]
    recv_ffn = [[None, None], [None, None]]
    dln_g = [[None] * 3, [None] * 3]
    dln_b = [[None] * 3, [None] * 3]

    def ffn_weight_grads(tag, xin_b, dg, du, a_act, dys):
        handles = []
        for k, (lhs, rhs) in enumerate(((dg, xin_b), (du, xin_b), (a_act, dys))):
            part = _matmul(lhs, rhs, "tn", BF16, "ffn_dw", bm_cap=1408, bk_cap=2304)[None]
            handles.append(_exchange_start(part, "exchange_start_ffn%s_%d" % (tag, k)))
        return handles

    def pin(handles):
        total = handles[0][4][0, 0]
        for hd in handles[1:]:
            total = total + hd[4][0, 0]
        return total

    up = (dy,)
    dmod_next = None
    last_sent = None
    for l in (1, 0):
        st = saved[l]
        dm = [None] * N_MOD

        def put_stats(stats, gate_idx, nxt):
            dm[gate_idx] = stats[:, 2, :]
            if nxt is not None:
                nxt[0][nxt[1]] = stats[:, 4, :]
                nxt[0][nxt[1] + 1] = stats[:, 3, :]

        lng3 = ln_g_f[l, 2][None] if last_sent is None else ln_g_f[l, 2][None] + pin(last_sent)
        if len(up) > 1:
            up = (up[0], up[1], ln_b_f[l, 2][None], up[3], up[4])
        dres, dys, stats = _ln_bwd(cfg, up, st["xhat3"], st["rstd3"], st["y3"], mods[l], 8, 0.5,
                                   lng3, "ln_bwd_fused" if len(up) > 1 else "ln_bwd_last")
        put_stats(stats, 8, None if len(up) == 1 else (dmod_next, 0))
        dln_g[l][2], dln_b[l][2] = stats[:, 0, :].sum(0), stats[:, 1, :].sum(0)
        dg, du, a_act, dxin = _ffn_bwd(dys, st["g3"], st["u3"], wf[l][1], "ffn_bwd")
        recv_ffn[l][1] = ffn_weight_grads("%d1" % l, st["xin3"], dg, du, a_act, dys)
        dres, dys, stats = _ln_bwd(cfg, (dres, dxin, ln_b_f[l, 1][None], mods[l], 7), st["xhat2"], st["rstd2"], st["y2"],
                                   mods[l], 5, 1.0, ln_g_f[l, 1][None] + pin(recv_ffn[l][1]), "ln_bwd_fused")
        put_stats(stats, 5, (dm, 6))
        dln_g[l][1], dln_b[l][1] = stats[:, 0, :].sum(0), stats[:, 1, :].sum(0)
        if l == 0:
            dw_out = _matmul(st["cat"], dys, "tn", BF16, "mix_ab_dw_out")
            dcat = _matmul(dys, w_ab_out, "nt", BF16, "mix_ab_dcat")
            dq, dk, dv, dqc, dkc, dvc, dsink = _attn_bwd(cfg, st["p"], dcat, cos, sin, sink_rows, "attn_bwd")
            du_l, dpw_l, dps_l = _pool_bwd(st["p"], pool_w[0], pool_scale, dcat, n_lat, 0, 2, "pool_bwd_lat")
            du_c, dpw_c, dps_c = _pool_bwd(st["p"], pool_w[0], pool_scale, dcat, n_ctx, cfg.ctx_blk, 2, "pool_bwd_ctx")
            dp = jnp.concatenate([jnp.concatenate([dq, dk, dv, du_l], axis=1),
                                  jnp.concatenate([dqc, dkc, dvc, du_c], axis=1)], axis=0)
            dw_in_t = _matmul(dp, st["xin2"], "tn", BF16, "mix_ab_dw_in", bm_cap=1280)
            dxin = _matmul(dp, w_ab_in_t, "nn", BF16, "mix_ab_dx")
            recv_mix = [_exchange_start(part, "exchange_start_mix_ab_%d" % k)
                        for k, part in enumerate((dw_in_t[None], dw_out[None], _as2d(dpw_l + dpw_c)[None]))]
            grads["attn_sink"] = (dsink[0, :, 0] + dsink[1, :, 0])[None, :]
            grads["pool_scale"] = dps_l + dps_c
        else:
            dw_out = _matmul(st["cat"], dys, "tn", BF16, "lru_dw_out")
            dz = _matmul(dys, w_lru_out, "nt", BF16, "lru_dz")
            dgl, dul, dgc, duc, dwa, dwx, vec = _lru_bwd(cfg, st["p"], dz, st["h_lat"], st["h_ctx"], lru_consts, "lru_bwd")
            dp = jnp.concatenate([jnp.concatenate([dgl, dul], axis=1), jnp.concatenate([dgc, duc], axis=1)], axis=0)
            dw_in_t = _matmul(dp, st["xin2"], "tn", BF16, "lru_dw_in", bm_cap=1024)
            dxin = _matmul(dp, w_lru_in_t, "nn", BF16, "lru_dx")
            recv_mix = [_exchange_start(part, "exchange_start_lru_%d" % k)
                        for k, part in enumerate((dw_in_t[None], dw_out[None], _as2d(dwa)[None], _as2d(dwx)[None]))]
            vec_t = jnp.moveaxis(vec, 0, 1).reshape(16, D)
            grads["lru_ba"], grads["lru_bx"] = vec_t[0:2], vec_t[2:4]
            grads["lru_lambda"], grads["lru_conv_w"], grads["lru_conv_b"] = vec_t[4:6], vec_t[6:10], vec_t[10:11]
        if l == 0:
            recv_ab = recv_mix
        else:
            recv_lru = recv_mix
        dres, dys, stats = _ln_bwd(cfg, (dres, dxin, ln_b_f[l, 0][None], mods[l], 4), st["xhat1"], st["rstd1"], st["y1"],
                                   mods[l], 2, 0.5, ln_g_f[l, 0][None] + pin(recv_mix), "ln_bwd_fused")
        put_stats(stats, 2, (dm, 3))
        dln_g[l][0], dln_b[l][0] = stats[:, 0, :].sum(0), stats[:, 1, :].sum(0)
        dg, du, a_act, dxin = _ffn_bwd(dys, st["g1"], st["u1"], wf[l][0], "ffn_bwd")
        recv_ffn[l][0] = ffn_weight_grads("%d0" % l, st["xin1"], dg, du, a_act, dys)
        last_sent = recv_ffn[l][0]
        dmod[l] = dm
        dmod_next = dm
        up = (dres, dxin, None, mods[l], 1)
    dh0, stats = _modulate_bwd(cfg, up[0], up[1], h0, mods[0] + pin(last_sent), 1, "modulate_bwd")
    dmod[0][0], dmod[0][1] = stats[:, 4, :], stats[:, 3, :]
    grad_x = dh0.reshape(x.shape)

    dmod_mine = jnp.stack([jnp.stack(dmod[l], axis=1).reshape(3, N_MOD * D) for l in range(2)])
    n_dm = 6 * N_MOD * D // 128
    dmod_sent = _gather_start(dmod_mine.reshape(1, n_dm, 128), "gather_start_dmod")

    def arrived(handle, name):
        return _exchange_wait(handle, dmod_sent[4], name)

    recv_ffn = [[[arrived(hd, "exchange_wait_ffn%d%d_%d" % (l, i, k)) for k, hd in enumerate(recv_ffn[l][i])]
                 for i in range(2)] for l in range(2)]
    recv_ab = [arrived(hd, "exchange_wait_mix_ab_%d" % k) for k, hd in enumerate(recv_ab)]
    recv_lru = [arrived(hd, "exchange_wait_lru_%d" % k) for k, hd in enumerate(recv_lru)]

    def shard_sum(recv, name):
        return _sum_blocks(recv.reshape(N_DEV, recv.shape[2], recv.shape[3]), name)

    gate_g = [[None, None], [None, None]]
    up_g = [[None, None], [None, None]]
    down_g = [[None, None], [None, None]]
    for l in range(2):
        for i in range(2):
            gt, ut, dn = [shard_sum(r, "sum_ffn") for r in recv_ffn[l][i]]
            gate_g[l][i], up_g[l][i], down_g[l][i] = gt.T, ut.T, dn
    grads["ffn_w_gate"] = jnp.stack([jnp.stack(gate_g[l]) for l in range(2)])
    grads["ffn_w_up"] = jnp.stack([jnp.stack(up_g[l]) for l in range(2)])
    grads["ffn_w_down"] = jnp.stack([jnp.stack(down_g[l]) for l in range(2)])
    grads["mix_ab_w_in"] = shard_sum(recv_ab[0], "sum_mix_in").T[None]
    grads["mix_ab_w_out"] = shard_sum(recv_ab[1], "sum_mix_out")[None]
    grads["lru_w_in"] = shard_sum(recv_lru[0], "sum_lru_in").T[None]
    grads["lru_w_out"] = shard_sum(recv_lru[1], "sum_lru_out")[None]
    rep_parts = [shard_sum(recv_lru[2], "sum_rep"), shard_sum(recv_lru[3], "sum_rep"), shard_sum(recv_ab[2], "sum_rep")]
    rep_names = ["lru_wa", "lru_wx", "pool_w"]

    dmod_all = _gather_wait(dmod_sent, rep_parts[2], "gather_wait_dmod").reshape(N_DEV, n_dm, 128)
    dmod_sum = _sum_blocks(dmod_all, "sum_dmod").reshape(2, 3, N_MOD * D)
    dmod_all = dmod_all.reshape(N_DEV, 2, 3, N_MOD * D)
    grads["b_mod"] = dmod_sum[:, 0] + dmod_sum[:, 1] + dmod_sum[:, 2]
    dmod_ex = jnp.moveaxis(dmod_all[:, :, 0:2, :], 1, 0).reshape(2, 2 * N_DEV, N_MOD * D)
    dm_rows = jnp.zeros((2, 32, N_MOD * D), F32).at[:, :16].set(dmod_ex).at[:, 16].set(dmod_sum[:, 2])
    dm_cols = lax.dynamic_slice_in_dim(dm_rows, me * mcols, mcols, axis=2).astype(BF16)
    grads["w_mod"] = jnp.stack([_matmul(s_rows, dm_cols[l], "tn", F32, "mod_dw", bn_cap=1280) for l in range(2)])
    ds_part = None
    for l in range(2):
        part = _matmul(dm_cols[l, 16:32], w_mod[l], "nt", F32, "mod_ds", bk_cap=1280)[0]
        ds_part = part if ds_part is None else ds_part + part

    dln_g_f = jnp.stack([jnp.stack(dln_g[l]) for l in range(2)])
    dln_b_f = jnp.stack([jnp.stack(dln_b[l]) for l in range(2)])
    sink_pad = jnp.zeros((1, 128), F32).at[0, :8].set(grads["attn_sink"][0])
    part_list = [p_.reshape(-1, 128) for p_ in rep_parts] + [
        dln_g_f.reshape(-1, 128), dln_b_f.reshape(-1, 128), grads["lru_conv_w"].reshape(-1, 128),
        grads["lru_conv_b"].reshape(-1, 128), grads["lru_ba"].reshape(-1, 128), grads["lru_bx"].reshape(-1, 128),
        grads["lru_lambda"].reshape(-1, 128), ds_part.reshape(-1, 128), sink_pad, grads["pool_scale"].reshape(-1, 128)]
    parts, part_off = _pack_rows(part_list)
    parts_sent = _gather_start(parts[None], "gather_start_partials")

    delta, new_m, new_v = {}, {}, {}
    for n in ("w_mod", "b_mod", "ffn_w_gate", "ffn_w_up", "ffn_w_down", "mix_ab_w_in", "mix_ab_w_out",
              "lru_w_in", "lru_w_out"):
        grads[n] = grads[n].reshape(weights[n].shape)
        delta[n], new_m[n], new_v[n] = _adamw(weights[n], grads[n], mom_m[n], mom_v[n], "adamw", dep=parts_sent[4])
    parts_all = _gather_wait(parts_sent, delta["lru_w_out"], "gather_wait_partials").reshape(N_DEV, parts.shape[0], 128)
    parts_sum = _sum_blocks(parts_all, "sum_partials")

    for i, n in enumerate(rep_names):
        rows = part_list[i].shape[0]
        grads[n] = parts_all[:, part_off[i]:part_off[i] + rows, :].reshape(weights[n].shape)

    def take(idx):
        return parts_sum[part_off[idx]:part_off[idx] + part_list[idx].shape[0]]

    def my_cols(full, shp):
        w = shp[-1]
        return lax.dynamic_slice_in_dim(full, me * w, w, axis=full.ndim - 1)

    grads["ln_g"] = my_cols(take(3).reshape(2, 3, D), ln_g.shape)
    grads["ln_b"] = my_cols(take(4).reshape(2, 3, D), ln_b.shape)
    grads["lru_conv_w"] = my_cols(take(5).reshape(1, 4, D), lru_conv_w.shape)
    grads["lru_conv_b"] = my_cols(take(6).reshape(1, D), lru_conv_b.shape)
    grads["lru_ba"] = my_cols(take(7).reshape(1, 2, D), lru_ba.shape)
    grads["lru_bx"] = my_cols(take(8).reshape(1, 2, D), lru_bx.shape)
    grads["lru_lambda"] = my_cols(take(9).reshape(1, 2, D), lru_lambda.shape)
    sg = jax.nn.sigmoid(c_ctx)
    grads["c_ctx"] = take(10).reshape(D) * (sg * (1.0 + c_ctx * (1.0 - sg)))
    grads["attn_sink"] = take(11)[:, :8]
    grads["pool_scale"] = take(12).reshape(pool_scale.shape)

    for n in names:
        if n in delta:
            continue
        grads[n] = grads[n].reshape(weights[n].shape)
        delta[n], new_m[n], new_v[n] = _adamw(weights[n], grads[n], mom_m[n], mom_v[n], "adamw")

    return (loss, grad_x, *[grads[n] for n in names], *[delta[n] for n in names],
            *[new_m[n] for n in names], *[new_v[n] for n in names])
```

```python
import functools
import math

import jax
import jax.numpy as jnp
from jax import lax
from jax.experimental import pallas as pl
from jax.experimental.pallas import tpu as pltpu

F32 = jnp.float32
BF16 = jnp.bfloat16
MESH = pl.DeviceIdType.MESH

D = 1024
N_MOD = 9
N_DEV = 8
HEAD_DIM = 64
ATT_HEADS = 8
KV_HEADS = 2
ATT_W = 512
BLK = 128
ATT_SCALE = HEAD_DIM ** -0.5
GRID_W = 64
ROPE_FREQS = HEAD_DIM // 4
ROPE_THETA = 10000.0
POOL_R = (1, 2, 4, 8)
LRU_C = 8.0
LN_EPS = 1e-5
NEG_INF = -1e30
ALPHA = 4.0 ** 0.25
LR, B1, B2, EPS, WD, STEP = 0.001, 0.9, 0.999, 1e-08, 0.01, 10
VMEM_LIMIT = 56 * 1024 * 1024
ROW_TILE = 512


def _params(sem=None):
    if sem is None:
        return pltpu.CompilerParams(vmem_limit_bytes=VMEM_LIMIT)
    return pltpu.CompilerParams(dimension_semantics=sem, vmem_limit_bytes=VMEM_LIMIT)


def _sigmoid(x):
    return 0.5 * jnp.tanh(0.5 * x) + 0.5


def _dot(a, b):
    return jnp.dot(a.astype(BF16), b.astype(BF16), preferred_element_type=F32)


def _dot_nt(a, b):
    return lax.dot_general(a.astype(BF16), b.astype(BF16), (((1,), (1,)), ((), ())), preferred_element_type=F32)


def _dot_tn(a, b):
    return lax.dot_general(a.astype(BF16), b.astype(BF16), (((0,), (0,)), ((), ())), preferred_element_type=F32)


def _pick(n, cap):
    best = None
    for m in range(128, min(n, cap) + 1, 128):
        if n % m == 0:
            best = m
    return n if best is None else best


def _chunks(width, step=256):
    out, c = [], 0
    while c < width:
        w = min(step, width - c)
        out.append((c, w))
        c += w
    return out


class _Cfg:
    def __init__(self, n_lat, n_ctx):
        self.n_lat, self.n_ctx = n_lat, n_ctx
        self.t_lat, self.t_ctx = 2 * n_lat, 2 * n_ctx
        self.T = self.t_lat + self.t_ctx
        self.tm = min(ROW_TILE, self.t_ctx)
        assert n_lat % self.tm == 0 and self.t_ctx % self.tm == 0 and n_lat >= 3 * BLK and n_ctx % BLK == 0
        self.nt = self.T // self.tm
        self.nlt = n_lat // self.tm
        self.ctx_blk = self.t_lat // n_ctx

    def seg(self, i):
        return jnp.minimum(i // self.nlt, 2)

    def first_of_seg(self, i):
        return jnp.where(i < 2 * self.nlt, i % self.nlt == 0, i == 2 * self.nlt)


def _modulate(cfg, h, mod, shift_idx, scale_idx, name):
    tm = cfg.tm

    def body(h_ref, mod_ref, o_ref):
        sh = mod_ref[shift_idx:shift_idx + 1, :]
        sc = mod_ref[scale_idx:scale_idx + 1, :]
        o_ref[...] = (h_ref[...] * (1.0 + sc) + sh).astype(BF16)

    return pl.pallas_call(
        body, grid=(cfg.nt,), name=name,
        in_specs=[pl.BlockSpec((tm, D), lambda i: (i, 0)),
                  pl.BlockSpec((None, N_MOD, D), lambda i: (cfg.seg(i), 0, 0))],
        out_specs=pl.BlockSpec((tm, D), lambda i: (i, 0)),
        out_shape=jax.ShapeDtypeStruct((cfg.T, D), BF16),
        compiler_params=_params(("parallel",)),
    )(h, mod)


def _ln_fwd(cfg, h, y, mod, gate_idx, coef, lng, lnb, mod_next, next_idx, name):
    tm = cfg.tm
    has_next = next_idx is not None

    def body(*refs):
        if has_next:
            h_ref, y_ref, mod_ref, g_ref, b_ref, modn_ref, hn_ref, xhat_ref, rstd_ref, xin_ref = refs
        else:
            h_ref, y_ref, mod_ref, g_ref, b_ref, hn_ref, xhat_ref, rstd_ref = refs
        gate = mod_ref[gate_idx:gate_idx + 1, :]
        z = ALPHA * h_ref[...] + (coef * gate) * y_ref[...].astype(F32)
        mu = jnp.mean(z, axis=-1, keepdims=True)
        zc = z - mu
        var = jnp.mean(zc * zc, axis=-1, keepdims=True)
        rstd = lax.rsqrt(var + LN_EPS)
        xhat = zc * rstd
        hn = xhat * g_ref[...] + b_ref[...]
        hn_ref[...] = hn
        xhat_ref[...] = xhat.astype(BF16)
        rstd_ref[...] = rstd
        if has_next:
            sh = modn_ref[next_idx[0]:next_idx[0] + 1, :]
            sc = modn_ref[next_idx[1]:next_idx[1] + 1, :]
            xin_ref[...] = (hn * (1.0 + sc) + sh).astype(BF16)

    row = pl.BlockSpec((tm, D), lambda i: (i, 0))
    modspec = pl.BlockSpec((None, N_MOD, D), lambda i: (cfg.seg(i), 0, 0))
    vec = pl.BlockSpec((1, D), lambda i: (0, 0))
    in_specs = [row, row, modspec, vec, vec]
    args = [h, y, mod, lng, lnb]
    out_specs = [row, row, pl.BlockSpec((tm, 1), lambda i: (i, 0))]
    out_shape = [jax.ShapeDtypeStruct((cfg.T, D), F32), jax.ShapeDtypeStruct((cfg.T, D), BF16),
                 jax.ShapeDtypeStruct((cfg.T, 1), F32)]
    if has_next:
        in_specs.append(modspec)
        args.append(mod_next)
        out_specs.append(row)
        out_shape.append(jax.ShapeDtypeStruct((cfg.T, D), BF16))
    return pl.pallas_call(body, grid=(cfg.nt,), name=name, in_specs=in_specs, out_specs=out_specs,
                          out_shape=out_shape, compiler_params=_params(("parallel",)))(*args)


def _ln_bwd(cfg, up, xhat, rstd, y, mod, gate_idx, coef, lng, name):
    tm = cfg.tm
    fused = len(up) > 1
    scale_next = up[4] if fused else None

    def body(*refs):
        if fused:
            dres_n, dxin_n, b_ref, modn_ref, xhat_ref, rstd_ref, y_ref, mod_ref, g_ref, dres_ref, dys_ref, st_ref = refs
        else:
            dhn_ref, xhat_ref, rstd_ref, y_ref, mod_ref, g_ref, dres_ref, dys_ref, st_ref = refs
        i = pl.program_id(0)

        @pl.when(cfg.first_of_seg(i))
        def _():
            st_ref[...] = jnp.zeros_like(st_ref)

        xhat = xhat_ref[...].astype(F32)
        if fused:
            dxin = dxin_n[...].astype(F32)
            sc = modn_ref[scale_next:scale_next + 1, :]
            dhn = dres_n[...] + dxin * (1.0 + sc)
            shift_sum = jnp.sum(dxin, axis=0, keepdims=True)
            st_ref[3:4, :] += g_ref[...] * jnp.sum(dxin * xhat, axis=0, keepdims=True) + b_ref[...] * shift_sum
            st_ref[4:5, :] += shift_sum
        else:
            dhn = dhn_ref[...]
        gdh = dhn * g_ref[...]
        m1 = jnp.mean(gdh, axis=-1, keepdims=True)
        m2 = jnp.mean(gdh * xhat, axis=-1, keepdims=True)
        dz = rstd_ref[...] * (gdh - m1 - xhat * m2)
        gate = mod_ref[gate_idx:gate_idx + 1, :]
        dres_ref[...] = ALPHA * dz
        dys_ref[...] = ((coef * gate) * dz).astype(BF16)
        st_ref[0:1, :] += jnp.sum(dhn * xhat, axis=0, keepdims=True)
        st_ref[1:2, :] += jnp.sum(dhn, axis=0, keepdims=True)
        st_ref[2:3, :] += jnp.sum((coef * dz) * y_ref[...].astype(F32), axis=0, keepdims=True)

    row = pl.BlockSpec((tm, D), lambda i: (i, 0))
    modspec = pl.BlockSpec((None, N_MOD, D), lambda i: (cfg.seg(i), 0, 0))
    vec = pl.BlockSpec((1, D), lambda i: (0, 0))
    col = pl.BlockSpec((tm, 1), lambda i: (i, 0))
    if fused:
        in_specs = [row, row, vec, modspec, row, col, row, modspec, vec]
        args = [up[0], up[1], up[2], up[3], xhat, rstd, y, mod, lng]
    else:
        in_specs = [row, row, col, row, modspec, vec]
        args = [up[0], xhat, rstd, y, mod, lng]
    return pl.pallas_call(
        body, grid=(cfg.nt,), name=name, in_specs=in_specs,
        out_specs=[row, row, pl.BlockSpec((None, 8, D), lambda i: (cfg.seg(i), 0, 0))],
        out_shape=[jax.ShapeDtypeStruct((cfg.T, D), F32), jax.ShapeDtypeStruct((cfg.T, D), BF16),
                   jax.ShapeDtypeStruct((3, 8, D), F32)],
        compiler_params=_params(("arbitrary",)))(*args)


def _modulate_bwd(cfg, dres, dxin, h, mod, scale_idx, name):
    tm = cfg.tm
    n_lt = 2 * cfg.nlt

    def body(dres_ref, dxin_ref, h_ref, mod_ref, dh_ref, st_ref):
        i = pl.program_id(0)

        @pl.when(cfg.first_of_seg(i))
        def _():
            st_ref[...] = jnp.zeros_like(st_ref)

        dxin = dxin_ref[...].astype(F32)
        sc = mod_ref[scale_idx:scale_idx + 1, :]

        @pl.when(i < n_lt)
        def _():
            dh_ref[...] = dres_ref[...] + dxin * (1.0 + sc)

        st_ref[3:4, :] += jnp.sum(dxin * h_ref[...], axis=0, keepdims=True)
        st_ref[4:5, :] += jnp.sum(dxin, axis=0, keepdims=True)

    row = pl.BlockSpec((tm, D), lambda i: (i, 0))
    return pl.pallas_call(
        body, grid=(cfg.nt,), name=name,
        in_specs=[row, row, row, pl.BlockSpec((None, N_MOD, D), lambda i: (cfg.seg(i), 0, 0))],
        out_specs=[pl.BlockSpec((tm, D), lambda i: (jnp.minimum(i, n_lt - 1), 0)),
                   pl.BlockSpec((None, 8, D), lambda i: (cfg.seg(i), 0, 0))],
        out_shape=[jax.ShapeDtypeStruct((cfg.t_lat, D), F32), jax.ShapeDtypeStruct((3, 8, D), F32)],
        compiler_params=_params(("arbitrary",)))(dres, dxin, h, mod)


def _loss(cfg, h, target, name):
    tm = cfg.tm
    n_lt = 2 * cfg.nlt

    def body(h_ref, t_ref, dy_ref, l_ref):
        i = pl.program_id(0)

        @pl.when(i == 0)
        def _():
            l_ref[...] = jnp.zeros_like(l_ref)

        @pl.when(i < n_lt)
        def _():
            err = h_ref[...] - t_ref[...]
            dy_ref[...] = err * (1.0 / D)
            part = jnp.sum(jnp.sum(err * err, axis=1, keepdims=True), axis=0, keepdims=True) * (0.5 / D)
            l_ref[...] += jnp.broadcast_to(part, l_ref.shape)

        @pl.when(i >= n_lt)
        def _():
            dy_ref[...] = jnp.zeros_like(dy_ref)

    return pl.pallas_call(
        body, grid=(cfg.nt,), name=name,
        in_specs=[pl.BlockSpec((tm, D), lambda i: (i, 0)),
                  pl.BlockSpec((tm, D), lambda i: (jnp.minimum(i, n_lt - 1), 0))],
        out_specs=[pl.BlockSpec((tm, D), lambda i: (i, 0)), pl.BlockSpec((8, 128), lambda i: (0, 0))],
        out_shape=[jax.ShapeDtypeStruct((cfg.T, D), F32), jax.ShapeDtypeStruct((8, 128), F32)],
        compiler_params=_params(("arbitrary",)))(h, target)


def _matmul(a, b, mode, out_dtype, name, bm_cap=1536, bn_cap=1408, bk_cap=1024):
    if mode == "nn":
        (M, K), N = a.shape, b.shape[1]
    elif mode == "nt":
        (M, K), N = a.shape, b.shape[0]
    else:
        (K, M), N = a.shape, b.shape[1]
    bm, bn, bk = _pick(M, bm_cap), _pick(N, bn_cap), _pick(K, bk_cap)
    nk = K // bk

    def body(a_ref, b_ref, o_ref, acc_ref=None):
        k = pl.program_id(2)
        if mode == "nn":
            part = _dot(a_ref[...], b_ref[...])
        elif mode == "nt":
            part = _dot_nt(a_ref[...], b_ref[...])
        else:
            part = _dot_tn(a_ref[...], b_ref[...])
        if nk == 1:
            o_ref[...] = part.astype(out_dtype)
            return

        @pl.when(k == 0)
        def _():
            acc_ref[...] = part

        @pl.when((k > 0) & (k < nk - 1))
        def _():
            acc_ref[...] += part

        @pl.when(k == nk - 1)
        def _():
            o_ref[...] = (acc_ref[...] + part).astype(out_dtype)

    if mode == "nn":
        a_spec = pl.BlockSpec((bm, bk), lambda i, j, k: (i, k))
        b_spec = pl.BlockSpec((bk, bn), lambda i, j, k: (k, j))
    elif mode == "nt":
        a_spec = pl.BlockSpec((bm, bk), lambda i, j, k: (i, k))
        b_spec = pl.BlockSpec((bn, bk), lambda i, j, k: (j, k))
    else:
        a_spec = pl.BlockSpec((bk, bm), lambda i, j, k: (k, i))
        b_spec = pl.BlockSpec((bk, bn), lambda i, j, k: (k, j))
    return pl.pallas_call(
        body, grid=(M // bm, N // bn, nk), name=name, in_specs=[a_spec, b_spec],
        out_specs=pl.BlockSpec((bm, bn), lambda i, j, k: (i, j)),
        out_shape=jax.ShapeDtypeStruct((M, N), out_dtype),
        scratch_shapes=[pltpu.VMEM((bm, bn), F32)] if nk > 1 else [],
        compiler_params=_params(("parallel", "parallel", "arbitrary")))(a, b)


def _ffn_tile(T, cap):
    best = 256
    for t in range(256, cap + 1, 256):
        if T % t == 0:
            best = t
    return best


def _ffn_fwd(xin, wf, name):
    T = xin.shape[0]
    F = wf.shape[1]
    tm, tf = _ffn_tile(T, 768), F // 2
    assert tf % 128 == 0 and T % tm == 0

    def body(x_ref, wg_ref, wu_ref, wd_ref, g_ref, u_ref, y_ref, acc_ref):
        j = pl.program_id(1)
        x = x_ref[...]
        acc = None
        for c0, cw in _chunks(tf):
            g = _dot_nt(x, wg_ref[c0:c0 + cw, :])
            u = _dot_nt(x, wu_ref[c0:c0 + cw, :])
            g_ref[:, c0:c0 + cw] = g.astype(BF16)
            u_ref[:, c0:c0 + cw] = u.astype(BF16)
            part = _dot(g * _sigmoid(g) * u, wd_ref[c0:c0 + cw, :])
            acc = part if acc is None else acc + part

        @pl.when(j == 0)
        def _():
            acc_ref[...] = acc

        @pl.when(j == 1)
        def _():
            y_ref[...] = (acc_ref[...] + acc).astype(BF16)

    return pl.pallas_call(
        body, grid=(T // tm, 2), name=name,
        in_specs=[pl.BlockSpec((tm, D), lambda i, j: (i, 0)),
                  pl.BlockSpec((None, tf, D), lambda i, j: (0, j, 0)),
                  pl.BlockSpec((None, tf, D), lambda i, j: (1, j, 0)),
                  pl.BlockSpec((None, tf, D), lambda i, j: (2, j, 0))],
        out_specs=[pl.BlockSpec((tm, tf), lambda i, j: (i, j)),
                   pl.BlockSpec((tm, tf), lambda i, j: (i, j)),
                   pl.BlockSpec((tm, D), lambda i, j: (i, 0))],
        out_shape=[jax.ShapeDtypeStruct((T, F), BF16), jax.ShapeDtypeStruct((T, F), BF16),
                   jax.ShapeDtypeStruct((T, D), BF16)],
        scratch_shapes=[pltpu.VMEM((tm, D), F32)],
        compiler_params=_params(("parallel", "arbitrary")))(xin, wf, wf, wf)


def _ffn_bwd(dys, g, u, wf, name):
    T = dys.shape[0]
    F = wf.shape[1]
    tm, tf = _ffn_tile(T, 512), F // 2

    def body(dy_ref, g_ref, u_ref, wg_ref, wu_ref, wd_ref, dg_ref, du_ref, a_ref, dx_ref, acc_ref):
        j = pl.program_id(1)
        da_all = _dot_nt(dy_ref[...], wd_ref[...])
        for c0, cw in _chunks(tf):
            gg = g_ref[:, c0:c0 + cw].astype(F32)
            uu = u_ref[:, c0:c0 + cw].astype(F32)
            da = da_all[:, c0:c0 + cw]
            s = _sigmoid(gg)
            silu = gg * s
            a_ref[:, c0:c0 + cw] = (silu * uu).astype(BF16)
            du_ref[:, c0:c0 + cw] = (da * silu).astype(BF16)
            dg_ref[:, c0:c0 + cw] = (da * uu * (s * (1.0 + gg * (1.0 - s)))).astype(BF16)
        acc = _dot(dg_ref[...], wg_ref[...]) + _dot(du_ref[...], wu_ref[...])

        @pl.when(j == 0)
        def _():
            acc_ref[...] = acc

        @pl.when(j == 1)
        def _():
            dx_ref[...] = (acc_ref[...] + acc).astype(BF16)

    blk = pl.BlockSpec((tm, tf), lambda i, j: (i, j))
    return pl.pallas_call(
        body, grid=(T // tm, 2), name=name,
        in_specs=[pl.BlockSpec((tm, D), lambda i, j: (i, 0)), blk, blk,
                  pl.BlockSpec((None, tf, D), lambda i, j: (0, j, 0)),
                  pl.BlockSpec((None, tf, D), lambda i, j: (1, j, 0)),
                  pl.BlockSpec((None, tf, D), lambda i, j: (2, j, 0))],
        out_specs=[blk, blk, blk, pl.BlockSpec((tm, D), lambda i, j: (i, 0))],
        out_shape=[jax.ShapeDtypeStruct((T, F), BF16), jax.ShapeDtypeStruct((T, F), BF16),
                   jax.ShapeDtypeStruct((T, F), BF16), jax.ShapeDtypeStruct((T, D), BF16)],
        scratch_shapes=[pltpu.VMEM((tm, D), F32)],
        compiler_params=_params(("parallel", "arbitrary")))(dys, g, u, wf, wf, wf)


def _swap_halves(x):
    w = x.shape[1]
    lane = lax.broadcasted_iota(jnp.int32, (1, w), 1)
    return jnp.where((lane & 63) < 32, pltpu.roll(x, w - 32, 1), pltpu.roll(x, 32, 1))


def _rope(x, cos, sin):
    return x * cos + _swap_halves(x) * sin


def _rope_t(dy, cos, sin):
    return dy * cos + _swap_halves(dy * sin)


def _rope_tables(n_lat):
    rows = n_lat // GRID_W
    row = jnp.repeat(jnp.arange(rows, dtype=F32), GRID_W)
    col = jnp.tile(jnp.arange(GRID_W, dtype=F32), rows)
    inv = ROPE_THETA ** (-jnp.arange(ROPE_FREQS, dtype=F32) / ROPE_FREQS)
    ang = jnp.concatenate([row[:, None] * inv, col[:, None] * inv], axis=-1)
    cs, sn = jnp.cos(ang), jnp.sin(ang)
    cos = jnp.concatenate([cs, cs, cs, cs], axis=-1)
    sin = jnp.concatenate([-sn, sn, -sn, sn], axis=-1)
    return cos, sin


def _attn_specs(cfg):
    n_lat, n_ctx, cb = cfg.n_lat, cfg.n_ctx, cfg.ctx_blk
    return [pl.BlockSpec((n_lat, ATT_W), lambda e: (e, 0)),
            pl.BlockSpec((n_lat, 128), lambda e: (e, 4)),
            pl.BlockSpec((n_lat, 128), lambda e: (e, 5)),
            pl.BlockSpec((n_ctx, ATT_W), lambda e: (cb + e, 0)),
            pl.BlockSpec((n_ctx, 128), lambda e: (cb + e, 4)),
            pl.BlockSpec((n_ctx, 128), lambda e: (cb + e, 5)),
            pl.BlockSpec((n_lat, 128), lambda e: (0, 0)),
            pl.BlockSpec((n_lat, 128), lambda e: (0, 0)),
            pl.BlockSpec((8, 128), lambda e: (0, 0))]


def _attn_prepare(kh, kl, vl, kc, vc, ka, kb, va, vb, kca, kcb, vca, vcb):
    lane = lax.broadcasted_iota(jnp.int32, (1, 128), 1)
    own = (lane < 64) if kh == 0 else (lane >= 64)

    def split(x, ra, rb):
        mine = jnp.where(own, x, 0.0)
        other = pltpu.roll(mine, 64, 1)
        a, b = (mine, other) if kh == 0 else (other, mine)
        ra[...] = a.astype(BF16)
        rb[...] = b.astype(BF16)

    split(kl, ka, kb)
    split(vl, va, vb)
    split(kc, kca, kcb)
    split(vc, vca, vcb)


def _softmax_parts(s_list, sk):
    m = sk
    for s in s_list:
        m = jnp.maximum(m, jnp.max(s, axis=1, keepdims=True))
    es = [jnp.exp(s - m) for s in s_list]
    esk = jnp.exp(sk - m)
    den = esk
    for e in es:
        den = den + jnp.sum(e, axis=1, keepdims=True)
    inv = 1.0 / den
    return [e * inv for e in es], esk * inv


def _window(cfg, n):
    r0 = pl.multiple_of(n * BLK, BLK)
    start = pl.multiple_of(jnp.clip((n - 1) * BLK, 0, cfg.n_lat - 3 * BLK), BLK)
    qpos = r0 + lax.broadcasted_iota(jnp.int32, (BLK, 1), 0)
    kpos = start + lax.broadcasted_iota(jnp.int32, (1, 3 * BLK), 1)
    valid = jnp.abs(qpos - kpos) <= BLK
    return r0, start, valid


def _attn_fwd(cfg, p, cos, sin, sink_rows, name):
    n_lat, n_ctx = cfg.n_lat, cfg.n_ctx

    def body(q_ref, k_ref, v_ref, qc_ref, kc_ref, vc_ref, cos_ref, sin_ref, sink_ref, o_ref, oc_ref,
             qr, ka, kb, va, vb, kca, kcb, vca, vcb):
        cos_t, sin_t = cos_ref[...], sin_ref[...]
        for gq in range(4):
            qr[:, gq * 128:(gq + 1) * 128] = _rope(q_ref[:, gq * 128:(gq + 1) * 128].astype(F32), cos_t, sin_t).astype(BF16)
        kl = _rope(k_ref[...].astype(F32), cos_t, sin_t)
        for kh in range(KV_HEADS):
            _attn_prepare(kh, kl, v_ref[...].astype(F32), kc_ref[...].astype(F32), vc_ref[...].astype(F32),
                          ka, kb, va, vb, kca, kcb, vca, vcb)

            def lat_block(n, carry):
                r0, start, valid_all = _window(cfg, n)
                win = pl.ds(start, 3 * BLK)
                lanes = [slice((kh * 2 + pr) * 128, (kh * 2 + pr + 1) * 128) for pr in range(2)]
                kws, kcs = (ka[win, :], kb[win, :]), (kca[...], kcb[...])
                vws, vcs = (va[win, :], vb[win, :]), (vca[...], vcb[...])
                half_blk = BLK // 2
                for sub in range(2):
                    rows = pl.ds(pl.multiple_of(r0 + sub * half_blk, half_blk), half_blk)
                    valid = valid_all[sub * half_blk:(sub + 1) * half_blk, :]
                    qps = [qr[rows, lanes[pr]] for pr in range(2)]
                    scores = [(jnp.where(valid, _dot_nt(qps[pr], kws[half]) * ATT_SCALE, NEG_INF),
                               _dot_nt(qps[pr], kcs[half]) * ATT_SCALE) for pr in range(2) for half in range(2)]
                    probs = []
                    for idx, (s_w, s_c) in enumerate(scores):
                        head = kh * 4 + idx
                        (p_w, p_c), _ = _softmax_parts([s_w, s_c], sink_ref[head:head + 1, 0:1])
                        probs.append((p_w.astype(BF16), p_c.astype(BF16)))
                    for pr in range(2):
                        o = (_dot(probs[2 * pr][0], vws[0]) + _dot(probs[2 * pr][1], vcs[0])
                             + _dot(probs[2 * pr + 1][0], vws[1]) + _dot(probs[2 * pr + 1][1], vcs[1]))
                        o_ref[rows, lanes[pr]] = o.astype(BF16)
                return carry

            lax.fori_loop(0, n_lat // BLK, lat_block, 0, unroll=2)
            for n in range(n_ctx // BLK):
                rows = slice(n * BLK, (n + 1) * BLK)
                for pr in range(2):
                    lanes = slice((kh * 2 + pr) * 128, (kh * 2 + pr + 1) * 128)
                    qp = qc_ref[rows, lanes]
                    o = None
                    for half, (kcx, vcx) in enumerate(((kca, vca), (kcb, vcb))):
                        head = kh * 4 + pr * 2 + half
                        s_c = _dot_nt(qp, kcx[...]) * ATT_SCALE
                        (p_c,), _ = _softmax_parts([s_c], sink_ref[head:head + 1, 0:1])
                        part = _dot(p_c, vcx[...])
                        o = part if o is None else o + part
                    oc_ref[rows, lanes] = o.astype(BF16)

    return pl.pallas_call(
        body, grid=(2,), name=name, in_specs=_attn_specs(cfg),
        out_specs=[pl.BlockSpec((n_lat, ATT_W), lambda e: (e, 0)), pl.BlockSpec((n_ctx, ATT_W), lambda e: (e, 0))],
        out_shape=[jax.ShapeDtypeStruct((cfg.t_lat, ATT_W), BF16), jax.ShapeDtypeStruct((cfg.t_ctx, ATT_W), BF16)],
        scratch_shapes=[pltpu.VMEM((n_lat, ATT_W), BF16)] + [pltpu.VMEM((n_lat, 128), BF16)] * 4
        + [pltpu.VMEM((n_ctx, 128), BF16)] * 4,
        compiler_params=_params(("parallel",)))(p, p, p, p, p, p, cos, sin, sink_rows)


def _attn_bwd(cfg, p, dcat, cos, sin, sink_rows, name):
    n_lat, n_ctx, cb = cfg.n_lat, cfg.n_ctx, cfg.ctx_blk

    def body(q_ref, k_ref, v_ref, qc_ref, kc_ref, vc_ref, cos_ref, sin_ref, sink_ref, do_ref, doc_ref,
             dq_ref, dk_ref, dv_ref, dqc_ref, dkc_ref, dvc_ref, dsink_ref,
             qr, ka, kb, va, vb, kca, kcb, vca, vcb, dqs, dka, dva, dkca, dvca):
        cos_t, sin_t = cos_ref[...], sin_ref[...]
        lane = lax.broadcasted_iota(jnp.int32, (1, 128), 1)
        lo = lane < 64
        for gq in range(4):
            qr[:, gq * 128:(gq + 1) * 128] = _rope(q_ref[:, gq * 128:(gq + 1) * 128].astype(F32), cos_t, sin_t).astype(BF16)
        kl = _rope(k_ref[...].astype(F32), cos_t, sin_t)
        dsink_ref[...] = jnp.zeros_like(dsink_ref)
        dka[...] = jnp.zeros_like(dka)
        dva[...] = jnp.zeros_like(dva)
        dkca[...] = jnp.zeros_like(dkca)
        dvca[...] = jnp.zeros_like(dvca)

        def halves(x):
            return jnp.where(lo, x, 0).astype(BF16), jnp.where(lo, 0, x).astype(BF16)

        for kh in range(KV_HEADS):
            _attn_prepare(kh, kl, v_ref[...].astype(F32), kc_ref[...].astype(F32), vc_ref[...].astype(F32),
                          ka, kb, va, vb, kca, kcb, vca, vcb)

            def one_head(head, qp, q_half, do_p, do_half, kw, kcx, vw, vcx, win, valid):
                sk = sink_ref[head:head + 1, 0:1]
                s_list = [_dot_nt(qp, kcx[...]) * ATT_SCALE]
                if win is not None:
                    s_list.insert(0, jnp.where(valid, _dot_nt(qp, kw[win, :]) * ATT_SCALE, NEG_INF))
                probs, p_sink = _softmax_parts(s_list, sk)
                vals = [vcx[...]] if win is None else [vw[win, :], vcx[...]]
                dps = [_dot_nt(do_p, vv) for vv in vals]
                dr = None
                for pp, dp in zip(probs, dps):
                    t = jnp.sum(pp * dp, axis=1, keepdims=True)
                    dr = t if dr is None else dr + t
                dss = [(pp * (dp - dr) * ATT_SCALE).astype(BF16) for pp, dp in zip(probs, dps)]
                dsink_ref[head:head + 1, :] += jnp.broadcast_to(
                    jnp.sum(-p_sink * dr, axis=0, keepdims=True), (1, 128))
                p_c, ds_c = probs[-1], dss[-1]
                dq = _dot(ds_c, kcx[...])
                dkca[kh] += _dot_tn(ds_c, q_half)
                dvca[kh] += _dot_tn(p_c, do_half)
                if win is not None:
                    dq = dq + _dot(dss[0], kw[win, :])
                    dka[kh, win, :] += _dot_tn(dss[0], q_half)
                    dva[kh, win, :] += _dot_tn(probs[0], do_half)
                return dq

            def lat_block(n, carry):
                r0, start, valid = _window(cfg, n)
                win = pl.ds(start, 3 * BLK)
                lanes = [slice((kh * 2 + pr) * 128, (kh * 2 + pr + 1) * 128) for pr in range(2)]
                qps = [qr[pl.ds(r0, BLK), lanes[pr]] for pr in range(2)]
                dops = [do_ref[pl.ds(r0, BLK), lanes[pr]].astype(BF16) for pr in range(2)]
                heads = [(pr, half) for pr in range(2) for half in range(2)]
                kws, kcs = (ka[win, :], kb[win, :]), (kca[...], kcb[...])
                vws, vcs = (va[win, :], vb[win, :]), (vca[...], vcb[...])
                soft = []
                for idx, (pr, half) in enumerate(heads):
                    s_w = jnp.where(valid, _dot_nt(qps[pr], kws[half]) * ATT_SCALE, NEG_INF)
                    s_c = _dot_nt(qps[pr], kcs[half]) * ATT_SCALE
                    soft.append(_softmax_parts([s_w, s_c], sink_ref[kh * 4 + idx:kh * 4 + idx + 1, 0:1]))
                dps = [(_dot_nt(dops[pr], vws[half]), _dot_nt(dops[pr], vcs[half])) for pr, half in heads]
                ds_w, ds_c, pb_w, pb_c = [], [], [], []
                for idx in range(4):
                    (p_w, p_c), p_sink = soft[idx]
                    dp_w, dp_c = dps[idx]
                    dr = jnp.sum(p_w * dp_w, axis=1, keepdims=True) + jnp.sum(p_c * dp_c, axis=1, keepdims=True)
                    ds_w.append((p_w * (dp_w - dr) * ATT_SCALE).astype(BF16))
                    ds_c.append((p_c * (dp_c - dr) * ATT_SCALE).astype(BF16))
                    pb_w.append(p_w.astype(BF16))
                    pb_c.append(p_c.astype(BF16))
                    head = kh * 4 + idx
                    dsink_ref[head:head + 1, :] += jnp.broadcast_to(
                        jnp.sum(-p_sink * dr, axis=0, keepdims=True), (1, 128))
                for pr in range(2):
                    dqs[pl.ds(r0, BLK), lanes[pr]] = (
                        _dot(ds_w[2 * pr], kws[0]) + _dot(ds_c[2 * pr], kcs[0])
                        + _dot(ds_w[2 * pr + 1], kws[1]) + _dot(ds_c[2 * pr + 1], kcs[1]))
                q_hs, do_hs = [halves(qp) for qp in qps], [halves(do_p) for do_p in dops]
                q_stack = jnp.concatenate([q_hs[pr][half] for pr, half in heads], axis=0)
                do_stack = jnp.concatenate([do_hs[pr][half] for pr, half in heads], axis=0)
                dka[kh, win, :] += _dot_tn(jnp.concatenate(ds_w, axis=0), q_stack)
                dva[kh, win, :] += _dot_tn(jnp.concatenate(pb_w, axis=0), do_stack)
                dkca[kh] += _dot_tn(jnp.concatenate(ds_c, axis=0), q_stack)
                dvca[kh] += _dot_tn(jnp.concatenate(pb_c, axis=0), do_stack)
                return carry

            lax.fori_loop(0, n_lat // BLK, lat_block, 0, unroll=2)
            for n in range(n_ctx // BLK):
                rows = slice(n * BLK, (n + 1) * BLK)
                for pr in range(2):
                    lanes = slice((kh * 2 + pr) * 128, (kh * 2 + pr + 1) * 128)
                    qp = qc_ref[rows, lanes].astype(BF16)
                    do_p = doc_ref[rows, lanes]
                    q_h, do_h = halves(qp), halves(do_p)
                    dq = None
                    for half, (kcx, vcx) in enumerate(((kca, vca), (kcb, vcb))):
                        part = one_head(kh * 4 + pr * 2 + half, qp, q_h[half], do_p, do_h[half],
                                        None, kcx, None, vcx, None, None)
                        dq = part if dq is None else dq + part
                    dqc_ref[rows, lanes] = dq.astype(BF16)

        def fold(acc):
            r0 = acc[0] + pltpu.roll(acc[0], 64, 1)
            r1 = acc[1] + pltpu.roll(acc[1], 64, 1)
            return jnp.where(lo, r0, r1)

        for gq in range(4):
            sl = slice(gq * 128, (gq + 1) * 128)
            dq_ref[:, sl] = _rope_t(dqs[:, sl], cos_t, sin_t).astype(BF16)
        dk_ref[...] = _rope_t(fold(dka), cos_t, sin_t).astype(BF16)
        dv_ref[...] = fold(dva).astype(BF16)
        dkc_ref[...] = fold(dkca).astype(BF16)
        dvc_ref[...] = fold(dvca).astype(BF16)

    lat = lambda w: pl.BlockSpec((n_lat, w), lambda e: (e, 0))
    ctx = lambda w: pl.BlockSpec((n_ctx, w), lambda e: (e, 0))
    sd = jax.ShapeDtypeStruct
    return pl.pallas_call(
        body, grid=(2,), name=name,
        in_specs=_attn_specs(cfg) + [pl.BlockSpec((n_lat, ATT_W), lambda e: (e, 0)),
                                     pl.BlockSpec((n_ctx, ATT_W), lambda e: (cb + e, 0))],
        out_specs=[lat(ATT_W), lat(128), lat(128), ctx(ATT_W), ctx(128), ctx(128),
                   pl.BlockSpec((None, 8, 128), lambda e: (e, 0, 0))],
        out_shape=[sd((cfg.t_lat, ATT_W), BF16), sd((cfg.t_lat, 128), BF16), sd((cfg.t_lat, 128), BF16),
                   sd((cfg.t_ctx, ATT_W), BF16), sd((cfg.t_ctx, 128), BF16), sd((cfg.t_ctx, 128), BF16),
                   sd((2, 8, 128), F32)],
        scratch_shapes=[pltpu.VMEM((n_lat, ATT_W), BF16)] + [pltpu.VMEM((n_lat, 128), BF16)] * 4
        + [pltpu.VMEM((n_ctx, 128), BF16)] * 4
        + [pltpu.VMEM((n_lat, ATT_W), F32), pltpu.VMEM((2, n_lat, 128), F32), pltpu.VMEM((2, n_lat, 128), F32),
           pltpu.VMEM((2, n_ctx, 128), F32), pltpu.VMEM((2, n_ctx, 128), F32)],
        compiler_params=_params(("parallel",)))(p, p, p, p, p, p, cos, sin, sink_rows, dcat, dcat)


def _shift_down(x, k, row):
    return jnp.where(row >= k, pltpu.roll(x, k, 0), 0.0)


def _shift_up(x, k, row):
    n = x.shape[0]
    return jnp.where(row < n - k, pltpu.roll(x, n - k, 0), 0.0)


def _window_sum(x, r, row):
    below, above, k = x, x, 1
    while k < r:
        below = below + _shift_down(below, k, row)
        above = above + _shift_up(above, k, row)
        k *= 2
    return below + _shift_down(x, r, row) + _shift_up(above, 1, row)


def _inv_count(r, row, n):
    cnt = jnp.minimum(row + r, n - 1) + 1 - jnp.maximum(row - r, 0)
    return 1.0 / cnt.astype(F32)


def _pool_fwd(p, w, scale, n, blk0, n_seg, name):
    def body(u0, u1, u2, u3, w_ref, sc_ref, o_ref):
        row = lax.broadcasted_iota(jnp.int32, (n, 1), 0)
        for g, u_ref in enumerate((u0, u1, u2, u3)):
            u = u_ref[...].astype(F32)
            d = _window_sum(u, POOL_R[g], row) * _inv_count(POOL_R[g], row, n) - u
            o_ref[:, g * 128:(g + 1) * 128] = (_dot(d, w_ref[g]) * sc_ref[:, g * 128:(g + 1) * 128]).astype(BF16)

    return pl.pallas_call(
        body, grid=(n_seg,), name=name,
        in_specs=[pl.BlockSpec((n, 128), functools.partial(lambda g, e: (blk0 + e, 6 + g), g)) for g in range(4)]
        + [pl.BlockSpec((4, 128, 128), lambda e: (0, 0, 0)), pl.BlockSpec((1, 512), lambda e: (0, 0))],
        out_specs=pl.BlockSpec((n, 512), lambda e: (e, 0)),
        out_shape=jax.ShapeDtypeStruct((n_seg * n, 512), BF16),
        compiler_params=_params(("parallel",)))(p, p, p, p, w, scale)


def _pool_bwd(p, w, scale, dcat, n, blk0, n_seg, name):
    def body(u0, u1, u2, u3, w_ref, sc_ref, dp_ref, du_ref, dw_ref, dsc_ref):
        e = pl.program_id(0)

        @pl.when(e == 0)
        def _():
            dw_ref[...] = jnp.zeros_like(dw_ref)
            dsc_ref[...] = jnp.zeros_like(dsc_ref)

        row = lax.broadcasted_iota(jnp.int32, (n, 1), 0)
        for g, u_ref in enumerate((u0, u1, u2, u3)):
            sl = slice(g * 128, (g + 1) * 128)
            u = u_ref[...].astype(F32)
            inv = _inv_count(POOL_R[g], row, n)
            d = _window_sum(u, POOL_R[g], row) * inv - u
            dp = dp_ref[:, sl].astype(F32)
            dsc_ref[:, sl] += jnp.sum(dp * _dot(d, w_ref[g]), axis=0, keepdims=True)
            dyp = dp * sc_ref[:, sl]
            dw_ref[g] += _dot_tn(d, dyp)
            dd = _dot_nt(dyp, w_ref[g])
            du_ref[:, sl] = (_window_sum(dd * inv, POOL_R[g], row) - dd).astype(BF16)

    return pl.pallas_call(
        body, grid=(n_seg,), name=name,
        in_specs=[pl.BlockSpec((n, 128), functools.partial(lambda g, e: (blk0 + e, 6 + g), g)) for g in range(4)]
        + [pl.BlockSpec((4, 128, 128), lambda e: (0, 0, 0)), pl.BlockSpec((1, 512), lambda e: (0, 0)),
           pl.BlockSpec((n, 512), lambda e: (blk0 + e, 1))],
        out_specs=[pl.BlockSpec((n, 512), lambda e: (e, 0)),
                   pl.BlockSpec((4, 128, 128), lambda e: (0, 0, 0)), pl.BlockSpec((1, 512), lambda e: (0, 0))],
        out_shape=[jax.ShapeDtypeStruct((n_seg * n, 512), BF16), jax.ShapeDtypeStruct((4, 128, 128), F32),
                   jax.ShapeDtypeStruct((1, 512), F32)],
        compiler_params=_params(("arbitrary",)))(p, p, p, p, w, scale, dcat)


def _gelu(x):
    t = jnp.tanh(math.sqrt(2.0 / math.pi) * (x + 0.044715 * x * x * x))
    return 0.5 * x * (1.0 + t), t


def _gelu_grad(x, t):
    return 0.5 * (1.0 + t) + 0.5 * x * (1.0 - t * t) * (math.sqrt(2.0 / math.pi) * (1.0 + 3 * 0.044715 * x * x))


def _neg_expm1_twice(x):
    t = jnp.tanh(x)
    return (-2.0 * t) / (1.0 - t)


def _softplus_neg(lam):
    x = -lam
    e = jnp.exp(-jnp.abs(x))
    log1p = jnp.where(e < 1e-2, e * (1.0 - e * (0.5 - e * (1.0 / 3.0))), jnp.log(1.0 + e))
    return jnp.maximum(x, 0.0) + log1p, -_sigmoid(x)


def _conv(u, w_ref, b_ref, row):
    return (b_ref[...] + _shift_down(u, 1, row) * w_ref[0:1, :] + u * w_ref[1:2, :]
            + _shift_up(u, 1, row) * w_ref[2:3, :] + _shift_up(u, 2, row) * w_ref[3:4, :])


def _lru_gates(uc, d, wa_ref, ba_ref, wx_ref, bx_ref, lam_ref):
    r = _sigmoid(_dot(uc, wa_ref[d]) + ba_ref[d:d + 1, :])
    gi = _sigmoid(_dot(uc, wx_ref[d]) + bx_ref[d:d + 1, :])
    sp, dsp = _softplus_neg(lam_ref[d:d + 1, :])
    la = (-LRU_C) * r * sp
    a = jnp.exp(la)
    sq = jnp.sqrt(_neg_expm1_twice(la))
    return r, gi, sp, dsp, a, sq


def _tile_scan(a_ref, b_ref, n, reverse):
    m = n // 8
    first = 7 if reverse else 0
    a_prev = a_ref[pl.ds(first, m, stride=8), :]
    b_prev = b_ref[pl.ds(first, m, stride=8), :]
    for j in (range(6, -1, -1) if reverse else range(1, 8)):
        rows = pl.ds(j, m, stride=8)
        aj = a_ref[rows, :]
        b_prev = aj * b_prev + b_ref[rows, :]
        a_prev = aj * a_prev
        b_ref[rows, :] = b_prev
        a_ref[rows, :] = a_prev


def _carry_scan(a_ref, b_ref, n, reverse, carry):
    nt8 = n // 8

    def step(i, c):
        t = (nt8 - 1 - i) if reverse else i
        off = pl.multiple_of(t * 8, 8)
        h = a_ref[pl.ds(off, 8), :] * c + b_ref[pl.ds(off, 8), :]
        b_ref[pl.ds(off, 8), :] = h
        return h[0:1, :] if reverse else h[7:8, :]

    return lax.fori_loop(0, nt8, step, carry, unroll=4)


def _chain_scan(segs, reverse):
    carry = jnp.zeros((1, 128), F32)
    for a, b, a_ref, b_ref, n in segs:
        a_ref[...] = a
        b_ref[...] = b
        _tile_scan(a_ref, b_ref, n, reverse)
        carry = _carry_scan(a_ref, b_ref, n, reverse, carry)


def _lru_specs(cfg):
    n_lat, n_ctx, cb = cfg.n_lat, cfg.n_ctx, cfg.ctx_blk
    return [pl.BlockSpec((n_lat, 128), lambda hb, e: (e, hb)),
            pl.BlockSpec((n_lat, 128), lambda hb, e: (e, 8 + hb)),
            pl.BlockSpec((n_ctx, 128), lambda hb, e: (cb + e, hb)),
            pl.BlockSpec((n_ctx, 128), lambda hb, e: (cb + e, 8 + hb)),
            pl.BlockSpec((4, 128), lambda hb, e: (0, hb)),
            pl.BlockSpec((1, 128), lambda hb, e: (0, hb)),
            pl.BlockSpec((2, None, 128, 128), lambda hb, e: (0, hb, 0, 0)),
            pl.BlockSpec((2, 128), lambda hb, e: (0, hb)),
            pl.BlockSpec((2, None, 128, 128), lambda hb, e: (0, hb, 0, 0)),
            pl.BlockSpec((2, 128), lambda hb, e: (0, hb)),
            pl.BlockSpec((2, 128), lambda hb, e: (0, hb))]


def _lru_fwd(cfg, p, consts, name):
    n_lat, n_ctx = cfg.n_lat, cfg.n_ctx

    def body(gl_ref, ul_ref, gc_ref, uc_ref, cw_ref, cb_ref, wa_ref, ba_ref, wx_ref, bx_ref, lam_ref,
             zl_ref, zc_ref, hl_ref, hc_ref, al, ac):
        row_l = lax.broadcasted_iota(jnp.int32, (n_lat, 1), 0)
        row_c = lax.broadcasted_iota(jnp.int32, (n_ctx, 1), 0)
        uc_l = _conv(ul_ref[...].astype(F32), cw_ref, cb_ref, row_l)
        uc_c = _conv(uc_ref[...].astype(F32), cw_ref, cb_ref, row_c)
        for d in range(2):
            _, gi_l, _, _, a_l, sq_l = _lru_gates(uc_l, d, wa_ref, ba_ref, wx_ref, bx_ref, lam_ref)
            _, gi_c, _, _, a_c, sq_c = _lru_gates(uc_c, d, wa_ref, ba_ref, wx_ref, bx_ref, lam_ref)
            _chain_scan([(a_c, sq_c * (gi_c * uc_c), ac, hc_ref.at[d], n_ctx),
                         (a_l, sq_l * (gi_l * uc_l), al, hl_ref.at[d], n_lat)], reverse=(d == 1))
        zl_ref[...] = (_gelu(gl_ref[...].astype(F32))[0] * (hl_ref[0] + hl_ref[1])).astype(BF16)
        zc_ref[...] = (_gelu(gc_ref[...].astype(F32))[0] * (hc_ref[0] + hc_ref[1])).astype(BF16)

    return pl.pallas_call(
        body, grid=(8, 2), name=name, in_specs=_lru_specs(cfg),
        out_specs=[pl.BlockSpec((n_lat, 128), lambda hb, e: (e, hb)), pl.BlockSpec((n_ctx, 128), lambda hb, e: (e, hb)),
                   pl.BlockSpec((2, n_lat, 128), lambda hb, e: (0, e, hb)),
                   pl.BlockSpec((2, n_ctx, 128), lambda hb, e: (0, e, hb))],
        out_shape=[jax.ShapeDtypeStruct((cfg.t_lat, D), BF16), jax.ShapeDtypeStruct((cfg.t_ctx, D), BF16),
                   jax.ShapeDtypeStruct((2, cfg.t_lat, D), F32), jax.ShapeDtypeStruct((2, cfg.t_ctx, D), F32)],
        scratch_shapes=[pltpu.VMEM((n_lat, 128), F32), pltpu.VMEM((n_ctx, 128), F32)],
        compiler_params=_params(("parallel", "arbitrary")))(p, p, p, p, *consts)


def _lru_bwd(cfg, p, dz, h_lat, h_ctx, consts, name):
    n_lat, n_ctx, cb = cfg.n_lat, cfg.n_ctx, cfg.ctx_blk

    def body(gl_ref, ul_ref, gc_ref, uc_ref, cw_ref, cb_ref, wa_ref, ba_ref, wx_ref, bx_ref, lam_ref,
             dzl_ref, dzc_ref, hl, hc, dgl_ref, dul_ref, dgc_ref, duc_ref, dwa_ref, dwx_ref, vec_ref,
             al, bl, ac, bc):
        e = pl.program_id(1)

        @pl.when(e == 0)
        def _():
            dwa_ref[...] = jnp.zeros_like(dwa_ref)
            dwx_ref[...] = jnp.zeros_like(dwx_ref)
            vec_ref[...] = jnp.zeros_like(vec_ref)

        row_l = lax.broadcasted_iota(jnp.int32, (n_lat, 1), 0)
        row_c = lax.broadcasted_iota(jnp.int32, (n_ctx, 1), 0)
        u_l, u_c = ul_ref[...].astype(F32), uc_ref[...].astype(F32)
        uc_l = _conv(u_l, cw_ref, cb_ref, row_l)
        uc_c = _conv(u_c, cw_ref, cb_ref, row_c)
        gel_l, t_l = _gelu(gl_ref[...].astype(F32))
        gel_c, t_c = _gelu(gc_ref[...].astype(F32))
        dz_l, dz_c = dzl_ref[...].astype(F32), dzc_ref[...].astype(F32)
        dgl_ref[...] = (dz_l * (hl[0] + hl[1]) * _gelu_grad(gl_ref[...].astype(F32), t_l)).astype(BF16)
        dgc_ref[...] = (dz_c * (hc[0] + hc[1]) * _gelu_grad(gc_ref[...].astype(F32), t_c)).astype(BF16)
        dy_l, dy_c = dz_l * gel_l, dz_c * gel_c
        duc_l = jnp.zeros((n_lat, 128), F32)
        duc_c = jnp.zeros((n_ctx, 128), F32)
        for d in range(2):
            r_l, gi_l, sp, dsp, a_l, sq_l = _lru_gates(uc_l, d, wa_ref, ba_ref, wx_ref, bx_ref, lam_ref)
            r_c, gi_c, _, _, a_c, sq_c = _lru_gates(uc_c, d, wa_ref, ba_ref, wx_ref, bx_ref, lam_ref)
            if d == 0:
                an_l = _shift_up(a_l, 1, row_l)
                an_c = jnp.where(row_c < n_ctx - 1, pltpu.roll(a_c, n_ctx - 1, 0), a_l[0:1, :])
            else:
                an_l = _shift_down(a_l, 1, row_l)
                an_c = jnp.where(row_c >= 1, pltpu.roll(a_c, 1, 0), a_l[n_lat - 1:n_lat, :])
            _chain_scan([(an_l, dy_l, al, bl, n_lat), (an_c, dy_c, ac, bc, n_ctx)], reverse=(d == 0))
            dsp_sum = jnp.zeros((1, 128), F32)
            for (dh, h, r, gi, a, sq, uc, seg) in ((bl[...], hl[d], r_l, gi_l, a_l, sq_l, uc_l, "l"),
                                                  (bc[...], hc[d], r_c, gi_c, a_c, sq_c, uc_c, "c")):
                b0 = sq * (gi * uc)
                t1 = dh * sq
                dla = dh * (h - b0) - (dh * gi * uc) * (a * a) / sq
                dzr = (dla * ((-LRU_C) * sp)) * r * (1.0 - r)
                dzi = (t1 * uc) * gi * (1.0 - gi)
                dsp_sum = dsp_sum + jnp.sum(dla * ((-LRU_C) * r), axis=0, keepdims=True)
                dwa_ref[d] += _dot_tn(uc, dzr)
                dwx_ref[d] += _dot_tn(uc, dzi)
                vec_ref[d:d + 1, :] += jnp.sum(dzr, axis=0, keepdims=True)
                vec_ref[2 + d:3 + d, :] += jnp.sum(dzi, axis=0, keepdims=True)
                duc = t1 * gi + _dot_nt(dzr, wa_ref[d]) + _dot_nt(dzi, wx_ref[d])
                if seg == "l":
                    duc_l = duc_l + duc
                else:
                    duc_c = duc_c + duc
            vec_ref[4 + d:5 + d, :] += dsp_sum * dsp
        for duc, u, row, du_ref in ((duc_l, u_l, row_l, dul_ref), (duc_c, u_c, row_c, duc_ref)):
            du_ref[...] = (_shift_up(duc, 1, row) * cw_ref[0:1, :] + duc * cw_ref[1:2, :]
                           + _shift_down(duc, 1, row) * cw_ref[2:3, :]
                           + _shift_down(duc, 2, row) * cw_ref[3:4, :]).astype(BF16)
            vec_ref[6:7, :] += jnp.sum(duc * _shift_down(u, 1, row), axis=0, keepdims=True)
            vec_ref[7:8, :] += jnp.sum(duc * u, axis=0, keepdims=True)
            vec_ref[8:9, :] += jnp.sum(duc * _shift_up(u, 1, row), axis=0, keepdims=True)
            vec_ref[9:10, :] += jnp.sum(duc * _shift_up(u, 2, row), axis=0, keepdims=True)
            vec_ref[10:11, :] += jnp.sum(duc, axis=0, keepdims=True)

    lat = pl.BlockSpec((n_lat, 128), lambda hb, e: (e, hb))
    ctx = pl.BlockSpec((n_ctx, 128), lambda hb, e: (e, hb))
    wspec = pl.BlockSpec((2, None, 128, 128), lambda hb, e: (0, hb, 0, 0))
    sd = jax.ShapeDtypeStruct
    return pl.pallas_call(
        body, grid=(8, 2), name=name,
        in_specs=_lru_specs(cfg) + [pl.BlockSpec((n_lat, 128), lambda hb, e: (e, hb)),
                                    pl.BlockSpec((n_ctx, 128), lambda hb, e: (cb + e, hb)),
                                    pl.BlockSpec((2, n_lat, 128), lambda hb, e: (0, e, hb)),
                                    pl.BlockSpec((2, n_ctx, 128), lambda hb, e: (0, e, hb))],
        out_specs=[lat, lat, ctx, ctx, wspec, wspec, pl.BlockSpec((None, 16, 128), lambda hb, e: (hb, 0, 0))],
        out_shape=[sd((cfg.t_lat, D), BF16), sd((cfg.t_lat, D), BF16), sd((cfg.t_ctx, D), BF16), sd((cfg.t_ctx, D), BF16),
                   sd((2, 8, 128, 128), F32), sd((2, 8, 128, 128), F32), sd((8, 16, 128), F32)],
        scratch_shapes=[pltpu.VMEM((n_lat, 128), F32)] * 2 + [pltpu.VMEM((n_ctx, 128), F32)] * 2,
        compiler_params=_params(("parallel", "arbitrary")))(p, p, p, p, *consts, dz, dz, h_lat, h_ctx)


def _position():
    x, y, c = lax.axis_index("x"), lax.axis_index("y"), lax.axis_index("c")
    return x, y, c, 4 * x + 2 * y + c


def _peer(x, y, c, k):
    px = 1 - x if k & 4 else x
    py = 1 - y if k & 2 else y
    pc = 1 - c if k & 1 else c
    return (px, py, pc), 4 * px + 2 * py + pc


def _all_gather(v, name, in_vmem):
    def body(v_ref, o_ref, send_sems, recv_sems, local_sem):
        x, y, c, me = _position()
        mine = pltpu.make_async_copy(v_ref, o_ref.at[me], local_sem)
        mine.start()
        sends = []
        for k in range(1, N_DEV):
            peer, _ = _peer(x, y, c, k)
            cp = pltpu.make_async_remote_copy(src_ref=v_ref, dst_ref=o_ref.at[me], send_sem=send_sems.at[k - 1],
                                              recv_sem=recv_sems.at[k - 1], device_id=peer, device_id_type=MESH)
            cp.start()
            sends.append(cp)
        for k in range(1, N_DEV):
            peer, peer_lin = _peer(x, y, c, k)
            pltpu.make_async_remote_copy(src_ref=v_ref, dst_ref=o_ref.at[peer_lin], send_sem=send_sems.at[k - 1],
                                         recv_sem=recv_sems.at[k - 1], device_id=peer, device_id_type=MESH).wait_recv()
        for cp in sends:
            cp.wait_send()
        mine.wait()

    space = pltpu.VMEM if in_vmem else pl.ANY
    return pl.pallas_call(
        body, name=name,
        in_specs=[pl.BlockSpec(memory_space=space)], out_specs=pl.BlockSpec(memory_space=space),
        out_shape=jax.ShapeDtypeStruct((N_DEV,) + v.shape, v.dtype),
        scratch_shapes=[pltpu.SemaphoreType.DMA((N_DEV - 1,)), pltpu.SemaphoreType.DMA((N_DEV - 1,)),
                        pltpu.SemaphoreType.DMA],
        compiler_params=pltpu.CompilerParams(vmem_limit_bytes=VMEM_LIMIT))(v)


_HBM = pl.BlockSpec(memory_space=pltpu.HBM)
_SEM = pl.BlockSpec(memory_space=pltpu.SEMAPHORE)
_EFFECT = pltpu.SideEffectType.DATAFLOW_SIDE_EFFECTING


ALL_PEERS = tuple(range(1, N_DEV))
SAME_CORE_AND_SIBLING = (1, 2, 4, 6)


def _push_start(src, land, block_of, name, relations=ALL_PEERS):
    def body(src_ref, land_ref, send_sem, recv_sem, src_thru, land_thru, token):
        x, y, c, me = _position()
        for k in relations:
            peer, peer_lin = _peer(x, y, c, k)
            mine, there = block_of(src_ref, land_ref, me, peer_lin)
            pltpu.make_async_remote_copy(src_ref=mine, dst_ref=there, send_sem=send_sem, recv_sem=recv_sem,
                                         device_id=peer, device_id_type=MESH).start()
        mine, here = block_of(src_ref, land_ref, me, me)
        pltpu.make_async_copy(mine, here, recv_sem).start()
        token[...] = jnp.zeros_like(token)

    return pl.pallas_call(
        body, name=name,
        out_shape=(pltpu.SemaphoreType.DMA(()), pltpu.SemaphoreType.DMA(()), pltpu.HBM(src.shape, src.dtype),
                   pltpu.HBM(land.shape, land.dtype), jax.ShapeDtypeStruct((8, 128), F32)),
        in_specs=(_HBM, _HBM), out_specs=(_SEM, _SEM, _HBM, _HBM, pl.BlockSpec(memory_space=pltpu.VMEM)),
        input_output_aliases={0: 2, 1: 3},
        compiler_params=pltpu.CompilerParams(has_side_effects=_EFFECT),
    )(pltpu.with_memory_space_constraint(src, pltpu.HBM), pltpu.with_memory_space_constraint(land, pltpu.HBM))


def _push_wait(handle, blocks_of, after, name, n_peers=N_DEV - 1):
    send_sem, recv_sem, src_thru, land_thru, _ = handle

    def body(src_ref, land_ref, send_sem, recv_sem, after_ref, src_dead, got_ref):
        x, y, c, _ = _position()
        sent, landed = blocks_of(land_ref, n_peers), blocks_of(land_ref, n_peers + 1)
        pltpu.make_async_remote_copy(src_ref=sent, dst_ref=sent, send_sem=send_sem, recv_sem=recv_sem,
                                     device_id=(x, y, 1 - c), device_id_type=MESH).wait_send()
        pltpu.make_async_remote_copy(src_ref=landed, dst_ref=landed, send_sem=send_sem, recv_sem=recv_sem,
                                     device_id=(x, y, 1 - c), device_id_type=MESH).wait_recv()

    return pl.pallas_call(
        body, name=name,
        out_shape=(pltpu.HBM(src_thru.shape, src_thru.dtype), pltpu.HBM(land_thru.shape, land_thru.dtype)),
        in_specs=(_HBM, _HBM, _SEM, _SEM, pl.BlockSpec(memory_space=pl.ANY)), out_specs=(_HBM, _HBM),
        input_output_aliases={0: 0, 1: 1},
        compiler_params=pltpu.CompilerParams(has_side_effects=_EFFECT),
    )(src_thru, land_thru, send_sem, recv_sem, after)[1]


def _gather_start(src, name, relations=ALL_PEERS):
    g, r, C = src.shape
    land = lax.empty((g, N_DEV * r, C), src.dtype)
    return _push_start(src, land, lambda s, z, i, p: (s, z.at[:, pl.ds(i * r, r), :]), name, relations)


def _gather_wait(handle, after, name, n_peers=N_DEV - 1):
    r = handle[2].shape[1]
    return _push_wait(handle, lambda z, n: z.at[:, pl.ds(0, n * r), :], after, name, n_peers)


def _relay_start(land, r, name):
    def body(land_ref, send_sem, recv_sem, land_thru, token):
        x, y, c, _ = _position()
        for k in (2, 4, 6):
            _, origin = _peer(x, y, c, k)
            rows = land_ref.at[:, pl.ds(origin * r, r), :]
            pltpu.make_async_remote_copy(src_ref=rows, dst_ref=rows, send_sem=send_sem, recv_sem=recv_sem,
                                         device_id=(x, y, 1 - c), device_id_type=MESH).start()
        token[...] = jnp.zeros_like(token)

    return pl.pallas_call(
        body, name=name,
        out_shape=(pltpu.SemaphoreType.DMA(()), pltpu.SemaphoreType.DMA(()), pltpu.HBM(land.shape, land.dtype),
                   jax.ShapeDtypeStruct((8, 128), F32)),
        in_specs=(_HBM,), out_specs=(_SEM, _SEM, _HBM, pl.BlockSpec(memory_space=pltpu.VMEM)),
        input_output_aliases={0: 2},
        compiler_params=pltpu.CompilerParams(has_side_effects=_EFFECT),
    )(pltpu.with_memory_space_constraint(land, pltpu.HBM))


def _relay_wait(handle, r, after, name):
    send_sem, recv_sem, land_thru, _ = handle

    def body(land_ref, send_sem, recv_sem, after_ref, got_ref):
        x, y, c, _ = _position()
        three = land_ref.at[:, pl.ds(0, 3 * r), :]
        cp = pltpu.make_async_remote_copy(src_ref=three, dst_ref=three, send_sem=send_sem, recv_sem=recv_sem,
                                          device_id=(x, y, 1 - c), device_id_type=MESH)
        cp.wait_send()
        cp.wait_recv()

    return pl.pallas_call(
        body, name=name, out_shape=(pltpu.HBM(land_thru.shape, land_thru.dtype),),
        in_specs=(_HBM, _SEM, _SEM, pl.BlockSpec(memory_space=pl.ANY)), out_specs=(_HBM,),
        input_output_aliases={0: 0},
        compiler_params=pltpu.CompilerParams(has_side_effects=_EFFECT),
    )(land_thru, send_sem, recv_sem, after)[0]


def _exchange_start(grad, name):
    g, rows, C = grad.shape
    r = rows // N_DEV
    land = lax.empty((N_DEV, g, r, C), grad.dtype)
    return _push_start(grad, land, lambda s, z, i, p: (s.at[:, pl.ds(p * r, r), :], z.at[i]), name)


def _exchange_wait(handle, after, name):
    return _push_wait(handle, lambda z, n: z.at[pl.ds(0, n)], after, name)


def _sum_blocks(v, name):
    k, rows, cols = v.shape
    tr = rows
    for cand in (rows, 512, 352, 256, 176, 128, 64, 32, 16):
        if rows % cand == 0 and k * cand * cols * v.dtype.itemsize <= 6 * 1024 * 1024:
            tr = cand
            break

    def body(v_ref, o_ref):
        acc = v_ref[0].astype(F32)
        for s in range(1, k):
            acc = acc + v_ref[s].astype(F32)
        o_ref[...] = acc

    return pl.pallas_call(
        body, grid=(rows // tr,), name=name,
        in_specs=[pl.BlockSpec((k, tr, cols), lambda i: (0, i, 0))],
        out_specs=pl.BlockSpec((tr, cols), lambda i: (i, 0)),
        out_shape=jax.ShapeDtypeStruct((rows, cols), F32),
        compiler_params=_params(("parallel",)))(v)


def _adam_math(w, g, m, v):
    m2 = B1 * m + (1.0 - B1) * g
    v2 = B2 * v + (1.0 - B2) * (g * g)
    m_hat = m2 / (1.0 - B1 ** STEP)
    v_hat = v2 / (1.0 - B2 ** STEP)
    return -LR * (m_hat / (jnp.sqrt(v_hat) + EPS) + WD * w), m2, v2


def _adamw(w, g, m, v, name, dep=None):
    shp = w.shape
    rows, cols = (shp[-2], shp[-1]) if len(shp) >= 2 else (1, shp[-1])
    lead = math.prod(shp[:-2]) if len(shp) > 2 else 1
    fits = [t for t in range(8, rows + 1, 8) if rows % t == 0 and t * cols * 4 <= 2 * 1024 * 1024]
    tr = max(fits) if fits else rows

    def body(w_ref, g_ref, m_ref, v_ref, *rest):
        d_ref, m2_ref, v2_ref = rest[-3:]
        d_ref[...], m2_ref[...], v2_ref[...] = _adam_math(w_ref[...], g_ref[...], m_ref[...], v_ref[...])

    blk = pl.BlockSpec((None, tr, cols), lambda b, i: (b, i, 0))
    extra = [] if dep is None else [dep]
    outs = pl.pallas_call(
        body, grid=(lead, rows // tr), name=name,
        in_specs=[blk] * 4 + [pl.BlockSpec(memory_space=pl.ANY)] * len(extra), out_specs=[blk] * 3,
        out_shape=[jax.ShapeDtypeStruct((lead, rows, cols), F32)] * 3,
        compiler_params=_params(("parallel", "parallel")))(*[a.reshape(lead, rows, cols) for a in (w, g, m, v)], *extra)
    return [o.reshape(shp) for o in outs]


def _as2d(a):
    n = a.size
    if n % 1024 == 0:
        return a.reshape(n // 1024, 1024)
    if n % 128 == 0:
        return a.reshape(n // 128, 128)
    return a.reshape(1, n)


def _blocks_to_cols(a):
    b = jnp.moveaxis(a, 0, -2)
    return b.reshape(b.shape[:-2] + (b.shape[-2] * b.shape[-1],))


def _pack_rows(parts):
    padded, offs, r = [], [], 0
    for p in parts:
        pad = (-p.shape[0]) % 8
        padded.append(jnp.pad(p, ((0, pad), (0, 0))) if pad else p)
        offs.append(r)
        r += p.shape[0] + pad
    return jnp.concatenate(padded, axis=0), offs


def _silu(x):
    return x * jax.nn.sigmoid(x)


def kernel(x, c, ctx, c_ctx, w_mod, b_mod, ln_g, ln_b, ffn_w_gate, ffn_w_up, ffn_w_down, mix_ab_w_in, attn_sink, pool_w, pool_scale, mix_ab_w_out, lru_w_in, lru_conv_w, lru_conv_b, lru_wa, lru_ba, lru_wx, lru_bx, lru_lambda, lru_w_out, loss_target, m_c_ctx, m_w_mod, m_b_mod, m_ln_g, m_ln_b, m_ffn_w_gate, m_ffn_w_up, m_ffn_w_down, m_mix_ab_w_in, m_attn_sink, m_pool_w, m_pool_scale, m_mix_ab_w_out, m_lru_w_in, m_lru_conv_w, m_lru_conv_b, m_lru_wa, m_lru_ba, m_lru_wx, m_lru_bx, m_lru_lambda, m_lru_w_out, v_c_ctx, v_w_mod, v_b_mod, v_ln_g, v_ln_b, v_ffn_w_gate, v_ffn_w_up, v_ffn_w_down, v_mix_ab_w_in, v_attn_sink, v_pool_w, v_pool_scale, v_mix_ab_w_out, v_lru_w_in, v_lru_conv_w, v_lru_conv_b, v_lru_wa, v_lru_ba, v_lru_wx, v_lru_bx, v_lru_lambda, v_lru_w_out):
    weights = dict(c_ctx=c_ctx, w_mod=w_mod, b_mod=b_mod, ln_g=ln_g, ln_b=ln_b, ffn_w_gate=ffn_w_gate,
                   ffn_w_up=ffn_w_up, ffn_w_down=ffn_w_down, mix_ab_w_in=mix_ab_w_in, attn_sink=attn_sink,
                   pool_w=pool_w, pool_scale=pool_scale, mix_ab_w_out=mix_ab_w_out, lru_w_in=lru_w_in,
                   lru_conv_w=lru_conv_w, lru_conv_b=lru_conv_b, lru_wa=lru_wa, lru_ba=lru_ba, lru_wx=lru_wx,
                   lru_bx=lru_bx, lru_lambda=lru_lambda, lru_w_out=lru_w_out)
    mom_m = dict(c_ctx=m_c_ctx, w_mod=m_w_mod, b_mod=m_b_mod, ln_g=m_ln_g, ln_b=m_ln_b, ffn_w_gate=m_ffn_w_gate,
                 ffn_w_up=m_ffn_w_up, ffn_w_down=m_ffn_w_down, mix_ab_w_in=m_mix_ab_w_in, attn_sink=m_attn_sink,
                 pool_w=m_pool_w, pool_scale=m_pool_scale, mix_ab_w_out=m_mix_ab_w_out, lru_w_in=m_lru_w_in,
                 lru_conv_w=m_lru_conv_w, lru_conv_b=m_lru_conv_b, lru_wa=m_lru_wa, lru_ba=m_lru_ba, lru_wx=m_lru_wx,
                 lru_bx=m_lru_bx, lru_lambda=m_lru_lambda, lru_w_out=m_lru_w_out)
    mom_v = dict(c_ctx=v_c_ctx, w_mod=v_w_mod, b_mod=v_b_mod, ln_g=v_ln_g, ln_b=v_ln_b, ffn_w_gate=v_ffn_w_gate,
                 ffn_w_up=v_ffn_w_up, ffn_w_down=v_ffn_w_down, mix_ab_w_in=v_mix_ab_w_in, attn_sink=v_attn_sink,
                 pool_w=v_pool_w, pool_scale=v_pool_scale, mix_ab_w_out=v_mix_ab_w_out, lru_w_in=v_lru_w_in,
                 lru_conv_w=v_lru_conv_w, lru_conv_b=v_lru_conv_b, lru_wa=v_lru_wa, lru_ba=v_lru_ba, lru_wx=v_lru_wx,
                 lru_bx=v_lru_bx, lru_lambda=v_lru_lambda, lru_w_out=v_lru_w_out)
    names = list(weights)

    n_lat, n_ctx = x.shape[1], ctx.shape[1]
    cfg = _Cfg(n_lat, n_ctx)
    _, _, _, me = _position()
    mcols = w_mod.shape[2]

    def t_bf16(w):
        return jnp.swapaxes(w, -1, -2).astype(BF16)

    def ffn_src(l, i):
        return jnp.stack([t_bf16(ffn_w_gate[l, i]), t_bf16(ffn_w_up[l, i]), ffn_w_down[l, i].astype(BF16)])

    pending = {}

    def start_gathers(items, tok):
        for key, make_src in items:
            pending[key] = _gather_start(make_src() + tok.astype(BF16), "gather_start_" + key)
            tok = pending[key][4][0, 0]
        return tok

    def weights_now(key, after):
        return _gather_wait(pending[key], after, "gather_wait_" + key)

    first = _gather_start(ffn_src(0, 0), "gather_start_ffn00", SAME_CORE_AND_SIBLING)
    tok = first[4][0, 0]

    small_names = ["ln_g", "ln_b", "lru_conv_w", "lru_conv_b", "lru_ba", "lru_bx", "lru_lambda"]
    small, small_off = _pack_rows([(c + tok).reshape(-1, 128)] + [weights[n].reshape(-1, 128) for n in small_names])
    small_all = _all_gather(small, "gather_small", True)

    def small_full(idx, shp):
        rows = math.prod(shp) // 128
        return _blocks_to_cols(small_all[:, small_off[idx]:small_off[idx] + rows, :].reshape((N_DEV,) + shp))

    c_all = small_all[:, :2 * D // 128, :].reshape(2 * N_DEV, D)
    ln_g_f, ln_b_f = small_full(1, ln_g.shape), small_full(2, ln_b.shape)
    lru_consts = (small_full(3, lru_conv_w.shape)[0], small_full(4, lru_conv_b.shape), lru_wa[0],
                  small_full(5, lru_ba.shape)[0], lru_wx[0], small_full(6, lru_bx.shape)[0],
                  small_full(7, lru_lambda.shape)[0])

    s_rows = jnp.zeros((32, D), F32).at[:16].set(_silu(c_all)).at[16].set(_silu(c_ctx)).astype(BF16)
    mod_mine = jnp.stack([_matmul(s_rows, w_mod[l], "nn", F32, "mod_fwd", bn_cap=1280) for l in range(2)])
    mod_all = _all_gather(mod_mine.reshape(64, mcols), "gather_mod", True).reshape(N_DEV, 2, 32, mcols)
    r_ffn = ffn_w_down.shape[2]
    relay = _relay_start(_gather_wait(first, mod_all, "gather_wait_ffn00", n_peers=len(SAME_CORE_AND_SIBLING)),
                         r_ffn, "gather_relay_start_ffn00")
    tok = start_gathers([("ab_in", lambda: t_bf16(mix_ab_w_in)), ("ab_out", lambda: mix_ab_w_out.astype(BF16)),
                         ("ffn01", lambda: ffn_src(0, 1)), ("ffn10", lambda: ffn_src(1, 0)),
                         ("lru_in", lambda: t_bf16(lru_w_in)), ("lru_out", lambda: lru_w_out.astype(BF16)),
                         ("ffn11", lambda: ffn_src(1, 1))], relay[3][0, 0])
    mod_full = _blocks_to_cols(mod_all) + (b_mod[:, None, :] + tok)
    ex0 = 2 * me
    mods = []
    for l in range(2):
        rows = jnp.stack([lax.dynamic_index_in_dim(mod_full[l], ex0, 0, False),
                          lax.dynamic_index_in_dim(mod_full[l], ex0 + 1, 0, False), mod_full[l, 16]])
        mods.append(rows.reshape(3, N_MOD, D))

    h0 = jnp.concatenate([x.reshape(cfg.t_lat, D), ctx.reshape(cfg.t_ctx, D)], axis=0)
    cos, sin = _rope_tables(n_lat)
    sink_rows = jnp.broadcast_to(attn_sink[0][:, None], (8, 128)).astype(F32)

    saved = []
    wf = [[None, None], [None, None]]
    h = h0
    xin = _modulate(cfg, h0, mods[0], 0, 1, "modulate_in")
    for l in range(2):
        st = {"h_in": h, "xin1": xin}
        wf[l][0] = (_relay_wait(relay, r_ffn, xin, "gather_relay_wait_ffn00") if l == 0
                    else weights_now("ffn10", xin))
        g1, u1, y1 = _ffn_fwd(xin, wf[l][0], "ffn_fwd")
        h1, xhat1, rstd1, xin2 = _ln_fwd(cfg, h, y1, mods[l], 2, 0.5, ln_g_f[l, 0][None], ln_b_f[l, 0][None],
                                          mods[l], (3, 4), "ln_fwd_a")
        st.update(g1=g1, u1=u1, y1=y1, h1=h1, xhat1=xhat1, rstd1=rstd1, xin2=xin2)
        if l == 0:
            w_ab_in_t = weights_now("ab_in", xin2)[0]
            p = _matmul(xin2, w_ab_in_t, "nt", BF16, "mix_ab_in")
            att_l, att_c = _attn_fwd(cfg, p, cos, sin, sink_rows, "attn_fwd")
            pool_l = _pool_fwd(p, pool_w[0], pool_scale, n_lat, 0, 2, "pool_fwd_lat")
            pool_c = _pool_fwd(p, pool_w[0], pool_scale, n_ctx, cfg.ctx_blk, 2, "pool_fwd_ctx")
            cat = jnp.concatenate([jnp.concatenate([att_l, pool_l], axis=1),
                                   jnp.concatenate([att_c, pool_c], axis=1)], axis=0)
            w_ab_out = weights_now("ab_out", cat)[0]
            y2 = _matmul(cat, w_ab_out, "nn", BF16, "mix_ab_out")
        else:
            w_lru_in_t = weights_now("lru_in", xin2)[0]
            p = _matmul(xin2, w_lru_in_t, "nt", BF16, "lru_in")
            z_l, z_c, st["h_lat"], st["h_ctx"] = _lru_fwd(cfg, p, lru_consts, "lru_fwd")
            cat = jnp.concatenate([z_l, z_c], axis=0)
            w_lru_out = weights_now("lru_out", cat)[0]
            y2 = _matmul(cat, w_lru_out, "nn", BF16, "lru_out")
        h2, xhat2, rstd2, xin3 = _ln_fwd(cfg, h1, y2, mods[l], 5, 1.0, ln_g_f[l, 1][None], ln_b_f[l, 1][None],
                                          mods[l], (6, 7), "ln_fwd_b")
        wf[l][1] = weights_now("ffn%d1" % l, xin3)
        g3, u3, y3 = _ffn_fwd(xin3, wf[l][1], "ffn_fwd")
        if l == 0:
            h3, xhat3, rstd3, xin = _ln_fwd(cfg, h2, y3, mods[l], 8, 0.5, ln_g_f[l, 2][None], ln_b_f[l, 2][None],
                                            mods[1], (0, 1), "ln_fwd_a")
        else:
            h3, xhat3, rstd3 = _ln_fwd(cfg, h2, y3, mods[l], 8, 0.5, ln_g_f[l, 2][None], ln_b_f[l, 2][None],
                                       None, None, "ln_fwd_last")
        st.update(p=p, cat=cat, y2=y2, h2=h2, xhat2=xhat2, rstd2=rstd2, xin3=xin3, g3=g3, u3=u3, y3=y3,
                  xhat3=xhat3, rstd3=rstd3)
        saved.append(st)
        h = h3

    dy, loss_tile = _loss(cfg, h, loss_target.reshape(cfg.t_lat, D), "loss")
    loss = lax.psum(loss_tile[0, 0], ("x", "y", "c"))

    grads = {}
    dmod = [None, None]
    recv_ffn = [[None, None], [None, None]]
    dln_g = [[None] * 3, [None] * 3]
    dln_b = [[None] * 3, [None] * 3]

    def ffn_weight_grads(tag, xin_b, dg, du, a_act, dys):
        handles = []
        for k, (lhs, rhs) in enumerate(((dg, xin_b), (du, xin_b), (a_act, dys))):
            part = _matmul(lhs, rhs, "tn", BF16, "ffn_dw", bm_cap=1408, bk_cap=2304)[None]
            handles.append(_exchange_start(part, "exchange_start_ffn%s_%d" % (tag, k)))
        return handles

    def pin(handles):
        total = handles[0][4][0, 0]
        for hd in handles[1:]:
            total = total + hd[4][0, 0]
        return total

    up = (dy,)
    dmod_next = None
    last_sent = None
    for l in (1, 0):
        st = saved[l]
        dm = [None] * N_MOD

        def put_stats(stats, gate_idx, nxt):
            dm[gate_idx] = stats[:, 2, :]
            if nxt is not None:
                nxt[0][nxt[1]] = stats[:, 4, :]
                nxt[0][nxt[1] + 1] = stats[:, 3, :]

        lng3 = ln_g_f[l, 2][None] if last_sent is None else ln_g_f[l, 2][None] + pin(last_sent)
        if len(up) > 1:
            up = (up[0], up[1], ln_b_f[l, 2][None], up[3], up[4])
        dres, dys, stats = _ln_bwd(cfg, up, st["xhat3"], st["rstd3"], st["y3"], mods[l], 8, 0.5,
                                   lng3, "ln_bwd_fused" if len(up) > 1 else "ln_bwd_last")
        put_stats(stats, 8, None if len(up) == 1 else (dmod_next, 0))
        dln_g[l][2], dln_b[l][2] = stats[:, 0, :].sum(0), stats[:, 1, :].sum(0)
        dg, du, a_act, dxin = _ffn_bwd(dys, st["g3"], st["u3"], wf[l][1], "ffn_bwd")
        recv_ffn[l][1] = ffn_weight_grads("%d1" % l, st["xin3"], dg, du, a_act, dys)
        dres, dys, stats = _ln_bwd(cfg, (dres, dxin, ln_b_f[l, 1][None], mods[l], 7), st["xhat2"], st["rstd2"], st["y2"],
                                   mods[l], 5, 1.0, ln_g_f[l, 1][None] + pin(recv_ffn[l][1]), "ln_bwd_fused")
        put_stats(stats, 5, (dm, 6))
        dln_g[l][1], dln_b[l][1] = stats[:, 0, :].sum(0), stats[:, 1, :].sum(0)
        if l == 0:
            dw_out = _matmul(st["cat"], dys, "tn", BF16, "mix_ab_dw_out")
            dcat = _matmul(dys, w_ab_out, "nt", BF16, "mix_ab_dcat")
            dq, dk, dv, dqc, dkc, dvc, dsink = _attn_bwd(cfg, st["p"], dcat, cos, sin, sink_rows, "attn_bwd")
            du_l, dpw_l, dps_l = _pool_bwd(st["p"], pool_w[0], pool_scale, dcat, n_lat, 0, 2, "pool_bwd_lat")
            du_c, dpw_c, dps_c = _pool_bwd(st["p"], pool_w[0], pool_scale, dcat, n_ctx, cfg.ctx_blk, 2, "pool_bwd_ctx")
            dp = jnp.concatenate([jnp.concatenate([dq, dk, dv, du_l], axis=1),
                                  jnp.concatenate([dqc, dkc, dvc, du_c], axis=1)], axis=0)
            dw_in_t = _matmul(dp, st["xin2"], "tn", BF16, "mix_ab_dw_in", bm_cap=1280)
            dxin = _matmul(dp, w_ab_in_t, "nn", BF16, "mix_ab_dx")
            recv_mix = [_exchange_start(part, "exchange_start_mix_ab_%d" % k)
                        for k, part in enumerate((dw_in_t[None], dw_out[None], _as2d(dpw_l + dpw_c)[None]))]
            grads["attn_sink"] = (dsink[0, :, 0] + dsink[1, :, 0])[None, :]
            grads["pool_scale"] = dps_l + dps_c
        else:
            dw_out = _matmul(st["cat"], dys, "tn", BF16, "lru_dw_out")
            dz = _matmul(dys, w_lru_out, "nt", BF16, "lru_dz")
            dgl, dul, dgc, duc, dwa, dwx, vec = _lru_bwd(cfg, st["p"], dz, st["h_lat"], st["h_ctx"], lru_consts, "lru_bwd")
            dp = jnp.concatenate([jnp.concatenate([dgl, dul], axis=1), jnp.concatenate([dgc, duc], axis=1)], axis=0)
            dw_in_t = _matmul(dp, st["xin2"], "tn", BF16, "lru_dw_in", bm_cap=1024)
            dxin = _matmul(dp, w_lru_in_t, "nn", BF16, "lru_dx")
            recv_mix = [_exchange_start(part, "exchange_start_lru_%d" % k)
                        for k, part in enumerate((dw_in_t[None], dw_out[None], _as2d(dwa)[None], _as2d(dwx)[None]))]
            vec_t = jnp.moveaxis(vec, 0, 1).reshape(16, D)
            grads["lru_ba"], grads["lru_bx"] = vec_t[0:2], vec_t[2:4]
            grads["lru_lambda"], grads["lru_conv_w"], grads["lru_conv_b"] = vec_t[4:6], vec_t[6:10], vec_t[10:11]
        if l == 0:
            recv_ab = recv_mix
        else:
            recv_lru = recv_mix
        dres, dys, stats = _ln_bwd(cfg, (dres, dxin, ln_b_f[l, 0][None], mods[l], 4), st["xhat1"], st["rstd1"], st["y1"],
                                   mods[l], 2, 0.5, ln_g_f[l, 0][None] + pin(recv_mix), "ln_bwd_fused")
        put_stats(stats, 2, (dm, 3))
        dln_g[l][0], dln_b[l][0] = stats[:, 0, :].sum(0), stats[:, 1, :].sum(0)
        dg, du, a_act, dxin = _ffn_bwd(dys, st["g1"], st["u1"], wf[l][0], "ffn_bwd")
        recv_ffn[l][0] = ffn_weight_grads("%d0" % l, st["xin1"], dg, du, a_act, dys)
        last_sent = recv_ffn[l][0]
        dmod[l] = dm
        dmod_next = dm
        up = (dres, dxin, None, mods[l], 1)
    dh0, stats = _modulate_bwd(cfg, up[0], up[1], h0, mods[0] + pin(last_sent), 1, "modulate_bwd")
    dmod[0][0], dmod[0][1] = stats[:, 4, :], stats[:, 3, :]
    grad_x = dh0.reshape(x.shape)

    dmod_mine = jnp.stack([jnp.stack(dmod[l], axis=1).reshape(3, N_MOD * D) for l in range(2)])
    n_dm = 6 * N_MOD * D // 128
    dmod_sent = _gather_start(dmod_mine.reshape(1, n_dm, 128), "gather_start_dmod")

    def arrived(handle, name):
        return _exchange_wait(handle, dmod_sent[4], name)

    recv_ffn = [[[arrived(hd, "exchange_wait_ffn%d%d_%d" % (l, i, k)) for k, hd in enumerate(recv_ffn[l][i])]
                 for i in range(2)] for l in range(2)]
    recv_ab = [arrived(hd, "exchange_wait_mix_ab_%d" % k) for k, hd in enumerate(recv_ab)]
    recv_lru = [arrived(hd, "exchange_wait_lru_%d" % k) for k, hd in enumerate(recv_lru)]

    def shard_sum(recv, name):
        return _sum_blocks(recv.reshape(N_DEV, recv.shape[2], recv.shape[3]), name)

    gate_g = [[None, None], [None, None]]
    up_g = [[None, None], [None, None]]
    down_g = [[None, None], [None, None]]
    for l in range(2):
        for i in range(2):
            gt, ut, dn = [shard_sum(r, "sum_ffn") for r in recv_ffn[l][i]]
            gate_g[l][i], up_g[l][i], down_g[l][i] = gt.T, ut.T, dn
    grads["ffn_w_gate"] = jnp.stack([jnp.stack(gate_g[l]) for l in range(2)])
    grads["ffn_w_up"] = jnp.stack([jnp.stack(up_g[l]) for l in range(2)])
    grads["ffn_w_down"] = jnp.stack([jnp.stack(down_g[l]) for l in range(2)])
    grads["mix_ab_w_in"] = shard_sum(recv_ab[0], "sum_mix_in").T[None]
    grads["mix_ab_w_out"] = shard_sum(recv_ab[1], "sum_mix_out")[None]
    grads["lru_w_in"] = shard_sum(recv_lru[0], "sum_lru_in").T[None]
    grads["lru_w_out"] = shard_sum(recv_lru[1], "sum_lru_out")[None]
    rep_parts = [shard_sum(recv_lru[2], "sum_rep"), shard_sum(recv_lru[3], "sum_rep"), shard_sum(recv_ab[2], "sum_rep")]
    rep_names = ["lru_wa", "lru_wx", "pool_w"]

    dmod_all = _gather_wait(dmod_sent, rep_parts[2], "gather_wait_dmod").reshape(N_DEV, n_dm, 128)
    dmod_sum = _sum_blocks(dmod_all, "sum_dmod").reshape(2, 3, N_MOD * D)
    dmod_all = dmod_all.reshape(N_DEV, 2, 3, N_MOD * D)
    grads["b_mod"] = dmod_sum[:, 0] + dmod_sum[:, 1] + dmod_sum[:, 2]
    dmod_ex = jnp.moveaxis(dmod_all[:, :, 0:2, :], 1, 0).reshape(2, 2 * N_DEV, N_MOD * D)
    dm_rows = jnp.zeros((2, 32, N_MOD * D), F32).at[:, :16].set(dmod_ex).at[:, 16].set(dmod_sum[:, 2])
    dm_cols = lax.dynamic_slice_in_dim(dm_rows, me * mcols, mcols, axis=2).astype(BF16)
    grads["w_mod"] = jnp.stack([_matmul(s_rows, dm_cols[l], "tn", F32, "mod_dw", bn_cap=1280) for l in range(2)])
    ds_part = None
    for l in range(2):
        part = _matmul(dm_cols[l, 16:32], w_mod[l], "nt", F32, "mod_ds", bk_cap=1280)[0]
        ds_part = part if ds_part is None else ds_part + part

    dln_g_f = jnp.stack([jnp.stack(dln_g[l]) for l in range(2)])
    dln_b_f = jnp.stack([jnp.stack(dln_b[l]) for l in range(2)])
    sink_pad = jnp.zeros((1, 128), F32).at[0, :8].set(grads["attn_sink"][0])
    part_list = [p_.reshape(-1, 128) for p_ in rep_parts] + [
        dln_g_f.reshape(-1, 128), dln_b_f.reshape(-1, 128), grads["lru_conv_w"].reshape(-1, 128),
        grads["lru_conv_b"].reshape(-1, 128), grads["lru_ba"].reshape(-1, 128), grads["lru_bx"].reshape(-1, 128),
        grads["lru_lambda"].reshape(-1, 128), ds_part.reshape(-1, 128), sink_pad, grads["pool_scale"].reshape(-1, 128)]
    parts, part_off = _pack_rows(part_list)
    parts_sent = _gather_start(parts[None], "gather_start_partials")

    delta, new_m, new_v = {}, {}, {}
    for n in ("w_mod", "b_mod", "ffn_w_gate", "ffn_w_up", "ffn_w_down", "mix_ab_w_in", "mix_ab_w_out",
              "lru_w_in", "lru_w_out"):
        grads[n] = grads[n].reshape(weights[n].shape)
        delta[n], new_m[n], new_v[n] = _adamw(weights[n], grads[n], mom_m[n], mom_v[n], "adamw", dep=parts_sent[4])
    parts_all = _gather_wait(parts_sent, delta["lru_w_out"], "gather_wait_partials").reshape(N_DEV, parts.shape[0], 128)
    parts_sum = _sum_blocks(parts_all, "sum_partials")

    for i, n in enumerate(rep_names):
        rows = part_list[i].shape[0]
        grads[n] = parts_all[:, part_off[i]:part_off[i] + rows, :].reshape(weights[n].shape)

    def take(idx):
        return parts_sum[part_off[idx]:part_off[idx] + part_list[idx].shape[0]]

    def my_cols(full, shp):
        w = shp[-1]
        return lax.dynamic_slice_in_dim(full, me * w, w, axis=full.ndim - 1)

    grads["ln_g"] = my_cols(take(3).reshape(2, 3, D), ln_g.shape)
    grads["ln_b"] = my_cols(take(4).reshape(2, 3, D), ln_b.shape)
    grads["lru_conv_w"] = my_cols(take(5).reshape(1, 4, D), lru_conv_w.shape)
    grads["lru_conv_b"] = my_cols(take(6).reshape(1, D), lru_conv_b.shape)
    grads["lru_ba"] = my_cols(take(7).reshape(1, 2, D), lru_ba.shape)
    grads["lru_bx"] = my_cols(take(8).reshape(1, 2, D), lru_bx.shape)
    grads["lru_lambda"] = my_cols(take(9).reshape(1, 2, D), lru_lambda.shape)
    sg = jax.nn.sigmoid(c_ctx)
    grads["c_ctx"] = take(10).reshape(D) * (sg * (1.0 + c_ctx * (1.0 - sg)))
    grads["attn_sink"] = take(11)[:, :8]
    grads["pool_scale"] = take(12).reshape(pool_scale.shape)

    for n in names:
        if n in delta:
            continue
        grads[n] = grads[n].reshape(weights[n].shape)
        delta[n], new_m[n], new_v[n] = _adamw(weights[n], grads[n], mom_m[n], mom_v[n], "adamw")

    return (loss, grad_x, *[grads[n] for n in names], *[delta[n] for n in names],
            *[new_m[n] for n in names], *[new_v[n] for n in names])
```
